```python
import jax, jax.numpy as jnp
from jax import lax
import numpy as np

D_MODEL = 1024
BATCH = 8
SEQ = 4096
DEPTH = 1

CHUNK = 64
RWKV_HEADS = 8
RWKV_HEAD_DIM = 64
RWKV_WIDTH = RWKV_HEADS * RWKV_HEAD_DIM
DECAY_LORA = 64
ICLR_LORA = 64
GATE_LORA = 128
GDN_HEADS = 4
GDN_HEAD_DIM = 128
GDN_WIDTH = GDN_HEADS * GDN_HEAD_DIM
GDN_CONV = 4
FFN_HIDDEN = 2816
FFN_CONV = 3
NORM_EPS = 1e-6
L2_EPS = 1e-6
RWKV_GN_EPS = 64e-5

RWKV_SHIFT_WIDTH = 3 * RWKV_WIDTH + DECAY_LORA + ICLR_LORA + GATE_LORA
IN_SPLITS = (RWKV_SHIFT_WIDTH, 3 * GDN_WIDTH, GDN_WIDTH, GDN_HEADS, GDN_HEADS, D_MODEL, D_MODEL)
IN_WIDTH = RWKV_SHIFT_WIDTH + 4 * GDN_WIDTH + 2 * GDN_HEADS + 2 * D_MODEL

kernel_name = 'hybrid_rwkv7_gdn_gated_merge_block'


def _split(t, sizes):
    cuts = [int(c) for c in np.cumsum(sizes)[:-1]]
    return jnp.split(t, cuts, axis=-1)


def rms_norm(t, gain, eps=NORM_EPS):
    tf = t.astype(jnp.float32)
    y = tf * lax.rsqrt(jnp.mean(tf * tf, axis=-1, keepdims=True) + eps)
    return (y * gain.astype(jnp.float32)).astype(t.dtype)


def l2norm(t):
    tf = t.astype(jnp.float32)
    return (tf * lax.rsqrt(jnp.sum(tf * tf, axis=-1, keepdims=True) + L2_EPS)).astype(t.dtype)


def causal_depthwise_conv(t, w):
    width = w.shape[0]
    T = t.shape[1]
    tp = jnp.pad(t, ((0, 0), (width - 1, 0), (0, 0)))
    out = tp[:, 0:T] * w[0]
    for i in range(1, width):
        out = out + tp[:, i:i + T] * w[i]
    return out


def token_shift(t):
    return jnp.pad(t, ((0, 0), (1, 0), (0, 0)))[:, :-1]


def group_norm_heads(y, w, b):
    yf = y.astype(jnp.float32)
    mean = jnp.mean(yf, axis=-1, keepdims=True)
    var = jnp.mean(jnp.square(yf - mean), axis=-1, keepdims=True)
    yn = (yf - mean) * lax.rsqrt(var + RWKV_GN_EPS)
    H, D = y.shape[-2], y.shape[-1]
    return (yn * w.reshape(H, D) + b.reshape(H, D)).astype(y.dtype)


def wkv7_scan(r, w, k, v, a, b):
    dtype = r.dtype
    B, T, H, D = r.shape
    xs = tuple(jnp.moveaxis(t.astype(jnp.float32), 1, 0) for t in (r, w, k, v, a, b))

    def step(S, inp):
        r_t, w_t, k_t, v_t, a_t, b_t = inp
        sa = jnp.einsum('bhvk,bhk->bhv', S, a_t)
        S = S * w_t[:, :, None, :] + sa[..., None] * b_t[:, :, None, :] + v_t[..., None] * k_t[:, :, None, :]
        y = jnp.einsum('bhvk,bhk->bhv', S, r_t)
        return S, y

    S0 = jnp.zeros((B, H, D, D), jnp.float32)
    _, y = lax.scan(step, S0, xs)
    return jnp.moveaxis(y, 0, 1).astype(dtype)


def rwkv7_mix(p, mu, w0, w2, a0, a2, g2, k_k, k_a, r_k, ln_w, ln_b):
    B, T, _ = p.shape
    p = p + (token_shift(p) - p) * mu
    r, k, v, wl, al, gl = _split(p, (RWKV_WIDTH, RWKV_WIDTH, RWKV_WIDTH, DECAY_LORA, ICLR_LORA, GATE_LORA))
    w_log = -jax.nn.softplus(-(w0 + jnp.tanh(wl) @ w2)) - 0.5
    a = jax.nn.sigmoid(a0 + al @ a2)
    g = jax.nn.sigmoid(gl) @ g2

    def heads(t):
        return t.reshape(B, T, RWKV_HEADS, RWKV_HEAD_DIM)

    kk = l2norm(heads(k * k_k))
    k = k * (1 + (a - 1) * k_a)
    r_h, k_h, v_h, a_h = heads(r), heads(k), heads(v), heads(a)
    decay = jnp.exp(-jnp.exp(heads(w_log).astype(jnp.float32)))
    y = wkv7_scan(r_h, decay, k_h, v_h, -kk, kk * a_h)
    y = group_norm_heads(y, ln_w, ln_b)
    y = y + jnp.sum(r_h * k_h * r_k, axis=-1, keepdims=True) * v_h
    return y.reshape(B, T, RWKV_WIDTH) * g


def chunk_gated_delta_rule(q, k, v, g, beta):
    dtype = v.dtype
    B, T, H, Dk = q.shape
    Dv = v.shape[-1]
    N = T // CHUNK

    def to_chunks(t):
        t = t.astype(jnp.float32).reshape((B, N, CHUNK, H) + t.shape[3:])
        return jnp.moveaxis(t, 3, 1)

    q = to_chunks(q) * (Dk ** -0.5)
    k, v, g, beta = to_chunks(k), to_chunks(v), to_chunks(g), to_chunks(beta)
    gc = jnp.cumsum(g, axis=-1)
    causal = jnp.tril(jnp.ones((CHUNK, CHUNK), bool))
    strict = jnp.tril(jnp.ones((CHUNK, CHUNK), bool), -1)
    diff = gc[..., :, None] - gc[..., None, :]
    decay = jnp.where(causal, jnp.exp(jnp.where(causal, diff, 0.0)), 0.0)
    k_beta = k * beta[..., None]
    v_beta = v * beta[..., None]
    Lmat = jnp.where(strict, jnp.einsum('bhncd,bhnsd->bhncs', k_beta, k) * decay, 0.0)
    eye = jnp.eye(CHUNK, dtype=jnp.float32)
    Tinv = lax.linalg.triangular_solve(Lmat + eye, jnp.broadcast_to(eye, Lmat.shape),
                                       left_side=True, lower=True, unit_diagonal=True)
    u = jnp.einsum('bhncs,bhnsd->bhncd', Tinv, v_beta)
    wk = jnp.einsum('bhncs,bhnsd->bhncd', Tinv, k_beta * jnp.exp(gc)[..., None])
    attn = jnp.where(causal, jnp.einsum('bhncd,bhnsd->bhncs', q, k) * decay, 0.0)
    q_dec = q * jnp.exp(gc)[..., None]
    g_last = gc[..., -1]
    k_dec = k * jnp.exp(g_last[..., None] - gc)[..., None]
    xs = (jnp.moveaxis(q_dec, 2, 0), jnp.moveaxis(wk, 2, 0), jnp.moveaxis(u, 2, 0),
          jnp.moveaxis(attn, 2, 0), jnp.moveaxis(k_dec, 2, 0), jnp.moveaxis(g_last, 2, 0))

    def step(S, inp):
        q_n, w_n, u_n, attn_n, k_n, gl_n = inp
        v_new = u_n - jnp.einsum('bhcd,bhde->bhce', w_n, S)
        o = jnp.einsum('bhcd,bhde->bhce', q_n, S) + jnp.einsum('bhcs,bhse->bhce', attn_n, v_new)
        S = S * jnp.exp(gl_n)[..., None, None] + jnp.einsum('bhcd,bhce->bhde', k_n, v_new)
        return S, o

    S0 = jnp.zeros((B, H, Dk, Dv), jnp.float32)
    _, o = lax.scan(step, S0, xs)
    o = jnp.transpose(o, (1, 0, 3, 2, 4)).reshape(B, T, H, Dv)
    return o.astype(dtype)


def gated_deltanet_mix(qkv, z, a_raw, b_raw, conv_w, a_log, dt_bias, norm_w):
    B, T, _ = qkv.shape
    qkv = jax.nn.silu(causal_depthwise_conv(qkv, conv_w))
    q, k, v = _split(qkv, (GDN_WIDTH, GDN_WIDTH, GDN_WIDTH))
    q = l2norm(q.reshape(B, T, GDN_HEADS, GDN_HEAD_DIM))
    k = l2norm(k.reshape(B, T, GDN_HEADS, GDN_HEAD_DIM))
    v = v.reshape(B, T, GDN_HEADS, GDN_HEAD_DIM)
    beta = jax.nn.sigmoid(b_raw)
    g = -jnp.exp(a_log.astype(jnp.float32)) * jax.nn.softplus(a_raw.astype(jnp.float32) + dt_bias.astype(jnp.float32))
    o = chunk_gated_delta_rule(q, k, v, g, beta)
    o = rms_norm(o, norm_w) * jax.nn.silu(z.reshape(B, T, GDN_HEADS, GDN_HEAD_DIM))
    return o.reshape(B, T, GDN_WIDTH)


def _fwd_setup_inputs(seed: int = 0) -> dict:
    key = jax.random.key(seed)
    ks = jax.random.split(key, 32)
    L = DEPTH

    def nrm(k, shape, scale):
        return jax.random.normal(k, shape, jnp.float32) * scale

    dt = jnp.exp(jax.random.uniform(ks[17], (L, GDN_HEADS), minval=float(np.log(1e-3)), maxval=float(np.log(1e-1))))
    return {
        'x': nrm(ks[0], (BATCH, SEQ, D_MODEL), 1.0),
        'norm1_g': 1.0 + nrm(ks[1], (L, D_MODEL), 0.02),
        'w_in': nrm(ks[2], (L, D_MODEL, IN_WIDTH), D_MODEL ** -0.5),
        'rwkv_mu': jax.random.uniform(ks[3], (L, RWKV_SHIFT_WIDTH)),
        'rwkv_w0': jax.random.uniform(ks[4], (L, RWKV_WIDTH), minval=-6.5, maxval=-1.0),
        'rwkv_w2': nrm(ks[5], (L, DECAY_LORA, RWKV_WIDTH), 0.5 * DECAY_LORA ** -0.5),
        'rwkv_a0': nrm(ks[6], (L, RWKV_WIDTH), 0.1),
        'rwkv_a2': nrm(ks[7], (L, ICLR_LORA, RWKV_WIDTH), 0.5 * ICLR_LORA ** -0.5),
        'rwkv_g2': nrm(ks[8], (L, GATE_LORA, RWKV_WIDTH), GATE_LORA ** -0.5),
        'rwkv_k_k': 0.85 + nrm(ks[9], (L, RWKV_WIDTH), 0.05),
        'rwkv_k_a': 1.0 + nrm(ks[10], (L, RWKV_WIDTH), 0.05),
        'rwkv_r_k': nrm(ks[11], (L, RWKV_HEADS, RWKV_HEAD_DIM), 0.1),
        'rwkv_ln_w': 1.0 + nrm(ks[12], (L, RWKV_WIDTH), 0.02),
        'rwkv_ln_b': nrm(ks[13], (L, RWKV_WIDTH), 0.02),
        'rwkv_proj': nrm(ks[14], (L, RWKV_WIDTH, D_MODEL), RWKV_WIDTH ** -0.5),
        'gdn_conv_w': nrm(ks[15], (L, GDN_CONV, 3 * GDN_WIDTH), GDN_CONV ** -0.5),
        'gdn_a_log': jnp.log(jax.random.uniform(ks[16], (L, GDN_HEADS), minval=1.0, maxval=16.0)),
        'gdn_dt_bias': dt + jnp.log(-jnp.expm1(-dt)),
        'gdn_norm_w': 1.0 + nrm(ks[18], (L, GDN_HEAD_DIM), 0.02),
        'gdn_proj': nrm(ks[19], (L, GDN_WIDTH, D_MODEL), GDN_WIDTH ** -0.5),
        'w_out': nrm(ks[20], (L, D_MODEL, D_MODEL), D_MODEL ** -0.5),
        'norm2_g': 1.0 + nrm(ks[21], (L, D_MODEL), 0.02),
        'ffn_up': nrm(ks[22], (L, D_MODEL, 2 * FFN_HIDDEN), D_MODEL ** -0.5),
        'ffn_conv_w': nrm(ks[23], (L, FFN_CONV, 2 * FFN_HIDDEN), FFN_CONV ** -0.5),
        'ffn_down': nrm(ks[24], (L, FFN_HIDDEN, D_MODEL), FFN_HIDDEN ** -0.5),
        'final_g': 1.0 + nrm(ks[25], (D_MODEL,), 0.02),
    }


def _fwd_reference(x, norm1_g, w_in, rwkv_mu, rwkv_w0, rwkv_w2, rwkv_a0, rwkv_a2, rwkv_g2, rwkv_k_k,
              rwkv_k_a, rwkv_r_k, rwkv_ln_w, rwkv_ln_b, rwkv_proj, gdn_conv_w, gdn_a_log, gdn_dt_bias,
              gdn_norm_w, gdn_proj, w_out, norm2_g, ffn_up, ffn_conv_w, ffn_down, final_g):
    for l in range(DEPTH):
        u = rms_norm(x, norm1_g[l])
        p = u @ w_in[l]
        p_rwkv, qkv, z, a_raw, b_raw, gate_a, gate_b = _split(p, IN_SPLITS)
        y_a = rwkv7_mix(p_rwkv, rwkv_mu[l], rwkv_w0[l], rwkv_w2[l], rwkv_a0[l], rwkv_a2[l], rwkv_g2[l],
                        rwkv_k_k[l], rwkv_k_a[l], rwkv_r_k[l], rwkv_ln_w[l], rwkv_ln_b[l]) @ rwkv_proj[l]
        y_b = gated_deltanet_mix(qkv, z, a_raw, b_raw, gdn_conv_w[l], gdn_a_log[l], gdn_dt_bias[l],
                                 gdn_norm_w[l]) @ gdn_proj[l]
        mixed = jax.nn.sigmoid(gate_a) * y_a + jax.nn.sigmoid(gate_b) * y_b
        x = x + mixed @ w_out[l]
        h = rms_norm(x, norm2_g[l]) @ ffn_up[l]
        h = causal_depthwise_conv(h, ffn_conv_w[l])
        h_gate, h_up = _split(h, (FFN_HIDDEN, FFN_HIDDEN))
        x = x + (jax.nn.silu(h_gate) * h_up) @ ffn_down[l]
    return rms_norm(x, final_g)


import jax as _jax
import jax.numpy as _jnp

TWIN_FORMAT = 'train_step'
FWD_PARAMS = ['x', 'norm1_g', 'w_in', 'rwkv_mu', 'rwkv_w0', 'rwkv_w2', 'rwkv_a0', 'rwkv_a2', 'rwkv_g2', 'rwkv_k_k', 'rwkv_k_a', 'rwkv_r_k', 'rwkv_ln_w', 'rwkv_ln_b', 'rwkv_proj', 'gdn_conv_w', 'gdn_a_log', 'gdn_dt_bias', 'gdn_norm_w', 'gdn_proj', 'w_out', 'norm2_g', 'ffn_up', 'ffn_conv_w', 'ffn_down', 'final_g']
TWIN_WEIGHTS = ['norm1_g', 'w_in', 'rwkv_mu', 'rwkv_w0', 'rwkv_w2', 'rwkv_a0', 'rwkv_a2', 'rwkv_g2', 'rwkv_k_k', 'rwkv_k_a', 'rwkv_r_k', 'rwkv_ln_w', 'rwkv_ln_b', 'rwkv_proj', 'gdn_conv_w', 'gdn_a_log', 'gdn_dt_bias', 'gdn_norm_w', 'gdn_proj', 'w_out', 'norm2_g', 'ffn_up', 'ffn_conv_w', 'ffn_down', 'final_g']
TWIN_DIFF_INPUT = 'x'
TWIN_INPUTS = ['x', 'norm1_g', 'w_in', 'rwkv_mu', 'rwkv_w0', 'rwkv_w2', 'rwkv_a0', 'rwkv_a2', 'rwkv_g2', 'rwkv_k_k', 'rwkv_k_a', 'rwkv_r_k', 'rwkv_ln_w', 'rwkv_ln_b', 'rwkv_proj', 'gdn_conv_w', 'gdn_a_log', 'gdn_dt_bias', 'gdn_norm_w', 'gdn_proj', 'w_out', 'norm2_g', 'ffn_up', 'ffn_conv_w', 'ffn_down', 'final_g', 'loss_target', 'm_norm1_g', 'm_w_in', 'm_rwkv_mu', 'm_rwkv_w0', 'm_rwkv_w2', 'm_rwkv_a0', 'm_rwkv_a2', 'm_rwkv_g2', 'm_rwkv_k_k', 'm_rwkv_k_a', 'm_rwkv_r_k', 'm_rwkv_ln_w', 'm_rwkv_ln_b', 'm_rwkv_proj', 'm_gdn_conv_w', 'm_gdn_a_log', 'm_gdn_dt_bias', 'm_gdn_norm_w', 'm_gdn_proj', 'm_w_out', 'm_norm2_g', 'm_ffn_up', 'm_ffn_conv_w', 'm_ffn_down', 'm_final_g', 'v_norm1_g', 'v_w_in', 'v_rwkv_mu', 'v_rwkv_w0', 'v_rwkv_w2', 'v_rwkv_a0', 'v_rwkv_a2', 'v_rwkv_g2', 'v_rwkv_k_k', 'v_rwkv_k_a', 'v_rwkv_r_k', 'v_rwkv_ln_w', 'v_rwkv_ln_b', 'v_rwkv_proj', 'v_gdn_conv_w', 'v_gdn_a_log', 'v_gdn_dt_bias', 'v_gdn_norm_w', 'v_gdn_proj', 'v_w_out', 'v_norm2_g', 'v_ffn_up', 'v_ffn_conv_w', 'v_ffn_down', 'v_final_g']
TWIN_OUTPUTS = ['loss', 'grad_x', 'grad_norm1_g', 'grad_w_in', 'grad_rwkv_mu', 'grad_rwkv_w0', 'grad_rwkv_w2', 'grad_rwkv_a0', 'grad_rwkv_a2', 'grad_rwkv_g2', 'grad_rwkv_k_k', 'grad_rwkv_k_a', 'grad_rwkv_r_k', 'grad_rwkv_ln_w', 'grad_rwkv_ln_b', 'grad_rwkv_proj', 'grad_gdn_conv_w', 'grad_gdn_a_log', 'grad_gdn_dt_bias', 'grad_gdn_norm_w', 'grad_gdn_proj', 'grad_w_out', 'grad_norm2_g', 'grad_ffn_up', 'grad_ffn_conv_w', 'grad_ffn_down', 'grad_final_g', 'delta_norm1_g', 'delta_w_in', 'delta_rwkv_mu', 'delta_rwkv_w0', 'delta_rwkv_w2', 'delta_rwkv_a0', 'delta_rwkv_a2', 'delta_rwkv_g2', 'delta_rwkv_k_k', 'delta_rwkv_k_a', 'delta_rwkv_r_k', 'delta_rwkv_ln_w', 'delta_rwkv_ln_b', 'delta_rwkv_proj', 'delta_gdn_conv_w', 'delta_gdn_a_log', 'delta_gdn_dt_bias', 'delta_gdn_norm_w', 'delta_gdn_proj', 'delta_w_out', 'delta_norm2_g', 'delta_ffn_up', 'delta_ffn_conv_w', 'delta_ffn_down', 'delta_final_g', 'new_m_norm1_g', 'new_m_w_in', 'new_m_rwkv_mu', 'new_m_rwkv_w0', 'new_m_rwkv_w2', 'new_m_rwkv_a0', 'new_m_rwkv_a2', 'new_m_rwkv_g2', 'new_m_rwkv_k_k', 'new_m_rwkv_k_a', 'new_m_rwkv_r_k', 'new_m_rwkv_ln_w', 'new_m_rwkv_ln_b', 'new_m_rwkv_proj', 'new_m_gdn_conv_w', 'new_m_gdn_a_log', 'new_m_gdn_dt_bias', 'new_m_gdn_norm_w', 'new_m_gdn_proj', 'new_m_w_out', 'new_m_norm2_g', 'new_m_ffn_up', 'new_m_ffn_conv_w', 'new_m_ffn_down', 'new_m_final_g', 'new_v_norm1_g', 'new_v_w_in', 'new_v_rwkv_mu', 'new_v_rwkv_w0', 'new_v_rwkv_w2', 'new_v_rwkv_a0', 'new_v_rwkv_a2', 'new_v_rwkv_g2', 'new_v_rwkv_k_k', 'new_v_rwkv_k_a', 'new_v_rwkv_r_k', 'new_v_rwkv_ln_w', 'new_v_rwkv_ln_b', 'new_v_rwkv_proj', 'new_v_gdn_conv_w', 'new_v_gdn_a_log', 'new_v_gdn_dt_bias', 'new_v_gdn_norm_w', 'new_v_gdn_proj', 'new_v_w_out', 'new_v_norm2_g', 'new_v_ffn_up', 'new_v_ffn_conv_w', 'new_v_ffn_down', 'new_v_final_g']
TWIN_LEAF_KINDS = {'loss': 'loss', 'grad_x': 'grad_x', 'grad_norm1_g': 'grad_w', 'grad_w_in': 'grad_w', 'grad_rwkv_mu': 'grad_w', 'grad_rwkv_w0': 'grad_w', 'grad_rwkv_w2': 'grad_w', 'grad_rwkv_a0': 'grad_w', 'grad_rwkv_a2': 'grad_w', 'grad_rwkv_g2': 'grad_w', 'grad_rwkv_k_k': 'grad_w', 'grad_rwkv_k_a': 'grad_w', 'grad_rwkv_r_k': 'grad_w', 'grad_rwkv_ln_w': 'grad_w', 'grad_rwkv_ln_b': 'grad_w', 'grad_rwkv_proj': 'grad_w', 'grad_gdn_conv_w': 'grad_w', 'grad_gdn_a_log': 'grad_w', 'grad_gdn_dt_bias': 'grad_w', 'grad_gdn_norm_w': 'grad_w', 'grad_gdn_proj': 'grad_w', 'grad_w_out': 'grad_w', 'grad_norm2_g': 'grad_w', 'grad_ffn_up': 'grad_w', 'grad_ffn_conv_w': 'grad_w', 'grad_ffn_down': 'grad_w', 'grad_final_g': 'grad_w', 'delta_norm1_g': 'delta_w', 'delta_w_in': 'delta_w', 'delta_rwkv_mu': 'delta_w', 'delta_rwkv_w0': 'delta_w', 'delta_rwkv_w2': 'delta_w', 'delta_rwkv_a0': 'delta_w', 'delta_rwkv_a2': 'delta_w', 'delta_rwkv_g2': 'delta_w', 'delta_rwkv_k_k': 'delta_w', 'delta_rwkv_k_a': 'delta_w', 'delta_rwkv_r_k': 'delta_w', 'delta_rwkv_ln_w': 'delta_w', 'delta_rwkv_ln_b': 'delta_w', 'delta_rwkv_proj': 'delta_w', 'delta_gdn_conv_w': 'delta_w', 'delta_gdn_a_log': 'delta_w', 'delta_gdn_dt_bias': 'delta_w', 'delta_gdn_norm_w': 'delta_w', 'delta_gdn_proj': 'delta_w', 'delta_w_out': 'delta_w', 'delta_norm2_g': 'delta_w', 'delta_ffn_up': 'delta_w', 'delta_ffn_conv_w': 'delta_w', 'delta_ffn_down': 'delta_w', 'delta_final_g': 'delta_w', 'new_m_norm1_g': 'new_m', 'new_m_w_in': 'new_m', 'new_m_rwkv_mu': 'new_m', 'new_m_rwkv_w0': 'new_m', 'new_m_rwkv_w2': 'new_m', 'new_m_rwkv_a0': 'new_m', 'new_m_rwkv_a2': 'new_m', 'new_m_rwkv_g2': 'new_m', 'new_m_rwkv_k_k': 'new_m', 'new_m_rwkv_k_a': 'new_m', 'new_m_rwkv_r_k': 'new_m', 'new_m_rwkv_ln_w': 'new_m', 'new_m_rwkv_ln_b': 'new_m', 'new_m_rwkv_proj': 'new_m', 'new_m_gdn_conv_w': 'new_m', 'new_m_gdn_a_log': 'new_m', 'new_m_gdn_dt_bias': 'new_m', 'new_m_gdn_norm_w': 'new_m', 'new_m_gdn_proj': 'new_m', 'new_m_w_out': 'new_m', 'new_m_norm2_g': 'new_m', 'new_m_ffn_up': 'new_m', 'new_m_ffn_conv_w': 'new_m', 'new_m_ffn_down': 'new_m', 'new_m_final_g': 'new_m', 'new_v_norm1_g': 'new_v', 'new_v_w_in': 'new_v', 'new_v_rwkv_mu': 'new_v', 'new_v_rwkv_w0': 'new_v', 'new_v_rwkv_w2': 'new_v', 'new_v_rwkv_a0': 'new_v', 'new_v_rwkv_a2': 'new_v', 'new_v_rwkv_g2': 'new_v', 'new_v_rwkv_k_k': 'new_v', 'new_v_rwkv_k_a': 'new_v', 'new_v_rwkv_r_k': 'new_v', 'new_v_rwkv_ln_w': 'new_v', 'new_v_rwkv_ln_b': 'new_v', 'new_v_rwkv_proj': 'new_v', 'new_v_gdn_conv_w': 'new_v', 'new_v_gdn_a_log': 'new_v', 'new_v_gdn_dt_bias': 'new_v', 'new_v_gdn_norm_w': 'new_v', 'new_v_gdn_proj': 'new_v', 'new_v_w_out': 'new_v', 'new_v_norm2_g': 'new_v', 'new_v_ffn_up': 'new_v', 'new_v_ffn_conv_w': 'new_v', 'new_v_ffn_down': 'new_v', 'new_v_final_g': 'new_v'}


def _forward(args):
    return _fwd_reference(*[args[k] for k in FWD_PARAMS])


def _output_shape():
    out = _jax.eval_shape(lambda: _forward(_fwd_setup_inputs(0)))
    return out.shape, out.dtype

N_MICROBATCH = 1
ADAM_LR = 0.001
ADAM_B1 = 0.9
ADAM_B2 = 0.999
ADAM_EPS = 1e-08
ADAM_WD = 0.01
ADAM_STEP = 10
PER_EXAMPLE_BATCH_AXIS = {'x': 0, 'loss_target': 0}
SHARED_INPUTS = []
_WEIGHT_DTYPES = {'norm1_g': _jnp.float32, 'w_in': _jnp.float32, 'rwkv_mu': _jnp.float32, 'rwkv_w0': _jnp.float32, 'rwkv_w2': _jnp.float32, 'rwkv_a0': _jnp.float32, 'rwkv_a2': _jnp.float32, 'rwkv_g2': _jnp.float32, 'rwkv_k_k': _jnp.float32, 'rwkv_k_a': _jnp.float32, 'rwkv_r_k': _jnp.float32, 'rwkv_ln_w': _jnp.float32, 'rwkv_ln_b': _jnp.float32, 'rwkv_proj': _jnp.float32, 'gdn_conv_w': _jnp.float32, 'gdn_a_log': _jnp.float32, 'gdn_dt_bias': _jnp.float32, 'gdn_norm_w': _jnp.float32, 'gdn_proj': _jnp.float32, 'w_out': _jnp.float32, 'norm2_g': _jnp.float32, 'ffn_up': _jnp.float32, 'ffn_conv_w': _jnp.float32, 'ffn_down': _jnp.float32, 'final_g': _jnp.float32}
MOMENT_SCALE = {'norm1_g': 1.590245e-01, 'w_in': 5.989317e-02, 'rwkv_mu': 1.306198e-01, 'rwkv_w0': 2.486279e-02, 'rwkv_w2': 3.121470e-03, 'rwkv_a0': 3.251867e-02, 'rwkv_a2': 3.014036e-02, 'rwkv_g2': 7.533163e-02, 'rwkv_k_k': 1.462364e-01, 'rwkv_k_a': 8.062494e-02, 'rwkv_r_k': 1.660545e-01, 'rwkv_ln_w': 7.745605e-02, 'rwkv_ln_b': 8.106290e-02, 'rwkv_proj': 5.470059e-02, 'gdn_conv_w': 6.315749e-02, 'gdn_a_log': 2.944775e-01, 'gdn_dt_bias': 2.820463e-01, 'gdn_norm_w': 1.695255e-01, 'gdn_proj': 5.868606e-02, 'w_out': 8.026407e-02, 'norm2_g': 1.325319e-01, 'ffn_up': 5.395309e-02, 'ffn_conv_w': 5.380756e-02, 'ffn_down': 8.827554e-02, 'final_g': 3.195735e+01}


def _to_microbatches(a, axis):
    t = _jnp.moveaxis(a, axis, 0)
    t = t.reshape((N_MICROBATCH, t.shape[0] // N_MICROBATCH) + t.shape[1:])
    return _jnp.moveaxis(t, 1, axis + 1)


def setup_inputs(seed: int = 0) -> dict:
    inp = _fwd_setup_inputs(seed)
    key = _jax.random.fold_in(_jax.random.key(seed), 7919)
    shape, _ = _output_shape()
    out = dict(inp)
    out["loss_target"] = _jax.random.normal(_jax.random.fold_in(key, 0), shape, _jnp.float32)
    for i, name in enumerate(TWIN_WEIGHTS):
        w = inp[name].astype(_jnp.float32)
        if MOMENT_SCALE is None:
            s = _jnp.sqrt(_jnp.mean(_jnp.square(w)) + 1e-30)
        else:
            s = MOMENT_SCALE[name]
        km, kv = _jax.random.split(_jax.random.fold_in(key, i + 1))
        out[name] = w
        out["m_" + name] = s * _jax.random.normal(km, w.shape, _jnp.float32)
        out["v_" + name] = (s * s) * _jax.random.uniform(kv, w.shape, _jnp.float32, 0.5, 1.5)
    if N_MICROBATCH > 1:
        for name, axis in PER_EXAMPLE_BATCH_AXIS.items():
            out[name] = _to_microbatches(out[name], axis)
    return {'x': out['x'], 'norm1_g': out['norm1_g'], 'w_in': out['w_in'], 'rwkv_mu': out['rwkv_mu'], 'rwkv_w0': out['rwkv_w0'], 'rwkv_w2': out['rwkv_w2'], 'rwkv_a0': out['rwkv_a0'], 'rwkv_a2': out['rwkv_a2'], 'rwkv_g2': out['rwkv_g2'], 'rwkv_k_k': out['rwkv_k_k'], 'rwkv_k_a': out['rwkv_k_a'], 'rwkv_r_k': out['rwkv_r_k'], 'rwkv_ln_w': out['rwkv_ln_w'], 'rwkv_ln_b': out['rwkv_ln_b'], 'rwkv_proj': out['rwkv_proj'], 'gdn_conv_w': out['gdn_conv_w'], 'gdn_a_log': out['gdn_a_log'], 'gdn_dt_bias': out['gdn_dt_bias'], 'gdn_norm_w': out['gdn_norm_w'], 'gdn_proj': out['gdn_proj'], 'w_out': out['w_out'], 'norm2_g': out['norm2_g'], 'ffn_up': out['ffn_up'], 'ffn_conv_w': out['ffn_conv_w'], 'ffn_down': out['ffn_down'], 'final_g': out['final_g'], 'loss_target': out['loss_target'], 'm_norm1_g': out['m_norm1_g'], 'm_w_in': out['m_w_in'], 'm_rwkv_mu': out['m_rwkv_mu'], 'm_rwkv_w0': out['m_rwkv_w0'], 'm_rwkv_w2': out['m_rwkv_w2'], 'm_rwkv_a0': out['m_rwkv_a0'], 'm_rwkv_a2': out['m_rwkv_a2'], 'm_rwkv_g2': out['m_rwkv_g2'], 'm_rwkv_k_k': out['m_rwkv_k_k'], 'm_rwkv_k_a': out['m_rwkv_k_a'], 'm_rwkv_r_k': out['m_rwkv_r_k'], 'm_rwkv_ln_w': out['m_rwkv_ln_w'], 'm_rwkv_ln_b': out['m_rwkv_ln_b'], 'm_rwkv_proj': out['m_rwkv_proj'], 'm_gdn_conv_w': out['m_gdn_conv_w'], 'm_gdn_a_log': out['m_gdn_a_log'], 'm_gdn_dt_bias': out['m_gdn_dt_bias'], 'm_gdn_norm_w': out['m_gdn_norm_w'], 'm_gdn_proj': out['m_gdn_proj'], 'm_w_out': out['m_w_out'], 'm_norm2_g': out['m_norm2_g'], 'm_ffn_up': out['m_ffn_up'], 'm_ffn_conv_w': out['m_ffn_conv_w'], 'm_ffn_down': out['m_ffn_down'], 'm_final_g': out['m_final_g'], 'v_norm1_g': out['v_norm1_g'], 'v_w_in': out['v_w_in'], 'v_rwkv_mu': out['v_rwkv_mu'], 'v_rwkv_w0': out['v_rwkv_w0'], 'v_rwkv_w2': out['v_rwkv_w2'], 'v_rwkv_a0': out['v_rwkv_a0'], 'v_rwkv_a2': out['v_rwkv_a2'], 'v_rwkv_g2': out['v_rwkv_g2'], 'v_rwkv_k_k': out['v_rwkv_k_k'], 'v_rwkv_k_a': out['v_rwkv_k_a'], 'v_rwkv_r_k': out['v_rwkv_r_k'], 'v_rwkv_ln_w': out['v_rwkv_ln_w'], 'v_rwkv_ln_b': out['v_rwkv_ln_b'], 'v_rwkv_proj': out['v_rwkv_proj'], 'v_gdn_conv_w': out['v_gdn_conv_w'], 'v_gdn_a_log': out['v_gdn_a_log'], 'v_gdn_dt_bias': out['v_gdn_dt_bias'], 'v_gdn_norm_w': out['v_gdn_norm_w'], 'v_gdn_proj': out['v_gdn_proj'], 'v_w_out': out['v_w_out'], 'v_norm2_g': out['v_norm2_g'], 'v_ffn_up': out['v_ffn_up'], 'v_ffn_conv_w': out['v_ffn_conv_w'], 'v_ffn_down': out['v_ffn_down'], 'v_final_g': out['v_final_g']}


def _loss(weights, diff, rest, loss_target):
    with _jax.named_scope("forward"):
        args = {**rest, TWIN_DIFF_INPUT: diff, **{k: w.astype(_WEIGHT_DTYPES[k]) for k, w in weights.items()}}
        y = _forward(args)
    with _jax.named_scope("loss_head"):
        err = _jnp.square(y.astype(_jnp.float32) - loss_target)
        return 0.5 * _jnp.sum(_jnp.mean(err, axis=-1)) if err.ndim else 0.5 * err


def _adamw(w, g, m, v):
    m = ADAM_B1 * m + (1.0 - ADAM_B1) * g
    v = ADAM_B2 * v + (1.0 - ADAM_B2) * _jnp.square(g)
    m_hat = m / (1.0 - ADAM_B1 ** ADAM_STEP)
    v_hat = v / (1.0 - ADAM_B2 ** ADAM_STEP)
    delta = -ADAM_LR * (m_hat / (_jnp.sqrt(v_hat) + ADAM_EPS) + ADAM_WD * w)
    return delta, m, v


def reference(x, norm1_g, w_in, rwkv_mu, rwkv_w0, rwkv_w2, rwkv_a0, rwkv_a2, rwkv_g2, rwkv_k_k, rwkv_k_a, rwkv_r_k, rwkv_ln_w, rwkv_ln_b, rwkv_proj, gdn_conv_w, gdn_a_log, gdn_dt_bias, gdn_norm_w, gdn_proj, w_out, norm2_g, ffn_up, ffn_conv_w, ffn_down, final_g, loss_target, m_norm1_g, m_w_in, m_rwkv_mu, m_rwkv_w0, m_rwkv_w2, m_rwkv_a0, m_rwkv_a2, m_rwkv_g2, m_rwkv_k_k, m_rwkv_k_a, m_rwkv_r_k, m_rwkv_ln_w, m_rwkv_ln_b, m_rwkv_proj, m_gdn_conv_w, m_gdn_a_log, m_gdn_dt_bias, m_gdn_norm_w, m_gdn_proj, m_w_out, m_norm2_g, m_ffn_up, m_ffn_conv_w, m_ffn_down, m_final_g, v_norm1_g, v_w_in, v_rwkv_mu, v_rwkv_w0, v_rwkv_w2, v_rwkv_a0, v_rwkv_a2, v_rwkv_g2, v_rwkv_k_k, v_rwkv_k_a, v_rwkv_r_k, v_rwkv_ln_w, v_rwkv_ln_b, v_rwkv_proj, v_gdn_conv_w, v_gdn_a_log, v_gdn_dt_bias, v_gdn_norm_w, v_gdn_proj, v_w_out, v_norm2_g, v_ffn_up, v_ffn_conv_w, v_ffn_down, v_final_g):
    given = dict(x=x, norm1_g=norm1_g, w_in=w_in, rwkv_mu=rwkv_mu, rwkv_w0=rwkv_w0, rwkv_w2=rwkv_w2, rwkv_a0=rwkv_a0, rwkv_a2=rwkv_a2, rwkv_g2=rwkv_g2, rwkv_k_k=rwkv_k_k, rwkv_k_a=rwkv_k_a, rwkv_r_k=rwkv_r_k, rwkv_ln_w=rwkv_ln_w, rwkv_ln_b=rwkv_ln_b, rwkv_proj=rwkv_proj, gdn_conv_w=gdn_conv_w, gdn_a_log=gdn_a_log, gdn_dt_bias=gdn_dt_bias, gdn_norm_w=gdn_norm_w, gdn_proj=gdn_proj, w_out=w_out, norm2_g=norm2_g, ffn_up=ffn_up, ffn_conv_w=ffn_conv_w, ffn_down=ffn_down, final_g=final_g, loss_target=loss_target, m_norm1_g=m_norm1_g, m_w_in=m_w_in, m_rwkv_mu=m_rwkv_mu, m_rwkv_w0=m_rwkv_w0, m_rwkv_w2=m_rwkv_w2, m_rwkv_a0=m_rwkv_a0, m_rwkv_a2=m_rwkv_a2, m_rwkv_g2=m_rwkv_g2, m_rwkv_k_k=m_rwkv_k_k, m_rwkv_k_a=m_rwkv_k_a, m_rwkv_r_k=m_rwkv_r_k, m_rwkv_ln_w=m_rwkv_ln_w, m_rwkv_ln_b=m_rwkv_ln_b, m_rwkv_proj=m_rwkv_proj, m_gdn_conv_w=m_gdn_conv_w, m_gdn_a_log=m_gdn_a_log, m_gdn_dt_bias=m_gdn_dt_bias, m_gdn_norm_w=m_gdn_norm_w, m_gdn_proj=m_gdn_proj, m_w_out=m_w_out, m_norm2_g=m_norm2_g, m_ffn_up=m_ffn_up, m_ffn_conv_w=m_ffn_conv_w, m_ffn_down=m_ffn_down, m_final_g=m_final_g, v_norm1_g=v_norm1_g, v_w_in=v_w_in, v_rwkv_mu=v_rwkv_mu, v_rwkv_w0=v_rwkv_w0, v_rwkv_w2=v_rwkv_w2, v_rwkv_a0=v_rwkv_a0, v_rwkv_a2=v_rwkv_a2, v_rwkv_g2=v_rwkv_g2, v_rwkv_k_k=v_rwkv_k_k, v_rwkv_k_a=v_rwkv_k_a, v_rwkv_r_k=v_rwkv_r_k, v_rwkv_ln_w=v_rwkv_ln_w, v_rwkv_ln_b=v_rwkv_ln_b, v_rwkv_proj=v_rwkv_proj, v_gdn_conv_w=v_gdn_conv_w, v_gdn_a_log=v_gdn_a_log, v_gdn_dt_bias=v_gdn_dt_bias, v_gdn_norm_w=v_gdn_norm_w, v_gdn_proj=v_gdn_proj, v_w_out=v_w_out, v_norm2_g=v_norm2_g, v_ffn_up=v_ffn_up, v_ffn_conv_w=v_ffn_conv_w, v_ffn_down=v_ffn_down, v_final_g=v_final_g)
    weights = {n: given[n] for n in TWIN_WEIGHTS}
    shared = {n: given[n] for n in SHARED_INPUTS}
    per_example = {n: given[n] for n in ['x']}
    grad_fn = _jax.value_and_grad(_loss, argnums=(0, 1))

    def one_microbatch(ex, loss_target):
        ex = dict(ex)
        diff = ex.pop(TWIN_DIFF_INPUT)
        return grad_fn(weights, diff, {**shared, **ex}, loss_target)

    if N_MICROBATCH == 1:
        loss, (grad_w, grad_x) = one_microbatch(per_example, given["loss_target"])
    else:
        def body(carry, xs):
            loss_sum, grad_sum = carry
            l_k, (gw_k, gx_k) = one_microbatch(xs[0], xs[1])
            with _jax.named_scope("update"):
                return (loss_sum + l_k, _jax.tree.map(_jnp.add, grad_sum, gw_k)), gx_k

        init = (_jnp.zeros((), _jnp.float32), _jax.tree.map(_jnp.zeros_like, weights))
        (loss, grad_w), grad_x = _jax.lax.scan(body, init, (per_example, given["loss_target"]))
    with _jax.named_scope("update"):
        delta_w, new_m, new_v = {}, {}, {}
        for n in TWIN_WEIGHTS:
            delta_w[n], new_m[n], new_v[n] = _adamw(weights[n], grad_w[n], given["m_" + n], given["v_" + n])
    return (loss, grad_x, *[grad_w[n] for n in TWIN_WEIGHTS], *[delta_w[n] for n in TWIN_WEIGHTS],
            *[new_m[n] for n in TWIN_WEIGHTS], *[new_v[n] for n in TWIN_WEIGHTS])
```

```python
import functools

import jax
import jax.numpy as jnp
from jax import lax
from jax.experimental import pallas as pl
from jax.experimental.pallas import tpu as pltpu

f32 = jnp.float32
bf16 = jnp.bfloat16
HI = lax.Precision.HIGHEST

D_MODEL = 1024
RWKV_HEADS, RWKV_HD, RWKV_W = 8, 64, 512
GDN_HEADS, GDN_HD, GDN_W = 4, 128, 512
FFN_H = 2816
NORM_EPS, L2_EPS, GN_EPS = 1e-6, 1e-6, 64e-5
W_RWKV, W_QKV, W_Z, W_GATES, W_AB = 1792, 1536, 512, 2048, 256
OFF_QKV, OFF_Z, OFF_GATES, OFF_AB = 1792, 3328, 3840, 5888
W_IN_PAD = OFF_AB + W_AB
IN_WIDTH = 5896
WKV_CHUNK = 16
GDN_CHUNK = 64
HALO = 8
LANES = 1024
PACK_ROW_TILE = 256
VMEM_LIMIT = 56 * 1024 * 1024

ADAM_LR, ADAM_B1, ADAM_B2, ADAM_EPS, ADAM_WD, ADAM_STEP = 0.001, 0.9, 0.999, 1e-08, 0.01, 10

SHARDED = ('w_in', 'rwkv_w2', 'rwkv_a2', 'rwkv_g2', 'rwkv_proj', 'gdn_conv_w', 'gdn_proj', 'w_out', 'ffn_up',
           'ffn_conv_w', 'ffn_down')
ROW_SHARDED = ('w_out', 'ffn_down')
SMALL = ('norm1_g', 'rwkv_mu', 'rwkv_w0', 'rwkv_a0', 'rwkv_k_k', 'rwkv_k_a', 'rwkv_r_k', 'rwkv_ln_w', 'rwkv_ln_b',
         'gdn_a_log', 'gdn_dt_bias', 'gdn_norm_w', 'norm2_g', 'final_g')
WEIGHTS = ('norm1_g', 'w_in', 'rwkv_mu', 'rwkv_w0', 'rwkv_w2', 'rwkv_a0', 'rwkv_a2', 'rwkv_g2', 'rwkv_k_k', 'rwkv_k_a',
           'rwkv_r_k', 'rwkv_ln_w', 'rwkv_ln_b', 'rwkv_proj', 'gdn_conv_w', 'gdn_a_log', 'gdn_dt_bias', 'gdn_norm_w',
           'gdn_proj', 'w_out', 'norm2_g', 'ffn_up', 'ffn_conv_w', 'ffn_down', 'final_g')


def _params(*sem):
    return pltpu.CompilerParams(dimension_semantics=sem, vmem_limit_bytes=VMEM_LIMIT)


def _pick(n, cands):
    for c in cands:
        if n % c == 0:
            return c
    raise ValueError(f"no tile for {n}")


def _mm(a, b, mode, name, add=None):
    if mode == 'nn':
        (M, K), N = a.shape, b.shape[1]
    elif mode == 'nt':
        (M, K), N = a.shape, b.shape[0]
    else:
        (K, M), N = a.shape, b.shape[1]
    tm = _pick(M, (512, 256, 128))
    tn = _pick(N, (512, 256, 128))
    tk = _pick(K, (1024, 512, 256, 128))
    dn = {'nn': (((1,), (0,)), ((), ())), 'nt': (((1,), (1,)), ((), ())), 'tn': (((0,), (0,)), ((), ()))}[mode]

    def body(a_ref, b_ref, *rest):
        o_ref = rest[-1]
        k = pl.program_id(2)
        acc = lax.dot_general(a_ref[...].astype(bf16), b_ref[...].astype(bf16), dn, preferred_element_type=f32)

        @pl.when(k == 0)
        def _():
            o_ref[...] = acc + rest[0][...] if add is not None else acc

        @pl.when(k > 0)
        def _():
            o_ref[...] += acc

    a_spec = (pl.BlockSpec((tk, tm), lambda i, j, k: (k, i)) if mode == 'tn'
              else pl.BlockSpec((tm, tk), lambda i, j, k: (i, k)))
    b_spec = (pl.BlockSpec((tn, tk), lambda i, j, k: (j, k)) if mode == 'nt'
              else pl.BlockSpec((tk, tn), lambda i, j, k: (k, j)))
    o_spec = pl.BlockSpec((tm, tn), lambda i, j, k: (i, j))
    ins, specs = [a, b], [a_spec, b_spec]
    if add is not None:
        ins.append(add)
        specs.append(o_spec)
    return pl.pallas_call(
        body, grid=(M // tm, N // tn, K // tk), in_specs=specs, out_specs=o_spec,
        out_shape=jax.ShapeDtypeStruct((M, N), f32), name=name,
        compiler_params=_params("parallel", "parallel", "arbitrary"))(*ins)


def _shift_down(cur, prev, s):
    if s == 0:
        return cur
    ext = jnp.concatenate([prev, cur], axis=0)
    return pltpu.roll(ext, s, 0)[HALO:]


def _shift_up(cur, nxt, s):
    if s == 0:
        return cur
    ext = jnp.concatenate([cur, nxt], axis=0)
    return pltpu.roll(ext, ext.shape[0] - s, 0)[:cur.shape[0]]


def _conv_apply(cur, prev, w_ref):
    taps = w_ref.shape[0]
    out = None
    for i in range(taps):
        term = _shift_down(cur, prev, taps - 1 - i) * w_ref[pl.ds(i, 1), :]
        out = term if out is None else out + term
    return out


def _row_spec(tm, w):
    return pl.BlockSpec((tm, w), lambda i: (i, 0))


def _prev_spec(tm, w):
    return pl.BlockSpec((HALO, w), lambda i: (jnp.maximum(i * (tm // HALO) - 1, 0), 0))


def _next_spec(tm, w, T):
    return pl.BlockSpec((HALO, w), lambda i: (jnp.minimum((i + 1) * (tm // HALO), T // HALO - 1), 0))


def _full_spec(shape):
    return pl.BlockSpec(shape, lambda i: (0,) * len(shape))


def _pw_fwd(name, fn, rows, consts, out_widths, tm, conv_w=None):
    T = rows[0].shape[0]
    nr, nc = len(rows), len(consts)

    def body(*refs):
        i = pl.program_id(0)
        vals = [r[...] for r in refs[:nr]]
        p = nr
        if conv_w is not None:
            prev = jnp.where(i > 0, refs[p][...], 0.0)
            vals[0] = _conv_apply(vals[0], prev, refs[p + 1])
            p += 2
        cvals = [r[...] for r in refs[p:p + nc]]
        outs = fn(*vals, *cvals)
        for o_ref, o in zip(refs[p + nc:], outs):
            o_ref[...] = o

    ins = list(rows)
    specs = [_row_spec(tm, r.shape[1]) for r in rows]
    if conv_w is not None:
        ins += [rows[0], conv_w]
        specs += [_prev_spec(tm, rows[0].shape[1]), _full_spec(conv_w.shape)]
    ins += list(consts)
    specs += [_full_spec(c.shape) for c in consts]
    outs = pl.pallas_call(
        body, grid=(T // tm,), in_specs=specs,
        out_specs=[_row_spec(tm, w) for w in out_widths],
        out_shape=[jax.ShapeDtypeStruct((T, w), f32) for w in out_widths], name=name,
        compiler_params=_params("parallel"))(*ins)
    return outs


def _pw_bwd(name, fn, rows, consts, cots, tm, conv_w=None, add_to_first=None):
    T = rows[0].shape[0]
    nr, nc = len(rows), len(consts)
    flat_cots = [c for grp in cots for c in grp]
    nct = len(flat_cots)

    def body(*refs):
        i = pl.program_id(0)
        vals = [r[...] for r in refs[:nr]]
        p = nr
        if conv_w is not None:
            prev = jnp.where(i > 0, refs[p][...], 0.0)
            vals[0] = _conv_apply(vals[0], prev, refs[p + 1])
            p += 2
        cvals = [r[...] for r in refs[p:p + nc]]
        p += nc
        cot_vals = []
        for grp in cots:
            acc = refs[p][...]
            for q in range(1, len(grp)):
                acc = acc + refs[p + q][...]
            p += len(grp)
            cot_vals.append(acc)
        extra = None
        if add_to_first is not None:
            extra = refs[p][...]
            p += 1
        _, vjp = jax.vjp(fn, *vals, *cvals)
        grads = vjp(tuple(cot_vals))
        row_out = refs[p:p + nr]
        const_out = refs[p + nr:]
        for q in range(nr):
            g = grads[q]
            if q == 0 and extra is not None:
                g = g + extra
            row_out[q][...] = g

        @pl.when(i == 0)
        def _():
            for q in range(nc):
                const_out[q][...] = grads[nr + q]

        @pl.when(i > 0)
        def _():
            for q in range(nc):
                const_out[q][...] += grads[nr + q]

    ins = list(rows)
    specs = [_row_spec(tm, r.shape[1]) for r in rows]
    if conv_w is not None:
        ins += [rows[0], conv_w]
        specs += [_prev_spec(tm, rows[0].shape[1]), _full_spec(conv_w.shape)]
    ins += list(consts)
    specs += [_full_spec(c.shape) for c in consts]
    ins += flat_cots
    specs += [_row_spec(tm, c.shape[1]) for c in flat_cots]
    if add_to_first is not None:
        ins.append(add_to_first)
        specs.append(_row_spec(tm, add_to_first.shape[1]))
    out_shapes = [jax.ShapeDtypeStruct(r.shape, f32) for r in rows] + [jax.ShapeDtypeStruct(c.shape, f32) for c in consts]
    out_specs = [_row_spec(tm, r.shape[1]) for r in rows] + [_full_spec(c.shape) for c in consts]
    outs = pl.pallas_call(
        body, grid=(T // tm,), in_specs=specs, out_specs=out_specs, out_shape=out_shapes, name=name,
        compiler_params=_params("arbitrary"))(*ins)
    return list(outs[:nr]), list(outs[nr:])


def _conv_bwd(name, dc, x, w, tm):
    T, W = x.shape
    taps = w.shape[0]
    nblk = T // tm

    def body(dc_ref, dcn_ref, x_ref, xp_ref, w_ref, dx_ref, dw_ref):
        i = pl.program_id(0)
        dcv, xv = dc_ref[...], x_ref[...]
        nxt = jnp.where(i < nblk - 1, dcn_ref[...], 0.0)
        prev = jnp.where(i > 0, xp_ref[...], 0.0)

        @pl.when(i == 0)
        def _():
            dw_ref[...] = jnp.zeros_like(dw_ref)

        dx = None
        for k in range(taps):
            s = taps - 1 - k
            term = _shift_up(dcv, nxt, s) * w_ref[pl.ds(k, 1), :]
            dx = term if dx is None else dx + term
            dw_ref[pl.ds(k, 1), :] += jnp.sum(dcv * _shift_down(xv, prev, s), axis=0, keepdims=True)
        dx_ref[...] = dx

    return pl.pallas_call(
        body, grid=(nblk,),
        in_specs=[_row_spec(tm, W), _next_spec(tm, W, T), _row_spec(tm, W), _prev_spec(tm, W), _full_spec(w.shape)],
        out_specs=[_row_spec(tm, W), _full_spec(w.shape)],
        out_shape=[jax.ShapeDtypeStruct((T, W), f32), jax.ShapeDtypeStruct(w.shape, f32)], name=name,
        compiler_params=_params("arbitrary"))(dc, dc, x, x, w)


def _sigmoid(x):
    return 1.0 / (1.0 + jnp.exp(-x))


def _softplus(x):
    return jnp.maximum(x, 0.0) + jnp.log(1.0 + jnp.exp(jnp.minimum(x, -x)))


def _seg_sum(x, seg):
    w = x.shape[-1]
    r = lax.broadcasted_iota(jnp.int32, (w, w), 0) // seg
    c = lax.broadcasted_iota(jnp.int32, (w, w), 1) // seg
    return jnp.dot(x, (r == c).astype(f32), precision=HI, preferred_element_type=f32)


def _rms(x, g):
    return x * lax.rsqrt(jnp.mean(x * x, axis=-1, keepdims=True) + NORM_EPS) * g


def _rms_fn(x, g):
    return (_rms(x, g),)


def _loss_rows(x2, tgt, g):
    e = _rms(x2, g) - tgt
    return 0.5 * jnp.sum(e * e, axis=-1, keepdims=True) * (1.0 / D_MODEL)


def _rwkv_prep_fn(ps, w0, w2p, a0, a2p, g2, k_k, k_a):
    r, k, v = ps[:, 0:512], ps[:, 512:1024], ps[:, 1024:1536]
    wa, gl = ps[:, 1536:1664], ps[:, 1664:1792]
    z = w0 + jnp.dot(jnp.tanh(wa), w2p, precision=HI, preferred_element_type=f32)
    w_log = -_softplus(-z) - 0.5
    lw = -jnp.exp(w_log)
    a = _sigmoid(a0 + jnp.dot(wa, a2p, precision=HI, preferred_element_type=f32))
    g = jnp.dot(_sigmoid(gl), g2, precision=HI, preferred_element_type=f32)
    kx = k * k_k
    kk = kx * lax.rsqrt(_seg_sum(kx * kx, RWKV_HD) + L2_EPS)
    k2 = k * (1.0 + (a - 1.0) * k_a)
    return r, lw, k2, v, -kk, kk * a, g


def _rwkv_post_fn(y, r, k2, v, g, ln_w, ln_b, rk):
    mean = _seg_sum(y, RWKV_HD) * (1.0 / RWKV_HD)
    yc = y - mean
    var = _seg_sum(yc * yc, RWKV_HD) * (1.0 / RWKV_HD)
    yn = yc * lax.rsqrt(var + GN_EPS) * ln_w + ln_b
    bonus = _seg_sum(r * k2 * rk, RWKV_HD) * v
    return ((yn + bonus) * g,)


def _gdn_prep_fn(c, ab, al_p, dt_p):
    s = c * _sigmoid(c)
    q, k, v = s[:, 0:512], s[:, 512:1024], s[:, 1024:1536]
    q = q * lax.rsqrt(_seg_sum(q * q, GDN_HD) + L2_EPS) * (GDN_HD ** -0.5)
    k = k * lax.rsqrt(_seg_sum(k * k, GDN_HD) + L2_EPS)
    lane = lax.broadcasted_iota(jnp.int32, ab.shape, 1)
    gpart = -jnp.exp(al_p) * _softplus(ab + dt_p)
    gbeta = jnp.where(lane < GDN_HEADS, gpart, jnp.where(lane < 2 * GDN_HEADS, _sigmoid(ab), 0.0))
    return q, k, v, gbeta


def _gdn_post_fn(o, z, nw):
    ms = _seg_sum(o * o, GDN_HD) * (1.0 / GDN_HD)
    return (o * lax.rsqrt(ms + NORM_EPS) * nw * (z * _sigmoid(z)),)


def _mix_fn(gates, ya, yb):
    return (_sigmoid(gates[:, :D_MODEL]) * ya + _sigmoid(gates[:, D_MODEL:]) * yb,)


def _ffn_fn(c):
    hg, hu = c[:, :FFN_H], c[:, FFN_H:]
    return (hg * _sigmoid(hg) * hu,)


def _bmm(a, b):
    return jnp.einsum('hcs,hsd->hcd', a, b, precision=HI, preferred_element_type=f32)


def _bmm_nt(a, b):
    return jnp.einsum('hcd,hsd->hcs', a, b, precision=HI, preferred_element_type=f32)


def _bmm_tn(a, b):
    return jnp.einsum('hcd,hce->hde', a, b, precision=HI, preferred_element_type=f32)


def _masks(H, C):
    row = lax.broadcasted_iota(jnp.int32, (H, C, C), 1)
    col = lax.broadcasted_iota(jnp.int32, (H, C, C), 2)
    return row, col


def _tri_inv(L):
    H, C, _ = L.shape
    row, col = _masks(H, C)
    eye = (row == col).astype(f32)
    base = 16
    same = (row // base) == (col // base)
    Ld = jnp.where(same, L, 0.0)
    X = -Ld
    inv = eye + X
    for _ in range(3):
        X = _bmm(X, X)
        inv = _bmm(inv, eye + X)
    if C == base:
        return inv
    N = _bmm(inv, L - Ld)
    out = eye - N
    levels = C // base
    P = N
    span = 2
    while span < levels:
        P = _bmm(P, P)
        out = _bmm(out, eye + P)
        span *= 2
    return _bmm(out, inv)


def _wkv_chunk(Z, r, lw, k, v, a, b):
    H, C, D = r.shape
    row, col = _masks(H, C)
    incl, strict = row >= col, row > col
    ones_cc = jnp.ones((H, C, C), f32)
    cw = _bmm(incl.astype(f32), lw)
    cwp = cw - lw
    cwl = _bmm(ones_cc, lw)
    en = jnp.exp(-cw)
    at, rt, bt, kt = a * jnp.exp(cwp), r * jnp.exp(cw), b * en, k * en
    Lab = jnp.where(strict, _bmm_nt(at, bt), 0.0)
    Lak = jnp.where(strict, _bmm_nt(at, kt), 0.0)
    Tm = _tri_inv(-Lab)
    U = _bmm(Tm, _bmm(at, Z) + _bmm(Lak, v))
    Rb = jnp.where(incl, _bmm_nt(rt, bt), 0.0)
    Rk = jnp.where(incl, _bmm_nt(rt, kt), 0.0)
    y = _bmm(rt, Z) + _bmm(Rb, U) + _bmm(Rk, v)
    ed = jnp.exp(cwl - cw)
    zdec = jnp.exp(_bmm_tn(lw, jnp.ones((H, C, Z.shape[2]), f32)))
    Z1 = Z * zdec + _bmm_tn(b * ed, U) + _bmm_tn(k * ed, v)
    return y, Z1


def _gdn_chunk(S, q, k, v, g, beta):
    H, C, D = q.shape
    row, col = _masks(H, C)
    incl, strict = row >= col, row > col
    tril = incl.astype(f32)
    gD = jnp.broadcast_to(g, (H, C, D))
    gcD = _bmm(tril, gD)
    gcC = _bmm(tril, jnp.broadcast_to(g, (H, C, C)))
    diff = gcC - jnp.swapaxes(gcC, 1, 2)
    decay = jnp.where(incl, jnp.exp(jnp.where(incl, diff, 0.0)), 0.0)
    glD = _bmm(jnp.ones((H, C, C), f32), gD)
    glS = _bmm(jnp.ones((H, S.shape[1], C), f32), jnp.broadcast_to(g, (H, C, S.shape[2])))
    bD = jnp.broadcast_to(beta, (H, C, D))
    kb, vb = k * bD, v * bD
    L = jnp.where(strict, _bmm_nt(kb, k) * decay, 0.0)
    Tm = _tri_inv(L)
    egc = jnp.exp(gcD)
    u = _bmm(Tm, vb)
    wk = _bmm(Tm, kb * egc)
    attn = jnp.where(incl, _bmm_nt(q, k) * decay, 0.0)
    v_new = u - _bmm(wk, S)
    o = _bmm(q * egc, S) + _bmm(attn, v_new)
    S1 = S * jnp.exp(glS) + _bmm_tn(k * jnp.exp(glD - gcD), v_new)
    return o, S1


def _scan_fwd(name, chunk_fn, ins, C, CB, dv):
    H, T, Dk = ins[0].shape
    n_in = len(ins)
    blk = C * CB

    def body(*refs):
        in_refs, y_ref, zs_ref, z_scr = refs[:n_in], refs[n_in], refs[n_in + 1], refs[n_in + 2]

        @pl.when(pl.program_id(0) == 0)
        def _():
            z_scr[...] = jnp.zeros_like(z_scr)

        Z = z_scr[...]
        for j in range(CB):
            zs_ref[j] = Z
            y, Z = chunk_fn(Z, *[r[:, j * C:(j + 1) * C, :] for r in in_refs])
            y_ref[:, j * C:(j + 1) * C, :] = y
        z_scr[...] = Z

    return pl.pallas_call(
        body, grid=(T // blk,),
        in_specs=[pl.BlockSpec((H, blk, a.shape[2]), lambda i: (0, i, 0)) for a in ins],
        out_specs=[pl.BlockSpec((H, blk, dv), lambda i: (0, i, 0)),
                   pl.BlockSpec((CB, H, Dk, dv), lambda i: (i, 0, 0, 0))],
        out_shape=[jax.ShapeDtypeStruct((H, T, dv), f32), jax.ShapeDtypeStruct((T // C, H, Dk, dv), f32)],
        scratch_shapes=[pltpu.VMEM((H, Dk, dv), f32)], name=name,
        compiler_params=_params("arbitrary"))(*ins)


def _scan_bwd(name, chunk_fn, ins, dy, zs, C, CB):
    H, T, Dk = ins[0].shape
    dv = dy.shape[2]
    n_in = len(ins)
    blk = C * CB
    nblk = T // blk

    def body(*refs):
        in_refs, dy_ref, zs_ref = refs[:n_in], refs[n_in], refs[n_in + 1]
        out_refs, dz_scr = refs[n_in + 2:2 * n_in + 2], refs[2 * n_in + 2]

        @pl.when(pl.program_id(0) == 0)
        def _():
            dz_scr[...] = jnp.zeros_like(dz_scr)

        dZ = dz_scr[...]
        for j in reversed(range(CB)):
            sl = slice(j * C, (j + 1) * C)
            _, vjp = jax.vjp(chunk_fn, zs_ref[j], *[r[:, sl, :] for r in in_refs])
            grads = vjp((dy_ref[:, sl, :], dZ))
            dZ = grads[0]
            for o_ref, gval in zip(out_refs, grads[1:]):
                o_ref[:, sl, :] = gval
        dz_scr[...] = dZ

    rev = lambda i: (0, nblk - 1 - i, 0)
    return pl.pallas_call(
        body, grid=(nblk,),
        in_specs=[pl.BlockSpec((H, blk, a.shape[2]), rev) for a in ins]
        + [pl.BlockSpec((H, blk, dv), rev), pl.BlockSpec((CB, H, Dk, dv), lambda i: (nblk - 1 - i, 0, 0, 0))],
        out_specs=[pl.BlockSpec((H, blk, a.shape[2]), rev) for a in ins],
        out_shape=[jax.ShapeDtypeStruct(a.shape, f32) for a in ins],
        scratch_shapes=[pltpu.VMEM((H, Dk, dv), f32)], name=name,
        compiler_params=_params("arbitrary"))(*ins, dy, zs)


def _to_heads(x, H):
    T, W = x.shape
    return x.reshape(T, H, W // H).transpose(1, 0, 2)


def _from_heads(x):
    H, T, d = x.shape
    return x.transpose(1, 0, 2).reshape(T, H * d)


def _loss_call(x2, tgt, g, tm):
    T, W = x2.shape

    def body(x_ref, t_ref, g_ref, dx_ref, dg_ref, l_ref):
        i = pl.program_id(0)
        tv = t_ref[...]
        l, vjp = jax.vjp(lambda xv, gv: _loss_rows(xv, tv, gv), x_ref[...], g_ref[...])
        dx, dg = vjp(jnp.ones_like(l))
        dx_ref[...] = dx
        tot = jnp.zeros((1, 128), f32) + jnp.sum(l)

        @pl.when(i == 0)
        def _():
            dg_ref[...] = dg
            l_ref[...] = tot

        @pl.when(i > 0)
        def _():
            dg_ref[...] += dg
            l_ref[...] += tot

    return pl.pallas_call(
        body, grid=(T // tm,),
        in_specs=[_row_spec(tm, W), _row_spec(tm, W), _full_spec(g.shape)],
        out_specs=[_row_spec(tm, W), _full_spec(g.shape), _full_spec((1, 128))],
        out_shape=[jax.ShapeDtypeStruct((T, W), f32), jax.ShapeDtypeStruct(g.shape, f32),
                   jax.ShapeDtypeStruct((1, 128), f32)], name="loss_head",
        compiler_params=_params("arbitrary"))(x2, tgt, g)


def _local_step(x, tgt, W):
    T = x.shape[0]
    row = lambda a: a.reshape(1, -1)
    wp = W['w_in_pad']
    w_rwkv, w_qkv, w_z = wp[:, :OFF_QKV], wp[:, OFF_QKV:OFF_Z], wp[:, OFF_Z:OFF_GATES]
    w_gates, w_ab = wp[:, OFF_GATES:OFF_AB], wp[:, OFF_AB:]
    mu = row(W['rwkv_mu'])
    mixw = jnp.concatenate([mu, 1.0 - mu], axis=0)
    zpad = jnp.zeros((64, RWKV_W), f32)
    w2p = jnp.concatenate([W['rwkv_w2'], zpad], axis=0)
    a2p = jnp.concatenate([zpad, W['rwkv_a2']], axis=0)
    rw_consts = [row(W['rwkv_w0']), w2p, row(W['rwkv_a0']), a2p, W['rwkv_g2'], row(W['rwkv_k_k']), row(W['rwkv_k_a'])]
    post_consts = [row(W['rwkv_ln_w']), row(W['rwkv_ln_b']), row(W['rwkv_r_k'])]
    pad4 = lambda a: jnp.pad(row(a), ((0, 0), (0, W_AB - GDN_HEADS)))
    gd_consts = [pad4(W['gdn_a_log']), pad4(W['gdn_dt_bias'])]
    nw_t = jnp.tile(row(W['gdn_norm_w']), (1, GDN_HEADS))
    g1, g2n, gf = row(W['norm1_g']), row(W['norm2_g']), row(W['final_g'])

    (u,) = _pw_fwd("norm1", _rms_fn, [x], [g1], [D_MODEL], 256)
    p_rwkv = _mm(u, w_rwkv, 'nn', "in_rwkv")
    qkv_raw = _mm(u, w_qkv, 'nn', "in_qkv")
    z = _mm(u, w_z, 'nn', "in_z")
    gates = _mm(u, w_gates, 'nn', "in_gates")
    ab = _mm(u, w_ab, 'nn', "in_ab")

    r, lw, k2, v, a_, b_, g = _pw_fwd("rwkv_prep", _rwkv_prep_fn, [p_rwkv], rw_consts, [RWKV_W] * 7, 256, conv_w=mixw)
    wkv_in = [_to_heads(t, RWKV_HEADS) for t in (r, lw, k2, v, a_, b_)]
    y_h, zs_wkv = _scan_fwd("wkv_fwd", _wkv_chunk, wkv_in, WKV_CHUNK, 4, RWKV_HD)
    y = _from_heads(y_h)
    (ya_in,) = _pw_fwd("rwkv_post", _rwkv_post_fn, [y, r, k2, v, g], post_consts, [RWKV_W], 256)
    ya = _mm(ya_in, W['rwkv_proj'], 'nn', "rwkv_proj")

    gq, gk, gv, gbeta = _pw_fwd("gdn_prep", _gdn_prep_fn, [qkv_raw, ab], gd_consts, [GDN_W] * 3 + [W_AB], 256,
                                conv_w=W['gdn_conv_w'])
    col = lambda t, s: t[:, s:s + GDN_HEADS].T[:, :, None]
    gdn_in = [_to_heads(t, GDN_HEADS) for t in (gq, gk, gv)] + [col(gbeta, 0), col(gbeta, GDN_HEADS)]
    o_h, zs_gdn = _scan_fwd("gdn_fwd", _gdn_chunk, gdn_in, GDN_CHUNK, 1, GDN_HD)
    o = _from_heads(o_h)
    (yb_in,) = _pw_fwd("gdn_post", _gdn_post_fn, [o, z], [nw_t], [GDN_W], 256)
    yb = _mm(yb_in, W['gdn_proj'], 'nn', "gdn_proj")

    (mixed,) = _pw_fwd("mix", _mix_fn, [gates, ya, yb], [], [D_MODEL], 256)
    x1 = _mm(mixed, W['w_out'], 'nn', "w_out", add=x)
    (u2,) = _pw_fwd("norm2", _rms_fn, [x1], [g2n], [D_MODEL], 256)
    h = _mm(u2, W['ffn_up'], 'nn', "ffn_up")
    (act,) = _pw_fwd("ffn_act", _ffn_fn, [h], [], [FFN_H], 128, conv_w=W['ffn_conv_w'])
    x2 = _mm(act, W['ffn_down'], 'nn', "ffn_down", add=x1)

    G = {}
    dx2, dgf, loss = _loss_call(x2, tgt, gf, 256)
    G['final_g'] = dgf
    dact = _mm(dx2, W['ffn_down'], 'nt', "d_act")
    G['ffn_down'] = _mm(act, dx2, 'tn', "g_ffn_down")
    (dc_ffn,), _ = _pw_bwd("ffn_act_bwd", _ffn_fn, [h], [], [(dact,)], 128, conv_w=W['ffn_conv_w'])
    dh, G['ffn_conv_w'] = _conv_bwd("ffn_conv_bwd", dc_ffn, h, W['ffn_conv_w'], 128)
    du2 = _mm(dh, W['ffn_up'], 'nt', "d_u2")
    G['ffn_up'] = _mm(u2, dh, 'tn', "g_ffn_up")
    (dx1,), (G['norm2_g'],) = _pw_bwd("norm2_bwd", _rms_fn, [x1], [g2n], [(du2,)], 256, add_to_first=dx2)
    dmixed = _mm(dx1, W['w_out'], 'nt', "d_mixed")
    G['w_out'] = _mm(mixed, dx1, 'tn', "g_w_out")
    (dgates, dya, dyb), _ = _pw_bwd("mix_bwd", _mix_fn, [gates, ya, yb], [], [(dmixed,)], 256)
    dya_in = _mm(dya, W['rwkv_proj'], 'nt', "d_ya_in")
    G['rwkv_proj'] = _mm(ya_in, dya, 'tn', "g_rwkv_proj")
    dyb_in = _mm(dyb, W['gdn_proj'], 'nt', "d_yb_in")
    G['gdn_proj'] = _mm(yb_in, dyb, 'tn', "g_gdn_proj")

    (do, dz), (dnw_t,) = _pw_bwd("gdn_post_bwd", _gdn_post_fn, [o, z], [nw_t], [(dyb_in,)], 256)
    G['gdn_norm_w'] = dnw_t.reshape(GDN_HEADS, GDN_HD).sum(axis=0)
    dq_h, dk_h, dv_h, dg_h, dbeta_h = _scan_bwd("gdn_bwd", _gdn_chunk, gdn_in, _to_heads(do, GDN_HEADS), zs_gdn,
                                                GDN_CHUNK, 1)
    dgbeta = jnp.concatenate([dg_h[:, :, 0].T, dbeta_h[:, :, 0].T, jnp.zeros((T, W_AB - 2 * GDN_HEADS), f32)], axis=1)
    (dc_qkv, dab), (dal_p, ddt_p) = _pw_bwd(
        "gdn_prep_bwd", _gdn_prep_fn, [qkv_raw, ab], gd_consts,
        [(_from_heads(dq_h),), (_from_heads(dk_h),), (_from_heads(dv_h),), (dgbeta,)], 256, conv_w=W['gdn_conv_w'])
    G['gdn_a_log'], G['gdn_dt_bias'] = dal_p[0, :GDN_HEADS], ddt_p[0, :GDN_HEADS]
    dqkv_raw, G['gdn_conv_w'] = _conv_bwd("gdn_conv_bwd", dc_qkv, qkv_raw, W['gdn_conv_w'], 256)

    (dy, dr1, dk21, dv1, dg_), (G['rwkv_ln_w'], G['rwkv_ln_b'], G['rwkv_r_k']) = _pw_bwd(
        "rwkv_post_bwd", _rwkv_post_fn, [y, r, k2, v, g], post_consts, [(dya_in,)], 256)
    dr2, dlw, dk22, dv2, da_, db_ = [_from_heads(t) for t in _scan_bwd(
        "wkv_bwd", _wkv_chunk, wkv_in, _to_heads(dy, RWKV_HEADS), zs_wkv, WKV_CHUNK, 4)]
    (dps,), rw_grads = _pw_bwd(
        "rwkv_prep_bwd", _rwkv_prep_fn, [p_rwkv], rw_consts,
        [(dr1, dr2), (dlw,), (dk21, dk22), (dv1, dv2), (da_,), (db_,), (dg_,)], 256, conv_w=mixw)
    G['rwkv_w0'], dw2p, G['rwkv_a0'], da2p, G['rwkv_g2'], G['rwkv_k_k'], G['rwkv_k_a'] = rw_grads
    G['rwkv_w2'], G['rwkv_a2'] = dw2p[:64], da2p[64:]
    dp_rwkv, dmixw = _conv_bwd("shift_bwd", dps, p_rwkv, mixw, 256)
    G['rwkv_mu'] = dmixw[0] - dmixw[1]

    dp = jnp.concatenate([dp_rwkv, dqkv_raw, dz, dgates, dab], axis=1)
    du = _mm(dp, wp, 'nt', "d_u")
    G['w_in_pad'] = _mm(u, dp, 'tn', "g_w_in")
    (dx,), (G['norm1_g'],) = _pw_bwd("norm1_bwd", _rms_fn, [x], [g1], [(du,)], 256, add_to_first=dx1)
    return loss, dx, G


def _pad_w_in(w):
    return jnp.concatenate([w[:, :OFF_GATES], w[:, OFF_GATES + 8:], w[:, OFF_GATES:OFF_GATES + 8],
                            jnp.zeros((w.shape[0], W_AB - 8), w.dtype)], axis=1)


def _unpad_w_in(wp):
    return jnp.concatenate([wp[:, :OFF_GATES], wp[:, OFF_AB:OFF_AB + 8], wp[:, OFF_GATES:OFF_AB]], axis=1)


BIG = ('w_in', 'rwkv_proj', 'gdn_proj', 'w_out', 'ffn_up', 'ffn_down')
SMALL_SHARDED = ('rwkv_w2', 'rwkv_a2', 'rwkv_g2', 'gdn_conv_w', 'ffn_conv_w')
PACK_ORDER = SHARDED + SMALL


def _rows_for(n_elems):
    per = LANES * PACK_ROW_TILE
    return -(-n_elems // per) * PACK_ROW_TILE


def _pack(arrays, dtype):
    flat = jnp.concatenate([a.reshape(-1).astype(dtype) for a in arrays])
    rows = _rows_for(flat.shape[0])
    return jnp.pad(flat, (0, rows * LANES - flat.shape[0])).reshape(rows, LANES)


def _unpack(buf, shapes):
    flat = buf.reshape(-1)
    out, off = [], 0
    for s in shapes:
        n = 1
        for d in s:
            n *= d
        out.append(flat[off:off + n].reshape(s))
        off += n
    return out


def _xy_exchange(name, buf, scatter):
    R, L = buf.shape[-2:]

    def body(in_ref, out_ref, send_sems, recv_sems, local_sem):
        x, y, c = lax.axis_index("x"), lax.axis_index("y"), lax.axis_index("c")
        me = 2 * x + y
        peers = [(1 - x, y), (x, 1 - y), (1 - x, 1 - y)]

        def copy(k, src, slot, peer):
            return pltpu.make_async_remote_copy(
                src_ref=src, dst_ref=out_ref.at[slot], send_sem=send_sems.at[k], recv_sem=recv_sems.at[k],
                device_id=(peer[0], peer[1], c), device_id_type=pl.DeviceIdType.MESH)

        own = pltpu.make_async_copy(in_ref.at[me] if scatter else in_ref, out_ref.at[me], local_sem)
        own.start()
        sends = []
        for k, peer in enumerate(peers):
            src = in_ref.at[2 * peer[0] + peer[1]] if scatter else in_ref
            sends.append(copy(k, src, me, peer))
            sends[-1].start()
        for k, peer in enumerate(peers):
            src = in_ref.at[me] if scatter else in_ref
            copy(k, src, 2 * peer[0] + peer[1], peer).wait_recv()
        for cp in sends:
            cp.wait_send()
        own.wait()

    return pl.pallas_call(
        body, in_specs=[pl.BlockSpec(memory_space=pl.ANY)], out_specs=pl.BlockSpec(memory_space=pl.ANY),
        out_shape=jax.ShapeDtypeStruct((4, R, L), buf.dtype),
        scratch_shapes=[pltpu.SemaphoreType.DMA((3,)), pltpu.SemaphoreType.DMA((3,)), pltpu.SemaphoreType.DMA(())],
        name=name)(buf)


def _sibling_exchange(name, buf):
    def body(in_ref, out_ref, send_sem, recv_sem):
        x, y, c = lax.axis_index("x"), lax.axis_index("y"), lax.axis_index("c")
        cp = pltpu.make_async_remote_copy(
            src_ref=in_ref, dst_ref=out_ref, send_sem=send_sem, recv_sem=recv_sem,
            device_id=(x, y, 1 - c), device_id_type=pl.DeviceIdType.MESH)
        cp.start()
        cp.wait()

    return pl.pallas_call(
        body, in_specs=[pl.BlockSpec(memory_space=pl.ANY)], out_specs=pl.BlockSpec(memory_space=pl.ANY),
        out_shape=jax.ShapeDtypeStruct(buf.shape, buf.dtype),
        scratch_shapes=[pltpu.SemaphoreType.DMA(()), pltpu.SemaphoreType.DMA(())], name=name)(buf)


def _sum_slots(buf):
    _, R, L = buf.shape

    def body(b_ref, o_ref):
        o_ref[...] = ((b_ref[0] + b_ref[1]) + b_ref[2]) + b_ref[3]

    return pl.pallas_call(
        body, grid=(R // PACK_ROW_TILE,),
        in_specs=[pl.BlockSpec((4, PACK_ROW_TILE, L), lambda i: (0, i, 0))],
        out_specs=pl.BlockSpec((PACK_ROW_TILE, L), lambda i: (i, 0)),
        out_shape=jax.ShapeDtypeStruct((R, L), f32), name="sum_slots",
        compiler_params=_params("parallel"))(buf)


def _adamw(w, ga, gb, m, v):
    R, L = w.shape
    c1 = 1.0 / (1.0 - ADAM_B1 ** ADAM_STEP)
    c2 = 1.0 / (1.0 - ADAM_B2 ** ADAM_STEP)

    def body(w_ref, ga_ref, gb_ref, m_ref, v_ref, g_out, d_out, m_out, v_out):
        g = ga_ref[...] + gb_ref[...]
        m_new = ADAM_B1 * m_ref[...] + (1.0 - ADAM_B1) * g
        v_new = ADAM_B2 * v_ref[...] + (1.0 - ADAM_B2) * (g * g)
        g_out[...] = g
        m_out[...] = m_new
        v_out[...] = v_new
        d_out[...] = -ADAM_LR * ((m_new * c1) / (jnp.sqrt(v_new * c2) + ADAM_EPS) + ADAM_WD * w_ref[...])

    spec = pl.BlockSpec((PACK_ROW_TILE, L), lambda i: (i, 0))
    return pl.pallas_call(
        body, grid=(R // PACK_ROW_TILE,), in_specs=[spec] * 5, out_specs=[spec] * 4,
        out_shape=[jax.ShapeDtypeStruct((R, L), f32)] * 4, name="adamw",
        compiler_params=_params("parallel"))(w, ga, gb, m, v)


def _step(x, loss_target, P, M, V):
    shapes = {n: tuple(P[n].shape) for n in WEIGHTS}
    n_pos = 4

    gather_in = _pack([P[n].astype(bf16) for n in BIG]
                      + [lax.bitcast_convert_type(P[n], bf16) for n in SMALL_SHARDED], bf16)
    gathered = _xy_exchange("gather_weights", gather_in, scatter=False)
    gshapes = [shapes[n] for n in BIG] + [shapes[n] + (2,) for n in SMALL_SHARDED]
    per_pos = [_unpack(gathered[j], gshapes) for j in range(n_pos)]
    W = {n: P[n] for n in SMALL}
    for q, n in enumerate(BIG + SMALL_SHARDED):
        parts = [per_pos[j][q] for j in range(n_pos)]
        if n in SMALL_SHARDED:
            parts = [lax.bitcast_convert_type(t, f32) for t in parts]
        W[n] = jnp.concatenate(parts, axis=0 if n in ROW_SHARDED else 1)
    W['w_in_pad'] = _pad_w_in(W.pop('w_in'))

    loss_rows, dx, G = _local_step(x, loss_target, W)
    G['w_in'] = _unpad_w_in(G.pop('w_in_pad'))

    slabs = []
    for j in range(n_pos):
        parts = []
        for n in SHARDED:
            axis = 0 if n in ROW_SHARDED else 1
            width = shapes[n][axis]
            parts.append(lax.slice_in_dim(G[n], j * width, (j + 1) * width, axis=axis))
        parts += [G[n] for n in SMALL]
        slabs.append(_pack(parts, f32))
    contributions = _xy_exchange("scatter_grads", jnp.stack(slabs), scatter=True)
    plane_sum = _sum_slots(contributions)
    sibling_sum = _sibling_exchange("sibling_grads", plane_sum)

    pack_shapes = [shapes[n] for n in PACK_ORDER]
    g, delta, m_new, v_new = _adamw(_pack([P[n] for n in PACK_ORDER], f32), plane_sum, sibling_sum,
                                    _pack([M[n] for n in PACK_ORDER], f32), _pack([V[n] for n in PACK_ORDER], f32))
    out = {}
    for tag, buf in (('grad', g), ('delta', delta), ('new_m', m_new), ('new_v', v_new)):
        for n, t in zip(PACK_ORDER, _unpack(buf, pack_shapes)):
            out[tag + '_' + n] = t
    loss = lax.psum(loss_rows[0, 0], ("x", "y", "c"))
    return loss, dx, out


def kernel(x, norm1_g, w_in, rwkv_mu, rwkv_w0, rwkv_w2, rwkv_a0, rwkv_a2, rwkv_g2, rwkv_k_k, rwkv_k_a, rwkv_r_k, rwkv_ln_w, rwkv_ln_b, rwkv_proj, gdn_conv_w, gdn_a_log, gdn_dt_bias, gdn_norm_w, gdn_proj, w_out, norm2_g, ffn_up, ffn_conv_w, ffn_down, final_g, loss_target, m_norm1_g, m_w_in, m_rwkv_mu, m_rwkv_w0, m_rwkv_w2, m_rwkv_a0, m_rwkv_a2, m_rwkv_g2, m_rwkv_k_k, m_rwkv_k_a, m_rwkv_r_k, m_rwkv_ln_w, m_rwkv_ln_b, m_rwkv_proj, m_gdn_conv_w, m_gdn_a_log, m_gdn_dt_bias, m_gdn_norm_w, m_gdn_proj, m_w_out, m_norm2_g, m_ffn_up, m_ffn_conv_w, m_ffn_down, m_final_g, v_norm1_g, v_w_in, v_rwkv_mu, v_rwkv_w0, v_rwkv_w2, v_rwkv_a0, v_rwkv_a2, v_rwkv_g2, v_rwkv_k_k, v_rwkv_k_a, v_rwkv_r_k, v_rwkv_ln_w, v_rwkv_ln_b, v_rwkv_proj, v_gdn_conv_w, v_gdn_a_log, v_gdn_dt_bias, v_gdn_norm_w, v_gdn_proj, v_w_out, v_norm2_g, v_ffn_up, v_ffn_conv_w, v_ffn_down, v_final_g):
    weights = (norm1_g, w_in, rwkv_mu, rwkv_w0, rwkv_w2, rwkv_a0, rwkv_a2, rwkv_g2, rwkv_k_k, rwkv_k_a, rwkv_r_k, rwkv_ln_w,
               rwkv_ln_b, rwkv_proj, gdn_conv_w, gdn_a_log, gdn_dt_bias, gdn_norm_w, gdn_proj, w_out, norm2_g, ffn_up,
               ffn_conv_w, ffn_down, final_g)
    m_in = (m_norm1_g, m_w_in, m_rwkv_mu, m_rwkv_w0, m_rwkv_w2, m_rwkv_a0, m_rwkv_a2, m_rwkv_g2, m_rwkv_k_k, m_rwkv_k_a,
            m_rwkv_r_k, m_rwkv_ln_w, m_rwkv_ln_b, m_rwkv_proj, m_gdn_conv_w, m_gdn_a_log, m_gdn_dt_bias, m_gdn_norm_w,
            m_gdn_proj, m_w_out, m_norm2_g, m_ffn_up, m_ffn_conv_w, m_ffn_down, m_final_g)
    v_in = (v_norm1_g, v_w_in, v_rwkv_mu, v_rwkv_w0, v_rwkv_w2, v_rwkv_a0, v_rwkv_a2, v_rwkv_g2, v_rwkv_k_k, v_rwkv_k_a,
            v_rwkv_r_k, v_rwkv_ln_w, v_rwkv_ln_b, v_rwkv_proj, v_gdn_conv_w, v_gdn_a_log, v_gdn_dt_bias, v_gdn_norm_w,
            v_gdn_proj, v_w_out, v_norm2_g, v_ffn_up, v_ffn_conv_w, v_ffn_down, v_final_g)
    drop = lambda n, a: a if n == 'final_g' else a[0]
    P = {n: drop(n, a) for n, a in zip(WEIGHTS, weights)}
    M = {n: drop(n, a) for n, a in zip(WEIGHTS, m_in)}
    V = {n: drop(n, a) for n, a in zip(WEIGHTS, v_in)}
    loss, dx, out = _step(x[0], loss_target[0], P, M, V)
    lift = lambda n, a: a if n == 'final_g' else a[None]
    res = [loss, dx[None]]
    for tag in ('grad', 'delta', 'new_m', 'new_v'):
        res += [lift(n, out[tag + '_' + n]) for n in WEIGHTS]
    return tuple(res)
```

```python
import functools

import jax
import jax.numpy as jnp
from jax import lax
from jax.experimental import pallas as pl
from jax.experimental.pallas import tpu as pltpu

f32 = jnp.float32
bf16 = jnp.bfloat16
HI = lax.Precision.HIGHEST

D_MODEL = 1024
RWKV_HEADS, RWKV_HD, RWKV_W = 8, 64, 512
GDN_HEADS, GDN_HD, GDN_W = 4, 128, 512
FFN_H = 2816
NORM_EPS, L2_EPS, GN_EPS = 1e-6, 1e-6, 64e-5
W_RWKV, W_QKV, W_Z, W_GATES, W_AB = 1792, 1536, 512, 2048, 256
OFF_QKV, OFF_Z, OFF_GATES, OFF_AB = 1792, 3328, 3840, 5888
W_IN_PAD = OFF_AB + W_AB
IN_WIDTH = 5896
WKV_CHUNK = 64
WKV_CB = 1
GDN_CHUNK = 64
HALO = 8
LANES = 1024
PACK_ROW_TILE = 256
VMEM_LIMIT = 56 * 1024 * 1024

ADAM_LR, ADAM_B1, ADAM_B2, ADAM_EPS, ADAM_WD, ADAM_STEP = 0.001, 0.9, 0.999, 1e-08, 0.01, 10

SHARDED = ('w_in', 'rwkv_w2', 'rwkv_a2', 'rwkv_g2', 'rwkv_proj', 'gdn_conv_w', 'gdn_proj', 'w_out', 'ffn_up',
           'ffn_conv_w', 'ffn_down')
ROW_SHARDED = ('w_out', 'ffn_down')
SMALL = ('norm1_g', 'rwkv_mu', 'rwkv_w0', 'rwkv_a0', 'rwkv_k_k', 'rwkv_k_a', 'rwkv_r_k', 'rwkv_ln_w', 'rwkv_ln_b',
         'gdn_a_log', 'gdn_dt_bias', 'gdn_norm_w', 'norm2_g', 'final_g')
WEIGHTS = ('norm1_g', 'w_in', 'rwkv_mu', 'rwkv_w0', 'rwkv_w2', 'rwkv_a0', 'rwkv_a2', 'rwkv_g2', 'rwkv_k_k', 'rwkv_k_a',
           'rwkv_r_k', 'rwkv_ln_w', 'rwkv_ln_b', 'rwkv_proj', 'gdn_conv_w', 'gdn_a_log', 'gdn_dt_bias', 'gdn_norm_w',
           'gdn_proj', 'w_out', 'norm2_g', 'ffn_up', 'ffn_conv_w', 'ffn_down', 'final_g')


def _params(*sem):
    return pltpu.CompilerParams(dimension_semantics=sem, vmem_limit_bytes=VMEM_LIMIT)


def _pick(n, cands):
    for c in cands:
        if n % c == 0:
            return c
    raise ValueError(f"no tile for {n}")


def _mm(a, b, mode, name, add=None):
    if mode == 'nn':
        (M, K), N = a.shape, b.shape[1]
    elif mode == 'nt':
        (M, K), N = a.shape, b.shape[0]
    else:
        (K, M), N = a.shape, b.shape[1]
    tm = _pick(M, (512, 256, 128))
    tn = _pick(N, (512, 256, 128))
    tk = _pick(K, (1024, 512, 256, 128))
    dn = {'nn': (((1,), (0,)), ((), ())), 'nt': (((1,), (1,)), ((), ())), 'tn': (((0,), (0,)), ((), ()))}[mode]

    def body(a_ref, b_ref, *rest):
        o_ref = rest[-1]
        k = pl.program_id(2)
        acc = lax.dot_general(a_ref[...].astype(bf16), b_ref[...].astype(bf16), dn, preferred_element_type=f32)

        @pl.when(k == 0)
        def _():
            o_ref[...] = acc + rest[0][...] if add is not None else acc

        @pl.when(k > 0)
        def _():
            o_ref[...] += acc

    a_spec = (pl.BlockSpec((tk, tm), lambda i, j, k: (k, i)) if mode == 'tn'
              else pl.BlockSpec((tm, tk), lambda i, j, k: (i, k)))
    b_spec = (pl.BlockSpec((tn, tk), lambda i, j, k: (j, k)) if mode == 'nt'
              else pl.BlockSpec((tk, tn), lambda i, j, k: (k, j)))
    o_spec = pl.BlockSpec((tm, tn), lambda i, j, k: (i, j))
    ins, specs = [a, b], [a_spec, b_spec]
    if add is not None:
        ins.append(add)
        specs.append(o_spec)
    return pl.pallas_call(
        body, grid=(M // tm, N // tn, K // tk), in_specs=specs, out_specs=o_spec,
        out_shape=jax.ShapeDtypeStruct((M, N), f32), name=name,
        compiler_params=_params("parallel", "parallel", "arbitrary"))(*ins)


def _shift_down(cur, prev, s):
    if s == 0:
        return cur
    ext = jnp.concatenate([prev, cur], axis=0)
    return pltpu.roll(ext, s, 0)[HALO:]


def _shift_up(cur, nxt, s):
    if s == 0:
        return cur
    ext = jnp.concatenate([cur, nxt], axis=0)
    return pltpu.roll(ext, ext.shape[0] - s, 0)[:cur.shape[0]]


def _conv_apply(cur, prev, w_ref):
    taps = w_ref.shape[0]
    out = None
    for i in range(taps):
        term = _shift_down(cur, prev, taps - 1 - i) * w_ref[pl.ds(i, 1), :]
        out = term if out is None else out + term
    return out


def _row_spec(tm, w):
    return pl.BlockSpec((tm, w), lambda i: (i, 0))


def _prev_spec(tm, w):
    return pl.BlockSpec((HALO, w), lambda i: (jnp.maximum(i * (tm // HALO) - 1, 0), 0))


def _next_spec(tm, w, T):
    return pl.BlockSpec((HALO, w), lambda i: (jnp.minimum((i + 1) * (tm // HALO), T // HALO - 1), 0))


def _full_spec(shape):
    return pl.BlockSpec(shape, lambda i: (0,) * len(shape))


def _pw_fwd(name, fn, rows, consts, out_widths, tm, conv_w=None):
    T = rows[0].shape[0]
    nr, nc = len(rows), len(consts)

    def body(*refs):
        i = pl.program_id(0)
        vals = [r[...] for r in refs[:nr]]
        p = nr
        if conv_w is not None:
            prev = jnp.where(i > 0, refs[p][...], 0.0)
            vals[0] = _conv_apply(vals[0], prev, refs[p + 1])
            p += 2
        cvals = [r[...] for r in refs[p:p + nc]]
        outs = fn(*vals, *cvals)
        for o_ref, o in zip(refs[p + nc:], outs):
            o_ref[...] = o

    ins = list(rows)
    specs = [_row_spec(tm, r.shape[1]) for r in rows]
    if conv_w is not None:
        ins += [rows[0], conv_w]
        specs += [_prev_spec(tm, rows[0].shape[1]), _full_spec(conv_w.shape)]
    ins += list(consts)
    specs += [_full_spec(c.shape) for c in consts]
    outs = pl.pallas_call(
        body, grid=(T // tm,), in_specs=specs,
        out_specs=[_row_spec(tm, w) for w in out_widths],
        out_shape=[jax.ShapeDtypeStruct((T, w), f32) for w in out_widths], name=name,
        compiler_params=_params("parallel"))(*ins)
    return outs


def _pw_bwd(name, fn, rows, consts, cots, tm, conv_w=None, add_to_first=None):
    T = rows[0].shape[0]
    nr, nc = len(rows), len(consts)
    flat_cots = [c for grp in cots for c in grp]
    nct = len(flat_cots)

    def body(*refs):
        i = pl.program_id(0)
        vals = [r[...] for r in refs[:nr]]
        p = nr
        if conv_w is not None:
            prev = jnp.where(i > 0, refs[p][...], 0.0)
            vals[0] = _conv_apply(vals[0], prev, refs[p + 1])
            p += 2
        cvals = [r[...] for r in refs[p:p + nc]]
        p += nc
        cot_vals = []
        for grp in cots:
            acc = refs[p][...]
            for q in range(1, len(grp)):
                acc = acc + refs[p + q][...]
            p += len(grp)
            cot_vals.append(acc)
        extra = None
        if add_to_first is not None:
            extra = refs[p][...]
            p += 1
        _, vjp = jax.vjp(fn, *vals, *cvals)
        grads = vjp(tuple(cot_vals))
        row_out = refs[p:p + nr]
        const_out = refs[p + nr:]
        for q in range(nr):
            g = grads[q]
            if q == 0 and extra is not None:
                g = g + extra
            row_out[q][...] = g

        @pl.when(i == 0)
        def _():
            for q in range(nc):
                const_out[q][...] = grads[nr + q]

        @pl.when(i > 0)
        def _():
            for q in range(nc):
                const_out[q][...] += grads[nr + q]

    ins = list(rows)
    specs = [_row_spec(tm, r.shape[1]) for r in rows]
    if conv_w is not None:
        ins += [rows[0], conv_w]
        specs += [_prev_spec(tm, rows[0].shape[1]), _full_spec(conv_w.shape)]
    ins += list(consts)
    specs += [_full_spec(c.shape) for c in consts]
    ins += flat_cots
    specs += [_row_spec(tm, c.shape[1]) for c in flat_cots]
    if add_to_first is not None:
        ins.append(add_to_first)
        specs.append(_row_spec(tm, add_to_first.shape[1]))
    out_shapes = [jax.ShapeDtypeStruct(r.shape, f32) for r in rows] + [jax.ShapeDtypeStruct(c.shape, f32) for c in consts]
    out_specs = [_row_spec(tm, r.shape[1]) for r in rows] + [_full_spec(c.shape) for c in consts]
    outs = pl.pallas_call(
        body, grid=(T // tm,), in_specs=specs, out_specs=out_specs, out_shape=out_shapes, name=name,
        compiler_params=_params("arbitrary"))(*ins)
    return list(outs[:nr]), list(outs[nr:])


def _conv_bwd(name, dc, x, w, tm):
    T, W = x.shape
    taps = w.shape[0]
    nblk = T // tm

    def body(dc_ref, dcn_ref, x_ref, xp_ref, w_ref, dx_ref, dw_ref):
        i = pl.program_id(0)
        dcv, xv = dc_ref[...], x_ref[...]
        nxt = jnp.where(i < nblk - 1, dcn_ref[...], 0.0)
        prev = jnp.where(i > 0, xp_ref[...], 0.0)

        @pl.when(i == 0)
        def _():
            dw_ref[...] = jnp.zeros_like(dw_ref)

        dx = None
        for k in range(taps):
            s = taps - 1 - k
            term = _shift_up(dcv, nxt, s) * w_ref[pl.ds(k, 1), :]
            dx = term if dx is None else dx + term
            dw_ref[pl.ds(k, 1), :] += jnp.sum(dcv * _shift_down(xv, prev, s), axis=0, keepdims=True)
        dx_ref[...] = dx

    return pl.pallas_call(
        body, grid=(nblk,),
        in_specs=[_row_spec(tm, W), _next_spec(tm, W, T), _row_spec(tm, W), _prev_spec(tm, W), _full_spec(w.shape)],
        out_specs=[_row_spec(tm, W), _full_spec(w.shape)],
        out_shape=[jax.ShapeDtypeStruct((T, W), f32), jax.ShapeDtypeStruct(w.shape, f32)], name=name,
        compiler_params=_params("arbitrary"))(dc, dc, x, x, w)


def _sigmoid(x):
    return 1.0 / (1.0 + jnp.exp(-x))


def _softplus(x):
    return jnp.maximum(x, 0.0) + jnp.log(1.0 + jnp.exp(jnp.minimum(x, -x)))


def _seg_sum(x, seg):
    w = x.shape[-1]
    r = lax.broadcasted_iota(jnp.int32, (w, w), 0) // seg
    c = lax.broadcasted_iota(jnp.int32, (w, w), 1) // seg
    return jnp.dot(x, (r == c).astype(f32), precision=HI, preferred_element_type=f32)


def _rms(x, g):
    return x * lax.rsqrt(jnp.mean(x * x, axis=-1, keepdims=True) + NORM_EPS) * g


def _rms_fn(x, g):
    return (_rms(x, g),)


def _loss_rows(x2, tgt, g):
    e = _rms(x2, g) - tgt
    return 0.5 * jnp.sum(e * e, axis=-1, keepdims=True) * (1.0 / D_MODEL)


def _rwkv_prep_fn(ps, w0, w2p, a0, a2p, g2, k_k, k_a):
    r, k, v = ps[:, 0:512], ps[:, 512:1024], ps[:, 1024:1536]
    wa, gl = ps[:, 1536:1664], ps[:, 1664:1792]
    z = w0 + jnp.dot(jnp.tanh(wa), w2p, precision=HI, preferred_element_type=f32)
    w_log = -_softplus(-z) - 0.5
    lw = -jnp.exp(w_log)
    a = _sigmoid(a0 + jnp.dot(wa, a2p, precision=HI, preferred_element_type=f32))
    g = jnp.dot(_sigmoid(gl), g2, precision=HI, preferred_element_type=f32)
    kx = k * k_k
    kk = kx * lax.rsqrt(_seg_sum(kx * kx, RWKV_HD) + L2_EPS)
    k2 = k * (1.0 + (a - 1.0) * k_a)
    return r, lw, k2, v, -kk, kk * a, g


def _rwkv_post_fn(y, r, k2, v, g, ln_w, ln_b, rk):
    mean = _seg_sum(y, RWKV_HD) * (1.0 / RWKV_HD)
    yc = y - mean
    var = _seg_sum(yc * yc, RWKV_HD) * (1.0 / RWKV_HD)
    yn = yc * lax.rsqrt(var + GN_EPS) * ln_w + ln_b
    bonus = _seg_sum(r * k2 * rk, RWKV_HD) * v
    return ((yn + bonus) * g,)


def _gdn_prep_fn(c, ab, al_p, dt_p):
    s = c * _sigmoid(c)
    q, k, v = s[:, 0:512], s[:, 512:1024], s[:, 1024:1536]
    q = q * lax.rsqrt(_seg_sum(q * q, GDN_HD) + L2_EPS) * (GDN_HD ** -0.5)
    k = k * lax.rsqrt(_seg_sum(k * k, GDN_HD) + L2_EPS)
    lane = lax.broadcasted_iota(jnp.int32, ab.shape, 1)
    gpart = -jnp.exp(al_p) * _softplus(ab + dt_p)
    gbeta = jnp.where(lane < GDN_HEADS, gpart, jnp.where(lane < 2 * GDN_HEADS, _sigmoid(ab), 0.0))
    return q, k, v, gbeta


def _gdn_post_fn(o, z, nw):
    ms = _seg_sum(o * o, GDN_HD) * (1.0 / GDN_HD)
    return (o * lax.rsqrt(ms + NORM_EPS) * nw * (z * _sigmoid(z)),)


def _mix_fn(gates, ya, yb):
    return (_sigmoid(gates[:, :D_MODEL]) * ya + _sigmoid(gates[:, D_MODEL:]) * yb,)


def _ffn_fn(c):
    hg, hu = c[:, :FFN_H], c[:, FFN_H:]
    return (hg * _sigmoid(hg) * hu,)


SCAN_PREC = lax.Precision.HIGH


def _bmm(a, b):
    return jnp.einsum('hcs,hsd->hcd', a, b, precision=SCAN_PREC, preferred_element_type=f32)


def _bmm_nt(a, b):
    return jnp.einsum('hcd,hsd->hcs', a, b, precision=SCAN_PREC, preferred_element_type=f32)


def _bmm_tn(a, b):
    return jnp.einsum('hcd,hce->hde', a, b, precision=SCAN_PREC, preferred_element_type=f32)


def _masks(H, C):
    row = lax.broadcasted_iota(jnp.int32, (H, C, C), 1)
    col = lax.broadcasted_iota(jnp.int32, (H, C, C), 2)
    return row, col


def _tri_inv_impl(L):
    H, C, _ = L.shape
    row, col = _masks(H, C)
    eye = (row == col).astype(f32)
    base = 16
    same = (row // base) == (col // base)
    Ld = jnp.where(same, L, 0.0)
    X = -Ld
    inv = eye + X
    for _ in range(3):
        X = _bmm(X, X)
        inv = _bmm(inv, eye + X)
    if C == base:
        return inv
    N = _bmm(inv, L - Ld)
    out = eye - N
    levels = C // base
    P = N
    span = 2
    while span < levels:
        P = _bmm(P, P)
        out = _bmm(out, eye + P)
        span *= 2
    return _bmm(out, inv)


@jax.custom_vjp
def _tri_inv(L):
    return _tri_inv_impl(L)


def _tri_inv_fwd(L):
    T = _tri_inv_impl(L)
    return T, T


def _tri_inv_bwd(T, dT):
    return (-_bmm_nt(_bmm_tn(T, dT), T),)


_tri_inv.defvjp(_tri_inv_fwd, _tri_inv_bwd)


def _cumsum_impl(x, reverse):
    C = x.shape[1]
    row = lax.broadcasted_iota(jnp.int32, x.shape, 1)
    s = 1
    while s < C:
        if reverse:
            x = x + jnp.where(row < C - s, pltpu.roll(x, C - s, 1), 0.0)
        else:
            x = x + jnp.where(row >= s, pltpu.roll(x, s, 1), 0.0)
        s *= 2
    return x


@jax.custom_vjp
def _cumsum(x):
    return _cumsum_impl(x, False)


_cumsum.defvjp(lambda x: (_cumsum_impl(x, False), None), lambda _, g: (_cumsum_impl(g, True),))


def _wkv_chunk(Z, r, lw, k, v, a, b):
    H, C, D = r.shape
    row, col = _masks(H, C)
    incl, strict = row >= col, row > col
    cw = _cumsum(lw)
    cwp = cw - lw
    cwl = jnp.sum(lw, axis=1, keepdims=True)
    en = jnp.exp(-cw)
    at, rt, bt, kt = a * jnp.exp(cwp), r * jnp.exp(cw), b * en, k * en
    Lab = jnp.where(strict, _bmm_nt(at, bt), 0.0)
    Lak = jnp.where(strict, _bmm_nt(at, kt), 0.0)
    Tm = _tri_inv(-Lab)
    U = _bmm(Tm, _bmm(at, Z) + _bmm(Lak, v))
    Rb = jnp.where(incl, _bmm_nt(rt, bt), 0.0)
    Rk = jnp.where(incl, _bmm_nt(rt, kt), 0.0)
    y = _bmm(rt, Z) + _bmm(Rb, U) + _bmm(Rk, v)
    ed = jnp.exp(cwl - cw)
    zdec = jnp.swapaxes(jnp.broadcast_to(jnp.exp(cwl), (H, Z.shape[2], D)), 1, 2)
    Z1 = Z * zdec + _bmm_tn(b * ed, U) + _bmm_tn(k * ed, v)
    return y, Z1


def _gdn_chunk(S, q, k, v, g, beta):
    H, C, D = q.shape
    row, col = _masks(H, C)
    incl, strict = row >= col, row > col
    gcD = _cumsum(jnp.broadcast_to(g, (H, C, D)))
    gcC = _cumsum(jnp.broadcast_to(g, (H, C, C)))
    diff = gcC - jnp.swapaxes(gcC, 1, 2)
    decay = jnp.where(incl, jnp.exp(jnp.where(incl, diff, 0.0)), 0.0)
    gl = jnp.sum(g, axis=1, keepdims=True)
    bD = jnp.broadcast_to(beta, (H, C, D))
    kb, vb = k * bD, v * bD
    L = jnp.where(strict, _bmm_nt(kb, k) * decay, 0.0)
    Tm = _tri_inv(L)
    egc = jnp.exp(gcD)
    u = _bmm(Tm, vb)
    wk = _bmm(Tm, kb * egc)
    attn = jnp.where(incl, _bmm_nt(q, k) * decay, 0.0)
    v_new = u - _bmm(wk, S)
    o = _bmm(q * egc, S) + _bmm(attn, v_new)
    S1 = S * jnp.exp(gl) + _bmm_tn(k * jnp.exp(gl - gcD), v_new)
    return o, S1


def _scan_fwd(name, chunk_fn, ins, C, CB, dv):
    H, T, Dk = ins[0].shape
    n_in = len(ins)
    blk = C * CB

    def body(*refs):
        in_refs, y_ref, zs_ref, z_scr = refs[:n_in], refs[n_in], refs[n_in + 1], refs[n_in + 2]

        @pl.when(pl.program_id(0) == 0)
        def _():
            z_scr[...] = jnp.zeros_like(z_scr)

        Z = z_scr[...]
        for j in range(CB):
            zs_ref[j] = Z
            y, Z = chunk_fn(Z, *[r[:, j * C:(j + 1) * C, :] for r in in_refs])
            y_ref[:, j * C:(j + 1) * C, :] = y
        z_scr[...] = Z

    return pl.pallas_call(
        body, grid=(T // blk,),
        in_specs=[pl.BlockSpec((H, blk, a.shape[2]), lambda i: (0, i, 0)) for a in ins],
        out_specs=[pl.BlockSpec((H, blk, dv), lambda i: (0, i, 0)),
                   pl.BlockSpec((CB, H, Dk, dv), lambda i: (i, 0, 0, 0))],
        out_shape=[jax.ShapeDtypeStruct((H, T, dv), f32), jax.ShapeDtypeStruct((T // C, H, Dk, dv), f32)],
        scratch_shapes=[pltpu.VMEM((H, Dk, dv), f32)], name=name,
        compiler_params=_params("arbitrary"))(*ins)


def _scan_bwd(name, chunk_fn, ins, dy, zs, C, CB):
    H, T, Dk = ins[0].shape
    dv = dy.shape[2]
    n_in = len(ins)
    blk = C * CB
    nblk = T // blk

    def body(*refs):
        in_refs, dy_ref, zs_ref = refs[:n_in], refs[n_in], refs[n_in + 1]
        out_refs, dz_scr = refs[n_in + 2:2 * n_in + 2], refs[2 * n_in + 2]

        @pl.when(pl.program_id(0) == 0)
        def _():
            dz_scr[...] = jnp.zeros_like(dz_scr)

        dZ = dz_scr[...]
        for j in reversed(range(CB)):
            sl = slice(j * C, (j + 1) * C)
            _, vjp = jax.vjp(chunk_fn, zs_ref[j], *[r[:, sl, :] for r in in_refs])
            grads = vjp((dy_ref[:, sl, :], dZ))
            dZ = grads[0]
            for o_ref, gval in zip(out_refs, grads[1:]):
                o_ref[:, sl, :] = gval
        dz_scr[...] = dZ

    rev = lambda i: (0, nblk - 1 - i, 0)
    return pl.pallas_call(
        body, grid=(nblk,),
        in_specs=[pl.BlockSpec((H, blk, a.shape[2]), rev) for a in ins]
        + [pl.BlockSpec((H, blk, dv), rev), pl.BlockSpec((CB, H, Dk, dv), lambda i: (nblk - 1 - i, 0, 0, 0))],
        out_specs=[pl.BlockSpec((H, blk, a.shape[2]), rev) for a in ins],
        out_shape=[jax.ShapeDtypeStruct(a.shape, f32) for a in ins],
        scratch_shapes=[pltpu.VMEM((H, Dk, dv), f32)], name=name,
        compiler_params=_params("arbitrary"))(*ins, dy, zs)


def _to_heads(x, H):
    T, W = x.shape
    return x.reshape(T, H, W // H).transpose(1, 0, 2)


def _from_heads(x):
    H, T, d = x.shape
    return x.transpose(1, 0, 2).reshape(T, H * d)


def _loss_call(x2, tgt, g, tm):
    T, W = x2.shape

    def body(x_ref, t_ref, g_ref, dx_ref, dg_ref, l_ref):
        i = pl.program_id(0)
        tv = t_ref[...]
        l, vjp = jax.vjp(lambda xv, gv: _loss_rows(xv, tv, gv), x_ref[...], g_ref[...])
        dx, dg = vjp(jnp.ones_like(l))
        dx_ref[...] = dx
        tot = jnp.zeros((1, 128), f32) + jnp.sum(l)

        @pl.when(i == 0)
        def _():
            dg_ref[...] = dg
            l_ref[...] = tot

        @pl.when(i > 0)
        def _():
            dg_ref[...] += dg
            l_ref[...] += tot

    return pl.pallas_call(
        body, grid=(T // tm,),
        in_specs=[_row_spec(tm, W), _row_spec(tm, W), _full_spec(g.shape)],
        out_specs=[_row_spec(tm, W), _full_spec(g.shape), _full_spec((1, 128))],
        out_shape=[jax.ShapeDtypeStruct((T, W), f32), jax.ShapeDtypeStruct(g.shape, f32),
                   jax.ShapeDtypeStruct((1, 128), f32)], name="loss_head",
        compiler_params=_params("arbitrary"))(x2, tgt, g)


def _local_step(x, tgt, W):
    T = x.shape[0]
    row = lambda a: a.reshape(1, -1)
    wp = W['w_in_pad']
    w_rwkv, w_qkv, w_z = wp[:, :OFF_QKV], wp[:, OFF_QKV:OFF_Z], wp[:, OFF_Z:OFF_GATES]
    w_gates, w_ab = wp[:, OFF_GATES:OFF_AB], wp[:, OFF_AB:]
    mu = row(W['rwkv_mu'])
    mixw = jnp.concatenate([mu, 1.0 - mu], axis=0)
    zpad = jnp.zeros((64, RWKV_W), f32)
    w2p = jnp.concatenate([W['rwkv_w2'], zpad], axis=0)
    a2p = jnp.concatenate([zpad, W['rwkv_a2']], axis=0)
    rw_consts = [row(W['rwkv_w0']), w2p, row(W['rwkv_a0']), a2p, W['rwkv_g2'], row(W['rwkv_k_k']), row(W['rwkv_k_a'])]
    post_consts = [row(W['rwkv_ln_w']), row(W['rwkv_ln_b']), row(W['rwkv_r_k'])]
    pad4 = lambda a: jnp.pad(row(a), ((0, 0), (0, W_AB - GDN_HEADS)))
    gd_consts = [pad4(W['gdn_a_log']), pad4(W['gdn_dt_bias'])]
    nw_t = jnp.tile(row(W['gdn_norm_w']), (1, GDN_HEADS))
    g1, g2n, gf = row(W['norm1_g']), row(W['norm2_g']), row(W['final_g'])

    (u,) = _pw_fwd("norm1", _rms_fn, [x], [g1], [D_MODEL], 256)
    p_rwkv = _mm(u, w_rwkv, 'nn', "in_rwkv")
    qkv_raw = _mm(u, w_qkv, 'nn', "in_qkv")
    z = _mm(u, w_z, 'nn', "in_z")
    gates = _mm(u, w_gates, 'nn', "in_gates")
    ab = _mm(u, w_ab, 'nn', "in_ab")

    r, lw, k2, v, a_, b_, g = _pw_fwd("rwkv_prep", _rwkv_prep_fn, [p_rwkv], rw_consts, [RWKV_W] * 7, 256, conv_w=mixw)
    wkv_in = [_to_heads(t, RWKV_HEADS) for t in (r, lw, k2, v, a_, b_)]
    y_h, zs_wkv = _scan_fwd("wkv_fwd", _wkv_chunk, wkv_in, WKV_CHUNK, WKV_CB, RWKV_HD)
    y = _from_heads(y_h)
    (ya_in,) = _pw_fwd("rwkv_post", _rwkv_post_fn, [y, r, k2, v, g], post_consts, [RWKV_W], 256)
    ya = _mm(ya_in, W['rwkv_proj'], 'nn', "rwkv_proj")

    gq, gk, gv, gbeta = _pw_fwd("gdn_prep", _gdn_prep_fn, [qkv_raw, ab], gd_consts, [GDN_W] * 3 + [W_AB], 256,
                                conv_w=W['gdn_conv_w'])
    col = lambda t, s: t[:, s:s + GDN_HEADS].T[:, :, None]
    gdn_in = [_to_heads(t, GDN_HEADS) for t in (gq, gk, gv)] + [col(gbeta, 0), col(gbeta, GDN_HEADS)]
    o_h, zs_gdn = _scan_fwd("gdn_fwd", _gdn_chunk, gdn_in, GDN_CHUNK, 1, GDN_HD)
    o = _from_heads(o_h)
    (yb_in,) = _pw_fwd("gdn_post", _gdn_post_fn, [o, z], [nw_t], [GDN_W], 256)
    yb = _mm(yb_in, W['gdn_proj'], 'nn', "gdn_proj")

    (mixed,) = _pw_fwd("mix", _mix_fn, [gates, ya, yb], [], [D_MODEL], 256)
    x1 = _mm(mixed, W['w_out'], 'nn', "w_out", add=x)
    (u2,) = _pw_fwd("norm2", _rms_fn, [x1], [g2n], [D_MODEL], 256)
    h = _mm(u2, W['ffn_up'], 'nn', "ffn_up")
    (act,) = _pw_fwd("ffn_act", _ffn_fn, [h], [], [FFN_H], 128, conv_w=W['ffn_conv_w'])
    x2 = _mm(act, W['ffn_down'], 'nn', "ffn_down", add=x1)

    G = {}
    dx2, dgf, loss = _loss_call(x2, tgt, gf, 256)
    G['final_g'] = dgf
    dact = _mm(dx2, W['ffn_down'], 'nt', "d_act")
    G['ffn_down'] = _mm(act, dx2, 'tn', "g_ffn_down")
    (dc_ffn,), _ = _pw_bwd("ffn_act_bwd", _ffn_fn, [h], [], [(dact,)], 128, conv_w=W['ffn_conv_w'])
    dh, G['ffn_conv_w'] = _conv_bwd("ffn_conv_bwd", dc_ffn, h, W['ffn_conv_w'], 128)
    du2 = _mm(dh, W['ffn_up'], 'nt', "d_u2")
    G['ffn_up'] = _mm(u2, dh, 'tn', "g_ffn_up")
    (dx1,), (G['norm2_g'],) = _pw_bwd("norm2_bwd", _rms_fn, [x1], [g2n], [(du2,)], 256, add_to_first=dx2)
    dmixed = _mm(dx1, W['w_out'], 'nt', "d_mixed")
    G['w_out'] = _mm(mixed, dx1, 'tn', "g_w_out")
    (dgates, dya, dyb), _ = _pw_bwd("mix_bwd", _mix_fn, [gates, ya, yb], [], [(dmixed,)], 256)
    dya_in = _mm(dya, W['rwkv_proj'], 'nt', "d_ya_in")
    G['rwkv_proj'] = _mm(ya_in, dya, 'tn', "g_rwkv_proj")
    dyb_in = _mm(dyb, W['gdn_proj'], 'nt', "d_yb_in")
    G['gdn_proj'] = _mm(yb_in, dyb, 'tn', "g_gdn_proj")

    (do, dz), (dnw_t,) = _pw_bwd("gdn_post_bwd", _gdn_post_fn, [o, z], [nw_t], [(dyb_in,)], 256)
    G['gdn_norm_w'] = dnw_t.reshape(GDN_HEADS, GDN_HD).sum(axis=0)
    dq_h, dk_h, dv_h, dg_h, dbeta_h = _scan_bwd("gdn_bwd", _gdn_chunk, gdn_in, _to_heads(do, GDN_HEADS), zs_gdn,
                                                GDN_CHUNK, 1)
    dgbeta = jnp.concatenate([dg_h[:, :, 0].T, dbeta_h[:, :, 0].T, jnp.zeros((T, W_AB - 2 * GDN_HEADS), f32)], axis=1)
    (dc_qkv, dab), (dal_p, ddt_p) = _pw_bwd(
        "gdn_prep_bwd", _gdn_prep_fn, [qkv_raw, ab], gd_consts,
        [(_from_heads(dq_h),), (_from_heads(dk_h),), (_from_heads(dv_h),), (dgbeta,)], 256, conv_w=W['gdn_conv_w'])
    G['gdn_a_log'], G['gdn_dt_bias'] = dal_p[0, :GDN_HEADS], ddt_p[0, :GDN_HEADS]
    dqkv_raw, G['gdn_conv_w'] = _conv_bwd("gdn_conv_bwd", dc_qkv, qkv_raw, W['gdn_conv_w'], 256)

    (dy, dr1, dk21, dv1, dg_), (G['rwkv_ln_w'], G['rwkv_ln_b'], G['rwkv_r_k']) = _pw_bwd(
        "rwkv_post_bwd", _rwkv_post_fn, [y, r, k2, v, g], post_consts, [(dya_in,)], 256)
    dr2, dlw, dk22, dv2, da_, db_ = [_from_heads(t) for t in _scan_bwd(
        "wkv_bwd", _wkv_chunk, wkv_in, _to_heads(dy, RWKV_HEADS), zs_wkv, WKV_CHUNK, WKV_CB)]
    (dps,), rw_grads = _pw_bwd(
        "rwkv_prep_bwd", _rwkv_prep_fn, [p_rwkv], rw_consts,
        [(dr1, dr2), (dlw,), (dk21, dk22), (dv1, dv2), (da_,), (db_,), (dg_,)], 256, conv_w=mixw)
    G['rwkv_w0'], dw2p, G['rwkv_a0'], da2p, G['rwkv_g2'], G['rwkv_k_k'], G['rwkv_k_a'] = rw_grads
    G['rwkv_w2'], G['rwkv_a2'] = dw2p[:64], da2p[64:]
    dp_rwkv, dmixw = _conv_bwd("shift_bwd", dps, p_rwkv, mixw, 256)
    G['rwkv_mu'] = dmixw[0] - dmixw[1]

    dp = jnp.concatenate([dp_rwkv, dqkv_raw, dz, dgates, dab], axis=1)
    du = _mm(dp, wp, 'nt', "d_u")
    G['w_in_pad'] = _mm(u, dp, 'tn', "g_w_in")
    (dx,), (G['norm1_g'],) = _pw_bwd("norm1_bwd", _rms_fn, [x], [g1], [(du,)], 256, add_to_first=dx1)
    return loss, dx, G


def _pad_w_in(w):
    return jnp.concatenate([w[:, :OFF_GATES], w[:, OFF_GATES + 8:], w[:, OFF_GATES:OFF_GATES + 8],
                            jnp.zeros((w.shape[0], W_AB - 8), w.dtype)], axis=1)


def _unpad_w_in(wp):
    return jnp.concatenate([wp[:, :OFF_GATES], wp[:, OFF_AB:OFF_AB + 8], wp[:, OFF_GATES:OFF_AB]], axis=1)


BIG = ('w_in', 'rwkv_proj', 'gdn_proj', 'w_out', 'ffn_up', 'ffn_down')
SMALL_SHARDED = ('rwkv_w2', 'rwkv_a2', 'rwkv_g2', 'gdn_conv_w', 'ffn_conv_w')
PACK_ORDER = SHARDED + SMALL


def _rows_for(n_elems):
    per = LANES * PACK_ROW_TILE
    return -(-n_elems // per) * PACK_ROW_TILE


def _pack(arrays, dtype):
    flat = jnp.concatenate([a.reshape(-1).astype(dtype) for a in arrays])
    rows = _rows_for(flat.shape[0])
    return jnp.pad(flat, (0, rows * LANES - flat.shape[0])).reshape(rows, LANES)


def _unpack(buf, shapes):
    flat = buf.reshape(-1)
    out, off = [], 0
    for s in shapes:
        n = 1
        for d in s:
            n *= d
        out.append(flat[off:off + n].reshape(s))
        off += n
    return out


def _xy_exchange(name, buf, scatter):
    R, L = buf.shape[-2:]

    def body(in_ref, out_ref, send_sems, recv_sems, local_sem):
        x, y, c = lax.axis_index("x"), lax.axis_index("y"), lax.axis_index("c")
        me = 2 * x + y
        peers = [(1 - x, y), (x, 1 - y), (1 - x, 1 - y)]

        def copy(k, src, slot, peer):
            return pltpu.make_async_remote_copy(
                src_ref=src, dst_ref=out_ref.at[slot], send_sem=send_sems.at[k], recv_sem=recv_sems.at[k],
                device_id=(peer[0], peer[1], c), device_id_type=pl.DeviceIdType.MESH)

        own = pltpu.make_async_copy(in_ref.at[me] if scatter else in_ref, out_ref.at[me], local_sem)
        own.start()
        sends = []
        for k, peer in enumerate(peers):
            src = in_ref.at[2 * peer[0] + peer[1]] if scatter else in_ref
            sends.append(copy(k, src, me, peer))
            sends[-1].start()
        for k, peer in enumerate(peers):
            src = in_ref.at[me] if scatter else in_ref
            copy(k, src, 2 * peer[0] + peer[1], peer).wait_recv()
        for cp in sends:
            cp.wait_send()
        own.wait()

    return pl.pallas_call(
        body, in_specs=[pl.BlockSpec(memory_space=pl.ANY)], out_specs=pl.BlockSpec(memory_space=pl.ANY),
        out_shape=jax.ShapeDtypeStruct((4, R, L), buf.dtype),
        scratch_shapes=[pltpu.SemaphoreType.DMA((3,)), pltpu.SemaphoreType.DMA((3,)), pltpu.SemaphoreType.DMA(())],
        name=name)(buf)


def _sibling_exchange(name, buf):
    def body(in_ref, out_ref, send_sem, recv_sem):
        x, y, c = lax.axis_index("x"), lax.axis_index("y"), lax.axis_index("c")
        cp = pltpu.make_async_remote_copy(
            src_ref=in_ref, dst_ref=out_ref, send_sem=send_sem, recv_sem=recv_sem,
            device_id=(x, y, 1 - c), device_id_type=pl.DeviceIdType.MESH)
        cp.start()
        cp.wait()

    return pl.pallas_call(
        body, in_specs=[pl.BlockSpec(memory_space=pl.ANY)], out_specs=pl.BlockSpec(memory_space=pl.ANY),
        out_shape=jax.ShapeDtypeStruct(buf.shape, buf.dtype),
        scratch_shapes=[pltpu.SemaphoreType.DMA(()), pltpu.SemaphoreType.DMA(())], name=name)(buf)


def _sum_slots(buf):
    _, R, L = buf.shape

    def body(b_ref, o_ref):
        o_ref[...] = ((b_ref[0] + b_ref[1]) + b_ref[2]) + b_ref[3]

    return pl.pallas_call(
        body, grid=(R // PACK_ROW_TILE,),
        in_specs=[pl.BlockSpec((4, PACK_ROW_TILE, L), lambda i: (0, i, 0))],
        out_specs=pl.BlockSpec((PACK_ROW_TILE, L), lambda i: (i, 0)),
        out_shape=jax.ShapeDtypeStruct((R, L), f32), name="sum_slots",
        compiler_params=_params("parallel"))(buf)


def _adamw(w, ga, gb, m, v):
    R, L = w.shape
    c1 = 1.0 / (1.0 - ADAM_B1 ** ADAM_STEP)
    c2 = 1.0 / (1.0 - ADAM_B2 ** ADAM_STEP)

    def body(w_ref, ga_ref, gb_ref, m_ref, v_ref, g_out, d_out, m_out, v_out):
        g = ga_ref[...] + gb_ref[...]
        m_new = ADAM_B1 * m_ref[...] + (1.0 - ADAM_B1) * g
        v_new = ADAM_B2 * v_ref[...] + (1.0 - ADAM_B2) * (g * g)
        g_out[...] = g
        m_out[...] = m_new
        v_out[...] = v_new
        d_out[...] = -ADAM_LR * ((m_new * c1) / (jnp.sqrt(v_new * c2) + ADAM_EPS) + ADAM_WD * w_ref[...])

    spec = pl.BlockSpec((PACK_ROW_TILE, L), lambda i: (i, 0))
    return pl.pallas_call(
        body, grid=(R // PACK_ROW_TILE,), in_specs=[spec] * 5, out_specs=[spec] * 4,
        out_shape=[jax.ShapeDtypeStruct((R, L), f32)] * 4, name="adamw",
        compiler_params=_params("parallel"))(w, ga, gb, m, v)


def _step(x, loss_target, P, M, V):
    shapes = {n: tuple(P[n].shape) for n in WEIGHTS}
    n_pos = 4

    gather_in = _pack([P[n].astype(bf16) for n in BIG]
                      + [lax.bitcast_convert_type(P[n], bf16) for n in SMALL_SHARDED], bf16)
    gathered = _xy_exchange("gather_weights", gather_in, scatter=False)
    gshapes = [shapes[n] for n in BIG] + [shapes[n] + (2,) for n in SMALL_SHARDED]
    per_pos = [_unpack(gathered[j], gshapes) for j in range(n_pos)]
    W = {n: P[n] for n in SMALL}
    for q, n in enumerate(BIG + SMALL_SHARDED):
        parts = [per_pos[j][q] for j in range(n_pos)]
        if n in SMALL_SHARDED:
            parts = [lax.bitcast_convert_type(t, f32) for t in parts]
        W[n] = jnp.concatenate(parts, axis=0 if n in ROW_SHARDED else 1)
    W['w_in_pad'] = _pad_w_in(W.pop('w_in'))

    loss_rows, dx, G = _local_step(x, loss_target, W)
    G['w_in'] = _unpad_w_in(G.pop('w_in_pad'))

    slabs = []
    for j in range(n_pos):
        parts = []
        for n in SHARDED:
            axis = 0 if n in ROW_SHARDED else 1
            width = shapes[n][axis]
            parts.append(lax.slice_in_dim(G[n], j * width, (j + 1) * width, axis=axis))
        parts += [G[n] for n in SMALL]
        slabs.append(_pack(parts, f32))
    contributions = _xy_exchange("scatter_grads", jnp.stack(slabs), scatter=True)
    plane_sum = _sum_slots(contributions)
    sibling_sum = _sibling_exchange("sibling_grads", plane_sum)

    pack_shapes = [shapes[n] for n in PACK_ORDER]
    g, delta, m_new, v_new = _adamw(_pack([P[n] for n in PACK_ORDER], f32), plane_sum, sibling_sum,
                                    _pack([M[n] for n in PACK_ORDER], f32), _pack([V[n] for n in PACK_ORDER], f32))
    out = {}
    for tag, buf in (('grad', g), ('delta', delta), ('new_m', m_new), ('new_v', v_new)):
        for n, t in zip(PACK_ORDER, _unpack(buf, pack_shapes)):
            out[tag + '_' + n] = t
    loss = lax.psum(loss_rows[0, 0], ("x", "y", "c"))
    return loss, dx, out


def kernel(x, norm1_g, w_in, rwkv_mu, rwkv_w0, rwkv_w2, rwkv_a0, rwkv_a2, rwkv_g2, rwkv_k_k, rwkv_k_a, rwkv_r_k, rwkv_ln_w, rwkv_ln_b, rwkv_proj, gdn_conv_w, gdn_a_log, gdn_dt_bias, gdn_norm_w, gdn_proj, w_out, norm2_g, ffn_up, ffn_conv_w, ffn_down, final_g, loss_target, m_norm1_g, m_w_in, m_rwkv_mu, m_rwkv_w0, m_rwkv_w2, m_rwkv_a0, m_rwkv_a2, m_rwkv_g2, m_rwkv_k_k, m_rwkv_k_a, m_rwkv_r_k, m_rwkv_ln_w, m_rwkv_ln_b, m_rwkv_proj, m_gdn_conv_w, m_gdn_a_log, m_gdn_dt_bias, m_gdn_norm_w, m_gdn_proj, m_w_out, m_norm2_g, m_ffn_up, m_ffn_conv_w, m_ffn_down, m_final_g, v_norm1_g, v_w_in, v_rwkv_mu, v_rwkv_w0, v_rwkv_w2, v_rwkv_a0, v_rwkv_a2, v_rwkv_g2, v_rwkv_k_k, v_rwkv_k_a, v_rwkv_r_k, v_rwkv_ln_w, v_rwkv_ln_b, v_rwkv_proj, v_gdn_conv_w, v_gdn_a_log, v_gdn_dt_bias, v_gdn_norm_w, v_gdn_proj, v_w_out, v_norm2_g, v_ffn_up, v_ffn_conv_w, v_ffn_down, v_final_g):
    weights = (norm1_g, w_in, rwkv_mu, rwkv_w0, rwkv_w2, rwkv_a0, rwkv_a2, rwkv_g2, rwkv_k_k, rwkv_k_a, rwkv_r_k, rwkv_ln_w,
               rwkv_ln_b, rwkv_proj, gdn_conv_w, gdn_a_log, gdn_dt_bias, gdn_norm_w, gdn_proj, w_out, norm2_g, ffn_up,
               ffn_conv_w, ffn_down, final_g)
    m_in = (m_norm1_g, m_w_in, m_rwkv_mu, m_rwkv_w0, m_rwkv_w2, m_rwkv_a0, m_rwkv_a2, m_rwkv_g2, m_rwkv_k_k, m_rwkv_k_a,
            m_rwkv_r_k, m_rwkv_ln_w, m_rwkv_ln_b, m_rwkv_proj, m_gdn_conv_w, m_gdn_a_log, m_gdn_dt_bias, m_gdn_norm_w,
            m_gdn_proj, m_w_out, m_norm2_g, m_ffn_up, m_ffn_conv_w, m_ffn_down, m_final_g)
    v_in = (v_norm1_g, v_w_in, v_rwkv_mu, v_rwkv_w0, v_rwkv_w2, v_rwkv_a0, v_rwkv_a2, v_rwkv_g2, v_rwkv_k_k, v_rwkv_k_a,
            v_rwkv_r_k, v_rwkv_ln_w, v_rwkv_ln_b, v_rwkv_proj, v_gdn_conv_w, v_gdn_a_log, v_gdn_dt_bias, v_gdn_norm_w,
            v_gdn_proj, v_w_out, v_norm2_g, v_ffn_up, v_ffn_conv_w, v_ffn_down, v_final_g)
    drop = lambda n, a: a if n == 'final_g' else a[0]
    P = {n: drop(n, a) for n, a in zip(WEIGHTS, weights)}
    M = {n: drop(n, a) for n, a in zip(WEIGHTS, m_in)}
    V = {n: drop(n, a) for n, a in zip(WEIGHTS, v_in)}
    loss, dx, out = _step(x[0], loss_target[0], P, M, V)
    lift = lambda n, a: a if n == 'final_g' else a[None]
    res = [loss, dx[None]]
    for tag in ('grad', 'delta', 'new_m', 'new_v'):
        res += [lift(n, out[tag + '_' + n]) for n in WEIGHTS]
    return tuple(res)
```

```python
import jax
import jax.numpy as jnp
from jax import lax
from jax.experimental import pallas as pl
from jax.experimental.pallas import tpu as pltpu

f32 = jnp.float32
bf16 = jnp.bfloat16
HI = lax.Precision.HIGHEST

D_MODEL = 1024
RWKV_HEADS, RWKV_HD, RWKV_W = 8, 64, 512
GDN_HEADS, GDN_HD, GDN_W = 4, 128, 512
FFN_H = 2816
NORM_EPS, L2_EPS, GN_EPS = 1e-6, 1e-6, 64e-5
W_AB = 256
OFF_QKV, OFF_Z, OFF_GATES, OFF_AB = 1792, 3328, 3840, 5888
W_IN_PAD = OFF_AB + W_AB
WKV_CHUNK = 64
GDN_CHUNK = 128
HALO = 8
LANES = 128
TILE_BYTES = 1 << 20
VMEM_LIMIT = 56 * 1024 * 1024

ADAM_LR, ADAM_B1, ADAM_B2, ADAM_EPS, ADAM_WD, ADAM_STEP = 0.001, 0.9, 0.999, 1e-08, 0.01, 10

ROW_SHARDED = ('w_out', 'ffn_down')
SMALL = ('norm1_g', 'rwkv_mu', 'rwkv_w0', 'rwkv_a0', 'rwkv_k_k', 'rwkv_k_a', 'rwkv_r_k', 'rwkv_ln_w', 'rwkv_ln_b',
         'gdn_a_log', 'gdn_dt_bias', 'gdn_norm_w', 'norm2_g', 'final_g')
WEIGHTS = ('norm1_g', 'w_in', 'rwkv_mu', 'rwkv_w0', 'rwkv_w2', 'rwkv_a0', 'rwkv_a2', 'rwkv_g2', 'rwkv_k_k', 'rwkv_k_a',
           'rwkv_r_k', 'rwkv_ln_w', 'rwkv_ln_b', 'rwkv_proj', 'gdn_conv_w', 'gdn_a_log', 'gdn_dt_bias', 'gdn_norm_w',
           'gdn_proj', 'w_out', 'norm2_g', 'ffn_up', 'ffn_conv_w', 'ffn_down', 'final_g')


def _params(*sem):
    return pltpu.CompilerParams(dimension_semantics=sem, vmem_limit_bytes=VMEM_LIMIT)


def _tile(n, limit):
    if n <= limit:
        return n
    best = None
    for d in range(128, limit + 1, 128):
        if n % d == 0:
            best = d
    if best is None:
        raise ValueError(f"no tile for {n} under {limit}")
    return best


MM_BLOCK_BYTES = 4 << 20


def _mm(a, b, mode, name, add=None, out_dtype=f32):
    if mode == 'nn':
        (M, K), N = a.shape, b.shape[1]
    elif mode == 'nt':
        (M, K), N = a.shape, b.shape[0]
    else:
        (K, M), N = a.shape, b.shape[1]
    tm = _tile(M, 512)
    tk = _tile(K, min(2816, MM_BLOCK_BYTES // (tm * a.dtype.itemsize)))
    tn = _tile(N, max(128, min(MM_BLOCK_BYTES // (tk * b.dtype.itemsize), MM_BLOCK_BYTES // (tm * 4)) // 128 * 128))
    nk = K // tk
    dn = {'nn': (((1,), (0,)), ((), ())), 'nt': (((1,), (1,)), ((), ())), 'tn': (((0,), (0,)), ((), ()))}[mode]

    def body(a_ref, b_ref, *rest):
        o_ref = rest[1] if add is not None else rest[0]
        acc = lax.dot_general(a_ref[...].astype(bf16), b_ref[...].astype(bf16), dn, preferred_element_type=f32)
        if nk == 1:
            o_ref[...] = (acc + rest[0][...] if add is not None else acc).astype(out_dtype)
            return
        acc_ref = rest[-1]
        k = pl.program_id(2)

        @pl.when(k == 0)
        def _():
            acc_ref[...] = acc + rest[0][...] if add is not None else acc

        @pl.when(k > 0)
        def _():
            acc_ref[...] += acc

        @pl.when(k == nk - 1)
        def _():
            o_ref[...] = acc_ref[...].astype(out_dtype)

    a_spec = (pl.BlockSpec((tk, tm), lambda i, j, k: (k, i)) if mode == 'tn'
              else pl.BlockSpec((tm, tk), lambda i, j, k: (i, k)))
    b_spec = (pl.BlockSpec((tn, tk), lambda i, j, k: (j, k)) if mode == 'nt'
              else pl.BlockSpec((tk, tn), lambda i, j, k: (k, j)))
    o_spec = pl.BlockSpec((tm, tn), lambda i, j, k: (i, j))
    ins, specs = [a, b], [a_spec, b_spec]
    if add is not None:
        ins.append(add)
        specs.append(o_spec)
    return pl.pallas_call(
        body, grid=(M // tm, N // tn, nk), in_specs=specs, out_specs=o_spec,
        out_shape=jax.ShapeDtypeStruct((M, N), out_dtype),
        scratch_shapes=[pltpu.VMEM((tm, tn), f32)] if nk > 1 else [], name=name,
        compiler_params=_params("parallel", "parallel", "arbitrary"))(*ins)


def _shift_down(cur, prev, s):
    if s == 0:
        return cur
    ext = jnp.concatenate([prev, cur], axis=0)
    return pltpu.roll(ext, s, 0)[HALO:]


def _shift_up(cur, nxt, s):
    if s == 0:
        return cur
    ext = jnp.concatenate([cur, nxt], axis=0)
    return pltpu.roll(ext, ext.shape[0] - s, 0)[:cur.shape[0]]


def _conv_apply(cur, prev, w_ref):
    taps = w_ref.shape[0]
    out = None
    for i in range(taps):
        term = _shift_down(cur, prev, taps - 1 - i) * w_ref[pl.ds(i, 1), :]
        out = term if out is None else out + term
    return out


def _row_spec(tm, w):
    return pl.BlockSpec((tm, w), lambda i: (i, 0))


def _prev_spec(tm, w):
    return pl.BlockSpec((HALO, w), lambda i: (jnp.maximum(i * (tm // HALO) - 1, 0), 0))


def _next_spec(tm, w, T):
    return pl.BlockSpec((HALO, w), lambda i: (jnp.minimum((i + 1) * (tm // HALO), T // HALO - 1), 0))


def _full_spec(shape):
    return pl.BlockSpec(shape, lambda i: (0,) * len(shape))


def _pw_fwd(name, fn, rows, consts, out_widths, tm, conv_w=None, out_dtype=f32):
    T = rows[0].shape[0]
    nr, nc = len(rows), len(consts)

    def body(*refs):
        i = pl.program_id(0)
        vals = [r[...] for r in refs[:nr]]
        p = nr
        if conv_w is not None:
            prev = jnp.where(i > 0, refs[p][...], 0.0)
            vals[0] = _conv_apply(vals[0], prev, refs[p + 1])
            p += 2
        cvals = [r[...] for r in refs[p:p + nc]]
        outs = fn(*vals, *cvals)
        for o_ref, o in zip(refs[p + nc:], outs):
            o_ref[...] = o.astype(out_dtype)

    ins = list(rows)
    specs = [_row_spec(tm, r.shape[1]) for r in rows]
    if conv_w is not None:
        ins += [rows[0], conv_w]
        specs += [_prev_spec(tm, rows[0].shape[1]), _full_spec(conv_w.shape)]
    ins += list(consts)
    specs += [_full_spec(c.shape) for c in consts]
    outs = pl.pallas_call(
        body, grid=(T // tm,), in_specs=specs,
        out_specs=[_row_spec(tm, w) for w in out_widths],
        out_shape=[jax.ShapeDtypeStruct((T, w), out_dtype) for w in out_widths], name=name,
        compiler_params=_params("parallel"))(*ins)
    return outs


def _pw_bwd(name, fn, rows, consts, cots, tm, conv_w=None, add_to_first=None, row_dtypes=None):
    T = rows[0].shape[0]
    nr, nc = len(rows), len(consts)
    flat_cots = [c for grp in cots for c in grp]
    row_dtypes = row_dtypes or [f32] * nr

    def body(*refs):
        i = pl.program_id(0)
        vals = [r[...] for r in refs[:nr]]
        p = nr
        if conv_w is not None:
            prev = jnp.where(i > 0, refs[p][...], 0.0)
            vals[0] = _conv_apply(vals[0], prev, refs[p + 1])
            p += 2
        cvals = [r[...] for r in refs[p:p + nc]]
        p += nc
        cot_vals = []
        for grp in cots:
            acc = refs[p][...]
            for q in range(1, len(grp)):
                acc = acc + refs[p + q][...]
            p += len(grp)
            cot_vals.append(acc)
        extra = None
        if add_to_first is not None:
            extra = refs[p][...]
            p += 1
        _, vjp = jax.vjp(fn, *vals, *cvals)
        grads = vjp(tuple(cot_vals))
        row_out = refs[p:p + nr]
        const_out = refs[p + nr:]
        for q in range(nr):
            g = grads[q]
            if q == 0 and extra is not None:
                g = g + extra
            row_out[q][...] = g.astype(row_dtypes[q])

        @pl.when(i == 0)
        def _():
            for q in range(nc):
                const_out[q][...] = grads[nr + q]

        @pl.when(i > 0)
        def _():
            for q in range(nc):
                const_out[q][...] += grads[nr + q]

    ins = list(rows)
    specs = [_row_spec(tm, r.shape[1]) for r in rows]
    if conv_w is not None:
        ins += [rows[0], conv_w]
        specs += [_prev_spec(tm, rows[0].shape[1]), _full_spec(conv_w.shape)]
    ins += list(consts)
    specs += [_full_spec(c.shape) for c in consts]
    ins += flat_cots
    specs += [_row_spec(tm, c.shape[1]) for c in flat_cots]
    if add_to_first is not None:
        ins.append(add_to_first)
        specs.append(_row_spec(tm, add_to_first.shape[1]))
    out_shapes = ([jax.ShapeDtypeStruct(r.shape, d) for r, d in zip(rows, row_dtypes)]
                  + [jax.ShapeDtypeStruct(c.shape, f32) for c in consts])
    out_specs = [_row_spec(tm, r.shape[1]) for r in rows] + [_full_spec(c.shape) for c in consts]
    outs = pl.pallas_call(
        body, grid=(T // tm,), in_specs=specs, out_specs=out_specs, out_shape=out_shapes, name=name,
        compiler_params=_params("arbitrary"))(*ins)
    return list(outs[:nr]), list(outs[nr:])


def _conv_bwd(name, dc, x, w, tm, dx_dtype=f32):
    T, W = x.shape
    taps = w.shape[0]
    nblk = T // tm

    def body(dc_ref, dcn_ref, x_ref, xp_ref, w_ref, dx_ref, dw_ref):
        i = pl.program_id(0)
        dcv, xv = dc_ref[...], x_ref[...]
        nxt = jnp.where(i < nblk - 1, dcn_ref[...], 0.0)
        prev = jnp.where(i > 0, xp_ref[...], 0.0)

        @pl.when(i == 0)
        def _():
            dw_ref[...] = jnp.zeros_like(dw_ref)

        dx = None
        for k in range(taps):
            s = taps - 1 - k
            term = _shift_up(dcv, nxt, s) * w_ref[pl.ds(k, 1), :]
            dx = term if dx is None else dx + term
            dw_ref[pl.ds(k, 1), :] += jnp.sum(dcv * _shift_down(xv, prev, s), axis=0, keepdims=True)
        dx_ref[...] = dx.astype(dx_dtype)

    return pl.pallas_call(
        body, grid=(nblk,),
        in_specs=[_row_spec(tm, W), _next_spec(tm, W, T), _row_spec(tm, W), _prev_spec(tm, W), _full_spec(w.shape)],
        out_specs=[_row_spec(tm, W), _full_spec(w.shape)],
        out_shape=[jax.ShapeDtypeStruct((T, W), dx_dtype), jax.ShapeDtypeStruct(w.shape, f32)], name=name,
        compiler_params=_params("arbitrary"))(dc, dc, x, x, w)


def _sigmoid(x):
    return 1.0 / (1.0 + jnp.exp(-x))


def _softplus(x):
    return jnp.maximum(x, 0.0) + jnp.log(1.0 + jnp.exp(jnp.minimum(x, -x)))


def _seg_sum(x, seg):
    w = x.shape[-1]
    r = lax.broadcasted_iota(jnp.int32, (w, w), 0) // seg
    c = lax.broadcasted_iota(jnp.int32, (w, w), 1) // seg
    return jnp.dot(x, (r == c).astype(f32), precision=HI, preferred_element_type=f32)


def _rms(x, g):
    return x * lax.rsqrt(jnp.mean(x * x, axis=-1, keepdims=True) + NORM_EPS) * g


def _rms_fn(x, g):
    return (_rms(x, g),)


def _loss_rows(x2, tgt, g):
    e = _rms(x2, g) - tgt
    return 0.5 * jnp.sum(e * e, axis=-1, keepdims=True) * (1.0 / D_MODEL)


def _rwkv_prep_fn(ps, w0, w2p, a0, a2p, g2, k_k, k_a):
    r, k, v = ps[:, 0:512], ps[:, 512:1024], ps[:, 1024:1536]
    wa, gl = ps[:, 1536:1664], ps[:, 1664:1792]
    z = w0 + jnp.dot(jnp.tanh(wa), w2p, precision=HI, preferred_element_type=f32)
    w_log = -_softplus(-z) - 0.5
    lw = -jnp.exp(w_log)
    a = _sigmoid(a0 + jnp.dot(wa, a2p, precision=HI, preferred_element_type=f32))
    g = jnp.dot(_sigmoid(gl), g2, precision=HI, preferred_element_type=f32)
    kx = k * k_k
    kk = kx * lax.rsqrt(_seg_sum(kx * kx, RWKV_HD) + L2_EPS)
    k2 = k * (1.0 + (a - 1.0) * k_a)
    return r, lw, k2, v, -kk, kk * a, g


def _rwkv_post_fn(y, r, k2, v, g, ln_w, ln_b, rk):
    mean = _seg_sum(y, RWKV_HD) * (1.0 / RWKV_HD)
    yc = y - mean
    var = _seg_sum(yc * yc, RWKV_HD) * (1.0 / RWKV_HD)
    yn = yc * lax.rsqrt(var + GN_EPS) * ln_w + ln_b
    bonus = _seg_sum(r * k2 * rk, RWKV_HD) * v
    return ((yn + bonus) * g,)


def _gdn_prep_fn(c, ab, al_p, dt_p):
    s = c * _sigmoid(c)
    q, k, v = s[:, 0:512], s[:, 512:1024], s[:, 1024:1536]
    q = q * lax.rsqrt(_seg_sum(q * q, GDN_HD) + L2_EPS) * (GDN_HD ** -0.5)
    k = k * lax.rsqrt(_seg_sum(k * k, GDN_HD) + L2_EPS)
    lane = lax.broadcasted_iota(jnp.int32, ab.shape, 1)
    gpart = -jnp.exp(al_p) * _softplus(ab + dt_p)
    gbeta = jnp.where(lane < GDN_HEADS, gpart, jnp.where(lane < 2 * GDN_HEADS, _sigmoid(ab), 0.0))
    return q, k, v, gbeta


def _gdn_post_fn(o, z, nw):
    ms = _seg_sum(o * o, GDN_HD) * (1.0 / GDN_HD)
    return (o * lax.rsqrt(ms + NORM_EPS) * nw * (z * _sigmoid(z)),)


def _mix_fn(gates, ya, yb):
    return (_sigmoid(gates[:, :D_MODEL]) * ya + _sigmoid(gates[:, D_MODEL:]) * yb,)


def _ffn_fn(c):
    hg, hu = c[:, :FFN_H], c[:, FFN_H:]
    return (hg * _sigmoid(hg) * hu,)


SCAN_PREC = lax.Precision.HIGH


def _bmm(a, b):
    return jnp.einsum('hcs,hsd->hcd', a, b, precision=SCAN_PREC, preferred_element_type=f32)


def _bmm_nt(a, b):
    return jnp.einsum('hcd,hsd->hcs', a, b, precision=SCAN_PREC, preferred_element_type=f32)


def _bmm_tn(a, b):
    return jnp.einsum('hcd,hce->hde', a, b, precision=SCAN_PREC, preferred_element_type=f32)


def _masks(H, C):
    row = lax.broadcasted_iota(jnp.int32, (H, C, C), 1)
    col = lax.broadcasted_iota(jnp.int32, (H, C, C), 2)
    return row, col


def _tri_inv_impl(L):
    H, C, _ = L.shape
    row, col = _masks(H, C)
    eye = (row == col).astype(f32)
    base = 16
    same = (row // base) == (col // base)
    Ld = jnp.where(same, L, 0.0)
    X = -Ld
    inv = eye + X
    for _ in range(3):
        X = _bmm(X, X)
        inv = _bmm(inv, eye + X)
    if C == base:
        return inv
    N = _bmm(inv, L - Ld)
    out = eye - N
    levels = C // base
    P = N
    span = 2
    while span < levels:
        P = _bmm(P, P)
        out = _bmm(out, eye + P)
        span *= 2
    return _bmm(out, inv)


@jax.custom_vjp
def _tri_inv(L):
    return _tri_inv_impl(L)


def _tri_inv_fwd(L):
    T = _tri_inv_impl(L)
    return T, T


def _tri_inv_bwd(T, dT):
    return (-_bmm_nt(_bmm_tn(T, dT), T),)


_tri_inv.defvjp(_tri_inv_fwd, _tri_inv_bwd)


def _cumsum_impl(x, reverse):
    C = x.shape[1]
    row = lax.broadcasted_iota(jnp.int32, x.shape, 1)
    s = 1
    while s < C:
        if reverse:
            x = x + jnp.where(row < C - s, pltpu.roll(x, C - s, 1), 0.0)
        else:
            x = x + jnp.where(row >= s, pltpu.roll(x, s, 1), 0.0)
        s *= 2
    return x


@jax.custom_vjp
def _cumsum(x):
    return _cumsum_impl(x, False)


_cumsum.defvjp(lambda x: (_cumsum_impl(x, False), None), lambda _, g: (_cumsum_impl(g, True),))


def _wkv_chunk(Z, r, lw, k, v, a, b):
    H, C, D = r.shape
    row, col = _masks(H, C)
    incl, strict = row >= col, row > col
    cw = _cumsum(lw)
    cwp = cw - lw
    cwl = jnp.sum(lw, axis=1, keepdims=True)
    en = jnp.exp(-cw)
    at, rt, bt, kt = a * jnp.exp(cwp), r * jnp.exp(cw), b * en, k * en
    Lab = jnp.where(strict, _bmm_nt(at, bt), 0.0)
    Lak = jnp.where(strict, _bmm_nt(at, kt), 0.0)
    Tm = _tri_inv(-Lab)
    U = _bmm(Tm, _bmm(at, Z) + _bmm(Lak, v))
    Rb = jnp.where(incl, _bmm_nt(rt, bt), 0.0)
    Rk = jnp.where(incl, _bmm_nt(rt, kt), 0.0)
    y = _bmm(rt, Z) + _bmm(Rb, U) + _bmm(Rk, v)
    ed = jnp.exp(cwl - cw)
    zdec = jnp.swapaxes(jnp.broadcast_to(jnp.exp(cwl), (H, Z.shape[2], D)), 1, 2)
    Z1 = Z * zdec + _bmm_tn(b * ed, U) + _bmm_tn(k * ed, v)
    return y, Z1


def _gdn_chunk(S, q, k, v, g, beta):
    H, C, D = q.shape
    row, col = _masks(H, C)
    incl, strict = row >= col, row > col
    gc = _cumsum(g)
    diff = gc - jnp.swapaxes(gc, 1, 2)
    decay = jnp.where(incl, jnp.exp(jnp.where(incl, diff, 0.0)), 0.0)
    gl = jnp.sum(g, axis=1, keepdims=True)
    kb, vb = k * beta, v * beta
    L = jnp.where(strict, _bmm_nt(kb, k) * decay, 0.0)
    Tm = _tri_inv(L)
    egc = jnp.exp(gc)
    u = _bmm(Tm, vb)
    wk = _bmm(Tm, kb * egc)
    attn = jnp.where(incl, _bmm_nt(q, k) * decay, 0.0)
    v_new = u - _bmm(wk, S)
    o = _bmm(q * egc, S) + _bmm(attn, v_new)
    S1 = S * jnp.exp(gl) + _bmm_tn(k * jnp.exp(gl - gc), v_new)
    return o, S1


def _wkv_block(Z, r, lw, k, v, a, b):
    lane = lax.broadcasted_iota(jnp.int32, (r.shape[0], 128), 1)
    low = lane < RWKV_HD

    def heads(t):
        out = []
        for p in range(RWKV_HEADS // 2):
            pair = t[:, 128 * p:128 * (p + 1)]
            out += [jnp.where(low, pair, 0.0), jnp.where(low, 0.0, pair)]
        return jnp.stack(out)

    y, Z1 = _wkv_chunk(Z, *[heads(t) for t in (r, lw, k, v, a, b)])
    return jnp.concatenate([y[2 * p] + y[2 * p + 1] for p in range(RWKV_HEADS // 2)], axis=1), Z1


def _gdn_block(S, q, k, v, gbeta):
    heads = lambda t: jnp.stack([t[:, GDN_HD * h:GDN_HD * (h + 1)] for h in range(GDN_HEADS)])
    src = lax.broadcasted_iota(jnp.int32, (W_AB, 2 * GDN_W), 0)
    dst = lax.broadcasted_iota(jnp.int32, (W_AB, 2 * GDN_W), 1) // GDN_HD
    spread = jnp.dot(gbeta, (src == dst).astype(f32), precision=HI, preferred_element_type=f32)
    o, S1 = _gdn_chunk(S, heads(q), heads(k), heads(v), heads(spread[:, :GDN_W]), heads(spread[:, GDN_W:]))
    return jnp.concatenate([o[h] for h in range(GDN_HEADS)], axis=1), S1


def _scan_fwd(name, block_fn, ins, C, H, dh, w_out):
    T = ins[0].shape[0]
    n_in = len(ins)

    def body(*refs):
        in_refs, y_ref, zs_ref, z_scr = refs[:n_in], refs[n_in], refs[n_in + 1], refs[n_in + 2]

        @pl.when(pl.program_id(0) == 0)
        def _():
            z_scr[...] = jnp.zeros_like(z_scr)

        Z = z_scr[...]
        zs_ref[0] = Z
        y, Z1 = block_fn(Z, *[r[...] for r in in_refs])
        y_ref[...] = y
        z_scr[...] = Z1

    return pl.pallas_call(
        body, grid=(T // C,),
        in_specs=[pl.BlockSpec((C, a.shape[1]), lambda i: (i, 0)) for a in ins],
        out_specs=[pl.BlockSpec((C, w_out), lambda i: (i, 0)), pl.BlockSpec((1, H, dh, dh), lambda i: (i, 0, 0, 0))],
        out_shape=[jax.ShapeDtypeStruct((T, w_out), f32), jax.ShapeDtypeStruct((T // C, H, dh, dh), f32)],
        scratch_shapes=[pltpu.VMEM((H, dh, dh), f32)], name=name,
        compiler_params=_params("arbitrary"))(*ins)


def _scan_bwd(name, block_fn, ins, dy, zs, C):
    T = ins[0].shape[0]
    _, H, dh, _ = zs.shape
    n_in = len(ins)
    nblk = T // C

    def body(*refs):
        in_refs, dy_ref, zs_ref = refs[:n_in], refs[n_in], refs[n_in + 1]
        out_refs, dz_scr = refs[n_in + 2:2 * n_in + 2], refs[2 * n_in + 2]

        @pl.when(pl.program_id(0) == 0)
        def _():
            dz_scr[...] = jnp.zeros_like(dz_scr)

        _, vjp = jax.vjp(block_fn, zs_ref[0], *[r[...] for r in in_refs])
        grads = vjp((dy_ref[...], dz_scr[...]))
        dz_scr[...] = grads[0]
        for o_ref, gval in zip(out_refs, grads[1:]):
            o_ref[...] = gval

    rev = lambda i: (nblk - 1 - i, 0)
    return pl.pallas_call(
        body, grid=(nblk,),
        in_specs=[pl.BlockSpec((C, a.shape[1]), rev) for a in ins]
        + [pl.BlockSpec((C, dy.shape[1]), rev), pl.BlockSpec((1, H, dh, dh), lambda i: (nblk - 1 - i, 0, 0, 0))],
        out_specs=[pl.BlockSpec((C, a.shape[1]), rev) for a in ins],
        out_shape=[jax.ShapeDtypeStruct(a.shape, f32) for a in ins],
        scratch_shapes=[pltpu.VMEM((H, dh, dh), f32)], name=name,
        compiler_params=_params("arbitrary"))(*ins, dy, zs)


def _loss_call(x2, tgt, g, tm):
    T, W = x2.shape

    def body(x_ref, t_ref, g_ref, dx_ref, dg_ref, l_ref):
        i = pl.program_id(0)
        tv = t_ref[...]
        l, vjp = jax.vjp(lambda xv, gv: _loss_rows(xv, tv, gv), x_ref[...], g_ref[...])
        dx, dg = vjp(jnp.ones_like(l))
        dx_ref[...] = dx
        tot = jnp.zeros((1, 128), f32) + jnp.sum(l)

        @pl.when(i == 0)
        def _():
            dg_ref[...] = dg
            l_ref[...] = tot

        @pl.when(i > 0)
        def _():
            dg_ref[...] += dg
            l_ref[...] += tot

    return pl.pallas_call(
        body, grid=(T // tm,),
        in_specs=[_row_spec(tm, W), _row_spec(tm, W), _full_spec(g.shape)],
        out_specs=[_row_spec(tm, W), _full_spec(g.shape), _full_spec((1, 128))],
        out_shape=[jax.ShapeDtypeStruct((T, W), f32), jax.ShapeDtypeStruct(g.shape, f32),
                   jax.ShapeDtypeStruct((1, 128), f32)], name="loss_head",
        compiler_params=_params("arbitrary"))(x2, tgt, g)


def _local_step(x, tgt, W):
    row = lambda a: a.reshape(1, -1)
    wp = W['w_in_pad']
    w_rwkv, w_qkv, w_z = wp[:, :OFF_QKV], wp[:, OFF_QKV:OFF_Z], wp[:, OFF_Z:OFF_GATES]
    w_gates, w_ab = wp[:, OFF_GATES:OFF_AB], wp[:, OFF_AB:]
    mu = row(W['rwkv_mu'])
    mixw = jnp.concatenate([mu, 1.0 - mu], axis=0)
    zpad = jnp.zeros((64, RWKV_W), f32)
    w2p = jnp.concatenate([W['rwkv_w2'], zpad], axis=0)
    a2p = jnp.concatenate([zpad, W['rwkv_a2']], axis=0)
    rw_consts = [row(W['rwkv_w0']), w2p, row(W['rwkv_a0']), a2p, W['rwkv_g2'], row(W['rwkv_k_k']), row(W['rwkv_k_a'])]
    post_consts = [row(W['rwkv_ln_w']), row(W['rwkv_ln_b']), row(W['rwkv_r_k'])]
    pad4 = lambda a: jnp.pad(row(a), ((0, 0), (0, W_AB - GDN_HEADS)))
    gd_consts = [pad4(W['gdn_a_log']), pad4(W['gdn_dt_bias'])]
    nw_t = jnp.tile(row(W['gdn_norm_w']), (1, GDN_HEADS))
    g1, g2n, gf = row(W['norm1_g']), row(W['norm2_g']), row(W['final_g'])

    (u,) = _pw_fwd("norm1", _rms_fn, [x], [g1], [D_MODEL], 256, out_dtype=bf16)
    p_rwkv = _mm(u, w_rwkv, 'nn', "in_rwkv")
    qkv_raw = _mm(u, w_qkv, 'nn', "in_qkv")
    z = _mm(u, w_z, 'nn', "in_z")
    gates = _mm(u, w_gates, 'nn', "in_gates")
    ab = _mm(u, w_ab, 'nn', "in_ab")

    r, lw, k2, v, a_, b_, g = _pw_fwd("rwkv_prep", _rwkv_prep_fn, [p_rwkv], rw_consts, [RWKV_W] * 7, 256, conv_w=mixw)
    wkv_in = [r, lw, k2, v, a_, b_]
    y, zs_wkv = _scan_fwd("wkv_fwd", _wkv_block, wkv_in, WKV_CHUNK, RWKV_HEADS, 2 * RWKV_HD, RWKV_W)
    (ya_in,) = _pw_fwd("rwkv_post", _rwkv_post_fn, [y, r, k2, v, g], post_consts, [RWKV_W], 256, out_dtype=bf16)
    ya = _mm(ya_in, W['rwkv_proj'], 'nn', "rwkv_proj")

    gq, gk, gv, gbeta = _pw_fwd("gdn_prep", _gdn_prep_fn, [qkv_raw, ab], gd_consts, [GDN_W] * 3 + [W_AB], 256,
                                conv_w=W['gdn_conv_w'])
    gdn_in = [gq, gk, gv, gbeta]
    o, zs_gdn = _scan_fwd("gdn_fwd", _gdn_block, gdn_in, GDN_CHUNK, GDN_HEADS, GDN_HD, GDN_W)
    (yb_in,) = _pw_fwd("gdn_post", _gdn_post_fn, [o, z], [nw_t], [GDN_W], 256, out_dtype=bf16)
    yb = _mm(yb_in, W['gdn_proj'], 'nn', "gdn_proj")

    (mixed,) = _pw_fwd("mix", _mix_fn, [gates, ya, yb], [], [D_MODEL], 256, out_dtype=bf16)
    x1 = _mm(mixed, W['w_out'], 'nn', "w_out", add=x)
    (u2,) = _pw_fwd("norm2", _rms_fn, [x1], [g2n], [D_MODEL], 256, out_dtype=bf16)
    h = _mm(u2, W['ffn_up'], 'nn', "ffn_up")
    (act,) = _pw_fwd("ffn_act", _ffn_fn, [h], [], [FFN_H], 128, conv_w=W['ffn_conv_w'], out_dtype=bf16)
    x2 = _mm(act, W['ffn_down'], 'nn', "ffn_down", add=x1)

    G = {}
    dx2, dgf, loss = _loss_call(x2, tgt, gf, 256)
    G['final_g'] = dgf
    dact = _mm(dx2, W['ffn_down'], 'nt', "d_act")
    G['ffn_down'] = _mm(act, dx2, 'tn', "g_ffn_down")
    (dc_ffn,), _ = _pw_bwd("ffn_act_bwd", _ffn_fn, [h], [], [(dact,)], 128, conv_w=W['ffn_conv_w'])
    dh, G['ffn_conv_w'] = _conv_bwd("ffn_conv_bwd", dc_ffn, h, W['ffn_conv_w'], 128, dx_dtype=bf16)
    du2 = _mm(dh, W['ffn_up'], 'nt', "d_u2")
    G['ffn_up'] = _mm(u2, dh, 'tn', "g_ffn_up")
    (dx1,), (G['norm2_g'],) = _pw_bwd("norm2_bwd", _rms_fn, [x1], [g2n], [(du2,)], 256, add_to_first=dx2)
    dmixed = _mm(dx1, W['w_out'], 'nt', "d_mixed")
    G['w_out'] = _mm(mixed, dx1, 'tn', "g_w_out")
    (dgates, dya, dyb), _ = _pw_bwd("mix_bwd", _mix_fn, [gates, ya, yb], [], [(dmixed,)], 256, row_dtypes=[bf16] * 3)
    dya_in = _mm(dya, W['rwkv_proj'], 'nt', "d_ya_in")
    G['rwkv_proj'] = _mm(ya_in, dya, 'tn', "g_rwkv_proj")
    dyb_in = _mm(dyb, W['gdn_proj'], 'nt', "d_yb_in")
    G['gdn_proj'] = _mm(yb_in, dyb, 'tn', "g_gdn_proj")

    (do, dz), (dnw_t,) = _pw_bwd("gdn_post_bwd", _gdn_post_fn, [o, z], [nw_t], [(dyb_in,)], 256, row_dtypes=[f32, bf16])
    G['gdn_norm_w'] = dnw_t.reshape(GDN_HEADS, GDN_HD).sum(axis=0)
    dgq, dgk, dgv, dgbeta = _scan_bwd("gdn_bwd", _gdn_block, gdn_in, do, zs_gdn, GDN_CHUNK)
    (dc_qkv, dab), (dal_p, ddt_p) = _pw_bwd(
        "gdn_prep_bwd", _gdn_prep_fn, [qkv_raw, ab], gd_consts, [(dgq,), (dgk,), (dgv,), (dgbeta,)], 256,
        conv_w=W['gdn_conv_w'], row_dtypes=[f32, bf16])
    G['gdn_a_log'], G['gdn_dt_bias'] = dal_p[0, :GDN_HEADS], ddt_p[0, :GDN_HEADS]
    dqkv_raw, G['gdn_conv_w'] = _conv_bwd("gdn_conv_bwd", dc_qkv, qkv_raw, W['gdn_conv_w'], 256, dx_dtype=bf16)

    (dy, dr1, dk21, dv1, dg_), (G['rwkv_ln_w'], G['rwkv_ln_b'], G['rwkv_r_k']) = _pw_bwd(
        "rwkv_post_bwd", _rwkv_post_fn, [y, r, k2, v, g], post_consts, [(dya_in,)], 256)
    dr2, dlw, dk22, dv2, da_, db_ = _scan_bwd("wkv_bwd", _wkv_block, wkv_in, dy, zs_wkv, WKV_CHUNK)
    (dps,), rw_grads = _pw_bwd(
        "rwkv_prep_bwd", _rwkv_prep_fn, [p_rwkv], rw_consts,
        [(dr1, dr2), (dlw,), (dk21, dk22), (dv1, dv2), (da_,), (db_,), (dg_,)], 256, conv_w=mixw)
    G['rwkv_w0'], dw2p, G['rwkv_a0'], da2p, G['rwkv_g2'], G['rwkv_k_k'], G['rwkv_k_a'] = rw_grads
    G['rwkv_w2'], G['rwkv_a2'] = dw2p[:64], da2p[64:]
    dp_rwkv, dmixw = _conv_bwd("shift_bwd", dps, p_rwkv, mixw, 256, dx_dtype=bf16)
    G['rwkv_mu'] = dmixw[0] - dmixw[1]

    dp = jnp.concatenate([dp_rwkv, dqkv_raw, dz, dgates, dab], axis=1)
    du = _mm(dp, wp, 'nt', "d_u")
    G['w_in_pad'] = _mm(u, dp, 'tn', "g_w_in")
    (dx,), (G['norm1_g'],) = _pw_bwd("norm1_bwd", _rms_fn, [x], [g1], [(du,)], 256, add_to_first=dx1)
    return loss, dx, G


def _pad_w_in(w):
    return jnp.concatenate([w[:, :OFF_GATES], w[:, OFF_GATES + 8:], w[:, OFF_GATES:OFF_GATES + 8],
                            jnp.zeros((w.shape[0], W_AB - 8), w.dtype)], axis=1)


def _unpad_w_in(wp):
    return jnp.concatenate([wp[:, :OFF_GATES], wp[:, OFF_AB:OFF_AB + 8], wp[:, OFF_GATES:OFF_AB]], axis=1)


BIG = ('w_in', 'rwkv_proj', 'gdn_proj', 'w_out', 'ffn_up', 'ffn_down')
SMALL_SHARDED = ('rwkv_w2', 'rwkv_a2', 'rwkv_g2', 'gdn_conv_w', 'ffn_conv_w')
N_POS = 4


def _rows128(shape):
    n = 1
    for d in shape:
        n *= d
    return -(-n // LANES)


def _pack128(arrays):
    parts = []
    for a in arrays:
        flat = a.reshape(-1)
        rows = _rows128(a.shape)
        parts.append(jnp.pad(flat, (0, rows * LANES - flat.shape[0])).reshape(rows, LANES))
    buf = jnp.concatenate(parts, axis=0)
    return jnp.pad(buf, ((0, -buf.shape[0] % HALO), (0, 0)))


def _unpack128(buf, shapes):
    out, off = [], 0
    for s in shapes:
        rows, n = _rows128(s), 1
        for d in s:
            n *= d
        out.append(buf[off:off + rows].reshape(-1)[:n].reshape(s))
        off += rows
    return out


def _row_tile(r, c):
    best = None
    for d in range(HALO, r + 1, HALO):
        if r % d == 0 and d * c * 4 <= TILE_BYTES:
            best = d
    return best if best is not None else r


def _xy_exchange(name, bufs, scatter):
    n = len(bufs)

    def body(*refs):
        in_refs, out_refs = refs[:n], refs[n:2 * n]
        send_sems, recv_sems, local_sems = refs[2 * n:]
        x, y, c = lax.axis_index("x"), lax.axis_index("y"), lax.axis_index("c")
        me = 2 * x + y
        peers = [(1 - x, y), (x, 1 - y), (1 - x, 1 - y)]

        def copy(a, k, src, slot, peer):
            return pltpu.make_async_remote_copy(
                src_ref=src, dst_ref=out_refs[a].at[slot], send_sem=send_sems.at[3 * a + k],
                recv_sem=recv_sems.at[3 * a + k], device_id=(peer[0], peer[1], c), device_id_type=pl.DeviceIdType.MESH)

        started = []
        for a in range(n):
            own = pltpu.make_async_copy(in_refs[a].at[me] if scatter else in_refs[a], out_refs[a].at[me], local_sems.at[a])
            own.start()
            started.append(own)
        sends = []
        for a in range(n):
            for k, peer in enumerate(peers):
                src = in_refs[a].at[2 * peer[0] + peer[1]] if scatter else in_refs[a]
                sends.append(copy(a, k, src, me, peer))
                sends[-1].start()
        for a in range(n):
            for k, peer in enumerate(peers):
                src = in_refs[a].at[me] if scatter else in_refs[a]
                copy(a, k, src, 2 * peer[0] + peer[1], peer).wait_recv()
        for cp in sends:
            cp.wait_send()
        for own in started:
            own.wait()

    out_shapes = [jax.ShapeDtypeStruct((N_POS,) + tuple(b.shape[1:] if scatter else b.shape), b.dtype) for b in bufs]
    return pl.pallas_call(
        body, in_specs=[pl.BlockSpec(memory_space=pl.ANY)] * n, out_specs=[pl.BlockSpec(memory_space=pl.ANY)] * n,
        out_shape=out_shapes,
        scratch_shapes=[pltpu.SemaphoreType.DMA((3 * n,)), pltpu.SemaphoreType.DMA((3 * n,)), pltpu.SemaphoreType.DMA((n,))],
        name=name)(*bufs)


def _sibling_exchange(name, bufs):
    n = len(bufs)

    def body(*refs):
        in_refs, out_refs, send_sems, recv_sems = refs[:n], refs[n:2 * n], refs[2 * n], refs[2 * n + 1]
        x, y, c = lax.axis_index("x"), lax.axis_index("y"), lax.axis_index("c")
        copies = [pltpu.make_async_remote_copy(
            src_ref=in_refs[a], dst_ref=out_refs[a], send_sem=send_sems.at[a], recv_sem=recv_sems.at[a],
            device_id=(x, y, 1 - c), device_id_type=pl.DeviceIdType.MESH) for a in range(n)]
        for cp in copies:
            cp.start()
        for cp in copies:
            cp.wait()

    return pl.pallas_call(
        body, in_specs=[pl.BlockSpec(memory_space=pl.ANY)] * n, out_specs=[pl.BlockSpec(memory_space=pl.ANY)] * n,
        out_shape=[jax.ShapeDtypeStruct(b.shape, b.dtype) for b in bufs],
        scratch_shapes=[pltpu.SemaphoreType.DMA((n,)), pltpu.SemaphoreType.DMA((n,))], name=name)(*bufs)


def _sum_slots(name, buf):
    _, R, L = buf.shape
    tr = _row_tile(R, L)

    def body(b_ref, o_ref):
        o_ref[...] = ((b_ref[0] + b_ref[1]) + b_ref[2]) + b_ref[3]

    return pl.pallas_call(
        body, grid=(R // tr,),
        in_specs=[pl.BlockSpec((N_POS, tr, L), lambda i: (0, i, 0))],
        out_specs=pl.BlockSpec((tr, L), lambda i: (i, 0)),
        out_shape=jax.ShapeDtypeStruct((R, L), f32), name=name,
        compiler_params=_params("parallel"))(buf)


def _adamw(name, w, ga, gb, m, v):
    R, L = w.shape
    tr = _row_tile(R, L)
    c1 = 1.0 / (1.0 - ADAM_B1 ** ADAM_STEP)
    c2 = 1.0 / (1.0 - ADAM_B2 ** ADAM_STEP)

    def body(w_ref, ga_ref, gb_ref, m_ref, v_ref, g_out, d_out, m_out, v_out):
        g = ga_ref[...] + gb_ref[...]
        m_new = ADAM_B1 * m_ref[...] + (1.0 - ADAM_B1) * g
        v_new = ADAM_B2 * v_ref[...] + (1.0 - ADAM_B2) * (g * g)
        g_out[...] = g
        m_out[...] = m_new
        v_out[...] = v_new
        d_out[...] = -ADAM_LR * ((m_new * c1) / (jnp.sqrt(v_new * c2) + ADAM_EPS) + ADAM_WD * w_ref[...])

    spec = pl.BlockSpec((tr, L), lambda i: (i, 0))
    return pl.pallas_call(
        body, grid=(R // tr,), in_specs=[spec] * 5, out_specs=[spec] * 4,
        out_shape=[jax.ShapeDtypeStruct((R, L), f32)] * 4, name=name,
        compiler_params=_params("parallel"))(w, ga, gb, m, v)


def _step(x, loss_target, P, M, V):
    shapes = {n: tuple(P[n].shape) for n in WEIGHTS}
    sh_shapes = [shapes[n] for n in SMALL_SHARDED]
    packed = SMALL_SHARDED + SMALL

    gathered = _xy_exchange("gather_weights", [P[n].astype(bf16) for n in BIG] + [_pack128([P[n] for n in SMALL_SHARDED])],
                            scatter=False)
    W = {n: P[n] for n in SMALL}
    for n, g in zip(BIG, gathered):
        W[n] = g.reshape(-1, g.shape[2]) if n in ROW_SHARDED else jnp.concatenate([g[j] for j in range(N_POS)], axis=1)
    per_pos = [_unpack128(gathered[-1][j], sh_shapes) for j in range(N_POS)]
    for q, n in enumerate(SMALL_SHARDED):
        W[n] = jnp.concatenate([per_pos[j][q] for j in range(N_POS)], axis=1)
    W['w_in_pad'] = _pad_w_in(W.pop('w_in'))

    loss_rows, dx, G = _local_step(x, loss_target, W)
    G['w_in'] = _unpad_w_in(G.pop('w_in_pad'))

    def slabs(n):
        r, c = shapes[n]
        if n in ROW_SHARDED:
            return G[n].reshape(N_POS, r, c)
        return G[n].reshape(r, N_POS, c).transpose(1, 0, 2)

    small_slabs = jnp.stack([_pack128([slabs(n)[j] for n in SMALL_SHARDED] + [G[n] for n in SMALL]) for j in range(N_POS)])
    contributions = _xy_exchange("scatter_grads", [slabs(n) for n in BIG] + [small_slabs], scatter=True)
    tags = list(BIG) + ['small']
    plane = [_sum_slots("sum_" + t, cbuf) for t, cbuf in zip(tags, contributions)]
    sibling = _sibling_exchange("sibling_grads", plane)

    out = {}
    names4 = ('grad', 'delta', 'new_m', 'new_v')
    for q, n in enumerate(BIG):
        for tag, t in zip(names4, _adamw("adamw_" + n, P[n], plane[q], sibling[q], M[n], V[n])):
            out[tag + '_' + n] = t
    small_out = _adamw("adamw_small", _pack128([P[n] for n in packed]), plane[-1], sibling[-1],
                       _pack128([M[n] for n in packed]), _pack128([V[n] for n in packed]))
    for tag, buf in zip(names4, small_out):
        for n, t in zip(packed, _unpack128(buf, [shapes[n] for n in packed])):
            out[tag + '_' + n] = t
    loss = lax.psum(loss_rows[0, 0], ("x", "y", "c"))
    return loss, dx, out


def kernel(x, norm1_g, w_in, rwkv_mu, rwkv_w0, rwkv_w2, rwkv_a0, rwkv_a2, rwkv_g2, rwkv_k_k, rwkv_k_a, rwkv_r_k, rwkv_ln_w, rwkv_ln_b, rwkv_proj, gdn_conv_w, gdn_a_log, gdn_dt_bias, gdn_norm_w, gdn_proj, w_out, norm2_g, ffn_up, ffn_conv_w, ffn_down, final_g, loss_target, m_norm1_g, m_w_in, m_rwkv_mu, m_rwkv_w0, m_rwkv_w2, m_rwkv_a0, m_rwkv_a2, m_rwkv_g2, m_rwkv_k_k, m_rwkv_k_a, m_rwkv_r_k, m_rwkv_ln_w, m_rwkv_ln_b, m_rwkv_proj, m_gdn_conv_w, m_gdn_a_log, m_gdn_dt_bias, m_gdn_norm_w, m_gdn_proj, m_w_out, m_norm2_g, m_ffn_up, m_ffn_conv_w, m_ffn_down, m_final_g, v_norm1_g, v_w_in, v_rwkv_mu, v_rwkv_w0, v_rwkv_w2, v_rwkv_a0, v_rwkv_a2, v_rwkv_g2, v_rwkv_k_k, v_rwkv_k_a, v_rwkv_r_k, v_rwkv_ln_w, v_rwkv_ln_b, v_rwkv_proj, v_gdn_conv_w, v_gdn_a_log, v_gdn_dt_bias, v_gdn_norm_w, v_gdn_proj, v_w_out, v_norm2_g, v_ffn_up, v_ffn_conv_w, v_ffn_down, v_final_g):
    weights = (norm1_g, w_in, rwkv_mu, rwkv_w0, rwkv_w2, rwkv_a0, rwkv_a2, rwkv_g2, rwkv_k_k, rwkv_k_a, rwkv_r_k, rwkv_ln_w,
               rwkv_ln_b, rwkv_proj, gdn_conv_w, gdn_a_log, gdn_dt_bias, gdn_norm_w, gdn_proj, w_out, norm2_g, ffn_up,
               ffn_conv_w, ffn_down, final_g)
    m_in = (m_norm1_g, m_w_in, m_rwkv_mu, m_rwkv_w0, m_rwkv_w2, m_rwkv_a0, m_rwkv_a2, m_rwkv_g2, m_rwkv_k_k, m_rwkv_k_a,
            m_rwkv_r_k, m_rwkv_ln_w, m_rwkv_ln_b, m_rwkv_proj, m_gdn_conv_w, m_gdn_a_log, m_gdn_dt_bias, m_gdn_norm_w,
            m_gdn_proj, m_w_out, m_norm2_g, m_ffn_up, m_ffn_conv_w, m_ffn_down, m_final_g)
    v_in = (v_norm1_g, v_w_in, v_rwkv_mu, v_rwkv_w0, v_rwkv_w2, v_rwkv_a0, v_rwkv_a2, v_rwkv_g2, v_rwkv_k_k, v_rwkv_k_a,
            v_rwkv_r_k, v_rwkv_ln_w, v_rwkv_ln_b, v_rwkv_proj, v_gdn_conv_w, v_gdn_a_log, v_gdn_dt_bias, v_gdn_norm_w,
            v_gdn_proj, v_w_out, v_norm2_g, v_ffn_up, v_ffn_conv_w, v_ffn_down, v_final_g)
    drop = lambda n, a: a if n == 'final_g' else a[0]
    P = {n: drop(n, a) for n, a in zip(WEIGHTS, weights)}
    M = {n: drop(n, a) for n, a in zip(WEIGHTS, m_in)}
    V = {n: drop(n, a) for n, a in zip(WEIGHTS, v_in)}
    loss, dx, out = _step(x[0], loss_target[0], P, M, V)
    lift = lambda n, a: a if n == 'final_g' else a[None]
    res = [loss, dx[None]]
    for tag in ('grad', 'delta', 'new_m', 'new_v'):
        res += [lift(n, out[tag + '_' + n]) for n in WEIGHTS]
    return tuple(res)
```

```python
import jax
import jax.numpy as jnp
from jax import lax
from jax.experimental import pallas as pl
from jax.experimental.pallas import tpu as pltpu

f32 = jnp.float32
bf16 = jnp.bfloat16
HI = lax.Precision.HIGHEST

D_MODEL = 1024
RWKV_HEADS, RWKV_HD, RWKV_W = 8, 64, 512
GDN_HEADS, GDN_HD, GDN_W = 4, 128, 512
FFN_H = 2816
NORM_EPS, L2_EPS, GN_EPS = 1e-6, 1e-6, 64e-5
W_AB = 256
OFF_QKV, OFF_Z, OFF_GATES, OFF_AB = 1792, 3328, 3840, 5888
W_IN_PAD = OFF_AB + W_AB
WKV_CHUNK = 64
GDN_CHUNK = 128
HALO = 8
LANES = 128
TILE_BYTES = 1 << 20
VMEM_LIMIT = 56 * 1024 * 1024

ADAM_LR, ADAM_B1, ADAM_B2, ADAM_EPS, ADAM_WD, ADAM_STEP = 0.001, 0.9, 0.999, 1e-08, 0.01, 10

ROW_SHARDED = ('w_out', 'ffn_down')
SMALL = ('norm1_g', 'rwkv_mu', 'rwkv_w0', 'rwkv_a0', 'rwkv_k_k', 'rwkv_k_a', 'rwkv_r_k', 'rwkv_ln_w', 'rwkv_ln_b',
         'gdn_a_log', 'gdn_dt_bias', 'gdn_norm_w', 'norm2_g', 'final_g')
WEIGHTS = ('norm1_g', 'w_in', 'rwkv_mu', 'rwkv_w0', 'rwkv_w2', 'rwkv_a0', 'rwkv_a2', 'rwkv_g2', 'rwkv_k_k', 'rwkv_k_a',
           'rwkv_r_k', 'rwkv_ln_w', 'rwkv_ln_b', 'rwkv_proj', 'gdn_conv_w', 'gdn_a_log', 'gdn_dt_bias', 'gdn_norm_w',
           'gdn_proj', 'w_out', 'norm2_g', 'ffn_up', 'ffn_conv_w', 'ffn_down', 'final_g')


def _params(*sem):
    return pltpu.CompilerParams(dimension_semantics=sem, vmem_limit_bytes=VMEM_LIMIT)


def _tile(n, limit):
    if n <= limit:
        return n
    best = None
    for d in range(128, limit + 1, 128):
        if n % d == 0:
            best = d
    if best is None:
        raise ValueError(f"no tile for {n} under {limit}")
    return best


MM_BLOCK_BYTES = 4 << 20


def _mm(a, b, mode, name, add=None, out_dtype=f32):
    if mode == 'nn':
        (M, K), N = a.shape, b.shape[1]
    elif mode == 'nt':
        (M, K), N = a.shape, b.shape[0]
    else:
        (K, M), N = a.shape, b.shape[1]
    tm = _tile(M, 512)
    tk = _tile(K, min(2816, MM_BLOCK_BYTES // (tm * a.dtype.itemsize)))
    tn = _tile(N, max(128, min(MM_BLOCK_BYTES // (tk * b.dtype.itemsize), MM_BLOCK_BYTES // (tm * 4)) // 128 * 128))
    nk = K // tk
    dn = {'nn': (((1,), (0,)), ((), ())), 'nt': (((1,), (1,)), ((), ())), 'tn': (((0,), (0,)), ((), ()))}[mode]

    def body(a_ref, b_ref, *rest):
        o_ref = rest[1] if add is not None else rest[0]
        acc = lax.dot_general(a_ref[...].astype(bf16), b_ref[...].astype(bf16), dn, preferred_element_type=f32)
        if nk == 1:
            o_ref[...] = (acc + rest[0][...] if add is not None else acc).astype(out_dtype)
            return
        acc_ref = rest[-1]
        k = pl.program_id(2)

        @pl.when(k == 0)
        def _():
            acc_ref[...] = acc + rest[0][...] if add is not None else acc

        @pl.when(k > 0)
        def _():
            acc_ref[...] += acc

        @pl.when(k == nk - 1)
        def _():
            o_ref[...] = acc_ref[...].astype(out_dtype)

    a_spec = (pl.BlockSpec((tk, tm), lambda i, j, k: (k, i)) if mode == 'tn'
              else pl.BlockSpec((tm, tk), lambda i, j, k: (i, k)))
    b_spec = (pl.BlockSpec((tn, tk), lambda i, j, k: (j, k)) if mode == 'nt'
              else pl.BlockSpec((tk, tn), lambda i, j, k: (k, j)))
    o_spec = pl.BlockSpec((tm, tn), lambda i, j, k: (i, j))
    ins, specs = [a, b], [a_spec, b_spec]
    if add is not None:
        ins.append(add)
        specs.append(o_spec)
    return pl.pallas_call(
        body, grid=(M // tm, N // tn, nk), in_specs=specs, out_specs=o_spec,
        out_shape=jax.ShapeDtypeStruct((M, N), out_dtype),
        scratch_shapes=[pltpu.VMEM((tm, tn), f32)] if nk > 1 else [], name=name,
        compiler_params=_params("parallel", "parallel", "arbitrary"))(*ins)


def _shift_down(cur, prev, s):
    if s == 0:
        return cur
    ext = jnp.concatenate([prev, cur], axis=0)
    return pltpu.roll(ext, s, 0)[HALO:]


def _shift_up(cur, nxt, s):
    if s == 0:
        return cur
    ext = jnp.concatenate([cur, nxt], axis=0)
    return pltpu.roll(ext, ext.shape[0] - s, 0)[:cur.shape[0]]


def _conv_apply(cur, prev, w_ref):
    taps = w_ref.shape[0]
    out = None
    for i in range(taps):
        term = _shift_down(cur, prev, taps - 1 - i) * w_ref[pl.ds(i, 1), :]
        out = term if out is None else out + term
    return out


def _row_spec(tm, w):
    return pl.BlockSpec((tm, w), lambda i: (i, 0))


def _prev_spec(tm, w):
    return pl.BlockSpec((HALO, w), lambda i: (jnp.maximum(i * (tm // HALO) - 1, 0), 0))


def _next_spec(tm, w, T):
    return pl.BlockSpec((HALO, w), lambda i: (jnp.minimum((i + 1) * (tm // HALO), T // HALO - 1), 0))


def _full_spec(shape):
    return pl.BlockSpec(shape, lambda i: (0,) * len(shape))


def _pw_fwd(name, fn, rows, consts, out_widths, tm, conv_w=None, out_dtype=f32):
    T = rows[0].shape[0]
    nr, nc = len(rows), len(consts)

    def body(*refs):
        i = pl.program_id(0)
        vals = [r[...] for r in refs[:nr]]
        p = nr
        if conv_w is not None:
            prev = jnp.where(i > 0, refs[p][...], 0.0)
            vals[0] = _conv_apply(vals[0], prev, refs[p + 1])
            p += 2
        cvals = [r[...] for r in refs[p:p + nc]]
        outs = fn(*vals, *cvals)
        for o_ref, o in zip(refs[p + nc:], outs):
            o_ref[...] = o.astype(out_dtype)

    ins = list(rows)
    specs = [_row_spec(tm, r.shape[1]) for r in rows]
    if conv_w is not None:
        ins += [rows[0], conv_w]
        specs += [_prev_spec(tm, rows[0].shape[1]), _full_spec(conv_w.shape)]
    ins += list(consts)
    specs += [_full_spec(c.shape) for c in consts]
    outs = pl.pallas_call(
        body, grid=(T // tm,), in_specs=specs,
        out_specs=[_row_spec(tm, w) for w in out_widths],
        out_shape=[jax.ShapeDtypeStruct((T, w), out_dtype) for w in out_widths], name=name,
        compiler_params=_params("parallel"))(*ins)
    return outs


def _pw_bwd(name, fn, rows, consts, cots, tm, conv_w=None, add_to_first=None, row_dtypes=None):
    T = rows[0].shape[0]
    nr, nc = len(rows), len(consts)
    flat_cots = [c for grp in cots for c in grp]
    row_dtypes = row_dtypes or [f32] * nr

    def body(*refs):
        i = pl.program_id(0)
        vals = [r[...] for r in refs[:nr]]
        p = nr
        if conv_w is not None:
            prev = jnp.where(i > 0, refs[p][...], 0.0)
            vals[0] = _conv_apply(vals[0], prev, refs[p + 1])
            p += 2
        cvals = [r[...] for r in refs[p:p + nc]]
        p += nc
        cot_vals = []
        for grp in cots:
            acc = refs[p][...]
            for q in range(1, len(grp)):
                acc = acc + refs[p + q][...]
            p += len(grp)
            cot_vals.append(acc)
        extra = None
        if add_to_first is not None:
            extra = refs[p][...]
            p += 1
        _, vjp = jax.vjp(fn, *vals, *cvals)
        grads = vjp(tuple(cot_vals))
        row_out = refs[p:p + nr]
        const_out = refs[p + nr:]
        for q in range(nr):
            g = grads[q]
            if q == 0 and extra is not None:
                g = g + extra
            row_out[q][...] = g.astype(row_dtypes[q])

        @pl.when(i == 0)
        def _():
            for q in range(nc):
                const_out[q][...] = grads[nr + q]

        @pl.when(i > 0)
        def _():
            for q in range(nc):
                const_out[q][...] += grads[nr + q]

    ins = list(rows)
    specs = [_row_spec(tm, r.shape[1]) for r in rows]
    if conv_w is not None:
        ins += [rows[0], conv_w]
        specs += [_prev_spec(tm, rows[0].shape[1]), _full_spec(conv_w.shape)]
    ins += list(consts)
    specs += [_full_spec(c.shape) for c in consts]
    ins += flat_cots
    specs += [_row_spec(tm, c.shape[1]) for c in flat_cots]
    if add_to_first is not None:
        ins.append(add_to_first)
        specs.append(_row_spec(tm, add_to_first.shape[1]))
    out_shapes = ([jax.ShapeDtypeStruct(r.shape, d) for r, d in zip(rows, row_dtypes)]
                  + [jax.ShapeDtypeStruct(c.shape, f32) for c in consts])
    out_specs = [_row_spec(tm, r.shape[1]) for r in rows] + [_full_spec(c.shape) for c in consts]
    outs = pl.pallas_call(
        body, grid=(T // tm,), in_specs=specs, out_specs=out_specs, out_shape=out_shapes, name=name,
        compiler_params=_params("arbitrary"))(*ins)
    return list(outs[:nr]), list(outs[nr:])


def _conv_bwd(name, dc, x, w, tm, dx_dtype=f32):
    T, W = x.shape
    taps = w.shape[0]
    nblk = T // tm

    def body(dc_ref, dcn_ref, x_ref, xp_ref, w_ref, dx_ref, dw_ref):
        i = pl.program_id(0)
        dcv, xv = dc_ref[...], x_ref[...]
        nxt = jnp.where(i < nblk - 1, dcn_ref[...], 0.0)
        prev = jnp.where(i > 0, xp_ref[...], 0.0)

        @pl.when(i == 0)
        def _():
            dw_ref[...] = jnp.zeros_like(dw_ref)

        dx = None
        for k in range(taps):
            s = taps - 1 - k
            term = _shift_up(dcv, nxt, s) * w_ref[pl.ds(k, 1), :]
            dx = term if dx is None else dx + term
            dw_ref[pl.ds(k, 1), :] += jnp.sum(dcv * _shift_down(xv, prev, s), axis=0, keepdims=True)
        dx_ref[...] = dx.astype(dx_dtype)

    return pl.pallas_call(
        body, grid=(nblk,),
        in_specs=[_row_spec(tm, W), _next_spec(tm, W, T), _row_spec(tm, W), _prev_spec(tm, W), _full_spec(w.shape)],
        out_specs=[_row_spec(tm, W), _full_spec(w.shape)],
        out_shape=[jax.ShapeDtypeStruct((T, W), dx_dtype), jax.ShapeDtypeStruct(w.shape, f32)], name=name,
        compiler_params=_params("arbitrary"))(dc, dc, x, x, w)


def _sigmoid(x):
    return 1.0 / (1.0 + jnp.exp(-x))


def _softplus(x):
    return jnp.maximum(x, 0.0) + jnp.log(1.0 + jnp.exp(jnp.minimum(x, -x)))


def _seg_sum(x, seg):
    w = x.shape[-1]
    r = lax.broadcasted_iota(jnp.int32, (w, w), 0) // seg
    c = lax.broadcasted_iota(jnp.int32, (w, w), 1) // seg
    return jnp.dot(x, (r == c).astype(f32), precision=HI, preferred_element_type=f32)


def _rms(x, g):
    return x * lax.rsqrt(jnp.mean(x * x, axis=-1, keepdims=True) + NORM_EPS) * g


def _rms_fn(x, g):
    return (_rms(x, g),)


def _loss_rows(x2, tgt, g):
    e = _rms(x2, g) - tgt
    return 0.5 * jnp.sum(e * e, axis=-1, keepdims=True) * (1.0 / D_MODEL)


def _rwkv_prep_fn(ps, w0, w2p, a0, a2p, g2, k_k, k_a):
    r, k, v = ps[:, 0:512], ps[:, 512:1024], ps[:, 1024:1536]
    wa, gl = ps[:, 1536:1664], ps[:, 1664:1792]
    z = w0 + jnp.dot(jnp.tanh(wa), w2p, precision=HI, preferred_element_type=f32)
    w_log = -_softplus(-z) - 0.5
    lw = -jnp.exp(w_log)
    a = _sigmoid(a0 + jnp.dot(wa, a2p, precision=HI, preferred_element_type=f32))
    g = jnp.dot(_sigmoid(gl), g2, precision=HI, preferred_element_type=f32)
    kx = k * k_k
    kk = kx * lax.rsqrt(_seg_sum(kx * kx, RWKV_HD) + L2_EPS)
    k2 = k * (1.0 + (a - 1.0) * k_a)
    return r, lw, k2, v, -kk, kk * a, g


def _rwkv_post_fn(y, r, k2, v, g, ln_w, ln_b, rk):
    mean = _seg_sum(y, RWKV_HD) * (1.0 / RWKV_HD)
    yc = y - mean
    var = _seg_sum(yc * yc, RWKV_HD) * (1.0 / RWKV_HD)
    yn = yc * lax.rsqrt(var + GN_EPS) * ln_w + ln_b
    bonus = _seg_sum(r * k2 * rk, RWKV_HD) * v
    return ((yn + bonus) * g,)


def _gdn_prep_fn(c, ab, al_p, dt_p):
    s = c * _sigmoid(c)
    q, k, v = s[:, 0:512], s[:, 512:1024], s[:, 1024:1536]
    q = q * lax.rsqrt(_seg_sum(q * q, GDN_HD) + L2_EPS) * (GDN_HD ** -0.5)
    k = k * lax.rsqrt(_seg_sum(k * k, GDN_HD) + L2_EPS)
    lane = lax.broadcasted_iota(jnp.int32, ab.shape, 1)
    gpart = -jnp.exp(al_p) * _softplus(ab + dt_p)
    gbeta = jnp.where(lane < GDN_HEADS, gpart, jnp.where(lane < 2 * GDN_HEADS, _sigmoid(ab), 0.0))
    return q, k, v, gbeta


def _gdn_post_fn(o, z, nw):
    ms = _seg_sum(o * o, GDN_HD) * (1.0 / GDN_HD)
    return (o * lax.rsqrt(ms + NORM_EPS) * nw * (z * _sigmoid(z)),)


def _mix_fn(gates, ya, yb):
    return (_sigmoid(gates[:, :D_MODEL]) * ya + _sigmoid(gates[:, D_MODEL:]) * yb,)


def _ffn_fn(c):
    hg, hu = c[:, :FFN_H], c[:, FFN_H:]
    return (hg * _sigmoid(hg) * hu,)


N_POS = 4


def _xy_out_shapes(bufs, scatter):
    return [jax.ShapeDtypeStruct((N_POS,) + tuple(b.shape[1:] if scatter else b.shape), b.dtype) for b in bufs]


def _xy_sems(n):
    return [pltpu.SemaphoreType.DMA((3 * n,)), pltpu.SemaphoreType.DMA((3 * n,)), pltpu.SemaphoreType.DMA((n,))]


def _xy_copies(in_refs, out_refs, send_sems, recv_sems, local_sems, scatter):
    n = len(in_refs)

    def plan(arriving):
        x, y, c = lax.axis_index("x"), lax.axis_index("y"), lax.axis_index("c")
        me = 2 * x + y
        peers = [(1 - x, y), (x, 1 - y), (1 - x, 1 - y)]

        def copy(a, k, src, slot, peer):
            return pltpu.make_async_remote_copy(
                src_ref=src, dst_ref=out_refs[a].at[slot], send_sem=send_sems.at[3 * a + k],
                recv_sem=recv_sems.at[3 * a + k], device_id=(peer[0], peer[1], c), device_id_type=pl.DeviceIdType.MESH)

        if arriving:
            return [copy(a, k, in_refs[a].at[me] if scatter else in_refs[a], 2 * peer[0] + peer[1], peer)
                    for a in range(n) for k, peer in enumerate(peers)]
        own = [pltpu.make_async_copy(in_refs[a].at[me] if scatter else in_refs[a], out_refs[a].at[me], local_sems.at[a])
               for a in range(n)]
        sends = [copy(a, k, in_refs[a].at[2 * peer[0] + peer[1]] if scatter else in_refs[a], me, peer)
                 for a in range(n) for k, peer in enumerate(peers)]
        return own + sends

    def start():
        for cp in plan(False):
            cp.start()

    def finish():
        for cp in plan(True):
            cp.wait_recv()
        outgoing = plan(False)
        for cp in outgoing[n:]:
            cp.wait_send()
        for cp in outgoing[:n]:
            cp.wait()

    return start, finish


SCAN_PREC = lax.Precision.HIGH


def _bmm(a, b):
    return jnp.einsum('hcs,hsd->hcd', a, b, precision=SCAN_PREC, preferred_element_type=f32)


def _bmm_nt(a, b):
    return jnp.einsum('hcd,hsd->hcs', a, b, precision=SCAN_PREC, preferred_element_type=f32)


def _bmm_tn(a, b):
    return jnp.einsum('hcd,hce->hde', a, b, precision=SCAN_PREC, preferred_element_type=f32)


def _masks(H, C):
    row = lax.broadcasted_iota(jnp.int32, (H, C, C), 1)
    col = lax.broadcasted_iota(jnp.int32, (H, C, C), 2)
    return row, col


def _tri_inv_impl(L):
    H, C, _ = L.shape
    row, col = _masks(H, C)
    eye = (row == col).astype(f32)
    base = 16
    same = (row // base) == (col // base)
    Ld = jnp.where(same, L, 0.0)
    X = -Ld
    inv = eye + X
    for _ in range(3):
        X = _bmm(X, X)
        inv = _bmm(inv, eye + X)
    if C == base:
        return inv
    N = _bmm(inv, L - Ld)
    out = eye - N
    levels = C // base
    P = N
    span = 2
    while span < levels:
        P = _bmm(P, P)
        out = _bmm(out, eye + P)
        span *= 2
    return _bmm(out, inv)


@jax.custom_vjp
def _tri_inv(L):
    return _tri_inv_impl(L)


def _tri_inv_fwd(L):
    T = _tri_inv_impl(L)
    return T, T


def _tri_inv_bwd(T, dT):
    return (-_bmm_nt(_bmm_tn(T, dT), T),)


_tri_inv.defvjp(_tri_inv_fwd, _tri_inv_bwd)


def _cumsum_impl(x, reverse):
    C = x.shape[1]
    row = lax.broadcasted_iota(jnp.int32, x.shape, 1)
    s = 1
    while s < C:
        if reverse:
            x = x + jnp.where(row < C - s, pltpu.roll(x, C - s, 1), 0.0)
        else:
            x = x + jnp.where(row >= s, pltpu.roll(x, s, 1), 0.0)
        s *= 2
    return x


@jax.custom_vjp
def _cumsum(x):
    return _cumsum_impl(x, False)


_cumsum.defvjp(lambda x: (_cumsum_impl(x, False), None), lambda _, g: (_cumsum_impl(g, True),))


def _wkv_chunk(Z, r, lw, k, v, a, b):
    H, C, D = r.shape
    row, col = _masks(H, C)
    incl, strict = row >= col, row > col
    cw = _cumsum(lw)
    cwp = cw - lw
    cwl = jnp.sum(lw, axis=1, keepdims=True)
    en = jnp.exp(-cw)
    at, rt, bt, kt = a * jnp.exp(cwp), r * jnp.exp(cw), b * en, k * en
    Lab = jnp.where(strict, _bmm_nt(at, bt), 0.0)
    Lak = jnp.where(strict, _bmm_nt(at, kt), 0.0)
    Tm = _tri_inv(-Lab)
    U = _bmm(Tm, _bmm(at, Z) + _bmm(Lak, v))
    Rb = jnp.where(incl, _bmm_nt(rt, bt), 0.0)
    Rk = jnp.where(incl, _bmm_nt(rt, kt), 0.0)
    y = _bmm(rt, Z) + _bmm(Rb, U) + _bmm(Rk, v)
    ed = jnp.exp(cwl - cw)
    zdec = jnp.swapaxes(jnp.broadcast_to(jnp.exp(cwl), (H, Z.shape[2], D)), 1, 2)
    Z1 = Z * zdec + _bmm_tn(b * ed, U) + _bmm_tn(k * ed, v)
    return y, Z1


def _gdn_chunk(S, q, k, v, g, beta):
    H, C, D = q.shape
    row, col = _masks(H, C)
    incl, strict = row >= col, row > col
    gc = _cumsum(g)
    diff = gc - jnp.swapaxes(gc, 1, 2)
    decay = jnp.where(incl, jnp.exp(jnp.where(incl, diff, 0.0)), 0.0)
    gl = jnp.sum(g, axis=1, keepdims=True)
    kb, vb = k * beta, v * beta
    L = jnp.where(strict, _bmm_nt(kb, k) * decay, 0.0)
    Tm = _tri_inv(L)
    egc = jnp.exp(gc)
    u = _bmm(Tm, vb)
    wk = _bmm(Tm, kb * egc)
    attn = jnp.where(incl, _bmm_nt(q, k) * decay, 0.0)
    v_new = u - _bmm(wk, S)
    o = _bmm(q * egc, S) + _bmm(attn, v_new)
    S1 = S * jnp.exp(gl) + _bmm_tn(k * jnp.exp(gl - gc), v_new)
    return o, S1


def _wkv_block(Z, r, lw, k, v, a, b):
    lane = lax.broadcasted_iota(jnp.int32, (r.shape[0], 128), 1)
    low = lane < RWKV_HD

    def heads(t):
        out = []
        for p in range(RWKV_HEADS // 2):
            pair = t[:, 128 * p:128 * (p + 1)]
            out += [jnp.where(low, pair, 0.0), jnp.where(low, 0.0, pair)]
        return jnp.concatenate([t[None] for t in out], axis=0)

    y, Z1 = _wkv_chunk(Z, *[heads(t) for t in (r, lw, k, v, a, b)])
    return jnp.concatenate([y[2 * p] + y[2 * p + 1] for p in range(RWKV_HEADS // 2)], axis=1), Z1


def _gdn_block(S, q, k, v, gbeta):
    heads = lambda t: jnp.concatenate([t[None, :, GDN_HD * h:GDN_HD * (h + 1)] for h in range(GDN_HEADS)], axis=0)
    src = lax.broadcasted_iota(jnp.int32, (W_AB, 2 * GDN_W), 0)
    dst = lax.broadcasted_iota(jnp.int32, (W_AB, 2 * GDN_W), 1) // GDN_HD
    spread = jnp.dot(gbeta, (src == dst).astype(f32), precision=HI, preferred_element_type=f32)
    o, S1 = _gdn_chunk(S, heads(q), heads(k), heads(v), heads(spread[:, :GDN_W]), heads(spread[:, GDN_W:]))
    return jnp.concatenate([o[h] for h in range(GDN_HEADS)], axis=1), S1


def _scan_fwd(name, block_fn, ins, C, H, dh, w_out, side=None):
    T = ins[0].shape[0]
    n_in = len(ins)
    nblk = T // C
    n_side = 0 if side is None else len(side[0])

    def body(*refs):
        in_refs, refs = refs[:n_in], refs[n_in:]
        side_in, refs = refs[:n_side], refs[n_side:]
        y_ref, zs_ref, refs = refs[0], refs[1], refs[2:]
        side_out, refs = refs[:n_side], refs[n_side:]
        z_scr = refs[0]
        if side is not None:
            start, finish = _xy_copies(side_in, side_out, refs[1], refs[2], refs[3], side[1])
            pl.when(pl.program_id(0) == 0)(start)

        @pl.when(pl.program_id(0) == 0)
        def _():
            z_scr[...] = jnp.zeros_like(z_scr)

        Z = z_scr[...]
        zs_ref[0] = Z
        y, Z1 = block_fn(Z, *[r[...] for r in in_refs])
        y_ref[...] = y
        z_scr[...] = Z1
        if side is not None:
            pl.when(pl.program_id(0) == nblk - 1)(finish)

    side_bufs = [] if side is None else list(side[0])
    any_spec = pl.BlockSpec(memory_space=pl.ANY)
    return pl.pallas_call(
        body, grid=(nblk,),
        in_specs=[pl.BlockSpec((C, a.shape[1]), lambda i: (i, 0)) for a in ins] + [any_spec] * n_side,
        out_specs=[pl.BlockSpec((C, w_out), lambda i: (i, 0)), pl.BlockSpec((1, H, dh, dh), lambda i: (i, 0, 0, 0))]
        + [any_spec] * n_side,
        out_shape=[jax.ShapeDtypeStruct((T, w_out), f32), jax.ShapeDtypeStruct((T // C, H, dh, dh), f32)]
        + (_xy_out_shapes(side_bufs, side[1]) if side is not None else []),
        scratch_shapes=[pltpu.VMEM((H, dh, dh), f32)] + (_xy_sems(n_side) if side is not None else []), name=name,
        compiler_params=_params("arbitrary"))(*ins, *side_bufs)


def _scan_bwd(name, block_fn, ins, dy, zs, C, side=None):
    T = ins[0].shape[0]
    _, H, dh, _ = zs.shape
    n_in = len(ins)
    nblk = T // C
    n_side = 0 if side is None else len(side[0])

    def body(*refs):
        in_refs, dy_ref, zs_ref, refs = refs[:n_in], refs[n_in], refs[n_in + 1], refs[n_in + 2:]
        side_in, refs = refs[:n_side], refs[n_side:]
        out_refs, refs = refs[:n_in], refs[n_in:]
        side_out, refs = refs[:n_side], refs[n_side:]
        dz_scr = refs[0]
        if side is not None:
            start, finish = _xy_copies(side_in, side_out, refs[1], refs[2], refs[3], side[1])
            pl.when(pl.program_id(0) == 0)(start)

        @pl.when(pl.program_id(0) == 0)
        def _():
            dz_scr[...] = jnp.zeros_like(dz_scr)

        _, vjp = jax.vjp(block_fn, zs_ref[0], *[r[...] for r in in_refs])
        grads = vjp((dy_ref[...], dz_scr[...]))
        dz_scr[...] = grads[0]
        for o_ref, gval in zip(out_refs, grads[1:]):
            o_ref[...] = gval
        if side is not None:
            pl.when(pl.program_id(0) == nblk - 1)(finish)

    side_bufs = [] if side is None else list(side[0])
    any_spec = pl.BlockSpec(memory_space=pl.ANY)
    rev = lambda i: (nblk - 1 - i, 0)
    return pl.pallas_call(
        body, grid=(nblk,),
        in_specs=[pl.BlockSpec((C, a.shape[1]), rev) for a in ins]
        + [pl.BlockSpec((C, dy.shape[1]), rev), pl.BlockSpec((1, H, dh, dh), lambda i: (nblk - 1 - i, 0, 0, 0))]
        + [any_spec] * n_side,
        out_specs=[pl.BlockSpec((C, a.shape[1]), rev) for a in ins] + [any_spec] * n_side,
        out_shape=[jax.ShapeDtypeStruct(a.shape, f32) for a in ins]
        + (_xy_out_shapes(side_bufs, side[1]) if side is not None else []),
        scratch_shapes=[pltpu.VMEM((H, dh, dh), f32)] + (_xy_sems(n_side) if side is not None else []), name=name,
        compiler_params=_params("arbitrary"))(*ins, dy, zs, *side_bufs)


def _loss_call(x2, tgt, g, tm):
    T, W = x2.shape

    def body(x_ref, t_ref, g_ref, dx_ref, dg_ref, l_ref):
        i = pl.program_id(0)
        tv = t_ref[...]
        l, vjp = jax.vjp(lambda xv, gv: _loss_rows(xv, tv, gv), x_ref[...], g_ref[...])
        dx, dg = vjp(jnp.ones_like(l))
        dx_ref[...] = dx
        tot = jnp.zeros((1, 128), f32) + jnp.sum(l)

        @pl.when(i == 0)
        def _():
            dg_ref[...] = dg
            l_ref[...] = tot

        @pl.when(i > 0)
        def _():
            dg_ref[...] += dg
            l_ref[...] += tot

    return pl.pallas_call(
        body, grid=(T // tm,),
        in_specs=[_row_spec(tm, W), _row_spec(tm, W), _full_spec(g.shape)],
        out_specs=[_row_spec(tm, W), _full_spec(g.shape), _full_spec((1, 128))],
        out_shape=[jax.ShapeDtypeStruct((T, W), f32), jax.ShapeDtypeStruct(g.shape, f32),
                   jax.ShapeDtypeStruct((1, 128), f32)], name="loss_head",
        compiler_params=_params("arbitrary"))(x2, tgt, g)


def _local_step(x, tgt, W, late=None):
    row = lambda a: a.reshape(1, -1)
    wp = W['w_in_pad']
    w_rwkv, w_qkv, w_z = wp[:, :OFF_QKV], wp[:, OFF_QKV:OFF_Z], wp[:, OFF_Z:OFF_GATES]
    w_gates, w_ab = wp[:, OFF_GATES:OFF_AB], wp[:, OFF_AB:]
    mu = row(W['rwkv_mu'])
    mixw = jnp.concatenate([mu, 1.0 - mu], axis=0)
    zpad = jnp.zeros((64, RWKV_W), f32)
    w2p = jnp.concatenate([W['rwkv_w2'], zpad], axis=0)
    a2p = jnp.concatenate([zpad, W['rwkv_a2']], axis=0)
    rw_consts = [row(W['rwkv_w0']), w2p, row(W['rwkv_a0']), a2p, W['rwkv_g2'], row(W['rwkv_k_k']), row(W['rwkv_k_a'])]
    post_consts = [row(W['rwkv_ln_w']), row(W['rwkv_ln_b']), row(W['rwkv_r_k'])]
    pad4 = lambda a: jnp.pad(row(a), ((0, 0), (0, W_AB - GDN_HEADS)))
    gd_consts = [pad4(W['gdn_a_log']), pad4(W['gdn_dt_bias'])]
    nw_t = jnp.tile(row(W['gdn_norm_w']), (1, GDN_HEADS))
    g1, g2n, gf = row(W['norm1_g']), row(W['norm2_g']), row(W['final_g'])

    (u,) = _pw_fwd("norm1", _rms_fn, [x], [g1], [D_MODEL], 256, out_dtype=bf16)
    p_rwkv = _mm(u, w_rwkv, 'nn', "in_rwkv")
    qkv_raw = _mm(u, w_qkv, 'nn', "in_qkv")
    z = _mm(u, w_z, 'nn', "in_z")
    gates = _mm(u, w_gates, 'nn', "in_gates")
    ab = _mm(u, w_ab, 'nn', "in_ab")

    r, lw, k2, v, a_, b_, g = _pw_fwd("rwkv_prep", _rwkv_prep_fn, [p_rwkv], rw_consts, [RWKV_W] * 7, 256, conv_w=mixw)
    wkv_in = [r, lw, k2, v, a_, b_]
    y, zs_wkv, *gathered = _scan_fwd("wkv_fwd", _wkv_block, wkv_in, WKV_CHUNK, RWKV_HEADS, 2 * RWKV_HD, RWKV_W,
                                     side=None if late is None else (late['shards'], False))
    if late is not None:
        W = dict(W, **late['assemble'](gathered))
    (ya_in,) = _pw_fwd("rwkv_post", _rwkv_post_fn, [y, r, k2, v, g], post_consts, [RWKV_W], 256, out_dtype=bf16)
    ya = _mm(ya_in, W['rwkv_proj'], 'nn', "rwkv_proj")

    gq, gk, gv, gbeta = _pw_fwd("gdn_prep", _gdn_prep_fn, [qkv_raw, ab], gd_consts, [GDN_W] * 3 + [W_AB], 256,
                                conv_w=W['gdn_conv_w'])
    gdn_in = [gq, gk, gv, gbeta]
    o, zs_gdn = _scan_fwd("gdn_fwd", _gdn_block, gdn_in, GDN_CHUNK, GDN_HEADS, GDN_HD, GDN_W)
    (yb_in,) = _pw_fwd("gdn_post", _gdn_post_fn, [o, z], [nw_t], [GDN_W], 256, out_dtype=bf16)
    yb = _mm(yb_in, W['gdn_proj'], 'nn', "gdn_proj")

    (mixed,) = _pw_fwd("mix", _mix_fn, [gates, ya, yb], [], [D_MODEL], 256, out_dtype=bf16)
    x1 = _mm(mixed, W['w_out'], 'nn', "w_out", add=x)
    (u2,) = _pw_fwd("norm2", _rms_fn, [x1], [g2n], [D_MODEL], 256, out_dtype=bf16)
    h = _mm(u2, W['ffn_up'], 'nn', "ffn_up")
    (act,) = _pw_fwd("ffn_act", _ffn_fn, [h], [], [FFN_H], 128, conv_w=W['ffn_conv_w'], out_dtype=bf16)
    x2 = _mm(act, W['ffn_down'], 'nn', "ffn_down", add=x1)

    G = {}
    dx2, dgf, loss = _loss_call(x2, tgt, gf, 256)
    G['final_g'] = dgf
    dact = _mm(dx2, W['ffn_down'], 'nt', "d_act")
    G['ffn_down'] = _mm(act, dx2, 'tn', "g_ffn_down")
    (dc_ffn,), _ = _pw_bwd("ffn_act_bwd", _ffn_fn, [h], [], [(dact,)], 128, conv_w=W['ffn_conv_w'])
    dh, G['ffn_conv_w'] = _conv_bwd("ffn_conv_bwd", dc_ffn, h, W['ffn_conv_w'], 128, dx_dtype=bf16)
    du2 = _mm(dh, W['ffn_up'], 'nt', "d_u2")
    G['ffn_up'] = _mm(u2, dh, 'tn', "g_ffn_up")
    (dx1,), (G['norm2_g'],) = _pw_bwd("norm2_bwd", _rms_fn, [x1], [g2n], [(du2,)], 256, add_to_first=dx2)
    dmixed = _mm(dx1, W['w_out'], 'nt', "d_mixed")
    G['w_out'] = _mm(mixed, dx1, 'tn', "g_w_out")
    (dgates, dya, dyb), _ = _pw_bwd("mix_bwd", _mix_fn, [gates, ya, yb], [], [(dmixed,)], 256, row_dtypes=[bf16] * 3)
    dya_in = _mm(dya, W['rwkv_proj'], 'nt', "d_ya_in")
    G['rwkv_proj'] = _mm(ya_in, dya, 'tn', "g_rwkv_proj")
    dyb_in = _mm(dyb, W['gdn_proj'], 'nt', "d_yb_in")
    G['gdn_proj'] = _mm(yb_in, dyb, 'tn', "g_gdn_proj")

    (do, dz), (dnw_t,) = _pw_bwd("gdn_post_bwd", _gdn_post_fn, [o, z], [nw_t], [(dyb_in,)], 256, row_dtypes=[f32, bf16])
    G['gdn_norm_w'] = dnw_t.reshape(GDN_HEADS, GDN_HD).sum(axis=0)
    dgq, dgk, dgv, dgbeta = _scan_bwd("gdn_bwd", _gdn_block, gdn_in, do, zs_gdn, GDN_CHUNK)
    (dc_qkv, dab), (dal_p, ddt_p) = _pw_bwd(
        "gdn_prep_bwd", _gdn_prep_fn, [qkv_raw, ab], gd_consts, [(dgq,), (dgk,), (dgv,), (dgbeta,)], 256,
        conv_w=W['gdn_conv_w'], row_dtypes=[f32, bf16])
    G['gdn_a_log'], G['gdn_dt_bias'] = dal_p[0, :GDN_HEADS], ddt_p[0, :GDN_HEADS]
    dqkv_raw, G['gdn_conv_w'] = _conv_bwd("gdn_conv_bwd", dc_qkv, qkv_raw, W['gdn_conv_w'], 256, dx_dtype=bf16)

    (dy, dr1, dk21, dv1, dg_), (G['rwkv_ln_w'], G['rwkv_ln_b'], G['rwkv_r_k']) = _pw_bwd(
        "rwkv_post_bwd", _rwkv_post_fn, [y, r, k2, v, g], post_consts, [(dya_in,)], 256)
    dr2, dlw, dk22, dv2, da_, db_, *G['_arrived'] = _scan_bwd(
        "wkv_bwd", _wkv_block, wkv_in, dy, zs_wkv, WKV_CHUNK, side=None if late is None else (late['slabs'](G), True))
    (dps,), rw_grads = _pw_bwd(
        "rwkv_prep_bwd", _rwkv_prep_fn, [p_rwkv], rw_consts,
        [(dr1, dr2), (dlw,), (dk21, dk22), (dv1, dv2), (da_,), (db_,), (dg_,)], 256, conv_w=mixw)
    G['rwkv_w0'], dw2p, G['rwkv_a0'], da2p, G['rwkv_g2'], G['rwkv_k_k'], G['rwkv_k_a'] = rw_grads
    G['rwkv_w2'], G['rwkv_a2'] = dw2p[:64], da2p[64:]
    dp_rwkv, dmixw = _conv_bwd("shift_bwd", dps, p_rwkv, mixw, 256, dx_dtype=bf16)
    G['rwkv_mu'] = dmixw[0] - dmixw[1]

    dp = jnp.concatenate([dp_rwkv, dqkv_raw, dz, dgates, dab], axis=1)
    du = _mm(dp, wp, 'nt', "d_u")
    G['w_in_pad'] = _mm(u, dp, 'tn', "g_w_in")
    (dx,), (G['norm1_g'],) = _pw_bwd("norm1_bwd", _rms_fn, [x], [g1], [(du,)], 256, add_to_first=dx1)
    return loss, dx, G


def _pad_w_in(w):
    return jnp.concatenate([w[:, :OFF_GATES], w[:, OFF_GATES + 8:], w[:, OFF_GATES:OFF_GATES + 8],
                            jnp.zeros((w.shape[0], W_AB - 8), w.dtype)], axis=1)


def _unpad_w_in(wp):
    return jnp.concatenate([wp[:, :OFF_GATES], wp[:, OFF_AB:OFF_AB + 8], wp[:, OFF_GATES:OFF_AB]], axis=1)


BIG = ('w_in', 'rwkv_proj', 'gdn_proj', 'w_out', 'ffn_up', 'ffn_down')
SMALL_SHARDED = ('rwkv_w2', 'rwkv_a2', 'rwkv_g2', 'gdn_conv_w', 'ffn_conv_w')


def _rows128(shape):
    n = 1
    for d in shape:
        n *= d
    return -(-n // LANES)


def _pack128(arrays):
    parts = []
    for a in arrays:
        flat = a.reshape(-1)
        rows = _rows128(a.shape)
        parts.append(jnp.pad(flat, (0, rows * LANES - flat.shape[0])).reshape(rows, LANES))
    buf = jnp.concatenate(parts, axis=0)
    return jnp.pad(buf, ((0, -buf.shape[0] % HALO), (0, 0)))


def _unpack128(buf, shapes):
    out, off = [], 0
    for s in shapes:
        rows, n = _rows128(s), 1
        for d in s:
            n *= d
        out.append(buf[off:off + rows].reshape(-1)[:n].reshape(s))
        off += rows
    return out


def _row_tile(r, c):
    best = None
    for d in range(HALO, r + 1, HALO):
        if r % d == 0 and d * c * 4 <= TILE_BYTES:
            best = d
    return best if best is not None else r


def _xy_exchange(name, bufs, scatter):
    n = len(bufs)

    def body(*refs):
        start, finish = _xy_copies(refs[:n], refs[n:2 * n], *refs[2 * n:], scatter)
        start()
        finish()

    return pl.pallas_call(
        body, in_specs=[pl.BlockSpec(memory_space=pl.ANY)] * n, out_specs=[pl.BlockSpec(memory_space=pl.ANY)] * n,
        out_shape=_xy_out_shapes(bufs, scatter), scratch_shapes=_xy_sems(n), name=name)(*bufs)


def _sibling_exchange(name, bufs):
    n = len(bufs)

    def body(*refs):
        in_refs, out_refs, send_sems, recv_sems = refs[:n], refs[n:2 * n], refs[2 * n], refs[2 * n + 1]
        x, y, c = lax.axis_index("x"), lax.axis_index("y"), lax.axis_index("c")
        copies = [pltpu.make_async_remote_copy(
            src_ref=in_refs[a], dst_ref=out_refs[a], send_sem=send_sems.at[a], recv_sem=recv_sems.at[a],
            device_id=(x, y, 1 - c), device_id_type=pl.DeviceIdType.MESH) for a in range(n)]
        for cp in copies:
            cp.start()
        for cp in copies:
            cp.wait()

    return pl.pallas_call(
        body, in_specs=[pl.BlockSpec(memory_space=pl.ANY)] * n, out_specs=[pl.BlockSpec(memory_space=pl.ANY)] * n,
        out_shape=[jax.ShapeDtypeStruct(b.shape, b.dtype) for b in bufs],
        scratch_shapes=[pltpu.SemaphoreType.DMA((n,)), pltpu.SemaphoreType.DMA((n,))], name=name)(*bufs)


def _sum_slots(name, buf):
    _, R, L = buf.shape
    tr = _row_tile(R, L)

    def body(b_ref, o_ref):
        o_ref[...] = ((b_ref[0] + b_ref[1]) + b_ref[2]) + b_ref[3]

    return pl.pallas_call(
        body, grid=(R // tr,),
        in_specs=[pl.BlockSpec((N_POS, tr, L), lambda i: (0, i, 0))],
        out_specs=pl.BlockSpec((tr, L), lambda i: (i, 0)),
        out_shape=jax.ShapeDtypeStruct((R, L), f32), name=name,
        compiler_params=_params("parallel"))(buf)


def _adamw(name, w, ga, gb, m, v):
    R, L = w.shape
    tr = _row_tile(R, L)
    c1 = 1.0 / (1.0 - ADAM_B1 ** ADAM_STEP)
    c2 = 1.0 / (1.0 - ADAM_B2 ** ADAM_STEP)

    def body(w_ref, ga_ref, gb_ref, m_ref, v_ref, g_out, d_out, m_out, v_out):
        g = ga_ref[...] + gb_ref[...]
        m_new = ADAM_B1 * m_ref[...] + (1.0 - ADAM_B1) * g
        v_new = ADAM_B2 * v_ref[...] + (1.0 - ADAM_B2) * (g * g)
        g_out[...] = g
        m_out[...] = m_new
        v_out[...] = v_new
        d_out[...] = -ADAM_LR * ((m_new * c1) / (jnp.sqrt(v_new * c2) + ADAM_EPS) + ADAM_WD * w_ref[...])

    spec = pl.BlockSpec((tr, L), lambda i: (i, 0))
    return pl.pallas_call(
        body, grid=(R // tr,), in_specs=[spec] * 5, out_specs=[spec] * 4,
        out_shape=[jax.ShapeDtypeStruct((R, L), f32)] * 4, name=name,
        compiler_params=_params("parallel"))(w, ga, gb, m, v)


def _step(x, loss_target, P, M, V):
    shapes = {n: tuple(P[n].shape) for n in WEIGHTS}
    sh_shapes = [shapes[n] for n in SMALL_SHARDED]
    packed = SMALL_SHARDED + SMALL
    late_names = BIG[1:]

    def whole(n, g):
        return g.reshape(-1, g.shape[2]) if n in ROW_SHARDED else jnp.concatenate([g[j] for j in range(N_POS)], axis=1)

    def slabs(G, n):
        r, c = shapes[n]
        return G[n].reshape(N_POS, r, c) if n in ROW_SHARDED else G[n].reshape(r, N_POS, c).transpose(1, 0, 2)

    g_w_in, g_small = _xy_exchange("gather_w_in", [P['w_in'].astype(bf16), _pack128([P[n] for n in SMALL_SHARDED])],
                                   scatter=False)
    W = {n: P[n] for n in SMALL}
    W['w_in_pad'] = _pad_w_in(whole('w_in', g_w_in))
    per_pos = [_unpack128(g_small[j], sh_shapes) for j in range(N_POS)]
    for q, n in enumerate(SMALL_SHARDED):
        W[n] = jnp.concatenate([per_pos[j][q] for j in range(N_POS)], axis=1)
    late = dict(shards=[P[n].astype(bf16) for n in late_names],
                assemble=lambda gathered: {n: whole(n, g) for n, g in zip(late_names, gathered)},
                slabs=lambda G: [slabs(G, n) for n in late_names])

    loss_rows, dx, G = _local_step(x, loss_target, W, late)
    arrived_late = G.pop('_arrived')
    G['w_in'] = _unpad_w_in(G.pop('w_in_pad'))

    small_slabs = jnp.stack([_pack128([slabs(G, n)[j] for n in SMALL_SHARDED] + [G[n] for n in SMALL]) for j in range(N_POS)])
    arrived_w_in, arrived_small = _xy_exchange("scatter_w_in", [slabs(G, 'w_in'), small_slabs], scatter=True)
    contributions = [arrived_w_in] + list(arrived_late) + [arrived_small]
    tags = list(BIG) + ['small']
    plane = [_sum_slots("sum_" + t, cbuf) for t, cbuf in zip(tags, contributions)]
    sibling = _sibling_exchange("sibling_grads", plane)

    out = {}
    names4 = ('grad', 'delta', 'new_m', 'new_v')
    for q, n in enumerate(BIG):
        for tag, t in zip(names4, _adamw("adamw_" + n, P[n], plane[q], sibling[q], M[n], V[n])):
            out[tag + '_' + n] = t
    small_out = _adamw("adamw_small", _pack128([P[n] for n in packed]), plane[-1], sibling[-1],
                       _pack128([M[n] for n in packed]), _pack128([V[n] for n in packed]))
    for tag, buf in zip(names4, small_out):
        for n, t in zip(packed, _unpack128(buf, [shapes[n] for n in packed])):
            out[tag + '_' + n] = t
    loss = lax.psum(loss_rows[0, 0], ("x", "y", "c"))
    return loss, dx, out


def kernel(x, norm1_g, w_in, rwkv_mu, rwkv_w0, rwkv_w2, rwkv_a0, rwkv_a2, rwkv_g2, rwkv_k_k, rwkv_k_a, rwkv_r_k, rwkv_ln_w, rwkv_ln_b, rwkv_proj, gdn_conv_w, gdn_a_log, gdn_dt_bias, gdn_norm_w, gdn_proj, w_out, norm2_g, ffn_up, ffn_conv_w, ffn_down, final_g, loss_target, m_norm1_g, m_w_in, m_rwkv_mu, m_rwkv_w0, m_rwkv_w2, m_rwkv_a0, m_rwkv_a2, m_rwkv_g2, m_rwkv_k_k, m_rwkv_k_a, m_rwkv_r_k, m_rwkv_ln_w, m_rwkv_ln_b, m_rwkv_proj, m_gdn_conv_w, m_gdn_a_log, m_gdn_dt_bias, m_gdn_norm_w, m_gdn_proj, m_w_out, m_norm2_g, m_ffn_up, m_ffn_conv_w, m_ffn_down, m_final_g, v_norm1_g, v_w_in, v_rwkv_mu, v_rwkv_w0, v_rwkv_w2, v_rwkv_a0, v_rwkv_a2, v_rwkv_g2, v_rwkv_k_k, v_rwkv_k_a, v_rwkv_r_k, v_rwkv_ln_w, v_rwkv_ln_b, v_rwkv_proj, v_gdn_conv_w, v_gdn_a_log, v_gdn_dt_bias, v_gdn_norm_w, v_gdn_proj, v_w_out, v_norm2_g, v_ffn_up, v_ffn_conv_w, v_ffn_down, v_final_g):
    weights = (norm1_g, w_in, rwkv_mu, rwkv_w0, rwkv_w2, rwkv_a0, rwkv_a2, rwkv_g2, rwkv_k_k, rwkv_k_a, rwkv_r_k, rwkv_ln_w,
               rwkv_ln_b, rwkv_proj, gdn_conv_w, gdn_a_log, gdn_dt_bias, gdn_norm_w, gdn_proj, w_out, norm2_g, ffn_up,
               ffn_conv_w, ffn_down, final_g)
    m_in = (m_norm1_g, m_w_in, m_rwkv_mu, m_rwkv_w0, m_rwkv_w2, m_rwkv_a0, m_rwkv_a2, m_rwkv_g2, m_rwkv_k_k, m_rwkv_k_a,
            m_rwkv_r_k, m_rwkv_ln_w, m_rwkv_ln_b, m_rwkv_proj, m_gdn_conv_w, m_gdn_a_log, m_gdn_dt_bias, m_gdn_norm_w,
            m_gdn_proj, m_w_out, m_norm2_g, m_ffn_up, m_ffn_conv_w, m_ffn_down, m_final_g)
    v_in = (v_norm1_g, v_w_in, v_rwkv_mu, v_rwkv_w0, v_rwkv_w2, v_rwkv_a0, v_rwkv_a2, v_rwkv_g2, v_rwkv_k_k, v_rwkv_k_a,
            v_rwkv_r_k, v_rwkv_ln_w, v_rwkv_ln_b, v_rwkv_proj, v_gdn_conv_w, v_gdn_a_log, v_gdn_dt_bias, v_gdn_norm_w,
            v_gdn_proj, v_w_out, v_norm2_g, v_ffn_up, v_ffn_conv_w, v_ffn_down, v_final_g)
    drop = lambda n, a: a if n == 'final_g' else a[0]
    P = {n: drop(n, a) for n, a in zip(WEIGHTS, weights)}
    M = {n: drop(n, a) for n, a in zip(WEIGHTS, m_in)}
    V = {n: drop(n, a) for n, a in zip(WEIGHTS, v_in)}
    loss, dx, out = _step(x[0], loss_target[0], P, M, V)
    lift = lambda n, a: a if n == 'final_g' else a[None]
    res = [loss, dx[None]]
    for tag in ('grad', 'delta', 'new_m', 'new_v'):
        res += [lift(n, out[tag + '_' + n]) for n in WEIGHTS]
    return tuple(res)
```

```python
import functools

import jax
import jax.numpy as jnp
from jax import lax
from jax.experimental import pallas as pl
from jax.experimental.pallas import tpu as pltpu

f32 = jnp.float32
bf16 = jnp.bfloat16
HI = lax.Precision.HIGHEST

D_MODEL = 1024
RWKV_HEADS, RWKV_HD, RWKV_W = 8, 64, 512
GDN_HEADS, GDN_HD, GDN_W = 4, 128, 512
FFN_H = 2816
NORM_EPS, L2_EPS, GN_EPS = 1e-6, 1e-6, 64e-5
W_AB = 256
OFF_QKV, OFF_Z, OFF_GATES, OFF_AB = 1792, 3328, 3840, 5888
W_IN_PAD = OFF_AB + W_AB
WKV_CHUNK = 64
GDN_CHUNK = 128
HALO = 8
LANES = 128
TILE_BYTES = 1 << 20
VMEM_LIMIT = 56 * 1024 * 1024

ADAM_LR, ADAM_B1, ADAM_B2, ADAM_EPS, ADAM_WD, ADAM_STEP = 0.001, 0.9, 0.999, 1e-08, 0.01, 10

ROW_SHARDED = ('w_out', 'ffn_down')
SMALL = ('norm1_g', 'rwkv_mu', 'rwkv_w0', 'rwkv_a0', 'rwkv_k_k', 'rwkv_k_a', 'rwkv_r_k', 'rwkv_ln_w', 'rwkv_ln_b',
         'gdn_a_log', 'gdn_dt_bias', 'gdn_norm_w', 'norm2_g', 'final_g')
WEIGHTS = ('norm1_g', 'w_in', 'rwkv_mu', 'rwkv_w0', 'rwkv_w2', 'rwkv_a0', 'rwkv_a2', 'rwkv_g2', 'rwkv_k_k', 'rwkv_k_a',
           'rwkv_r_k', 'rwkv_ln_w', 'rwkv_ln_b', 'rwkv_proj', 'gdn_conv_w', 'gdn_a_log', 'gdn_dt_bias', 'gdn_norm_w',
           'gdn_proj', 'w_out', 'norm2_g', 'ffn_up', 'ffn_conv_w', 'ffn_down', 'final_g')


def _params(*sem):
    return pltpu.CompilerParams(dimension_semantics=sem, vmem_limit_bytes=VMEM_LIMIT)


def _tile(n, limit):
    if n <= limit:
        return n
    best = None
    for d in range(128, limit + 1, 128):
        if n % d == 0:
            best = d
    if best is None:
        raise ValueError(f"no tile for {n} under {limit}")
    return best


MM_BLOCK_BYTES = 4 << 20


def _mm(a, b, mode, name, add=None, out_dtype=f32):
    if mode == 'nn':
        (M, K), N = a.shape, b.shape[1]
    elif mode == 'nt':
        (M, K), N = a.shape, b.shape[0]
    else:
        (K, M), N = a.shape, b.shape[1]
    tm = _tile(M, 512)
    tk = _tile(K, min(2816, MM_BLOCK_BYTES // (tm * a.dtype.itemsize)))
    tn = _tile(N, max(128, min(MM_BLOCK_BYTES // (tk * b.dtype.itemsize), MM_BLOCK_BYTES // (tm * 4)) // 128 * 128))
    nk = K // tk
    dn = {'nn': (((1,), (0,)), ((), ())), 'nt': (((1,), (1,)), ((), ())), 'tn': (((0,), (0,)), ((), ()))}[mode]

    def body(a_ref, b_ref, *rest):
        o_ref = rest[1] if add is not None else rest[0]
        acc = lax.dot_general(a_ref[...].astype(bf16), b_ref[...].astype(bf16), dn, preferred_element_type=f32)
        if nk == 1:
            o_ref[...] = (acc + rest[0][...] if add is not None else acc).astype(out_dtype)
            return
        acc_ref = rest[-1]
        k = pl.program_id(2)

        @pl.when(k == 0)
        def _():
            acc_ref[...] = acc + rest[0][...] if add is not None else acc

        @pl.when(k > 0)
        def _():
            acc_ref[...] += acc

        @pl.when(k == nk - 1)
        def _():
            o_ref[...] = acc_ref[...].astype(out_dtype)

    a_spec = (pl.BlockSpec((tk, tm), lambda i, j, k: (k, i)) if mode == 'tn'
              else pl.BlockSpec((tm, tk), lambda i, j, k: (i, k)))
    b_spec = (pl.BlockSpec((tn, tk), lambda i, j, k: (j, k)) if mode == 'nt'
              else pl.BlockSpec((tk, tn), lambda i, j, k: (k, j)))
    o_spec = pl.BlockSpec((tm, tn), lambda i, j, k: (i, j))
    ins, specs = [a, b], [a_spec, b_spec]
    if add is not None:
        ins.append(add)
        specs.append(o_spec)
    return pl.pallas_call(
        body, grid=(M // tm, N // tn, nk), in_specs=specs, out_specs=o_spec,
        out_shape=jax.ShapeDtypeStruct((M, N), out_dtype),
        scratch_shapes=[pltpu.VMEM((tm, tn), f32)] if nk > 1 else [], name=name,
        compiler_params=_params("parallel", "parallel", "arbitrary"))(*ins)


def _shift_down(cur, prev, s):
    if s == 0:
        return cur
    ext = jnp.concatenate([prev, cur], axis=0)
    return pltpu.roll(ext, s, 0)[HALO:]


def _shift_up(cur, nxt, s):
    if s == 0:
        return cur
    ext = jnp.concatenate([cur, nxt], axis=0)
    return pltpu.roll(ext, ext.shape[0] - s, 0)[:cur.shape[0]]


def _conv_apply(cur, prev, w_ref):
    taps = w_ref.shape[0]
    out = None
    for i in range(taps):
        term = _shift_down(cur, prev, taps - 1 - i) * w_ref[pl.ds(i, 1), :]
        out = term if out is None else out + term
    return out


def _row_spec(tm, w):
    return pl.BlockSpec((tm, w), lambda i: (i, 0))


def _prev_spec(tm, w):
    return pl.BlockSpec((HALO, w), lambda i: (jnp.maximum(i * (tm // HALO) - 1, 0), 0))


def _next_spec(tm, w, T):
    return pl.BlockSpec((HALO, w), lambda i: (jnp.minimum((i + 1) * (tm // HALO), T // HALO - 1), 0))


def _full_spec(shape):
    return pl.BlockSpec(shape, lambda i: (0,) * len(shape))


def _pw_fwd(name, fn, rows, consts, out_widths, tm, conv_w=None, out_dtype=f32):
    T = rows[0].shape[0]
    nr, nc = len(rows), len(consts)

    def body(*refs):
        i = pl.program_id(0)
        vals = [r[...] for r in refs[:nr]]
        p = nr
        if conv_w is not None:
            prev = jnp.where(i > 0, refs[p][...], 0.0)
            vals[0] = _conv_apply(vals[0], prev, refs[p + 1])
            p += 2
        cvals = [r[...] for r in refs[p:p + nc]]
        outs = fn(*vals, *cvals)
        for o_ref, o in zip(refs[p + nc:], outs):
            o_ref[...] = o.astype(out_dtype)

    ins = list(rows)
    specs = [_row_spec(tm, r.shape[1]) for r in rows]
    if conv_w is not None:
        ins += [rows[0], conv_w]
        specs += [_prev_spec(tm, rows[0].shape[1]), _full_spec(conv_w.shape)]
    ins += list(consts)
    specs += [_full_spec(c.shape) for c in consts]
    outs = pl.pallas_call(
        body, grid=(T // tm,), in_specs=specs,
        out_specs=[_row_spec(tm, w) for w in out_widths],
        out_shape=[jax.ShapeDtypeStruct((T, w), out_dtype) for w in out_widths], name=name,
        compiler_params=_params("parallel"))(*ins)
    return outs


def _pw_bwd(name, fn, rows, consts, cots, tm, conv_w=None, add_to_first=None, row_dtypes=None):
    T = rows[0].shape[0]
    nr, nc = len(rows), len(consts)
    flat_cots = [c for grp in cots for c in grp]
    row_dtypes = row_dtypes or [f32] * nr

    def body(*refs):
        i = pl.program_id(0)
        vals = [r[...] for r in refs[:nr]]
        p = nr
        if conv_w is not None:
            prev = jnp.where(i > 0, refs[p][...], 0.0)
            vals[0] = _conv_apply(vals[0], prev, refs[p + 1])
            p += 2
        cvals = [r[...] for r in refs[p:p + nc]]
        p += nc
        cot_vals = []
        for grp in cots:
            acc = refs[p][...]
            for q in range(1, len(grp)):
                acc = acc + refs[p + q][...]
            p += len(grp)
            cot_vals.append(acc)
        extra = None
        if add_to_first is not None:
            extra = refs[p][...]
            p += 1
        _, vjp = jax.vjp(fn, *vals, *cvals)
        grads = vjp(tuple(cot_vals))
        row_out = refs[p:p + nr]
        const_out = refs[p + nr:]
        for q in range(nr):
            g = grads[q]
            if q == 0 and extra is not None:
                g = g + extra
            row_out[q][...] = g.astype(row_dtypes[q])

        @pl.when(i == 0)
        def _():
            for q in range(nc):
                const_out[q][...] = grads[nr + q]

        @pl.when(i > 0)
        def _():
            for q in range(nc):
                const_out[q][...] += grads[nr + q]

    ins = list(rows)
    specs = [_row_spec(tm, r.shape[1]) for r in rows]
    if conv_w is not None:
        ins += [rows[0], conv_w]
        specs += [_prev_spec(tm, rows[0].shape[1]), _full_spec(conv_w.shape)]
    ins += list(consts)
    specs += [_full_spec(c.shape) for c in consts]
    ins += flat_cots
    specs += [_row_spec(tm, c.shape[1]) for c in flat_cots]
    if add_to_first is not None:
        ins.append(add_to_first)
        specs.append(_row_spec(tm, add_to_first.shape[1]))
    out_shapes = ([jax.ShapeDtypeStruct(r.shape, d) for r, d in zip(rows, row_dtypes)]
                  + [jax.ShapeDtypeStruct(c.shape, f32) for c in consts])
    out_specs = [_row_spec(tm, r.shape[1]) for r in rows] + [_full_spec(c.shape) for c in consts]
    outs = pl.pallas_call(
        body, grid=(T // tm,), in_specs=specs, out_specs=out_specs, out_shape=out_shapes, name=name,
        compiler_params=_params("arbitrary"))(*ins)
    return list(outs[:nr]), list(outs[nr:])


def _conv_bwd(name, dc, x, w, tm, dx_dtype=f32):
    T, W = x.shape
    taps = w.shape[0]
    nblk = T // tm

    def body(dc_ref, dcn_ref, x_ref, xp_ref, w_ref, dx_ref, dw_ref):
        i = pl.program_id(0)
        dcv, xv = dc_ref[...], x_ref[...]
        nxt = jnp.where(i < nblk - 1, dcn_ref[...], 0.0)
        prev = jnp.where(i > 0, xp_ref[...], 0.0)

        @pl.when(i == 0)
        def _():
            dw_ref[...] = jnp.zeros_like(dw_ref)

        dx = None
        for k in range(taps):
            s = taps - 1 - k
            term = _shift_up(dcv, nxt, s) * w_ref[pl.ds(k, 1), :]
            dx = term if dx is None else dx + term
            dw_ref[pl.ds(k, 1), :] += jnp.sum(dcv * _shift_down(xv, prev, s), axis=0, keepdims=True)
        dx_ref[...] = dx.astype(dx_dtype)

    return pl.pallas_call(
        body, grid=(nblk,),
        in_specs=[_row_spec(tm, W), _next_spec(tm, W, T), _row_spec(tm, W), _prev_spec(tm, W), _full_spec(w.shape)],
        out_specs=[_row_spec(tm, W), _full_spec(w.shape)],
        out_shape=[jax.ShapeDtypeStruct((T, W), dx_dtype), jax.ShapeDtypeStruct(w.shape, f32)], name=name,
        compiler_params=_params("arbitrary"))(dc, dc, x, x, w)


def _sigmoid(x):
    return 1.0 / (1.0 + jnp.exp(-x))


def _softplus(x):
    return jnp.maximum(x, 0.0) + jnp.log(1.0 + jnp.exp(jnp.minimum(x, -x)))


def _seg_sum_impl(x, seg):
    w = x.shape[-1]
    r = lax.broadcasted_iota(jnp.int32, (w, w), 0) // seg
    c = lax.broadcasted_iota(jnp.int32, (w, w), 1) // seg
    ones = (r == c).astype(bf16)
    hi = x.astype(bf16)
    lo = (x - hi.astype(f32)).astype(bf16)
    return (jnp.dot(hi, ones, preferred_element_type=f32) + jnp.dot(lo, ones, preferred_element_type=f32))


@functools.partial(jax.custom_vjp, nondiff_argnums=(1,))
def _seg_sum(x, seg):
    return _seg_sum_impl(x, seg)


_seg_sum.defvjp(lambda x, seg: (_seg_sum_impl(x, seg), None), lambda seg, _, g: (_seg_sum_impl(g, seg),))


def _rms(x, g):
    return x * lax.rsqrt(jnp.mean(x * x, axis=-1, keepdims=True) + NORM_EPS) * g


def _rms_fn(x, g):
    return (_rms(x, g),)


def _loss_rows(x2, tgt, g):
    e = _rms(x2, g) - tgt
    return 0.5 * jnp.sum(e * e, axis=-1, keepdims=True) * (1.0 / D_MODEL)


def _rwkv_prep_fn(ps, w0, w2p, a0, a2p, g2, k_k, k_a):
    r, k, v = ps[:, 0:512], ps[:, 512:1024], ps[:, 1024:1536]
    wa, gl = ps[:, 1536:1664], ps[:, 1664:1792]
    z = w0 + jnp.dot(jnp.tanh(wa), w2p, precision=HI, preferred_element_type=f32)
    w_log = -_softplus(-z) - 0.5
    lw = -jnp.exp(w_log)
    a = _sigmoid(a0 + jnp.dot(wa, a2p, precision=HI, preferred_element_type=f32))
    g = jnp.dot(_sigmoid(gl), g2, precision=HI, preferred_element_type=f32)
    kx = k * k_k
    kk = kx * lax.rsqrt(_seg_sum(kx * kx, RWKV_HD) + L2_EPS)
    k2 = k * (1.0 + (a - 1.0) * k_a)
    return r, lw, k2, v, -kk, kk * a, g


def _rwkv_post_fn(y, r, k2, v, g, ln_w, ln_b, rk):
    mean = _seg_sum(y, RWKV_HD) * (1.0 / RWKV_HD)
    yc = y - mean
    var = _seg_sum(yc * yc, RWKV_HD) * (1.0 / RWKV_HD)
    yn = yc * lax.rsqrt(var + GN_EPS) * ln_w + ln_b
    bonus = _seg_sum(r * k2 * rk, RWKV_HD) * v
    return ((yn + bonus) * g,)


def _gdn_prep_fn(c, ab, al_p, dt_p):
    s = c * _sigmoid(c)
    q, k, v = s[:, 0:512], s[:, 512:1024], s[:, 1024:1536]
    q = q * lax.rsqrt(_seg_sum(q * q, GDN_HD) + L2_EPS) * (GDN_HD ** -0.5)
    k = k * lax.rsqrt(_seg_sum(k * k, GDN_HD) + L2_EPS)
    lane = lax.broadcasted_iota(jnp.int32, ab.shape, 1)
    gpart = -jnp.exp(al_p) * _softplus(ab + dt_p)
    gbeta = jnp.where(lane < GDN_HEADS, gpart, jnp.where(lane < 2 * GDN_HEADS, _sigmoid(ab), 0.0))
    return q, k, v, gbeta


def _gdn_post_fn(o, z, nw):
    ms = _seg_sum(o * o, GDN_HD) * (1.0 / GDN_HD)
    return (o * lax.rsqrt(ms + NORM_EPS) * nw * (z * _sigmoid(z)),)


def _mix_fn(gates, ya, yb):
    return (_sigmoid(gates[:, :D_MODEL]) * ya + _sigmoid(gates[:, D_MODEL:]) * yb,)


def _ffn_fn(c):
    hg, hu = c[:, :FFN_H], c[:, FFN_H:]
    return (hg * _sigmoid(hg) * hu,)


N_POS = 4


def _xy_out_shapes(bufs, scatter):
    return [jax.ShapeDtypeStruct((N_POS,) + tuple(b.shape[1:] if scatter else b.shape), b.dtype) for b in bufs]


def _xy_sems(n):
    return [pltpu.SemaphoreType.DMA((3 * n,)), pltpu.SemaphoreType.DMA((3 * n,)), pltpu.SemaphoreType.DMA((n,))]


def _xy_copies(in_refs, out_refs, send_sems, recv_sems, local_sems, scatter):
    n = len(in_refs)

    def plan(arriving):
        x, y, c = lax.axis_index("x"), lax.axis_index("y"), lax.axis_index("c")
        me = 2 * x + y
        peers = [(1 - x, y), (x, 1 - y), (1 - x, 1 - y)]

        def copy(a, k, src, slot, peer):
            return pltpu.make_async_remote_copy(
                src_ref=src, dst_ref=out_refs[a].at[slot], send_sem=send_sems.at[3 * a + k],
                recv_sem=recv_sems.at[3 * a + k], device_id=(peer[0], peer[1], c), device_id_type=pl.DeviceIdType.MESH)

        if arriving:
            return [copy(a, k, in_refs[a].at[me] if scatter else in_refs[a], 2 * peer[0] + peer[1], peer)
                    for a in range(n) for k, peer in enumerate(peers)]
        own = [pltpu.make_async_copy(in_refs[a].at[me] if scatter else in_refs[a], out_refs[a].at[me], local_sems.at[a])
               for a in range(n)]
        sends = [copy(a, k, in_refs[a].at[2 * peer[0] + peer[1]] if scatter else in_refs[a], me, peer)
                 for a in range(n) for k, peer in enumerate(peers)]
        return own + sends

    def start():
        for cp in plan(False):
            cp.start()

    def finish():
        for cp in plan(True):
            cp.wait_recv()
        outgoing = plan(False)
        for cp in outgoing[n:]:
            cp.wait_send()
        for cp in outgoing[:n]:
            cp.wait()

    return start, finish


_NN, _NT, _TN = 'hcs,hsd->hcd', 'hcd,hsd->hcs', 'hcd,hce->hde'


def _lo(spec, a, b):
    return jnp.einsum(spec, a.astype(bf16), b.astype(bf16), preferred_element_type=f32)


@jax.custom_vjp
def _bmm(a, b):
    return _lo(_NN, a, b)


_bmm.defvjp(lambda a, b: (_lo(_NN, a, b), (a, b)), lambda ab, g: (_lo(_NT, g, ab[1]), _lo(_TN, ab[0], g)))


@jax.custom_vjp
def _bmm_nt(a, b):
    return _lo(_NT, a, b)


_bmm_nt.defvjp(lambda a, b: (_lo(_NT, a, b), (a, b)), lambda ab, g: (_lo(_NN, g, ab[1]), _lo(_TN, g, ab[0])))


@jax.custom_vjp
def _bmm_tn(a, b):
    return _lo(_TN, a, b)


_bmm_tn.defvjp(lambda a, b: (_lo(_TN, a, b), (a, b)), lambda ab, g: (_lo(_NT, ab[1], g), _lo(_NN, ab[0], g)))


def _masks(H, C):
    row = lax.broadcasted_iota(jnp.int32, (H, C, C), 1)
    col = lax.broadcasted_iota(jnp.int32, (H, C, C), 2)
    return row, col


def _tri_inv_impl(L):
    H, C, _ = L.shape
    row, col = _masks(H, C)
    eye = (row == col).astype(f32)
    base = 16
    same = (row // base) == (col // base)
    Ld = jnp.where(same, L, 0.0)
    X = -Ld
    inv = eye + X
    for _ in range(3):
        X = _bmm(X, X)
        inv = _bmm(inv, eye + X)
    if C == base:
        return inv
    N = _bmm(inv, L - Ld)
    out = eye - N
    levels = C // base
    P = N
    span = 2
    while span < levels:
        P = _bmm(P, P)
        out = _bmm(out, eye + P)
        span *= 2
    return _bmm(out, inv)


@jax.custom_vjp
def _tri_inv(L):
    return _tri_inv_impl(L)


def _tri_inv_fwd(L):
    T = _tri_inv_impl(L)
    return T, T


def _tri_inv_bwd(T, dT):
    return (-_bmm_nt(_bmm_tn(T, dT), T),)


_tri_inv.defvjp(_tri_inv_fwd, _tri_inv_bwd)


def _cumsum_impl(x, reverse):
    C = x.shape[1]
    row = lax.broadcasted_iota(jnp.int32, x.shape, 1)
    s = 1
    while s < C:
        if reverse:
            x = x + jnp.where(row < C - s, pltpu.roll(x, C - s, 1), 0.0)
        else:
            x = x + jnp.where(row >= s, pltpu.roll(x, s, 1), 0.0)
        s *= 2
    return x


@jax.custom_vjp
def _cumsum(x):
    return _cumsum_impl(x, False)


_cumsum.defvjp(lambda x: (_cumsum_impl(x, False), None), lambda _, g: (_cumsum_impl(g, True),))


def _wkv_chunk(Z, r, lw, k, v, a, b):
    H, C, D = r.shape
    row, col = _masks(H, C)
    incl, strict = row >= col, row > col
    cw = _cumsum(lw)
    cwp = cw - lw
    cwl = jnp.sum(lw, axis=1, keepdims=True)
    en = jnp.exp(-cw)
    at, rt, bt, kt = a * jnp.exp(cwp), r * jnp.exp(cw), b * en, k * en
    Lab = jnp.where(strict, _bmm_nt(at, bt), 0.0)
    Lak = jnp.where(strict, _bmm_nt(at, kt), 0.0)
    Tm = _tri_inv(-Lab)
    U = _bmm(Tm, _bmm(at, Z) + _bmm(Lak, v))
    Rb = jnp.where(incl, _bmm_nt(rt, bt), 0.0)
    Rk = jnp.where(incl, _bmm_nt(rt, kt), 0.0)
    y = _bmm(rt, Z) + _bmm(Rb, U) + _bmm(Rk, v)
    ed = jnp.exp(cwl - cw)
    zdec = jnp.swapaxes(jnp.broadcast_to(jnp.exp(cwl), (H, Z.shape[2], D)), 1, 2)
    Z1 = Z * zdec + _bmm_tn(b * ed, U) + _bmm_tn(k * ed, v)
    return y, Z1


def _gdn_chunk(S, q, k, v, g, beta):
    H, C, D = q.shape
    row, col = _masks(H, C)
    incl, strict = row >= col, row > col
    gc = _cumsum(g)
    diff = gc - jnp.swapaxes(gc, 1, 2)
    decay = jnp.where(incl, jnp.exp(jnp.where(incl, diff, 0.0)), 0.0)
    gl = jnp.sum(g, axis=1, keepdims=True)
    kb, vb = k * beta, v * beta
    L = jnp.where(strict, _bmm_nt(kb, k) * decay, 0.0)
    Tm = _tri_inv(L)
    egc = jnp.exp(gc)
    u = _bmm(Tm, vb)
    wk = _bmm(Tm, kb * egc)
    attn = jnp.where(incl, _bmm_nt(q, k) * decay, 0.0)
    v_new = u - _bmm(wk, S)
    o = _bmm(q * egc, S) + _bmm(attn, v_new)
    S1 = S * jnp.exp(gl) + _bmm_tn(k * jnp.exp(gl - gc), v_new)
    return o, S1


def _wkv_block(Z, r, lw, k, v, a, b):
    lane = lax.broadcasted_iota(jnp.int32, (r.shape[0], 128), 1)
    low = lane < RWKV_HD

    def heads(t):
        out = []
        for p in range(RWKV_HEADS // 2):
            pair = t[:, 128 * p:128 * (p + 1)]
            out += [jnp.where(low, pair, 0.0), jnp.where(low, 0.0, pair)]
        return jnp.concatenate([t[None] for t in out], axis=0)

    y, Z1 = _wkv_chunk(Z, *[heads(t) for t in (r, lw, k, v, a, b)])
    return jnp.concatenate([y[2 * p] + y[2 * p + 1] for p in range(RWKV_HEADS // 2)], axis=1), Z1


def _gdn_block(S, q, k, v, gbeta):
    heads = lambda t: jnp.concatenate([t[None, :, GDN_HD * h:GDN_HD * (h + 1)] for h in range(GDN_HEADS)], axis=0)
    src = lax.broadcasted_iota(jnp.int32, (W_AB, 2 * GDN_W), 0)
    dst = lax.broadcasted_iota(jnp.int32, (W_AB, 2 * GDN_W), 1) // GDN_HD
    spread = jnp.dot(gbeta, (src == dst).astype(f32), precision=HI, preferred_element_type=f32)
    o, S1 = _gdn_chunk(S, heads(q), heads(k), heads(v), heads(spread[:, :GDN_W]), heads(spread[:, GDN_W:]))
    return jnp.concatenate([o[h] for h in range(GDN_HEADS)], axis=1), S1


def _scan_fwd(name, block_fn, ins, C, H, dh, w_out, side=None):
    T = ins[0].shape[0]
    n_in = len(ins)
    nblk = T // C
    n_side = 0 if side is None else len(side[0])

    def body(*refs):
        in_refs, refs = refs[:n_in], refs[n_in:]
        side_in, refs = refs[:n_side], refs[n_side:]
        y_ref, zs_ref, refs = refs[0], refs[1], refs[2:]
        side_out, refs = refs[:n_side], refs[n_side:]
        z_scr = refs[0]
        if side is not None:
            start, finish = _xy_copies(side_in, side_out, refs[1], refs[2], refs[3], side[1])
            pl.when(pl.program_id(0) == 0)(start)

        @pl.when(pl.program_id(0) == 0)
        def _():
            z_scr[...] = jnp.zeros_like(z_scr)

        Z = z_scr[...]
        zs_ref[0] = Z
        y, Z1 = block_fn(Z, *[r[...] for r in in_refs])
        y_ref[...] = y
        z_scr[...] = Z1
        if side is not None:
            pl.when(pl.program_id(0) == nblk - 1)(finish)

    side_bufs = [] if side is None else list(side[0])
    any_spec = pl.BlockSpec(memory_space=pl.ANY)
    return pl.pallas_call(
        body, grid=(nblk,),
        in_specs=[pl.BlockSpec((C, a.shape[1]), lambda i: (i, 0)) for a in ins] + [any_spec] * n_side,
        out_specs=[pl.BlockSpec((C, w_out), lambda i: (i, 0)), pl.BlockSpec((1, H, dh, dh), lambda i: (i, 0, 0, 0))]
        + [any_spec] * n_side,
        out_shape=[jax.ShapeDtypeStruct((T, w_out), f32), jax.ShapeDtypeStruct((T // C, H, dh, dh), f32)]
        + (_xy_out_shapes(side_bufs, side[1]) if side is not None else []),
        scratch_shapes=[pltpu.VMEM((H, dh, dh), f32)] + (_xy_sems(n_side) if side is not None else []), name=name,
        compiler_params=_params("arbitrary"))(*ins, *side_bufs)


def _scan_bwd(name, block_fn, ins, dy, zs, C, side=None):
    T = ins[0].shape[0]
    _, H, dh, _ = zs.shape
    n_in = len(ins)
    nblk = T // C
    n_side = 0 if side is None else len(side[0])

    def body(*refs):
        in_refs, dy_ref, zs_ref, refs = refs[:n_in], refs[n_in], refs[n_in + 1], refs[n_in + 2:]
        side_in, refs = refs[:n_side], refs[n_side:]
        out_refs, refs = refs[:n_in], refs[n_in:]
        side_out, refs = refs[:n_side], refs[n_side:]
        dz_scr = refs[0]
        if side is not None:
            start, finish = _xy_copies(side_in, side_out, refs[1], refs[2], refs[3], side[1])
            pl.when(pl.program_id(0) == 0)(start)

        @pl.when(pl.program_id(0) == 0)
        def _():
            dz_scr[...] = jnp.zeros_like(dz_scr)

        _, vjp = jax.vjp(block_fn, zs_ref[0], *[r[...] for r in in_refs])
        grads = vjp((dy_ref[...], dz_scr[...]))
        dz_scr[...] = grads[0]
        for o_ref, gval in zip(out_refs, grads[1:]):
            o_ref[...] = gval
        if side is not None:
            pl.when(pl.program_id(0) == nblk - 1)(finish)

    side_bufs = [] if side is None else list(side[0])
    any_spec = pl.BlockSpec(memory_space=pl.ANY)
    rev = lambda i: (nblk - 1 - i, 0)
    return pl.pallas_call(
        body, grid=(nblk,),
        in_specs=[pl.BlockSpec((C, a.shape[1]), rev) for a in ins]
        + [pl.BlockSpec((C, dy.shape[1]), rev), pl.BlockSpec((1, H, dh, dh), lambda i: (nblk - 1 - i, 0, 0, 0))]
        + [any_spec] * n_side,
        out_specs=[pl.BlockSpec((C, a.shape[1]), rev) for a in ins] + [any_spec] * n_side,
        out_shape=[jax.ShapeDtypeStruct(a.shape, f32) for a in ins]
        + (_xy_out_shapes(side_bufs, side[1]) if side is not None else []),
        scratch_shapes=[pltpu.VMEM((H, dh, dh), f32)] + (_xy_sems(n_side) if side is not None else []), name=name,
        compiler_params=_params("arbitrary"))(*ins, dy, zs, *side_bufs)


def _loss_call(x2, tgt, g, tm):
    T, W = x2.shape

    def body(x_ref, t_ref, g_ref, dx_ref, dg_ref, l_ref):
        i = pl.program_id(0)
        tv = t_ref[...]
        l, vjp = jax.vjp(lambda xv, gv: _loss_rows(xv, tv, gv), x_ref[...], g_ref[...])
        dx, dg = vjp(jnp.ones_like(l))
        dx_ref[...] = dx
        tot = jnp.zeros((1, 128), f32) + jnp.sum(l)

        @pl.when(i == 0)
        def _():
            dg_ref[...] = dg
            l_ref[...] = tot

        @pl.when(i > 0)
        def _():
            dg_ref[...] += dg
            l_ref[...] += tot

    return pl.pallas_call(
        body, grid=(T // tm,),
        in_specs=[_row_spec(tm, W), _row_spec(tm, W), _full_spec(g.shape)],
        out_specs=[_row_spec(tm, W), _full_spec(g.shape), _full_spec((1, 128))],
        out_shape=[jax.ShapeDtypeStruct((T, W), f32), jax.ShapeDtypeStruct(g.shape, f32),
                   jax.ShapeDtypeStruct((1, 128), f32)], name="loss_head",
        compiler_params=_params("arbitrary"))(x2, tgt, g)


def _local_step(x, tgt, W, late=None):
    row = lambda a: a.reshape(1, -1)
    wp = W['w_in_pad']
    w_rwkv, w_qkv, w_z = wp[:, :OFF_QKV], wp[:, OFF_QKV:OFF_Z], wp[:, OFF_Z:OFF_GATES]
    w_gates, w_ab = wp[:, OFF_GATES:OFF_AB], wp[:, OFF_AB:]
    mu = row(W['rwkv_mu'])
    mixw = jnp.concatenate([mu, 1.0 - mu], axis=0)
    zpad = jnp.zeros((64, RWKV_W), f32)
    w2p = jnp.concatenate([W['rwkv_w2'], zpad], axis=0)
    a2p = jnp.concatenate([zpad, W['rwkv_a2']], axis=0)
    rw_consts = [row(W['rwkv_w0']), w2p, row(W['rwkv_a0']), a2p, W['rwkv_g2'], row(W['rwkv_k_k']), row(W['rwkv_k_a'])]
    post_consts = [row(W['rwkv_ln_w']), row(W['rwkv_ln_b']), row(W['rwkv_r_k'])]
    pad4 = lambda a: jnp.pad(row(a), ((0, 0), (0, W_AB - GDN_HEADS)))
    gd_consts = [pad4(W['gdn_a_log']), pad4(W['gdn_dt_bias'])]
    nw_t = jnp.tile(row(W['gdn_norm_w']), (1, GDN_HEADS))
    g1, g2n, gf = row(W['norm1_g']), row(W['norm2_g']), row(W['final_g'])

    (u,) = _pw_fwd("norm1", _rms_fn, [x], [g1], [D_MODEL], 256, out_dtype=bf16)
    p_rwkv = _mm(u, w_rwkv, 'nn', "in_rwkv")
    qkv_raw = _mm(u, w_qkv, 'nn', "in_qkv")
    z = _mm(u, w_z, 'nn', "in_z")
    gates = _mm(u, w_gates, 'nn', "in_gates")
    ab = _mm(u, w_ab, 'nn', "in_ab")

    r, lw, k2, v, a_, b_, g = _pw_fwd("rwkv_prep", _rwkv_prep_fn, [p_rwkv], rw_consts, [RWKV_W] * 7, 256, conv_w=mixw)
    wkv_in = [r, lw, k2, v, a_, b_]
    y, zs_wkv, *gathered = _scan_fwd("wkv_fwd", _wkv_block, wkv_in, WKV_CHUNK, RWKV_HEADS, 2 * RWKV_HD, RWKV_W,
                                     side=None if late is None else (late['shards'], False))
    if late is not None:
        W = dict(W, **late['assemble'](gathered))
    (ya_in,) = _pw_fwd("rwkv_post", _rwkv_post_fn, [y, r, k2, v, g], post_consts, [RWKV_W], 256, out_dtype=bf16)
    ya = _mm(ya_in, W['rwkv_proj'], 'nn', "rwkv_proj")

    gq, gk, gv, gbeta = _pw_fwd("gdn_prep", _gdn_prep_fn, [qkv_raw, ab], gd_consts, [GDN_W] * 3 + [W_AB], 256,
                                conv_w=W['gdn_conv_w'])
    gdn_in = [gq, gk, gv, gbeta]
    o, zs_gdn = _scan_fwd("gdn_fwd", _gdn_block, gdn_in, GDN_CHUNK, GDN_HEADS, GDN_HD, GDN_W)
    (yb_in,) = _pw_fwd("gdn_post", _gdn_post_fn, [o, z], [nw_t], [GDN_W], 256, out_dtype=bf16)
    yb = _mm(yb_in, W['gdn_proj'], 'nn', "gdn_proj")

    (mixed,) = _pw_fwd("mix", _mix_fn, [gates, ya, yb], [], [D_MODEL], 256, out_dtype=bf16)
    x1 = _mm(mixed, W['w_out'], 'nn', "w_out", add=x)
    (u2,) = _pw_fwd("norm2", _rms_fn, [x1], [g2n], [D_MODEL], 256, out_dtype=bf16)
    h = _mm(u2, W['ffn_up'], 'nn', "ffn_up")
    (act,) = _pw_fwd("ffn_act", _ffn_fn, [h], [], [FFN_H], 128, conv_w=W['ffn_conv_w'], out_dtype=bf16)
    x2 = _mm(act, W['ffn_down'], 'nn', "ffn_down", add=x1)

    G = {}
    dx2, dgf, loss = _loss_call(x2, tgt, gf, 256)
    G['final_g'] = dgf
    dact = _mm(dx2, W['ffn_down'], 'nt', "d_act")
    G['ffn_down'] = _mm(act, dx2, 'tn', "g_ffn_down")
    (dc_ffn,), _ = _pw_bwd("ffn_act_bwd", _ffn_fn, [h], [], [(dact,)], 128, conv_w=W['ffn_conv_w'])
    dh, G['ffn_conv_w'] = _conv_bwd("ffn_conv_bwd", dc_ffn, h, W['ffn_conv_w'], 128, dx_dtype=bf16)
    du2 = _mm(dh, W['ffn_up'], 'nt', "d_u2")
    G['ffn_up'] = _mm(u2, dh, 'tn', "g_ffn_up")
    (dx1,), (G['norm2_g'],) = _pw_bwd("norm2_bwd", _rms_fn, [x1], [g2n], [(du2,)], 256, add_to_first=dx2)
    dmixed = _mm(dx1, W['w_out'], 'nt', "d_mixed")
    G['w_out'] = _mm(mixed, dx1, 'tn', "g_w_out")
    (dgates, dya, dyb), _ = _pw_bwd("mix_bwd", _mix_fn, [gates, ya, yb], [], [(dmixed,)], 256, row_dtypes=[bf16] * 3)
    dya_in = _mm(dya, W['rwkv_proj'], 'nt', "d_ya_in")
    G['rwkv_proj'] = _mm(ya_in, dya, 'tn', "g_rwkv_proj")
    dyb_in = _mm(dyb, W['gdn_proj'], 'nt', "d_yb_in")
    G['gdn_proj'] = _mm(yb_in, dyb, 'tn', "g_gdn_proj")

    (do, dz), (dnw_t,) = _pw_bwd("gdn_post_bwd", _gdn_post_fn, [o, z], [nw_t], [(dyb_in,)], 256, row_dtypes=[f32, bf16])
    G['gdn_norm_w'] = dnw_t.reshape(GDN_HEADS, GDN_HD).sum(axis=0)
    dgq, dgk, dgv, dgbeta = _scan_bwd("gdn_bwd", _gdn_block, gdn_in, do, zs_gdn, GDN_CHUNK)
    (dc_qkv, dab), (dal_p, ddt_p) = _pw_bwd(
        "gdn_prep_bwd", _gdn_prep_fn, [qkv_raw, ab], gd_consts, [(dgq,), (dgk,), (dgv,), (dgbeta,)], 256,
        conv_w=W['gdn_conv_w'], row_dtypes=[f32, bf16])
    G['gdn_a_log'], G['gdn_dt_bias'] = dal_p[0, :GDN_HEADS], ddt_p[0, :GDN_HEADS]
    dqkv_raw, G['gdn_conv_w'] = _conv_bwd("gdn_conv_bwd", dc_qkv, qkv_raw, W['gdn_conv_w'], 256, dx_dtype=bf16)

    (dy, dr1, dk21, dv1, dg_), (G['rwkv_ln_w'], G['rwkv_ln_b'], G['rwkv_r_k']) = _pw_bwd(
        "rwkv_post_bwd", _rwkv_post_fn, [y, r, k2, v, g], post_consts, [(dya_in,)], 256)
    dr2, dlw, dk22, dv2, da_, db_, *G['_arrived'] = _scan_bwd(
        "wkv_bwd", _wkv_block, wkv_in, dy, zs_wkv, WKV_CHUNK, side=None if late is None else (late['slabs'](G), True))
    (dps,), rw_grads = _pw_bwd(
        "rwkv_prep_bwd", _rwkv_prep_fn, [p_rwkv], rw_consts,
        [(dr1, dr2), (dlw,), (dk21, dk22), (dv1, dv2), (da_,), (db_,), (dg_,)], 256, conv_w=mixw)
    G['rwkv_w0'], dw2p, G['rwkv_a0'], da2p, G['rwkv_g2'], G['rwkv_k_k'], G['rwkv_k_a'] = rw_grads
    G['rwkv_w2'], G['rwkv_a2'] = dw2p[:64], da2p[64:]
    dp_rwkv, dmixw = _conv_bwd("shift_bwd", dps, p_rwkv, mixw, 256, dx_dtype=bf16)
    G['rwkv_mu'] = dmixw[0] - dmixw[1]

    dp = jnp.concatenate([dp_rwkv, dqkv_raw, dz, dgates, dab], axis=1)
    du = _mm(dp, wp, 'nt', "d_u")
    G['w_in_pad'] = _mm(u, dp, 'tn', "g_w_in")
    (dx,), (G['norm1_g'],) = _pw_bwd("norm1_bwd", _rms_fn, [x], [g1], [(du,)], 256, add_to_first=dx1)
    return loss, dx, G


def _pad_w_in(w):
    return jnp.concatenate([w[:, :OFF_GATES], w[:, OFF_GATES + 8:], w[:, OFF_GATES:OFF_GATES + 8],
                            jnp.zeros((w.shape[0], W_AB - 8), w.dtype)], axis=1)


def _unpad_w_in(wp):
    return jnp.concatenate([wp[:, :OFF_GATES], wp[:, OFF_AB:OFF_AB + 8], wp[:, OFF_GATES:OFF_AB]], axis=1)


BIG = ('w_in', 'rwkv_proj', 'gdn_proj', 'w_out', 'ffn_up', 'ffn_down')
SMALL_SHARDED = ('rwkv_w2', 'rwkv_a2', 'rwkv_g2', 'gdn_conv_w', 'ffn_conv_w')


def _rows128(shape):
    n = 1
    for d in shape:
        n *= d
    return -(-n // LANES)


def _pack128(arrays):
    parts = []
    for a in arrays:
        flat = a.reshape(-1)
        rows = _rows128(a.shape)
        parts.append(jnp.pad(flat, (0, rows * LANES - flat.shape[0])).reshape(rows, LANES))
    buf = jnp.concatenate(parts, axis=0)
    return jnp.pad(buf, ((0, -buf.shape[0] % HALO), (0, 0)))


def _unpack128(buf, shapes):
    out, off = [], 0
    for s in shapes:
        rows, n = _rows128(s), 1
        for d in s:
            n *= d
        out.append(buf[off:off + rows].reshape(-1)[:n].reshape(s))
        off += rows
    return out


def _row_tile(r, c):
    best = None
    for d in range(HALO, r + 1, HALO):
        if r % d == 0 and d * c * 4 <= TILE_BYTES:
            best = d
    return best if best is not None else r


def _xy_exchange(name, bufs, scatter):
    n = len(bufs)

    def body(*refs):
        start, finish = _xy_copies(refs[:n], refs[n:2 * n], *refs[2 * n:], scatter)
        start()
        finish()

    return pl.pallas_call(
        body, in_specs=[pl.BlockSpec(memory_space=pl.ANY)] * n, out_specs=[pl.BlockSpec(memory_space=pl.ANY)] * n,
        out_shape=_xy_out_shapes(bufs, scatter), scratch_shapes=_xy_sems(n), name=name)(*bufs)


def _sibling_exchange(name, bufs):
    n = len(bufs)

    def body(*refs):
        in_refs, out_refs, send_sems, recv_sems = refs[:n], refs[n:2 * n], refs[2 * n], refs[2 * n + 1]
        x, y, c = lax.axis_index("x"), lax.axis_index("y"), lax.axis_index("c")
        copies = [pltpu.make_async_remote_copy(
            src_ref=in_refs[a], dst_ref=out_refs[a], send_sem=send_sems.at[a], recv_sem=recv_sems.at[a],
            device_id=(x, y, 1 - c), device_id_type=pl.DeviceIdType.MESH) for a in range(n)]
        for cp in copies:
            cp.start()
        for cp in copies:
            cp.wait()

    return pl.pallas_call(
        body, in_specs=[pl.BlockSpec(memory_space=pl.ANY)] * n, out_specs=[pl.BlockSpec(memory_space=pl.ANY)] * n,
        out_shape=[jax.ShapeDtypeStruct(b.shape, b.dtype) for b in bufs],
        scratch_shapes=[pltpu.SemaphoreType.DMA((n,)), pltpu.SemaphoreType.DMA((n,))], name=name)(*bufs)


def _sum_slots(name, buf):
    _, R, L = buf.shape
    tr = _row_tile(R, L)

    def body(b_ref, o_ref):
        o_ref[...] = ((b_ref[0] + b_ref[1]) + b_ref[2]) + b_ref[3]

    return pl.pallas_call(
        body, grid=(R // tr,),
        in_specs=[pl.BlockSpec((N_POS, tr, L), lambda i: (0, i, 0))],
        out_specs=pl.BlockSpec((tr, L), lambda i: (i, 0)),
        out_shape=jax.ShapeDtypeStruct((R, L), f32), name=name,
        compiler_params=_params("parallel"))(buf)


def _adamw(name, w, ga, gb, m, v):
    R, L = w.shape
    tr = _row_tile(R, L)
    c1 = 1.0 / (1.0 - ADAM_B1 ** ADAM_STEP)
    c2 = 1.0 / (1.0 - ADAM_B2 ** ADAM_STEP)

    def body(w_ref, ga_ref, gb_ref, m_ref, v_ref, g_out, d_out, m_out, v_out):
        g = ga_ref[...] + gb_ref[...]
        m_new = ADAM_B1 * m_ref[...] + (1.0 - ADAM_B1) * g
        v_new = ADAM_B2 * v_ref[...] + (1.0 - ADAM_B2) * (g * g)
        g_out[...] = g
        m_out[...] = m_new
        v_out[...] = v_new
        d_out[...] = -ADAM_LR * ((m_new * c1) / (jnp.sqrt(v_new * c2) + ADAM_EPS) + ADAM_WD * w_ref[...])

    spec = pl.BlockSpec((tr, L), lambda i: (i, 0))
    return pl.pallas_call(
        body, grid=(R // tr,), in_specs=[spec] * 5, out_specs=[spec] * 4,
        out_shape=[jax.ShapeDtypeStruct((R, L), f32)] * 4, name=name,
        compiler_params=_params("parallel"))(w, ga, gb, m, v)


def _step(x, loss_target, P, M, V):
    shapes = {n: tuple(P[n].shape) for n in WEIGHTS}
    sh_shapes = [shapes[n] for n in SMALL_SHARDED]
    packed = SMALL_SHARDED + SMALL
    late_names = BIG[1:]

    def whole(n, g):
        return g.reshape(-1, g.shape[2]) if n in ROW_SHARDED else jnp.concatenate([g[j] for j in range(N_POS)], axis=1)

    def slabs(G, n):
        r, c = shapes[n]
        return G[n].reshape(N_POS, r, c) if n in ROW_SHARDED else G[n].reshape(r, N_POS, c).transpose(1, 0, 2)

    g_w_in, g_small = _xy_exchange("gather_w_in", [P['w_in'].astype(bf16), _pack128([P[n] for n in SMALL_SHARDED])],
                                   scatter=False)
    W = {n: P[n] for n in SMALL}
    W['w_in_pad'] = _pad_w_in(whole('w_in', g_w_in))
    per_pos = [_unpack128(g_small[j], sh_shapes) for j in range(N_POS)]
    for q, n in enumerate(SMALL_SHARDED):
        W[n] = jnp.concatenate([per_pos[j][q] for j in range(N_POS)], axis=1)
    late = dict(shards=[P[n].astype(bf16) for n in late_names],
                assemble=lambda gathered: {n: whole(n, g) for n, g in zip(late_names, gathered)},
                slabs=lambda G: [slabs(G, n) for n in late_names])

    loss_rows, dx, G = _local_step(x, loss_target, W, late)
    arrived_late = G.pop('_arrived')
    G['w_in'] = _unpad_w_in(G.pop('w_in_pad'))

    small_slabs = jnp.stack([_pack128([slabs(G, n)[j] for n in SMALL_SHARDED] + [G[n] for n in SMALL]) for j in range(N_POS)])
    arrived_w_in, arrived_small = _xy_exchange("scatter_w_in", [slabs(G, 'w_in'), small_slabs], scatter=True)
    contributions = [arrived_w_in] + list(arrived_late) + [arrived_small]
    tags = list(BIG) + ['small']
    plane = [_sum_slots("sum_" + t, cbuf) for t, cbuf in zip(tags, contributions)]
    sibling = _sibling_exchange("sibling_grads", plane)

    out = {}
    names4 = ('grad', 'delta', 'new_m', 'new_v')
    for q, n in enumerate(BIG):
        for tag, t in zip(names4, _adamw("adamw_" + n, P[n], plane[q], sibling[q], M[n], V[n])):
            out[tag + '_' + n] = t
    small_out = _adamw("adamw_small", _pack128([P[n] for n in packed]), plane[-1], sibling[-1],
                       _pack128([M[n] for n in packed]), _pack128([V[n] for n in packed]))
    for tag, buf in zip(names4, small_out):
        for n, t in zip(packed, _unpack128(buf, [shapes[n] for n in packed])):
            out[tag + '_' + n] = t
    loss = lax.psum(loss_rows[0, 0], ("x", "y", "c"))
    return loss, dx, out


def kernel(x, norm1_g, w_in, rwkv_mu, rwkv_w0, rwkv_w2, rwkv_a0, rwkv_a2, rwkv_g2, rwkv_k_k, rwkv_k_a, rwkv_r_k, rwkv_ln_w, rwkv_ln_b, rwkv_proj, gdn_conv_w, gdn_a_log, gdn_dt_bias, gdn_norm_w, gdn_proj, w_out, norm2_g, ffn_up, ffn_conv_w, ffn_down, final_g, loss_target, m_norm1_g, m_w_in, m_rwkv_mu, m_rwkv_w0, m_rwkv_w2, m_rwkv_a0, m_rwkv_a2, m_rwkv_g2, m_rwkv_k_k, m_rwkv_k_a, m_rwkv_r_k, m_rwkv_ln_w, m_rwkv_ln_b, m_rwkv_proj, m_gdn_conv_w, m_gdn_a_log, m_gdn_dt_bias, m_gdn_norm_w, m_gdn_proj, m_w_out, m_norm2_g, m_ffn_up, m_ffn_conv_w, m_ffn_down, m_final_g, v_norm1_g, v_w_in, v_rwkv_mu, v_rwkv_w0, v_rwkv_w2, v_rwkv_a0, v_rwkv_a2, v_rwkv_g2, v_rwkv_k_k, v_rwkv_k_a, v_rwkv_r_k, v_rwkv_ln_w, v_rwkv_ln_b, v_rwkv_proj, v_gdn_conv_w, v_gdn_a_log, v_gdn_dt_bias, v_gdn_norm_w, v_gdn_proj, v_w_out, v_norm2_g, v_ffn_up, v_ffn_conv_w, v_ffn_down, v_final_g):
    weights = (norm1_g, w_in, rwkv_mu, rwkv_w0, rwkv_w2, rwkv_a0, rwkv_a2, rwkv_g2, rwkv_k_k, rwkv_k_a, rwkv_r_k, rwkv_ln_w,
               rwkv_ln_b, rwkv_proj, gdn_conv_w, gdn_a_log, gdn_dt_bias, gdn_norm_w, gdn_proj, w_out, norm2_g, ffn_up,
               ffn_conv_w, ffn_down, final_g)
    m_in = (m_norm1_g, m_w_in, m_rwkv_mu, m_rwkv_w0, m_rwkv_w2, m_rwkv_a0, m_rwkv_a2, m_rwkv_g2, m_rwkv_k_k, m_rwkv_k_a,
            m_rwkv_r_k, m_rwkv_ln_w, m_rwkv_ln_b, m_rwkv_proj, m_gdn_conv_w, m_gdn_a_log, m_gdn_dt_bias, m_gdn_norm_w,
            m_gdn_proj, m_w_out, m_norm2_g, m_ffn_up, m_ffn_conv_w, m_ffn_down, m_final_g)
    v_in = (v_norm1_g, v_w_in, v_rwkv_mu, v_rwkv_w0, v_rwkv_w2, v_rwkv_a0, v_rwkv_a2, v_rwkv_g2, v_rwkv_k_k, v_rwkv_k_a,
            v_rwkv_r_k, v_rwkv_ln_w, v_rwkv_ln_b, v_rwkv_proj, v_gdn_conv_w, v_gdn_a_log, v_gdn_dt_bias, v_gdn_norm_w,
            v_gdn_proj, v_w_out, v_norm2_g, v_ffn_up, v_ffn_conv_w, v_ffn_down, v_final_g)
    drop = lambda n, a: a if n == 'final_g' else a[0]
    P = {n: drop(n, a) for n, a in zip(WEIGHTS, weights)}
    M = {n: drop(n, a) for n, a in zip(WEIGHTS, m_in)}
    V = {n: drop(n, a) for n, a in zip(WEIGHTS, v_in)}
    loss, dx, out = _step(x[0], loss_target[0], P, M, V)
    lift = lambda n, a: a if n == 'final_g' else a[None]
    res = [loss, dx[None]]
    for tag in ('grad', 'delta', 'new_m', 'new_v'):
        res += [lift(n, out[tag + '_' + n]) for n in WEIGHTS]
    return tuple(res)
```

```python
import functools

import jax
import jax.numpy as jnp
from jax import lax
from jax.experimental import pallas as pl
from jax.experimental.pallas import tpu as pltpu

f32 = jnp.float32
bf16 = jnp.bfloat16
HI = lax.Precision.HIGHEST

D_MODEL = 1024
RWKV_HEADS, RWKV_HD, RWKV_W = 8, 64, 512
GDN_HEADS, GDN_HD, GDN_W = 4, 128, 512
FFN_H = 2816
NORM_EPS, L2_EPS, GN_EPS = 1e-6, 1e-6, 64e-5
W_AB = 256
OFF_QKV, OFF_Z, OFF_GATES, OFF_AB = 1792, 3328, 3840, 5888
W_IN_PAD = OFF_AB + W_AB
WKV_CHUNK = 64
GDN_CHUNK = 128
HALO = 8
LANES = 128
TILE_BYTES = 1 << 20
VMEM_LIMIT = 56 * 1024 * 1024

ADAM_LR, ADAM_B1, ADAM_B2, ADAM_EPS, ADAM_WD, ADAM_STEP = 0.001, 0.9, 0.999, 1e-08, 0.01, 10

ROW_SHARDED = ('w_out', 'ffn_down')
SMALL = ('norm1_g', 'rwkv_mu', 'rwkv_w0', 'rwkv_a0', 'rwkv_k_k', 'rwkv_k_a', 'rwkv_r_k', 'rwkv_ln_w', 'rwkv_ln_b',
         'gdn_a_log', 'gdn_dt_bias', 'gdn_norm_w', 'norm2_g', 'final_g')
WEIGHTS = ('norm1_g', 'w_in', 'rwkv_mu', 'rwkv_w0', 'rwkv_w2', 'rwkv_a0', 'rwkv_a2', 'rwkv_g2', 'rwkv_k_k', 'rwkv_k_a',
           'rwkv_r_k', 'rwkv_ln_w', 'rwkv_ln_b', 'rwkv_proj', 'gdn_conv_w', 'gdn_a_log', 'gdn_dt_bias', 'gdn_norm_w',
           'gdn_proj', 'w_out', 'norm2_g', 'ffn_up', 'ffn_conv_w', 'ffn_down', 'final_g')


def _params(*sem):
    return pltpu.CompilerParams(dimension_semantics=sem, vmem_limit_bytes=VMEM_LIMIT)


def _tile(n, limit):
    if n <= limit:
        return n
    best = None
    for d in range(128, limit + 1, 128):
        if n % d == 0:
            best = d
    if best is None:
        raise ValueError(f"no tile for {n} under {limit}")
    return best


MM_BLOCK_BYTES = 4 << 20


def _mm(a, b, mode, name, add=None, out_dtype=f32):
    if mode == 'nn':
        (M, K), N = a.shape, b.shape[1]
    elif mode == 'nt':
        (M, K), N = a.shape, b.shape[0]
    else:
        (K, M), N = a.shape, b.shape[1]
    tm = _tile(M, 512)
    tk = _tile(K, min(2816, MM_BLOCK_BYTES // (tm * a.dtype.itemsize)))
    tn = _tile(N, max(128, min(MM_BLOCK_BYTES // (tk * b.dtype.itemsize), MM_BLOCK_BYTES // (tm * 4)) // 128 * 128))
    nk = K // tk
    dn = {'nn': (((1,), (0,)), ((), ())), 'nt': (((1,), (1,)), ((), ())), 'tn': (((0,), (0,)), ((), ()))}[mode]

    def body(a_ref, b_ref, *rest):
        o_ref = rest[1] if add is not None else rest[0]
        acc = lax.dot_general(a_ref[...].astype(bf16), b_ref[...].astype(bf16), dn, preferred_element_type=f32)
        if nk == 1:
            o_ref[...] = (acc + rest[0][...] if add is not None else acc).astype(out_dtype)
            return
        acc_ref = rest[-1]
        k = pl.program_id(2)

        @pl.when(k == 0)
        def _():
            acc_ref[...] = acc + rest[0][...] if add is not None else acc

        @pl.when(k > 0)
        def _():
            acc_ref[...] += acc

        @pl.when(k == nk - 1)
        def _():
            o_ref[...] = acc_ref[...].astype(out_dtype)

    a_spec = (pl.BlockSpec((tk, tm), lambda i, j, k: (k, i)) if mode == 'tn'
              else pl.BlockSpec((tm, tk), lambda i, j, k: (i, k)))
    b_spec = (pl.BlockSpec((tn, tk), lambda i, j, k: (j, k)) if mode == 'nt'
              else pl.BlockSpec((tk, tn), lambda i, j, k: (k, j)))
    o_spec = pl.BlockSpec((tm, tn), lambda i, j, k: (i, j))
    ins, specs = [a, b], [a_spec, b_spec]
    if add is not None:
        ins.append(add)
        specs.append(o_spec)
    return pl.pallas_call(
        body, grid=(M // tm, N // tn, nk), in_specs=specs, out_specs=o_spec,
        out_shape=jax.ShapeDtypeStruct((M, N), out_dtype),
        scratch_shapes=[pltpu.VMEM((tm, tn), f32)] if nk > 1 else [], name=name,
        compiler_params=_params("parallel", "parallel", "arbitrary"))(*ins)


def _shift_down(cur, prev, s):
    if s == 0:
        return cur
    ext = jnp.concatenate([prev, cur], axis=0)
    return pltpu.roll(ext, s, 0)[HALO:]


def _shift_up(cur, nxt, s):
    if s == 0:
        return cur
    ext = jnp.concatenate([cur, nxt], axis=0)
    return pltpu.roll(ext, ext.shape[0] - s, 0)[:cur.shape[0]]


def _conv_apply(cur, prev, w_ref):
    taps = w_ref.shape[0]
    out = None
    for i in range(taps):
        term = _shift_down(cur, prev, taps - 1 - i) * w_ref[pl.ds(i, 1), :]
        out = term if out is None else out + term
    return out


def _row_spec(tm, w):
    return pl.BlockSpec((tm, w), lambda i: (i, 0))


def _prev_spec(tm, w):
    return pl.BlockSpec((HALO, w), lambda i: (jnp.maximum(i * (tm // HALO) - 1, 0), 0))


def _next_spec(tm, w, T):
    return pl.BlockSpec((HALO, w), lambda i: (jnp.minimum((i + 1) * (tm // HALO), T // HALO - 1), 0))


def _full_spec(shape):
    return pl.BlockSpec(shape, lambda i: (0,) * len(shape))


def _pw_fwd(name, fn, rows, consts, out_widths, tm, conv_w=None, out_dtype=f32):
    T = rows[0].shape[0]
    nr, nc = len(rows), len(consts)

    def body(*refs):
        i = pl.program_id(0)
        vals = [r[...] for r in refs[:nr]]
        p = nr
        if conv_w is not None:
            prev = jnp.where(i > 0, refs[p][...], 0.0)
            vals[0] = _conv_apply(vals[0], prev, refs[p + 1])
            p += 2
        cvals = [r[...] for r in refs[p:p + nc]]
        outs = fn(*vals, *cvals)
        for o_ref, o in zip(refs[p + nc:], outs):
            o_ref[...] = o.astype(out_dtype)

    ins = list(rows)
    specs = [_row_spec(tm, r.shape[1]) for r in rows]
    if conv_w is not None:
        ins += [rows[0], conv_w]
        specs += [_prev_spec(tm, rows[0].shape[1]), _full_spec(conv_w.shape)]
    ins += list(consts)
    specs += [_full_spec(c.shape) for c in consts]
    outs = pl.pallas_call(
        body, grid=(T // tm,), in_specs=specs,
        out_specs=[_row_spec(tm, w) for w in out_widths],
        out_shape=[jax.ShapeDtypeStruct((T, w), out_dtype) for w in out_widths], name=name,
        compiler_params=_params("parallel"))(*ins)
    return outs


def _pw_bwd(name, fn, rows, consts, cots, tm, conv_w=None, add_to_first=None, row_dtypes=None):
    T = rows[0].shape[0]
    nr, nc = len(rows), len(consts)
    flat_cots = [c for grp in cots for c in grp]
    row_dtypes = row_dtypes or [f32] * nr

    def body(*refs):
        i = pl.program_id(0)
        vals = [r[...] for r in refs[:nr]]
        p = nr
        if conv_w is not None:
            prev = jnp.where(i > 0, refs[p][...], 0.0)
            vals[0] = _conv_apply(vals[0], prev, refs[p + 1])
            p += 2
        cvals = [r[...] for r in refs[p:p + nc]]
        p += nc
        cot_vals = []
        for grp in cots:
            acc = refs[p][...]
            for q in range(1, len(grp)):
                acc = acc + refs[p + q][...]
            p += len(grp)
            cot_vals.append(acc)
        extra = None
        if add_to_first is not None:
            extra = refs[p][...]
            p += 1
        _, vjp = jax.vjp(fn, *vals, *cvals)
        grads = vjp(tuple(cot_vals))
        row_out = refs[p:p + nr]
        const_out = refs[p + nr:]
        for q in range(nr):
            g = grads[q]
            if q == 0 and extra is not None:
                g = g + extra
            row_out[q][...] = g.astype(row_dtypes[q])

        @pl.when(i == 0)
        def _():
            for q in range(nc):
                const_out[q][...] = grads[nr + q]

        @pl.when(i > 0)
        def _():
            for q in range(nc):
                const_out[q][...] += grads[nr + q]

    ins = list(rows)
    specs = [_row_spec(tm, r.shape[1]) for r in rows]
    if conv_w is not None:
        ins += [rows[0], conv_w]
        specs += [_prev_spec(tm, rows[0].shape[1]), _full_spec(conv_w.shape)]
    ins += list(consts)
    specs += [_full_spec(c.shape) for c in consts]
    ins += flat_cots
    specs += [_row_spec(tm, c.shape[1]) for c in flat_cots]
    if add_to_first is not None:
        ins.append(add_to_first)
        specs.append(_row_spec(tm, add_to_first.shape[1]))
    out_shapes = ([jax.ShapeDtypeStruct(r.shape, d) for r, d in zip(rows, row_dtypes)]
                  + [jax.ShapeDtypeStruct(c.shape, f32) for c in consts])
    out_specs = [_row_spec(tm, r.shape[1]) for r in rows] + [_full_spec(c.shape) for c in consts]
    outs = pl.pallas_call(
        body, grid=(T // tm,), in_specs=specs, out_specs=out_specs, out_shape=out_shapes, name=name,
        compiler_params=_params("arbitrary"))(*ins)
    return list(outs[:nr]), list(outs[nr:])


def _conv_bwd(name, dc, x, w, tm, dx_dtype=f32):
    T, W = x.shape
    taps = w.shape[0]
    nblk = T // tm

    def body(dc_ref, dcn_ref, x_ref, xp_ref, w_ref, dx_ref, dw_ref):
        i = pl.program_id(0)
        dcv, xv = dc_ref[...], x_ref[...]
        nxt = jnp.where(i < nblk - 1, dcn_ref[...], 0.0)
        prev = jnp.where(i > 0, xp_ref[...], 0.0)

        @pl.when(i == 0)
        def _():
            dw_ref[...] = jnp.zeros_like(dw_ref)

        dx = None
        for k in range(taps):
            s = taps - 1 - k
            term = _shift_up(dcv, nxt, s) * w_ref[pl.ds(k, 1), :]
            dx = term if dx is None else dx + term
            dw_ref[pl.ds(k, 1), :] += jnp.sum(dcv * _shift_down(xv, prev, s), axis=0, keepdims=True)
        dx_ref[...] = dx.astype(dx_dtype)

    return pl.pallas_call(
        body, grid=(nblk,),
        in_specs=[_row_spec(tm, W), _next_spec(tm, W, T), _row_spec(tm, W), _prev_spec(tm, W), _full_spec(w.shape)],
        out_specs=[_row_spec(tm, W), _full_spec(w.shape)],
        out_shape=[jax.ShapeDtypeStruct((T, W), dx_dtype), jax.ShapeDtypeStruct(w.shape, f32)], name=name,
        compiler_params=_params("arbitrary"))(dc, dc, x, x, w)


def _sigmoid(x):
    return 1.0 / (1.0 + jnp.exp(-x))


def _softplus(x):
    return jnp.maximum(x, 0.0) + jnp.log(1.0 + jnp.exp(jnp.minimum(x, -x)))


def _seg_sum_impl(x, seg):
    w = x.shape[-1]
    r = lax.broadcasted_iota(jnp.int32, (w, w), 0) // seg
    c = lax.broadcasted_iota(jnp.int32, (w, w), 1) // seg
    ones = (r == c).astype(bf16)
    hi = x.astype(bf16)
    lo = (x - hi.astype(f32)).astype(bf16)
    return (jnp.dot(hi, ones, preferred_element_type=f32) + jnp.dot(lo, ones, preferred_element_type=f32))


@functools.partial(jax.custom_vjp, nondiff_argnums=(1,))
def _seg_sum(x, seg):
    return _seg_sum_impl(x, seg)


_seg_sum.defvjp(lambda x, seg: (_seg_sum_impl(x, seg), None), lambda seg, _, g: (_seg_sum_impl(g, seg),))


def _rms(x, g):
    return x * lax.rsqrt(jnp.mean(x * x, axis=-1, keepdims=True) + NORM_EPS) * g


def _rms_fn(x, g):
    return (_rms(x, g),)


def _loss_rows(x2, tgt, g):
    e = _rms(x2, g) - tgt
    return 0.5 * jnp.sum(e * e, axis=-1, keepdims=True) * (1.0 / D_MODEL)


def _rwkv_prep_fn(ps, w0, w2p, a0, a2p, g2, k_k, k_a):
    r, k, v = ps[:, 0:512], ps[:, 512:1024], ps[:, 1024:1536]
    wa, gl = ps[:, 1536:1664], ps[:, 1664:1792]
    z = w0 + jnp.dot(jnp.tanh(wa), w2p, precision=HI, preferred_element_type=f32)
    w_log = -_softplus(-z) - 0.5
    lw = -jnp.exp(w_log)
    a = _sigmoid(a0 + jnp.dot(wa, a2p, precision=HI, preferred_element_type=f32))
    g = jnp.dot(_sigmoid(gl), g2, precision=HI, preferred_element_type=f32)
    kx = k * k_k
    kk = kx * lax.rsqrt(_seg_sum(kx * kx, RWKV_HD) + L2_EPS)
    k2 = k * (1.0 + (a - 1.0) * k_a)
    return r, lw, k2, v, -kk, kk * a, g


def _rwkv_post_fn(y, r, k2, v, g, ln_w, ln_b, rk):
    mean = _seg_sum(y, RWKV_HD) * (1.0 / RWKV_HD)
    yc = y - mean
    var = _seg_sum(yc * yc, RWKV_HD) * (1.0 / RWKV_HD)
    yn = yc * lax.rsqrt(var + GN_EPS) * ln_w + ln_b
    bonus = _seg_sum(r * k2 * rk, RWKV_HD) * v
    return ((yn + bonus) * g,)


def _gdn_prep_fn(c, ab, al_p, dt_p):
    s = c * _sigmoid(c)
    q, k, v = s[:, 0:512], s[:, 512:1024], s[:, 1024:1536]
    q = q * lax.rsqrt(_seg_sum(q * q, GDN_HD) + L2_EPS) * (GDN_HD ** -0.5)
    k = k * lax.rsqrt(_seg_sum(k * k, GDN_HD) + L2_EPS)
    lane = lax.broadcasted_iota(jnp.int32, ab.shape, 1)
    gpart = -jnp.exp(al_p) * _softplus(ab + dt_p)
    gbeta = jnp.where(lane < GDN_HEADS, gpart, jnp.where(lane < 2 * GDN_HEADS, _sigmoid(ab), 0.0))
    return q, k, v, gbeta


def _gdn_post_fn(o, z, nw):
    ms = _seg_sum(o * o, GDN_HD) * (1.0 / GDN_HD)
    return (o * lax.rsqrt(ms + NORM_EPS) * nw * (z * _sigmoid(z)),)


def _mix_fn(gates, ya, yb):
    return (_sigmoid(gates[:, :D_MODEL]) * ya + _sigmoid(gates[:, D_MODEL:]) * yb,)


def _ffn_fn(c):
    hg, hu = c[:, :FFN_H], c[:, FFN_H:]
    return (hg * _sigmoid(hg) * hu,)


N_POS = 4


def _xy_out_shapes(bufs, scatter):
    return [jax.ShapeDtypeStruct((N_POS,) + tuple(b.shape[1:] if scatter else b.shape), b.dtype) for b in bufs]


def _xy_sems(n):
    return [pltpu.SemaphoreType.DMA((3 * n,)), pltpu.SemaphoreType.DMA((3 * n,)), pltpu.SemaphoreType.DMA((n,))]


def _xy_copies(in_refs, out_refs, send_sems, recv_sems, local_sems, scatter):
    n = len(in_refs)

    def plan(arriving):
        x, y, c = lax.axis_index("x"), lax.axis_index("y"), lax.axis_index("c")
        me = 2 * x + y
        peers = [(1 - x, y), (x, 1 - y), (1 - x, 1 - y)]

        def copy(a, k, src, slot, peer):
            return pltpu.make_async_remote_copy(
                src_ref=src, dst_ref=out_refs[a].at[slot], send_sem=send_sems.at[3 * a + k],
                recv_sem=recv_sems.at[3 * a + k], device_id=(peer[0], peer[1], c), device_id_type=pl.DeviceIdType.MESH)

        if arriving:
            return [copy(a, k, in_refs[a].at[me] if scatter else in_refs[a], 2 * peer[0] + peer[1], peer)
                    for a in range(n) for k, peer in enumerate(peers)]
        own = [pltpu.make_async_copy(in_refs[a].at[me] if scatter else in_refs[a], out_refs[a].at[me], local_sems.at[a])
               for a in range(n)]
        sends = [copy(a, k, in_refs[a].at[2 * peer[0] + peer[1]] if scatter else in_refs[a], me, peer)
                 for a in range(n) for k, peer in enumerate(peers)]
        return own + sends

    def start():
        for cp in plan(False):
            cp.start()

    def finish():
        for cp in plan(True):
            cp.wait_recv()
        outgoing = plan(False)
        for cp in outgoing[n:]:
            cp.wait_send()
        for cp in outgoing[:n]:
            cp.wait()

    return start, finish


_NN, _NT, _TN = 'hcs,hsd->hcd', 'hcd,hsd->hcs', 'hcd,hce->hde'


def _lo(spec, a, b):
    return jnp.einsum(spec, a.astype(bf16), b.astype(bf16), preferred_element_type=f32)


@jax.custom_vjp
def _bmm(a, b):
    return _lo(_NN, a, b)


_bmm.defvjp(lambda a, b: (_lo(_NN, a, b), (a, b)), lambda ab, g: (_lo(_NT, g, ab[1]), _lo(_TN, ab[0], g)))


@jax.custom_vjp
def _bmm_nt(a, b):
    return _lo(_NT, a, b)


_bmm_nt.defvjp(lambda a, b: (_lo(_NT, a, b), (a, b)), lambda ab, g: (_lo(_NN, g, ab[1]), _lo(_TN, g, ab[0])))


@jax.custom_vjp
def _bmm_tn(a, b):
    return _lo(_TN, a, b)


_bmm_tn.defvjp(lambda a, b: (_lo(_TN, a, b), (a, b)), lambda ab, g: (_lo(_NT, ab[1], g), _lo(_NN, ab[0], g)))


def _masks(H, C):
    row = lax.broadcasted_iota(jnp.int32, (H, C, C), 1)
    col = lax.broadcasted_iota(jnp.int32, (H, C, C), 2)
    return row, col


def _tri_inv_impl(L):
    H, C, _ = L.shape
    row, col = _masks(H, C)
    eye = (row == col).astype(f32)
    base = 16
    same = (row // base) == (col // base)
    Ld = jnp.where(same, L, 0.0)
    X = -Ld
    inv = eye + X
    for _ in range(3):
        X = _bmm(X, X)
        inv = _bmm(inv, eye + X)
    if C == base:
        return inv
    N = _bmm(inv, L - Ld)
    out = eye - N
    levels = C // base
    P = N
    span = 2
    while span < levels:
        P = _bmm(P, P)
        out = _bmm(out, eye + P)
        span *= 2
    return _bmm(out, inv)


@jax.custom_vjp
def _tri_inv(L):
    return _tri_inv_impl(L)


def _tri_inv_fwd(L):
    T = _tri_inv_impl(L)
    return T, T


def _tri_inv_bwd(T, dT):
    return (-_bmm_nt(_bmm_tn(T, dT), T),)


_tri_inv.defvjp(_tri_inv_fwd, _tri_inv_bwd)


def _cumsum_impl(x, reverse):
    C = x.shape[1]
    row = lax.broadcasted_iota(jnp.int32, x.shape, 1)
    s = 1
    while s < C:
        if reverse:
            x = x + jnp.where(row < C - s, pltpu.roll(x, C - s, 1), 0.0)
        else:
            x = x + jnp.where(row >= s, pltpu.roll(x, s, 1), 0.0)
        s *= 2
    return x


@jax.custom_vjp
def _cumsum(x):
    return _cumsum_impl(x, False)


_cumsum.defvjp(lambda x: (_cumsum_impl(x, False), None), lambda _, g: (_cumsum_impl(g, True),))


def _wkv_chunk(Z, r, lw, k, v, a, b):
    H, C, D = r.shape
    row, col = _masks(H, C)
    incl, strict = row >= col, row > col
    cw = _cumsum(lw)
    cwp = cw - lw
    cwl = jnp.sum(lw, axis=1, keepdims=True)
    en = jnp.exp(-cw)
    at, rt, bt, kt = a * jnp.exp(cwp), r * jnp.exp(cw), b * en, k * en
    Lab = jnp.where(strict, _bmm_nt(at, bt), 0.0)
    Lak = jnp.where(strict, _bmm_nt(at, kt), 0.0)
    Tm = _tri_inv(-Lab)
    U = _bmm(Tm, _bmm(at, Z) + _bmm(Lak, v))
    Rb = jnp.where(incl, _bmm_nt(rt, bt), 0.0)
    Rk = jnp.where(incl, _bmm_nt(rt, kt), 0.0)
    y = _bmm(rt, Z) + _bmm(Rb, U) + _bmm(Rk, v)
    ed = jnp.exp(cwl - cw)
    zdec = jnp.swapaxes(jnp.broadcast_to(jnp.exp(cwl), (H, Z.shape[2], D)), 1, 2)
    Z1 = Z * zdec + _bmm_tn(b * ed, U) + _bmm_tn(k * ed, v)
    return y, Z1


def _gdn_chunk(S, q, k, v, g, beta):
    H, C, D = q.shape
    row, col = _masks(H, C)
    incl, strict = row >= col, row > col
    gc = _cumsum(g)
    diff = gc - jnp.swapaxes(gc, 1, 2)
    decay = jnp.where(incl, jnp.exp(jnp.where(incl, diff, 0.0)), 0.0)
    gl = jnp.sum(g, axis=1, keepdims=True)
    kb, vb = k * beta, v * beta
    L = jnp.where(strict, _bmm_nt(kb, k) * decay, 0.0)
    Tm = _tri_inv(L)
    egc = jnp.exp(gc)
    u = _bmm(Tm, vb)
    wk = _bmm(Tm, kb * egc)
    attn = jnp.where(incl, _bmm_nt(q, k) * decay, 0.0)
    v_new = u - _bmm(wk, S)
    o = _bmm(q * egc, S) + _bmm(attn, v_new)
    S1 = S * jnp.exp(gl) + _bmm_tn(k * jnp.exp(gl - gc), v_new)
    return o, S1


def _wkv_block(Z, r, lw, k, v, a, b):
    lane = lax.broadcasted_iota(jnp.int32, (r.shape[0], 128), 1)
    low = lane < RWKV_HD

    def heads(t):
        out = []
        for p in range(RWKV_HEADS // 2):
            pair = t[:, 128 * p:128 * (p + 1)]
            out += [jnp.where(low, pair, 0.0), jnp.where(low, 0.0, pair)]
        return jnp.concatenate([t[None] for t in out], axis=0)

    y, Z1 = _wkv_chunk(Z, *[heads(t) for t in (r, lw, k, v, a, b)])
    return jnp.concatenate([y[2 * p] + y[2 * p + 1] for p in range(RWKV_HEADS // 2)], axis=1), Z1


def _gdn_block(S, q, k, v, gbeta):
    heads = lambda t: jnp.concatenate([t[None, :, GDN_HD * h:GDN_HD * (h + 1)] for h in range(GDN_HEADS)], axis=0)
    src = lax.broadcasted_iota(jnp.int32, (W_AB, 2 * GDN_W), 0)
    dst = lax.broadcasted_iota(jnp.int32, (W_AB, 2 * GDN_W), 1) // GDN_HD
    spread = jnp.dot(gbeta, (src == dst).astype(f32), precision=HI, preferred_element_type=f32)
    o, S1 = _gdn_chunk(S, heads(q), heads(k), heads(v), heads(spread[:, :GDN_W]), heads(spread[:, GDN_W:]))
    return jnp.concatenate([o[h] for h in range(GDN_HEADS)], axis=1), S1


def _scan_fwd(name, block_fn, ins, C, H, dh, w_out, side=None):
    T = ins[0].shape[0]
    n_in = len(ins)
    nblk = T // C
    n_side = 0 if side is None else len(side[0])

    def body(*refs):
        in_refs, refs = refs[:n_in], refs[n_in:]
        side_in, refs = refs[:n_side], refs[n_side:]
        y_ref, zs_ref, refs = refs[0], refs[1], refs[2:]
        side_out, refs = refs[:n_side], refs[n_side:]
        z_scr = refs[0]
        if side is not None:
            start, finish = _xy_copies(side_in, side_out, refs[1], refs[2], refs[3], side[1])
            pl.when(pl.program_id(0) == 0)(start)

        @pl.when(pl.program_id(0) == 0)
        def _():
            z_scr[...] = jnp.zeros_like(z_scr)

        Z = z_scr[...]
        zs_ref[0] = Z
        y, Z1 = block_fn(Z, *[r[...] for r in in_refs])
        y_ref[...] = y
        z_scr[...] = Z1
        if side is not None:
            pl.when(pl.program_id(0) == nblk - 1)(finish)

    side_bufs = [] if side is None else list(side[0])
    any_spec = pl.BlockSpec(memory_space=pl.ANY)
    return pl.pallas_call(
        body, grid=(nblk,),
        in_specs=[pl.BlockSpec((C, a.shape[1]), lambda i: (i, 0)) for a in ins] + [any_spec] * n_side,
        out_specs=[pl.BlockSpec((C, w_out), lambda i: (i, 0)), pl.BlockSpec((1, H, dh, dh), lambda i: (i, 0, 0, 0))]
        + [any_spec] * n_side,
        out_shape=[jax.ShapeDtypeStruct((T, w_out), f32), jax.ShapeDtypeStruct((T // C, H, dh, dh), f32)]
        + (_xy_out_shapes(side_bufs, side[1]) if side is not None else []),
        scratch_shapes=[pltpu.VMEM((H, dh, dh), f32)] + (_xy_sems(n_side) if side is not None else []), name=name,
        compiler_params=_params("arbitrary"))(*ins, *side_bufs)


def _scan_bwd(name, block_fn, ins, dy, zs, C, side=None):
    T = ins[0].shape[0]
    _, H, dh, _ = zs.shape
    n_in = len(ins)
    nblk = T // C
    n_side = 0 if side is None else len(side[0])

    def body(*refs):
        in_refs, dy_ref, zs_ref, refs = refs[:n_in], refs[n_in], refs[n_in + 1], refs[n_in + 2:]
        side_in, refs = refs[:n_side], refs[n_side:]
        out_refs, refs = refs[:n_in], refs[n_in:]
        side_out, refs = refs[:n_side], refs[n_side:]
        dz_scr = refs[0]
        if side is not None:
            start, finish = _xy_copies(side_in, side_out, refs[1], refs[2], refs[3], side[1])
            pl.when(pl.program_id(0) == 0)(start)

        @pl.when(pl.program_id(0) == 0)
        def _():
            dz_scr[...] = jnp.zeros_like(dz_scr)

        _, vjp = jax.vjp(block_fn, zs_ref[0], *[r[...] for r in in_refs])
        grads = vjp((dy_ref[...], dz_scr[...]))
        dz_scr[...] = grads[0]
        for o_ref, gval in zip(out_refs, grads[1:]):
            o_ref[...] = gval
        if side is not None:
            pl.when(pl.program_id(0) == nblk - 1)(finish)

    side_bufs = [] if side is None else list(side[0])
    any_spec = pl.BlockSpec(memory_space=pl.ANY)
    rev = lambda i: (nblk - 1 - i, 0)
    return pl.pallas_call(
        body, grid=(nblk,),
        in_specs=[pl.BlockSpec((C, a.shape[1]), rev) for a in ins]
        + [pl.BlockSpec((C, dy.shape[1]), rev), pl.BlockSpec((1, H, dh, dh), lambda i: (nblk - 1 - i, 0, 0, 0))]
        + [any_spec] * n_side,
        out_specs=[pl.BlockSpec((C, a.shape[1]), rev) for a in ins] + [any_spec] * n_side,
        out_shape=[jax.ShapeDtypeStruct(a.shape, f32) for a in ins]
        + (_xy_out_shapes(side_bufs, side[1]) if side is not None else []),
        scratch_shapes=[pltpu.VMEM((H, dh, dh), f32)] + (_xy_sems(n_side) if side is not None else []), name=name,
        compiler_params=_params("arbitrary"))(*ins, dy, zs, *side_bufs)


def _loss_call(x2, tgt, g, tm):
    T, W = x2.shape

    def body(x_ref, t_ref, g_ref, dx_ref, dg_ref, l_ref):
        i = pl.program_id(0)
        tv = t_ref[...]
        l, vjp = jax.vjp(lambda xv, gv: _loss_rows(xv, tv, gv), x_ref[...], g_ref[...])
        dx, dg = vjp(jnp.ones_like(l))
        dx_ref[...] = dx
        tot = jnp.zeros((1, 128), f32) + jnp.sum(l)

        @pl.when(i == 0)
        def _():
            dg_ref[...] = dg
            l_ref[...] = tot

        @pl.when(i > 0)
        def _():
            dg_ref[...] += dg
            l_ref[...] += tot

    return pl.pallas_call(
        body, grid=(T // tm,),
        in_specs=[_row_spec(tm, W), _row_spec(tm, W), _full_spec(g.shape)],
        out_specs=[_row_spec(tm, W), _full_spec(g.shape), _full_spec((1, 128))],
        out_shape=[jax.ShapeDtypeStruct((T, W), f32), jax.ShapeDtypeStruct(g.shape, f32),
                   jax.ShapeDtypeStruct((1, 128), f32)], name="loss_head",
        compiler_params=_params("arbitrary"))(x2, tgt, g)


def _local_step(x, tgt, W, late=None):
    row = lambda a: a.reshape(1, -1)
    wp = W['w_in_pad']
    w_rwkv, w_qkv, w_z = wp[:, :OFF_QKV], wp[:, OFF_QKV:OFF_Z], wp[:, OFF_Z:OFF_GATES]
    w_gates, w_ab = wp[:, OFF_GATES:OFF_AB], wp[:, OFF_AB:]
    mu = row(W['rwkv_mu'])
    mixw = jnp.concatenate([mu, 1.0 - mu], axis=0)
    zpad = jnp.zeros((64, RWKV_W), f32)
    w2p = jnp.concatenate([W['rwkv_w2'], zpad], axis=0)
    a2p = jnp.concatenate([zpad, W['rwkv_a2']], axis=0)
    rw_consts = [row(W['rwkv_w0']), w2p, row(W['rwkv_a0']), a2p, W['rwkv_g2'], row(W['rwkv_k_k']), row(W['rwkv_k_a'])]
    post_consts = [row(W['rwkv_ln_w']), row(W['rwkv_ln_b']), row(W['rwkv_r_k'])]
    pad4 = lambda a: jnp.pad(row(a), ((0, 0), (0, W_AB - GDN_HEADS)))
    gd_consts = [pad4(W['gdn_a_log']), pad4(W['gdn_dt_bias'])]
    nw_t = jnp.tile(row(W['gdn_norm_w']), (1, GDN_HEADS))
    g1, g2n, gf = row(W['norm1_g']), row(W['norm2_g']), row(W['final_g'])

    (u,) = _pw_fwd("norm1", _rms_fn, [x], [g1], [D_MODEL], 256, out_dtype=bf16)
    p_rwkv = _mm(u, w_rwkv, 'nn', "in_rwkv")
    qkv_raw = _mm(u, w_qkv, 'nn', "in_qkv")
    z = _mm(u, w_z, 'nn', "in_z")
    gates = _mm(u, w_gates, 'nn', "in_gates")
    ab = _mm(u, w_ab, 'nn', "in_ab")

    r, lw, k2, v, a_, b_, g = _pw_fwd("rwkv_prep", _rwkv_prep_fn, [p_rwkv], rw_consts, [RWKV_W] * 7, 256, conv_w=mixw)
    wkv_in = [r, lw, k2, v, a_, b_]
    y, zs_wkv, *gathered = _scan_fwd("wkv_fwd", _wkv_block, wkv_in, WKV_CHUNK, RWKV_HEADS, 2 * RWKV_HD, RWKV_W,
                                     side=None if late is None else (late['shards'], False))
    if late is not None:
        W = dict(W, **late['assemble'](gathered))
    (ya_in,) = _pw_fwd("rwkv_post", _rwkv_post_fn, [y, r, k2, v, g], post_consts, [RWKV_W], 256, out_dtype=bf16)
    ya = _mm(ya_in, W['rwkv_proj'], 'nn', "rwkv_proj")

    gq, gk, gv, gbeta = _pw_fwd("gdn_prep", _gdn_prep_fn, [qkv_raw, ab], gd_consts, [GDN_W] * 3 + [W_AB], 256,
                                conv_w=W['gdn_conv_w'])
    gdn_in = [gq, gk, gv, gbeta]
    o, zs_gdn = _scan_fwd("gdn_fwd", _gdn_block, gdn_in, GDN_CHUNK, GDN_HEADS, GDN_HD, GDN_W)
    (yb_in,) = _pw_fwd("gdn_post", _gdn_post_fn, [o, z], [nw_t], [GDN_W], 256, out_dtype=bf16)
    yb = _mm(yb_in, W['gdn_proj'], 'nn', "gdn_proj")

    (mixed,) = _pw_fwd("mix", _mix_fn, [gates, ya, yb], [], [D_MODEL], 256, out_dtype=bf16)
    x1 = _mm(mixed, W['w_out'], 'nn', "w_out", add=x)
    (u2,) = _pw_fwd("norm2", _rms_fn, [x1], [g2n], [D_MODEL], 256, out_dtype=bf16)
    h = _mm(u2, W['ffn_up'], 'nn', "ffn_up")
    (act,) = _pw_fwd("ffn_act", _ffn_fn, [h], [], [FFN_H], 128, conv_w=W['ffn_conv_w'], out_dtype=bf16)
    x2 = _mm(act, W['ffn_down'], 'nn', "ffn_down", add=x1)

    G = {}
    dx2, dgf, loss = _loss_call(x2, tgt, gf, 256)
    G['final_g'] = dgf
    dact = _mm(dx2, W['ffn_down'], 'nt', "d_act")
    G['ffn_down'] = _mm(act, dx2, 'tn', "g_ffn_down", out_dtype=bf16)
    (dc_ffn,), _ = _pw_bwd("ffn_act_bwd", _ffn_fn, [h], [], [(dact,)], 128, conv_w=W['ffn_conv_w'])
    dh, G['ffn_conv_w'] = _conv_bwd("ffn_conv_bwd", dc_ffn, h, W['ffn_conv_w'], 128, dx_dtype=bf16)
    du2 = _mm(dh, W['ffn_up'], 'nt', "d_u2")
    G['ffn_up'] = _mm(u2, dh, 'tn', "g_ffn_up", out_dtype=bf16)
    (dx1,), (G['norm2_g'],) = _pw_bwd("norm2_bwd", _rms_fn, [x1], [g2n], [(du2,)], 256, add_to_first=dx2)
    dmixed = _mm(dx1, W['w_out'], 'nt', "d_mixed")
    G['w_out'] = _mm(mixed, dx1, 'tn', "g_w_out", out_dtype=bf16)
    (dgates, dya, dyb), _ = _pw_bwd("mix_bwd", _mix_fn, [gates, ya, yb], [], [(dmixed,)], 256, row_dtypes=[bf16] * 3)
    dya_in = _mm(dya, W['rwkv_proj'], 'nt', "d_ya_in")
    G['rwkv_proj'] = _mm(ya_in, dya, 'tn', "g_rwkv_proj", out_dtype=bf16)
    dyb_in = _mm(dyb, W['gdn_proj'], 'nt', "d_yb_in")
    G['gdn_proj'] = _mm(yb_in, dyb, 'tn', "g_gdn_proj", out_dtype=bf16)

    (do, dz), (dnw_t,) = _pw_bwd("gdn_post_bwd", _gdn_post_fn, [o, z], [nw_t], [(dyb_in,)], 256, row_dtypes=[f32, bf16])
    G['gdn_norm_w'] = dnw_t.reshape(GDN_HEADS, GDN_HD).sum(axis=0)
    dgq, dgk, dgv, dgbeta = _scan_bwd("gdn_bwd", _gdn_block, gdn_in, do, zs_gdn, GDN_CHUNK)
    (dc_qkv, dab), (dal_p, ddt_p) = _pw_bwd(
        "gdn_prep_bwd", _gdn_prep_fn, [qkv_raw, ab], gd_consts, [(dgq,), (dgk,), (dgv,), (dgbeta,)], 256,
        conv_w=W['gdn_conv_w'], row_dtypes=[f32, bf16])
    G['gdn_a_log'], G['gdn_dt_bias'] = dal_p[0, :GDN_HEADS], ddt_p[0, :GDN_HEADS]
    dqkv_raw, G['gdn_conv_w'] = _conv_bwd("gdn_conv_bwd", dc_qkv, qkv_raw, W['gdn_conv_w'], 256, dx_dtype=bf16)

    (dy, dr1, dk21, dv1, dg_), (G['rwkv_ln_w'], G['rwkv_ln_b'], G['rwkv_r_k']) = _pw_bwd(
        "rwkv_post_bwd", _rwkv_post_fn, [y, r, k2, v, g], post_consts, [(dya_in,)], 256)
    dr2, dlw, dk22, dv2, da_, db_, *G['_arrived'] = _scan_bwd(
        "wkv_bwd", _wkv_block, wkv_in, dy, zs_wkv, WKV_CHUNK, side=None if late is None else (late['slabs'](G), True))
    (dps,), rw_grads = _pw_bwd(
        "rwkv_prep_bwd", _rwkv_prep_fn, [p_rwkv], rw_consts,
        [(dr1, dr2), (dlw,), (dk21, dk22), (dv1, dv2), (da_,), (db_,), (dg_,)], 256, conv_w=mixw)
    G['rwkv_w0'], dw2p, G['rwkv_a0'], da2p, G['rwkv_g2'], G['rwkv_k_k'], G['rwkv_k_a'] = rw_grads
    G['rwkv_w2'], G['rwkv_a2'] = dw2p[:64], da2p[64:]
    dp_rwkv, dmixw = _conv_bwd("shift_bwd", dps, p_rwkv, mixw, 256, dx_dtype=bf16)
    G['rwkv_mu'] = dmixw[0] - dmixw[1]

    dp = jnp.concatenate([dp_rwkv, dqkv_raw, dz, dgates, dab], axis=1)
    du = _mm(dp, wp, 'nt', "d_u")
    G['w_in_pad'] = _mm(u, dp, 'tn', "g_w_in", out_dtype=bf16)
    (dx,), (G['norm1_g'],) = _pw_bwd("norm1_bwd", _rms_fn, [x], [g1], [(du,)], 256, add_to_first=dx1)
    return loss, dx, G


def _pad_w_in(w):
    return jnp.concatenate([w[:, :OFF_GATES], w[:, OFF_GATES + 8:], w[:, OFF_GATES:OFF_GATES + 8],
                            jnp.zeros((w.shape[0], W_AB - 8), w.dtype)], axis=1)


def _unpad_w_in(wp):
    return jnp.concatenate([wp[:, :OFF_GATES], wp[:, OFF_AB:OFF_AB + 8], wp[:, OFF_GATES:OFF_AB]], axis=1)


BIG = ('w_in', 'rwkv_proj', 'gdn_proj', 'w_out', 'ffn_up', 'ffn_down')
SMALL_SHARDED = ('rwkv_w2', 'rwkv_a2', 'rwkv_g2', 'gdn_conv_w', 'ffn_conv_w')


def _rows128(shape):
    n = 1
    for d in shape:
        n *= d
    return -(-n // LANES)


def _pack128(arrays):
    parts = []
    for a in arrays:
        flat = a.reshape(-1)
        rows = _rows128(a.shape)
        parts.append(jnp.pad(flat, (0, rows * LANES - flat.shape[0])).reshape(rows, LANES))
    buf = jnp.concatenate(parts, axis=0)
    return jnp.pad(buf, ((0, -buf.shape[0] % HALO), (0, 0)))


def _unpack128(buf, shapes):
    out, off = [], 0
    for s in shapes:
        rows, n = _rows128(s), 1
        for d in s:
            n *= d
        out.append(buf[off:off + rows].reshape(-1)[:n].reshape(s))
        off += rows
    return out


def _row_tile(r, c):
    best = None
    for d in range(HALO, r + 1, HALO):
        if r % d == 0 and d * c * 4 <= TILE_BYTES:
            best = d
    return best if best is not None else r


def _xy_exchange(name, bufs, scatter):
    n = len(bufs)

    def body(*refs):
        start, finish = _xy_copies(refs[:n], refs[n:2 * n], *refs[2 * n:], scatter)
        start()
        finish()

    return pl.pallas_call(
        body, in_specs=[pl.BlockSpec(memory_space=pl.ANY)] * n, out_specs=[pl.BlockSpec(memory_space=pl.ANY)] * n,
        out_shape=_xy_out_shapes(bufs, scatter), scratch_shapes=_xy_sems(n), name=name)(*bufs)


def _sibling_exchange(name, bufs):
    n = len(bufs)

    def body(*refs):
        in_refs, out_refs, send_sems, recv_sems = refs[:n], refs[n:2 * n], refs[2 * n], refs[2 * n + 1]
        x, y, c = lax.axis_index("x"), lax.axis_index("y"), lax.axis_index("c")
        copies = [pltpu.make_async_remote_copy(
            src_ref=in_refs[a], dst_ref=out_refs[a], send_sem=send_sems.at[a], recv_sem=recv_sems.at[a],
            device_id=(x, y, 1 - c), device_id_type=pl.DeviceIdType.MESH) for a in range(n)]
        for cp in copies:
            cp.start()
        for cp in copies:
            cp.wait()

    return pl.pallas_call(
        body, in_specs=[pl.BlockSpec(memory_space=pl.ANY)] * n, out_specs=[pl.BlockSpec(memory_space=pl.ANY)] * n,
        out_shape=[jax.ShapeDtypeStruct(b.shape, b.dtype) for b in bufs],
        scratch_shapes=[pltpu.SemaphoreType.DMA((n,)), pltpu.SemaphoreType.DMA((n,))], name=name)(*bufs)


def _sum_slots(name, buf):
    _, R, L = buf.shape
    tr = _row_tile(R, L)

    def body(b_ref, o_ref):
        part = lambda s: b_ref[s].astype(f32)
        o_ref[...] = ((part(0) + part(1)) + part(2)) + part(3)

    return pl.pallas_call(
        body, grid=(R // tr,),
        in_specs=[pl.BlockSpec((N_POS, tr, L), lambda i: (0, i, 0))],
        out_specs=pl.BlockSpec((tr, L), lambda i: (i, 0)),
        out_shape=jax.ShapeDtypeStruct((R, L), f32), name=name,
        compiler_params=_params("parallel"))(buf)


def _adamw(name, w, ga, gb, m, v):
    R, L = w.shape
    tr = _row_tile(R, L)
    c1 = 1.0 / (1.0 - ADAM_B1 ** ADAM_STEP)
    c2 = 1.0 / (1.0 - ADAM_B2 ** ADAM_STEP)

    def body(w_ref, ga_ref, gb_ref, m_ref, v_ref, g_out, d_out, m_out, v_out):
        g = ga_ref[...] + gb_ref[...]
        m_new = ADAM_B1 * m_ref[...] + (1.0 - ADAM_B1) * g
        v_new = ADAM_B2 * v_ref[...] + (1.0 - ADAM_B2) * (g * g)
        g_out[...] = g
        m_out[...] = m_new
        v_out[...] = v_new
        d_out[...] = -ADAM_LR * ((m_new * c1) / (jnp.sqrt(v_new * c2) + ADAM_EPS) + ADAM_WD * w_ref[...])

    spec = pl.BlockSpec((tr, L), lambda i: (i, 0))
    return pl.pallas_call(
        body, grid=(R // tr,), in_specs=[spec] * 5, out_specs=[spec] * 4,
        out_shape=[jax.ShapeDtypeStruct((R, L), f32)] * 4, name=name,
        compiler_params=_params("parallel"))(w, ga, gb, m, v)


def _step(x, loss_target, P, M, V):
    shapes = {n: tuple(P[n].shape) for n in WEIGHTS}
    sh_shapes = [shapes[n] for n in SMALL_SHARDED]
    packed = SMALL_SHARDED + SMALL
    late_names = BIG[1:]

    def whole(n, g):
        return g.reshape(-1, g.shape[2]) if n in ROW_SHARDED else jnp.concatenate([g[j] for j in range(N_POS)], axis=1)

    def slabs(G, n, dtype=f32):
        r, c = shapes[n]
        full = G[n].astype(dtype)
        return full.reshape(N_POS, r, c) if n in ROW_SHARDED else full.reshape(r, N_POS, c).transpose(1, 0, 2)

    g_w_in, g_small = _xy_exchange("gather_w_in", [P['w_in'].astype(bf16), _pack128([P[n] for n in SMALL_SHARDED])],
                                   scatter=False)
    W = {n: P[n] for n in SMALL}
    W['w_in_pad'] = _pad_w_in(whole('w_in', g_w_in))
    per_pos = [_unpack128(g_small[j], sh_shapes) for j in range(N_POS)]
    for q, n in enumerate(SMALL_SHARDED):
        W[n] = jnp.concatenate([per_pos[j][q] for j in range(N_POS)], axis=1)
    late = dict(shards=[P[n].astype(bf16) for n in late_names],
                assemble=lambda gathered: {n: whole(n, g) for n, g in zip(late_names, gathered)},
                slabs=lambda G: [slabs(G, n, bf16) for n in late_names])

    loss_rows, dx, G = _local_step(x, loss_target, W, late)
    arrived_late = G.pop('_arrived')
    G['w_in'] = _unpad_w_in(G.pop('w_in_pad'))

    small_slabs = jnp.stack([_pack128([slabs(G, n)[j] for n in SMALL_SHARDED] + [G[n] for n in SMALL]) for j in range(N_POS)])
    arrived_w_in, arrived_small = _xy_exchange("scatter_w_in", [slabs(G, 'w_in', bf16), small_slabs], scatter=True)
    contributions = [arrived_w_in] + list(arrived_late) + [arrived_small]
    tags = list(BIG) + ['small']
    plane = [_sum_slots("sum_" + t, cbuf) for t, cbuf in zip(tags, contributions)]
    sibling = _sibling_exchange("sibling_grads", plane)

    out = {}
    names4 = ('grad', 'delta', 'new_m', 'new_v')
    for q, n in enumerate(BIG):
        for tag, t in zip(names4, _adamw("adamw_" + n, P[n], plane[q], sibling[q], M[n], V[n])):
            out[tag + '_' + n] = t
    small_out = _adamw("adamw_small", _pack128([P[n] for n in packed]), plane[-1], sibling[-1],
                       _pack128([M[n] for n in packed]), _pack128([V[n] for n in packed]))
    for tag, buf in zip(names4, small_out):
        for n, t in zip(packed, _unpack128(buf, [shapes[n] for n in packed])):
            out[tag + '_' + n] = t
    loss = lax.psum(loss_rows[0, 0], ("x", "y", "c"))
    return loss, dx, out


def kernel(x, norm1_g, w_in, rwkv_mu, rwkv_w0, rwkv_w2, rwkv_a0, rwkv_a2, rwkv_g2, rwkv_k_k, rwkv_k_a, rwkv_r_k, rwkv_ln_w, rwkv_ln_b, rwkv_proj, gdn_conv_w, gdn_a_log, gdn_dt_bias, gdn_norm_w, gdn_proj, w_out, norm2_g, ffn_up, ffn_conv_w, ffn_down, final_g, loss_target, m_norm1_g, m_w_in, m_rwkv_mu, m_rwkv_w0, m_rwkv_w2, m_rwkv_a0, m_rwkv_a2, m_rwkv_g2, m_rwkv_k_k, m_rwkv_k_a, m_rwkv_r_k, m_rwkv_ln_w, m_rwkv_ln_b, m_rwkv_proj, m_gdn_conv_w, m_gdn_a_log, m_gdn_dt_bias, m_gdn_norm_w, m_gdn_proj, m_w_out, m_norm2_g, m_ffn_up, m_ffn_conv_w, m_ffn_down, m_final_g, v_norm1_g, v_w_in, v_rwkv_mu, v_rwkv_w0, v_rwkv_w2, v_rwkv_a0, v_rwkv_a2, v_rwkv_g2, v_rwkv_k_k, v_rwkv_k_a, v_rwkv_r_k, v_rwkv_ln_w, v_rwkv_ln_b, v_rwkv_proj, v_gdn_conv_w, v_gdn_a_log, v_gdn_dt_bias, v_gdn_norm_w, v_gdn_proj, v_w_out, v_norm2_g, v_ffn_up, v_ffn_conv_w, v_ffn_down, v_final_g):
    weights = (norm1_g, w_in, rwkv_mu, rwkv_w0, rwkv_w2, rwkv_a0, rwkv_a2, rwkv_g2, rwkv_k_k, rwkv_k_a, rwkv_r_k, rwkv_ln_w,
               rwkv_ln_b, rwkv_proj, gdn_conv_w, gdn_a_log, gdn_dt_bias, gdn_norm_w, gdn_proj, w_out, norm2_g, ffn_up,
               ffn_conv_w, ffn_down, final_g)
    m_in = (m_norm1_g, m_w_in, m_rwkv_mu, m_rwkv_w0, m_rwkv_w2, m_rwkv_a0, m_rwkv_a2, m_rwkv_g2, m_rwkv_k_k, m_rwkv_k_a,
            m_rwkv_r_k, m_rwkv_ln_w, m_rwkv_ln_b, m_rwkv_proj, m_gdn_conv_w, m_gdn_a_log, m_gdn_dt_bias, m_gdn_norm_w,
            m_gdn_proj, m_w_out, m_norm2_g, m_ffn_up, m_ffn_conv_w, m_ffn_down, m_final_g)
    v_in = (v_norm1_g, v_w_in, v_rwkv_mu, v_rwkv_w0, v_rwkv_w2, v_rwkv_a0, v_rwkv_a2, v_rwkv_g2, v_rwkv_k_k, v_rwkv_k_a,
            v_rwkv_r_k, v_rwkv_ln_w, v_rwkv_ln_b, v_rwkv_proj, v_gdn_conv_w, v_gdn_a_log, v_gdn_dt_bias, v_gdn_norm_w,
            v_gdn_proj, v_w_out, v_norm2_g, v_ffn_up, v_ffn_conv_w, v_ffn_down, v_final_g)
    drop = lambda n, a: a if n == 'final_g' else a[0]
    P = {n: drop(n, a) for n, a in zip(WEIGHTS, weights)}
    M = {n: drop(n, a) for n, a in zip(WEIGHTS, m_in)}
    V = {n: drop(n, a) for n, a in zip(WEIGHTS, v_in)}
    loss, dx, out = _step(x[0], loss_target[0], P, M, V)
    lift = lambda n, a: a if n == 'final_g' else a[None]
    res = [loss, dx[None]]
    for tag in ('grad', 'delta', 'new_m', 'new_v'):
        res += [lift(n, out[tag + '_' + n]) for n in WEIGHTS]
    return tuple(res)
```

```python
import functools

import jax
import jax.numpy as jnp
from jax import lax
from jax.experimental import pallas as pl
from jax.experimental.pallas import tpu as pltpu

f32 = jnp.float32
bf16 = jnp.bfloat16
HI = lax.Precision.HIGHEST

D_MODEL = 1024
RWKV_HEADS, RWKV_HD, RWKV_W = 8, 64, 512
GDN_HEADS, GDN_HD, GDN_W = 4, 128, 512
FFN_H = 2816
NORM_EPS, L2_EPS, GN_EPS = 1e-6, 1e-6, 64e-5
W_AB = 256
OFF_QKV, OFF_Z, OFF_GATES, OFF_AB = 1792, 3328, 3840, 5888
W_IN_PAD = OFF_AB + W_AB
WKV_CHUNK = 64
GDN_CHUNK = 128
HALO = 8
LANES = 128
TILE_BYTES = 1 << 20
VMEM_LIMIT = 56 * 1024 * 1024

ADAM_LR, ADAM_B1, ADAM_B2, ADAM_EPS, ADAM_WD, ADAM_STEP = 0.001, 0.9, 0.999, 1e-08, 0.01, 10

ROW_SHARDED = ('w_out', 'ffn_down')
SMALL = ('norm1_g', 'rwkv_mu', 'rwkv_w0', 'rwkv_a0', 'rwkv_k_k', 'rwkv_k_a', 'rwkv_r_k', 'rwkv_ln_w', 'rwkv_ln_b',
         'gdn_a_log', 'gdn_dt_bias', 'gdn_norm_w', 'norm2_g', 'final_g')
WEIGHTS = ('norm1_g', 'w_in', 'rwkv_mu', 'rwkv_w0', 'rwkv_w2', 'rwkv_a0', 'rwkv_a2', 'rwkv_g2', 'rwkv_k_k', 'rwkv_k_a',
           'rwkv_r_k', 'rwkv_ln_w', 'rwkv_ln_b', 'rwkv_proj', 'gdn_conv_w', 'gdn_a_log', 'gdn_dt_bias', 'gdn_norm_w',
           'gdn_proj', 'w_out', 'norm2_g', 'ffn_up', 'ffn_conv_w', 'ffn_down', 'final_g')


def _params(*sem):
    return pltpu.CompilerParams(dimension_semantics=sem, vmem_limit_bytes=VMEM_LIMIT)


def _tile(n, limit):
    if n <= limit:
        return n
    best = None
    for d in range(128, limit + 1, 128):
        if n % d == 0:
            best = d
    if best is None:
        raise ValueError(f"no tile for {n} under {limit}")
    return best


MM_BLOCK_BYTES = 4 << 20


def _mm(a, b, mode, name, add=None, out_dtype=f32):
    if mode == 'nn':
        (M, K), N = a.shape, b.shape[1]
    elif mode == 'nt':
        (M, K), N = a.shape, b.shape[0]
    else:
        (K, M), N = a.shape, b.shape[1]
    tm = _tile(M, 1408)
    tk = _tile(K, min(2816, MM_BLOCK_BYTES // (tm * a.dtype.itemsize)))
    tn = _tile(N, max(128, min(MM_BLOCK_BYTES // (tk * b.dtype.itemsize), MM_BLOCK_BYTES // (tm * 4)) // 128 * 128))
    nk = K // tk
    dn = {'nn': (((1,), (0,)), ((), ())), 'nt': (((1,), (1,)), ((), ())), 'tn': (((0,), (0,)), ((), ()))}[mode]

    def body(a_ref, b_ref, *rest):
        o_ref = rest[1] if add is not None else rest[0]
        acc = lax.dot_general(a_ref[...].astype(bf16), b_ref[...].astype(bf16), dn, preferred_element_type=f32)
        if nk == 1:
            o_ref[...] = (acc + rest[0][...] if add is not None else acc).astype(out_dtype)
            return
        acc_ref = rest[-1]
        k = pl.program_id(2)

        @pl.when(k == 0)
        def _():
            acc_ref[...] = acc + rest[0][...] if add is not None else acc

        @pl.when(k > 0)
        def _():
            acc_ref[...] += acc

        @pl.when(k == nk - 1)
        def _():
            o_ref[...] = acc_ref[...].astype(out_dtype)

    a_spec = (pl.BlockSpec((tk, tm), lambda i, j, k: (k, i)) if mode == 'tn'
              else pl.BlockSpec((tm, tk), lambda i, j, k: (i, k)))
    b_spec = (pl.BlockSpec((tn, tk), lambda i, j, k: (j, k)) if mode == 'nt'
              else pl.BlockSpec((tk, tn), lambda i, j, k: (k, j)))
    o_spec = pl.BlockSpec((tm, tn), lambda i, j, k: (i, j))
    ins, specs = [a, b], [a_spec, b_spec]
    if add is not None:
        ins.append(add)
        specs.append(o_spec)
    return pl.pallas_call(
        body, grid=(M // tm, N // tn, nk), in_specs=specs, out_specs=o_spec,
        out_shape=jax.ShapeDtypeStruct((M, N), out_dtype),
        scratch_shapes=[pltpu.VMEM((tm, tn), f32)] if nk > 1 else [], name=name,
        compiler_params=_params("parallel", "parallel", "arbitrary"))(*ins)


def _shift_down(cur, prev, s):
    if s == 0:
        return cur
    ext = jnp.concatenate([prev, cur], axis=0)
    return pltpu.roll(ext, s, 0)[HALO:]


def _shift_up(cur, nxt, s):
    if s == 0:
        return cur
    ext = jnp.concatenate([cur, nxt], axis=0)
    return pltpu.roll(ext, ext.shape[0] - s, 0)[:cur.shape[0]]


def _conv_apply(cur, prev, w_ref):
    taps = w_ref.shape[0]
    out = None
    for i in range(taps):
        term = _shift_down(cur, prev, taps - 1 - i) * w_ref[pl.ds(i, 1), :]
        out = term if out is None else out + term
    return out


def _row_spec(tm, w):
    return pl.BlockSpec((tm, w), lambda i: (i, 0))


def _prev_spec(tm, w):
    return pl.BlockSpec((HALO, w), lambda i: (jnp.maximum(i * (tm // HALO) - 1, 0), 0))


def _next_spec(tm, w, T):
    return pl.BlockSpec((HALO, w), lambda i: (jnp.minimum((i + 1) * (tm // HALO), T // HALO - 1), 0))


def _full_spec(shape):
    return pl.BlockSpec(shape, lambda i: (0,) * len(shape))


def _pw_fwd(name, fn, rows, consts, out_widths, tm, conv_w=None, out_dtype=f32):
    T = rows[0].shape[0]
    nr, nc = len(rows), len(consts)

    def body(*refs):
        i = pl.program_id(0)
        vals = [r[...] for r in refs[:nr]]
        p = nr
        if conv_w is not None:
            prev = jnp.where(i > 0, refs[p][...], 0.0)
            vals[0] = _conv_apply(vals[0], prev, refs[p + 1])
            p += 2
        cvals = [r[...] for r in refs[p:p + nc]]
        outs = fn(*vals, *cvals)
        for o_ref, o in zip(refs[p + nc:], outs):
            o_ref[...] = o.astype(out_dtype)

    ins = list(rows)
    specs = [_row_spec(tm, r.shape[1]) for r in rows]
    if conv_w is not None:
        ins += [rows[0], conv_w]
        specs += [_prev_spec(tm, rows[0].shape[1]), _full_spec(conv_w.shape)]
    ins += list(consts)
    specs += [_full_spec(c.shape) for c in consts]
    outs = pl.pallas_call(
        body, grid=(T // tm,), in_specs=specs,
        out_specs=[_row_spec(tm, w) for w in out_widths],
        out_shape=[jax.ShapeDtypeStruct((T, w), out_dtype) for w in out_widths], name=name,
        compiler_params=_params("parallel"))(*ins)
    return outs


def _pw_bwd(name, fn, rows, consts, cots, tm, conv_w=None, add_to_first=None, row_dtypes=None):
    T = rows[0].shape[0]
    nr, nc = len(rows), len(consts)
    flat_cots = [c for grp in cots for c in grp]
    row_dtypes = row_dtypes or [f32] * nr

    def body(*refs):
        i = pl.program_id(0)
        vals = [r[...] for r in refs[:nr]]
        p = nr
        if conv_w is not None:
            prev = jnp.where(i > 0, refs[p][...], 0.0)
            vals[0] = _conv_apply(vals[0], prev, refs[p + 1])
            p += 2
        cvals = [r[...] for r in refs[p:p + nc]]
        p += nc
        cot_vals = []
        for grp in cots:
            acc = refs[p][...]
            for q in range(1, len(grp)):
                acc = acc + refs[p + q][...]
            p += len(grp)
            cot_vals.append(acc)
        extra = None
        if add_to_first is not None:
            extra = refs[p][...]
            p += 1
        _, vjp = jax.vjp(fn, *vals, *cvals)
        grads = vjp(tuple(cot_vals))
        row_out = refs[p:p + nr]
        const_out = refs[p + nr:]
        for q in range(nr):
            g = grads[q]
            if q == 0 and extra is not None:
                g = g + extra
            row_out[q][...] = g.astype(row_dtypes[q])

        @pl.when(i == 0)
        def _():
            for q in range(nc):
                const_out[q][...] = grads[nr + q]

        @pl.when(i > 0)
        def _():
            for q in range(nc):
                const_out[q][...] += grads[nr + q]

    ins = list(rows)
    specs = [_row_spec(tm, r.shape[1]) for r in rows]
    if conv_w is not None:
        ins += [rows[0], conv_w]
        specs += [_prev_spec(tm, rows[0].shape[1]), _full_spec(conv_w.shape)]
    ins += list(consts)
    specs += [_full_spec(c.shape) for c in consts]
    ins += flat_cots
    specs += [_row_spec(tm, c.shape[1]) for c in flat_cots]
    if add_to_first is not None:
        ins.append(add_to_first)
        specs.append(_row_spec(tm, add_to_first.shape[1]))
    out_shapes = ([jax.ShapeDtypeStruct(r.shape, d) for r, d in zip(rows, row_dtypes)]
                  + [jax.ShapeDtypeStruct(c.shape, f32) for c in consts])
    out_specs = [_row_spec(tm, r.shape[1]) for r in rows] + [_full_spec(c.shape) for c in consts]
    outs = pl.pallas_call(
        body, grid=(T // tm,), in_specs=specs, out_specs=out_specs, out_shape=out_shapes, name=name,
        compiler_params=_params("arbitrary"))(*ins)
    return list(outs[:nr]), list(outs[nr:])


def _pw_conv_bwd(name, fn, rows, consts, cots, conv_w, tm, row_dtypes=None):
    T, W0 = rows[0].shape
    nr, nc = len(rows), len(consts)
    taps = conv_w.shape[0]
    nblk = T // tm
    flat_cots = [c for grp in cots for c in grp]
    row_dtypes = row_dtypes or [f32] * nr

    def body(*refs):
        i = pl.program_id(0)
        p = 0
        cur = [r[...] for r in refs[p:p + nr]]; p += nr
        nxt = [r[...] for r in refs[p:p + nr]]; p += nr
        prev = jnp.where(i > 0, refs[p][...], 0.0); p += 1
        w_ref = refs[p]; p += 1
        cvals = [r[...] for r in refs[p:p + nc]]; p += nc

        def summed(p0):
            out, q = [], p0
            for grp in cots:
                acc = refs[q][...]
                for t in range(1, len(grp)):
                    acc = acc + refs[q + t][...]
                q += len(grp)
                out.append(acc)
            return out, q

        cot_cur, p = summed(p)
        cot_nxt, p = summed(p)
        row_out, dw_ref, const_out = refs[p:p + nr], refs[p + nr], refs[p + nr + 1:]

        x_cur = cur[0]
        _, vjp = jax.vjp(fn, _conv_apply(x_cur, prev, w_ref), *cur[1:], *cvals)
        grads = vjp(tuple(cot_cur))
        _, vjp_n = jax.vjp(fn, _conv_apply(nxt[0], x_cur[tm - HALO:], w_ref), *nxt[1:], *cvals)
        dc_n = jnp.where(i < nblk - 1, vjp_n(tuple(cot_nxt))[0], 0.0)
        dc = grads[0]

        @pl.when(i == 0)
        def _():
            dw_ref[...] = jnp.zeros_like(dw_ref)
            for q in range(nc):
                const_out[q][...] = jnp.zeros_like(const_out[q])

        dx = None
        for k in range(taps):
            s_ = taps - 1 - k
            term = _shift_up(dc, dc_n, s_) * w_ref[pl.ds(k, 1), :]
            dx = term if dx is None else dx + term
            dw_ref[pl.ds(k, 1), :] += jnp.sum(dc * _shift_down(x_cur, prev, s_), axis=0, keepdims=True)
        row_out[0][...] = dx.astype(row_dtypes[0])
        for q in range(1, nr):
            row_out[q][...] = grads[q].astype(row_dtypes[q])
        for q in range(nc):
            const_out[q][...] += grads[nr + q]

    ins = list(rows) + list(rows) + [rows[0], conv_w] + list(consts) + flat_cots + flat_cots
    specs = ([_row_spec(tm, r.shape[1]) for r in rows] + [_next_spec(tm, r.shape[1], T) for r in rows]
             + [_prev_spec(tm, W0), _full_spec(conv_w.shape)] + [_full_spec(c.shape) for c in consts]
             + [_row_spec(tm, c.shape[1]) for c in flat_cots] + [_next_spec(tm, c.shape[1], T) for c in flat_cots])
    out_shapes = ([jax.ShapeDtypeStruct(r.shape, d) for r, d in zip(rows, row_dtypes)]
                  + [jax.ShapeDtypeStruct(conv_w.shape, f32)] + [jax.ShapeDtypeStruct(c.shape, f32) for c in consts])
    out_specs = ([_row_spec(tm, r.shape[1]) for r in rows] + [_full_spec(conv_w.shape)]
                 + [_full_spec(c.shape) for c in consts])
    outs = pl.pallas_call(
        body, grid=(nblk,), in_specs=specs, out_specs=out_specs, out_shape=out_shapes, name=name,
        compiler_params=_params("arbitrary"))(*ins)
    return list(outs[:nr]), outs[nr], list(outs[nr + 1:])


def _sigmoid(x):
    return 1.0 / (1.0 + jnp.exp(-x))


def _softplus(x):
    return jnp.maximum(x, 0.0) + jnp.log(1.0 + jnp.exp(jnp.minimum(x, -x)))


def _seg_sum_impl(x, seg):
    w = x.shape[-1]
    r = lax.broadcasted_iota(jnp.int32, (w, w), 0) // seg
    c = lax.broadcasted_iota(jnp.int32, (w, w), 1) // seg
    ones = (r == c).astype(bf16)
    hi = x.astype(bf16)
    lo = (x - hi.astype(f32)).astype(bf16)
    return (jnp.dot(hi, ones, preferred_element_type=f32) + jnp.dot(lo, ones, preferred_element_type=f32))


@functools.partial(jax.custom_vjp, nondiff_argnums=(1,))
def _seg_sum(x, seg):
    return _seg_sum_impl(x, seg)


_seg_sum.defvjp(lambda x, seg: (_seg_sum_impl(x, seg), None), lambda seg, _, g: (_seg_sum_impl(g, seg),))


def _rms(x, g):
    return x * lax.rsqrt(jnp.mean(x * x, axis=-1, keepdims=True) + NORM_EPS) * g


def _rms_fn(x, g):
    return (_rms(x, g),)


def _loss_rows(x2, tgt, g):
    e = _rms(x2, g) - tgt
    return 0.5 * jnp.sum(e * e, axis=-1, keepdims=True) * (1.0 / D_MODEL)


def _rwkv_prep_fn(ps, w0, w2p, a0, a2p, g2, k_k, k_a):
    r, k, v = ps[:, 0:512], ps[:, 512:1024], ps[:, 1024:1536]
    wa, gl = ps[:, 1536:1664], ps[:, 1664:1792]
    z = w0 + jnp.dot(jnp.tanh(wa), w2p, precision=HI, preferred_element_type=f32)
    w_log = -_softplus(-z) - 0.5
    lw = -jnp.exp(w_log)
    a = _sigmoid(a0 + jnp.dot(wa, a2p, precision=HI, preferred_element_type=f32))
    g = jnp.dot(_sigmoid(gl), g2, precision=HI, preferred_element_type=f32)
    kx = k * k_k
    kk = kx * lax.rsqrt(_seg_sum(kx * kx, RWKV_HD) + L2_EPS)
    k2 = k * (1.0 + (a - 1.0) * k_a)
    return r, lw, k2, v, -kk, kk * a, g


def _rwkv_post_fn(y, r, k2, v, g, ln_w, ln_b, rk):
    mean = _seg_sum(y, RWKV_HD) * (1.0 / RWKV_HD)
    yc = y - mean
    var = _seg_sum(yc * yc, RWKV_HD) * (1.0 / RWKV_HD)
    yn = yc * lax.rsqrt(var + GN_EPS) * ln_w + ln_b
    bonus = _seg_sum(r * k2 * rk, RWKV_HD) * v
    return ((yn + bonus) * g,)


def _gdn_prep_fn(c, ab, al_p, dt_p):
    s = c * _sigmoid(c)
    q, k, v = s[:, 0:512], s[:, 512:1024], s[:, 1024:1536]
    q = q * lax.rsqrt(_seg_sum(q * q, GDN_HD) + L2_EPS) * (GDN_HD ** -0.5)
    k = k * lax.rsqrt(_seg_sum(k * k, GDN_HD) + L2_EPS)
    lane = lax.broadcasted_iota(jnp.int32, ab.shape, 1)
    gpart = -jnp.exp(al_p) * _softplus(ab + dt_p)
    gbeta = jnp.where(lane < GDN_HEADS, gpart, jnp.where(lane < 2 * GDN_HEADS, _sigmoid(ab), 0.0))
    return q, k, v, gbeta


def _gdn_post_fn(o, z, nw):
    ms = _seg_sum(o * o, GDN_HD) * (1.0 / GDN_HD)
    return (o * lax.rsqrt(ms + NORM_EPS) * nw * (z * _sigmoid(z)),)


def _mix_fn(gates, ya, yb):
    return (_sigmoid(gates[:, :D_MODEL]) * ya + _sigmoid(gates[:, D_MODEL:]) * yb,)


def _ffn_fn(c):
    hg, hu = c[:, :FFN_H], c[:, FFN_H:]
    return (hg * _sigmoid(hg) * hu,)


N_POS = 4


def _xy_out_shapes(bufs, scatter):
    return [jax.ShapeDtypeStruct((N_POS,) + tuple(b.shape[1:] if scatter else b.shape), b.dtype) for b in bufs]


def _xy_sems(n):
    return [pltpu.SemaphoreType.DMA((3 * n,)), pltpu.SemaphoreType.DMA((3 * n,)), pltpu.SemaphoreType.DMA((n,))]


def _xy_copies(in_refs, out_refs, send_sems, recv_sems, local_sems, scatter):
    n = len(in_refs)

    def plan(arriving):
        x, y, c = lax.axis_index("x"), lax.axis_index("y"), lax.axis_index("c")
        me = 2 * x + y
        peers = [(1 - x, y), (x, 1 - y), (1 - x, 1 - y)]

        def copy(a, k, src, slot, peer):
            return pltpu.make_async_remote_copy(
                src_ref=src, dst_ref=out_refs[a].at[slot], send_sem=send_sems.at[3 * a + k],
                recv_sem=recv_sems.at[3 * a + k], device_id=(peer[0], peer[1], c), device_id_type=pl.DeviceIdType.MESH)

        if arriving:
            return [copy(a, k, in_refs[a].at[me] if scatter else in_refs[a], 2 * peer[0] + peer[1], peer)
                    for a in range(n) for k, peer in enumerate(peers)]
        own = [pltpu.make_async_copy(in_refs[a].at[me] if scatter else in_refs[a], out_refs[a].at[me], local_sems.at[a])
               for a in range(n)]
        sends = [copy(a, k, in_refs[a].at[2 * peer[0] + peer[1]] if scatter else in_refs[a], me, peer)
                 for a in range(n) for k, peer in enumerate(peers)]
        return own + sends

    def start():
        for cp in plan(False):
            cp.start()

    def finish():
        for cp in plan(True):
            cp.wait_recv()
        outgoing = plan(False)
        for cp in outgoing[n:]:
            cp.wait_send()
        for cp in outgoing[:n]:
            cp.wait()

    return start, finish


_NN, _NT, _TN = 'hcs,hsd->hcd', 'hcd,hsd->hcs', 'hcd,hce->hde'


def _lo(spec, a, b):
    return jnp.einsum(spec, a.astype(bf16), b.astype(bf16), preferred_element_type=f32)


@jax.custom_vjp
def _bmm(a, b):
    return _lo(_NN, a, b)


_bmm.defvjp(lambda a, b: (_lo(_NN, a, b), (a, b)), lambda ab, g: (_lo(_NT, g, ab[1]), _lo(_TN, ab[0], g)))


@jax.custom_vjp
def _bmm_nt(a, b):
    return _lo(_NT, a, b)


_bmm_nt.defvjp(lambda a, b: (_lo(_NT, a, b), (a, b)), lambda ab, g: (_lo(_NN, g, ab[1]), _lo(_TN, g, ab[0])))


@jax.custom_vjp
def _bmm_tn(a, b):
    return _lo(_TN, a, b)


_bmm_tn.defvjp(lambda a, b: (_lo(_TN, a, b), (a, b)), lambda ab, g: (_lo(_NT, ab[1], g), _lo(_NN, ab[0], g)))


def _masks(H, C):
    row = lax.broadcasted_iota(jnp.int32, (H, C, C), 1)
    col = lax.broadcasted_iota(jnp.int32, (H, C, C), 2)
    return row, col


def _tri_inv_impl(L):
    H, C, _ = L.shape
    row, col = _masks(H, C)
    eye = (row == col).astype(f32)
    base = 16
    same = (row // base) == (col // base)
    Ld = jnp.where(same, L, 0.0)
    X = -Ld
    inv = eye + X
    for _ in range(3):
        X = _bmm(X, X)
        inv = _bmm(inv, eye + X)
    if C == base:
        return inv
    N = _bmm(inv, L - Ld)
    out = eye - N
    levels = C // base
    P = N
    span = 2
    while span < levels:
        P = _bmm(P, P)
        out = _bmm(out, eye + P)
        span *= 2
    return _bmm(out, inv)


@jax.custom_vjp
def _tri_inv(L):
    return _tri_inv_impl(L)


def _tri_inv_fwd(L):
    T = _tri_inv_impl(L)
    return T, T


def _tri_inv_bwd(T, dT):
    return (-_bmm_nt(_bmm_tn(T, dT), T),)


_tri_inv.defvjp(_tri_inv_fwd, _tri_inv_bwd)


def _cumsum_impl(x, reverse):
    C = x.shape[1]
    row = lax.broadcasted_iota(jnp.int32, x.shape, 1)
    s = 1
    while s < C:
        if reverse:
            x = x + jnp.where(row < C - s, pltpu.roll(x, C - s, 1), 0.0)
        else:
            x = x + jnp.where(row >= s, pltpu.roll(x, s, 1), 0.0)
        s *= 2
    return x


@jax.custom_vjp
def _cumsum(x):
    return _cumsum_impl(x, False)


_cumsum.defvjp(lambda x: (_cumsum_impl(x, False), None), lambda _, g: (_cumsum_impl(g, True),))


def _wkv_chunk(Z, r, lw, k, v, a, b):
    H, C, D = r.shape
    row, col = _masks(H, C)
    incl, strict = row >= col, row > col
    cw = _cumsum(lw)
    cwp = cw - lw
    cwl = jnp.sum(lw, axis=1, keepdims=True)
    en = jnp.exp(-cw)
    at, rt, bt, kt = a * jnp.exp(cwp), r * jnp.exp(cw), b * en, k * en
    Lab = jnp.where(strict, _bmm_nt(at, bt), 0.0)
    Lak = jnp.where(strict, _bmm_nt(at, kt), 0.0)
    Tm = _tri_inv(-Lab)
    U = _bmm(Tm, _bmm(at, Z) + _bmm(Lak, v))
    Rb = jnp.where(incl, _bmm_nt(rt, bt), 0.0)
    Rk = jnp.where(incl, _bmm_nt(rt, kt), 0.0)
    y = _bmm(rt, Z) + _bmm(Rb, U) + _bmm(Rk, v)
    ed = jnp.exp(cwl - cw)
    zdec = jnp.swapaxes(jnp.broadcast_to(jnp.exp(cwl), (H, Z.shape[2], D)), 1, 2)
    Z1 = Z * zdec + _bmm_tn(b * ed, U) + _bmm_tn(k * ed, v)
    return y, Z1


def _gdn_chunk(S, q, k, v, g, beta):
    H, C, D = q.shape
    row, col = _masks(H, C)
    incl, strict = row >= col, row > col
    gc = _cumsum(g)
    diff = gc - jnp.swapaxes(gc, 1, 2)
    decay = jnp.where(incl, jnp.exp(jnp.where(incl, diff, 0.0)), 0.0)
    gl = jnp.sum(g, axis=1, keepdims=True)
    kb, vb = k * beta, v * beta
    L = jnp.where(strict, _bmm_nt(kb, k) * decay, 0.0)
    Tm = _tri_inv(L)
    egc = jnp.exp(gc)
    u = _bmm(Tm, vb)
    wk = _bmm(Tm, kb * egc)
    attn = jnp.where(incl, _bmm_nt(q, k) * decay, 0.0)
    v_new = u - _bmm(wk, S)
    o = _bmm(q * egc, S) + _bmm(attn, v_new)
    S1 = S * jnp.exp(gl) + _bmm_tn(k * jnp.exp(gl - gc), v_new)
    return o, S1


def _wkv_block(Z, r, lw, k, v, a, b):
    lane = lax.broadcasted_iota(jnp.int32, (r.shape[0], 128), 1)
    low = lane < RWKV_HD

    def heads(t):
        out = []
        for p in range(RWKV_HEADS // 2):
            pair = t[:, 128 * p:128 * (p + 1)]
            out += [jnp.where(low, pair, 0.0), jnp.where(low, 0.0, pair)]
        return jnp.concatenate([t[None] for t in out], axis=0)

    y, Z1 = _wkv_chunk(Z, *[heads(t) for t in (r, lw, k, v, a, b)])
    return jnp.concatenate([y[2 * p] + y[2 * p + 1] for p in range(RWKV_HEADS // 2)], axis=1), Z1


def _gdn_block(S, q, k, v, gbeta):
    heads = lambda t: jnp.concatenate([t[None, :, GDN_HD * h:GDN_HD * (h + 1)] for h in range(GDN_HEADS)], axis=0)
    src = lax.broadcasted_iota(jnp.int32, (W_AB, 2 * GDN_W), 0)
    dst = lax.broadcasted_iota(jnp.int32, (W_AB, 2 * GDN_W), 1) // GDN_HD
    spread = jnp.dot(gbeta, (src == dst).astype(f32), precision=HI, preferred_element_type=f32)
    o, S1 = _gdn_chunk(S, heads(q), heads(k), heads(v), heads(spread[:, :GDN_W]), heads(spread[:, GDN_W:]))
    return jnp.concatenate([o[h] for h in range(GDN_HEADS)], axis=1), S1


def _scan_fwd(name, block_fn, ins, C, H, dh, w_out, side=None):
    T = ins[0].shape[0]
    n_in = len(ins)
    nblk = T // C
    n_side = 0 if side is None else len(side[0])

    def body(*refs):
        in_refs, refs = refs[:n_in], refs[n_in:]
        side_in, refs = refs[:n_side], refs[n_side:]
        y_ref, zs_ref, refs = refs[0], refs[1], refs[2:]
        side_out, refs = refs[:n_side], refs[n_side:]
        z_scr = refs[0]
        if side is not None:
            start, finish = _xy_copies(side_in, side_out, refs[1], refs[2], refs[3], side[1])
            pl.when(pl.program_id(0) == 0)(start)

        @pl.when(pl.program_id(0) == 0)
        def _():
            z_scr[...] = jnp.zeros_like(z_scr)

        Z = z_scr[...]
        zs_ref[0] = Z
        y, Z1 = block_fn(Z, *[r[...] for r in in_refs])
        y_ref[...] = y
        z_scr[...] = Z1
        if side is not None:
            pl.when(pl.program_id(0) == nblk - 1)(finish)

    side_bufs = [] if side is None else list(side[0])
    any_spec = pl.BlockSpec(memory_space=pl.ANY)
    return pl.pallas_call(
        body, grid=(nblk,),
        in_specs=[pl.BlockSpec((C, a.shape[1]), lambda i: (i, 0)) for a in ins] + [any_spec] * n_side,
        out_specs=[pl.BlockSpec((C, w_out), lambda i: (i, 0)), pl.BlockSpec((1, H, dh, dh), lambda i: (i, 0, 0, 0))]
        + [any_spec] * n_side,
        out_shape=[jax.ShapeDtypeStruct((T, w_out), f32), jax.ShapeDtypeStruct((T // C, H, dh, dh), f32)]
        + (_xy_out_shapes(side_bufs, side[1]) if side is not None else []),
        scratch_shapes=[pltpu.VMEM((H, dh, dh), f32)] + (_xy_sems(n_side) if side is not None else []), name=name,
        compiler_params=_params("arbitrary"))(*ins, *side_bufs)


def _scan_bwd(name, block_fn, ins, dy, zs, C, side=None):
    T = ins[0].shape[0]
    _, H, dh, _ = zs.shape
    n_in = len(ins)
    nblk = T // C
    n_side = 0 if side is None else len(side[0])

    def body(*refs):
        in_refs, dy_ref, zs_ref, refs = refs[:n_in], refs[n_in], refs[n_in + 1], refs[n_in + 2:]
        side_in, refs = refs[:n_side], refs[n_side:]
        out_refs, refs = refs[:n_in], refs[n_in:]
        side_out, refs = refs[:n_side], refs[n_side:]
        dz_scr = refs[0]
        if side is not None:
            start, finish = _xy_copies(side_in, side_out, refs[1], refs[2], refs[3], side[1])
            pl.when(pl.program_id(0) == 0)(start)

        @pl.when(pl.program_id(0) == 0)
        def _():
            dz_scr[...] = jnp.zeros_like(dz_scr)

        _, vjp = jax.vjp(block_fn, zs_ref[0], *[r[...] for r in in_refs])
        grads = vjp((dy_ref[...], dz_scr[...]))
        dz_scr[...] = grads[0]
        for o_ref, gval in zip(out_refs, grads[1:]):
            o_ref[...] = gval
        if side is not None:
            pl.when(pl.program_id(0) == nblk - 1)(finish)

    side_bufs = [] if side is None else list(side[0])
    any_spec = pl.BlockSpec(memory_space=pl.ANY)
    rev = lambda i: (nblk - 1 - i, 0)
    return pl.pallas_call(
        body, grid=(nblk,),
        in_specs=[pl.BlockSpec((C, a.shape[1]), rev) for a in ins]
        + [pl.BlockSpec((C, dy.shape[1]), rev), pl.BlockSpec((1, H, dh, dh), lambda i: (nblk - 1 - i, 0, 0, 0))]
        + [any_spec] * n_side,
        out_specs=[pl.BlockSpec((C, a.shape[1]), rev) for a in ins] + [any_spec] * n_side,
        out_shape=[jax.ShapeDtypeStruct(a.shape, f32) for a in ins]
        + (_xy_out_shapes(side_bufs, side[1]) if side is not None else []),
        scratch_shapes=[pltpu.VMEM((H, dh, dh), f32)] + (_xy_sems(n_side) if side is not None else []), name=name,
        compiler_params=_params("arbitrary"))(*ins, dy, zs, *side_bufs)


def _loss_call(x2, tgt, g, tm):
    T, W = x2.shape

    def body(x_ref, t_ref, g_ref, dx_ref, dg_ref, l_ref):
        i = pl.program_id(0)
        tv = t_ref[...]
        l, vjp = jax.vjp(lambda xv, gv: _loss_rows(xv, tv, gv), x_ref[...], g_ref[...])
        dx, dg = vjp(jnp.ones_like(l))
        dx_ref[...] = dx
        tot = jnp.zeros((1, 128), f32) + jnp.sum(l)

        @pl.when(i == 0)
        def _():
            dg_ref[...] = dg
            l_ref[...] = tot

        @pl.when(i > 0)
        def _():
            dg_ref[...] += dg
            l_ref[...] += tot

    return pl.pallas_call(
        body, grid=(T // tm,),
        in_specs=[_row_spec(tm, W), _row_spec(tm, W), _full_spec(g.shape)],
        out_specs=[_row_spec(tm, W), _full_spec(g.shape), _full_spec((1, 128))],
        out_shape=[jax.ShapeDtypeStruct((T, W), f32), jax.ShapeDtypeStruct(g.shape, f32),
                   jax.ShapeDtypeStruct((1, 128), f32)], name="loss_head",
        compiler_params=_params("arbitrary"))(x2, tgt, g)


def _local_step(x, tgt, W, late=None):
    row = lambda a: a.reshape(1, -1)
    wp = W['w_in_pad']
    w_rwkv, w_qkv, w_z = wp[:, :OFF_QKV], wp[:, OFF_QKV:OFF_Z], wp[:, OFF_Z:OFF_GATES]
    w_gates, w_ab = wp[:, OFF_GATES:OFF_AB], wp[:, OFF_AB:]
    mu = row(W['rwkv_mu'])
    mixw = jnp.concatenate([mu, 1.0 - mu], axis=0)
    zpad = jnp.zeros((64, RWKV_W), f32)
    w2p = jnp.concatenate([W['rwkv_w2'], zpad], axis=0)
    a2p = jnp.concatenate([zpad, W['rwkv_a2']], axis=0)
    rw_consts = [row(W['rwkv_w0']), w2p, row(W['rwkv_a0']), a2p, W['rwkv_g2'], row(W['rwkv_k_k']), row(W['rwkv_k_a'])]
    post_consts = [row(W['rwkv_ln_w']), row(W['rwkv_ln_b']), row(W['rwkv_r_k'])]
    pad4 = lambda a: jnp.pad(row(a), ((0, 0), (0, W_AB - GDN_HEADS)))
    gd_consts = [pad4(W['gdn_a_log']), pad4(W['gdn_dt_bias'])]
    nw_t = jnp.tile(row(W['gdn_norm_w']), (1, GDN_HEADS))
    g1, g2n, gf = row(W['norm1_g']), row(W['norm2_g']), row(W['final_g'])

    (u,) = _pw_fwd("norm1", _rms_fn, [x], [g1], [D_MODEL], 256, out_dtype=bf16)
    p_rwkv = _mm(u, w_rwkv, 'nn', "in_rwkv")
    qkv_raw = _mm(u, w_qkv, 'nn', "in_qkv")
    z = _mm(u, w_z, 'nn', "in_z")
    gates = _mm(u, w_gates, 'nn', "in_gates")
    ab = _mm(u, w_ab, 'nn', "in_ab")

    r, lw, k2, v, a_, b_, g = _pw_fwd("rwkv_prep", _rwkv_prep_fn, [p_rwkv], rw_consts, [RWKV_W] * 7, 256, conv_w=mixw)
    wkv_in = [r, lw, k2, v, a_, b_]
    y, zs_wkv, *gathered = _scan_fwd("wkv_fwd", _wkv_block, wkv_in, WKV_CHUNK, RWKV_HEADS, 2 * RWKV_HD, RWKV_W,
                                     side=None if late is None else (late['shards'], False))
    if late is not None:
        W = dict(W, **late['assemble'](gathered))
    (ya_in,) = _pw_fwd("rwkv_post", _rwkv_post_fn, [y, r, k2, v, g], post_consts, [RWKV_W], 256, out_dtype=bf16)
    ya = _mm(ya_in, W['rwkv_proj'], 'nn', "rwkv_proj")

    gq, gk, gv, gbeta = _pw_fwd("gdn_prep", _gdn_prep_fn, [qkv_raw, ab], gd_consts, [GDN_W] * 3 + [W_AB], 256,
                                conv_w=W['gdn_conv_w'])
    gdn_in = [gq, gk, gv, gbeta]
    o, zs_gdn = _scan_fwd("gdn_fwd", _gdn_block, gdn_in, GDN_CHUNK, GDN_HEADS, GDN_HD, GDN_W)
    (yb_in,) = _pw_fwd("gdn_post", _gdn_post_fn, [o, z], [nw_t], [GDN_W], 256, out_dtype=bf16)
    yb = _mm(yb_in, W['gdn_proj'], 'nn', "gdn_proj")

    (mixed,) = _pw_fwd("mix", _mix_fn, [gates, ya, yb], [], [D_MODEL], 256, out_dtype=bf16)
    x1 = _mm(mixed, W['w_out'], 'nn', "w_out", add=x)
    (u2,) = _pw_fwd("norm2", _rms_fn, [x1], [g2n], [D_MODEL], 256, out_dtype=bf16)
    h = _mm(u2, W['ffn_up'], 'nn', "ffn_up")
    (act,) = _pw_fwd("ffn_act", _ffn_fn, [h], [], [FFN_H], 128, conv_w=W['ffn_conv_w'], out_dtype=bf16)
    x2 = _mm(act, W['ffn_down'], 'nn', "ffn_down", add=x1)

    G = {}
    dx2, dgf, loss = _loss_call(x2, tgt, gf, 256)
    G['final_g'] = dgf
    dact = _mm(dx2, W['ffn_down'], 'nt', "d_act")
    G['ffn_down'] = _mm(act, dx2, 'tn', "g_ffn_down", out_dtype=bf16)
    (dh,), G['ffn_conv_w'], _ = _pw_conv_bwd("ffn_act_bwd", _ffn_fn, [h], [], [(dact,)], W['ffn_conv_w'], 128,
                                             row_dtypes=[bf16])
    du2 = _mm(dh, W['ffn_up'], 'nt', "d_u2")
    G['ffn_up'] = _mm(u2, dh, 'tn', "g_ffn_up", out_dtype=bf16)
    (dx1,), (G['norm2_g'],) = _pw_bwd("norm2_bwd", _rms_fn, [x1], [g2n], [(du2,)], 256, add_to_first=dx2)
    dmixed = _mm(dx1, W['w_out'], 'nt', "d_mixed")
    G['w_out'] = _mm(mixed, dx1, 'tn', "g_w_out", out_dtype=bf16)
    (dgates, dya, dyb), _ = _pw_bwd("mix_bwd", _mix_fn, [gates, ya, yb], [], [(dmixed,)], 256, row_dtypes=[bf16] * 3)
    dya_in = _mm(dya, W['rwkv_proj'], 'nt', "d_ya_in")
    G['rwkv_proj'] = _mm(ya_in, dya, 'tn', "g_rwkv_proj", out_dtype=bf16)
    dyb_in = _mm(dyb, W['gdn_proj'], 'nt', "d_yb_in")
    G['gdn_proj'] = _mm(yb_in, dyb, 'tn', "g_gdn_proj", out_dtype=bf16)

    (do, dz), (dnw_t,) = _pw_bwd("gdn_post_bwd", _gdn_post_fn, [o, z], [nw_t], [(dyb_in,)], 256, row_dtypes=[f32, bf16])
    G['gdn_norm_w'] = dnw_t.reshape(GDN_HEADS, GDN_HD).sum(axis=0)
    dgq, dgk, dgv, dgbeta = _scan_bwd("gdn_bwd", _gdn_block, gdn_in, do, zs_gdn, GDN_CHUNK)
    (dqkv_raw, dab), G['gdn_conv_w'], (dal_p, ddt_p) = _pw_conv_bwd(
        "gdn_prep_bwd", _gdn_prep_fn, [qkv_raw, ab], gd_consts, [(dgq,), (dgk,), (dgv,), (dgbeta,)], W['gdn_conv_w'], 256,
        row_dtypes=[bf16, bf16])
    G['gdn_a_log'], G['gdn_dt_bias'] = dal_p[0, :GDN_HEADS], ddt_p[0, :GDN_HEADS]

    (dy, dr1, dk21, dv1, dg_), (G['rwkv_ln_w'], G['rwkv_ln_b'], G['rwkv_r_k']) = _pw_bwd(
        "rwkv_post_bwd", _rwkv_post_fn, [y, r, k2, v, g], post_consts, [(dya_in,)], 256)
    dr2, dlw, dk22, dv2, da_, db_, *G['_arrived'] = _scan_bwd(
        "wkv_bwd", _wkv_block, wkv_in, dy, zs_wkv, WKV_CHUNK, side=None if late is None else (late['slabs'](G), True))
    (dp_rwkv,), dmixw, rw_grads = _pw_conv_bwd(
        "rwkv_prep_bwd", _rwkv_prep_fn, [p_rwkv], rw_consts,
        [(dr1, dr2), (dlw,), (dk21, dk22), (dv1, dv2), (da_,), (db_,), (dg_,)], mixw, 256, row_dtypes=[bf16])
    G['rwkv_w0'], dw2p, G['rwkv_a0'], da2p, G['rwkv_g2'], G['rwkv_k_k'], G['rwkv_k_a'] = rw_grads
    G['rwkv_w2'], G['rwkv_a2'] = dw2p[:64], da2p[64:]
    G['rwkv_mu'] = dmixw[0] - dmixw[1]

    dp = jnp.concatenate([dp_rwkv, dqkv_raw, dz, dgates, dab], axis=1)
    du = _mm(dp, wp, 'nt', "d_u")
    G['w_in_pad'] = _mm(u, dp, 'tn', "g_w_in", out_dtype=bf16)
    (dx,), (G['norm1_g'],) = _pw_bwd("norm1_bwd", _rms_fn, [x], [g1], [(du,)], 256, add_to_first=dx1)
    return loss, dx, G


def _pad_w_in(w):
    return jnp.concatenate([w[:, :OFF_GATES], w[:, OFF_GATES + 8:], w[:, OFF_GATES:OFF_GATES + 8],
                            jnp.zeros((w.shape[0], W_AB - 8), w.dtype)], axis=1)


def _unpad_w_in(wp):
    return jnp.concatenate([wp[:, :OFF_GATES], wp[:, OFF_AB:OFF_AB + 8], wp[:, OFF_GATES:OFF_AB]], axis=1)


BIG = ('w_in', 'rwkv_proj', 'gdn_proj', 'w_out', 'ffn_up', 'ffn_down')
SMALL_SHARDED = ('rwkv_w2', 'rwkv_a2', 'rwkv_g2', 'gdn_conv_w', 'ffn_conv_w')


def _rows128(shape):
    n = 1
    for d in shape:
        n *= d
    return -(-n // LANES)


def _pack128(arrays):
    parts = []
    for a in arrays:
        flat = a.reshape(-1)
        rows = _rows128(a.shape)
        parts.append(jnp.pad(flat, (0, rows * LANES - flat.shape[0])).reshape(rows, LANES))
    buf = jnp.concatenate(parts, axis=0)
    return jnp.pad(buf, ((0, -buf.shape[0] % HALO), (0, 0)))


def _unpack128(buf, shapes):
    out, off = [], 0
    for s in shapes:
        rows, n = _rows128(s), 1
        for d in s:
            n *= d
        out.append(buf[off:off + rows].reshape(-1)[:n].reshape(s))
        off += rows
    return out


def _row_tile(r, c):
    best = None
    for d in range(HALO, r + 1, HALO):
        if r % d == 0 and d * c * 4 <= TILE_BYTES:
            best = d
    return best if best is not None else r


def _xy_exchange(name, bufs, scatter):
    n = len(bufs)

    def body(*refs):
        start, finish = _xy_copies(refs[:n], refs[n:2 * n], *refs[2 * n:], scatter)
        start()
        finish()

    return pl.pallas_call(
        body, in_specs=[pl.BlockSpec(memory_space=pl.ANY)] * n, out_specs=[pl.BlockSpec(memory_space=pl.ANY)] * n,
        out_shape=_xy_out_shapes(bufs, scatter), scratch_shapes=_xy_sems(n), name=name)(*bufs)


def _sibling_exchange(name, bufs):
    n = len(bufs)

    def body(*refs):
        in_refs, out_refs, send_sems, recv_sems = refs[:n], refs[n:2 * n], refs[2 * n], refs[2 * n + 1]
        x, y, c = lax.axis_index("x"), lax.axis_index("y"), lax.axis_index("c")
        copies = [pltpu.make_async_remote_copy(
            src_ref=in_refs[a], dst_ref=out_refs[a], send_sem=send_sems.at[a], recv_sem=recv_sems.at[a],
            device_id=(x, y, 1 - c), device_id_type=pl.DeviceIdType.MESH) for a in range(n)]
        for cp in copies:
            cp.start()
        for cp in copies:
            cp.wait()

    return pl.pallas_call(
        body, in_specs=[pl.BlockSpec(memory_space=pl.ANY)] * n, out_specs=[pl.BlockSpec(memory_space=pl.ANY)] * n,
        out_shape=[jax.ShapeDtypeStruct(b.shape, b.dtype) for b in bufs],
        scratch_shapes=[pltpu.SemaphoreType.DMA((n,)), pltpu.SemaphoreType.DMA((n,))], name=name)(*bufs)


def _sum_slots(name, buf):
    _, R, L = buf.shape
    tr = _row_tile(R, L)

    def body(b_ref, o_ref):
        part = lambda s: b_ref[s].astype(f32)
        o_ref[...] = ((part(0) + part(1)) + part(2)) + part(3)

    return pl.pallas_call(
        body, grid=(R // tr,),
        in_specs=[pl.BlockSpec((N_POS, tr, L), lambda i: (0, i, 0))],
        out_specs=pl.BlockSpec((tr, L), lambda i: (i, 0)),
        out_shape=jax.ShapeDtypeStruct((R, L), f32), name=name,
        compiler_params=_params("parallel"))(buf)


def _adamw(name, w, ga, gb, m, v):
    R, L = w.shape
    tr = _row_tile(R, L)
    c1 = 1.0 / (1.0 - ADAM_B1 ** ADAM_STEP)
    c2 = 1.0 / (1.0 - ADAM_B2 ** ADAM_STEP)

    def body(w_ref, ga_ref, gb_ref, m_ref, v_ref, g_out, d_out, m_out, v_out):
        g = ga_ref[...] + gb_ref[...]
        m_new = ADAM_B1 * m_ref[...] + (1.0 - ADAM_B1) * g
        v_new = ADAM_B2 * v_ref[...] + (1.0 - ADAM_B2) * (g * g)
        g_out[...] = g
        m_out[...] = m_new
        v_out[...] = v_new
        d_out[...] = -ADAM_LR * ((m_new * c1) / (jnp.sqrt(v_new * c2) + ADAM_EPS) + ADAM_WD * w_ref[...])

    spec = pl.BlockSpec((tr, L), lambda i: (i, 0))
    return pl.pallas_call(
        body, grid=(R // tr,), in_specs=[spec] * 5, out_specs=[spec] * 4,
        out_shape=[jax.ShapeDtypeStruct((R, L), f32)] * 4, name=name,
        compiler_params=_params("parallel"))(w, ga, gb, m, v)


def _step(x, loss_target, P, M, V):
    shapes = {n: tuple(P[n].shape) for n in WEIGHTS}
    sh_shapes = [shapes[n] for n in SMALL_SHARDED]
    packed = SMALL_SHARDED + SMALL
    late_names = BIG[1:]

    def whole(n, g):
        return g.reshape(-1, g.shape[2]) if n in ROW_SHARDED else jnp.concatenate([g[j] for j in range(N_POS)], axis=1)

    def slabs(G, n, dtype=f32):
        r, c = shapes[n]
        full = G[n].astype(dtype)
        return full.reshape(N_POS, r, c) if n in ROW_SHARDED else full.reshape(r, N_POS, c).transpose(1, 0, 2)

    g_w_in, g_small = _xy_exchange("gather_w_in", [P['w_in'].astype(bf16), _pack128([P[n] for n in SMALL_SHARDED])],
                                   scatter=False)
    W = {n: P[n] for n in SMALL}
    W['w_in_pad'] = _pad_w_in(whole('w_in', g_w_in))
    per_pos = [_unpack128(g_small[j], sh_shapes) for j in range(N_POS)]
    for q, n in enumerate(SMALL_SHARDED):
        W[n] = jnp.concatenate([per_pos[j][q] for j in range(N_POS)], axis=1)
    late = dict(shards=[P[n].astype(bf16) for n in late_names],
                assemble=lambda gathered: {n: whole(n, g) for n, g in zip(late_names, gathered)},
                slabs=lambda G: [slabs(G, n, bf16) for n in late_names])

    loss_rows, dx, G = _local_step(x, loss_target, W, late)
    arrived_late = G.pop('_arrived')
    G['w_in'] = _unpad_w_in(G.pop('w_in_pad'))

    small_slabs = jnp.stack([_pack128([slabs(G, n)[j] for n in SMALL_SHARDED] + [G[n] for n in SMALL]) for j in range(N_POS)])
    arrived_w_in, arrived_small = _xy_exchange("scatter_w_in", [slabs(G, 'w_in', bf16), small_slabs], scatter=True)
    contributions = [arrived_w_in] + list(arrived_late) + [arrived_small]
    tags = list(BIG) + ['small']
    plane = [_sum_slots("sum_" + t, cbuf) for t, cbuf in zip(tags, contributions)]
    sibling = _sibling_exchange("sibling_grads", plane)

    out = {}
    names4 = ('grad', 'delta', 'new_m', 'new_v')
    for q, n in enumerate(BIG):
        for tag, t in zip(names4, _adamw("adamw_" + n, P[n], plane[q], sibling[q], M[n], V[n])):
            out[tag + '_' + n] = t
    small_out = _adamw("adamw_small", _pack128([P[n] for n in packed]), plane[-1], sibling[-1],
                       _pack128([M[n] for n in packed]), _pack128([V[n] for n in packed]))
    for tag, buf in zip(names4, small_out):
        for n, t in zip(packed, _unpack128(buf, [shapes[n] for n in packed])):
            out[tag + '_' + n] = t
    loss = lax.psum(loss_rows[0, 0], ("x", "y", "c"))
    return loss, dx, out


def kernel(x, norm1_g, w_in, rwkv_mu, rwkv_w0, rwkv_w2, rwkv_a0, rwkv_a2, rwkv_g2, rwkv_k_k, rwkv_k_a, rwkv_r_k, rwkv_ln_w, rwkv_ln_b, rwkv_proj, gdn_conv_w, gdn_a_log, gdn_dt_bias, gdn_norm_w, gdn_proj, w_out, norm2_g, ffn_up, ffn_conv_w, ffn_down, final_g, loss_target, m_norm1_g, m_w_in, m_rwkv_mu, m_rwkv_w0, m_rwkv_w2, m_rwkv_a0, m_rwkv_a2, m_rwkv_g2, m_rwkv_k_k, m_rwkv_k_a, m_rwkv_r_k, m_rwkv_ln_w, m_rwkv_ln_b, m_rwkv_proj, m_gdn_conv_w, m_gdn_a_log, m_gdn_dt_bias, m_gdn_norm_w, m_gdn_proj, m_w_out, m_norm2_g, m_ffn_up, m_ffn_conv_w, m_ffn_down, m_final_g, v_norm1_g, v_w_in, v_rwkv_mu, v_rwkv_w0, v_rwkv_w2, v_rwkv_a0, v_rwkv_a2, v_rwkv_g2, v_rwkv_k_k, v_rwkv_k_a, v_rwkv_r_k, v_rwkv_ln_w, v_rwkv_ln_b, v_rwkv_proj, v_gdn_conv_w, v_gdn_a_log, v_gdn_dt_bias, v_gdn_norm_w, v_gdn_proj, v_w_out, v_norm2_g, v_ffn_up, v_ffn_conv_w, v_ffn_down, v_final_g):
    weights = (norm1_g, w_in, rwkv_mu, rwkv_w0, rwkv_w2, rwkv_a0, rwkv_a2, rwkv_g2, rwkv_k_k, rwkv_k_a, rwkv_r_k, rwkv_ln_w,
               rwkv_ln_b, rwkv_proj, gdn_conv_w, gdn_a_log, gdn_dt_bias, gdn_norm_w, gdn_proj, w_out, norm2_g, ffn_up,
               ffn_conv_w, ffn_down, final_g)
    m_in = (m_norm1_g, m_w_in, m_rwkv_mu, m_rwkv_w0, m_rwkv_w2, m_rwkv_a0, m_rwkv_a2, m_rwkv_g2, m_rwkv_k_k, m_rwkv_k_a,
            m_rwkv_r_k, m_rwkv_ln_w, m_rwkv_ln_b, m_rwkv_proj, m_gdn_conv_w, m_gdn_a_log, m_gdn_dt_bias, m_gdn_norm_w,
            m_gdn_proj, m_w_out, m_norm2_g, m_ffn_up, m_ffn_conv_w, m_ffn_down, m_final_g)
    v_in = (v_norm1_g, v_w_in, v_rwkv_mu, v_rwkv_w0, v_rwkv_w2, v_rwkv_a0, v_rwkv_a2, v_rwkv_g2, v_rwkv_k_k, v_rwkv_k_a,
            v_rwkv_r_k, v_rwkv_ln_w, v_rwkv_ln_b, v_rwkv_proj, v_gdn_conv_w, v_gdn_a_log, v_gdn_dt_bias, v_gdn_norm_w,
            v_gdn_proj, v_w_out, v_norm2_g, v_ffn_up, v_ffn_conv_w, v_ffn_down, v_final_g)
    drop = lambda n, a: a if n == 'final_g' else a[0]
    P = {n: drop(n, a) for n, a in zip(WEIGHTS, weights)}
    M = {n: drop(n, a) for n, a in zip(WEIGHTS, m_in)}
    V = {n: drop(n, a) for n, a in zip(WEIGHTS, v_in)}
    loss, dx, out = _step(x[0], loss_target[0], P, M, V)
    lift = lambda n, a: a if n == 'final_g' else a[None]
    res = [loss, dx[None]]
    for tag in ('grad', 'delta', 'new_m', 'new_v'):
        res += [lift(n, out[tag + '_' + n]) for n in WEIGHTS]
    return tuple(res)
```

```python
import functools

import jax
import jax.numpy as jnp
from jax import lax
from jax.experimental import pallas as pl
from jax.experimental.pallas import tpu as pltpu

f32 = jnp.float32
bf16 = jnp.bfloat16
HI = lax.Precision.HIGHEST

D_MODEL = 1024
RWKV_HEADS, RWKV_HD, RWKV_W = 8, 64, 512
GDN_HEADS, GDN_HD, GDN_W = 4, 128, 512
FFN_H = 2816
NORM_EPS, L2_EPS, GN_EPS = 1e-6, 1e-6, 64e-5
W_AB = 256
OFF_QKV, OFF_Z, OFF_GATES, OFF_AB = 1792, 3328, 3840, 5888
W_IN_PAD = OFF_AB + W_AB
WKV_CHUNK = 64
GDN_CHUNK = 128
HALO = 8
LANES = 128
TILE_BYTES = 1 << 20
VMEM_LIMIT = 56 * 1024 * 1024

ADAM_LR, ADAM_B1, ADAM_B2, ADAM_EPS, ADAM_WD, ADAM_STEP = 0.001, 0.9, 0.999, 1e-08, 0.01, 10

ROW_SHARDED = ('w_out', 'ffn_down')
SMALL = ('norm1_g', 'rwkv_mu', 'rwkv_w0', 'rwkv_a0', 'rwkv_k_k', 'rwkv_k_a', 'rwkv_r_k', 'rwkv_ln_w', 'rwkv_ln_b',
         'gdn_a_log', 'gdn_dt_bias', 'gdn_norm_w', 'norm2_g', 'final_g')
WEIGHTS = ('norm1_g', 'w_in', 'rwkv_mu', 'rwkv_w0', 'rwkv_w2', 'rwkv_a0', 'rwkv_a2', 'rwkv_g2', 'rwkv_k_k', 'rwkv_k_a',
           'rwkv_r_k', 'rwkv_ln_w', 'rwkv_ln_b', 'rwkv_proj', 'gdn_conv_w', 'gdn_a_log', 'gdn_dt_bias', 'gdn_norm_w',
           'gdn_proj', 'w_out', 'norm2_g', 'ffn_up', 'ffn_conv_w', 'ffn_down', 'final_g')


def _params(*sem):
    return pltpu.CompilerParams(dimension_semantics=sem, vmem_limit_bytes=VMEM_LIMIT)


def _tile(n, limit):
    if n <= limit:
        return n
    best = None
    for d in range(128, limit + 1, 128):
        if n % d == 0:
            best = d
    if best is None:
        raise ValueError(f"no tile for {n} under {limit}")
    return best


MM_BLOCK_BYTES = 4 << 20


def _mm(a, b, mode, name, add=None, out_dtype=f32, side=None):
    if mode == 'nn':
        (M, K), N = a.shape, b.shape[1]
    elif mode == 'nt':
        (M, K), N = a.shape, b.shape[0]
    else:
        (K, M), N = a.shape, b.shape[1]
    tm = _tile(M, 1408)
    tk = _tile(K, min(2816, MM_BLOCK_BYTES // (tm * a.dtype.itemsize)))
    tn = _tile(N, max(128, min(MM_BLOCK_BYTES // (tk * b.dtype.itemsize), MM_BLOCK_BYTES // (tm * 4)) // 128 * 128))
    nk = K // tk
    grid = (M // tm, N // tn, nk)
    dn = {'nn': (((1,), (0,)), ((), ())), 'nt': (((1,), (1,)), ((), ())), 'tn': (((0,), (0,)), ((), ()))}[mode]
    n_add = 0 if add is None else 1
    n_side = 0 if side is None else len(side[0])

    def body(a_ref, b_ref, *rest):
        add_ref = rest[0] if add is not None else None
        side_in, rest = rest[n_add:n_add + n_side], rest[n_add + n_side:]
        o_ref, side_out, rest = rest[0], rest[1:1 + n_side], rest[1 + n_side:]
        acc_ref, rest = (rest[0], rest[1:]) if nk > 1 else (None, rest)
        ids = [pl.program_id(d) for d in range(3)]
        if side is not None:
            start, finish = _xy_copies(side_in, side_out, rest, side[1])
            pl.when((ids[0] == 0) & (ids[1] == 0) & (ids[2] == 0))(start)
        acc = lax.dot_general(a_ref[...].astype(bf16), b_ref[...].astype(bf16), dn, preferred_element_type=f32)
        if nk == 1:
            o_ref[...] = (acc + add_ref[...] if add is not None else acc).astype(out_dtype)
        else:
            k = ids[2]

            @pl.when(k == 0)
            def _():
                acc_ref[...] = acc + add_ref[...] if add is not None else acc

            @pl.when(k > 0)
            def _():
                acc_ref[...] += acc

            @pl.when(k == nk - 1)
            def _():
                o_ref[...] = acc_ref[...].astype(out_dtype)
        if side is not None:
            pl.when((ids[0] == grid[0] - 1) & (ids[1] == grid[1] - 1) & (ids[2] == nk - 1))(finish)

    a_spec = (pl.BlockSpec((tk, tm), lambda i, j, k: (k, i)) if mode == 'tn'
              else pl.BlockSpec((tm, tk), lambda i, j, k: (i, k)))
    b_spec = (pl.BlockSpec((tn, tk), lambda i, j, k: (j, k)) if mode == 'nt'
              else pl.BlockSpec((tk, tn), lambda i, j, k: (k, j)))
    o_spec = pl.BlockSpec((tm, tn), lambda i, j, k: (i, j))
    any_spec = pl.BlockSpec(memory_space=pl.ANY)
    side_bufs = [] if side is None else list(side[0])
    ins, specs = [a, b], [a_spec, b_spec]
    if add is not None:
        ins.append(add)
        specs.append(o_spec)
    outs = pl.pallas_call(
        body, grid=grid, in_specs=specs + [any_spec] * n_side, out_specs=[o_spec] + [any_spec] * n_side,
        out_shape=[jax.ShapeDtypeStruct((M, N), out_dtype)] + (_xy_out_shapes(side_bufs, side[1]) if side is not None else []),
        scratch_shapes=([pltpu.VMEM((tm, tn), f32)] if nk > 1 else []) + (_xy_sems(n_side, side[1]) if side is not None else []),
        name=name,
        compiler_params=_params(*(("arbitrary",) * 3 if side is not None else ("parallel", "parallel", "arbitrary"))))(
            *ins, *side_bufs)
    return list(outs) if side is not None else outs[0]


def _shift_down(cur, prev, s):
    if s == 0:
        return cur
    ext = jnp.concatenate([prev, cur], axis=0)
    return pltpu.roll(ext, s, 0)[HALO:]


def _shift_up(cur, nxt, s):
    if s == 0:
        return cur
    ext = jnp.concatenate([cur, nxt], axis=0)
    return pltpu.roll(ext, ext.shape[0] - s, 0)[:cur.shape[0]]


def _conv_apply(cur, prev, w_ref, shifted=None):
    taps = w_ref.shape[0]
    out = None
    for i in range(taps):
        s = taps - 1 - i
        term = (shifted[s] if shifted is not None else _shift_down(cur, prev, s)) * w_ref[pl.ds(i, 1), :]
        out = term if out is None else out + term
    return out


def _row_spec(tm, w):
    return pl.BlockSpec((tm, w), lambda i: (i, 0))


def _prev_spec(tm, w):
    return pl.BlockSpec((HALO, w), lambda i: (jnp.maximum(i * (tm // HALO) - 1, 0), 0))


def _next_spec(tm, w, T):
    return pl.BlockSpec((HALO, w), lambda i: (jnp.minimum((i + 1) * (tm // HALO), T // HALO - 1), 0))


def _full_spec(shape):
    return pl.BlockSpec(shape, lambda i: (0,) * len(shape))


def _pw_fwd(name, fn, rows, consts, out_widths, tm, conv_w=None, out_dtype=f32):
    T = rows[0].shape[0]
    nr, nc = len(rows), len(consts)

    def body(*refs):
        i = pl.program_id(0)
        vals = [r[...] for r in refs[:nr]]
        p = nr
        if conv_w is not None:
            prev = jnp.where(i > 0, refs[p][...], 0.0)
            vals[0] = _conv_apply(vals[0], prev, refs[p + 1])
            p += 2
        cvals = [r[...] for r in refs[p:p + nc]]
        outs = fn(*vals, *cvals)
        for o_ref, o in zip(refs[p + nc:], outs):
            o_ref[...] = o.astype(out_dtype)

    ins = list(rows)
    specs = [_row_spec(tm, r.shape[1]) for r in rows]
    if conv_w is not None:
        ins += [rows[0], conv_w]
        specs += [_prev_spec(tm, rows[0].shape[1]), _full_spec(conv_w.shape)]
    ins += list(consts)
    specs += [_full_spec(c.shape) for c in consts]
    outs = pl.pallas_call(
        body, grid=(T // tm,), in_specs=specs,
        out_specs=[_row_spec(tm, w) for w in out_widths],
        out_shape=[jax.ShapeDtypeStruct((T, w), out_dtype) for w in out_widths], name=name,
        compiler_params=_params("parallel"))(*ins)
    return outs


def _pw_bwd(name, fn, rows, consts, cots, tm, conv_w=None, add_to_first=None, row_dtypes=None):
    T = rows[0].shape[0]
    nr, nc = len(rows), len(consts)
    flat_cots = [c for grp in cots for c in grp]
    row_dtypes = row_dtypes or [f32] * nr

    def body(*refs):
        i = pl.program_id(0)
        vals = [r[...] for r in refs[:nr]]
        p = nr
        if conv_w is not None:
            prev = jnp.where(i > 0, refs[p][...], 0.0)
            vals[0] = _conv_apply(vals[0], prev, refs[p + 1])
            p += 2
        cvals = [r[...] for r in refs[p:p + nc]]
        p += nc
        cot_vals = []
        for grp in cots:
            acc = refs[p][...]
            for q in range(1, len(grp)):
                acc = acc + refs[p + q][...]
            p += len(grp)
            cot_vals.append(acc)
        extra = None
        if add_to_first is not None:
            extra = refs[p][...]
            p += 1
        _, vjp = jax.vjp(fn, *vals, *cvals)
        grads = vjp(tuple(cot_vals))
        row_out = refs[p:p + nr]
        const_out = refs[p + nr:]
        for q in range(nr):
            g = grads[q]
            if q == 0 and extra is not None:
                g = g + extra
            row_out[q][...] = g.astype(row_dtypes[q])

        @pl.when(i == 0)
        def _():
            for q in range(nc):
                const_out[q][...] = grads[nr + q]

        @pl.when(i > 0)
        def _():
            for q in range(nc):
                const_out[q][...] += grads[nr + q]

    ins = list(rows)
    specs = [_row_spec(tm, r.shape[1]) for r in rows]
    if conv_w is not None:
        ins += [rows[0], conv_w]
        specs += [_prev_spec(tm, rows[0].shape[1]), _full_spec(conv_w.shape)]
    ins += list(consts)
    specs += [_full_spec(c.shape) for c in consts]
    ins += flat_cots
    specs += [_row_spec(tm, c.shape[1]) for c in flat_cots]
    if add_to_first is not None:
        ins.append(add_to_first)
        specs.append(_row_spec(tm, add_to_first.shape[1]))
    out_shapes = ([jax.ShapeDtypeStruct(r.shape, d) for r, d in zip(rows, row_dtypes)]
                  + [jax.ShapeDtypeStruct(c.shape, f32) for c in consts])
    out_specs = [_row_spec(tm, r.shape[1]) for r in rows] + [_full_spec(c.shape) for c in consts]
    outs = pl.pallas_call(
        body, grid=(T // tm,), in_specs=specs, out_specs=out_specs, out_shape=out_shapes, name=name,
        compiler_params=_params("arbitrary"))(*ins)
    return list(outs[:nr]), list(outs[nr:])


def _pw_conv_bwd(name, fn, rows, consts, cots, conv_w, tm, row_dtypes=None):
    T, W0 = rows[0].shape
    nr, nc = len(rows), len(consts)
    taps = conv_w.shape[0]
    nblk = T // tm
    flat_cots = [c for grp in cots for c in grp]
    row_dtypes = row_dtypes or [f32] * nr

    def body(*refs):
        i = pl.program_id(0)
        p = 0
        cur = [r[...] for r in refs[p:p + nr]]; p += nr
        nxt = [r[...] for r in refs[p:p + nr]]; p += nr
        prev = jnp.where(i > 0, refs[p][...], 0.0); p += 1
        w_ref = refs[p]; p += 1
        cvals = [r[...] for r in refs[p:p + nc]]; p += nc

        def summed(p0):
            out, q = [], p0
            for grp in cots:
                acc = refs[q][...]
                for t in range(1, len(grp)):
                    acc = acc + refs[q + t][...]
                q += len(grp)
                out.append(acc)
            return out, q

        cot_cur, p = summed(p)
        cot_nxt, p = summed(p)
        row_out, dw_ref, const_out = refs[p:p + nr], refs[p + nr], refs[p + nr + 1:]

        x_cur = cur[0]
        x_down = [_shift_down(x_cur, prev, s_) for s_ in range(taps)]
        _, vjp = jax.vjp(fn, _conv_apply(x_cur, prev, w_ref, x_down), *cur[1:], *cvals)
        grads = vjp(tuple(cot_cur))
        _, vjp_n = jax.vjp(fn, _conv_apply(nxt[0], x_cur[tm - HALO:], w_ref), *nxt[1:], *cvals)
        dc_n = jnp.where(i < nblk - 1, vjp_n(tuple(cot_nxt))[0], 0.0)
        dc = grads[0]

        @pl.when(i == 0)
        def _():
            dw_ref[...] = jnp.zeros_like(dw_ref)
            for q in range(nc):
                const_out[q][...] = jnp.zeros_like(const_out[q])

        dx = None
        for k in range(taps):
            s_ = taps - 1 - k
            term = _shift_up(dc, dc_n, s_) * w_ref[pl.ds(k, 1), :]
            dx = term if dx is None else dx + term
            dw_ref[pl.ds(k, 1), :] += jnp.sum(dc * x_down[s_], axis=0, keepdims=True)
        row_out[0][...] = dx.astype(row_dtypes[0])
        for q in range(1, nr):
            row_out[q][...] = grads[q].astype(row_dtypes[q])
        for q in range(nc):
            const_out[q][...] += grads[nr + q]

    ins = list(rows) + list(rows) + [rows[0], conv_w] + list(consts) + flat_cots + flat_cots
    specs = ([_row_spec(tm, r.shape[1]) for r in rows] + [_next_spec(tm, r.shape[1], T) for r in rows]
             + [_prev_spec(tm, W0), _full_spec(conv_w.shape)] + [_full_spec(c.shape) for c in consts]
             + [_row_spec(tm, c.shape[1]) for c in flat_cots] + [_next_spec(tm, c.shape[1], T) for c in flat_cots])
    out_shapes = ([jax.ShapeDtypeStruct(r.shape, d) for r, d in zip(rows, row_dtypes)]
                  + [jax.ShapeDtypeStruct(conv_w.shape, f32)] + [jax.ShapeDtypeStruct(c.shape, f32) for c in consts])
    out_specs = ([_row_spec(tm, r.shape[1]) for r in rows] + [_full_spec(conv_w.shape)]
                 + [_full_spec(c.shape) for c in consts])
    outs = pl.pallas_call(
        body, grid=(nblk,), in_specs=specs, out_specs=out_specs, out_shape=out_shapes, name=name,
        compiler_params=_params("arbitrary"))(*ins)
    return list(outs[:nr]), outs[nr], list(outs[nr + 1:])


def _sigmoid(x):
    return 0.5 * jnp.tanh(0.5 * x) + 0.5


def _softplus(x):
    return jnp.maximum(x, 0.0) + jnp.log(1.0 + jnp.exp(jnp.minimum(x, -x)))


def _seg_sum_impl(x, seg):
    w = x.shape[-1]
    r = lax.broadcasted_iota(jnp.int32, (w, w), 0) // seg
    c = lax.broadcasted_iota(jnp.int32, (w, w), 1) // seg
    ones = (r == c).astype(bf16)
    hi = x.astype(bf16)
    lo = (x - hi.astype(f32)).astype(bf16)
    return (jnp.dot(hi, ones, preferred_element_type=f32) + jnp.dot(lo, ones, preferred_element_type=f32))


@functools.partial(jax.custom_vjp, nondiff_argnums=(1,))
def _seg_sum(x, seg):
    return _seg_sum_impl(x, seg)


_seg_sum.defvjp(lambda x, seg: (_seg_sum_impl(x, seg), None), lambda seg, _, g: (_seg_sum_impl(g, seg),))


def _rms(x, g):
    return x * lax.rsqrt(jnp.mean(x * x, axis=-1, keepdims=True) + NORM_EPS) * g


def _rms_fn(x, g):
    return (_rms(x, g),)


def _loss_rows(x2, tgt, g):
    e = _rms(x2, g) - tgt
    return 0.5 * jnp.sum(e * e, axis=-1, keepdims=True) * (1.0 / D_MODEL)


def _rwkv_prep_fn(ps, w0, w2p, a0, a2p, g2, k_k, k_a):
    r, k, v = ps[:, 0:512], ps[:, 512:1024], ps[:, 1024:1536]
    wa, gl = ps[:, 1536:1664], ps[:, 1664:1792]
    z = w0 + jnp.dot(jnp.tanh(wa), w2p, precision=HI, preferred_element_type=f32)
    w_log = -_softplus(-z) - 0.5
    lw = -jnp.exp(w_log)
    a = _sigmoid(a0 + jnp.dot(wa, a2p, precision=HI, preferred_element_type=f32))
    g = jnp.dot(_sigmoid(gl), g2, precision=HI, preferred_element_type=f32)
    kx = k * k_k
    kk = kx * lax.rsqrt(_seg_sum(kx * kx, RWKV_HD) + L2_EPS)
    k2 = k * (1.0 + (a - 1.0) * k_a)
    return r, lw, k2, v, -kk, kk * a, g


def _rwkv_post_fn(y, r, k2, v, g, ln_w, ln_b, rk):
    mean = _seg_sum(y, RWKV_HD) * (1.0 / RWKV_HD)
    yc = y - mean
    var = _seg_sum(yc * yc, RWKV_HD) * (1.0 / RWKV_HD)
    yn = yc * lax.rsqrt(var + GN_EPS) * ln_w + ln_b
    bonus = _seg_sum(r * k2 * rk, RWKV_HD) * v
    return ((yn + bonus) * g,)


def _gdn_prep_fn(c, ab, al_p, dt_p):
    s = c * _sigmoid(c)
    q, k, v = s[:, 0:512], s[:, 512:1024], s[:, 1024:1536]
    q = q * lax.rsqrt(_seg_sum(q * q, GDN_HD) + L2_EPS) * (GDN_HD ** -0.5)
    k = k * lax.rsqrt(_seg_sum(k * k, GDN_HD) + L2_EPS)
    lane = lax.broadcasted_iota(jnp.int32, ab.shape, 1)
    gpart = -jnp.exp(al_p) * _softplus(ab + dt_p)
    gbeta = jnp.where(lane < GDN_HEADS, gpart, jnp.where(lane < 2 * GDN_HEADS, _sigmoid(ab), 0.0))
    return q, k, v, gbeta


def _gdn_post_fn(o, z, nw):
    ms = _seg_sum(o * o, GDN_HD) * (1.0 / GDN_HD)
    return (o * lax.rsqrt(ms + NORM_EPS) * nw * (z * _sigmoid(z)),)


def _mix_fn(gates, ya, yb):
    return (_sigmoid(gates[:, :D_MODEL]) * ya + _sigmoid(gates[:, D_MODEL:]) * yb,)


def _ffn_fn(c):
    hg, hu = c[:, :FFN_H], c[:, FFN_H:]
    return (hg * _sigmoid(hg) * hu,)


N_POS = 4


def _xy_out_shapes(bufs, scatter):
    return [jax.ShapeDtypeStruct((N_POS,) + tuple(b.shape[1:] if scatter else b.shape), b.dtype) for b in bufs]


def _xy_sems(n, scatter):
    sems = [pltpu.SemaphoreType.DMA((3 * n,)), pltpu.SemaphoreType.DMA((3 * n,)), pltpu.SemaphoreType.DMA((n,))]
    return sems if scatter else sems + [pltpu.SemaphoreType.DMA((3 * n,)), pltpu.SemaphoreType.DMA((3 * n,))]


def _xy_copies(in_refs, out_refs, sems, scatter):
    n = len(in_refs)
    send_sems, recv_sems, local_sems = sems[:3]

    def place():
        x, y, c = lax.axis_index("x"), lax.axis_index("y"), lax.axis_index("c")
        return x, y, c, 2 * x + y, [(1 - x, y), (x, 1 - y), (1 - x, 1 - y)]

    def half(ref, a, which):
        rows = in_refs[a].shape[0] // 2
        return ref.at[pl.ds(pl.multiple_of(which * rows, HALO), rows)]

    def ici(a, k, src, dst, peer, c):
        return pltpu.make_async_remote_copy(
            src_ref=src, dst_ref=dst, send_sem=send_sems.at[3 * a + k], recv_sem=recv_sems.at[3 * a + k],
            device_id=(peer[0], peer[1], c), device_id_type=pl.DeviceIdType.MESH)

    def outgoing():
        x, y, c, me, peers = place()
        own = [pltpu.make_async_copy(in_refs[a].at[me] if scatter else in_refs[a], out_refs[a].at[me], local_sems.at[a])
               for a in range(n)]
        if scatter:
            sends = [ici(a, k, in_refs[a].at[2 * p[0] + p[1]], out_refs[a].at[me], p, c)
                     for a in range(n) for k, p in enumerate(peers)]
        else:
            sends = [ici(a, k, half(in_refs[a], a, c), half(out_refs[a].at[me], a, c), p, c)
                     for a in range(n) for k, p in enumerate(peers)]
        return own, sends

    def arrivals():
        x, y, c, me, peers = place()
        if scatter:
            return [ici(a, k, in_refs[a].at[me], out_refs[a].at[2 * p[0] + p[1]], p, c)
                    for a in range(n) for k, p in enumerate(peers)]
        return [ici(a, k, half(in_refs[a], a, c), half(out_refs[a].at[2 * p[0] + p[1]], a, c), p, c)
                for a in range(n) for k, p in enumerate(peers)]

    def to_sibling(mine):
        x, y, c, me, peers = place()
        which = c if mine else 1 - c
        return [pltpu.make_async_remote_copy(
            src_ref=half(out_refs[a].at[2 * p[0] + p[1]], a, which), dst_ref=half(out_refs[a].at[2 * p[0] + p[1]], a, which),
            send_sem=sems[3].at[3 * a + k], recv_sem=sems[4].at[3 * a + k],
            device_id=(x, y, 1 - c), device_id_type=pl.DeviceIdType.MESH) for a in range(n) for k, p in enumerate(peers)]

    def start():
        own, sends = outgoing()
        for cp in own + sends:
            cp.start()

    def finish():
        if scatter:
            for cp in arrivals():
                cp.wait_recv()
        else:
            passed = to_sibling(True)
            for cp, fwd in zip(arrivals(), passed):
                cp.wait_recv()
                fwd.start()
            for cp in to_sibling(False):
                cp.wait_recv()
            for fwd in passed:
                fwd.wait_send()
        own, sends = outgoing()
        for cp in sends:
            cp.wait_send()
        for cp in own:
            cp.wait()

    return start, finish


_NN, _NT, _TN = 'hcs,hsd->hcd', 'hcd,hsd->hcs', 'hcd,hce->hde'


def _lo(spec, a, b):
    return jnp.einsum(spec, a.astype(bf16), b.astype(bf16), preferred_element_type=f32)


@jax.custom_vjp
def _bmm(a, b):
    return _lo(_NN, a, b)


_bmm.defvjp(lambda a, b: (_lo(_NN, a, b), (a, b)), lambda ab, g: (_lo(_NT, g, ab[1]), _lo(_TN, ab[0], g)))


@jax.custom_vjp
def _bmm_nt(a, b):
    return _lo(_NT, a, b)


_bmm_nt.defvjp(lambda a, b: (_lo(_NT, a, b), (a, b)), lambda ab, g: (_lo(_NN, g, ab[1]), _lo(_TN, g, ab[0])))


@jax.custom_vjp
def _bmm_tn(a, b):
    return _lo(_TN, a, b)


_bmm_tn.defvjp(lambda a, b: (_lo(_TN, a, b), (a, b)), lambda ab, g: (_lo(_NT, ab[1], g), _lo(_NN, ab[0], g)))


def _masks(H, C):
    row = lax.broadcasted_iota(jnp.int32, (H, C, C), 1)
    col = lax.broadcasted_iota(jnp.int32, (H, C, C), 2)
    return row, col


def _tri_inv_impl(L):
    H, C, _ = L.shape
    row, col = _masks(H, C)
    eye = (row == col).astype(f32)
    base = 16
    same = (row // base) == (col // base)
    Ld = jnp.where(same, L, 0.0)
    X = -Ld
    inv = eye + X
    for _ in range(3):
        X = _bmm(X, X)
        inv = _bmm(inv, eye + X)
    if C == base:
        return inv
    N = _bmm(inv, L - Ld)
    out = eye - N
    levels = C // base
    P = N
    span = 2
    while span < levels:
        P = _bmm(P, P)
        out = _bmm(out, eye + P)
        span *= 2
    return _bmm(out, inv)


@jax.custom_vjp
def _tri_inv(L):
    return _tri_inv_impl(L)


def _tri_inv_fwd(L):
    T = _tri_inv_impl(L)
    return T, T


def _tri_inv_bwd(T, dT):
    return (-_bmm_nt(_bmm_tn(T, dT), T),)


_tri_inv.defvjp(_tri_inv_fwd, _tri_inv_bwd)


def _cumsum_impl(x, reverse):
    C = x.shape[1]
    row = lax.broadcasted_iota(jnp.int32, x.shape, 1)
    s = 1
    while s < C:
        if reverse:
            x = x + jnp.where(row < C - s, pltpu.roll(x, C - s, 1), 0.0)
        else:
            x = x + jnp.where(row >= s, pltpu.roll(x, s, 1), 0.0)
        s *= 2
    return x


@jax.custom_vjp
def _cumsum(x):
    return _cumsum_impl(x, False)


_cumsum.defvjp(lambda x: (_cumsum_impl(x, False), None), lambda _, g: (_cumsum_impl(g, True),))


def _wkv_chunk(Z, r, lw, k, v, a, b):
    H, C, D = r.shape
    row, col = _masks(H, C)
    incl, strict = row >= col, row > col
    cw = _cumsum(lw)
    cwp = cw - lw
    cwl = jnp.sum(lw, axis=1, keepdims=True)
    en = jnp.exp(-cw)
    at, rt, bt, kt = a * jnp.exp(cwp), r * jnp.exp(cw), b * en, k * en
    Lab = jnp.where(strict, _bmm_nt(at, bt), 0.0)
    Lak = jnp.where(strict, _bmm_nt(at, kt), 0.0)
    Tm = _tri_inv(-Lab)
    U = _bmm(Tm, _bmm(at, Z) + _bmm(Lak, v))
    Rb = jnp.where(incl, _bmm_nt(rt, bt), 0.0)
    Rk = jnp.where(incl, _bmm_nt(rt, kt), 0.0)
    y = _bmm(rt, Z) + _bmm(Rb, U) + _bmm(Rk, v)
    ed = jnp.exp(cwl - cw)
    zdec = jnp.swapaxes(jnp.broadcast_to(jnp.exp(cwl), (H, Z.shape[2], D)), 1, 2)
    Z1 = Z * zdec + _bmm_tn(b * ed, U) + _bmm_tn(k * ed, v)
    return y, Z1


def _gdn_chunk(S, q, k, v, g, beta):
    H, C, D = q.shape
    row, col = _masks(H, C)
    incl, strict = row >= col, row > col
    gc = _cumsum(g)
    diff = gc - jnp.swapaxes(gc, 1, 2)
    decay = jnp.where(incl, jnp.exp(jnp.where(incl, diff, 0.0)), 0.0)
    gl = jnp.sum(g, axis=1, keepdims=True)
    kb, vb = k * beta, v * beta
    L = jnp.where(strict, _bmm_nt(kb, k) * decay, 0.0)
    Tm = _tri_inv(L)
    egc = jnp.exp(gc)
    u = _bmm(Tm, vb)
    wk = _bmm(Tm, kb * egc)
    attn = jnp.where(incl, _bmm_nt(q, k) * decay, 0.0)
    v_new = u - _bmm(wk, S)
    o = _bmm(q * egc, S) + _bmm(attn, v_new)
    S1 = S * jnp.exp(gl) + _bmm_tn(k * jnp.exp(gl - gc), v_new)
    return o, S1


def _wkv_block(Z, r, lw, k, v, a, b):
    lane = lax.broadcasted_iota(jnp.int32, (r.shape[0], 128), 1)
    low = lane < RWKV_HD

    def heads(t):
        out = []
        for p in range(RWKV_HEADS // 2):
            pair = t[:, 128 * p:128 * (p + 1)]
            out += [jnp.where(low, pair, 0.0), jnp.where(low, 0.0, pair)]
        return jnp.concatenate([t[None] for t in out], axis=0)

    y, Z1 = _wkv_chunk(Z, *[heads(t) for t in (r, lw, k, v, a, b)])
    return jnp.concatenate([y[2 * p] + y[2 * p + 1] for p in range(RWKV_HEADS // 2)], axis=1), Z1


def _gdn_block(S, q, k, v, gbeta):
    heads = lambda t: jnp.concatenate([t[None, :, GDN_HD * h:GDN_HD * (h + 1)] for h in range(GDN_HEADS)], axis=0)
    src = lax.broadcasted_iota(jnp.int32, (W_AB, 2 * GDN_W), 0)
    dst = lax.broadcasted_iota(jnp.int32, (W_AB, 2 * GDN_W), 1) // GDN_HD
    spread = jnp.dot(gbeta, (src == dst).astype(f32), precision=HI, preferred_element_type=f32)
    o, S1 = _gdn_chunk(S, heads(q), heads(k), heads(v), heads(spread[:, :GDN_W]), heads(spread[:, GDN_W:]))
    return jnp.concatenate([o[h] for h in range(GDN_HEADS)], axis=1), S1


def _scan_fwd(name, block_fn, ins, C, H, dh, w_out, side=None):
    T = ins[0].shape[0]
    n_in = len(ins)
    nblk = T // C
    n_side = 0 if side is None else len(side[0])

    def body(*refs):
        in_refs, refs = refs[:n_in], refs[n_in:]
        side_in, refs = refs[:n_side], refs[n_side:]
        y_ref, zs_ref, refs = refs[0], refs[1], refs[2:]
        side_out, refs = refs[:n_side], refs[n_side:]
        z_scr = refs[0]
        if side is not None:
            start, finish = _xy_copies(side_in, side_out, refs[1:], side[1])
            pl.when(pl.program_id(0) == 0)(start)

        @pl.when(pl.program_id(0) == 0)
        def _():
            z_scr[...] = jnp.zeros_like(z_scr)

        Z = z_scr[...]
        zs_ref[0] = Z
        y, Z1 = block_fn(Z, *[r[...] for r in in_refs])
        y_ref[...] = y
        z_scr[...] = Z1
        if side is not None:
            pl.when(pl.program_id(0) == nblk - 1)(finish)

    side_bufs = [] if side is None else list(side[0])
    any_spec = pl.BlockSpec(memory_space=pl.ANY)
    return pl.pallas_call(
        body, grid=(nblk,),
        in_specs=[pl.BlockSpec((C, a.shape[1]), lambda i: (i, 0)) for a in ins] + [any_spec] * n_side,
        out_specs=[pl.BlockSpec((C, w_out), lambda i: (i, 0)), pl.BlockSpec((1, H, dh, dh), lambda i: (i, 0, 0, 0))]
        + [any_spec] * n_side,
        out_shape=[jax.ShapeDtypeStruct((T, w_out), f32), jax.ShapeDtypeStruct((T // C, H, dh, dh), f32)]
        + (_xy_out_shapes(side_bufs, side[1]) if side is not None else []),
        scratch_shapes=[pltpu.VMEM((H, dh, dh), f32)] + (_xy_sems(n_side, side[1]) if side is not None else []), name=name,
        compiler_params=_params("arbitrary"))(*ins, *side_bufs)


def _scan_bwd(name, block_fn, ins, dy, zs, C, side=None):
    T = ins[0].shape[0]
    _, H, dh, _ = zs.shape
    n_in = len(ins)
    nblk = T // C
    n_side = 0 if side is None else len(side[0])

    def body(*refs):
        in_refs, dy_ref, zs_ref, refs = refs[:n_in], refs[n_in], refs[n_in + 1], refs[n_in + 2:]
        side_in, refs = refs[:n_side], refs[n_side:]
        out_refs, refs = refs[:n_in], refs[n_in:]
        side_out, refs = refs[:n_side], refs[n_side:]
        dz_scr = refs[0]
        if side is not None:
            start, finish = _xy_copies(side_in, side_out, refs[1:], side[1])
            pl.when(pl.program_id(0) == 0)(start)

        @pl.when(pl.program_id(0) == 0)
        def _():
            dz_scr[...] = jnp.zeros_like(dz_scr)

        _, vjp = jax.vjp(block_fn, zs_ref[0], *[r[...] for r in in_refs])
        grads = vjp((dy_ref[...], dz_scr[...]))
        dz_scr[...] = grads[0]
        for o_ref, gval in zip(out_refs, grads[1:]):
            o_ref[...] = gval
        if side is not None:
            pl.when(pl.program_id(0) == nblk - 1)(finish)

    side_bufs = [] if side is None else list(side[0])
    any_spec = pl.BlockSpec(memory_space=pl.ANY)
    rev = lambda i: (nblk - 1 - i, 0)
    return pl.pallas_call(
        body, grid=(nblk,),
        in_specs=[pl.BlockSpec((C, a.shape[1]), rev) for a in ins]
        + [pl.BlockSpec((C, dy.shape[1]), rev), pl.BlockSpec((1, H, dh, dh), lambda i: (nblk - 1 - i, 0, 0, 0))]
        + [any_spec] * n_side,
        out_specs=[pl.BlockSpec((C, a.shape[1]), rev) for a in ins] + [any_spec] * n_side,
        out_shape=[jax.ShapeDtypeStruct(a.shape, f32) for a in ins]
        + (_xy_out_shapes(side_bufs, side[1]) if side is not None else []),
        scratch_shapes=[pltpu.VMEM((H, dh, dh), f32)] + (_xy_sems(n_side, side[1]) if side is not None else []), name=name,
        compiler_params=_params("arbitrary"))(*ins, dy, zs, *side_bufs)


def _loss_call(x2, tgt, g, tm):
    T, W = x2.shape

    def body(x_ref, t_ref, g_ref, dx_ref, dg_ref, l_ref):
        i = pl.program_id(0)
        tv = t_ref[...]
        l, vjp = jax.vjp(lambda xv, gv: _loss_rows(xv, tv, gv), x_ref[...], g_ref[...])
        dx, dg = vjp(jnp.ones_like(l))
        dx_ref[...] = dx
        tot = jnp.zeros((1, 128), f32) + jnp.sum(l)

        @pl.when(i == 0)
        def _():
            dg_ref[...] = dg
            l_ref[...] = tot

        @pl.when(i > 0)
        def _():
            dg_ref[...] += dg
            l_ref[...] += tot

    return pl.pallas_call(
        body, grid=(T // tm,),
        in_specs=[_row_spec(tm, W), _row_spec(tm, W), _full_spec(g.shape)],
        out_specs=[_row_spec(tm, W), _full_spec(g.shape), _full_spec((1, 128))],
        out_shape=[jax.ShapeDtypeStruct((T, W), f32), jax.ShapeDtypeStruct(g.shape, f32),
                   jax.ShapeDtypeStruct((1, 128), f32)], name="loss_head",
        compiler_params=_params("arbitrary"))(x2, tgt, g)


def _local_step(x, tgt, W, late=None):
    row = lambda a: a.reshape(1, -1)
    wp = W['w_in_pad']
    w_rwkv, w_qkv, w_z = wp[:, :OFF_QKV], wp[:, OFF_QKV:OFF_Z], wp[:, OFF_Z:OFF_GATES]
    w_gates, w_ab = wp[:, OFF_GATES:OFF_AB], wp[:, OFF_AB:]
    mu = row(W['rwkv_mu'])
    mixw = jnp.concatenate([mu, 1.0 - mu], axis=0)
    zpad = jnp.zeros((64, RWKV_W), f32)
    w2p = jnp.concatenate([W['rwkv_w2'], zpad], axis=0)
    a2p = jnp.concatenate([zpad, W['rwkv_a2']], axis=0)
    rw_consts = [row(W['rwkv_w0']), w2p, row(W['rwkv_a0']), a2p, W['rwkv_g2'], row(W['rwkv_k_k']), row(W['rwkv_k_a'])]
    post_consts = [row(W['rwkv_ln_w']), row(W['rwkv_ln_b']), row(W['rwkv_r_k'])]
    pad4 = lambda a: jnp.pad(row(a), ((0, 0), (0, W_AB - GDN_HEADS)))
    gd_consts = [pad4(W['gdn_a_log']), pad4(W['gdn_dt_bias'])]
    nw_t = jnp.tile(row(W['gdn_norm_w']), (1, GDN_HEADS))
    g1, g2n, gf = row(W['norm1_g']), row(W['norm2_g']), row(W['final_g'])

    (u,) = _pw_fwd("norm1", _rms_fn, [x], [g1], [D_MODEL], 256, out_dtype=bf16)
    p_rwkv = _mm(u, w_rwkv, 'nn', "in_rwkv")
    qkv_raw = _mm(u, w_qkv, 'nn', "in_qkv")
    z = _mm(u, w_z, 'nn', "in_z")
    gates = _mm(u, w_gates, 'nn', "in_gates")
    ab = _mm(u, w_ab, 'nn', "in_ab")

    r, lw, k2, v, a_, b_, g = _pw_fwd("rwkv_prep", _rwkv_prep_fn, [p_rwkv], rw_consts, [RWKV_W] * 7, 256, conv_w=mixw)
    wkv_in = [r, lw, k2, v, a_, b_]
    y, zs_wkv, *gathered = _scan_fwd("wkv_fwd", _wkv_block, wkv_in, WKV_CHUNK, RWKV_HEADS, 2 * RWKV_HD, RWKV_W,
                                     side=None if late is None else (late['shards'], False))
    if late is not None:
        W = dict(W, **late['assemble'](gathered))
    (ya_in,) = _pw_fwd("rwkv_post", _rwkv_post_fn, [y, r, k2, v, g], post_consts, [RWKV_W], 256, out_dtype=bf16)
    ya = _mm(ya_in, W['rwkv_proj'], 'nn', "rwkv_proj")

    gq, gk, gv, gbeta = _pw_fwd("gdn_prep", _gdn_prep_fn, [qkv_raw, ab], gd_consts, [GDN_W] * 3 + [W_AB], 256,
                                conv_w=W['gdn_conv_w'])
    gdn_in = [gq, gk, gv, gbeta]
    o, zs_gdn = _scan_fwd("gdn_fwd", _gdn_block, gdn_in, GDN_CHUNK, GDN_HEADS, GDN_HD, GDN_W)
    (yb_in,) = _pw_fwd("gdn_post", _gdn_post_fn, [o, z], [nw_t], [GDN_W], 256, out_dtype=bf16)
    yb = _mm(yb_in, W['gdn_proj'], 'nn', "gdn_proj")

    (mixed,) = _pw_fwd("mix", _mix_fn, [gates, ya, yb], [], [D_MODEL], 256, out_dtype=bf16)
    x1 = _mm(mixed, W['w_out'], 'nn', "w_out", add=x)
    (u2,) = _pw_fwd("norm2", _rms_fn, [x1], [g2n], [D_MODEL], 256, out_dtype=bf16)
    h = _mm(u2, W['ffn_up'], 'nn', "ffn_up")
    (act,) = _pw_fwd("ffn_act", _ffn_fn, [h], [], [FFN_H], 128, conv_w=W['ffn_conv_w'], out_dtype=bf16)
    x2 = _mm(act, W['ffn_down'], 'nn', "ffn_down", add=x1)

    G = {}
    dx2, dgf, loss = _loss_call(x2, tgt, gf, 256)
    G['final_g'] = dgf
    dact = _mm(dx2, W['ffn_down'], 'nt', "d_act")
    G['ffn_down'] = _mm(act, dx2, 'tn', "g_ffn_down", out_dtype=bf16)
    (dh,), G['ffn_conv_w'], _ = _pw_conv_bwd("ffn_act_bwd", _ffn_fn, [h], [], [(dact,)], W['ffn_conv_w'], 128,
                                             row_dtypes=[bf16])
    du2 = _mm(dh, W['ffn_up'], 'nt', "d_u2")
    G['ffn_up'] = _mm(u2, dh, 'tn', "g_ffn_up", out_dtype=bf16)
    (dx1,), (G['norm2_g'],) = _pw_bwd("norm2_bwd", _rms_fn, [x1], [g2n], [(du2,)], 256, add_to_first=dx2)
    dmixed = _mm(dx1, W['w_out'], 'nt', "d_mixed")
    G['w_out'] = _mm(mixed, dx1, 'tn', "g_w_out", out_dtype=bf16)
    (dgates, dya, dyb), _ = _pw_bwd("mix_bwd", _mix_fn, [gates, ya, yb], [], [(dmixed,)], 256, row_dtypes=[bf16] * 3)
    dya_in = _mm(dya, W['rwkv_proj'], 'nt', "d_ya_in")
    G['rwkv_proj'] = _mm(ya_in, dya, 'tn', "g_rwkv_proj", out_dtype=bf16)
    dyb_in = _mm(dyb, W['gdn_proj'], 'nt', "d_yb_in")
    G['gdn_proj'] = _mm(yb_in, dyb, 'tn', "g_gdn_proj", out_dtype=bf16)

    (do, dz), (dnw_t,) = _pw_bwd("gdn_post_bwd", _gdn_post_fn, [o, z], [nw_t], [(dyb_in,)], 256, row_dtypes=[f32, bf16])
    G['gdn_norm_w'] = dnw_t.reshape(GDN_HEADS, GDN_HD).sum(axis=0)
    dgq, dgk, dgv, dgbeta = _scan_bwd("gdn_bwd", _gdn_block, gdn_in, do, zs_gdn, GDN_CHUNK)
    (dqkv_raw, dab), G['gdn_conv_w'], (dal_p, ddt_p) = _pw_conv_bwd(
        "gdn_prep_bwd", _gdn_prep_fn, [qkv_raw, ab], gd_consts, [(dgq,), (dgk,), (dgv,), (dgbeta,)], W['gdn_conv_w'], 256,
        row_dtypes=[bf16, bf16])
    G['gdn_a_log'], G['gdn_dt_bias'] = dal_p[0, :GDN_HEADS], ddt_p[0, :GDN_HEADS]

    (dy, dr1, dk21, dv1, dg_), (G['rwkv_ln_w'], G['rwkv_ln_b'], G['rwkv_r_k']) = _pw_bwd(
        "rwkv_post_bwd", _rwkv_post_fn, [y, r, k2, v, g], post_consts, [(dya_in,)], 256)
    dr2, dlw, dk22, dv2, da_, db_, *G['_arrived'] = _scan_bwd(
        "wkv_bwd", _wkv_block, wkv_in, dy, zs_wkv, WKV_CHUNK, side=None if late is None else (late['slabs'](G), True))
    (dp_rwkv,), dmixw, rw_grads = _pw_conv_bwd(
        "rwkv_prep_bwd", _rwkv_prep_fn, [p_rwkv], rw_consts,
        [(dr1, dr2), (dlw,), (dk21, dk22), (dv1, dv2), (da_,), (db_,), (dg_,)], mixw, 256, row_dtypes=[bf16])
    G['rwkv_w0'], dw2p, G['rwkv_a0'], da2p, G['rwkv_g2'], G['rwkv_k_k'], G['rwkv_k_a'] = rw_grads
    G['rwkv_w2'], G['rwkv_a2'] = dw2p[:64], da2p[64:]
    G['rwkv_mu'] = dmixw[0] - dmixw[1]

    dp = jnp.concatenate([dp_rwkv, dqkv_raw, dz, dgates, dab], axis=1)
    G['w_in_pad'] = _mm(u, dp, 'tn', "g_w_in", out_dtype=bf16)
    if late is None:
        du = _mm(dp, wp, 'nt', "d_u")
    else:
        du, *G['_arrived_w_in'] = _mm(dp, wp, 'nt', "d_u", side=(late['w_in_slabs'](G), True))
    (dx,), (G['norm1_g'],) = _pw_bwd("norm1_bwd", _rms_fn, [x], [g1], [(du,)], 256, add_to_first=dx1)
    return loss, dx, G


def _pad_w_in(w):
    return jnp.concatenate([w[:, :OFF_GATES], w[:, OFF_GATES + 8:], w[:, OFF_GATES:OFF_GATES + 8],
                            jnp.zeros((w.shape[0], W_AB - 8), w.dtype)], axis=1)


def _unpad_w_in(wp):
    return jnp.concatenate([wp[:, :OFF_GATES], wp[:, OFF_AB:OFF_AB + 8], wp[:, OFF_GATES:OFF_AB]], axis=1)


BIG = ('w_in', 'rwkv_proj', 'gdn_proj', 'w_out', 'ffn_up', 'ffn_down')
SMALL_SHARDED = ('rwkv_w2', 'rwkv_a2', 'rwkv_g2', 'gdn_conv_w', 'ffn_conv_w')


def _rows128(shape):
    n = 1
    for d in shape:
        n *= d
    return -(-n // LANES)


def _pack128(arrays):
    parts = []
    for a in arrays:
        flat = a.reshape(-1)
        rows = _rows128(a.shape)
        parts.append(jnp.pad(flat, (0, rows * LANES - flat.shape[0])).reshape(rows, LANES))
    buf = jnp.concatenate(parts, axis=0)
    return jnp.pad(buf, ((0, -buf.shape[0] % HALO), (0, 0)))


def _unpack128(buf, shapes):
    out, off = [], 0
    for s in shapes:
        rows, n = _rows128(s), 1
        for d in s:
            n *= d
        out.append(buf[off:off + rows].reshape(-1)[:n].reshape(s))
        off += rows
    return out


def _row_tile(r, c):
    best = None
    for d in range(HALO, r + 1, HALO):
        if r % d == 0 and d * c * 4 <= TILE_BYTES:
            best = d
    return best if best is not None else r


def _xy_exchange(name, bufs, scatter):
    n = len(bufs)

    def body(*refs):
        start, finish = _xy_copies(refs[:n], refs[n:2 * n], refs[2 * n:], scatter)
        start()
        finish()

    return pl.pallas_call(
        body, in_specs=[pl.BlockSpec(memory_space=pl.ANY)] * n, out_specs=[pl.BlockSpec(memory_space=pl.ANY)] * n,
        out_shape=_xy_out_shapes(bufs, scatter), scratch_shapes=_xy_sems(n, scatter), name=name)(*bufs)


def _sibling_exchange(name, bufs):
    n = len(bufs)

    def body(*refs):
        in_refs, out_refs, send_sems, recv_sems = refs[:n], refs[n:2 * n], refs[2 * n], refs[2 * n + 1]
        x, y, c = lax.axis_index("x"), lax.axis_index("y"), lax.axis_index("c")
        copies = [pltpu.make_async_remote_copy(
            src_ref=in_refs[a], dst_ref=out_refs[a], send_sem=send_sems.at[a], recv_sem=recv_sems.at[a],
            device_id=(x, y, 1 - c), device_id_type=pl.DeviceIdType.MESH) for a in range(n)]
        for cp in copies:
            cp.start()
        for cp in copies:
            cp.wait()

    return pl.pallas_call(
        body, in_specs=[pl.BlockSpec(memory_space=pl.ANY)] * n, out_specs=[pl.BlockSpec(memory_space=pl.ANY)] * n,
        out_shape=[jax.ShapeDtypeStruct(b.shape, b.dtype) for b in bufs],
        scratch_shapes=[pltpu.SemaphoreType.DMA((n,)), pltpu.SemaphoreType.DMA((n,))], name=name)(*bufs)


def _sum_slots(name, buf):
    _, R, L = buf.shape
    tr = _row_tile(R, L)

    def body(b_ref, o_ref):
        part = lambda s: b_ref[s].astype(f32)
        o_ref[...] = ((part(0) + part(1)) + part(2)) + part(3)

    return pl.pallas_call(
        body, grid=(R // tr,),
        in_specs=[pl.BlockSpec((N_POS, tr, L), lambda i: (0, i, 0))],
        out_specs=pl.BlockSpec((tr, L), lambda i: (i, 0)),
        out_shape=jax.ShapeDtypeStruct((R, L), f32), name=name,
        compiler_params=_params("parallel"))(buf)


def _adamw(name, w, ga, gb, m, v):
    R, L = w.shape
    tr = _row_tile(R, L)
    c1 = 1.0 / (1.0 - ADAM_B1 ** ADAM_STEP)
    c2 = 1.0 / (1.0 - ADAM_B2 ** ADAM_STEP)

    def body(w_ref, ga_ref, gb_ref, m_ref, v_ref, g_out, d_out, m_out, v_out):
        g = ga_ref[...] + gb_ref[...]
        m_new = ADAM_B1 * m_ref[...] + (1.0 - ADAM_B1) * g
        v_new = ADAM_B2 * v_ref[...] + (1.0 - ADAM_B2) * (g * g)
        g_out[...] = g
        m_out[...] = m_new
        v_out[...] = v_new
        d_out[...] = -ADAM_LR * ((m_new * c1) / (jnp.sqrt(v_new * c2) + ADAM_EPS) + ADAM_WD * w_ref[...])

    spec = pl.BlockSpec((tr, L), lambda i: (i, 0))
    return pl.pallas_call(
        body, grid=(R // tr,), in_specs=[spec] * 5, out_specs=[spec] * 4,
        out_shape=[jax.ShapeDtypeStruct((R, L), f32)] * 4, name=name,
        compiler_params=_params("parallel"))(w, ga, gb, m, v)


def _step(x, loss_target, P, M, V):
    shapes = {n: tuple(P[n].shape) for n in WEIGHTS}
    sh_shapes = [shapes[n] for n in SMALL_SHARDED]
    packed = SMALL_SHARDED + SMALL
    late_names = BIG[1:]

    def whole(n, g):
        return g.reshape(-1, g.shape[2]) if n in ROW_SHARDED else jnp.concatenate([g[j] for j in range(N_POS)], axis=1)

    def slabs(G, n, dtype=f32):
        r, c = shapes[n]
        full = G[n].astype(dtype)
        return full.reshape(N_POS, r, c) if n in ROW_SHARDED else full.reshape(r, N_POS, c).transpose(1, 0, 2)

    g_w_in, g_small = _xy_exchange("gather_w_in", [P['w_in'].astype(bf16), _pack128([P[n] for n in SMALL_SHARDED])],
                                   scatter=False)
    W = {n: P[n] for n in SMALL}
    W['w_in_pad'] = _pad_w_in(whole('w_in', g_w_in))
    per_pos = [_unpack128(g_small[j], sh_shapes) for j in range(N_POS)]
    for q, n in enumerate(SMALL_SHARDED):
        W[n] = jnp.concatenate([per_pos[j][q] for j in range(N_POS)], axis=1)
    late = dict(shards=[P[n].astype(bf16) for n in late_names],
                assemble=lambda gathered: {n: whole(n, g) for n, g in zip(late_names, gathered)},
                slabs=lambda G: [slabs(G, n, bf16) for n in late_names],
                w_in_slabs=lambda G: [slabs({'w_in': _unpad_w_in(G['w_in_pad'])}, 'w_in', bf16)])

    loss_rows, dx, G = _local_step(x, loss_target, W, late)
    arrived_late = G.pop('_arrived')
    (arrived_w_in,) = G.pop('_arrived_w_in')
    G.pop('w_in_pad')

    small_slabs = jnp.stack([_pack128([slabs(G, n)[j] for n in SMALL_SHARDED] + [G[n] for n in SMALL]) for j in range(N_POS)])
    (arrived_small,) = _xy_exchange("scatter_small", [small_slabs], scatter=True)
    contributions = [arrived_w_in] + list(arrived_late) + [arrived_small]
    tags = list(BIG) + ['small']
    plane = [_sum_slots("sum_" + t, cbuf) for t, cbuf in zip(tags, contributions)]
    sibling = _sibling_exchange("sibling_grads", plane)

    out = {}
    names4 = ('grad', 'delta', 'new_m', 'new_v')
    for q, n in enumerate(BIG):
        for tag, t in zip(names4, _adamw("adamw_" + n, P[n], plane[q], sibling[q], M[n], V[n])):
            out[tag + '_' + n] = t
    small_out = _adamw("adamw_small", _pack128([P[n] for n in packed]), plane[-1], sibling[-1],
                       _pack128([M[n] for n in packed]), _pack128([V[n] for n in packed]))
    for tag, buf in zip(names4, small_out):
        for n, t in zip(packed, _unpack128(buf, [shapes[n] for n in packed])):
            out[tag + '_' + n] = t
    loss = lax.psum(loss_rows[0, 0], ("x", "y", "c"))
    return loss, dx, out


def kernel(x, norm1_g, w_in, rwkv_mu, rwkv_w0, rwkv_w2, rwkv_a0, rwkv_a2, rwkv_g2, rwkv_k_k, rwkv_k_a, rwkv_r_k, rwkv_ln_w, rwkv_ln_b, rwkv_proj, gdn_conv_w, gdn_a_log, gdn_dt_bias, gdn_norm_w, gdn_proj, w_out, norm2_g, ffn_up, ffn_conv_w, ffn_down, final_g, loss_target, m_norm1_g, m_w_in, m_rwkv_mu, m_rwkv_w0, m_rwkv_w2, m_rwkv_a0, m_rwkv_a2, m_rwkv_g2, m_rwkv_k_k, m_rwkv_k_a, m_rwkv_r_k, m_rwkv_ln_w, m_rwkv_ln_b, m_rwkv_proj, m_gdn_conv_w, m_gdn_a_log, m_gdn_dt_bias, m_gdn_norm_w, m_gdn_proj, m_w_out, m_norm2_g, m_ffn_up, m_ffn_conv_w, m_ffn_down, m_final_g, v_norm1_g, v_w_in, v_rwkv_mu, v_rwkv_w0, v_rwkv_w2, v_rwkv_a0, v_rwkv_a2, v_rwkv_g2, v_rwkv_k_k, v_rwkv_k_a, v_rwkv_r_k, v_rwkv_ln_w, v_rwkv_ln_b, v_rwkv_proj, v_gdn_conv_w, v_gdn_a_log, v_gdn_dt_bias, v_gdn_norm_w, v_gdn_proj, v_w_out, v_norm2_g, v_ffn_up, v_ffn_conv_w, v_ffn_down, v_final_g):
    weights = (norm1_g, w_in, rwkv_mu, rwkv_w0, rwkv_w2, rwkv_a0, rwkv_a2, rwkv_g2, rwkv_k_k, rwkv_k_a, rwkv_r_k, rwkv_ln_w,
               rwkv_ln_b, rwkv_proj, gdn_conv_w, gdn_a_log, gdn_dt_bias, gdn_norm_w, gdn_proj, w_out, norm2_g, ffn_up,
               ffn_conv_w, ffn_down, final_g)
    m_in = (m_norm1_g, m_w_in, m_rwkv_mu, m_rwkv_w0, m_rwkv_w2, m_rwkv_a0, m_rwkv_a2, m_rwkv_g2, m_rwkv_k_k, m_rwkv_k_a,
            m_rwkv_r_k, m_rwkv_ln_w, m_rwkv_ln_b, m_rwkv_proj, m_gdn_conv_w, m_gdn_a_log, m_gdn_dt_bias, m_gdn_norm_w,
            m_gdn_proj, m_w_out, m_norm2_g, m_ffn_up, m_ffn_conv_w, m_ffn_down, m_final_g)
    v_in = (v_norm1_g, v_w_in, v_rwkv_mu, v_rwkv_w0, v_rwkv_w2, v_rwkv_a0, v_rwkv_a2, v_rwkv_g2, v_rwkv_k_k, v_rwkv_k_a,
            v_rwkv_r_k, v_rwkv_ln_w, v_rwkv_ln_b, v_rwkv_proj, v_gdn_conv_w, v_gdn_a_log, v_gdn_dt_bias, v_gdn_norm_w,
            v_gdn_proj, v_w_out, v_norm2_g, v_ffn_up, v_ffn_conv_w, v_ffn_down, v_final_g)
    drop = lambda n, a: a if n == 'final_g' else a[0]
    P = {n: drop(n, a) for n, a in zip(WEIGHTS, weights)}
    M = {n: drop(n, a) for n, a in zip(WEIGHTS, m_in)}
    V = {n: drop(n, a) for n, a in zip(WEIGHTS, v_in)}
    loss, dx, out = _step(x[0], loss_target[0], P, M, V)
    lift = lambda n, a: a if n == 'final_g' else a[None]
    res = [loss, dx[None]]
    for tag in ('grad', 'delta', 'new_m', 'new_v'):
        res += [lift(n, out[tag + '_' + n]) for n in WEIGHTS]
    return tuple(res)
```

```python
import functools

import jax
import jax.numpy as jnp
from jax import lax
from jax.experimental import pallas as pl
from jax.experimental.pallas import tpu as pltpu

f32 = jnp.float32
bf16 = jnp.bfloat16
HI = lax.Precision.HIGHEST

D_MODEL = 1024
RWKV_HEADS, RWKV_HD, RWKV_W = 8, 64, 512
GDN_HEADS, GDN_HD, GDN_W = 4, 128, 512
FFN_H = 2816
NORM_EPS, L2_EPS, GN_EPS = 1e-6, 1e-6, 64e-5
W_AB = 256
OFF_QKV, OFF_Z, OFF_GATES, OFF_AB = 1792, 3328, 3840, 5888
W_IN_PAD = OFF_AB + W_AB
WKV_CHUNK = 64
GDN_CHUNK = 128
HALO = 8
LANES = 128
TILE_BYTES = 1 << 20
VMEM_LIMIT = 56 * 1024 * 1024

ADAM_LR, ADAM_B1, ADAM_B2, ADAM_EPS, ADAM_WD, ADAM_STEP = 0.001, 0.9, 0.999, 1e-08, 0.01, 10

ROW_SHARDED = ('w_out', 'ffn_down')
SMALL = ('norm1_g', 'rwkv_mu', 'rwkv_w0', 'rwkv_a0', 'rwkv_k_k', 'rwkv_k_a', 'rwkv_r_k', 'rwkv_ln_w', 'rwkv_ln_b',
         'gdn_a_log', 'gdn_dt_bias', 'gdn_norm_w', 'norm2_g', 'final_g')
WEIGHTS = ('norm1_g', 'w_in', 'rwkv_mu', 'rwkv_w0', 'rwkv_w2', 'rwkv_a0', 'rwkv_a2', 'rwkv_g2', 'rwkv_k_k', 'rwkv_k_a',
           'rwkv_r_k', 'rwkv_ln_w', 'rwkv_ln_b', 'rwkv_proj', 'gdn_conv_w', 'gdn_a_log', 'gdn_dt_bias', 'gdn_norm_w',
           'gdn_proj', 'w_out', 'norm2_g', 'ffn_up', 'ffn_conv_w', 'ffn_down', 'final_g')


def _params(*sem):
    return pltpu.CompilerParams(dimension_semantics=sem, vmem_limit_bytes=VMEM_LIMIT)


def _tile(n, limit):
    if n <= limit:
        return n
    best = None
    for d in range(128, limit + 1, 128):
        if n % d == 0:
            best = d
    if best is None:
        raise ValueError(f"no tile for {n} under {limit}")
    return best


MM_BLOCK_BYTES = 4 << 20


def _mm(a, b, mode, name, add=None, out_dtype=f32, side=None):
    if mode == 'nn':
        (M, K), N = a.shape, b.shape[1]
    elif mode == 'nt':
        (M, K), N = a.shape, b.shape[0]
    else:
        (K, M), N = a.shape, b.shape[1]
    tm = _tile(M, 1408)
    tk = _tile(K, min(2816, MM_BLOCK_BYTES // (tm * a.dtype.itemsize)))
    tn = _tile(N, max(128, min(MM_BLOCK_BYTES // (tk * b.dtype.itemsize), MM_BLOCK_BYTES // (tm * 4)) // 128 * 128))
    nk = K // tk
    grid = (M // tm, N // tn, nk)
    dn = {'nn': (((1,), (0,)), ((), ())), 'nt': (((1,), (1,)), ((), ())), 'tn': (((0,), (0,)), ((), ()))}[mode]
    n_add = 0 if add is None else 1
    n_side = 0 if side is None else len(side[0])

    def body(a_ref, b_ref, *rest):
        add_ref = rest[0] if add is not None else None
        side_in, rest = rest[n_add:n_add + n_side], rest[n_add + n_side:]
        o_ref, side_out, rest = rest[0], rest[1:1 + n_side], rest[1 + n_side:]
        acc_ref, rest = (rest[0], rest[1:]) if nk > 1 else (None, rest)
        ids = [pl.program_id(d) for d in range(3)]
        if side is not None:
            start, finish = _xy_copies(side_in, side_out, rest, side[1])
            pl.when((ids[0] == 0) & (ids[1] == 0) & (ids[2] == 0))(start)
        acc = lax.dot_general(a_ref[...].astype(bf16), b_ref[...].astype(bf16), dn, preferred_element_type=f32)
        if nk == 1:
            o_ref[...] = (acc + add_ref[...] if add is not None else acc).astype(out_dtype)
        else:
            k = ids[2]

            @pl.when(k == 0)
            def _():
                acc_ref[...] = acc + add_ref[...] if add is not None else acc

            @pl.when(k > 0)
            def _():
                acc_ref[...] += acc

            @pl.when(k == nk - 1)
            def _():
                o_ref[...] = acc_ref[...].astype(out_dtype)
        if side is not None:
            pl.when((ids[0] == grid[0] - 1) & (ids[1] == grid[1] - 1) & (ids[2] == nk - 1))(finish)

    a_spec = (pl.BlockSpec((tk, tm), lambda i, j, k: (k, i)) if mode == 'tn'
              else pl.BlockSpec((tm, tk), lambda i, j, k: (i, k)))
    b_spec = (pl.BlockSpec((tn, tk), lambda i, j, k: (j, k)) if mode == 'nt'
              else pl.BlockSpec((tk, tn), lambda i, j, k: (k, j)))
    o_spec = pl.BlockSpec((tm, tn), lambda i, j, k: (i, j))
    any_spec = pl.BlockSpec(memory_space=pl.ANY)
    side_bufs = [] if side is None else list(side[0])
    ins, specs = [a, b], [a_spec, b_spec]
    if add is not None:
        ins.append(add)
        specs.append(o_spec)
    outs = pl.pallas_call(
        body, grid=grid, in_specs=specs + [any_spec] * n_side, out_specs=[o_spec] + [any_spec] * n_side,
        out_shape=[jax.ShapeDtypeStruct((M, N), out_dtype)] + (_xy_out_shapes(side_bufs, side[1]) if side is not None else []),
        scratch_shapes=([pltpu.VMEM((tm, tn), f32)] if nk > 1 else []) + (_xy_sems(n_side, side[1]) if side is not None else []),
        name=name,
        compiler_params=_params(*(("arbitrary",) * 3 if side is not None else ("parallel", "parallel", "arbitrary"))))(
            *ins, *side_bufs)
    return list(outs) if side is not None else outs[0]


def _shift_down(cur, prev, s):
    if s == 0:
        return cur
    ext = jnp.concatenate([prev, cur], axis=0)
    return pltpu.roll(ext, s, 0)[HALO:]


def _shift_up(cur, nxt, s):
    if s == 0:
        return cur
    ext = jnp.concatenate([cur, nxt], axis=0)
    return pltpu.roll(ext, ext.shape[0] - s, 0)[:cur.shape[0]]


def _conv_apply(cur, prev, w_ref, shifted=None):
    taps = w_ref.shape[0]
    out = None
    for i in range(taps):
        s = taps - 1 - i
        term = (shifted[s] if shifted is not None else _shift_down(cur, prev, s)) * w_ref[pl.ds(i, 1), :]
        out = term if out is None else out + term
    return out


def _row_spec(tm, w):
    return pl.BlockSpec((tm, w), lambda i: (i, 0))


def _prev_spec(tm, w):
    return pl.BlockSpec((HALO, w), lambda i: (jnp.maximum(i * (tm // HALO) - 1, 0), 0))


def _next_spec(tm, w, T):
    return pl.BlockSpec((HALO, w), lambda i: (jnp.minimum((i + 1) * (tm // HALO), T // HALO - 1), 0))


def _full_spec(shape):
    return pl.BlockSpec(shape, lambda i: (0,) * len(shape))


def _pw_fwd(name, fn, rows, consts, out_widths, tm, conv_w=None, out_dtype=f32, strip=None):
    T = rows[0].shape[0]
    nr, nc = len(rows), len(consts)

    def body(*refs):
        i = pl.program_id(0)
        if strip is not None:
            for j in range(out_widths[0] // strip):
                sl = slice(strip * j, strip * (j + 1))
                outs = fn(*[r[:, sl] for r in refs[:nr + nc]])
                for o_ref, o in zip(refs[nr + nc:], outs):
                    o_ref[:, sl] = o.astype(out_dtype)
            return
        vals = [r[...] for r in refs[:nr]]
        p = nr
        if conv_w is not None:
            prev = jnp.where(i > 0, refs[p][...], 0.0)
            vals[0] = _conv_apply(vals[0], prev, refs[p + 1])
            p += 2
        cvals = [r[...] for r in refs[p:p + nc]]
        outs = fn(*vals, *cvals)
        for o_ref, o in zip(refs[p + nc:], outs):
            o_ref[...] = o.astype(out_dtype)

    ins = list(rows)
    specs = [_row_spec(tm, r.shape[1]) for r in rows]
    if conv_w is not None:
        ins += [rows[0], conv_w]
        specs += [_prev_spec(tm, rows[0].shape[1]), _full_spec(conv_w.shape)]
    ins += list(consts)
    specs += [_full_spec(c.shape) for c in consts]
    outs = pl.pallas_call(
        body, grid=(T // tm,), in_specs=specs,
        out_specs=[_row_spec(tm, w) for w in out_widths],
        out_shape=[jax.ShapeDtypeStruct((T, w), out_dtype) for w in out_widths], name=name,
        compiler_params=_params("parallel"))(*ins)
    return outs


def _pw_bwd(name, fn, rows, consts, cots, tm, add_to_first=None, row_dtypes=None, strip=None):
    T = rows[0].shape[0]
    nr, nc = len(rows), len(consts)
    flat_cots = [c for grp in cots for c in grp]
    row_dtypes = row_dtypes or [f32] * nr
    n_extra = 0 if add_to_first is None else 1
    width = rows[0].shape[1]

    def body(*refs):
        i = pl.program_id(0)
        in_refs, cot_refs = refs[:nr + nc], refs[nr + nc:nr + nc + len(flat_cots)]
        extra_ref = refs[nr + nc + len(flat_cots)] if add_to_first is not None else None
        row_out = refs[nr + nc + len(flat_cots) + n_extra:][:nr]
        const_out = refs[nr + nc + len(flat_cots) + n_extra + nr:]

        @pl.when(i == 0)
        def _():
            for q in range(nc):
                const_out[q][...] = jnp.zeros_like(const_out[q])

        def part(sl):
            cot_vals, p = [], 0
            for grp in cots:
                acc = cot_refs[p][:, sl]
                for q in range(1, len(grp)):
                    acc = acc + cot_refs[p + q][:, sl]
                p += len(grp)
                cot_vals.append(acc)
            _, vjp = jax.vjp(fn, *[r[:, sl] for r in in_refs])
            grads = vjp(tuple(cot_vals))
            for q in range(nr):
                g = grads[q]
                if q == 0 and extra_ref is not None:
                    g = g + extra_ref[:, sl]
                row_out[q][:, sl] = g.astype(row_dtypes[q])
            for q in range(nc):
                const_out[q][:, sl] += grads[nr + q]

        if strip is None:
            part(slice(None))
        else:
            for j in range(width // strip):
                part(slice(strip * j, strip * (j + 1)))

    ins = list(rows) + list(consts) + flat_cots
    specs = ([_row_spec(tm, r.shape[1]) for r in rows] + [_full_spec(c.shape) for c in consts]
             + [_row_spec(tm, c.shape[1]) for c in flat_cots])
    if add_to_first is not None:
        ins.append(add_to_first)
        specs.append(_row_spec(tm, add_to_first.shape[1]))
    out_shapes = ([jax.ShapeDtypeStruct(r.shape, d) for r, d in zip(rows, row_dtypes)]
                  + [jax.ShapeDtypeStruct(c.shape, f32) for c in consts])
    out_specs = [_row_spec(tm, r.shape[1]) for r in rows] + [_full_spec(c.shape) for c in consts]
    outs = pl.pallas_call(
        body, grid=(T // tm,), in_specs=specs, out_specs=out_specs, out_shape=out_shapes, name=name,
        compiler_params=_params("arbitrary"))(*ins)
    return list(outs[:nr]), list(outs[nr:])


def _pw_conv_bwd(name, fn, rows, consts, cots, conv_w, tm, row_dtypes=None):
    T, W0 = rows[0].shape
    nr, nc = len(rows), len(consts)
    taps = conv_w.shape[0]
    nblk = T // tm
    flat_cots = [c for grp in cots for c in grp]
    row_dtypes = row_dtypes or [f32] * nr

    def body(*refs):
        i = pl.program_id(0)
        p = 0
        cur = [r[...] for r in refs[p:p + nr]]; p += nr
        nxt = [r[...] for r in refs[p:p + nr]]; p += nr
        prev = jnp.where(i > 0, refs[p][...], 0.0); p += 1
        w_ref = refs[p]; p += 1
        cvals = [r[...] for r in refs[p:p + nc]]; p += nc

        def summed(p0):
            out, q = [], p0
            for grp in cots:
                acc = refs[q][...]
                for t in range(1, len(grp)):
                    acc = acc + refs[q + t][...]
                q += len(grp)
                out.append(acc)
            return out, q

        cot_cur, p = summed(p)
        cot_nxt, p = summed(p)
        row_out, dw_ref, const_out = refs[p:p + nr], refs[p + nr], refs[p + nr + 1:]

        x_cur = cur[0]
        x_down = [_shift_down(x_cur, prev, s_) for s_ in range(taps)]
        _, vjp = jax.vjp(fn, _conv_apply(x_cur, prev, w_ref, x_down), *cur[1:], *cvals)
        grads = vjp(tuple(cot_cur))
        _, vjp_n = jax.vjp(fn, _conv_apply(nxt[0], x_cur[tm - HALO:], w_ref), *nxt[1:], *cvals)
        dc_n = jnp.where(i < nblk - 1, vjp_n(tuple(cot_nxt))[0], 0.0)
        dc = grads[0]

        @pl.when(i == 0)
        def _():
            dw_ref[...] = jnp.zeros_like(dw_ref)
            for q in range(nc):
                const_out[q][...] = jnp.zeros_like(const_out[q])

        dx = None
        for k in range(taps):
            s_ = taps - 1 - k
            term = _shift_up(dc, dc_n, s_) * w_ref[pl.ds(k, 1), :]
            dx = term if dx is None else dx + term
            dw_ref[pl.ds(k, 1), :] += jnp.sum(dc * x_down[s_], axis=0, keepdims=True)
        row_out[0][...] = dx.astype(row_dtypes[0])
        for q in range(1, nr):
            row_out[q][...] = grads[q].astype(row_dtypes[q])
        for q in range(nc):
            const_out[q][...] += grads[nr + q]

    ins = list(rows) + list(rows) + [rows[0], conv_w] + list(consts) + flat_cots + flat_cots
    specs = ([_row_spec(tm, r.shape[1]) for r in rows] + [_next_spec(tm, r.shape[1], T) for r in rows]
             + [_prev_spec(tm, W0), _full_spec(conv_w.shape)] + [_full_spec(c.shape) for c in consts]
             + [_row_spec(tm, c.shape[1]) for c in flat_cots] + [_next_spec(tm, c.shape[1], T) for c in flat_cots])
    out_shapes = ([jax.ShapeDtypeStruct(r.shape, d) for r, d in zip(rows, row_dtypes)]
                  + [jax.ShapeDtypeStruct(conv_w.shape, f32)] + [jax.ShapeDtypeStruct(c.shape, f32) for c in consts])
    out_specs = ([_row_spec(tm, r.shape[1]) for r in rows] + [_full_spec(conv_w.shape)]
                 + [_full_spec(c.shape) for c in consts])
    outs = pl.pallas_call(
        body, grid=(nblk,), in_specs=specs, out_specs=out_specs, out_shape=out_shapes, name=name,
        compiler_params=_params("arbitrary"))(*ins)
    return list(outs[:nr]), outs[nr], list(outs[nr + 1:])


def _sigmoid(x):
    return 0.5 * jnp.tanh(0.5 * x) + 0.5


def _softplus(x):
    return jnp.maximum(x, 0.0) + jnp.log(1.0 + jnp.exp(jnp.minimum(x, -x)))


def _seg_sum_impl(x, seg):
    w = x.shape[-1]
    r = lax.broadcasted_iota(jnp.int32, (w, w), 0) // seg
    c = lax.broadcasted_iota(jnp.int32, (w, w), 1) // seg
    ones = (r == c).astype(bf16)
    hi = x.astype(bf16)
    lo = (x - hi.astype(f32)).astype(bf16)
    return (jnp.dot(hi, ones, preferred_element_type=f32) + jnp.dot(lo, ones, preferred_element_type=f32))


@functools.partial(jax.custom_vjp, nondiff_argnums=(1,))
def _seg_sum(x, seg):
    return _seg_sum_impl(x, seg)


_seg_sum.defvjp(lambda x, seg: (_seg_sum_impl(x, seg), None), lambda seg, _, g: (_seg_sum_impl(g, seg),))


def _rms(x, g):
    return x * lax.rsqrt(jnp.mean(x * x, axis=-1, keepdims=True) + NORM_EPS) * g


def _rms_fn(x, g):
    return (_rms(x, g),)


def _loss_rows(x2, tgt, g):
    e = _rms(x2, g) - tgt
    return 0.5 * jnp.sum(e * e, axis=-1, keepdims=True) * (1.0 / D_MODEL)


def _rwkv_prep_fn(ps, w0, w2p, a0, a2p, g2, k_k, k_a):
    r, k, v = ps[:, 0:512], ps[:, 512:1024], ps[:, 1024:1536]
    wa, gl = ps[:, 1536:1664], ps[:, 1664:1792]
    z = w0 + jnp.dot(jnp.tanh(wa), w2p, precision=HI, preferred_element_type=f32)
    w_log = -_softplus(-z) - 0.5
    lw = -jnp.exp(w_log)
    a = _sigmoid(a0 + jnp.dot(wa, a2p, precision=HI, preferred_element_type=f32))
    g = jnp.dot(_sigmoid(gl), g2, precision=HI, preferred_element_type=f32)
    kx = k * k_k
    kk = kx * lax.rsqrt(_seg_sum(kx * kx, RWKV_HD) + L2_EPS)
    k2 = k * (1.0 + (a - 1.0) * k_a)
    return r, lw, k2, v, -kk, kk * a, g


def _rwkv_post_fn(y, r, k2, v, g, ln_w, ln_b, rk):
    mean = _seg_sum(y, RWKV_HD) * (1.0 / RWKV_HD)
    yc = y - mean
    var = _seg_sum(yc * yc, RWKV_HD) * (1.0 / RWKV_HD)
    yn = yc * lax.rsqrt(var + GN_EPS) * ln_w + ln_b
    bonus = _seg_sum(r * k2 * rk, RWKV_HD) * v
    return ((yn + bonus) * g,)


def _gdn_prep_fn(c, ab, al_p, dt_p):
    s = c * _sigmoid(c)
    q, k, v = s[:, 0:512], s[:, 512:1024], s[:, 1024:1536]
    q = q * lax.rsqrt(_seg_sum(q * q, GDN_HD) + L2_EPS) * (GDN_HD ** -0.5)
    k = k * lax.rsqrt(_seg_sum(k * k, GDN_HD) + L2_EPS)
    lane = lax.broadcasted_iota(jnp.int32, ab.shape, 1)
    gpart = -jnp.exp(al_p) * _softplus(ab + dt_p)
    gbeta = jnp.where(lane < GDN_HEADS, gpart, jnp.where(lane < 2 * GDN_HEADS, _sigmoid(ab), 0.0))
    return q, k, v, gbeta


def _gdn_post_fn(o, z, nw):
    ms = _seg_sum(o * o, GDN_HD) * (1.0 / GDN_HD)
    return (o * lax.rsqrt(ms + NORM_EPS) * nw * (z * _sigmoid(z)),)


def _mix_fn(ga, gb, ya, yb):
    return (_sigmoid(ga) * ya + _sigmoid(gb) * yb,)


STRIP = 128


def _ffn_strip_fn(cg, cu):
    return cg * _sigmoid(cg) * cu


def _strip_conv(ref, prev_ref, w_ref, sl, first, taps):
    cur = ref[:, sl]
    prev = jnp.where(first, 0.0, prev_ref[:, sl])
    down = [_shift_down(cur, prev, s) for s in range(taps)]
    conv = None
    for k in range(taps):
        term = down[taps - 1 - k] * w_ref[pl.ds(k, 1), sl]
        conv = term if conv is None else conv + term
    return cur, down, conv


def _ffn_act_fwd(h, w, tm):
    T, W2 = h.shape
    H = W2 // 2
    taps = w.shape[0]

    def body(h_ref, hp_ref, w_ref, o_ref):
        first = pl.program_id(0) == 0
        for j in range(H // STRIP):
            gs, us = slice(STRIP * j, STRIP * (j + 1)), slice(H + STRIP * j, H + STRIP * (j + 1))
            cg = _strip_conv(h_ref, hp_ref, w_ref, gs, first, taps)[2]
            cu = _strip_conv(h_ref, hp_ref, w_ref, us, first, taps)[2]
            o_ref[:, gs] = _ffn_strip_fn(cg, cu).astype(o_ref.dtype)

    return pl.pallas_call(
        body, grid=(T // tm,), in_specs=[_row_spec(tm, W2), _prev_spec(tm, W2), _full_spec(w.shape)],
        out_specs=_row_spec(tm, H), out_shape=jax.ShapeDtypeStruct((T, H), bf16), name="ffn_act",
        compiler_params=_params("parallel"))(h, h, w)


def _ffn_act_bwd(h, dact, w, tm):
    T, W2 = h.shape
    H = W2 // 2
    taps = w.shape[0]
    nblk = T // tm

    def body(h_ref, hp_ref, hn_ref, d_ref, dn_ref, w_ref, dh_ref, dw_ref):
        i = pl.program_id(0)
        first, last = i == 0, i == nblk - 1

        @pl.when(first)
        def _():
            dw_ref[...] = jnp.zeros_like(dw_ref)

        for j in range(H // STRIP):
            gs, us = slice(STRIP * j, STRIP * (j + 1)), slice(H + STRIP * j, H + STRIP * (j + 1))
            parts = {}
            for name, sl in (('g', gs), ('u', us)):
                cur, down, conv = _strip_conv(h_ref, hp_ref, w_ref, sl, first, taps)
                nxt = hn_ref[:, sl]
                conv_n = None
                for k in range(taps):
                    term = _shift_down(nxt, cur[tm - HALO:], taps - 1 - k) * w_ref[pl.ds(k, 1), sl]
                    conv_n = term if conv_n is None else conv_n + term
                parts[name] = (down, conv, conv_n)
            _, vjp = jax.vjp(_ffn_strip_fn, parts['g'][1], parts['u'][1])
            dcs = vjp(d_ref[:, gs])
            _, vjp_n = jax.vjp(_ffn_strip_fn, parts['g'][2], parts['u'][2])
            dcs_n = vjp_n(jnp.where(last, 0.0, dn_ref[:, gs]))
            for (name, sl), dc, dc_n in zip((('g', gs), ('u', us)), dcs, dcs_n):
                down = parts[name][0]
                dx = None
                for k in range(taps):
                    s_ = taps - 1 - k
                    term = _shift_up(dc, dc_n, s_) * w_ref[pl.ds(k, 1), sl]
                    dx = term if dx is None else dx + term
                    dw_ref[pl.ds(k, 1), sl] += jnp.sum(dc * down[s_], axis=0, keepdims=True)
                dh_ref[:, sl] = dx.astype(dh_ref.dtype)

    return pl.pallas_call(
        body, grid=(nblk,),
        in_specs=[_row_spec(tm, W2), _prev_spec(tm, W2), _next_spec(tm, W2, T), _row_spec(tm, H), _next_spec(tm, H, T),
                  _full_spec(w.shape)],
        out_specs=[_row_spec(tm, W2), _full_spec(w.shape)],
        out_shape=[jax.ShapeDtypeStruct((T, W2), bf16), jax.ShapeDtypeStruct(w.shape, f32)], name="ffn_act_bwd",
        compiler_params=_params("arbitrary"))(h, h, h, dact, dact, w)


N_POS = 4


def _xy_out_shapes(bufs, scatter):
    return [jax.ShapeDtypeStruct((N_POS,) + tuple(b.shape[1:] if scatter else b.shape), b.dtype) for b in bufs]


def _xy_sems(n, scatter):
    sems = [pltpu.SemaphoreType.DMA((3 * n,)), pltpu.SemaphoreType.DMA((3 * n,)), pltpu.SemaphoreType.DMA((n,))]
    return sems if scatter else sems + [pltpu.SemaphoreType.DMA((3 * n,)), pltpu.SemaphoreType.DMA((3 * n,))]


def _xy_copies(in_refs, out_refs, sems, scatter):
    n = len(in_refs)
    send_sems, recv_sems, local_sems = sems[:3]

    def place():
        x, y, c = lax.axis_index("x"), lax.axis_index("y"), lax.axis_index("c")
        return x, y, c, 2 * x + y, [(1 - x, y), (x, 1 - y), (1 - x, 1 - y)]

    def half(ref, a, which):
        rows = in_refs[a].shape[0] // 2
        return ref.at[pl.ds(pl.multiple_of(which * rows, HALO), rows)]

    def ici(a, k, src, dst, peer, c):
        return pltpu.make_async_remote_copy(
            src_ref=src, dst_ref=dst, send_sem=send_sems.at[3 * a + k], recv_sem=recv_sems.at[3 * a + k],
            device_id=(peer[0], peer[1], c), device_id_type=pl.DeviceIdType.MESH)

    def outgoing():
        x, y, c, me, peers = place()
        own = [pltpu.make_async_copy(in_refs[a].at[me] if scatter else in_refs[a], out_refs[a].at[me], local_sems.at[a])
               for a in range(n)]
        if scatter:
            sends = [ici(a, k, in_refs[a].at[2 * p[0] + p[1]], out_refs[a].at[me], p, c)
                     for a in range(n) for k, p in enumerate(peers)]
        else:
            sends = [ici(a, k, half(in_refs[a], a, c), half(out_refs[a].at[me], a, c), p, c)
                     for a in range(n) for k, p in enumerate(peers)]
        return own, sends

    def arrivals():
        x, y, c, me, peers = place()
        if scatter:
            return [ici(a, k, in_refs[a].at[me], out_refs[a].at[2 * p[0] + p[1]], p, c)
                    for a in range(n) for k, p in enumerate(peers)]
        return [ici(a, k, half(in_refs[a], a, c), half(out_refs[a].at[2 * p[0] + p[1]], a, c), p, c)
                for a in range(n) for k, p in enumerate(peers)]

    def to_sibling(mine):
        x, y, c, me, peers = place()
        which = c if mine else 1 - c
        return [pltpu.make_async_remote_copy(
            src_ref=half(out_refs[a].at[2 * p[0] + p[1]], a, which), dst_ref=half(out_refs[a].at[2 * p[0] + p[1]], a, which),
            send_sem=sems[3].at[3 * a + k], recv_sem=sems[4].at[3 * a + k],
            device_id=(x, y, 1 - c), device_id_type=pl.DeviceIdType.MESH) for a in range(n) for k, p in enumerate(peers)]

    def start():
        own, sends = outgoing()
        for cp in own + sends:
            cp.start()

    def finish():
        if scatter:
            for cp in arrivals():
                cp.wait_recv()
        else:
            passed = to_sibling(True)
            for cp, fwd in zip(arrivals(), passed):
                cp.wait_recv()
                fwd.start()
            for cp in to_sibling(False):
                cp.wait_recv()
            for fwd in passed:
                fwd.wait_send()
        own, sends = outgoing()
        for cp in sends:
            cp.wait_send()
        for cp in own:
            cp.wait()

    return start, finish


_NN, _NT, _TN = 'hcs,hsd->hcd', 'hcd,hsd->hcs', 'hcd,hce->hde'


def _lo(spec, a, b):
    return jnp.einsum(spec, a.astype(bf16), b.astype(bf16), preferred_element_type=f32)


@jax.custom_vjp
def _bmm(a, b):
    return _lo(_NN, a, b)


_bmm.defvjp(lambda a, b: (_lo(_NN, a, b), (a, b)), lambda ab, g: (_lo(_NT, g, ab[1]), _lo(_TN, ab[0], g)))


@jax.custom_vjp
def _bmm_nt(a, b):
    return _lo(_NT, a, b)


_bmm_nt.defvjp(lambda a, b: (_lo(_NT, a, b), (a, b)), lambda ab, g: (_lo(_NN, g, ab[1]), _lo(_TN, g, ab[0])))


@jax.custom_vjp
def _bmm_tn(a, b):
    return _lo(_TN, a, b)


_bmm_tn.defvjp(lambda a, b: (_lo(_TN, a, b), (a, b)), lambda ab, g: (_lo(_NT, ab[1], g), _lo(_NN, ab[0], g)))


def _masks(H, C):
    row = lax.broadcasted_iota(jnp.int32, (H, C, C), 1)
    col = lax.broadcasted_iota(jnp.int32, (H, C, C), 2)
    return row, col


def _tri_inv_impl(L):
    H, C, _ = L.shape
    row, col = _masks(H, C)
    eye = (row == col).astype(f32)
    base = 16
    same = (row // base) == (col // base)
    Ld = jnp.where(same, L, 0.0)
    X = -Ld
    inv = eye + X
    for _ in range(3):
        X = _bmm(X, X)
        inv = _bmm(inv, eye + X)
    if C == base:
        return inv
    N = _bmm(inv, L - Ld)
    out = eye - N
    levels = C // base
    P = N
    span = 2
    while span < levels:
        P = _bmm(P, P)
        out = _bmm(out, eye + P)
        span *= 2
    return _bmm(out, inv)


@jax.custom_vjp
def _tri_inv(L):
    return _tri_inv_impl(L)


def _tri_inv_fwd(L):
    T = _tri_inv_impl(L)
    return T, T


def _tri_inv_bwd(T, dT):
    return (-_bmm_nt(_bmm_tn(T, dT), T),)


_tri_inv.defvjp(_tri_inv_fwd, _tri_inv_bwd)


def _cumsum_impl(x, reverse):
    C = x.shape[1]
    row = lax.broadcasted_iota(jnp.int32, x.shape, 1)
    s = 1
    while s < C:
        if reverse:
            x = x + jnp.where(row < C - s, pltpu.roll(x, C - s, 1), 0.0)
        else:
            x = x + jnp.where(row >= s, pltpu.roll(x, s, 1), 0.0)
        s *= 2
    return x


@jax.custom_vjp
def _cumsum(x):
    return _cumsum_impl(x, False)


_cumsum.defvjp(lambda x: (_cumsum_impl(x, False), None), lambda _, g: (_cumsum_impl(g, True),))


def _wkv_chunk(Z, r, lw, k, v, a, b):
    H, C, D = r.shape
    row, col = _masks(H, C)
    incl, strict = row >= col, row > col
    cw = _cumsum(lw)
    cwp = cw - lw
    cwl = jnp.sum(lw, axis=1, keepdims=True)
    en = jnp.exp(-cw)
    at, rt, bt, kt = a * jnp.exp(cwp), r * jnp.exp(cw), b * en, k * en
    Lab = jnp.where(strict, _bmm_nt(at, bt), 0.0)
    Lak = jnp.where(strict, _bmm_nt(at, kt), 0.0)
    Tm = _tri_inv(-Lab)
    U = _bmm(Tm, _bmm(at, Z) + _bmm(Lak, v))
    Rb = jnp.where(incl, _bmm_nt(rt, bt), 0.0)
    Rk = jnp.where(incl, _bmm_nt(rt, kt), 0.0)
    y = _bmm(rt, Z) + _bmm(Rb, U) + _bmm(Rk, v)
    ed = jnp.exp(cwl - cw)
    zdec = jnp.swapaxes(jnp.broadcast_to(jnp.exp(cwl), (H, Z.shape[2], D)), 1, 2)
    Z1 = Z * zdec + _bmm_tn(b * ed, U) + _bmm_tn(k * ed, v)
    return y, Z1


def _gdn_chunk(S, q, k, v, g, beta):
    H, C, D = q.shape
    row, col = _masks(H, C)
    incl, strict = row >= col, row > col
    gc = _cumsum(g)
    diff = gc - jnp.swapaxes(gc, 1, 2)
    decay = jnp.where(incl, jnp.exp(jnp.where(incl, diff, 0.0)), 0.0)
    gl = jnp.sum(g, axis=1, keepdims=True)
    kb, vb = k * beta, v * beta
    L = jnp.where(strict, _bmm_nt(kb, k) * decay, 0.0)
    Tm = _tri_inv(L)
    egc = jnp.exp(gc)
    u = _bmm(Tm, vb)
    wk = _bmm(Tm, kb * egc)
    attn = jnp.where(incl, _bmm_nt(q, k) * decay, 0.0)
    v_new = u - _bmm(wk, S)
    o = _bmm(q * egc, S) + _bmm(attn, v_new)
    S1 = S * jnp.exp(gl) + _bmm_tn(k * jnp.exp(gl - gc), v_new)
    return o, S1


def _wkv_block(Z, r, lw, k, v, a, b):
    lane = lax.broadcasted_iota(jnp.int32, (r.shape[0], 128), 1)
    low = lane < RWKV_HD

    def heads(t):
        out = []
        for p in range(RWKV_HEADS // 2):
            pair = t[:, 128 * p:128 * (p + 1)]
            out += [jnp.where(low, pair, 0.0), jnp.where(low, 0.0, pair)]
        return jnp.concatenate([t[None] for t in out], axis=0)

    y, Z1 = _wkv_chunk(Z, *[heads(t) for t in (r, lw, k, v, a, b)])
    return jnp.concatenate([y[2 * p] + y[2 * p + 1] for p in range(RWKV_HEADS // 2)], axis=1), Z1


def _gdn_block(S, q, k, v, gbeta):
    heads = lambda t: jnp.concatenate([t[None, :, GDN_HD * h:GDN_HD * (h + 1)] for h in range(GDN_HEADS)], axis=0)
    src = lax.broadcasted_iota(jnp.int32, (W_AB, 2 * GDN_W), 0)
    dst = lax.broadcasted_iota(jnp.int32, (W_AB, 2 * GDN_W), 1) // GDN_HD
    spread = jnp.dot(gbeta, (src == dst).astype(f32), precision=HI, preferred_element_type=f32)
    o, S1 = _gdn_chunk(S, heads(q), heads(k), heads(v), heads(spread[:, :GDN_W]), heads(spread[:, GDN_W:]))
    return jnp.concatenate([o[h] for h in range(GDN_HEADS)], axis=1), S1


def _scan_fwd(name, block_fn, ins, C, H, dh, w_out, side=None):
    T = ins[0].shape[0]
    n_in = len(ins)
    nblk = T // C
    n_side = 0 if side is None else len(side[0])

    def body(*refs):
        in_refs, refs = refs[:n_in], refs[n_in:]
        side_in, refs = refs[:n_side], refs[n_side:]
        y_ref, zs_ref, refs = refs[0], refs[1], refs[2:]
        side_out, refs = refs[:n_side], refs[n_side:]
        z_scr = refs[0]
        if side is not None:
            start, finish = _xy_copies(side_in, side_out, refs[1:], side[1])
            pl.when(pl.program_id(0) == 0)(start)

        @pl.when(pl.program_id(0) == 0)
        def _():
            z_scr[...] = jnp.zeros_like(z_scr)

        Z = z_scr[...]
        zs_ref[0] = Z
        y, Z1 = block_fn(Z, *[r[...] for r in in_refs])
        y_ref[...] = y
        z_scr[...] = Z1
        if side is not None:
            pl.when(pl.program_id(0) == nblk - 1)(finish)

    side_bufs = [] if side is None else list(side[0])
    any_spec = pl.BlockSpec(memory_space=pl.ANY)
    return pl.pallas_call(
        body, grid=(nblk,),
        in_specs=[pl.BlockSpec((C, a.shape[1]), lambda i: (i, 0)) for a in ins] + [any_spec] * n_side,
        out_specs=[pl.BlockSpec((C, w_out), lambda i: (i, 0)), pl.BlockSpec((1, H, dh, dh), lambda i: (i, 0, 0, 0))]
        + [any_spec] * n_side,
        out_shape=[jax.ShapeDtypeStruct((T, w_out), f32), jax.ShapeDtypeStruct((T // C, H, dh, dh), f32)]
        + (_xy_out_shapes(side_bufs, side[1]) if side is not None else []),
        scratch_shapes=[pltpu.VMEM((H, dh, dh), f32)] + (_xy_sems(n_side, side[1]) if side is not None else []), name=name,
        compiler_params=_params("arbitrary"))(*ins, *side_bufs)


def _scan_bwd(name, block_fn, ins, dy, zs, C, side=None):
    T = ins[0].shape[0]
    _, H, dh, _ = zs.shape
    n_in = len(ins)
    nblk = T // C
    n_side = 0 if side is None else len(side[0])

    def body(*refs):
        in_refs, dy_ref, zs_ref, refs = refs[:n_in], refs[n_in], refs[n_in + 1], refs[n_in + 2:]
        side_in, refs = refs[:n_side], refs[n_side:]
        out_refs, refs = refs[:n_in], refs[n_in:]
        side_out, refs = refs[:n_side], refs[n_side:]
        dz_scr = refs[0]
        if side is not None:
            start, finish = _xy_copies(side_in, side_out, refs[1:], side[1])
            pl.when(pl.program_id(0) == 0)(start)

        @pl.when(pl.program_id(0) == 0)
        def _():
            dz_scr[...] = jnp.zeros_like(dz_scr)

        _, vjp = jax.vjp(block_fn, zs_ref[0], *[r[...] for r in in_refs])
        grads = vjp((dy_ref[...], dz_scr[...]))
        dz_scr[...] = grads[0]
        for o_ref, gval in zip(out_refs, grads[1:]):
            o_ref[...] = gval
        if side is not None:
            pl.when(pl.program_id(0) == nblk - 1)(finish)

    side_bufs = [] if side is None else list(side[0])
    any_spec = pl.BlockSpec(memory_space=pl.ANY)
    rev = lambda i: (nblk - 1 - i, 0)
    return pl.pallas_call(
        body, grid=(nblk,),
        in_specs=[pl.BlockSpec((C, a.shape[1]), rev) for a in ins]
        + [pl.BlockSpec((C, dy.shape[1]), rev), pl.BlockSpec((1, H, dh, dh), lambda i: (nblk - 1 - i, 0, 0, 0))]
        + [any_spec] * n_side,
        out_specs=[pl.BlockSpec((C, a.shape[1]), rev) for a in ins] + [any_spec] * n_side,
        out_shape=[jax.ShapeDtypeStruct(a.shape, f32) for a in ins]
        + (_xy_out_shapes(side_bufs, side[1]) if side is not None else []),
        scratch_shapes=[pltpu.VMEM((H, dh, dh), f32)] + (_xy_sems(n_side, side[1]) if side is not None else []), name=name,
        compiler_params=_params("arbitrary"))(*ins, dy, zs, *side_bufs)


def _loss_call(x2, tgt, g, tm):
    T, W = x2.shape

    def body(x_ref, t_ref, g_ref, dx_ref, dg_ref, l_ref):
        i = pl.program_id(0)
        tv = t_ref[...]
        l, vjp = jax.vjp(lambda xv, gv: _loss_rows(xv, tv, gv), x_ref[...], g_ref[...])
        dx, dg = vjp(jnp.ones_like(l))
        dx_ref[...] = dx
        tot = jnp.zeros((1, 128), f32) + jnp.sum(l)

        @pl.when(i == 0)
        def _():
            dg_ref[...] = dg
            l_ref[...] = tot

        @pl.when(i > 0)
        def _():
            dg_ref[...] += dg
            l_ref[...] += tot

    return pl.pallas_call(
        body, grid=(T // tm,),
        in_specs=[_row_spec(tm, W), _row_spec(tm, W), _full_spec(g.shape)],
        out_specs=[_row_spec(tm, W), _full_spec(g.shape), _full_spec((1, 128))],
        out_shape=[jax.ShapeDtypeStruct((T, W), f32), jax.ShapeDtypeStruct(g.shape, f32),
                   jax.ShapeDtypeStruct((1, 128), f32)], name="loss_head",
        compiler_params=_params("arbitrary"))(x2, tgt, g)


def _local_step(x, tgt, W, late=None):
    row = lambda a: a.reshape(1, -1)
    wp = W['w_in_pad']
    w_rwkv, w_qkv, w_z = wp[:, :OFF_QKV], wp[:, OFF_QKV:OFF_Z], wp[:, OFF_Z:OFF_GATES]
    w_gates, w_ab = wp[:, OFF_GATES:OFF_AB], wp[:, OFF_AB:]
    mu = row(W['rwkv_mu'])
    mixw = jnp.concatenate([mu, 1.0 - mu], axis=0)
    zpad = jnp.zeros((64, RWKV_W), f32)
    w2p = jnp.concatenate([W['rwkv_w2'], zpad], axis=0)
    a2p = jnp.concatenate([zpad, W['rwkv_a2']], axis=0)
    rw_consts = [row(W['rwkv_w0']), w2p, row(W['rwkv_a0']), a2p, W['rwkv_g2'], row(W['rwkv_k_k']), row(W['rwkv_k_a'])]
    post_consts = [row(W['rwkv_ln_w']), row(W['rwkv_ln_b']), row(W['rwkv_r_k'])]
    pad4 = lambda a: jnp.pad(row(a), ((0, 0), (0, W_AB - GDN_HEADS)))
    gd_consts = [pad4(W['gdn_a_log']), pad4(W['gdn_dt_bias'])]
    nw_t = jnp.tile(row(W['gdn_norm_w']), (1, GDN_HEADS))
    g1, g2n, gf = row(W['norm1_g']), row(W['norm2_g']), row(W['final_g'])

    (u,) = _pw_fwd("norm1", _rms_fn, [x], [g1], [D_MODEL], 256, out_dtype=bf16)
    p_rwkv = _mm(u, w_rwkv, 'nn', "in_rwkv")
    qkv_raw = _mm(u, w_qkv, 'nn', "in_qkv")
    z = _mm(u, w_z, 'nn', "in_z")
    gates = _mm(u, w_gates, 'nn', "in_gates")
    ab = _mm(u, w_ab, 'nn', "in_ab")

    r, lw, k2, v, a_, b_, g = _pw_fwd("rwkv_prep", _rwkv_prep_fn, [p_rwkv], rw_consts, [RWKV_W] * 7, 256, conv_w=mixw)
    wkv_in = [r, lw, k2, v, a_, b_]
    y, zs_wkv, *gathered = _scan_fwd("wkv_fwd", _wkv_block, wkv_in, WKV_CHUNK, RWKV_HEADS, 2 * RWKV_HD, RWKV_W,
                                     side=None if late is None else (late['shards'], False))
    if late is not None:
        W = dict(W, **late['assemble'](gathered))
    (ya_in,) = _pw_fwd("rwkv_post", _rwkv_post_fn, [y, r, k2, v, g], post_consts, [RWKV_W], 256, out_dtype=bf16, strip=128)
    ya = _mm(ya_in, W['rwkv_proj'], 'nn', "rwkv_proj")

    gq, gk, gv, gbeta = _pw_fwd("gdn_prep", _gdn_prep_fn, [qkv_raw, ab], gd_consts, [GDN_W] * 3 + [W_AB], 256,
                                conv_w=W['gdn_conv_w'])
    gdn_in = [gq, gk, gv, gbeta]
    o, zs_gdn = _scan_fwd("gdn_fwd", _gdn_block, gdn_in, GDN_CHUNK, GDN_HEADS, GDN_HD, GDN_W)
    (yb_in,) = _pw_fwd("gdn_post", _gdn_post_fn, [o, z], [nw_t], [GDN_W], 256, out_dtype=bf16, strip=128)
    yb = _mm(yb_in, W['gdn_proj'], 'nn', "gdn_proj")

    ga, gb = gates[:, :D_MODEL], gates[:, D_MODEL:]
    (mixed,) = _pw_fwd("mix", _mix_fn, [ga, gb, ya, yb], [], [D_MODEL], 256, out_dtype=bf16, strip=256)
    x1 = _mm(mixed, W['w_out'], 'nn', "w_out", add=x)
    (u2,) = _pw_fwd("norm2", _rms_fn, [x1], [g2n], [D_MODEL], 256, out_dtype=bf16)
    h = _mm(u2, W['ffn_up'], 'nn', "ffn_up")
    act = _ffn_act_fwd(h, W['ffn_conv_w'], 256)
    x2 = _mm(act, W['ffn_down'], 'nn', "ffn_down", add=x1)

    G = {}
    dx2, dgf, loss = _loss_call(x2, tgt, gf, 256)
    G['final_g'] = dgf
    dact = _mm(dx2, W['ffn_down'], 'nt', "d_act")
    G['ffn_down'] = _mm(act, dx2, 'tn', "g_ffn_down", out_dtype=bf16)
    dh, G['ffn_conv_w'] = _ffn_act_bwd(h, dact, W['ffn_conv_w'], 128)
    du2 = _mm(dh, W['ffn_up'], 'nt', "d_u2")
    G['ffn_up'] = _mm(u2, dh, 'tn', "g_ffn_up", out_dtype=bf16)
    (dx1,), (G['norm2_g'],) = _pw_bwd("norm2_bwd", _rms_fn, [x1], [g2n], [(du2,)], 256, add_to_first=dx2)
    dmixed = _mm(dx1, W['w_out'], 'nt', "d_mixed")
    G['w_out'] = _mm(mixed, dx1, 'tn', "g_w_out", out_dtype=bf16)
    (dga, dgb, dya, dyb), _ = _pw_bwd("mix_bwd", _mix_fn, [ga, gb, ya, yb], [], [(dmixed,)], 256, row_dtypes=[bf16] * 4,
                                      strip=256)
    dya_in = _mm(dya, W['rwkv_proj'], 'nt', "d_ya_in")
    G['rwkv_proj'] = _mm(ya_in, dya, 'tn', "g_rwkv_proj", out_dtype=bf16)
    dyb_in = _mm(dyb, W['gdn_proj'], 'nt', "d_yb_in")
    G['gdn_proj'] = _mm(yb_in, dyb, 'tn', "g_gdn_proj", out_dtype=bf16)

    (do, dz), (dnw_t,) = _pw_bwd("gdn_post_bwd", _gdn_post_fn, [o, z], [nw_t], [(dyb_in,)], 256, row_dtypes=[f32, bf16],
                                 strip=128)
    G['gdn_norm_w'] = dnw_t.reshape(GDN_HEADS, GDN_HD).sum(axis=0)
    dgq, dgk, dgv, dgbeta = _scan_bwd("gdn_bwd", _gdn_block, gdn_in, do, zs_gdn, GDN_CHUNK)
    (dqkv_raw, dab), G['gdn_conv_w'], (dal_p, ddt_p) = _pw_conv_bwd(
        "gdn_prep_bwd", _gdn_prep_fn, [qkv_raw, ab], gd_consts, [(dgq,), (dgk,), (dgv,), (dgbeta,)], W['gdn_conv_w'], 256,
        row_dtypes=[bf16, bf16])
    G['gdn_a_log'], G['gdn_dt_bias'] = dal_p[0, :GDN_HEADS], ddt_p[0, :GDN_HEADS]

    (dy, dr1, dk21, dv1, dg_), (G['rwkv_ln_w'], G['rwkv_ln_b'], G['rwkv_r_k']) = _pw_bwd(
        "rwkv_post_bwd", _rwkv_post_fn, [y, r, k2, v, g], post_consts, [(dya_in,)], 256, strip=128)
    dr2, dlw, dk22, dv2, da_, db_, *G['_arrived'] = _scan_bwd(
        "wkv_bwd", _wkv_block, wkv_in, dy, zs_wkv, WKV_CHUNK, side=None if late is None else (late['slabs'](G), True))
    (dp_rwkv,), dmixw, rw_grads = _pw_conv_bwd(
        "rwkv_prep_bwd", _rwkv_prep_fn, [p_rwkv], rw_consts,
        [(dr1, dr2), (dlw,), (dk21, dk22), (dv1, dv2), (da_,), (db_,), (dg_,)], mixw, 256, row_dtypes=[bf16])
    G['rwkv_w0'], dw2p, G['rwkv_a0'], da2p, G['rwkv_g2'], G['rwkv_k_k'], G['rwkv_k_a'] = rw_grads
    G['rwkv_w2'], G['rwkv_a2'] = dw2p[:64], da2p[64:]
    G['rwkv_mu'] = dmixw[0] - dmixw[1]

    dp = jnp.concatenate([dp_rwkv, dqkv_raw, dz, dga, dgb, dab], axis=1)
    G['w_in_pad'] = _mm(u, dp, 'tn', "g_w_in", out_dtype=bf16)
    if late is None:
        du = _mm(dp, wp, 'nt', "d_u")
    else:
        du, *G['_arrived_w_in'] = _mm(dp, wp, 'nt', "d_u", side=(late['w_in_slabs'](G), True))
    (dx,), (G['norm1_g'],) = _pw_bwd("norm1_bwd", _rms_fn, [x], [g1], [(du,)], 256, add_to_first=dx1)
    return loss, dx, G


def _pad_w_in(w):
    return jnp.concatenate([w[:, :OFF_GATES], w[:, OFF_GATES + 8:], w[:, OFF_GATES:OFF_GATES + 8],
                            jnp.zeros((w.shape[0], W_AB - 8), w.dtype)], axis=1)


def _unpad_w_in(wp):
    return jnp.concatenate([wp[:, :OFF_GATES], wp[:, OFF_AB:OFF_AB + 8], wp[:, OFF_GATES:OFF_AB]], axis=1)


BIG = ('w_in', 'rwkv_proj', 'gdn_proj', 'w_out', 'ffn_up', 'ffn_down')
SMALL_SHARDED = ('rwkv_w2', 'rwkv_a2', 'rwkv_g2', 'gdn_conv_w', 'ffn_conv_w')


def _rows128(shape):
    n = 1
    for d in shape:
        n *= d
    return -(-n // LANES)


def _pack128(arrays):
    parts = []
    for a in arrays:
        flat = a.reshape(-1)
        rows = _rows128(a.shape)
        parts.append(jnp.pad(flat, (0, rows * LANES - flat.shape[0])).reshape(rows, LANES))
    buf = jnp.concatenate(parts, axis=0)
    return jnp.pad(buf, ((0, -buf.shape[0] % HALO), (0, 0)))


def _unpack128(buf, shapes):
    out, off = [], 0
    for s in shapes:
        rows, n = _rows128(s), 1
        for d in s:
            n *= d
        out.append(buf[off:off + rows].reshape(-1)[:n].reshape(s))
        off += rows
    return out


def _row_tile(r, c):
    best = None
    for d in range(HALO, r + 1, HALO):
        if r % d == 0 and d * c * 4 <= TILE_BYTES:
            best = d
    return best if best is not None else r


def _xy_exchange(name, bufs, scatter):
    n = len(bufs)

    def body(*refs):
        start, finish = _xy_copies(refs[:n], refs[n:2 * n], refs[2 * n:], scatter)
        start()
        finish()

    return pl.pallas_call(
        body, in_specs=[pl.BlockSpec(memory_space=pl.ANY)] * n, out_specs=[pl.BlockSpec(memory_space=pl.ANY)] * n,
        out_shape=_xy_out_shapes(bufs, scatter), scratch_shapes=_xy_sems(n, scatter), name=name)(*bufs)


def _sibling_exchange(name, bufs):
    n = len(bufs)

    def body(*refs):
        in_refs, out_refs, send_sems, recv_sems = refs[:n], refs[n:2 * n], refs[2 * n], refs[2 * n + 1]
        x, y, c = lax.axis_index("x"), lax.axis_index("y"), lax.axis_index("c")
        copies = [pltpu.make_async_remote_copy(
            src_ref=in_refs[a], dst_ref=out_refs[a], send_sem=send_sems.at[a], recv_sem=recv_sems.at[a],
            device_id=(x, y, 1 - c), device_id_type=pl.DeviceIdType.MESH) for a in range(n)]
        for cp in copies:
            cp.start()
        for cp in copies:
            cp.wait()

    return pl.pallas_call(
        body, in_specs=[pl.BlockSpec(memory_space=pl.ANY)] * n, out_specs=[pl.BlockSpec(memory_space=pl.ANY)] * n,
        out_shape=[jax.ShapeDtypeStruct(b.shape, b.dtype) for b in bufs],
        scratch_shapes=[pltpu.SemaphoreType.DMA((n,)), pltpu.SemaphoreType.DMA((n,))], name=name)(*bufs)


def _sum_slots(name, buf):
    _, R, L = buf.shape
    tr = _row_tile(R, L)

    def body(b_ref, o_ref):
        part = lambda s: b_ref[s].astype(f32)
        o_ref[...] = ((part(0) + part(1)) + part(2)) + part(3)

    return pl.pallas_call(
        body, grid=(R // tr,),
        in_specs=[pl.BlockSpec((N_POS, tr, L), lambda i: (0, i, 0))],
        out_specs=pl.BlockSpec((tr, L), lambda i: (i, 0)),
        out_shape=jax.ShapeDtypeStruct((R, L), f32), name=name,
        compiler_params=_params("parallel"))(buf)


def _adamw(name, w, ga, gb, m, v):
    R, L = w.shape
    tr = _row_tile(R, L)
    c1 = 1.0 / (1.0 - ADAM_B1 ** ADAM_STEP)
    c2 = 1.0 / (1.0 - ADAM_B2 ** ADAM_STEP)

    def body(w_ref, ga_ref, gb_ref, m_ref, v_ref, g_out, d_out, m_out, v_out):
        g = ga_ref[...] + gb_ref[...]
        m_new = ADAM_B1 * m_ref[...] + (1.0 - ADAM_B1) * g
        v_new = ADAM_B2 * v_ref[...] + (1.0 - ADAM_B2) * (g * g)
        g_out[...] = g
        m_out[...] = m_new
        v_out[...] = v_new
        d_out[...] = -ADAM_LR * ((m_new * c1) / (jnp.sqrt(v_new * c2) + ADAM_EPS) + ADAM_WD * w_ref[...])

    spec = pl.BlockSpec((tr, L), lambda i: (i, 0))
    return pl.pallas_call(
        body, grid=(R // tr,), in_specs=[spec] * 5, out_specs=[spec] * 4,
        out_shape=[jax.ShapeDtypeStruct((R, L), f32)] * 4, name=name,
        compiler_params=_params("parallel"))(w, ga, gb, m, v)


def _step(x, loss_target, P, M, V):
    shapes = {n: tuple(P[n].shape) for n in WEIGHTS}
    sh_shapes = [shapes[n] for n in SMALL_SHARDED]
    packed = SMALL_SHARDED + SMALL
    late_names = BIG[1:]

    def whole(n, g):
        return g.reshape(-1, g.shape[2]) if n in ROW_SHARDED else jnp.concatenate([g[j] for j in range(N_POS)], axis=1)

    def slabs(G, n, dtype=f32):
        r, c = shapes[n]
        full = G[n].astype(dtype)
        return full.reshape(N_POS, r, c) if n in ROW_SHARDED else full.reshape(r, N_POS, c).transpose(1, 0, 2)

    g_w_in, g_small = _xy_exchange("gather_w_in", [P['w_in'].astype(bf16), _pack128([P[n] for n in SMALL_SHARDED])],
                                   scatter=False)
    W = {n: P[n] for n in SMALL}
    W['w_in_pad'] = _pad_w_in(whole('w_in', g_w_in))
    per_pos = [_unpack128(g_small[j], sh_shapes) for j in range(N_POS)]
    for q, n in enumerate(SMALL_SHARDED):
        W[n] = jnp.concatenate([per_pos[j][q] for j in range(N_POS)], axis=1)
    late = dict(shards=[P[n].astype(bf16) for n in late_names],
                assemble=lambda gathered: {n: whole(n, g) for n, g in zip(late_names, gathered)},
                slabs=lambda G: [slabs(G, n, bf16) for n in late_names],
                w_in_slabs=lambda G: [slabs({'w_in': _unpad_w_in(G['w_in_pad'])}, 'w_in', bf16)])

    loss_rows, dx, G = _local_step(x, loss_target, W, late)
    arrived_late = G.pop('_arrived')
    (arrived_w_in,) = G.pop('_arrived_w_in')
    G.pop('w_in_pad')

    small_slabs = jnp.stack([_pack128([slabs(G, n)[j] for n in SMALL_SHARDED] + [G[n] for n in SMALL]) for j in range(N_POS)])
    (arrived_small,) = _xy_exchange("scatter_small", [small_slabs], scatter=True)
    contributions = [arrived_w_in] + list(arrived_late) + [arrived_small]
    tags = list(BIG) + ['small']
    plane = [_sum_slots("sum_" + t, cbuf) for t, cbuf in zip(tags, contributions)]
    sibling = _sibling_exchange("sibling_grads", plane)

    out = {}
    names4 = ('grad', 'delta', 'new_m', 'new_v')
    for q, n in enumerate(BIG):
        for tag, t in zip(names4, _adamw("adamw_" + n, P[n], plane[q], sibling[q], M[n], V[n])):
            out[tag + '_' + n] = t
    small_out = _adamw("adamw_small", _pack128([P[n] for n in packed]), plane[-1], sibling[-1],
                       _pack128([M[n] for n in packed]), _pack128([V[n] for n in packed]))
    for tag, buf in zip(names4, small_out):
        for n, t in zip(packed, _unpack128(buf, [shapes[n] for n in packed])):
            out[tag + '_' + n] = t
    loss = lax.psum(loss_rows[0, 0], ("x", "y", "c"))
    return loss, dx, out


def kernel(x, norm1_g, w_in, rwkv_mu, rwkv_w0, rwkv_w2, rwkv_a0, rwkv_a2, rwkv_g2, rwkv_k_k, rwkv_k_a, rwkv_r_k, rwkv_ln_w, rwkv_ln_b, rwkv_proj, gdn_conv_w, gdn_a_log, gdn_dt_bias, gdn_norm_w, gdn_proj, w_out, norm2_g, ffn_up, ffn_conv_w, ffn_down, final_g, loss_target, m_norm1_g, m_w_in, m_rwkv_mu, m_rwkv_w0, m_rwkv_w2, m_rwkv_a0, m_rwkv_a2, m_rwkv_g2, m_rwkv_k_k, m_rwkv_k_a, m_rwkv_r_k, m_rwkv_ln_w, m_rwkv_ln_b, m_rwkv_proj, m_gdn_conv_w, m_gdn_a_log, m_gdn_dt_bias, m_gdn_norm_w, m_gdn_proj, m_w_out, m_norm2_g, m_ffn_up, m_ffn_conv_w, m_ffn_down, m_final_g, v_norm1_g, v_w_in, v_rwkv_mu, v_rwkv_w0, v_rwkv_w2, v_rwkv_a0, v_rwkv_a2, v_rwkv_g2, v_rwkv_k_k, v_rwkv_k_a, v_rwkv_r_k, v_rwkv_ln_w, v_rwkv_ln_b, v_rwkv_proj, v_gdn_conv_w, v_gdn_a_log, v_gdn_dt_bias, v_gdn_norm_w, v_gdn_proj, v_w_out, v_norm2_g, v_ffn_up, v_ffn_conv_w, v_ffn_down, v_final_g):
    weights = (norm1_g, w_in, rwkv_mu, rwkv_w0, rwkv_w2, rwkv_a0, rwkv_a2, rwkv_g2, rwkv_k_k, rwkv_k_a, rwkv_r_k, rwkv_ln_w,
               rwkv_ln_b, rwkv_proj, gdn_conv_w, gdn_a_log, gdn_dt_bias, gdn_norm_w, gdn_proj, w_out, norm2_g, ffn_up,
               ffn_conv_w, ffn_down, final_g)
    m_in = (m_norm1_g, m_w_in, m_rwkv_mu, m_rwkv_w0, m_rwkv_w2, m_rwkv_a0, m_rwkv_a2, m_rwkv_g2, m_rwkv_k_k, m_rwkv_k_a,
            m_rwkv_r_k, m_rwkv_ln_w, m_rwkv_ln_b, m_rwkv_proj, m_gdn_conv_w, m_gdn_a_log, m_gdn_dt_bias, m_gdn_norm_w,
            m_gdn_proj, m_w_out, m_norm2_g, m_ffn_up, m_ffn_conv_w, m_ffn_down, m_final_g)
    v_in = (v_norm1_g, v_w_in, v_rwkv_mu, v_rwkv_w0, v_rwkv_w2, v_rwkv_a0, v_rwkv_a2, v_rwkv_g2, v_rwkv_k_k, v_rwkv_k_a,
            v_rwkv_r_k, v_rwkv_ln_w, v_rwkv_ln_b, v_rwkv_proj, v_gdn_conv_w, v_gdn_a_log, v_gdn_dt_bias, v_gdn_norm_w,
            v_gdn_proj, v_w_out, v_norm2_g, v_ffn_up, v_ffn_conv_w, v_ffn_down, v_final_g)
    drop = lambda n, a: a if n == 'final_g' else a[0]
    P = {n: drop(n, a) for n, a in zip(WEIGHTS, weights)}
    M = {n: drop(n, a) for n, a in zip(WEIGHTS, m_in)}
    V = {n: drop(n, a) for n, a in zip(WEIGHTS, v_in)}
    loss, dx, out = _step(x[0], loss_target[0], P, M, V)
    lift = lambda n, a: a if n == 'final_g' else a[None]
    res = [loss, dx[None]]
    for tag in ('grad', 'delta', 'new_m', 'new_v'):
        res += [lift(n, out[tag + '_' + n]) for n in WEIGHTS]
    return tuple(res)
```

```python
import functools

import jax
import jax.numpy as jnp
from jax import lax
from jax.experimental import pallas as pl
from jax.experimental.pallas import tpu as pltpu

f32 = jnp.float32
bf16 = jnp.bfloat16
HI = lax.Precision.HIGHEST

D_MODEL = 1024
RWKV_HEADS, RWKV_HD, RWKV_W = 8, 64, 512
GDN_HEADS, GDN_HD, GDN_W = 4, 128, 512
FFN_H = 2816
NORM_EPS, L2_EPS, GN_EPS = 1e-6, 1e-6, 64e-5
W_AB = 256
OFF_QKV, OFF_Z, OFF_GATES, OFF_AB = 1792, 3328, 3840, 5888
W_IN_PAD = OFF_AB + W_AB
WKV_CHUNK = 64
GDN_CHUNK = 128
HALO = 8
LANES = 128
TILE_BYTES = 1 << 20
VMEM_LIMIT = 56 * 1024 * 1024

ADAM_LR, ADAM_B1, ADAM_B2, ADAM_EPS, ADAM_WD, ADAM_STEP = 0.001, 0.9, 0.999, 1e-08, 0.01, 10

ROW_SHARDED = ('w_out', 'ffn_down')
SMALL = ('norm1_g', 'rwkv_mu', 'rwkv_w0', 'rwkv_a0', 'rwkv_k_k', 'rwkv_k_a', 'rwkv_r_k', 'rwkv_ln_w', 'rwkv_ln_b',
         'gdn_a_log', 'gdn_dt_bias', 'gdn_norm_w', 'norm2_g', 'final_g')
WEIGHTS = ('norm1_g', 'w_in', 'rwkv_mu', 'rwkv_w0', 'rwkv_w2', 'rwkv_a0', 'rwkv_a2', 'rwkv_g2', 'rwkv_k_k', 'rwkv_k_a',
           'rwkv_r_k', 'rwkv_ln_w', 'rwkv_ln_b', 'rwkv_proj', 'gdn_conv_w', 'gdn_a_log', 'gdn_dt_bias', 'gdn_norm_w',
           'gdn_proj', 'w_out', 'norm2_g', 'ffn_up', 'ffn_conv_w', 'ffn_down', 'final_g')


def _params(*sem):
    return pltpu.CompilerParams(dimension_semantics=sem, vmem_limit_bytes=VMEM_LIMIT)


def _tile(n, limit):
    if n <= limit:
        return n
    best = None
    for d in range(128, limit + 1, 128):
        if n % d == 0:
            best = d
    if best is None:
        raise ValueError(f"no tile for {n} under {limit}")
    return best


MM_BLOCK_BYTES = 4 << 20


def _mm(a, b, mode, name, add=None, out_dtype=f32, side=None, col_slabs=None):
    if mode == 'nn':
        (M, K), N = a.shape, b.shape[1]
    elif mode == 'nt':
        (M, K), N = a.shape, b.shape[0]
    else:
        (K, M), N = a.shape, b.shape[1]
    tm = _tile(M, 1408)
    tk = _tile(K, min(2816, MM_BLOCK_BYTES // (tm * a.dtype.itemsize)))
    tn = _tile(N, max(128, min(MM_BLOCK_BYTES // (tk * b.dtype.itemsize), MM_BLOCK_BYTES // (tm * 4)) // 128 * 128))
    if col_slabs is not None:
        tn = N // col_slabs
    nk = K // tk
    grid = (M // tm, N // tn, nk)
    dn = {'nn': (((1,), (0,)), ((), ())), 'nt': (((1,), (1,)), ((), ())), 'tn': (((0,), (0,)), ((), ()))}[mode]
    n_add = 0 if add is None else 1
    n_side = 0 if side is None else len(side[0])

    def body(a_ref, b_ref, *rest):
        add_ref = rest[0] if add is not None else None
        side_in, rest = rest[n_add:n_add + n_side], rest[n_add + n_side:]
        o_ref, side_out, rest = rest[0], rest[1:1 + n_side], rest[1 + n_side:]
        acc_ref, rest = (rest[0], rest[1:]) if nk > 1 else (None, rest)
        ids = [pl.program_id(d) for d in range(3)]
        if side is not None:
            start, finish = _xy_copies(side_in, side_out, rest, side[1])
            pl.when((ids[0] == 0) & (ids[1] == 0) & (ids[2] == 0))(start)
        acc = lax.dot_general(a_ref[...].astype(bf16), b_ref[...].astype(bf16), dn, preferred_element_type=f32)
        if nk == 1:
            o_ref[...] = (acc + add_ref[...] if add is not None else acc).astype(out_dtype)
        else:
            k = ids[2]

            @pl.when(k == 0)
            def _():
                acc_ref[...] = acc + add_ref[...] if add is not None else acc

            @pl.when(k > 0)
            def _():
                acc_ref[...] += acc

            @pl.when(k == nk - 1)
            def _():
                o_ref[...] = acc_ref[...].astype(out_dtype)
        if side is not None:
            pl.when((ids[0] == grid[0] - 1) & (ids[1] == grid[1] - 1) & (ids[2] == nk - 1))(finish)

    a_spec = (pl.BlockSpec((tk, tm), lambda i, j, k: (k, i)) if mode == 'tn'
              else pl.BlockSpec((tm, tk), lambda i, j, k: (i, k)))
    b_spec = (pl.BlockSpec((tn, tk), lambda i, j, k: (j, k)) if mode == 'nt'
              else pl.BlockSpec((tk, tn), lambda i, j, k: (k, j)))
    o_spec = pl.BlockSpec((tm, tn), lambda i, j, k: (i, j))
    o_shape = jax.ShapeDtypeStruct((M, N), out_dtype)
    if col_slabs is not None:
        o_spec = pl.BlockSpec((None, tm, tn), lambda i, j, k: (j, i, 0))
        o_shape = jax.ShapeDtypeStruct((col_slabs, M, tn), out_dtype)
    any_spec = pl.BlockSpec(memory_space=pl.ANY)
    side_bufs = [] if side is None else list(side[0])
    ins, specs = [a, b], [a_spec, b_spec]
    if add is not None:
        ins.append(add)
        specs.append(o_spec)
    outs = pl.pallas_call(
        body, grid=grid, in_specs=specs + [any_spec] * n_side, out_specs=[o_spec] + [any_spec] * n_side,
        out_shape=[o_shape] + (_xy_out_shapes(side_bufs, side[1]) if side is not None else []),
        scratch_shapes=([pltpu.VMEM((tm, tn), f32)] if nk > 1 else []) + (_xy_sems(n_side, side[1]) if side is not None else []),
        name=name,
        compiler_params=_params(*(("arbitrary",) * 3 if side is not None else ("parallel", "parallel", "arbitrary"))))(
            *ins, *side_bufs)
    return list(outs) if side is not None else outs[0]


def _shift_down(cur, prev, s):
    if s == 0:
        return cur
    ext = jnp.concatenate([prev, cur], axis=0)
    return pltpu.roll(ext, s, 0)[HALO:]


def _shift_up(cur, nxt, s):
    if s == 0:
        return cur
    ext = jnp.concatenate([cur, nxt], axis=0)
    return pltpu.roll(ext, ext.shape[0] - s, 0)[:cur.shape[0]]


def _conv_apply(cur, prev, w_ref, shifted=None):
    taps = w_ref.shape[0]
    out = None
    for i in range(taps):
        s = taps - 1 - i
        term = (shifted[s] if shifted is not None else _shift_down(cur, prev, s)) * w_ref[pl.ds(i, 1), :]
        out = term if out is None else out + term
    return out


def _row_spec(tm, w, col=0):
    return pl.BlockSpec((tm, w), lambda i: (i, col))


def _cols(a, width, col):
    return (a, width, col)


def _row_of(r):
    return r if isinstance(r, tuple) else (r, r.shape[1], 0)


def _prev_spec(tm, w):
    return pl.BlockSpec((HALO, w), lambda i: (jnp.maximum(i * (tm // HALO) - 1, 0), 0))


def _next_spec(tm, w, T):
    return pl.BlockSpec((HALO, w), lambda i: (jnp.minimum((i + 1) * (tm // HALO), T // HALO - 1), 0))


def _full_spec(shape):
    return pl.BlockSpec(shape, lambda i: (0,) * len(shape))


def _pw_fwd(name, fn, rows, consts, out_widths, tm, conv_w=None, out_dtype=f32, strip=None):
    T = _row_of(rows[0])[0].shape[0]
    nr, nc = len(rows), len(consts)

    def body(*refs):
        i = pl.program_id(0)
        if strip is not None:
            for j in range(out_widths[0] // strip):
                sl = slice(strip * j, strip * (j + 1))
                outs = fn(*[r[:, sl] for r in refs[:nr + nc]])
                for o_ref, o in zip(refs[nr + nc:], outs):
                    o_ref[:, sl] = o.astype(out_dtype)
            return
        vals = [r[...] for r in refs[:nr]]
        p = nr
        if conv_w is not None:
            prev = jnp.where(i > 0, refs[p][...], 0.0)
            vals[0] = _conv_apply(vals[0], prev, refs[p + 1])
            p += 2
        cvals = [r[...] for r in refs[p:p + nc]]
        outs = fn(*vals, *cvals)
        for o_ref, o in zip(refs[p + nc:], outs):
            o_ref[...] = o.astype(out_dtype)

    ins = [_row_of(r)[0] for r in rows]
    specs = [_row_spec(tm, *_row_of(r)[1:]) for r in rows]
    if conv_w is not None:
        ins += [rows[0], conv_w]
        specs += [_prev_spec(tm, rows[0].shape[1]), _full_spec(conv_w.shape)]
    ins += list(consts)
    specs += [_full_spec(c.shape) for c in consts]
    outs = pl.pallas_call(
        body, grid=(T // tm,), in_specs=specs,
        out_specs=[_row_spec(tm, w) for w in out_widths],
        out_shape=[jax.ShapeDtypeStruct((T, w), out_dtype) for w in out_widths], name=name,
        compiler_params=_params("parallel"))(*ins)
    return outs


def _pw_bwd(name, fn, rows, consts, cots, tm, add_to_first=None, row_dtypes=None, strip=None):
    rows = [_row_of(r) for r in rows]
    T = rows[0][0].shape[0]
    nr, nc = len(rows), len(consts)
    flat_cots = [c for grp in cots for c in grp]
    row_dtypes = row_dtypes or [f32] * nr
    n_extra = 0 if add_to_first is None else 1
    width = rows[0][1]

    def body(*refs):
        i = pl.program_id(0)
        in_refs, cot_refs = refs[:nr + nc], refs[nr + nc:nr + nc + len(flat_cots)]
        extra_ref = refs[nr + nc + len(flat_cots)] if add_to_first is not None else None
        row_out = refs[nr + nc + len(flat_cots) + n_extra:][:nr]
        const_out = refs[nr + nc + len(flat_cots) + n_extra + nr:]

        @pl.when(i == 0)
        def _():
            for q in range(nc):
                const_out[q][...] = jnp.zeros_like(const_out[q])

        def part(sl):
            cot_vals, p = [], 0
            for grp in cots:
                acc = cot_refs[p][:, sl]
                for q in range(1, len(grp)):
                    acc = acc + cot_refs[p + q][:, sl]
                p += len(grp)
                cot_vals.append(acc)
            _, vjp = jax.vjp(fn, *[r[:, sl] for r in in_refs])
            grads = vjp(tuple(cot_vals))
            for q in range(nr):
                g = grads[q]
                if q == 0 and extra_ref is not None:
                    g = g + extra_ref[:, sl]
                row_out[q][:, sl] = g.astype(row_dtypes[q])
            for q in range(nc):
                const_out[q][:, sl] += grads[nr + q]

        if strip is None:
            part(slice(None))
        else:
            for j in range(width // strip):
                part(slice(strip * j, strip * (j + 1)))

    ins = [r[0] for r in rows] + list(consts) + flat_cots
    specs = ([_row_spec(tm, r[1], r[2]) for r in rows] + [_full_spec(c.shape) for c in consts]
             + [_row_spec(tm, c.shape[1]) for c in flat_cots])
    if add_to_first is not None:
        ins.append(add_to_first)
        specs.append(_row_spec(tm, add_to_first.shape[1]))
    out_shapes = ([jax.ShapeDtypeStruct((T, r[1]), d) for r, d in zip(rows, row_dtypes)]
                  + [jax.ShapeDtypeStruct(c.shape, f32) for c in consts])
    out_specs = [_row_spec(tm, r[1]) for r in rows] + [_full_spec(c.shape) for c in consts]
    outs = pl.pallas_call(
        body, grid=(T // tm,), in_specs=specs, out_specs=out_specs, out_shape=out_shapes, name=name,
        compiler_params=_params("arbitrary"))(*ins)
    return list(outs[:nr]), list(outs[nr:])


def _pw_conv_bwd(name, fn, rows, consts, cots, conv_w, tm, row_dtypes=None):
    T, W0 = rows[0].shape
    nr, nc = len(rows), len(consts)
    taps = conv_w.shape[0]
    nblk = T // tm
    flat_cots = [c for grp in cots for c in grp]
    row_dtypes = row_dtypes or [f32] * nr

    def body(*refs):
        i = pl.program_id(0)
        p = 0
        cur = [r[...] for r in refs[p:p + nr]]; p += nr
        nxt = [r[...] for r in refs[p:p + nr]]; p += nr
        prev = jnp.where(i > 0, refs[p][...], 0.0); p += 1
        w_ref = refs[p]; p += 1
        cvals = [r[...] for r in refs[p:p + nc]]; p += nc

        def summed(p0):
            out, q = [], p0
            for grp in cots:
                acc = refs[q][...]
                for t in range(1, len(grp)):
                    acc = acc + refs[q + t][...]
                q += len(grp)
                out.append(acc)
            return out, q

        cot_cur, p = summed(p)
        cot_nxt, p = summed(p)
        row_out, dw_ref, const_out = refs[p:p + nr], refs[p + nr], refs[p + nr + 1:]

        x_cur = cur[0]
        x_down = [_shift_down(x_cur, prev, s_) for s_ in range(taps)]
        _, vjp = jax.vjp(fn, _conv_apply(x_cur, prev, w_ref, x_down), *cur[1:], *cvals)
        grads = vjp(tuple(cot_cur))
        _, vjp_n = jax.vjp(fn, _conv_apply(nxt[0], x_cur[tm - HALO:], w_ref), *nxt[1:], *cvals)
        dc_n = jnp.where(i < nblk - 1, vjp_n(tuple(cot_nxt))[0], 0.0)
        dc = grads[0]

        @pl.when(i == 0)
        def _():
            dw_ref[...] = jnp.zeros_like(dw_ref)
            for q in range(nc):
                const_out[q][...] = jnp.zeros_like(const_out[q])

        dx = None
        for k in range(taps):
            s_ = taps - 1 - k
            term = _shift_up(dc, dc_n, s_) * w_ref[pl.ds(k, 1), :]
            dx = term if dx is None else dx + term
            dw_ref[pl.ds(k, 1), :] += jnp.sum(dc * x_down[s_], axis=0, keepdims=True)
        row_out[0][...] = dx.astype(row_dtypes[0])
        for q in range(1, nr):
            row_out[q][...] = grads[q].astype(row_dtypes[q])
        for q in range(nc):
            const_out[q][...] += grads[nr + q]

    ins = list(rows) + list(rows) + [rows[0], conv_w] + list(consts) + flat_cots + flat_cots
    specs = ([_row_spec(tm, r.shape[1]) for r in rows] + [_next_spec(tm, r.shape[1], T) for r in rows]
             + [_prev_spec(tm, W0), _full_spec(conv_w.shape)] + [_full_spec(c.shape) for c in consts]
             + [_row_spec(tm, c.shape[1]) for c in flat_cots] + [_next_spec(tm, c.shape[1], T) for c in flat_cots])
    out_shapes = ([jax.ShapeDtypeStruct(r.shape, d) for r, d in zip(rows, row_dtypes)]
                  + [jax.ShapeDtypeStruct(conv_w.shape, f32)] + [jax.ShapeDtypeStruct(c.shape, f32) for c in consts])
    out_specs = ([_row_spec(tm, r.shape[1]) for r in rows] + [_full_spec(conv_w.shape)]
                 + [_full_spec(c.shape) for c in consts])
    outs = pl.pallas_call(
        body, grid=(nblk,), in_specs=specs, out_specs=out_specs, out_shape=out_shapes, name=name,
        compiler_params=_params("arbitrary"))(*ins)
    return list(outs[:nr]), outs[nr], list(outs[nr + 1:])


def _sigmoid(x):
    return 0.5 * jnp.tanh(0.5 * x) + 0.5


def _softplus(x):
    return jnp.maximum(x, 0.0) + jnp.log(1.0 + jnp.exp(jnp.minimum(x, -x)))


def _seg_sum_impl(x, seg):
    w = x.shape[-1]
    r = lax.broadcasted_iota(jnp.int32, (w, w), 0) // seg
    c = lax.broadcasted_iota(jnp.int32, (w, w), 1) // seg
    ones = (r == c).astype(bf16)
    hi = x.astype(bf16)
    lo = (x - hi.astype(f32)).astype(bf16)
    return (jnp.dot(hi, ones, preferred_element_type=f32) + jnp.dot(lo, ones, preferred_element_type=f32))


@functools.partial(jax.custom_vjp, nondiff_argnums=(1,))
def _seg_sum(x, seg):
    return _seg_sum_impl(x, seg)


_seg_sum.defvjp(lambda x, seg: (_seg_sum_impl(x, seg), None), lambda seg, _, g: (_seg_sum_impl(g, seg),))


def _rms(x, g):
    return x * lax.rsqrt(jnp.mean(x * x, axis=-1, keepdims=True) + NORM_EPS) * g


def _rms_fn(x, g):
    return (_rms(x, g),)


def _loss_rows(x2, tgt, g):
    e = _rms(x2, g) - tgt
    return 0.5 * jnp.sum(e * e, axis=-1, keepdims=True) * (1.0 / D_MODEL)


def _rwkv_prep_fn(ps, w0, w2p, a0, a2p, g2, k_k, k_a):
    r, k, v = ps[:, 0:512], ps[:, 512:1024], ps[:, 1024:1536]
    wa, gl = ps[:, 1536:1664], ps[:, 1664:1792]
    z = w0 + jnp.dot(jnp.tanh(wa), w2p, precision=HI, preferred_element_type=f32)
    w_log = -_softplus(-z) - 0.5
    lw = -jnp.exp(w_log)
    a = _sigmoid(a0 + jnp.dot(wa, a2p, precision=HI, preferred_element_type=f32))
    g = jnp.dot(_sigmoid(gl), g2, precision=HI, preferred_element_type=f32)
    kx = k * k_k
    kk = kx * lax.rsqrt(_seg_sum(kx * kx, RWKV_HD) + L2_EPS)
    k2 = k * (1.0 + (a - 1.0) * k_a)
    return r, lw, k2, v, -kk, kk * a, g


def _rwkv_post_fn(y, r, k2, v, g, ln_w, ln_b, rk):
    mean = _seg_sum(y, RWKV_HD) * (1.0 / RWKV_HD)
    yc = y - mean
    var = _seg_sum(yc * yc, RWKV_HD) * (1.0 / RWKV_HD)
    yn = yc * lax.rsqrt(var + GN_EPS) * ln_w + ln_b
    bonus = _seg_sum(r * k2 * rk, RWKV_HD) * v
    return ((yn + bonus) * g,)


def _gdn_prep_fn(c, ab, al_p, dt_p):
    s = c * _sigmoid(c)
    q, k, v = s[:, 0:512], s[:, 512:1024], s[:, 1024:1536]
    q = q * lax.rsqrt(_seg_sum(q * q, GDN_HD) + L2_EPS) * (GDN_HD ** -0.5)
    k = k * lax.rsqrt(_seg_sum(k * k, GDN_HD) + L2_EPS)
    lane = lax.broadcasted_iota(jnp.int32, ab.shape, 1)
    gpart = -jnp.exp(al_p) * _softplus(ab + dt_p)
    gbeta = jnp.where(lane < GDN_HEADS, gpart, jnp.where(lane < 2 * GDN_HEADS, _sigmoid(ab), 0.0))
    return q, k, v, gbeta


def _gdn_post_fn(o, z, nw):
    ms = _seg_sum(o * o, GDN_HD) * (1.0 / GDN_HD)
    return (o * lax.rsqrt(ms + NORM_EPS) * nw * (z * _sigmoid(z)),)


def _mix_fn(ga, gb, ya, yb):
    return (_sigmoid(ga) * ya + _sigmoid(gb) * yb,)


STRIP = 128


def _ffn_strip_fn(cg, cu):
    return cg * _sigmoid(cg) * cu


def _strip_conv(ref, prev_ref, w_ref, sl, first, taps):
    cur = ref[:, sl]
    prev = jnp.where(first, 0.0, prev_ref[:, sl])
    down = [_shift_down(cur, prev, s) for s in range(taps)]
    conv = None
    for k in range(taps):
        term = down[taps - 1 - k] * w_ref[pl.ds(k, 1), sl]
        conv = term if conv is None else conv + term
    return cur, down, conv


def _ffn_act_fwd(h, w, tm):
    T, W2 = h.shape
    H = W2 // 2
    taps = w.shape[0]

    def body(h_ref, hp_ref, w_ref, o_ref):
        first = pl.program_id(0) == 0
        for j in range(H // STRIP):
            gs, us = slice(STRIP * j, STRIP * (j + 1)), slice(H + STRIP * j, H + STRIP * (j + 1))
            cg = _strip_conv(h_ref, hp_ref, w_ref, gs, first, taps)[2]
            cu = _strip_conv(h_ref, hp_ref, w_ref, us, first, taps)[2]
            o_ref[:, gs] = _ffn_strip_fn(cg, cu).astype(o_ref.dtype)

    return pl.pallas_call(
        body, grid=(T // tm,), in_specs=[_row_spec(tm, W2), _prev_spec(tm, W2), _full_spec(w.shape)],
        out_specs=_row_spec(tm, H), out_shape=jax.ShapeDtypeStruct((T, H), bf16), name="ffn_act",
        compiler_params=_params("parallel"))(h, h, w)


def _ffn_act_bwd(h, dact, w, tm):
    T, W2 = h.shape
    H = W2 // 2
    taps = w.shape[0]
    nblk = T // tm

    def body(h_ref, hp_ref, hn_ref, d_ref, dn_ref, w_ref, dh_ref, dw_ref):
        i = pl.program_id(0)
        first, last = i == 0, i == nblk - 1

        @pl.when(first)
        def _():
            dw_ref[...] = jnp.zeros_like(dw_ref)

        for j in range(H // STRIP):
            gs, us = slice(STRIP * j, STRIP * (j + 1)), slice(H + STRIP * j, H + STRIP * (j + 1))
            parts = {}
            for name, sl in (('g', gs), ('u', us)):
                cur, down, conv = _strip_conv(h_ref, hp_ref, w_ref, sl, first, taps)
                nxt = hn_ref[:, sl]
                conv_n = None
                for k in range(taps):
                    term = _shift_down(nxt, cur[tm - HALO:], taps - 1 - k) * w_ref[pl.ds(k, 1), sl]
                    conv_n = term if conv_n is None else conv_n + term
                parts[name] = (down, conv, conv_n)
            _, vjp = jax.vjp(_ffn_strip_fn, parts['g'][1], parts['u'][1])
            dcs = vjp(d_ref[:, gs])
            _, vjp_n = jax.vjp(_ffn_strip_fn, parts['g'][2], parts['u'][2])
            dcs_n = vjp_n(jnp.where(last, 0.0, dn_ref[:, gs]))
            for (name, sl), dc, dc_n in zip((('g', gs), ('u', us)), dcs, dcs_n):
                down = parts[name][0]
                dx = None
                for k in range(taps):
                    s_ = taps - 1 - k
                    term = _shift_up(dc, dc_n, s_) * w_ref[pl.ds(k, 1), sl]
                    dx = term if dx is None else dx + term
                    dw_ref[pl.ds(k, 1), sl] += jnp.sum(dc * down[s_], axis=0, keepdims=True)
                dh_ref[:, sl] = dx.astype(dh_ref.dtype)

    return pl.pallas_call(
        body, grid=(nblk,),
        in_specs=[_row_spec(tm, W2), _prev_spec(tm, W2), _next_spec(tm, W2, T), _row_spec(tm, H), _next_spec(tm, H, T),
                  _full_spec(w.shape)],
        out_specs=[_row_spec(tm, W2), _full_spec(w.shape)],
        out_shape=[jax.ShapeDtypeStruct((T, W2), bf16), jax.ShapeDtypeStruct(w.shape, f32)], name="ffn_act_bwd",
        compiler_params=_params("arbitrary"))(h, h, h, dact, dact, w)


N_POS = 4


def _xy_out_shapes(bufs, scatter):
    return [jax.ShapeDtypeStruct((N_POS,) + tuple(b.shape[1:] if scatter else b.shape), b.dtype) for b in bufs]


def _xy_sems(n, scatter):
    sems = [pltpu.SemaphoreType.DMA((3 * n,)), pltpu.SemaphoreType.DMA((3 * n,)), pltpu.SemaphoreType.DMA((n,))]
    return sems if scatter else sems + [pltpu.SemaphoreType.DMA((3 * n,)), pltpu.SemaphoreType.DMA((3 * n,))]


def _xy_copies(in_refs, out_refs, sems, scatter):
    n = len(in_refs)
    send_sems, recv_sems, local_sems = sems[:3]

    def place():
        x, y, c = lax.axis_index("x"), lax.axis_index("y"), lax.axis_index("c")
        return x, y, c, 2 * x + y, [(1 - x, y), (x, 1 - y), (1 - x, 1 - y)]

    def half(ref, a, which):
        rows = in_refs[a].shape[0] // 2
        return ref.at[pl.ds(pl.multiple_of(which * rows, HALO), rows)]

    def ici(a, k, src, dst, peer, c):
        return pltpu.make_async_remote_copy(
            src_ref=src, dst_ref=dst, send_sem=send_sems.at[3 * a + k], recv_sem=recv_sems.at[3 * a + k],
            device_id=(peer[0], peer[1], c), device_id_type=pl.DeviceIdType.MESH)

    def outgoing():
        x, y, c, me, peers = place()
        own = [pltpu.make_async_copy(in_refs[a].at[me] if scatter else in_refs[a], out_refs[a].at[me], local_sems.at[a])
               for a in range(n)]
        if scatter:
            sends = [ici(a, k, in_refs[a].at[2 * p[0] + p[1]], out_refs[a].at[me], p, c)
                     for a in range(n) for k, p in enumerate(peers)]
        else:
            sends = [ici(a, k, half(in_refs[a], a, c), half(out_refs[a].at[me], a, c), p, c)
                     for a in range(n) for k, p in enumerate(peers)]
        return own, sends

    def arrivals():
        x, y, c, me, peers = place()
        if scatter:
            return [ici(a, k, in_refs[a].at[me], out_refs[a].at[2 * p[0] + p[1]], p, c)
                    for a in range(n) for k, p in enumerate(peers)]
        return [ici(a, k, half(in_refs[a], a, c), half(out_refs[a].at[2 * p[0] + p[1]], a, c), p, c)
                for a in range(n) for k, p in enumerate(peers)]

    def to_sibling(mine):
        x, y, c, me, peers = place()
        which = c if mine else 1 - c
        return [pltpu.make_async_remote_copy(
            src_ref=half(out_refs[a].at[2 * p[0] + p[1]], a, which), dst_ref=half(out_refs[a].at[2 * p[0] + p[1]], a, which),
            send_sem=sems[3].at[3 * a + k], recv_sem=sems[4].at[3 * a + k],
            device_id=(x, y, 1 - c), device_id_type=pl.DeviceIdType.MESH) for a in range(n) for k, p in enumerate(peers)]

    def start():
        own, sends = outgoing()
        for cp in own + sends:
            cp.start()

    def finish():
        if scatter:
            for cp in arrivals():
                cp.wait_recv()
        else:
            passed = to_sibling(True)
            for cp, fwd in zip(arrivals(), passed):
                cp.wait_recv()
                fwd.start()
            for cp in to_sibling(False):
                cp.wait_recv()
            for fwd in passed:
                fwd.wait_send()
        own, sends = outgoing()
        for cp in sends:
            cp.wait_send()
        for cp in own:
            cp.wait()

    return start, finish


_NN, _NT, _TN = 'hcs,hsd->hcd', 'hcd,hsd->hcs', 'hcd,hce->hde'


def _lo(spec, a, b):
    return jnp.einsum(spec, a.astype(bf16), b.astype(bf16), preferred_element_type=f32)


@jax.custom_vjp
def _bmm(a, b):
    return _lo(_NN, a, b)


_bmm.defvjp(lambda a, b: (_lo(_NN, a, b), (a, b)), lambda ab, g: (_lo(_NT, g, ab[1]), _lo(_TN, ab[0], g)))


@jax.custom_vjp
def _bmm_nt(a, b):
    return _lo(_NT, a, b)


_bmm_nt.defvjp(lambda a, b: (_lo(_NT, a, b), (a, b)), lambda ab, g: (_lo(_NN, g, ab[1]), _lo(_TN, g, ab[0])))


@jax.custom_vjp
def _bmm_tn(a, b):
    return _lo(_TN, a, b)


_bmm_tn.defvjp(lambda a, b: (_lo(_TN, a, b), (a, b)), lambda ab, g: (_lo(_NT, ab[1], g), _lo(_NN, ab[0], g)))


def _masks(H, C):
    row = lax.broadcasted_iota(jnp.int32, (H, C, C), 1)
    col = lax.broadcasted_iota(jnp.int32, (H, C, C), 2)
    return row, col


def _tri_inv_impl(L):
    H, C, _ = L.shape
    row, col = _masks(H, C)
    eye = (row == col).astype(f32)
    base = 16
    same = (row // base) == (col // base)
    Ld = jnp.where(same, L, 0.0)
    X = -Ld
    inv = eye + X
    for _ in range(3):
        X = _bmm(X, X)
        inv = _bmm(inv, eye + X)
    if C == base:
        return inv
    N = _bmm(inv, L - Ld)
    out = eye - N
    levels = C // base
    P = N
    span = 2
    while span < levels:
        P = _bmm(P, P)
        out = _bmm(out, eye + P)
        span *= 2
    return _bmm(out, inv)


@jax.custom_vjp
def _tri_inv(L):
    return _tri_inv_impl(L)


def _tri_inv_fwd(L):
    T = _tri_inv_impl(L)
    return T, T


def _tri_inv_bwd(T, dT):
    return (-_bmm_nt(_bmm_tn(T, dT), T),)


_tri_inv.defvjp(_tri_inv_fwd, _tri_inv_bwd)


def _cumsum_impl(x, reverse):
    C = x.shape[1]
    row = lax.broadcasted_iota(jnp.int32, x.shape, 1)
    s = 1
    while s < C:
        if reverse:
            x = x + jnp.where(row < C - s, pltpu.roll(x, C - s, 1), 0.0)
        else:
            x = x + jnp.where(row >= s, pltpu.roll(x, s, 1), 0.0)
        s *= 2
    return x


@jax.custom_vjp
def _cumsum(x):
    return _cumsum_impl(x, False)


_cumsum.defvjp(lambda x: (_cumsum_impl(x, False), None), lambda _, g: (_cumsum_impl(g, True),))


def _wkv_chunk(Z, r, lw, k, v, a, b):
    H, C, D = r.shape
    row, col = _masks(H, C)
    incl, strict = row >= col, row > col
    cw = _cumsum(lw)
    cwp = cw - lw
    cwl = jnp.sum(lw, axis=1, keepdims=True)
    en = jnp.exp(-cw)
    at, rt, bt, kt = a * jnp.exp(cwp), r * jnp.exp(cw), b * en, k * en
    Lab = jnp.where(strict, _bmm_nt(at, bt), 0.0)
    Lak = jnp.where(strict, _bmm_nt(at, kt), 0.0)
    Tm = _tri_inv(-Lab)
    U = _bmm(Tm, _bmm(at, Z) + _bmm(Lak, v))
    Rb = jnp.where(incl, _bmm_nt(rt, bt), 0.0)
    Rk = jnp.where(incl, _bmm_nt(rt, kt), 0.0)
    y = _bmm(rt, Z) + _bmm(Rb, U) + _bmm(Rk, v)
    ed = jnp.exp(cwl - cw)
    zdec = jnp.swapaxes(jnp.broadcast_to(jnp.exp(cwl), (H, Z.shape[2], D)), 1, 2)
    Z1 = Z * zdec + _bmm_tn(b * ed, U) + _bmm_tn(k * ed, v)
    return y, Z1


def _gdn_chunk(S, q, k, v, g, beta):
    H, C, D = q.shape
    row, col = _masks(H, C)
    incl, strict = row >= col, row > col
    gc = _cumsum(g)
    diff = gc - jnp.swapaxes(gc, 1, 2)
    decay = jnp.where(incl, jnp.exp(jnp.where(incl, diff, 0.0)), 0.0)
    gl = jnp.sum(g, axis=1, keepdims=True)
    kb, vb = k * beta, v * beta
    L = jnp.where(strict, _bmm_nt(kb, k) * decay, 0.0)
    Tm = _tri_inv(L)
    egc = jnp.exp(gc)
    u = _bmm(Tm, vb)
    wk = _bmm(Tm, kb * egc)
    attn = jnp.where(incl, _bmm_nt(q, k) * decay, 0.0)
    v_new = u - _bmm(wk, S)
    o = _bmm(q * egc, S) + _bmm(attn, v_new)
    S1 = S * jnp.exp(gl) + _bmm_tn(k * jnp.exp(gl - gc), v_new)
    return o, S1


def _wkv_block(Z, r, lw, k, v, a, b):
    lane = lax.broadcasted_iota(jnp.int32, (r.shape[0], 128), 1)
    low = lane < RWKV_HD

    def heads(t):
        out = []
        for p in range(RWKV_HEADS // 2):
            pair = t[:, 128 * p:128 * (p + 1)]
            out += [jnp.where(low, pair, 0.0), jnp.where(low, 0.0, pair)]
        return jnp.concatenate([t[None] for t in out], axis=0)

    y, Z1 = _wkv_chunk(Z, *[heads(t) for t in (r, lw, k, v, a, b)])
    return jnp.concatenate([y[2 * p] + y[2 * p + 1] for p in range(RWKV_HEADS // 2)], axis=1), Z1


def _gdn_block(S, q, k, v, gbeta):
    heads = lambda t: jnp.concatenate([t[None, :, GDN_HD * h:GDN_HD * (h + 1)] for h in range(GDN_HEADS)], axis=0)
    src = lax.broadcasted_iota(jnp.int32, (W_AB, 2 * GDN_W), 0)
    dst = lax.broadcasted_iota(jnp.int32, (W_AB, 2 * GDN_W), 1) // GDN_HD
    spread = jnp.dot(gbeta, (src == dst).astype(f32), precision=HI, preferred_element_type=f32)
    o, S1 = _gdn_chunk(S, heads(q), heads(k), heads(v), heads(spread[:, :GDN_W]), heads(spread[:, GDN_W:]))
    return jnp.concatenate([o[h] for h in range(GDN_HEADS)], axis=1), S1


def _scan_fwd(name, block_fn, ins, C, H, dh, w_out, side=None):
    T = ins[0].shape[0]
    n_in = len(ins)
    nblk = T // C
    n_side = 0 if side is None else len(side[0])

    def body(*refs):
        in_refs, refs = refs[:n_in], refs[n_in:]
        side_in, refs = refs[:n_side], refs[n_side:]
        y_ref, zs_ref, refs = refs[0], refs[1], refs[2:]
        side_out, refs = refs[:n_side], refs[n_side:]
        z_scr = refs[0]
        if side is not None:
            start, finish = _xy_copies(side_in, side_out, refs[1:], side[1])
            pl.when(pl.program_id(0) == 0)(start)

        @pl.when(pl.program_id(0) == 0)
        def _():
            z_scr[...] = jnp.zeros_like(z_scr)

        Z = z_scr[...]
        zs_ref[0] = Z
        y, Z1 = block_fn(Z, *[r[...] for r in in_refs])
        y_ref[...] = y
        z_scr[...] = Z1
        if side is not None:
            pl.when(pl.program_id(0) == nblk - 1)(finish)

    side_bufs = [] if side is None else list(side[0])
    any_spec = pl.BlockSpec(memory_space=pl.ANY)
    return pl.pallas_call(
        body, grid=(nblk,),
        in_specs=[pl.BlockSpec((C, a.shape[1]), lambda i: (i, 0)) for a in ins] + [any_spec] * n_side,
        out_specs=[pl.BlockSpec((C, w_out), lambda i: (i, 0)), pl.BlockSpec((1, H, dh, dh), lambda i: (i, 0, 0, 0))]
        + [any_spec] * n_side,
        out_shape=[jax.ShapeDtypeStruct((T, w_out), f32), jax.ShapeDtypeStruct((T // C, H, dh, dh), f32)]
        + (_xy_out_shapes(side_bufs, side[1]) if side is not None else []),
        scratch_shapes=[pltpu.VMEM((H, dh, dh), f32)] + (_xy_sems(n_side, side[1]) if side is not None else []), name=name,
        compiler_params=_params("arbitrary"))(*ins, *side_bufs)


def _scan_bwd(name, block_fn, ins, dy, zs, C, side=None):
    T = ins[0].shape[0]
    _, H, dh, _ = zs.shape
    n_in = len(ins)
    nblk = T // C
    n_side = 0 if side is None else len(side[0])

    def body(*refs):
        in_refs, dy_ref, zs_ref, refs = refs[:n_in], refs[n_in], refs[n_in + 1], refs[n_in + 2:]
        side_in, refs = refs[:n_side], refs[n_side:]
        out_refs, refs = refs[:n_in], refs[n_in:]
        side_out, refs = refs[:n_side], refs[n_side:]
        dz_scr = refs[0]
        if side is not None:
            start, finish = _xy_copies(side_in, side_out, refs[1:], side[1])
            pl.when(pl.program_id(0) == 0)(start)

        @pl.when(pl.program_id(0) == 0)
        def _():
            dz_scr[...] = jnp.zeros_like(dz_scr)

        _, vjp = jax.vjp(block_fn, zs_ref[0], *[r[...] for r in in_refs])
        grads = vjp((dy_ref[...], dz_scr[...]))
        dz_scr[...] = grads[0]
        for o_ref, gval in zip(out_refs, grads[1:]):
            o_ref[...] = gval
        if side is not None:
            pl.when(pl.program_id(0) == nblk - 1)(finish)

    side_bufs = [] if side is None else list(side[0])
    any_spec = pl.BlockSpec(memory_space=pl.ANY)
    rev = lambda i: (nblk - 1 - i, 0)
    return pl.pallas_call(
        body, grid=(nblk,),
        in_specs=[pl.BlockSpec((C, a.shape[1]), rev) for a in ins]
        + [pl.BlockSpec((C, dy.shape[1]), rev), pl.BlockSpec((1, H, dh, dh), lambda i: (nblk - 1 - i, 0, 0, 0))]
        + [any_spec] * n_side,
        out_specs=[pl.BlockSpec((C, a.shape[1]), rev) for a in ins] + [any_spec] * n_side,
        out_shape=[jax.ShapeDtypeStruct(a.shape, f32) for a in ins]
        + (_xy_out_shapes(side_bufs, side[1]) if side is not None else []),
        scratch_shapes=[pltpu.VMEM((H, dh, dh), f32)] + (_xy_sems(n_side, side[1]) if side is not None else []), name=name,
        compiler_params=_params("arbitrary"))(*ins, dy, zs, *side_bufs)


def _loss_call(x2, tgt, g, tm):
    T, W = x2.shape

    def body(x_ref, t_ref, g_ref, dx_ref, dg_ref, l_ref):
        i = pl.program_id(0)
        tv = t_ref[...]
        l, vjp = jax.vjp(lambda xv, gv: _loss_rows(xv, tv, gv), x_ref[...], g_ref[...])
        dx, dg = vjp(jnp.ones_like(l))
        dx_ref[...] = dx
        tot = jnp.zeros((1, 128), f32) + jnp.sum(l)

        @pl.when(i == 0)
        def _():
            dg_ref[...] = dg
            l_ref[...] = tot

        @pl.when(i > 0)
        def _():
            dg_ref[...] += dg
            l_ref[...] += tot

    return pl.pallas_call(
        body, grid=(T // tm,),
        in_specs=[_row_spec(tm, W), _row_spec(tm, W), _full_spec(g.shape)],
        out_specs=[_row_spec(tm, W), _full_spec(g.shape), _full_spec((1, 128))],
        out_shape=[jax.ShapeDtypeStruct((T, W), f32), jax.ShapeDtypeStruct(g.shape, f32),
                   jax.ShapeDtypeStruct((1, 128), f32)], name="loss_head",
        compiler_params=_params("arbitrary"))(x2, tgt, g)


def _local_step(x, tgt, W, late=None):
    row = lambda a: a.reshape(1, -1)
    wp = W['w_in_pad']
    w_rwkv, w_qkv, w_z = wp[:, :OFF_QKV], wp[:, OFF_QKV:OFF_Z], wp[:, OFF_Z:OFF_GATES]
    w_gates, w_ab = wp[:, OFF_GATES:OFF_AB], wp[:, OFF_AB:]
    mu = row(W['rwkv_mu'])
    mixw = jnp.concatenate([mu, 1.0 - mu], axis=0)
    zpad = jnp.zeros((64, RWKV_W), f32)
    w2p = jnp.concatenate([W['rwkv_w2'], zpad], axis=0)
    a2p = jnp.concatenate([zpad, W['rwkv_a2']], axis=0)
    rw_consts = [row(W['rwkv_w0']), w2p, row(W['rwkv_a0']), a2p, W['rwkv_g2'], row(W['rwkv_k_k']), row(W['rwkv_k_a'])]
    post_consts = [row(W['rwkv_ln_w']), row(W['rwkv_ln_b']), row(W['rwkv_r_k'])]
    pad4 = lambda a: jnp.pad(row(a), ((0, 0), (0, W_AB - GDN_HEADS)))
    gd_consts = [pad4(W['gdn_a_log']), pad4(W['gdn_dt_bias'])]
    nw_t = jnp.tile(row(W['gdn_norm_w']), (1, GDN_HEADS))
    g1, g2n, gf = row(W['norm1_g']), row(W['norm2_g']), row(W['final_g'])

    (u,) = _pw_fwd("norm1", _rms_fn, [x], [g1], [D_MODEL], 256, out_dtype=bf16)
    p_rwkv = _mm(u, w_rwkv, 'nn', "in_rwkv")
    qkv_raw = _mm(u, w_qkv, 'nn', "in_qkv")
    z = _mm(u, w_z, 'nn', "in_z")
    gates = _mm(u, w_gates, 'nn', "in_gates")
    ab = _mm(u, w_ab, 'nn', "in_ab")

    r, lw, k2, v, a_, b_, g = _pw_fwd("rwkv_prep", _rwkv_prep_fn, [p_rwkv], rw_consts, [RWKV_W] * 7, 256, conv_w=mixw)
    wkv_in = [r, lw, k2, v, a_, b_]
    y, zs_wkv, *gathered = _scan_fwd("wkv_fwd", _wkv_block, wkv_in, WKV_CHUNK, RWKV_HEADS, 2 * RWKV_HD, RWKV_W,
                                     side=None if late is None else (late['shards'], False))
    if late is not None:
        W = dict(W, **late['assemble'](gathered))
    (ya_in,) = _pw_fwd("rwkv_post", _rwkv_post_fn, [y, r, k2, v, g], post_consts, [RWKV_W], 256, out_dtype=bf16, strip=128)
    ya = _mm(ya_in, W['rwkv_proj'], 'nn', "rwkv_proj")

    gq, gk, gv, gbeta = _pw_fwd("gdn_prep", _gdn_prep_fn, [qkv_raw, ab], gd_consts, [GDN_W] * 3 + [W_AB], 256,
                                conv_w=W['gdn_conv_w'])
    gdn_in = [gq, gk, gv, gbeta]
    o, zs_gdn = _scan_fwd("gdn_fwd", _gdn_block, gdn_in, GDN_CHUNK, GDN_HEADS, GDN_HD, GDN_W)
    (yb_in,) = _pw_fwd("gdn_post", _gdn_post_fn, [o, z], [nw_t], [GDN_W], 256, out_dtype=bf16, strip=128)
    yb = _mm(yb_in, W['gdn_proj'], 'nn', "gdn_proj")

    ga, gb = _cols(gates, D_MODEL, 0), _cols(gates, D_MODEL, 1)
    (mixed,) = _pw_fwd("mix", _mix_fn, [ga, gb, ya, yb], [], [D_MODEL], 256, out_dtype=bf16, strip=256)
    x1 = _mm(mixed, W['w_out'], 'nn', "w_out", add=x)
    (u2,) = _pw_fwd("norm2", _rms_fn, [x1], [g2n], [D_MODEL], 256, out_dtype=bf16)
    h = _mm(u2, W['ffn_up'], 'nn', "ffn_up")
    act = _ffn_act_fwd(h, W['ffn_conv_w'], 256)
    x2 = _mm(act, W['ffn_down'], 'nn', "ffn_down", add=x1)

    G = {}
    slab_out = None if late is None else N_POS
    dx2, dgf, loss = _loss_call(x2, tgt, gf, 256)
    G['final_g'] = dgf
    dact = _mm(dx2, W['ffn_down'], 'nt', "d_act")
    G['ffn_down'] = _mm(act, dx2, 'tn', "g_ffn_down", out_dtype=bf16)
    dh, G['ffn_conv_w'] = _ffn_act_bwd(h, dact, W['ffn_conv_w'], 128)
    du2 = _mm(dh, W['ffn_up'], 'nt', "d_u2")
    G['ffn_up'] = _mm(u2, dh, 'tn', "g_ffn_up", out_dtype=bf16, col_slabs=slab_out)
    (dx1,), (G['norm2_g'],) = _pw_bwd("norm2_bwd", _rms_fn, [x1], [g2n], [(du2,)], 256, add_to_first=dx2)
    dmixed = _mm(dx1, W['w_out'], 'nt', "d_mixed")
    G['w_out'] = _mm(mixed, dx1, 'tn', "g_w_out", out_dtype=bf16)
    (dga, dgb, dya, dyb), _ = _pw_bwd("mix_bwd", _mix_fn, [ga, gb, ya, yb], [], [(dmixed,)], 256, row_dtypes=[bf16] * 4,
                                      strip=256)
    dya_in = _mm(dya, W['rwkv_proj'], 'nt', "d_ya_in")
    G['rwkv_proj'] = _mm(ya_in, dya, 'tn', "g_rwkv_proj", out_dtype=bf16, col_slabs=slab_out)
    dyb_in = _mm(dyb, W['gdn_proj'], 'nt', "d_yb_in")
    G['gdn_proj'] = _mm(yb_in, dyb, 'tn', "g_gdn_proj", out_dtype=bf16, col_slabs=slab_out)

    (do, dz), (dnw_t,) = _pw_bwd("gdn_post_bwd", _gdn_post_fn, [o, z], [nw_t], [(dyb_in,)], 256, row_dtypes=[f32, bf16],
                                 strip=128)
    G['gdn_norm_w'] = dnw_t.reshape(GDN_HEADS, GDN_HD).sum(axis=0)
    dgq, dgk, dgv, dgbeta = _scan_bwd("gdn_bwd", _gdn_block, gdn_in, do, zs_gdn, GDN_CHUNK)
    (dqkv_raw, dab), G['gdn_conv_w'], (dal_p, ddt_p) = _pw_conv_bwd(
        "gdn_prep_bwd", _gdn_prep_fn, [qkv_raw, ab], gd_consts, [(dgq,), (dgk,), (dgv,), (dgbeta,)], W['gdn_conv_w'], 256,
        row_dtypes=[bf16, bf16])
    G['gdn_a_log'], G['gdn_dt_bias'] = dal_p[0, :GDN_HEADS], ddt_p[0, :GDN_HEADS]

    (dy, dr1, dk21, dv1, dg_), (G['rwkv_ln_w'], G['rwkv_ln_b'], G['rwkv_r_k']) = _pw_bwd(
        "rwkv_post_bwd", _rwkv_post_fn, [y, r, k2, v, g], post_consts, [(dya_in,)], 256, strip=128)
    dr2, dlw, dk22, dv2, da_, db_, *G['_arrived'] = _scan_bwd(
        "wkv_bwd", _wkv_block, wkv_in, dy, zs_wkv, WKV_CHUNK, side=None if late is None else (late['slabs'](G), True))
    (dp_rwkv,), dmixw, rw_grads = _pw_conv_bwd(
        "rwkv_prep_bwd", _rwkv_prep_fn, [p_rwkv], rw_consts,
        [(dr1, dr2), (dlw,), (dk21, dk22), (dv1, dv2), (da_,), (db_,), (dg_,)], mixw, 256, row_dtypes=[bf16])
    G['rwkv_w0'], dw2p, G['rwkv_a0'], da2p, G['rwkv_g2'], G['rwkv_k_k'], G['rwkv_k_a'] = rw_grads
    G['rwkv_w2'], G['rwkv_a2'] = dw2p[:64], da2p[64:]
    G['rwkv_mu'] = dmixw[0] - dmixw[1]

    dp = jnp.concatenate([dp_rwkv, dqkv_raw, dz, dga, dgb, dab], axis=1)
    G['w_in_pad'] = _mm(u, dp, 'tn', "g_w_in", out_dtype=bf16)
    if late is None:
        du = _mm(dp, wp, 'nt', "d_u")
    else:
        du, *G['_arrived_w_in'] = _mm(dp, wp, 'nt', "d_u", side=(late['w_in_slabs'](G), True))
    (dx,), (G['norm1_g'],) = _pw_bwd("norm1_bwd", _rms_fn, [x], [g1], [(du,)], 256, add_to_first=dx1)
    return loss, dx, G


IN_WIDTH = OFF_AB + 8
PAD_ORDER = ((0, OFF_GATES), (OFF_GATES + 8, IN_WIDTH), (OFF_GATES, OFF_GATES + 8))


def _pad_w_in(w):
    return jnp.concatenate([w[:, a:b] for a, b in PAD_ORDER] + [jnp.zeros((w.shape[0], W_AB - 8), w.dtype)], axis=1)


def _pad_w_in_shards(shards):
    width = shards[0].shape[1]
    parts = []
    for a, b in PAD_ORDER:
        for j, sh in enumerate(shards):
            lo, hi = max(a, j * width), min(b, (j + 1) * width)
            if lo < hi:
                parts.append(sh[:, lo - j * width:hi - j * width])
    return jnp.concatenate(parts + [jnp.zeros((shards[0].shape[0], W_AB - 8), shards[0].dtype)], axis=1)


def _unpad_cols(wp, lo, hi):
    parts, off = [], 0
    for a, b in PAD_ORDER:
        l, h = max(a, lo), min(b, hi)
        if l < h:
            parts.append((l, wp[:, off + l - a:off + h - a]))
        off += b - a
    parts.sort(key=lambda t: t[0])
    return parts[0][1] if len(parts) == 1 else jnp.concatenate([p for _, p in parts], axis=1)


def _unpad_w_in(wp):
    return _unpad_cols(wp, 0, IN_WIDTH)


BIG = ('w_in', 'rwkv_proj', 'gdn_proj', 'w_out', 'ffn_up', 'ffn_down')
SMALL_SHARDED = ('rwkv_w2', 'rwkv_a2', 'rwkv_g2', 'gdn_conv_w', 'ffn_conv_w')


def _rows128(shape):
    n = 1
    for d in shape:
        n *= d
    return -(-n // LANES)


def _pack128(arrays):
    parts = []
    for a in arrays:
        flat = a.reshape(-1)
        rows = _rows128(a.shape)
        parts.append(jnp.pad(flat, (0, rows * LANES - flat.shape[0])).reshape(rows, LANES))
    buf = jnp.concatenate(parts, axis=0)
    return jnp.pad(buf, ((0, -buf.shape[0] % HALO), (0, 0)))


def _unpack128(buf, shapes):
    out, off = [], 0
    for s in shapes:
        rows, n = _rows128(s), 1
        for d in s:
            n *= d
        out.append(buf[off:off + rows].reshape(-1)[:n].reshape(s))
        off += rows
    return out


def _row_tile(r, c):
    best = None
    for d in range(HALO, r + 1, HALO):
        if r % d == 0 and d * c * 4 <= TILE_BYTES:
            best = d
    return best if best is not None else r


def _xy_exchange(name, bufs, scatter):
    n = len(bufs)

    def body(*refs):
        start, finish = _xy_copies(refs[:n], refs[n:2 * n], refs[2 * n:], scatter)
        start()
        finish()

    return pl.pallas_call(
        body, in_specs=[pl.BlockSpec(memory_space=pl.ANY)] * n, out_specs=[pl.BlockSpec(memory_space=pl.ANY)] * n,
        out_shape=_xy_out_shapes(bufs, scatter), scratch_shapes=_xy_sems(n, scatter), name=name)(*bufs)


def _sibling_exchange(name, bufs):
    n = len(bufs)

    def body(*refs):
        in_refs, out_refs, send_sems, recv_sems = refs[:n], refs[n:2 * n], refs[2 * n], refs[2 * n + 1]
        x, y, c = lax.axis_index("x"), lax.axis_index("y"), lax.axis_index("c")
        copies = [pltpu.make_async_remote_copy(
            src_ref=in_refs[a], dst_ref=out_refs[a], send_sem=send_sems.at[a], recv_sem=recv_sems.at[a],
            device_id=(x, y, 1 - c), device_id_type=pl.DeviceIdType.MESH) for a in range(n)]
        for cp in copies:
            cp.start()
        for cp in copies:
            cp.wait()

    return pl.pallas_call(
        body, in_specs=[pl.BlockSpec(memory_space=pl.ANY)] * n, out_specs=[pl.BlockSpec(memory_space=pl.ANY)] * n,
        out_shape=[jax.ShapeDtypeStruct(b.shape, b.dtype) for b in bufs],
        scratch_shapes=[pltpu.SemaphoreType.DMA((n,)), pltpu.SemaphoreType.DMA((n,))], name=name)(*bufs)


def _sum_slots(name, buf):
    _, R, L = buf.shape
    tr = _row_tile(R, L)

    def body(b_ref, o_ref):
        part = lambda s: b_ref[s].astype(f32)
        o_ref[...] = ((part(0) + part(1)) + part(2)) + part(3)

    return pl.pallas_call(
        body, grid=(R // tr,),
        in_specs=[pl.BlockSpec((N_POS, tr, L), lambda i: (0, i, 0))],
        out_specs=pl.BlockSpec((tr, L), lambda i: (i, 0)),
        out_shape=jax.ShapeDtypeStruct((R, L), f32), name=name,
        compiler_params=_params("parallel"))(buf)


def _adamw(name, w, ga, gb, m, v):
    R, L = w.shape
    tr = _row_tile(R, L)
    c1 = 1.0 / (1.0 - ADAM_B1 ** ADAM_STEP)
    c2 = 1.0 / (1.0 - ADAM_B2 ** ADAM_STEP)

    def body(w_ref, ga_ref, gb_ref, m_ref, v_ref, g_out, d_out, m_out, v_out):
        g = ga_ref[...] + gb_ref[...]
        m_new = ADAM_B1 * m_ref[...] + (1.0 - ADAM_B1) * g
        v_new = ADAM_B2 * v_ref[...] + (1.0 - ADAM_B2) * (g * g)
        g_out[...] = g
        m_out[...] = m_new
        v_out[...] = v_new
        d_out[...] = -ADAM_LR * ((m_new * c1) / (jnp.sqrt(v_new * c2) + ADAM_EPS) + ADAM_WD * w_ref[...])

    spec = pl.BlockSpec((tr, L), lambda i: (i, 0))
    return pl.pallas_call(
        body, grid=(R // tr,), in_specs=[spec] * 5, out_specs=[spec] * 4,
        out_shape=[jax.ShapeDtypeStruct((R, L), f32)] * 4, name=name,
        compiler_params=_params("parallel"))(w, ga, gb, m, v)


def _step(x, loss_target, P, M, V):
    shapes = {n: tuple(P[n].shape) for n in WEIGHTS}
    sh_shapes = [shapes[n] for n in SMALL_SHARDED]
    packed = SMALL_SHARDED + SMALL
    late_names = BIG[1:]

    def whole(n, g):
        return g.reshape(-1, g.shape[2]) if n in ROW_SHARDED else jnp.concatenate([g[j] for j in range(N_POS)], axis=1)

    def slabs(G, n, dtype=f32):
        r, c = shapes[n]
        full = G[n].astype(dtype)
        if full.ndim == 3:
            return full
        return full.reshape(N_POS, r, c) if n in ROW_SHARDED else full.reshape(r, N_POS, c).transpose(1, 0, 2)

    g_w_in, g_small = _xy_exchange("gather_w_in", [P['w_in'].astype(bf16), _pack128([P[n] for n in SMALL_SHARDED])],
                                   scatter=False)
    W = {n: P[n] for n in SMALL}
    W['w_in_pad'] = _pad_w_in_shards([g_w_in[j] for j in range(N_POS)])
    per_pos = [_unpack128(g_small[j], sh_shapes) for j in range(N_POS)]
    for q, n in enumerate(SMALL_SHARDED):
        W[n] = jnp.concatenate([per_pos[j][q] for j in range(N_POS)], axis=1)
    late = dict(shards=[P[n].astype(bf16) for n in late_names],
                assemble=lambda gathered: {n: whole(n, g) for n, g in zip(late_names, gathered)},
                slabs=lambda G: [slabs(G, n, bf16) for n in late_names],
                w_in_slabs=lambda G: [jnp.stack([_unpad_cols(G['w_in_pad'], j * shapes['w_in'][1], (j + 1) * shapes['w_in'][1])
                                                 for j in range(N_POS)])])

    loss_rows, dx, G = _local_step(x, loss_target, W, late)
    arrived_late = G.pop('_arrived')
    (arrived_w_in,) = G.pop('_arrived_w_in')
    G.pop('w_in_pad')

    small_slabs = jnp.stack([_pack128([slabs(G, n)[j] for n in SMALL_SHARDED] + [G[n] for n in SMALL]) for j in range(N_POS)])
    (arrived_small,) = _xy_exchange("scatter_small", [small_slabs], scatter=True)
    contributions = [arrived_w_in] + list(arrived_late) + [arrived_small]
    tags = list(BIG) + ['small']
    plane = [_sum_slots("sum_" + t, cbuf) for t, cbuf in zip(tags, contributions)]
    sibling = _sibling_exchange("sibling_grads", plane)

    out = {}
    names4 = ('grad', 'delta', 'new_m', 'new_v')
    for q, n in enumerate(BIG):
        for tag, t in zip(names4, _adamw("adamw_" + n, P[n], plane[q], sibling[q], M[n], V[n])):
            out[tag + '_' + n] = t
    small_out = _adamw("adamw_small", _pack128([P[n] for n in packed]), plane[-1], sibling[-1],
                       _pack128([M[n] for n in packed]), _pack128([V[n] for n in packed]))
    for tag, buf in zip(names4, small_out):
        for n, t in zip(packed, _unpack128(buf, [shapes[n] for n in packed])):
            out[tag + '_' + n] = t
    loss = lax.psum(loss_rows[0, 0], ("x", "y", "c"))
    return loss, dx, out


def kernel(x, norm1_g, w_in, rwkv_mu, rwkv_w0, rwkv_w2, rwkv_a0, rwkv_a2, rwkv_g2, rwkv_k_k, rwkv_k_a, rwkv_r_k, rwkv_ln_w, rwkv_ln_b, rwkv_proj, gdn_conv_w, gdn_a_log, gdn_dt_bias, gdn_norm_w, gdn_proj, w_out, norm2_g, ffn_up, ffn_conv_w, ffn_down, final_g, loss_target, m_norm1_g, m_w_in, m_rwkv_mu, m_rwkv_w0, m_rwkv_w2, m_rwkv_a0, m_rwkv_a2, m_rwkv_g2, m_rwkv_k_k, m_rwkv_k_a, m_rwkv_r_k, m_rwkv_ln_w, m_rwkv_ln_b, m_rwkv_proj, m_gdn_conv_w, m_gdn_a_log, m_gdn_dt_bias, m_gdn_norm_w, m_gdn_proj, m_w_out, m_norm2_g, m_ffn_up, m_ffn_conv_w, m_ffn_down, m_final_g, v_norm1_g, v_w_in, v_rwkv_mu, v_rwkv_w0, v_rwkv_w2, v_rwkv_a0, v_rwkv_a2, v_rwkv_g2, v_rwkv_k_k, v_rwkv_k_a, v_rwkv_r_k, v_rwkv_ln_w, v_rwkv_ln_b, v_rwkv_proj, v_gdn_conv_w, v_gdn_a_log, v_gdn_dt_bias, v_gdn_norm_w, v_gdn_proj, v_w_out, v_norm2_g, v_ffn_up, v_ffn_conv_w, v_ffn_down, v_final_g):
    weights = (norm1_g, w_in, rwkv_mu, rwkv_w0, rwkv_w2, rwkv_a0, rwkv_a2, rwkv_g2, rwkv_k_k, rwkv_k_a, rwkv_r_k, rwkv_ln_w,
               rwkv_ln_b, rwkv_proj, gdn_conv_w, gdn_a_log, gdn_dt_bias, gdn_norm_w, gdn_proj, w_out, norm2_g, ffn_up,
               ffn_conv_w, ffn_down, final_g)
    m_in = (m_norm1_g, m_w_in, m_rwkv_mu, m_rwkv_w0, m_rwkv_w2, m_rwkv_a0, m_rwkv_a2, m_rwkv_g2, m_rwkv_k_k, m_rwkv_k_a,
            m_rwkv_r_k, m_rwkv_ln_w, m_rwkv_ln_b, m_rwkv_proj, m_gdn_conv_w, m_gdn_a_log, m_gdn_dt_bias, m_gdn_norm_w,
            m_gdn_proj, m_w_out, m_norm2_g, m_ffn_up, m_ffn_conv_w, m_ffn_down, m_final_g)
    v_in = (v_norm1_g, v_w_in, v_rwkv_mu, v_rwkv_w0, v_rwkv_w2, v_rwkv_a0, v_rwkv_a2, v_rwkv_g2, v_rwkv_k_k, v_rwkv_k_a,
            v_rwkv_r_k, v_rwkv_ln_w, v_rwkv_ln_b, v_rwkv_proj, v_gdn_conv_w, v_gdn_a_log, v_gdn_dt_bias, v_gdn_norm_w,
            v_gdn_proj, v_w_out, v_norm2_g, v_ffn_up, v_ffn_conv_w, v_ffn_down, v_final_g)
    drop = lambda n, a: a if n == 'final_g' else a[0]
    P = {n: drop(n, a) for n, a in zip(WEIGHTS, weights)}
    M = {n: drop(n, a) for n, a in zip(WEIGHTS, m_in)}
    V = {n: drop(n, a) for n, a in zip(WEIGHTS, v_in)}
    loss, dx, out = _step(x[0], loss_target[0], P, M, V)
    lift = lambda n, a: a if n == 'final_g' else a[None]
    res = [loss, dx[None]]
    for tag in ('grad', 'delta', 'new_m', 'new_v'):
        res += [lift(n, out[tag + '_' + n]) for n in WEIGHTS]
    return tuple(res)
```

```python
import functools

import jax
import jax.numpy as jnp
from jax import lax
from jax.experimental import pallas as pl
from jax.experimental.pallas import tpu as pltpu

f32 = jnp.float32
bf16 = jnp.bfloat16
HI = lax.Precision.HIGHEST

D_MODEL = 1024
RWKV_HEADS, RWKV_HD, RWKV_W = 8, 64, 512
GDN_HEADS, GDN_HD, GDN_W = 4, 128, 512
FFN_H = 2816
NORM_EPS, L2_EPS, GN_EPS = 1e-6, 1e-6, 64e-5
W_AB = 256
OFF_QKV, OFF_Z, OFF_GATES, OFF_AB = 1792, 3328, 3840, 5888
W_IN_PAD = OFF_AB + W_AB
WKV_CHUNK, WKV_PER_STEP = 64, 4
GDN_CHUNK, GDN_PER_STEP = 128, 4
HALO = 8
LANES = 128
TILE_BYTES = 1 << 20
VMEM_LIMIT = 56 * 1024 * 1024

ADAM_LR, ADAM_B1, ADAM_B2, ADAM_EPS, ADAM_WD, ADAM_STEP = 0.001, 0.9, 0.999, 1e-08, 0.01, 10

ROW_SHARDED = ('w_out', 'ffn_down')
SMALL = ('norm1_g', 'rwkv_mu', 'rwkv_w0', 'rwkv_a0', 'rwkv_k_k', 'rwkv_k_a', 'rwkv_r_k', 'rwkv_ln_w', 'rwkv_ln_b',
         'gdn_a_log', 'gdn_dt_bias', 'gdn_norm_w', 'norm2_g', 'final_g')
WEIGHTS = ('norm1_g', 'w_in', 'rwkv_mu', 'rwkv_w0', 'rwkv_w2', 'rwkv_a0', 'rwkv_a2', 'rwkv_g2', 'rwkv_k_k', 'rwkv_k_a',
           'rwkv_r_k', 'rwkv_ln_w', 'rwkv_ln_b', 'rwkv_proj', 'gdn_conv_w', 'gdn_a_log', 'gdn_dt_bias', 'gdn_norm_w',
           'gdn_proj', 'w_out', 'norm2_g', 'ffn_up', 'ffn_conv_w', 'ffn_down', 'final_g')


def _params(*sem):
    return pltpu.CompilerParams(dimension_semantics=sem, vmem_limit_bytes=VMEM_LIMIT)


def _tile(n, limit):
    if n <= limit:
        return n
    best = None
    for d in range(128, limit + 1, 128):
        if n % d == 0:
            best = d
    if best is None:
        raise ValueError(f"no tile for {n} under {limit}")
    return best


MM_BLOCK_BYTES = 4 << 20


def _mm(a, b, mode, name, add=None, out_dtype=f32, side=None, col_slabs=None):
    if mode == 'nn':
        (M, K), N = a.shape, b.shape[1]
    elif mode == 'nt':
        (M, K), N = a.shape, b.shape[0]
    else:
        (K, M), N = a.shape, b.shape[1]
    tm = _tile(M, 1408)
    tk = _tile(K, min(2816, MM_BLOCK_BYTES // (tm * a.dtype.itemsize)))
    tn = _tile(N, max(128, min(MM_BLOCK_BYTES // (tk * b.dtype.itemsize), MM_BLOCK_BYTES // (tm * 4)) // 128 * 128))
    if col_slabs is not None:
        tn = N // col_slabs
    nk = K // tk
    grid = (M // tm, N // tn, nk)
    dn = {'nn': (((1,), (0,)), ((), ())), 'nt': (((1,), (1,)), ((), ())), 'tn': (((0,), (0,)), ((), ()))}[mode]
    n_add = 0 if add is None else 1
    n_side = 0 if side is None else len(side[0])

    def body(a_ref, b_ref, *rest):
        add_ref = rest[0] if add is not None else None
        side_in, rest = rest[n_add:n_add + n_side], rest[n_add + n_side:]
        o_ref, side_out, rest = rest[0], rest[1:1 + n_side], rest[1 + n_side:]
        acc_ref, rest = (rest[0], rest[1:]) if nk > 1 else (None, rest)
        ids = [pl.program_id(d) for d in range(3)]
        if side is not None:
            start, finish = _xy_copies(side_in, side_out, rest, side[1])
            pl.when((ids[0] == 0) & (ids[1] == 0) & (ids[2] == 0))(start)
        acc = lax.dot_general(a_ref[...].astype(bf16), b_ref[...].astype(bf16), dn, preferred_element_type=f32)
        if nk == 1:
            o_ref[...] = (acc + add_ref[...] if add is not None else acc).astype(out_dtype)
        else:
            k = ids[2]

            @pl.when(k == 0)
            def _():
                acc_ref[...] = acc + add_ref[...] if add is not None else acc

            @pl.when(k > 0)
            def _():
                acc_ref[...] += acc

            @pl.when(k == nk - 1)
            def _():
                o_ref[...] = acc_ref[...].astype(out_dtype)
        if side is not None:
            pl.when((ids[0] == grid[0] - 1) & (ids[1] == grid[1] - 1) & (ids[2] == nk - 1))(finish)

    a_spec = (pl.BlockSpec((tk, tm), lambda i, j, k: (k, i)) if mode == 'tn'
              else pl.BlockSpec((tm, tk), lambda i, j, k: (i, k)))
    b_spec = (pl.BlockSpec((tn, tk), lambda i, j, k: (j, k)) if mode == 'nt'
              else pl.BlockSpec((tk, tn), lambda i, j, k: (k, j)))
    o_spec = pl.BlockSpec((tm, tn), lambda i, j, k: (i, j))
    o_shape = jax.ShapeDtypeStruct((M, N), out_dtype)
    if col_slabs is not None:
        o_spec = pl.BlockSpec((None, tm, tn), lambda i, j, k: (j, i, 0))
        o_shape = jax.ShapeDtypeStruct((col_slabs, M, tn), out_dtype)
    any_spec = pl.BlockSpec(memory_space=pl.ANY)
    side_bufs = [] if side is None else list(side[0])
    ins, specs = [a, b], [a_spec, b_spec]
    if add is not None:
        ins.append(add)
        specs.append(o_spec)
    outs = pl.pallas_call(
        body, grid=grid, in_specs=specs + [any_spec] * n_side, out_specs=[o_spec] + [any_spec] * n_side,
        out_shape=[o_shape] + (_xy_out_shapes(side_bufs, side[1]) if side is not None else []),
        scratch_shapes=([pltpu.VMEM((tm, tn), f32)] if nk > 1 else []) + (_xy_sems(n_side, side[1]) if side is not None else []),
        name=name,
        compiler_params=_params(*(("arbitrary",) * 3 if side is not None else ("parallel", "parallel", "arbitrary"))))(
            *ins, *side_bufs)
    return list(outs) if side is not None else outs[0]


def _shift_down(cur, prev, s):
    if s == 0:
        return cur
    ext = jnp.concatenate([prev, cur], axis=0)
    return pltpu.roll(ext, s, 0)[HALO:]


def _shift_up(cur, nxt, s):
    if s == 0:
        return cur
    ext = jnp.concatenate([cur, nxt], axis=0)
    return pltpu.roll(ext, ext.shape[0] - s, 0)[:cur.shape[0]]


def _conv_apply(cur, prev, w_ref, shifted=None):
    taps = w_ref.shape[0]
    out = None
    for i in range(taps):
        s = taps - 1 - i
        term = (shifted[s] if shifted is not None else _shift_down(cur, prev, s)) * w_ref[pl.ds(i, 1), :]
        out = term if out is None else out + term
    return out


def _row_spec(tm, w, col=0):
    return pl.BlockSpec((tm, w), lambda i: (i, col))


def _cols(a, width, col):
    return (a, width, col)


def _row_of(r):
    return r if isinstance(r, tuple) else (r, r.shape[1], 0)


def _prev_spec(tm, w):
    return pl.BlockSpec((HALO, w), lambda i: (jnp.maximum(i * (tm // HALO) - 1, 0), 0))


def _next_spec(tm, w, T):
    return pl.BlockSpec((HALO, w), lambda i: (jnp.minimum((i + 1) * (tm // HALO), T // HALO - 1), 0))


def _full_spec(shape):
    return pl.BlockSpec(shape, lambda i: (0,) * len(shape))


def _pw_fwd(name, fn, rows, consts, out_widths, tm, conv_w=None, out_dtype=f32, strip=None):
    T = _row_of(rows[0])[0].shape[0]
    nr, nc = len(rows), len(consts)

    def body(*refs):
        i = pl.program_id(0)
        if strip is not None:
            for j in range(out_widths[0] // strip):
                sl = slice(strip * j, strip * (j + 1))
                outs = fn(*[r[:, sl] for r in refs[:nr + nc]])
                for o_ref, o in zip(refs[nr + nc:], outs):
                    o_ref[:, sl] = o.astype(out_dtype)
            return
        vals = [r[...] for r in refs[:nr]]
        p = nr
        if conv_w is not None:
            prev = jnp.where(i > 0, refs[p][...], 0.0)
            vals[0] = _conv_apply(vals[0], prev, refs[p + 1])
            p += 2
        cvals = [r[...] for r in refs[p:p + nc]]
        outs = fn(*vals, *cvals)
        for o_ref, o in zip(refs[p + nc:], outs):
            o_ref[...] = o.astype(out_dtype)

    ins = [_row_of(r)[0] for r in rows]
    specs = [_row_spec(tm, *_row_of(r)[1:]) for r in rows]
    if conv_w is not None:
        ins += [rows[0], conv_w]
        specs += [_prev_spec(tm, rows[0].shape[1]), _full_spec(conv_w.shape)]
    ins += list(consts)
    specs += [_full_spec(c.shape) for c in consts]
    outs = pl.pallas_call(
        body, grid=(T // tm,), in_specs=specs,
        out_specs=[_row_spec(tm, w) for w in out_widths],
        out_shape=[jax.ShapeDtypeStruct((T, w), out_dtype) for w in out_widths], name=name,
        compiler_params=_params("parallel"))(*ins)
    return outs


def _pw_bwd(name, fn, rows, consts, cots, tm, add_to_first=None, row_dtypes=None, strip=None):
    rows = [_row_of(r) for r in rows]
    T = rows[0][0].shape[0]
    nr, nc = len(rows), len(consts)
    flat_cots = [c for grp in cots for c in grp]
    row_dtypes = row_dtypes or [f32] * nr
    n_extra = 0 if add_to_first is None else 1
    width = rows[0][1]

    def body(*refs):
        i = pl.program_id(0)
        in_refs, cot_refs = refs[:nr + nc], refs[nr + nc:nr + nc + len(flat_cots)]
        extra_ref = refs[nr + nc + len(flat_cots)] if add_to_first is not None else None
        row_out = refs[nr + nc + len(flat_cots) + n_extra:][:nr]
        const_out = refs[nr + nc + len(flat_cots) + n_extra + nr:]

        @pl.when(i == 0)
        def _():
            for q in range(nc):
                const_out[q][...] = jnp.zeros_like(const_out[q])

        def part(sl):
            cot_vals, p = [], 0
            for grp in cots:
                acc = cot_refs[p][:, sl]
                for q in range(1, len(grp)):
                    acc = acc + cot_refs[p + q][:, sl]
                p += len(grp)
                cot_vals.append(acc)
            _, vjp = jax.vjp(fn, *[r[:, sl] for r in in_refs])
            grads = vjp(tuple(cot_vals))
            for q in range(nr):
                g = grads[q]
                if q == 0 and extra_ref is not None:
                    g = g + extra_ref[:, sl]
                row_out[q][:, sl] = g.astype(row_dtypes[q])
            for q in range(nc):
                const_out[q][:, sl] += grads[nr + q]

        if strip is None:
            part(slice(None))
        else:
            for j in range(width // strip):
                part(slice(strip * j, strip * (j + 1)))

    ins = [r[0] for r in rows] + list(consts) + flat_cots
    specs = ([_row_spec(tm, r[1], r[2]) for r in rows] + [_full_spec(c.shape) for c in consts]
             + [_row_spec(tm, c.shape[1]) for c in flat_cots])
    if add_to_first is not None:
        ins.append(add_to_first)
        specs.append(_row_spec(tm, add_to_first.shape[1]))
    out_shapes = ([jax.ShapeDtypeStruct((T, r[1]), d) for r, d in zip(rows, row_dtypes)]
                  + [jax.ShapeDtypeStruct(c.shape, f32) for c in consts])
    out_specs = [_row_spec(tm, r[1]) for r in rows] + [_full_spec(c.shape) for c in consts]
    outs = pl.pallas_call(
        body, grid=(T // tm,), in_specs=specs, out_specs=out_specs, out_shape=out_shapes, name=name,
        compiler_params=_params("arbitrary"))(*ins)
    return list(outs[:nr]), list(outs[nr:])


def _pw_conv_bwd(name, fn, rows, consts, cots, conv_w, tm, row_dtypes=None):
    T, W0 = rows[0].shape
    nr, nc = len(rows), len(consts)
    taps = conv_w.shape[0]
    nblk = T // tm
    flat_cots = [c for grp in cots for c in grp]
    row_dtypes = row_dtypes or [f32] * nr

    def body(*refs):
        i = pl.program_id(0)
        p = 0
        cur = [r[...] for r in refs[p:p + nr]]; p += nr
        nxt = [r[...] for r in refs[p:p + nr]]; p += nr
        prev = jnp.where(i > 0, refs[p][...], 0.0); p += 1
        w_ref = refs[p]; p += 1
        cvals = [r[...] for r in refs[p:p + nc]]; p += nc

        def summed(p0):
            out, q = [], p0
            for grp in cots:
                acc = refs[q][...]
                for t in range(1, len(grp)):
                    acc = acc + refs[q + t][...]
                q += len(grp)
                out.append(acc)
            return out, q

        cot_cur, p = summed(p)
        cot_nxt, p = summed(p)
        row_out, dw_ref, const_out = refs[p:p + nr], refs[p + nr], refs[p + nr + 1:]

        x_cur = cur[0]
        x_down = [_shift_down(x_cur, prev, s_) for s_ in range(taps)]
        _, vjp = jax.vjp(fn, _conv_apply(x_cur, prev, w_ref, x_down), *cur[1:], *cvals)
        grads = vjp(tuple(cot_cur))
        _, vjp_n = jax.vjp(fn, _conv_apply(nxt[0], x_cur[tm - HALO:], w_ref), *nxt[1:], *cvals)
        dc_n = jnp.where(i < nblk - 1, vjp_n(tuple(cot_nxt))[0], 0.0)
        dc = grads[0]

        @pl.when(i == 0)
        def _():
            dw_ref[...] = jnp.zeros_like(dw_ref)
            for q in range(nc):
                const_out[q][...] = jnp.zeros_like(const_out[q])

        dx = None
        for k in range(taps):
            s_ = taps - 1 - k
            term = _shift_up(dc, dc_n, s_) * w_ref[pl.ds(k, 1), :]
            dx = term if dx is None else dx + term
            dw_ref[pl.ds(k, 1), :] += jnp.sum(dc * x_down[s_], axis=0, keepdims=True)
        row_out[0][...] = dx.astype(row_dtypes[0])
        for q in range(1, nr):
            row_out[q][...] = grads[q].astype(row_dtypes[q])
        for q in range(nc):
            const_out[q][...] += grads[nr + q]

    ins = list(rows) + list(rows) + [rows[0], conv_w] + list(consts) + flat_cots + flat_cots
    specs = ([_row_spec(tm, r.shape[1]) for r in rows] + [_next_spec(tm, r.shape[1], T) for r in rows]
             + [_prev_spec(tm, W0), _full_spec(conv_w.shape)] + [_full_spec(c.shape) for c in consts]
             + [_row_spec(tm, c.shape[1]) for c in flat_cots] + [_next_spec(tm, c.shape[1], T) for c in flat_cots])
    out_shapes = ([jax.ShapeDtypeStruct(r.shape, d) for r, d in zip(rows, row_dtypes)]
                  + [jax.ShapeDtypeStruct(conv_w.shape, f32)] + [jax.ShapeDtypeStruct(c.shape, f32) for c in consts])
    out_specs = ([_row_spec(tm, r.shape[1]) for r in rows] + [_full_spec(conv_w.shape)]
                 + [_full_spec(c.shape) for c in consts])
    outs = pl.pallas_call(
        body, grid=(nblk,), in_specs=specs, out_specs=out_specs, out_shape=out_shapes, name=name,
        compiler_params=_params("arbitrary"))(*ins)
    return list(outs[:nr]), outs[nr], list(outs[nr + 1:])


def _sigmoid(x):
    return 0.5 * jnp.tanh(0.5 * x) + 0.5


def _softplus(x):
    return jnp.maximum(x, 0.0) + jnp.log(1.0 + jnp.exp(jnp.minimum(x, -x)))


def _seg_sum_impl(x, seg):
    w = x.shape[-1]
    r = lax.broadcasted_iota(jnp.int32, (w, w), 0) // seg
    c = lax.broadcasted_iota(jnp.int32, (w, w), 1) // seg
    ones = (r == c).astype(bf16)
    hi = x.astype(bf16)
    lo = (x - hi.astype(f32)).astype(bf16)
    return (jnp.dot(hi, ones, preferred_element_type=f32) + jnp.dot(lo, ones, preferred_element_type=f32))


@functools.partial(jax.custom_vjp, nondiff_argnums=(1,))
def _seg_sum(x, seg):
    return _seg_sum_impl(x, seg)


_seg_sum.defvjp(lambda x, seg: (_seg_sum_impl(x, seg), None), lambda seg, _, g: (_seg_sum_impl(g, seg),))


def _rms(x, g):
    return x * lax.rsqrt(jnp.mean(x * x, axis=-1, keepdims=True) + NORM_EPS) * g


def _rms_fn(x, g):
    return (_rms(x, g),)


def _loss_rows(x2, tgt, g):
    e = _rms(x2, g) - tgt
    return 0.5 * jnp.sum(e * e, axis=-1, keepdims=True) * (1.0 / D_MODEL)


def _rwkv_prep_fn(ps, w0, w2p, a0, a2p, g2, k_k, k_a):
    r, k, v = ps[:, 0:512], ps[:, 512:1024], ps[:, 1024:1536]
    wa, gl = ps[:, 1536:1664], ps[:, 1664:1792]
    z = w0 + jnp.dot(jnp.tanh(wa), w2p, precision=HI, preferred_element_type=f32)
    w_log = -_softplus(-z) - 0.5
    lw = -jnp.exp(w_log)
    a = _sigmoid(a0 + jnp.dot(wa, a2p, precision=HI, preferred_element_type=f32))
    g = jnp.dot(_sigmoid(gl), g2, precision=HI, preferred_element_type=f32)
    kx = k * k_k
    kk = kx * lax.rsqrt(_seg_sum(kx * kx, RWKV_HD) + L2_EPS)
    k2 = k * (1.0 + (a - 1.0) * k_a)
    return r, lw, k2, v, -kk, kk * a, g


def _rwkv_post_fn(y, r, k2, v, g, ln_w, ln_b, rk):
    mean = _seg_sum(y, RWKV_HD) * (1.0 / RWKV_HD)
    yc = y - mean
    var = _seg_sum(yc * yc, RWKV_HD) * (1.0 / RWKV_HD)
    yn = yc * lax.rsqrt(var + GN_EPS) * ln_w + ln_b
    bonus = _seg_sum(r * k2 * rk, RWKV_HD) * v
    return ((yn + bonus) * g,)


def _gdn_prep_fn(c, ab, al_p, dt_p):
    s = c * _sigmoid(c)
    q, k, v = s[:, 0:512], s[:, 512:1024], s[:, 1024:1536]
    q = q * lax.rsqrt(_seg_sum(q * q, GDN_HD) + L2_EPS) * (GDN_HD ** -0.5)
    k = k * lax.rsqrt(_seg_sum(k * k, GDN_HD) + L2_EPS)
    lane = lax.broadcasted_iota(jnp.int32, ab.shape, 1)
    gpart = -jnp.exp(al_p) * _softplus(ab + dt_p)
    gbeta = jnp.where(lane < GDN_HEADS, gpart, jnp.where(lane < 2 * GDN_HEADS, _sigmoid(ab), 0.0))
    return q, k, v, gbeta


def _gdn_post_fn(o, z, nw):
    ms = _seg_sum(o * o, GDN_HD) * (1.0 / GDN_HD)
    return (o * lax.rsqrt(ms + NORM_EPS) * nw * (z * _sigmoid(z)),)


def _mix_fn(ga, gb, ya, yb):
    return (_sigmoid(ga) * ya + _sigmoid(gb) * yb,)


STRIP = 128


def _ffn_strip_fn(cg, cu):
    return cg * _sigmoid(cg) * cu


def _strip_conv(ref, prev_ref, w_ref, sl, first, taps):
    cur = ref[:, sl]
    prev = jnp.where(first, 0.0, prev_ref[:, sl])
    down = [_shift_down(cur, prev, s) for s in range(taps)]
    conv = None
    for k in range(taps):
        term = down[taps - 1 - k] * w_ref[pl.ds(k, 1), sl]
        conv = term if conv is None else conv + term
    return cur, down, conv


def _ffn_act_fwd(h, w, tm):
    T, W2 = h.shape
    H = W2 // 2
    taps = w.shape[0]

    def body(h_ref, hp_ref, w_ref, o_ref):
        first = pl.program_id(0) == 0
        for j in range(H // STRIP):
            gs, us = slice(STRIP * j, STRIP * (j + 1)), slice(H + STRIP * j, H + STRIP * (j + 1))
            cg = _strip_conv(h_ref, hp_ref, w_ref, gs, first, taps)[2]
            cu = _strip_conv(h_ref, hp_ref, w_ref, us, first, taps)[2]
            o_ref[:, gs] = _ffn_strip_fn(cg, cu).astype(o_ref.dtype)

    return pl.pallas_call(
        body, grid=(T // tm,), in_specs=[_row_spec(tm, W2), _prev_spec(tm, W2), _full_spec(w.shape)],
        out_specs=_row_spec(tm, H), out_shape=jax.ShapeDtypeStruct((T, H), bf16), name="ffn_act",
        compiler_params=_params("parallel"))(h, h, w)


def _ffn_act_bwd(h, dact, w, tm):
    T, W2 = h.shape
    H = W2 // 2
    taps = w.shape[0]
    nblk = T // tm

    def body(h_ref, hp_ref, hn_ref, d_ref, dn_ref, w_ref, dh_ref, dw_ref):
        i = pl.program_id(0)
        first, last = i == 0, i == nblk - 1

        @pl.when(first)
        def _():
            dw_ref[...] = jnp.zeros_like(dw_ref)

        for j in range(H // STRIP):
            gs, us = slice(STRIP * j, STRIP * (j + 1)), slice(H + STRIP * j, H + STRIP * (j + 1))
            parts = {}
            for name, sl in (('g', gs), ('u', us)):
                cur, down, conv = _strip_conv(h_ref, hp_ref, w_ref, sl, first, taps)
                nxt = hn_ref[:, sl]
                conv_n = None
                for k in range(taps):
                    term = _shift_down(nxt, cur[tm - HALO:], taps - 1 - k) * w_ref[pl.ds(k, 1), sl]
                    conv_n = term if conv_n is None else conv_n + term
                parts[name] = (down, conv, conv_n)
            _, vjp = jax.vjp(_ffn_strip_fn, parts['g'][1], parts['u'][1])
            dcs = vjp(d_ref[:, gs])
            _, vjp_n = jax.vjp(_ffn_strip_fn, parts['g'][2], parts['u'][2])
            dcs_n = vjp_n(jnp.where(last, 0.0, dn_ref[:, gs]))
            for (name, sl), dc, dc_n in zip((('g', gs), ('u', us)), dcs, dcs_n):
                down = parts[name][0]
                dx = None
                for k in range(taps):
                    s_ = taps - 1 - k
                    term = _shift_up(dc, dc_n, s_) * w_ref[pl.ds(k, 1), sl]
                    dx = term if dx is None else dx + term
                    dw_ref[pl.ds(k, 1), sl] += jnp.sum(dc * down[s_], axis=0, keepdims=True)
                dh_ref[:, sl] = dx.astype(dh_ref.dtype)

    return pl.pallas_call(
        body, grid=(nblk,),
        in_specs=[_row_spec(tm, W2), _prev_spec(tm, W2), _next_spec(tm, W2, T), _row_spec(tm, H), _next_spec(tm, H, T),
                  _full_spec(w.shape)],
        out_specs=[_row_spec(tm, W2), _full_spec(w.shape)],
        out_shape=[jax.ShapeDtypeStruct((T, W2), bf16), jax.ShapeDtypeStruct(w.shape, f32)], name="ffn_act_bwd",
        compiler_params=_params("arbitrary"))(h, h, h, dact, dact, w)


N_POS = 4


def _xy_out_shapes(bufs, scatter):
    return [jax.ShapeDtypeStruct((N_POS,) + tuple(b.shape[1:] if scatter else b.shape), b.dtype) for b in bufs]


def _xy_sems(n, scatter):
    sems = [pltpu.SemaphoreType.DMA((3 * n,)), pltpu.SemaphoreType.DMA((3 * n,)), pltpu.SemaphoreType.DMA((n,))]
    return sems if scatter else sems + [pltpu.SemaphoreType.DMA((3 * n,)), pltpu.SemaphoreType.DMA((3 * n,))]


def _xy_copies(in_refs, out_refs, sems, scatter):
    n = len(in_refs)
    send_sems, recv_sems, local_sems = sems[:3]

    def place():
        x, y, c = lax.axis_index("x"), lax.axis_index("y"), lax.axis_index("c")
        return x, y, c, 2 * x + y, [(1 - x, y), (x, 1 - y), (1 - x, 1 - y)]

    def half(ref, a, which):
        rows = in_refs[a].shape[0] // 2
        return ref.at[pl.ds(pl.multiple_of(which * rows, HALO), rows)]

    def ici(a, k, src, dst, peer, c):
        return pltpu.make_async_remote_copy(
            src_ref=src, dst_ref=dst, send_sem=send_sems.at[3 * a + k], recv_sem=recv_sems.at[3 * a + k],
            device_id=(peer[0], peer[1], c), device_id_type=pl.DeviceIdType.MESH)

    def outgoing():
        x, y, c, me, peers = place()
        own = [pltpu.make_async_copy(in_refs[a].at[me] if scatter else in_refs[a], out_refs[a].at[me], local_sems.at[a])
               for a in range(n)]
        if scatter:
            sends = [ici(a, k, in_refs[a].at[2 * p[0] + p[1]], out_refs[a].at[me], p, c)
                     for a in range(n) for k, p in enumerate(peers)]
        else:
            sends = [ici(a, k, half(in_refs[a], a, c), half(out_refs[a].at[me], a, c), p, c)
                     for a in range(n) for k, p in enumerate(peers)]
        return own, sends

    def arrivals():
        x, y, c, me, peers = place()
        if scatter:
            return [ici(a, k, in_refs[a].at[me], out_refs[a].at[2 * p[0] + p[1]], p, c)
                    for a in range(n) for k, p in enumerate(peers)]
        return [ici(a, k, half(in_refs[a], a, c), half(out_refs[a].at[2 * p[0] + p[1]], a, c), p, c)
                for a in range(n) for k, p in enumerate(peers)]

    def to_sibling(mine):
        x, y, c, me, peers = place()
        which = c if mine else 1 - c
        return [pltpu.make_async_remote_copy(
            src_ref=half(out_refs[a].at[2 * p[0] + p[1]], a, which), dst_ref=half(out_refs[a].at[2 * p[0] + p[1]], a, which),
            send_sem=sems[3].at[3 * a + k], recv_sem=sems[4].at[3 * a + k],
            device_id=(x, y, 1 - c), device_id_type=pl.DeviceIdType.MESH) for a in range(n) for k, p in enumerate(peers)]

    def start():
        own, sends = outgoing()
        for cp in own + sends:
            cp.start()

    def finish():
        if scatter:
            for cp in arrivals():
                cp.wait_recv()
        else:
            passed = to_sibling(True)
            for cp, fwd in zip(arrivals(), passed):
                cp.wait_recv()
                fwd.start()
            for cp in to_sibling(False):
                cp.wait_recv()
            for fwd in passed:
                fwd.wait_send()
        own, sends = outgoing()
        for cp in sends:
            cp.wait_send()
        for cp in own:
            cp.wait()

    return start, finish


_NN, _NT, _TN = 'hcs,hsd->hcd', 'hcd,hsd->hcs', 'hcd,hce->hde'


def _lo(spec, a, b):
    return jnp.einsum(spec, a.astype(bf16), b.astype(bf16), preferred_element_type=f32)


@jax.custom_vjp
def _bmm(a, b):
    return _lo(_NN, a, b)


_bmm.defvjp(lambda a, b: (_lo(_NN, a, b), (a, b)), lambda ab, g: (_lo(_NT, g, ab[1]), _lo(_TN, ab[0], g)))


@jax.custom_vjp
def _bmm_nt(a, b):
    return _lo(_NT, a, b)


_bmm_nt.defvjp(lambda a, b: (_lo(_NT, a, b), (a, b)), lambda ab, g: (_lo(_NN, g, ab[1]), _lo(_TN, g, ab[0])))


@jax.custom_vjp
def _bmm_tn(a, b):
    return _lo(_TN, a, b)


_bmm_tn.defvjp(lambda a, b: (_lo(_TN, a, b), (a, b)), lambda ab, g: (_lo(_NT, ab[1], g), _lo(_NN, ab[0], g)))


def _masks(H, C):
    row = lax.broadcasted_iota(jnp.int32, (H, C, C), 1)
    col = lax.broadcasted_iota(jnp.int32, (H, C, C), 2)
    return row, col


def _tri_inv_impl(L):
    H, C, _ = L.shape
    row, col = _masks(H, C)
    eye = (row == col).astype(f32)
    base = 16
    same = (row // base) == (col // base)
    Ld = jnp.where(same, L, 0.0)
    X = -Ld
    inv = eye + X
    for _ in range(3):
        X = _bmm(X, X)
        inv = _bmm(inv, eye + X)
    if C == base:
        return inv
    N = _bmm(inv, L - Ld)
    out = eye - N
    levels = C // base
    P = N
    span = 2
    while span < levels:
        P = _bmm(P, P)
        out = _bmm(out, eye + P)
        span *= 2
    return _bmm(out, inv)


@jax.custom_vjp
def _tri_inv(L):
    return _tri_inv_impl(L)


def _tri_inv_fwd(L):
    T = _tri_inv_impl(L)
    return T, T


def _tri_inv_bwd(T, dT):
    return (-_bmm_nt(_bmm_tn(T, dT), T),)


_tri_inv.defvjp(_tri_inv_fwd, _tri_inv_bwd)


def _cumsum_impl(x, reverse):
    C = x.shape[1]
    row = lax.broadcasted_iota(jnp.int32, x.shape, 1)
    s = 1
    while s < C:
        if reverse:
            x = x + jnp.where(row < C - s, pltpu.roll(x, C - s, 1), 0.0)
        else:
            x = x + jnp.where(row >= s, pltpu.roll(x, s, 1), 0.0)
        s *= 2
    return x


@jax.custom_vjp
def _cumsum(x):
    return _cumsum_impl(x, False)


_cumsum.defvjp(lambda x: (_cumsum_impl(x, False), None), lambda _, g: (_cumsum_impl(g, True),))


def _wkv_prep(r, lw, k, v, a, b):
    lane = lax.broadcasted_iota(jnp.int32, (r.shape[0], 128), 1)
    low = lane < RWKV_HD

    def heads(t):
        out = []
        for p in range(RWKV_HEADS // 2):
            pair = t[:, 128 * p:128 * (p + 1)]
            out += [jnp.where(low, pair, 0.0), jnp.where(low, 0.0, pair)]
        return jnp.concatenate([t[None] for t in out], axis=0)

    r, lw, k, v, a, b = [heads(t) for t in (r, lw, k, v, a, b)]
    H, C, D = r.shape
    row, col = _masks(H, C)
    incl, strict = row >= col, row > col
    cw = _cumsum(lw)
    cwp = cw - lw
    cwl = jnp.sum(lw, axis=1, keepdims=True)
    en = jnp.exp(-cw)
    at, rt, bt, kt = a * jnp.exp(cwp), r * jnp.exp(cw), b * en, k * en
    Tm = _tri_inv(-jnp.where(strict, _bmm_nt(at, bt), 0.0))
    ar = jnp.concatenate([at, rt], axis=1)
    gram = _bmm_nt(ar, jnp.concatenate([bt, kt], axis=1))
    row2 = lax.broadcasted_iota(jnp.int32, (H, 2 * C, 2 * C), 1)
    col2 = lax.broadcasted_iota(jnp.int32, (H, 2 * C, 2 * C), 2) % C
    gram = jnp.where(((row2 < C) & (row2 > col2)) | ((row2 >= C) & (row2 - C >= col2)), gram, 0.0)
    a_bk, r_bk = gram[:, :C], gram[:, C:]
    lak_v = _bmm(a_bk, jnp.concatenate([jnp.zeros_like(v), v], axis=1))
    ed = jnp.exp(cwl - cw)
    zdec = jnp.swapaxes(jnp.broadcast_to(jnp.exp(cwl), (H, D, D)), 1, 2)
    return ar, Tm, lak_v, r_bk, jnp.concatenate([b * ed, k * ed], axis=1), zdec, v


def _wkv_step(Z, ar, Tm, lak_v, r_bk, bk_d, zdec, v):
    C = Tm.shape[1]
    ar_z = _bmm(ar, Z)
    uv = jnp.concatenate([_bmm(Tm, ar_z[:, :C] + lak_v), v], axis=1)
    y = ar_z[:, C:] + _bmm(r_bk, uv)
    Z1 = Z * zdec + _bmm_tn(bk_d, uv)
    return jnp.concatenate([y[2 * p] + y[2 * p + 1] for p in range(RWKV_HEADS // 2)], axis=1), Z1


def _gdn_prep(q, k, v, gbeta):
    heads = lambda t: jnp.concatenate([t[None, :, GDN_HD * h:GDN_HD * (h + 1)] for h in range(GDN_HEADS)], axis=0)
    src = lax.broadcasted_iota(jnp.int32, (W_AB, 2 * GDN_W), 0)
    dst = lax.broadcasted_iota(jnp.int32, (W_AB, 2 * GDN_W), 1) // GDN_HD
    spread = jnp.dot(gbeta, (src == dst).astype(f32), precision=HI, preferred_element_type=f32)
    q, k, v, g, beta = heads(q), heads(k), heads(v), heads(spread[:, :GDN_W]), heads(spread[:, GDN_W:])
    H, C, D = q.shape
    row, col = _masks(H, C)
    incl, strict = row >= col, row > col
    gc = _cumsum(g)
    diff = gc - jnp.swapaxes(gc, 1, 2)
    decay = jnp.where(incl, jnp.exp(jnp.where(incl, diff, 0.0)), 0.0)
    gl = jnp.sum(g, axis=1, keepdims=True)
    kb, vb = k * beta, v * beta
    gram = _bmm_nt(jnp.concatenate([kb, q], axis=1), k)
    L = jnp.where(strict, gram[:, :C] * decay, 0.0)
    attn = jnp.where(incl, gram[:, C:] * decay, 0.0)
    egc = jnp.exp(gc)
    t_vk = _bmm(_tri_inv(L), jnp.concatenate([vb, kb * egc], axis=2))
    return t_vk[:, :, :D], jnp.concatenate([t_vk[:, :, D:], q * egc], axis=1), attn, k * jnp.exp(gl - gc), jnp.exp(gl)


def _gdn_step(S, u, wq, attn, ke, sdec):
    C = u.shape[1]
    wq_s = _bmm(wq, S)
    v_new = u - wq_s[:, :C]
    o = wq_s[:, C:] + _bmm(attn, v_new)
    S1 = S * sdec + _bmm_tn(ke, v_new)
    return jnp.concatenate([o[h] for h in range(GDN_HEADS)], axis=1), S1


def _scan_fwd(name, fns, ins, C, H, dh, w_out, per_step, side=None):
    prep, step = fns
    T = ins[0].shape[0]
    n_in = len(ins)
    blk = C * per_step
    nblk = T // blk
    n_side = 0 if side is None else len(side[0])

    def body(*refs):
        in_refs, refs = refs[:n_in], refs[n_in:]
        side_in, refs = refs[:n_side], refs[n_side:]
        y_ref, zs_ref, refs = refs[0], refs[1], refs[2:]
        side_out, refs = refs[:n_side], refs[n_side:]
        z_scr = refs[0]
        if side is not None:
            start, finish = _xy_copies(side_in, side_out, refs[1:], side[1])
            pl.when(pl.program_id(0) == 0)(start)

        @pl.when(pl.program_id(0) == 0)
        def _():
            z_scr[...] = jnp.zeros_like(z_scr)

        rows = [slice(C * j, C * (j + 1)) for j in range(per_step)]
        prepped = [prep(*[r[rw, :] for r in in_refs]) for rw in rows]
        Z = z_scr[...]
        for j, rw in enumerate(rows):
            zs_ref[j] = Z
            y, Z = step(Z, *prepped[j])
            y_ref[rw, :] = y
        z_scr[...] = Z
        if side is not None:
            pl.when(pl.program_id(0) == nblk - 1)(finish)

    side_bufs = [] if side is None else list(side[0])
    any_spec = pl.BlockSpec(memory_space=pl.ANY)
    return pl.pallas_call(
        body, grid=(nblk,),
        in_specs=[pl.BlockSpec((blk, a.shape[1]), lambda i: (i, 0)) for a in ins] + [any_spec] * n_side,
        out_specs=[pl.BlockSpec((blk, w_out), lambda i: (i, 0)), pl.BlockSpec((per_step, H, dh, dh), lambda i: (i, 0, 0, 0))]
        + [any_spec] * n_side,
        out_shape=[jax.ShapeDtypeStruct((T, w_out), f32), jax.ShapeDtypeStruct((T // C, H, dh, dh), f32)]
        + (_xy_out_shapes(side_bufs, side[1]) if side is not None else []),
        scratch_shapes=[pltpu.VMEM((H, dh, dh), f32)] + (_xy_sems(n_side, side[1]) if side is not None else []), name=name,
        compiler_params=_params("arbitrary"))(*ins, *side_bufs)


def _scan_bwd(name, fns, ins, dy, zs, C, per_step, side=None):
    prep, step = fns
    T = ins[0].shape[0]
    _, H, dh, _ = zs.shape
    n_in = len(ins)
    blk = C * per_step
    nblk = T // blk
    n_side = 0 if side is None else len(side[0])

    def body(*refs):
        in_refs, dy_ref, zs_ref, refs = refs[:n_in], refs[n_in], refs[n_in + 1], refs[n_in + 2:]
        side_in, refs = refs[:n_side], refs[n_side:]
        out_refs, refs = refs[:n_in], refs[n_in:]
        side_out, refs = refs[:n_side], refs[n_side:]
        dz_scr = refs[0]
        if side is not None:
            start, finish = _xy_copies(side_in, side_out, refs[1:], side[1])
            pl.when(pl.program_id(0) == 0)(start)

        @pl.when(pl.program_id(0) == 0)
        def _():
            dz_scr[...] = jnp.zeros_like(dz_scr)

        rows = [slice(C * j, C * (j + 1)) for j in range(per_step)]
        prepped = [jax.vjp(prep, *[r[rw, :] for r in in_refs]) for rw in rows]
        d_prepped = [None] * per_step
        dZ = dz_scr[...]
        for j in reversed(range(per_step)):
            _, pull = jax.vjp(step, zs_ref[j], *prepped[j][0])
            dZ, *d_prepped[j] = pull((dy_ref[rows[j], :], dZ))
        dz_scr[...] = dZ
        for j, rw in enumerate(rows):
            for o_ref, gval in zip(out_refs, prepped[j][1](tuple(d_prepped[j]))):
                o_ref[rw, :] = gval
        if side is not None:
            pl.when(pl.program_id(0) == nblk - 1)(finish)

    side_bufs = [] if side is None else list(side[0])
    any_spec = pl.BlockSpec(memory_space=pl.ANY)
    rev = lambda i: (nblk - 1 - i, 0)
    return pl.pallas_call(
        body, grid=(nblk,),
        in_specs=[pl.BlockSpec((blk, a.shape[1]), rev) for a in ins]
        + [pl.BlockSpec((blk, dy.shape[1]), rev), pl.BlockSpec((per_step, H, dh, dh), lambda i: (nblk - 1 - i, 0, 0, 0))]
        + [any_spec] * n_side,
        out_specs=[pl.BlockSpec((blk, a.shape[1]), rev) for a in ins] + [any_spec] * n_side,
        out_shape=[jax.ShapeDtypeStruct(a.shape, f32) for a in ins]
        + (_xy_out_shapes(side_bufs, side[1]) if side is not None else []),
        scratch_shapes=[pltpu.VMEM((H, dh, dh), f32)] + (_xy_sems(n_side, side[1]) if side is not None else []), name=name,
        compiler_params=_params("arbitrary"))(*ins, dy, zs, *side_bufs)


def _loss_call(x2, tgt, g, tm):
    T, W = x2.shape

    def body(x_ref, t_ref, g_ref, dx_ref, dg_ref, l_ref):
        i = pl.program_id(0)
        tv = t_ref[...]
        l, vjp = jax.vjp(lambda xv, gv: _loss_rows(xv, tv, gv), x_ref[...], g_ref[...])
        dx, dg = vjp(jnp.ones_like(l))
        dx_ref[...] = dx
        tot = jnp.zeros((1, 128), f32) + jnp.sum(l)

        @pl.when(i == 0)
        def _():
            dg_ref[...] = dg
            l_ref[...] = tot

        @pl.when(i > 0)
        def _():
            dg_ref[...] += dg
            l_ref[...] += tot

    return pl.pallas_call(
        body, grid=(T // tm,),
        in_specs=[_row_spec(tm, W), _row_spec(tm, W), _full_spec(g.shape)],
        out_specs=[_row_spec(tm, W), _full_spec(g.shape), _full_spec((1, 128))],
        out_shape=[jax.ShapeDtypeStruct((T, W), f32), jax.ShapeDtypeStruct(g.shape, f32),
                   jax.ShapeDtypeStruct((1, 128), f32)], name="loss_head",
        compiler_params=_params("arbitrary"))(x2, tgt, g)


def _local_step(x, tgt, W, late=None):
    row = lambda a: a.reshape(1, -1)
    wp = W['w_in_pad']
    w_rwkv, w_qkv, w_z = wp[:, :OFF_QKV], wp[:, OFF_QKV:OFF_Z], wp[:, OFF_Z:OFF_GATES]
    w_gates, w_ab = wp[:, OFF_GATES:OFF_AB], wp[:, OFF_AB:]
    mu = row(W['rwkv_mu'])
    mixw = jnp.concatenate([mu, 1.0 - mu], axis=0)
    zpad = jnp.zeros((64, RWKV_W), f32)
    w2p = jnp.concatenate([W['rwkv_w2'], zpad], axis=0)
    a2p = jnp.concatenate([zpad, W['rwkv_a2']], axis=0)
    rw_consts = [row(W['rwkv_w0']), w2p, row(W['rwkv_a0']), a2p, W['rwkv_g2'], row(W['rwkv_k_k']), row(W['rwkv_k_a'])]
    post_consts = [row(W['rwkv_ln_w']), row(W['rwkv_ln_b']), row(W['rwkv_r_k'])]
    pad4 = lambda a: jnp.pad(row(a), ((0, 0), (0, W_AB - GDN_HEADS)))
    gd_consts = [pad4(W['gdn_a_log']), pad4(W['gdn_dt_bias'])]
    nw_t = jnp.tile(row(W['gdn_norm_w']), (1, GDN_HEADS))
    g1, g2n, gf = row(W['norm1_g']), row(W['norm2_g']), row(W['final_g'])

    (u,) = _pw_fwd("norm1", _rms_fn, [x], [g1], [D_MODEL], 256, out_dtype=bf16)
    p_rwkv = _mm(u, w_rwkv, 'nn', "in_rwkv")
    qkv_raw = _mm(u, w_qkv, 'nn', "in_qkv")
    z = _mm(u, w_z, 'nn', "in_z")
    gates = _mm(u, w_gates, 'nn', "in_gates")
    ab = _mm(u, w_ab, 'nn', "in_ab")

    r, lw, k2, v, a_, b_, g = _pw_fwd("rwkv_prep", _rwkv_prep_fn, [p_rwkv], rw_consts, [RWKV_W] * 7, 256, conv_w=mixw)
    wkv_in = [r, lw, k2, v, a_, b_]
    y, zs_wkv, *gathered = _scan_fwd("wkv_fwd", (_wkv_prep, _wkv_step), wkv_in, WKV_CHUNK, RWKV_HEADS, 2 * RWKV_HD, RWKV_W, WKV_PER_STEP,
                                     side=None if late is None else (late['shards'], False))
    if late is not None:
        W = dict(W, **late['assemble'](gathered))
    (ya_in,) = _pw_fwd("rwkv_post", _rwkv_post_fn, [y, r, k2, v, g], post_consts, [RWKV_W], 256, out_dtype=bf16, strip=128)
    ya = _mm(ya_in, W['rwkv_proj'], 'nn', "rwkv_proj")

    gq, gk, gv, gbeta = _pw_fwd("gdn_prep", _gdn_prep_fn, [qkv_raw, ab], gd_consts, [GDN_W] * 3 + [W_AB], 256,
                                conv_w=W['gdn_conv_w'])
    gdn_in = [gq, gk, gv, gbeta]
    o, zs_gdn = _scan_fwd("gdn_fwd", (_gdn_prep, _gdn_step), gdn_in, GDN_CHUNK, GDN_HEADS, GDN_HD, GDN_W, GDN_PER_STEP)
    (yb_in,) = _pw_fwd("gdn_post", _gdn_post_fn, [o, z], [nw_t], [GDN_W], 256, out_dtype=bf16, strip=128)
    yb = _mm(yb_in, W['gdn_proj'], 'nn', "gdn_proj")

    ga, gb = _cols(gates, D_MODEL, 0), _cols(gates, D_MODEL, 1)
    (mixed,) = _pw_fwd("mix", _mix_fn, [ga, gb, ya, yb], [], [D_MODEL], 256, out_dtype=bf16, strip=256)
    x1 = _mm(mixed, W['w_out'], 'nn', "w_out", add=x)
    (u2,) = _pw_fwd("norm2", _rms_fn, [x1], [g2n], [D_MODEL], 256, out_dtype=bf16)
    h = _mm(u2, W['ffn_up'], 'nn', "ffn_up")
    act = _ffn_act_fwd(h, W['ffn_conv_w'], 256)
    x2 = _mm(act, W['ffn_down'], 'nn', "ffn_down", add=x1)

    G = {}
    slab_out = None if late is None else N_POS
    dx2, dgf, loss = _loss_call(x2, tgt, gf, 256)
    G['final_g'] = dgf
    dact = _mm(dx2, W['ffn_down'], 'nt', "d_act")
    G['ffn_down'] = _mm(act, dx2, 'tn', "g_ffn_down", out_dtype=bf16)
    dh, G['ffn_conv_w'] = _ffn_act_bwd(h, dact, W['ffn_conv_w'], 128)
    du2 = _mm(dh, W['ffn_up'], 'nt', "d_u2")
    G['ffn_up'] = _mm(u2, dh, 'tn', "g_ffn_up", out_dtype=bf16, col_slabs=slab_out)
    (dx1,), (G['norm2_g'],) = _pw_bwd("norm2_bwd", _rms_fn, [x1], [g2n], [(du2,)], 256, add_to_first=dx2)
    dmixed = _mm(dx1, W['w_out'], 'nt', "d_mixed")
    G['w_out'] = _mm(mixed, dx1, 'tn', "g_w_out", out_dtype=bf16)
    (dga, dgb, dya, dyb), _ = _pw_bwd("mix_bwd", _mix_fn, [ga, gb, ya, yb], [], [(dmixed,)], 256, row_dtypes=[bf16] * 4,
                                      strip=256)
    dya_in = _mm(dya, W['rwkv_proj'], 'nt', "d_ya_in")
    G['rwkv_proj'] = _mm(ya_in, dya, 'tn', "g_rwkv_proj", out_dtype=bf16, col_slabs=slab_out)
    dyb_in = _mm(dyb, W['gdn_proj'], 'nt', "d_yb_in")
    G['gdn_proj'] = _mm(yb_in, dyb, 'tn', "g_gdn_proj", out_dtype=bf16, col_slabs=slab_out)

    (do, dz), (dnw_t,) = _pw_bwd("gdn_post_bwd", _gdn_post_fn, [o, z], [nw_t], [(dyb_in,)], 256, row_dtypes=[f32, bf16],
                                 strip=128)
    G['gdn_norm_w'] = dnw_t.reshape(GDN_HEADS, GDN_HD).sum(axis=0)
    dgq, dgk, dgv, dgbeta = _scan_bwd("gdn_bwd", (_gdn_prep, _gdn_step), gdn_in, do, zs_gdn, GDN_CHUNK, GDN_PER_STEP)
    (dqkv_raw, dab), G['gdn_conv_w'], (dal_p, ddt_p) = _pw_conv_bwd(
        "gdn_prep_bwd", _gdn_prep_fn, [qkv_raw, ab], gd_consts, [(dgq,), (dgk,), (dgv,), (dgbeta,)], W['gdn_conv_w'], 256,
        row_dtypes=[bf16, bf16])
    G['gdn_a_log'], G['gdn_dt_bias'] = dal_p[0, :GDN_HEADS], ddt_p[0, :GDN_HEADS]

    (dy, dr1, dk21, dv1, dg_), (G['rwkv_ln_w'], G['rwkv_ln_b'], G['rwkv_r_k']) = _pw_bwd(
        "rwkv_post_bwd", _rwkv_post_fn, [y, r, k2, v, g], post_consts, [(dya_in,)], 256, strip=128)
    dr2, dlw, dk22, dv2, da_, db_, *G['_arrived'] = _scan_bwd(
        "wkv_bwd", (_wkv_prep, _wkv_step), wkv_in, dy, zs_wkv, WKV_CHUNK, WKV_PER_STEP, side=None if late is None else (late['slabs'](G), True))
    (dp_rwkv,), dmixw, rw_grads = _pw_conv_bwd(
        "rwkv_prep_bwd", _rwkv_prep_fn, [p_rwkv], rw_consts,
        [(dr1, dr2), (dlw,), (dk21, dk22), (dv1, dv2), (da_,), (db_,), (dg_,)], mixw, 256, row_dtypes=[bf16])
    G['rwkv_w0'], dw2p, G['rwkv_a0'], da2p, G['rwkv_g2'], G['rwkv_k_k'], G['rwkv_k_a'] = rw_grads
    G['rwkv_w2'], G['rwkv_a2'] = dw2p[:64], da2p[64:]
    G['rwkv_mu'] = dmixw[0] - dmixw[1]

    dp = jnp.concatenate([dp_rwkv, dqkv_raw, dz, dga, dgb, dab], axis=1)
    G['w_in_pad'] = _mm(u, dp, 'tn', "g_w_in", out_dtype=bf16)
    if late is None:
        du = _mm(dp, wp, 'nt', "d_u")
    else:
        du, *G['_arrived_w_in'] = _mm(dp, wp, 'nt', "d_u", side=(late['w_in_slabs'](G), True))
    (dx,), (G['norm1_g'],) = _pw_bwd("norm1_bwd", _rms_fn, [x], [g1], [(du,)], 256, add_to_first=dx1)
    return loss, dx, G


IN_WIDTH = OFF_AB + 8
PAD_ORDER = ((0, OFF_GATES), (OFF_GATES + 8, IN_WIDTH), (OFF_GATES, OFF_GATES + 8))


def _pad_w_in(w):
    return jnp.concatenate([w[:, a:b] for a, b in PAD_ORDER] + [jnp.zeros((w.shape[0], W_AB - 8), w.dtype)], axis=1)


def _pad_w_in_shards(shards):
    width = shards[0].shape[1]
    parts = []
    for a, b in PAD_ORDER:
        for j, sh in enumerate(shards):
            lo, hi = max(a, j * width), min(b, (j + 1) * width)
            if lo < hi:
                parts.append(sh[:, lo - j * width:hi - j * width])
    return jnp.concatenate(parts + [jnp.zeros((shards[0].shape[0], W_AB - 8), shards[0].dtype)], axis=1)


def _unpad_cols(wp, lo, hi):
    parts, off = [], 0
    for a, b in PAD_ORDER:
        l, h = max(a, lo), min(b, hi)
        if l < h:
            parts.append((l, wp[:, off + l - a:off + h - a]))
        off += b - a
    parts.sort(key=lambda t: t[0])
    return parts[0][1] if len(parts) == 1 else jnp.concatenate([p for _, p in parts], axis=1)


def _unpad_w_in(wp):
    return _unpad_cols(wp, 0, IN_WIDTH)


BIG = ('w_in', 'rwkv_proj', 'gdn_proj', 'w_out', 'ffn_up', 'ffn_down')
SMALL_SHARDED = ('rwkv_w2', 'rwkv_a2', 'rwkv_g2', 'gdn_conv_w', 'ffn_conv_w')


def _rows128(shape):
    n = 1
    for d in shape:
        n *= d
    return -(-n // LANES)


def _pack128(arrays):
    parts = []
    for a in arrays:
        flat = a.reshape(-1)
        rows = _rows128(a.shape)
        parts.append(jnp.pad(flat, (0, rows * LANES - flat.shape[0])).reshape(rows, LANES))
    buf = jnp.concatenate(parts, axis=0)
    return jnp.pad(buf, ((0, -buf.shape[0] % HALO), (0, 0)))


def _unpack128(buf, shapes):
    out, off = [], 0
    for s in shapes:
        rows, n = _rows128(s), 1
        for d in s:
            n *= d
        out.append(buf[off:off + rows].reshape(-1)[:n].reshape(s))
        off += rows
    return out


def _row_tile(r, c):
    best = None
    for d in range(HALO, r + 1, HALO):
        if r % d == 0 and d * c * 4 <= TILE_BYTES:
            best = d
    return best if best is not None else r


def _xy_exchange(name, bufs, scatter):
    n = len(bufs)

    def body(*refs):
        start, finish = _xy_copies(refs[:n], refs[n:2 * n], refs[2 * n:], scatter)
        start()
        finish()

    return pl.pallas_call(
        body, in_specs=[pl.BlockSpec(memory_space=pl.ANY)] * n, out_specs=[pl.BlockSpec(memory_space=pl.ANY)] * n,
        out_shape=_xy_out_shapes(bufs, scatter), scratch_shapes=_xy_sems(n, scatter), name=name)(*bufs)


def _sibling_exchange(name, bufs):
    n = len(bufs)

    def body(*refs):
        in_refs, out_refs, send_sems, recv_sems = refs[:n], refs[n:2 * n], refs[2 * n], refs[2 * n + 1]
        x, y, c = lax.axis_index("x"), lax.axis_index("y"), lax.axis_index("c")
        copies = [pltpu.make_async_remote_copy(
            src_ref=in_refs[a], dst_ref=out_refs[a], send_sem=send_sems.at[a], recv_sem=recv_sems.at[a],
            device_id=(x, y, 1 - c), device_id_type=pl.DeviceIdType.MESH) for a in range(n)]
        for cp in copies:
            cp.start()
        for cp in copies:
            cp.wait()

    return pl.pallas_call(
        body, in_specs=[pl.BlockSpec(memory_space=pl.ANY)] * n, out_specs=[pl.BlockSpec(memory_space=pl.ANY)] * n,
        out_shape=[jax.ShapeDtypeStruct(b.shape, b.dtype) for b in bufs],
        scratch_shapes=[pltpu.SemaphoreType.DMA((n,)), pltpu.SemaphoreType.DMA((n,))], name=name)(*bufs)


def _sum_slots(name, buf):
    _, R, L = buf.shape
    tr = _row_tile(R, L)

    def body(b_ref, o_ref):
        part = lambda s: b_ref[s].astype(f32)
        o_ref[...] = ((part(0) + part(1)) + part(2)) + part(3)

    return pl.pallas_call(
        body, grid=(R // tr,),
        in_specs=[pl.BlockSpec((N_POS, tr, L), lambda i: (0, i, 0))],
        out_specs=pl.BlockSpec((tr, L), lambda i: (i, 0)),
        out_shape=jax.ShapeDtypeStruct((R, L), f32), name=name,
        compiler_params=_params("parallel"))(buf)


def _adamw(name, w, ga, gb, m, v):
    R, L = w.shape
    tr = _row_tile(R, L)
    c1 = 1.0 / (1.0 - ADAM_B1 ** ADAM_STEP)
    c2 = 1.0 / (1.0 - ADAM_B2 ** ADAM_STEP)

    def body(w_ref, ga_ref, gb_ref, m_ref, v_ref, g_out, d_out, m_out, v_out):
        g = ga_ref[...] + gb_ref[...]
        m_new = ADAM_B1 * m_ref[...] + (1.0 - ADAM_B1) * g
        v_new = ADAM_B2 * v_ref[...] + (1.0 - ADAM_B2) * (g * g)
        g_out[...] = g
        m_out[...] = m_new
        v_out[...] = v_new
        d_out[...] = -ADAM_LR * ((m_new * c1) / (jnp.sqrt(v_new * c2) + ADAM_EPS) + ADAM_WD * w_ref[...])

    spec = pl.BlockSpec((tr, L), lambda i: (i, 0))
    return pl.pallas_call(
        body, grid=(R // tr,), in_specs=[spec] * 5, out_specs=[spec] * 4,
        out_shape=[jax.ShapeDtypeStruct((R, L), f32)] * 4, name=name,
        compiler_params=_params("parallel"))(w, ga, gb, m, v)


def _step(x, loss_target, P, M, V):
    shapes = {n: tuple(P[n].shape) for n in WEIGHTS}
    sh_shapes = [shapes[n] for n in SMALL_SHARDED]
    packed = SMALL_SHARDED + SMALL
    late_names = BIG[1:]

    def whole(n, g):
        return g.reshape(-1, g.shape[2]) if n in ROW_SHARDED else jnp.concatenate([g[j] for j in range(N_POS)], axis=1)

    def slabs(G, n, dtype=f32):
        r, c = shapes[n]
        full = G[n].astype(dtype)
        if full.ndim == 3:
            return full
        return full.reshape(N_POS, r, c) if n in ROW_SHARDED else full.reshape(r, N_POS, c).transpose(1, 0, 2)

    g_w_in, g_small = _xy_exchange("gather_w_in", [P['w_in'].astype(bf16), _pack128([P[n] for n in SMALL_SHARDED])],
                                   scatter=False)
    W = {n: P[n] for n in SMALL}
    W['w_in_pad'] = _pad_w_in_shards([g_w_in[j] for j in range(N_POS)])
    per_pos = [_unpack128(g_small[j], sh_shapes) for j in range(N_POS)]
    for q, n in enumerate(SMALL_SHARDED):
        W[n] = jnp.concatenate([per_pos[j][q] for j in range(N_POS)], axis=1)
    late = dict(shards=[P[n].astype(bf16) for n in late_names],
                assemble=lambda gathered: {n: whole(n, g) for n, g in zip(late_names, gathered)},
                slabs=lambda G: [slabs(G, n, bf16) for n in late_names],
                w_in_slabs=lambda G: [jnp.stack([_unpad_cols(G['w_in_pad'], j * shapes['w_in'][1], (j + 1) * shapes['w_in'][1])
                                                 for j in range(N_POS)])])

    loss_rows, dx, G = _local_step(x, loss_target, W, late)
    arrived_late = G.pop('_arrived')
    (arrived_w_in,) = G.pop('_arrived_w_in')
    G.pop('w_in_pad')

    small_slabs = jnp.stack([_pack128([slabs(G, n)[j] for n in SMALL_SHARDED] + [G[n] for n in SMALL]) for j in range(N_POS)])
    (arrived_small,) = _xy_exchange("scatter_small", [small_slabs], scatter=True)
    contributions = [arrived_w_in] + list(arrived_late) + [arrived_small]
    tags = list(BIG) + ['small']
    plane = [_sum_slots("sum_" + t, cbuf) for t, cbuf in zip(tags, contributions)]
    sibling = _sibling_exchange("sibling_grads", plane)

    out = {}
    names4 = ('grad', 'delta', 'new_m', 'new_v')
    for q, n in enumerate(BIG):
        for tag, t in zip(names4, _adamw("adamw_" + n, P[n], plane[q], sibling[q], M[n], V[n])):
            out[tag + '_' + n] = t
    small_out = _adamw("adamw_small", _pack128([P[n] for n in packed]), plane[-1], sibling[-1],
                       _pack128([M[n] for n in packed]), _pack128([V[n] for n in packed]))
    for tag, buf in zip(names4, small_out):
        for n, t in zip(packed, _unpack128(buf, [shapes[n] for n in packed])):
            out[tag + '_' + n] = t
    loss = lax.psum(loss_rows[0, 0], ("x", "y", "c"))
    return loss, dx, out


def kernel(x, norm1_g, w_in, rwkv_mu, rwkv_w0, rwkv_w2, rwkv_a0, rwkv_a2, rwkv_g2, rwkv_k_k, rwkv_k_a, rwkv_r_k, rwkv_ln_w, rwkv_ln_b, rwkv_proj, gdn_conv_w, gdn_a_log, gdn_dt_bias, gdn_norm_w, gdn_proj, w_out, norm2_g, ffn_up, ffn_conv_w, ffn_down, final_g, loss_target, m_norm1_g, m_w_in, m_rwkv_mu, m_rwkv_w0, m_rwkv_w2, m_rwkv_a0, m_rwkv_a2, m_rwkv_g2, m_rwkv_k_k, m_rwkv_k_a, m_rwkv_r_k, m_rwkv_ln_w, m_rwkv_ln_b, m_rwkv_proj, m_gdn_conv_w, m_gdn_a_log, m_gdn_dt_bias, m_gdn_norm_w, m_gdn_proj, m_w_out, m_norm2_g, m_ffn_up, m_ffn_conv_w, m_ffn_down, m_final_g, v_norm1_g, v_w_in, v_rwkv_mu, v_rwkv_w0, v_rwkv_w2, v_rwkv_a0, v_rwkv_a2, v_rwkv_g2, v_rwkv_k_k, v_rwkv_k_a, v_rwkv_r_k, v_rwkv_ln_w, v_rwkv_ln_b, v_rwkv_proj, v_gdn_conv_w, v_gdn_a_log, v_gdn_dt_bias, v_gdn_norm_w, v_gdn_proj, v_w_out, v_norm2_g, v_ffn_up, v_ffn_conv_w, v_ffn_down, v_final_g):
    weights = (norm1_g, w_in, rwkv_mu, rwkv_w0, rwkv_w2, rwkv_a0, rwkv_a2, rwkv_g2, rwkv_k_k, rwkv_k_a, rwkv_r_k, rwkv_ln_w,
               rwkv_ln_b, rwkv_proj, gdn_conv_w, gdn_a_log, gdn_dt_bias, gdn_norm_w, gdn_proj, w_out, norm2_g, ffn_up,
               ffn_conv_w, ffn_down, final_g)
    m_in = (m_norm1_g, m_w_in, m_rwkv_mu, m_rwkv_w0, m_rwkv_w2, m_rwkv_a0, m_rwkv_a2, m_rwkv_g2, m_rwkv_k_k, m_rwkv_k_a,
            m_rwkv_r_k, m_rwkv_ln_w, m_rwkv_ln_b, m_rwkv_proj, m_gdn_conv_w, m_gdn_a_log, m_gdn_dt_bias, m_gdn_norm_w,
            m_gdn_proj, m_w_out, m_norm2_g, m_ffn_up, m_ffn_conv_w, m_ffn_down, m_final_g)
    v_in = (v_norm1_g, v_w_in, v_rwkv_mu, v_rwkv_w0, v_rwkv_w2, v_rwkv_a0, v_rwkv_a2, v_rwkv_g2, v_rwkv_k_k, v_rwkv_k_a,
            v_rwkv_r_k, v_rwkv_ln_w, v_rwkv_ln_b, v_rwkv_proj, v_gdn_conv_w, v_gdn_a_log, v_gdn_dt_bias, v_gdn_norm_w,
            v_gdn_proj, v_w_out, v_norm2_g, v_ffn_up, v_ffn_conv_w, v_ffn_down, v_final_g)
    drop = lambda n, a: a if n == 'final_g' else a[0]
    P = {n: drop(n, a) for n, a in zip(WEIGHTS, weights)}
    M = {n: drop(n, a) for n, a in zip(WEIGHTS, m_in)}
    V = {n: drop(n, a) for n, a in zip(WEIGHTS, v_in)}
    loss, dx, out = _step(x[0], loss_target[0], P, M, V)
    lift = lambda n, a: a if n == 'final_g' else a[None]
    res = [loss, dx[None]]
    for tag in ('grad', 'delta', 'new_m', 'new_v'):
        res += [lift(n, out[tag + '_' + n]) for n in WEIGHTS]
    return tuple(res)
```

```python
import functools

import jax
import jax.numpy as jnp
from jax import lax
from jax.experimental import pallas as pl
from jax.experimental.pallas import tpu as pltpu

f32 = jnp.float32
bf16 = jnp.bfloat16
HI = lax.Precision.HIGHEST

D_MODEL = 1024
RWKV_HEADS, RWKV_HD, RWKV_W = 8, 64, 512
GDN_HEADS, GDN_HD, GDN_W = 4, 128, 512
FFN_H = 2816
NORM_EPS, L2_EPS, GN_EPS = 1e-6, 1e-6, 64e-5
W_AB = 256
OFF_QKV, OFF_Z, OFF_GATES, OFF_AB = 1792, 3328, 3840, 5888
W_IN_PAD = OFF_AB + W_AB
WKV_CHUNK, WKV_PER_STEP = 64, 4
GDN_CHUNK, GDN_PER_STEP = 128, 4
HALO = 8
LANES = 128
TILE_BYTES = 1 << 20
VMEM_LIMIT = 56 * 1024 * 1024

ADAM_LR, ADAM_B1, ADAM_B2, ADAM_EPS, ADAM_WD, ADAM_STEP = 0.001, 0.9, 0.999, 1e-08, 0.01, 10

ROW_SHARDED = ('w_out', 'ffn_down')
SMALL = ('norm1_g', 'rwkv_mu', 'rwkv_w0', 'rwkv_a0', 'rwkv_k_k', 'rwkv_k_a', 'rwkv_r_k', 'rwkv_ln_w', 'rwkv_ln_b',
         'gdn_a_log', 'gdn_dt_bias', 'gdn_norm_w', 'norm2_g', 'final_g')
WEIGHTS = ('norm1_g', 'w_in', 'rwkv_mu', 'rwkv_w0', 'rwkv_w2', 'rwkv_a0', 'rwkv_a2', 'rwkv_g2', 'rwkv_k_k', 'rwkv_k_a',
           'rwkv_r_k', 'rwkv_ln_w', 'rwkv_ln_b', 'rwkv_proj', 'gdn_conv_w', 'gdn_a_log', 'gdn_dt_bias', 'gdn_norm_w',
           'gdn_proj', 'w_out', 'norm2_g', 'ffn_up', 'ffn_conv_w', 'ffn_down', 'final_g')


def _params(*sem):
    return pltpu.CompilerParams(dimension_semantics=sem, vmem_limit_bytes=VMEM_LIMIT)


def _tile(n, limit):
    if n <= limit:
        return n
    best = None
    for d in range(128, limit + 1, 128):
        if n % d == 0:
            best = d
    if best is None:
        raise ValueError(f"no tile for {n} under {limit}")
    return best


MM_BLOCK_BYTES = 6 << 20


def _mm(a, b, mode, name, add=None, out_dtype=f32, side=None, col_slabs=None):
    if mode == 'nn':
        (M, K), N = a.shape, b.shape[1]
    elif mode == 'nt':
        (M, K), N = a.shape, b.shape[0]
    else:
        (K, M), N = a.shape, b.shape[1]
    tm = _tile(M, 1408)
    tk = _tile(K, min(2816, MM_BLOCK_BYTES // (tm * a.dtype.itemsize)))
    tn = _tile(N, max(128, min(MM_BLOCK_BYTES // (tk * b.dtype.itemsize), MM_BLOCK_BYTES // (tm * 4)) // 128 * 128))
    if col_slabs is not None:
        tn = N // col_slabs
    nk = K // tk
    grid = (M // tm, N // tn, nk)
    dn = {'nn': (((1,), (0,)), ((), ())), 'nt': (((1,), (1,)), ((), ())), 'tn': (((0,), (0,)), ((), ()))}[mode]
    n_add = 0 if add is None else 1
    n_side = 0 if side is None else len(side[0])

    def body(a_ref, b_ref, *rest):
        add_ref = rest[0] if add is not None else None
        side_in, rest = rest[n_add:n_add + n_side], rest[n_add + n_side:]
        o_ref, side_out, rest = rest[0], rest[1:1 + n_side], rest[1 + n_side:]
        acc_ref, rest = (rest[0], rest[1:]) if nk > 1 else (None, rest)
        ids = [pl.program_id(d) for d in range(3)]
        if side is not None:
            start, finish = _xy_copies(side_in, side_out, rest, side[1])
            pl.when((ids[0] == 0) & (ids[1] == 0) & (ids[2] == 0))(start)
        acc = lax.dot_general(a_ref[...].astype(bf16), b_ref[...].astype(bf16), dn, preferred_element_type=f32)
        if nk == 1:
            o_ref[...] = (acc + add_ref[...] if add is not None else acc).astype(out_dtype)
        else:
            k = ids[2]

            @pl.when(k == 0)
            def _():
                acc_ref[...] = acc + add_ref[...] if add is not None else acc

            @pl.when(k > 0)
            def _():
                acc_ref[...] += acc

            @pl.when(k == nk - 1)
            def _():
                o_ref[...] = acc_ref[...].astype(out_dtype)
        if side is not None:
            pl.when((ids[0] == grid[0] - 1) & (ids[1] == grid[1] - 1) & (ids[2] == nk - 1))(finish)

    a_spec = (pl.BlockSpec((tk, tm), lambda i, j, k: (k, i)) if mode == 'tn'
              else pl.BlockSpec((tm, tk), lambda i, j, k: (i, k)))
    b_spec = (pl.BlockSpec((tn, tk), lambda i, j, k: (j, k)) if mode == 'nt'
              else pl.BlockSpec((tk, tn), lambda i, j, k: (k, j)))
    o_spec = pl.BlockSpec((tm, tn), lambda i, j, k: (i, j))
    o_shape = jax.ShapeDtypeStruct((M, N), out_dtype)
    if col_slabs is not None:
        o_spec = pl.BlockSpec((None, tm, tn), lambda i, j, k: (j, i, 0))
        o_shape = jax.ShapeDtypeStruct((col_slabs, M, tn), out_dtype)
    any_spec = pl.BlockSpec(memory_space=pl.ANY)
    side_bufs = [] if side is None else list(side[0])
    ins, specs = [a, b], [a_spec, b_spec]
    if add is not None:
        ins.append(add)
        specs.append(o_spec)
    outs = pl.pallas_call(
        body, grid=grid, in_specs=specs + [any_spec] * n_side, out_specs=[o_spec] + [any_spec] * n_side,
        out_shape=[o_shape] + (_xy_out_shapes(side_bufs, side[1]) if side is not None else []),
        scratch_shapes=([pltpu.VMEM((tm, tn), f32)] if nk > 1 else []) + (_xy_sems(n_side, side[1]) if side is not None else []),
        name=name,
        compiler_params=_params(*(("arbitrary",) * 3 if side is not None else ("parallel", "parallel", "arbitrary"))))(
            *ins, *side_bufs)
    return list(outs) if side is not None else outs[0]


def _shift_down(cur, prev, s):
    if s == 0:
        return cur
    ext = jnp.concatenate([prev, cur], axis=0)
    return pltpu.roll(ext, s, 0)[HALO:]


def _shift_up(cur, nxt, s):
    if s == 0:
        return cur
    ext = jnp.concatenate([cur, nxt], axis=0)
    return pltpu.roll(ext, ext.shape[0] - s, 0)[:cur.shape[0]]


def _conv_apply(cur, prev, w_ref, shifted=None):
    taps = w_ref.shape[0]
    out = None
    for i in range(taps):
        s = taps - 1 - i
        term = (shifted[s] if shifted is not None else _shift_down(cur, prev, s)) * w_ref[pl.ds(i, 1), :]
        out = term if out is None else out + term
    return out


def _row_spec(tm, w, col=0):
    return pl.BlockSpec((tm, w), lambda i: (i, col))


def _cols(a, width, col):
    return (a, width, col)


def _row_of(r):
    return r if isinstance(r, tuple) else (r, r.shape[1], 0)


def _prev_spec(tm, w):
    return pl.BlockSpec((HALO, w), lambda i: (jnp.maximum(i * (tm // HALO) - 1, 0), 0))


def _next_spec(tm, w, T):
    return pl.BlockSpec((HALO, w), lambda i: (jnp.minimum((i + 1) * (tm // HALO), T // HALO - 1), 0))


def _full_spec(shape):
    return pl.BlockSpec(shape, lambda i: (0,) * len(shape))


def _pw_fwd(name, fn, rows, consts, out_widths, tm, conv_w=None, out_dtype=f32, strip=None):
    T = _row_of(rows[0])[0].shape[0]
    nr, nc = len(rows), len(consts)

    def body(*refs):
        i = pl.program_id(0)
        if strip is not None:
            for j in range(out_widths[0] // strip):
                sl = slice(strip * j, strip * (j + 1))
                outs = fn(*[r[:, sl] for r in refs[:nr + nc]])
                for o_ref, o in zip(refs[nr + nc:], outs):
                    o_ref[:, sl] = o.astype(out_dtype)
            return
        vals = [r[...] for r in refs[:nr]]
        p = nr
        if conv_w is not None:
            prev = jnp.where(i > 0, refs[p][...], 0.0)
            vals[0] = _conv_apply(vals[0], prev, refs[p + 1])
            p += 2
        cvals = [r[...] for r in refs[p:p + nc]]
        outs = fn(*vals, *cvals)
        for o_ref, o in zip(refs[p + nc:], outs):
            o_ref[...] = o.astype(out_dtype)

    ins = [_row_of(r)[0] for r in rows]
    specs = [_row_spec(tm, *_row_of(r)[1:]) for r in rows]
    if conv_w is not None:
        ins += [rows[0], conv_w]
        specs += [_prev_spec(tm, rows[0].shape[1]), _full_spec(conv_w.shape)]
    ins += list(consts)
    specs += [_full_spec(c.shape) for c in consts]
    outs = pl.pallas_call(
        body, grid=(T // tm,), in_specs=specs,
        out_specs=[_row_spec(tm, w) for w in out_widths],
        out_shape=[jax.ShapeDtypeStruct((T, w), out_dtype) for w in out_widths], name=name,
        compiler_params=_params("parallel"))(*ins)
    return outs


def _pw_bwd(name, fn, rows, consts, cots, tm, add_to_first=None, row_dtypes=None, strip=None):
    rows = [_row_of(r) for r in rows]
    T = rows[0][0].shape[0]
    nr, nc = len(rows), len(consts)
    flat_cots = [c for grp in cots for c in grp]
    row_dtypes = row_dtypes or [f32] * nr
    n_extra = 0 if add_to_first is None else 1
    width = rows[0][1]

    def body(*refs):
        i = pl.program_id(0)
        in_refs, cot_refs = refs[:nr + nc], refs[nr + nc:nr + nc + len(flat_cots)]
        extra_ref = refs[nr + nc + len(flat_cots)] if add_to_first is not None else None
        row_out = refs[nr + nc + len(flat_cots) + n_extra:][:nr]
        const_out = refs[nr + nc + len(flat_cots) + n_extra + nr:]

        @pl.when(i == 0)
        def _():
            for q in range(nc):
                const_out[q][...] = jnp.zeros_like(const_out[q])

        def part(sl):
            cot_vals, p = [], 0
            for grp in cots:
                acc = cot_refs[p][:, sl]
                for q in range(1, len(grp)):
                    acc = acc + cot_refs[p + q][:, sl]
                p += len(grp)
                cot_vals.append(acc)
            _, vjp = jax.vjp(fn, *[r[:, sl] for r in in_refs])
            grads = vjp(tuple(cot_vals))
            for q in range(nr):
                g = grads[q]
                if q == 0 and extra_ref is not None:
                    g = g + extra_ref[:, sl]
                row_out[q][:, sl] = g.astype(row_dtypes[q])
            for q in range(nc):
                const_out[q][:, sl] += grads[nr + q]

        if strip is None:
            part(slice(None))
        else:
            for j in range(width // strip):
                part(slice(strip * j, strip * (j + 1)))

    ins = [r[0] for r in rows] + list(consts) + flat_cots
    specs = ([_row_spec(tm, r[1], r[2]) for r in rows] + [_full_spec(c.shape) for c in consts]
             + [_row_spec(tm, c.shape[1]) for c in flat_cots])
    if add_to_first is not None:
        ins.append(add_to_first)
        specs.append(_row_spec(tm, add_to_first.shape[1]))
    out_shapes = ([jax.ShapeDtypeStruct((T, r[1]), d) for r, d in zip(rows, row_dtypes)]
                  + [jax.ShapeDtypeStruct(c.shape, f32) for c in consts])
    out_specs = [_row_spec(tm, r[1]) for r in rows] + [_full_spec(c.shape) for c in consts]
    outs = pl.pallas_call(
        body, grid=(T // tm,), in_specs=specs, out_specs=out_specs, out_shape=out_shapes, name=name,
        compiler_params=_params("arbitrary"))(*ins)
    return list(outs[:nr]), list(outs[nr:])


def _pw_conv_bwd(name, fn, rows, consts, cots, conv_w, tm, row_dtypes=None):
    T, W0 = rows[0].shape
    nr, nc = len(rows), len(consts)
    taps = conv_w.shape[0]
    nblk = T // tm
    flat_cots = [c for grp in cots for c in grp]
    row_dtypes = row_dtypes or [f32] * nr

    def body(*refs):
        i = pl.program_id(0)
        p = 0
        cur = [r[...] for r in refs[p:p + nr]]; p += nr
        nxt = [r[...] for r in refs[p:p + nr]]; p += nr
        prev = jnp.where(i > 0, refs[p][...], 0.0); p += 1
        w_ref = refs[p]; p += 1
        cvals = [r[...] for r in refs[p:p + nc]]; p += nc

        def summed(p0):
            out, q = [], p0
            for grp in cots:
                acc = refs[q][...]
                for t in range(1, len(grp)):
                    acc = acc + refs[q + t][...]
                q += len(grp)
                out.append(acc)
            return out, q

        cot_cur, p = summed(p)
        cot_nxt, p = summed(p)
        row_out, dw_ref, const_out = refs[p:p + nr], refs[p + nr], refs[p + nr + 1:]

        x_cur = cur[0]
        x_down = [_shift_down(x_cur, prev, s_) for s_ in range(taps)]
        _, vjp = jax.vjp(fn, _conv_apply(x_cur, prev, w_ref, x_down), *cur[1:], *cvals)
        grads = vjp(tuple(cot_cur))
        _, vjp_n = jax.vjp(fn, _conv_apply(nxt[0], x_cur[tm - HALO:], w_ref), *nxt[1:], *cvals)
        dc_n = jnp.where(i < nblk - 1, vjp_n(tuple(cot_nxt))[0], 0.0)
        dc = grads[0]

        @pl.when(i == 0)
        def _():
            dw_ref[...] = jnp.zeros_like(dw_ref)
            for q in range(nc):
                const_out[q][...] = jnp.zeros_like(const_out[q])

        dx = None
        for k in range(taps):
            s_ = taps - 1 - k
            term = _shift_up(dc, dc_n, s_) * w_ref[pl.ds(k, 1), :]
            dx = term if dx is None else dx + term
            dw_ref[pl.ds(k, 1), :] += jnp.sum(dc * x_down[s_], axis=0, keepdims=True)
        row_out[0][...] = dx.astype(row_dtypes[0])
        for q in range(1, nr):
            row_out[q][...] = grads[q].astype(row_dtypes[q])
        for q in range(nc):
            const_out[q][...] += grads[nr + q]

    ins = list(rows) + list(rows) + [rows[0], conv_w] + list(consts) + flat_cots + flat_cots
    specs = ([_row_spec(tm, r.shape[1]) for r in rows] + [_next_spec(tm, r.shape[1], T) for r in rows]
             + [_prev_spec(tm, W0), _full_spec(conv_w.shape)] + [_full_spec(c.shape) for c in consts]
             + [_row_spec(tm, c.shape[1]) for c in flat_cots] + [_next_spec(tm, c.shape[1], T) for c in flat_cots])
    out_shapes = ([jax.ShapeDtypeStruct(r.shape, d) for r, d in zip(rows, row_dtypes)]
                  + [jax.ShapeDtypeStruct(conv_w.shape, f32)] + [jax.ShapeDtypeStruct(c.shape, f32) for c in consts])
    out_specs = ([_row_spec(tm, r.shape[1]) for r in rows] + [_full_spec(conv_w.shape)]
                 + [_full_spec(c.shape) for c in consts])
    outs = pl.pallas_call(
        body, grid=(nblk,), in_specs=specs, out_specs=out_specs, out_shape=out_shapes, name=name,
        compiler_params=_params("arbitrary"))(*ins)
    return list(outs[:nr]), outs[nr], list(outs[nr + 1:])


def _sigmoid(x):
    return 0.5 * jnp.tanh(0.5 * x) + 0.5


def _softplus(x):
    return jnp.maximum(x, 0.0) + jnp.log(1.0 + jnp.exp(jnp.minimum(x, -x)))


def _seg_sum_impl(x, seg):
    w = x.shape[-1]
    r = lax.broadcasted_iota(jnp.int32, (w, w), 0) // seg
    c = lax.broadcasted_iota(jnp.int32, (w, w), 1) // seg
    ones = (r == c).astype(bf16)
    hi = x.astype(bf16)
    lo = (x - hi.astype(f32)).astype(bf16)
    return (jnp.dot(hi, ones, preferred_element_type=f32) + jnp.dot(lo, ones, preferred_element_type=f32))


@functools.partial(jax.custom_vjp, nondiff_argnums=(1,))
def _seg_sum(x, seg):
    return _seg_sum_impl(x, seg)


_seg_sum.defvjp(lambda x, seg: (_seg_sum_impl(x, seg), None), lambda seg, _, g: (_seg_sum_impl(g, seg),))


def _rms(x, g):
    return x * lax.rsqrt(jnp.mean(x * x, axis=-1, keepdims=True) + NORM_EPS) * g


def _rms_fn(x, g):
    return (_rms(x, g),)


def _loss_rows(x2, tgt, g):
    e = _rms(x2, g) - tgt
    return 0.5 * jnp.sum(e * e, axis=-1, keepdims=True) * (1.0 / D_MODEL)


def _rwkv_prep_fn(ps, w0, w2p, a0, a2p, g2, k_k, k_a):
    r, k, v = ps[:, 0:512], ps[:, 512:1024], ps[:, 1024:1536]
    wa, gl = ps[:, 1536:1664], ps[:, 1664:1792]
    z = w0 + jnp.dot(jnp.tanh(wa), w2p, precision=HI, preferred_element_type=f32)
    w_log = -_softplus(-z) - 0.5
    lw = -jnp.exp(w_log)
    a = _sigmoid(a0 + jnp.dot(wa, a2p, precision=HI, preferred_element_type=f32))
    g = jnp.dot(_sigmoid(gl), g2, precision=HI, preferred_element_type=f32)
    kx = k * k_k
    kk = kx * lax.rsqrt(_seg_sum(kx * kx, RWKV_HD) + L2_EPS)
    k2 = k * (1.0 + (a - 1.0) * k_a)
    return r, lw, k2, v, -kk, kk * a, g


def _rwkv_post_fn(y, r, k2, v, g, ln_w, ln_b, rk):
    mean = _seg_sum(y, RWKV_HD) * (1.0 / RWKV_HD)
    yc = y - mean
    var = _seg_sum(yc * yc, RWKV_HD) * (1.0 / RWKV_HD)
    yn = yc * lax.rsqrt(var + GN_EPS) * ln_w + ln_b
    bonus = _seg_sum(r * k2 * rk, RWKV_HD) * v
    return ((yn + bonus) * g,)


def _gdn_prep_fn(c, ab, al_p, dt_p):
    s = c * _sigmoid(c)
    q, k, v = s[:, 0:512], s[:, 512:1024], s[:, 1024:1536]
    q = q * lax.rsqrt(_seg_sum(q * q, GDN_HD) + L2_EPS) * (GDN_HD ** -0.5)
    k = k * lax.rsqrt(_seg_sum(k * k, GDN_HD) + L2_EPS)
    lane = lax.broadcasted_iota(jnp.int32, ab.shape, 1)
    gpart = -jnp.exp(al_p) * _softplus(ab + dt_p)
    gbeta = jnp.where(lane < GDN_HEADS, gpart, jnp.where(lane < 2 * GDN_HEADS, _sigmoid(ab), 0.0))
    return q, k, v, gbeta


def _gdn_post_fn(o, z, nw):
    ms = _seg_sum(o * o, GDN_HD) * (1.0 / GDN_HD)
    return (o * lax.rsqrt(ms + NORM_EPS) * nw * (z * _sigmoid(z)),)


def _mix_fn(ga, gb, ya, yb):
    return (_sigmoid(ga) * ya + _sigmoid(gb) * yb,)


STRIP = 128


def _ffn_strip_fn(cg, cu):
    return cg * _sigmoid(cg) * cu


def _strip_conv(ref, prev_ref, w_ref, sl, first, taps):
    cur = ref[:, sl]
    prev = jnp.where(first, 0.0, prev_ref[:, sl])
    down = [_shift_down(cur, prev, s) for s in range(taps)]
    conv = None
    for k in range(taps):
        term = down[taps - 1 - k] * w_ref[pl.ds(k, 1), sl]
        conv = term if conv is None else conv + term
    return cur, down, conv


def _ffn_act_fwd(h, w, tm):
    T, W2 = h.shape
    H = W2 // 2
    taps = w.shape[0]

    def body(h_ref, hp_ref, w_ref, o_ref):
        first = pl.program_id(0) == 0
        for j in range(H // STRIP):
            gs, us = slice(STRIP * j, STRIP * (j + 1)), slice(H + STRIP * j, H + STRIP * (j + 1))
            cg = _strip_conv(h_ref, hp_ref, w_ref, gs, first, taps)[2]
            cu = _strip_conv(h_ref, hp_ref, w_ref, us, first, taps)[2]
            o_ref[:, gs] = _ffn_strip_fn(cg, cu).astype(o_ref.dtype)

    return pl.pallas_call(
        body, grid=(T // tm,), in_specs=[_row_spec(tm, W2), _prev_spec(tm, W2), _full_spec(w.shape)],
        out_specs=_row_spec(tm, H), out_shape=jax.ShapeDtypeStruct((T, H), bf16), name="ffn_act",
        compiler_params=_params("parallel"))(h, h, w)


def _ffn_act_bwd(h, dact, w, tm):
    T, W2 = h.shape
    H = W2 // 2
    taps = w.shape[0]
    nblk = T // tm

    def body(h_ref, hp_ref, hn_ref, d_ref, dn_ref, w_ref, dh_ref, dw_ref):
        i = pl.program_id(0)
        first, last = i == 0, i == nblk - 1

        @pl.when(first)
        def _():
            dw_ref[...] = jnp.zeros_like(dw_ref)

        for j in range(H // STRIP):
            gs, us = slice(STRIP * j, STRIP * (j + 1)), slice(H + STRIP * j, H + STRIP * (j + 1))
            parts = {}
            for name, sl in (('g', gs), ('u', us)):
                cur, down, conv = _strip_conv(h_ref, hp_ref, w_ref, sl, first, taps)
                nxt = hn_ref[:, sl]
                conv_n = None
                for k in range(taps):
                    term = _shift_down(nxt, cur[tm - HALO:], taps - 1 - k) * w_ref[pl.ds(k, 1), sl]
                    conv_n = term if conv_n is None else conv_n + term
                parts[name] = (down, conv, conv_n)
            _, vjp = jax.vjp(_ffn_strip_fn, parts['g'][1], parts['u'][1])
            dcs = vjp(d_ref[:, gs])
            _, vjp_n = jax.vjp(_ffn_strip_fn, parts['g'][2], parts['u'][2])
            dcs_n = vjp_n(jnp.where(last, 0.0, dn_ref[:, gs]))
            for (name, sl), dc, dc_n in zip((('g', gs), ('u', us)), dcs, dcs_n):
                down = parts[name][0]
                dx = None
                for k in range(taps):
                    s_ = taps - 1 - k
                    term = _shift_up(dc, dc_n, s_) * w_ref[pl.ds(k, 1), sl]
                    dx = term if dx is None else dx + term
                    dw_ref[pl.ds(k, 1), sl] += jnp.sum(dc * down[s_], axis=0, keepdims=True)
                dh_ref[:, sl] = dx.astype(dh_ref.dtype)

    return pl.pallas_call(
        body, grid=(nblk,),
        in_specs=[_row_spec(tm, W2), _prev_spec(tm, W2), _next_spec(tm, W2, T), _row_spec(tm, H), _next_spec(tm, H, T),
                  _full_spec(w.shape)],
        out_specs=[_row_spec(tm, W2), _full_spec(w.shape)],
        out_shape=[jax.ShapeDtypeStruct((T, W2), bf16), jax.ShapeDtypeStruct(w.shape, f32)], name="ffn_act_bwd",
        compiler_params=_params("arbitrary"))(h, h, h, dact, dact, w)


N_POS = 4


def _xy_out_shapes(bufs, scatter):
    return [jax.ShapeDtypeStruct((N_POS,) + tuple(b.shape[1:] if scatter else b.shape), b.dtype) for b in bufs]


def _xy_sems(n, scatter):
    sems = [pltpu.SemaphoreType.DMA((3 * n,)), pltpu.SemaphoreType.DMA((3 * n,)), pltpu.SemaphoreType.DMA((n,))]
    return sems if scatter else sems + [pltpu.SemaphoreType.DMA((3 * n,)), pltpu.SemaphoreType.DMA((3 * n,))]


def _xy_copies(in_refs, out_refs, sems, scatter):
    n = len(in_refs)
    send_sems, recv_sems, local_sems = sems[:3]

    def place():
        x, y, c = lax.axis_index("x"), lax.axis_index("y"), lax.axis_index("c")
        return x, y, c, 2 * x + y, [(1 - x, y), (x, 1 - y), (1 - x, 1 - y)]

    def half(ref, a, which):
        rows = in_refs[a].shape[0] // 2
        return ref.at[pl.ds(pl.multiple_of(which * rows, HALO), rows)]

    def ici(a, k, src, dst, peer, c):
        return pltpu.make_async_remote_copy(
            src_ref=src, dst_ref=dst, send_sem=send_sems.at[3 * a + k], recv_sem=recv_sems.at[3 * a + k],
            device_id=(peer[0], peer[1], c), device_id_type=pl.DeviceIdType.MESH)

    def outgoing():
        x, y, c, me, peers = place()
        own = [pltpu.make_async_copy(in_refs[a].at[me] if scatter else in_refs[a], out_refs[a].at[me], local_sems.at[a])
               for a in range(n)]
        if scatter:
            sends = [ici(a, k, in_refs[a].at[2 * p[0] + p[1]], out_refs[a].at[me], p, c)
                     for a in range(n) for k, p in enumerate(peers)]
        else:
            sends = [ici(a, k, half(in_refs[a], a, c), half(out_refs[a].at[me], a, c), p, c)
                     for a in range(n) for k, p in enumerate(peers)]
        return own, sends

    def arrivals():
        x, y, c, me, peers = place()
        if scatter:
            return [ici(a, k, in_refs[a].at[me], out_refs[a].at[2 * p[0] + p[1]], p, c)
                    for a in range(n) for k, p in enumerate(peers)]
        return [ici(a, k, half(in_refs[a], a, c), half(out_refs[a].at[2 * p[0] + p[1]], a, c), p, c)
                for a in range(n) for k, p in enumerate(peers)]

    def to_sibling(mine):
        x, y, c, me, peers = place()
        which = c if mine else 1 - c
        return [pltpu.make_async_remote_copy(
            src_ref=half(out_refs[a].at[2 * p[0] + p[1]], a, which), dst_ref=half(out_refs[a].at[2 * p[0] + p[1]], a, which),
            send_sem=sems[3].at[3 * a + k], recv_sem=sems[4].at[3 * a + k],
            device_id=(x, y, 1 - c), device_id_type=pl.DeviceIdType.MESH) for a in range(n) for k, p in enumerate(peers)]

    def start():
        own, sends = outgoing()
        for cp in own + sends:
            cp.start()

    def finish():
        if scatter:
            for cp in arrivals():
                cp.wait_recv()
        else:
            passed = to_sibling(True)
            for cp, fwd in zip(arrivals(), passed):
                cp.wait_recv()
                fwd.start()
            for cp in to_sibling(False):
                cp.wait_recv()
            for fwd in passed:
                fwd.wait_send()
        own, sends = outgoing()
        for cp in sends:
            cp.wait_send()
        for cp in own:
            cp.wait()

    return start, finish


_NN, _NT, _TN = 'hcs,hsd->hcd', 'hcd,hsd->hcs', 'hcd,hce->hde'


def _lo(spec, a, b):
    return jnp.einsum(spec, a.astype(bf16), b.astype(bf16), preferred_element_type=f32)


@jax.custom_vjp
def _bmm(a, b):
    return _lo(_NN, a, b)


_bmm.defvjp(lambda a, b: (_lo(_NN, a, b), (a, b)), lambda ab, g: (_lo(_NT, g, ab[1]), _lo(_TN, ab[0], g)))


@jax.custom_vjp
def _bmm_nt(a, b):
    return _lo(_NT, a, b)


_bmm_nt.defvjp(lambda a, b: (_lo(_NT, a, b), (a, b)), lambda ab, g: (_lo(_NN, g, ab[1]), _lo(_TN, g, ab[0])))


@jax.custom_vjp
def _bmm_tn(a, b):
    return _lo(_TN, a, b)


_bmm_tn.defvjp(lambda a, b: (_lo(_TN, a, b), (a, b)), lambda ab, g: (_lo(_NT, ab[1], g), _lo(_NN, ab[0], g)))


def _masks(H, C):
    row = lax.broadcasted_iota(jnp.int32, (H, C, C), 1)
    col = lax.broadcasted_iota(jnp.int32, (H, C, C), 2)
    return row, col


def _tri_inv_impl(L):
    H, C, _ = L.shape
    row, col = _masks(H, C)
    eye = (row == col).astype(f32)
    base = 16
    same = (row // base) == (col // base)
    Ld = jnp.where(same, L, 0.0)
    X = -Ld
    inv = eye + X
    for _ in range(3):
        X = _bmm(X, X)
        inv = _bmm(inv, eye + X)
    if C == base:
        return inv
    N = _bmm(inv, L - Ld)
    out = eye - N
    levels = C // base
    P = N
    span = 2
    while span < levels:
        P = _bmm(P, P)
        out = _bmm(out, eye + P)
        span *= 2
    return _bmm(out, inv)


@jax.custom_vjp
def _tri_inv(L):
    return _tri_inv_impl(L)


def _tri_inv_fwd(L):
    T = _tri_inv_impl(L)
    return T, T


def _tri_inv_bwd(T, dT):
    return (-_bmm_nt(_bmm_tn(T, dT), T),)


_tri_inv.defvjp(_tri_inv_fwd, _tri_inv_bwd)


def _cumsum_impl(x, reverse):
    C = x.shape[1]
    row = lax.broadcasted_iota(jnp.int32, x.shape, 1)
    s = 1
    while s < C:
        if reverse:
            x = x + jnp.where(row < C - s, pltpu.roll(x, C - s, 1), 0.0)
        else:
            x = x + jnp.where(row >= s, pltpu.roll(x, s, 1), 0.0)
        s *= 2
    return x


@jax.custom_vjp
def _cumsum(x):
    return _cumsum_impl(x, False)


_cumsum.defvjp(lambda x: (_cumsum_impl(x, False), None), lambda _, g: (_cumsum_impl(g, True),))


def _wkv_prep(r, lw, k, v, a, b):
    lane = lax.broadcasted_iota(jnp.int32, (r.shape[0], 128), 1)
    low = lane < RWKV_HD

    def heads(t):
        out = []
        for p in range(RWKV_HEADS // 2):
            pair = t[:, 128 * p:128 * (p + 1)]
            out += [jnp.where(low, pair, 0.0), jnp.where(low, 0.0, pair)]
        return jnp.concatenate([t[None] for t in out], axis=0)

    r, lw, k, v, a, b = [heads(t) for t in (r, lw, k, v, a, b)]
    H, C, D = r.shape
    row, col = _masks(H, C)
    incl, strict = row >= col, row > col
    cw = _cumsum(lw)
    cwp = cw - lw
    cwl = jnp.sum(lw, axis=1, keepdims=True)
    en = jnp.exp(-cw)
    at, rt, bt, kt = a * jnp.exp(cwp), r * jnp.exp(cw), b * en, k * en
    Tm = _tri_inv(-jnp.where(strict, _bmm_nt(at, bt), 0.0))
    ar = jnp.concatenate([at, rt], axis=1)
    gram = _bmm_nt(ar, jnp.concatenate([bt, kt], axis=1))
    row2 = lax.broadcasted_iota(jnp.int32, (H, 2 * C, 2 * C), 1)
    col2 = lax.broadcasted_iota(jnp.int32, (H, 2 * C, 2 * C), 2) % C
    gram = jnp.where(((row2 < C) & (row2 > col2)) | ((row2 >= C) & (row2 - C >= col2)), gram, 0.0)
    a_bk, r_bk = gram[:, :C], gram[:, C:]
    lak_v = _bmm(a_bk, jnp.concatenate([jnp.zeros_like(v), v], axis=1))
    ed = jnp.exp(cwl - cw)
    zdec = jnp.swapaxes(jnp.broadcast_to(jnp.exp(cwl), (H, D, D)), 1, 2)
    return ar, Tm, lak_v, r_bk, jnp.concatenate([b * ed, k * ed], axis=1), zdec, v


def _wkv_step(Z, ar, Tm, lak_v, r_bk, bk_d, zdec, v):
    C = Tm.shape[1]
    ar_z = _bmm(ar, Z)
    uv = jnp.concatenate([_bmm(Tm, ar_z[:, :C] + lak_v), v], axis=1)
    y = ar_z[:, C:] + _bmm(r_bk, uv)
    Z1 = Z * zdec + _bmm_tn(bk_d, uv)
    return jnp.concatenate([y[2 * p] + y[2 * p + 1] for p in range(RWKV_HEADS // 2)], axis=1), Z1


def _gdn_prep(q, k, v, gbeta):
    heads = lambda t: jnp.concatenate([t[None, :, GDN_HD * h:GDN_HD * (h + 1)] for h in range(GDN_HEADS)], axis=0)
    src = lax.broadcasted_iota(jnp.int32, (W_AB, 2 * GDN_W), 0)
    dst = lax.broadcasted_iota(jnp.int32, (W_AB, 2 * GDN_W), 1) // GDN_HD
    spread = jnp.dot(gbeta, (src == dst).astype(f32), precision=HI, preferred_element_type=f32)
    q, k, v, g, beta = heads(q), heads(k), heads(v), heads(spread[:, :GDN_W]), heads(spread[:, GDN_W:])
    H, C, D = q.shape
    row, col = _masks(H, C)
    incl, strict = row >= col, row > col
    gc = _cumsum(g)
    diff = gc - jnp.swapaxes(gc, 1, 2)
    decay = jnp.where(incl, jnp.exp(jnp.where(incl, diff, 0.0)), 0.0)
    gl = jnp.sum(g, axis=1, keepdims=True)
    kb, vb = k * beta, v * beta
    gram = _bmm_nt(jnp.concatenate([kb, q], axis=1), k)
    L = jnp.where(strict, gram[:, :C] * decay, 0.0)
    attn = jnp.where(incl, gram[:, C:] * decay, 0.0)
    egc = jnp.exp(gc)
    t_vk = _bmm(_tri_inv(L), jnp.concatenate([vb, kb * egc], axis=2))
    return t_vk[:, :, :D], jnp.concatenate([t_vk[:, :, D:], q * egc], axis=1), attn, k * jnp.exp(gl - gc), jnp.exp(gl)


def _gdn_step(S, u, wq, attn, ke, sdec):
    C = u.shape[1]
    wq_s = _bmm(wq, S)
    v_new = u - wq_s[:, :C]
    o = wq_s[:, C:] + _bmm(attn, v_new)
    S1 = S * sdec + _bmm_tn(ke, v_new)
    return jnp.concatenate([o[h] for h in range(GDN_HEADS)], axis=1), S1


def _scan_fwd(name, fns, ins, C, H, dh, w_out, per_step, side=None):
    prep, step = fns
    T = ins[0].shape[0]
    n_in = len(ins)
    blk = C * per_step
    nblk = T // blk
    n_side = 0 if side is None else len(side[0])

    def body(*refs):
        in_refs, refs = refs[:n_in], refs[n_in:]
        side_in, refs = refs[:n_side], refs[n_side:]
        y_ref, zs_ref, refs = refs[0], refs[1], refs[2:]
        side_out, refs = refs[:n_side], refs[n_side:]
        z_scr = refs[0]
        if side is not None:
            start, finish = _xy_copies(side_in, side_out, refs[1:], side[1])
            pl.when(pl.program_id(0) == 0)(start)

        @pl.when(pl.program_id(0) == 0)
        def _():
            z_scr[...] = jnp.zeros_like(z_scr)

        rows = [slice(C * j, C * (j + 1)) for j in range(per_step)]
        prepped = [prep(*[r[rw, :] for r in in_refs]) for rw in rows]
        Z = z_scr[...]
        for j, rw in enumerate(rows):
            zs_ref[j] = Z
            y, Z = step(Z, *prepped[j])
            y_ref[rw, :] = y
        z_scr[...] = Z
        if side is not None:
            pl.when(pl.program_id(0) == nblk - 1)(finish)

    side_bufs = [] if side is None else list(side[0])
    any_spec = pl.BlockSpec(memory_space=pl.ANY)
    return pl.pallas_call(
        body, grid=(nblk,),
        in_specs=[pl.BlockSpec((blk, a.shape[1]), lambda i: (i, 0)) for a in ins] + [any_spec] * n_side,
        out_specs=[pl.BlockSpec((blk, w_out), lambda i: (i, 0)), pl.BlockSpec((per_step, H, dh, dh), lambda i: (i, 0, 0, 0))]
        + [any_spec] * n_side,
        out_shape=[jax.ShapeDtypeStruct((T, w_out), f32), jax.ShapeDtypeStruct((T // C, H, dh, dh), f32)]
        + (_xy_out_shapes(side_bufs, side[1]) if side is not None else []),
        scratch_shapes=[pltpu.VMEM((H, dh, dh), f32)] + (_xy_sems(n_side, side[1]) if side is not None else []), name=name,
        compiler_params=_params("arbitrary"))(*ins, *side_bufs)


def _scan_bwd(name, fns, ins, dy, zs, C, per_step, side=None):
    prep, step = fns
    T = ins[0].shape[0]
    _, H, dh, _ = zs.shape
    n_in = len(ins)
    blk = C * per_step
    nblk = T // blk
    n_side = 0 if side is None else len(side[0])

    def body(*refs):
        in_refs, dy_ref, zs_ref, refs = refs[:n_in], refs[n_in], refs[n_in + 1], refs[n_in + 2:]
        side_in, refs = refs[:n_side], refs[n_side:]
        out_refs, refs = refs[:n_in], refs[n_in:]
        side_out, refs = refs[:n_side], refs[n_side:]
        dz_scr = refs[0]
        if side is not None:
            start, finish = _xy_copies(side_in, side_out, refs[1:], side[1])
            pl.when(pl.program_id(0) == 0)(start)

        @pl.when(pl.program_id(0) == 0)
        def _():
            dz_scr[...] = jnp.zeros_like(dz_scr)

        rows = [slice(C * j, C * (j + 1)) for j in range(per_step)]
        prepped = [jax.vjp(prep, *[r[rw, :] for r in in_refs]) for rw in rows]
        d_prepped = [None] * per_step
        dZ = dz_scr[...]
        for j in reversed(range(per_step)):
            _, pull = jax.vjp(step, zs_ref[j], *prepped[j][0])
            dZ, *d_prepped[j] = pull((dy_ref[rows[j], :], dZ))
        dz_scr[...] = dZ
        for j, rw in enumerate(rows):
            for o_ref, gval in zip(out_refs, prepped[j][1](tuple(d_prepped[j]))):
                o_ref[rw, :] = gval
        if side is not None:
            pl.when(pl.program_id(0) == nblk - 1)(finish)

    side_bufs = [] if side is None else list(side[0])
    any_spec = pl.BlockSpec(memory_space=pl.ANY)
    rev = lambda i: (nblk - 1 - i, 0)
    return pl.pallas_call(
        body, grid=(nblk,),
        in_specs=[pl.BlockSpec((blk, a.shape[1]), rev) for a in ins]
        + [pl.BlockSpec((blk, dy.shape[1]), rev), pl.BlockSpec((per_step, H, dh, dh), lambda i: (nblk - 1 - i, 0, 0, 0))]
        + [any_spec] * n_side,
        out_specs=[pl.BlockSpec((blk, a.shape[1]), rev) for a in ins] + [any_spec] * n_side,
        out_shape=[jax.ShapeDtypeStruct(a.shape, f32) for a in ins]
        + (_xy_out_shapes(side_bufs, side[1]) if side is not None else []),
        scratch_shapes=[pltpu.VMEM((H, dh, dh), f32)] + (_xy_sems(n_side, side[1]) if side is not None else []), name=name,
        compiler_params=_params("arbitrary"))(*ins, dy, zs, *side_bufs)


def _loss_call(x2, tgt, g, tm):
    T, W = x2.shape

    def body(x_ref, t_ref, g_ref, dx_ref, dg_ref, l_ref):
        i = pl.program_id(0)
        tv = t_ref[...]
        l, vjp = jax.vjp(lambda xv, gv: _loss_rows(xv, tv, gv), x_ref[...], g_ref[...])
        dx, dg = vjp(jnp.ones_like(l))
        dx_ref[...] = dx
        tot = jnp.zeros((1, 128), f32) + jnp.sum(l)

        @pl.when(i == 0)
        def _():
            dg_ref[...] = dg
            l_ref[...] = tot

        @pl.when(i > 0)
        def _():
            dg_ref[...] += dg
            l_ref[...] += tot

    return pl.pallas_call(
        body, grid=(T // tm,),
        in_specs=[_row_spec(tm, W), _row_spec(tm, W), _full_spec(g.shape)],
        out_specs=[_row_spec(tm, W), _full_spec(g.shape), _full_spec((1, 128))],
        out_shape=[jax.ShapeDtypeStruct((T, W), f32), jax.ShapeDtypeStruct(g.shape, f32),
                   jax.ShapeDtypeStruct((1, 128), f32)], name="loss_head",
        compiler_params=_params("arbitrary"))(x2, tgt, g)


def _local_step(x, tgt, W, late=None):
    row = lambda a: a.reshape(1, -1)
    wp = W['w_in_pad']
    w_rwkv, w_qkv, w_z = wp[:, :OFF_QKV], wp[:, OFF_QKV:OFF_Z], wp[:, OFF_Z:OFF_GATES]
    w_gates, w_ab = wp[:, OFF_GATES:OFF_AB], wp[:, OFF_AB:]
    mu = row(W['rwkv_mu'])
    mixw = jnp.concatenate([mu, 1.0 - mu], axis=0)
    zpad = jnp.zeros((64, RWKV_W), f32)
    w2p = jnp.concatenate([W['rwkv_w2'], zpad], axis=0)
    a2p = jnp.concatenate([zpad, W['rwkv_a2']], axis=0)
    rw_consts = [row(W['rwkv_w0']), w2p, row(W['rwkv_a0']), a2p, W['rwkv_g2'], row(W['rwkv_k_k']), row(W['rwkv_k_a'])]
    post_consts = [row(W['rwkv_ln_w']), row(W['rwkv_ln_b']), row(W['rwkv_r_k'])]
    pad4 = lambda a: jnp.pad(row(a), ((0, 0), (0, W_AB - GDN_HEADS)))
    gd_consts = [pad4(W['gdn_a_log']), pad4(W['gdn_dt_bias'])]
    nw_t = jnp.tile(row(W['gdn_norm_w']), (1, GDN_HEADS))
    g1, g2n, gf = row(W['norm1_g']), row(W['norm2_g']), row(W['final_g'])

    (u,) = _pw_fwd("norm1", _rms_fn, [x], [g1], [D_MODEL], 256, out_dtype=bf16)
    p_rwkv = _mm(u, w_rwkv, 'nn', "in_rwkv")
    qkv_raw = _mm(u, w_qkv, 'nn', "in_qkv")
    z = _mm(u, w_z, 'nn', "in_z")
    gates = _mm(u, w_gates, 'nn', "in_gates")
    ab = _mm(u, w_ab, 'nn', "in_ab")

    r, lw, k2, v, a_, b_, g = _pw_fwd("rwkv_prep", _rwkv_prep_fn, [p_rwkv], rw_consts, [RWKV_W] * 7, 256, conv_w=mixw)
    wkv_in = [r, lw, k2, v, a_, b_]
    y, zs_wkv, *gathered = _scan_fwd("wkv_fwd", (_wkv_prep, _wkv_step), wkv_in, WKV_CHUNK, RWKV_HEADS, 2 * RWKV_HD, RWKV_W, WKV_PER_STEP,
                                     side=None if late is None else (late['shards'][0], False))
    if late is not None:
        W = dict(W, **late['assemble'](0, gathered))
    (ya_in,) = _pw_fwd("rwkv_post", _rwkv_post_fn, [y, r, k2, v, g], post_consts, [RWKV_W], 256, out_dtype=bf16, strip=128)
    ya = _mm(ya_in, W['rwkv_proj'], 'nn', "rwkv_proj")

    gq, gk, gv, gbeta = _pw_fwd("gdn_prep", _gdn_prep_fn, [qkv_raw, ab], gd_consts, [GDN_W] * 3 + [W_AB], 256,
                                conv_w=W['gdn_conv_w'])
    gdn_in = [gq, gk, gv, gbeta]
    o, zs_gdn, *gathered = _scan_fwd("gdn_fwd", (_gdn_prep, _gdn_step), gdn_in, GDN_CHUNK, GDN_HEADS, GDN_HD, GDN_W, GDN_PER_STEP,
                                     side=None if late is None else (late['shards'][1], False))
    if late is not None:
        W = dict(W, **late['assemble'](1, gathered))
    (yb_in,) = _pw_fwd("gdn_post", _gdn_post_fn, [o, z], [nw_t], [GDN_W], 256, out_dtype=bf16, strip=128)
    yb = _mm(yb_in, W['gdn_proj'], 'nn', "gdn_proj")

    ga, gb = _cols(gates, D_MODEL, 0), _cols(gates, D_MODEL, 1)
    (mixed,) = _pw_fwd("mix", _mix_fn, [ga, gb, ya, yb], [], [D_MODEL], 256, out_dtype=bf16, strip=256)
    x1 = _mm(mixed, W['w_out'], 'nn', "w_out", add=x)
    (u2,) = _pw_fwd("norm2", _rms_fn, [x1], [g2n], [D_MODEL], 256, out_dtype=bf16)
    h = _mm(u2, W['ffn_up'], 'nn', "ffn_up")
    act = _ffn_act_fwd(h, W['ffn_conv_w'], 256)
    x2 = _mm(act, W['ffn_down'], 'nn', "ffn_down", add=x1)

    G = {}
    slab_out = None if late is None else N_POS
    dx2, dgf, loss = _loss_call(x2, tgt, gf, 256)
    G['final_g'] = dgf
    dact = _mm(dx2, W['ffn_down'], 'nt', "d_act")
    G['ffn_down'] = _mm(act, dx2, 'tn', "g_ffn_down", out_dtype=bf16)
    dh, G['ffn_conv_w'] = _ffn_act_bwd(h, dact, W['ffn_conv_w'], 128)
    du2 = _mm(dh, W['ffn_up'], 'nt', "d_u2")
    G['ffn_up'] = _mm(u2, dh, 'tn', "g_ffn_up", out_dtype=bf16, col_slabs=slab_out)
    (dx1,), (G['norm2_g'],) = _pw_bwd("norm2_bwd", _rms_fn, [x1], [g2n], [(du2,)], 256, add_to_first=dx2)
    dmixed = _mm(dx1, W['w_out'], 'nt', "d_mixed")
    G['w_out'] = _mm(mixed, dx1, 'tn', "g_w_out", out_dtype=bf16)
    (dga, dgb, dya, dyb), _ = _pw_bwd("mix_bwd", _mix_fn, [ga, gb, ya, yb], [], [(dmixed,)], 256, row_dtypes=[bf16] * 4,
                                      strip=256)
    dya_in = _mm(dya, W['rwkv_proj'], 'nt', "d_ya_in")
    G['rwkv_proj'] = _mm(ya_in, dya, 'tn', "g_rwkv_proj", out_dtype=bf16, col_slabs=slab_out)
    dyb_in = _mm(dyb, W['gdn_proj'], 'nt', "d_yb_in")
    G['gdn_proj'] = _mm(yb_in, dyb, 'tn', "g_gdn_proj", out_dtype=bf16, col_slabs=slab_out)

    (do, dz), (dnw_t,) = _pw_bwd("gdn_post_bwd", _gdn_post_fn, [o, z], [nw_t], [(dyb_in,)], 256, row_dtypes=[f32, bf16],
                                 strip=128)
    G['gdn_norm_w'] = dnw_t.reshape(GDN_HEADS, GDN_HD).sum(axis=0)
    dgq, dgk, dgv, dgbeta = _scan_bwd("gdn_bwd", (_gdn_prep, _gdn_step), gdn_in, do, zs_gdn, GDN_CHUNK, GDN_PER_STEP)
    (dqkv_raw, dab), G['gdn_conv_w'], (dal_p, ddt_p) = _pw_conv_bwd(
        "gdn_prep_bwd", _gdn_prep_fn, [qkv_raw, ab], gd_consts, [(dgq,), (dgk,), (dgv,), (dgbeta,)], W['gdn_conv_w'], 256,
        row_dtypes=[bf16, bf16])
    G['gdn_a_log'], G['gdn_dt_bias'] = dal_p[0, :GDN_HEADS], ddt_p[0, :GDN_HEADS]

    (dy, dr1, dk21, dv1, dg_), (G['rwkv_ln_w'], G['rwkv_ln_b'], G['rwkv_r_k']) = _pw_bwd(
        "rwkv_post_bwd", _rwkv_post_fn, [y, r, k2, v, g], post_consts, [(dya_in,)], 256, strip=128)
    dr2, dlw, dk22, dv2, da_, db_, *G['_arrived'] = _scan_bwd(
        "wkv_bwd", (_wkv_prep, _wkv_step), wkv_in, dy, zs_wkv, WKV_CHUNK, WKV_PER_STEP, side=None if late is None else (late['slabs'](G), True))
    (dp_rwkv,), dmixw, rw_grads = _pw_conv_bwd(
        "rwkv_prep_bwd", _rwkv_prep_fn, [p_rwkv], rw_consts,
        [(dr1, dr2), (dlw,), (dk21, dk22), (dv1, dv2), (da_,), (db_,), (dg_,)], mixw, 256, row_dtypes=[bf16])
    G['rwkv_w0'], dw2p, G['rwkv_a0'], da2p, G['rwkv_g2'], G['rwkv_k_k'], G['rwkv_k_a'] = rw_grads
    G['rwkv_w2'], G['rwkv_a2'] = dw2p[:64], da2p[64:]
    G['rwkv_mu'] = dmixw[0] - dmixw[1]

    dp = jnp.concatenate([dp_rwkv, dqkv_raw, dz, dga, dgb, dab], axis=1)
    G['w_in_pad'] = _mm(u, dp, 'tn', "g_w_in", out_dtype=bf16)
    if late is None:
        du = _mm(dp, wp, 'nt', "d_u")
    else:
        du, *G['_arrived_w_in'] = _mm(dp, wp, 'nt', "d_u", side=(late['w_in_slabs'](G), True))
    (dx,), (G['norm1_g'],) = _pw_bwd("norm1_bwd", _rms_fn, [x], [g1], [(du,)], 256, add_to_first=dx1)
    return loss, dx, G


IN_WIDTH = OFF_AB + 8
PAD_ORDER = ((0, OFF_GATES), (OFF_GATES + 8, IN_WIDTH), (OFF_GATES, OFF_GATES + 8))


def _pad_w_in(w):
    return jnp.concatenate([w[:, a:b] for a, b in PAD_ORDER] + [jnp.zeros((w.shape[0], W_AB - 8), w.dtype)], axis=1)


def _pad_w_in_shards(shards):
    width = shards[0].shape[1]
    parts = []
    for a, b in PAD_ORDER:
        for j, sh in enumerate(shards):
            lo, hi = max(a, j * width), min(b, (j + 1) * width)
            if lo < hi:
                parts.append(sh[:, lo - j * width:hi - j * width])
    return jnp.concatenate(parts + [jnp.zeros((shards[0].shape[0], W_AB - 8), shards[0].dtype)], axis=1)


def _unpad_cols(wp, lo, hi):
    parts, off = [], 0
    for a, b in PAD_ORDER:
        l, h = max(a, lo), min(b, hi)
        if l < h:
            parts.append((l, wp[:, off + l - a:off + h - a]))
        off += b - a
    parts.sort(key=lambda t: t[0])
    return parts[0][1] if len(parts) == 1 else jnp.concatenate([p for _, p in parts], axis=1)


def _unpad_w_in(wp):
    return _unpad_cols(wp, 0, IN_WIDTH)


BIG = ('w_in', 'rwkv_proj', 'gdn_proj', 'w_out', 'ffn_up', 'ffn_down')
SMALL_SHARDED = ('rwkv_w2', 'rwkv_a2', 'rwkv_g2', 'gdn_conv_w', 'ffn_conv_w')


def _rows128(shape):
    n = 1
    for d in shape:
        n *= d
    return -(-n // LANES)


def _pack128(arrays):
    parts = []
    for a in arrays:
        flat = a.reshape(-1)
        rows = _rows128(a.shape)
        parts.append(jnp.pad(flat, (0, rows * LANES - flat.shape[0])).reshape(rows, LANES))
    buf = jnp.concatenate(parts, axis=0)
    return jnp.pad(buf, ((0, -buf.shape[0] % HALO), (0, 0)))


def _unpack128(buf, shapes):
    out, off = [], 0
    for s in shapes:
        rows, n = _rows128(s), 1
        for d in s:
            n *= d
        out.append(buf[off:off + rows].reshape(-1)[:n].reshape(s))
        off += rows
    return out


def _param_tile(r, c):
    best = None
    for d in range(2 * HALO, r + 1, 2 * HALO):
        if r % d == 0 and d * c * 4 <= TILE_BYTES:
            best = d
    if best is not None or r * c * 4 <= TILE_BYTES:
        return (best if best is not None else r), c
    return r, 128


def _xy_exchange(name, bufs, scatter):
    n = len(bufs)

    def body(*refs):
        start, finish = _xy_copies(refs[:n], refs[n:2 * n], refs[2 * n:], scatter)
        start()
        finish()

    return pl.pallas_call(
        body, in_specs=[pl.BlockSpec(memory_space=pl.ANY)] * n, out_specs=[pl.BlockSpec(memory_space=pl.ANY)] * n,
        out_shape=_xy_out_shapes(bufs, scatter), scratch_shapes=_xy_sems(n, scatter), name=name)(*bufs)


def _sibling_exchange(name, bufs):
    n = len(bufs)

    def body(*refs):
        in_refs, out_refs, send_sems, recv_sems = refs[:n], refs[n:2 * n], refs[2 * n], refs[2 * n + 1]
        x, y, c = lax.axis_index("x"), lax.axis_index("y"), lax.axis_index("c")
        copies = [pltpu.make_async_remote_copy(
            src_ref=in_refs[a], dst_ref=out_refs[a], send_sem=send_sems.at[a], recv_sem=recv_sems.at[a],
            device_id=(x, y, 1 - c), device_id_type=pl.DeviceIdType.MESH) for a in range(n)]
        for cp in copies:
            cp.start()
        for cp in copies:
            cp.wait()

    return pl.pallas_call(
        body, in_specs=[pl.BlockSpec(memory_space=pl.ANY)] * n, out_specs=[pl.BlockSpec(memory_space=pl.ANY)] * n,
        out_shape=[jax.ShapeDtypeStruct(b.shape, b.dtype) for b in bufs],
        scratch_shapes=[pltpu.SemaphoreType.DMA((n,)), pltpu.SemaphoreType.DMA((n,))], name=name)(*bufs)


def _sum_slots(name, buf):
    _, R, L = buf.shape
    tr, tc = _param_tile(R, L)

    def body(b_ref, o_ref):
        part = lambda s: b_ref[s].astype(f32)
        o_ref[...] = ((part(0) + part(1)) + part(2)) + part(3)

    return pl.pallas_call(
        body, grid=(R // tr, L // tc),
        in_specs=[pl.BlockSpec((N_POS, tr, tc), lambda i, j: (0, i, j))],
        out_specs=pl.BlockSpec((tr, tc), lambda i, j: (i, j)),
        out_shape=jax.ShapeDtypeStruct((R, L), f32), name=name,
        compiler_params=_params("parallel", "parallel"))(buf)


def _adamw(name, w, ga, gb, m, v):
    R, L = w.shape
    tr, tc = _param_tile(R, L)
    c1 = 1.0 / (1.0 - ADAM_B1 ** ADAM_STEP)
    c2 = 1.0 / (1.0 - ADAM_B2 ** ADAM_STEP)

    def body(w_ref, ga_ref, gb_ref, m_ref, v_ref, g_out, d_out, m_out, v_out):
        g = ga_ref[...] + gb_ref[...]
        m_new = ADAM_B1 * m_ref[...] + (1.0 - ADAM_B1) * g
        v_new = ADAM_B2 * v_ref[...] + (1.0 - ADAM_B2) * (g * g)
        g_out[...] = g
        m_out[...] = m_new
        v_out[...] = v_new
        d_out[...] = -ADAM_LR * ((m_new * c1) / (jnp.sqrt(v_new * c2) + ADAM_EPS) + ADAM_WD * w_ref[...])

    spec = pl.BlockSpec((tr, tc), lambda i, j: (i, j))
    return pl.pallas_call(
        body, grid=(R // tr, L // tc), in_specs=[spec] * 5, out_specs=[spec] * 4,
        out_shape=[jax.ShapeDtypeStruct((R, L), f32)] * 4, name=name,
        compiler_params=_params("parallel", "parallel"))(w, ga, gb, m, v)


def _step(x, loss_target, P, M, V):
    shapes = {n: tuple(P[n].shape) for n in WEIGHTS}
    sh_shapes = [shapes[n] for n in SMALL_SHARDED]
    packed = SMALL_SHARDED + SMALL
    late_names = BIG[1:]

    def whole(n, g):
        return g.reshape(-1, g.shape[2]) if n in ROW_SHARDED else jnp.concatenate([g[j] for j in range(N_POS)], axis=1)

    def slabs(G, n, dtype=f32):
        r, c = shapes[n]
        full = G[n].astype(dtype)
        if full.ndim == 3:
            return full
        return full.reshape(N_POS, r, c) if n in ROW_SHARDED else full.reshape(r, N_POS, c).transpose(1, 0, 2)

    g_w_in, g_small = _xy_exchange("gather_w_in", [P['w_in'].astype(bf16), _pack128([P[n] for n in SMALL_SHARDED])],
                                   scatter=False)
    W = {n: P[n] for n in SMALL}
    W['w_in_pad'] = _pad_w_in_shards([g_w_in[j] for j in range(N_POS)])
    per_pos = [_unpack128(g_small[j], sh_shapes) for j in range(N_POS)]
    for q, n in enumerate(SMALL_SHARDED):
        W[n] = jnp.concatenate([per_pos[j][q] for j in range(N_POS)], axis=1)
    groups = (('rwkv_proj', 'gdn_proj', 'ffn_up'), ('w_out', 'ffn_down'))
    late = dict(shards=[[P[n].astype(bf16) for n in grp] for grp in groups],
                assemble=lambda q, gathered: {n: whole(n, g) for n, g in zip(groups[q], gathered)},
                slabs=lambda G: [slabs(G, n, bf16) for n in late_names],
                w_in_slabs=lambda G: [jnp.stack([_unpad_cols(G['w_in_pad'], j * shapes['w_in'][1], (j + 1) * shapes['w_in'][1])
                                                 for j in range(N_POS)])])

    loss_rows, dx, G = _local_step(x, loss_target, W, late)
    arrived_late = G.pop('_arrived')
    (arrived_w_in,) = G.pop('_arrived_w_in')
    G.pop('w_in_pad')

    small_slabs = jnp.stack([_pack128([slabs(G, n)[j] for n in SMALL_SHARDED] + [G[n] for n in SMALL]) for j in range(N_POS)])
    (arrived_small,) = _xy_exchange("scatter_small", [small_slabs], scatter=True)
    contributions = [arrived_w_in] + list(arrived_late) + [arrived_small]
    tags = list(BIG) + ['small']
    plane = [_sum_slots("sum_" + t, cbuf) for t, cbuf in zip(tags, contributions)]
    sibling = _sibling_exchange("sibling_grads", plane)

    out = {}
    names4 = ('grad', 'delta', 'new_m', 'new_v')
    for q, n in enumerate(BIG):
        for tag, t in zip(names4, _adamw("adamw_" + n, P[n], plane[q], sibling[q], M[n], V[n])):
            out[tag + '_' + n] = t
    small_out = _adamw("adamw_small", _pack128([P[n] for n in packed]), plane[-1], sibling[-1],
                       _pack128([M[n] for n in packed]), _pack128([V[n] for n in packed]))
    for tag, buf in zip(names4, small_out):
        for n, t in zip(packed, _unpack128(buf, [shapes[n] for n in packed])):
            out[tag + '_' + n] = t
    loss = lax.psum(loss_rows[0, 0], ("x", "y", "c"))
    return loss, dx, out


def kernel(x, norm1_g, w_in, rwkv_mu, rwkv_w0, rwkv_w2, rwkv_a0, rwkv_a2, rwkv_g2, rwkv_k_k, rwkv_k_a, rwkv_r_k, rwkv_ln_w, rwkv_ln_b, rwkv_proj, gdn_conv_w, gdn_a_log, gdn_dt_bias, gdn_norm_w, gdn_proj, w_out, norm2_g, ffn_up, ffn_conv_w, ffn_down, final_g, loss_target, m_norm1_g, m_w_in, m_rwkv_mu, m_rwkv_w0, m_rwkv_w2, m_rwkv_a0, m_rwkv_a2, m_rwkv_g2, m_rwkv_k_k, m_rwkv_k_a, m_rwkv_r_k, m_rwkv_ln_w, m_rwkv_ln_b, m_rwkv_proj, m_gdn_conv_w, m_gdn_a_log, m_gdn_dt_bias, m_gdn_norm_w, m_gdn_proj, m_w_out, m_norm2_g, m_ffn_up, m_ffn_conv_w, m_ffn_down, m_final_g, v_norm1_g, v_w_in, v_rwkv_mu, v_rwkv_w0, v_rwkv_w2, v_rwkv_a0, v_rwkv_a2, v_rwkv_g2, v_rwkv_k_k, v_rwkv_k_a, v_rwkv_r_k, v_rwkv_ln_w, v_rwkv_ln_b, v_rwkv_proj, v_gdn_conv_w, v_gdn_a_log, v_gdn_dt_bias, v_gdn_norm_w, v_gdn_proj, v_w_out, v_norm2_g, v_ffn_up, v_ffn_conv_w, v_ffn_down, v_final_g):
    weights = (norm1_g, w_in, rwkv_mu, rwkv_w0, rwkv_w2, rwkv_a0, rwkv_a2, rwkv_g2, rwkv_k_k, rwkv_k_a, rwkv_r_k, rwkv_ln_w,
               rwkv_ln_b, rwkv_proj, gdn_conv_w, gdn_a_log, gdn_dt_bias, gdn_norm_w, gdn_proj, w_out, norm2_g, ffn_up,
               ffn_conv_w, ffn_down, final_g)
    m_in = (m_norm1_g, m_w_in, m_rwkv_mu, m_rwkv_w0, m_rwkv_w2, m_rwkv_a0, m_rwkv_a2, m_rwkv_g2, m_rwkv_k_k, m_rwkv_k_a,
            m_rwkv_r_k, m_rwkv_ln_w, m_rwkv_ln_b, m_rwkv_proj, m_gdn_conv_w, m_gdn_a_log, m_gdn_dt_bias, m_gdn_norm_w,
            m_gdn_proj, m_w_out, m_norm2_g, m_ffn_up, m_ffn_conv_w, m_ffn_down, m_final_g)
    v_in = (v_norm1_g, v_w_in, v_rwkv_mu, v_rwkv_w0, v_rwkv_w2, v_rwkv_a0, v_rwkv_a2, v_rwkv_g2, v_rwkv_k_k, v_rwkv_k_a,
            v_rwkv_r_k, v_rwkv_ln_w, v_rwkv_ln_b, v_rwkv_proj, v_gdn_conv_w, v_gdn_a_log, v_gdn_dt_bias, v_gdn_norm_w,
            v_gdn_proj, v_w_out, v_norm2_g, v_ffn_up, v_ffn_conv_w, v_ffn_down, v_final_g)
    drop = lambda n, a: a if n == 'final_g' else a[0]
    P = {n: drop(n, a) for n, a in zip(WEIGHTS, weights)}
    M = {n: drop(n, a) for n, a in zip(WEIGHTS, m_in)}
    V = {n: drop(n, a) for n, a in zip(WEIGHTS, v_in)}
    loss, dx, out = _step(x[0], loss_target[0], P, M, V)
    lift = lambda n, a: a if n == 'final_g' else a[None]
    res = [loss, dx[None]]
    for tag in ('grad', 'delta', 'new_m', 'new_v'):
        res += [lift(n, out[tag + '_' + n]) for n in WEIGHTS]
    return tuple(res)
```

```python
import functools

import jax
import jax.numpy as jnp
from jax import lax
from jax.experimental import pallas as pl
from jax.experimental.pallas import tpu as pltpu

f32 = jnp.float32
bf16 = jnp.bfloat16
HI = lax.Precision.HIGHEST

D_MODEL = 1024
RWKV_HEADS, RWKV_HD, RWKV_W = 8, 64, 512
GDN_HEADS, GDN_HD, GDN_W = 4, 128, 512
FFN_H = 2816
NORM_EPS, L2_EPS, GN_EPS = 1e-6, 1e-6, 64e-5
W_AB = 256
OFF_QKV, OFF_Z, OFF_GATES, OFF_AB = 1792, 3328, 3840, 5888
W_IN_PAD = OFF_AB + W_AB
WKV_CHUNK, WKV_PER_STEP = 64, 4
GDN_CHUNK, GDN_PER_STEP = 128, 4
HALO = 8
LANES = 128
TILE_BYTES = 1 << 20
VMEM_LIMIT = 56 * 1024 * 1024

ADAM_LR, ADAM_B1, ADAM_B2, ADAM_EPS, ADAM_WD, ADAM_STEP = 0.001, 0.9, 0.999, 1e-08, 0.01, 10

ROW_SHARDED = ('w_out', 'ffn_down')
SMALL = ('norm1_g', 'rwkv_mu', 'rwkv_w0', 'rwkv_a0', 'rwkv_k_k', 'rwkv_k_a', 'rwkv_r_k', 'rwkv_ln_w', 'rwkv_ln_b',
         'gdn_a_log', 'gdn_dt_bias', 'gdn_norm_w', 'norm2_g', 'final_g')
WEIGHTS = ('norm1_g', 'w_in', 'rwkv_mu', 'rwkv_w0', 'rwkv_w2', 'rwkv_a0', 'rwkv_a2', 'rwkv_g2', 'rwkv_k_k', 'rwkv_k_a',
           'rwkv_r_k', 'rwkv_ln_w', 'rwkv_ln_b', 'rwkv_proj', 'gdn_conv_w', 'gdn_a_log', 'gdn_dt_bias', 'gdn_norm_w',
           'gdn_proj', 'w_out', 'norm2_g', 'ffn_up', 'ffn_conv_w', 'ffn_down', 'final_g')


def _params(*sem):
    return pltpu.CompilerParams(dimension_semantics=sem, vmem_limit_bytes=VMEM_LIMIT)


def _tile(n, limit):
    if n <= limit:
        return n
    best = None
    for d in range(128, limit + 1, 128):
        if n % d == 0:
            best = d
    if best is None:
        raise ValueError(f"no tile for {n} under {limit}")
    return best


MM_BLOCK_BYTES = 6 << 20


def _mm(a, b, mode, name, add=None, out_dtype=f32, side=None, col_slabs=None):
    if mode == 'nn':
        (M, K), N = a.shape, b.shape[1]
    elif mode == 'nt':
        (M, K), N = a.shape, b.shape[0]
    else:
        (K, M), N = a.shape, b.shape[1]
    tm = _tile(M, 1408)
    tk = _tile(K, min(2816, MM_BLOCK_BYTES // (tm * a.dtype.itemsize)))
    tn = _tile(N, max(128, min(MM_BLOCK_BYTES // (tk * b.dtype.itemsize), MM_BLOCK_BYTES // (tm * 4)) // 128 * 128))
    if col_slabs is not None:
        tn = N // col_slabs
    nk = K // tk
    grid = (M // tm, N // tn, nk)
    dn = {'nn': (((1,), (0,)), ((), ())), 'nt': (((1,), (1,)), ((), ())), 'tn': (((0,), (0,)), ((), ()))}[mode]
    n_add = 0 if add is None else 1
    n_side = 0 if side is None else len(side[0])

    def body(a_ref, b_ref, *rest):
        add_ref = rest[0] if add is not None else None
        side_in, rest = rest[n_add:n_add + n_side], rest[n_add + n_side:]
        o_ref, side_out, rest = rest[0], rest[1:1 + n_side], rest[1 + n_side:]
        acc_ref, rest = (rest[0], rest[1:]) if nk > 1 else (None, rest)
        ids = [pl.program_id(d) for d in range(3)]
        if side is not None:
            start, finish = _xy_copies(side_in, side_out, rest, side[1])
            pl.when((ids[0] == 0) & (ids[1] == 0) & (ids[2] == 0))(start)
        acc = lax.dot_general(a_ref[...].astype(bf16), b_ref[...].astype(bf16), dn, preferred_element_type=f32)
        if nk == 1:
            o_ref[...] = (acc + add_ref[...] if add is not None else acc).astype(out_dtype)
        else:
            k = ids[2]

            @pl.when(k == 0)
            def _():
                acc_ref[...] = acc + add_ref[...] if add is not None else acc

            @pl.when(k > 0)
            def _():
                acc_ref[...] += acc

            @pl.when(k == nk - 1)
            def _():
                o_ref[...] = acc_ref[...].astype(out_dtype)
        if side is not None:
            pl.when((ids[0] == grid[0] - 1) & (ids[1] == grid[1] - 1) & (ids[2] == nk - 1))(finish)

    a_spec = (pl.BlockSpec((tk, tm), lambda i, j, k: (k, i)) if mode == 'tn'
              else pl.BlockSpec((tm, tk), lambda i, j, k: (i, k)))
    b_spec = (pl.BlockSpec((tn, tk), lambda i, j, k: (j, k)) if mode == 'nt'
              else pl.BlockSpec((tk, tn), lambda i, j, k: (k, j)))
    o_spec = pl.BlockSpec((tm, tn), lambda i, j, k: (i, j))
    o_shape = jax.ShapeDtypeStruct((M, N), out_dtype)
    if col_slabs is not None:
        o_spec = pl.BlockSpec((None, tm, tn), lambda i, j, k: (j, i, 0))
        o_shape = jax.ShapeDtypeStruct((col_slabs, M, tn), out_dtype)
    any_spec = pl.BlockSpec(memory_space=pl.ANY)
    side_bufs = [] if side is None else list(side[0])
    ins, specs = [a, b], [a_spec, b_spec]
    if add is not None:
        ins.append(add)
        specs.append(o_spec)
    outs = pl.pallas_call(
        body, grid=grid, in_specs=specs + [any_spec] * n_side, out_specs=[o_spec] + [any_spec] * n_side,
        out_shape=[o_shape] + (_xy_out_shapes(side_bufs, side[1]) if side is not None else []),
        scratch_shapes=([pltpu.VMEM((tm, tn), f32)] if nk > 1 else []) + (_xy_sems(n_side, side[1]) if side is not None else []),
        name=name,
        compiler_params=_params(*(("arbitrary",) * 3 if side is not None else ("parallel", "parallel", "arbitrary"))))(
            *ins, *side_bufs)
    return list(outs) if side is not None else outs[0]


def _shift_down(cur, prev, s):
    if s == 0:
        return cur
    ext = jnp.concatenate([prev, cur], axis=0)
    return pltpu.roll(ext, s, 0)[HALO:]


def _shift_up(cur, nxt, s):
    if s == 0:
        return cur
    ext = jnp.concatenate([cur, nxt], axis=0)
    return pltpu.roll(ext, ext.shape[0] - s, 0)[:cur.shape[0]]


def _conv_apply(cur, prev, w_ref, shifted=None):
    taps = w_ref.shape[0]
    out = None
    for i in range(taps):
        s = taps - 1 - i
        term = (shifted[s] if shifted is not None else _shift_down(cur, prev, s)) * w_ref[pl.ds(i, 1), :]
        out = term if out is None else out + term
    return out


def _row_spec(tm, w, col=0):
    return pl.BlockSpec((tm, w), lambda i: (i, col))


def _cols(a, width, col):
    return (a, width, col)


def _row_of(r):
    return r if isinstance(r, tuple) else (r, r.shape[1], 0)


def _prev_spec(tm, w):
    return pl.BlockSpec((HALO, w), lambda i: (jnp.maximum(i * (tm // HALO) - 1, 0), 0))


def _next_spec(tm, w, T):
    return pl.BlockSpec((HALO, w), lambda i: (jnp.minimum((i + 1) * (tm // HALO), T // HALO - 1), 0))


def _full_spec(shape):
    return pl.BlockSpec(shape, lambda i: (0,) * len(shape))


def _pw_fwd(name, fn, rows, consts, out_widths, tm, conv_w=None, out_dtype=f32, strip=None):
    T = _row_of(rows[0])[0].shape[0]
    nr, nc = len(rows), len(consts)

    def body(*refs):
        i = pl.program_id(0)
        if strip is not None:
            for j in range(out_widths[0] // strip):
                sl = slice(strip * j, strip * (j + 1))
                outs = fn(*[r[:, sl] for r in refs[:nr + nc]])
                for o_ref, o in zip(refs[nr + nc:], outs):
                    o_ref[:, sl] = o.astype(out_dtype)
            return
        vals = [r[...] for r in refs[:nr]]
        p = nr
        if conv_w is not None:
            prev = jnp.where(i > 0, refs[p][...], 0.0)
            vals[0] = _conv_apply(vals[0], prev, refs[p + 1])
            p += 2
        cvals = [r[...] for r in refs[p:p + nc]]
        outs = fn(*vals, *cvals)
        for o_ref, o in zip(refs[p + nc:], outs):
            o_ref[...] = o.astype(out_dtype)

    ins = [_row_of(r)[0] for r in rows]
    specs = [_row_spec(tm, *_row_of(r)[1:]) for r in rows]
    if conv_w is not None:
        ins += [rows[0], conv_w]
        specs += [_prev_spec(tm, rows[0].shape[1]), _full_spec(conv_w.shape)]
    ins += list(consts)
    specs += [_full_spec(c.shape) for c in consts]
    outs = pl.pallas_call(
        body, grid=(T // tm,), in_specs=specs,
        out_specs=[_row_spec(tm, w) for w in out_widths],
        out_shape=[jax.ShapeDtypeStruct((T, w), out_dtype) for w in out_widths], name=name,
        compiler_params=_params("parallel"))(*ins)
    return outs


def _pw_bwd(name, fn, rows, consts, cots, tm, add_to_first=None, row_dtypes=None, strip=None):
    rows = [_row_of(r) for r in rows]
    T = rows[0][0].shape[0]
    nr, nc = len(rows), len(consts)
    flat_cots = [c for grp in cots for c in grp]
    row_dtypes = row_dtypes or [f32] * nr
    n_extra = 0 if add_to_first is None else 1
    width = rows[0][1]

    def body(*refs):
        i = pl.program_id(0)
        in_refs, cot_refs = refs[:nr + nc], refs[nr + nc:nr + nc + len(flat_cots)]
        extra_ref = refs[nr + nc + len(flat_cots)] if add_to_first is not None else None
        row_out = refs[nr + nc + len(flat_cots) + n_extra:][:nr]
        const_out = refs[nr + nc + len(flat_cots) + n_extra + nr:]

        @pl.when(i == 0)
        def _():
            for q in range(nc):
                const_out[q][...] = jnp.zeros_like(const_out[q])

        def part(sl):
            cot_vals, p = [], 0
            for grp in cots:
                acc = cot_refs[p][:, sl]
                for q in range(1, len(grp)):
                    acc = acc + cot_refs[p + q][:, sl]
                p += len(grp)
                cot_vals.append(acc)
            _, vjp = jax.vjp(fn, *[r[:, sl] for r in in_refs])
            grads = vjp(tuple(cot_vals))
            for q in range(nr):
                g = grads[q]
                if q == 0 and extra_ref is not None:
                    g = g + extra_ref[:, sl]
                row_out[q][:, sl] = g.astype(row_dtypes[q])
            for q in range(nc):
                const_out[q][:, sl] += grads[nr + q]

        if strip is None:
            part(slice(None))
        else:
            for j in range(width // strip):
                part(slice(strip * j, strip * (j + 1)))

    ins = [r[0] for r in rows] + list(consts) + flat_cots
    specs = ([_row_spec(tm, r[1], r[2]) for r in rows] + [_full_spec(c.shape) for c in consts]
             + [_row_spec(tm, c.shape[1]) for c in flat_cots])
    if add_to_first is not None:
        ins.append(add_to_first)
        specs.append(_row_spec(tm, add_to_first.shape[1]))
    out_shapes = ([jax.ShapeDtypeStruct((T, r[1]), d) for r, d in zip(rows, row_dtypes)]
                  + [jax.ShapeDtypeStruct(c.shape, f32) for c in consts])
    out_specs = [_row_spec(tm, r[1]) for r in rows] + [_full_spec(c.shape) for c in consts]
    outs = pl.pallas_call(
        body, grid=(T // tm,), in_specs=specs, out_specs=out_specs, out_shape=out_shapes, name=name,
        compiler_params=_params("arbitrary"))(*ins)
    return list(outs[:nr]), list(outs[nr:])


def _sigmoid(x):
    return 0.5 * jnp.tanh(0.5 * x) + 0.5


def _softplus(x):
    return jnp.maximum(x, 0.0) + jnp.log(1.0 + jnp.exp(jnp.minimum(x, -x)))


def _seg_sum_impl(x, seg):
    w = x.shape[-1]
    r = lax.broadcasted_iota(jnp.int32, (w, w), 0) // seg
    c = lax.broadcasted_iota(jnp.int32, (w, w), 1) // seg
    ones = (r == c).astype(bf16)
    hi = x.astype(bf16)
    lo = (x - hi.astype(f32)).astype(bf16)
    return (jnp.dot(hi, ones, preferred_element_type=f32) + jnp.dot(lo, ones, preferred_element_type=f32))


@functools.partial(jax.custom_vjp, nondiff_argnums=(1,))
def _seg_sum(x, seg):
    return _seg_sum_impl(x, seg)


_seg_sum.defvjp(lambda x, seg: (_seg_sum_impl(x, seg), None), lambda seg, _, g: (_seg_sum_impl(g, seg),))


def _rms(x, g):
    return x * lax.rsqrt(jnp.mean(x * x, axis=-1, keepdims=True) + NORM_EPS) * g


def _rms_fn(x, g):
    return (_rms(x, g),)


def _loss_rows(x2, tgt, g):
    e = _rms(x2, g) - tgt
    return 0.5 * jnp.sum(e * e, axis=-1, keepdims=True) * (1.0 / D_MODEL)


def _rwkv_prep_fn(r, k, v, wa, gl, w0, w2p, a0, a2p, g2, k_k, k_a):
    z = w0 + jnp.dot(jnp.tanh(wa), w2p, precision=HI, preferred_element_type=f32)
    w_log = -_softplus(-z) - 0.5
    lw = -jnp.exp(w_log)
    a = _sigmoid(a0 + jnp.dot(wa, a2p, precision=HI, preferred_element_type=f32))
    g = jnp.dot(_sigmoid(gl), g2, precision=HI, preferred_element_type=f32)
    kx = k * k_k
    kk = kx * lax.rsqrt(_seg_sum(kx * kx, RWKV_HD) + L2_EPS)
    k2 = k * (1.0 + (a - 1.0) * k_a)
    return r, lw, k2, v, -kk, kk * a, g


def _rwkv_post_fn(y, r, k2, v, g, ln_w, ln_b, rk):
    mean = _seg_sum(y, RWKV_HD) * (1.0 / RWKV_HD)
    yc = y - mean
    var = _seg_sum(yc * yc, RWKV_HD) * (1.0 / RWKV_HD)
    yn = yc * lax.rsqrt(var + GN_EPS) * ln_w + ln_b
    bonus = _seg_sum(r * k2 * rk, RWKV_HD) * v
    return ((yn + bonus) * g,)


def _gdn_prep_fn(cq, ck, cv):
    silu = lambda c: c * _sigmoid(c)
    q, k = silu(cq), silu(ck)
    q = q * lax.rsqrt(jnp.sum(q * q, axis=-1, keepdims=True) + L2_EPS) * (GDN_HD ** -0.5)
    k = k * lax.rsqrt(jnp.sum(k * k, axis=-1, keepdims=True) + L2_EPS)
    return q, k, silu(cv)


def _gdn_gate_fn(ab, al_p, dt_p):
    lane = lax.broadcasted_iota(jnp.int32, ab.shape, 1)
    gpart = -jnp.exp(al_p) * _softplus(ab + dt_p)
    return (jnp.where(lane < GDN_HEADS, gpart, jnp.where(lane < 2 * GDN_HEADS, _sigmoid(ab), 0.0)),)


def _gdn_post_fn(o, z, nw):
    ms = _seg_sum(o * o, GDN_HD) * (1.0 / GDN_HD)
    return (o * lax.rsqrt(ms + NORM_EPS) * nw * (z * _sigmoid(z)),)


def _mix_fn(ga, gb, ya, yb):
    return (_sigmoid(ga) * ya + _sigmoid(gb) * yb,)


STRIP = 128


def _strip_conv(ref, prev_ref, w_ref, sl, first, taps):
    cur = ref[:, sl]
    prev = jnp.where(first, 0.0, prev_ref[:, sl])
    down = [_shift_down(cur, prev, s) for s in range(taps)]
    conv = None
    for k in range(taps):
        term = down[taps - 1 - k] * w_ref[pl.ds(k, 1), sl]
        conv = term if conv is None else conv + term
    return cur, down, conv


def _group_fwd(name, fn, x, w, shared_cols, group_cols, consts, n_out, tm):
    T, W = x.shape
    taps = w.shape[0]
    n_groups = len(group_cols)
    nc = len(consts)

    def body(x_ref, xp_ref, w_ref, *refs):
        const_refs, out_refs = refs[:nc], refs[nc:]
        first = pl.program_id(0) == 0
        shared = [_strip_conv(x_ref, xp_ref, w_ref, sl, first, taps)[2] for sl in shared_cols]
        for j, cols in enumerate(group_cols):
            sl = slice(STRIP * j, STRIP * (j + 1))
            convs = [_strip_conv(x_ref, xp_ref, w_ref, c, first, taps)[2] for c in cols]
            outs = fn(*convs, *shared, *[c[:, sl] for c in const_refs])
            for o_ref, o in zip(out_refs, outs):
                o_ref[:, sl] = o

    return pl.pallas_call(
        body, grid=(T // tm,),
        in_specs=[_row_spec(tm, W), _prev_spec(tm, W), _full_spec(w.shape)] + [_full_spec(c.shape) for c in consts],
        out_specs=[_row_spec(tm, STRIP * n_groups)] * n_out,
        out_shape=[jax.ShapeDtypeStruct((T, STRIP * n_groups), f32)] * n_out, name=name,
        compiler_params=_params("parallel"))(x, x, w, *consts)


def _group_bwd(name, fn, x, w, shared_cols, group_cols, consts, cots, tm):
    T, W = x.shape
    taps = w.shape[0]
    nblk = T // tm
    nc, ns = len(consts), len(shared_cols)
    flat_cots = [c for grp in cots for c in grp]
    n_cot = len(flat_cots)

    def body(x_ref, xp_ref, xn_ref, w_ref, *refs):
        const_refs, refs = refs[:nc], refs[nc:]
        cot_refs, cotn_refs, refs = refs[:n_cot], refs[n_cot:2 * n_cot], refs[2 * n_cot:]
        dx_ref, dw_ref, const_out = refs[0], refs[1], refs[2:]
        i = pl.program_id(0)
        first, last = i == 0, i == nblk - 1

        @pl.when(first)
        def _():
            dw_ref[...] = jnp.zeros_like(dw_ref)
            for q in range(nc):
                const_out[q][...] = jnp.zeros_like(const_out[q])

        def convs_of(sl):
            cur, down, conv = _strip_conv(x_ref, xp_ref, w_ref, sl, first, taps)
            nxt, conv_n = xn_ref[:, sl], None
            for k in range(taps):
                term = _shift_down(nxt, cur[tm - HALO:], taps - 1 - k) * w_ref[pl.ds(k, 1), sl]
                conv_n = term if conv_n is None else conv_n + term
            return down, conv, conv_n

        def conv_back(sl, down, dc, dc_n):
            dx = None
            for k in range(taps):
                s_ = taps - 1 - k
                term = _shift_up(dc, dc_n, s_) * w_ref[pl.ds(k, 1), sl]
                dx = term if dx is None else dx + term
                dw_ref[pl.ds(k, 1), sl] += jnp.sum(dc * down[s_], axis=0, keepdims=True)
            dx_ref[:, sl] = dx.astype(dx_ref.dtype)

        def summed(refs_, sl, mask):
            out, p = [], 0
            for grp in cots:
                acc = refs_[p][:, sl]
                for t in range(1, len(grp)):
                    acc = acc + refs_[p + t][:, sl]
                p += len(grp)
                out.append(jnp.where(last, 0.0, acc) if mask else acc)
            return tuple(out)

        shared = [convs_of(sl) for sl in shared_cols]
        d_shared, d_shared_n = [None] * ns, [None] * ns
        for j, cols in enumerate(group_cols):
            sl = slice(STRIP * j, STRIP * (j + 1))
            mine = [convs_of(c) for c in cols]
            cj = [c[:, sl] for c in const_refs]
            _, vjp = jax.vjp(fn, *[m[1] for m in mine], *[m[1] for m in shared], *cj)
            grads = vjp(summed(cot_refs, sl, False))
            _, vjp_n = jax.vjp(fn, *[m[2] for m in mine], *[m[2] for m in shared], *cj)
            grads_n = vjp_n(summed(cotn_refs, sl, True))
            for q, c in enumerate(cols):
                conv_back(c, mine[q][0], grads[q], grads_n[q])
            for q in range(ns):
                g, gn = grads[len(cols) + q], grads_n[len(cols) + q]
                d_shared[q] = g if d_shared[q] is None else d_shared[q] + g
                d_shared_n[q] = gn if d_shared_n[q] is None else d_shared_n[q] + gn
            for q in range(nc):
                const_out[q][:, sl] += grads[len(cols) + ns + q]
        for q, c in enumerate(shared_cols):
            conv_back(c, shared[q][0], d_shared[q], d_shared_n[q])

    outs = pl.pallas_call(
        body, grid=(nblk,),
        in_specs=[_row_spec(tm, W), _prev_spec(tm, W), _next_spec(tm, W, T), _full_spec(w.shape)]
        + [_full_spec(c.shape) for c in consts] + [_row_spec(tm, c.shape[1]) for c in flat_cots]
        + [_next_spec(tm, c.shape[1], T) for c in flat_cots],
        out_specs=[_row_spec(tm, W), _full_spec(w.shape)] + [_full_spec(c.shape) for c in consts],
        out_shape=[jax.ShapeDtypeStruct((T, W), bf16), jax.ShapeDtypeStruct(w.shape, f32)]
        + [jax.ShapeDtypeStruct(c.shape, f32) for c in consts], name=name,
        compiler_params=_params("arbitrary"))(x, x, x, w, *consts, *flat_cots, *flat_cots)
    return outs[0], outs[1], list(outs[2:])


def _ffn_strip_fn(cg, cu):
    return cg * _sigmoid(cg) * cu


def _ffn_act_fwd(h, w, tm):
    T, W2 = h.shape
    H = W2 // 2
    taps = w.shape[0]

    def body(h_ref, hp_ref, w_ref, o_ref):
        first = pl.program_id(0) == 0
        for j in range(H // STRIP):
            gs, us = slice(STRIP * j, STRIP * (j + 1)), slice(H + STRIP * j, H + STRIP * (j + 1))
            cg = _strip_conv(h_ref, hp_ref, w_ref, gs, first, taps)[2]
            cu = _strip_conv(h_ref, hp_ref, w_ref, us, first, taps)[2]
            o_ref[:, gs] = _ffn_strip_fn(cg, cu).astype(o_ref.dtype)

    return pl.pallas_call(
        body, grid=(T // tm,), in_specs=[_row_spec(tm, W2), _prev_spec(tm, W2), _full_spec(w.shape)],
        out_specs=_row_spec(tm, H), out_shape=jax.ShapeDtypeStruct((T, H), bf16), name="ffn_act",
        compiler_params=_params("parallel"))(h, h, w)


def _ffn_act_bwd(h, dact, w, tm):
    T, W2 = h.shape
    H = W2 // 2
    taps = w.shape[0]
    nblk = T // tm

    def body(h_ref, hp_ref, hn_ref, d_ref, dn_ref, w_ref, dh_ref, dw_ref):
        i = pl.program_id(0)
        first, last = i == 0, i == nblk - 1

        @pl.when(first)
        def _():
            dw_ref[...] = jnp.zeros_like(dw_ref)

        for j in range(H // STRIP):
            gs, us = slice(STRIP * j, STRIP * (j + 1)), slice(H + STRIP * j, H + STRIP * (j + 1))
            parts = {}
            for name, sl in (('g', gs), ('u', us)):
                cur, down, conv = _strip_conv(h_ref, hp_ref, w_ref, sl, first, taps)
                nxt = hn_ref[:, sl]
                conv_n = None
                for k in range(taps):
                    term = _shift_down(nxt, cur[tm - HALO:], taps - 1 - k) * w_ref[pl.ds(k, 1), sl]
                    conv_n = term if conv_n is None else conv_n + term
                parts[name] = (down, conv, conv_n)
            _, vjp = jax.vjp(_ffn_strip_fn, parts['g'][1], parts['u'][1])
            dcs = vjp(d_ref[:, gs])
            _, vjp_n = jax.vjp(_ffn_strip_fn, parts['g'][2], parts['u'][2])
            dcs_n = vjp_n(jnp.where(last, 0.0, dn_ref[:, gs]))
            for (name, sl), dc, dc_n in zip((('g', gs), ('u', us)), dcs, dcs_n):
                down = parts[name][0]
                dx = None
                for k in range(taps):
                    s_ = taps - 1 - k
                    term = _shift_up(dc, dc_n, s_) * w_ref[pl.ds(k, 1), sl]
                    dx = term if dx is None else dx + term
                    dw_ref[pl.ds(k, 1), sl] += jnp.sum(dc * down[s_], axis=0, keepdims=True)
                dh_ref[:, sl] = dx.astype(dh_ref.dtype)

    return pl.pallas_call(
        body, grid=(nblk,),
        in_specs=[_row_spec(tm, W2), _prev_spec(tm, W2), _next_spec(tm, W2, T), _row_spec(tm, H), _next_spec(tm, H, T),
                  _full_spec(w.shape)],
        out_specs=[_row_spec(tm, W2), _full_spec(w.shape)],
        out_shape=[jax.ShapeDtypeStruct((T, W2), bf16), jax.ShapeDtypeStruct(w.shape, f32)], name="ffn_act_bwd",
        compiler_params=_params("arbitrary"))(h, h, h, dact, dact, w)


N_POS = 4


def _xy_out_shapes(bufs, scatter):
    return [jax.ShapeDtypeStruct((N_POS,) + tuple(b.shape[1:] if scatter else b.shape), b.dtype) for b in bufs]


def _xy_sems(n, scatter):
    sems = [pltpu.SemaphoreType.DMA((3 * n,)), pltpu.SemaphoreType.DMA((3 * n,)), pltpu.SemaphoreType.DMA((n,))]
    return sems if scatter else sems + [pltpu.SemaphoreType.DMA((3 * n,)), pltpu.SemaphoreType.DMA((3 * n,))]


def _xy_copies(in_refs, out_refs, sems, scatter):
    n = len(in_refs)
    send_sems, recv_sems, local_sems = sems[:3]

    def place():
        x, y, c = lax.axis_index("x"), lax.axis_index("y"), lax.axis_index("c")
        return x, y, c, 2 * x + y, [(1 - x, y), (x, 1 - y), (1 - x, 1 - y)]

    def half(ref, a, which):
        rows = in_refs[a].shape[0] // 2
        return ref.at[pl.ds(pl.multiple_of(which * rows, HALO), rows)]

    def ici(a, k, src, dst, peer, c):
        return pltpu.make_async_remote_copy(
            src_ref=src, dst_ref=dst, send_sem=send_sems.at[3 * a + k], recv_sem=recv_sems.at[3 * a + k],
            device_id=(peer[0], peer[1], c), device_id_type=pl.DeviceIdType.MESH)

    def outgoing():
        x, y, c, me, peers = place()
        own = [pltpu.make_async_copy(in_refs[a].at[me] if scatter else in_refs[a], out_refs[a].at[me], local_sems.at[a])
               for a in range(n)]
        if scatter:
            sends = [ici(a, k, in_refs[a].at[2 * p[0] + p[1]], out_refs[a].at[me], p, c)
                     for a in range(n) for k, p in enumerate(peers)]
        else:
            sends = [ici(a, k, half(in_refs[a], a, c), half(out_refs[a].at[me], a, c), p, c)
                     for a in range(n) for k, p in enumerate(peers)]
        return own, sends

    def arrivals():
        x, y, c, me, peers = place()
        if scatter:
            return [ici(a, k, in_refs[a].at[me], out_refs[a].at[2 * p[0] + p[1]], p, c)
                    for a in range(n) for k, p in enumerate(peers)]
        return [ici(a, k, half(in_refs[a], a, c), half(out_refs[a].at[2 * p[0] + p[1]], a, c), p, c)
                for a in range(n) for k, p in enumerate(peers)]

    def to_sibling(mine):
        x, y, c, me, peers = place()
        which = c if mine else 1 - c
        return [pltpu.make_async_remote_copy(
            src_ref=half(out_refs[a].at[2 * p[0] + p[1]], a, which), dst_ref=half(out_refs[a].at[2 * p[0] + p[1]], a, which),
            send_sem=sems[3].at[3 * a + k], recv_sem=sems[4].at[3 * a + k],
            device_id=(x, y, 1 - c), device_id_type=pl.DeviceIdType.MESH) for a in range(n) for k, p in enumerate(peers)]

    def start():
        own, sends = outgoing()
        for cp in own + sends:
            cp.start()

    def finish():
        if scatter:
            for cp in arrivals():
                cp.wait_recv()
        else:
            passed = to_sibling(True)
            for cp, fwd in zip(arrivals(), passed):
                cp.wait_recv()
                fwd.start()
            for cp in to_sibling(False):
                cp.wait_recv()
            for fwd in passed:
                fwd.wait_send()
        own, sends = outgoing()
        for cp in sends:
            cp.wait_send()
        for cp in own:
            cp.wait()

    return start, finish


_NN, _NT, _TN = 'hcs,hsd->hcd', 'hcd,hsd->hcs', 'hcd,hce->hde'


def _lo(spec, a, b):
    return jnp.einsum(spec, a.astype(bf16), b.astype(bf16), preferred_element_type=f32)


@jax.custom_vjp
def _bmm(a, b):
    return _lo(_NN, a, b)


_bmm.defvjp(lambda a, b: (_lo(_NN, a, b), (a, b)), lambda ab, g: (_lo(_NT, g, ab[1]), _lo(_TN, ab[0], g)))


@jax.custom_vjp
def _bmm_nt(a, b):
    return _lo(_NT, a, b)


_bmm_nt.defvjp(lambda a, b: (_lo(_NT, a, b), (a, b)), lambda ab, g: (_lo(_NN, g, ab[1]), _lo(_TN, g, ab[0])))


@jax.custom_vjp
def _bmm_tn(a, b):
    return _lo(_TN, a, b)


_bmm_tn.defvjp(lambda a, b: (_lo(_TN, a, b), (a, b)), lambda ab, g: (_lo(_NT, ab[1], g), _lo(_NN, ab[0], g)))


def _masks(H, C):
    row = lax.broadcasted_iota(jnp.int32, (H, C, C), 1)
    col = lax.broadcasted_iota(jnp.int32, (H, C, C), 2)
    return row, col


def _tri_inv_impl(L):
    H, C, _ = L.shape
    row, col = _masks(H, C)
    eye = (row == col).astype(f32)
    base = 16
    same = (row // base) == (col // base)
    Ld = jnp.where(same, L, 0.0)
    X = -Ld
    inv = eye + X
    for _ in range(3):
        X = _bmm(X, X)
        inv = _bmm(inv, eye + X)
    if C == base:
        return inv
    N = _bmm(inv, L - Ld)
    out = eye - N
    levels = C // base
    P = N
    span = 2
    while span < levels:
        P = _bmm(P, P)
        out = _bmm(out, eye + P)
        span *= 2
    return _bmm(out, inv)


@jax.custom_vjp
def _tri_inv(L):
    return _tri_inv_impl(L)


def _tri_inv_fwd(L):
    T = _tri_inv_impl(L)
    return T, T


def _tri_inv_bwd(T, dT):
    return (-_bmm_nt(_bmm_tn(T, dT), T),)


_tri_inv.defvjp(_tri_inv_fwd, _tri_inv_bwd)


def _cumsum_impl(x, reverse):
    C = x.shape[1]
    row = lax.broadcasted_iota(jnp.int32, x.shape, 1)
    s = 1
    while s < C:
        if reverse:
            x = x + jnp.where(row < C - s, pltpu.roll(x, C - s, 1), 0.0)
        else:
            x = x + jnp.where(row >= s, pltpu.roll(x, s, 1), 0.0)
        s *= 2
    return x


@jax.custom_vjp
def _cumsum(x):
    return _cumsum_impl(x, False)


_cumsum.defvjp(lambda x: (_cumsum_impl(x, False), None), lambda _, g: (_cumsum_impl(g, True),))


def _wkv_prep(r, lw, k, v, a, b):
    lane = lax.broadcasted_iota(jnp.int32, (r.shape[0], 128), 1)
    low = lane < RWKV_HD

    def heads(t):
        out = []
        for p in range(RWKV_HEADS // 2):
            pair = t[:, 128 * p:128 * (p + 1)]
            out += [jnp.where(low, pair, 0.0), jnp.where(low, 0.0, pair)]
        return jnp.concatenate([t[None] for t in out], axis=0)

    r, lw, k, v, a, b = [heads(t) for t in (r, lw, k, v, a, b)]
    H, C, D = r.shape
    row, col = _masks(H, C)
    incl, strict = row >= col, row > col
    cw = _cumsum(lw)
    cwp = cw - lw
    cwl = jnp.sum(lw, axis=1, keepdims=True)
    en = jnp.exp(-cw)
    at, rt, bt, kt = a * jnp.exp(cwp), r * jnp.exp(cw), b * en, k * en
    Tm = _tri_inv(-jnp.where(strict, _bmm_nt(at, bt), 0.0))
    ar = jnp.concatenate([at, rt], axis=1)
    gram = _bmm_nt(ar, jnp.concatenate([bt, kt], axis=1))
    row2 = lax.broadcasted_iota(jnp.int32, (H, 2 * C, 2 * C), 1)
    col2 = lax.broadcasted_iota(jnp.int32, (H, 2 * C, 2 * C), 2) % C
    gram = jnp.where(((row2 < C) & (row2 > col2)) | ((row2 >= C) & (row2 - C >= col2)), gram, 0.0)
    a_bk, r_bk = gram[:, :C], gram[:, C:]
    lak_v = _bmm(a_bk, jnp.concatenate([jnp.zeros_like(v), v], axis=1))
    ed = jnp.exp(cwl - cw)
    zdec = jnp.swapaxes(jnp.broadcast_to(jnp.exp(cwl), (H, D, D)), 1, 2)
    return ar, Tm, lak_v, r_bk, jnp.concatenate([b * ed, k * ed], axis=1), zdec, v


def _wkv_step(Z, ar, Tm, lak_v, r_bk, bk_d, zdec, v):
    C = Tm.shape[1]
    ar_z = _bmm(ar, Z)
    uv = jnp.concatenate([_bmm(Tm, ar_z[:, :C] + lak_v), v], axis=1)
    y = ar_z[:, C:] + _bmm(r_bk, uv)
    Z1 = Z * zdec + _bmm_tn(bk_d, uv)
    return jnp.concatenate([y[2 * p] + y[2 * p + 1] for p in range(RWKV_HEADS // 2)], axis=1), Z1


def _gdn_prep(q, k, v, gbeta):
    heads = lambda t: jnp.concatenate([t[None, :, GDN_HD * h:GDN_HD * (h + 1)] for h in range(GDN_HEADS)], axis=0)
    src = lax.broadcasted_iota(jnp.int32, (W_AB, 2 * GDN_W), 0)
    dst = lax.broadcasted_iota(jnp.int32, (W_AB, 2 * GDN_W), 1) // GDN_HD
    spread = jnp.dot(gbeta, (src == dst).astype(f32), precision=HI, preferred_element_type=f32)
    q, k, v, g, beta = heads(q), heads(k), heads(v), heads(spread[:, :GDN_W]), heads(spread[:, GDN_W:])
    H, C, D = q.shape
    row, col = _masks(H, C)
    incl, strict = row >= col, row > col
    gc = _cumsum(g)
    diff = gc - jnp.swapaxes(gc, 1, 2)
    decay = jnp.where(incl, jnp.exp(jnp.where(incl, diff, 0.0)), 0.0)
    gl = jnp.sum(g, axis=1, keepdims=True)
    kb, vb = k * beta, v * beta
    gram = _bmm_nt(jnp.concatenate([kb, q], axis=1), k)
    L = jnp.where(strict, gram[:, :C] * decay, 0.0)
    attn = jnp.where(incl, gram[:, C:] * decay, 0.0)
    egc = jnp.exp(gc)
    t_vk = _bmm(_tri_inv(L), jnp.concatenate([vb, kb * egc], axis=2))
    return t_vk[:, :, :D], jnp.concatenate([t_vk[:, :, D:], q * egc], axis=1), attn, k * jnp.exp(gl - gc), jnp.exp(gl)


def _gdn_step(S, u, wq, attn, ke, sdec):
    C = u.shape[1]
    wq_s = _bmm(wq, S)
    v_new = u - wq_s[:, :C]
    o = wq_s[:, C:] + _bmm(attn, v_new)
    S1 = S * sdec + _bmm_tn(ke, v_new)
    return jnp.concatenate([o[h] for h in range(GDN_HEADS)], axis=1), S1


def _scan_fwd(name, fns, ins, C, H, dh, w_out, per_step, side=None):
    prep, step = fns
    T = ins[0].shape[0]
    n_in = len(ins)
    blk = C * per_step
    nblk = T // blk
    n_side = 0 if side is None else len(side[0])

    def body(*refs):
        in_refs, refs = refs[:n_in], refs[n_in:]
        side_in, refs = refs[:n_side], refs[n_side:]
        y_ref, zs_ref, refs = refs[0], refs[1], refs[2:]
        side_out, refs = refs[:n_side], refs[n_side:]
        z_scr = refs[0]
        if side is not None:
            start, finish = _xy_copies(side_in, side_out, refs[1:], side[1])
            pl.when(pl.program_id(0) == 0)(start)

        @pl.when(pl.program_id(0) == 0)
        def _():
            z_scr[...] = jnp.zeros_like(z_scr)

        rows = [slice(C * j, C * (j + 1)) for j in range(per_step)]
        prepped = [prep(*[r[rw, :] for r in in_refs]) for rw in rows]
        Z = z_scr[...]
        for j, rw in enumerate(rows):
            zs_ref[j] = Z
            y, Z = step(Z, *prepped[j])
            y_ref[rw, :] = y
        z_scr[...] = Z
        if side is not None:
            pl.when(pl.program_id(0) == nblk - 1)(finish)

    side_bufs = [] if side is None else list(side[0])
    any_spec = pl.BlockSpec(memory_space=pl.ANY)
    return pl.pallas_call(
        body, grid=(nblk,),
        in_specs=[pl.BlockSpec((blk, a.shape[1]), lambda i: (i, 0)) for a in ins] + [any_spec] * n_side,
        out_specs=[pl.BlockSpec((blk, w_out), lambda i: (i, 0)), pl.BlockSpec((per_step, H, dh, dh), lambda i: (i, 0, 0, 0))]
        + [any_spec] * n_side,
        out_shape=[jax.ShapeDtypeStruct((T, w_out), f32), jax.ShapeDtypeStruct((T // C, H, dh, dh), f32)]
        + (_xy_out_shapes(side_bufs, side[1]) if side is not None else []),
        scratch_shapes=[pltpu.VMEM((H, dh, dh), f32)] + (_xy_sems(n_side, side[1]) if side is not None else []), name=name,
        compiler_params=_params("arbitrary"))(*ins, *side_bufs)


def _scan_bwd(name, fns, ins, dy, zs, C, per_step, side=None):
    prep, step = fns
    T = ins[0].shape[0]
    _, H, dh, _ = zs.shape
    n_in = len(ins)
    blk = C * per_step
    nblk = T // blk
    n_side = 0 if side is None else len(side[0])

    def body(*refs):
        in_refs, dy_ref, zs_ref, refs = refs[:n_in], refs[n_in], refs[n_in + 1], refs[n_in + 2:]
        side_in, refs = refs[:n_side], refs[n_side:]
        out_refs, refs = refs[:n_in], refs[n_in:]
        side_out, refs = refs[:n_side], refs[n_side:]
        dz_scr = refs[0]
        if side is not None:
            start, finish = _xy_copies(side_in, side_out, refs[1:], side[1])
            pl.when(pl.program_id(0) == 0)(start)

        @pl.when(pl.program_id(0) == 0)
        def _():
            dz_scr[...] = jnp.zeros_like(dz_scr)

        rows = [slice(C * j, C * (j + 1)) for j in range(per_step)]
        prepped = [jax.vjp(prep, *[r[rw, :] for r in in_refs]) for rw in rows]
        d_prepped = [None] * per_step
        dZ = dz_scr[...]
        for j in reversed(range(per_step)):
            _, pull = jax.vjp(step, zs_ref[j], *prepped[j][0])
            dZ, *d_prepped[j] = pull((dy_ref[rows[j], :], dZ))
        dz_scr[...] = dZ
        for j, rw in enumerate(rows):
            for o_ref, gval in zip(out_refs, prepped[j][1](tuple(d_prepped[j]))):
                o_ref[rw, :] = gval
        if side is not None:
            pl.when(pl.program_id(0) == nblk - 1)(finish)

    side_bufs = [] if side is None else list(side[0])
    any_spec = pl.BlockSpec(memory_space=pl.ANY)
    rev = lambda i: (nblk - 1 - i, 0)
    return pl.pallas_call(
        body, grid=(nblk,),
        in_specs=[pl.BlockSpec((blk, a.shape[1]), rev) for a in ins]
        + [pl.BlockSpec((blk, dy.shape[1]), rev), pl.BlockSpec((per_step, H, dh, dh), lambda i: (nblk - 1 - i, 0, 0, 0))]
        + [any_spec] * n_side,
        out_specs=[pl.BlockSpec((blk, a.shape[1]), rev) for a in ins] + [any_spec] * n_side,
        out_shape=[jax.ShapeDtypeStruct(a.shape, f32) for a in ins]
        + (_xy_out_shapes(side_bufs, side[1]) if side is not None else []),
        scratch_shapes=[pltpu.VMEM((H, dh, dh), f32)] + (_xy_sems(n_side, side[1]) if side is not None else []), name=name,
        compiler_params=_params("arbitrary"))(*ins, dy, zs, *side_bufs)


def _loss_call(x2, tgt, g, tm):
    T, W = x2.shape

    def body(x_ref, t_ref, g_ref, dx_ref, dg_ref, l_ref):
        i = pl.program_id(0)
        tv = t_ref[...]
        l, vjp = jax.vjp(lambda xv, gv: _loss_rows(xv, tv, gv), x_ref[...], g_ref[...])
        dx, dg = vjp(jnp.ones_like(l))
        dx_ref[...] = dx
        tot = jnp.zeros((1, 128), f32) + jnp.sum(l)

        @pl.when(i == 0)
        def _():
            dg_ref[...] = dg
            l_ref[...] = tot

        @pl.when(i > 0)
        def _():
            dg_ref[...] += dg
            l_ref[...] += tot

    return pl.pallas_call(
        body, grid=(T // tm,),
        in_specs=[_row_spec(tm, W), _row_spec(tm, W), _full_spec(g.shape)],
        out_specs=[_row_spec(tm, W), _full_spec(g.shape), _full_spec((1, 128))],
        out_shape=[jax.ShapeDtypeStruct((T, W), f32), jax.ShapeDtypeStruct(g.shape, f32),
                   jax.ShapeDtypeStruct((1, 128), f32)], name="loss_head",
        compiler_params=_params("arbitrary"))(x2, tgt, g)


def _local_step(x, tgt, W, late=None):
    row = lambda a: a.reshape(1, -1)
    wp = W['w_in_pad']
    w_rwkv, w_qkv, w_z = wp[:, :OFF_QKV], wp[:, OFF_QKV:OFF_Z], wp[:, OFF_Z:OFF_GATES]
    w_gates, w_ab = wp[:, OFF_GATES:OFF_AB], wp[:, OFF_AB:]
    mu = row(W['rwkv_mu'])
    mixw = jnp.concatenate([mu, 1.0 - mu], axis=0)
    zpad = jnp.zeros((64, RWKV_W), f32)
    w2p = jnp.concatenate([W['rwkv_w2'], zpad], axis=0)
    a2p = jnp.concatenate([zpad, W['rwkv_a2']], axis=0)
    rw_consts = [row(W['rwkv_w0']), w2p, row(W['rwkv_a0']), a2p, W['rwkv_g2'], row(W['rwkv_k_k']), row(W['rwkv_k_a'])]
    post_consts = [row(W['rwkv_ln_w']), row(W['rwkv_ln_b']), row(W['rwkv_r_k'])]
    pad4 = lambda a: jnp.pad(row(a), ((0, 0), (0, W_AB - GDN_HEADS)))
    gd_consts = [pad4(W['gdn_a_log']), pad4(W['gdn_dt_bias'])]
    nw_t = jnp.tile(row(W['gdn_norm_w']), (1, GDN_HEADS))
    g1, g2n, gf = row(W['norm1_g']), row(W['norm2_g']), row(W['final_g'])

    (u,) = _pw_fwd("norm1", _rms_fn, [x], [g1], [D_MODEL], 256, out_dtype=bf16)
    p_rwkv = _mm(u, w_rwkv, 'nn', "in_rwkv")
    qkv_raw = _mm(u, w_qkv, 'nn', "in_qkv")
    z = _mm(u, w_z, 'nn', "in_z")
    gates = _mm(u, w_gates, 'nn', "in_gates")
    ab = _mm(u, w_ab, 'nn', "in_ab")

    lanes = lambda off: slice(off, off + STRIP)
    rw_groups = [[lanes(STRIP * j), lanes(512 + STRIP * j), lanes(1024 + STRIP * j)] for j in range(RWKV_W // STRIP)]
    rw_shared = [lanes(1536), lanes(1664)]
    r, lw, k2, v, a_, b_, g = _group_fwd("rwkv_prep", _rwkv_prep_fn, p_rwkv, mixw, rw_shared, rw_groups, rw_consts, 7, 256)
    wkv_in = [r, lw, k2, v, a_, b_]
    y, zs_wkv, *gathered = _scan_fwd("wkv_fwd", (_wkv_prep, _wkv_step), wkv_in, WKV_CHUNK, RWKV_HEADS, 2 * RWKV_HD, RWKV_W, WKV_PER_STEP,
                                     side=None if late is None else (late['shards'][0], False))
    if late is not None:
        W = dict(W, **late['assemble'](0, gathered))
    (ya_in,) = _pw_fwd("rwkv_post", _rwkv_post_fn, [y, r, k2, v, g], post_consts, [RWKV_W], 256, out_dtype=bf16, strip=128)
    ya = _mm(ya_in, W['rwkv_proj'], 'nn', "rwkv_proj")

    gd_groups = [[lanes(GDN_HD * h), lanes(GDN_W + GDN_HD * h), lanes(2 * GDN_W + GDN_HD * h)] for h in range(GDN_HEADS)]
    gq, gk, gv = _group_fwd("gdn_prep", _gdn_prep_fn, qkv_raw, W['gdn_conv_w'], [], gd_groups, [], 3, 256)
    (gbeta,) = _pw_fwd("gdn_gate", _gdn_gate_fn, [ab], gd_consts, [W_AB], 256)
    gdn_in = [gq, gk, gv, gbeta]
    o, zs_gdn, *gathered = _scan_fwd("gdn_fwd", (_gdn_prep, _gdn_step), gdn_in, GDN_CHUNK, GDN_HEADS, GDN_HD, GDN_W, GDN_PER_STEP,
                                     side=None if late is None else (late['shards'][1], False))
    if late is not None:
        W = dict(W, **late['assemble'](1, gathered))
    (yb_in,) = _pw_fwd("gdn_post", _gdn_post_fn, [o, z], [nw_t], [GDN_W], 256, out_dtype=bf16, strip=128)
    yb = _mm(yb_in, W['gdn_proj'], 'nn', "gdn_proj")

    ga, gb = _cols(gates, D_MODEL, 0), _cols(gates, D_MODEL, 1)
    (mixed,) = _pw_fwd("mix", _mix_fn, [ga, gb, ya, yb], [], [D_MODEL], 256, out_dtype=bf16, strip=256)
    x1 = _mm(mixed, W['w_out'], 'nn', "w_out", add=x)
    (u2,) = _pw_fwd("norm2", _rms_fn, [x1], [g2n], [D_MODEL], 256, out_dtype=bf16)
    h = _mm(u2, W['ffn_up'], 'nn', "ffn_up")
    act = _ffn_act_fwd(h, W['ffn_conv_w'], 256)
    x2 = _mm(act, W['ffn_down'], 'nn', "ffn_down", add=x1)

    G = {}
    slab_out = None if late is None else N_POS
    dx2, dgf, loss = _loss_call(x2, tgt, gf, 256)
    G['final_g'] = dgf
    dact = _mm(dx2, W['ffn_down'], 'nt', "d_act")
    G['ffn_down'] = _mm(act, dx2, 'tn', "g_ffn_down", out_dtype=bf16)
    dh, G['ffn_conv_w'] = _ffn_act_bwd(h, dact, W['ffn_conv_w'], 128)
    du2 = _mm(dh, W['ffn_up'], 'nt', "d_u2")
    G['ffn_up'] = _mm(u2, dh, 'tn', "g_ffn_up", out_dtype=bf16, col_slabs=slab_out)
    (dx1,), (G['norm2_g'],) = _pw_bwd("norm2_bwd", _rms_fn, [x1], [g2n], [(du2,)], 256, add_to_first=dx2)
    dmixed = _mm(dx1, W['w_out'], 'nt', "d_mixed")
    G['w_out'] = _mm(mixed, dx1, 'tn', "g_w_out", out_dtype=bf16)
    (dga, dgb, dya, dyb), _ = _pw_bwd("mix_bwd", _mix_fn, [ga, gb, ya, yb], [], [(dmixed,)], 256, row_dtypes=[bf16] * 4,
                                      strip=256)
    dya_in = _mm(dya, W['rwkv_proj'], 'nt', "d_ya_in")
    G['rwkv_proj'] = _mm(ya_in, dya, 'tn', "g_rwkv_proj", out_dtype=bf16, col_slabs=slab_out)
    dyb_in = _mm(dyb, W['gdn_proj'], 'nt', "d_yb_in")
    G['gdn_proj'] = _mm(yb_in, dyb, 'tn', "g_gdn_proj", out_dtype=bf16, col_slabs=slab_out)

    (do, dz), (dnw_t,) = _pw_bwd("gdn_post_bwd", _gdn_post_fn, [o, z], [nw_t], [(dyb_in,)], 256, row_dtypes=[f32, bf16],
                                 strip=128)
    G['gdn_norm_w'] = dnw_t.reshape(GDN_HEADS, GDN_HD).sum(axis=0)
    dgq, dgk, dgv, dgbeta = _scan_bwd("gdn_bwd", (_gdn_prep, _gdn_step), gdn_in, do, zs_gdn, GDN_CHUNK, GDN_PER_STEP)
    dqkv_raw, G['gdn_conv_w'], _ = _group_bwd("gdn_prep_bwd", _gdn_prep_fn, qkv_raw, W['gdn_conv_w'], [], gd_groups, [],
                                              [(dgq,), (dgk,), (dgv,)], 128)
    (dab,), (dal_p, ddt_p) = _pw_bwd("gdn_gate_bwd", _gdn_gate_fn, [ab], gd_consts, [(dgbeta,)], 256, row_dtypes=[bf16])
    G['gdn_a_log'], G['gdn_dt_bias'] = dal_p[0, :GDN_HEADS], ddt_p[0, :GDN_HEADS]

    (dy, dr1, dk21, dv1, dg_), (G['rwkv_ln_w'], G['rwkv_ln_b'], G['rwkv_r_k']) = _pw_bwd(
        "rwkv_post_bwd", _rwkv_post_fn, [y, r, k2, v, g], post_consts, [(dya_in,)], 256, strip=128)
    dr2, dlw, dk22, dv2, da_, db_, *G['_arrived'] = _scan_bwd(
        "wkv_bwd", (_wkv_prep, _wkv_step), wkv_in, dy, zs_wkv, WKV_CHUNK, WKV_PER_STEP, side=None if late is None else (late['slabs'](G), True))
    dp_rwkv, dmixw, rw_grads = _group_bwd(
        "rwkv_prep_bwd", _rwkv_prep_fn, p_rwkv, mixw, rw_shared, rw_groups, rw_consts,
        [(dr1, dr2), (dlw,), (dk21, dk22), (dv1, dv2), (da_,), (db_,), (dg_,)], 128)
    G['rwkv_w0'], dw2p, G['rwkv_a0'], da2p, G['rwkv_g2'], G['rwkv_k_k'], G['rwkv_k_a'] = rw_grads
    G['rwkv_w2'], G['rwkv_a2'] = dw2p[:64], da2p[64:]
    G['rwkv_mu'] = dmixw[0] - dmixw[1]

    dp = jnp.concatenate([dp_rwkv, dqkv_raw, dz, dga, dgb, dab], axis=1)
    G['w_in_pad'] = _mm(u, dp, 'tn', "g_w_in", out_dtype=bf16)
    if late is None:
        du = _mm(dp, wp, 'nt', "d_u")
    else:
        du, *G['_arrived_w_in'] = _mm(dp, wp, 'nt', "d_u", side=(late['w_in_slabs'](G), True))
    (dx,), (G['norm1_g'],) = _pw_bwd("norm1_bwd", _rms_fn, [x], [g1], [(du,)], 256, add_to_first=dx1)
    return loss, dx, G


IN_WIDTH = OFF_AB + 8
PAD_ORDER = ((0, OFF_GATES), (OFF_GATES + 8, IN_WIDTH), (OFF_GATES, OFF_GATES + 8))


def _pad_w_in(w):
    return jnp.concatenate([w[:, a:b] for a, b in PAD_ORDER] + [jnp.zeros((w.shape[0], W_AB - 8), w.dtype)], axis=1)


def _pad_w_in_shards(shards):
    width = shards[0].shape[1]
    parts = []
    for a, b in PAD_ORDER:
        for j, sh in enumerate(shards):
            lo, hi = max(a, j * width), min(b, (j + 1) * width)
            if lo < hi:
                parts.append(sh[:, lo - j * width:hi - j * width])
    return jnp.concatenate(parts + [jnp.zeros((shards[0].shape[0], W_AB - 8), shards[0].dtype)], axis=1)


def _unpad_cols(wp, lo, hi):
    parts, off = [], 0
    for a, b in PAD_ORDER:
        l, h = max(a, lo), min(b, hi)
        if l < h:
            parts.append((l, wp[:, off + l - a:off + h - a]))
        off += b - a
    parts.sort(key=lambda t: t[0])
    return parts[0][1] if len(parts) == 1 else jnp.concatenate([p for _, p in parts], axis=1)


def _unpad_w_in(wp):
    return _unpad_cols(wp, 0, IN_WIDTH)


BIG = ('w_in', 'rwkv_proj', 'gdn_proj', 'w_out', 'ffn_up', 'ffn_down')
SMALL_SHARDED = ('rwkv_w2', 'rwkv_a2', 'rwkv_g2', 'gdn_conv_w', 'ffn_conv_w')


def _rows128(shape):
    n = 1
    for d in shape:
        n *= d
    return -(-n // LANES)


def _pack128(arrays):
    parts = []
    for a in arrays:
        flat = a.reshape(-1)
        rows = _rows128(a.shape)
        parts.append(jnp.pad(flat, (0, rows * LANES - flat.shape[0])).reshape(rows, LANES))
    buf = jnp.concatenate(parts, axis=0)
    return jnp.pad(buf, ((0, -buf.shape[0] % HALO), (0, 0)))


def _unpack128(buf, shapes):
    out, off = [], 0
    for s in shapes:
        rows, n = _rows128(s), 1
        for d in s:
            n *= d
        out.append(buf[off:off + rows].reshape(-1)[:n].reshape(s))
        off += rows
    return out


def _param_tile(r, c):
    best = None
    for d in range(2 * HALO, r + 1, 2 * HALO):
        if r % d == 0 and d * c * 4 <= TILE_BYTES:
            best = d
    if best is not None or r * c * 4 <= TILE_BYTES:
        return (best if best is not None else r), c
    return r, 128


def _xy_exchange(name, bufs, scatter):
    n = len(bufs)

    def body(*refs):
        start, finish = _xy_copies(refs[:n], refs[n:2 * n], refs[2 * n:], scatter)
        start()
        finish()

    return pl.pallas_call(
        body, in_specs=[pl.BlockSpec(memory_space=pl.ANY)] * n, out_specs=[pl.BlockSpec(memory_space=pl.ANY)] * n,
        out_shape=_xy_out_shapes(bufs, scatter), scratch_shapes=_xy_sems(n, scatter), name=name)(*bufs)


def _sibling_exchange(name, bufs):
    n = len(bufs)

    def body(*refs):
        in_refs, out_refs, send_sems, recv_sems = refs[:n], refs[n:2 * n], refs[2 * n], refs[2 * n + 1]
        x, y, c = lax.axis_index("x"), lax.axis_index("y"), lax.axis_index("c")
        copies = [pltpu.make_async_remote_copy(
            src_ref=in_refs[a], dst_ref=out_refs[a], send_sem=send_sems.at[a], recv_sem=recv_sems.at[a],
            device_id=(x, y, 1 - c), device_id_type=pl.DeviceIdType.MESH) for a in range(n)]
        for cp in copies:
            cp.start()
        for cp in copies:
            cp.wait()

    return pl.pallas_call(
        body, in_specs=[pl.BlockSpec(memory_space=pl.ANY)] * n, out_specs=[pl.BlockSpec(memory_space=pl.ANY)] * n,
        out_shape=[jax.ShapeDtypeStruct(b.shape, b.dtype) for b in bufs],
        scratch_shapes=[pltpu.SemaphoreType.DMA((n,)), pltpu.SemaphoreType.DMA((n,))], name=name)(*bufs)


def _sum_slots(name, buf):
    _, R, L = buf.shape
    tr, tc = _param_tile(R, L)

    def body(b_ref, o_ref):
        part = lambda s: b_ref[s].astype(f32)
        o_ref[...] = ((part(0) + part(1)) + part(2)) + part(3)

    return pl.pallas_call(
        body, grid=(R // tr, L // tc),
        in_specs=[pl.BlockSpec((N_POS, tr, tc), lambda i, j: (0, i, j))],
        out_specs=pl.BlockSpec((tr, tc), lambda i, j: (i, j)),
        out_shape=jax.ShapeDtypeStruct((R, L), f32), name=name,
        compiler_params=_params("parallel", "parallel"))(buf)


def _adamw(name, w, ga, gb, m, v):
    R, L = w.shape
    tr, tc = _param_tile(R, L)
    c1 = 1.0 / (1.0 - ADAM_B1 ** ADAM_STEP)
    c2 = 1.0 / (1.0 - ADAM_B2 ** ADAM_STEP)

    def body(w_ref, ga_ref, gb_ref, m_ref, v_ref, g_out, d_out, m_out, v_out):
        g = ga_ref[...] + gb_ref[...]
        m_new = ADAM_B1 * m_ref[...] + (1.0 - ADAM_B1) * g
        v_new = ADAM_B2 * v_ref[...] + (1.0 - ADAM_B2) * (g * g)
        g_out[...] = g
        m_out[...] = m_new
        v_out[...] = v_new
        d_out[...] = -ADAM_LR * ((m_new * c1) / (jnp.sqrt(v_new * c2) + ADAM_EPS) + ADAM_WD * w_ref[...])

    spec = pl.BlockSpec((tr, tc), lambda i, j: (i, j))
    return pl.pallas_call(
        body, grid=(R // tr, L // tc), in_specs=[spec] * 5, out_specs=[spec] * 4,
        out_shape=[jax.ShapeDtypeStruct((R, L), f32)] * 4, name=name,
        compiler_params=_params("parallel", "parallel"))(w, ga, gb, m, v)


def _step(x, loss_target, P, M, V):
    shapes = {n: tuple(P[n].shape) for n in WEIGHTS}
    sh_shapes = [shapes[n] for n in SMALL_SHARDED]
    packed = SMALL_SHARDED + SMALL
    late_names = BIG[1:]

    def whole(n, g):
        return g.reshape(-1, g.shape[2]) if n in ROW_SHARDED else jnp.concatenate([g[j] for j in range(N_POS)], axis=1)

    def slabs(G, n, dtype=f32):
        r, c = shapes[n]
        full = G[n].astype(dtype)
        if full.ndim == 3:
            return full
        return full.reshape(N_POS, r, c) if n in ROW_SHARDED else full.reshape(r, N_POS, c).transpose(1, 0, 2)

    g_w_in, g_small = _xy_exchange("gather_w_in", [P['w_in'].astype(bf16), _pack128([P[n] for n in SMALL_SHARDED])],
                                   scatter=False)
    W = {n: P[n] for n in SMALL}
    W['w_in_pad'] = _pad_w_in_shards([g_w_in[j] for j in range(N_POS)])
    per_pos = [_unpack128(g_small[j], sh_shapes) for j in range(N_POS)]
    for q, n in enumerate(SMALL_SHARDED):
        W[n] = jnp.concatenate([per_pos[j][q] for j in range(N_POS)], axis=1)
    groups = (('rwkv_proj', 'gdn_proj', 'ffn_up'), ('w_out', 'ffn_down'))
    late = dict(shards=[[P[n].astype(bf16) for n in grp] for grp in groups],
                assemble=lambda q, gathered: {n: whole(n, g) for n, g in zip(groups[q], gathered)},
                slabs=lambda G: [slabs(G, n, bf16) for n in late_names],
                w_in_slabs=lambda G: [jnp.stack([_unpad_cols(G['w_in_pad'], j * shapes['w_in'][1], (j + 1) * shapes['w_in'][1])
                                                 for j in range(N_POS)])])

    loss_rows, dx, G = _local_step(x, loss_target, W, late)
    arrived_late = G.pop('_arrived')
    (arrived_w_in,) = G.pop('_arrived_w_in')
    G.pop('w_in_pad')

    small_slabs = jnp.stack([_pack128([slabs(G, n)[j] for n in SMALL_SHARDED] + [G[n] for n in SMALL]) for j in range(N_POS)])
    (arrived_small,) = _xy_exchange("scatter_small", [small_slabs], scatter=True)
    contributions = [arrived_w_in] + list(arrived_late) + [arrived_small]
    tags = list(BIG) + ['small']
    plane = [_sum_slots("sum_" + t, cbuf) for t, cbuf in zip(tags, contributions)]
    sibling = _sibling_exchange("sibling_grads", plane)

    out = {}
    names4 = ('grad', 'delta', 'new_m', 'new_v')
    for q, n in enumerate(BIG):
        for tag, t in zip(names4, _adamw("adamw_" + n, P[n], plane[q], sibling[q], M[n], V[n])):
            out[tag + '_' + n] = t
    small_out = _adamw("adamw_small", _pack128([P[n] for n in packed]), plane[-1], sibling[-1],
                       _pack128([M[n] for n in packed]), _pack128([V[n] for n in packed]))
    for tag, buf in zip(names4, small_out):
        for n, t in zip(packed, _unpack128(buf, [shapes[n] for n in packed])):
            out[tag + '_' + n] = t
    loss = lax.psum(loss_rows[0, 0], ("x", "y", "c"))
    return loss, dx, out


def kernel(x, norm1_g, w_in, rwkv_mu, rwkv_w0, rwkv_w2, rwkv_a0, rwkv_a2, rwkv_g2, rwkv_k_k, rwkv_k_a, rwkv_r_k, rwkv_ln_w, rwkv_ln_b, rwkv_proj, gdn_conv_w, gdn_a_log, gdn_dt_bias, gdn_norm_w, gdn_proj, w_out, norm2_g, ffn_up, ffn_conv_w, ffn_down, final_g, loss_target, m_norm1_g, m_w_in, m_rwkv_mu, m_rwkv_w0, m_rwkv_w2, m_rwkv_a0, m_rwkv_a2, m_rwkv_g2, m_rwkv_k_k, m_rwkv_k_a, m_rwkv_r_k, m_rwkv_ln_w, m_rwkv_ln_b, m_rwkv_proj, m_gdn_conv_w, m_gdn_a_log, m_gdn_dt_bias, m_gdn_norm_w, m_gdn_proj, m_w_out, m_norm2_g, m_ffn_up, m_ffn_conv_w, m_ffn_down, m_final_g, v_norm1_g, v_w_in, v_rwkv_mu, v_rwkv_w0, v_rwkv_w2, v_rwkv_a0, v_rwkv_a2, v_rwkv_g2, v_rwkv_k_k, v_rwkv_k_a, v_rwkv_r_k, v_rwkv_ln_w, v_rwkv_ln_b, v_rwkv_proj, v_gdn_conv_w, v_gdn_a_log, v_gdn_dt_bias, v_gdn_norm_w, v_gdn_proj, v_w_out, v_norm2_g, v_ffn_up, v_ffn_conv_w, v_ffn_down, v_final_g):
    weights = (norm1_g, w_in, rwkv_mu, rwkv_w0, rwkv_w2, rwkv_a0, rwkv_a2, rwkv_g2, rwkv_k_k, rwkv_k_a, rwkv_r_k, rwkv_ln_w,
               rwkv_ln_b, rwkv_proj, gdn_conv_w, gdn_a_log, gdn_dt_bias, gdn_norm_w, gdn_proj, w_out, norm2_g, ffn_up,
               ffn_conv_w, ffn_down, final_g)
    m_in = (m_norm1_g, m_w_in, m_rwkv_mu, m_rwkv_w0, m_rwkv_w2, m_rwkv_a0, m_rwkv_a2, m_rwkv_g2, m_rwkv_k_k, m_rwkv_k_a,
            m_rwkv_r_k, m_rwkv_ln_w, m_rwkv_ln_b, m_rwkv_proj, m_gdn_conv_w, m_gdn_a_log, m_gdn_dt_bias, m_gdn_norm_w,
            m_gdn_proj, m_w_out, m_norm2_g, m_ffn_up, m_ffn_conv_w, m_ffn_down, m_final_g)
    v_in = (v_norm1_g, v_w_in, v_rwkv_mu, v_rwkv_w0, v_rwkv_w2, v_rwkv_a0, v_rwkv_a2, v_rwkv_g2, v_rwkv_k_k, v_rwkv_k_a,
            v_rwkv_r_k, v_rwkv_ln_w, v_rwkv_ln_b, v_rwkv_proj, v_gdn_conv_w, v_gdn_a_log, v_gdn_dt_bias, v_gdn_norm_w,
            v_gdn_proj, v_w_out, v_norm2_g, v_ffn_up, v_ffn_conv_w, v_ffn_down, v_final_g)
    drop = lambda n, a: a if n == 'final_g' else a[0]
    P = {n: drop(n, a) for n, a in zip(WEIGHTS, weights)}
    M = {n: drop(n, a) for n, a in zip(WEIGHTS, m_in)}
    V = {n: drop(n, a) for n, a in zip(WEIGHTS, v_in)}
    loss, dx, out = _step(x[0], loss_target[0], P, M, V)
    lift = lambda n, a: a if n == 'final_g' else a[None]
    res = [loss, dx[None]]
    for tag in ('grad', 'delta', 'new_m', 'new_v'):
        res += [lift(n, out[tag + '_' + n]) for n in WEIGHTS]
    return tuple(res)
```

```python
import functools

import jax
import jax.numpy as jnp
from jax import lax
from jax.experimental import pallas as pl
from jax.experimental.pallas import tpu as pltpu

f32 = jnp.float32
bf16 = jnp.bfloat16
HI = lax.Precision.HIGHEST

D_MODEL = 1024
RWKV_HEADS, RWKV_HD, RWKV_W = 8, 64, 512
GDN_HEADS, GDN_HD, GDN_W = 4, 128, 512
FFN_H = 2816
NORM_EPS, L2_EPS, GN_EPS = 1e-6, 1e-6, 64e-5
W_AB = 256
OFF_QKV, OFF_Z, OFF_GATES, OFF_AB = 1792, 3328, 3840, 5888
W_IN_PAD = OFF_AB + W_AB
WKV_CHUNK, WKV_PER_STEP = 64, 4
GDN_CHUNK, GDN_PER_STEP = 128, 4
HALO = 8
LANES = 128
TILE_BYTES = 1 << 20
VMEM_LIMIT = 56 * 1024 * 1024

ADAM_LR, ADAM_B1, ADAM_B2, ADAM_EPS, ADAM_WD, ADAM_STEP = 0.001, 0.9, 0.999, 1e-08, 0.01, 10

ROW_SHARDED = ('w_out', 'ffn_down')
SMALL = ('norm1_g', 'rwkv_mu', 'rwkv_w0', 'rwkv_a0', 'rwkv_k_k', 'rwkv_k_a', 'rwkv_r_k', 'rwkv_ln_w', 'rwkv_ln_b',
         'gdn_a_log', 'gdn_dt_bias', 'gdn_norm_w', 'norm2_g', 'final_g')
WEIGHTS = ('norm1_g', 'w_in', 'rwkv_mu', 'rwkv_w0', 'rwkv_w2', 'rwkv_a0', 'rwkv_a2', 'rwkv_g2', 'rwkv_k_k', 'rwkv_k_a',
           'rwkv_r_k', 'rwkv_ln_w', 'rwkv_ln_b', 'rwkv_proj', 'gdn_conv_w', 'gdn_a_log', 'gdn_dt_bias', 'gdn_norm_w',
           'gdn_proj', 'w_out', 'norm2_g', 'ffn_up', 'ffn_conv_w', 'ffn_down', 'final_g')


def _params(*sem):
    return pltpu.CompilerParams(dimension_semantics=sem, vmem_limit_bytes=VMEM_LIMIT)


def _tile(n, limit):
    if n <= limit:
        return n
    best = None
    for d in range(128, limit + 1, 128):
        if n % d == 0:
            best = d
    if best is None:
        raise ValueError(f"no tile for {n} under {limit}")
    return best


MM_BLOCK_BYTES = 6 << 20


def _mm(a, b, mode, name, add=None, out_dtype=f32, side=None, col_slabs=None):
    if mode == 'nn':
        (M, K), N = a.shape, b.shape[1]
    elif mode == 'nt':
        (M, K), N = a.shape, b.shape[0]
    else:
        (K, M), N = a.shape, b.shape[1]
    tm = _tile(M, 1408)
    tk = _tile(K, min(2816, MM_BLOCK_BYTES // (tm * a.dtype.itemsize)))
    tn = _tile(N, max(128, min(MM_BLOCK_BYTES // (tk * b.dtype.itemsize), MM_BLOCK_BYTES // (tm * 4)) // 128 * 128))
    if col_slabs is not None:
        tn = N // col_slabs
    nk = K // tk
    grid = (M // tm, N // tn, nk)
    dn = {'nn': (((1,), (0,)), ((), ())), 'nt': (((1,), (1,)), ((), ())), 'tn': (((0,), (0,)), ((), ()))}[mode]
    n_add = 0 if add is None else 1
    n_side = 0 if side is None else len(side[0])

    def body(a_ref, b_ref, *rest):
        add_ref = rest[0] if add is not None else None
        side_in, rest = rest[n_add:n_add + n_side], rest[n_add + n_side:]
        o_ref, side_out, rest = rest[0], rest[1:1 + n_side], rest[1 + n_side:]
        acc_ref, rest = (rest[0], rest[1:]) if nk > 1 else (None, rest)
        ids = [pl.program_id(d) for d in range(3)]
        if side is not None:
            start, finish = _xy_copies(side_in, side_out, rest, side[1])
            pl.when((ids[0] == 0) & (ids[1] == 0) & (ids[2] == 0))(start)
        acc = lax.dot_general(a_ref[...].astype(bf16), b_ref[...].astype(bf16), dn, preferred_element_type=f32)
        if nk == 1:
            o_ref[...] = (acc + add_ref[...] if add is not None else acc).astype(out_dtype)
        else:
            k = ids[2]

            @pl.when(k == 0)
            def _():
                acc_ref[...] = acc + add_ref[...] if add is not None else acc

            @pl.when(k > 0)
            def _():
                acc_ref[...] += acc

            @pl.when(k == nk - 1)
            def _():
                o_ref[...] = acc_ref[...].astype(out_dtype)
        if side is not None:
            pl.when((ids[0] == grid[0] - 1) & (ids[1] == grid[1] - 1) & (ids[2] == nk - 1))(finish)

    a_spec = (pl.BlockSpec((tk, tm), lambda i, j, k: (k, i)) if mode == 'tn'
              else pl.BlockSpec((tm, tk), lambda i, j, k: (i, k)))
    b_spec = (pl.BlockSpec((tn, tk), lambda i, j, k: (j, k)) if mode == 'nt'
              else pl.BlockSpec((tk, tn), lambda i, j, k: (k, j)))
    o_spec = pl.BlockSpec((tm, tn), lambda i, j, k: (i, j))
    o_shape = jax.ShapeDtypeStruct((M, N), out_dtype)
    if col_slabs is not None:
        o_spec = pl.BlockSpec((None, tm, tn), lambda i, j, k: (j, i, 0))
        o_shape = jax.ShapeDtypeStruct((col_slabs, M, tn), out_dtype)
    any_spec = pl.BlockSpec(memory_space=pl.ANY)
    side_bufs = [] if side is None else list(side[0])
    ins, specs = [a, b], [a_spec, b_spec]
    if add is not None:
        ins.append(add)
        specs.append(o_spec)
    outs = pl.pallas_call(
        body, grid=grid, in_specs=specs + [any_spec] * n_side, out_specs=[o_spec] + [any_spec] * n_side,
        out_shape=[o_shape] + (_xy_out_shapes(side_bufs, side[1]) if side is not None else []),
        scratch_shapes=([pltpu.VMEM((tm, tn), f32)] if nk > 1 else []) + (_xy_sems(n_side, side[1]) if side is not None else []),
        name=name,
        compiler_params=_params(*(("arbitrary",) * 3 if side is not None else ("parallel", "parallel", "arbitrary"))))(
            *ins, *side_bufs)
    return list(outs) if side is not None else outs[0]


def _shift_down(cur, prev, s):
    if s == 0:
        return cur
    ext = jnp.concatenate([prev, cur], axis=0)
    return pltpu.roll(ext, s, 0)[HALO:]


def _shift_up(cur, nxt, s):
    if s == 0:
        return cur
    ext = jnp.concatenate([cur, nxt], axis=0)
    return pltpu.roll(ext, ext.shape[0] - s, 0)[:cur.shape[0]]


def _conv_apply(cur, prev, w_ref, shifted=None):
    taps = w_ref.shape[0]
    out = None
    for i in range(taps):
        s = taps - 1 - i
        term = (shifted[s] if shifted is not None else _shift_down(cur, prev, s)) * w_ref[pl.ds(i, 1), :]
        out = term if out is None else out + term
    return out


def _row_spec(tm, w, col=0):
    return pl.BlockSpec((tm, w), lambda i: (i, col))


def _cols(a, width, col):
    return (a, width, col)


def _row_of(r):
    return r if isinstance(r, tuple) else (r, r.shape[1], 0)


def _prev_spec(tm, w):
    return pl.BlockSpec((HALO, w), lambda i: (jnp.maximum(i * (tm // HALO) - 1, 0), 0))


def _next_spec(tm, w, T):
    return pl.BlockSpec((HALO, w), lambda i: (jnp.minimum((i + 1) * (tm // HALO), T // HALO - 1), 0))


def _full_spec(shape):
    return pl.BlockSpec(shape, lambda i: (0,) * len(shape))


def _pw_fwd(name, fn, rows, consts, out_widths, tm, conv_w=None, out_dtype=f32, strip=None):
    T = _row_of(rows[0])[0].shape[0]
    nr, nc = len(rows), len(consts)

    def body(*refs):
        i = pl.program_id(0)
        if strip is not None:
            for j in range(out_widths[0] // strip):
                sl = slice(strip * j, strip * (j + 1))
                outs = fn(*[r[:, sl] for r in refs[:nr + nc]])
                for o_ref, o in zip(refs[nr + nc:], outs):
                    o_ref[:, sl] = o.astype(out_dtype)
            return
        vals = [r[...] for r in refs[:nr]]
        p = nr
        if conv_w is not None:
            prev = jnp.where(i > 0, refs[p][...], 0.0)
            vals[0] = _conv_apply(vals[0], prev, refs[p + 1])
            p += 2
        cvals = [r[...] for r in refs[p:p + nc]]
        outs = fn(*vals, *cvals)
        for o_ref, o in zip(refs[p + nc:], outs):
            o_ref[...] = o.astype(out_dtype)

    ins = [_row_of(r)[0] for r in rows]
    specs = [_row_spec(tm, *_row_of(r)[1:]) for r in rows]
    if conv_w is not None:
        ins += [rows[0], conv_w]
        specs += [_prev_spec(tm, rows[0].shape[1]), _full_spec(conv_w.shape)]
    ins += list(consts)
    specs += [_full_spec(c.shape) for c in consts]
    outs = pl.pallas_call(
        body, grid=(T // tm,), in_specs=specs,
        out_specs=[_row_spec(tm, w) for w in out_widths],
        out_shape=[jax.ShapeDtypeStruct((T, w), out_dtype) for w in out_widths], name=name,
        compiler_params=_params("parallel"))(*ins)
    return outs


def _pw_bwd(name, fn, rows, consts, cots, tm, add_to_first=None, row_dtypes=None, strip=None):
    rows = [_row_of(r) for r in rows]
    T = rows[0][0].shape[0]
    nr, nc = len(rows), len(consts)
    flat_cots = [c for grp in cots for c in grp]
    row_dtypes = row_dtypes or [f32] * nr
    n_extra = 0 if add_to_first is None else 1
    width = rows[0][1]

    def body(*refs):
        i = pl.program_id(0)
        in_refs, cot_refs = refs[:nr + nc], refs[nr + nc:nr + nc + len(flat_cots)]
        extra_ref = refs[nr + nc + len(flat_cots)] if add_to_first is not None else None
        row_out = refs[nr + nc + len(flat_cots) + n_extra:][:nr]
        const_out = refs[nr + nc + len(flat_cots) + n_extra + nr:]

        @pl.when(i == 0)
        def _():
            for q in range(nc):
                const_out[q][...] = jnp.zeros_like(const_out[q])

        def part(sl):
            cot_vals, p = [], 0
            for grp in cots:
                acc = cot_refs[p][:, sl]
                for q in range(1, len(grp)):
                    acc = acc + cot_refs[p + q][:, sl]
                p += len(grp)
                cot_vals.append(acc)
            _, vjp = jax.vjp(fn, *[r[:, sl] for r in in_refs])
            grads = vjp(tuple(cot_vals))
            for q in range(nr):
                g = grads[q]
                if q == 0 and extra_ref is not None:
                    g = g + extra_ref[:, sl]
                row_out[q][:, sl] = g.astype(row_dtypes[q])
            for q in range(nc):
                const_out[q][:, sl] += grads[nr + q]

        if strip is None:
            part(slice(None))
        else:
            for j in range(width // strip):
                part(slice(strip * j, strip * (j + 1)))

    ins = [r[0] for r in rows] + list(consts) + flat_cots
    specs = ([_row_spec(tm, r[1], r[2]) for r in rows] + [_full_spec(c.shape) for c in consts]
             + [_row_spec(tm, c.shape[1]) for c in flat_cots])
    if add_to_first is not None:
        ins.append(add_to_first)
        specs.append(_row_spec(tm, add_to_first.shape[1]))
    out_shapes = ([jax.ShapeDtypeStruct((T, r[1]), d) for r, d in zip(rows, row_dtypes)]
                  + [jax.ShapeDtypeStruct(c.shape, f32) for c in consts])
    out_specs = [_row_spec(tm, r[1]) for r in rows] + [_full_spec(c.shape) for c in consts]
    outs = pl.pallas_call(
        body, grid=(T // tm,), in_specs=specs, out_specs=out_specs, out_shape=out_shapes, name=name,
        compiler_params=_params("arbitrary"))(*ins)
    return list(outs[:nr]), list(outs[nr:])


def _pw_conv_bwd(name, fn, rows, consts, cots, conv_w, tm, row_dtypes=None):
    T, W0 = rows[0].shape
    nr, nc = len(rows), len(consts)
    taps = conv_w.shape[0]
    nblk = T // tm
    flat_cots = [c for grp in cots for c in grp]
    row_dtypes = row_dtypes or [f32] * nr

    def body(*refs):
        i = pl.program_id(0)
        p = 0
        cur = [r[...] for r in refs[p:p + nr]]; p += nr
        nxt = [r[...] for r in refs[p:p + nr]]; p += nr
        prev = jnp.where(i > 0, refs[p][...], 0.0); p += 1
        w_ref = refs[p]; p += 1
        cvals = [r[...] for r in refs[p:p + nc]]; p += nc

        def summed(p0):
            out, q = [], p0
            for grp in cots:
                acc = refs[q][...]
                for t in range(1, len(grp)):
                    acc = acc + refs[q + t][...]
                q += len(grp)
                out.append(acc)
            return out, q

        cot_cur, p = summed(p)
        cot_nxt, p = summed(p)
        row_out, dw_ref, const_out = refs[p:p + nr], refs[p + nr], refs[p + nr + 1:]

        x_cur = cur[0]
        x_down = [_shift_down(x_cur, prev, s_) for s_ in range(taps)]
        _, vjp = jax.vjp(fn, _conv_apply(x_cur, prev, w_ref, x_down), *cur[1:], *cvals)
        grads = vjp(tuple(cot_cur))
        _, vjp_n = jax.vjp(fn, _conv_apply(nxt[0], x_cur[tm - HALO:], w_ref), *nxt[1:], *cvals)
        dc_n = jnp.where(i < nblk - 1, vjp_n(tuple(cot_nxt))[0], 0.0)
        dc = grads[0]

        @pl.when(i == 0)
        def _():
            dw_ref[...] = jnp.zeros_like(dw_ref)
            for q in range(nc):
                const_out[q][...] = jnp.zeros_like(const_out[q])

        dx = None
        for k in range(taps):
            s_ = taps - 1 - k
            term = _shift_up(dc, dc_n, s_) * w_ref[pl.ds(k, 1), :]
            dx = term if dx is None else dx + term
            dw_ref[pl.ds(k, 1), :] += jnp.sum(dc * x_down[s_], axis=0, keepdims=True)
        row_out[0][...] = dx.astype(row_dtypes[0])
        for q in range(1, nr):
            row_out[q][...] = grads[q].astype(row_dtypes[q])
        for q in range(nc):
            const_out[q][...] += grads[nr + q]

    ins = list(rows) + list(rows) + [rows[0], conv_w] + list(consts) + flat_cots + flat_cots
    specs = ([_row_spec(tm, r.shape[1]) for r in rows] + [_next_spec(tm, r.shape[1], T) for r in rows]
             + [_prev_spec(tm, W0), _full_spec(conv_w.shape)] + [_full_spec(c.shape) for c in consts]
             + [_row_spec(tm, c.shape[1]) for c in flat_cots] + [_next_spec(tm, c.shape[1], T) for c in flat_cots])
    out_shapes = ([jax.ShapeDtypeStruct(r.shape, d) for r, d in zip(rows, row_dtypes)]
                  + [jax.ShapeDtypeStruct(conv_w.shape, f32)] + [jax.ShapeDtypeStruct(c.shape, f32) for c in consts])
    out_specs = ([_row_spec(tm, r.shape[1]) for r in rows] + [_full_spec(conv_w.shape)]
                 + [_full_spec(c.shape) for c in consts])
    outs = pl.pallas_call(
        body, grid=(nblk,), in_specs=specs, out_specs=out_specs, out_shape=out_shapes, name=name,
        compiler_params=_params("arbitrary"))(*ins)
    return list(outs[:nr]), outs[nr], list(outs[nr + 1:])


def _sigmoid(x):
    return 0.5 * jnp.tanh(0.5 * x) + 0.5


def _softplus(x):
    return jnp.maximum(x, 0.0) + jnp.log(1.0 + jnp.exp(jnp.minimum(x, -x)))


def _seg_sum_impl(x, seg):
    w = x.shape[-1]
    r = lax.broadcasted_iota(jnp.int32, (w, w), 0) // seg
    c = lax.broadcasted_iota(jnp.int32, (w, w), 1) // seg
    ones = (r == c).astype(bf16)
    hi = x.astype(bf16)
    lo = (x - hi.astype(f32)).astype(bf16)
    return (jnp.dot(hi, ones, preferred_element_type=f32) + jnp.dot(lo, ones, preferred_element_type=f32))


@functools.partial(jax.custom_vjp, nondiff_argnums=(1,))
def _seg_sum(x, seg):
    return _seg_sum_impl(x, seg)


_seg_sum.defvjp(lambda x, seg: (_seg_sum_impl(x, seg), None), lambda seg, _, g: (_seg_sum_impl(g, seg),))


def _rms(x, g):
    return x * lax.rsqrt(jnp.mean(x * x, axis=-1, keepdims=True) + NORM_EPS) * g


def _rms_fn(x, g):
    return (_rms(x, g),)


def _loss_rows(x2, tgt, g):
    e = _rms(x2, g) - tgt
    return 0.5 * jnp.sum(e * e, axis=-1, keepdims=True) * (1.0 / D_MODEL)


@jax.custom_vjp
def _dot_lo(a, b):
    return jnp.dot(a.astype(bf16), b.astype(bf16), preferred_element_type=f32)


def _dot_lo_bwd(ab, g):
    a, b = ab
    gl = g.astype(bf16)
    return (lax.dot_general(gl, b.astype(bf16), (((1,), (1,)), ((), ())), preferred_element_type=f32),
            lax.dot_general(a.astype(bf16), gl, (((0,), (0,)), ((), ())), preferred_element_type=f32))


_dot_lo.defvjp(lambda a, b: (_dot_lo(a, b), (a, b)), _dot_lo_bwd)


def _rwkv_prep_fn(ps, w0, w2p, a0, a2p, g2, k_k, k_a):
    r, k, v = ps[:, 0:512], ps[:, 512:1024], ps[:, 1024:1536]
    wa, gl = ps[:, 1536:1664], ps[:, 1664:1792]
    z = w0 + _dot_lo(jnp.tanh(wa), w2p)
    w_log = -_softplus(-z) - 0.5
    lw = -jnp.exp(w_log)
    a = _sigmoid(a0 + _dot_lo(wa, a2p))
    g = _dot_lo(_sigmoid(gl), g2)
    kx = k * k_k
    kk = kx * lax.rsqrt(_seg_sum(kx * kx, RWKV_HD) + L2_EPS)
    k2 = k * (1.0 + (a - 1.0) * k_a)
    return r, lw, k2, v, -kk, kk * a, g


def _rwkv_post_fn(y, r, k2, v, g, ln_w, ln_b, rk):
    mean = _seg_sum(y, RWKV_HD) * (1.0 / RWKV_HD)
    yc = y - mean
    var = _seg_sum(yc * yc, RWKV_HD) * (1.0 / RWKV_HD)
    yn = yc * lax.rsqrt(var + GN_EPS) * ln_w + ln_b
    bonus = _seg_sum(r * k2 * rk, RWKV_HD) * v
    return ((yn + bonus) * g,)


def _gdn_prep_fn(cq, ck, cv):
    silu = lambda c: c * _sigmoid(c)
    q, k = silu(cq), silu(ck)
    q = q * lax.rsqrt(jnp.sum(q * q, axis=-1, keepdims=True) + L2_EPS) * (GDN_HD ** -0.5)
    k = k * lax.rsqrt(jnp.sum(k * k, axis=-1, keepdims=True) + L2_EPS)
    return q, k, silu(cv)


def _gdn_gate_fn(ab, al_p, dt_p):
    lane = lax.broadcasted_iota(jnp.int32, ab.shape, 1)
    gpart = -jnp.exp(al_p) * _softplus(ab + dt_p)
    return (jnp.where(lane < GDN_HEADS, gpart, jnp.where(lane < 2 * GDN_HEADS, _sigmoid(ab), 0.0)),)


def _gdn_post_fn(o, z, nw):
    ms = _seg_sum(o * o, GDN_HD) * (1.0 / GDN_HD)
    return (o * lax.rsqrt(ms + NORM_EPS) * nw * (z * _sigmoid(z)),)


def _mix_fn(ga, gb, ya, yb):
    return (_sigmoid(ga) * ya + _sigmoid(gb) * yb,)


STRIP = 128


def _strip_conv(ref, prev_ref, w_ref, sl, first, taps):
    cur = ref[:, sl]
    prev = jnp.where(first, 0.0, prev_ref[:, sl])
    down = [_shift_down(cur, prev, s) for s in range(taps)]
    conv = None
    for k in range(taps):
        term = down[taps - 1 - k] * w_ref[pl.ds(k, 1), sl]
        conv = term if conv is None else conv + term
    return cur, down, conv


def _group_fwd(name, fn, x, w, shared_cols, group_cols, consts, n_out, tm):
    T, W = x.shape
    taps = w.shape[0]
    n_groups = len(group_cols)
    nc = len(consts)

    def body(x_ref, xp_ref, w_ref, *refs):
        const_refs, out_refs = refs[:nc], refs[nc:]
        first = pl.program_id(0) == 0
        shared = [_strip_conv(x_ref, xp_ref, w_ref, sl, first, taps)[2] for sl in shared_cols]
        for j, cols in enumerate(group_cols):
            sl = slice(STRIP * j, STRIP * (j + 1))
            convs = [_strip_conv(x_ref, xp_ref, w_ref, c, first, taps)[2] for c in cols]
            outs = fn(*convs, *shared, *[c[:, sl] for c in const_refs])
            for o_ref, o in zip(out_refs, outs):
                o_ref[:, sl] = o

    return pl.pallas_call(
        body, grid=(T // tm,),
        in_specs=[_row_spec(tm, W), _prev_spec(tm, W), _full_spec(w.shape)] + [_full_spec(c.shape) for c in consts],
        out_specs=[_row_spec(tm, STRIP * n_groups)] * n_out,
        out_shape=[jax.ShapeDtypeStruct((T, STRIP * n_groups), f32)] * n_out, name=name,
        compiler_params=_params("parallel"))(x, x, w, *consts)


def _group_bwd(name, fn, x, w, shared_cols, group_cols, consts, cots, tm):
    T, W = x.shape
    taps = w.shape[0]
    nblk = T // tm
    nc, ns = len(consts), len(shared_cols)
    flat_cots = [c for grp in cots for c in grp]
    n_cot = len(flat_cots)

    def body(x_ref, xp_ref, xn_ref, w_ref, *refs):
        const_refs, refs = refs[:nc], refs[nc:]
        cot_refs, cotn_refs, refs = refs[:n_cot], refs[n_cot:2 * n_cot], refs[2 * n_cot:]
        dx_ref, dw_ref, const_out = refs[0], refs[1], refs[2:]
        i = pl.program_id(0)
        first, last = i == 0, i == nblk - 1

        @pl.when(first)
        def _():
            dw_ref[...] = jnp.zeros_like(dw_ref)
            for q in range(nc):
                const_out[q][...] = jnp.zeros_like(const_out[q])

        def convs_of(sl):
            cur, down, conv = _strip_conv(x_ref, xp_ref, w_ref, sl, first, taps)
            nxt, conv_n = xn_ref[:, sl], None
            for k in range(taps):
                term = _shift_down(nxt, cur[tm - HALO:], taps - 1 - k) * w_ref[pl.ds(k, 1), sl]
                conv_n = term if conv_n is None else conv_n + term
            return down, conv, conv_n

        def conv_back(sl, down, dc, dc_n):
            dx = None
            for k in range(taps):
                s_ = taps - 1 - k
                term = _shift_up(dc, dc_n, s_) * w_ref[pl.ds(k, 1), sl]
                dx = term if dx is None else dx + term
                dw_ref[pl.ds(k, 1), sl] += jnp.sum(dc * down[s_], axis=0, keepdims=True)
            dx_ref[:, sl] = dx.astype(dx_ref.dtype)

        def summed(refs_, sl, mask):
            out, p = [], 0
            for grp in cots:
                acc = refs_[p][:, sl]
                for t in range(1, len(grp)):
                    acc = acc + refs_[p + t][:, sl]
                p += len(grp)
                out.append(jnp.where(last, 0.0, acc) if mask else acc)
            return tuple(out)

        shared = [convs_of(sl) for sl in shared_cols]
        d_shared, d_shared_n = [None] * ns, [None] * ns
        for j, cols in enumerate(group_cols):
            sl = slice(STRIP * j, STRIP * (j + 1))
            mine = [convs_of(c) for c in cols]
            cj = [c[:, sl] for c in const_refs]
            _, vjp = jax.vjp(fn, *[m[1] for m in mine], *[m[1] for m in shared], *cj)
            grads = vjp(summed(cot_refs, sl, False))
            _, vjp_n = jax.vjp(fn, *[m[2] for m in mine], *[m[2] for m in shared], *cj)
            grads_n = vjp_n(summed(cotn_refs, sl, True))
            for q, c in enumerate(cols):
                conv_back(c, mine[q][0], grads[q], grads_n[q])
            for q in range(ns):
                g, gn = grads[len(cols) + q], grads_n[len(cols) + q]
                d_shared[q] = g if d_shared[q] is None else d_shared[q] + g
                d_shared_n[q] = gn if d_shared_n[q] is None else d_shared_n[q] + gn
            for q in range(nc):
                const_out[q][:, sl] += grads[len(cols) + ns + q]
        for q, c in enumerate(shared_cols):
            conv_back(c, shared[q][0], d_shared[q], d_shared_n[q])

    outs = pl.pallas_call(
        body, grid=(nblk,),
        in_specs=[_row_spec(tm, W), _prev_spec(tm, W), _next_spec(tm, W, T), _full_spec(w.shape)]
        + [_full_spec(c.shape) for c in consts] + [_row_spec(tm, c.shape[1]) for c in flat_cots]
        + [_next_spec(tm, c.shape[1], T) for c in flat_cots],
        out_specs=[_row_spec(tm, W), _full_spec(w.shape)] + [_full_spec(c.shape) for c in consts],
        out_shape=[jax.ShapeDtypeStruct((T, W), bf16), jax.ShapeDtypeStruct(w.shape, f32)]
        + [jax.ShapeDtypeStruct(c.shape, f32) for c in consts], name=name,
        compiler_params=_params("arbitrary"))(x, x, x, w, *consts, *flat_cots, *flat_cots)
    return outs[0], outs[1], list(outs[2:])


def _ffn_strip_fn(cg, cu):
    return cg * _sigmoid(cg) * cu


def _ffn_act_fwd(h, w, tm):
    T, W2 = h.shape
    H = W2 // 2
    taps = w.shape[0]

    def body(h_ref, hp_ref, w_ref, o_ref):
        first = pl.program_id(0) == 0
        for j in range(H // STRIP):
            gs, us = slice(STRIP * j, STRIP * (j + 1)), slice(H + STRIP * j, H + STRIP * (j + 1))
            cg = _strip_conv(h_ref, hp_ref, w_ref, gs, first, taps)[2]
            cu = _strip_conv(h_ref, hp_ref, w_ref, us, first, taps)[2]
            o_ref[:, gs] = _ffn_strip_fn(cg, cu).astype(o_ref.dtype)

    return pl.pallas_call(
        body, grid=(T // tm,), in_specs=[_row_spec(tm, W2), _prev_spec(tm, W2), _full_spec(w.shape)],
        out_specs=_row_spec(tm, H), out_shape=jax.ShapeDtypeStruct((T, H), bf16), name="ffn_act",
        compiler_params=_params("parallel"))(h, h, w)


def _ffn_act_bwd(h, dact, w, tm):
    T, W2 = h.shape
    H = W2 // 2
    taps = w.shape[0]
    nblk = T // tm

    def body(h_ref, hp_ref, hn_ref, d_ref, dn_ref, w_ref, dh_ref, dw_ref):
        i = pl.program_id(0)
        first, last = i == 0, i == nblk - 1

        @pl.when(first)
        def _():
            dw_ref[...] = jnp.zeros_like(dw_ref)

        for j in range(H // STRIP):
            gs, us = slice(STRIP * j, STRIP * (j + 1)), slice(H + STRIP * j, H + STRIP * (j + 1))
            parts = {}
            for name, sl in (('g', gs), ('u', us)):
                cur, down, conv = _strip_conv(h_ref, hp_ref, w_ref, sl, first, taps)
                nxt = hn_ref[:, sl]
                conv_n = None
                for k in range(taps):
                    term = _shift_down(nxt, cur[tm - HALO:], taps - 1 - k) * w_ref[pl.ds(k, 1), sl]
                    conv_n = term if conv_n is None else conv_n + term
                parts[name] = (down, conv, conv_n)
            _, vjp = jax.vjp(_ffn_strip_fn, parts['g'][1], parts['u'][1])
            dcs = vjp(d_ref[:, gs])
            _, vjp_n = jax.vjp(_ffn_strip_fn, parts['g'][2], parts['u'][2])
            dcs_n = vjp_n(jnp.where(last, 0.0, dn_ref[:, gs]))
            for (name, sl), dc, dc_n in zip((('g', gs), ('u', us)), dcs, dcs_n):
                down = parts[name][0]
                dx = None
                for k in range(taps):
                    s_ = taps - 1 - k
                    term = _shift_up(dc, dc_n, s_) * w_ref[pl.ds(k, 1), sl]
                    dx = term if dx is None else dx + term
                    dw_ref[pl.ds(k, 1), sl] += jnp.sum(dc * down[s_], axis=0, keepdims=True)
                dh_ref[:, sl] = dx.astype(dh_ref.dtype)

    return pl.pallas_call(
        body, grid=(nblk,),
        in_specs=[_row_spec(tm, W2), _prev_spec(tm, W2), _next_spec(tm, W2, T), _row_spec(tm, H), _next_spec(tm, H, T),
                  _full_spec(w.shape)],
        out_specs=[_row_spec(tm, W2), _full_spec(w.shape)],
        out_shape=[jax.ShapeDtypeStruct((T, W2), bf16), jax.ShapeDtypeStruct(w.shape, f32)], name="ffn_act_bwd",
        compiler_params=_params("arbitrary"))(h, h, h, dact, dact, w)


N_POS = 4


def _xy_out_shapes(bufs, scatter):
    return [jax.ShapeDtypeStruct((N_POS,) + tuple(b.shape[1:] if scatter else b.shape), b.dtype) for b in bufs]


def _xy_sems(n, scatter):
    sems = [pltpu.SemaphoreType.DMA((3 * n,)), pltpu.SemaphoreType.DMA((3 * n,)), pltpu.SemaphoreType.DMA((n,))]
    return sems if scatter else sems + [pltpu.SemaphoreType.DMA((3 * n,)), pltpu.SemaphoreType.DMA((3 * n,))]


def _xy_copies(in_refs, out_refs, sems, scatter):
    n = len(in_refs)
    send_sems, recv_sems, local_sems = sems[:3]

    def place():
        x, y, c = lax.axis_index("x"), lax.axis_index("y"), lax.axis_index("c")
        return x, y, c, 2 * x + y, [(1 - x, y), (x, 1 - y), (1 - x, 1 - y)]

    def half(ref, a, which):
        rows = in_refs[a].shape[0] // 2
        return ref.at[pl.ds(pl.multiple_of(which * rows, HALO), rows)]

    def ici(a, k, src, dst, peer, c):
        return pltpu.make_async_remote_copy(
            src_ref=src, dst_ref=dst, send_sem=send_sems.at[3 * a + k], recv_sem=recv_sems.at[3 * a + k],
            device_id=(peer[0], peer[1], c), device_id_type=pl.DeviceIdType.MESH)

    def outgoing():
        x, y, c, me, peers = place()
        own = [pltpu.make_async_copy(in_refs[a].at[me] if scatter else in_refs[a], out_refs[a].at[me], local_sems.at[a])
               for a in range(n)]
        if scatter:
            sends = [ici(a, k, in_refs[a].at[2 * p[0] + p[1]], out_refs[a].at[me], p, c)
                     for a in range(n) for k, p in enumerate(peers)]
        else:
            sends = [ici(a, k, half(in_refs[a], a, c), half(out_refs[a].at[me], a, c), p, c)
                     for a in range(n) for k, p in enumerate(peers)]
        return own, sends

    def arrivals():
        x, y, c, me, peers = place()
        if scatter:
            return [ici(a, k, in_refs[a].at[me], out_refs[a].at[2 * p[0] + p[1]], p, c)
                    for a in range(n) for k, p in enumerate(peers)]
        return [ici(a, k, half(in_refs[a], a, c), half(out_refs[a].at[2 * p[0] + p[1]], a, c), p, c)
                for a in range(n) for k, p in enumerate(peers)]

    def to_sibling(mine):
        x, y, c, me, peers = place()
        which = c if mine else 1 - c
        return [pltpu.make_async_remote_copy(
            src_ref=half(out_refs[a].at[2 * p[0] + p[1]], a, which), dst_ref=half(out_refs[a].at[2 * p[0] + p[1]], a, which),
            send_sem=sems[3].at[3 * a + k], recv_sem=sems[4].at[3 * a + k],
            device_id=(x, y, 1 - c), device_id_type=pl.DeviceIdType.MESH) for a in range(n) for k, p in enumerate(peers)]

    def start():
        own, sends = outgoing()
        for cp in own + sends:
            cp.start()

    def finish():
        if scatter:
            for cp in arrivals():
                cp.wait_recv()
        else:
            passed = to_sibling(True)
            for cp, fwd in zip(arrivals(), passed):
                cp.wait_recv()
                fwd.start()
            for cp in to_sibling(False):
                cp.wait_recv()
            for fwd in passed:
                fwd.wait_send()
        own, sends = outgoing()
        for cp in sends:
            cp.wait_send()
        for cp in own:
            cp.wait()

    return start, finish


_NN, _NT, _TN = 'hcs,hsd->hcd', 'hcd,hsd->hcs', 'hcd,hce->hde'


def _lo(spec, a, b):
    return jnp.einsum(spec, a.astype(bf16), b.astype(bf16), preferred_element_type=f32)


@jax.custom_vjp
def _bmm(a, b):
    return _lo(_NN, a, b)


_bmm.defvjp(lambda a, b: (_lo(_NN, a, b), (a, b)), lambda ab, g: (_lo(_NT, g, ab[1]), _lo(_TN, ab[0], g)))


@jax.custom_vjp
def _bmm_nt(a, b):
    return _lo(_NT, a, b)


_bmm_nt.defvjp(lambda a, b: (_lo(_NT, a, b), (a, b)), lambda ab, g: (_lo(_NN, g, ab[1]), _lo(_TN, g, ab[0])))


@jax.custom_vjp
def _bmm_tn(a, b):
    return _lo(_TN, a, b)


_bmm_tn.defvjp(lambda a, b: (_lo(_TN, a, b), (a, b)), lambda ab, g: (_lo(_NT, ab[1], g), _lo(_NN, ab[0], g)))


def _masks(H, C):
    row = lax.broadcasted_iota(jnp.int32, (H, C, C), 1)
    col = lax.broadcasted_iota(jnp.int32, (H, C, C), 2)
    return row, col


def _tri_inv_impl(L):
    H, C, _ = L.shape
    row, col = _masks(H, C)
    eye = (row == col).astype(f32)
    base = 16
    same = (row // base) == (col // base)
    Ld = jnp.where(same, L, 0.0)
    X = -Ld
    inv = eye + X
    for _ in range(3):
        X = _bmm(X, X)
        inv = _bmm(inv, eye + X)
    if C == base:
        return inv
    N = _bmm(inv, L - Ld)
    out = eye - N
    levels = C // base
    P = N
    span = 2
    while span < levels:
        P = _bmm(P, P)
        out = _bmm(out, eye + P)
        span *= 2
    return _bmm(out, inv)


@jax.custom_vjp
def _tri_inv(L):
    return _tri_inv_impl(L)


def _tri_inv_fwd(L):
    T = _tri_inv_impl(L)
    return T, T


def _tri_inv_bwd(T, dT):
    return (-_bmm_nt(_bmm_tn(T, dT), T),)


_tri_inv.defvjp(_tri_inv_fwd, _tri_inv_bwd)


def _cumsum_impl(x, reverse):
    C = x.shape[1]
    row = lax.broadcasted_iota(jnp.int32, x.shape, 1)
    s = 1
    while s < C:
        if reverse:
            x = x + jnp.where(row < C - s, pltpu.roll(x, C - s, 1), 0.0)
        else:
            x = x + jnp.where(row >= s, pltpu.roll(x, s, 1), 0.0)
        s *= 2
    return x


@jax.custom_vjp
def _cumsum(x):
    return _cumsum_impl(x, False)


_cumsum.defvjp(lambda x: (_cumsum_impl(x, False), None), lambda _, g: (_cumsum_impl(g, True),))


def _wkv_prep(r, lw, k, v, a, b):
    lane = lax.broadcasted_iota(jnp.int32, (r.shape[0], 128), 1)
    low = lane < RWKV_HD

    def heads(t):
        out = []
        for p in range(RWKV_HEADS // 2):
            pair = t[:, 128 * p:128 * (p + 1)]
            out += [jnp.where(low, pair, 0.0), jnp.where(low, 0.0, pair)]
        return jnp.concatenate([t[None] for t in out], axis=0)

    r, lw, k, v, a, b = [heads(t) for t in (r, lw, k, v, a, b)]
    H, C, D = r.shape
    row, col = _masks(H, C)
    incl, strict = row >= col, row > col
    cw = _cumsum(lw)
    cwp = cw - lw
    cwl = jnp.sum(lw, axis=1, keepdims=True)
    en = jnp.exp(-cw)
    at, rt, bt, kt = a * jnp.exp(cwp), r * jnp.exp(cw), b * en, k * en
    Tm = _tri_inv(-jnp.where(strict, _bmm_nt(at, bt), 0.0))
    ar = jnp.concatenate([at, rt], axis=1)
    gram = _bmm_nt(ar, jnp.concatenate([bt, kt], axis=1))
    row2 = lax.broadcasted_iota(jnp.int32, (H, 2 * C, 2 * C), 1)
    col2 = lax.broadcasted_iota(jnp.int32, (H, 2 * C, 2 * C), 2) % C
    gram = jnp.where(((row2 < C) & (row2 > col2)) | ((row2 >= C) & (row2 - C >= col2)), gram, 0.0)
    a_bk, r_bk = gram[:, :C], gram[:, C:]
    lak_v = _bmm(a_bk, jnp.concatenate([jnp.zeros_like(v), v], axis=1))
    ed = jnp.exp(cwl - cw)
    zdec = jnp.swapaxes(jnp.broadcast_to(jnp.exp(cwl), (H, D, D)), 1, 2)
    return ar, Tm, lak_v, r_bk, jnp.concatenate([b * ed, k * ed], axis=1), zdec, v


def _wkv_step(Z, ar, Tm, lak_v, r_bk, bk_d, zdec, v):
    C = Tm.shape[1]
    ar_z = _bmm(ar, Z)
    uv = jnp.concatenate([_bmm(Tm, ar_z[:, :C] + lak_v), v], axis=1)
    y = ar_z[:, C:] + _bmm(r_bk, uv)
    Z1 = Z * zdec + _bmm_tn(bk_d, uv)
    return jnp.concatenate([y[2 * p] + y[2 * p + 1] for p in range(RWKV_HEADS // 2)], axis=1), Z1


def _gdn_prep(q, k, v, gbeta):
    heads = lambda t: jnp.concatenate([t[None, :, GDN_HD * h:GDN_HD * (h + 1)] for h in range(GDN_HEADS)], axis=0)
    src = lax.broadcasted_iota(jnp.int32, (W_AB, 2 * GDN_W), 0)
    dst = lax.broadcasted_iota(jnp.int32, (W_AB, 2 * GDN_W), 1) // GDN_HD
    spread = jnp.dot(gbeta, (src == dst).astype(f32), precision=HI, preferred_element_type=f32)
    q, k, v, g, beta = heads(q), heads(k), heads(v), heads(spread[:, :GDN_W]), heads(spread[:, GDN_W:])
    H, C, D = q.shape
    row, col = _masks(H, C)
    incl, strict = row >= col, row > col
    gc = _cumsum(g)
    diff = gc - jnp.swapaxes(gc, 1, 2)
    decay = jnp.where(incl, jnp.exp(jnp.where(incl, diff, 0.0)), 0.0)
    gl = jnp.sum(g, axis=1, keepdims=True)
    kb, vb = k * beta, v * beta
    gram = _bmm_nt(jnp.concatenate([kb, q], axis=1), k)
    L = jnp.where(strict, gram[:, :C] * decay, 0.0)
    attn = jnp.where(incl, gram[:, C:] * decay, 0.0)
    egc = jnp.exp(gc)
    t_vk = _bmm(_tri_inv(L), jnp.concatenate([vb, kb * egc], axis=2))
    return t_vk[:, :, :D], jnp.concatenate([t_vk[:, :, D:], q * egc], axis=1), attn, k * jnp.exp(gl - gc), jnp.exp(gl)


def _gdn_step(S, u, wq, attn, ke, sdec):
    C = u.shape[1]
    wq_s = _bmm(wq, S)
    v_new = u - wq_s[:, :C]
    o = wq_s[:, C:] + _bmm(attn, v_new)
    S1 = S * sdec + _bmm_tn(ke, v_new)
    return jnp.concatenate([o[h] for h in range(GDN_HEADS)], axis=1), S1


def _scan_fwd(name, fns, ins, C, H, dh, w_out, per_step, side=None):
    prep, step = fns
    T = ins[0].shape[0]
    n_in = len(ins)
    blk = C * per_step
    nblk = T // blk
    n_side = 0 if side is None else len(side[0])

    def body(*refs):
        in_refs, refs = refs[:n_in], refs[n_in:]
        side_in, refs = refs[:n_side], refs[n_side:]
        y_ref, zs_ref, refs = refs[0], refs[1], refs[2:]
        side_out, refs = refs[:n_side], refs[n_side:]
        z_scr = refs[0]
        if side is not None:
            start, finish = _xy_copies(side_in, side_out, refs[1:], side[1])
            pl.when(pl.program_id(0) == 0)(start)

        @pl.when(pl.program_id(0) == 0)
        def _():
            z_scr[...] = jnp.zeros_like(z_scr)

        rows = [slice(C * j, C * (j + 1)) for j in range(per_step)]
        prepped = [prep(*[r[rw, :] for r in in_refs]) for rw in rows]
        Z = z_scr[...]
        for j, rw in enumerate(rows):
            zs_ref[j] = Z
            y, Z = step(Z, *prepped[j])
            y_ref[rw, :] = y
        z_scr[...] = Z
        if side is not None:
            pl.when(pl.program_id(0) == nblk - 1)(finish)

    side_bufs = [] if side is None else list(side[0])
    any_spec = pl.BlockSpec(memory_space=pl.ANY)
    return pl.pallas_call(
        body, grid=(nblk,),
        in_specs=[pl.BlockSpec((blk, a.shape[1]), lambda i: (i, 0)) for a in ins] + [any_spec] * n_side,
        out_specs=[pl.BlockSpec((blk, w_out), lambda i: (i, 0)), pl.BlockSpec((per_step, H, dh, dh), lambda i: (i, 0, 0, 0))]
        + [any_spec] * n_side,
        out_shape=[jax.ShapeDtypeStruct((T, w_out), f32), jax.ShapeDtypeStruct((T // C, H, dh, dh), f32)]
        + (_xy_out_shapes(side_bufs, side[1]) if side is not None else []),
        scratch_shapes=[pltpu.VMEM((H, dh, dh), f32)] + (_xy_sems(n_side, side[1]) if side is not None else []), name=name,
        compiler_params=_params("arbitrary"))(*ins, *side_bufs)


def _scan_bwd(name, fns, ins, dy, zs, C, per_step, side=None):
    prep, step = fns
    T = ins[0].shape[0]
    _, H, dh, _ = zs.shape
    n_in = len(ins)
    blk = C * per_step
    nblk = T // blk
    n_side = 0 if side is None else len(side[0])

    def body(*refs):
        in_refs, dy_ref, zs_ref, refs = refs[:n_in], refs[n_in], refs[n_in + 1], refs[n_in + 2:]
        side_in, refs = refs[:n_side], refs[n_side:]
        out_refs, refs = refs[:n_in], refs[n_in:]
        side_out, refs = refs[:n_side], refs[n_side:]
        dz_scr = refs[0]
        if side is not None:
            start, finish = _xy_copies(side_in, side_out, refs[1:], side[1])
            pl.when(pl.program_id(0) == 0)(start)

        @pl.when(pl.program_id(0) == 0)
        def _():
            dz_scr[...] = jnp.zeros_like(dz_scr)

        rows = [slice(C * j, C * (j + 1)) for j in range(per_step)]
        prepped = [jax.vjp(prep, *[r[rw, :] for r in in_refs]) for rw in rows]
        d_prepped = [None] * per_step
        dZ = dz_scr[...]
        for j in reversed(range(per_step)):
            _, pull = jax.vjp(step, zs_ref[j], *prepped[j][0])
            dZ, *d_prepped[j] = pull((dy_ref[rows[j], :], dZ))
        dz_scr[...] = dZ
        for j, rw in enumerate(rows):
            for o_ref, gval in zip(out_refs, prepped[j][1](tuple(d_prepped[j]))):
                o_ref[rw, :] = gval
        if side is not None:
            pl.when(pl.program_id(0) == nblk - 1)(finish)

    side_bufs = [] if side is None else list(side[0])
    any_spec = pl.BlockSpec(memory_space=pl.ANY)
    rev = lambda i: (nblk - 1 - i, 0)
    return pl.pallas_call(
        body, grid=(nblk,),
        in_specs=[pl.BlockSpec((blk, a.shape[1]), rev) for a in ins]
        + [pl.BlockSpec((blk, dy.shape[1]), rev), pl.BlockSpec((per_step, H, dh, dh), lambda i: (nblk - 1 - i, 0, 0, 0))]
        + [any_spec] * n_side,
        out_specs=[pl.BlockSpec((blk, a.shape[1]), rev) for a in ins] + [any_spec] * n_side,
        out_shape=[jax.ShapeDtypeStruct(a.shape, f32) for a in ins]
        + (_xy_out_shapes(side_bufs, side[1]) if side is not None else []),
        scratch_shapes=[pltpu.VMEM((H, dh, dh), f32)] + (_xy_sems(n_side, side[1]) if side is not None else []), name=name,
        compiler_params=_params("arbitrary"))(*ins, dy, zs, *side_bufs)


def _loss_call(x2, tgt, g, tm):
    T, W = x2.shape

    def body(x_ref, t_ref, g_ref, dx_ref, dg_ref, l_ref):
        i = pl.program_id(0)
        tv = t_ref[...]
        l, vjp = jax.vjp(lambda xv, gv: _loss_rows(xv, tv, gv), x_ref[...], g_ref[...])
        dx, dg = vjp(jnp.ones_like(l))
        dx_ref[...] = dx
        tot = jnp.zeros((1, 128), f32) + jnp.sum(l)

        @pl.when(i == 0)
        def _():
            dg_ref[...] = dg
            l_ref[...] = tot

        @pl.when(i > 0)
        def _():
            dg_ref[...] += dg
            l_ref[...] += tot

    return pl.pallas_call(
        body, grid=(T // tm,),
        in_specs=[_row_spec(tm, W), _row_spec(tm, W), _full_spec(g.shape)],
        out_specs=[_row_spec(tm, W), _full_spec(g.shape), _full_spec((1, 128))],
        out_shape=[jax.ShapeDtypeStruct((T, W), f32), jax.ShapeDtypeStruct(g.shape, f32),
                   jax.ShapeDtypeStruct((1, 128), f32)], name="loss_head",
        compiler_params=_params("arbitrary"))(x2, tgt, g)


def _local_step(x, tgt, W, late=None):
    row = lambda a: a.reshape(1, -1)
    wp = W['w_in_pad']
    w_rwkv, w_qkv, w_z = wp[:, :OFF_QKV], wp[:, OFF_QKV:OFF_Z], wp[:, OFF_Z:OFF_GATES]
    w_gates, w_ab = wp[:, OFF_GATES:OFF_AB], wp[:, OFF_AB:]
    mu = row(W['rwkv_mu'])
    mixw = jnp.concatenate([mu, 1.0 - mu], axis=0)
    zpad = jnp.zeros((64, RWKV_W), f32)
    w2p = jnp.concatenate([W['rwkv_w2'], zpad], axis=0)
    a2p = jnp.concatenate([zpad, W['rwkv_a2']], axis=0)
    rw_consts = [row(W['rwkv_w0']), w2p, row(W['rwkv_a0']), a2p, W['rwkv_g2'], row(W['rwkv_k_k']), row(W['rwkv_k_a'])]
    post_consts = [row(W['rwkv_ln_w']), row(W['rwkv_ln_b']), row(W['rwkv_r_k'])]
    pad4 = lambda a: jnp.pad(row(a), ((0, 0), (0, W_AB - GDN_HEADS)))
    gd_consts = [pad4(W['gdn_a_log']), pad4(W['gdn_dt_bias'])]
    nw_t = jnp.tile(row(W['gdn_norm_w']), (1, GDN_HEADS))
    g1, g2n, gf = row(W['norm1_g']), row(W['norm2_g']), row(W['final_g'])

    (u,) = _pw_fwd("norm1", _rms_fn, [x], [g1], [D_MODEL], 256, out_dtype=bf16)
    p_rwkv = _mm(u, w_rwkv, 'nn', "in_rwkv")
    qkv_raw = _mm(u, w_qkv, 'nn', "in_qkv")
    z = _mm(u, w_z, 'nn', "in_z")
    gates = _mm(u, w_gates, 'nn', "in_gates")
    ab = _mm(u, w_ab, 'nn', "in_ab")

    r, lw, k2, v, a_, b_, g = _pw_fwd("rwkv_prep", _rwkv_prep_fn, [p_rwkv], rw_consts, [RWKV_W] * 7, 256, conv_w=mixw)
    wkv_in = [r, lw, k2, v, a_, b_]
    y, zs_wkv, *gathered = _scan_fwd("wkv_fwd", (_wkv_prep, _wkv_step), wkv_in, WKV_CHUNK, RWKV_HEADS, 2 * RWKV_HD, RWKV_W, WKV_PER_STEP,
                                     side=None if late is None else (late['shards'][0], False))
    if late is not None:
        W = dict(W, **late['assemble'](0, gathered))
    (ya_in,) = _pw_fwd("rwkv_post", _rwkv_post_fn, [y, r, k2, v, g], post_consts, [RWKV_W], 256, out_dtype=bf16, strip=128)
    ya = _mm(ya_in, W['rwkv_proj'], 'nn', "rwkv_proj")

    lanes = lambda off: slice(off, off + STRIP)
    gd_groups = [[lanes(GDN_HD * h), lanes(GDN_W + GDN_HD * h), lanes(2 * GDN_W + GDN_HD * h)] for h in range(GDN_HEADS)]
    gq, gk, gv = _group_fwd("gdn_prep", _gdn_prep_fn, qkv_raw, W['gdn_conv_w'], [], gd_groups, [], 3, 256)
    (gbeta,) = _pw_fwd("gdn_gate", _gdn_gate_fn, [ab], gd_consts, [W_AB], 256)
    gdn_in = [gq, gk, gv, gbeta]
    o, zs_gdn, *gathered = _scan_fwd("gdn_fwd", (_gdn_prep, _gdn_step), gdn_in, GDN_CHUNK, GDN_HEADS, GDN_HD, GDN_W, GDN_PER_STEP,
                                     side=None if late is None else (late['shards'][1], False))
    if late is not None:
        W = dict(W, **late['assemble'](1, gathered))
    (yb_in,) = _pw_fwd("gdn_post", _gdn_post_fn, [o, z], [nw_t], [GDN_W], 256, out_dtype=bf16, strip=128)
    yb = _mm(yb_in, W['gdn_proj'], 'nn', "gdn_proj")

    ga, gb = _cols(gates, D_MODEL, 0), _cols(gates, D_MODEL, 1)
    (mixed,) = _pw_fwd("mix", _mix_fn, [ga, gb, ya, yb], [], [D_MODEL], 256, out_dtype=bf16, strip=256)
    x1 = _mm(mixed, W['w_out'], 'nn', "w_out", add=x)
    (u2,) = _pw_fwd("norm2", _rms_fn, [x1], [g2n], [D_MODEL], 256, out_dtype=bf16)
    h = _mm(u2, W['ffn_up'], 'nn', "ffn_up")
    act = _ffn_act_fwd(h, W['ffn_conv_w'], 256)
    x2 = _mm(act, W['ffn_down'], 'nn', "ffn_down", add=x1)

    G = {}
    slab_out = None if late is None else N_POS
    dx2, dgf, loss = _loss_call(x2, tgt, gf, 256)
    G['final_g'] = dgf
    dact = _mm(dx2, W['ffn_down'], 'nt', "d_act")
    G['ffn_down'] = _mm(act, dx2, 'tn', "g_ffn_down", out_dtype=bf16)
    dh, G['ffn_conv_w'] = _ffn_act_bwd(h, dact, W['ffn_conv_w'], 128)
    du2 = _mm(dh, W['ffn_up'], 'nt', "d_u2")
    G['ffn_up'] = _mm(u2, dh, 'tn', "g_ffn_up", out_dtype=bf16, col_slabs=slab_out)
    (dx1,), (G['norm2_g'],) = _pw_bwd("norm2_bwd", _rms_fn, [x1], [g2n], [(du2,)], 256, add_to_first=dx2)
    dmixed = _mm(dx1, W['w_out'], 'nt', "d_mixed")
    G['w_out'] = _mm(mixed, dx1, 'tn', "g_w_out", out_dtype=bf16)
    (dga, dgb, dya, dyb), _ = _pw_bwd("mix_bwd", _mix_fn, [ga, gb, ya, yb], [], [(dmixed,)], 256, row_dtypes=[bf16] * 4,
                                      strip=256)
    dya_in = _mm(dya, W['rwkv_proj'], 'nt', "d_ya_in")
    G['rwkv_proj'] = _mm(ya_in, dya, 'tn', "g_rwkv_proj", out_dtype=bf16, col_slabs=slab_out)
    dyb_in = _mm(dyb, W['gdn_proj'], 'nt', "d_yb_in")
    G['gdn_proj'] = _mm(yb_in, dyb, 'tn', "g_gdn_proj", out_dtype=bf16, col_slabs=slab_out)

    (do, dz), (dnw_t,) = _pw_bwd("gdn_post_bwd", _gdn_post_fn, [o, z], [nw_t], [(dyb_in,)], 256, row_dtypes=[f32, bf16],
                                 strip=128)
    G['gdn_norm_w'] = dnw_t.reshape(GDN_HEADS, GDN_HD).sum(axis=0)
    dgq, dgk, dgv, dgbeta = _scan_bwd("gdn_bwd", (_gdn_prep, _gdn_step), gdn_in, do, zs_gdn, GDN_CHUNK, GDN_PER_STEP)
    dqkv_raw, G['gdn_conv_w'], _ = _group_bwd("gdn_prep_bwd", _gdn_prep_fn, qkv_raw, W['gdn_conv_w'], [], gd_groups, [],
                                              [(dgq,), (dgk,), (dgv,)], 128)
    (dab,), (dal_p, ddt_p) = _pw_bwd("gdn_gate_bwd", _gdn_gate_fn, [ab], gd_consts, [(dgbeta,)], 256, row_dtypes=[bf16])
    G['gdn_a_log'], G['gdn_dt_bias'] = dal_p[0, :GDN_HEADS], ddt_p[0, :GDN_HEADS]

    (dy, dr1, dk21, dv1, dg_), (G['rwkv_ln_w'], G['rwkv_ln_b'], G['rwkv_r_k']) = _pw_bwd(
        "rwkv_post_bwd", _rwkv_post_fn, [y, r, k2, v, g], post_consts, [(dya_in,)], 256, strip=128)
    dr2, dlw, dk22, dv2, da_, db_, *G['_arrived'] = _scan_bwd(
        "wkv_bwd", (_wkv_prep, _wkv_step), wkv_in, dy, zs_wkv, WKV_CHUNK, WKV_PER_STEP, side=None if late is None else (late['slabs'](G), True))
    (dp_rwkv,), dmixw, rw_grads = _pw_conv_bwd(
        "rwkv_prep_bwd", _rwkv_prep_fn, [p_rwkv], rw_consts,
        [(dr1, dr2), (dlw,), (dk21, dk22), (dv1, dv2), (da_,), (db_,), (dg_,)], mixw, 256, row_dtypes=[bf16])
    G['rwkv_w0'], dw2p, G['rwkv_a0'], da2p, G['rwkv_g2'], G['rwkv_k_k'], G['rwkv_k_a'] = rw_grads
    G['rwkv_w2'], G['rwkv_a2'] = dw2p[:64], da2p[64:]
    G['rwkv_mu'] = dmixw[0] - dmixw[1]

    dp = jnp.concatenate([dp_rwkv, dqkv_raw, dz, dga, dgb, dab], axis=1)
    G['w_in_pad'] = _mm(u, dp, 'tn', "g_w_in", out_dtype=bf16)
    if late is None:
        du = _mm(dp, wp, 'nt', "d_u")
    else:
        du, *G['_arrived_w_in'] = _mm(dp, wp, 'nt', "d_u", side=(late['w_in_slabs'](G), True))
    (dx,), (G['norm1_g'],) = _pw_bwd("norm1_bwd", _rms_fn, [x], [g1], [(du,)], 256, add_to_first=dx1)
    return loss, dx, G


IN_WIDTH = OFF_AB + 8
PAD_ORDER = ((0, OFF_GATES), (OFF_GATES + 8, IN_WIDTH), (OFF_GATES, OFF_GATES + 8))


def _pad_w_in(w):
    return jnp.concatenate([w[:, a:b] for a, b in PAD_ORDER] + [jnp.zeros((w.shape[0], W_AB - 8), w.dtype)], axis=1)


def _pad_w_in_shards(shards):
    width = shards[0].shape[1]
    parts = []
    for a, b in PAD_ORDER:
        for j, sh in enumerate(shards):
            lo, hi = max(a, j * width), min(b, (j + 1) * width)
            if lo < hi:
                parts.append(sh[:, lo - j * width:hi - j * width])
    return jnp.concatenate(parts + [jnp.zeros((shards[0].shape[0], W_AB - 8), shards[0].dtype)], axis=1)


def _unpad_cols(wp, lo, hi):
    parts, off = [], 0
    for a, b in PAD_ORDER:
        l, h = max(a, lo), min(b, hi)
        if l < h:
            parts.append((l, wp[:, off + l - a:off + h - a]))
        off += b - a
    parts.sort(key=lambda t: t[0])
    return parts[0][1] if len(parts) == 1 else jnp.concatenate([p for _, p in parts], axis=1)


def _unpad_w_in(wp):
    return _unpad_cols(wp, 0, IN_WIDTH)


BIG = ('w_in', 'rwkv_proj', 'gdn_proj', 'w_out', 'ffn_up', 'ffn_down')
SMALL_SHARDED = ('rwkv_w2', 'rwkv_a2', 'rwkv_g2', 'gdn_conv_w', 'ffn_conv_w')


def _rows128(shape):
    n = 1
    for d in shape:
        n *= d
    return -(-n // LANES)


def _pack128(arrays):
    parts = []
    for a in arrays:
        flat = a.reshape(-1)
        rows = _rows128(a.shape)
        parts.append(jnp.pad(flat, (0, rows * LANES - flat.shape[0])).reshape(rows, LANES))
    buf = jnp.concatenate(parts, axis=0)
    return jnp.pad(buf, ((0, -buf.shape[0] % HALO), (0, 0)))


def _unpack128(buf, shapes):
    out, off = [], 0
    for s in shapes:
        rows, n = _rows128(s), 1
        for d in s:
            n *= d
        out.append(buf[off:off + rows].reshape(-1)[:n].reshape(s))
        off += rows
    return out


def _param_tile(r, c):
    best = None
    for d in range(2 * HALO, r + 1, 2 * HALO):
        if r % d == 0 and d * c * 4 <= TILE_BYTES:
            best = d
    if best is not None or r * c * 4 <= TILE_BYTES:
        return (best if best is not None else r), c
    return r, 128


def _xy_exchange(name, bufs, scatter):
    n = len(bufs)

    def body(*refs):
        start, finish = _xy_copies(refs[:n], refs[n:2 * n], refs[2 * n:], scatter)
        start()
        finish()

    return pl.pallas_call(
        body, in_specs=[pl.BlockSpec(memory_space=pl.ANY)] * n, out_specs=[pl.BlockSpec(memory_space=pl.ANY)] * n,
        out_shape=_xy_out_shapes(bufs, scatter), scratch_shapes=_xy_sems(n, scatter), name=name)(*bufs)


def _sibling_exchange(name, bufs):
    n = len(bufs)

    def body(*refs):
        in_refs, out_refs, send_sems, recv_sems = refs[:n], refs[n:2 * n], refs[2 * n], refs[2 * n + 1]
        x, y, c = lax.axis_index("x"), lax.axis_index("y"), lax.axis_index("c")
        copies = [pltpu.make_async_remote_copy(
            src_ref=in_refs[a], dst_ref=out_refs[a], send_sem=send_sems.at[a], recv_sem=recv_sems.at[a],
            device_id=(x, y, 1 - c), device_id_type=pl.DeviceIdType.MESH) for a in range(n)]
        for cp in copies:
            cp.start()
        for cp in copies:
            cp.wait()

    return pl.pallas_call(
        body, in_specs=[pl.BlockSpec(memory_space=pl.ANY)] * n, out_specs=[pl.BlockSpec(memory_space=pl.ANY)] * n,
        out_shape=[jax.ShapeDtypeStruct(b.shape, b.dtype) for b in bufs],
        scratch_shapes=[pltpu.SemaphoreType.DMA((n,)), pltpu.SemaphoreType.DMA((n,))], name=name)(*bufs)


def _sum_slots(name, buf):
    _, R, L = buf.shape
    tr, tc = _param_tile(R, L)

    def body(b_ref, o_ref):
        part = lambda s: b_ref[s].astype(f32)
        o_ref[...] = ((part(0) + part(1)) + part(2)) + part(3)

    return pl.pallas_call(
        body, grid=(R // tr, L // tc),
        in_specs=[pl.BlockSpec((N_POS, tr, tc), lambda i, j: (0, i, j))],
        out_specs=pl.BlockSpec((tr, tc), lambda i, j: (i, j)),
        out_shape=jax.ShapeDtypeStruct((R, L), f32), name=name,
        compiler_params=_params("parallel", "parallel"))(buf)


def _adamw(name, w, ga, gb, m, v):
    R, L = w.shape
    tr, tc = _param_tile(R, L)
    c1 = 1.0 / (1.0 - ADAM_B1 ** ADAM_STEP)
    c2 = 1.0 / (1.0 - ADAM_B2 ** ADAM_STEP)

    def body(w_ref, ga_ref, gb_ref, m_ref, v_ref, g_out, d_out, m_out, v_out):
        g = ga_ref[...] + gb_ref[...]
        m_new = ADAM_B1 * m_ref[...] + (1.0 - ADAM_B1) * g
        v_new = ADAM_B2 * v_ref[...] + (1.0 - ADAM_B2) * (g * g)
        g_out[...] = g
        m_out[...] = m_new
        v_out[...] = v_new
        d_out[...] = -ADAM_LR * ((m_new * c1) / (jnp.sqrt(v_new * c2) + ADAM_EPS) + ADAM_WD * w_ref[...])

    spec = pl.BlockSpec((tr, tc), lambda i, j: (i, j))
    return pl.pallas_call(
        body, grid=(R // tr, L // tc), in_specs=[spec] * 5, out_specs=[spec] * 4,
        out_shape=[jax.ShapeDtypeStruct((R, L), f32)] * 4, name=name,
        compiler_params=_params("parallel", "parallel"))(w, ga, gb, m, v)


def _step(x, loss_target, P, M, V):
    shapes = {n: tuple(P[n].shape) for n in WEIGHTS}
    sh_shapes = [shapes[n] for n in SMALL_SHARDED]
    packed = SMALL_SHARDED + SMALL
    late_names = BIG[1:]

    def whole(n, g):
        return g.reshape(-1, g.shape[2]) if n in ROW_SHARDED else jnp.concatenate([g[j] for j in range(N_POS)], axis=1)

    def slabs(G, n, dtype=f32):
        r, c = shapes[n]
        full = G[n].astype(dtype)
        if full.ndim == 3:
            return full
        return full.reshape(N_POS, r, c) if n in ROW_SHARDED else full.reshape(r, N_POS, c).transpose(1, 0, 2)

    g_w_in, g_small = _xy_exchange("gather_w_in", [P['w_in'].astype(bf16), _pack128([P[n] for n in SMALL_SHARDED])],
                                   scatter=False)
    W = {n: P[n] for n in SMALL}
    W['w_in_pad'] = _pad_w_in_shards([g_w_in[j] for j in range(N_POS)])
    per_pos = [_unpack128(g_small[j], sh_shapes) for j in range(N_POS)]
    for q, n in enumerate(SMALL_SHARDED):
        W[n] = jnp.concatenate([per_pos[j][q] for j in range(N_POS)], axis=1)
    groups = (('rwkv_proj', 'gdn_proj', 'ffn_up'), ('w_out', 'ffn_down'))
    late = dict(shards=[[P[n].astype(bf16) for n in grp] for grp in groups],
                assemble=lambda q, gathered: {n: whole(n, g) for n, g in zip(groups[q], gathered)},
                slabs=lambda G: [slabs(G, n, bf16) for n in late_names],
                w_in_slabs=lambda G: [jnp.stack([_unpad_cols(G['w_in_pad'], j * shapes['w_in'][1], (j + 1) * shapes['w_in'][1])
                                                 for j in range(N_POS)])])

    loss_rows, dx, G = _local_step(x, loss_target, W, late)
    arrived_late = G.pop('_arrived')
    (arrived_w_in,) = G.pop('_arrived_w_in')
    G.pop('w_in_pad')

    small_slabs = jnp.stack([_pack128([slabs(G, n)[j] for n in SMALL_SHARDED] + [G[n] for n in SMALL]) for j in range(N_POS)])
    (arrived_small,) = _xy_exchange("scatter_small", [small_slabs], scatter=True)
    contributions = [arrived_w_in] + list(arrived_late) + [arrived_small]
    tags = list(BIG) + ['small']
    plane = [_sum_slots("sum_" + t, cbuf) for t, cbuf in zip(tags, contributions)]
    sibling = _sibling_exchange("sibling_grads", plane)

    out = {}
    names4 = ('grad', 'delta', 'new_m', 'new_v')
    for q, n in enumerate(BIG):
        for tag, t in zip(names4, _adamw("adamw_" + n, P[n], plane[q], sibling[q], M[n], V[n])):
            out[tag + '_' + n] = t
    small_out = _adamw("adamw_small", _pack128([P[n] for n in packed]), plane[-1], sibling[-1],
                       _pack128([M[n] for n in packed]), _pack128([V[n] for n in packed]))
    for tag, buf in zip(names4, small_out):
        for n, t in zip(packed, _unpack128(buf, [shapes[n] for n in packed])):
            out[tag + '_' + n] = t
    loss = lax.psum(loss_rows[0, 0], ("x", "y", "c"))
    return loss, dx, out


def kernel(x, norm1_g, w_in, rwkv_mu, rwkv_w0, rwkv_w2, rwkv_a0, rwkv_a2, rwkv_g2, rwkv_k_k, rwkv_k_a, rwkv_r_k, rwkv_ln_w, rwkv_ln_b, rwkv_proj, gdn_conv_w, gdn_a_log, gdn_dt_bias, gdn_norm_w, gdn_proj, w_out, norm2_g, ffn_up, ffn_conv_w, ffn_down, final_g, loss_target, m_norm1_g, m_w_in, m_rwkv_mu, m_rwkv_w0, m_rwkv_w2, m_rwkv_a0, m_rwkv_a2, m_rwkv_g2, m_rwkv_k_k, m_rwkv_k_a, m_rwkv_r_k, m_rwkv_ln_w, m_rwkv_ln_b, m_rwkv_proj, m_gdn_conv_w, m_gdn_a_log, m_gdn_dt_bias, m_gdn_norm_w, m_gdn_proj, m_w_out, m_norm2_g, m_ffn_up, m_ffn_conv_w, m_ffn_down, m_final_g, v_norm1_g, v_w_in, v_rwkv_mu, v_rwkv_w0, v_rwkv_w2, v_rwkv_a0, v_rwkv_a2, v_rwkv_g2, v_rwkv_k_k, v_rwkv_k_a, v_rwkv_r_k, v_rwkv_ln_w, v_rwkv_ln_b, v_rwkv_proj, v_gdn_conv_w, v_gdn_a_log, v_gdn_dt_bias, v_gdn_norm_w, v_gdn_proj, v_w_out, v_norm2_g, v_ffn_up, v_ffn_conv_w, v_ffn_down, v_final_g):
    weights = (norm1_g, w_in, rwkv_mu, rwkv_w0, rwkv_w2, rwkv_a0, rwkv_a2, rwkv_g2, rwkv_k_k, rwkv_k_a, rwkv_r_k, rwkv_ln_w,
               rwkv_ln_b, rwkv_proj, gdn_conv_w, gdn_a_log, gdn_dt_bias, gdn_norm_w, gdn_proj, w_out, norm2_g, ffn_up,
               ffn_conv_w, ffn_down, final_g)
    m_in = (m_norm1_g, m_w_in, m_rwkv_mu, m_rwkv_w0, m_rwkv_w2, m_rwkv_a0, m_rwkv_a2, m_rwkv_g2, m_rwkv_k_k, m_rwkv_k_a,
            m_rwkv_r_k, m_rwkv_ln_w, m_rwkv_ln_b, m_rwkv_proj, m_gdn_conv_w, m_gdn_a_log, m_gdn_dt_bias, m_gdn_norm_w,
            m_gdn_proj, m_w_out, m_norm2_g, m_ffn_up, m_ffn_conv_w, m_ffn_down, m_final_g)
    v_in = (v_norm1_g, v_w_in, v_rwkv_mu, v_rwkv_w0, v_rwkv_w2, v_rwkv_a0, v_rwkv_a2, v_rwkv_g2, v_rwkv_k_k, v_rwkv_k_a,
            v_rwkv_r_k, v_rwkv_ln_w, v_rwkv_ln_b, v_rwkv_proj, v_gdn_conv_w, v_gdn_a_log, v_gdn_dt_bias, v_gdn_norm_w,
            v_gdn_proj, v_w_out, v_norm2_g, v_ffn_up, v_ffn_conv_w, v_ffn_down, v_final_g)
    drop = lambda n, a: a if n == 'final_g' else a[0]
    P = {n: drop(n, a) for n, a in zip(WEIGHTS, weights)}
    M = {n: drop(n, a) for n, a in zip(WEIGHTS, m_in)}
    V = {n: drop(n, a) for n, a in zip(WEIGHTS, v_in)}
    loss, dx, out = _step(x[0], loss_target[0], P, M, V)
    lift = lambda n, a: a if n == 'final_g' else a[None]
    res = [loss, dx[None]]
    for tag in ('grad', 'delta', 'new_m', 'new_v'):
        res += [lift(n, out[tag + '_' + n]) for n in WEIGHTS]
    return tuple(res)
```

```python
import functools

import jax
import jax.numpy as jnp
from jax import lax
from jax.experimental import pallas as pl
from jax.experimental.pallas import tpu as pltpu

f32 = jnp.float32
bf16 = jnp.bfloat16
HI = lax.Precision.HIGHEST

D_MODEL = 1024
RWKV_HEADS, RWKV_HD, RWKV_W = 8, 64, 512
GDN_HEADS, GDN_HD, GDN_W = 4, 128, 512
FFN_H = 2816
NORM_EPS, L2_EPS, GN_EPS = 1e-6, 1e-6, 64e-5
W_AB = 256
OFF_QKV, OFF_Z, OFF_GATES, OFF_AB = 1792, 3328, 3840, 5888
W_IN_PAD = OFF_AB + W_AB
WKV_CHUNK, WKV_PER_STEP = 64, 4
GDN_CHUNK, GDN_PER_STEP = 128, 4
HALO = 8
LANES = 128
TILE_BYTES = 1 << 20
VMEM_LIMIT = 56 * 1024 * 1024

ADAM_LR, ADAM_B1, ADAM_B2, ADAM_EPS, ADAM_WD, ADAM_STEP = 0.001, 0.9, 0.999, 1e-08, 0.01, 10

ROW_SHARDED = ('w_out', 'ffn_down')
SMALL = ('norm1_g', 'rwkv_mu', 'rwkv_w0', 'rwkv_a0', 'rwkv_k_k', 'rwkv_k_a', 'rwkv_r_k', 'rwkv_ln_w', 'rwkv_ln_b',
         'gdn_a_log', 'gdn_dt_bias', 'gdn_norm_w', 'norm2_g', 'final_g')
WEIGHTS = ('norm1_g', 'w_in', 'rwkv_mu', 'rwkv_w0', 'rwkv_w2', 'rwkv_a0', 'rwkv_a2', 'rwkv_g2', 'rwkv_k_k', 'rwkv_k_a',
           'rwkv_r_k', 'rwkv_ln_w', 'rwkv_ln_b', 'rwkv_proj', 'gdn_conv_w', 'gdn_a_log', 'gdn_dt_bias', 'gdn_norm_w',
           'gdn_proj', 'w_out', 'norm2_g', 'ffn_up', 'ffn_conv_w', 'ffn_down', 'final_g')


def _params(*sem):
    return pltpu.CompilerParams(dimension_semantics=sem, vmem_limit_bytes=VMEM_LIMIT)


def _tile(n, limit):
    if n <= limit:
        return n
    best = None
    for d in range(128, limit + 1, 128):
        if n % d == 0:
            best = d
    if best is None:
        raise ValueError(f"no tile for {n} under {limit}")
    return best


MM_BLOCK_BYTES = 6 << 20


def _mm(a, b, mode, name, add=None, out_dtype=f32, side=None, col_slabs=None):
    if mode == 'nn':
        (M, K), N = a.shape, b.shape[1]
    elif mode == 'nt':
        (M, K), N = a.shape, b.shape[0]
    else:
        (K, M), N = a.shape, b.shape[1]
    tm = _tile(M, 1408)
    tk = _tile(K, min(2816, MM_BLOCK_BYTES // (tm * a.dtype.itemsize)))
    tn = _tile(N, max(128, min(MM_BLOCK_BYTES // (tk * b.dtype.itemsize), MM_BLOCK_BYTES // (tm * 4)) // 128 * 128))
    if col_slabs is not None:
        tn = N // col_slabs
    nk = K // tk
    grid = (M // tm, N // tn, nk)
    dn = {'nn': (((1,), (0,)), ((), ())), 'nt': (((1,), (1,)), ((), ())), 'tn': (((0,), (0,)), ((), ()))}[mode]
    n_add = 0 if add is None else 1
    n_side = 0 if side is None else len(side[0])

    def body(a_ref, b_ref, *rest):
        add_ref = rest[0] if add is not None else None
        side_in, rest = rest[n_add:n_add + n_side], rest[n_add + n_side:]
        o_ref, side_out, rest = rest[0], rest[1:1 + n_side], rest[1 + n_side:]
        acc_ref, rest = (rest[0], rest[1:]) if nk > 1 else (None, rest)
        ids = [pl.program_id(d) for d in range(3)]
        if side is not None:
            start, finish = _xy_copies(side_in, side_out, rest, side[1])
            pl.when((ids[0] == 0) & (ids[1] == 0) & (ids[2] == 0))(start)
        acc = lax.dot_general(a_ref[...].astype(bf16), b_ref[...].astype(bf16), dn, preferred_element_type=f32)
        if nk == 1:
            o_ref[...] = (acc + add_ref[...] if add is not None else acc).astype(out_dtype)
        else:
            k = ids[2]

            @pl.when(k == 0)
            def _():
                acc_ref[...] = acc + add_ref[...] if add is not None else acc

            @pl.when(k > 0)
            def _():
                acc_ref[...] += acc

            @pl.when(k == nk - 1)
            def _():
                o_ref[...] = acc_ref[...].astype(out_dtype)
        if side is not None:
            pl.when((ids[0] == grid[0] - 1) & (ids[1] == grid[1] - 1) & (ids[2] == nk - 1))(finish)

    a_spec = (pl.BlockSpec((tk, tm), lambda i, j, k: (k, i)) if mode == 'tn'
              else pl.BlockSpec((tm, tk), lambda i, j, k: (i, k)))
    b_spec = (pl.BlockSpec((tn, tk), lambda i, j, k: (j, k)) if mode == 'nt'
              else pl.BlockSpec((tk, tn), lambda i, j, k: (k, j)))
    o_spec = pl.BlockSpec((tm, tn), lambda i, j, k: (i, j))
    o_shape = jax.ShapeDtypeStruct((M, N), out_dtype)
    if col_slabs is not None:
        o_spec = pl.BlockSpec((None, tm, tn), lambda i, j, k: (j, i, 0))
        o_shape = jax.ShapeDtypeStruct((col_slabs, M, tn), out_dtype)
    any_spec = pl.BlockSpec(memory_space=pl.ANY)
    side_bufs = [] if side is None else list(side[0])
    ins, specs = [a, b], [a_spec, b_spec]
    if add is not None:
        ins.append(add)
        specs.append(o_spec)
    outs = pl.pallas_call(
        body, grid=grid, in_specs=specs + [any_spec] * n_side, out_specs=[o_spec] + [any_spec] * n_side,
        out_shape=[o_shape] + (_xy_out_shapes(side_bufs, side[1]) if side is not None else []),
        scratch_shapes=([pltpu.VMEM((tm, tn), f32)] if nk > 1 else []) + (_xy_sems(n_side, side[1]) if side is not None else []),
        name=name,
        compiler_params=_params(*(("arbitrary",) * 3 if side is not None else ("parallel", "parallel", "arbitrary"))))(
            *ins, *side_bufs)
    return list(outs) if side is not None else outs[0]


def _shift_down(cur, prev, s):
    if s == 0:
        return cur
    ext = jnp.concatenate([prev, cur], axis=0)
    return pltpu.roll(ext, s, 0)[HALO:]


def _shift_up(cur, nxt, s):
    if s == 0:
        return cur
    ext = jnp.concatenate([cur, nxt], axis=0)
    return pltpu.roll(ext, ext.shape[0] - s, 0)[:cur.shape[0]]


def _conv_apply(cur, prev, w_ref, shifted=None):
    taps = w_ref.shape[0]
    out = None
    for i in range(taps):
        s = taps - 1 - i
        term = (shifted[s] if shifted is not None else _shift_down(cur, prev, s)) * w_ref[pl.ds(i, 1), :]
        out = term if out is None else out + term
    return out


def _row_spec(tm, w, col=0):
    return pl.BlockSpec((tm, w), lambda i: (i, col))


def _cols(a, width, col):
    return (a, width, col)


def _row_of(r):
    return r if isinstance(r, tuple) else (r, r.shape[1], 0)


def _prev_spec(tm, w):
    return pl.BlockSpec((HALO, w), lambda i: (jnp.maximum(i * (tm // HALO) - 1, 0), 0))


def _next_spec(tm, w, T):
    return pl.BlockSpec((HALO, w), lambda i: (jnp.minimum((i + 1) * (tm // HALO), T // HALO - 1), 0))


def _full_spec(shape):
    return pl.BlockSpec(shape, lambda i: (0,) * len(shape))


def _pw_fwd(name, fn, rows, consts, out_widths, tm, conv_w=None, out_dtype=f32, strip=None):
    T = _row_of(rows[0])[0].shape[0]
    nr, nc = len(rows), len(consts)

    def body(*refs):
        i = pl.program_id(0)
        if strip is not None:
            for j in range(out_widths[0] // strip):
                sl = slice(strip * j, strip * (j + 1))
                outs = fn(*[r[:, sl] for r in refs[:nr + nc]])
                for o_ref, o in zip(refs[nr + nc:], outs):
                    o_ref[:, sl] = o.astype(out_dtype)
            return
        vals = [r[...] for r in refs[:nr]]
        p = nr
        if conv_w is not None:
            prev = jnp.where(i > 0, refs[p][...], 0.0)
            vals[0] = _conv_apply(vals[0], prev, refs[p + 1])
            p += 2
        cvals = [r[...] for r in refs[p:p + nc]]
        outs = fn(*vals, *cvals)
        for o_ref, o in zip(refs[p + nc:], outs):
            o_ref[...] = o.astype(out_dtype)

    ins = [_row_of(r)[0] for r in rows]
    specs = [_row_spec(tm, *_row_of(r)[1:]) for r in rows]
    if conv_w is not None:
        ins += [rows[0], conv_w]
        specs += [_prev_spec(tm, rows[0].shape[1]), _full_spec(conv_w.shape)]
    ins += list(consts)
    specs += [_full_spec(c.shape) for c in consts]
    outs = pl.pallas_call(
        body, grid=(T // tm,), in_specs=specs,
        out_specs=[_row_spec(tm, w) for w in out_widths],
        out_shape=[jax.ShapeDtypeStruct((T, w), out_dtype) for w in out_widths], name=name,
        compiler_params=_params("parallel"))(*ins)
    return outs


def _pw_bwd(name, fn, rows, consts, cots, tm, add_to_first=None, row_dtypes=None, strip=None):
    rows = [_row_of(r) for r in rows]
    T = rows[0][0].shape[0]
    nr, nc = len(rows), len(consts)
    flat_cots = [c for grp in cots for c in grp]
    row_dtypes = row_dtypes or [f32] * nr
    n_extra = 0 if add_to_first is None else 1
    width = rows[0][1]

    def body(*refs):
        i = pl.program_id(0)
        in_refs, cot_refs = refs[:nr + nc], refs[nr + nc:nr + nc + len(flat_cots)]
        extra_ref = refs[nr + nc + len(flat_cots)] if add_to_first is not None else None
        row_out = refs[nr + nc + len(flat_cots) + n_extra:][:nr]
        const_out = refs[nr + nc + len(flat_cots) + n_extra + nr:]

        @pl.when(i == 0)
        def _():
            for q in range(nc):
                const_out[q][...] = jnp.zeros_like(const_out[q])

        def part(sl):
            cot_vals, p = [], 0
            for grp in cots:
                acc = cot_refs[p][:, sl]
                for q in range(1, len(grp)):
                    acc = acc + cot_refs[p + q][:, sl]
                p += len(grp)
                cot_vals.append(acc)
            _, vjp = jax.vjp(fn, *[r[:, sl] for r in in_refs])
            grads = vjp(tuple(cot_vals))
            for q in range(nr):
                g = grads[q]
                if q == 0 and extra_ref is not None:
                    g = g + extra_ref[:, sl]
                row_out[q][:, sl] = g.astype(row_dtypes[q])
            for q in range(nc):
                const_out[q][:, sl] += grads[nr + q]

        if strip is None:
            part(slice(None))
        else:
            for j in range(width // strip):
                part(slice(strip * j, strip * (j + 1)))

    ins = [r[0] for r in rows] + list(consts) + flat_cots
    specs = ([_row_spec(tm, r[1], r[2]) for r in rows] + [_full_spec(c.shape) for c in consts]
             + [_row_spec(tm, c.shape[1]) for c in flat_cots])
    if add_to_first is not None:
        ins.append(add_to_first)
        specs.append(_row_spec(tm, add_to_first.shape[1]))
    out_shapes = ([jax.ShapeDtypeStruct((T, r[1]), d) for r, d in zip(rows, row_dtypes)]
                  + [jax.ShapeDtypeStruct(c.shape, f32) for c in consts])
    out_specs = [_row_spec(tm, r[1]) for r in rows] + [_full_spec(c.shape) for c in consts]
    outs = pl.pallas_call(
        body, grid=(T // tm,), in_specs=specs, out_specs=out_specs, out_shape=out_shapes, name=name,
        compiler_params=_params("arbitrary"))(*ins)
    return list(outs[:nr]), list(outs[nr:])


def _pw_conv_bwd(name, fn, rows, consts, cots, conv_w, tm, row_dtypes=None):
    T, W0 = rows[0].shape
    nr, nc = len(rows), len(consts)
    taps = conv_w.shape[0]
    nblk = T // tm
    flat_cots = [c for grp in cots for c in grp]
    row_dtypes = row_dtypes or [f32] * nr

    def body(*refs):
        i = pl.program_id(0)
        p = 0
        cur = [r[...] for r in refs[p:p + nr]]; p += nr
        nxt = [r[...] for r in refs[p:p + nr]]; p += nr
        prev = jnp.where(i > 0, refs[p][...], 0.0); p += 1
        w_ref = refs[p]; p += 1
        cvals = [r[...] for r in refs[p:p + nc]]; p += nc

        def summed(p0):
            out, q = [], p0
            for grp in cots:
                acc = refs[q][...]
                for t in range(1, len(grp)):
                    acc = acc + refs[q + t][...]
                q += len(grp)
                out.append(acc)
            return out, q

        cot_cur, p = summed(p)
        cot_nxt, p = summed(p)
        row_out, dw_ref, const_out = refs[p:p + nr], refs[p + nr], refs[p + nr + 1:]

        x_cur = cur[0]
        x_down = [_shift_down(x_cur, prev, s_) for s_ in range(taps)]
        _, vjp = jax.vjp(fn, _conv_apply(x_cur, prev, w_ref, x_down), *cur[1:], *cvals)
        grads = vjp(tuple(cot_cur))
        _, vjp_n = jax.vjp(fn, _conv_apply(nxt[0], x_cur[tm - HALO:], w_ref), *nxt[1:], *cvals)
        dc_n = jnp.where(i < nblk - 1, vjp_n(tuple(cot_nxt))[0], 0.0)
        dc = grads[0]

        @pl.when(i == 0)
        def _():
            dw_ref[...] = jnp.zeros_like(dw_ref)
            for q in range(nc):
                const_out[q][...] = jnp.zeros_like(const_out[q])

        dx = None
        for k in range(taps):
            s_ = taps - 1 - k
            term = _shift_up(dc, dc_n, s_) * w_ref[pl.ds(k, 1), :]
            dx = term if dx is None else dx + term
            dw_ref[pl.ds(k, 1), :] += jnp.sum(dc * x_down[s_], axis=0, keepdims=True)
        row_out[0][...] = dx.astype(row_dtypes[0])
        for q in range(1, nr):
            row_out[q][...] = grads[q].astype(row_dtypes[q])
        for q in range(nc):
            const_out[q][...] += grads[nr + q]

    ins = list(rows) + list(rows) + [rows[0], conv_w] + list(consts) + flat_cots + flat_cots
    specs = ([_row_spec(tm, r.shape[1]) for r in rows] + [_next_spec(tm, r.shape[1], T) for r in rows]
             + [_prev_spec(tm, W0), _full_spec(conv_w.shape)] + [_full_spec(c.shape) for c in consts]
             + [_row_spec(tm, c.shape[1]) for c in flat_cots] + [_next_spec(tm, c.shape[1], T) for c in flat_cots])
    out_shapes = ([jax.ShapeDtypeStruct(r.shape, d) for r, d in zip(rows, row_dtypes)]
                  + [jax.ShapeDtypeStruct(conv_w.shape, f32)] + [jax.ShapeDtypeStruct(c.shape, f32) for c in consts])
    out_specs = ([_row_spec(tm, r.shape[1]) for r in rows] + [_full_spec(conv_w.shape)]
                 + [_full_spec(c.shape) for c in consts])
    outs = pl.pallas_call(
        body, grid=(nblk,), in_specs=specs, out_specs=out_specs, out_shape=out_shapes, name=name,
        compiler_params=_params("arbitrary"))(*ins)
    return list(outs[:nr]), outs[nr], list(outs[nr + 1:])


def _sigmoid(x):
    return 0.5 * jnp.tanh(0.5 * x) + 0.5


def _softplus(x):
    return jnp.maximum(x, 0.0) + jnp.log(1.0 + jnp.exp(jnp.minimum(x, -x)))


def _seg_sum_impl(x, seg):
    w = x.shape[-1]
    r = lax.broadcasted_iota(jnp.int32, (w, w), 0) // seg
    c = lax.broadcasted_iota(jnp.int32, (w, w), 1) // seg
    ones = (r == c).astype(bf16)
    hi = x.astype(bf16)
    lo = (x - hi.astype(f32)).astype(bf16)
    return (jnp.dot(hi, ones, preferred_element_type=f32) + jnp.dot(lo, ones, preferred_element_type=f32))


@functools.partial(jax.custom_vjp, nondiff_argnums=(1,))
def _seg_sum(x, seg):
    return _seg_sum_impl(x, seg)


_seg_sum.defvjp(lambda x, seg: (_seg_sum_impl(x, seg), None), lambda seg, _, g: (_seg_sum_impl(g, seg),))


def _rms(x, g):
    return x * lax.rsqrt(jnp.mean(x * x, axis=-1, keepdims=True) + NORM_EPS) * g


def _rms_fn(x, g):
    return (_rms(x, g),)


def _loss_rows(x2, tgt, g):
    e = _rms(x2, g) - tgt
    return 0.5 * jnp.sum(e * e, axis=-1, keepdims=True) * (1.0 / D_MODEL)


@jax.custom_vjp
def _dot_lo(a, b):
    return jnp.dot(a.astype(bf16), b.astype(bf16), preferred_element_type=f32)


def _dot_lo_bwd(ab, g):
    a, b = ab
    gl = g.astype(bf16)
    return (lax.dot_general(gl, b.astype(bf16), (((1,), (1,)), ((), ())), preferred_element_type=f32),
            lax.dot_general(a.astype(bf16), gl, (((0,), (0,)), ((), ())), preferred_element_type=f32))


_dot_lo.defvjp(lambda a, b: (_dot_lo(a, b), (a, b)), _dot_lo_bwd)


def _rwkv_prep_fn(ps, w0, w2p, a0, a2p, g2, k_k, k_a):
    r, k, v = ps[:, 0:512], ps[:, 512:1024], ps[:, 1024:1536]
    wa, gl = ps[:, 1536:1664], ps[:, 1664:1792]
    z = w0 + _dot_lo(jnp.tanh(wa), w2p)
    w_log = -_softplus(-z) - 0.5
    lw = -jnp.exp(w_log)
    a = _sigmoid(a0 + _dot_lo(wa, a2p))
    g = _dot_lo(_sigmoid(gl), g2)
    kx = k * k_k
    kk = kx * lax.rsqrt(_seg_sum(kx * kx, RWKV_HD) + L2_EPS)
    k2 = k * (1.0 + (a - 1.0) * k_a)
    return r, lw, k2, v, -kk, kk * a, g


def _rwkv_post_fn(y, r, k2, v, g, ln_w, ln_b, rk):
    mean = _seg_sum(y, RWKV_HD) * (1.0 / RWKV_HD)
    yc = y - mean
    var = _seg_sum(yc * yc, RWKV_HD) * (1.0 / RWKV_HD)
    yn = yc * lax.rsqrt(var + GN_EPS) * ln_w + ln_b
    bonus = _seg_sum(r * k2 * rk, RWKV_HD) * v
    return ((yn + bonus) * g,)


def _gdn_prep_fn(cq, ck, cv):
    silu = lambda c: c * _sigmoid(c)
    q, k = silu(cq), silu(ck)
    q = q * lax.rsqrt(jnp.sum(q * q, axis=-1, keepdims=True) + L2_EPS) * (GDN_HD ** -0.5)
    k = k * lax.rsqrt(jnp.sum(k * k, axis=-1, keepdims=True) + L2_EPS)
    return q, k, silu(cv)


def _gdn_gate_fn(ab, al_p, dt_p):
    lane = lax.broadcasted_iota(jnp.int32, ab.shape, 1)
    gpart = -jnp.exp(al_p) * _softplus(ab + dt_p)
    return (jnp.where(lane < GDN_HEADS, gpart, jnp.where(lane < 2 * GDN_HEADS, _sigmoid(ab), 0.0)),)


def _gdn_post_fn(o, z, nw):
    ms = _seg_sum(o * o, GDN_HD) * (1.0 / GDN_HD)
    return (o * lax.rsqrt(ms + NORM_EPS) * nw * (z * _sigmoid(z)),)


def _mix_fn(ga, gb, ya, yb):
    return (_sigmoid(ga) * ya + _sigmoid(gb) * yb,)


STRIP = 128


def _strip_conv(ref, prev_ref, w_ref, sl, first, taps):
    cur = ref[:, sl]
    prev = jnp.where(first, 0.0, prev_ref[:, sl])
    down = [_shift_down(cur, prev, s) for s in range(taps)]
    conv = None
    for k in range(taps):
        term = down[taps - 1 - k] * w_ref[pl.ds(k, 1), sl]
        conv = term if conv is None else conv + term
    return cur, down, conv


def _group_fwd(name, fn, x, w, shared_cols, group_cols, consts, n_out, tm):
    T, W = x.shape
    taps = w.shape[0]
    n_groups = len(group_cols)
    nc = len(consts)

    def body(x_ref, xp_ref, w_ref, *refs):
        const_refs, out_refs = refs[:nc], refs[nc:]
        first = pl.program_id(0) == 0
        shared = [_strip_conv(x_ref, xp_ref, w_ref, sl, first, taps)[2] for sl in shared_cols]
        for j, cols in enumerate(group_cols):
            sl = slice(STRIP * j, STRIP * (j + 1))
            convs = [_strip_conv(x_ref, xp_ref, w_ref, c, first, taps)[2] for c in cols]
            outs = fn(*convs, *shared, *[c[:, sl] for c in const_refs])
            for o_ref, o in zip(out_refs, outs):
                o_ref[:, sl] = o

    return pl.pallas_call(
        body, grid=(T // tm,),
        in_specs=[_row_spec(tm, W), _prev_spec(tm, W), _full_spec(w.shape)] + [_full_spec(c.shape) for c in consts],
        out_specs=[_row_spec(tm, STRIP * n_groups)] * n_out,
        out_shape=[jax.ShapeDtypeStruct((T, STRIP * n_groups), f32)] * n_out, name=name,
        compiler_params=_params("parallel"))(x, x, w, *consts)


def _group_bwd(name, fn, x, w, shared_cols, group_cols, consts, cots, tm):
    T, W = x.shape
    taps = w.shape[0]
    nblk = T // tm
    nc, ns = len(consts), len(shared_cols)
    flat_cots = [c for grp in cots for c in grp]
    n_cot = len(flat_cots)

    def body(x_ref, xp_ref, xn_ref, w_ref, *refs):
        const_refs, refs = refs[:nc], refs[nc:]
        cot_refs, cotn_refs, refs = refs[:n_cot], refs[n_cot:2 * n_cot], refs[2 * n_cot:]
        dx_ref, dw_ref, const_out = refs[0], refs[1], refs[2:]
        i = pl.program_id(0)
        first, last = i == 0, i == nblk - 1

        @pl.when(first)
        def _():
            dw_ref[...] = jnp.zeros_like(dw_ref)
            for q in range(nc):
                const_out[q][...] = jnp.zeros_like(const_out[q])

        def convs_of(sl):
            cur, down, conv = _strip_conv(x_ref, xp_ref, w_ref, sl, first, taps)
            nxt, conv_n = xn_ref[:, sl], None
            for k in range(taps):
                term = _shift_down(nxt, cur[tm - HALO:], taps - 1 - k) * w_ref[pl.ds(k, 1), sl]
                conv_n = term if conv_n is None else conv_n + term
            return down, conv, conv_n

        def conv_back(sl, down, dc, dc_n):
            dx = None
            for k in range(taps):
                s_ = taps - 1 - k
                term = _shift_up(dc, dc_n, s_) * w_ref[pl.ds(k, 1), sl]
                dx = term if dx is None else dx + term
                dw_ref[pl.ds(k, 1), sl] += jnp.sum(dc * down[s_], axis=0, keepdims=True)
            dx_ref[:, sl] = dx.astype(dx_ref.dtype)

        def summed(refs_, sl, mask):
            out, p = [], 0
            for grp in cots:
                acc = refs_[p][:, sl]
                for t in range(1, len(grp)):
                    acc = acc + refs_[p + t][:, sl]
                p += len(grp)
                out.append(jnp.where(last, 0.0, acc) if mask else acc)
            return tuple(out)

        shared = [convs_of(sl) for sl in shared_cols]
        d_shared, d_shared_n = [None] * ns, [None] * ns
        for j, cols in enumerate(group_cols):
            sl = slice(STRIP * j, STRIP * (j + 1))
            mine = [convs_of(c) for c in cols]
            cj = [c[:, sl] for c in const_refs]
            _, vjp = jax.vjp(fn, *[m[1] for m in mine], *[m[1] for m in shared], *cj)
            grads = vjp(summed(cot_refs, sl, False))
            _, vjp_n = jax.vjp(fn, *[m[2] for m in mine], *[m[2] for m in shared], *cj)
            grads_n = vjp_n(summed(cotn_refs, sl, True))
            for q, c in enumerate(cols):
                conv_back(c, mine[q][0], grads[q], grads_n[q])
            for q in range(ns):
                g, gn = grads[len(cols) + q], grads_n[len(cols) + q]
                d_shared[q] = g if d_shared[q] is None else d_shared[q] + g
                d_shared_n[q] = gn if d_shared_n[q] is None else d_shared_n[q] + gn
            for q in range(nc):
                const_out[q][:, sl] += grads[len(cols) + ns + q]
        for q, c in enumerate(shared_cols):
            conv_back(c, shared[q][0], d_shared[q], d_shared_n[q])

    outs = pl.pallas_call(
        body, grid=(nblk,),
        in_specs=[_row_spec(tm, W), _prev_spec(tm, W), _next_spec(tm, W, T), _full_spec(w.shape)]
        + [_full_spec(c.shape) for c in consts] + [_row_spec(tm, c.shape[1]) for c in flat_cots]
        + [_next_spec(tm, c.shape[1], T) for c in flat_cots],
        out_specs=[_row_spec(tm, W), _full_spec(w.shape)] + [_full_spec(c.shape) for c in consts],
        out_shape=[jax.ShapeDtypeStruct((T, W), bf16), jax.ShapeDtypeStruct(w.shape, f32)]
        + [jax.ShapeDtypeStruct(c.shape, f32) for c in consts], name=name,
        compiler_params=_params("arbitrary"))(x, x, x, w, *consts, *flat_cots, *flat_cots)
    return outs[0], outs[1], list(outs[2:])


def _ffn_strip_fn(cg, cu):
    return cg * _sigmoid(cg) * cu


def _ffn_act_fwd(h, w, tm):
    T, W2 = h.shape
    H = W2 // 2
    taps = w.shape[0]

    def body(h_ref, hp_ref, w_ref, o_ref):
        first = pl.program_id(0) == 0
        for j in range(H // STRIP):
            gs, us = slice(STRIP * j, STRIP * (j + 1)), slice(H + STRIP * j, H + STRIP * (j + 1))
            cg = _strip_conv(h_ref, hp_ref, w_ref, gs, first, taps)[2]
            cu = _strip_conv(h_ref, hp_ref, w_ref, us, first, taps)[2]
            o_ref[:, gs] = _ffn_strip_fn(cg, cu).astype(o_ref.dtype)

    return pl.pallas_call(
        body, grid=(T // tm,), in_specs=[_row_spec(tm, W2), _prev_spec(tm, W2), _full_spec(w.shape)],
        out_specs=_row_spec(tm, H), out_shape=jax.ShapeDtypeStruct((T, H), bf16), name="ffn_act",
        compiler_params=_params("parallel"))(h, h, w)


def _ffn_act_bwd(h, dact, w, tm):
    T, W2 = h.shape
    H = W2 // 2
    taps = w.shape[0]
    nblk = T // tm

    def body(h_ref, hp_ref, hn_ref, d_ref, dn_ref, w_ref, dh_ref, dw_ref):
        i = pl.program_id(0)
        first, last = i == 0, i == nblk - 1

        @pl.when(first)
        def _():
            dw_ref[...] = jnp.zeros_like(dw_ref)

        for j in range(H // STRIP):
            gs, us = slice(STRIP * j, STRIP * (j + 1)), slice(H + STRIP * j, H + STRIP * (j + 1))
            parts = {}
            for name, sl in (('g', gs), ('u', us)):
                cur, down, conv = _strip_conv(h_ref, hp_ref, w_ref, sl, first, taps)
                nxt = hn_ref[:, sl]
                conv_n = None
                for k in range(taps):
                    term = _shift_down(nxt, cur[tm - HALO:], taps - 1 - k) * w_ref[pl.ds(k, 1), sl]
                    conv_n = term if conv_n is None else conv_n + term
                parts[name] = (down, conv, conv_n)
            _, vjp = jax.vjp(_ffn_strip_fn, parts['g'][1], parts['u'][1])
            dcs = vjp(d_ref[:, gs])
            _, vjp_n = jax.vjp(_ffn_strip_fn, parts['g'][2], parts['u'][2])
            dcs_n = vjp_n(jnp.where(last, 0.0, dn_ref[:, gs]))
            for (name, sl), dc, dc_n in zip((('g', gs), ('u', us)), dcs, dcs_n):
                down = parts[name][0]
                dx = None
                for k in range(taps):
                    s_ = taps - 1 - k
                    term = _shift_up(dc, dc_n, s_) * w_ref[pl.ds(k, 1), sl]
                    dx = term if dx is None else dx + term
                    dw_ref[pl.ds(k, 1), sl] += jnp.sum(dc * down[s_], axis=0, keepdims=True)
                dh_ref[:, sl] = dx.astype(dh_ref.dtype)

    return pl.pallas_call(
        body, grid=(nblk,),
        in_specs=[_row_spec(tm, W2), _prev_spec(tm, W2), _next_spec(tm, W2, T), _row_spec(tm, H), _next_spec(tm, H, T),
                  _full_spec(w.shape)],
        out_specs=[_row_spec(tm, W2), _full_spec(w.shape)],
        out_shape=[jax.ShapeDtypeStruct((T, W2), bf16), jax.ShapeDtypeStruct(w.shape, f32)], name="ffn_act_bwd",
        compiler_params=_params("arbitrary"))(h, h, h, dact, dact, w)


N_POS = 4


def _xy_out_shapes(bufs, scatter):
    return [jax.ShapeDtypeStruct((N_POS,) + tuple(b.shape[1:] if scatter else b.shape), b.dtype) for b in bufs]


def _xy_sems(n, scatter):
    sems = [pltpu.SemaphoreType.DMA((3 * n,)), pltpu.SemaphoreType.DMA((3 * n,)), pltpu.SemaphoreType.DMA((n,))]
    return sems if scatter else sems + [pltpu.SemaphoreType.DMA((3 * n,)), pltpu.SemaphoreType.DMA((3 * n,))]


def _xy_copies(in_refs, out_refs, sems, scatter):
    n = len(in_refs)
    send_sems, recv_sems, local_sems = sems[:3]

    def place():
        x, y, c = lax.axis_index("x"), lax.axis_index("y"), lax.axis_index("c")
        return x, y, c, 2 * x + y, [(1 - x, y), (x, 1 - y), (1 - x, 1 - y)]

    def half(ref, a, which):
        rows = in_refs[a].shape[0] // 2
        return ref.at[pl.ds(pl.multiple_of(which * rows, HALO), rows)]

    def ici(a, k, src, dst, peer, c):
        return pltpu.make_async_remote_copy(
            src_ref=src, dst_ref=dst, send_sem=send_sems.at[3 * a + k], recv_sem=recv_sems.at[3 * a + k],
            device_id=(peer[0], peer[1], c), device_id_type=pl.DeviceIdType.MESH)

    def outgoing():
        x, y, c, me, peers = place()
        own = [pltpu.make_async_copy(in_refs[a].at[me] if scatter else in_refs[a], out_refs[a].at[me], local_sems.at[a])
               for a in range(n)]
        if scatter:
            sends = [ici(a, k, in_refs[a].at[2 * p[0] + p[1]], out_refs[a].at[me], p, c)
                     for a in range(n) for k, p in enumerate(peers)]
        else:
            sends = [ici(a, k, half(in_refs[a], a, c), half(out_refs[a].at[me], a, c), p, c)
                     for a in range(n) for k, p in enumerate(peers)]
        return own, sends

    def arrivals():
        x, y, c, me, peers = place()
        if scatter:
            return [ici(a, k, in_refs[a].at[me], out_refs[a].at[2 * p[0] + p[1]], p, c)
                    for a in range(n) for k, p in enumerate(peers)]
        return [ici(a, k, half(in_refs[a], a, c), half(out_refs[a].at[2 * p[0] + p[1]], a, c), p, c)
                for a in range(n) for k, p in enumerate(peers)]

    def to_sibling(mine):
        x, y, c, me, peers = place()
        which = c if mine else 1 - c
        return [pltpu.make_async_remote_copy(
            src_ref=half(out_refs[a].at[2 * p[0] + p[1]], a, which), dst_ref=half(out_refs[a].at[2 * p[0] + p[1]], a, which),
            send_sem=sems[3].at[3 * a + k], recv_sem=sems[4].at[3 * a + k],
            device_id=(x, y, 1 - c), device_id_type=pl.DeviceIdType.MESH) for a in range(n) for k, p in enumerate(peers)]

    def start():
        own, sends = outgoing()
        for cp in own + sends:
            cp.start()

    def finish():
        if scatter:
            for cp in arrivals():
                cp.wait_recv()
        else:
            passed = to_sibling(True)
            for cp, fwd in zip(arrivals(), passed):
                cp.wait_recv()
                fwd.start()
            for cp in to_sibling(False):
                cp.wait_recv()
            for fwd in passed:
                fwd.wait_send()
        own, sends = outgoing()
        for cp in sends:
            cp.wait_send()
        for cp in own:
            cp.wait()

    return start, finish


_NN, _NT, _TN = 'hcs,hsd->hcd', 'hcd,hsd->hcs', 'hcd,hce->hde'


def _lo(spec, a, b):
    return jnp.einsum(spec, a.astype(bf16), b.astype(bf16), preferred_element_type=f32)


@jax.custom_vjp
def _bmm(a, b):
    return _lo(_NN, a, b)


_bmm.defvjp(lambda a, b: (_lo(_NN, a, b), (a, b)), lambda ab, g: (_lo(_NT, g, ab[1]), _lo(_TN, ab[0], g)))


@jax.custom_vjp
def _bmm_nt(a, b):
    return _lo(_NT, a, b)


_bmm_nt.defvjp(lambda a, b: (_lo(_NT, a, b), (a, b)), lambda ab, g: (_lo(_NN, g, ab[1]), _lo(_TN, g, ab[0])))


@jax.custom_vjp
def _bmm_tn(a, b):
    return _lo(_TN, a, b)


_bmm_tn.defvjp(lambda a, b: (_lo(_TN, a, b), (a, b)), lambda ab, g: (_lo(_NT, ab[1], g), _lo(_NN, ab[0], g)))


def _masks(H, C):
    row = lax.broadcasted_iota(jnp.int32, (H, C, C), 1)
    col = lax.broadcasted_iota(jnp.int32, (H, C, C), 2)
    return row, col


def _tri_inv_impl(L):
    H, C, _ = L.shape
    row, col = _masks(H, C)
    eye = (row == col).astype(f32)
    base = 16
    same = (row // base) == (col // base)
    Ld = jnp.where(same, L, 0.0)
    X = -Ld
    inv = eye + X
    for _ in range(3):
        X = _bmm(X, X)
        inv = _bmm(inv, eye + X)
    if C == base:
        return inv
    N = _bmm(inv, L - Ld)
    out = eye - N
    levels = C // base
    P = N
    span = 2
    while span < levels:
        P = _bmm(P, P)
        out = _bmm(out, eye + P)
        span *= 2
    return _bmm(out, inv)


@jax.custom_vjp
def _tri_inv(L):
    return _tri_inv_impl(L)


def _tri_inv_fwd(L):
    T = _tri_inv_impl(L)
    return T, T


def _tri_inv_bwd(T, dT):
    return (-_bmm_nt(_bmm_tn(T, dT), T),)


_tri_inv.defvjp(_tri_inv_fwd, _tri_inv_bwd)


def _cumsum_impl(x, reverse):
    C = x.shape[1]
    row = lax.broadcasted_iota(jnp.int32, x.shape, 1)
    s = 1
    while s < C:
        if reverse:
            x = x + jnp.where(row < C - s, pltpu.roll(x, C - s, 1), 0.0)
        else:
            x = x + jnp.where(row >= s, pltpu.roll(x, s, 1), 0.0)
        s *= 2
    return x


@jax.custom_vjp
def _cumsum(x):
    return _cumsum_impl(x, False)


_cumsum.defvjp(lambda x: (_cumsum_impl(x, False), None), lambda _, g: (_cumsum_impl(g, True),))


def _wkv_prep(r, lw, k, v, a, b):
    lane = lax.broadcasted_iota(jnp.int32, (r.shape[0], 128), 1)
    low = lane < RWKV_HD

    def heads(t):
        out = []
        for p in range(RWKV_HEADS // 2):
            pair = t[:, 128 * p:128 * (p + 1)]
            out += [jnp.where(low, pair, 0.0), jnp.where(low, 0.0, pair)]
        return jnp.concatenate([t[None] for t in out], axis=0)

    r, lw, k, v, a, b = [heads(t) for t in (r, lw, k, v, a, b)]
    H, C, D = r.shape
    row, col = _masks(H, C)
    incl, strict = row >= col, row > col
    cw = _cumsum(lw)
    cwp = cw - lw
    cwl = jnp.sum(lw, axis=1, keepdims=True)
    en = jnp.exp(-cw)
    at, rt, bt, kt = a * jnp.exp(cwp), r * jnp.exp(cw), b * en, k * en
    Tm = _tri_inv(-jnp.where(strict, _bmm_nt(at, bt), 0.0))
    ar = jnp.concatenate([at, rt], axis=1)
    gram = _bmm_nt(ar, jnp.concatenate([bt, kt], axis=1))
    row2 = lax.broadcasted_iota(jnp.int32, (H, 2 * C, 2 * C), 1)
    col2 = lax.broadcasted_iota(jnp.int32, (H, 2 * C, 2 * C), 2) % C
    gram = jnp.where(((row2 < C) & (row2 > col2)) | ((row2 >= C) & (row2 - C >= col2)), gram, 0.0)
    a_bk, r_bk = gram[:, :C], gram[:, C:]
    lak_v = _bmm(a_bk, jnp.concatenate([jnp.zeros_like(v), v], axis=1))
    ed = jnp.exp(cwl - cw)
    zdec = jnp.swapaxes(jnp.broadcast_to(jnp.exp(cwl), (H, D, D)), 1, 2)
    return ar, Tm, lak_v, r_bk, jnp.concatenate([b * ed, k * ed], axis=1), zdec, v


def _wkv_step(Z, ar, Tm, lak_v, r_bk, bk_d, zdec, v):
    C = Tm.shape[1]
    ar_z = _bmm(ar, Z)
    uv = jnp.concatenate([_bmm(Tm, ar_z[:, :C] + lak_v), v], axis=1)
    y = ar_z[:, C:] + _bmm(r_bk, uv)
    Z1 = Z * zdec + _bmm_tn(bk_d, uv)
    return jnp.concatenate([y[2 * p] + y[2 * p + 1] for p in range(RWKV_HEADS // 2)], axis=1), Z1


def _split3(x):
    hi = x.astype(bf16)
    mid = (x - hi.astype(f32)).astype(bf16)
    lo = (x - hi.astype(f32) - mid.astype(f32)).astype(bf16)
    return hi, mid, lo


@jax.custom_vjp
def _spread(x, sel):
    return sum(jnp.dot(t, sel, preferred_element_type=f32) for t in _split3(x))


def _spread_bwd(sel, g):
    dn = (((1,), (1,)), ((), ()))
    return sum(lax.dot_general(t, sel, dn, preferred_element_type=f32) for t in _split3(g)), None


_spread.defvjp(lambda x, sel: (_spread(x, sel), sel), _spread_bwd)


def _gdn_prep(q, k, v, gbeta):
    heads = lambda t: jnp.concatenate([t[None, :, GDN_HD * h:GDN_HD * (h + 1)] for h in range(GDN_HEADS)], axis=0)
    src = lax.broadcasted_iota(jnp.int32, (W_AB, 2 * GDN_W), 0)
    dst = lax.broadcasted_iota(jnp.int32, (W_AB, 2 * GDN_W), 1) // GDN_HD
    spread = _spread(gbeta, (src == dst).astype(bf16))
    q, k, v, g, beta = heads(q), heads(k), heads(v), heads(spread[:, :GDN_W]), heads(spread[:, GDN_W:])
    H, C, D = q.shape
    row, col = _masks(H, C)
    incl, strict = row >= col, row > col
    gc = _cumsum(g)
    diff = gc - jnp.swapaxes(gc, 1, 2)
    decay = jnp.where(incl, jnp.exp(jnp.where(incl, diff, 0.0)), 0.0)
    gl = jnp.sum(g, axis=1, keepdims=True)
    kb, vb = k * beta, v * beta
    gram = _bmm_nt(jnp.concatenate([kb, q], axis=1), k)
    L = jnp.where(strict, gram[:, :C] * decay, 0.0)
    attn = jnp.where(incl, gram[:, C:] * decay, 0.0)
    egc = jnp.exp(gc)
    t_vk = _bmm(_tri_inv(L), jnp.concatenate([vb, kb * egc], axis=2))
    return t_vk[:, :, :D], jnp.concatenate([t_vk[:, :, D:], q * egc], axis=1), attn, k * jnp.exp(gl - gc), jnp.exp(gl)


def _gdn_step(S, u, wq, attn, ke, sdec):
    C = u.shape[1]
    wq_s = _bmm(wq, S)
    v_new = u - wq_s[:, :C]
    o = wq_s[:, C:] + _bmm(attn, v_new)
    S1 = S * sdec + _bmm_tn(ke, v_new)
    return jnp.concatenate([o[h] for h in range(GDN_HEADS)], axis=1), S1


def _scan_fwd(name, fns, ins, C, H, dh, w_out, per_step, side=None):
    prep, step = fns
    T = ins[0].shape[0]
    n_in = len(ins)
    blk = C * per_step
    nblk = T // blk
    n_side = 0 if side is None else len(side[0])

    def body(*refs):
        in_refs, refs = refs[:n_in], refs[n_in:]
        side_in, refs = refs[:n_side], refs[n_side:]
        y_ref, zs_ref, refs = refs[0], refs[1], refs[2:]
        side_out, refs = refs[:n_side], refs[n_side:]
        z_scr = refs[0]
        if side is not None:
            start, finish = _xy_copies(side_in, side_out, refs[1:], side[1])
            pl.when(pl.program_id(0) == 0)(start)

        @pl.when(pl.program_id(0) == 0)
        def _():
            z_scr[...] = jnp.zeros_like(z_scr)

        rows = [slice(C * j, C * (j + 1)) for j in range(per_step)]
        prepped = [prep(*[r[rw, :] for r in in_refs]) for rw in rows]
        Z = z_scr[...]
        for j, rw in enumerate(rows):
            zs_ref[j] = Z
            y, Z = step(Z, *prepped[j])
            y_ref[rw, :] = y
        z_scr[...] = Z
        if side is not None:
            pl.when(pl.program_id(0) == nblk - 1)(finish)

    side_bufs = [] if side is None else list(side[0])
    any_spec = pl.BlockSpec(memory_space=pl.ANY)
    return pl.pallas_call(
        body, grid=(nblk,),
        in_specs=[pl.BlockSpec((blk, a.shape[1]), lambda i: (i, 0)) for a in ins] + [any_spec] * n_side,
        out_specs=[pl.BlockSpec((blk, w_out), lambda i: (i, 0)), pl.BlockSpec((per_step, H, dh, dh), lambda i: (i, 0, 0, 0))]
        + [any_spec] * n_side,
        out_shape=[jax.ShapeDtypeStruct((T, w_out), f32), jax.ShapeDtypeStruct((T // C, H, dh, dh), f32)]
        + (_xy_out_shapes(side_bufs, side[1]) if side is not None else []),
        scratch_shapes=[pltpu.VMEM((H, dh, dh), f32)] + (_xy_sems(n_side, side[1]) if side is not None else []), name=name,
        compiler_params=_params("arbitrary"))(*ins, *side_bufs)


def _scan_bwd(name, fns, ins, dy, zs, C, per_step, side=None):
    prep, step = fns
    T = ins[0].shape[0]
    _, H, dh, _ = zs.shape
    n_in = len(ins)
    blk = C * per_step
    nblk = T // blk
    n_side = 0 if side is None else len(side[0])

    def body(*refs):
        in_refs, dy_ref, zs_ref, refs = refs[:n_in], refs[n_in], refs[n_in + 1], refs[n_in + 2:]
        side_in, refs = refs[:n_side], refs[n_side:]
        out_refs, refs = refs[:n_in], refs[n_in:]
        side_out, refs = refs[:n_side], refs[n_side:]
        dz_scr = refs[0]
        if side is not None:
            start, finish = _xy_copies(side_in, side_out, refs[1:], side[1])
            pl.when(pl.program_id(0) == 0)(start)

        @pl.when(pl.program_id(0) == 0)
        def _():
            dz_scr[...] = jnp.zeros_like(dz_scr)

        rows = [slice(C * j, C * (j + 1)) for j in range(per_step)]
        prepped = [jax.vjp(prep, *[r[rw, :] for r in in_refs]) for rw in rows]
        d_prepped = [None] * per_step
        dZ = dz_scr[...]
        for j in reversed(range(per_step)):
            _, pull = jax.vjp(step, zs_ref[j], *prepped[j][0])
            dZ, *d_prepped[j] = pull((dy_ref[rows[j], :], dZ))
        dz_scr[...] = dZ
        for j, rw in enumerate(rows):
            for o_ref, gval in zip(out_refs, prepped[j][1](tuple(d_prepped[j]))):
                o_ref[rw, :] = gval
        if side is not None:
            pl.when(pl.program_id(0) == nblk - 1)(finish)

    side_bufs = [] if side is None else list(side[0])
    any_spec = pl.BlockSpec(memory_space=pl.ANY)
    rev = lambda i: (nblk - 1 - i, 0)
    return pl.pallas_call(
        body, grid=(nblk,),
        in_specs=[pl.BlockSpec((blk, a.shape[1]), rev) for a in ins]
        + [pl.BlockSpec((blk, dy.shape[1]), rev), pl.BlockSpec((per_step, H, dh, dh), lambda i: (nblk - 1 - i, 0, 0, 0))]
        + [any_spec] * n_side,
        out_specs=[pl.BlockSpec((blk, a.shape[1]), rev) for a in ins] + [any_spec] * n_side,
        out_shape=[jax.ShapeDtypeStruct(a.shape, f32) for a in ins]
        + (_xy_out_shapes(side_bufs, side[1]) if side is not None else []),
        scratch_shapes=[pltpu.VMEM((H, dh, dh), f32)] + (_xy_sems(n_side, side[1]) if side is not None else []), name=name,
        compiler_params=_params("arbitrary"))(*ins, dy, zs, *side_bufs)


def _loss_call(x2, tgt, g, tm):
    T, W = x2.shape

    def body(x_ref, t_ref, g_ref, dx_ref, dg_ref, l_ref):
        i = pl.program_id(0)
        tv = t_ref[...]
        l, vjp = jax.vjp(lambda xv, gv: _loss_rows(xv, tv, gv), x_ref[...], g_ref[...])
        dx, dg = vjp(jnp.ones_like(l))
        dx_ref[...] = dx
        tot = jnp.zeros((1, 128), f32) + jnp.sum(l)

        @pl.when(i == 0)
        def _():
            dg_ref[...] = dg
            l_ref[...] = tot

        @pl.when(i > 0)
        def _():
            dg_ref[...] += dg
            l_ref[...] += tot

    return pl.pallas_call(
        body, grid=(T // tm,),
        in_specs=[_row_spec(tm, W), _row_spec(tm, W), _full_spec(g.shape)],
        out_specs=[_row_spec(tm, W), _full_spec(g.shape), _full_spec((1, 128))],
        out_shape=[jax.ShapeDtypeStruct((T, W), f32), jax.ShapeDtypeStruct(g.shape, f32),
                   jax.ShapeDtypeStruct((1, 128), f32)], name="loss_head",
        compiler_params=_params("arbitrary"))(x2, tgt, g)


def _local_step(x, tgt, W, late=None):
    row = lambda a: a.reshape(1, -1)
    wp = W['w_in_pad']
    w_rwkv, w_qkv, w_z = wp[:, :OFF_QKV], wp[:, OFF_QKV:OFF_Z], wp[:, OFF_Z:OFF_GATES]
    w_gates, w_ab = wp[:, OFF_GATES:OFF_AB], wp[:, OFF_AB:]
    mu = row(W['rwkv_mu'])
    mixw = jnp.concatenate([mu, 1.0 - mu], axis=0)
    zpad = jnp.zeros((64, RWKV_W), f32)
    w2p = jnp.concatenate([W['rwkv_w2'], zpad], axis=0)
    a2p = jnp.concatenate([zpad, W['rwkv_a2']], axis=0)
    rw_consts = [row(W['rwkv_w0']), w2p, row(W['rwkv_a0']), a2p, W['rwkv_g2'], row(W['rwkv_k_k']), row(W['rwkv_k_a'])]
    post_consts = [row(W['rwkv_ln_w']), row(W['rwkv_ln_b']), row(W['rwkv_r_k'])]
    pad4 = lambda a: jnp.pad(row(a), ((0, 0), (0, W_AB - GDN_HEADS)))
    gd_consts = [pad4(W['gdn_a_log']), pad4(W['gdn_dt_bias'])]
    nw_t = jnp.tile(row(W['gdn_norm_w']), (1, GDN_HEADS))
    g1, g2n, gf = row(W['norm1_g']), row(W['norm2_g']), row(W['final_g'])

    (u,) = _pw_fwd("norm1", _rms_fn, [x], [g1], [D_MODEL], 512, out_dtype=bf16)
    p_rwkv = _mm(u, w_rwkv, 'nn', "in_rwkv")
    qkv_raw = _mm(u, w_qkv, 'nn', "in_qkv")
    z = _mm(u, w_z, 'nn', "in_z")
    gates = _mm(u, w_gates, 'nn', "in_gates")
    ab = _mm(u, w_ab, 'nn', "in_ab")

    r, lw, k2, v, a_, b_, g = _pw_fwd("rwkv_prep", _rwkv_prep_fn, [p_rwkv], rw_consts, [RWKV_W] * 7, 256, conv_w=mixw)
    wkv_in = [r, lw, k2, v, a_, b_]
    y, zs_wkv, *gathered = _scan_fwd("wkv_fwd", (_wkv_prep, _wkv_step), wkv_in, WKV_CHUNK, RWKV_HEADS, 2 * RWKV_HD, RWKV_W, WKV_PER_STEP,
                                     side=None if late is None else (late['shards'][0], False))
    if late is not None:
        W = dict(W, **late['assemble'](0, gathered))
    (ya_in,) = _pw_fwd("rwkv_post", _rwkv_post_fn, [y, r, k2, v, g], post_consts, [RWKV_W], 256, out_dtype=bf16, strip=128)
    ya = _mm(ya_in, W['rwkv_proj'], 'nn', "rwkv_proj")

    lanes = lambda off: slice(off, off + STRIP)
    gd_groups = [[lanes(GDN_HD * h), lanes(GDN_W + GDN_HD * h), lanes(2 * GDN_W + GDN_HD * h)] for h in range(GDN_HEADS)]
    gq, gk, gv = _group_fwd("gdn_prep", _gdn_prep_fn, qkv_raw, W['gdn_conv_w'], [], gd_groups, [], 3, 256)
    (gbeta,) = _pw_fwd("gdn_gate", _gdn_gate_fn, [ab], gd_consts, [W_AB], 512)
    gdn_in = [gq, gk, gv, gbeta]
    o, zs_gdn, *gathered = _scan_fwd("gdn_fwd", (_gdn_prep, _gdn_step), gdn_in, GDN_CHUNK, GDN_HEADS, GDN_HD, GDN_W, GDN_PER_STEP,
                                     side=None if late is None else (late['shards'][1], False))
    if late is not None:
        W = dict(W, **late['assemble'](1, gathered))
    (yb_in,) = _pw_fwd("gdn_post", _gdn_post_fn, [o, z], [nw_t], [GDN_W], 256, out_dtype=bf16, strip=128)
    yb = _mm(yb_in, W['gdn_proj'], 'nn', "gdn_proj")

    ga, gb = _cols(gates, D_MODEL, 0), _cols(gates, D_MODEL, 1)
    (mixed,) = _pw_fwd("mix", _mix_fn, [ga, gb, ya, yb], [], [D_MODEL], 256, out_dtype=bf16, strip=256)
    x1 = _mm(mixed, W['w_out'], 'nn', "w_out", add=x)
    (u2,) = _pw_fwd("norm2", _rms_fn, [x1], [g2n], [D_MODEL], 512, out_dtype=bf16)
    h = _mm(u2, W['ffn_up'], 'nn', "ffn_up")
    act = _ffn_act_fwd(h, W['ffn_conv_w'], 256)
    x2 = _mm(act, W['ffn_down'], 'nn', "ffn_down", add=x1)

    G = {}
    slab_out = None if late is None else N_POS
    dx2, dgf, loss = _loss_call(x2, tgt, gf, 512)
    G['final_g'] = dgf
    dact = _mm(dx2, W['ffn_down'], 'nt', "d_act")
    G['ffn_down'] = _mm(act, dx2, 'tn', "g_ffn_down", out_dtype=bf16)
    dh, G['ffn_conv_w'] = _ffn_act_bwd(h, dact, W['ffn_conv_w'], 128)
    du2 = _mm(dh, W['ffn_up'], 'nt', "d_u2")
    G['ffn_up'] = _mm(u2, dh, 'tn', "g_ffn_up", out_dtype=bf16, col_slabs=slab_out)
    (dx1,), (G['norm2_g'],) = _pw_bwd("norm2_bwd", _rms_fn, [x1], [g2n], [(du2,)], 512, add_to_first=dx2)
    dmixed = _mm(dx1, W['w_out'], 'nt', "d_mixed")
    G['w_out'] = _mm(mixed, dx1, 'tn', "g_w_out", out_dtype=bf16)
    (dga, dgb, dya, dyb), _ = _pw_bwd("mix_bwd", _mix_fn, [ga, gb, ya, yb], [], [(dmixed,)], 256, row_dtypes=[bf16] * 4,
                                      strip=256)
    dya_in = _mm(dya, W['rwkv_proj'], 'nt', "d_ya_in")
    G['rwkv_proj'] = _mm(ya_in, dya, 'tn', "g_rwkv_proj", out_dtype=bf16, col_slabs=slab_out)
    dyb_in = _mm(dyb, W['gdn_proj'], 'nt', "d_yb_in")
    G['gdn_proj'] = _mm(yb_in, dyb, 'tn', "g_gdn_proj", out_dtype=bf16, col_slabs=slab_out)

    (do, dz), (dnw_t,) = _pw_bwd("gdn_post_bwd", _gdn_post_fn, [o, z], [nw_t], [(dyb_in,)], 256, row_dtypes=[f32, bf16],
                                 strip=128)
    G['gdn_norm_w'] = dnw_t.reshape(GDN_HEADS, GDN_HD).sum(axis=0)
    dgq, dgk, dgv, dgbeta = _scan_bwd("gdn_bwd", (_gdn_prep, _gdn_step), gdn_in, do, zs_gdn, GDN_CHUNK, GDN_PER_STEP)
    dqkv_raw, G['gdn_conv_w'], _ = _group_bwd("gdn_prep_bwd", _gdn_prep_fn, qkv_raw, W['gdn_conv_w'], [], gd_groups, [],
                                              [(dgq,), (dgk,), (dgv,)], 128)
    (dab,), (dal_p, ddt_p) = _pw_bwd("gdn_gate_bwd", _gdn_gate_fn, [ab], gd_consts, [(dgbeta,)], 512, row_dtypes=[bf16])
    G['gdn_a_log'], G['gdn_dt_bias'] = dal_p[0, :GDN_HEADS], ddt_p[0, :GDN_HEADS]

    (dy, dr1, dk21, dv1, dg_), (G['rwkv_ln_w'], G['rwkv_ln_b'], G['rwkv_r_k']) = _pw_bwd(
        "rwkv_post_bwd", _rwkv_post_fn, [y, r, k2, v, g], post_consts, [(dya_in,)], 256, strip=128)
    dr2, dlw, dk22, dv2, da_, db_, *G['_arrived'] = _scan_bwd(
        "wkv_bwd", (_wkv_prep, _wkv_step), wkv_in, dy, zs_wkv, WKV_CHUNK, WKV_PER_STEP, side=None if late is None else (late['slabs'](G), True))
    (dp_rwkv,), dmixw, rw_grads = _pw_conv_bwd(
        "rwkv_prep_bwd", _rwkv_prep_fn, [p_rwkv], rw_consts,
        [(dr1, dr2), (dlw,), (dk21, dk22), (dv1, dv2), (da_,), (db_,), (dg_,)], mixw, 256, row_dtypes=[bf16])
    G['rwkv_w0'], dw2p, G['rwkv_a0'], da2p, G['rwkv_g2'], G['rwkv_k_k'], G['rwkv_k_a'] = rw_grads
    G['rwkv_w2'], G['rwkv_a2'] = dw2p[:64], da2p[64:]
    G['rwkv_mu'] = dmixw[0] - dmixw[1]

    dp = jnp.concatenate([dp_rwkv, dqkv_raw, dz, dga, dgb, dab], axis=1)
    G['w_in_pad'] = _mm(u, dp, 'tn', "g_w_in", out_dtype=bf16)
    if late is None:
        du = _mm(dp, wp, 'nt', "d_u")
    else:
        du, *G['_arrived_w_in'] = _mm(dp, wp, 'nt', "d_u", side=(late['w_in_slabs'](G), True))
    (dx,), (G['norm1_g'],) = _pw_bwd("norm1_bwd", _rms_fn, [x], [g1], [(du,)], 512, add_to_first=dx1)
    return loss, dx, G


IN_WIDTH = OFF_AB + 8
PAD_ORDER = ((0, OFF_GATES), (OFF_GATES + 8, IN_WIDTH), (OFF_GATES, OFF_GATES + 8))


def _pad_w_in(w):
    return jnp.concatenate([w[:, a:b] for a, b in PAD_ORDER] + [jnp.zeros((w.shape[0], W_AB - 8), w.dtype)], axis=1)


def _pad_w_in_shards(shards):
    width = shards[0].shape[1]
    parts = []
    for a, b in PAD_ORDER:
        for j, sh in enumerate(shards):
            lo, hi = max(a, j * width), min(b, (j + 1) * width)
            if lo < hi:
                parts.append(sh[:, lo - j * width:hi - j * width])
    return jnp.concatenate(parts + [jnp.zeros((shards[0].shape[0], W_AB - 8), shards[0].dtype)], axis=1)


def _unpad_cols(wp, lo, hi):
    parts, off = [], 0
    for a, b in PAD_ORDER:
        l, h = max(a, lo), min(b, hi)
        if l < h:
            parts.append((l, wp[:, off + l - a:off + h - a]))
        off += b - a
    parts.sort(key=lambda t: t[0])
    return parts[0][1] if len(parts) == 1 else jnp.concatenate([p for _, p in parts], axis=1)


def _unpad_w_in(wp):
    return _unpad_cols(wp, 0, IN_WIDTH)


BIG = ('w_in', 'rwkv_proj', 'gdn_proj', 'w_out', 'ffn_up', 'ffn_down')
SMALL_SHARDED = ('rwkv_w2', 'rwkv_a2', 'rwkv_g2', 'gdn_conv_w', 'ffn_conv_w')


def _rows128(shape):
    n = 1
    for d in shape:
        n *= d
    return -(-n // LANES)


def _pack128(arrays):
    parts = []
    for a in arrays:
        flat = a.reshape(-1)
        rows = _rows128(a.shape)
        parts.append(jnp.pad(flat, (0, rows * LANES - flat.shape[0])).reshape(rows, LANES))
    buf = jnp.concatenate(parts, axis=0)
    return jnp.pad(buf, ((0, -buf.shape[0] % HALO), (0, 0)))


def _unpack128(buf, shapes):
    out, off = [], 0
    for s in shapes:
        rows, n = _rows128(s), 1
        for d in s:
            n *= d
        out.append(buf[off:off + rows].reshape(-1)[:n].reshape(s))
        off += rows
    return out


def _param_tile(r, c):
    best = None
    for d in range(2 * HALO, r + 1, 2 * HALO):
        if r % d == 0 and d * c * 4 <= TILE_BYTES:
            best = d
    if best is not None or r * c * 4 <= TILE_BYTES:
        return (best if best is not None else r), c
    return r, 128


def _xy_exchange(name, bufs, scatter):
    n = len(bufs)

    def body(*refs):
        start, finish = _xy_copies(refs[:n], refs[n:2 * n], refs[2 * n:], scatter)
        start()
        finish()

    return pl.pallas_call(
        body, in_specs=[pl.BlockSpec(memory_space=pl.ANY)] * n, out_specs=[pl.BlockSpec(memory_space=pl.ANY)] * n,
        out_shape=_xy_out_shapes(bufs, scatter), scratch_shapes=_xy_sems(n, scatter), name=name)(*bufs)


def _sibling_exchange(name, bufs):
    n = len(bufs)

    def body(*refs):
        in_refs, out_refs, send_sems, recv_sems = refs[:n], refs[n:2 * n], refs[2 * n], refs[2 * n + 1]
        x, y, c = lax.axis_index("x"), lax.axis_index("y"), lax.axis_index("c")
        copies = [pltpu.make_async_remote_copy(
            src_ref=in_refs[a], dst_ref=out_refs[a], send_sem=send_sems.at[a], recv_sem=recv_sems.at[a],
            device_id=(x, y, 1 - c), device_id_type=pl.DeviceIdType.MESH) for a in range(n)]
        for cp in copies:
            cp.start()
        for cp in copies:
            cp.wait()

    return pl.pallas_call(
        body, in_specs=[pl.BlockSpec(memory_space=pl.ANY)] * n, out_specs=[pl.BlockSpec(memory_space=pl.ANY)] * n,
        out_shape=[jax.ShapeDtypeStruct(b.shape, b.dtype) for b in bufs],
        scratch_shapes=[pltpu.SemaphoreType.DMA((n,)), pltpu.SemaphoreType.DMA((n,))], name=name)(*bufs)


def _sum_slots(name, buf):
    _, R, L = buf.shape
    tr, tc = _param_tile(R, L)

    def body(b_ref, o_ref):
        part = lambda s: b_ref[s].astype(f32)
        o_ref[...] = ((part(0) + part(1)) + part(2)) + part(3)

    return pl.pallas_call(
        body, grid=(R // tr, L // tc),
        in_specs=[pl.BlockSpec((N_POS, tr, tc), lambda i, j: (0, i, j))],
        out_specs=pl.BlockSpec((tr, tc), lambda i, j: (i, j)),
        out_shape=jax.ShapeDtypeStruct((R, L), f32), name=name,
        compiler_params=_params("parallel", "parallel"))(buf)


def _adamw(name, w, ga, gb, m, v):
    R, L = w.shape
    tr, tc = _param_tile(R, L)
    c1 = 1.0 / (1.0 - ADAM_B1 ** ADAM_STEP)
    c2 = 1.0 / (1.0 - ADAM_B2 ** ADAM_STEP)

    def body(w_ref, ga_ref, gb_ref, m_ref, v_ref, g_out, d_out, m_out, v_out):
        g = ga_ref[...] + gb_ref[...]
        m_new = ADAM_B1 * m_ref[...] + (1.0 - ADAM_B1) * g
        v_new = ADAM_B2 * v_ref[...] + (1.0 - ADAM_B2) * (g * g)
        g_out[...] = g
        m_out[...] = m_new
        v_out[...] = v_new
        d_out[...] = -ADAM_LR * ((m_new * c1) / (jnp.sqrt(v_new * c2) + ADAM_EPS) + ADAM_WD * w_ref[...])

    spec = pl.BlockSpec((tr, tc), lambda i, j: (i, j))
    return pl.pallas_call(
        body, grid=(R // tr, L // tc), in_specs=[spec] * 5, out_specs=[spec] * 4,
        out_shape=[jax.ShapeDtypeStruct((R, L), f32)] * 4, name=name,
        compiler_params=_params("parallel", "parallel"))(w, ga, gb, m, v)


def _step(x, loss_target, P, M, V):
    shapes = {n: tuple(P[n].shape) for n in WEIGHTS}
    sh_shapes = [shapes[n] for n in SMALL_SHARDED]
    packed = SMALL_SHARDED + SMALL
    late_names = BIG[1:]

    def whole(n, g):
        return g.reshape(-1, g.shape[2]) if n in ROW_SHARDED else jnp.concatenate([g[j] for j in range(N_POS)], axis=1)

    def slabs(G, n, dtype=f32):
        r, c = shapes[n]
        full = G[n].astype(dtype)
        if full.ndim == 3:
            return full
        return full.reshape(N_POS, r, c) if n in ROW_SHARDED else full.reshape(r, N_POS, c).transpose(1, 0, 2)

    g_w_in, g_small = _xy_exchange("gather_w_in", [P['w_in'].astype(bf16), _pack128([P[n] for n in SMALL_SHARDED])],
                                   scatter=False)
    W = {n: P[n] for n in SMALL}
    W['w_in_pad'] = _pad_w_in_shards([g_w_in[j] for j in range(N_POS)])
    per_pos = [_unpack128(g_small[j], sh_shapes) for j in range(N_POS)]
    for q, n in enumerate(SMALL_SHARDED):
        W[n] = jnp.concatenate([per_pos[j][q] for j in range(N_POS)], axis=1)
    groups = (('rwkv_proj', 'gdn_proj', 'ffn_up'), ('w_out', 'ffn_down'))
    late = dict(shards=[[P[n].astype(bf16) for n in grp] for grp in groups],
                assemble=lambda q, gathered: {n: whole(n, g) for n, g in zip(groups[q], gathered)},
                slabs=lambda G: [slabs(G, n, bf16) for n in late_names],
                w_in_slabs=lambda G: [jnp.stack([_unpad_cols(G['w_in_pad'], j * shapes['w_in'][1], (j + 1) * shapes['w_in'][1])
                                                 for j in range(N_POS)])])

    loss_rows, dx, G = _local_step(x, loss_target, W, late)
    arrived_late = G.pop('_arrived')
    (arrived_w_in,) = G.pop('_arrived_w_in')
    G.pop('w_in_pad')

    small_slabs = jnp.stack([_pack128([slabs(G, n)[j] for n in SMALL_SHARDED] + [G[n] for n in SMALL]) for j in range(N_POS)])
    (arrived_small,) = _xy_exchange("scatter_small", [small_slabs], scatter=True)
    contributions = [arrived_w_in] + list(arrived_late) + [arrived_small]
    tags = list(BIG) + ['small']
    plane = [_sum_slots("sum_" + t, cbuf) for t, cbuf in zip(tags, contributions)]
    sibling = _sibling_exchange("sibling_grads", plane)

    out = {}
    names4 = ('grad', 'delta', 'new_m', 'new_v')
    for q, n in enumerate(BIG):
        for tag, t in zip(names4, _adamw("adamw_" + n, P[n], plane[q], sibling[q], M[n], V[n])):
            out[tag + '_' + n] = t
    small_out = _adamw("adamw_small", _pack128([P[n] for n in packed]), plane[-1], sibling[-1],
                       _pack128([M[n] for n in packed]), _pack128([V[n] for n in packed]))
    for tag, buf in zip(names4, small_out):
        for n, t in zip(packed, _unpack128(buf, [shapes[n] for n in packed])):
            out[tag + '_' + n] = t
    loss = lax.psum(loss_rows[0, 0], ("x", "y", "c"))
    return loss, dx, out


def kernel(x, norm1_g, w_in, rwkv_mu, rwkv_w0, rwkv_w2, rwkv_a0, rwkv_a2, rwkv_g2, rwkv_k_k, rwkv_k_a, rwkv_r_k, rwkv_ln_w, rwkv_ln_b, rwkv_proj, gdn_conv_w, gdn_a_log, gdn_dt_bias, gdn_norm_w, gdn_proj, w_out, norm2_g, ffn_up, ffn_conv_w, ffn_down, final_g, loss_target, m_norm1_g, m_w_in, m_rwkv_mu, m_rwkv_w0, m_rwkv_w2, m_rwkv_a0, m_rwkv_a2, m_rwkv_g2, m_rwkv_k_k, m_rwkv_k_a, m_rwkv_r_k, m_rwkv_ln_w, m_rwkv_ln_b, m_rwkv_proj, m_gdn_conv_w, m_gdn_a_log, m_gdn_dt_bias, m_gdn_norm_w, m_gdn_proj, m_w_out, m_norm2_g, m_ffn_up, m_ffn_conv_w, m_ffn_down, m_final_g, v_norm1_g, v_w_in, v_rwkv_mu, v_rwkv_w0, v_rwkv_w2, v_rwkv_a0, v_rwkv_a2, v_rwkv_g2, v_rwkv_k_k, v_rwkv_k_a, v_rwkv_r_k, v_rwkv_ln_w, v_rwkv_ln_b, v_rwkv_proj, v_gdn_conv_w, v_gdn_a_log, v_gdn_dt_bias, v_gdn_norm_w, v_gdn_proj, v_w_out, v_norm2_g, v_ffn_up, v_ffn_conv_w, v_ffn_down, v_final_g):
    weights = (norm1_g, w_in, rwkv_mu, rwkv_w0, rwkv_w2, rwkv_a0, rwkv_a2, rwkv_g2, rwkv_k_k, rwkv_k_a, rwkv_r_k, rwkv_ln_w,
               rwkv_ln_b, rwkv_proj, gdn_conv_w, gdn_a_log, gdn_dt_bias, gdn_norm_w, gdn_proj, w_out, norm2_g, ffn_up,
               ffn_conv_w, ffn_down, final_g)
    m_in = (m_norm1_g, m_w_in, m_rwkv_mu, m_rwkv_w0, m_rwkv_w2, m_rwkv_a0, m_rwkv_a2, m_rwkv_g2, m_rwkv_k_k, m_rwkv_k_a,
            m_rwkv_r_k, m_rwkv_ln_w, m_rwkv_ln_b, m_rwkv_proj, m_gdn_conv_w, m_gdn_a_log, m_gdn_dt_bias, m_gdn_norm_w,
            m_gdn_proj, m_w_out, m_norm2_g, m_ffn_up, m_ffn_conv_w, m_ffn_down, m_final_g)
    v_in = (v_norm1_g, v_w_in, v_rwkv_mu, v_rwkv_w0, v_rwkv_w2, v_rwkv_a0, v_rwkv_a2, v_rwkv_g2, v_rwkv_k_k, v_rwkv_k_a,
            v_rwkv_r_k, v_rwkv_ln_w, v_rwkv_ln_b, v_rwkv_proj, v_gdn_conv_w, v_gdn_a_log, v_gdn_dt_bias, v_gdn_norm_w,
            v_gdn_proj, v_w_out, v_norm2_g, v_ffn_up, v_ffn_conv_w, v_ffn_down, v_final_g)
    drop = lambda n, a: a if n == 'final_g' else a[0]
    P = {n: drop(n, a) for n, a in zip(WEIGHTS, weights)}
    M = {n: drop(n, a) for n, a in zip(WEIGHTS, m_in)}
    V = {n: drop(n, a) for n, a in zip(WEIGHTS, v_in)}
    loss, dx, out = _step(x[0], loss_target[0], P, M, V)
    lift = lambda n, a: a if n == 'final_g' else a[None]
    res = [loss, dx[None]]
    for tag in ('grad', 'delta', 'new_m', 'new_v'):
        res += [lift(n, out[tag + '_' + n]) for n in WEIGHTS]
    return tuple(res)
```

```python
import functools

import jax
import jax.numpy as jnp
from jax import lax
from jax.experimental import pallas as pl
from jax.experimental.pallas import tpu as pltpu

f32 = jnp.float32
bf16 = jnp.bfloat16
HI = lax.Precision.HIGHEST

D_MODEL = 1024
RWKV_HEADS, RWKV_HD, RWKV_W = 8, 64, 512
GDN_HEADS, GDN_HD, GDN_W = 4, 128, 512
FFN_H = 2816
NORM_EPS, L2_EPS, GN_EPS = 1e-6, 1e-6, 64e-5
W_AB = 256
OFF_QKV, OFF_Z, OFF_GATES, OFF_AB = 1792, 3328, 3840, 5888
W_IN_PAD = OFF_AB + W_AB
WKV_CHUNK, WKV_PER_STEP = 64, 4
GDN_CHUNK, GDN_PER_STEP = 128, 4
HALO = 8
LANES = 128
TILE_BYTES = 1 << 20
VMEM_LIMIT = 56 * 1024 * 1024

ADAM_LR, ADAM_B1, ADAM_B2, ADAM_EPS, ADAM_WD, ADAM_STEP = 0.001, 0.9, 0.999, 1e-08, 0.01, 10

ROW_SHARDED = ('w_out', 'ffn_down')
SMALL = ('norm1_g', 'rwkv_mu', 'rwkv_w0', 'rwkv_a0', 'rwkv_k_k', 'rwkv_k_a', 'rwkv_r_k', 'rwkv_ln_w', 'rwkv_ln_b',
         'gdn_a_log', 'gdn_dt_bias', 'gdn_norm_w', 'norm2_g', 'final_g')
WEIGHTS = ('norm1_g', 'w_in', 'rwkv_mu', 'rwkv_w0', 'rwkv_w2', 'rwkv_a0', 'rwkv_a2', 'rwkv_g2', 'rwkv_k_k', 'rwkv_k_a',
           'rwkv_r_k', 'rwkv_ln_w', 'rwkv_ln_b', 'rwkv_proj', 'gdn_conv_w', 'gdn_a_log', 'gdn_dt_bias', 'gdn_norm_w',
           'gdn_proj', 'w_out', 'norm2_g', 'ffn_up', 'ffn_conv_w', 'ffn_down', 'final_g')


def _params(*sem):
    return pltpu.CompilerParams(dimension_semantics=sem, vmem_limit_bytes=VMEM_LIMIT)


def _tile(n, limit):
    if n <= limit:
        return n
    best = None
    for d in range(128, limit + 1, 128):
        if n % d == 0:
            best = d
    if best is None:
        raise ValueError(f"no tile for {n} under {limit}")
    return best


MM_BLOCK_BYTES = 6 << 20


def _mm(a, b, mode, name, add=None, out_dtype=f32, side=None, col_slabs=None):
    if mode == 'nn':
        (M, K), N = a.shape, b.shape[1]
    elif mode == 'nt':
        (M, K), N = a.shape, b.shape[0]
    else:
        (K, M), N = a.shape, b.shape[1]
    tm = _tile(M, 1408)
    tk = _tile(K, min(2816, MM_BLOCK_BYTES // (tm * a.dtype.itemsize)))
    tn = _tile(N, max(128, min(MM_BLOCK_BYTES // (tk * b.dtype.itemsize), MM_BLOCK_BYTES // (tm * 4)) // 128 * 128))
    if col_slabs is not None:
        tn = N // col_slabs
    nk = K // tk
    grid = (M // tm, N // tn, nk)
    dn = {'nn': (((1,), (0,)), ((), ())), 'nt': (((1,), (1,)), ((), ())), 'tn': (((0,), (0,)), ((), ()))}[mode]
    n_add = 0 if add is None else 1
    n_side = 0 if side is None else len(side[0])

    def body(a_ref, b_ref, *rest):
        add_ref = rest[0] if add is not None else None
        side_in, rest = rest[n_add:n_add + n_side], rest[n_add + n_side:]
        o_ref, side_out, rest = rest[0], rest[1:1 + n_side], rest[1 + n_side:]
        acc_ref, rest = (rest[0], rest[1:]) if nk > 1 else (None, rest)
        ids = [pl.program_id(d) for d in range(3)]
        if side is not None:
            start, finish = _xy_copies(side_in, side_out, rest, side[1])
            pl.when((ids[0] == 0) & (ids[1] == 0) & (ids[2] == 0))(start)
        acc = lax.dot_general(a_ref[...].astype(bf16), b_ref[...].astype(bf16), dn, preferred_element_type=f32)
        if nk == 1:
            o_ref[...] = (acc + add_ref[...] if add is not None else acc).astype(out_dtype)
        else:
            k = ids[2]

            @pl.when(k == 0)
            def _():
                acc_ref[...] = acc + add_ref[...] if add is not None else acc

            @pl.when(k > 0)
            def _():
                acc_ref[...] += acc

            @pl.when(k == nk - 1)
            def _():
                o_ref[...] = acc_ref[...].astype(out_dtype)
        if side is not None:
            pl.when((ids[0] == grid[0] - 1) & (ids[1] == grid[1] - 1) & (ids[2] == nk - 1))(finish)

    a_spec = (pl.BlockSpec((tk, tm), lambda i, j, k: (k, i)) if mode == 'tn'
              else pl.BlockSpec((tm, tk), lambda i, j, k: (i, k)))
    b_spec = (pl.BlockSpec((tn, tk), lambda i, j, k: (j, k)) if mode == 'nt'
              else pl.BlockSpec((tk, tn), lambda i, j, k: (k, j)))
    o_spec = pl.BlockSpec((tm, tn), lambda i, j, k: (i, j))
    o_shape = jax.ShapeDtypeStruct((M, N), out_dtype)
    if col_slabs is not None:
        o_spec = pl.BlockSpec((None, tm, tn), lambda i, j, k: (j, i, 0))
        o_shape = jax.ShapeDtypeStruct((col_slabs, M, tn), out_dtype)
    any_spec = pl.BlockSpec(memory_space=pl.ANY)
    side_bufs = [] if side is None else list(side[0])
    ins, specs = [a, b], [a_spec, b_spec]
    if add is not None:
        ins.append(add)
        specs.append(o_spec)
    outs = pl.pallas_call(
        body, grid=grid, in_specs=specs + [any_spec] * n_side, out_specs=[o_spec] + [any_spec] * n_side,
        out_shape=[o_shape] + (_xy_out_shapes(side_bufs, side[1]) if side is not None else []),
        scratch_shapes=([pltpu.VMEM((tm, tn), f32)] if nk > 1 else []) + (_xy_sems(n_side, side[1]) if side is not None else []),
        name=name,
        compiler_params=_params(*(("arbitrary",) * 3 if side is not None else ("parallel", "parallel", "arbitrary"))))(
            *ins, *side_bufs)
    return list(outs) if side is not None else outs[0]


def _shift_down(cur, prev, s):
    if s == 0:
        return cur
    ext = jnp.concatenate([prev, cur], axis=0)
    return pltpu.roll(ext, s, 0)[HALO:]


def _shift_up(cur, nxt, s):
    if s == 0:
        return cur
    ext = jnp.concatenate([cur, nxt], axis=0)
    return pltpu.roll(ext, ext.shape[0] - s, 0)[:cur.shape[0]]


def _conv_apply(cur, prev, w_ref, shifted=None):
    taps = w_ref.shape[0]
    out = None
    for i in range(taps):
        s = taps - 1 - i
        term = (shifted[s] if shifted is not None else _shift_down(cur, prev, s)) * w_ref[pl.ds(i, 1), :]
        out = term if out is None else out + term
    return out


def _row_spec(tm, w, col=0):
    return pl.BlockSpec((tm, w), lambda i: (i, col))


def _cols(a, width, col):
    return (a, width, col)


def _row_of(r):
    return r if isinstance(r, tuple) else (r, r.shape[1], 0)


def _prev_spec(tm, w):
    return pl.BlockSpec((HALO, w), lambda i: (jnp.maximum(i * (tm // HALO) - 1, 0), 0))


def _next_spec(tm, w, T):
    return pl.BlockSpec((HALO, w), lambda i: (jnp.minimum((i + 1) * (tm // HALO), T // HALO - 1), 0))


def _full_spec(shape):
    return pl.BlockSpec(shape, lambda i: (0,) * len(shape))


def _pw_fwd(name, fn, rows, consts, out_widths, tm, conv_w=None, out_dtype=f32, strip=None):
    T = _row_of(rows[0])[0].shape[0]
    nr, nc = len(rows), len(consts)

    def body(*refs):
        i = pl.program_id(0)
        if strip is not None:
            for j in range(out_widths[0] // strip):
                sl = slice(strip * j, strip * (j + 1))
                outs = fn(*[r[:, sl] for r in refs[:nr + nc]])
                for o_ref, o in zip(refs[nr + nc:], outs):
                    o_ref[:, sl] = o.astype(out_dtype)
            return
        vals = [r[...] for r in refs[:nr]]
        p = nr
        if conv_w is not None:
            prev = jnp.where(i > 0, refs[p][...], 0.0)
            vals[0] = _conv_apply(vals[0], prev, refs[p + 1])
            p += 2
        cvals = [r[...] for r in refs[p:p + nc]]
        outs = fn(*vals, *cvals)
        for o_ref, o in zip(refs[p + nc:], outs):
            o_ref[...] = o.astype(out_dtype)

    ins = [_row_of(r)[0] for r in rows]
    specs = [_row_spec(tm, *_row_of(r)[1:]) for r in rows]
    if conv_w is not None:
        ins += [rows[0], conv_w]
        specs += [_prev_spec(tm, rows[0].shape[1]), _full_spec(conv_w.shape)]
    ins += list(consts)
    specs += [_full_spec(c.shape) for c in consts]
    outs = pl.pallas_call(
        body, grid=(T // tm,), in_specs=specs,
        out_specs=[_row_spec(tm, w) for w in out_widths],
        out_shape=[jax.ShapeDtypeStruct((T, w), out_dtype) for w in out_widths], name=name,
        compiler_params=_params("parallel"))(*ins)
    return outs


def _pw_bwd(name, fn, rows, consts, cots, tm, add_to_first=None, row_dtypes=None, strip=None):
    rows = [_row_of(r) for r in rows]
    T = rows[0][0].shape[0]
    nr, nc = len(rows), len(consts)
    flat_cots = [c for grp in cots for c in grp]
    row_dtypes = row_dtypes or [f32] * nr
    n_extra = 0 if add_to_first is None else 1
    width = rows[0][1]

    def body(*refs):
        i = pl.program_id(0)
        in_refs, cot_refs = refs[:nr + nc], refs[nr + nc:nr + nc + len(flat_cots)]
        extra_ref = refs[nr + nc + len(flat_cots)] if add_to_first is not None else None
        row_out = refs[nr + nc + len(flat_cots) + n_extra:][:nr]
        const_out = refs[nr + nc + len(flat_cots) + n_extra + nr:]

        @pl.when(i == 0)
        def _():
            for q in range(nc):
                const_out[q][...] = jnp.zeros_like(const_out[q])

        def part(sl):
            cot_vals, p = [], 0
            for grp in cots:
                acc = cot_refs[p][:, sl]
                for q in range(1, len(grp)):
                    acc = acc + cot_refs[p + q][:, sl]
                p += len(grp)
                cot_vals.append(acc)
            _, vjp = jax.vjp(fn, *[r[:, sl] for r in in_refs])
            grads = vjp(tuple(cot_vals))
            for q in range(nr):
                g = grads[q]
                if q == 0 and extra_ref is not None:
                    g = g + extra_ref[:, sl]
                row_out[q][:, sl] = g.astype(row_dtypes[q])
            for q in range(nc):
                const_out[q][:, sl] += grads[nr + q]

        if strip is None:
            part(slice(None))
        else:
            for j in range(width // strip):
                part(slice(strip * j, strip * (j + 1)))

    ins = [r[0] for r in rows] + list(consts) + flat_cots
    specs = ([_row_spec(tm, r[1], r[2]) for r in rows] + [_full_spec(c.shape) for c in consts]
             + [_row_spec(tm, c.shape[1]) for c in flat_cots])
    if add_to_first is not None:
        ins.append(add_to_first)
        specs.append(_row_spec(tm, add_to_first.shape[1]))
    out_shapes = ([jax.ShapeDtypeStruct((T, r[1]), d) for r, d in zip(rows, row_dtypes)]
                  + [jax.ShapeDtypeStruct(c.shape, f32) for c in consts])
    out_specs = [_row_spec(tm, r[1]) for r in rows] + [_full_spec(c.shape) for c in consts]
    outs = pl.pallas_call(
        body, grid=(T // tm,), in_specs=specs, out_specs=out_specs, out_shape=out_shapes, name=name,
        compiler_params=_params("arbitrary"))(*ins)
    return list(outs[:nr]), list(outs[nr:])


def _pw_conv_bwd(name, fn, rows, consts, cots, conv_w, tm, row_dtypes=None):
    T, W0 = rows[0].shape
    nr, nc = len(rows), len(consts)
    taps = conv_w.shape[0]
    nblk = T // tm
    flat_cots = [c for grp in cots for c in grp]
    row_dtypes = row_dtypes or [f32] * nr

    def body(*refs):
        i = pl.program_id(0)
        p = 0
        cur = [r[...] for r in refs[p:p + nr]]; p += nr
        nxt = [r[...] for r in refs[p:p + nr]]; p += nr
        prev = jnp.where(i > 0, refs[p][...], 0.0); p += 1
        w_ref = refs[p]; p += 1
        cvals = [r[...] for r in refs[p:p + nc]]; p += nc

        def summed(p0):
            out, q = [], p0
            for grp in cots:
                acc = refs[q][...]
                for t in range(1, len(grp)):
                    acc = acc + refs[q + t][...]
                q += len(grp)
                out.append(acc)
            return out, q

        cot_cur, p = summed(p)
        cot_nxt, p = summed(p)
        row_out, dw_ref, const_out = refs[p:p + nr], refs[p + nr], refs[p + nr + 1:]

        x_cur = cur[0]
        x_down = [_shift_down(x_cur, prev, s_) for s_ in range(taps)]
        _, vjp = jax.vjp(fn, _conv_apply(x_cur, prev, w_ref, x_down), *cur[1:], *cvals)
        grads = vjp(tuple(cot_cur))
        _, vjp_n = jax.vjp(fn, _conv_apply(nxt[0], x_cur[tm - HALO:], w_ref), *nxt[1:], *cvals)
        dc_n = jnp.where(i < nblk - 1, vjp_n(tuple(cot_nxt))[0], 0.0)
        dc = grads[0]

        @pl.when(i == 0)
        def _():
            dw_ref[...] = jnp.zeros_like(dw_ref)
            for q in range(nc):
                const_out[q][...] = jnp.zeros_like(const_out[q])

        dx = None
        for k in range(taps):
            s_ = taps - 1 - k
            term = _shift_up(dc, dc_n, s_) * w_ref[pl.ds(k, 1), :]
            dx = term if dx is None else dx + term
            dw_ref[pl.ds(k, 1), :] += jnp.sum(dc * x_down[s_], axis=0, keepdims=True)
        row_out[0][...] = dx.astype(row_dtypes[0])
        for q in range(1, nr):
            row_out[q][...] = grads[q].astype(row_dtypes[q])
        for q in range(nc):
            const_out[q][...] += grads[nr + q]

    ins = list(rows) + list(rows) + [rows[0], conv_w] + list(consts) + flat_cots + flat_cots
    specs = ([_row_spec(tm, r.shape[1]) for r in rows] + [_next_spec(tm, r.shape[1], T) for r in rows]
             + [_prev_spec(tm, W0), _full_spec(conv_w.shape)] + [_full_spec(c.shape) for c in consts]
             + [_row_spec(tm, c.shape[1]) for c in flat_cots] + [_next_spec(tm, c.shape[1], T) for c in flat_cots])
    out_shapes = ([jax.ShapeDtypeStruct(r.shape, d) for r, d in zip(rows, row_dtypes)]
                  + [jax.ShapeDtypeStruct(conv_w.shape, f32)] + [jax.ShapeDtypeStruct(c.shape, f32) for c in consts])
    out_specs = ([_row_spec(tm, r.shape[1]) for r in rows] + [_full_spec(conv_w.shape)]
                 + [_full_spec(c.shape) for c in consts])
    outs = pl.pallas_call(
        body, grid=(nblk,), in_specs=specs, out_specs=out_specs, out_shape=out_shapes, name=name,
        compiler_params=_params("arbitrary"))(*ins)
    return list(outs[:nr]), outs[nr], list(outs[nr + 1:])


def _sigmoid(x):
    return 0.5 * jnp.tanh(0.5 * x) + 0.5


def _softplus(x):
    return jnp.maximum(x, 0.0) + jnp.log(1.0 + jnp.exp(jnp.minimum(x, -x)))


def _seg_sum_impl(x, seg):
    w = x.shape[-1]
    r = lax.broadcasted_iota(jnp.int32, (w, w), 0) // seg
    c = lax.broadcasted_iota(jnp.int32, (w, w), 1) // seg
    ones = (r == c).astype(bf16)
    hi = x.astype(bf16)
    lo = (x - hi.astype(f32)).astype(bf16)
    return (jnp.dot(hi, ones, preferred_element_type=f32) + jnp.dot(lo, ones, preferred_element_type=f32))


@functools.partial(jax.custom_vjp, nondiff_argnums=(1,))
def _seg_sum(x, seg):
    return _seg_sum_impl(x, seg)


_seg_sum.defvjp(lambda x, seg: (_seg_sum_impl(x, seg), None), lambda seg, _, g: (_seg_sum_impl(g, seg),))


def _rms(x, g):
    return x * lax.rsqrt(jnp.mean(x * x, axis=-1, keepdims=True) + NORM_EPS) * g


def _rms_fn(x, g):
    return (_rms(x, g),)


def _loss_rows(x2, tgt, g):
    e = _rms(x2, g) - tgt
    return 0.5 * jnp.sum(e * e, axis=-1, keepdims=True) * (1.0 / D_MODEL)


@jax.custom_vjp
def _dot_lo(a, b):
    return jnp.dot(a.astype(bf16), b.astype(bf16), preferred_element_type=f32)


def _dot_lo_bwd(ab, g):
    a, b = ab
    gl = g.astype(bf16)
    return (lax.dot_general(gl, b.astype(bf16), (((1,), (1,)), ((), ())), preferred_element_type=f32),
            lax.dot_general(a.astype(bf16), gl, (((0,), (0,)), ((), ())), preferred_element_type=f32))


_dot_lo.defvjp(lambda a, b: (_dot_lo(a, b), (a, b)), _dot_lo_bwd)


def _rwkv_prep_fn(ps, w0, w2p, a0, a2p, g2, k_k, k_a):
    r, k, v = ps[:, 0:512], ps[:, 512:1024], ps[:, 1024:1536]
    wa, gl = ps[:, 1536:1664], ps[:, 1664:1792]
    z = w0 + _dot_lo(jnp.tanh(wa), w2p)
    w_log = -_softplus(-z) - 0.5
    lw = -jnp.exp(w_log)
    a = _sigmoid(a0 + _dot_lo(wa, a2p))
    g = _dot_lo(_sigmoid(gl), g2)
    kx = k * k_k
    kk = kx * lax.rsqrt(_seg_sum(kx * kx, RWKV_HD) + L2_EPS)
    k2 = k * (1.0 + (a - 1.0) * k_a)
    return r, lw, k2, v, -kk, kk * a, g


def _rwkv_post_fn(y, r, k2, v, g, ln_w, ln_b, rk):
    mean = _seg_sum(y, RWKV_HD) * (1.0 / RWKV_HD)
    yc = y - mean
    var = _seg_sum(yc * yc, RWKV_HD) * (1.0 / RWKV_HD)
    yn = yc * lax.rsqrt(var + GN_EPS) * ln_w + ln_b
    bonus = _seg_sum(r * k2 * rk, RWKV_HD) * v
    return ((yn + bonus) * g,)


def _gdn_prep_fn(cq, ck, cv):
    silu = lambda c: c * _sigmoid(c)
    q, k = silu(cq), silu(ck)
    q = q * lax.rsqrt(jnp.sum(q * q, axis=-1, keepdims=True) + L2_EPS) * (GDN_HD ** -0.5)
    k = k * lax.rsqrt(jnp.sum(k * k, axis=-1, keepdims=True) + L2_EPS)
    return q, k, silu(cv)


def _gdn_gate_fn(ab, al_p, dt_p):
    lane = lax.broadcasted_iota(jnp.int32, ab.shape, 1)
    gpart = -jnp.exp(al_p) * _softplus(ab + dt_p)
    return (jnp.where(lane < GDN_HEADS, gpart, jnp.where(lane < 2 * GDN_HEADS, _sigmoid(ab), 0.0)),)


def _gdn_post_fn(o, z, nw):
    ms = _seg_sum(o * o, GDN_HD) * (1.0 / GDN_HD)
    return (o * lax.rsqrt(ms + NORM_EPS) * nw * (z * _sigmoid(z)),)


def _mix_fn(ga, gb, ya, yb):
    return (_sigmoid(ga) * ya + _sigmoid(gb) * yb,)


STRIP = 128


def _strip_conv(ref, prev_ref, w_ref, sl, first, taps):
    cur = ref[:, sl]
    prev = jnp.where(first, 0.0, prev_ref[:, sl])
    down = [_shift_down(cur, prev, s) for s in range(taps)]
    conv = None
    for k in range(taps):
        term = down[taps - 1 - k] * w_ref[pl.ds(k, 1), sl]
        conv = term if conv is None else conv + term
    return cur, down, conv


def _group_fwd(name, fn, x, w, shared_cols, group_cols, consts, n_out, tm):
    T, W = x.shape
    taps = w.shape[0]
    n_groups = len(group_cols)
    nc = len(consts)

    def body(x_ref, xp_ref, w_ref, *refs):
        const_refs, out_refs = refs[:nc], refs[nc:]
        first = pl.program_id(0) == 0
        shared = [_strip_conv(x_ref, xp_ref, w_ref, sl, first, taps)[2] for sl in shared_cols]
        for j, cols in enumerate(group_cols):
            sl = slice(STRIP * j, STRIP * (j + 1))
            convs = [_strip_conv(x_ref, xp_ref, w_ref, c, first, taps)[2] for c in cols]
            outs = fn(*convs, *shared, *[c[:, sl] for c in const_refs])
            for o_ref, o in zip(out_refs, outs):
                o_ref[:, sl] = o

    return pl.pallas_call(
        body, grid=(T // tm,),
        in_specs=[_row_spec(tm, W), _prev_spec(tm, W), _full_spec(w.shape)] + [_full_spec(c.shape) for c in consts],
        out_specs=[_row_spec(tm, STRIP * n_groups)] * n_out,
        out_shape=[jax.ShapeDtypeStruct((T, STRIP * n_groups), f32)] * n_out, name=name,
        compiler_params=_params("parallel"))(x, x, w, *consts)


def _group_bwd(name, fn, x, w, shared_cols, group_cols, consts, cots, tm):
    T, W = x.shape
    taps = w.shape[0]
    nblk = T // tm
    nc, ns = len(consts), len(shared_cols)
    flat_cots = [c for grp in cots for c in grp]
    n_cot = len(flat_cots)

    def body(x_ref, xp_ref, xn_ref, w_ref, *refs):
        const_refs, refs = refs[:nc], refs[nc:]
        cot_refs, cotn_refs, refs = refs[:n_cot], refs[n_cot:2 * n_cot], refs[2 * n_cot:]
        dx_ref, dw_ref, const_out = refs[0], refs[1], refs[2:]
        i = pl.program_id(0)
        first, last = i == 0, i == nblk - 1

        @pl.when(first)
        def _():
            dw_ref[...] = jnp.zeros_like(dw_ref)
            for q in range(nc):
                const_out[q][...] = jnp.zeros_like(const_out[q])

        def convs_of(sl):
            cur, down, conv = _strip_conv(x_ref, xp_ref, w_ref, sl, first, taps)
            nxt, conv_n = xn_ref[:, sl], None
            for k in range(taps):
                term = _shift_down(nxt, cur[tm - HALO:], taps - 1 - k) * w_ref[pl.ds(k, 1), sl]
                conv_n = term if conv_n is None else conv_n + term
            return down, conv, conv_n

        def conv_back(sl, down, dc, dc_n):
            dx = None
            for k in range(taps):
                s_ = taps - 1 - k
                term = _shift_up(dc, dc_n, s_) * w_ref[pl.ds(k, 1), sl]
                dx = term if dx is None else dx + term
                dw_ref[pl.ds(k, 1), sl] += jnp.sum(dc * down[s_], axis=0, keepdims=True)
            dx_ref[:, sl] = dx.astype(dx_ref.dtype)

        def summed(refs_, sl, mask):
            out, p = [], 0
            for grp in cots:
                acc = refs_[p][:, sl]
                for t in range(1, len(grp)):
                    acc = acc + refs_[p + t][:, sl]
                p += len(grp)
                out.append(jnp.where(last, 0.0, acc) if mask else acc)
            return tuple(out)

        shared = [convs_of(sl) for sl in shared_cols]
        d_shared, d_shared_n = [None] * ns, [None] * ns
        for j, cols in enumerate(group_cols):
            sl = slice(STRIP * j, STRIP * (j + 1))
            mine = [convs_of(c) for c in cols]
            cj = [c[:, sl] for c in const_refs]
            _, vjp = jax.vjp(fn, *[m[1] for m in mine], *[m[1] for m in shared], *cj)
            grads = vjp(summed(cot_refs, sl, False))
            _, vjp_n = jax.vjp(fn, *[m[2] for m in mine], *[m[2] for m in shared], *cj)
            grads_n = vjp_n(summed(cotn_refs, sl, True))
            for q, c in enumerate(cols):
                conv_back(c, mine[q][0], grads[q], grads_n[q])
            for q in range(ns):
                g, gn = grads[len(cols) + q], grads_n[len(cols) + q]
                d_shared[q] = g if d_shared[q] is None else d_shared[q] + g
                d_shared_n[q] = gn if d_shared_n[q] is None else d_shared_n[q] + gn
            for q in range(nc):
                const_out[q][:, sl] += grads[len(cols) + ns + q]
        for q, c in enumerate(shared_cols):
            conv_back(c, shared[q][0], d_shared[q], d_shared_n[q])

    outs = pl.pallas_call(
        body, grid=(nblk,),
        in_specs=[_row_spec(tm, W), _prev_spec(tm, W), _next_spec(tm, W, T), _full_spec(w.shape)]
        + [_full_spec(c.shape) for c in consts] + [_row_spec(tm, c.shape[1]) for c in flat_cots]
        + [_next_spec(tm, c.shape[1], T) for c in flat_cots],
        out_specs=[_row_spec(tm, W), _full_spec(w.shape)] + [_full_spec(c.shape) for c in consts],
        out_shape=[jax.ShapeDtypeStruct((T, W), bf16), jax.ShapeDtypeStruct(w.shape, f32)]
        + [jax.ShapeDtypeStruct(c.shape, f32) for c in consts], name=name,
        compiler_params=_params("arbitrary"))(x, x, x, w, *consts, *flat_cots, *flat_cots)
    return outs[0], outs[1], list(outs[2:])


def _ffn_strip_fn(cg, cu):
    return cg * _sigmoid(cg) * cu


def _ffn_act_fwd(h, w, tm):
    T, W2 = h.shape
    H = W2 // 2
    taps = w.shape[0]

    def body(h_ref, hp_ref, w_ref, o_ref):
        first = pl.program_id(0) == 0
        for j in range(H // STRIP):
            gs, us = slice(STRIP * j, STRIP * (j + 1)), slice(H + STRIP * j, H + STRIP * (j + 1))
            cg = _strip_conv(h_ref, hp_ref, w_ref, gs, first, taps)[2]
            cu = _strip_conv(h_ref, hp_ref, w_ref, us, first, taps)[2]
            o_ref[:, gs] = _ffn_strip_fn(cg, cu).astype(o_ref.dtype)

    return pl.pallas_call(
        body, grid=(T // tm,), in_specs=[_row_spec(tm, W2), _prev_spec(tm, W2), _full_spec(w.shape)],
        out_specs=_row_spec(tm, H), out_shape=jax.ShapeDtypeStruct((T, H), bf16), name="ffn_act",
        compiler_params=_params("parallel"))(h, h, w)


def _ffn_act_bwd(h, dact, w, tm):
    T, W2 = h.shape
    H = W2 // 2
    taps = w.shape[0]
    nblk = T // tm

    def body(h_ref, hp_ref, hn_ref, d_ref, dn_ref, w_ref, dh_ref, dw_ref):
        i = pl.program_id(0)
        first, last = i == 0, i == nblk - 1

        @pl.when(first)
        def _():
            dw_ref[...] = jnp.zeros_like(dw_ref)

        for j in range(H // STRIP):
            gs, us = slice(STRIP * j, STRIP * (j + 1)), slice(H + STRIP * j, H + STRIP * (j + 1))
            parts = {}
            for name, sl in (('g', gs), ('u', us)):
                cur, down, conv = _strip_conv(h_ref, hp_ref, w_ref, sl, first, taps)
                nxt = hn_ref[:, sl]
                conv_n = None
                for k in range(taps):
                    term = _shift_down(nxt, cur[tm - HALO:], taps - 1 - k) * w_ref[pl.ds(k, 1), sl]
                    conv_n = term if conv_n is None else conv_n + term
                parts[name] = (down, conv, conv_n)
            _, vjp = jax.vjp(_ffn_strip_fn, parts['g'][1], parts['u'][1])
            dcs = vjp(d_ref[:, gs])
            _, vjp_n = jax.vjp(_ffn_strip_fn, parts['g'][2], parts['u'][2])
            dcs_n = vjp_n(jnp.where(last, 0.0, dn_ref[:, gs]))
            for (name, sl), dc, dc_n in zip((('g', gs), ('u', us)), dcs, dcs_n):
                down = parts[name][0]
                dx = None
                for k in range(taps):
                    s_ = taps - 1 - k
                    term = _shift_up(dc, dc_n, s_) * w_ref[pl.ds(k, 1), sl]
                    dx = term if dx is None else dx + term
                    dw_ref[pl.ds(k, 1), sl] += jnp.sum(dc * down[s_], axis=0, keepdims=True)
                dh_ref[:, sl] = dx.astype(dh_ref.dtype)

    return pl.pallas_call(
        body, grid=(nblk,),
        in_specs=[_row_spec(tm, W2), _prev_spec(tm, W2), _next_spec(tm, W2, T), _row_spec(tm, H), _next_spec(tm, H, T),
                  _full_spec(w.shape)],
        out_specs=[_row_spec(tm, W2), _full_spec(w.shape)],
        out_shape=[jax.ShapeDtypeStruct((T, W2), bf16), jax.ShapeDtypeStruct(w.shape, f32)], name="ffn_act_bwd",
        compiler_params=_params("arbitrary"))(h, h, h, dact, dact, w)


N_POS = 4


def _xy_out_shapes(bufs, scatter):
    return [jax.ShapeDtypeStruct((N_POS,) + tuple(b.shape[1:] if scatter else b.shape), b.dtype) for b in bufs]


def _xy_sems(n, scatter):
    sems = [pltpu.SemaphoreType.DMA((3 * n,)), pltpu.SemaphoreType.DMA((3 * n,)), pltpu.SemaphoreType.DMA((n,))]
    return sems if scatter else sems + [pltpu.SemaphoreType.DMA((3 * n,)), pltpu.SemaphoreType.DMA((3 * n,))]


def _xy_copies(in_refs, out_refs, sems, scatter):
    n = len(in_refs)
    send_sems, recv_sems, local_sems = sems[:3]

    def place():
        x, y, c = lax.axis_index("x"), lax.axis_index("y"), lax.axis_index("c")
        return x, y, c, 2 * x + y, [(1 - x, y), (x, 1 - y), (1 - x, 1 - y)]

    def half(ref, a, which):
        rows = in_refs[a].shape[0] // 2
        return ref.at[pl.ds(pl.multiple_of(which * rows, HALO), rows)]

    def ici(a, k, src, dst, peer, c):
        return pltpu.make_async_remote_copy(
            src_ref=src, dst_ref=dst, send_sem=send_sems.at[3 * a + k], recv_sem=recv_sems.at[3 * a + k],
            device_id=(peer[0], peer[1], c), device_id_type=pl.DeviceIdType.MESH)

    def outgoing():
        x, y, c, me, peers = place()
        own = [pltpu.make_async_copy(in_refs[a].at[me] if scatter else in_refs[a], out_refs[a].at[me], local_sems.at[a])
               for a in range(n)]
        if scatter:
            sends = [ici(a, k, in_refs[a].at[2 * p[0] + p[1]], out_refs[a].at[me], p, c)
                     for a in range(n) for k, p in enumerate(peers)]
        else:
            sends = [ici(a, k, half(in_refs[a], a, c), half(out_refs[a].at[me], a, c), p, c)
                     for a in range(n) for k, p in enumerate(peers)]
        return own, sends

    def arrivals():
        x, y, c, me, peers = place()
        if scatter:
            return [ici(a, k, in_refs[a].at[me], out_refs[a].at[2 * p[0] + p[1]], p, c)
                    for a in range(n) for k, p in enumerate(peers)]
        return [ici(a, k, half(in_refs[a], a, c), half(out_refs[a].at[2 * p[0] + p[1]], a, c), p, c)
                for a in range(n) for k, p in enumerate(peers)]

    def to_sibling(mine):
        x, y, c, me, peers = place()
        which = c if mine else 1 - c
        return [pltpu.make_async_remote_copy(
            src_ref=half(out_refs[a].at[2 * p[0] + p[1]], a, which), dst_ref=half(out_refs[a].at[2 * p[0] + p[1]], a, which),
            send_sem=sems[3].at[3 * a + k], recv_sem=sems[4].at[3 * a + k],
            device_id=(x, y, 1 - c), device_id_type=pl.DeviceIdType.MESH) for a in range(n) for k, p in enumerate(peers)]

    def start():
        own, sends = outgoing()
        for cp in own + sends:
            cp.start()

    def finish():
        if scatter:
            for cp in arrivals():
                cp.wait_recv()
        else:
            passed = to_sibling(True)
            for cp, fwd in zip(arrivals(), passed):
                cp.wait_recv()
                fwd.start()
            for cp in to_sibling(False):
                cp.wait_recv()
            for fwd in passed:
                fwd.wait_send()
        own, sends = outgoing()
        for cp in sends:
            cp.wait_send()
        for cp in own:
            cp.wait()

    return start, finish


_NN, _NT, _TN = 'hcs,hsd->hcd', 'hcd,hsd->hcs', 'hcd,hce->hde'


def _lo(spec, a, b):
    return jnp.einsum(spec, a.astype(bf16), b.astype(bf16), preferred_element_type=f32)


@jax.custom_vjp
def _bmm(a, b):
    return _lo(_NN, a, b)


_bmm.defvjp(lambda a, b: (_lo(_NN, a, b), (a, b)), lambda ab, g: (_lo(_NT, g, ab[1]), _lo(_TN, ab[0], g)))


@jax.custom_vjp
def _bmm_nt(a, b):
    return _lo(_NT, a, b)


_bmm_nt.defvjp(lambda a, b: (_lo(_NT, a, b), (a, b)), lambda ab, g: (_lo(_NN, g, ab[1]), _lo(_TN, g, ab[0])))


@jax.custom_vjp
def _bmm_tn(a, b):
    return _lo(_TN, a, b)


_bmm_tn.defvjp(lambda a, b: (_lo(_TN, a, b), (a, b)), lambda ab, g: (_lo(_NT, ab[1], g), _lo(_NN, ab[0], g)))


def _masks(H, C):
    row = lax.broadcasted_iota(jnp.int32, (H, C, C), 1)
    col = lax.broadcasted_iota(jnp.int32, (H, C, C), 2)
    return row, col


def _tri_inv_impl(L):
    H, C, _ = L.shape
    row, col = _masks(H, C)
    eye = (row == col).astype(f32)
    base = 16
    same = (row // base) == (col // base)
    Ld = jnp.where(same, L, 0.0)
    X = -Ld
    inv = eye + X
    for _ in range(3):
        X = _bmm(X, X)
        inv = _bmm(inv, eye + X)
    if C == base:
        return inv
    N = _bmm(inv, L - Ld)
    out = eye - N
    levels = C // base
    P = N
    span = 2
    while span < levels:
        P = _bmm(P, P)
        out = _bmm(out, eye + P)
        span *= 2
    return _bmm(out, inv)


@jax.custom_vjp
def _tri_inv(L):
    return _tri_inv_impl(L)


def _tri_inv_fwd(L):
    T = _tri_inv_impl(L)
    return T, T


def _tri_inv_bwd(T, dT):
    return (-_bmm_nt(_bmm_tn(T, dT), T),)


_tri_inv.defvjp(_tri_inv_fwd, _tri_inv_bwd)


@jax.custom_vjp
def _tri_inv_known(L, T):
    return T


_tri_inv_known.defvjp(lambda L, T: (T, T), lambda T, dT: (_tri_inv_bwd(T, dT)[0], jnp.zeros_like(T)))


def _cumsum_impl(x, reverse):
    C = x.shape[1]
    row = lax.broadcasted_iota(jnp.int32, x.shape, 1)
    s = 1
    while s < C:
        if reverse:
            x = x + jnp.where(row < C - s, pltpu.roll(x, C - s, 1), 0.0)
        else:
            x = x + jnp.where(row >= s, pltpu.roll(x, s, 1), 0.0)
        s *= 2
    return x


@jax.custom_vjp
def _cumsum(x):
    return _cumsum_impl(x, False)


_cumsum.defvjp(lambda x: (_cumsum_impl(x, False), None), lambda _, g: (_cumsum_impl(g, True),))


def _wkv_prep(r, lw, k, v, a, b, inv=None):
    lane = lax.broadcasted_iota(jnp.int32, (r.shape[0], 128), 1)
    low = lane < RWKV_HD

    def heads(t):
        out = []
        for p in range(RWKV_HEADS // 2):
            pair = t[:, 128 * p:128 * (p + 1)]
            out += [jnp.where(low, pair, 0.0), jnp.where(low, 0.0, pair)]
        return jnp.concatenate([t[None] for t in out], axis=0)

    r, lw, k, v, a, b = [heads(t) for t in (r, lw, k, v, a, b)]
    H, C, D = r.shape
    row, col = _masks(H, C)
    incl, strict = row >= col, row > col
    cw = _cumsum(lw)
    cwp = cw - lw
    cwl = jnp.sum(lw, axis=1, keepdims=True)
    en = jnp.exp(-cw)
    at, rt, bt, kt = a * jnp.exp(cwp), r * jnp.exp(cw), b * en, k * en
    Lab = -jnp.where(strict, _bmm_nt(at, bt), 0.0)
    Tm = _tri_inv(Lab) if inv is None else _tri_inv_known(Lab, inv)
    ar = jnp.concatenate([at, rt], axis=1)
    gram = _bmm_nt(ar, jnp.concatenate([bt, kt], axis=1))
    row2 = lax.broadcasted_iota(jnp.int32, (H, 2 * C, 2 * C), 1)
    col2 = lax.broadcasted_iota(jnp.int32, (H, 2 * C, 2 * C), 2) % C
    gram = jnp.where(((row2 < C) & (row2 > col2)) | ((row2 >= C) & (row2 - C >= col2)), gram, 0.0)
    a_bk, r_bk = gram[:, :C], gram[:, C:]
    lak_v = _bmm(a_bk, jnp.concatenate([jnp.zeros_like(v), v], axis=1))
    ed = jnp.exp(cwl - cw)
    zdec = jnp.swapaxes(jnp.broadcast_to(jnp.exp(cwl), (H, D, D)), 1, 2)
    return (ar, Tm, lak_v, r_bk, jnp.concatenate([b * ed, k * ed], axis=1), zdec, v), Tm


def _wkv_step(Z, ar, Tm, lak_v, r_bk, bk_d, zdec, v):
    C = Tm.shape[1]
    ar_z = _bmm(ar, Z)
    uv = jnp.concatenate([_bmm(Tm, ar_z[:, :C] + lak_v), v], axis=1)
    y = ar_z[:, C:] + _bmm(r_bk, uv)
    Z1 = Z * zdec + _bmm_tn(bk_d, uv)
    return jnp.concatenate([y[2 * p] + y[2 * p + 1] for p in range(RWKV_HEADS // 2)], axis=1), Z1


def _split3(x):
    hi = x.astype(bf16)
    mid = (x - hi.astype(f32)).astype(bf16)
    lo = (x - hi.astype(f32) - mid.astype(f32)).astype(bf16)
    return hi, mid, lo


@jax.custom_vjp
def _spread(x, sel):
    return sum(jnp.dot(t, sel, preferred_element_type=f32) for t in _split3(x))


def _spread_bwd(sel, g):
    dn = (((1,), (1,)), ((), ()))
    return sum(lax.dot_general(t, sel, dn, preferred_element_type=f32) for t in _split3(g)), None


_spread.defvjp(lambda x, sel: (_spread(x, sel), sel), _spread_bwd)


def _gdn_prep(q, k, v, gbeta, inv=None):
    heads = lambda t: jnp.concatenate([t[None, :, GDN_HD * h:GDN_HD * (h + 1)] for h in range(GDN_HEADS)], axis=0)
    src = lax.broadcasted_iota(jnp.int32, (W_AB, 2 * GDN_W), 0)
    dst = lax.broadcasted_iota(jnp.int32, (W_AB, 2 * GDN_W), 1) // GDN_HD
    spread = _spread(gbeta, (src == dst).astype(bf16))
    q, k, v, g, beta = heads(q), heads(k), heads(v), heads(spread[:, :GDN_W]), heads(spread[:, GDN_W:])
    H, C, D = q.shape
    row, col = _masks(H, C)
    incl, strict = row >= col, row > col
    gc = _cumsum(g)
    diff = gc - jnp.swapaxes(gc, 1, 2)
    decay = jnp.where(incl, jnp.exp(jnp.where(incl, diff, 0.0)), 0.0)
    gl = jnp.sum(g, axis=1, keepdims=True)
    kb, vb = k * beta, v * beta
    gram = _bmm_nt(jnp.concatenate([kb, q], axis=1), k)
    L = jnp.where(strict, gram[:, :C] * decay, 0.0)
    attn = jnp.where(incl, gram[:, C:] * decay, 0.0)
    egc = jnp.exp(gc)
    Tm = _tri_inv(L) if inv is None else _tri_inv_known(L, inv)
    t_vk = _bmm(Tm, jnp.concatenate([vb, kb * egc], axis=2))
    return (t_vk[:, :, :D], jnp.concatenate([t_vk[:, :, D:], q * egc], axis=1), attn, k * jnp.exp(gl - gc), jnp.exp(gl)), Tm


def _gdn_step(S, u, wq, attn, ke, sdec):
    C = u.shape[1]
    wq_s = _bmm(wq, S)
    v_new = u - wq_s[:, :C]
    o = wq_s[:, C:] + _bmm(attn, v_new)
    S1 = S * sdec + _bmm_tn(ke, v_new)
    return jnp.concatenate([o[h] for h in range(GDN_HEADS)], axis=1), S1


def _scan_fwd(name, fns, ins, C, H, dh, w_out, per_step, side=None):
    prep, step = fns
    T = ins[0].shape[0]
    n_in = len(ins)
    blk = C * per_step
    nblk = T // blk
    n_side = 0 if side is None else len(side[0])

    def body(*refs):
        in_refs, refs = refs[:n_in], refs[n_in:]
        side_in, refs = refs[:n_side], refs[n_side:]
        y_ref, zs_ref, inv_ref, refs = refs[0], refs[1], refs[2], refs[3:]
        side_out, refs = refs[:n_side], refs[n_side:]
        z_scr = refs[0]
        if side is not None:
            start, finish = _xy_copies(side_in, side_out, refs[1:], side[1])
            pl.when(pl.program_id(0) == 0)(start)

        @pl.when(pl.program_id(0) == 0)
        def _():
            z_scr[...] = jnp.zeros_like(z_scr)

        rows = [slice(C * j, C * (j + 1)) for j in range(per_step)]
        prepped = [prep(*[r[rw, :] for r in in_refs]) for rw in rows]
        Z = z_scr[...]
        for j, rw in enumerate(rows):
            zs_ref[j] = Z
            inv_ref[j] = prepped[j][1]
            y, Z = step(Z, *prepped[j][0])
            y_ref[rw, :] = y
        z_scr[...] = Z
        if side is not None:
            pl.when(pl.program_id(0) == nblk - 1)(finish)

    side_bufs = [] if side is None else list(side[0])
    any_spec = pl.BlockSpec(memory_space=pl.ANY)
    return pl.pallas_call(
        body, grid=(nblk,),
        in_specs=[pl.BlockSpec((blk, a.shape[1]), lambda i: (i, 0)) for a in ins] + [any_spec] * n_side,
        out_specs=[pl.BlockSpec((blk, w_out), lambda i: (i, 0)), pl.BlockSpec((per_step, H, dh, dh), lambda i: (i, 0, 0, 0)),
                   pl.BlockSpec((per_step, H, C, C), lambda i: (i, 0, 0, 0))] + [any_spec] * n_side,
        out_shape=[jax.ShapeDtypeStruct((T, w_out), f32), jax.ShapeDtypeStruct((T // C, H, dh, dh), f32),
                   jax.ShapeDtypeStruct((T // C, H, C, C), f32)]
        + (_xy_out_shapes(side_bufs, side[1]) if side is not None else []),
        scratch_shapes=[pltpu.VMEM((H, dh, dh), f32)] + (_xy_sems(n_side, side[1]) if side is not None else []), name=name,
        compiler_params=_params("arbitrary"))(*ins, *side_bufs)


def _scan_bwd(name, fns, ins, dy, zs, invs, C, per_step, side=None):
    prep, step = fns
    T = ins[0].shape[0]
    _, H, dh, _ = zs.shape
    n_in = len(ins)
    blk = C * per_step
    nblk = T // blk
    n_side = 0 if side is None else len(side[0])

    def body(*refs):
        in_refs, dy_ref, zs_ref, inv_ref, refs = refs[:n_in], refs[n_in], refs[n_in + 1], refs[n_in + 2], refs[n_in + 3:]
        side_in, refs = refs[:n_side], refs[n_side:]
        out_refs, refs = refs[:n_in], refs[n_in:]
        side_out, refs = refs[:n_side], refs[n_side:]
        dz_scr = refs[0]
        if side is not None:
            start, finish = _xy_copies(side_in, side_out, refs[1:], side[1])
            pl.when(pl.program_id(0) == 0)(start)

        @pl.when(pl.program_id(0) == 0)
        def _():
            dz_scr[...] = jnp.zeros_like(dz_scr)

        rows = [slice(C * j, C * (j + 1)) for j in range(per_step)]
        prepped = [jax.vjp(lambda *a, j=j: prep(*a, inv=inv_ref[j])[0], *[r[rw, :] for r in in_refs])
                   for j, rw in enumerate(rows)]
        d_prepped = [None] * per_step
        dZ = dz_scr[...]
        for j in reversed(range(per_step)):
            _, pull = jax.vjp(step, zs_ref[j], *prepped[j][0])
            dZ, *d_prepped[j] = pull((dy_ref[rows[j], :], dZ))
        dz_scr[...] = dZ
        for j, rw in enumerate(rows):
            for o_ref, gval in zip(out_refs, prepped[j][1](tuple(d_prepped[j]))):
                o_ref[rw, :] = gval
        if side is not None:
            pl.when(pl.program_id(0) == nblk - 1)(finish)

    side_bufs = [] if side is None else list(side[0])
    any_spec = pl.BlockSpec(memory_space=pl.ANY)
    rev = lambda i: (nblk - 1 - i, 0)
    return pl.pallas_call(
        body, grid=(nblk,),
        in_specs=[pl.BlockSpec((blk, a.shape[1]), rev) for a in ins]
        + [pl.BlockSpec((blk, dy.shape[1]), rev), pl.BlockSpec((per_step, H, dh, dh), lambda i: (nblk - 1 - i, 0, 0, 0)),
           pl.BlockSpec((per_step, H, C, C), lambda i: (nblk - 1 - i, 0, 0, 0))] + [any_spec] * n_side,
        out_specs=[pl.BlockSpec((blk, a.shape[1]), rev) for a in ins] + [any_spec] * n_side,
        out_shape=[jax.ShapeDtypeStruct(a.shape, f32) for a in ins]
        + (_xy_out_shapes(side_bufs, side[1]) if side is not None else []),
        scratch_shapes=[pltpu.VMEM((H, dh, dh), f32)] + (_xy_sems(n_side, side[1]) if side is not None else []), name=name,
        compiler_params=_params("arbitrary"))(*ins, dy, zs, invs, *side_bufs)


def _loss_call(x2, tgt, g, tm):
    T, W = x2.shape

    def body(x_ref, t_ref, g_ref, dx_ref, dg_ref, l_ref):
        i = pl.program_id(0)
        tv = t_ref[...]
        l, vjp = jax.vjp(lambda xv, gv: _loss_rows(xv, tv, gv), x_ref[...], g_ref[...])
        dx, dg = vjp(jnp.ones_like(l))
        dx_ref[...] = dx
        tot = jnp.zeros((1, 128), f32) + jnp.sum(l)

        @pl.when(i == 0)
        def _():
            dg_ref[...] = dg
            l_ref[...] = tot

        @pl.when(i > 0)
        def _():
            dg_ref[...] += dg
            l_ref[...] += tot

    return pl.pallas_call(
        body, grid=(T // tm,),
        in_specs=[_row_spec(tm, W), _row_spec(tm, W), _full_spec(g.shape)],
        out_specs=[_row_spec(tm, W), _full_spec(g.shape), _full_spec((1, 128))],
        out_shape=[jax.ShapeDtypeStruct((T, W), f32), jax.ShapeDtypeStruct(g.shape, f32),
                   jax.ShapeDtypeStruct((1, 128), f32)], name="loss_head",
        compiler_params=_params("arbitrary"))(x2, tgt, g)


def _local_step(x, tgt, W, late=None):
    row = lambda a: a.reshape(1, -1)
    wp = W['w_in_pad']
    w_rwkv, w_qkv, w_z = wp[:, :OFF_QKV], wp[:, OFF_QKV:OFF_Z], wp[:, OFF_Z:OFF_GATES]
    w_gates, w_ab = wp[:, OFF_GATES:OFF_AB], wp[:, OFF_AB:]
    mu = row(W['rwkv_mu'])
    mixw = jnp.concatenate([mu, 1.0 - mu], axis=0)
    zpad = jnp.zeros((64, RWKV_W), f32)
    w2p = jnp.concatenate([W['rwkv_w2'], zpad], axis=0)
    a2p = jnp.concatenate([zpad, W['rwkv_a2']], axis=0)
    rw_consts = [row(W['rwkv_w0']), w2p, row(W['rwkv_a0']), a2p, W['rwkv_g2'], row(W['rwkv_k_k']), row(W['rwkv_k_a'])]
    post_consts = [row(W['rwkv_ln_w']), row(W['rwkv_ln_b']), row(W['rwkv_r_k'])]
    pad4 = lambda a: jnp.pad(row(a), ((0, 0), (0, W_AB - GDN_HEADS)))
    gd_consts = [pad4(W['gdn_a_log']), pad4(W['gdn_dt_bias'])]
    nw_t = jnp.tile(row(W['gdn_norm_w']), (1, GDN_HEADS))
    g1, g2n, gf = row(W['norm1_g']), row(W['norm2_g']), row(W['final_g'])

    (u,) = _pw_fwd("norm1", _rms_fn, [x], [g1], [D_MODEL], 512, out_dtype=bf16)
    p_rwkv = _mm(u, w_rwkv, 'nn', "in_rwkv")
    qkv_raw = _mm(u, w_qkv, 'nn', "in_qkv")
    z = _mm(u, w_z, 'nn', "in_z")
    gates = _mm(u, w_gates, 'nn', "in_gates")
    ab = _mm(u, w_ab, 'nn', "in_ab")

    r, lw, k2, v, a_, b_, g = _pw_fwd("rwkv_prep", _rwkv_prep_fn, [p_rwkv], rw_consts, [RWKV_W] * 7, 256, conv_w=mixw)
    wkv_in = [r, lw, k2, v, a_, b_]
    y, zs_wkv, inv_wkv, *gathered = _scan_fwd("wkv_fwd", (_wkv_prep, _wkv_step), wkv_in, WKV_CHUNK, RWKV_HEADS, 2 * RWKV_HD, RWKV_W, WKV_PER_STEP,
                                     side=None if late is None else (late['shards'][0], False))
    if late is not None:
        W = dict(W, **late['assemble'](0, gathered))
    (ya_in,) = _pw_fwd("rwkv_post", _rwkv_post_fn, [y, r, k2, v, g], post_consts, [RWKV_W], 256, out_dtype=bf16, strip=128)
    ya = _mm(ya_in, W['rwkv_proj'], 'nn', "rwkv_proj")

    lanes = lambda off: slice(off, off + STRIP)
    gd_groups = [[lanes(GDN_HD * h), lanes(GDN_W + GDN_HD * h), lanes(2 * GDN_W + GDN_HD * h)] for h in range(GDN_HEADS)]
    gq, gk, gv = _group_fwd("gdn_prep", _gdn_prep_fn, qkv_raw, W['gdn_conv_w'], [], gd_groups, [], 3, 256)
    (gbeta,) = _pw_fwd("gdn_gate", _gdn_gate_fn, [ab], gd_consts, [W_AB], 512)
    gdn_in = [gq, gk, gv, gbeta]
    o, zs_gdn, inv_gdn, *gathered = _scan_fwd("gdn_fwd", (_gdn_prep, _gdn_step), gdn_in, GDN_CHUNK, GDN_HEADS, GDN_HD, GDN_W, GDN_PER_STEP,
                                     side=None if late is None else (late['shards'][1], False))
    if late is not None:
        W = dict(W, **late['assemble'](1, gathered))
    (yb_in,) = _pw_fwd("gdn_post", _gdn_post_fn, [o, z], [nw_t], [GDN_W], 256, out_dtype=bf16, strip=128)
    yb = _mm(yb_in, W['gdn_proj'], 'nn', "gdn_proj")

    ga, gb = _cols(gates, D_MODEL, 0), _cols(gates, D_MODEL, 1)
    (mixed,) = _pw_fwd("mix", _mix_fn, [ga, gb, ya, yb], [], [D_MODEL], 256, out_dtype=bf16, strip=256)
    x1 = _mm(mixed, W['w_out'], 'nn', "w_out", add=x)
    (u2,) = _pw_fwd("norm2", _rms_fn, [x1], [g2n], [D_MODEL], 512, out_dtype=bf16)
    h = _mm(u2, W['ffn_up'], 'nn', "ffn_up")
    act = _ffn_act_fwd(h, W['ffn_conv_w'], 256)
    x2 = _mm(act, W['ffn_down'], 'nn', "ffn_down", add=x1)

    G = {}
    slab_out = None if late is None else N_POS
    dx2, dgf, loss = _loss_call(x2, tgt, gf, 512)
    G['final_g'] = dgf
    dact = _mm(dx2, W['ffn_down'], 'nt', "d_act")
    G['ffn_down'] = _mm(act, dx2, 'tn', "g_ffn_down", out_dtype=bf16)
    dh, G['ffn_conv_w'] = _ffn_act_bwd(h, dact, W['ffn_conv_w'], 128)
    du2 = _mm(dh, W['ffn_up'], 'nt', "d_u2")
    G['ffn_up'] = _mm(u2, dh, 'tn', "g_ffn_up", out_dtype=bf16, col_slabs=slab_out)
    (dx1,), (G['norm2_g'],) = _pw_bwd("norm2_bwd", _rms_fn, [x1], [g2n], [(du2,)], 512, add_to_first=dx2)
    dmixed = _mm(dx1, W['w_out'], 'nt', "d_mixed")
    G['w_out'] = _mm(mixed, dx1, 'tn', "g_w_out", out_dtype=bf16)
    (dga, dgb, dya, dyb), _ = _pw_bwd("mix_bwd", _mix_fn, [ga, gb, ya, yb], [], [(dmixed,)], 256, row_dtypes=[bf16] * 4,
                                      strip=256)
    dya_in = _mm(dya, W['rwkv_proj'], 'nt', "d_ya_in")
    G['rwkv_proj'] = _mm(ya_in, dya, 'tn', "g_rwkv_proj", out_dtype=bf16, col_slabs=slab_out)
    dyb_in = _mm(dyb, W['gdn_proj'], 'nt', "d_yb_in")
    G['gdn_proj'] = _mm(yb_in, dyb, 'tn', "g_gdn_proj", out_dtype=bf16, col_slabs=slab_out)

    (do, dz), (dnw_t,) = _pw_bwd("gdn_post_bwd", _gdn_post_fn, [o, z], [nw_t], [(dyb_in,)], 256, row_dtypes=[f32, bf16],
                                 strip=128)
    G['gdn_norm_w'] = dnw_t.reshape(GDN_HEADS, GDN_HD).sum(axis=0)
    dgq, dgk, dgv, dgbeta, *arrived_b = _scan_bwd("gdn_bwd", (_gdn_prep, _gdn_step), gdn_in, do, zs_gdn, inv_gdn, GDN_CHUNK,
                                                  GDN_PER_STEP, side=None if late is None else (late['slabs'](G, 1), True))
    dqkv_raw, G['gdn_conv_w'], _ = _group_bwd("gdn_prep_bwd", _gdn_prep_fn, qkv_raw, W['gdn_conv_w'], [], gd_groups, [],
                                              [(dgq,), (dgk,), (dgv,)], 128)
    (dab,), (dal_p, ddt_p) = _pw_bwd("gdn_gate_bwd", _gdn_gate_fn, [ab], gd_consts, [(dgbeta,)], 512, row_dtypes=[bf16])
    G['gdn_a_log'], G['gdn_dt_bias'] = dal_p[0, :GDN_HEADS], ddt_p[0, :GDN_HEADS]

    (dy, dr1, dk21, dv1, dg_), (G['rwkv_ln_w'], G['rwkv_ln_b'], G['rwkv_r_k']) = _pw_bwd(
        "rwkv_post_bwd", _rwkv_post_fn, [y, r, k2, v, g], post_consts, [(dya_in,)], 256, strip=128)
    dr2, dlw, dk22, dv2, da_, db_, *arrived_a = _scan_bwd(
        "wkv_bwd", (_wkv_prep, _wkv_step), wkv_in, dy, zs_wkv, inv_wkv, WKV_CHUNK, WKV_PER_STEP,
        side=None if late is None else (late['slabs'](G, 0), True))
    G['_arrived'] = (arrived_a, arrived_b)
    (dp_rwkv,), dmixw, rw_grads = _pw_conv_bwd(
        "rwkv_prep_bwd", _rwkv_prep_fn, [p_rwkv], rw_consts,
        [(dr1, dr2), (dlw,), (dk21, dk22), (dv1, dv2), (da_,), (db_,), (dg_,)], mixw, 256, row_dtypes=[bf16])
    G['rwkv_w0'], dw2p, G['rwkv_a0'], da2p, G['rwkv_g2'], G['rwkv_k_k'], G['rwkv_k_a'] = rw_grads
    G['rwkv_w2'], G['rwkv_a2'] = dw2p[:64], da2p[64:]
    G['rwkv_mu'] = dmixw[0] - dmixw[1]

    dp = jnp.concatenate([dp_rwkv, dqkv_raw, dz, dga, dgb, dab], axis=1)
    G['w_in_pad'] = _mm(u, dp, 'tn', "g_w_in", out_dtype=bf16)
    if late is None:
        du = _mm(dp, wp, 'nt', "d_u")
    else:
        du, *G['_arrived_w_in'] = _mm(dp, wp, 'nt', "d_u", side=(late['w_in_slabs'](G), True))
    (dx,), (G['norm1_g'],) = _pw_bwd("norm1_bwd", _rms_fn, [x], [g1], [(du,)], 512, add_to_first=dx1)
    return loss, dx, G


IN_WIDTH = OFF_AB + 8
PAD_ORDER = ((0, OFF_GATES), (OFF_GATES + 8, IN_WIDTH), (OFF_GATES, OFF_GATES + 8))


def _pad_w_in(w):
    return jnp.concatenate([w[:, a:b] for a, b in PAD_ORDER] + [jnp.zeros((w.shape[0], W_AB - 8), w.dtype)], axis=1)


def _pad_w_in_shards(shards):
    width = shards[0].shape[1]
    parts = []
    for a, b in PAD_ORDER:
        for j, sh in enumerate(shards):
            lo, hi = max(a, j * width), min(b, (j + 1) * width)
            if lo < hi:
                parts.append(sh[:, lo - j * width:hi - j * width])
    return jnp.concatenate(parts + [jnp.zeros((shards[0].shape[0], W_AB - 8), shards[0].dtype)], axis=1)


def _unpad_cols(wp, lo, hi):
    parts, off = [], 0
    for a, b in PAD_ORDER:
        l, h = max(a, lo), min(b, hi)
        if l < h:
            parts.append((l, wp[:, off + l - a:off + h - a]))
        off += b - a
    parts.sort(key=lambda t: t[0])
    return parts[0][1] if len(parts) == 1 else jnp.concatenate([p for _, p in parts], axis=1)


def _unpad_w_in(wp):
    return _unpad_cols(wp, 0, IN_WIDTH)


BIG = ('w_in', 'rwkv_proj', 'gdn_proj', 'w_out', 'ffn_up', 'ffn_down')
SMALL_SHARDED = ('rwkv_w2', 'rwkv_a2', 'rwkv_g2', 'gdn_conv_w', 'ffn_conv_w')


def _rows128(shape):
    n = 1
    for d in shape:
        n *= d
    return -(-n // LANES)


def _pack128(arrays):
    parts = []
    for a in arrays:
        flat = a.reshape(-1)
        rows = _rows128(a.shape)
        parts.append(jnp.pad(flat, (0, rows * LANES - flat.shape[0])).reshape(rows, LANES))
    buf = jnp.concatenate(parts, axis=0)
    return jnp.pad(buf, ((0, -buf.shape[0] % HALO), (0, 0)))


def _unpack128(buf, shapes):
    out, off = [], 0
    for s in shapes:
        rows, n = _rows128(s), 1
        for d in s:
            n *= d
        out.append(buf[off:off + rows].reshape(-1)[:n].reshape(s))
        off += rows
    return out


def _param_tile(r, c):
    best = None
    for d in range(2 * HALO, r + 1, 2 * HALO):
        if r % d == 0 and d * c * 4 <= TILE_BYTES:
            best = d
    if best is not None or r * c * 4 <= TILE_BYTES:
        return (best if best is not None else r), c
    return r, 128


def _xy_exchange(name, bufs, scatter):
    n = len(bufs)

    def body(*refs):
        start, finish = _xy_copies(refs[:n], refs[n:2 * n], refs[2 * n:], scatter)
        start()
        finish()

    return pl.pallas_call(
        body, in_specs=[pl.BlockSpec(memory_space=pl.ANY)] * n, out_specs=[pl.BlockSpec(memory_space=pl.ANY)] * n,
        out_shape=_xy_out_shapes(bufs, scatter), scratch_shapes=_xy_sems(n, scatter), name=name)(*bufs)


def _sibling_exchange(name, bufs):
    n = len(bufs)

    def body(*refs):
        in_refs, out_refs, send_sems, recv_sems = refs[:n], refs[n:2 * n], refs[2 * n], refs[2 * n + 1]
        x, y, c = lax.axis_index("x"), lax.axis_index("y"), lax.axis_index("c")
        copies = [pltpu.make_async_remote_copy(
            src_ref=in_refs[a], dst_ref=out_refs[a], send_sem=send_sems.at[a], recv_sem=recv_sems.at[a],
            device_id=(x, y, 1 - c), device_id_type=pl.DeviceIdType.MESH) for a in range(n)]
        for cp in copies:
            cp.start()
        for cp in copies:
            cp.wait()

    return pl.pallas_call(
        body, in_specs=[pl.BlockSpec(memory_space=pl.ANY)] * n, out_specs=[pl.BlockSpec(memory_space=pl.ANY)] * n,
        out_shape=[jax.ShapeDtypeStruct(b.shape, b.dtype) for b in bufs],
        scratch_shapes=[pltpu.SemaphoreType.DMA((n,)), pltpu.SemaphoreType.DMA((n,))], name=name)(*bufs)


def _sum_slots(name, buf):
    _, R, L = buf.shape
    tr, tc = _param_tile(R, L)

    def body(b_ref, o_ref):
        part = lambda s: b_ref[s].astype(f32)
        o_ref[...] = ((part(0) + part(1)) + part(2)) + part(3)

    return pl.pallas_call(
        body, grid=(R // tr, L // tc),
        in_specs=[pl.BlockSpec((N_POS, tr, tc), lambda i, j: (0, i, j))],
        out_specs=pl.BlockSpec((tr, tc), lambda i, j: (i, j)),
        out_shape=jax.ShapeDtypeStruct((R, L), f32), name=name,
        compiler_params=_params("parallel", "parallel"))(buf)


def _adamw(name, w, ga, gb, m, v):
    R, L = w.shape
    tr, tc = _param_tile(R, L)
    c1 = 1.0 / (1.0 - ADAM_B1 ** ADAM_STEP)
    c2 = 1.0 / (1.0 - ADAM_B2 ** ADAM_STEP)

    def body(w_ref, ga_ref, gb_ref, m_ref, v_ref, g_out, d_out, m_out, v_out):
        g = ga_ref[...] + gb_ref[...]
        m_new = ADAM_B1 * m_ref[...] + (1.0 - ADAM_B1) * g
        v_new = ADAM_B2 * v_ref[...] + (1.0 - ADAM_B2) * (g * g)
        g_out[...] = g
        m_out[...] = m_new
        v_out[...] = v_new
        d_out[...] = -ADAM_LR * ((m_new * c1) / (jnp.sqrt(v_new * c2) + ADAM_EPS) + ADAM_WD * w_ref[...])

    spec = pl.BlockSpec((tr, tc), lambda i, j: (i, j))
    return pl.pallas_call(
        body, grid=(R // tr, L // tc), in_specs=[spec] * 5, out_specs=[spec] * 4,
        out_shape=[jax.ShapeDtypeStruct((R, L), f32)] * 4, name=name,
        compiler_params=_params("parallel", "parallel"))(w, ga, gb, m, v)


def _step(x, loss_target, P, M, V):
    shapes = {n: tuple(P[n].shape) for n in WEIGHTS}
    sh_shapes = [shapes[n] for n in SMALL_SHARDED]
    packed = SMALL_SHARDED + SMALL

    def whole(n, g):
        return g.reshape(-1, g.shape[2]) if n in ROW_SHARDED else jnp.concatenate([g[j] for j in range(N_POS)], axis=1)

    def slabs(G, n, dtype=f32):
        r, c = shapes[n]
        full = G[n].astype(dtype)
        if full.ndim == 3:
            return full
        return full.reshape(N_POS, r, c) if n in ROW_SHARDED else full.reshape(r, N_POS, c).transpose(1, 0, 2)

    g_w_in, g_small = _xy_exchange("gather_w_in", [P['w_in'].astype(bf16), _pack128([P[n] for n in SMALL_SHARDED])],
                                   scatter=False)
    W = {n: P[n] for n in SMALL}
    W['w_in_pad'] = _pad_w_in_shards([g_w_in[j] for j in range(N_POS)])
    per_pos = [_unpack128(g_small[j], sh_shapes) for j in range(N_POS)]
    for q, n in enumerate(SMALL_SHARDED):
        W[n] = jnp.concatenate([per_pos[j][q] for j in range(N_POS)], axis=1)
    groups = (('rwkv_proj', 'gdn_proj', 'ffn_up'), ('w_out', 'ffn_down'))
    late = dict(shards=[[P[n].astype(bf16) for n in grp] for grp in groups],
                assemble=lambda q, gathered: {n: whole(n, g) for n, g in zip(groups[q], gathered)},
                slabs=lambda G, q: [slabs(G, n, bf16) for n in groups[q]],
                w_in_slabs=lambda G: [jnp.stack([_unpad_cols(G['w_in_pad'], j * shapes['w_in'][1], (j + 1) * shapes['w_in'][1])
                                                 for j in range(N_POS)])])

    loss_rows, dx, G = _local_step(x, loss_target, W, late)
    arrived = {n: a for grp, got in zip(groups, G.pop('_arrived')) for n, a in zip(grp, got)}
    (arrived_w_in,) = G.pop('_arrived_w_in')
    G.pop('w_in_pad')

    small_slabs = jnp.stack([_pack128([slabs(G, n)[j] for n in SMALL_SHARDED] + [G[n] for n in SMALL]) for j in range(N_POS)])
    (arrived_small,) = _xy_exchange("scatter_small", [small_slabs], scatter=True)
    contributions = [arrived_w_in] + [arrived[n] for n in BIG[1:]] + [arrived_small]
    tags = list(BIG) + ['small']
    plane = [_sum_slots("sum_" + t, cbuf) for t, cbuf in zip(tags, contributions)]
    sibling = _sibling_exchange("sibling_grads", plane)

    out = {}
    names4 = ('grad', 'delta', 'new_m', 'new_v')
    for q, n in enumerate(BIG):
        for tag, t in zip(names4, _adamw("adamw_" + n, P[n], plane[q], sibling[q], M[n], V[n])):
            out[tag + '_' + n] = t
    small_out = _adamw("adamw_small", _pack128([P[n] for n in packed]), plane[-1], sibling[-1],
                       _pack128([M[n] for n in packed]), _pack128([V[n] for n in packed]))
    for tag, buf in zip(names4, small_out):
        for n, t in zip(packed, _unpack128(buf, [shapes[n] for n in packed])):
            out[tag + '_' + n] = t
    loss = lax.psum(loss_rows[0, 0], ("x", "y", "c"))
    return loss, dx, out


def kernel(x, norm1_g, w_in, rwkv_mu, rwkv_w0, rwkv_w2, rwkv_a0, rwkv_a2, rwkv_g2, rwkv_k_k, rwkv_k_a, rwkv_r_k, rwkv_ln_w, rwkv_ln_b, rwkv_proj, gdn_conv_w, gdn_a_log, gdn_dt_bias, gdn_norm_w, gdn_proj, w_out, norm2_g, ffn_up, ffn_conv_w, ffn_down, final_g, loss_target, m_norm1_g, m_w_in, m_rwkv_mu, m_rwkv_w0, m_rwkv_w2, m_rwkv_a0, m_rwkv_a2, m_rwkv_g2, m_rwkv_k_k, m_rwkv_k_a, m_rwkv_r_k, m_rwkv_ln_w, m_rwkv_ln_b, m_rwkv_proj, m_gdn_conv_w, m_gdn_a_log, m_gdn_dt_bias, m_gdn_norm_w, m_gdn_proj, m_w_out, m_norm2_g, m_ffn_up, m_ffn_conv_w, m_ffn_down, m_final_g, v_norm1_g, v_w_in, v_rwkv_mu, v_rwkv_w0, v_rwkv_w2, v_rwkv_a0, v_rwkv_a2, v_rwkv_g2, v_rwkv_k_k, v_rwkv_k_a, v_rwkv_r_k, v_rwkv_ln_w, v_rwkv_ln_b, v_rwkv_proj, v_gdn_conv_w, v_gdn_a_log, v_gdn_dt_bias, v_gdn_norm_w, v_gdn_proj, v_w_out, v_norm2_g, v_ffn_up, v_ffn_conv_w, v_ffn_down, v_final_g):
    weights = (norm1_g, w_in, rwkv_mu, rwkv_w0, rwkv_w2, rwkv_a0, rwkv_a2, rwkv_g2, rwkv_k_k, rwkv_k_a, rwkv_r_k, rwkv_ln_w,
               rwkv_ln_b, rwkv_proj, gdn_conv_w, gdn_a_log, gdn_dt_bias, gdn_norm_w, gdn_proj, w_out, norm2_g, ffn_up,
               ffn_conv_w, ffn_down, final_g)
    m_in = (m_norm1_g, m_w_in, m_rwkv_mu, m_rwkv_w0, m_rwkv_w2, m_rwkv_a0, m_rwkv_a2, m_rwkv_g2, m_rwkv_k_k, m_rwkv_k_a,
            m_rwkv_r_k, m_rwkv_ln_w, m_rwkv_ln_b, m_rwkv_proj, m_gdn_conv_w, m_gdn_a_log, m_gdn_dt_bias, m_gdn_norm_w,
            m_gdn_proj, m_w_out, m_norm2_g, m_ffn_up, m_ffn_conv_w, m_ffn_down, m_final_g)
    v_in = (v_norm1_g, v_w_in, v_rwkv_mu, v_rwkv_w0, v_rwkv_w2, v_rwkv_a0, v_rwkv_a2, v_rwkv_g2, v_rwkv_k_k, v_rwkv_k_a,
            v_rwkv_r_k, v_rwkv_ln_w, v_rwkv_ln_b, v_rwkv_proj, v_gdn_conv_w, v_gdn_a_log, v_gdn_dt_bias, v_gdn_norm_w,
            v_gdn_proj, v_w_out, v_norm2_g, v_ffn_up, v_ffn_conv_w, v_ffn_down, v_final_g)
    drop = lambda n, a: a if n == 'final_g' else a[0]
    P = {n: drop(n, a) for n, a in zip(WEIGHTS, weights)}
    M = {n: drop(n, a) for n, a in zip(WEIGHTS, m_in)}
    V = {n: drop(n, a) for n, a in zip(WEIGHTS, v_in)}
    loss, dx, out = _step(x[0], loss_target[0], P, M, V)
    lift = lambda n, a: a if n == 'final_g' else a[None]
    res = [loss, dx[None]]
    for tag in ('grad', 'delta', 'new_m', 'new_v'):
        res += [lift(n, out[tag + '_' + n]) for n in WEIGHTS]
    return tuple(res)
```

```python
import functools

import jax
import jax.numpy as jnp
from jax import lax
from jax.experimental import pallas as pl
from jax.experimental.pallas import tpu as pltpu

f32 = jnp.float32
bf16 = jnp.bfloat16

D_MODEL = 1024
RWKV_HEADS, RWKV_HD, RWKV_W = 8, 64, 512
GDN_HEADS, GDN_HD, GDN_W = 4, 128, 512
NORM_EPS, L2_EPS, GN_EPS = 1e-6, 1e-6, 64e-5
W_AB = 256
OFF_QKV, OFF_Z, OFF_GATES, OFF_AB = 1792, 3328, 3840, 5888
W_IN_PAD = OFF_AB + W_AB
WKV_CHUNK, WKV_PER_STEP = 64, 4
GDN_CHUNK, GDN_PER_STEP = 128, 4
HALO = 8
LANES = 128
TILE_BYTES = 1 << 20
VMEM_LIMIT = 56 * 1024 * 1024

ADAM_LR, ADAM_B1, ADAM_B2, ADAM_EPS, ADAM_WD, ADAM_STEP = 0.001, 0.9, 0.999, 1e-08, 0.01, 10

ROW_SHARDED = ('w_out', 'ffn_down')
SMALL = ('norm1_g', 'rwkv_mu', 'rwkv_w0', 'rwkv_a0', 'rwkv_k_k', 'rwkv_k_a', 'rwkv_r_k', 'rwkv_ln_w', 'rwkv_ln_b',
         'gdn_a_log', 'gdn_dt_bias', 'gdn_norm_w', 'norm2_g', 'final_g')
WEIGHTS = ('norm1_g', 'w_in', 'rwkv_mu', 'rwkv_w0', 'rwkv_w2', 'rwkv_a0', 'rwkv_a2', 'rwkv_g2', 'rwkv_k_k', 'rwkv_k_a',
           'rwkv_r_k', 'rwkv_ln_w', 'rwkv_ln_b', 'rwkv_proj', 'gdn_conv_w', 'gdn_a_log', 'gdn_dt_bias', 'gdn_norm_w',
           'gdn_proj', 'w_out', 'norm2_g', 'ffn_up', 'ffn_conv_w', 'ffn_down', 'final_g')


def _params(*sem):
    return pltpu.CompilerParams(dimension_semantics=sem, vmem_limit_bytes=VMEM_LIMIT)


def _tile(n, limit):
    if n <= limit:
        return n
    best = None
    for d in range(128, limit + 1, 128):
        if n % d == 0:
            best = d
    if best is None:
        raise ValueError(f"no tile for {n} under {limit}")
    return best


MM_BLOCK_BYTES = 6 << 20
MM_MAX_COLS = 1536


def _mm(a, b, mode, name, add=None, out_dtype=f32, side=None, col_slabs=None):
    if mode == 'nn':
        (M, K), N = a.shape, b.shape[1]
    elif mode == 'nt':
        (M, K), N = a.shape, b.shape[0]
    else:
        (K, M), N = a.shape, b.shape[1]
    tm = _tile(M, 1408)
    tk = _tile(K, min(2816, MM_BLOCK_BYTES // (tm * a.dtype.itemsize)))
    tn = _tile(N, max(128, min(MM_BLOCK_BYTES // (tk * b.dtype.itemsize), MM_BLOCK_BYTES // (tm * 4), MM_MAX_COLS) // 128 * 128))
    if col_slabs is not None:
        tn = N // col_slabs
    nk = K // tk
    grid = (M // tm, N // tn, nk)
    dn = {'nn': (((1,), (0,)), ((), ())), 'nt': (((1,), (1,)), ((), ())), 'tn': (((0,), (0,)), ((), ()))}[mode]
    n_add = 0 if add is None else 1
    n_side = 0 if side is None else len(side[0])

    def body(a_ref, b_ref, *rest):
        add_ref = rest[0] if add is not None else None
        side_in, rest = rest[n_add:n_add + n_side], rest[n_add + n_side:]
        o_ref, side_out, rest = rest[0], rest[1:1 + n_side], rest[1 + n_side:]
        acc_ref, rest = (rest[0], rest[1:]) if nk > 1 else (None, rest)
        ids = [pl.program_id(d) for d in range(3)]
        if side is not None:
            start, finish = _xy_copies(side_in, side_out, rest, side[1])
            pl.when((ids[0] == 0) & (ids[1] == 0) & (ids[2] == 0))(start)
        acc = lax.dot_general(a_ref[...].astype(bf16), b_ref[...].astype(bf16), dn, preferred_element_type=f32)
        if nk == 1:
            o_ref[...] = (acc + add_ref[...] if add is not None else acc).astype(out_dtype)
        else:
            k = ids[2]

            @pl.when(k == 0)
            def _():
                acc_ref[...] = acc + add_ref[...] if add is not None else acc

            @pl.when(k > 0)
            def _():
                acc_ref[...] += acc

            @pl.when(k == nk - 1)
            def _():
                o_ref[...] = acc_ref[...].astype(out_dtype)
        if side is not None:
            pl.when((ids[0] == grid[0] - 1) & (ids[1] == grid[1] - 1) & (ids[2] == nk - 1))(finish)

    a_spec = (pl.BlockSpec((tk, tm), lambda i, j, k: (k, i)) if mode == 'tn'
              else pl.BlockSpec((tm, tk), lambda i, j, k: (i, k)))
    b_spec = (pl.BlockSpec((tn, tk), lambda i, j, k: (j, k)) if mode == 'nt'
              else pl.BlockSpec((tk, tn), lambda i, j, k: (k, j)))
    o_spec = pl.BlockSpec((tm, tn), lambda i, j, k: (i, j))
    o_shape = jax.ShapeDtypeStruct((M, N), out_dtype)
    if col_slabs is not None:
        o_spec = pl.BlockSpec((None, tm, tn), lambda i, j, k: (j, i, 0))
        o_shape = jax.ShapeDtypeStruct((col_slabs, M, tn), out_dtype)
    any_spec = pl.BlockSpec(memory_space=pl.ANY)
    side_bufs = [] if side is None else list(side[0])
    ins, specs = [a, b], [a_spec, b_spec]
    if add is not None:
        ins.append(add)
        specs.append(o_spec)
    outs = pl.pallas_call(
        body, grid=grid, in_specs=specs + [any_spec] * n_side, out_specs=[o_spec] + [any_spec] * n_side,
        out_shape=[o_shape] + (_xy_out_shapes(side_bufs, side[1]) if side is not None else []),
        scratch_shapes=([pltpu.VMEM((tm, tn), f32)] if nk > 1 else []) + (_xy_sems(n_side, side[1]) if side is not None else []),
        name=name,
        compiler_params=_params(*(("arbitrary",) * 3 if side is not None else ("parallel", "parallel", "arbitrary"))))(
            *ins, *side_bufs)
    return list(outs) if side is not None else outs[0]


def _shift_down(cur, prev, s):
    if s == 0:
        return cur
    ext = jnp.concatenate([prev, cur], axis=0)
    return pltpu.roll(ext, s, 0)[HALO:]


def _shift_up(cur, nxt, s):
    if s == 0:
        return cur
    ext = jnp.concatenate([cur, nxt], axis=0)
    return pltpu.roll(ext, ext.shape[0] - s, 0)[:cur.shape[0]]


def _conv_apply(cur, prev, w_ref, shifted=None):
    taps = w_ref.shape[0]
    out = None
    for i in range(taps):
        s = taps - 1 - i
        term = (shifted[s] if shifted is not None else _shift_down(cur, prev, s)) * w_ref[pl.ds(i, 1), :]
        out = term if out is None else out + term
    return out


def _row_spec(tm, w, col=0):
    return pl.BlockSpec((tm, w), lambda i: (i, col))


def _cols(a, width, col):
    return (a, width, col)


def _row_of(r):
    return r if isinstance(r, tuple) else (r, r.shape[1], 0)


def _prev_spec(tm, w):
    return pl.BlockSpec((HALO, w), lambda i: (jnp.maximum(i * (tm // HALO) - 1, 0), 0))


def _next_spec(tm, w, T):
    return pl.BlockSpec((HALO, w), lambda i: (jnp.minimum((i + 1) * (tm // HALO), T // HALO - 1), 0))


def _full_spec(shape):
    return pl.BlockSpec(shape, lambda i: (0,) * len(shape))


def _pw_fwd(name, fn, rows, consts, out_widths, tm, conv_w=None, out_dtype=f32, strip=None):
    T = _row_of(rows[0])[0].shape[0]
    nr, nc = len(rows), len(consts)

    def body(*refs):
        i = pl.program_id(0)
        if strip is not None:
            for j in range(out_widths[0] // strip):
                sl = slice(strip * j, strip * (j + 1))
                outs = fn(*[r[:, sl] for r in refs[:nr + nc]])
                for o_ref, o in zip(refs[nr + nc:], outs):
                    o_ref[:, sl] = o.astype(out_dtype)
            return
        vals = [r[...] for r in refs[:nr]]
        p = nr
        if conv_w is not None:
            prev = jnp.where(i > 0, refs[p][...], 0.0)
            vals[0] = _conv_apply(vals[0], prev, refs[p + 1])
            p += 2
        cvals = [r[...] for r in refs[p:p + nc]]
        outs = fn(*vals, *cvals)
        for o_ref, o in zip(refs[p + nc:], outs):
            o_ref[...] = o.astype(out_dtype)

    ins = [_row_of(r)[0] for r in rows]
    specs = [_row_spec(tm, *_row_of(r)[1:]) for r in rows]
    if conv_w is not None:
        ins += [rows[0], conv_w]
        specs += [_prev_spec(tm, rows[0].shape[1]), _full_spec(conv_w.shape)]
    ins += list(consts)
    specs += [_full_spec(c.shape) for c in consts]
    outs = pl.pallas_call(
        body, grid=(T // tm,), in_specs=specs,
        out_specs=[_row_spec(tm, w) for w in out_widths],
        out_shape=[jax.ShapeDtypeStruct((T, w), out_dtype) for w in out_widths], name=name,
        compiler_params=_params("parallel"))(*ins)
    return outs


def _pw_bwd(name, fn, rows, consts, cots, tm, add_to_first=None, row_dtypes=None, strip=None):
    rows = [_row_of(r) for r in rows]
    T = rows[0][0].shape[0]
    nr, nc = len(rows), len(consts)
    flat_cots = [c for grp in cots for c in grp]
    row_dtypes = row_dtypes or [f32] * nr
    n_extra = 0 if add_to_first is None else 1
    width = rows[0][1]

    def body(*refs):
        i = pl.program_id(0)
        in_refs, cot_refs = refs[:nr + nc], refs[nr + nc:nr + nc + len(flat_cots)]
        extra_ref = refs[nr + nc + len(flat_cots)] if add_to_first is not None else None
        row_out = refs[nr + nc + len(flat_cots) + n_extra:][:nr]
        const_out = refs[nr + nc + len(flat_cots) + n_extra + nr:]

        @pl.when(i == 0)
        def _():
            for q in range(nc):
                const_out[q][...] = jnp.zeros_like(const_out[q])

        def part(sl):
            cot_vals, p = [], 0
            for grp in cots:
                acc = cot_refs[p][:, sl]
                for q in range(1, len(grp)):
                    acc = acc + cot_refs[p + q][:, sl]
                p += len(grp)
                cot_vals.append(acc)
            _, vjp = jax.vjp(fn, *[r[:, sl] for r in in_refs])
            grads = vjp(tuple(cot_vals))
            for q in range(nr):
                g = grads[q]
                if q == 0 and extra_ref is not None:
                    g = g + extra_ref[:, sl]
                row_out[q][:, sl] = g.astype(row_dtypes[q])
            for q in range(nc):
                const_out[q][:, sl] += grads[nr + q]

        if strip is None:
            part(slice(None))
        else:
            for j in range(width // strip):
                part(slice(strip * j, strip * (j + 1)))

    ins = [r[0] for r in rows] + list(consts) + flat_cots
    specs = ([_row_spec(tm, r[1], r[2]) for r in rows] + [_full_spec(c.shape) for c in consts]
             + [_row_spec(tm, c.shape[1]) for c in flat_cots])
    if add_to_first is not None:
        ins.append(add_to_first)
        specs.append(_row_spec(tm, add_to_first.shape[1]))
    out_shapes = ([jax.ShapeDtypeStruct((T, r[1]), d) for r, d in zip(rows, row_dtypes)]
                  + [jax.ShapeDtypeStruct(c.shape, f32) for c in consts])
    out_specs = [_row_spec(tm, r[1]) for r in rows] + [_full_spec(c.shape) for c in consts]
    outs = pl.pallas_call(
        body, grid=(T // tm,), in_specs=specs, out_specs=out_specs, out_shape=out_shapes, name=name,
        compiler_params=_params("arbitrary"))(*ins)
    return list(outs[:nr]), list(outs[nr:])


def _pw_conv_bwd(name, fn, rows, consts, cots, conv_w, tm, row_dtypes=None):
    T, W0 = rows[0].shape
    nr, nc = len(rows), len(consts)
    taps = conv_w.shape[0]
    nblk = T // tm
    flat_cots = [c for grp in cots for c in grp]
    row_dtypes = row_dtypes or [f32] * nr

    def body(*refs):
        i = pl.program_id(0)
        p = 0
        cur = [r[...] for r in refs[p:p + nr]]; p += nr
        nxt = [r[...] for r in refs[p:p + nr]]; p += nr
        prev = jnp.where(i > 0, refs[p][...], 0.0); p += 1
        w_ref = refs[p]; p += 1
        cvals = [r[...] for r in refs[p:p + nc]]; p += nc

        def summed(p0):
            out, q = [], p0
            for grp in cots:
                acc = refs[q][...]
                for t in range(1, len(grp)):
                    acc = acc + refs[q + t][...]
                q += len(grp)
                out.append(acc)
            return out, q

        cot_cur, p = summed(p)
        cot_nxt, p = summed(p)
        row_out, dw_ref, const_out = refs[p:p + nr], refs[p + nr], refs[p + nr + 1:]

        x_cur = cur[0]
        x_down = [_shift_down(x_cur, prev, s_) for s_ in range(taps)]
        _, vjp = jax.vjp(fn, _conv_apply(x_cur, prev, w_ref, x_down), *cur[1:], *cvals)
        grads = vjp(tuple(cot_cur))
        _, vjp_n = jax.vjp(fn, _conv_apply(nxt[0], x_cur[tm - HALO:], w_ref), *nxt[1:], *cvals)
        dc_n = jnp.where(i < nblk - 1, vjp_n(tuple(cot_nxt))[0], 0.0)
        dc = grads[0]

        @pl.when(i == 0)
        def _():
            dw_ref[...] = jnp.zeros_like(dw_ref)
            for q in range(nc):
                const_out[q][...] = jnp.zeros_like(const_out[q])

        dx = None
        for k in range(taps):
            s_ = taps - 1 - k
            term = _shift_up(dc, dc_n, s_) * w_ref[pl.ds(k, 1), :]
            dx = term if dx is None else dx + term
            dw_ref[pl.ds(k, 1), :] += jnp.sum(dc * x_down[s_], axis=0, keepdims=True)
        row_out[0][...] = dx.astype(row_dtypes[0])
        for q in range(1, nr):
            row_out[q][...] = grads[q].astype(row_dtypes[q])
        for q in range(nc):
            const_out[q][...] += grads[nr + q]

    ins = list(rows) + list(rows) + [rows[0], conv_w] + list(consts) + flat_cots + flat_cots
    specs = ([_row_spec(tm, r.shape[1]) for r in rows] + [_next_spec(tm, r.shape[1], T) for r in rows]
             + [_prev_spec(tm, W0), _full_spec(conv_w.shape)] + [_full_spec(c.shape) for c in consts]
             + [_row_spec(tm, c.shape[1]) for c in flat_cots] + [_next_spec(tm, c.shape[1], T) for c in flat_cots])
    out_shapes = ([jax.ShapeDtypeStruct(r.shape, d) for r, d in zip(rows, row_dtypes)]
                  + [jax.ShapeDtypeStruct(conv_w.shape, f32)] + [jax.ShapeDtypeStruct(c.shape, f32) for c in consts])
    out_specs = ([_row_spec(tm, r.shape[1]) for r in rows] + [_full_spec(conv_w.shape)]
                 + [_full_spec(c.shape) for c in consts])
    outs = pl.pallas_call(
        body, grid=(nblk,), in_specs=specs, out_specs=out_specs, out_shape=out_shapes, name=name,
        compiler_params=_params("arbitrary"))(*ins)
    return list(outs[:nr]), outs[nr], list(outs[nr + 1:])


def _sigmoid(x):
    return 0.5 * jnp.tanh(0.5 * x) + 0.5


def _softplus(x):
    return jnp.maximum(x, 0.0) + jnp.log(1.0 + jnp.exp(jnp.minimum(x, -x)))


def _seg_sum_impl(x, seg):
    w = x.shape[-1]
    r = lax.broadcasted_iota(jnp.int32, (w, w), 0) // seg
    c = lax.broadcasted_iota(jnp.int32, (w, w), 1) // seg
    ones = (r == c).astype(bf16)
    hi = x.astype(bf16)
    lo = (x - hi.astype(f32)).astype(bf16)
    return (jnp.dot(hi, ones, preferred_element_type=f32) + jnp.dot(lo, ones, preferred_element_type=f32))


@functools.partial(jax.custom_vjp, nondiff_argnums=(1,))
def _seg_sum(x, seg):
    return _seg_sum_impl(x, seg)


_seg_sum.defvjp(lambda x, seg: (_seg_sum_impl(x, seg), None), lambda seg, _, g: (_seg_sum_impl(g, seg),))


def _rms(x, g):
    return x * lax.rsqrt(jnp.mean(x * x, axis=-1, keepdims=True) + NORM_EPS) * g


def _rms_fn(x, g):
    return (_rms(x, g),)


def _loss_rows(x2, tgt, g):
    e = _rms(x2, g) - tgt
    return 0.5 * jnp.sum(e * e, axis=-1, keepdims=True) * (1.0 / D_MODEL)


@jax.custom_vjp
def _dot_lo(a, b):
    return jnp.dot(a.astype(bf16), b.astype(bf16), preferred_element_type=f32)


def _dot_lo_bwd(ab, g):
    a, b = ab
    gl = g.astype(bf16)
    return (lax.dot_general(gl, b.astype(bf16), (((1,), (1,)), ((), ())), preferred_element_type=f32),
            lax.dot_general(a.astype(bf16), gl, (((0,), (0,)), ((), ())), preferred_element_type=f32))


_dot_lo.defvjp(lambda a, b: (_dot_lo(a, b), (a, b)), _dot_lo_bwd)


def _rwkv_prep_fn(ps, w0, w2p, a0, a2p, g2, k_k, k_a):
    r, k, v = ps[:, 0:512], ps[:, 512:1024], ps[:, 1024:1536]
    wa, gl = ps[:, 1536:1664], ps[:, 1664:1792]
    z = w0 + _dot_lo(jnp.tanh(wa), w2p)
    w_log = -_softplus(-z) - 0.5
    lw = -jnp.exp(w_log)
    a = _sigmoid(a0 + _dot_lo(wa, a2p))
    g = _dot_lo(_sigmoid(gl), g2)
    kx = k * k_k
    kk = kx * lax.rsqrt(_seg_sum(kx * kx, RWKV_HD) + L2_EPS)
    k2 = k * (1.0 + (a - 1.0) * k_a)
    return r, lw, k2, v, -kk, kk * a, g


def _rwkv_post_fn(y, r, k2, v, g, ln_w, ln_b, rk):
    mean = _seg_sum(y, RWKV_HD) * (1.0 / RWKV_HD)
    yc = y - mean
    var = _seg_sum(yc * yc, RWKV_HD) * (1.0 / RWKV_HD)
    yn = yc * lax.rsqrt(var + GN_EPS) * ln_w + ln_b
    bonus = _seg_sum(r * k2 * rk, RWKV_HD) * v
    return ((yn + bonus) * g,)


def _gdn_prep_fn(cq, ck, cv):
    silu = lambda c: c * _sigmoid(c)
    q, k = silu(cq), silu(ck)
    q = q * lax.rsqrt(jnp.sum(q * q, axis=-1, keepdims=True) + L2_EPS) * (GDN_HD ** -0.5)
    k = k * lax.rsqrt(jnp.sum(k * k, axis=-1, keepdims=True) + L2_EPS)
    return q, k, silu(cv)


def _gdn_gate_fn(ab, al_p, dt_p):
    lane = lax.broadcasted_iota(jnp.int32, ab.shape, 1)
    gpart = -jnp.exp(al_p) * _softplus(ab + dt_p)
    return (jnp.where(lane < GDN_HEADS, gpart, jnp.where(lane < 2 * GDN_HEADS, _sigmoid(ab), 0.0)),)


def _gdn_post_fn(o, z, nw):
    ms = _seg_sum(o * o, GDN_HD) * (1.0 / GDN_HD)
    return (o * lax.rsqrt(ms + NORM_EPS) * nw * (z * _sigmoid(z)),)


def _mix_fn(ga, gb, ya, yb):
    return (_sigmoid(ga) * ya + _sigmoid(gb) * yb,)


STRIP = 128


def _strip_conv(ref, prev_ref, w_ref, sl, first, taps):
    cur = ref[:, sl]
    prev = jnp.where(first, 0.0, prev_ref[:, sl])
    down = [_shift_down(cur, prev, s) for s in range(taps)]
    conv = None
    for k in range(taps):
        term = down[taps - 1 - k] * w_ref[pl.ds(k, 1), sl]
        conv = term if conv is None else conv + term
    return cur, down, conv


def _group_fwd(name, fn, x, w, shared_cols, group_cols, consts, n_out, tm):
    T, W = x.shape
    taps = w.shape[0]
    n_groups = len(group_cols)
    nc = len(consts)

    def body(x_ref, xp_ref, w_ref, *refs):
        const_refs, out_refs = refs[:nc], refs[nc:]
        first = pl.program_id(0) == 0
        shared = [_strip_conv(x_ref, xp_ref, w_ref, sl, first, taps)[2] for sl in shared_cols]
        for j, cols in enumerate(group_cols):
            sl = slice(STRIP * j, STRIP * (j + 1))
            convs = [_strip_conv(x_ref, xp_ref, w_ref, c, first, taps)[2] for c in cols]
            outs = fn(*convs, *shared, *[c[:, sl] for c in const_refs])
            for o_ref, o in zip(out_refs, outs):
                o_ref[:, sl] = o

    return pl.pallas_call(
        body, grid=(T // tm,),
        in_specs=[_row_spec(tm, W), _prev_spec(tm, W), _full_spec(w.shape)] + [_full_spec(c.shape) for c in consts],
        out_specs=[_row_spec(tm, STRIP * n_groups)] * n_out,
        out_shape=[jax.ShapeDtypeStruct((T, STRIP * n_groups), f32)] * n_out, name=name,
        compiler_params=_params("parallel"))(x, x, w, *consts)


def _group_bwd(name, fn, x, w, shared_cols, group_cols, consts, cots, tm):
    T, W = x.shape
    taps = w.shape[0]
    nblk = T // tm
    nc, ns = len(consts), len(shared_cols)
    flat_cots = [c for grp in cots for c in grp]
    n_cot = len(flat_cots)

    def body(x_ref, xp_ref, xn_ref, w_ref, *refs):
        const_refs, refs = refs[:nc], refs[nc:]
        cot_refs, cotn_refs, refs = refs[:n_cot], refs[n_cot:2 * n_cot], refs[2 * n_cot:]
        dx_ref, dw_ref, const_out = refs[0], refs[1], refs[2:]
        i = pl.program_id(0)
        first, last = i == 0, i == nblk - 1

        @pl.when(first)
        def _():
            dw_ref[...] = jnp.zeros_like(dw_ref)
            for q in range(nc):
                const_out[q][...] = jnp.zeros_like(const_out[q])

        def convs_of(sl):
            cur, down, conv = _strip_conv(x_ref, xp_ref, w_ref, sl, first, taps)
            nxt, conv_n = xn_ref[:, sl], None
            for k in range(taps):
                term = _shift_down(nxt, cur[tm - HALO:], taps - 1 - k) * w_ref[pl.ds(k, 1), sl]
                conv_n = term if conv_n is None else conv_n + term
            return down, conv, conv_n

        def conv_back(sl, down, dc, dc_n):
            dx = None
            for k in range(taps):
                s_ = taps - 1 - k
                term = _shift_up(dc, dc_n, s_) * w_ref[pl.ds(k, 1), sl]
                dx = term if dx is None else dx + term
                dw_ref[pl.ds(k, 1), sl] += jnp.sum(dc * down[s_], axis=0, keepdims=True)
            dx_ref[:, sl] = dx.astype(dx_ref.dtype)

        def summed(refs_, sl, mask):
            out, p = [], 0
            for grp in cots:
                acc = refs_[p][:, sl]
                for t in range(1, len(grp)):
                    acc = acc + refs_[p + t][:, sl]
                p += len(grp)
                out.append(jnp.where(last, 0.0, acc) if mask else acc)
            return tuple(out)

        shared = [convs_of(sl) for sl in shared_cols]
        d_shared, d_shared_n = [None] * ns, [None] * ns
        for j, cols in enumerate(group_cols):
            sl = slice(STRIP * j, STRIP * (j + 1))
            mine = [convs_of(c) for c in cols]
            cj = [c[:, sl] for c in const_refs]
            _, vjp = jax.vjp(fn, *[m[1] for m in mine], *[m[1] for m in shared], *cj)
            grads = vjp(summed(cot_refs, sl, False))
            _, vjp_n = jax.vjp(fn, *[m[2] for m in mine], *[m[2] for m in shared], *cj)
            grads_n = vjp_n(summed(cotn_refs, sl, True))
            for q, c in enumerate(cols):
                conv_back(c, mine[q][0], grads[q], grads_n[q])
            for q in range(ns):
                g, gn = grads[len(cols) + q], grads_n[len(cols) + q]
                d_shared[q] = g if d_shared[q] is None else d_shared[q] + g
                d_shared_n[q] = gn if d_shared_n[q] is None else d_shared_n[q] + gn
            for q in range(nc):
                const_out[q][:, sl] += grads[len(cols) + ns + q]
        for q, c in enumerate(shared_cols):
            conv_back(c, shared[q][0], d_shared[q], d_shared_n[q])

    outs = pl.pallas_call(
        body, grid=(nblk,),
        in_specs=[_row_spec(tm, W), _prev_spec(tm, W), _next_spec(tm, W, T), _full_spec(w.shape)]
        + [_full_spec(c.shape) for c in consts] + [_row_spec(tm, c.shape[1]) for c in flat_cots]
        + [_next_spec(tm, c.shape[1], T) for c in flat_cots],
        out_specs=[_row_spec(tm, W), _full_spec(w.shape)] + [_full_spec(c.shape) for c in consts],
        out_shape=[jax.ShapeDtypeStruct((T, W), bf16), jax.ShapeDtypeStruct(w.shape, f32)]
        + [jax.ShapeDtypeStruct(c.shape, f32) for c in consts], name=name,
        compiler_params=_params("arbitrary"))(x, x, x, w, *consts, *flat_cots, *flat_cots)
    return outs[0], outs[1], list(outs[2:])


def _ffn_strip_fn(cg, cu):
    return cg * _sigmoid(cg) * cu


def _ffn_act_fwd(h, w, tm):
    T, W2 = h.shape
    H = W2 // 2
    taps = w.shape[0]

    def body(h_ref, hp_ref, w_ref, o_ref):
        first = pl.program_id(0) == 0
        for j in range(H // STRIP):
            gs, us = slice(STRIP * j, STRIP * (j + 1)), slice(H + STRIP * j, H + STRIP * (j + 1))
            cg = _strip_conv(h_ref, hp_ref, w_ref, gs, first, taps)[2]
            cu = _strip_conv(h_ref, hp_ref, w_ref, us, first, taps)[2]
            o_ref[:, gs] = _ffn_strip_fn(cg, cu).astype(o_ref.dtype)

    return pl.pallas_call(
        body, grid=(T // tm,), in_specs=[_row_spec(tm, W2), _prev_spec(tm, W2), _full_spec(w.shape)],
        out_specs=_row_spec(tm, H), out_shape=jax.ShapeDtypeStruct((T, H), bf16), name="ffn_act",
        compiler_params=_params("parallel"))(h, h, w)


def _ffn_act_bwd(h, dact, w, tm):
    T, W2 = h.shape
    H = W2 // 2
    taps = w.shape[0]
    nblk = T // tm

    def body(h_ref, hp_ref, hn_ref, d_ref, dn_ref, w_ref, dh_ref, dw_ref):
        i = pl.program_id(0)
        first, last = i == 0, i == nblk - 1

        @pl.when(first)
        def _():
            dw_ref[...] = jnp.zeros_like(dw_ref)

        for j in range(H // STRIP):
            gs, us = slice(STRIP * j, STRIP * (j + 1)), slice(H + STRIP * j, H + STRIP * (j + 1))
            parts = {}
            for name, sl in (('g', gs), ('u', us)):
                cur, down, conv = _strip_conv(h_ref, hp_ref, w_ref, sl, first, taps)
                nxt = hn_ref[:, sl]
                conv_n = None
                for k in range(taps):
                    term = _shift_down(nxt, cur[tm - HALO:], taps - 1 - k) * w_ref[pl.ds(k, 1), sl]
                    conv_n = term if conv_n is None else conv_n + term
                parts[name] = (down, conv, conv_n)
            _, vjp = jax.vjp(_ffn_strip_fn, parts['g'][1], parts['u'][1])
            dcs = vjp(d_ref[:, gs])
            _, vjp_n = jax.vjp(_ffn_strip_fn, parts['g'][2], parts['u'][2])
            dcs_n = vjp_n(jnp.where(last, 0.0, dn_ref[:, gs]))
            for (name, sl), dc, dc_n in zip((('g', gs), ('u', us)), dcs, dcs_n):
                down = parts[name][0]
                dx = None
                for k in range(taps):
                    s_ = taps - 1 - k
                    term = _shift_up(dc, dc_n, s_) * w_ref[pl.ds(k, 1), sl]
                    dx = term if dx is None else dx + term
                    dw_ref[pl.ds(k, 1), sl] += jnp.sum(dc * down[s_], axis=0, keepdims=True)
                dh_ref[:, sl] = dx.astype(dh_ref.dtype)

    return pl.pallas_call(
        body, grid=(nblk,),
        in_specs=[_row_spec(tm, W2), _prev_spec(tm, W2), _next_spec(tm, W2, T), _row_spec(tm, H), _next_spec(tm, H, T),
                  _full_spec(w.shape)],
        out_specs=[_row_spec(tm, W2), _full_spec(w.shape)],
        out_shape=[jax.ShapeDtypeStruct((T, W2), bf16), jax.ShapeDtypeStruct(w.shape, f32)], name="ffn_act_bwd",
        compiler_params=_params("arbitrary"))(h, h, h, dact, dact, w)


N_POS = 4


def _xy_out_shapes(bufs, scatter):
    return [jax.ShapeDtypeStruct((N_POS,) + tuple(b.shape[1:] if scatter else b.shape), b.dtype) for b in bufs]


def _xy_sems(n, scatter):
    sems = [pltpu.SemaphoreType.DMA((3 * n,)), pltpu.SemaphoreType.DMA((3 * n,)), pltpu.SemaphoreType.DMA((n,))]
    return sems if scatter else sems + [pltpu.SemaphoreType.DMA((3 * n,)), pltpu.SemaphoreType.DMA((3 * n,))]


def _xy_copies(in_refs, out_refs, sems, scatter):
    n = len(in_refs)
    send_sems, recv_sems, local_sems = sems[:3]

    def place():
        x, y, c = lax.axis_index("x"), lax.axis_index("y"), lax.axis_index("c")
        return x, y, c, 2 * x + y, [(1 - x, y), (x, 1 - y), (1 - x, 1 - y)]

    def half(ref, a, which):
        rows = in_refs[a].shape[0] // 2
        return ref.at[pl.ds(pl.multiple_of(which * rows, HALO), rows)]

    def ici(a, k, src, dst, peer, c):
        return pltpu.make_async_remote_copy(
            src_ref=src, dst_ref=dst, send_sem=send_sems.at[3 * a + k], recv_sem=recv_sems.at[3 * a + k],
            device_id=(peer[0], peer[1], c), device_id_type=pl.DeviceIdType.MESH)

    def outgoing():
        x, y, c, me, peers = place()
        own = [pltpu.make_async_copy(in_refs[a].at[me] if scatter else in_refs[a], out_refs[a].at[me], local_sems.at[a])
               for a in range(n)]
        if scatter:
            sends = [ici(a, k, in_refs[a].at[2 * p[0] + p[1]], out_refs[a].at[me], p, c)
                     for a in range(n) for k, p in enumerate(peers)]
        else:
            sends = [ici(a, k, half(in_refs[a], a, c), half(out_refs[a].at[me], a, c), p, c)
                     for a in range(n) for k, p in enumerate(peers)]
        return own, sends

    def arrivals():
        x, y, c, me, peers = place()
        if scatter:
            return [ici(a, k, in_refs[a].at[me], out_refs[a].at[2 * p[0] + p[1]], p, c)
                    for a in range(n) for k, p in enumerate(peers)]
        return [ici(a, k, half(in_refs[a], a, c), half(out_refs[a].at[2 * p[0] + p[1]], a, c), p, c)
                for a in range(n) for k, p in enumerate(peers)]

    def to_sibling(mine):
        x, y, c, me, peers = place()
        which = c if mine else 1 - c
        return [pltpu.make_async_remote_copy(
            src_ref=half(out_refs[a].at[2 * p[0] + p[1]], a, which), dst_ref=half(out_refs[a].at[2 * p[0] + p[1]], a, which),
            send_sem=sems[3].at[3 * a + k], recv_sem=sems[4].at[3 * a + k],
            device_id=(x, y, 1 - c), device_id_type=pl.DeviceIdType.MESH) for a in range(n) for k, p in enumerate(peers)]

    def start():
        own, sends = outgoing()
        for cp in own + sends:
            cp.start()

    def finish():
        if scatter:
            for cp in arrivals():
                cp.wait_recv()
        else:
            passed = to_sibling(True)
            for cp, fwd in zip(arrivals(), passed):
                cp.wait_recv()
                fwd.start()
            for cp in to_sibling(False):
                cp.wait_recv()
            for fwd in passed:
                fwd.wait_send()
        own, sends = outgoing()
        for cp in sends:
            cp.wait_send()
        for cp in own:
            cp.wait()

    return start, finish


_NN, _NT, _TN = 'hcs,hsd->hcd', 'hcd,hsd->hcs', 'hcd,hce->hde'


def _lo(spec, a, b):
    return jnp.einsum(spec, a.astype(bf16), b.astype(bf16), preferred_element_type=f32)


@jax.custom_vjp
def _bmm(a, b):
    return _lo(_NN, a, b)


_bmm.defvjp(lambda a, b: (_lo(_NN, a, b), (a, b)), lambda ab, g: (_lo(_NT, g, ab[1]), _lo(_TN, ab[0], g)))


@jax.custom_vjp
def _bmm_nt(a, b):
    return _lo(_NT, a, b)


_bmm_nt.defvjp(lambda a, b: (_lo(_NT, a, b), (a, b)), lambda ab, g: (_lo(_NN, g, ab[1]), _lo(_TN, g, ab[0])))


@jax.custom_vjp
def _bmm_tn(a, b):
    return _lo(_TN, a, b)


_bmm_tn.defvjp(lambda a, b: (_lo(_TN, a, b), (a, b)), lambda ab, g: (_lo(_NT, ab[1], g), _lo(_NN, ab[0], g)))


def _masks(H, C):
    row = lax.broadcasted_iota(jnp.int32, (H, C, C), 1)
    col = lax.broadcasted_iota(jnp.int32, (H, C, C), 2)
    return row, col


def _tri_inv_impl(L):
    H, C, _ = L.shape
    row, col = _masks(H, C)
    eye = (row == col).astype(f32)
    base = 16
    same = (row // base) == (col // base)
    Ld = jnp.where(same, L, 0.0)
    X = -Ld
    inv = eye + X
    for _ in range(3):
        X = _bmm(X, X)
        inv = _bmm(inv, eye + X)
    if C == base:
        return inv
    N = _bmm(inv, L - Ld)
    out = eye - N
    levels = C // base
    P = N
    span = 2
    while span < levels:
        P = _bmm(P, P)
        out = _bmm(out, eye + P)
        span *= 2
    return _bmm(out, inv)


@jax.custom_vjp
def _tri_inv(L):
    return _tri_inv_impl(L)


def _tri_inv_fwd(L):
    T = _tri_inv_impl(L)
    return T, T


def _tri_inv_bwd(T, dT):
    return (-_bmm_nt(_bmm_tn(T, dT), T),)


_tri_inv.defvjp(_tri_inv_fwd, _tri_inv_bwd)


@jax.custom_vjp
def _tri_inv_known(L, T):
    return T


_tri_inv_known.defvjp(lambda L, T: (T, T), lambda T, dT: (_tri_inv_bwd(T, dT)[0], jnp.zeros_like(T)))


def _cumsum_impl(x, reverse):
    C = x.shape[1]
    row = lax.broadcasted_iota(jnp.int32, x.shape, 1)
    s = 1
    while s < C:
        if reverse:
            x = x + jnp.where(row < C - s, pltpu.roll(x, C - s, 1), 0.0)
        else:
            x = x + jnp.where(row >= s, pltpu.roll(x, s, 1), 0.0)
        s *= 2
    return x


@jax.custom_vjp
def _cumsum(x):
    return _cumsum_impl(x, False)


_cumsum.defvjp(lambda x: (_cumsum_impl(x, False), None), lambda _, g: (_cumsum_impl(g, True),))


def _wkv_prep(r, lw, k, v, a, b, inv=None):
    lane = lax.broadcasted_iota(jnp.int32, (r.shape[0], 128), 1)
    low = lane < RWKV_HD

    def heads(t):
        out = []
        for p in range(RWKV_HEADS // 2):
            pair = t[:, 128 * p:128 * (p + 1)]
            out += [jnp.where(low, pair, 0.0), jnp.where(low, 0.0, pair)]
        return jnp.concatenate([t[None] for t in out], axis=0)

    r, lw, k, v, a, b = [heads(t) for t in (r, lw, k, v, a, b)]
    H, C, D = r.shape
    row, col = _masks(H, C)
    incl, strict = row >= col, row > col
    cw = _cumsum(lw)
    cwp = cw - lw
    cwl = jnp.sum(lw, axis=1, keepdims=True)
    en = jnp.exp(-cw)
    at, rt, bt, kt = a * jnp.exp(cwp), r * jnp.exp(cw), b * en, k * en
    Lab = -jnp.where(strict, _bmm_nt(at, bt), 0.0)
    Tm = _tri_inv(Lab) if inv is None else _tri_inv_known(Lab, inv)
    ar = jnp.concatenate([at, rt], axis=1)
    gram = _bmm_nt(ar, jnp.concatenate([bt, kt], axis=1))
    row2 = lax.broadcasted_iota(jnp.int32, (H, 2 * C, 2 * C), 1)
    col2 = lax.broadcasted_iota(jnp.int32, (H, 2 * C, 2 * C), 2) % C
    gram = jnp.where(((row2 < C) & (row2 > col2)) | ((row2 >= C) & (row2 - C >= col2)), gram, 0.0)
    a_bk, r_bk = gram[:, :C], gram[:, C:]
    lak_v = _bmm(a_bk, jnp.concatenate([jnp.zeros_like(v), v], axis=1))
    ed = jnp.exp(cwl - cw)
    zdec = jnp.swapaxes(jnp.broadcast_to(jnp.exp(cwl), (H, D, D)), 1, 2)
    return (ar, Tm, lak_v, r_bk, jnp.concatenate([b * ed, k * ed], axis=1), zdec, v), Tm


def _wkv_step(Z, ar, Tm, lak_v, r_bk, bk_d, zdec, v):
    C = Tm.shape[1]
    ar_z = _bmm(ar, Z)
    uv = jnp.concatenate([_bmm(Tm, ar_z[:, :C] + lak_v), v], axis=1)
    y = ar_z[:, C:] + _bmm(r_bk, uv)
    Z1 = Z * zdec + _bmm_tn(bk_d, uv)
    return jnp.concatenate([y[2 * p] + y[2 * p + 1] for p in range(RWKV_HEADS // 2)], axis=1), Z1


def _split3(x):
    hi = x.astype(bf16)
    mid = (x - hi.astype(f32)).astype(bf16)
    lo = (x - hi.astype(f32) - mid.astype(f32)).astype(bf16)
    return hi, mid, lo


@jax.custom_vjp
def _spread(x, sel):
    return sum(jnp.dot(t, sel, preferred_element_type=f32) for t in _split3(x))


def _spread_bwd(sel, g):
    dn = (((1,), (1,)), ((), ()))
    return sum(lax.dot_general(t, sel, dn, preferred_element_type=f32) for t in _split3(g)), None


_spread.defvjp(lambda x, sel: (_spread(x, sel), sel), _spread_bwd)


def _gdn_prep(q, k, v, gbeta, inv=None):
    heads = lambda t: jnp.concatenate([t[None, :, GDN_HD * h:GDN_HD * (h + 1)] for h in range(GDN_HEADS)], axis=0)
    src = lax.broadcasted_iota(jnp.int32, (W_AB, 2 * GDN_W), 0)
    dst = lax.broadcasted_iota(jnp.int32, (W_AB, 2 * GDN_W), 1) // GDN_HD
    spread = _spread(gbeta, (src == dst).astype(bf16))
    q, k, v, g, beta = heads(q), heads(k), heads(v), heads(spread[:, :GDN_W]), heads(spread[:, GDN_W:])
    H, C, D = q.shape
    row, col = _masks(H, C)
    incl, strict = row >= col, row > col
    gc = _cumsum(g)
    diff = gc - jnp.swapaxes(gc, 1, 2)
    decay = jnp.where(incl, jnp.exp(jnp.where(incl, diff, 0.0)), 0.0)
    gl = jnp.sum(g, axis=1, keepdims=True)
    kb, vb = k * beta, v * beta
    gram = _bmm_nt(jnp.concatenate([kb, q], axis=1), k)
    L = jnp.where(strict, gram[:, :C] * decay, 0.0)
    attn = jnp.where(incl, gram[:, C:] * decay, 0.0)
    egc = jnp.exp(gc)
    Tm = _tri_inv(L) if inv is None else _tri_inv_known(L, inv)
    t_vk = _bmm(Tm, jnp.concatenate([vb, kb * egc], axis=2))
    return (t_vk[:, :, :D], jnp.concatenate([t_vk[:, :, D:], q * egc], axis=1), attn, k * jnp.exp(gl - gc), jnp.exp(gl)), Tm


def _gdn_step(S, u, wq, attn, ke, sdec):
    C = u.shape[1]
    wq_s = _bmm(wq, S)
    v_new = u - wq_s[:, :C]
    o = wq_s[:, C:] + _bmm(attn, v_new)
    S1 = S * sdec + _bmm_tn(ke, v_new)
    return jnp.concatenate([o[h] for h in range(GDN_HEADS)], axis=1), S1


def _scan_fwd(name, fns, ins, C, H, dh, w_out, per_step, side=None):
    prep, step = fns
    T = ins[0].shape[0]
    n_in = len(ins)
    blk = C * per_step
    nblk = T // blk
    n_side = 0 if side is None else len(side[0])

    def body(*refs):
        in_refs, refs = refs[:n_in], refs[n_in:]
        side_in, refs = refs[:n_side], refs[n_side:]
        y_ref, zs_ref, inv_ref, refs = refs[0], refs[1], refs[2], refs[3:]
        side_out, refs = refs[:n_side], refs[n_side:]
        z_scr = refs[0]
        if side is not None:
            start, finish = _xy_copies(side_in, side_out, refs[1:], side[1])
            pl.when(pl.program_id(0) == 0)(start)

        @pl.when(pl.program_id(0) == 0)
        def _():
            z_scr[...] = jnp.zeros_like(z_scr)

        rows = [slice(C * j, C * (j + 1)) for j in range(per_step)]
        prepped = [prep(*[r[rw, :] for r in in_refs]) for rw in rows]
        Z = z_scr[...]
        for j, rw in enumerate(rows):
            zs_ref[j] = Z
            inv_ref[j] = prepped[j][1]
            y, Z = step(Z, *prepped[j][0])
            y_ref[rw, :] = y
        z_scr[...] = Z
        if side is not None:
            pl.when(pl.program_id(0) == nblk - 1)(finish)

    side_bufs = [] if side is None else list(side[0])
    any_spec = pl.BlockSpec(memory_space=pl.ANY)
    return pl.pallas_call(
        body, grid=(nblk,),
        in_specs=[pl.BlockSpec((blk, a.shape[1]), lambda i: (i, 0)) for a in ins] + [any_spec] * n_side,
        out_specs=[pl.BlockSpec((blk, w_out), lambda i: (i, 0)), pl.BlockSpec((per_step, H, dh, dh), lambda i: (i, 0, 0, 0)),
                   pl.BlockSpec((per_step, H, C, C), lambda i: (i, 0, 0, 0))] + [any_spec] * n_side,
        out_shape=[jax.ShapeDtypeStruct((T, w_out), f32), jax.ShapeDtypeStruct((T // C, H, dh, dh), f32),
                   jax.ShapeDtypeStruct((T // C, H, C, C), f32)]
        + (_xy_out_shapes(side_bufs, side[1]) if side is not None else []),
        scratch_shapes=[pltpu.VMEM((H, dh, dh), f32)] + (_xy_sems(n_side, side[1]) if side is not None else []), name=name,
        compiler_params=_params("arbitrary"))(*ins, *side_bufs)


def _scan_bwd(name, fns, ins, dy, zs, invs, C, per_step, side=None):
    prep, step = fns
    T = ins[0].shape[0]
    _, H, dh, _ = zs.shape
    n_in = len(ins)
    blk = C * per_step
    nblk = T // blk
    n_side = 0 if side is None else len(side[0])

    def body(*refs):
        in_refs, dy_ref, zs_ref, inv_ref, refs = refs[:n_in], refs[n_in], refs[n_in + 1], refs[n_in + 2], refs[n_in + 3:]
        side_in, refs = refs[:n_side], refs[n_side:]
        out_refs, refs = refs[:n_in], refs[n_in:]
        side_out, refs = refs[:n_side], refs[n_side:]
        dz_scr = refs[0]
        if side is not None:
            start, finish = _xy_copies(side_in, side_out, refs[1:], side[1])
            pl.when(pl.program_id(0) == 0)(start)

        @pl.when(pl.program_id(0) == 0)
        def _():
            dz_scr[...] = jnp.zeros_like(dz_scr)

        rows = [slice(C * j, C * (j + 1)) for j in range(per_step)]
        prepped = [jax.vjp(lambda *a, j=j: prep(*a, inv=inv_ref[j])[0], *[r[rw, :] for r in in_refs])
                   for j, rw in enumerate(rows)]
        d_prepped = [None] * per_step
        dZ = dz_scr[...]
        for j in reversed(range(per_step)):
            _, pull = jax.vjp(step, zs_ref[j], *prepped[j][0])
            dZ, *d_prepped[j] = pull((dy_ref[rows[j], :], dZ))
        dz_scr[...] = dZ
        for j, rw in enumerate(rows):
            for o_ref, gval in zip(out_refs, prepped[j][1](tuple(d_prepped[j]))):
                o_ref[rw, :] = gval
        if side is not None:
            pl.when(pl.program_id(0) == nblk - 1)(finish)

    side_bufs = [] if side is None else list(side[0])
    any_spec = pl.BlockSpec(memory_space=pl.ANY)
    rev = lambda i: (nblk - 1 - i, 0)
    return pl.pallas_call(
        body, grid=(nblk,),
        in_specs=[pl.BlockSpec((blk, a.shape[1]), rev) for a in ins]
        + [pl.BlockSpec((blk, dy.shape[1]), rev), pl.BlockSpec((per_step, H, dh, dh), lambda i: (nblk - 1 - i, 0, 0, 0)),
           pl.BlockSpec((per_step, H, C, C), lambda i: (nblk - 1 - i, 0, 0, 0))] + [any_spec] * n_side,
        out_specs=[pl.BlockSpec((blk, a.shape[1]), rev) for a in ins] + [any_spec] * n_side,
        out_shape=[jax.ShapeDtypeStruct(a.shape, f32) for a in ins]
        + (_xy_out_shapes(side_bufs, side[1]) if side is not None else []),
        scratch_shapes=[pltpu.VMEM((H, dh, dh), f32)] + (_xy_sems(n_side, side[1]) if side is not None else []), name=name,
        compiler_params=_params("arbitrary"))(*ins, dy, zs, invs, *side_bufs)


def _residual_mm(name, a, b, res, tail, row_extras, consts, row_out, acc_out, tm):
    M, K = a.shape
    N = b.shape[1]
    ne, nc, nr = len(row_extras), len(consts), len(row_out)

    def body(a_ref, b_ref, res_ref, *refs):
        extra_refs, const_refs, out_refs = refs[:ne], refs[ne:ne + nc], refs[ne + nc:]
        tile = res_ref[...] + jnp.dot(a_ref[...].astype(bf16), b_ref[...].astype(bf16), preferred_element_type=f32)
        outs = tail(tile, *[r[...] for r in extra_refs], *[c[...] for c in const_refs])
        for o_ref, o in zip(out_refs[:nr], outs[:nr]):
            o_ref[...] = o.astype(o_ref.dtype)

        @pl.when(pl.program_id(0) == 0)
        def _():
            for o_ref in out_refs[nr:]:
                o_ref[...] = jnp.zeros_like(o_ref)

        for o_ref, o in zip(out_refs[nr:], outs[nr:]):
            o_ref[...] += o

    return pl.pallas_call(
        body, grid=(M // tm,),
        in_specs=[_row_spec(tm, K), _full_spec(b.shape), _row_spec(tm, N)] + [_row_spec(tm, e.shape[1]) for e in row_extras]
        + [_full_spec(c.shape) for c in consts],
        out_specs=[_row_spec(tm, w) for w, _ in row_out] + [_full_spec(sh) for sh in acc_out],
        out_shape=[jax.ShapeDtypeStruct((M, w), d) for w, d in row_out] + [jax.ShapeDtypeStruct(sh, f32) for sh in acc_out],
        name=name, compiler_params=_params("arbitrary"))(a, b, res, *row_extras, *consts)


def _norm_tail(x1, g):
    return x1, _rms(x1, g)


def _loss_tail(x2, tgt, g):
    l, vjp = jax.vjp(lambda xv, gv: _loss_rows(xv, tgt, gv), x2, g)
    dx, dg = vjp(jnp.ones_like(l))
    return dx, dg, jnp.zeros((1, 128), f32) + jnp.sum(l)


def _local_step(x, tgt, W, late=None):
    row = lambda a: a.reshape(1, -1)
    wp = W['w_in_pad']
    w_rwkv, w_qkv, w_z = wp[:, :OFF_QKV], wp[:, OFF_QKV:OFF_Z], wp[:, OFF_Z:OFF_GATES]
    w_gates, w_ab = wp[:, OFF_GATES:OFF_AB], wp[:, OFF_AB:]
    mu = row(W['rwkv_mu'])
    mixw = jnp.concatenate([mu, 1.0 - mu], axis=0)
    zpad = jnp.zeros((64, RWKV_W), f32)
    w2p = jnp.concatenate([W['rwkv_w2'], zpad], axis=0)
    a2p = jnp.concatenate([zpad, W['rwkv_a2']], axis=0)
    rw_consts = [row(W['rwkv_w0']), w2p, row(W['rwkv_a0']), a2p, W['rwkv_g2'], row(W['rwkv_k_k']), row(W['rwkv_k_a'])]
    post_consts = [row(W['rwkv_ln_w']), row(W['rwkv_ln_b']), row(W['rwkv_r_k'])]
    pad4 = lambda a: jnp.pad(row(a), ((0, 0), (0, W_AB - GDN_HEADS)))
    gd_consts = [pad4(W['gdn_a_log']), pad4(W['gdn_dt_bias'])]
    nw_t = jnp.tile(row(W['gdn_norm_w']), (1, GDN_HEADS))
    g1, g2n, gf = row(W['norm1_g']), row(W['norm2_g']), row(W['final_g'])

    (u,) = _pw_fwd("norm1", _rms_fn, [x], [g1], [D_MODEL], 512, out_dtype=bf16)
    p_rwkv = _mm(u, w_rwkv, 'nn', "in_rwkv")
    qkv_raw = _mm(u, w_qkv, 'nn', "in_qkv")
    z = _mm(u, w_z, 'nn', "in_z")
    gates = _mm(u, w_gates, 'nn', "in_gates")
    ab = _mm(u, w_ab, 'nn', "in_ab")

    r, lw, k2, v, a_, b_, g = _pw_fwd("rwkv_prep", _rwkv_prep_fn, [p_rwkv], rw_consts, [RWKV_W] * 7, 256, conv_w=mixw)
    wkv_in = [r, lw, k2, v, a_, b_]
    y, zs_wkv, inv_wkv, *gathered = _scan_fwd("wkv_fwd", (_wkv_prep, _wkv_step), wkv_in, WKV_CHUNK, RWKV_HEADS, 2 * RWKV_HD, RWKV_W, WKV_PER_STEP,
                                     side=None if late is None else (late['shards'][0], False))
    if late is not None:
        W = dict(W, **late['assemble'](0, gathered))
    (ya_in,) = _pw_fwd("rwkv_post", _rwkv_post_fn, [y, r, k2, v, g], post_consts, [RWKV_W], 256, out_dtype=bf16, strip=128)
    ya = _mm(ya_in, W['rwkv_proj'], 'nn', "rwkv_proj")

    lanes = lambda off: slice(off, off + STRIP)
    gd_groups = [[lanes(GDN_HD * h), lanes(GDN_W + GDN_HD * h), lanes(2 * GDN_W + GDN_HD * h)] for h in range(GDN_HEADS)]
    gq, gk, gv = _group_fwd("gdn_prep", _gdn_prep_fn, qkv_raw, W['gdn_conv_w'], [], gd_groups, [], 3, 256)
    (gbeta,) = _pw_fwd("gdn_gate", _gdn_gate_fn, [ab], gd_consts, [W_AB], 512)
    gdn_in = [gq, gk, gv, gbeta]
    o, zs_gdn, inv_gdn, *gathered = _scan_fwd("gdn_fwd", (_gdn_prep, _gdn_step), gdn_in, GDN_CHUNK, GDN_HEADS, GDN_HD, GDN_W, GDN_PER_STEP,
                                     side=None if late is None else (late['shards'][1], False))
    if late is not None:
        W = dict(W, **late['assemble'](1, gathered))
    (yb_in,) = _pw_fwd("gdn_post", _gdn_post_fn, [o, z], [nw_t], [GDN_W], 256, out_dtype=bf16, strip=128)
    yb = _mm(yb_in, W['gdn_proj'], 'nn', "gdn_proj")

    ga, gb = _cols(gates, D_MODEL, 0), _cols(gates, D_MODEL, 1)
    (mixed,) = _pw_fwd("mix", _mix_fn, [ga, gb, ya, yb], [], [D_MODEL], 256, out_dtype=bf16, strip=256)
    x1, u2 = _residual_mm("w_out", mixed, W['w_out'], x, _norm_tail, [], [g2n], [(D_MODEL, f32), (D_MODEL, bf16)], [], 512)
    h = _mm(u2, W['ffn_up'], 'nn', "ffn_up")
    act = _ffn_act_fwd(h, W['ffn_conv_w'], 256)

    G = {}
    slab_out = None if late is None else N_POS
    dx2, dgf, loss = _residual_mm("ffn_down", act, W['ffn_down'], x1, _loss_tail, [tgt], [gf], [(D_MODEL, f32)],
                                  [gf.shape, (1, 128)], 512)
    G['final_g'] = dgf
    dact = _mm(dx2, W['ffn_down'], 'nt', "d_act")
    G['ffn_down'] = _mm(act, dx2, 'tn', "g_ffn_down", out_dtype=bf16)
    dh, G['ffn_conv_w'] = _ffn_act_bwd(h, dact, W['ffn_conv_w'], 128)
    du2 = _mm(dh, W['ffn_up'], 'nt', "d_u2")
    G['ffn_up'] = _mm(u2, dh, 'tn', "g_ffn_up", out_dtype=bf16, col_slabs=slab_out)
    (dx1,), (G['norm2_g'],) = _pw_bwd("norm2_bwd", _rms_fn, [x1], [g2n], [(du2,)], 512, add_to_first=dx2)
    dmixed = _mm(dx1, W['w_out'], 'nt', "d_mixed")
    G['w_out'] = _mm(mixed, dx1, 'tn', "g_w_out", out_dtype=bf16)
    (dga, dgb, dya, dyb), _ = _pw_bwd("mix_bwd", _mix_fn, [ga, gb, ya, yb], [], [(dmixed,)], 256, row_dtypes=[bf16] * 4,
                                      strip=256)
    dya_in = _mm(dya, W['rwkv_proj'], 'nt', "d_ya_in")
    G['rwkv_proj'] = _mm(ya_in, dya, 'tn', "g_rwkv_proj", out_dtype=bf16, col_slabs=slab_out)
    dyb_in = _mm(dyb, W['gdn_proj'], 'nt', "d_yb_in")
    G['gdn_proj'] = _mm(yb_in, dyb, 'tn', "g_gdn_proj", out_dtype=bf16, col_slabs=slab_out)

    (do, dz), (dnw_t,) = _pw_bwd("gdn_post_bwd", _gdn_post_fn, [o, z], [nw_t], [(dyb_in,)], 256, row_dtypes=[f32, bf16],
                                 strip=128)
    G['gdn_norm_w'] = dnw_t.reshape(GDN_HEADS, GDN_HD).sum(axis=0)
    dgq, dgk, dgv, dgbeta, *arrived_b = _scan_bwd("gdn_bwd", (_gdn_prep, _gdn_step), gdn_in, do, zs_gdn, inv_gdn, GDN_CHUNK,
                                                  GDN_PER_STEP, side=None if late is None else (late['slabs'](G, 1), True))
    dqkv_raw, G['gdn_conv_w'], _ = _group_bwd("gdn_prep_bwd", _gdn_prep_fn, qkv_raw, W['gdn_conv_w'], [], gd_groups, [],
                                              [(dgq,), (dgk,), (dgv,)], 128)
    (dab,), (dal_p, ddt_p) = _pw_bwd("gdn_gate_bwd", _gdn_gate_fn, [ab], gd_consts, [(dgbeta,)], 512, row_dtypes=[bf16])
    G['gdn_a_log'], G['gdn_dt_bias'] = dal_p[0, :GDN_HEADS], ddt_p[0, :GDN_HEADS]

    (dy, dr1, dk21, dv1, dg_), (G['rwkv_ln_w'], G['rwkv_ln_b'], G['rwkv_r_k']) = _pw_bwd(
        "rwkv_post_bwd", _rwkv_post_fn, [y, r, k2, v, g], post_consts, [(dya_in,)], 256, strip=128)
    dr2, dlw, dk22, dv2, da_, db_, *arrived_a = _scan_bwd(
        "wkv_bwd", (_wkv_prep, _wkv_step), wkv_in, dy, zs_wkv, inv_wkv, WKV_CHUNK, WKV_PER_STEP,
        side=None if late is None else (late['slabs'](G, 0), True))
    G['_arrived'] = (arrived_a, arrived_b)
    (dp_rwkv,), dmixw, rw_grads = _pw_conv_bwd(
        "rwkv_prep_bwd", _rwkv_prep_fn, [p_rwkv], rw_consts,
        [(dr1, dr2), (dlw,), (dk21, dk22), (dv1, dv2), (da_,), (db_,), (dg_,)], mixw, 256, row_dtypes=[bf16])
    G['rwkv_w0'], dw2p, G['rwkv_a0'], da2p, G['rwkv_g2'], G['rwkv_k_k'], G['rwkv_k_a'] = rw_grads
    G['rwkv_w2'], G['rwkv_a2'] = dw2p[:64], da2p[64:]
    G['rwkv_mu'] = dmixw[0] - dmixw[1]

    dp = jnp.concatenate([dp_rwkv, dqkv_raw, dz, dga, dgb, dab], axis=1)
    G['w_in_pad'] = _mm(u, dp, 'tn', "g_w_in", out_dtype=bf16)
    if late is None:
        du = _mm(dp, wp, 'nt', "d_u")
    else:
        du, *G['_arrived_w_in'] = _mm(dp, wp, 'nt', "d_u", side=(late['w_in_slabs'](G), True))
    (dx,), (G['norm1_g'],) = _pw_bwd("norm1_bwd", _rms_fn, [x], [g1], [(du,)], 512, add_to_first=dx1)
    return loss, dx, G


IN_WIDTH = OFF_AB + 8
PAD_ORDER = ((0, OFF_GATES), (OFF_GATES + 8, IN_WIDTH), (OFF_GATES, OFF_GATES + 8))


def _pad_w_in_shards(shards):
    width = shards[0].shape[1]
    parts = []
    for a, b in PAD_ORDER:
        for j, sh in enumerate(shards):
            lo, hi = max(a, j * width), min(b, (j + 1) * width)
            if lo < hi:
                parts.append(sh[:, lo - j * width:hi - j * width])
    return jnp.concatenate(parts + [jnp.zeros((shards[0].shape[0], W_AB - 8), shards[0].dtype)], axis=1)


def _unpad_cols(wp, lo, hi):
    parts, off = [], 0
    for a, b in PAD_ORDER:
        l, h = max(a, lo), min(b, hi)
        if l < h:
            parts.append((l, wp[:, off + l - a:off + h - a]))
        off += b - a
    parts.sort(key=lambda t: t[0])
    return parts[0][1] if len(parts) == 1 else jnp.concatenate([p for _, p in parts], axis=1)


BIG = ('w_in', 'rwkv_proj', 'gdn_proj', 'w_out', 'ffn_up', 'ffn_down')
SMALL_SHARDED = ('rwkv_w2', 'rwkv_a2', 'rwkv_g2', 'gdn_conv_w', 'ffn_conv_w')


def _rows128(shape):
    n = 1
    for d in shape:
        n *= d
    return -(-n // LANES)


def _pack128(arrays):
    parts = []
    for a in arrays:
        flat = a.reshape(-1)
        rows = _rows128(a.shape)
        parts.append(jnp.pad(flat, (0, rows * LANES - flat.shape[0])).reshape(rows, LANES))
    buf = jnp.concatenate(parts, axis=0)
    return jnp.pad(buf, ((0, -buf.shape[0] % HALO), (0, 0)))


def _unpack128(buf, shapes):
    out, off = [], 0
    for s in shapes:
        rows, n = _rows128(s), 1
        for d in s:
            n *= d
        out.append(buf[off:off + rows].reshape(-1)[:n].reshape(s))
        off += rows
    return out


def _param_tile(r, c):
    best = None
    for d in range(2 * HALO, r + 1, 2 * HALO):
        if r % d == 0 and d * c * 4 <= TILE_BYTES:
            best = d
    if best is not None or r * c * 4 <= TILE_BYTES:
        return (best if best is not None else r), c
    return r, 128


def _xy_exchange(name, bufs, scatter):
    n = len(bufs)

    def body(*refs):
        start, finish = _xy_copies(refs[:n], refs[n:2 * n], refs[2 * n:], scatter)
        start()
        finish()

    return pl.pallas_call(
        body, in_specs=[pl.BlockSpec(memory_space=pl.ANY)] * n, out_specs=[pl.BlockSpec(memory_space=pl.ANY)] * n,
        out_shape=_xy_out_shapes(bufs, scatter), scratch_shapes=_xy_sems(n, scatter), name=name)(*bufs)


def _sibling_exchange(name, bufs):
    n = len(bufs)

    def body(*refs):
        in_refs, out_refs, send_sems, recv_sems = refs[:n], refs[n:2 * n], refs[2 * n], refs[2 * n + 1]
        x, y, c = lax.axis_index("x"), lax.axis_index("y"), lax.axis_index("c")
        copies = [pltpu.make_async_remote_copy(
            src_ref=in_refs[a], dst_ref=out_refs[a], send_sem=send_sems.at[a], recv_sem=recv_sems.at[a],
            device_id=(x, y, 1 - c), device_id_type=pl.DeviceIdType.MESH) for a in range(n)]
        for cp in copies:
            cp.start()
        for cp in copies:
            cp.wait()

    return pl.pallas_call(
        body, in_specs=[pl.BlockSpec(memory_space=pl.ANY)] * n, out_specs=[pl.BlockSpec(memory_space=pl.ANY)] * n,
        out_shape=[jax.ShapeDtypeStruct(b.shape, b.dtype) for b in bufs],
        scratch_shapes=[pltpu.SemaphoreType.DMA((n,)), pltpu.SemaphoreType.DMA((n,))], name=name)(*bufs)


def _sum_slots(name, buf):
    _, R, L = buf.shape
    tr, tc = _param_tile(R, L)

    def body(b_ref, o_ref):
        part = lambda s: b_ref[s].astype(f32)
        o_ref[...] = ((part(0) + part(1)) + part(2)) + part(3)

    return pl.pallas_call(
        body, grid=(R // tr, L // tc),
        in_specs=[pl.BlockSpec((N_POS, tr, tc), lambda i, j: (0, i, j))],
        out_specs=pl.BlockSpec((tr, tc), lambda i, j: (i, j)),
        out_shape=jax.ShapeDtypeStruct((R, L), f32), name=name,
        compiler_params=_params("parallel", "parallel"))(buf)


def _adamw(name, w, ga, gb, m, v):
    R, L = w.shape
    tr, tc = _param_tile(R, L)
    c1 = 1.0 / (1.0 - ADAM_B1 ** ADAM_STEP)
    c2 = 1.0 / (1.0 - ADAM_B2 ** ADAM_STEP)

    def body(w_ref, ga_ref, gb_ref, m_ref, v_ref, g_out, d_out, m_out, v_out):
        g = ga_ref[...] + gb_ref[...]
        m_new = ADAM_B1 * m_ref[...] + (1.0 - ADAM_B1) * g
        v_new = ADAM_B2 * v_ref[...] + (1.0 - ADAM_B2) * (g * g)
        g_out[...] = g
        m_out[...] = m_new
        v_out[...] = v_new
        d_out[...] = -ADAM_LR * ((m_new * c1) / (jnp.sqrt(v_new * c2) + ADAM_EPS) + ADAM_WD * w_ref[...])

    spec = pl.BlockSpec((tr, tc), lambda i, j: (i, j))
    return pl.pallas_call(
        body, grid=(R // tr, L // tc), in_specs=[spec] * 5, out_specs=[spec] * 4,
        out_shape=[jax.ShapeDtypeStruct((R, L), f32)] * 4, name=name,
        compiler_params=_params("parallel", "parallel"))(w, ga, gb, m, v)


def _step(x, loss_target, P, M, V):
    shapes = {n: tuple(P[n].shape) for n in WEIGHTS}
    sh_shapes = [shapes[n] for n in SMALL_SHARDED]
    packed = SMALL_SHARDED + SMALL

    def whole(n, g):
        return g.reshape(-1, g.shape[2]) if n in ROW_SHARDED else jnp.concatenate([g[j] for j in range(N_POS)], axis=1)

    def slabs(G, n, dtype=f32):
        r, c = shapes[n]
        full = G[n].astype(dtype)
        if full.ndim == 3:
            return full
        return full.reshape(N_POS, r, c) if n in ROW_SHARDED else full.reshape(r, N_POS, c).transpose(1, 0, 2)

    g_w_in, g_small = _xy_exchange("gather_w_in", [P['w_in'].astype(bf16), _pack128([P[n] for n in SMALL_SHARDED])],
                                   scatter=False)
    W = {n: P[n] for n in SMALL}
    W['w_in_pad'] = _pad_w_in_shards([g_w_in[j] for j in range(N_POS)])
    per_pos = [_unpack128(g_small[j], sh_shapes) for j in range(N_POS)]
    for q, n in enumerate(SMALL_SHARDED):
        W[n] = jnp.concatenate([per_pos[j][q] for j in range(N_POS)], axis=1)
    groups = (('rwkv_proj', 'gdn_proj', 'ffn_up'), ('w_out', 'ffn_down'))
    late = dict(shards=[[P[n].astype(bf16) for n in grp] for grp in groups],
                assemble=lambda q, gathered: {n: whole(n, g) for n, g in zip(groups[q], gathered)},
                slabs=lambda G, q: [slabs(G, n, bf16) for n in groups[q]],
                w_in_slabs=lambda G: [jnp.stack([_unpad_cols(G['w_in_pad'], j * shapes['w_in'][1], (j + 1) * shapes['w_in'][1])
                                                 for j in range(N_POS)])])

    loss_rows, dx, G = _local_step(x, loss_target, W, late)
    arrived = {n: a for grp, got in zip(groups, G.pop('_arrived')) for n, a in zip(grp, got)}
    (arrived_w_in,) = G.pop('_arrived_w_in')
    G.pop('w_in_pad')

    small_slabs = jnp.stack([_pack128([slabs(G, n)[j] for n in SMALL_SHARDED] + [G[n] for n in SMALL]) for j in range(N_POS)])
    (arrived_small,) = _xy_exchange("scatter_small", [small_slabs], scatter=True)
    contributions = [arrived_w_in] + [arrived[n] for n in BIG[1:]] + [arrived_small]
    tags = list(BIG) + ['small']
    plane = [_sum_slots("sum_" + t, cbuf) for t, cbuf in zip(tags, contributions)]
    sibling = _sibling_exchange("sibling_grads", plane)

    out = {}
    names4 = ('grad', 'delta', 'new_m', 'new_v')
    for q, n in enumerate(BIG):
        for tag, t in zip(names4, _adamw("adamw_" + n, P[n], plane[q], sibling[q], M[n], V[n])):
            out[tag + '_' + n] = t
    small_out = _adamw("adamw_small", _pack128([P[n] for n in packed]), plane[-1], sibling[-1],
                       _pack128([M[n] for n in packed]), _pack128([V[n] for n in packed]))
    for tag, buf in zip(names4, small_out):
        for n, t in zip(packed, _unpack128(buf, [shapes[n] for n in packed])):
            out[tag + '_' + n] = t
    loss = lax.psum(loss_rows[0, 0], ("x", "y", "c"))
    return loss, dx, out


def kernel(x, norm1_g, w_in, rwkv_mu, rwkv_w0, rwkv_w2, rwkv_a0, rwkv_a2, rwkv_g2, rwkv_k_k, rwkv_k_a, rwkv_r_k, rwkv_ln_w, rwkv_ln_b, rwkv_proj, gdn_conv_w, gdn_a_log, gdn_dt_bias, gdn_norm_w, gdn_proj, w_out, norm2_g, ffn_up, ffn_conv_w, ffn_down, final_g, loss_target, m_norm1_g, m_w_in, m_rwkv_mu, m_rwkv_w0, m_rwkv_w2, m_rwkv_a0, m_rwkv_a2, m_rwkv_g2, m_rwkv_k_k, m_rwkv_k_a, m_rwkv_r_k, m_rwkv_ln_w, m_rwkv_ln_b, m_rwkv_proj, m_gdn_conv_w, m_gdn_a_log, m_gdn_dt_bias, m_gdn_norm_w, m_gdn_proj, m_w_out, m_norm2_g, m_ffn_up, m_ffn_conv_w, m_ffn_down, m_final_g, v_norm1_g, v_w_in, v_rwkv_mu, v_rwkv_w0, v_rwkv_w2, v_rwkv_a0, v_rwkv_a2, v_rwkv_g2, v_rwkv_k_k, v_rwkv_k_a, v_rwkv_r_k, v_rwkv_ln_w, v_rwkv_ln_b, v_rwkv_proj, v_gdn_conv_w, v_gdn_a_log, v_gdn_dt_bias, v_gdn_norm_w, v_gdn_proj, v_w_out, v_norm2_g, v_ffn_up, v_ffn_conv_w, v_ffn_down, v_final_g):
    weights = (norm1_g, w_in, rwkv_mu, rwkv_w0, rwkv_w2, rwkv_a0, rwkv_a2, rwkv_g2, rwkv_k_k, rwkv_k_a, rwkv_r_k, rwkv_ln_w,
               rwkv_ln_b, rwkv_proj, gdn_conv_w, gdn_a_log, gdn_dt_bias, gdn_norm_w, gdn_proj, w_out, norm2_g, ffn_up,
               ffn_conv_w, ffn_down, final_g)
    m_in = (m_norm1_g, m_w_in, m_rwkv_mu, m_rwkv_w0, m_rwkv_w2, m_rwkv_a0, m_rwkv_a2, m_rwkv_g2, m_rwkv_k_k, m_rwkv_k_a,
            m_rwkv_r_k, m_rwkv_ln_w, m_rwkv_ln_b, m_rwkv_proj, m_gdn_conv_w, m_gdn_a_log, m_gdn_dt_bias, m_gdn_norm_w,
            m_gdn_proj, m_w_out, m_norm2_g, m_ffn_up, m_ffn_conv_w, m_ffn_down, m_final_g)
    v_in = (v_norm1_g, v_w_in, v_rwkv_mu, v_rwkv_w0, v_rwkv_w2, v_rwkv_a0, v_rwkv_a2, v_rwkv_g2, v_rwkv_k_k, v_rwkv_k_a,
            v_rwkv_r_k, v_rwkv_ln_w, v_rwkv_ln_b, v_rwkv_proj, v_gdn_conv_w, v_gdn_a_log, v_gdn_dt_bias, v_gdn_norm_w,
            v_gdn_proj, v_w_out, v_norm2_g, v_ffn_up, v_ffn_conv_w, v_ffn_down, v_final_g)
    drop = lambda n, a: a if n == 'final_g' else a[0]
    P = {n: drop(n, a) for n, a in zip(WEIGHTS, weights)}
    M = {n: drop(n, a) for n, a in zip(WEIGHTS, m_in)}
    V = {n: drop(n, a) for n, a in zip(WEIGHTS, v_in)}
    loss, dx, out = _step(x[0], loss_target[0], P, M, V)
    lift = lambda n, a: a if n == 'final_g' else a[None]
    res = [loss, dx[None]]
    for tag in ('grad', 'delta', 'new_m', 'new_v'):
        res += [lift(n, out[tag + '_' + n]) for n in WEIGHTS]
    return tuple(res)
```

```python
import functools

import jax
import jax.numpy as jnp
from jax import lax
from jax.experimental import pallas as pl
from jax.experimental.pallas import tpu as pltpu

f32 = jnp.float32
bf16 = jnp.bfloat16

D_MODEL = 1024
RWKV_HEADS, RWKV_HD, RWKV_W = 8, 64, 512
GDN_HEADS, GDN_HD, GDN_W = 4, 128, 512
NORM_EPS, L2_EPS, GN_EPS = 1e-6, 1e-6, 64e-5
W_AB = 256
OFF_QKV, OFF_Z, OFF_GATES, OFF_AB = 1792, 3328, 3840, 5888
W_IN_PAD = OFF_AB + W_AB
WKV_CHUNK, WKV_PER_STEP = 64, 4
GDN_CHUNK, GDN_PER_STEP = 128, 4
HALO = 8
LANES = 128
TILE_BYTES = 1 << 20
VMEM_LIMIT = 56 * 1024 * 1024

ADAM_LR, ADAM_B1, ADAM_B2, ADAM_EPS, ADAM_WD, ADAM_STEP = 0.001, 0.9, 0.999, 1e-08, 0.01, 10

ROW_SHARDED = ('w_out', 'ffn_down')
SMALL = ('norm1_g', 'rwkv_mu', 'rwkv_w0', 'rwkv_a0', 'rwkv_k_k', 'rwkv_k_a', 'rwkv_r_k', 'rwkv_ln_w', 'rwkv_ln_b',
         'gdn_a_log', 'gdn_dt_bias', 'gdn_norm_w', 'norm2_g', 'final_g')
WEIGHTS = ('norm1_g', 'w_in', 'rwkv_mu', 'rwkv_w0', 'rwkv_w2', 'rwkv_a0', 'rwkv_a2', 'rwkv_g2', 'rwkv_k_k', 'rwkv_k_a',
           'rwkv_r_k', 'rwkv_ln_w', 'rwkv_ln_b', 'rwkv_proj', 'gdn_conv_w', 'gdn_a_log', 'gdn_dt_bias', 'gdn_norm_w',
           'gdn_proj', 'w_out', 'norm2_g', 'ffn_up', 'ffn_conv_w', 'ffn_down', 'final_g')


def _params(*sem):
    return pltpu.CompilerParams(dimension_semantics=sem, vmem_limit_bytes=VMEM_LIMIT)


def _tile(n, limit):
    if n <= limit:
        return n
    best = None
    for d in range(128, limit + 1, 128):
        if n % d == 0:
            best = d
    if best is None:
        raise ValueError(f"no tile for {n} under {limit}")
    return best


MM_BLOCK_BYTES = 6 << 20
MM_MAX_COLS = 1536


def _mm(a, b, mode, name, add=None, out_dtype=f32, side=None, col_slabs=None):
    if mode == 'nn':
        (M, K), N = a.shape, b.shape[1]
    elif mode == 'nt':
        (M, K), N = a.shape, b.shape[0]
    else:
        (K, M), N = a.shape, b.shape[1]
    tm = _tile(M, 1408)
    tk = _tile(K, min(2816, MM_BLOCK_BYTES // (tm * a.dtype.itemsize)))
    tn = _tile(N, max(128, min(MM_BLOCK_BYTES // (tk * b.dtype.itemsize), MM_BLOCK_BYTES // (tm * 4), MM_MAX_COLS) // 128 * 128))
    if col_slabs is not None:
        tn = N // col_slabs
    nk = K // tk
    grid = (M // tm, N // tn, nk)
    dn = {'nn': (((1,), (0,)), ((), ())), 'nt': (((1,), (1,)), ((), ())), 'tn': (((0,), (0,)), ((), ()))}[mode]
    n_add = 0 if add is None else 1
    n_side = 0 if side is None else len(side[0])

    def body(a_ref, b_ref, *rest):
        add_ref = rest[0] if add is not None else None
        side_in, rest = rest[n_add:n_add + n_side], rest[n_add + n_side:]
        o_ref, side_out, rest = rest[0], rest[1:1 + n_side], rest[1 + n_side:]
        acc_ref, rest = (rest[0], rest[1:]) if nk > 1 else (None, rest)
        ids = [pl.program_id(d) for d in range(3)]
        if side is not None:
            start, finish = _xy_copies(side_in, side_out, rest, side[1])
            pl.when((ids[0] == 0) & (ids[1] == 0) & (ids[2] == 0))(start)
        acc = lax.dot_general(a_ref[...].astype(bf16), b_ref[...].astype(bf16), dn, preferred_element_type=f32)
        if nk == 1:
            o_ref[...] = (acc + add_ref[...] if add is not None else acc).astype(out_dtype)
        else:
            k = ids[2]

            @pl.when(k == 0)
            def _():
                acc_ref[...] = acc + add_ref[...] if add is not None else acc

            @pl.when(k > 0)
            def _():
                acc_ref[...] += acc

            @pl.when(k == nk - 1)
            def _():
                o_ref[...] = acc_ref[...].astype(out_dtype)
        if side is not None:
            pl.when((ids[0] == grid[0] - 1) & (ids[1] == grid[1] - 1) & (ids[2] == nk - 1))(finish)

    a_spec = (pl.BlockSpec((tk, tm), lambda i, j, k: (k, i)) if mode == 'tn'
              else pl.BlockSpec((tm, tk), lambda i, j, k: (i, k)))
    b_spec = (pl.BlockSpec((tn, tk), lambda i, j, k: (j, k)) if mode == 'nt'
              else pl.BlockSpec((tk, tn), lambda i, j, k: (k, j)))
    o_spec = pl.BlockSpec((tm, tn), lambda i, j, k: (i, j))
    o_shape = jax.ShapeDtypeStruct((M, N), out_dtype)
    if col_slabs is not None:
        o_spec = pl.BlockSpec((None, tm, tn), lambda i, j, k: (j, i, 0))
        o_shape = jax.ShapeDtypeStruct((col_slabs, M, tn), out_dtype)
    any_spec = pl.BlockSpec(memory_space=pl.ANY)
    side_bufs = [] if side is None else list(side[0])
    ins, specs = [a, b], [a_spec, b_spec]
    if add is not None:
        ins.append(add)
        specs.append(o_spec)
    outs = pl.pallas_call(
        body, grid=grid, in_specs=specs + [any_spec] * n_side, out_specs=[o_spec] + [any_spec] * n_side,
        out_shape=[o_shape] + (_xy_out_shapes(side_bufs, side[1]) if side is not None else []),
        scratch_shapes=([pltpu.VMEM((tm, tn), f32)] if nk > 1 else []) + (_xy_sems(n_side, side[1]) if side is not None else []),
        name=name,
        compiler_params=_params(*(("arbitrary",) * 3 if side is not None else ("parallel", "parallel", "arbitrary"))))(
            *ins, *side_bufs)
    return list(outs) if side is not None else outs[0]


def _shift_down(cur, prev, s):
    if s == 0:
        return cur
    ext = jnp.concatenate([prev, cur], axis=0)
    return pltpu.roll(ext, s, 0)[HALO:]


def _shift_up(cur, nxt, s):
    if s == 0:
        return cur
    ext = jnp.concatenate([cur, nxt], axis=0)
    return pltpu.roll(ext, ext.shape[0] - s, 0)[:cur.shape[0]]


def _conv_apply(cur, prev, w_ref, shifted=None):
    taps = w_ref.shape[0]
    out = None
    for i in range(taps):
        s = taps - 1 - i
        term = (shifted[s] if shifted is not None else _shift_down(cur, prev, s)) * w_ref[pl.ds(i, 1), :]
        out = term if out is None else out + term
    return out


def _row_spec(tm, w, col=0):
    return pl.BlockSpec((tm, w), lambda i: (i, col))


def _cols(a, width, col):
    return (a, width, col)


def _row_of(r):
    return r if isinstance(r, tuple) else (r, r.shape[1], 0)


def _prev_spec(tm, w):
    return pl.BlockSpec((HALO, w), lambda i: (jnp.maximum(i * (tm // HALO) - 1, 0), 0))


def _next_spec(tm, w, T):
    return pl.BlockSpec((HALO, w), lambda i: (jnp.minimum((i + 1) * (tm // HALO), T // HALO - 1), 0))


def _full_spec(shape):
    return pl.BlockSpec(shape, lambda i: (0,) * len(shape))


def _pw_fwd(name, fn, rows, consts, out_widths, tm, conv_w=None, out_dtype=f32, strip=None):
    T = _row_of(rows[0])[0].shape[0]
    nr, nc = len(rows), len(consts)

    def body(*refs):
        i = pl.program_id(0)
        if strip is not None:
            for j in range(out_widths[0] // strip):
                sl = slice(strip * j, strip * (j + 1))
                outs = fn(*[r[:, sl] for r in refs[:nr + nc]])
                for o_ref, o in zip(refs[nr + nc:], outs):
                    o_ref[:, sl] = o.astype(out_dtype)
            return
        vals = [r[...] for r in refs[:nr]]
        p = nr
        if conv_w is not None:
            prev = jnp.where(i > 0, refs[p][...], 0.0)
            vals[0] = _conv_apply(vals[0], prev, refs[p + 1])
            p += 2
        cvals = [r[...] for r in refs[p:p + nc]]
        outs = fn(*vals, *cvals)
        for o_ref, o in zip(refs[p + nc:], outs):
            o_ref[...] = o.astype(out_dtype)

    ins = [_row_of(r)[0] for r in rows]
    specs = [_row_spec(tm, *_row_of(r)[1:]) for r in rows]
    if conv_w is not None:
        ins += [rows[0], conv_w]
        specs += [_prev_spec(tm, rows[0].shape[1]), _full_spec(conv_w.shape)]
    ins += list(consts)
    specs += [_full_spec(c.shape) for c in consts]
    outs = pl.pallas_call(
        body, grid=(T // tm,), in_specs=specs,
        out_specs=[_row_spec(tm, w) for w in out_widths],
        out_shape=[jax.ShapeDtypeStruct((T, w), out_dtype) for w in out_widths], name=name,
        compiler_params=_params("parallel"))(*ins)
    return outs


def _pw_bwd(name, fn, rows, consts, cots, tm, add_to_first=None, row_dtypes=None, strip=None):
    rows = [_row_of(r) for r in rows]
    T = rows[0][0].shape[0]
    nr, nc = len(rows), len(consts)
    flat_cots = [c for grp in cots for c in grp]
    row_dtypes = row_dtypes or [f32] * nr
    n_extra = 0 if add_to_first is None else 1
    width = rows[0][1]

    def body(*refs):
        i = pl.program_id(0)
        in_refs, cot_refs = refs[:nr + nc], refs[nr + nc:nr + nc + len(flat_cots)]
        extra_ref = refs[nr + nc + len(flat_cots)] if add_to_first is not None else None
        row_out = refs[nr + nc + len(flat_cots) + n_extra:][:nr]
        const_out = refs[nr + nc + len(flat_cots) + n_extra + nr:]

        @pl.when(i == 0)
        def _():
            for q in range(nc):
                const_out[q][...] = jnp.zeros_like(const_out[q])

        def part(sl):
            cot_vals, p = [], 0
            for grp in cots:
                acc = cot_refs[p][:, sl]
                for q in range(1, len(grp)):
                    acc = acc + cot_refs[p + q][:, sl]
                p += len(grp)
                cot_vals.append(acc)
            _, vjp = jax.vjp(fn, *[r[:, sl] for r in in_refs])
            grads = vjp(tuple(cot_vals))
            for q in range(nr):
                g = grads[q]
                if q == 0 and extra_ref is not None:
                    g = g + extra_ref[:, sl]
                row_out[q][:, sl] = g.astype(row_dtypes[q])
            for q in range(nc):
                const_out[q][:, sl] += grads[nr + q]

        if strip is None:
            part(slice(None))
        else:
            for j in range(width // strip):
                part(slice(strip * j, strip * (j + 1)))

    ins = [r[0] for r in rows] + list(consts) + flat_cots
    specs = ([_row_spec(tm, r[1], r[2]) for r in rows] + [_full_spec(c.shape) for c in consts]
             + [_row_spec(tm, c.shape[1]) for c in flat_cots])
    if add_to_first is not None:
        ins.append(add_to_first)
        specs.append(_row_spec(tm, add_to_first.shape[1]))
    out_shapes = ([jax.ShapeDtypeStruct((T, r[1]), d) for r, d in zip(rows, row_dtypes)]
                  + [jax.ShapeDtypeStruct(c.shape, f32) for c in consts])
    out_specs = [_row_spec(tm, r[1]) for r in rows] + [_full_spec(c.shape) for c in consts]
    outs = pl.pallas_call(
        body, grid=(T // tm,), in_specs=specs, out_specs=out_specs, out_shape=out_shapes, name=name,
        compiler_params=_params("arbitrary"))(*ins)
    return list(outs[:nr]), list(outs[nr:])


def _pw_conv_bwd(name, fn, rows, consts, cots, conv_w, tm, row_dtypes=None):
    T, W0 = rows[0].shape
    nr, nc = len(rows), len(consts)
    taps = conv_w.shape[0]
    nblk = T // tm
    flat_cots = [c for grp in cots for c in grp]
    row_dtypes = row_dtypes or [f32] * nr

    def body(*refs):
        i = pl.program_id(0)
        p = 0
        cur = [r[...] for r in refs[p:p + nr]]; p += nr
        nxt = [r[...] for r in refs[p:p + nr]]; p += nr
        prev = jnp.where(i > 0, refs[p][...], 0.0); p += 1
        w_ref = refs[p]; p += 1
        cvals = [r[...] for r in refs[p:p + nc]]; p += nc

        def summed(p0):
            out, q = [], p0
            for grp in cots:
                acc = refs[q][...]
                for t in range(1, len(grp)):
                    acc = acc + refs[q + t][...]
                q += len(grp)
                out.append(acc)
            return out, q

        cot_cur, p = summed(p)
        cot_nxt, p = summed(p)
        row_out, dw_ref, const_out = refs[p:p + nr], refs[p + nr], refs[p + nr + 1:]

        x_cur = cur[0]
        x_down = [_shift_down(x_cur, prev, s_) for s_ in range(taps)]
        _, vjp = jax.vjp(fn, _conv_apply(x_cur, prev, w_ref, x_down), *cur[1:], *cvals)
        grads = vjp(tuple(cot_cur))
        _, vjp_n = jax.vjp(fn, _conv_apply(nxt[0], x_cur[tm - HALO:], w_ref), *nxt[1:], *cvals)
        dc_n = jnp.where(i < nblk - 1, vjp_n(tuple(cot_nxt))[0], 0.0)
        dc = grads[0]

        @pl.when(i == 0)
        def _():
            dw_ref[...] = jnp.zeros_like(dw_ref)
            for q in range(nc):
                const_out[q][...] = jnp.zeros_like(const_out[q])

        dx = None
        for k in range(taps):
            s_ = taps - 1 - k
            term = _shift_up(dc, dc_n, s_) * w_ref[pl.ds(k, 1), :]
            dx = term if dx is None else dx + term
            dw_ref[pl.ds(k, 1), :] += jnp.sum(dc * x_down[s_], axis=0, keepdims=True)
        row_out[0][...] = dx.astype(row_dtypes[0])
        for q in range(1, nr):
            row_out[q][...] = grads[q].astype(row_dtypes[q])
        for q in range(nc):
            const_out[q][...] += grads[nr + q]

    ins = list(rows) + list(rows) + [rows[0], conv_w] + list(consts) + flat_cots + flat_cots
    specs = ([_row_spec(tm, r.shape[1]) for r in rows] + [_next_spec(tm, r.shape[1], T) for r in rows]
             + [_prev_spec(tm, W0), _full_spec(conv_w.shape)] + [_full_spec(c.shape) for c in consts]
             + [_row_spec(tm, c.shape[1]) for c in flat_cots] + [_next_spec(tm, c.shape[1], T) for c in flat_cots])
    out_shapes = ([jax.ShapeDtypeStruct(r.shape, d) for r, d in zip(rows, row_dtypes)]
                  + [jax.ShapeDtypeStruct(conv_w.shape, f32)] + [jax.ShapeDtypeStruct(c.shape, f32) for c in consts])
    out_specs = ([_row_spec(tm, r.shape[1]) for r in rows] + [_full_spec(conv_w.shape)]
                 + [_full_spec(c.shape) for c in consts])
    outs = pl.pallas_call(
        body, grid=(nblk,), in_specs=specs, out_specs=out_specs, out_shape=out_shapes, name=name,
        compiler_params=_params("arbitrary"))(*ins)
    return list(outs[:nr]), outs[nr], list(outs[nr + 1:])


def _sigmoid(x):
    return 0.5 * jnp.tanh(0.5 * x) + 0.5


def _softplus(x):
    return jnp.maximum(x, 0.0) + jnp.log(1.0 + jnp.exp(jnp.minimum(x, -x)))


def _seg_sum_impl(x, seg):
    w = x.shape[-1]
    r = lax.broadcasted_iota(jnp.int32, (w, w), 0) // seg
    c = lax.broadcasted_iota(jnp.int32, (w, w), 1) // seg
    ones = (r == c).astype(bf16)
    hi = x.astype(bf16)
    lo = (x - hi.astype(f32)).astype(bf16)
    return (jnp.dot(hi, ones, preferred_element_type=f32) + jnp.dot(lo, ones, preferred_element_type=f32))


@functools.partial(jax.custom_vjp, nondiff_argnums=(1,))
def _seg_sum(x, seg):
    return _seg_sum_impl(x, seg)


_seg_sum.defvjp(lambda x, seg: (_seg_sum_impl(x, seg), None), lambda seg, _, g: (_seg_sum_impl(g, seg),))


def _rms(x, g):
    return x * lax.rsqrt(jnp.mean(x * x, axis=-1, keepdims=True) + NORM_EPS) * g


def _rms_fn(x, g):
    return (_rms(x, g),)


def _loss_rows(x2, tgt, g):
    e = _rms(x2, g) - tgt
    return 0.5 * jnp.sum(e * e, axis=-1, keepdims=True) * (1.0 / D_MODEL)


@jax.custom_vjp
def _dot_lo(a, b):
    return jnp.dot(a.astype(bf16), b.astype(bf16), preferred_element_type=f32)


def _dot_lo_bwd(ab, g):
    a, b = ab
    gl = g.astype(bf16)
    return (lax.dot_general(gl, b.astype(bf16), (((1,), (1,)), ((), ())), preferred_element_type=f32),
            lax.dot_general(a.astype(bf16), gl, (((0,), (0,)), ((), ())), preferred_element_type=f32))


_dot_lo.defvjp(lambda a, b: (_dot_lo(a, b), (a, b)), _dot_lo_bwd)


def _rwkv_prep_fn(ps, w0, w2p, a0, a2p, g2, k_k, k_a):
    r, k, v = ps[:, 0:512], ps[:, 512:1024], ps[:, 1024:1536]
    wa, gl = ps[:, 1536:1664], ps[:, 1664:1792]
    z = w0 + _dot_lo(jnp.tanh(wa), w2p)
    w_log = -_softplus(-z) - 0.5
    lw = -jnp.exp(w_log)
    a = _sigmoid(a0 + _dot_lo(wa, a2p))
    g = _dot_lo(_sigmoid(gl), g2)
    kx = k * k_k
    kk = kx * lax.rsqrt(_seg_sum(kx * kx, RWKV_HD) + L2_EPS)
    k2 = k * (1.0 + (a - 1.0) * k_a)
    return r, lw, k2, v, -kk, kk * a, g


def _rwkv_post_fn(y, r, k2, v, g, ln_w, ln_b, rk):
    mean = _seg_sum(y, RWKV_HD) * (1.0 / RWKV_HD)
    yc = y - mean
    var = _seg_sum(yc * yc, RWKV_HD) * (1.0 / RWKV_HD)
    yn = yc * lax.rsqrt(var + GN_EPS) * ln_w + ln_b
    bonus = _seg_sum(r * k2 * rk, RWKV_HD) * v
    return ((yn + bonus) * g,)


def _gdn_prep_fn(cq, ck, cv):
    silu = lambda c: c * _sigmoid(c)
    q, k = silu(cq), silu(ck)
    q = q * lax.rsqrt(jnp.sum(q * q, axis=-1, keepdims=True) + L2_EPS) * (GDN_HD ** -0.5)
    k = k * lax.rsqrt(jnp.sum(k * k, axis=-1, keepdims=True) + L2_EPS)
    return q, k, silu(cv)


def _gdn_gate_fn(ab, al_p, dt_p):
    lane = lax.broadcasted_iota(jnp.int32, ab.shape, 1)
    gpart = -jnp.exp(al_p) * _softplus(ab + dt_p)
    return (jnp.where(lane < GDN_HEADS, gpart, jnp.where(lane < 2 * GDN_HEADS, _sigmoid(ab), 0.0)),)


def _gdn_post_fn(o, z, nw):
    ms = _seg_sum(o * o, GDN_HD) * (1.0 / GDN_HD)
    return (o * lax.rsqrt(ms + NORM_EPS) * nw * (z * _sigmoid(z)),)


def _mix_fn(ga, gb, ya, yb):
    return (_sigmoid(ga) * ya + _sigmoid(gb) * yb,)


STRIP = 128


def _strip_conv(ref, prev_ref, w_ref, sl, first, taps):
    cur = ref[:, sl]
    prev = jnp.where(first, 0.0, prev_ref[:, sl])
    down = [_shift_down(cur, prev, s) for s in range(taps)]
    conv = None
    for k in range(taps):
        term = down[taps - 1 - k] * w_ref[pl.ds(k, 1), sl]
        conv = term if conv is None else conv + term
    return cur, down, conv


def _group_fwd(name, fn, x, w, shared_cols, group_cols, consts, n_out, tm):
    T, W = x.shape
    taps = w.shape[0]
    n_groups = len(group_cols)
    nc = len(consts)

    def body(x_ref, xp_ref, w_ref, *refs):
        const_refs, out_refs = refs[:nc], refs[nc:]
        first = pl.program_id(0) == 0
        shared = [_strip_conv(x_ref, xp_ref, w_ref, sl, first, taps)[2] for sl in shared_cols]
        for j, cols in enumerate(group_cols):
            sl = slice(STRIP * j, STRIP * (j + 1))
            convs = [_strip_conv(x_ref, xp_ref, w_ref, c, first, taps)[2] for c in cols]
            outs = fn(*convs, *shared, *[c[:, sl] for c in const_refs])
            for o_ref, o in zip(out_refs, outs):
                o_ref[:, sl] = o

    return pl.pallas_call(
        body, grid=(T // tm,),
        in_specs=[_row_spec(tm, W), _prev_spec(tm, W), _full_spec(w.shape)] + [_full_spec(c.shape) for c in consts],
        out_specs=[_row_spec(tm, STRIP * n_groups)] * n_out,
        out_shape=[jax.ShapeDtypeStruct((T, STRIP * n_groups), f32)] * n_out, name=name,
        compiler_params=_params("parallel"))(x, x, w, *consts)


def _group_bwd(name, fn, x, w, shared_cols, group_cols, consts, cots, tm):
    T, W = x.shape
    taps = w.shape[0]
    nblk = T // tm
    nc, ns = len(consts), len(shared_cols)
    flat_cots = [c for grp in cots for c in grp]
    n_cot = len(flat_cots)

    def body(x_ref, xp_ref, xn_ref, w_ref, *refs):
        const_refs, refs = refs[:nc], refs[nc:]
        cot_refs, cotn_refs, refs = refs[:n_cot], refs[n_cot:2 * n_cot], refs[2 * n_cot:]
        dx_ref, dw_ref, const_out = refs[0], refs[1], refs[2:]
        i = pl.program_id(0)
        first, last = i == 0, i == nblk - 1

        @pl.when(first)
        def _():
            dw_ref[...] = jnp.zeros_like(dw_ref)
            for q in range(nc):
                const_out[q][...] = jnp.zeros_like(const_out[q])

        def convs_of(sl):
            cur, down, conv = _strip_conv(x_ref, xp_ref, w_ref, sl, first, taps)
            nxt, conv_n = xn_ref[:, sl], None
            for k in range(taps):
                term = _shift_down(nxt, cur[tm - HALO:], taps - 1 - k) * w_ref[pl.ds(k, 1), sl]
                conv_n = term if conv_n is None else conv_n + term
            return down, conv, conv_n

        def conv_back(sl, down, dc, dc_n):
            dx = None
            for k in range(taps):
                s_ = taps - 1 - k
                term = _shift_up(dc, dc_n, s_) * w_ref[pl.ds(k, 1), sl]
                dx = term if dx is None else dx + term
                dw_ref[pl.ds(k, 1), sl] += jnp.sum(dc * down[s_], axis=0, keepdims=True)
            dx_ref[:, sl] = dx.astype(dx_ref.dtype)

        def summed(refs_, sl, mask):
            out, p = [], 0
            for grp in cots:
                acc = refs_[p][:, sl]
                for t in range(1, len(grp)):
                    acc = acc + refs_[p + t][:, sl]
                p += len(grp)
                out.append(jnp.where(last, 0.0, acc) if mask else acc)
            return tuple(out)

        shared = [convs_of(sl) for sl in shared_cols]
        d_shared, d_shared_n = [None] * ns, [None] * ns
        for j, cols in enumerate(group_cols):
            sl = slice(STRIP * j, STRIP * (j + 1))
            mine = [convs_of(c) for c in cols]
            cj = [c[:, sl] for c in const_refs]
            _, vjp = jax.vjp(fn, *[m[1] for m in mine], *[m[1] for m in shared], *cj)
            grads = vjp(summed(cot_refs, sl, False))
            _, vjp_n = jax.vjp(fn, *[m[2] for m in mine], *[m[2] for m in shared], *cj)
            grads_n = vjp_n(summed(cotn_refs, sl, True))
            for q, c in enumerate(cols):
                conv_back(c, mine[q][0], grads[q], grads_n[q])
            for q in range(ns):
                g, gn = grads[len(cols) + q], grads_n[len(cols) + q]
                d_shared[q] = g if d_shared[q] is None else d_shared[q] + g
                d_shared_n[q] = gn if d_shared_n[q] is None else d_shared_n[q] + gn
            for q in range(nc):
                const_out[q][:, sl] += grads[len(cols) + ns + q]
        for q, c in enumerate(shared_cols):
            conv_back(c, shared[q][0], d_shared[q], d_shared_n[q])

    outs = pl.pallas_call(
        body, grid=(nblk,),
        in_specs=[_row_spec(tm, W), _prev_spec(tm, W), _next_spec(tm, W, T), _full_spec(w.shape)]
        + [_full_spec(c.shape) for c in consts] + [_row_spec(tm, c.shape[1]) for c in flat_cots]
        + [_next_spec(tm, c.shape[1], T) for c in flat_cots],
        out_specs=[_row_spec(tm, W), _full_spec(w.shape)] + [_full_spec(c.shape) for c in consts],
        out_shape=[jax.ShapeDtypeStruct((T, W), bf16), jax.ShapeDtypeStruct(w.shape, f32)]
        + [jax.ShapeDtypeStruct(c.shape, f32) for c in consts], name=name,
        compiler_params=_params("arbitrary"))(x, x, x, w, *consts, *flat_cots, *flat_cots)
    return outs[0], outs[1], list(outs[2:])


def _ffn_strip_fn(cg, cu):
    return cg * _sigmoid(cg) * cu


def _ffn_act_fwd(h, w, tm):
    T, W2 = h.shape
    H = W2 // 2
    taps = w.shape[0]

    def body(h_ref, hp_ref, w_ref, o_ref):
        first = pl.program_id(0) == 0
        for j in range(H // STRIP):
            gs, us = slice(STRIP * j, STRIP * (j + 1)), slice(H + STRIP * j, H + STRIP * (j + 1))
            cg = _strip_conv(h_ref, hp_ref, w_ref, gs, first, taps)[2]
            cu = _strip_conv(h_ref, hp_ref, w_ref, us, first, taps)[2]
            o_ref[:, gs] = _ffn_strip_fn(cg, cu).astype(o_ref.dtype)

    return pl.pallas_call(
        body, grid=(T // tm,), in_specs=[_row_spec(tm, W2), _prev_spec(tm, W2), _full_spec(w.shape)],
        out_specs=_row_spec(tm, H), out_shape=jax.ShapeDtypeStruct((T, H), bf16), name="ffn_act",
        compiler_params=_params("parallel"))(h, h, w)


def _ffn_act_bwd(h, dact, w, tm):
    T, W2 = h.shape
    H = W2 // 2
    taps = w.shape[0]
    nblk = T // tm

    def body(h_ref, hp_ref, hn_ref, d_ref, dn_ref, w_ref, dh_ref, dw_ref):
        i = pl.program_id(0)
        first, last = i == 0, i == nblk - 1

        @pl.when(first)
        def _():
            dw_ref[...] = jnp.zeros_like(dw_ref)

        for j in range(H // STRIP):
            gs, us = slice(STRIP * j, STRIP * (j + 1)), slice(H + STRIP * j, H + STRIP * (j + 1))
            parts = {}
            for name, sl in (('g', gs), ('u', us)):
                cur, down, conv = _strip_conv(h_ref, hp_ref, w_ref, sl, first, taps)
                nxt = hn_ref[:, sl]
                conv_n = None
                for k in range(taps):
                    term = _shift_down(nxt, cur[tm - HALO:], taps - 1 - k) * w_ref[pl.ds(k, 1), sl]
                    conv_n = term if conv_n is None else conv_n + term
                parts[name] = (down, conv, conv_n)
            _, vjp = jax.vjp(_ffn_strip_fn, parts['g'][1], parts['u'][1])
            dcs = vjp(d_ref[:, gs])
            _, vjp_n = jax.vjp(_ffn_strip_fn, parts['g'][2], parts['u'][2])
            dcs_n = vjp_n(jnp.where(last, 0.0, dn_ref[:, gs]))
            for (name, sl), dc, dc_n in zip((('g', gs), ('u', us)), dcs, dcs_n):
                down = parts[name][0]
                dx = None
                for k in range(taps):
                    s_ = taps - 1 - k
                    term = _shift_up(dc, dc_n, s_) * w_ref[pl.ds(k, 1), sl]
                    dx = term if dx is None else dx + term
                    dw_ref[pl.ds(k, 1), sl] += jnp.sum(dc * down[s_], axis=0, keepdims=True)
                dh_ref[:, sl] = dx.astype(dh_ref.dtype)

    return pl.pallas_call(
        body, grid=(nblk,),
        in_specs=[_row_spec(tm, W2), _prev_spec(tm, W2), _next_spec(tm, W2, T), _row_spec(tm, H), _next_spec(tm, H, T),
                  _full_spec(w.shape)],
        out_specs=[_row_spec(tm, W2), _full_spec(w.shape)],
        out_shape=[jax.ShapeDtypeStruct((T, W2), bf16), jax.ShapeDtypeStruct(w.shape, f32)], name="ffn_act_bwd",
        compiler_params=_params("arbitrary"))(h, h, h, dact, dact, w)


N_POS = 4


def _xy_out_shapes(bufs, scatter):
    return [jax.ShapeDtypeStruct((N_POS,) + tuple(b.shape[1:] if scatter else b.shape), b.dtype) for b in bufs]


def _xy_sems(n, scatter):
    sems = [pltpu.SemaphoreType.DMA((3 * n,)), pltpu.SemaphoreType.DMA((3 * n,)), pltpu.SemaphoreType.DMA((n,))]
    return sems if scatter else sems + [pltpu.SemaphoreType.DMA((3 * n,)), pltpu.SemaphoreType.DMA((3 * n,))]


def _xy_copies(in_refs, out_refs, sems, scatter):
    n = len(in_refs)
    send_sems, recv_sems, local_sems = sems[:3]

    def place():
        x, y, c = lax.axis_index("x"), lax.axis_index("y"), lax.axis_index("c")
        return x, y, c, 2 * x + y, [(1 - x, y), (x, 1 - y), (1 - x, 1 - y)]

    def half(ref, a, which):
        rows = in_refs[a].shape[0] // 2
        return ref.at[pl.ds(pl.multiple_of(which * rows, HALO), rows)]

    def ici(a, k, src, dst, peer, c):
        return pltpu.make_async_remote_copy(
            src_ref=src, dst_ref=dst, send_sem=send_sems.at[3 * a + k], recv_sem=recv_sems.at[3 * a + k],
            device_id=(peer[0], peer[1], c), device_id_type=pl.DeviceIdType.MESH)

    def outgoing():
        x, y, c, me, peers = place()
        own = [pltpu.make_async_copy(in_refs[a].at[me] if scatter else in_refs[a], out_refs[a].at[me], local_sems.at[a])
               for a in range(n)]
        if scatter:
            sends = [ici(a, k, in_refs[a].at[2 * p[0] + p[1]], out_refs[a].at[me], p, c)
                     for a in range(n) for k, p in enumerate(peers)]
        else:
            sends = [ici(a, k, half(in_refs[a], a, c), half(out_refs[a].at[me], a, c), p, c)
                     for a in range(n) for k, p in enumerate(peers)]
        return own, sends

    def arrivals():
        x, y, c, me, peers = place()
        if scatter:
            return [ici(a, k, in_refs[a].at[me], out_refs[a].at[2 * p[0] + p[1]], p, c)
                    for a in range(n) for k, p in enumerate(peers)]
        return [ici(a, k, half(in_refs[a], a, c), half(out_refs[a].at[2 * p[0] + p[1]], a, c), p, c)
                for a in range(n) for k, p in enumerate(peers)]

    def to_sibling(mine):
        x, y, c, me, peers = place()
        which = c if mine else 1 - c
        return [pltpu.make_async_remote_copy(
            src_ref=half(out_refs[a].at[2 * p[0] + p[1]], a, which), dst_ref=half(out_refs[a].at[2 * p[0] + p[1]], a, which),
            send_sem=sems[3].at[3 * a + k], recv_sem=sems[4].at[3 * a + k],
            device_id=(x, y, 1 - c), device_id_type=pl.DeviceIdType.MESH) for a in range(n) for k, p in enumerate(peers)]

    def start():
        own, sends = outgoing()
        for cp in own + sends:
            cp.start()

    def finish():
        if scatter:
            for cp in arrivals():
                cp.wait_recv()
        else:
            passed = to_sibling(True)
            for cp, fwd in zip(arrivals(), passed):
                cp.wait_recv()
                fwd.start()
            for cp in to_sibling(False):
                cp.wait_recv()
            for fwd in passed:
                fwd.wait_send()
        own, sends = outgoing()
        for cp in sends:
            cp.wait_send()
        for cp in own:
            cp.wait()

    return start, finish


_NN, _NT, _TN = 'hcs,hsd->hcd', 'hcd,hsd->hcs', 'hcd,hce->hde'


def _lo(spec, a, b):
    return jnp.einsum(spec, a.astype(bf16), b.astype(bf16), preferred_element_type=f32)


@jax.custom_vjp
def _bmm(a, b):
    return _lo(_NN, a, b)


_bmm.defvjp(lambda a, b: (_lo(_NN, a, b), (a, b)), lambda ab, g: (_lo(_NT, g, ab[1]), _lo(_TN, ab[0], g)))


@jax.custom_vjp
def _bmm_nt(a, b):
    return _lo(_NT, a, b)


_bmm_nt.defvjp(lambda a, b: (_lo(_NT, a, b), (a, b)), lambda ab, g: (_lo(_NN, g, ab[1]), _lo(_TN, g, ab[0])))


@jax.custom_vjp
def _bmm_tn(a, b):
    return _lo(_TN, a, b)


_bmm_tn.defvjp(lambda a, b: (_lo(_TN, a, b), (a, b)), lambda ab, g: (_lo(_NT, ab[1], g), _lo(_NN, ab[0], g)))


def _masks(H, C):
    row = lax.broadcasted_iota(jnp.int32, (H, C, C), 1)
    col = lax.broadcasted_iota(jnp.int32, (H, C, C), 2)
    return row, col


def _tri_inv_impl(L):
    H, C, _ = L.shape
    row, col = _masks(H, C)
    eye = (row == col).astype(f32)
    base = 16
    same = (row // base) == (col // base)
    Ld = jnp.where(same, L, 0.0)
    X = -Ld
    inv = eye + X
    for _ in range(3):
        X = _bmm(X, X)
        inv = _bmm(inv, eye + X)
    if C == base:
        return inv
    N = _bmm(inv, L - Ld)
    out = eye - N
    levels = C // base
    P = N
    span = 2
    while span < levels:
        P = _bmm(P, P)
        out = _bmm(out, eye + P)
        span *= 2
    return _bmm(out, inv)


@jax.custom_vjp
def _tri_inv(L):
    return _tri_inv_impl(L)


def _tri_inv_fwd(L):
    T = _tri_inv_impl(L)
    return T, T


def _tri_inv_bwd(T, dT):
    return (-_bmm_nt(_bmm_tn(T, dT), T),)


_tri_inv.defvjp(_tri_inv_fwd, _tri_inv_bwd)


@jax.custom_vjp
def _tri_inv_known(L, T):
    return T


_tri_inv_known.defvjp(lambda L, T: (T, T), lambda T, dT: (_tri_inv_bwd(T, dT)[0], jnp.zeros_like(T)))


def _cumsum_impl(x, reverse):
    C = x.shape[1]
    row = lax.broadcasted_iota(jnp.int32, x.shape, 1)
    s = 1
    while s < C:
        if reverse:
            x = x + jnp.where(row < C - s, pltpu.roll(x, C - s, 1), 0.0)
        else:
            x = x + jnp.where(row >= s, pltpu.roll(x, s, 1), 0.0)
        s *= 2
    return x


@jax.custom_vjp
def _cumsum(x):
    return _cumsum_impl(x, False)


_cumsum.defvjp(lambda x: (_cumsum_impl(x, False), None), lambda _, g: (_cumsum_impl(g, True),))


def _wkv_prep(r, lw, k, v, a, b, inv=None):
    lane = lax.broadcasted_iota(jnp.int32, (r.shape[0], 128), 1)
    low = lane < RWKV_HD

    def heads(t):
        out = []
        for p in range(RWKV_HEADS // 2):
            pair = t[:, 128 * p:128 * (p + 1)]
            out += [jnp.where(low, pair, 0.0), jnp.where(low, 0.0, pair)]
        return jnp.concatenate([t[None] for t in out], axis=0)

    r, lw, k, v, a, b = [heads(t) for t in (r, lw, k, v, a, b)]
    H, C, D = r.shape
    row, col = _masks(H, C)
    incl, strict = row >= col, row > col
    cw = _cumsum(lw)
    cwp = cw - lw
    cwl = jnp.sum(lw, axis=1, keepdims=True)
    en = jnp.exp(-cw)
    at, rt, bt, kt = a * jnp.exp(cwp), r * jnp.exp(cw), b * en, k * en
    Lab = -jnp.where(strict, _bmm_nt(at, bt), 0.0)
    Tm = _tri_inv(Lab) if inv is None else _tri_inv_known(Lab, inv)
    ar = jnp.concatenate([at, rt], axis=1)
    gram = _bmm_nt(ar, jnp.concatenate([bt, kt], axis=1))
    row2 = lax.broadcasted_iota(jnp.int32, (H, 2 * C, 2 * C), 1)
    col2 = lax.broadcasted_iota(jnp.int32, (H, 2 * C, 2 * C), 2) % C
    gram = jnp.where(((row2 < C) & (row2 > col2)) | ((row2 >= C) & (row2 - C >= col2)), gram, 0.0)
    a_bk, r_bk = gram[:, :C], gram[:, C:]
    lak_v = _bmm(a_bk, jnp.concatenate([jnp.zeros_like(v), v], axis=1))
    ed = jnp.exp(cwl - cw)
    zdec = jnp.swapaxes(jnp.broadcast_to(jnp.exp(cwl), (H, D, D)), 1, 2)
    return (ar, Tm, lak_v, r_bk, jnp.concatenate([b * ed, k * ed], axis=1), zdec, v), Tm


def _wkv_step(Z, ar, Tm, lak_v, r_bk, bk_d, zdec, v):
    C = Tm.shape[1]
    ar_z = _bmm(ar, Z)
    uv = jnp.concatenate([_bmm(Tm, ar_z[:, :C] + lak_v), v], axis=1)
    y = ar_z[:, C:] + _bmm(r_bk, uv)
    Z1 = Z * zdec + _bmm_tn(bk_d, uv)
    return jnp.concatenate([y[2 * p] + y[2 * p + 1] for p in range(RWKV_HEADS // 2)], axis=1), Z1


def _split3(x):
    hi = x.astype(bf16)
    mid = (x - hi.astype(f32)).astype(bf16)
    lo = (x - hi.astype(f32) - mid.astype(f32)).astype(bf16)
    return hi, mid, lo


@jax.custom_vjp
def _spread(x, sel):
    return sum(jnp.dot(t, sel, preferred_element_type=f32) for t in _split3(x))


def _spread_bwd(sel, g):
    dn = (((1,), (1,)), ((), ()))
    return sum(lax.dot_general(t, sel, dn, preferred_element_type=f32) for t in _split3(g)), None


_spread.defvjp(lambda x, sel: (_spread(x, sel), sel), _spread_bwd)


def _gdn_prep(q, k, v, gbeta, inv=None):
    heads = lambda t: jnp.concatenate([t[None, :, GDN_HD * h:GDN_HD * (h + 1)] for h in range(GDN_HEADS)], axis=0)
    src = lax.broadcasted_iota(jnp.int32, (W_AB, 2 * GDN_W), 0)
    dst = lax.broadcasted_iota(jnp.int32, (W_AB, 2 * GDN_W), 1) // GDN_HD
    spread = _spread(gbeta, (src == dst).astype(bf16))
    q, k, v, g, beta = heads(q), heads(k), heads(v), heads(spread[:, :GDN_W]), heads(spread[:, GDN_W:])
    H, C, D = q.shape
    row, col = _masks(H, C)
    incl, strict = row >= col, row > col
    gc = _cumsum(g)
    diff = gc - jnp.swapaxes(gc, 1, 2)
    decay = jnp.where(incl, jnp.exp(jnp.where(incl, diff, 0.0)), 0.0)
    gl = jnp.sum(g, axis=1, keepdims=True)
    kb, vb = k * beta, v * beta
    gram = _bmm_nt(jnp.concatenate([kb, q], axis=1), k)
    L = jnp.where(strict, gram[:, :C] * decay, 0.0)
    attn = jnp.where(incl, gram[:, C:] * decay, 0.0)
    egc = jnp.exp(gc)
    Tm = _tri_inv(L) if inv is None else _tri_inv_known(L, inv)
    t_vk = _bmm(Tm, jnp.concatenate([vb, kb * egc], axis=2))
    return (t_vk[:, :, :D], jnp.concatenate([t_vk[:, :, D:], q * egc], axis=1), attn, k * jnp.exp(gl - gc), jnp.exp(gl)), Tm


def _gdn_step(S, u, wq, attn, ke, sdec):
    C = u.shape[1]
    wq_s = _bmm(wq, S)
    v_new = u - wq_s[:, :C]
    o = wq_s[:, C:] + _bmm(attn, v_new)
    S1 = S * sdec + _bmm_tn(ke, v_new)
    return jnp.concatenate([o[h] for h in range(GDN_HEADS)], axis=1), S1


def _scan_fwd(name, fns, ins, C, H, dh, w_out, per_step, side=None):
    prep, step = fns
    T = ins[0].shape[0]
    n_in = len(ins)
    blk = C * per_step
    nblk = T // blk
    n_side = 0 if side is None else len(side[0])

    def body(*refs):
        in_refs, refs = refs[:n_in], refs[n_in:]
        side_in, refs = refs[:n_side], refs[n_side:]
        y_ref, zs_ref, inv_ref, refs = refs[0], refs[1], refs[2], refs[3:]
        side_out, refs = refs[:n_side], refs[n_side:]
        z_scr = refs[0]
        if side is not None:
            start, finish = _xy_copies(side_in, side_out, refs[1:], side[1])
            pl.when(pl.program_id(0) == 0)(start)

        @pl.when(pl.program_id(0) == 0)
        def _():
            z_scr[...] = jnp.zeros_like(z_scr)

        rows = [slice(C * j, C * (j + 1)) for j in range(per_step)]
        prepped = [prep(*[r[rw, :] for r in in_refs]) for rw in rows]
        Z = z_scr[...]
        for j, rw in enumerate(rows):
            zs_ref[j] = Z
            inv_ref[j] = prepped[j][1]
            y, Z = step(Z, *prepped[j][0])
            y_ref[rw, :] = y
        z_scr[...] = Z
        if side is not None:
            pl.when(pl.program_id(0) == nblk - 1)(finish)

    side_bufs = [] if side is None else list(side[0])
    any_spec = pl.BlockSpec(memory_space=pl.ANY)
    return pl.pallas_call(
        body, grid=(nblk,),
        in_specs=[pl.BlockSpec((blk, a.shape[1]), lambda i: (i, 0)) for a in ins] + [any_spec] * n_side,
        out_specs=[pl.BlockSpec((blk, w_out), lambda i: (i, 0)), pl.BlockSpec((per_step, H, dh, dh), lambda i: (i, 0, 0, 0)),
                   pl.BlockSpec((per_step, H, C, C), lambda i: (i, 0, 0, 0))] + [any_spec] * n_side,
        out_shape=[jax.ShapeDtypeStruct((T, w_out), f32), jax.ShapeDtypeStruct((T // C, H, dh, dh), f32),
                   jax.ShapeDtypeStruct((T // C, H, C, C), f32)]
        + (_xy_out_shapes(side_bufs, side[1]) if side is not None else []),
        scratch_shapes=[pltpu.VMEM((H, dh, dh), f32)] + (_xy_sems(n_side, side[1]) if side is not None else []), name=name,
        compiler_params=_params("arbitrary"))(*ins, *side_bufs)


def _scan_bwd(name, fns, ins, dy, zs, invs, C, per_step, side=None):
    prep, step = fns
    T = ins[0].shape[0]
    _, H, dh, _ = zs.shape
    n_in = len(ins)
    blk = C * per_step
    nblk = T // blk
    n_side = 0 if side is None else len(side[0])

    def body(*refs):
        in_refs, dy_ref, zs_ref, inv_ref, refs = refs[:n_in], refs[n_in], refs[n_in + 1], refs[n_in + 2], refs[n_in + 3:]
        side_in, refs = refs[:n_side], refs[n_side:]
        out_refs, refs = refs[:n_in], refs[n_in:]
        side_out, refs = refs[:n_side], refs[n_side:]
        dz_scr = refs[0]
        if side is not None:
            start, finish = _xy_copies(side_in, side_out, refs[1:], side[1])
            pl.when(pl.program_id(0) == 0)(start)

        @pl.when(pl.program_id(0) == 0)
        def _():
            dz_scr[...] = jnp.zeros_like(dz_scr)

        rows = [slice(C * j, C * (j + 1)) for j in range(per_step)]
        prepped = [jax.vjp(lambda *a, j=j: prep(*a, inv=inv_ref[j])[0], *[r[rw, :] for r in in_refs])
                   for j, rw in enumerate(rows)]
        d_prepped = [None] * per_step
        dZ = dz_scr[...]
        for j in reversed(range(per_step)):
            _, pull = jax.vjp(step, zs_ref[j], *prepped[j][0])
            dZ, *d_prepped[j] = pull((dy_ref[rows[j], :], dZ))
        dz_scr[...] = dZ
        for j, rw in enumerate(rows):
            for o_ref, gval in zip(out_refs, prepped[j][1](tuple(d_prepped[j]))):
                o_ref[rw, :] = gval
        if side is not None:
            pl.when(pl.program_id(0) == nblk - 1)(finish)

    side_bufs = [] if side is None else list(side[0])
    any_spec = pl.BlockSpec(memory_space=pl.ANY)
    rev = lambda i: (nblk - 1 - i, 0)
    return pl.pallas_call(
        body, grid=(nblk,),
        in_specs=[pl.BlockSpec((blk, a.shape[1]), rev) for a in ins]
        + [pl.BlockSpec((blk, dy.shape[1]), rev), pl.BlockSpec((per_step, H, dh, dh), lambda i: (nblk - 1 - i, 0, 0, 0)),
           pl.BlockSpec((per_step, H, C, C), lambda i: (nblk - 1 - i, 0, 0, 0))] + [any_spec] * n_side,
        out_specs=[pl.BlockSpec((blk, a.shape[1]), rev) for a in ins] + [any_spec] * n_side,
        out_shape=[jax.ShapeDtypeStruct(a.shape, f32) for a in ins]
        + (_xy_out_shapes(side_bufs, side[1]) if side is not None else []),
        scratch_shapes=[pltpu.VMEM((H, dh, dh), f32)] + (_xy_sems(n_side, side[1]) if side is not None else []), name=name,
        compiler_params=_params("arbitrary"))(*ins, dy, zs, invs, *side_bufs)


def _residual_mm(name, a, b, res, tail, row_extras, consts, row_out, acc_out, tm, head=None):
    K, N = b.shape
    h_rows, h_consts = ([], []) if head is None else (list(head[1]), list(head[2]))
    lhs = h_rows + h_consts if head is not None else [a]
    M = lhs[0].shape[0]
    row_extras = [_row_of(e) for e in row_extras]
    n_lhs, n_res = len(lhs), 0 if res is None else 1
    ne, nc, nr = len(row_extras), len(consts), len(row_out)

    def body(*refs):
        lhs_refs, refs = refs[:n_lhs], refs[n_lhs:]
        b_ref, refs = refs[0], refs[1:]
        res_ref, refs = (refs[0], refs[1:]) if res is not None else (None, refs)
        extra_refs, const_refs, out_refs = refs[:ne], refs[ne:ne + nc], refs[ne + nc:]
        if head is not None:
            left = head[0](*[r[...] for r in lhs_refs])[0].astype(bf16)
            out_refs[0][...] = left
            out_refs = out_refs[1:]
        else:
            left = lhs_refs[0][...].astype(bf16)
        tile = jnp.dot(left, b_ref[...].astype(bf16), preferred_element_type=f32)
        if res is not None:
            tile = res_ref[...] + tile
        outs = tail(tile, *[r[...] for r in extra_refs], *[c[...] for c in const_refs])
        for o_ref, o in zip(out_refs[:nr], outs[:nr]):
            o_ref[...] = o.astype(o_ref.dtype)

        @pl.when(pl.program_id(0) == 0)
        def _():
            for o_ref in out_refs[nr:]:
                o_ref[...] = jnp.zeros_like(o_ref)

        for o_ref, o in zip(out_refs[nr:], outs[nr:]):
            o_ref[...] += o

    lhs_specs = ([_row_spec(tm, r.shape[1]) for r in h_rows] + [_full_spec(c.shape) for c in h_consts]
                 if head is not None else [_row_spec(tm, K)])
    head_out = [(K, bf16)] if head is not None else []
    outs = pl.pallas_call(
        body, grid=(M // tm,),
        in_specs=lhs_specs + [_full_spec(b.shape)] + ([_row_spec(tm, N)] if res is not None else [])
        + [_row_spec(tm, e[1], e[2]) for e in row_extras] + [_full_spec(c.shape) for c in consts],
        out_specs=[_row_spec(tm, w) for w, _ in head_out + list(row_out)] + [_full_spec(sh) for sh in acc_out],
        out_shape=[jax.ShapeDtypeStruct((M, w), d) for w, d in head_out + list(row_out)]
        + [jax.ShapeDtypeStruct(sh, f32) for sh in acc_out],
        name=name, compiler_params=_params("arbitrary"))(
            *lhs, b, *([res] if res is not None else []), *[e[0] for e in row_extras], *consts)
    return outs


def _norm_tail(x1, g):
    return x1, _rms(x1, g)


def _loss_tail(x2, tgt, g):
    l, vjp = jax.vjp(lambda xv, gv: _loss_rows(xv, tgt, gv), x2, g)
    dx, dg = vjp(jnp.ones_like(l))
    return dx, dg, jnp.zeros((1, 128), f32) + jnp.sum(l)


def _local_step(x, tgt, W, late=None):
    row = lambda a: a.reshape(1, -1)
    wp = W['w_in_pad']
    w_rwkv, w_qkv, w_z = wp[:, :OFF_QKV], wp[:, OFF_QKV:OFF_Z], wp[:, OFF_Z:OFF_GATES]
    w_gates, w_ab = wp[:, OFF_GATES:OFF_AB], wp[:, OFF_AB:]
    mu = row(W['rwkv_mu'])
    mixw = jnp.concatenate([mu, 1.0 - mu], axis=0)
    zpad = jnp.zeros((64, RWKV_W), f32)
    w2p = jnp.concatenate([W['rwkv_w2'], zpad], axis=0)
    a2p = jnp.concatenate([zpad, W['rwkv_a2']], axis=0)
    rw_consts = [row(W['rwkv_w0']), w2p, row(W['rwkv_a0']), a2p, W['rwkv_g2'], row(W['rwkv_k_k']), row(W['rwkv_k_a'])]
    post_consts = [row(W['rwkv_ln_w']), row(W['rwkv_ln_b']), row(W['rwkv_r_k'])]
    pad4 = lambda a: jnp.pad(row(a), ((0, 0), (0, W_AB - GDN_HEADS)))
    gd_consts = [pad4(W['gdn_a_log']), pad4(W['gdn_dt_bias'])]
    nw_t = jnp.tile(row(W['gdn_norm_w']), (1, GDN_HEADS))
    g1, g2n, gf = row(W['norm1_g']), row(W['norm2_g']), row(W['final_g'])

    (u,) = _pw_fwd("norm1", _rms_fn, [x], [g1], [D_MODEL], 512, out_dtype=bf16)
    p_rwkv = _mm(u, w_rwkv, 'nn', "in_rwkv")
    qkv_raw = _mm(u, w_qkv, 'nn', "in_qkv")
    z = _mm(u, w_z, 'nn', "in_z")
    gates = _mm(u, w_gates, 'nn', "in_gates")
    ab = _mm(u, w_ab, 'nn', "in_ab")

    r, lw, k2, v, a_, b_, g = _pw_fwd("rwkv_prep", _rwkv_prep_fn, [p_rwkv], rw_consts, [RWKV_W] * 7, 256, conv_w=mixw)
    wkv_in = [r, lw, k2, v, a_, b_]
    y, zs_wkv, inv_wkv, *gathered = _scan_fwd("wkv_fwd", (_wkv_prep, _wkv_step), wkv_in, WKV_CHUNK, RWKV_HEADS, 2 * RWKV_HD, RWKV_W, WKV_PER_STEP,
                                     side=None if late is None else (late['shards'][0], False))
    if late is not None:
        W = dict(W, **late['assemble'](0, gathered))
    ya_in, ya = _residual_mm("rwkv_proj", None, W['rwkv_proj'], None, lambda t: (t,), [], [], [(D_MODEL, f32)], [], 512,
                             head=(_rwkv_post_fn, [y, r, k2, v, g], post_consts))

    lanes = lambda off: slice(off, off + STRIP)
    gd_groups = [[lanes(GDN_HD * h), lanes(GDN_W + GDN_HD * h), lanes(2 * GDN_W + GDN_HD * h)] for h in range(GDN_HEADS)]
    gq, gk, gv = _group_fwd("gdn_prep", _gdn_prep_fn, qkv_raw, W['gdn_conv_w'], [], gd_groups, [], 3, 256)
    (gbeta,) = _pw_fwd("gdn_gate", _gdn_gate_fn, [ab], gd_consts, [W_AB], 512)
    gdn_in = [gq, gk, gv, gbeta]
    o, zs_gdn, inv_gdn, *gathered = _scan_fwd("gdn_fwd", (_gdn_prep, _gdn_step), gdn_in, GDN_CHUNK, GDN_HEADS, GDN_HD, GDN_W, GDN_PER_STEP,
                                     side=None if late is None else (late['shards'][1], False))
    if late is not None:
        W = dict(W, **late['assemble'](1, gathered))
    ga, gb = _cols(gates, D_MODEL, 0), _cols(gates, D_MODEL, 1)
    yb_in, yb, mixed = _residual_mm("gdn_proj", None, W['gdn_proj'], None, lambda t, a_, b_, c_: (t,) + _mix_fn(a_, b_, c_, t),
                                    [ga, gb, ya], [], [(D_MODEL, f32), (D_MODEL, bf16)], [], 512,
                                    head=(_gdn_post_fn, [o, z], [nw_t]))

    x1, u2 = _residual_mm("w_out", mixed, W['w_out'], x, _norm_tail, [], [g2n], [(D_MODEL, f32), (D_MODEL, bf16)], [], 512)
    h = _mm(u2, W['ffn_up'], 'nn', "ffn_up")
    act = _ffn_act_fwd(h, W['ffn_conv_w'], 256)

    G = {}
    slab_out = None if late is None else N_POS
    dx2, dgf, loss = _residual_mm("ffn_down", act, W['ffn_down'], x1, _loss_tail, [tgt], [gf], [(D_MODEL, f32)],
                                  [gf.shape, (1, 128)], 512)
    G['final_g'] = dgf
    dact = _mm(dx2, W['ffn_down'], 'nt', "d_act")
    G['ffn_down'] = _mm(act, dx2, 'tn', "g_ffn_down", out_dtype=bf16)
    dh, G['ffn_conv_w'] = _ffn_act_bwd(h, dact, W['ffn_conv_w'], 128)
    du2 = _mm(dh, W['ffn_up'], 'nt', "d_u2")
    G['ffn_up'] = _mm(u2, dh, 'tn', "g_ffn_up", out_dtype=bf16, col_slabs=slab_out)
    (dx1,), (G['norm2_g'],) = _pw_bwd("norm2_bwd", _rms_fn, [x1], [g2n], [(du2,)], 512, add_to_first=dx2)
    dmixed = _mm(dx1, W['w_out'], 'nt', "d_mixed")
    G['w_out'] = _mm(mixed, dx1, 'tn', "g_w_out", out_dtype=bf16)
    (dga, dgb, dya, dyb), _ = _pw_bwd("mix_bwd", _mix_fn, [ga, gb, ya, yb], [], [(dmixed,)], 256, row_dtypes=[bf16] * 4,
                                      strip=256)
    dya_in = _mm(dya, W['rwkv_proj'], 'nt', "d_ya_in")
    G['rwkv_proj'] = _mm(ya_in, dya, 'tn', "g_rwkv_proj", out_dtype=bf16, col_slabs=slab_out)
    dyb_in = _mm(dyb, W['gdn_proj'], 'nt', "d_yb_in")
    G['gdn_proj'] = _mm(yb_in, dyb, 'tn', "g_gdn_proj", out_dtype=bf16, col_slabs=slab_out)

    (do, dz), (dnw_t,) = _pw_bwd("gdn_post_bwd", _gdn_post_fn, [o, z], [nw_t], [(dyb_in,)], 256, row_dtypes=[f32, bf16],
                                 strip=128)
    G['gdn_norm_w'] = dnw_t.reshape(GDN_HEADS, GDN_HD).sum(axis=0)
    dgq, dgk, dgv, dgbeta, *arrived_b = _scan_bwd("gdn_bwd", (_gdn_prep, _gdn_step), gdn_in, do, zs_gdn, inv_gdn, GDN_CHUNK,
                                                  GDN_PER_STEP, side=None if late is None else (late['slabs'](G, 1), True))
    dqkv_raw, G['gdn_conv_w'], _ = _group_bwd("gdn_prep_bwd", _gdn_prep_fn, qkv_raw, W['gdn_conv_w'], [], gd_groups, [],
                                              [(dgq,), (dgk,), (dgv,)], 128)
    (dab,), (dal_p, ddt_p) = _pw_bwd("gdn_gate_bwd", _gdn_gate_fn, [ab], gd_consts, [(dgbeta,)], 512, row_dtypes=[bf16])
    G['gdn_a_log'], G['gdn_dt_bias'] = dal_p[0, :GDN_HEADS], ddt_p[0, :GDN_HEADS]

    (dy, dr1, dk21, dv1, dg_), (G['rwkv_ln_w'], G['rwkv_ln_b'], G['rwkv_r_k']) = _pw_bwd(
        "rwkv_post_bwd", _rwkv_post_fn, [y, r, k2, v, g], post_consts, [(dya_in,)], 256, strip=128)
    dr2, dlw, dk22, dv2, da_, db_, *arrived_a = _scan_bwd(
        "wkv_bwd", (_wkv_prep, _wkv_step), wkv_in, dy, zs_wkv, inv_wkv, WKV_CHUNK, WKV_PER_STEP,
        side=None if late is None else (late['slabs'](G, 0), True))
    G['_arrived'] = (arrived_a, arrived_b)
    (dp_rwkv,), dmixw, rw_grads = _pw_conv_bwd(
        "rwkv_prep_bwd", _rwkv_prep_fn, [p_rwkv], rw_consts,
        [(dr1, dr2), (dlw,), (dk21, dk22), (dv1, dv2), (da_,), (db_,), (dg_,)], mixw, 256, row_dtypes=[bf16])
    G['rwkv_w0'], dw2p, G['rwkv_a0'], da2p, G['rwkv_g2'], G['rwkv_k_k'], G['rwkv_k_a'] = rw_grads
    G['rwkv_w2'], G['rwkv_a2'] = dw2p[:64], da2p[64:]
    G['rwkv_mu'] = dmixw[0] - dmixw[1]

    dp = jnp.concatenate([dp_rwkv, dqkv_raw, dz, dga, dgb, dab], axis=1)
    G['w_in_pad'] = _mm(u, dp, 'tn', "g_w_in", out_dtype=bf16)
    if late is None:
        du = _mm(dp, wp, 'nt', "d_u")
    else:
        du, *G['_arrived_w_in'] = _mm(dp, wp, 'nt', "d_u", side=(late['w_in_slabs'](G), True))
    (dx,), (G['norm1_g'],) = _pw_bwd("norm1_bwd", _rms_fn, [x], [g1], [(du,)], 512, add_to_first=dx1)
    return loss, dx, G


IN_WIDTH = OFF_AB + 8
PAD_ORDER = ((0, OFF_GATES), (OFF_GATES + 8, IN_WIDTH), (OFF_GATES, OFF_GATES + 8))


def _pad_w_in_shards(shards):
    width = shards[0].shape[1]
    parts = []
    for a, b in PAD_ORDER:
        for j, sh in enumerate(shards):
            lo, hi = max(a, j * width), min(b, (j + 1) * width)
            if lo < hi:
                parts.append(sh[:, lo - j * width:hi - j * width])
    return jnp.concatenate(parts + [jnp.zeros((shards[0].shape[0], W_AB - 8), shards[0].dtype)], axis=1)


def _unpad_cols(wp, lo, hi):
    parts, off = [], 0
    for a, b in PAD_ORDER:
        l, h = max(a, lo), min(b, hi)
        if l < h:
            parts.append((l, wp[:, off + l - a:off + h - a]))
        off += b - a
    parts.sort(key=lambda t: t[0])
    return parts[0][1] if len(parts) == 1 else jnp.concatenate([p for _, p in parts], axis=1)


BIG = ('w_in', 'rwkv_proj', 'gdn_proj', 'w_out', 'ffn_up', 'ffn_down')
SMALL_SHARDED = ('rwkv_w2', 'rwkv_a2', 'rwkv_g2', 'gdn_conv_w', 'ffn_conv_w')


def _rows128(shape):
    n = 1
    for d in shape:
        n *= d
    return -(-n // LANES)


def _pack128(arrays):
    parts = []
    for a in arrays:
        flat = a.reshape(-1)
        rows = _rows128(a.shape)
        parts.append(jnp.pad(flat, (0, rows * LANES - flat.shape[0])).reshape(rows, LANES))
    buf = jnp.concatenate(parts, axis=0)
    return jnp.pad(buf, ((0, -buf.shape[0] % HALO), (0, 0)))


def _unpack128(buf, shapes):
    out, off = [], 0
    for s in shapes:
        rows, n = _rows128(s), 1
        for d in s:
            n *= d
        out.append(buf[off:off + rows].reshape(-1)[:n].reshape(s))
        off += rows
    return out


def _param_tile(r, c):
    best = None
    for d in range(2 * HALO, r + 1, 2 * HALO):
        if r % d == 0 and d * c * 4 <= TILE_BYTES:
            best = d
    if best is not None or r * c * 4 <= TILE_BYTES:
        return (best if best is not None else r), c
    return r, 128


def _xy_exchange(name, bufs, scatter):
    n = len(bufs)

    def body(*refs):
        start, finish = _xy_copies(refs[:n], refs[n:2 * n], refs[2 * n:], scatter)
        start()
        finish()

    return pl.pallas_call(
        body, in_specs=[pl.BlockSpec(memory_space=pl.ANY)] * n, out_specs=[pl.BlockSpec(memory_space=pl.ANY)] * n,
        out_shape=_xy_out_shapes(bufs, scatter), scratch_shapes=_xy_sems(n, scatter), name=name)(*bufs)


def _sibling_exchange(name, bufs):
    n = len(bufs)

    def body(*refs):
        in_refs, out_refs, send_sems, recv_sems = refs[:n], refs[n:2 * n], refs[2 * n], refs[2 * n + 1]
        x, y, c = lax.axis_index("x"), lax.axis_index("y"), lax.axis_index("c")
        copies = [pltpu.make_async_remote_copy(
            src_ref=in_refs[a], dst_ref=out_refs[a], send_sem=send_sems.at[a], recv_sem=recv_sems.at[a],
            device_id=(x, y, 1 - c), device_id_type=pl.DeviceIdType.MESH) for a in range(n)]
        for cp in copies:
            cp.start()
        for cp in copies:
            cp.wait()

    return pl.pallas_call(
        body, in_specs=[pl.BlockSpec(memory_space=pl.ANY)] * n, out_specs=[pl.BlockSpec(memory_space=pl.ANY)] * n,
        out_shape=[jax.ShapeDtypeStruct(b.shape, b.dtype) for b in bufs],
        scratch_shapes=[pltpu.SemaphoreType.DMA((n,)), pltpu.SemaphoreType.DMA((n,))], name=name)(*bufs)


def _sum_slots(name, buf):
    _, R, L = buf.shape
    tr, tc = _param_tile(R, L)

    def body(b_ref, o_ref):
        part = lambda s: b_ref[s].astype(f32)
        o_ref[...] = ((part(0) + part(1)) + part(2)) + part(3)

    return pl.pallas_call(
        body, grid=(R // tr, L // tc),
        in_specs=[pl.BlockSpec((N_POS, tr, tc), lambda i, j: (0, i, j))],
        out_specs=pl.BlockSpec((tr, tc), lambda i, j: (i, j)),
        out_shape=jax.ShapeDtypeStruct((R, L), f32), name=name,
        compiler_params=_params("parallel", "parallel"))(buf)


def _adamw(name, w, ga, gb, m, v):
    R, L = w.shape
    tr, tc = _param_tile(R, L)
    c1 = 1.0 / (1.0 - ADAM_B1 ** ADAM_STEP)
    c2 = 1.0 / (1.0 - ADAM_B2 ** ADAM_STEP)

    def body(w_ref, ga_ref, gb_ref, m_ref, v_ref, g_out, d_out, m_out, v_out):
        g = ga_ref[...] + gb_ref[...]
        m_new = ADAM_B1 * m_ref[...] + (1.0 - ADAM_B1) * g
        v_new = ADAM_B2 * v_ref[...] + (1.0 - ADAM_B2) * (g * g)
        g_out[...] = g
        m_out[...] = m_new
        v_out[...] = v_new
        d_out[...] = -ADAM_LR * ((m_new * c1) / (jnp.sqrt(v_new * c2) + ADAM_EPS) + ADAM_WD * w_ref[...])

    spec = pl.BlockSpec((tr, tc), lambda i, j: (i, j))
    return pl.pallas_call(
        body, grid=(R // tr, L // tc), in_specs=[spec] * 5, out_specs=[spec] * 4,
        out_shape=[jax.ShapeDtypeStruct((R, L), f32)] * 4, name=name,
        compiler_params=_params("parallel", "parallel"))(w, ga, gb, m, v)


def _step(x, loss_target, P, M, V):
    shapes = {n: tuple(P[n].shape) for n in WEIGHTS}
    sh_shapes = [shapes[n] for n in SMALL_SHARDED]
    packed = SMALL_SHARDED + SMALL

    def whole(n, g):
        return g.reshape(-1, g.shape[2]) if n in ROW_SHARDED else jnp.concatenate([g[j] for j in range(N_POS)], axis=1)

    def slabs(G, n, dtype=f32):
        r, c = shapes[n]
        full = G[n].astype(dtype)
        if full.ndim == 3:
            return full
        return full.reshape(N_POS, r, c) if n in ROW_SHARDED else full.reshape(r, N_POS, c).transpose(1, 0, 2)

    g_w_in, g_small = _xy_exchange("gather_w_in", [P['w_in'].astype(bf16), _pack128([P[n] for n in SMALL_SHARDED])],
                                   scatter=False)
    W = {n: P[n] for n in SMALL}
    W['w_in_pad'] = _pad_w_in_shards([g_w_in[j] for j in range(N_POS)])
    per_pos = [_unpack128(g_small[j], sh_shapes) for j in range(N_POS)]
    for q, n in enumerate(SMALL_SHARDED):
        W[n] = jnp.concatenate([per_pos[j][q] for j in range(N_POS)], axis=1)
    groups = (('rwkv_proj', 'gdn_proj', 'ffn_up'), ('w_out', 'ffn_down'))
    late = dict(shards=[[P[n].astype(bf16) for n in grp] for grp in groups],
                assemble=lambda q, gathered: {n: whole(n, g) for n, g in zip(groups[q], gathered)},
                slabs=lambda G, q: [slabs(G, n, bf16) for n in groups[q]],
                w_in_slabs=lambda G: [jnp.stack([_unpad_cols(G['w_in_pad'], j * shapes['w_in'][1], (j + 1) * shapes['w_in'][1])
                                                 for j in range(N_POS)])])

    loss_rows, dx, G = _local_step(x, loss_target, W, late)
    arrived = {n: a for grp, got in zip(groups, G.pop('_arrived')) for n, a in zip(grp, got)}
    (arrived_w_in,) = G.pop('_arrived_w_in')
    G.pop('w_in_pad')

    small_slabs = jnp.stack([_pack128([slabs(G, n)[j] for n in SMALL_SHARDED] + [G[n] for n in SMALL]) for j in range(N_POS)])
    (arrived_small,) = _xy_exchange("scatter_small", [small_slabs], scatter=True)
    contributions = [arrived_w_in] + [arrived[n] for n in BIG[1:]] + [arrived_small]
    tags = list(BIG) + ['small']
    plane = [_sum_slots("sum_" + t, cbuf) for t, cbuf in zip(tags, contributions)]
    sibling = _sibling_exchange("sibling_grads", plane)

    out = {}
    names4 = ('grad', 'delta', 'new_m', 'new_v')
    for q, n in enumerate(BIG):
        for tag, t in zip(names4, _adamw("adamw_" + n, P[n], plane[q], sibling[q], M[n], V[n])):
            out[tag + '_' + n] = t
    small_out = _adamw("adamw_small", _pack128([P[n] for n in packed]), plane[-1], sibling[-1],
                       _pack128([M[n] for n in packed]), _pack128([V[n] for n in packed]))
    for tag, buf in zip(names4, small_out):
        for n, t in zip(packed, _unpack128(buf, [shapes[n] for n in packed])):
            out[tag + '_' + n] = t
    loss = lax.psum(loss_rows[0, 0], ("x", "y", "c"))
    return loss, dx, out


def kernel(x, norm1_g, w_in, rwkv_mu, rwkv_w0, rwkv_w2, rwkv_a0, rwkv_a2, rwkv_g2, rwkv_k_k, rwkv_k_a, rwkv_r_k, rwkv_ln_w, rwkv_ln_b, rwkv_proj, gdn_conv_w, gdn_a_log, gdn_dt_bias, gdn_norm_w, gdn_proj, w_out, norm2_g, ffn_up, ffn_conv_w, ffn_down, final_g, loss_target, m_norm1_g, m_w_in, m_rwkv_mu, m_rwkv_w0, m_rwkv_w2, m_rwkv_a0, m_rwkv_a2, m_rwkv_g2, m_rwkv_k_k, m_rwkv_k_a, m_rwkv_r_k, m_rwkv_ln_w, m_rwkv_ln_b, m_rwkv_proj, m_gdn_conv_w, m_gdn_a_log, m_gdn_dt_bias, m_gdn_norm_w, m_gdn_proj, m_w_out, m_norm2_g, m_ffn_up, m_ffn_conv_w, m_ffn_down, m_final_g, v_norm1_g, v_w_in, v_rwkv_mu, v_rwkv_w0, v_rwkv_w2, v_rwkv_a0, v_rwkv_a2, v_rwkv_g2, v_rwkv_k_k, v_rwkv_k_a, v_rwkv_r_k, v_rwkv_ln_w, v_rwkv_ln_b, v_rwkv_proj, v_gdn_conv_w, v_gdn_a_log, v_gdn_dt_bias, v_gdn_norm_w, v_gdn_proj, v_w_out, v_norm2_g, v_ffn_up, v_ffn_conv_w, v_ffn_down, v_final_g):
    weights = (norm1_g, w_in, rwkv_mu, rwkv_w0, rwkv_w2, rwkv_a0, rwkv_a2, rwkv_g2, rwkv_k_k, rwkv_k_a, rwkv_r_k, rwkv_ln_w,
               rwkv_ln_b, rwkv_proj, gdn_conv_w, gdn_a_log, gdn_dt_bias, gdn_norm_w, gdn_proj, w_out, norm2_g, ffn_up,
               ffn_conv_w, ffn_down, final_g)
    m_in = (m_norm1_g, m_w_in, m_rwkv_mu, m_rwkv_w0, m_rwkv_w2, m_rwkv_a0, m_rwkv_a2, m_rwkv_g2, m_rwkv_k_k, m_rwkv_k_a,
            m_rwkv_r_k, m_rwkv_ln_w, m_rwkv_ln_b, m_rwkv_proj, m_gdn_conv_w, m_gdn_a_log, m_gdn_dt_bias, m_gdn_norm_w,
            m_gdn_proj, m_w_out, m_norm2_g, m_ffn_up, m_ffn_conv_w, m_ffn_down, m_final_g)
    v_in = (v_norm1_g, v_w_in, v_rwkv_mu, v_rwkv_w0, v_rwkv_w2, v_rwkv_a0, v_rwkv_a2, v_rwkv_g2, v_rwkv_k_k, v_rwkv_k_a,
            v_rwkv_r_k, v_rwkv_ln_w, v_rwkv_ln_b, v_rwkv_proj, v_gdn_conv_w, v_gdn_a_log, v_gdn_dt_bias, v_gdn_norm_w,
            v_gdn_proj, v_w_out, v_norm2_g, v_ffn_up, v_ffn_conv_w, v_ffn_down, v_final_g)
    drop = lambda n, a: a if n == 'final_g' else a[0]
    P = {n: drop(n, a) for n, a in zip(WEIGHTS, weights)}
    M = {n: drop(n, a) for n, a in zip(WEIGHTS, m_in)}
    V = {n: drop(n, a) for n, a in zip(WEIGHTS, v_in)}
    loss, dx, out = _step(x[0], loss_target[0], P, M, V)
    lift = lambda n, a: a if n == 'final_g' else a[None]
    res = [loss, dx[None]]
    for tag in ('grad', 'delta', 'new_m', 'new_v'):
        res += [lift(n, out[tag + '_' + n]) for n in WEIGHTS]
    return tuple(res)
```

```python
import functools

import jax
import jax.numpy as jnp
from jax import lax
from jax.experimental import pallas as pl
from jax.experimental.pallas import tpu as pltpu

f32 = jnp.float32
bf16 = jnp.bfloat16

D_MODEL = 1024
RWKV_HEADS, RWKV_HD, RWKV_W = 8, 64, 512
GDN_HEADS, GDN_HD, GDN_W = 4, 128, 512
NORM_EPS, L2_EPS, GN_EPS = 1e-6, 1e-6, 64e-5
W_AB = 256
OFF_QKV, OFF_Z, OFF_GATES, OFF_AB = 1792, 3328, 3840, 5888
W_IN_PAD = OFF_AB + W_AB
WKV_CHUNK, WKV_PER_STEP = 64, 4
GDN_CHUNK, GDN_PER_STEP = 128, 4
HALO = 8
LANES = 128
TILE_BYTES = 1 << 20
VMEM_LIMIT = 56 * 1024 * 1024

ADAM_LR, ADAM_B1, ADAM_B2, ADAM_EPS, ADAM_WD, ADAM_STEP = 0.001, 0.9, 0.999, 1e-08, 0.01, 10

ROW_SHARDED = ('w_out', 'ffn_down')
SMALL = ('norm1_g', 'rwkv_mu', 'rwkv_w0', 'rwkv_a0', 'rwkv_k_k', 'rwkv_k_a', 'rwkv_r_k', 'rwkv_ln_w', 'rwkv_ln_b',
         'gdn_a_log', 'gdn_dt_bias', 'gdn_norm_w', 'norm2_g', 'final_g')
WEIGHTS = ('norm1_g', 'w_in', 'rwkv_mu', 'rwkv_w0', 'rwkv_w2', 'rwkv_a0', 'rwkv_a2', 'rwkv_g2', 'rwkv_k_k', 'rwkv_k_a',
           'rwkv_r_k', 'rwkv_ln_w', 'rwkv_ln_b', 'rwkv_proj', 'gdn_conv_w', 'gdn_a_log', 'gdn_dt_bias', 'gdn_norm_w',
           'gdn_proj', 'w_out', 'norm2_g', 'ffn_up', 'ffn_conv_w', 'ffn_down', 'final_g')


def _params(*sem):
    return pltpu.CompilerParams(dimension_semantics=sem, vmem_limit_bytes=VMEM_LIMIT)


def _tile(n, limit):
    if n <= limit:
        return n
    best = None
    for d in range(128, limit + 1, 128):
        if n % d == 0:
            best = d
    if best is None:
        raise ValueError(f"no tile for {n} under {limit}")
    return best


MM_BLOCK_BYTES = 6 << 20
MM_MAX_COLS = 1536


def _mm(a, b, mode, name, add=None, out_dtype=f32, side=None, col_slabs=None):
    if mode == 'nn':
        (M, K), N = a.shape, b.shape[1]
    elif mode == 'nt':
        (M, K), N = a.shape, b.shape[0]
    else:
        (K, M), N = a.shape, b.shape[1]
    tm = _tile(M, 1408)
    tk = _tile(K, min(2816, MM_BLOCK_BYTES // (tm * a.dtype.itemsize)))
    tn = _tile(N, max(128, min(MM_BLOCK_BYTES // (tk * b.dtype.itemsize), MM_BLOCK_BYTES // (tm * 4), MM_MAX_COLS) // 128 * 128))
    if col_slabs is not None:
        tn = N // col_slabs
    nk = K // tk
    grid = (M // tm, N // tn, nk)
    dn = {'nn': (((1,), (0,)), ((), ())), 'nt': (((1,), (1,)), ((), ())), 'tn': (((0,), (0,)), ((), ()))}[mode]
    n_add = 0 if add is None else 1
    n_side = 0 if side is None else len(side[0])

    def body(a_ref, b_ref, *rest):
        add_ref = rest[0] if add is not None else None
        side_in, rest = rest[n_add:n_add + n_side], rest[n_add + n_side:]
        o_ref, side_out, rest = rest[0], rest[1:1 + n_side], rest[1 + n_side:]
        acc_ref, rest = (rest[0], rest[1:]) if nk > 1 else (None, rest)
        ids = [pl.program_id(d) for d in range(3)]
        if side is not None:
            start, finish = _xy_copies(side_in, side_out, rest, side[1])
            pl.when((ids[0] == 0) & (ids[1] == 0) & (ids[2] == 0))(start)
        acc = lax.dot_general(a_ref[...].astype(bf16), b_ref[...].astype(bf16), dn, preferred_element_type=f32)
        if nk == 1:
            o_ref[...] = (acc + add_ref[...] if add is not None else acc).astype(out_dtype)
        else:
            k = ids[2]

            @pl.when(k == 0)
            def _():
                acc_ref[...] = acc + add_ref[...] if add is not None else acc

            @pl.when(k > 0)
            def _():
                acc_ref[...] += acc

            @pl.when(k == nk - 1)
            def _():
                o_ref[...] = acc_ref[...].astype(out_dtype)
        if side is not None:
            pl.when((ids[0] == grid[0] - 1) & (ids[1] == grid[1] - 1) & (ids[2] == nk - 1))(finish)

    a_spec = (pl.BlockSpec((tk, tm), lambda i, j, k: (k, i)) if mode == 'tn'
              else pl.BlockSpec((tm, tk), lambda i, j, k: (i, k)))
    b_spec = (pl.BlockSpec((tn, tk), lambda i, j, k: (j, k)) if mode == 'nt'
              else pl.BlockSpec((tk, tn), lambda i, j, k: (k, j)))
    o_spec = pl.BlockSpec((tm, tn), lambda i, j, k: (i, j))
    o_shape = jax.ShapeDtypeStruct((M, N), out_dtype)
    if col_slabs is not None:
        o_spec = pl.BlockSpec((None, tm, tn), lambda i, j, k: (j, i, 0))
        o_shape = jax.ShapeDtypeStruct((col_slabs, M, tn), out_dtype)
    any_spec = pl.BlockSpec(memory_space=pl.ANY)
    side_bufs = [] if side is None else list(side[0])
    ins, specs = [a, b], [a_spec, b_spec]
    if add is not None:
        ins.append(add)
        specs.append(o_spec)
    outs = pl.pallas_call(
        body, grid=grid, in_specs=specs + [any_spec] * n_side, out_specs=[o_spec] + [any_spec] * n_side,
        out_shape=[o_shape] + (_xy_out_shapes(side_bufs, side[1]) if side is not None else []),
        scratch_shapes=([pltpu.VMEM((tm, tn), f32)] if nk > 1 else []) + (_xy_sems(n_side, side[1]) if side is not None else []),
        name=name,
        compiler_params=_params(*(("arbitrary",) * 3 if side is not None else ("parallel", "parallel", "arbitrary"))))(
            *ins, *side_bufs)
    return list(outs) if side is not None else outs[0]


def _shift_down(cur, prev, s):
    if s == 0:
        return cur
    ext = jnp.concatenate([prev, cur], axis=0)
    return pltpu.roll(ext, s, 0)[HALO:]


def _shift_up(cur, nxt, s):
    if s == 0:
        return cur
    ext = jnp.concatenate([cur, nxt], axis=0)
    return pltpu.roll(ext, ext.shape[0] - s, 0)[:cur.shape[0]]


def _conv_apply(cur, prev, w_ref, shifted=None):
    taps = w_ref.shape[0]
    out = None
    for i in range(taps):
        s = taps - 1 - i
        term = (shifted[s] if shifted is not None else _shift_down(cur, prev, s)) * w_ref[pl.ds(i, 1), :]
        out = term if out is None else out + term
    return out


def _row_spec(tm, w, col=0):
    return pl.BlockSpec((tm, w), lambda i: (i, col))


def _cols(a, width, col):
    return (a, width, col)


def _row_of(r):
    return r if isinstance(r, tuple) else (r, r.shape[1], 0)


def _prev_spec(tm, w):
    return pl.BlockSpec((HALO, w), lambda i: (jnp.maximum(i * (tm // HALO) - 1, 0), 0))


def _next_spec(tm, w, T):
    return pl.BlockSpec((HALO, w), lambda i: (jnp.minimum((i + 1) * (tm // HALO), T // HALO - 1), 0))


def _full_spec(shape):
    return pl.BlockSpec(shape, lambda i: (0,) * len(shape))


def _pw_fwd(name, fn, rows, consts, out_widths, tm, conv_w=None, out_dtype=f32, strip=None):
    T = _row_of(rows[0])[0].shape[0]
    nr, nc = len(rows), len(consts)

    def body(*refs):
        i = pl.program_id(0)
        if strip is not None:
            for j in range(out_widths[0] // strip):
                sl = slice(strip * j, strip * (j + 1))
                outs = fn(*[r[:, sl] for r in refs[:nr + nc]])
                for o_ref, o in zip(refs[nr + nc:], outs):
                    o_ref[:, sl] = o.astype(out_dtype)
            return
        vals = [r[...] for r in refs[:nr]]
        p = nr
        if conv_w is not None:
            prev = jnp.where(i > 0, refs[p][...], 0.0)
            vals[0] = _conv_apply(vals[0], prev, refs[p + 1])
            p += 2
        cvals = [r[...] for r in refs[p:p + nc]]
        outs = fn(*vals, *cvals)
        for o_ref, o in zip(refs[p + nc:], outs):
            o_ref[...] = o.astype(out_dtype)

    ins = [_row_of(r)[0] for r in rows]
    specs = [_row_spec(tm, *_row_of(r)[1:]) for r in rows]
    if conv_w is not None:
        ins += [rows[0], conv_w]
        specs += [_prev_spec(tm, rows[0].shape[1]), _full_spec(conv_w.shape)]
    ins += list(consts)
    specs += [_full_spec(c.shape) for c in consts]
    outs = pl.pallas_call(
        body, grid=(T // tm,), in_specs=specs,
        out_specs=[_row_spec(tm, w) for w in out_widths],
        out_shape=[jax.ShapeDtypeStruct((T, w), out_dtype) for w in out_widths], name=name,
        compiler_params=_params("parallel"))(*ins)
    return outs


def _pw_bwd(name, fn, rows, consts, cots, tm, add_to_first=None, row_dtypes=None, strip=None):
    rows = [_row_of(r) for r in rows]
    T = rows[0][0].shape[0]
    nr, nc = len(rows), len(consts)
    flat_cots = [c for grp in cots for c in grp]
    row_dtypes = row_dtypes or [f32] * nr
    n_extra = 0 if add_to_first is None else 1
    width = rows[0][1]

    def body(*refs):
        i = pl.program_id(0)
        in_refs, cot_refs = refs[:nr + nc], refs[nr + nc:nr + nc + len(flat_cots)]
        extra_ref = refs[nr + nc + len(flat_cots)] if add_to_first is not None else None
        row_out = refs[nr + nc + len(flat_cots) + n_extra:][:nr]
        const_out = refs[nr + nc + len(flat_cots) + n_extra + nr:]

        @pl.when(i == 0)
        def _():
            for q in range(nc):
                const_out[q][...] = jnp.zeros_like(const_out[q])

        def part(sl):
            cot_vals, p = [], 0
            for grp in cots:
                acc = cot_refs[p][:, sl]
                for q in range(1, len(grp)):
                    acc = acc + cot_refs[p + q][:, sl]
                p += len(grp)
                cot_vals.append(acc)
            _, vjp = jax.vjp(fn, *[r[:, sl] for r in in_refs])
            grads = vjp(tuple(cot_vals))
            for q in range(nr):
                g = grads[q]
                if q == 0 and extra_ref is not None:
                    g = g + extra_ref[:, sl]
                row_out[q][:, sl] = g.astype(row_dtypes[q])
            for q in range(nc):
                const_out[q][:, sl] += grads[nr + q]

        if strip is None:
            part(slice(None))
        else:
            for j in range(width // strip):
                part(slice(strip * j, strip * (j + 1)))

    ins = [r[0] for r in rows] + list(consts) + flat_cots
    specs = ([_row_spec(tm, r[1], r[2]) for r in rows] + [_full_spec(c.shape) for c in consts]
             + [_row_spec(tm, c.shape[1]) for c in flat_cots])
    if add_to_first is not None:
        ins.append(add_to_first)
        specs.append(_row_spec(tm, add_to_first.shape[1]))
    out_shapes = ([jax.ShapeDtypeStruct((T, r[1]), d) for r, d in zip(rows, row_dtypes)]
                  + [jax.ShapeDtypeStruct(c.shape, f32) for c in consts])
    out_specs = [_row_spec(tm, r[1]) for r in rows] + [_full_spec(c.shape) for c in consts]
    outs = pl.pallas_call(
        body, grid=(T // tm,), in_specs=specs, out_specs=out_specs, out_shape=out_shapes, name=name,
        compiler_params=_params("arbitrary"))(*ins)
    return list(outs[:nr]), list(outs[nr:])


def _pw_conv_bwd(name, fn, rows, consts, cots, conv_w, tm, row_dtypes=None):
    T, W0 = rows[0].shape
    nr, nc = len(rows), len(consts)
    taps = conv_w.shape[0]
    nblk = T // tm
    flat_cots = [c for grp in cots for c in grp]
    row_dtypes = row_dtypes or [f32] * nr

    def body(*refs):
        i = pl.program_id(0)
        p = 0
        cur = [r[...] for r in refs[p:p + nr]]; p += nr
        nxt = [r[...] for r in refs[p:p + nr]]; p += nr
        prev = jnp.where(i > 0, refs[p][...], 0.0); p += 1
        w_ref = refs[p]; p += 1
        cvals = [r[...] for r in refs[p:p + nc]]; p += nc

        def summed(p0):
            out, q = [], p0
            for grp in cots:
                acc = refs[q][...]
                for t in range(1, len(grp)):
                    acc = acc + refs[q + t][...]
                q += len(grp)
                out.append(acc)
            return out, q

        cot_cur, p = summed(p)
        cot_nxt, p = summed(p)
        row_out, dw_ref, const_out = refs[p:p + nr], refs[p + nr], refs[p + nr + 1:]

        x_cur = cur[0]
        x_down = [_shift_down(x_cur, prev, s_) for s_ in range(taps)]
        _, vjp = jax.vjp(fn, _conv_apply(x_cur, prev, w_ref, x_down), *cur[1:], *cvals)
        grads = vjp(tuple(cot_cur))
        _, vjp_n = jax.vjp(fn, _conv_apply(nxt[0], x_cur[tm - HALO:], w_ref), *nxt[1:], *cvals)
        dc_n = jnp.where(i < nblk - 1, vjp_n(tuple(cot_nxt))[0], 0.0)
        dc = grads[0]

        @pl.when(i == 0)
        def _():
            dw_ref[...] = jnp.zeros_like(dw_ref)
            for q in range(nc):
                const_out[q][...] = jnp.zeros_like(const_out[q])

        dx = None
        for k in range(taps):
            s_ = taps - 1 - k
            term = _shift_up(dc, dc_n, s_) * w_ref[pl.ds(k, 1), :]
            dx = term if dx is None else dx + term
            dw_ref[pl.ds(k, 1), :] += jnp.sum(dc * x_down[s_], axis=0, keepdims=True)
        row_out[0][...] = dx.astype(row_dtypes[0])
        for q in range(1, nr):
            row_out[q][...] = grads[q].astype(row_dtypes[q])
        for q in range(nc):
            const_out[q][...] += grads[nr + q]

    ins = list(rows) + list(rows) + [rows[0], conv_w] + list(consts) + flat_cots + flat_cots
    specs = ([_row_spec(tm, r.shape[1]) for r in rows] + [_next_spec(tm, r.shape[1], T) for r in rows]
             + [_prev_spec(tm, W0), _full_spec(conv_w.shape)] + [_full_spec(c.shape) for c in consts]
             + [_row_spec(tm, c.shape[1]) for c in flat_cots] + [_next_spec(tm, c.shape[1], T) for c in flat_cots])
    out_shapes = ([jax.ShapeDtypeStruct(r.shape, d) for r, d in zip(rows, row_dtypes)]
                  + [jax.ShapeDtypeStruct(conv_w.shape, f32)] + [jax.ShapeDtypeStruct(c.shape, f32) for c in consts])
    out_specs = ([_row_spec(tm, r.shape[1]) for r in rows] + [_full_spec(conv_w.shape)]
                 + [_full_spec(c.shape) for c in consts])
    outs = pl.pallas_call(
        body, grid=(nblk,), in_specs=specs, out_specs=out_specs, out_shape=out_shapes, name=name,
        compiler_params=_params("arbitrary"))(*ins)
    return list(outs[:nr]), outs[nr], list(outs[nr + 1:])


def _sigmoid(x):
    return 0.5 * jnp.tanh(0.5 * x) + 0.5


def _softplus(x):
    return jnp.maximum(x, 0.0) + jnp.log(1.0 + jnp.exp(jnp.minimum(x, -x)))


def _seg_sum_impl(x, seg):
    w = x.shape[-1]
    r = lax.broadcasted_iota(jnp.int32, (w, w), 0) // seg
    c = lax.broadcasted_iota(jnp.int32, (w, w), 1) // seg
    ones = (r == c).astype(bf16)
    hi = x.astype(bf16)
    lo = (x - hi.astype(f32)).astype(bf16)
    return (jnp.dot(hi, ones, preferred_element_type=f32) + jnp.dot(lo, ones, preferred_element_type=f32))


@functools.partial(jax.custom_vjp, nondiff_argnums=(1,))
def _seg_sum(x, seg):
    return _seg_sum_impl(x, seg)


_seg_sum.defvjp(lambda x, seg: (_seg_sum_impl(x, seg), None), lambda seg, _, g: (_seg_sum_impl(g, seg),))


def _rms(x, g):
    return x * lax.rsqrt(jnp.mean(x * x, axis=-1, keepdims=True) + NORM_EPS) * g


def _rms_fn(x, g):
    return (_rms(x, g),)


def _loss_rows(x2, tgt, g):
    e = _rms(x2, g) - tgt
    return 0.5 * jnp.sum(e * e, axis=-1, keepdims=True) * (1.0 / D_MODEL)


@jax.custom_vjp
def _dot_lo(a, b):
    return jnp.dot(a.astype(bf16), b.astype(bf16), preferred_element_type=f32)


def _dot_lo_bwd(ab, g):
    a, b = ab
    gl = g.astype(bf16)
    return (lax.dot_general(gl, b.astype(bf16), (((1,), (1,)), ((), ())), preferred_element_type=f32),
            lax.dot_general(a.astype(bf16), gl, (((0,), (0,)), ((), ())), preferred_element_type=f32))


_dot_lo.defvjp(lambda a, b: (_dot_lo(a, b), (a, b)), _dot_lo_bwd)


def _rwkv_prep_fn(ps, w0, w2p, a0, a2p, g2, k_k, k_a):
    r, k, v = ps[:, 0:512], ps[:, 512:1024], ps[:, 1024:1536]
    wa, gl = ps[:, 1536:1664], ps[:, 1664:1792]
    z = w0 + _dot_lo(jnp.tanh(wa), w2p)
    w_log = -_softplus(-z) - 0.5
    lw = -jnp.exp(w_log)
    a = _sigmoid(a0 + _dot_lo(wa, a2p))
    g = _dot_lo(_sigmoid(gl), g2)
    kx = k * k_k
    kk = kx * lax.rsqrt(_seg_sum(kx * kx, RWKV_HD) + L2_EPS)
    k2 = k * (1.0 + (a - 1.0) * k_a)
    return r, lw, k2, v, -kk, kk * a, g


def _rwkv_post_fn(y, r, k2, v, g, ln_w, ln_b, rk):
    mean = _seg_sum(y, RWKV_HD) * (1.0 / RWKV_HD)
    yc = y - mean
    var = _seg_sum(yc * yc, RWKV_HD) * (1.0 / RWKV_HD)
    yn = yc * lax.rsqrt(var + GN_EPS) * ln_w + ln_b
    bonus = _seg_sum(r * k2 * rk, RWKV_HD) * v
    return ((yn + bonus) * g,)


def _gdn_prep_fn(cq, ck, cv):
    silu = lambda c: c * _sigmoid(c)
    q, k = silu(cq), silu(ck)
    q = q * lax.rsqrt(jnp.sum(q * q, axis=-1, keepdims=True) + L2_EPS) * (GDN_HD ** -0.5)
    k = k * lax.rsqrt(jnp.sum(k * k, axis=-1, keepdims=True) + L2_EPS)
    return q, k, silu(cv)


def _gdn_gate_fn(ab, al_p, dt_p):
    lane = lax.broadcasted_iota(jnp.int32, ab.shape, 1)
    gpart = -jnp.exp(al_p) * _softplus(ab + dt_p)
    return (jnp.where(lane < GDN_HEADS, gpart, jnp.where(lane < 2 * GDN_HEADS, _sigmoid(ab), 0.0)),)


def _gdn_post_fn(o, z, nw):
    ms = _seg_sum(o * o, GDN_HD) * (1.0 / GDN_HD)
    return (o * lax.rsqrt(ms + NORM_EPS) * nw * (z * _sigmoid(z)),)


def _mix_fn(ga, gb, ya, yb):
    return (_sigmoid(ga) * ya + _sigmoid(gb) * yb,)


STRIP = 128


def _strip_conv(ref, prev_ref, w_ref, sl, first, taps):
    cur = ref[:, sl]
    prev = jnp.where(first, 0.0, prev_ref[:, sl])
    down = [_shift_down(cur, prev, s) for s in range(taps)]
    conv = None
    for k in range(taps):
        term = down[taps - 1 - k] * w_ref[pl.ds(k, 1), sl]
        conv = term if conv is None else conv + term
    return cur, down, conv


def _group_fwd(name, fn, x, w, shared_cols, group_cols, consts, n_out, tm):
    T, W = x.shape
    taps = w.shape[0]
    n_groups = len(group_cols)
    nc = len(consts)

    def body(x_ref, xp_ref, w_ref, *refs):
        const_refs, out_refs = refs[:nc], refs[nc:]
        first = pl.program_id(0) == 0
        shared = [_strip_conv(x_ref, xp_ref, w_ref, sl, first, taps)[2] for sl in shared_cols]
        for j, cols in enumerate(group_cols):
            sl = slice(STRIP * j, STRIP * (j + 1))
            convs = [_strip_conv(x_ref, xp_ref, w_ref, c, first, taps)[2] for c in cols]
            outs = fn(*convs, *shared, *[c[:, sl] for c in const_refs])
            for o_ref, o in zip(out_refs, outs):
                o_ref[:, sl] = o

    return pl.pallas_call(
        body, grid=(T // tm,),
        in_specs=[_row_spec(tm, W), _prev_spec(tm, W), _full_spec(w.shape)] + [_full_spec(c.shape) for c in consts],
        out_specs=[_row_spec(tm, STRIP * n_groups)] * n_out,
        out_shape=[jax.ShapeDtypeStruct((T, STRIP * n_groups), f32)] * n_out, name=name,
        compiler_params=_params("parallel"))(x, x, w, *consts)


def _group_bwd(name, fn, x, w, shared_cols, group_cols, consts, cots, tm):
    T, W = x.shape
    taps = w.shape[0]
    nblk = T // tm
    nc, ns = len(consts), len(shared_cols)
    flat_cots = [c for grp in cots for c in grp]
    n_cot = len(flat_cots)

    def body(x_ref, xp_ref, xn_ref, w_ref, *refs):
        const_refs, refs = refs[:nc], refs[nc:]
        cot_refs, cotn_refs, refs = refs[:n_cot], refs[n_cot:2 * n_cot], refs[2 * n_cot:]
        dx_ref, dw_ref, const_out = refs[0], refs[1], refs[2:]
        i = pl.program_id(0)
        first, last = i == 0, i == nblk - 1

        @pl.when(first)
        def _():
            dw_ref[...] = jnp.zeros_like(dw_ref)
            for q in range(nc):
                const_out[q][...] = jnp.zeros_like(const_out[q])

        def convs_of(sl):
            cur, down, conv = _strip_conv(x_ref, xp_ref, w_ref, sl, first, taps)
            nxt, conv_n = xn_ref[:, sl], None
            for k in range(taps):
                term = _shift_down(nxt, cur[tm - HALO:], taps - 1 - k) * w_ref[pl.ds(k, 1), sl]
                conv_n = term if conv_n is None else conv_n + term
            return down, conv, conv_n

        def conv_back(sl, down, dc, dc_n):
            dx = None
            for k in range(taps):
                s_ = taps - 1 - k
                term = _shift_up(dc, dc_n, s_) * w_ref[pl.ds(k, 1), sl]
                dx = term if dx is None else dx + term
                dw_ref[pl.ds(k, 1), sl] += jnp.sum(dc * down[s_], axis=0, keepdims=True)
            dx_ref[:, sl] = dx.astype(dx_ref.dtype)

        def summed(refs_, sl, mask):
            out, p = [], 0
            for grp in cots:
                acc = refs_[p][:, sl]
                for t in range(1, len(grp)):
                    acc = acc + refs_[p + t][:, sl]
                p += len(grp)
                out.append(jnp.where(last, 0.0, acc) if mask else acc)
            return tuple(out)

        shared = [convs_of(sl) for sl in shared_cols]
        d_shared, d_shared_n = [None] * ns, [None] * ns
        for j, cols in enumerate(group_cols):
            sl = slice(STRIP * j, STRIP * (j + 1))
            mine = [convs_of(c) for c in cols]
            cj = [c[:, sl] for c in const_refs]
            _, vjp = jax.vjp(fn, *[m[1] for m in mine], *[m[1] for m in shared], *cj)
            grads = vjp(summed(cot_refs, sl, False))
            _, vjp_n = jax.vjp(fn, *[m[2] for m in mine], *[m[2] for m in shared], *cj)
            grads_n = vjp_n(summed(cotn_refs, sl, True))
            for q, c in enumerate(cols):
                conv_back(c, mine[q][0], grads[q], grads_n[q])
            for q in range(ns):
                g, gn = grads[len(cols) + q], grads_n[len(cols) + q]
                d_shared[q] = g if d_shared[q] is None else d_shared[q] + g
                d_shared_n[q] = gn if d_shared_n[q] is None else d_shared_n[q] + gn
            for q in range(nc):
                const_out[q][:, sl] += grads[len(cols) + ns + q]
        for q, c in enumerate(shared_cols):
            conv_back(c, shared[q][0], d_shared[q], d_shared_n[q])

    outs = pl.pallas_call(
        body, grid=(nblk,),
        in_specs=[_row_spec(tm, W), _prev_spec(tm, W), _next_spec(tm, W, T), _full_spec(w.shape)]
        + [_full_spec(c.shape) for c in consts] + [_row_spec(tm, c.shape[1]) for c in flat_cots]
        + [_next_spec(tm, c.shape[1], T) for c in flat_cots],
        out_specs=[_row_spec(tm, W), _full_spec(w.shape)] + [_full_spec(c.shape) for c in consts],
        out_shape=[jax.ShapeDtypeStruct((T, W), bf16), jax.ShapeDtypeStruct(w.shape, f32)]
        + [jax.ShapeDtypeStruct(c.shape, f32) for c in consts], name=name,
        compiler_params=_params("arbitrary"))(x, x, x, w, *consts, *flat_cots, *flat_cots)
    return outs[0], outs[1], list(outs[2:])


def _ffn_strip_fn(cg, cu):
    return cg * _sigmoid(cg) * cu


def _ffn_act_fwd(h, w, tm):
    T, W2 = h.shape
    H = W2 // 2
    taps = w.shape[0]

    def body(h_ref, hp_ref, w_ref, o_ref):
        first = pl.program_id(0) == 0
        for j in range(H // STRIP):
            gs, us = slice(STRIP * j, STRIP * (j + 1)), slice(H + STRIP * j, H + STRIP * (j + 1))
            cg = _strip_conv(h_ref, hp_ref, w_ref, gs, first, taps)[2]
            cu = _strip_conv(h_ref, hp_ref, w_ref, us, first, taps)[2]
            o_ref[:, gs] = _ffn_strip_fn(cg, cu).astype(o_ref.dtype)

    return pl.pallas_call(
        body, grid=(T // tm,), in_specs=[_row_spec(tm, W2), _prev_spec(tm, W2), _full_spec(w.shape)],
        out_specs=_row_spec(tm, H), out_shape=jax.ShapeDtypeStruct((T, H), bf16), name="ffn_act",
        compiler_params=_params("parallel"))(h, h, w)


def _ffn_act_bwd(h, dact, w, tm):
    T, W2 = h.shape
    H = W2 // 2
    taps = w.shape[0]
    nblk = T // tm

    def body(h_ref, hp_ref, hn_ref, d_ref, dn_ref, w_ref, dh_ref, dw_ref):
        i = pl.program_id(0)
        first, last = i == 0, i == nblk - 1

        @pl.when(first)
        def _():
            dw_ref[...] = jnp.zeros_like(dw_ref)

        for j in range(H // STRIP):
            gs, us = slice(STRIP * j, STRIP * (j + 1)), slice(H + STRIP * j, H + STRIP * (j + 1))
            parts = {}
            for name, sl in (('g', gs), ('u', us)):
                cur, down, conv = _strip_conv(h_ref, hp_ref, w_ref, sl, first, taps)
                nxt = hn_ref[:, sl]
                conv_n = None
                for k in range(taps):
                    term = _shift_down(nxt, cur[tm - HALO:], taps - 1 - k) * w_ref[pl.ds(k, 1), sl]
                    conv_n = term if conv_n is None else conv_n + term
                parts[name] = (down, conv, conv_n)
            _, vjp = jax.vjp(_ffn_strip_fn, parts['g'][1], parts['u'][1])
            dcs = vjp(d_ref[:, gs])
            _, vjp_n = jax.vjp(_ffn_strip_fn, parts['g'][2], parts['u'][2])
            dcs_n = vjp_n(jnp.where(last, 0.0, dn_ref[:, gs]))
            for (name, sl), dc, dc_n in zip((('g', gs), ('u', us)), dcs, dcs_n):
                down = parts[name][0]
                dx = None
                for k in range(taps):
                    s_ = taps - 1 - k
                    term = _shift_up(dc, dc_n, s_) * w_ref[pl.ds(k, 1), sl]
                    dx = term if dx is None else dx + term
                    dw_ref[pl.ds(k, 1), sl] += jnp.sum(dc * down[s_], axis=0, keepdims=True)
                dh_ref[:, sl] = dx.astype(dh_ref.dtype)

    return pl.pallas_call(
        body, grid=(nblk,),
        in_specs=[_row_spec(tm, W2), _prev_spec(tm, W2), _next_spec(tm, W2, T), _row_spec(tm, H), _next_spec(tm, H, T),
                  _full_spec(w.shape)],
        out_specs=[_row_spec(tm, W2), _full_spec(w.shape)],
        out_shape=[jax.ShapeDtypeStruct((T, W2), bf16), jax.ShapeDtypeStruct(w.shape, f32)], name="ffn_act_bwd",
        compiler_params=_params("arbitrary"))(h, h, h, dact, dact, w)


N_POS = 4


def _xy_out_shapes(bufs, scatter):
    return [jax.ShapeDtypeStruct((N_POS,) + tuple(b.shape[1:] if scatter else b.shape), b.dtype) for b in bufs]


def _xy_sems(n, scatter):
    sems = [pltpu.SemaphoreType.DMA((3 * n,)), pltpu.SemaphoreType.DMA((3 * n,)), pltpu.SemaphoreType.DMA((n,))]
    return sems if scatter else sems + [pltpu.SemaphoreType.DMA((3 * n,)), pltpu.SemaphoreType.DMA((3 * n,))]


def _xy_copies(in_refs, out_refs, sems, scatter):
    n = len(in_refs)
    send_sems, recv_sems, local_sems = sems[:3]

    def place():
        x, y, c = lax.axis_index("x"), lax.axis_index("y"), lax.axis_index("c")
        return x, y, c, 2 * x + y, [(1 - x, y), (x, 1 - y), (1 - x, 1 - y)]

    def half(ref, a, which):
        rows = in_refs[a].shape[0] // 2
        return ref.at[pl.ds(pl.multiple_of(which * rows, HALO), rows)]

    def ici(a, k, src, dst, peer, c):
        return pltpu.make_async_remote_copy(
            src_ref=src, dst_ref=dst, send_sem=send_sems.at[3 * a + k], recv_sem=recv_sems.at[3 * a + k],
            device_id=(peer[0], peer[1], c), device_id_type=pl.DeviceIdType.MESH)

    def outgoing():
        x, y, c, me, peers = place()
        own = [pltpu.make_async_copy(in_refs[a].at[me] if scatter else in_refs[a], out_refs[a].at[me], local_sems.at[a])
               for a in range(n)]
        if scatter:
            sends = [ici(a, k, in_refs[a].at[2 * p[0] + p[1]], out_refs[a].at[me], p, c)
                     for a in range(n) for k, p in enumerate(peers)]
        else:
            sends = [ici(a, k, half(in_refs[a], a, c), half(out_refs[a].at[me], a, c), p, c)
                     for a in range(n) for k, p in enumerate(peers)]
        return own, sends

    def arrivals():
        x, y, c, me, peers = place()
        if scatter:
            return [ici(a, k, in_refs[a].at[me], out_refs[a].at[2 * p[0] + p[1]], p, c)
                    for a in range(n) for k, p in enumerate(peers)]
        return [ici(a, k, half(in_refs[a], a, c), half(out_refs[a].at[2 * p[0] + p[1]], a, c), p, c)
                for a in range(n) for k, p in enumerate(peers)]

    def to_sibling(mine):
        x, y, c, me, peers = place()
        which = c if mine else 1 - c
        return [pltpu.make_async_remote_copy(
            src_ref=half(out_refs[a].at[2 * p[0] + p[1]], a, which), dst_ref=half(out_refs[a].at[2 * p[0] + p[1]], a, which),
            send_sem=sems[3].at[3 * a + k], recv_sem=sems[4].at[3 * a + k],
            device_id=(x, y, 1 - c), device_id_type=pl.DeviceIdType.MESH) for a in range(n) for k, p in enumerate(peers)]

    def start():
        own, sends = outgoing()
        for cp in own + sends:
            cp.start()

    def finish():
        if scatter:
            for cp in arrivals():
                cp.wait_recv()
        else:
            passed = to_sibling(True)
            for cp, fwd in zip(arrivals(), passed):
                cp.wait_recv()
                fwd.start()
            for cp in to_sibling(False):
                cp.wait_recv()
            for fwd in passed:
                fwd.wait_send()
        own, sends = outgoing()
        for cp in sends:
            cp.wait_send()
        for cp in own:
            cp.wait()

    return start, finish


_NN, _NT, _TN = 'hcs,hsd->hcd', 'hcd,hsd->hcs', 'hcd,hce->hde'


def _lo(spec, a, b):
    return jnp.einsum(spec, a.astype(bf16), b.astype(bf16), preferred_element_type=f32)


@jax.custom_vjp
def _bmm(a, b):
    return _lo(_NN, a, b)


_bmm.defvjp(lambda a, b: (_lo(_NN, a, b), (a, b)), lambda ab, g: (_lo(_NT, g, ab[1]), _lo(_TN, ab[0], g)))


@jax.custom_vjp
def _bmm_nt(a, b):
    return _lo(_NT, a, b)


_bmm_nt.defvjp(lambda a, b: (_lo(_NT, a, b), (a, b)), lambda ab, g: (_lo(_NN, g, ab[1]), _lo(_TN, g, ab[0])))


@jax.custom_vjp
def _bmm_tn(a, b):
    return _lo(_TN, a, b)


_bmm_tn.defvjp(lambda a, b: (_lo(_TN, a, b), (a, b)), lambda ab, g: (_lo(_NT, ab[1], g), _lo(_NN, ab[0], g)))


def _masks(H, C):
    row = lax.broadcasted_iota(jnp.int32, (H, C, C), 1)
    col = lax.broadcasted_iota(jnp.int32, (H, C, C), 2)
    return row, col


def _tri_inv_impl(L):
    H, C, _ = L.shape
    row, col = _masks(H, C)
    eye = (row == col).astype(f32)
    base = 16
    same = (row // base) == (col // base)
    Ld = jnp.where(same, L, 0.0)
    X = -Ld
    inv = eye + X
    for _ in range(3):
        X = _bmm(X, X)
        inv = _bmm(inv, eye + X)
    if C == base:
        return inv
    N = _bmm(inv, L - Ld)
    out = eye - N
    levels = C // base
    P = N
    span = 2
    while span < levels:
        P = _bmm(P, P)
        out = _bmm(out, eye + P)
        span *= 2
    return _bmm(out, inv)


@jax.custom_vjp
def _tri_inv(L):
    return _tri_inv_impl(L)


def _tri_inv_fwd(L):
    T = _tri_inv_impl(L)
    return T, T


def _tri_inv_bwd(T, dT):
    return (-_bmm_nt(_bmm_tn(T, dT), T),)


_tri_inv.defvjp(_tri_inv_fwd, _tri_inv_bwd)


@jax.custom_vjp
def _tri_inv_known(L, T):
    return T


_tri_inv_known.defvjp(lambda L, T: (T, T), lambda T, dT: (_tri_inv_bwd(T, dT)[0], jnp.zeros_like(T)))


def _cumsum_impl(x, reverse):
    C = x.shape[1]
    row = lax.broadcasted_iota(jnp.int32, x.shape, 1)
    s = 1
    while s < C:
        if reverse:
            x = x + jnp.where(row < C - s, pltpu.roll(x, C - s, 1), 0.0)
        else:
            x = x + jnp.where(row >= s, pltpu.roll(x, s, 1), 0.0)
        s *= 2
    return x


@jax.custom_vjp
def _cumsum(x):
    return _cumsum_impl(x, False)


_cumsum.defvjp(lambda x: (_cumsum_impl(x, False), None), lambda _, g: (_cumsum_impl(g, True),))


def _wkv_prep(r, lw, k, v, a, b, inv=None):
    lane = lax.broadcasted_iota(jnp.int32, (r.shape[0], 128), 1)
    low = lane < RWKV_HD

    def heads(t):
        out = []
        for p in range(RWKV_HEADS // 2):
            pair = t[:, 128 * p:128 * (p + 1)]
            out += [jnp.where(low, pair, 0.0), jnp.where(low, 0.0, pair)]
        return jnp.concatenate([t[None] for t in out], axis=0)

    r, lw, k, v, a, b = [heads(t) for t in (r, lw, k, v, a, b)]
    H, C, D = r.shape
    row, col = _masks(H, C)
    incl, strict = row >= col, row > col
    cw = _cumsum(lw)
    cwp = cw - lw
    cwl = jnp.sum(lw, axis=1, keepdims=True)
    en = jnp.exp(-cw)
    at, rt, bt, kt = a * jnp.exp(cwp), r * jnp.exp(cw), b * en, k * en
    Lab = -jnp.where(strict, _bmm_nt(at, bt), 0.0)
    Tm = _tri_inv(Lab) if inv is None else _tri_inv_known(Lab, inv)
    ar = jnp.concatenate([at, rt], axis=1)
    gram = _bmm_nt(ar, jnp.concatenate([bt, kt], axis=1))
    row2 = lax.broadcasted_iota(jnp.int32, (H, 2 * C, 2 * C), 1)
    col2 = lax.broadcasted_iota(jnp.int32, (H, 2 * C, 2 * C), 2) % C
    gram = jnp.where(((row2 < C) & (row2 > col2)) | ((row2 >= C) & (row2 - C >= col2)), gram, 0.0)
    a_bk, r_bk = gram[:, :C], gram[:, C:]
    lak_v = _bmm(a_bk, jnp.concatenate([jnp.zeros_like(v), v], axis=1))
    ed = jnp.exp(cwl - cw)
    zdec = jnp.swapaxes(jnp.broadcast_to(jnp.exp(cwl), (H, D, D)), 1, 2)
    return (ar, Tm, lak_v, r_bk, jnp.concatenate([b * ed, k * ed], axis=1), zdec, v), Tm


def _wkv_step(Z, ar, Tm, lak_v, r_bk, bk_d, zdec, v):
    C = Tm.shape[1]
    ar_z = _bmm(ar, Z)
    uv = jnp.concatenate([_bmm(Tm, ar_z[:, :C] + lak_v), v], axis=1)
    y = ar_z[:, C:] + _bmm(r_bk, uv)
    Z1 = Z * zdec + _bmm_tn(bk_d, uv)
    return jnp.concatenate([y[2 * p] + y[2 * p + 1] for p in range(RWKV_HEADS // 2)], axis=1), Z1


def _split3(x):
    hi = x.astype(bf16)
    mid = (x - hi.astype(f32)).astype(bf16)
    lo = (x - hi.astype(f32) - mid.astype(f32)).astype(bf16)
    return hi, mid, lo


@jax.custom_vjp
def _spread(x, sel):
    return sum(jnp.dot(t, sel, preferred_element_type=f32) for t in _split3(x))


def _spread_bwd(sel, g):
    dn = (((1,), (1,)), ((), ()))
    return sum(lax.dot_general(t, sel, dn, preferred_element_type=f32) for t in _split3(g)), None


_spread.defvjp(lambda x, sel: (_spread(x, sel), sel), _spread_bwd)


def _gdn_prep(q, k, v, gbeta, inv=None):
    heads = lambda t: jnp.concatenate([t[None, :, GDN_HD * h:GDN_HD * (h + 1)] for h in range(GDN_HEADS)], axis=0)
    src = lax.broadcasted_iota(jnp.int32, (W_AB, 2 * GDN_W), 0)
    dst = lax.broadcasted_iota(jnp.int32, (W_AB, 2 * GDN_W), 1) // GDN_HD
    spread = _spread(gbeta, (src == dst).astype(bf16))
    q, k, v, g, beta = heads(q), heads(k), heads(v), heads(spread[:, :GDN_W]), heads(spread[:, GDN_W:])
    H, C, D = q.shape
    row, col = _masks(H, C)
    incl, strict = row >= col, row > col
    gc = _cumsum(g)
    diff = gc - jnp.swapaxes(gc, 1, 2)
    decay = jnp.where(incl, jnp.exp(jnp.where(incl, diff, 0.0)), 0.0)
    gl = jnp.sum(g, axis=1, keepdims=True)
    kb, vb = k * beta, v * beta
    gram = _bmm_nt(jnp.concatenate([kb, q], axis=1), k)
    L = jnp.where(strict, gram[:, :C] * decay, 0.0)
    attn = jnp.where(incl, gram[:, C:] * decay, 0.0)
    egc = jnp.exp(gc)
    Tm = _tri_inv(L) if inv is None else _tri_inv_known(L, inv)
    t_vk = _bmm(Tm, jnp.concatenate([vb, kb * egc], axis=2))
    return (t_vk[:, :, :D], jnp.concatenate([t_vk[:, :, D:], q * egc], axis=1), attn, k * jnp.exp(gl - gc), jnp.exp(gl)), Tm


def _gdn_step(S, u, wq, attn, ke, sdec):
    C = u.shape[1]
    wq_s = _bmm(wq, S)
    v_new = u - wq_s[:, :C]
    o = wq_s[:, C:] + _bmm(attn, v_new)
    S1 = S * sdec + _bmm_tn(ke, v_new)
    return jnp.concatenate([o[h] for h in range(GDN_HEADS)], axis=1), S1


def _scan_fwd(name, fns, ins, C, H, dh, w_out, per_step, side=None):
    prep, step = fns
    T = ins[0].shape[0]
    n_in = len(ins)
    blk = C * per_step
    nblk = T // blk
    n_side = 0 if side is None else len(side[0])

    def body(*refs):
        in_refs, refs = refs[:n_in], refs[n_in:]
        side_in, refs = refs[:n_side], refs[n_side:]
        y_ref, zs_ref, inv_ref, refs = refs[0], refs[1], refs[2], refs[3:]
        side_out, refs = refs[:n_side], refs[n_side:]
        z_scr = refs[0]
        if side is not None:
            start, finish = _xy_copies(side_in, side_out, refs[1:], side[1])
            pl.when(pl.program_id(0) == 0)(start)

        @pl.when(pl.program_id(0) == 0)
        def _():
            z_scr[...] = jnp.zeros_like(z_scr)

        rows = [slice(C * j, C * (j + 1)) for j in range(per_step)]
        prepped = [prep(*[r[rw, :] for r in in_refs]) for rw in rows]
        Z = z_scr[...]
        for j, rw in enumerate(rows):
            zs_ref[j] = Z
            inv_ref[j] = prepped[j][1]
            y, Z = step(Z, *prepped[j][0])
            y_ref[rw, :] = y
        z_scr[...] = Z
        if side is not None:
            pl.when(pl.program_id(0) == nblk - 1)(finish)

    side_bufs = [] if side is None else list(side[0])
    any_spec = pl.BlockSpec(memory_space=pl.ANY)
    return pl.pallas_call(
        body, grid=(nblk,),
        in_specs=[pl.BlockSpec((blk, a.shape[1]), lambda i: (i, 0)) for a in ins] + [any_spec] * n_side,
        out_specs=[pl.BlockSpec((blk, w_out), lambda i: (i, 0)), pl.BlockSpec((per_step, H, dh, dh), lambda i: (i, 0, 0, 0)),
                   pl.BlockSpec((per_step, H, C, C), lambda i: (i, 0, 0, 0))] + [any_spec] * n_side,
        out_shape=[jax.ShapeDtypeStruct((T, w_out), f32), jax.ShapeDtypeStruct((T // C, H, dh, dh), f32),
                   jax.ShapeDtypeStruct((T // C, H, C, C), f32)]
        + (_xy_out_shapes(side_bufs, side[1]) if side is not None else []),
        scratch_shapes=[pltpu.VMEM((H, dh, dh), f32)] + (_xy_sems(n_side, side[1]) if side is not None else []), name=name,
        compiler_params=_params("arbitrary"))(*ins, *side_bufs)


def _scan_bwd(name, fns, ins, dy, zs, invs, C, per_step, side=None):
    prep, step = fns
    T = ins[0].shape[0]
    _, H, dh, _ = zs.shape
    n_in = len(ins)
    blk = C * per_step
    nblk = T // blk
    n_side = 0 if side is None else len(side[0])

    def body(*refs):
        in_refs, dy_ref, zs_ref, inv_ref, refs = refs[:n_in], refs[n_in], refs[n_in + 1], refs[n_in + 2], refs[n_in + 3:]
        side_in, refs = refs[:n_side], refs[n_side:]
        out_refs, refs = refs[:n_in], refs[n_in:]
        side_out, refs = refs[:n_side], refs[n_side:]
        dz_scr = refs[0]
        if side is not None:
            start, finish = _xy_copies(side_in, side_out, refs[1:], side[1])
            pl.when(pl.program_id(0) == 0)(start)

        @pl.when(pl.program_id(0) == 0)
        def _():
            dz_scr[...] = jnp.zeros_like(dz_scr)

        rows = [slice(C * j, C * (j + 1)) for j in range(per_step)]
        prepped = [jax.vjp(lambda *a, j=j: prep(*a, inv=inv_ref[j])[0], *[r[rw, :] for r in in_refs])
                   for j, rw in enumerate(rows)]
        d_prepped = [None] * per_step
        dZ = dz_scr[...]
        for j in reversed(range(per_step)):
            _, pull = jax.vjp(step, zs_ref[j], *prepped[j][0])
            dZ, *d_prepped[j] = pull((dy_ref[rows[j], :], dZ))
        dz_scr[...] = dZ
        for j, rw in enumerate(rows):
            for o_ref, gval in zip(out_refs, prepped[j][1](tuple(d_prepped[j]))):
                o_ref[rw, :] = gval
        if side is not None:
            pl.when(pl.program_id(0) == nblk - 1)(finish)

    side_bufs = [] if side is None else list(side[0])
    any_spec = pl.BlockSpec(memory_space=pl.ANY)
    rev = lambda i: (nblk - 1 - i, 0)
    return pl.pallas_call(
        body, grid=(nblk,),
        in_specs=[pl.BlockSpec((blk, a.shape[1]), rev) for a in ins]
        + [pl.BlockSpec((blk, dy.shape[1]), rev), pl.BlockSpec((per_step, H, dh, dh), lambda i: (nblk - 1 - i, 0, 0, 0)),
           pl.BlockSpec((per_step, H, C, C), lambda i: (nblk - 1 - i, 0, 0, 0))] + [any_spec] * n_side,
        out_specs=[pl.BlockSpec((blk, a.shape[1]), rev) for a in ins] + [any_spec] * n_side,
        out_shape=[jax.ShapeDtypeStruct(a.shape, f32) for a in ins]
        + (_xy_out_shapes(side_bufs, side[1]) if side is not None else []),
        scratch_shapes=[pltpu.VMEM((H, dh, dh), f32)] + (_xy_sems(n_side, side[1]) if side is not None else []), name=name,
        compiler_params=_params("arbitrary"))(*ins, dy, zs, invs, *side_bufs)


def _residual_mm(name, a, b, res, tail, row_extras, consts, row_out, acc_out, tm, head=None):
    K, N = b.shape
    h_rows, h_consts = ([], []) if head is None else (list(head[1]), list(head[2]))
    lhs = h_rows + h_consts if head is not None else [a]
    M = lhs[0].shape[0]
    row_extras = [_row_of(e) for e in row_extras]
    n_lhs, n_res = len(lhs), 0 if res is None else 1
    ne, nc, nr = len(row_extras), len(consts), len(row_out)

    def body(*refs):
        lhs_refs, refs = refs[:n_lhs], refs[n_lhs:]
        b_ref, refs = refs[0], refs[1:]
        res_ref, refs = (refs[0], refs[1:]) if res is not None else (None, refs)
        extra_refs, const_refs, out_refs = refs[:ne], refs[ne:ne + nc], refs[ne + nc:]
        if head is not None:
            left = head[0](*[r[...] for r in lhs_refs])[0].astype(bf16)
            out_refs[0][...] = left
            out_refs = out_refs[1:]
        else:
            left = lhs_refs[0][...].astype(bf16)
        tile = jnp.dot(left, b_ref[...].astype(bf16), preferred_element_type=f32)
        if res is not None:
            tile = res_ref[...] + tile
        outs = tail(tile, *[r[...] for r in extra_refs], *[c[...] for c in const_refs])
        for o_ref, o in zip(out_refs[:nr], outs[:nr]):
            o_ref[...] = o.astype(o_ref.dtype)

        @pl.when(pl.program_id(0) == 0)
        def _():
            for o_ref in out_refs[nr:]:
                o_ref[...] = jnp.zeros_like(o_ref)

        for o_ref, o in zip(out_refs[nr:], outs[nr:]):
            o_ref[...] += o

    lhs_specs = ([_row_spec(tm, r.shape[1]) for r in h_rows] + [_full_spec(c.shape) for c in h_consts]
                 if head is not None else [_row_spec(tm, K)])
    head_out = [(K, bf16)] if head is not None else []
    outs = pl.pallas_call(
        body, grid=(M // tm,),
        in_specs=lhs_specs + [_full_spec(b.shape)] + ([_row_spec(tm, N)] if res is not None else [])
        + [_row_spec(tm, e[1], e[2]) for e in row_extras] + [_full_spec(c.shape) for c in consts],
        out_specs=[_row_spec(tm, w) for w, _ in head_out + list(row_out)] + [_full_spec(sh) for sh in acc_out],
        out_shape=[jax.ShapeDtypeStruct((M, w), d) for w, d in head_out + list(row_out)]
        + [jax.ShapeDtypeStruct(sh, f32) for sh in acc_out],
        name=name, compiler_params=_params("arbitrary"))(
            *lhs, b, *([res] if res is not None else []), *[e[0] for e in row_extras], *consts)
    return outs


def _pull_tail(fn):
    def tail(cot, *args):
        _, vjp = jax.vjp(fn, *args)
        return vjp((cot,))
    return tail


def _norm_tail(x1, g):
    return x1, _rms(x1, g)


def _loss_tail(x2, tgt, g):
    l, vjp = jax.vjp(lambda xv, gv: _loss_rows(xv, tgt, gv), x2, g)
    dx, dg = vjp(jnp.ones_like(l))
    return dx, dg, jnp.zeros((1, 128), f32) + jnp.sum(l)


def _local_step(x, tgt, W, late=None):
    row = lambda a: a.reshape(1, -1)
    wp = W['w_in_pad']
    w_rwkv, w_qkv, w_z = wp[:, :OFF_QKV], wp[:, OFF_QKV:OFF_Z], wp[:, OFF_Z:OFF_GATES]
    w_gates, w_ab = wp[:, OFF_GATES:OFF_AB], wp[:, OFF_AB:]
    mu = row(W['rwkv_mu'])
    mixw = jnp.concatenate([mu, 1.0 - mu], axis=0)
    zpad = jnp.zeros((64, RWKV_W), f32)
    w2p = jnp.concatenate([W['rwkv_w2'], zpad], axis=0)
    a2p = jnp.concatenate([zpad, W['rwkv_a2']], axis=0)
    rw_consts = [row(W['rwkv_w0']), w2p, row(W['rwkv_a0']), a2p, W['rwkv_g2'], row(W['rwkv_k_k']), row(W['rwkv_k_a'])]
    post_consts = [row(W['rwkv_ln_w']), row(W['rwkv_ln_b']), row(W['rwkv_r_k'])]
    pad4 = lambda a: jnp.pad(row(a), ((0, 0), (0, W_AB - GDN_HEADS)))
    gd_consts = [pad4(W['gdn_a_log']), pad4(W['gdn_dt_bias'])]
    nw_t = jnp.tile(row(W['gdn_norm_w']), (1, GDN_HEADS))
    g1, g2n, gf = row(W['norm1_g']), row(W['norm2_g']), row(W['final_g'])

    (u,) = _pw_fwd("norm1", _rms_fn, [x], [g1], [D_MODEL], 512, out_dtype=bf16)
    p_rwkv = _mm(u, w_rwkv, 'nn', "in_rwkv")
    qkv_raw = _mm(u, w_qkv, 'nn', "in_qkv")
    z = _mm(u, w_z, 'nn', "in_z")
    gates = _mm(u, w_gates, 'nn', "in_gates")
    ab = _mm(u, w_ab, 'nn', "in_ab")

    r, lw, k2, v, a_, b_, g = _pw_fwd("rwkv_prep", _rwkv_prep_fn, [p_rwkv], rw_consts, [RWKV_W] * 7, 256, conv_w=mixw)
    wkv_in = [r, lw, k2, v, a_, b_]
    y, zs_wkv, inv_wkv, *gathered = _scan_fwd("wkv_fwd", (_wkv_prep, _wkv_step), wkv_in, WKV_CHUNK, RWKV_HEADS, 2 * RWKV_HD, RWKV_W, WKV_PER_STEP,
                                     side=None if late is None else (late['shards'][0], False))
    if late is not None:
        W = dict(W, **late['assemble'](0, gathered))
    ya_in, ya = _residual_mm("rwkv_proj", None, W['rwkv_proj'], None, lambda t: (t,), [], [], [(D_MODEL, f32)], [], 512,
                             head=(_rwkv_post_fn, [y, r, k2, v, g], post_consts))

    lanes = lambda off: slice(off, off + STRIP)
    gd_groups = [[lanes(GDN_HD * h), lanes(GDN_W + GDN_HD * h), lanes(2 * GDN_W + GDN_HD * h)] for h in range(GDN_HEADS)]
    gq, gk, gv = _group_fwd("gdn_prep", _gdn_prep_fn, qkv_raw, W['gdn_conv_w'], [], gd_groups, [], 3, 256)
    (gbeta,) = _pw_fwd("gdn_gate", _gdn_gate_fn, [ab], gd_consts, [W_AB], 512)
    gdn_in = [gq, gk, gv, gbeta]
    o, zs_gdn, inv_gdn, *gathered = _scan_fwd("gdn_fwd", (_gdn_prep, _gdn_step), gdn_in, GDN_CHUNK, GDN_HEADS, GDN_HD, GDN_W, GDN_PER_STEP,
                                     side=None if late is None else (late['shards'][1], False))
    if late is not None:
        W = dict(W, **late['assemble'](1, gathered))
    ga, gb = _cols(gates, D_MODEL, 0), _cols(gates, D_MODEL, 1)
    yb_in, yb, mixed = _residual_mm("gdn_proj", None, W['gdn_proj'], None, lambda t, a_, b_, c_: (t,) + _mix_fn(a_, b_, c_, t),
                                    [ga, gb, ya], [], [(D_MODEL, f32), (D_MODEL, bf16)], [], 512,
                                    head=(_gdn_post_fn, [o, z], [nw_t]))

    x1, u2 = _residual_mm("w_out", mixed, W['w_out'], x, _norm_tail, [], [g2n], [(D_MODEL, f32), (D_MODEL, bf16)], [], 512)
    h = _mm(u2, W['ffn_up'], 'nn', "ffn_up")
    act = _ffn_act_fwd(h, W['ffn_conv_w'], 256)

    G = {}
    slab_out = None if late is None else N_POS
    dx2, dgf, loss = _residual_mm("ffn_down", act, W['ffn_down'], x1, _loss_tail, [tgt], [gf], [(D_MODEL, f32)],
                                  [gf.shape, (1, 128)], 512)
    G['final_g'] = dgf
    dact = _mm(dx2, W['ffn_down'], 'nt', "d_act")
    G['ffn_down'] = _mm(act, dx2, 'tn', "g_ffn_down", out_dtype=bf16)
    dh, G['ffn_conv_w'] = _ffn_act_bwd(h, dact, W['ffn_conv_w'], 128)
    du2 = _mm(dh, W['ffn_up'], 'nt', "d_u2")
    G['ffn_up'] = _mm(u2, dh, 'tn', "g_ffn_up", out_dtype=bf16, col_slabs=slab_out)
    (dx1,), (G['norm2_g'],) = _pw_bwd("norm2_bwd", _rms_fn, [x1], [g2n], [(du2,)], 512, add_to_first=dx2)
    G['w_out'] = _mm(mixed, dx1, 'tn', "g_w_out", out_dtype=bf16)
    dga, dgb, dya, dyb = _residual_mm("d_mixed", dx1, W['w_out'].T, None, _pull_tail(_mix_fn), [ga, gb, ya, yb], [],
                                      [(D_MODEL, bf16)] * 4, [], 512)
    G['rwkv_proj'] = _mm(ya_in, dya, 'tn', "g_rwkv_proj", out_dtype=bf16, col_slabs=slab_out)
    G['gdn_proj'] = _mm(yb_in, dyb, 'tn', "g_gdn_proj", out_dtype=bf16, col_slabs=slab_out)

    do, dz, dnw_t = _residual_mm("d_yb_in", dyb, W['gdn_proj'].T, None, _pull_tail(_gdn_post_fn), [o, z], [nw_t],
                                 [(GDN_W, f32), (GDN_W, bf16)], [nw_t.shape], 512)
    G['gdn_norm_w'] = dnw_t.reshape(GDN_HEADS, GDN_HD).sum(axis=0)
    dgq, dgk, dgv, dgbeta, *arrived_b = _scan_bwd("gdn_bwd", (_gdn_prep, _gdn_step), gdn_in, do, zs_gdn, inv_gdn, GDN_CHUNK,
                                                  GDN_PER_STEP, side=None if late is None else (late['slabs'](G, 1), True))
    dqkv_raw, G['gdn_conv_w'], _ = _group_bwd("gdn_prep_bwd", _gdn_prep_fn, qkv_raw, W['gdn_conv_w'], [], gd_groups, [],
                                              [(dgq,), (dgk,), (dgv,)], 128)
    (dab,), (dal_p, ddt_p) = _pw_bwd("gdn_gate_bwd", _gdn_gate_fn, [ab], gd_consts, [(dgbeta,)], 512, row_dtypes=[bf16])
    G['gdn_a_log'], G['gdn_dt_bias'] = dal_p[0, :GDN_HEADS], ddt_p[0, :GDN_HEADS]

    dy, dr1, dk21, dv1, dg_, G['rwkv_ln_w'], G['rwkv_ln_b'], G['rwkv_r_k'] = _residual_mm(
        "d_ya_in", dya, W['rwkv_proj'].T, None, _pull_tail(_rwkv_post_fn), [y, r, k2, v, g], post_consts,
        [(RWKV_W, f32)] * 5, [c.shape for c in post_consts], 512)
    dr2, dlw, dk22, dv2, da_, db_, *arrived_a = _scan_bwd(
        "wkv_bwd", (_wkv_prep, _wkv_step), wkv_in, dy, zs_wkv, inv_wkv, WKV_CHUNK, WKV_PER_STEP,
        side=None if late is None else (late['slabs'](G, 0), True))
    G['_arrived'] = (arrived_a, arrived_b)
    (dp_rwkv,), dmixw, rw_grads = _pw_conv_bwd(
        "rwkv_prep_bwd", _rwkv_prep_fn, [p_rwkv], rw_consts,
        [(dr1, dr2), (dlw,), (dk21, dk22), (dv1, dv2), (da_,), (db_,), (dg_,)], mixw, 256, row_dtypes=[bf16])
    G['rwkv_w0'], dw2p, G['rwkv_a0'], da2p, G['rwkv_g2'], G['rwkv_k_k'], G['rwkv_k_a'] = rw_grads
    G['rwkv_w2'], G['rwkv_a2'] = dw2p[:64], da2p[64:]
    G['rwkv_mu'] = dmixw[0] - dmixw[1]

    dp = jnp.concatenate([dp_rwkv, dqkv_raw, dz, dga, dgb, dab], axis=1)
    G['w_in_pad'] = _mm(u, dp, 'tn', "g_w_in", out_dtype=bf16)
    if late is None:
        du = _mm(dp, wp, 'nt', "d_u")
    else:
        du, *G['_arrived_w_in'] = _mm(dp, wp, 'nt', "d_u", side=(late['w_in_slabs'](G), True))
    (dx,), (G['norm1_g'],) = _pw_bwd("norm1_bwd", _rms_fn, [x], [g1], [(du,)], 512, add_to_first=dx1)
    return loss, dx, G


IN_WIDTH = OFF_AB + 8
PAD_ORDER = ((0, OFF_GATES), (OFF_GATES + 8, IN_WIDTH), (OFF_GATES, OFF_GATES + 8))


def _pad_w_in_shards(shards):
    width = shards[0].shape[1]
    parts = []
    for a, b in PAD_ORDER:
        for j, sh in enumerate(shards):
            lo, hi = max(a, j * width), min(b, (j + 1) * width)
            if lo < hi:
                parts.append(sh[:, lo - j * width:hi - j * width])
    return jnp.concatenate(parts + [jnp.zeros((shards[0].shape[0], W_AB - 8), shards[0].dtype)], axis=1)


def _unpad_cols(wp, lo, hi):
    parts, off = [], 0
    for a, b in PAD_ORDER:
        l, h = max(a, lo), min(b, hi)
        if l < h:
            parts.append((l, wp[:, off + l - a:off + h - a]))
        off += b - a
    parts.sort(key=lambda t: t[0])
    return parts[0][1] if len(parts) == 1 else jnp.concatenate([p for _, p in parts], axis=1)


BIG = ('w_in', 'rwkv_proj', 'gdn_proj', 'w_out', 'ffn_up', 'ffn_down')
SMALL_SHARDED = ('rwkv_w2', 'rwkv_a2', 'rwkv_g2', 'gdn_conv_w', 'ffn_conv_w')


def _rows128(shape):
    n = 1
    for d in shape:
        n *= d
    return -(-n // LANES)


def _pack128(arrays):
    parts = []
    for a in arrays:
        flat = a.reshape(-1)
        rows = _rows128(a.shape)
        parts.append(jnp.pad(flat, (0, rows * LANES - flat.shape[0])).reshape(rows, LANES))
    buf = jnp.concatenate(parts, axis=0)
    return jnp.pad(buf, ((0, -buf.shape[0] % HALO), (0, 0)))


def _unpack128(buf, shapes):
    out, off = [], 0
    for s in shapes:
        rows, n = _rows128(s), 1
        for d in s:
            n *= d
        out.append(buf[off:off + rows].reshape(-1)[:n].reshape(s))
        off += rows
    return out


def _param_tile(r, c):
    best = None
    for d in range(2 * HALO, r + 1, 2 * HALO):
        if r % d == 0 and d * c * 4 <= TILE_BYTES:
            best = d
    if best is not None or r * c * 4 <= TILE_BYTES:
        return (best if best is not None else r), c
    return r, 128


def _xy_exchange(name, bufs, scatter):
    n = len(bufs)

    def body(*refs):
        start, finish = _xy_copies(refs[:n], refs[n:2 * n], refs[2 * n:], scatter)
        start()
        finish()

    return pl.pallas_call(
        body, in_specs=[pl.BlockSpec(memory_space=pl.ANY)] * n, out_specs=[pl.BlockSpec(memory_space=pl.ANY)] * n,
        out_shape=_xy_out_shapes(bufs, scatter), scratch_shapes=_xy_sems(n, scatter), name=name)(*bufs)


def _sibling_exchange(name, bufs):
    n = len(bufs)

    def body(*refs):
        in_refs, out_refs, send_sems, recv_sems = refs[:n], refs[n:2 * n], refs[2 * n], refs[2 * n + 1]
        x, y, c = lax.axis_index("x"), lax.axis_index("y"), lax.axis_index("c")
        copies = [pltpu.make_async_remote_copy(
            src_ref=in_refs[a], dst_ref=out_refs[a], send_sem=send_sems.at[a], recv_sem=recv_sems.at[a],
            device_id=(x, y, 1 - c), device_id_type=pl.DeviceIdType.MESH) for a in range(n)]
        for cp in copies:
            cp.start()
        for cp in copies:
            cp.wait()

    return pl.pallas_call(
        body, in_specs=[pl.BlockSpec(memory_space=pl.ANY)] * n, out_specs=[pl.BlockSpec(memory_space=pl.ANY)] * n,
        out_shape=[jax.ShapeDtypeStruct(b.shape, b.dtype) for b in bufs],
        scratch_shapes=[pltpu.SemaphoreType.DMA((n,)), pltpu.SemaphoreType.DMA((n,))], name=name)(*bufs)


def _sum_slots(name, buf):
    _, R, L = buf.shape
    tr, tc = _param_tile(R, L)

    def body(b_ref, o_ref):
        part = lambda s: b_ref[s].astype(f32)
        o_ref[...] = ((part(0) + part(1)) + part(2)) + part(3)

    return pl.pallas_call(
        body, grid=(R // tr, L // tc),
        in_specs=[pl.BlockSpec((N_POS, tr, tc), lambda i, j: (0, i, j))],
        out_specs=pl.BlockSpec((tr, tc), lambda i, j: (i, j)),
        out_shape=jax.ShapeDtypeStruct((R, L), f32), name=name,
        compiler_params=_params("parallel", "parallel"))(buf)


def _adamw(name, w, ga, gb, m, v):
    R, L = w.shape
    tr, tc = _param_tile(R, L)
    c1 = 1.0 / (1.0 - ADAM_B1 ** ADAM_STEP)
    c2 = 1.0 / (1.0 - ADAM_B2 ** ADAM_STEP)

    def body(w_ref, ga_ref, gb_ref, m_ref, v_ref, g_out, d_out, m_out, v_out):
        g = ga_ref[...] + gb_ref[...]
        m_new = ADAM_B1 * m_ref[...] + (1.0 - ADAM_B1) * g
        v_new = ADAM_B2 * v_ref[...] + (1.0 - ADAM_B2) * (g * g)
        g_out[...] = g
        m_out[...] = m_new
        v_out[...] = v_new
        d_out[...] = -ADAM_LR * ((m_new * c1) / (jnp.sqrt(v_new * c2) + ADAM_EPS) + ADAM_WD * w_ref[...])

    spec = pl.BlockSpec((tr, tc), lambda i, j: (i, j))
    return pl.pallas_call(
        body, grid=(R // tr, L // tc), in_specs=[spec] * 5, out_specs=[spec] * 4,
        out_shape=[jax.ShapeDtypeStruct((R, L), f32)] * 4, name=name,
        compiler_params=_params("parallel", "parallel"))(w, ga, gb, m, v)


def _step(x, loss_target, P, M, V):
    shapes = {n: tuple(P[n].shape) for n in WEIGHTS}
    sh_shapes = [shapes[n] for n in SMALL_SHARDED]
    packed = SMALL_SHARDED + SMALL

    def whole(n, g):
        return g.reshape(-1, g.shape[2]) if n in ROW_SHARDED else jnp.concatenate([g[j] for j in range(N_POS)], axis=1)

    def slabs(G, n, dtype=f32):
        r, c = shapes[n]
        full = G[n].astype(dtype)
        if full.ndim == 3:
            return full
        return full.reshape(N_POS, r, c) if n in ROW_SHARDED else full.reshape(r, N_POS, c).transpose(1, 0, 2)

    g_w_in, g_small = _xy_exchange("gather_w_in", [P['w_in'].astype(bf16), _pack128([P[n] for n in SMALL_SHARDED])],
                                   scatter=False)
    W = {n: P[n] for n in SMALL}
    W['w_in_pad'] = _pad_w_in_shards([g_w_in[j] for j in range(N_POS)])
    per_pos = [_unpack128(g_small[j], sh_shapes) for j in range(N_POS)]
    for q, n in enumerate(SMALL_SHARDED):
        W[n] = jnp.concatenate([per_pos[j][q] for j in range(N_POS)], axis=1)
    groups = (('rwkv_proj', 'gdn_proj', 'ffn_up'), ('w_out', 'ffn_down'))
    late = dict(shards=[[P[n].astype(bf16) for n in grp] for grp in groups],
                assemble=lambda q, gathered: {n: whole(n, g) for n, g in zip(groups[q], gathered)},
                slabs=lambda G, q: [slabs(G, n, bf16) for n in groups[q]],
                w_in_slabs=lambda G: [jnp.stack([_unpad_cols(G['w_in_pad'], j * shapes['w_in'][1], (j + 1) * shapes['w_in'][1])
                                                 for j in range(N_POS)])])

    loss_rows, dx, G = _local_step(x, loss_target, W, late)
    arrived = {n: a for grp, got in zip(groups, G.pop('_arrived')) for n, a in zip(grp, got)}
    (arrived_w_in,) = G.pop('_arrived_w_in')
    G.pop('w_in_pad')

    small_slabs = jnp.stack([_pack128([slabs(G, n)[j] for n in SMALL_SHARDED] + [G[n] for n in SMALL]) for j in range(N_POS)])
    (arrived_small,) = _xy_exchange("scatter_small", [small_slabs], scatter=True)
    contributions = [arrived_w_in] + [arrived[n] for n in BIG[1:]] + [arrived_small]
    tags = list(BIG) + ['small']
    plane = [_sum_slots("sum_" + t, cbuf) for t, cbuf in zip(tags, contributions)]
    sibling = _sibling_exchange("sibling_grads", plane)

    out = {}
    names4 = ('grad', 'delta', 'new_m', 'new_v')
    for q, n in enumerate(BIG):
        for tag, t in zip(names4, _adamw("adamw_" + n, P[n], plane[q], sibling[q], M[n], V[n])):
            out[tag + '_' + n] = t
    small_out = _adamw("adamw_small", _pack128([P[n] for n in packed]), plane[-1], sibling[-1],
                       _pack128([M[n] for n in packed]), _pack128([V[n] for n in packed]))
    for tag, buf in zip(names4, small_out):
        for n, t in zip(packed, _unpack128(buf, [shapes[n] for n in packed])):
            out[tag + '_' + n] = t
    loss = lax.psum(loss_rows[0, 0], ("x", "y", "c"))
    return loss, dx, out


def kernel(x, norm1_g, w_in, rwkv_mu, rwkv_w0, rwkv_w2, rwkv_a0, rwkv_a2, rwkv_g2, rwkv_k_k, rwkv_k_a, rwkv_r_k, rwkv_ln_w, rwkv_ln_b, rwkv_proj, gdn_conv_w, gdn_a_log, gdn_dt_bias, gdn_norm_w, gdn_proj, w_out, norm2_g, ffn_up, ffn_conv_w, ffn_down, final_g, loss_target, m_norm1_g, m_w_in, m_rwkv_mu, m_rwkv_w0, m_rwkv_w2, m_rwkv_a0, m_rwkv_a2, m_rwkv_g2, m_rwkv_k_k, m_rwkv_k_a, m_rwkv_r_k, m_rwkv_ln_w, m_rwkv_ln_b, m_rwkv_proj, m_gdn_conv_w, m_gdn_a_log, m_gdn_dt_bias, m_gdn_norm_w, m_gdn_proj, m_w_out, m_norm2_g, m_ffn_up, m_ffn_conv_w, m_ffn_down, m_final_g, v_norm1_g, v_w_in, v_rwkv_mu, v_rwkv_w0, v_rwkv_w2, v_rwkv_a0, v_rwkv_a2, v_rwkv_g2, v_rwkv_k_k, v_rwkv_k_a, v_rwkv_r_k, v_rwkv_ln_w, v_rwkv_ln_b, v_rwkv_proj, v_gdn_conv_w, v_gdn_a_log, v_gdn_dt_bias, v_gdn_norm_w, v_gdn_proj, v_w_out, v_norm2_g, v_ffn_up, v_ffn_conv_w, v_ffn_down, v_final_g):
    weights = (norm1_g, w_in, rwkv_mu, rwkv_w0, rwkv_w2, rwkv_a0, rwkv_a2, rwkv_g2, rwkv_k_k, rwkv_k_a, rwkv_r_k, rwkv_ln_w,
               rwkv_ln_b, rwkv_proj, gdn_conv_w, gdn_a_log, gdn_dt_bias, gdn_norm_w, gdn_proj, w_out, norm2_g, ffn_up,
               ffn_conv_w, ffn_down, final_g)
    m_in = (m_norm1_g, m_w_in, m_rwkv_mu, m_rwkv_w0, m_rwkv_w2, m_rwkv_a0, m_rwkv_a2, m_rwkv_g2, m_rwkv_k_k, m_rwkv_k_a,
            m_rwkv_r_k, m_rwkv_ln_w, m_rwkv_ln_b, m_rwkv_proj, m_gdn_conv_w, m_gdn_a_log, m_gdn_dt_bias, m_gdn_norm_w,
            m_gdn_proj, m_w_out, m_norm2_g, m_ffn_up, m_ffn_conv_w, m_ffn_down, m_final_g)
    v_in = (v_norm1_g, v_w_in, v_rwkv_mu, v_rwkv_w0, v_rwkv_w2, v_rwkv_a0, v_rwkv_a2, v_rwkv_g2, v_rwkv_k_k, v_rwkv_k_a,
            v_rwkv_r_k, v_rwkv_ln_w, v_rwkv_ln_b, v_rwkv_proj, v_gdn_conv_w, v_gdn_a_log, v_gdn_dt_bias, v_gdn_norm_w,
            v_gdn_proj, v_w_out, v_norm2_g, v_ffn_up, v_ffn_conv_w, v_ffn_down, v_final_g)
    drop = lambda n, a: a if n == 'final_g' else a[0]
    P = {n: drop(n, a) for n, a in zip(WEIGHTS, weights)}
    M = {n: drop(n, a) for n, a in zip(WEIGHTS, m_in)}
    V = {n: drop(n, a) for n, a in zip(WEIGHTS, v_in)}
    loss, dx, out = _step(x[0], loss_target[0], P, M, V)
    lift = lambda n, a: a if n == 'final_g' else a[None]
    res = [loss, dx[None]]
    for tag in ('grad', 'delta', 'new_m', 'new_v'):
        res += [lift(n, out[tag + '_' + n]) for n in WEIGHTS]
    return tuple(res)
```

```python
import functools

import jax
import jax.numpy as jnp
from jax import lax
from jax.experimental import pallas as pl
from jax.experimental.pallas import tpu as pltpu

f32 = jnp.float32
bf16 = jnp.bfloat16

D_MODEL = 1024
RWKV_HEADS, RWKV_HD, RWKV_W = 8, 64, 512
GDN_HEADS, GDN_HD, GDN_W = 4, 128, 512
NORM_EPS, L2_EPS, GN_EPS = 1e-6, 1e-6, 64e-5
W_AB = 256
OFF_QKV, OFF_Z, OFF_GATES, OFF_AB = 1792, 3328, 3840, 5888
W_IN_PAD = OFF_AB + W_AB
WKV_CHUNK, WKV_PER_STEP = 64, 4
GDN_CHUNK, GDN_PER_STEP = 128, 4
HALO = 8
LANES = 128
TILE_BYTES = 1 << 20
VMEM_LIMIT = 56 * 1024 * 1024

ADAM_LR, ADAM_B1, ADAM_B2, ADAM_EPS, ADAM_WD, ADAM_STEP = 0.001, 0.9, 0.999, 1e-08, 0.01, 10

ROW_SHARDED = ('w_out', 'ffn_down')
SMALL = ('norm1_g', 'rwkv_mu', 'rwkv_w0', 'rwkv_a0', 'rwkv_k_k', 'rwkv_k_a', 'rwkv_r_k', 'rwkv_ln_w', 'rwkv_ln_b',
         'gdn_a_log', 'gdn_dt_bias', 'gdn_norm_w', 'norm2_g', 'final_g')
WEIGHTS = ('norm1_g', 'w_in', 'rwkv_mu', 'rwkv_w0', 'rwkv_w2', 'rwkv_a0', 'rwkv_a2', 'rwkv_g2', 'rwkv_k_k', 'rwkv_k_a',
           'rwkv_r_k', 'rwkv_ln_w', 'rwkv_ln_b', 'rwkv_proj', 'gdn_conv_w', 'gdn_a_log', 'gdn_dt_bias', 'gdn_norm_w',
           'gdn_proj', 'w_out', 'norm2_g', 'ffn_up', 'ffn_conv_w', 'ffn_down', 'final_g')


def _params(*sem):
    return pltpu.CompilerParams(dimension_semantics=sem, vmem_limit_bytes=VMEM_LIMIT)


def _tile(n, limit):
    if n <= limit:
        return n
    best = None
    for d in range(128, limit + 1, 128):
        if n % d == 0:
            best = d
    if best is None:
        raise ValueError(f"no tile for {n} under {limit}")
    return best


MM_BLOCK_BYTES = 6 << 20
MM_MAX_COLS = 1536


def _mm(a, b, mode, name, add=None, out_dtype=f32, side=None, col_slabs=None):
    if mode == 'nn':
        (M, K), N = a.shape, b.shape[1]
    elif mode == 'nt':
        (M, K), N = a.shape, b.shape[0]
    else:
        (K, M), N = a.shape, b.shape[1]
    tm = _tile(M, 1408)
    tk = _tile(K, min(2816, MM_BLOCK_BYTES // (tm * a.dtype.itemsize)))
    tn = _tile(N, max(128, min(MM_BLOCK_BYTES // (tk * b.dtype.itemsize), MM_BLOCK_BYTES // (tm * 4), MM_MAX_COLS) // 128 * 128))
    if col_slabs is not None:
        tn = N // col_slabs
    nk = K // tk
    grid = (M // tm, N // tn, nk)
    dn = {'nn': (((1,), (0,)), ((), ())), 'nt': (((1,), (1,)), ((), ())), 'tn': (((0,), (0,)), ((), ()))}[mode]
    n_add = 0 if add is None else 1
    n_side = 0 if side is None else len(side[0])

    def body(a_ref, b_ref, *rest):
        add_ref = rest[0] if add is not None else None
        side_in, rest = rest[n_add:n_add + n_side], rest[n_add + n_side:]
        o_ref, side_out, rest = rest[0], rest[1:1 + n_side], rest[1 + n_side:]
        acc_ref, rest = (rest[0], rest[1:]) if nk > 1 else (None, rest)
        ids = [pl.program_id(d) for d in range(3)]
        if side is not None:
            start, finish = _xy_copies(side_in, side_out, rest, side[1])
            pl.when((ids[0] == 0) & (ids[1] == 0) & (ids[2] == 0))(start)
        acc = lax.dot_general(a_ref[...].astype(bf16), b_ref[...].astype(bf16), dn, preferred_element_type=f32)
        if nk == 1:
            o_ref[...] = (acc + add_ref[...] if add is not None else acc).astype(out_dtype)
        else:
            k = ids[2]

            @pl.when(k == 0)
            def _():
                acc_ref[...] = acc + add_ref[...] if add is not None else acc

            @pl.when(k > 0)
            def _():
                acc_ref[...] += acc

            @pl.when(k == nk - 1)
            def _():
                o_ref[...] = acc_ref[...].astype(out_dtype)
        if side is not None:
            pl.when((ids[0] == grid[0] - 1) & (ids[1] == grid[1] - 1) & (ids[2] == nk - 1))(finish)

    a_spec = (pl.BlockSpec((tk, tm), lambda i, j, k: (k, i)) if mode == 'tn'
              else pl.BlockSpec((tm, tk), lambda i, j, k: (i, k)))
    b_spec = (pl.BlockSpec((tn, tk), lambda i, j, k: (j, k)) if mode == 'nt'
              else pl.BlockSpec((tk, tn), lambda i, j, k: (k, j)))
    o_spec = pl.BlockSpec((tm, tn), lambda i, j, k: (i, j))
    o_shape = jax.ShapeDtypeStruct((M, N), out_dtype)
    if col_slabs is not None:
        o_spec = pl.BlockSpec((None, tm, tn), lambda i, j, k: (j, i, 0))
        o_shape = jax.ShapeDtypeStruct((col_slabs, M, tn), out_dtype)
    any_spec = pl.BlockSpec(memory_space=pl.ANY)
    side_bufs = [] if side is None else list(side[0])
    ins, specs = [a, b], [a_spec, b_spec]
    if add is not None:
        ins.append(add)
        specs.append(o_spec)
    outs = pl.pallas_call(
        body, grid=grid, in_specs=specs + [any_spec] * n_side, out_specs=[o_spec] + [any_spec] * n_side,
        out_shape=[o_shape] + (_xy_out_shapes(side_bufs, side[1]) if side is not None else []),
        scratch_shapes=([pltpu.VMEM((tm, tn), f32)] if nk > 1 else []) + (_xy_sems(n_side, side[1]) if side is not None else []),
        name=name,
        compiler_params=_params(*(("arbitrary",) * 3 if side is not None else ("parallel", "parallel", "arbitrary"))))(
            *ins, *side_bufs)
    return list(outs) if side is not None else outs[0]


def _shift_down(cur, prev, s):
    if s == 0:
        return cur
    ext = jnp.concatenate([prev, cur], axis=0)
    return pltpu.roll(ext, s, 0)[HALO:]


def _shift_up(cur, nxt, s):
    if s == 0:
        return cur
    ext = jnp.concatenate([cur, nxt], axis=0)
    return pltpu.roll(ext, ext.shape[0] - s, 0)[:cur.shape[0]]


def _conv_apply(cur, prev, w_ref, shifted=None):
    taps = w_ref.shape[0]
    out = None
    for i in range(taps):
        s = taps - 1 - i
        term = (shifted[s] if shifted is not None else _shift_down(cur, prev, s)) * w_ref[pl.ds(i, 1), :]
        out = term if out is None else out + term
    return out


def _row_spec(tm, w, col=0):
    return pl.BlockSpec((tm, w), lambda i: (i, col))


def _cols(a, width, col):
    return (a, width, col)


def _row_of(r):
    return r if isinstance(r, tuple) else (r, r.shape[1], 0)


def _prev_spec(tm, w):
    return pl.BlockSpec((HALO, w), lambda i: (jnp.maximum(i * (tm // HALO) - 1, 0), 0))


def _next_spec(tm, w, T):
    return pl.BlockSpec((HALO, w), lambda i: (jnp.minimum((i + 1) * (tm // HALO), T // HALO - 1), 0))


def _full_spec(shape):
    return pl.BlockSpec(shape, lambda i: (0,) * len(shape))


def _pw_fwd(name, fn, rows, consts, out_widths, tm, conv_w=None, out_dtype=f32):
    T = _row_of(rows[0])[0].shape[0]
    nr, nc = len(rows), len(consts)

    def body(*refs):
        i = pl.program_id(0)
        vals = [r[...] for r in refs[:nr]]
        p = nr
        if conv_w is not None:
            prev = jnp.where(i > 0, refs[p][...], 0.0)
            vals[0] = _conv_apply(vals[0], prev, refs[p + 1])
            p += 2
        cvals = [r[...] for r in refs[p:p + nc]]
        outs = fn(*vals, *cvals)
        for o_ref, o in zip(refs[p + nc:], outs):
            o_ref[...] = o.astype(out_dtype)

    ins = [_row_of(r)[0] for r in rows]
    specs = [_row_spec(tm, *_row_of(r)[1:]) for r in rows]
    if conv_w is not None:
        ins += [rows[0], conv_w]
        specs += [_prev_spec(tm, rows[0].shape[1]), _full_spec(conv_w.shape)]
    ins += list(consts)
    specs += [_full_spec(c.shape) for c in consts]
    outs = pl.pallas_call(
        body, grid=(T // tm,), in_specs=specs,
        out_specs=[_row_spec(tm, w) for w in out_widths],
        out_shape=[jax.ShapeDtypeStruct((T, w), out_dtype) for w in out_widths], name=name,
        compiler_params=_params("parallel"))(*ins)
    return outs


def _pw_bwd(name, fn, rows, consts, cots, tm, add_to_first=None, row_dtypes=None):
    rows = [_row_of(r) for r in rows]
    T = rows[0][0].shape[0]
    nr, nc = len(rows), len(consts)
    flat_cots = [c for grp in cots for c in grp]
    row_dtypes = row_dtypes or [f32] * nr
    n_extra = 0 if add_to_first is None else 1

    def body(*refs):
        i = pl.program_id(0)
        in_refs, cot_refs = refs[:nr + nc], refs[nr + nc:nr + nc + len(flat_cots)]
        extra_ref = refs[nr + nc + len(flat_cots)] if add_to_first is not None else None
        row_out = refs[nr + nc + len(flat_cots) + n_extra:][:nr]
        const_out = refs[nr + nc + len(flat_cots) + n_extra + nr:]

        @pl.when(i == 0)
        def _():
            for q in range(nc):
                const_out[q][...] = jnp.zeros_like(const_out[q])

        def part(sl):
            cot_vals, p = [], 0
            for grp in cots:
                acc = cot_refs[p][:, sl]
                for q in range(1, len(grp)):
                    acc = acc + cot_refs[p + q][:, sl]
                p += len(grp)
                cot_vals.append(acc)
            _, vjp = jax.vjp(fn, *[r[:, sl] for r in in_refs])
            grads = vjp(tuple(cot_vals))
            for q in range(nr):
                g = grads[q]
                if q == 0 and extra_ref is not None:
                    g = g + extra_ref[:, sl]
                row_out[q][:, sl] = g.astype(row_dtypes[q])
            for q in range(nc):
                const_out[q][:, sl] += grads[nr + q]

        part(slice(None))

    ins = [r[0] for r in rows] + list(consts) + flat_cots
    specs = ([_row_spec(tm, r[1], r[2]) for r in rows] + [_full_spec(c.shape) for c in consts]
             + [_row_spec(tm, c.shape[1]) for c in flat_cots])
    if add_to_first is not None:
        ins.append(add_to_first)
        specs.append(_row_spec(tm, add_to_first.shape[1]))
    out_shapes = ([jax.ShapeDtypeStruct((T, r[1]), d) for r, d in zip(rows, row_dtypes)]
                  + [jax.ShapeDtypeStruct(c.shape, f32) for c in consts])
    out_specs = [_row_spec(tm, r[1]) for r in rows] + [_full_spec(c.shape) for c in consts]
    outs = pl.pallas_call(
        body, grid=(T // tm,), in_specs=specs, out_specs=out_specs, out_shape=out_shapes, name=name,
        compiler_params=_params("arbitrary"))(*ins)
    return list(outs[:nr]), list(outs[nr:])


def _pw_conv_bwd(name, fn, rows, consts, cots, conv_w, tm, row_dtypes=None):
    T, W0 = rows[0].shape
    nr, nc = len(rows), len(consts)
    taps = conv_w.shape[0]
    nblk = T // tm
    flat_cots = [c for grp in cots for c in grp]
    row_dtypes = row_dtypes or [f32] * nr

    def body(*refs):
        i = pl.program_id(0)
        p = 0
        cur = [r[...] for r in refs[p:p + nr]]; p += nr
        nxt = [r[...] for r in refs[p:p + nr]]; p += nr
        prev = jnp.where(i > 0, refs[p][...], 0.0); p += 1
        w_ref = refs[p]; p += 1
        cvals = [r[...] for r in refs[p:p + nc]]; p += nc

        def summed(p0):
            out, q = [], p0
            for grp in cots:
                acc = refs[q][...]
                for t in range(1, len(grp)):
                    acc = acc + refs[q + t][...]
                q += len(grp)
                out.append(acc)
            return out, q

        cot_cur, p = summed(p)
        cot_nxt, p = summed(p)
        row_out, dw_ref, const_out = refs[p:p + nr], refs[p + nr], refs[p + nr + 1:]

        x_cur = cur[0]
        x_down = [_shift_down(x_cur, prev, s_) for s_ in range(taps)]
        _, vjp = jax.vjp(fn, _conv_apply(x_cur, prev, w_ref, x_down), *cur[1:], *cvals)
        grads = vjp(tuple(cot_cur))
        _, vjp_n = jax.vjp(fn, _conv_apply(nxt[0], x_cur[tm - HALO:], w_ref), *nxt[1:], *cvals)
        dc_n = jnp.where(i < nblk - 1, vjp_n(tuple(cot_nxt))[0], 0.0)
        dc = grads[0]

        @pl.when(i == 0)
        def _():
            dw_ref[...] = jnp.zeros_like(dw_ref)
            for q in range(nc):
                const_out[q][...] = jnp.zeros_like(const_out[q])

        dx = None
        for k in range(taps):
            s_ = taps - 1 - k
            term = _shift_up(dc, dc_n, s_) * w_ref[pl.ds(k, 1), :]
            dx = term if dx is None else dx + term
            dw_ref[pl.ds(k, 1), :] += jnp.sum(dc * x_down[s_], axis=0, keepdims=True)
        row_out[0][...] = dx.astype(row_dtypes[0])
        for q in range(1, nr):
            row_out[q][...] = grads[q].astype(row_dtypes[q])
        for q in range(nc):
            const_out[q][...] += grads[nr + q]

    ins = list(rows) + list(rows) + [rows[0], conv_w] + list(consts) + flat_cots + flat_cots
    specs = ([_row_spec(tm, r.shape[1]) for r in rows] + [_next_spec(tm, r.shape[1], T) for r in rows]
             + [_prev_spec(tm, W0), _full_spec(conv_w.shape)] + [_full_spec(c.shape) for c in consts]
             + [_row_spec(tm, c.shape[1]) for c in flat_cots] + [_next_spec(tm, c.shape[1], T) for c in flat_cots])
    out_shapes = ([jax.ShapeDtypeStruct(r.shape, d) for r, d in zip(rows, row_dtypes)]
                  + [jax.ShapeDtypeStruct(conv_w.shape, f32)] + [jax.ShapeDtypeStruct(c.shape, f32) for c in consts])
    out_specs = ([_row_spec(tm, r.shape[1]) for r in rows] + [_full_spec(conv_w.shape)]
                 + [_full_spec(c.shape) for c in consts])
    outs = pl.pallas_call(
        body, grid=(nblk,), in_specs=specs, out_specs=out_specs, out_shape=out_shapes, name=name,
        compiler_params=_params("arbitrary"))(*ins)
    return list(outs[:nr]), outs[nr], list(outs[nr + 1:])


def _sigmoid(x):
    return 0.5 * jnp.tanh(0.5 * x) + 0.5


def _softplus(x):
    return jnp.maximum(x, 0.0) + jnp.log(1.0 + jnp.exp(jnp.minimum(x, -x)))


def _seg_sum_impl(x, seg):
    w = x.shape[-1]
    r = lax.broadcasted_iota(jnp.int32, (w, w), 0) // seg
    c = lax.broadcasted_iota(jnp.int32, (w, w), 1) // seg
    ones = (r == c).astype(bf16)
    hi = x.astype(bf16)
    lo = (x - hi.astype(f32)).astype(bf16)
    return (jnp.dot(hi, ones, preferred_element_type=f32) + jnp.dot(lo, ones, preferred_element_type=f32))


@functools.partial(jax.custom_vjp, nondiff_argnums=(1,))
def _seg_sum(x, seg):
    return _seg_sum_impl(x, seg)


_seg_sum.defvjp(lambda x, seg: (_seg_sum_impl(x, seg), None), lambda seg, _, g: (_seg_sum_impl(g, seg),))


def _rms(x, g):
    return x * lax.rsqrt(jnp.mean(x * x, axis=-1, keepdims=True) + NORM_EPS) * g


def _rms_fn(x, g):
    return (_rms(x, g),)


def _loss_rows(x2, tgt, g):
    e = _rms(x2, g) - tgt
    return 0.5 * jnp.sum(e * e, axis=-1, keepdims=True) * (1.0 / D_MODEL)


@jax.custom_vjp
def _dot_lo(a, b):
    return jnp.dot(a.astype(bf16), b.astype(bf16), preferred_element_type=f32)


def _dot_lo_bwd(ab, g):
    a, b = ab
    gl = g.astype(bf16)
    return (lax.dot_general(gl, b.astype(bf16), (((1,), (1,)), ((), ())), preferred_element_type=f32),
            lax.dot_general(a.astype(bf16), gl, (((0,), (0,)), ((), ())), preferred_element_type=f32))


_dot_lo.defvjp(lambda a, b: (_dot_lo(a, b), (a, b)), _dot_lo_bwd)


def _rwkv_prep_fn(ps, w0, w2p, a0, a2p, g2, k_k, k_a):
    r, k, v = ps[:, 0:512], ps[:, 512:1024], ps[:, 1024:1536]
    wa, gl = ps[:, 1536:1664], ps[:, 1664:1792]
    z = w0 + _dot_lo(jnp.tanh(wa), w2p)
    w_log = -_softplus(-z) - 0.5
    lw = -jnp.exp(w_log)
    a = _sigmoid(a0 + _dot_lo(wa, a2p))
    g = _dot_lo(_sigmoid(gl), g2)
    kx = k * k_k
    kk = kx * lax.rsqrt(_seg_sum(kx * kx, RWKV_HD) + L2_EPS)
    k2 = k * (1.0 + (a - 1.0) * k_a)
    return r, lw, k2, v, -kk, kk * a, g


def _rwkv_post_fn(y, r, k2, v, g, ln_w, ln_b, rk):
    mean = _seg_sum(y, RWKV_HD) * (1.0 / RWKV_HD)
    yc = y - mean
    var = _seg_sum(yc * yc, RWKV_HD) * (1.0 / RWKV_HD)
    yn = yc * lax.rsqrt(var + GN_EPS) * ln_w + ln_b
    bonus = _seg_sum(r * k2 * rk, RWKV_HD) * v
    return ((yn + bonus) * g,)


def _gdn_prep_fn(cq, ck, cv):
    silu = lambda c: c * _sigmoid(c)
    q, k = silu(cq), silu(ck)
    q = q * lax.rsqrt(jnp.sum(q * q, axis=-1, keepdims=True) + L2_EPS) * (GDN_HD ** -0.5)
    k = k * lax.rsqrt(jnp.sum(k * k, axis=-1, keepdims=True) + L2_EPS)
    return q, k, silu(cv)


def _gdn_gate_fn(ab, al_p, dt_p):
    lane = lax.broadcasted_iota(jnp.int32, ab.shape, 1)
    gpart = -jnp.exp(al_p) * _softplus(ab + dt_p)
    return (jnp.where(lane < GDN_HEADS, gpart, jnp.where(lane < 2 * GDN_HEADS, _sigmoid(ab), 0.0)),)


def _gdn_post_fn(o, z, nw):
    ms = _seg_sum(o * o, GDN_HD) * (1.0 / GDN_HD)
    return (o * lax.rsqrt(ms + NORM_EPS) * nw * (z * _sigmoid(z)),)


def _mix_fn(ga, gb, ya, yb):
    return (_sigmoid(ga) * ya + _sigmoid(gb) * yb,)


STRIP = 128


def _strip_conv(ref, prev_ref, w_ref, sl, first, taps):
    cur = ref[:, sl]
    prev = jnp.where(first, 0.0, prev_ref[:, sl])
    down = [_shift_down(cur, prev, s) for s in range(taps)]
    conv = None
    for k in range(taps):
        term = down[taps - 1 - k] * w_ref[pl.ds(k, 1), sl]
        conv = term if conv is None else conv + term
    return cur, down, conv


def _group_fwd(name, fn, x, w, shared_cols, group_cols, consts, n_out, tm):
    T, W = x.shape
    taps = w.shape[0]
    n_groups = len(group_cols)
    nc = len(consts)

    def body(x_ref, xp_ref, w_ref, *refs):
        const_refs, out_refs = refs[:nc], refs[nc:]
        first = pl.program_id(0) == 0
        shared = [_strip_conv(x_ref, xp_ref, w_ref, sl, first, taps)[2] for sl in shared_cols]
        for j, cols in enumerate(group_cols):
            sl = slice(STRIP * j, STRIP * (j + 1))
            convs = [_strip_conv(x_ref, xp_ref, w_ref, c, first, taps)[2] for c in cols]
            outs = fn(*convs, *shared, *[c[:, sl] for c in const_refs])
            for o_ref, o in zip(out_refs, outs):
                o_ref[:, sl] = o

    return pl.pallas_call(
        body, grid=(T // tm,),
        in_specs=[_row_spec(tm, W), _prev_spec(tm, W), _full_spec(w.shape)] + [_full_spec(c.shape) for c in consts],
        out_specs=[_row_spec(tm, STRIP * n_groups)] * n_out,
        out_shape=[jax.ShapeDtypeStruct((T, STRIP * n_groups), f32)] * n_out, name=name,
        compiler_params=_params("parallel"))(x, x, w, *consts)


def _group_bwd(name, fn, x, w, shared_cols, group_cols, consts, cots, tm):
    T, W = x.shape
    taps = w.shape[0]
    nblk = T // tm
    nc, ns = len(consts), len(shared_cols)
    flat_cots = [c for grp in cots for c in grp]
    n_cot = len(flat_cots)

    def body(x_ref, xp_ref, xn_ref, w_ref, *refs):
        const_refs, refs = refs[:nc], refs[nc:]
        cot_refs, cotn_refs, refs = refs[:n_cot], refs[n_cot:2 * n_cot], refs[2 * n_cot:]
        dx_ref, dw_ref, const_out = refs[0], refs[1], refs[2:]
        i = pl.program_id(0)
        first, last = i == 0, i == nblk - 1

        @pl.when(first)
        def _():
            dw_ref[...] = jnp.zeros_like(dw_ref)
            for q in range(nc):
                const_out[q][...] = jnp.zeros_like(const_out[q])

        def convs_of(sl):
            cur, down, conv = _strip_conv(x_ref, xp_ref, w_ref, sl, first, taps)
            nxt, conv_n = xn_ref[:, sl], None
            for k in range(taps):
                term = _shift_down(nxt, cur[tm - HALO:], taps - 1 - k) * w_ref[pl.ds(k, 1), sl]
                conv_n = term if conv_n is None else conv_n + term
            return down, conv, conv_n

        def conv_back(sl, down, dc, dc_n):
            dx = None
            for k in range(taps):
                s_ = taps - 1 - k
                term = _shift_up(dc, dc_n, s_) * w_ref[pl.ds(k, 1), sl]
                dx = term if dx is None else dx + term
                dw_ref[pl.ds(k, 1), sl] += jnp.sum(dc * down[s_], axis=0, keepdims=True)
            dx_ref[:, sl] = dx.astype(dx_ref.dtype)

        def summed(refs_, sl, mask):
            out, p = [], 0
            for grp in cots:
                acc = refs_[p][:, sl]
                for t in range(1, len(grp)):
                    acc = acc + refs_[p + t][:, sl]
                p += len(grp)
                out.append(jnp.where(last, 0.0, acc) if mask else acc)
            return tuple(out)

        shared = [convs_of(sl) for sl in shared_cols]
        d_shared, d_shared_n = [None] * ns, [None] * ns
        for j, cols in enumerate(group_cols):
            sl = slice(STRIP * j, STRIP * (j + 1))
            mine = [convs_of(c) for c in cols]
            cj = [c[:, sl] for c in const_refs]
            _, vjp = jax.vjp(fn, *[m[1] for m in mine], *[m[1] for m in shared], *cj)
            grads = vjp(summed(cot_refs, sl, False))
            _, vjp_n = jax.vjp(fn, *[m[2] for m in mine], *[m[2] for m in shared], *cj)
            grads_n = vjp_n(summed(cotn_refs, sl, True))
            for q, c in enumerate(cols):
                conv_back(c, mine[q][0], grads[q], grads_n[q])
            for q in range(ns):
                g, gn = grads[len(cols) + q], grads_n[len(cols) + q]
                d_shared[q] = g if d_shared[q] is None else d_shared[q] + g
                d_shared_n[q] = gn if d_shared_n[q] is None else d_shared_n[q] + gn
            for q in range(nc):
                const_out[q][:, sl] += grads[len(cols) + ns + q]
        for q, c in enumerate(shared_cols):
            conv_back(c, shared[q][0], d_shared[q], d_shared_n[q])

    outs = pl.pallas_call(
        body, grid=(nblk,),
        in_specs=[_row_spec(tm, W), _prev_spec(tm, W), _next_spec(tm, W, T), _full_spec(w.shape)]
        + [_full_spec(c.shape) for c in consts] + [_row_spec(tm, c.shape[1]) for c in flat_cots]
        + [_next_spec(tm, c.shape[1], T) for c in flat_cots],
        out_specs=[_row_spec(tm, W), _full_spec(w.shape)] + [_full_spec(c.shape) for c in consts],
        out_shape=[jax.ShapeDtypeStruct((T, W), bf16), jax.ShapeDtypeStruct(w.shape, f32)]
        + [jax.ShapeDtypeStruct(c.shape, f32) for c in consts], name=name,
        compiler_params=_params("arbitrary"))(x, x, x, w, *consts, *flat_cots, *flat_cots)
    return outs[0], outs[1], list(outs[2:])


def _ffn_strip_fn(cg, cu):
    return cg * _sigmoid(cg) * cu


def _ffn_act_fwd(h, w, tm):
    T, W2 = h.shape
    H = W2 // 2
    taps = w.shape[0]

    def body(h_ref, hp_ref, w_ref, o_ref):
        first = pl.program_id(0) == 0
        for j in range(H // STRIP):
            gs, us = slice(STRIP * j, STRIP * (j + 1)), slice(H + STRIP * j, H + STRIP * (j + 1))
            cg = _strip_conv(h_ref, hp_ref, w_ref, gs, first, taps)[2]
            cu = _strip_conv(h_ref, hp_ref, w_ref, us, first, taps)[2]
            o_ref[:, gs] = _ffn_strip_fn(cg, cu).astype(o_ref.dtype)

    return pl.pallas_call(
        body, grid=(T // tm,), in_specs=[_row_spec(tm, W2), _prev_spec(tm, W2), _full_spec(w.shape)],
        out_specs=_row_spec(tm, H), out_shape=jax.ShapeDtypeStruct((T, H), bf16), name="ffn_act",
        compiler_params=_params("parallel"))(h, h, w)


def _ffn_act_bwd(h, dact, w, tm):
    T, W2 = h.shape
    H = W2 // 2
    taps = w.shape[0]
    nblk = T // tm

    def body(h_ref, hp_ref, hn_ref, d_ref, dn_ref, w_ref, dh_ref, dw_ref):
        i = pl.program_id(0)
        first, last = i == 0, i == nblk - 1

        @pl.when(first)
        def _():
            dw_ref[...] = jnp.zeros_like(dw_ref)

        for j in range(H // STRIP):
            gs, us = slice(STRIP * j, STRIP * (j + 1)), slice(H + STRIP * j, H + STRIP * (j + 1))
            parts = {}
            for name, sl in (('g', gs), ('u', us)):
                cur, down, conv = _strip_conv(h_ref, hp_ref, w_ref, sl, first, taps)
                nxt = hn_ref[:, sl]
                conv_n = None
                for k in range(taps):
                    term = _shift_down(nxt, cur[tm - HALO:], taps - 1 - k) * w_ref[pl.ds(k, 1), sl]
                    conv_n = term if conv_n is None else conv_n + term
                parts[name] = (down, conv, conv_n)
            _, vjp = jax.vjp(_ffn_strip_fn, parts['g'][1], parts['u'][1])
            dcs = vjp(d_ref[:, gs])
            _, vjp_n = jax.vjp(_ffn_strip_fn, parts['g'][2], parts['u'][2])
            dcs_n = vjp_n(jnp.where(last, 0.0, dn_ref[:, gs]))
            for (name, sl), dc, dc_n in zip((('g', gs), ('u', us)), dcs, dcs_n):
                down = parts[name][0]
                dx = None
                for k in range(taps):
                    s_ = taps - 1 - k
                    term = _shift_up(dc, dc_n, s_) * w_ref[pl.ds(k, 1), sl]
                    dx = term if dx is None else dx + term
                    dw_ref[pl.ds(k, 1), sl] += jnp.sum(dc * down[s_], axis=0, keepdims=True)
                dh_ref[:, sl] = dx.astype(dh_ref.dtype)

    return pl.pallas_call(
        body, grid=(nblk,),
        in_specs=[_row_spec(tm, W2), _prev_spec(tm, W2), _next_spec(tm, W2, T), _row_spec(tm, H), _next_spec(tm, H, T),
                  _full_spec(w.shape)],
        out_specs=[_row_spec(tm, W2), _full_spec(w.shape)],
        out_shape=[jax.ShapeDtypeStruct((T, W2), bf16), jax.ShapeDtypeStruct(w.shape, f32)], name="ffn_act_bwd",
        compiler_params=_params("arbitrary"))(h, h, h, dact, dact, w)


N_POS = 4


def _xy_out_shapes(bufs, scatter):
    return [jax.ShapeDtypeStruct((N_POS,) + tuple(b.shape[1:] if scatter else b.shape), b.dtype) for b in bufs]


def _xy_sems(n, scatter):
    sems = [pltpu.SemaphoreType.DMA((3 * n,)), pltpu.SemaphoreType.DMA((3 * n,)), pltpu.SemaphoreType.DMA((n,))]
    return sems if scatter else sems + [pltpu.SemaphoreType.DMA((3 * n,)), pltpu.SemaphoreType.DMA((3 * n,))]


def _xy_copies(in_refs, out_refs, sems, scatter):
    n = len(in_refs)
    send_sems, recv_sems, local_sems = sems[:3]

    def place():
        x, y, c = lax.axis_index("x"), lax.axis_index("y"), lax.axis_index("c")
        return x, y, c, 2 * x + y, [(1 - x, y), (x, 1 - y), (1 - x, 1 - y)]

    def half(ref, a, which):
        rows = in_refs[a].shape[0] // 2
        return ref.at[pl.ds(pl.multiple_of(which * rows, HALO), rows)]

    def ici(a, k, src, dst, peer, c):
        return pltpu.make_async_remote_copy(
            src_ref=src, dst_ref=dst, send_sem=send_sems.at[3 * a + k], recv_sem=recv_sems.at[3 * a + k],
            device_id=(peer[0], peer[1], c), device_id_type=pl.DeviceIdType.MESH)

    def outgoing():
        x, y, c, me, peers = place()
        own = [pltpu.make_async_copy(in_refs[a].at[me] if scatter else in_refs[a], out_refs[a].at[me], local_sems.at[a])
               for a in range(n)]
        if scatter:
            sends = [ici(a, k, in_refs[a].at[2 * p[0] + p[1]], out_refs[a].at[me], p, c)
                     for a in range(n) for k, p in enumerate(peers)]
        else:
            sends = [ici(a, k, half(in_refs[a], a, c), half(out_refs[a].at[me], a, c), p, c)
                     for a in range(n) for k, p in enumerate(peers)]
        return own, sends

    def arrivals():
        x, y, c, me, peers = place()
        if scatter:
            return [ici(a, k, in_refs[a].at[me], out_refs[a].at[2 * p[0] + p[1]], p, c)
                    for a in range(n) for k, p in enumerate(peers)]
        return [ici(a, k, half(in_refs[a], a, c), half(out_refs[a].at[2 * p[0] + p[1]], a, c), p, c)
                for a in range(n) for k, p in enumerate(peers)]

    def to_sibling(mine):
        x, y, c, me, peers = place()
        which = c if mine else 1 - c
        return [pltpu.make_async_remote_copy(
            src_ref=half(out_refs[a].at[2 * p[0] + p[1]], a, which), dst_ref=half(out_refs[a].at[2 * p[0] + p[1]], a, which),
            send_sem=sems[3].at[3 * a + k], recv_sem=sems[4].at[3 * a + k],
            device_id=(x, y, 1 - c), device_id_type=pl.DeviceIdType.MESH) for a in range(n) for k, p in enumerate(peers)]

    def start():
        own, sends = outgoing()
        for cp in own + sends:
            cp.start()

    def finish():
        if scatter:
            for cp in arrivals():
                cp.wait_recv()
        else:
            passed = to_sibling(True)
            for cp, fwd in zip(arrivals(), passed):
                cp.wait_recv()
                fwd.start()
            for cp in to_sibling(False):
                cp.wait_recv()
            for fwd in passed:
                fwd.wait_send()
        own, sends = outgoing()
        for cp in sends:
            cp.wait_send()
        for cp in own:
            cp.wait()

    return start, finish


_NN, _NT, _TN = 'hcs,hsd->hcd', 'hcd,hsd->hcs', 'hcd,hce->hde'


def _lo(spec, a, b):
    return jnp.einsum(spec, a.astype(bf16), b.astype(bf16), preferred_element_type=f32)


@jax.custom_vjp
def _bmm(a, b):
    return _lo(_NN, a, b)


_bmm.defvjp(lambda a, b: (_lo(_NN, a, b), (a, b)), lambda ab, g: (_lo(_NT, g, ab[1]), _lo(_TN, ab[0], g)))


@jax.custom_vjp
def _bmm_nt(a, b):
    return _lo(_NT, a, b)


_bmm_nt.defvjp(lambda a, b: (_lo(_NT, a, b), (a, b)), lambda ab, g: (_lo(_NN, g, ab[1]), _lo(_TN, g, ab[0])))


@jax.custom_vjp
def _bmm_tn(a, b):
    return _lo(_TN, a, b)


_bmm_tn.defvjp(lambda a, b: (_lo(_TN, a, b), (a, b)), lambda ab, g: (_lo(_NT, ab[1], g), _lo(_NN, ab[0], g)))


def _masks(H, C):
    row = lax.broadcasted_iota(jnp.int32, (H, C, C), 1)
    col = lax.broadcasted_iota(jnp.int32, (H, C, C), 2)
    return row, col


def _tri_inv_impl(L):
    H, C, _ = L.shape
    row, col = _masks(H, C)
    eye = (row == col).astype(f32)
    base = 16
    same = (row // base) == (col // base)
    Ld = jnp.where(same, L, 0.0)
    X = -Ld
    inv = eye + X
    for _ in range(3):
        X = _bmm(X, X)
        inv = _bmm(inv, eye + X)
    if C == base:
        return inv
    N = _bmm(inv, L - Ld)
    out = eye - N
    levels = C // base
    P = N
    span = 2
    while span < levels:
        P = _bmm(P, P)
        out = _bmm(out, eye + P)
        span *= 2
    return _bmm(out, inv)


@jax.custom_vjp
def _tri_inv(L):
    return _tri_inv_impl(L)


def _tri_inv_fwd(L):
    T = _tri_inv_impl(L)
    return T, T


def _tri_inv_bwd(T, dT):
    return (-_bmm_nt(_bmm_tn(T, dT), T),)


_tri_inv.defvjp(_tri_inv_fwd, _tri_inv_bwd)


@jax.custom_vjp
def _tri_inv_known(L, T):
    return T


_tri_inv_known.defvjp(lambda L, T: (T, T), lambda T, dT: (_tri_inv_bwd(T, dT)[0], jnp.zeros_like(T)))


def _cumsum_impl(x, reverse):
    C = x.shape[1]
    row = lax.broadcasted_iota(jnp.int32, x.shape, 1)
    s = 1
    while s < C:
        if reverse:
            x = x + jnp.where(row < C - s, pltpu.roll(x, C - s, 1), 0.0)
        else:
            x = x + jnp.where(row >= s, pltpu.roll(x, s, 1), 0.0)
        s *= 2
    return x


@jax.custom_vjp
def _cumsum(x):
    return _cumsum_impl(x, False)


_cumsum.defvjp(lambda x: (_cumsum_impl(x, False), None), lambda _, g: (_cumsum_impl(g, True),))


def _wkv_prep(r, lw, k, v, a, b, inv=None):
    lane = lax.broadcasted_iota(jnp.int32, (r.shape[0], 128), 1)
    low = lane < RWKV_HD

    def heads(t):
        out = []
        for p in range(RWKV_HEADS // 2):
            pair = t[:, 128 * p:128 * (p + 1)]
            out += [jnp.where(low, pair, 0.0), jnp.where(low, 0.0, pair)]
        return jnp.concatenate([t[None] for t in out], axis=0)

    r, lw, k, v, a, b = [heads(t) for t in (r, lw, k, v, a, b)]
    H, C, D = r.shape
    row, col = _masks(H, C)
    incl, strict = row >= col, row > col
    cw = _cumsum(lw)
    cwp = cw - lw
    cwl = jnp.sum(lw, axis=1, keepdims=True)
    en = jnp.exp(-cw)
    at, rt, bt, kt = a * jnp.exp(cwp), r * jnp.exp(cw), b * en, k * en
    Lab = -jnp.where(strict, _bmm_nt(at, bt), 0.0)
    Tm = _tri_inv(Lab) if inv is None else _tri_inv_known(Lab, inv)
    ar = jnp.concatenate([at, rt], axis=1)
    gram = _bmm_nt(ar, jnp.concatenate([bt, kt], axis=1))
    row2 = lax.broadcasted_iota(jnp.int32, (H, 2 * C, 2 * C), 1)
    col2 = lax.broadcasted_iota(jnp.int32, (H, 2 * C, 2 * C), 2) % C
    gram = jnp.where(((row2 < C) & (row2 > col2)) | ((row2 >= C) & (row2 - C >= col2)), gram, 0.0)
    a_bk, r_bk = gram[:, :C], gram[:, C:]
    lak_v = _bmm(a_bk, jnp.concatenate([jnp.zeros_like(v), v], axis=1))
    ed = jnp.exp(cwl - cw)
    zdec = jnp.swapaxes(jnp.broadcast_to(jnp.exp(cwl), (H, D, D)), 1, 2)
    return (ar, Tm, lak_v, r_bk, jnp.concatenate([b * ed, k * ed], axis=1), zdec, v), Tm


def _wkv_step(Z, ar, Tm, lak_v, r_bk, bk_d, zdec, v):
    C = Tm.shape[1]
    ar_z = _bmm(ar, Z)
    uv = jnp.concatenate([_bmm(Tm, ar_z[:, :C] + lak_v), v], axis=1)
    y = ar_z[:, C:] + _bmm(r_bk, uv)
    Z1 = Z * zdec + _bmm_tn(bk_d, uv)
    return jnp.concatenate([y[2 * p] + y[2 * p + 1] for p in range(RWKV_HEADS // 2)], axis=1), Z1


def _split3(x):
    hi = x.astype(bf16)
    mid = (x - hi.astype(f32)).astype(bf16)
    lo = (x - hi.astype(f32) - mid.astype(f32)).astype(bf16)
    return hi, mid, lo


@jax.custom_vjp
def _spread(x, sel):
    return sum(jnp.dot(t, sel, preferred_element_type=f32) for t in _split3(x))


def _spread_bwd(sel, g):
    dn = (((1,), (1,)), ((), ()))
    return sum(lax.dot_general(t, sel, dn, preferred_element_type=f32) for t in _split3(g)), None


_spread.defvjp(lambda x, sel: (_spread(x, sel), sel), _spread_bwd)


def _gdn_prep(q, k, v, gbeta, inv=None):
    heads = lambda t: jnp.concatenate([t[None, :, GDN_HD * h:GDN_HD * (h + 1)] for h in range(GDN_HEADS)], axis=0)
    src = lax.broadcasted_iota(jnp.int32, (W_AB, 2 * GDN_W), 0)
    dst = lax.broadcasted_iota(jnp.int32, (W_AB, 2 * GDN_W), 1) // GDN_HD
    spread = _spread(gbeta, (src == dst).astype(bf16))
    q, k, v, g, beta = heads(q), heads(k), heads(v), heads(spread[:, :GDN_W]), heads(spread[:, GDN_W:])
    H, C, D = q.shape
    row, col = _masks(H, C)
    incl, strict = row >= col, row > col
    gc = _cumsum(g)
    diff = gc - jnp.swapaxes(gc, 1, 2)
    decay = jnp.where(incl, jnp.exp(jnp.where(incl, diff, 0.0)), 0.0)
    gl = jnp.sum(g, axis=1, keepdims=True)
    kb, vb = k * beta, v * beta
    gram = _bmm_nt(jnp.concatenate([kb, q], axis=1), k)
    L = jnp.where(strict, gram[:, :C] * decay, 0.0)
    attn = jnp.where(incl, gram[:, C:] * decay, 0.0)
    egc = jnp.exp(gc)
    Tm = _tri_inv(L) if inv is None else _tri_inv_known(L, inv)
    t_vk = _bmm(Tm, jnp.concatenate([vb, kb * egc], axis=2))
    return (t_vk[:, :, :D], jnp.concatenate([t_vk[:, :, D:], q * egc], axis=1), attn, k * jnp.exp(gl - gc), jnp.exp(gl)), Tm


def _gdn_step(S, u, wq, attn, ke, sdec):
    C = u.shape[1]
    wq_s = _bmm(wq, S)
    v_new = u - wq_s[:, :C]
    o = wq_s[:, C:] + _bmm(attn, v_new)
    S1 = S * sdec + _bmm_tn(ke, v_new)
    return jnp.concatenate([o[h] for h in range(GDN_HEADS)], axis=1), S1


def _scan_fwd(name, fns, ins, C, H, dh, w_out, per_step, side=None):
    prep, step = fns
    T = ins[0].shape[0]
    n_in = len(ins)
    blk = C * per_step
    nblk = T // blk
    n_side = 0 if side is None else len(side[0])

    def body(*refs):
        in_refs, refs = refs[:n_in], refs[n_in:]
        side_in, refs = refs[:n_side], refs[n_side:]
        y_ref, zs_ref, inv_ref, refs = refs[0], refs[1], refs[2], refs[3:]
        side_out, refs = refs[:n_side], refs[n_side:]
        z_scr = refs[0]
        if side is not None:
            start, finish = _xy_copies(side_in, side_out, refs[1:], side[1])
            pl.when(pl.program_id(0) == 0)(start)

        @pl.when(pl.program_id(0) == 0)
        def _():
            z_scr[...] = jnp.zeros_like(z_scr)

        rows = [slice(C * j, C * (j + 1)) for j in range(per_step)]
        prepped = [prep(*[r[rw, :] for r in in_refs]) for rw in rows]
        Z = z_scr[...]
        for j, rw in enumerate(rows):
            zs_ref[j] = Z
            inv_ref[j] = prepped[j][1]
            y, Z = step(Z, *prepped[j][0])
            y_ref[rw, :] = y
        z_scr[...] = Z
        if side is not None:
            pl.when(pl.program_id(0) == nblk - 1)(finish)

    side_bufs = [] if side is None else list(side[0])
    any_spec = pl.BlockSpec(memory_space=pl.ANY)
    return pl.pallas_call(
        body, grid=(nblk,),
        in_specs=[pl.BlockSpec((blk, a.shape[1]), lambda i: (i, 0)) for a in ins] + [any_spec] * n_side,
        out_specs=[pl.BlockSpec((blk, w_out), lambda i: (i, 0)), pl.BlockSpec((per_step, H, dh, dh), lambda i: (i, 0, 0, 0)),
                   pl.BlockSpec((per_step, H, C, C), lambda i: (i, 0, 0, 0))] + [any_spec] * n_side,
        out_shape=[jax.ShapeDtypeStruct((T, w_out), f32), jax.ShapeDtypeStruct((T // C, H, dh, dh), f32),
                   jax.ShapeDtypeStruct((T // C, H, C, C), f32)]
        + (_xy_out_shapes(side_bufs, side[1]) if side is not None else []),
        scratch_shapes=[pltpu.VMEM((H, dh, dh), f32)] + (_xy_sems(n_side, side[1]) if side is not None else []), name=name,
        compiler_params=_params("arbitrary"))(*ins, *side_bufs)


def _scan_bwd(name, fns, ins, dy, zs, invs, C, per_step, side=None):
    prep, step = fns
    T = ins[0].shape[0]
    _, H, dh, _ = zs.shape
    n_in = len(ins)
    blk = C * per_step
    nblk = T // blk
    n_side = 0 if side is None else len(side[0])

    def body(*refs):
        in_refs, dy_ref, zs_ref, inv_ref, refs = refs[:n_in], refs[n_in], refs[n_in + 1], refs[n_in + 2], refs[n_in + 3:]
        side_in, refs = refs[:n_side], refs[n_side:]
        out_refs, refs = refs[:n_in], refs[n_in:]
        side_out, refs = refs[:n_side], refs[n_side:]
        dz_scr = refs[0]
        if side is not None:
            start, finish = _xy_copies(side_in, side_out, refs[1:], side[1])
            pl.when(pl.program_id(0) == 0)(start)

        @pl.when(pl.program_id(0) == 0)
        def _():
            dz_scr[...] = jnp.zeros_like(dz_scr)

        rows = [slice(C * j, C * (j + 1)) for j in range(per_step)]
        prepped = [jax.vjp(lambda *a, j=j: prep(*a, inv=inv_ref[j])[0], *[r[rw, :] for r in in_refs])
                   for j, rw in enumerate(rows)]
        d_prepped = [None] * per_step
        dZ = dz_scr[...]
        for j in reversed(range(per_step)):
            _, pull = jax.vjp(step, zs_ref[j], *prepped[j][0])
            dZ, *d_prepped[j] = pull((dy_ref[rows[j], :], dZ))
        dz_scr[...] = dZ
        for j, rw in enumerate(rows):
            for o_ref, gval in zip(out_refs, prepped[j][1](tuple(d_prepped[j]))):
                o_ref[rw, :] = gval
        if side is not None:
            pl.when(pl.program_id(0) == nblk - 1)(finish)

    side_bufs = [] if side is None else list(side[0])
    any_spec = pl.BlockSpec(memory_space=pl.ANY)
    rev = lambda i: (nblk - 1 - i, 0)
    return pl.pallas_call(
        body, grid=(nblk,),
        in_specs=[pl.BlockSpec((blk, a.shape[1]), rev) for a in ins]
        + [pl.BlockSpec((blk, dy.shape[1]), rev), pl.BlockSpec((per_step, H, dh, dh), lambda i: (nblk - 1 - i, 0, 0, 0)),
           pl.BlockSpec((per_step, H, C, C), lambda i: (nblk - 1 - i, 0, 0, 0))] + [any_spec] * n_side,
        out_specs=[pl.BlockSpec((blk, a.shape[1]), rev) for a in ins] + [any_spec] * n_side,
        out_shape=[jax.ShapeDtypeStruct(a.shape, f32) for a in ins]
        + (_xy_out_shapes(side_bufs, side[1]) if side is not None else []),
        scratch_shapes=[pltpu.VMEM((H, dh, dh), f32)] + (_xy_sems(n_side, side[1]) if side is not None else []), name=name,
        compiler_params=_params("arbitrary"))(*ins, dy, zs, invs, *side_bufs)


def _residual_mm(name, a, b, res, tail, row_extras, consts, row_out, acc_out, tm, head=None):
    K, N = b.shape
    h_rows, h_consts = ([], []) if head is None else (list(head[1]), list(head[2]))
    lhs = h_rows + h_consts if head is not None else [a]
    M = lhs[0].shape[0]
    row_extras = [_row_of(e) for e in row_extras]
    n_lhs, n_res = len(lhs), 0 if res is None else 1
    ne, nc, nr = len(row_extras), len(consts), len(row_out)

    def body(*refs):
        lhs_refs, refs = refs[:n_lhs], refs[n_lhs:]
        b_ref, refs = refs[0], refs[1:]
        res_ref, refs = (refs[0], refs[1:]) if res is not None else (None, refs)
        extra_refs, const_refs, out_refs = refs[:ne], refs[ne:ne + nc], refs[ne + nc:]
        if head is not None:
            left = head[0](*[r[...] for r in lhs_refs])[0].astype(bf16)
            out_refs[0][...] = left
            out_refs = out_refs[1:]
        else:
            left = lhs_refs[0][...].astype(bf16)
        tile = jnp.dot(left, b_ref[...].astype(bf16), preferred_element_type=f32)
        if res is not None:
            tile = res_ref[...] + tile
        outs = tail(tile, *[r[...] for r in extra_refs], *[c[...] for c in const_refs])
        for o_ref, o in zip(out_refs[:nr], outs[:nr]):
            o_ref[...] = o.astype(o_ref.dtype)

        @pl.when(pl.program_id(0) == 0)
        def _():
            for o_ref in out_refs[nr:]:
                o_ref[...] = jnp.zeros_like(o_ref)

        for o_ref, o in zip(out_refs[nr:], outs[nr:]):
            o_ref[...] += o

    lhs_specs = ([_row_spec(tm, r.shape[1]) for r in h_rows] + [_full_spec(c.shape) for c in h_consts]
                 if head is not None else [_row_spec(tm, K)])
    head_out = [(K, bf16)] if head is not None else []
    outs = pl.pallas_call(
        body, grid=(M // tm,),
        in_specs=lhs_specs + [_full_spec(b.shape)] + ([_row_spec(tm, N)] if res is not None else [])
        + [_row_spec(tm, e[1], e[2]) for e in row_extras] + [_full_spec(c.shape) for c in consts],
        out_specs=[_row_spec(tm, w) for w, _ in head_out + list(row_out)] + [_full_spec(sh) for sh in acc_out],
        out_shape=[jax.ShapeDtypeStruct((M, w), d) for w, d in head_out + list(row_out)]
        + [jax.ShapeDtypeStruct(sh, f32) for sh in acc_out],
        name=name, compiler_params=_params("arbitrary"))(
            *lhs, b, *([res] if res is not None else []), *[e[0] for e in row_extras], *consts)
    return outs


def _pull_tail(fn):
    def tail(cot, *args):
        _, vjp = jax.vjp(fn, *args)
        return vjp((cot,))
    return tail


def _norm_tail(x1, g):
    return x1, _rms(x1, g)


def _loss_tail(x2, tgt, g):
    l, vjp = jax.vjp(lambda xv, gv: _loss_rows(xv, tgt, gv), x2, g)
    dx, dg = vjp(jnp.ones_like(l))
    return dx, dg, jnp.zeros((1, 128), f32) + jnp.sum(l)


def _local_step(x, tgt, W, late=None):
    row = lambda a: a.reshape(1, -1)
    wp = W['w_in_pad']
    w_rwkv, w_qkv, w_z = wp[:, :OFF_QKV], wp[:, OFF_QKV:OFF_Z], wp[:, OFF_Z:OFF_GATES]
    w_gates, w_ab = wp[:, OFF_GATES:OFF_AB], wp[:, OFF_AB:]
    mu = row(W['rwkv_mu'])
    mixw = jnp.concatenate([mu, 1.0 - mu], axis=0)
    zpad = jnp.zeros((64, RWKV_W), f32)
    w2p = jnp.concatenate([W['rwkv_w2'], zpad], axis=0)
    a2p = jnp.concatenate([zpad, W['rwkv_a2']], axis=0)
    rw_consts = [row(W['rwkv_w0']), w2p, row(W['rwkv_a0']), a2p, W['rwkv_g2'], row(W['rwkv_k_k']), row(W['rwkv_k_a'])]
    post_consts = [row(W['rwkv_ln_w']), row(W['rwkv_ln_b']), row(W['rwkv_r_k'])]
    pad4 = lambda a: jnp.pad(row(a), ((0, 0), (0, W_AB - GDN_HEADS)))
    gd_consts = [pad4(W['gdn_a_log']), pad4(W['gdn_dt_bias'])]
    nw_t = jnp.tile(row(W['gdn_norm_w']), (1, GDN_HEADS))
    g1, g2n, gf = row(W['norm1_g']), row(W['norm2_g']), row(W['final_g'])

    (u,) = _pw_fwd("norm1", _rms_fn, [x], [g1], [D_MODEL], 512, out_dtype=bf16)
    p_rwkv = _mm(u, w_rwkv, 'nn', "in_rwkv")
    qkv_raw = _mm(u, w_qkv, 'nn', "in_qkv")
    z = _mm(u, w_z, 'nn', "in_z")
    gates = _mm(u, w_gates, 'nn', "in_gates")
    ab = _mm(u, w_ab, 'nn', "in_ab")

    r, lw, k2, v, a_, b_, g = _pw_fwd("rwkv_prep", _rwkv_prep_fn, [p_rwkv], rw_consts, [RWKV_W] * 7, 256, conv_w=mixw)
    wkv_in = [r, lw, k2, v, a_, b_]
    y, zs_wkv, inv_wkv, *gathered = _scan_fwd("wkv_fwd", (_wkv_prep, _wkv_step), wkv_in, WKV_CHUNK, RWKV_HEADS, 2 * RWKV_HD, RWKV_W, WKV_PER_STEP,
                                     side=None if late is None else (late['shards'][0], False))
    if late is not None:
        W = dict(W, **late['assemble'](0, gathered))
    ya_in, ya = _residual_mm("rwkv_proj", None, W['rwkv_proj'], None, lambda t: (t,), [], [], [(D_MODEL, f32)], [], 512,
                             head=(_rwkv_post_fn, [y, r, k2, v, g], post_consts))

    lanes = lambda off: slice(off, off + STRIP)
    gd_groups = [[lanes(GDN_HD * h), lanes(GDN_W + GDN_HD * h), lanes(2 * GDN_W + GDN_HD * h)] for h in range(GDN_HEADS)]
    gq, gk, gv = _group_fwd("gdn_prep", _gdn_prep_fn, qkv_raw, W['gdn_conv_w'], [], gd_groups, [], 3, 256)
    (gbeta,) = _pw_fwd("gdn_gate", _gdn_gate_fn, [ab], gd_consts, [W_AB], 512)
    gdn_in = [gq, gk, gv, gbeta]
    o, zs_gdn, inv_gdn, *gathered = _scan_fwd("gdn_fwd", (_gdn_prep, _gdn_step), gdn_in, GDN_CHUNK, GDN_HEADS, GDN_HD, GDN_W, GDN_PER_STEP,
                                     side=None if late is None else (late['shards'][1], False))
    if late is not None:
        W = dict(W, **late['assemble'](1, gathered))
    ga, gb = _cols(gates, D_MODEL, 0), _cols(gates, D_MODEL, 1)
    yb_in, yb, mixed = _residual_mm("gdn_proj", None, W['gdn_proj'], None, lambda t, a_, b_, c_: (t,) + _mix_fn(a_, b_, c_, t),
                                    [ga, gb, ya], [], [(D_MODEL, f32), (D_MODEL, bf16)], [], 512,
                                    head=(_gdn_post_fn, [o, z], [nw_t]))

    x1, u2 = _residual_mm("w_out", mixed, W['w_out'], x, _norm_tail, [], [g2n], [(D_MODEL, f32), (D_MODEL, bf16)], [], 512)
    h = _mm(u2, W['ffn_up'], 'nn', "ffn_up")
    act = _ffn_act_fwd(h, W['ffn_conv_w'], 256)

    G = {}
    slab_out = None if late is None else N_POS
    dx2, dgf, loss = _residual_mm("ffn_down", act, W['ffn_down'], x1, _loss_tail, [tgt], [gf], [(D_MODEL, f32)],
                                  [gf.shape, (1, 128)], 512)
    G['final_g'] = dgf
    dact = _mm(dx2, W['ffn_down'], 'nt', "d_act")
    G['ffn_down'] = _mm(act, dx2, 'tn', "g_ffn_down", out_dtype=bf16)
    dh, G['ffn_conv_w'] = _ffn_act_bwd(h, dact, W['ffn_conv_w'], 128)
    du2 = _mm(dh, W['ffn_up'], 'nt', "d_u2")
    G['ffn_up'] = _mm(u2, dh, 'tn', "g_ffn_up", out_dtype=bf16, col_slabs=slab_out)
    (dx1,), (G['norm2_g'],) = _pw_bwd("norm2_bwd", _rms_fn, [x1], [g2n], [(du2,)], 512, add_to_first=dx2)
    G['w_out'] = _mm(mixed, dx1, 'tn', "g_w_out", out_dtype=bf16)
    dga, dgb, dya, dyb = _residual_mm("d_mixed", dx1, W['w_out'].T, None, _pull_tail(_mix_fn), [ga, gb, ya, yb], [],
                                      [(D_MODEL, bf16)] * 4, [], 512)
    G['rwkv_proj'] = _mm(ya_in, dya, 'tn', "g_rwkv_proj", out_dtype=bf16, col_slabs=slab_out)
    G['gdn_proj'] = _mm(yb_in, dyb, 'tn', "g_gdn_proj", out_dtype=bf16, col_slabs=slab_out)

    do, dz, dnw_t = _residual_mm("d_yb_in", dyb, W['gdn_proj'].T, None, _pull_tail(_gdn_post_fn), [o, z], [nw_t],
                                 [(GDN_W, f32), (GDN_W, bf16)], [nw_t.shape], 512)
    G['gdn_norm_w'] = dnw_t.reshape(GDN_HEADS, GDN_HD).sum(axis=0)
    dgq, dgk, dgv, dgbeta, *arrived_b = _scan_bwd("gdn_bwd", (_gdn_prep, _gdn_step), gdn_in, do, zs_gdn, inv_gdn, GDN_CHUNK,
                                                  GDN_PER_STEP, side=None if late is None else (late['slabs'](G, 1), True))
    dqkv_raw, G['gdn_conv_w'], _ = _group_bwd("gdn_prep_bwd", _gdn_prep_fn, qkv_raw, W['gdn_conv_w'], [], gd_groups, [],
                                              [(dgq,), (dgk,), (dgv,)], 128)
    (dab,), (dal_p, ddt_p) = _pw_bwd("gdn_gate_bwd", _gdn_gate_fn, [ab], gd_consts, [(dgbeta,)], 512, row_dtypes=[bf16])
    G['gdn_a_log'], G['gdn_dt_bias'] = dal_p[0, :GDN_HEADS], ddt_p[0, :GDN_HEADS]

    dy, dr1, dk21, dv1, dg_, G['rwkv_ln_w'], G['rwkv_ln_b'], G['rwkv_r_k'] = _residual_mm(
        "d_ya_in", dya, W['rwkv_proj'].T, None, _pull_tail(_rwkv_post_fn), [y, r, k2, v, g], post_consts,
        [(RWKV_W, f32)] * 5, [c.shape for c in post_consts], 512)
    dr2, dlw, dk22, dv2, da_, db_, *arrived_a = _scan_bwd(
        "wkv_bwd", (_wkv_prep, _wkv_step), wkv_in, dy, zs_wkv, inv_wkv, WKV_CHUNK, WKV_PER_STEP,
        side=None if late is None else (late['slabs'](G, 0), True))
    G['_arrived'] = (arrived_a, arrived_b)
    (dp_rwkv,), dmixw, rw_grads = _pw_conv_bwd(
        "rwkv_prep_bwd", _rwkv_prep_fn, [p_rwkv], rw_consts,
        [(dr1, dr2), (dlw,), (dk21, dk22), (dv1, dv2), (da_,), (db_,), (dg_,)], mixw, 256, row_dtypes=[bf16])
    G['rwkv_w0'], dw2p, G['rwkv_a0'], da2p, G['rwkv_g2'], G['rwkv_k_k'], G['rwkv_k_a'] = rw_grads
    G['rwkv_w2'], G['rwkv_a2'] = dw2p[:64], da2p[64:]
    G['rwkv_mu'] = dmixw[0] - dmixw[1]

    dp = jnp.concatenate([dp_rwkv, dqkv_raw, dz, dga, dgb, dab], axis=1)
    G['w_in_pad'] = _mm(u, dp, 'tn', "g_w_in", out_dtype=bf16)
    if late is None:
        du = _mm(dp, wp, 'nt', "d_u")
    else:
        du, *G['_arrived_w_in'] = _mm(dp, wp, 'nt', "d_u", side=(late['w_in_slabs'](G), True))
    (dx,), (G['norm1_g'],) = _pw_bwd("norm1_bwd", _rms_fn, [x], [g1], [(du,)], 512, add_to_first=dx1)
    return loss, dx, G


IN_WIDTH = OFF_AB + 8
PAD_ORDER = ((0, OFF_GATES), (OFF_GATES + 8, IN_WIDTH), (OFF_GATES, OFF_GATES + 8))


def _pad_w_in_shards(shards):
    width = shards[0].shape[1]
    parts = []
    for a, b in PAD_ORDER:
        for j, sh in enumerate(shards):
            lo, hi = max(a, j * width), min(b, (j + 1) * width)
            if lo < hi:
                parts.append(sh[:, lo - j * width:hi - j * width])
    return jnp.concatenate(parts + [jnp.zeros((shards[0].shape[0], W_AB - 8), shards[0].dtype)], axis=1)


def _unpad_cols(wp, lo, hi):
    parts, off = [], 0
    for a, b in PAD_ORDER:
        l, h = max(a, lo), min(b, hi)
        if l < h:
            parts.append((l, wp[:, off + l - a:off + h - a]))
        off += b - a
    parts.sort(key=lambda t: t[0])
    return parts[0][1] if len(parts) == 1 else jnp.concatenate([p for _, p in parts], axis=1)


BIG = ('w_in', 'rwkv_proj', 'gdn_proj', 'w_out', 'ffn_up', 'ffn_down')
SMALL_SHARDED = ('rwkv_w2', 'rwkv_a2', 'rwkv_g2', 'gdn_conv_w', 'ffn_conv_w')


def _rows128(shape):
    n = 1
    for d in shape:
        n *= d
    return -(-n // LANES)


def _pack128(arrays):
    parts = []
    for a in arrays:
        flat = a.reshape(-1)
        rows = _rows128(a.shape)
        parts.append(jnp.pad(flat, (0, rows * LANES - flat.shape[0])).reshape(rows, LANES))
    buf = jnp.concatenate(parts, axis=0)
    return jnp.pad(buf, ((0, -buf.shape[0] % HALO), (0, 0)))


def _unpack128(buf, shapes):
    out, off = [], 0
    for s in shapes:
        rows, n = _rows128(s), 1
        for d in s:
            n *= d
        out.append(buf[off:off + rows].reshape(-1)[:n].reshape(s))
        off += rows
    return out


def _param_tile(r, c):
    best = None
    for d in range(2 * HALO, r + 1, 2 * HALO):
        if r % d == 0 and d * c * 4 <= TILE_BYTES:
            best = d
    if best is not None or r * c * 4 <= TILE_BYTES:
        return (best if best is not None else r), c
    return r, 128


def _xy_exchange(name, bufs, scatter):
    n = len(bufs)

    def body(*refs):
        start, finish = _xy_copies(refs[:n], refs[n:2 * n], refs[2 * n:], scatter)
        start()
        finish()

    return pl.pallas_call(
        body, in_specs=[pl.BlockSpec(memory_space=pl.ANY)] * n, out_specs=[pl.BlockSpec(memory_space=pl.ANY)] * n,
        out_shape=_xy_out_shapes(bufs, scatter), scratch_shapes=_xy_sems(n, scatter), name=name)(*bufs)


def _sibling_exchange(name, bufs):
    n = len(bufs)

    def body(*refs):
        in_refs, out_refs, send_sems, recv_sems = refs[:n], refs[n:2 * n], refs[2 * n], refs[2 * n + 1]
        x, y, c = lax.axis_index("x"), lax.axis_index("y"), lax.axis_index("c")
        copies = [pltpu.make_async_remote_copy(
            src_ref=in_refs[a], dst_ref=out_refs[a], send_sem=send_sems.at[a], recv_sem=recv_sems.at[a],
            device_id=(x, y, 1 - c), device_id_type=pl.DeviceIdType.MESH) for a in range(n)]
        for cp in copies:
            cp.start()
        for cp in copies:
            cp.wait()

    return pl.pallas_call(
        body, in_specs=[pl.BlockSpec(memory_space=pl.ANY)] * n, out_specs=[pl.BlockSpec(memory_space=pl.ANY)] * n,
        out_shape=[jax.ShapeDtypeStruct(b.shape, b.dtype) for b in bufs],
        scratch_shapes=[pltpu.SemaphoreType.DMA((n,)), pltpu.SemaphoreType.DMA((n,))], name=name)(*bufs)


def _sum_slots(name, buf):
    _, R, L = buf.shape
    tr, tc = _param_tile(R, L)

    def body(b_ref, o_ref):
        part = lambda s: b_ref[s].astype(f32)
        o_ref[...] = ((part(0) + part(1)) + part(2)) + part(3)

    return pl.pallas_call(
        body, grid=(R // tr, L // tc),
        in_specs=[pl.BlockSpec((N_POS, tr, tc), lambda i, j: (0, i, j))],
        out_specs=pl.BlockSpec((tr, tc), lambda i, j: (i, j)),
        out_shape=jax.ShapeDtypeStruct((R, L), f32), name=name,
        compiler_params=_params("parallel", "parallel"))(buf)


def _adamw(name, w, ga, gb, m, v):
    R, L = w.shape
    tr, tc = _param_tile(R, L)
    c1 = 1.0 / (1.0 - ADAM_B1 ** ADAM_STEP)
    c2 = 1.0 / (1.0 - ADAM_B2 ** ADAM_STEP)

    def body(w_ref, ga_ref, gb_ref, m_ref, v_ref, g_out, d_out, m_out, v_out):
        g = ga_ref[...] + gb_ref[...]
        m_new = ADAM_B1 * m_ref[...] + (1.0 - ADAM_B1) * g
        v_new = ADAM_B2 * v_ref[...] + (1.0 - ADAM_B2) * (g * g)
        g_out[...] = g
        m_out[...] = m_new
        v_out[...] = v_new
        d_out[...] = -ADAM_LR * ((m_new * c1) / (jnp.sqrt(v_new * c2) + ADAM_EPS) + ADAM_WD * w_ref[...])

    spec = pl.BlockSpec((tr, tc), lambda i, j: (i, j))
    return pl.pallas_call(
        body, grid=(R // tr, L // tc), in_specs=[spec] * 5, out_specs=[spec] * 4,
        out_shape=[jax.ShapeDtypeStruct((R, L), f32)] * 4, name=name,
        compiler_params=_params("parallel", "parallel"))(w, ga, gb, m, v)


def _step(x, loss_target, P, M, V):
    shapes = {n: tuple(P[n].shape) for n in WEIGHTS}
    sh_shapes = [shapes[n] for n in SMALL_SHARDED]
    packed = SMALL_SHARDED + SMALL

    def whole(n, g):
        return g.reshape(-1, g.shape[2]) if n in ROW_SHARDED else jnp.concatenate([g[j] for j in range(N_POS)], axis=1)

    def slabs(G, n, dtype=f32):
        r, c = shapes[n]
        full = G[n].astype(dtype)
        if full.ndim == 3:
            return full
        return full.reshape(N_POS, r, c) if n in ROW_SHARDED else full.reshape(r, N_POS, c).transpose(1, 0, 2)

    g_w_in, g_small = _xy_exchange("gather_w_in", [P['w_in'].astype(bf16), _pack128([P[n] for n in SMALL_SHARDED])],
                                   scatter=False)
    W = {n: P[n] for n in SMALL}
    W['w_in_pad'] = _pad_w_in_shards([g_w_in[j] for j in range(N_POS)])
    per_pos = [_unpack128(g_small[j], sh_shapes) for j in range(N_POS)]
    for q, n in enumerate(SMALL_SHARDED):
        W[n] = jnp.concatenate([per_pos[j][q] for j in range(N_POS)], axis=1)
    groups = (('rwkv_proj', 'gdn_proj', 'ffn_up'), ('w_out', 'ffn_down'))
    late = dict(shards=[[P[n].astype(bf16) for n in grp] for grp in groups],
                assemble=lambda q, gathered: {n: whole(n, g) for n, g in zip(groups[q], gathered)},
                slabs=lambda G, q: [slabs(G, n, bf16) for n in groups[q]],
                w_in_slabs=lambda G: [jnp.stack([_unpad_cols(G['w_in_pad'], j * shapes['w_in'][1], (j + 1) * shapes['w_in'][1])
                                                 for j in range(N_POS)])])

    loss_rows, dx, G = _local_step(x, loss_target, W, late)
    arrived = {n: a for grp, got in zip(groups, G.pop('_arrived')) for n, a in zip(grp, got)}
    (arrived_w_in,) = G.pop('_arrived_w_in')
    G.pop('w_in_pad')

    small_slabs = jnp.stack([_pack128([slabs(G, n)[j] for n in SMALL_SHARDED] + [G[n] for n in SMALL]) for j in range(N_POS)])
    (arrived_small,) = _xy_exchange("scatter_small", [small_slabs], scatter=True)
    contributions = [arrived_w_in] + [arrived[n] for n in BIG[1:]] + [arrived_small]
    tags = list(BIG) + ['small']
    plane = [_sum_slots("sum_" + t, cbuf) for t, cbuf in zip(tags, contributions)]
    sibling = _sibling_exchange("sibling_grads", plane)

    out = {}
    names4 = ('grad', 'delta', 'new_m', 'new_v')
    for q, n in enumerate(BIG):
        tr = (lambda t: t.T) if n == 'w_in' else (lambda t: t)
        for tag, t in zip(names4, _adamw("adamw_" + n, tr(P[n]), tr(plane[q]), tr(sibling[q]), tr(M[n]), tr(V[n]))):
            out[tag + '_' + n] = tr(t)
    small_out = _adamw("adamw_small", _pack128([P[n] for n in packed]), plane[-1], sibling[-1],
                       _pack128([M[n] for n in packed]), _pack128([V[n] for n in packed]))
    for tag, buf in zip(names4, small_out):
        for n, t in zip(packed, _unpack128(buf, [shapes[n] for n in packed])):
            out[tag + '_' + n] = t
    loss = lax.psum(loss_rows[0, 0], ("x", "y", "c"))
    return loss, dx, out


def kernel(x, norm1_g, w_in, rwkv_mu, rwkv_w0, rwkv_w2, rwkv_a0, rwkv_a2, rwkv_g2, rwkv_k_k, rwkv_k_a, rwkv_r_k, rwkv_ln_w, rwkv_ln_b, rwkv_proj, gdn_conv_w, gdn_a_log, gdn_dt_bias, gdn_norm_w, gdn_proj, w_out, norm2_g, ffn_up, ffn_conv_w, ffn_down, final_g, loss_target, m_norm1_g, m_w_in, m_rwkv_mu, m_rwkv_w0, m_rwkv_w2, m_rwkv_a0, m_rwkv_a2, m_rwkv_g2, m_rwkv_k_k, m_rwkv_k_a, m_rwkv_r_k, m_rwkv_ln_w, m_rwkv_ln_b, m_rwkv_proj, m_gdn_conv_w, m_gdn_a_log, m_gdn_dt_bias, m_gdn_norm_w, m_gdn_proj, m_w_out, m_norm2_g, m_ffn_up, m_ffn_conv_w, m_ffn_down, m_final_g, v_norm1_g, v_w_in, v_rwkv_mu, v_rwkv_w0, v_rwkv_w2, v_rwkv_a0, v_rwkv_a2, v_rwkv_g2, v_rwkv_k_k, v_rwkv_k_a, v_rwkv_r_k, v_rwkv_ln_w, v_rwkv_ln_b, v_rwkv_proj, v_gdn_conv_w, v_gdn_a_log, v_gdn_dt_bias, v_gdn_norm_w, v_gdn_proj, v_w_out, v_norm2_g, v_ffn_up, v_ffn_conv_w, v_ffn_down, v_final_g):
    weights = (norm1_g, w_in, rwkv_mu, rwkv_w0, rwkv_w2, rwkv_a0, rwkv_a2, rwkv_g2, rwkv_k_k, rwkv_k_a, rwkv_r_k, rwkv_ln_w,
               rwkv_ln_b, rwkv_proj, gdn_conv_w, gdn_a_log, gdn_dt_bias, gdn_norm_w, gdn_proj, w_out, norm2_g, ffn_up,
               ffn_conv_w, ffn_down, final_g)
    m_in = (m_norm1_g, m_w_in, m_rwkv_mu, m_rwkv_w0, m_rwkv_w2, m_rwkv_a0, m_rwkv_a2, m_rwkv_g2, m_rwkv_k_k, m_rwkv_k_a,
            m_rwkv_r_k, m_rwkv_ln_w, m_rwkv_ln_b, m_rwkv_proj, m_gdn_conv_w, m_gdn_a_log, m_gdn_dt_bias, m_gdn_norm_w,
            m_gdn_proj, m_w_out, m_norm2_g, m_ffn_up, m_ffn_conv_w, m_ffn_down, m_final_g)
    v_in = (v_norm1_g, v_w_in, v_rwkv_mu, v_rwkv_w0, v_rwkv_w2, v_rwkv_a0, v_rwkv_a2, v_rwkv_g2, v_rwkv_k_k, v_rwkv_k_a,
            v_rwkv_r_k, v_rwkv_ln_w, v_rwkv_ln_b, v_rwkv_proj, v_gdn_conv_w, v_gdn_a_log, v_gdn_dt_bias, v_gdn_norm_w,
            v_gdn_proj, v_w_out, v_norm2_g, v_ffn_up, v_ffn_conv_w, v_ffn_down, v_final_g)
    drop = lambda n, a: a if n == 'final_g' else a[0]
    P = {n: drop(n, a) for n, a in zip(WEIGHTS, weights)}
    M = {n: drop(n, a) for n, a in zip(WEIGHTS, m_in)}
    V = {n: drop(n, a) for n, a in zip(WEIGHTS, v_in)}
    loss, dx, out = _step(x[0], loss_target[0], P, M, V)
    lift = lambda n, a: a if n == 'final_g' else a[None]
    res = [loss, dx[None]]
    for tag in ('grad', 'delta', 'new_m', 'new_v'):
        res += [lift(n, out[tag + '_' + n]) for n in WEIGHTS]
    return tuple(res)
```

```python
import functools

import jax
import jax.numpy as jnp
from jax import lax
from jax.experimental import pallas as pl
from jax.experimental.pallas import tpu as pltpu

f32 = jnp.float32
bf16 = jnp.bfloat16

D_MODEL = 1024
RWKV_HEADS, RWKV_HD, RWKV_W = 8, 64, 512
GDN_HEADS, GDN_HD, GDN_W = 4, 128, 512
NORM_EPS, L2_EPS, GN_EPS = 1e-6, 1e-6, 64e-5
W_AB = 256
OFF_QKV, OFF_Z, OFF_GATES, OFF_AB = 1792, 3328, 3840, 5888
W_IN_PAD = OFF_AB + W_AB
WKV_CHUNK, WKV_PER_STEP = 64, 4
GDN_CHUNK, GDN_PER_STEP = 128, 4
HALO = 8
LANES = 128
TILE_BYTES = 1 << 20
VMEM_LIMIT = 56 * 1024 * 1024

ADAM_LR, ADAM_B1, ADAM_B2, ADAM_EPS, ADAM_WD, ADAM_STEP = 0.001, 0.9, 0.999, 1e-08, 0.01, 10

ROW_SHARDED = ('w_out', 'ffn_down')
SMALL = ('norm1_g', 'rwkv_mu', 'rwkv_w0', 'rwkv_a0', 'rwkv_k_k', 'rwkv_k_a', 'rwkv_r_k', 'rwkv_ln_w', 'rwkv_ln_b',
         'gdn_a_log', 'gdn_dt_bias', 'gdn_norm_w', 'norm2_g', 'final_g')
WEIGHTS = ('norm1_g', 'w_in', 'rwkv_mu', 'rwkv_w0', 'rwkv_w2', 'rwkv_a0', 'rwkv_a2', 'rwkv_g2', 'rwkv_k_k', 'rwkv_k_a',
           'rwkv_r_k', 'rwkv_ln_w', 'rwkv_ln_b', 'rwkv_proj', 'gdn_conv_w', 'gdn_a_log', 'gdn_dt_bias', 'gdn_norm_w',
           'gdn_proj', 'w_out', 'norm2_g', 'ffn_up', 'ffn_conv_w', 'ffn_down', 'final_g')


def _params(*sem):
    return pltpu.CompilerParams(dimension_semantics=sem, vmem_limit_bytes=VMEM_LIMIT)


def _tile(n, limit):
    if n <= limit:
        return n
    best = None
    for d in range(128, limit + 1, 128):
        if n % d == 0:
            best = d
    if best is None:
        raise ValueError(f"no tile for {n} under {limit}")
    return best


MM_BLOCK_BYTES = 6 << 20
MM_MAX_COLS = 1536


def _mm(a, b, mode, name, add=None, out_dtype=f32, side=None, col_slabs=None):
    if mode == 'nn':
        (M, K), N = a.shape, b.shape[1]
    elif mode == 'nt':
        (M, K), N = a.shape, b.shape[0]
    else:
        (K, M), N = a.shape, b.shape[1]
    tm = _tile(M, 1408)
    tk = _tile(K, min(2816, MM_BLOCK_BYTES // (tm * a.dtype.itemsize)))
    tn = _tile(N, max(128, min(MM_BLOCK_BYTES // (tk * b.dtype.itemsize), MM_BLOCK_BYTES // (tm * 4), MM_MAX_COLS) // 128 * 128))
    if col_slabs is not None:
        tn = N // col_slabs
    nk = K // tk
    grid = (M // tm, N // tn, nk)
    dn = {'nn': (((1,), (0,)), ((), ())), 'nt': (((1,), (1,)), ((), ())), 'tn': (((0,), (0,)), ((), ()))}[mode]
    n_add = 0 if add is None else 1
    n_side = 0 if side is None else len(side[0])

    def body(a_ref, b_ref, *rest):
        add_ref = rest[0] if add is not None else None
        side_in, rest = rest[n_add:n_add + n_side], rest[n_add + n_side:]
        o_ref, side_out, rest = rest[0], rest[1:1 + n_side], rest[1 + n_side:]
        acc_ref, rest = (rest[0], rest[1:]) if nk > 1 else (None, rest)
        ids = [pl.program_id(d) for d in range(3)]
        if side is not None:
            start, finish = _xy_copies(side_in, side_out, rest, side[1])
            pl.when((ids[0] == 0) & (ids[1] == 0) & (ids[2] == 0))(start)
        acc = lax.dot_general(a_ref[...].astype(bf16), b_ref[...].astype(bf16), dn, preferred_element_type=f32)
        if nk == 1:
            o_ref[...] = (acc + add_ref[...] if add is not None else acc).astype(out_dtype)
        else:
            k = ids[2]

            @pl.when(k == 0)
            def _():
                acc_ref[...] = acc + add_ref[...] if add is not None else acc

            @pl.when(k > 0)
            def _():
                acc_ref[...] += acc

            @pl.when(k == nk - 1)
            def _():
                o_ref[...] = acc_ref[...].astype(out_dtype)
        if side is not None:
            pl.when((ids[0] == grid[0] - 1) & (ids[1] == grid[1] - 1) & (ids[2] == nk - 1))(finish)

    a_spec = (pl.BlockSpec((tk, tm), lambda i, j, k: (k, i)) if mode == 'tn'
              else pl.BlockSpec((tm, tk), lambda i, j, k: (i, k)))
    b_spec = (pl.BlockSpec((tn, tk), lambda i, j, k: (j, k)) if mode == 'nt'
              else pl.BlockSpec((tk, tn), lambda i, j, k: (k, j)))
    o_spec = pl.BlockSpec((tm, tn), lambda i, j, k: (i, j))
    o_shape = jax.ShapeDtypeStruct((M, N), out_dtype)
    if col_slabs is not None:
        o_spec = pl.BlockSpec((None, tm, tn), lambda i, j, k: (j, i, 0))
        o_shape = jax.ShapeDtypeStruct((col_slabs, M, tn), out_dtype)
    any_spec = pl.BlockSpec(memory_space=pl.ANY)
    side_bufs = [] if side is None else list(side[0])
    ins, specs = [a, b], [a_spec, b_spec]
    if add is not None:
        ins.append(add)
        specs.append(o_spec)
    outs = pl.pallas_call(
        body, grid=grid, in_specs=specs + [any_spec] * n_side, out_specs=[o_spec] + [any_spec] * n_side,
        out_shape=[o_shape] + (_xy_out_shapes(side_bufs, side[1]) if side is not None else []),
        scratch_shapes=([pltpu.VMEM((tm, tn), f32)] if nk > 1 else []) + (_xy_sems(n_side, side[1]) if side is not None else []),
        name=name,
        compiler_params=_params(*(("arbitrary",) * 3 if side is not None else ("parallel", "parallel", "arbitrary"))))(
            *ins, *side_bufs)
    return list(outs) if side is not None else outs[0]


PARTS_TILE = 512


def _mm_nt_parts(pairs, name, side=None):
    n, M, N = len(pairs), pairs[0][0].shape[0], pairs[0][1].shape[0]
    tm = min(M, PARTS_TILE)
    steps = M // tm
    n_side = 0 if side is None else len(side[0])

    def body(*refs):
        a_refs, b_refs, side_in = refs[:n], refs[n:2 * n], refs[2 * n:2 * n + n_side]
        o_ref, side_out, sems = refs[2 * n + n_side], refs[2 * n + n_side + 1:2 * n + 2 * n_side + 1], refs[2 * n + 2 * n_side + 1:]
        i = pl.program_id(0)
        if side is not None:
            start, finish = _xy_copies(side_in, side_out, sems, side[1])
            pl.when(i == 0)(start)
        acc = None
        for a_ref, b_ref in zip(a_refs, b_refs):
            part = lax.dot_general(a_ref[...].astype(bf16), b_ref[...].astype(bf16), (((1,), (1,)), ((), ())),
                                   preferred_element_type=f32)
            acc = part if acc is None else acc + part
        o_ref[...] = acc
        if side is not None:
            pl.when(i == steps - 1)(finish)

    any_spec = pl.BlockSpec(memory_space=pl.ANY)
    side_bufs = [] if side is None else list(side[0])
    o_spec = pl.BlockSpec((tm, N), lambda i: (i, 0))
    outs = pl.pallas_call(
        body, grid=(steps,),
        in_specs=[pl.BlockSpec((tm, a.shape[1]), lambda i: (i, 0)) for a, _, _ in pairs]
        + [pl.BlockSpec((N, a.shape[1]), lambda i, col=col: (0, col)) for a, _, col in pairs] + [any_spec] * n_side,
        out_specs=[o_spec] + [any_spec] * n_side,
        out_shape=[jax.ShapeDtypeStruct((M, N), f32)] + (_xy_out_shapes(side_bufs, side[1]) if side is not None else []),
        scratch_shapes=_xy_sems(n_side, side[1]) if side is not None else [],
        name=name, compiler_params=_params("arbitrary" if side is not None else "parallel"))(
            *[a for a, _, _ in pairs], *[b for _, b, _ in pairs], *side_bufs)
    return list(outs) if side is not None else outs[0]


def _mm_tn_parts(a, parts, name):
    n, (K, M) = len(parts), a.shape
    tm, tk = min(M, PARTS_TILE), min(K, PARTS_TILE)
    nk = K // tk
    widths = [p.shape[1] for p in parts]
    offs = [sum(widths[:q]) for q in range(n)]

    def body(a_ref, *refs):
        p_refs, o_refs, acc_ref = refs[:n], refs[n:2 * n], refs[2 * n]
        k = pl.program_id(1)
        lhs = a_ref[...].astype(bf16)
        for p_ref, o_ref, off, w in zip(p_refs, o_refs, offs, widths):
            part = lax.dot_general(lhs, p_ref[...].astype(bf16), (((0,), (0,)), ((), ())), preferred_element_type=f32)
            if nk == 1:
                o_ref[...] = part.astype(bf16)
                continue
            cols = (slice(None), slice(off, off + w))

            @pl.when(k == 0)
            def _():
                acc_ref[cols] = part

            @pl.when(k > 0)
            def _():
                acc_ref[cols] += part

            @pl.when(k == nk - 1)
            def _():
                o_ref[...] = acc_ref[cols].astype(bf16)

    return pl.pallas_call(
        body, grid=(M // tm, nk),
        in_specs=[pl.BlockSpec((tk, tm), lambda i, k: (k, i))] + [pl.BlockSpec((tk, w), lambda i, k: (k, 0)) for w in widths],
        out_specs=[pl.BlockSpec((tm, w), lambda i, k: (i, 0)) for w in widths],
        out_shape=[jax.ShapeDtypeStruct((M, w), bf16) for w in widths],
        scratch_shapes=[pltpu.VMEM((tm, sum(widths)), f32)],
        name=name, compiler_params=_params("parallel", "arbitrary"))(a, *parts)


def _shift_down(cur, prev, s):
    if s == 0:
        return cur
    ext = jnp.concatenate([prev, cur], axis=0)
    return pltpu.roll(ext, s, 0)[HALO:]


def _shift_up(cur, nxt, s):
    if s == 0:
        return cur
    ext = jnp.concatenate([cur, nxt], axis=0)
    return pltpu.roll(ext, ext.shape[0] - s, 0)[:cur.shape[0]]


def _conv_apply(cur, prev, w_ref, shifted=None):
    taps = w_ref.shape[0]
    out = None
    for i in range(taps):
        s = taps - 1 - i
        term = (shifted[s] if shifted is not None else _shift_down(cur, prev, s)) * w_ref[pl.ds(i, 1), :]
        out = term if out is None else out + term
    return out


def _row_spec(tm, w, col=0):
    return pl.BlockSpec((tm, w), lambda i: (i, col))


def _cols(a, width, col):
    return (a, width, col)


def _row_of(r):
    return r if isinstance(r, tuple) else (r, r.shape[1], 0)


def _prev_spec(tm, w):
    return pl.BlockSpec((HALO, w), lambda i: (jnp.maximum(i * (tm // HALO) - 1, 0), 0))


def _next_spec(tm, w, T):
    return pl.BlockSpec((HALO, w), lambda i: (jnp.minimum((i + 1) * (tm // HALO), T // HALO - 1), 0))


def _full_spec(shape):
    return pl.BlockSpec(shape, lambda i: (0,) * len(shape))


def _pw_fwd(name, fn, rows, consts, out_widths, tm, conv_w=None, out_dtype=f32):
    T = _row_of(rows[0])[0].shape[0]
    nr, nc = len(rows), len(consts)

    def body(*refs):
        i = pl.program_id(0)
        vals = [r[...] for r in refs[:nr]]
        p = nr
        if conv_w is not None:
            prev = jnp.where(i > 0, refs[p][...], 0.0)
            vals[0] = _conv_apply(vals[0], prev, refs[p + 1])
            p += 2
        cvals = [r[...] for r in refs[p:p + nc]]
        outs = fn(*vals, *cvals)
        for o_ref, o in zip(refs[p + nc:], outs):
            o_ref[...] = o.astype(out_dtype)

    ins = [_row_of(r)[0] for r in rows]
    specs = [_row_spec(tm, *_row_of(r)[1:]) for r in rows]
    if conv_w is not None:
        ins += [rows[0], conv_w]
        specs += [_prev_spec(tm, rows[0].shape[1]), _full_spec(conv_w.shape)]
    ins += list(consts)
    specs += [_full_spec(c.shape) for c in consts]
    outs = pl.pallas_call(
        body, grid=(T // tm,), in_specs=specs,
        out_specs=[_row_spec(tm, w) for w in out_widths],
        out_shape=[jax.ShapeDtypeStruct((T, w), out_dtype) for w in out_widths], name=name,
        compiler_params=_params("parallel"))(*ins)
    return outs


def _pw_bwd(name, fn, rows, consts, cots, tm, add_to_first=None, row_dtypes=None):
    rows = [_row_of(r) for r in rows]
    T = rows[0][0].shape[0]
    nr, nc = len(rows), len(consts)
    flat_cots = [c for grp in cots for c in grp]
    row_dtypes = row_dtypes or [f32] * nr
    n_extra = 0 if add_to_first is None else 1

    def body(*refs):
        i = pl.program_id(0)
        in_refs, cot_refs = refs[:nr + nc], refs[nr + nc:nr + nc + len(flat_cots)]
        extra_ref = refs[nr + nc + len(flat_cots)] if add_to_first is not None else None
        row_out = refs[nr + nc + len(flat_cots) + n_extra:][:nr]
        const_out = refs[nr + nc + len(flat_cots) + n_extra + nr:]

        @pl.when(i == 0)
        def _():
            for q in range(nc):
                const_out[q][...] = jnp.zeros_like(const_out[q])

        def part(sl):
            cot_vals, p = [], 0
            for grp in cots:
                acc = cot_refs[p][:, sl]
                for q in range(1, len(grp)):
                    acc = acc + cot_refs[p + q][:, sl]
                p += len(grp)
                cot_vals.append(acc)
            _, vjp = jax.vjp(fn, *[r[:, sl] for r in in_refs])
            grads = vjp(tuple(cot_vals))
            for q in range(nr):
                g = grads[q]
                if q == 0 and extra_ref is not None:
                    g = g + extra_ref[:, sl]
                row_out[q][:, sl] = g.astype(row_dtypes[q])
            for q in range(nc):
                const_out[q][:, sl] += grads[nr + q]

        part(slice(None))

    ins = [r[0] for r in rows] + list(consts) + flat_cots
    specs = ([_row_spec(tm, r[1], r[2]) for r in rows] + [_full_spec(c.shape) for c in consts]
             + [_row_spec(tm, c.shape[1]) for c in flat_cots])
    if add_to_first is not None:
        ins.append(add_to_first)
        specs.append(_row_spec(tm, add_to_first.shape[1]))
    out_shapes = ([jax.ShapeDtypeStruct((T, r[1]), d) for r, d in zip(rows, row_dtypes)]
                  + [jax.ShapeDtypeStruct(c.shape, f32) for c in consts])
    out_specs = [_row_spec(tm, r[1]) for r in rows] + [_full_spec(c.shape) for c in consts]
    outs = pl.pallas_call(
        body, grid=(T // tm,), in_specs=specs, out_specs=out_specs, out_shape=out_shapes, name=name,
        compiler_params=_params("arbitrary"))(*ins)
    return list(outs[:nr]), list(outs[nr:])


def _pw_conv_bwd(name, fn, rows, consts, cots, conv_w, tm, row_dtypes=None):
    T, W0 = rows[0].shape
    nr, nc = len(rows), len(consts)
    taps = conv_w.shape[0]
    nblk = T // tm
    flat_cots = [c for grp in cots for c in grp]
    row_dtypes = row_dtypes or [f32] * nr

    def body(*refs):
        i = pl.program_id(0)
        p = 0
        cur = [r[...] for r in refs[p:p + nr]]; p += nr
        nxt = [r[...] for r in refs[p:p + nr]]; p += nr
        prev = jnp.where(i > 0, refs[p][...], 0.0); p += 1
        w_ref = refs[p]; p += 1
        cvals = [r[...] for r in refs[p:p + nc]]; p += nc

        def summed(p0):
            out, q = [], p0
            for grp in cots:
                acc = refs[q][...]
                for t in range(1, len(grp)):
                    acc = acc + refs[q + t][...]
                q += len(grp)
                out.append(acc)
            return out, q

        cot_cur, p = summed(p)
        cot_nxt, p = summed(p)
        row_out, dw_ref, const_out = refs[p:p + nr], refs[p + nr], refs[p + nr + 1:]

        x_cur = cur[0]
        x_down = [_shift_down(x_cur, prev, s_) for s_ in range(taps)]
        _, vjp = jax.vjp(fn, _conv_apply(x_cur, prev, w_ref, x_down), *cur[1:], *cvals)
        grads = vjp(tuple(cot_cur))
        _, vjp_n = jax.vjp(fn, _conv_apply(nxt[0], x_cur[tm - HALO:], w_ref), *nxt[1:], *cvals)
        dc_n = jnp.where(i < nblk - 1, vjp_n(tuple(cot_nxt))[0], 0.0)
        dc = grads[0]

        @pl.when(i == 0)
        def _():
            dw_ref[...] = jnp.zeros_like(dw_ref)
            for q in range(nc):
                const_out[q][...] = jnp.zeros_like(const_out[q])

        dx = None
        for k in range(taps):
            s_ = taps - 1 - k
            term = _shift_up(dc, dc_n, s_) * w_ref[pl.ds(k, 1), :]
            dx = term if dx is None else dx + term
            dw_ref[pl.ds(k, 1), :] += jnp.sum(dc * x_down[s_], axis=0, keepdims=True)
        row_out[0][...] = dx.astype(row_dtypes[0])
        for q in range(1, nr):
            row_out[q][...] = grads[q].astype(row_dtypes[q])
        for q in range(nc):
            const_out[q][...] += grads[nr + q]

    ins = list(rows) + list(rows) + [rows[0], conv_w] + list(consts) + flat_cots + flat_cots
    specs = ([_row_spec(tm, r.shape[1]) for r in rows] + [_next_spec(tm, r.shape[1], T) for r in rows]
             + [_prev_spec(tm, W0), _full_spec(conv_w.shape)] + [_full_spec(c.shape) for c in consts]
             + [_row_spec(tm, c.shape[1]) for c in flat_cots] + [_next_spec(tm, c.shape[1], T) for c in flat_cots])
    out_shapes = ([jax.ShapeDtypeStruct(r.shape, d) for r, d in zip(rows, row_dtypes)]
                  + [jax.ShapeDtypeStruct(conv_w.shape, f32)] + [jax.ShapeDtypeStruct(c.shape, f32) for c in consts])
    out_specs = ([_row_spec(tm, r.shape[1]) for r in rows] + [_full_spec(conv_w.shape)]
                 + [_full_spec(c.shape) for c in consts])
    outs = pl.pallas_call(
        body, grid=(nblk,), in_specs=specs, out_specs=out_specs, out_shape=out_shapes, name=name,
        compiler_params=_params("arbitrary"))(*ins)
    return list(outs[:nr]), outs[nr], list(outs[nr + 1:])


def _sigmoid(x):
    return 0.5 * jnp.tanh(0.5 * x) + 0.5


def _softplus(x):
    return jnp.maximum(x, 0.0) + jnp.log(1.0 + jnp.exp(jnp.minimum(x, -x)))


def _seg_sum_impl(x, seg):
    w = x.shape[-1]
    r = lax.broadcasted_iota(jnp.int32, (w, w), 0) // seg
    c = lax.broadcasted_iota(jnp.int32, (w, w), 1) // seg
    ones = (r == c).astype(bf16)
    hi = x.astype(bf16)
    lo = (x - hi.astype(f32)).astype(bf16)
    return (jnp.dot(hi, ones, preferred_element_type=f32) + jnp.dot(lo, ones, preferred_element_type=f32))


@functools.partial(jax.custom_vjp, nondiff_argnums=(1,))
def _seg_sum(x, seg):
    return _seg_sum_impl(x, seg)


_seg_sum.defvjp(lambda x, seg: (_seg_sum_impl(x, seg), None), lambda seg, _, g: (_seg_sum_impl(g, seg),))


def _rms(x, g):
    return x * lax.rsqrt(jnp.mean(x * x, axis=-1, keepdims=True) + NORM_EPS) * g


def _rms_fn(x, g):
    return (_rms(x, g),)


def _loss_rows(x2, tgt, g):
    e = _rms(x2, g) - tgt
    return 0.5 * jnp.sum(e * e, axis=-1, keepdims=True) * (1.0 / D_MODEL)


@jax.custom_vjp
def _dot_lo(a, b):
    return jnp.dot(a.astype(bf16), b.astype(bf16), preferred_element_type=f32)


def _dot_lo_bwd(ab, g):
    a, b = ab
    gl = g.astype(bf16)
    return (lax.dot_general(gl, b.astype(bf16), (((1,), (1,)), ((), ())), preferred_element_type=f32),
            lax.dot_general(a.astype(bf16), gl, (((0,), (0,)), ((), ())), preferred_element_type=f32))


_dot_lo.defvjp(lambda a, b: (_dot_lo(a, b), (a, b)), _dot_lo_bwd)


def _rwkv_prep_fn(ps, w0, w2p, a0, a2p, g2, k_k, k_a):
    r, k, v = ps[:, 0:512], ps[:, 512:1024], ps[:, 1024:1536]
    wa, gl = ps[:, 1536:1664], ps[:, 1664:1792]
    z = w0 + _dot_lo(jnp.tanh(wa), w2p)
    w_log = -_softplus(-z) - 0.5
    lw = -jnp.exp(w_log)
    a = _sigmoid(a0 + _dot_lo(wa, a2p))
    g = _dot_lo(_sigmoid(gl), g2)
    kx = k * k_k
    kk = kx * lax.rsqrt(_seg_sum(kx * kx, RWKV_HD) + L2_EPS)
    k2 = k * (1.0 + (a - 1.0) * k_a)
    return r, lw, k2, v, -kk, kk * a, g


def _rwkv_post_fn(y, r, k2, v, g, ln_w, ln_b, rk):
    mean = _seg_sum(y, RWKV_HD) * (1.0 / RWKV_HD)
    yc = y - mean
    var = _seg_sum(yc * yc, RWKV_HD) * (1.0 / RWKV_HD)
    yn = yc * lax.rsqrt(var + GN_EPS) * ln_w + ln_b
    bonus = _seg_sum(r * k2 * rk, RWKV_HD) * v
    return ((yn + bonus) * g,)


def _gdn_prep_fn(cq, ck, cv):
    silu = lambda c: c * _sigmoid(c)
    q, k = silu(cq), silu(ck)
    q = q * lax.rsqrt(jnp.sum(q * q, axis=-1, keepdims=True) + L2_EPS) * (GDN_HD ** -0.5)
    k = k * lax.rsqrt(jnp.sum(k * k, axis=-1, keepdims=True) + L2_EPS)
    return q, k, silu(cv)


def _gdn_gate_fn(ab, al_p, dt_p):
    lane = lax.broadcasted_iota(jnp.int32, ab.shape, 1)
    gpart = -jnp.exp(al_p) * _softplus(ab + dt_p)
    return (jnp.where(lane < GDN_HEADS, gpart, jnp.where(lane < 2 * GDN_HEADS, _sigmoid(ab), 0.0)),)


def _gdn_post_fn(o, z, nw):
    ms = _seg_sum(o * o, GDN_HD) * (1.0 / GDN_HD)
    return (o * lax.rsqrt(ms + NORM_EPS) * nw * (z * _sigmoid(z)),)


def _mix_fn(ga, gb, ya, yb):
    return (_sigmoid(ga) * ya + _sigmoid(gb) * yb,)


STRIP = 128


def _strip_conv(ref, prev_ref, w_ref, sl, first, taps):
    cur = ref[:, sl]
    prev = jnp.where(first, 0.0, prev_ref[:, sl])
    down = [_shift_down(cur, prev, s) for s in range(taps)]
    conv = None
    for k in range(taps):
        term = down[taps - 1 - k] * w_ref[pl.ds(k, 1), sl]
        conv = term if conv is None else conv + term
    return cur, down, conv


def _group_fwd(name, fn, x, w, shared_cols, group_cols, consts, n_out, tm):
    T, W = x.shape
    taps = w.shape[0]
    n_groups = len(group_cols)
    nc = len(consts)

    def body(x_ref, xp_ref, w_ref, *refs):
        const_refs, out_refs = refs[:nc], refs[nc:]
        first = pl.program_id(0) == 0
        shared = [_strip_conv(x_ref, xp_ref, w_ref, sl, first, taps)[2] for sl in shared_cols]
        for j, cols in enumerate(group_cols):
            sl = slice(STRIP * j, STRIP * (j + 1))
            convs = [_strip_conv(x_ref, xp_ref, w_ref, c, first, taps)[2] for c in cols]
            outs = fn(*convs, *shared, *[c[:, sl] for c in const_refs])
            for o_ref, o in zip(out_refs, outs):
                o_ref[:, sl] = o

    return pl.pallas_call(
        body, grid=(T // tm,),
        in_specs=[_row_spec(tm, W), _prev_spec(tm, W), _full_spec(w.shape)] + [_full_spec(c.shape) for c in consts],
        out_specs=[_row_spec(tm, STRIP * n_groups)] * n_out,
        out_shape=[jax.ShapeDtypeStruct((T, STRIP * n_groups), f32)] * n_out, name=name,
        compiler_params=_params("parallel"))(x, x, w, *consts)


def _group_bwd(name, fn, x, w, shared_cols, group_cols, consts, cots, tm):
    T, W = x.shape
    taps = w.shape[0]
    nblk = T // tm
    nc, ns = len(consts), len(shared_cols)
    flat_cots = [c for grp in cots for c in grp]
    n_cot = len(flat_cots)

    def body(x_ref, xp_ref, xn_ref, w_ref, *refs):
        const_refs, refs = refs[:nc], refs[nc:]
        cot_refs, cotn_refs, refs = refs[:n_cot], refs[n_cot:2 * n_cot], refs[2 * n_cot:]
        dx_ref, dw_ref, const_out = refs[0], refs[1], refs[2:]
        i = pl.program_id(0)
        first, last = i == 0, i == nblk - 1

        @pl.when(first)
        def _():
            dw_ref[...] = jnp.zeros_like(dw_ref)
            for q in range(nc):
                const_out[q][...] = jnp.zeros_like(const_out[q])

        def convs_of(sl):
            cur, down, conv = _strip_conv(x_ref, xp_ref, w_ref, sl, first, taps)
            nxt, conv_n = xn_ref[:, sl], None
            for k in range(taps):
                term = _shift_down(nxt, cur[tm - HALO:], taps - 1 - k) * w_ref[pl.ds(k, 1), sl]
                conv_n = term if conv_n is None else conv_n + term
            return down, conv, conv_n

        def conv_back(sl, down, dc, dc_n):
            dx = None
            for k in range(taps):
                s_ = taps - 1 - k
                term = _shift_up(dc, dc_n, s_) * w_ref[pl.ds(k, 1), sl]
                dx = term if dx is None else dx + term
                dw_ref[pl.ds(k, 1), sl] += jnp.sum(dc * down[s_], axis=0, keepdims=True)
            dx_ref[:, sl] = dx.astype(dx_ref.dtype)

        def summed(refs_, sl, mask):
            out, p = [], 0
            for grp in cots:
                acc = refs_[p][:, sl]
                for t in range(1, len(grp)):
                    acc = acc + refs_[p + t][:, sl]
                p += len(grp)
                out.append(jnp.where(last, 0.0, acc) if mask else acc)
            return tuple(out)

        shared = [convs_of(sl) for sl in shared_cols]
        d_shared, d_shared_n = [None] * ns, [None] * ns
        for j, cols in enumerate(group_cols):
            sl = slice(STRIP * j, STRIP * (j + 1))
            mine = [convs_of(c) for c in cols]
            cj = [c[:, sl] for c in const_refs]
            _, vjp = jax.vjp(fn, *[m[1] for m in mine], *[m[1] for m in shared], *cj)
            grads = vjp(summed(cot_refs, sl, False))
            _, vjp_n = jax.vjp(fn, *[m[2] for m in mine], *[m[2] for m in shared], *cj)
            grads_n = vjp_n(summed(cotn_refs, sl, True))
            for q, c in enumerate(cols):
                conv_back(c, mine[q][0], grads[q], grads_n[q])
            for q in range(ns):
                g, gn = grads[len(cols) + q], grads_n[len(cols) + q]
                d_shared[q] = g if d_shared[q] is None else d_shared[q] + g
                d_shared_n[q] = gn if d_shared_n[q] is None else d_shared_n[q] + gn
            for q in range(nc):
                const_out[q][:, sl] += grads[len(cols) + ns + q]
        for q, c in enumerate(shared_cols):
            conv_back(c, shared[q][0], d_shared[q], d_shared_n[q])

    outs = pl.pallas_call(
        body, grid=(nblk,),
        in_specs=[_row_spec(tm, W), _prev_spec(tm, W), _next_spec(tm, W, T), _full_spec(w.shape)]
        + [_full_spec(c.shape) for c in consts] + [_row_spec(tm, c.shape[1]) for c in flat_cots]
        + [_next_spec(tm, c.shape[1], T) for c in flat_cots],
        out_specs=[_row_spec(tm, W), _full_spec(w.shape)] + [_full_spec(c.shape) for c in consts],
        out_shape=[jax.ShapeDtypeStruct((T, W), bf16), jax.ShapeDtypeStruct(w.shape, f32)]
        + [jax.ShapeDtypeStruct(c.shape, f32) for c in consts], name=name,
        compiler_params=_params("arbitrary"))(x, x, x, w, *consts, *flat_cots, *flat_cots)
    return outs[0], outs[1], list(outs[2:])


def _ffn_strip_fn(cg, cu):
    return cg * _sigmoid(cg) * cu


def _ffn_act_fwd(h, w, tm):
    T, W2 = h.shape
    H = W2 // 2
    taps = w.shape[0]

    def body(h_ref, hp_ref, w_ref, o_ref):
        first = pl.program_id(0) == 0
        for j in range(H // STRIP):
            gs, us = slice(STRIP * j, STRIP * (j + 1)), slice(H + STRIP * j, H + STRIP * (j + 1))
            cg = _strip_conv(h_ref, hp_ref, w_ref, gs, first, taps)[2]
            cu = _strip_conv(h_ref, hp_ref, w_ref, us, first, taps)[2]
            o_ref[:, gs] = _ffn_strip_fn(cg, cu).astype(o_ref.dtype)

    return pl.pallas_call(
        body, grid=(T // tm,), in_specs=[_row_spec(tm, W2), _prev_spec(tm, W2), _full_spec(w.shape)],
        out_specs=_row_spec(tm, H), out_shape=jax.ShapeDtypeStruct((T, H), bf16), name="ffn_act",
        compiler_params=_params("parallel"))(h, h, w)


def _ffn_act_bwd(h, dact, w, tm):
    T, W2 = h.shape
    H = W2 // 2
    taps = w.shape[0]
    nblk = T // tm

    def body(h_ref, hp_ref, hn_ref, d_ref, dn_ref, w_ref, dh_ref, dw_ref):
        i = pl.program_id(0)
        first, last = i == 0, i == nblk - 1

        @pl.when(first)
        def _():
            dw_ref[...] = jnp.zeros_like(dw_ref)

        for j in range(H // STRIP):
            gs, us = slice(STRIP * j, STRIP * (j + 1)), slice(H + STRIP * j, H + STRIP * (j + 1))
            parts = {}
            for name, sl in (('g', gs), ('u', us)):
                cur, down, conv = _strip_conv(h_ref, hp_ref, w_ref, sl, first, taps)
                nxt = hn_ref[:, sl]
                conv_n = None
                for k in range(taps):
                    term = _shift_down(nxt, cur[tm - HALO:], taps - 1 - k) * w_ref[pl.ds(k, 1), sl]
                    conv_n = term if conv_n is None else conv_n + term
                parts[name] = (down, conv, conv_n)
            _, vjp = jax.vjp(_ffn_strip_fn, parts['g'][1], parts['u'][1])
            dcs = vjp(d_ref[:, gs])
            _, vjp_n = jax.vjp(_ffn_strip_fn, parts['g'][2], parts['u'][2])
            dcs_n = vjp_n(jnp.where(last, 0.0, dn_ref[:, gs]))
            for (name, sl), dc, dc_n in zip((('g', gs), ('u', us)), dcs, dcs_n):
                down = parts[name][0]
                dx = None
                for k in range(taps):
                    s_ = taps - 1 - k
                    term = _shift_up(dc, dc_n, s_) * w_ref[pl.ds(k, 1), sl]
                    dx = term if dx is None else dx + term
                    dw_ref[pl.ds(k, 1), sl] += jnp.sum(dc * down[s_], axis=0, keepdims=True)
                dh_ref[:, sl] = dx.astype(dh_ref.dtype)

    return pl.pallas_call(
        body, grid=(nblk,),
        in_specs=[_row_spec(tm, W2), _prev_spec(tm, W2), _next_spec(tm, W2, T), _row_spec(tm, H), _next_spec(tm, H, T),
                  _full_spec(w.shape)],
        out_specs=[_row_spec(tm, W2), _full_spec(w.shape)],
        out_shape=[jax.ShapeDtypeStruct((T, W2), bf16), jax.ShapeDtypeStruct(w.shape, f32)], name="ffn_act_bwd",
        compiler_params=_params("arbitrary"))(h, h, h, dact, dact, w)


N_POS = 4


def _xy_out_shapes(bufs, scatter):
    return [jax.ShapeDtypeStruct((N_POS,) + tuple(b.shape[1:] if scatter else b.shape), b.dtype) for b in bufs]


def _xy_sems(n, scatter):
    sems = [pltpu.SemaphoreType.DMA((3 * n,)), pltpu.SemaphoreType.DMA((3 * n,)), pltpu.SemaphoreType.DMA((n,))]
    return sems if scatter else sems + [pltpu.SemaphoreType.DMA((3 * n,)), pltpu.SemaphoreType.DMA((3 * n,))]


def _xy_copies(in_refs, out_refs, sems, scatter):
    n = len(in_refs)
    send_sems, recv_sems, local_sems = sems[:3]

    def place():
        x, y, c = lax.axis_index("x"), lax.axis_index("y"), lax.axis_index("c")
        return x, y, c, 2 * x + y, [(1 - x, y), (x, 1 - y), (1 - x, 1 - y)]

    def half(ref, a, which):
        rows = in_refs[a].shape[0] // 2
        return ref.at[pl.ds(pl.multiple_of(which * rows, HALO), rows)]

    def ici(a, k, src, dst, peer, c):
        return pltpu.make_async_remote_copy(
            src_ref=src, dst_ref=dst, send_sem=send_sems.at[3 * a + k], recv_sem=recv_sems.at[3 * a + k],
            device_id=(peer[0], peer[1], c), device_id_type=pl.DeviceIdType.MESH)

    def outgoing():
        x, y, c, me, peers = place()
        own = [pltpu.make_async_copy(in_refs[a].at[me] if scatter else in_refs[a], out_refs[a].at[me], local_sems.at[a])
               for a in range(n)]
        if scatter:
            sends = [ici(a, k, in_refs[a].at[2 * p[0] + p[1]], out_refs[a].at[me], p, c)
                     for a in range(n) for k, p in enumerate(peers)]
        else:
            sends = [ici(a, k, half(in_refs[a], a, c), half(out_refs[a].at[me], a, c), p, c)
                     for a in range(n) for k, p in enumerate(peers)]
        return own, sends

    def arrivals():
        x, y, c, me, peers = place()
        if scatter:
            return [ici(a, k, in_refs[a].at[me], out_refs[a].at[2 * p[0] + p[1]], p, c)
                    for a in range(n) for k, p in enumerate(peers)]
        return [ici(a, k, half(in_refs[a], a, c), half(out_refs[a].at[2 * p[0] + p[1]], a, c), p, c)
                for a in range(n) for k, p in enumerate(peers)]

    def to_sibling(mine):
        x, y, c, me, peers = place()
        which = c if mine else 1 - c
        return [pltpu.make_async_remote_copy(
            src_ref=half(out_refs[a].at[2 * p[0] + p[1]], a, which), dst_ref=half(out_refs[a].at[2 * p[0] + p[1]], a, which),
            send_sem=sems[3].at[3 * a + k], recv_sem=sems[4].at[3 * a + k],
            device_id=(x, y, 1 - c), device_id_type=pl.DeviceIdType.MESH) for a in range(n) for k, p in enumerate(peers)]

    def start():
        own, sends = outgoing()
        for cp in own + sends:
            cp.start()

    def finish():
        if scatter:
            for cp in arrivals():
                cp.wait_recv()
        else:
            passed = to_sibling(True)
            for cp, fwd in zip(arrivals(), passed):
                cp.wait_recv()
                fwd.start()
            for cp in to_sibling(False):
                cp.wait_recv()
            for fwd in passed:
                fwd.wait_send()
        own, sends = outgoing()
        for cp in sends:
            cp.wait_send()
        for cp in own:
            cp.wait()

    return start, finish


_NN, _NT, _TN = 'hcs,hsd->hcd', 'hcd,hsd->hcs', 'hcd,hce->hde'


def _lo(spec, a, b):
    return jnp.einsum(spec, a.astype(bf16), b.astype(bf16), preferred_element_type=f32)


@jax.custom_vjp
def _bmm(a, b):
    return _lo(_NN, a, b)


_bmm.defvjp(lambda a, b: (_lo(_NN, a, b), (a, b)), lambda ab, g: (_lo(_NT, g, ab[1]), _lo(_TN, ab[0], g)))


@jax.custom_vjp
def _bmm_nt(a, b):
    return _lo(_NT, a, b)


_bmm_nt.defvjp(lambda a, b: (_lo(_NT, a, b), (a, b)), lambda ab, g: (_lo(_NN, g, ab[1]), _lo(_TN, g, ab[0])))


@jax.custom_vjp
def _bmm_tn(a, b):
    return _lo(_TN, a, b)


_bmm_tn.defvjp(lambda a, b: (_lo(_TN, a, b), (a, b)), lambda ab, g: (_lo(_NT, ab[1], g), _lo(_NN, ab[0], g)))


def _masks(H, C):
    row = lax.broadcasted_iota(jnp.int32, (H, C, C), 1)
    col = lax.broadcasted_iota(jnp.int32, (H, C, C), 2)
    return row, col


def _tri_inv_impl(L):
    H, C, _ = L.shape
    row, col = _masks(H, C)
    eye = (row == col).astype(f32)
    base = 16
    same = (row // base) == (col // base)
    Ld = jnp.where(same, L, 0.0)
    X = -Ld
    inv = eye + X
    for _ in range(3):
        X = _bmm(X, X)
        inv = _bmm(inv, eye + X)
    if C == base:
        return inv
    N = _bmm(inv, L - Ld)
    out = eye - N
    levels = C // base
    P = N
    span = 2
    while span < levels:
        P = _bmm(P, P)
        out = _bmm(out, eye + P)
        span *= 2
    return _bmm(out, inv)


@jax.custom_vjp
def _tri_inv(L):
    return _tri_inv_impl(L)


def _tri_inv_fwd(L):
    T = _tri_inv_impl(L)
    return T, T


def _tri_inv_bwd(T, dT):
    return (-_bmm_nt(_bmm_tn(T, dT), T),)


_tri_inv.defvjp(_tri_inv_fwd, _tri_inv_bwd)


@jax.custom_vjp
def _tri_inv_known(L, T):
    return T


_tri_inv_known.defvjp(lambda L, T: (T, T), lambda T, dT: (_tri_inv_bwd(T, dT)[0], jnp.zeros_like(T)))


def _cumsum_impl(x, reverse):
    C = x.shape[1]
    row = lax.broadcasted_iota(jnp.int32, x.shape, 1)
    s = 1
    while s < C:
        if reverse:
            x = x + jnp.where(row < C - s, pltpu.roll(x, C - s, 1), 0.0)
        else:
            x = x + jnp.where(row >= s, pltpu.roll(x, s, 1), 0.0)
        s *= 2
    return x


@jax.custom_vjp
def _cumsum(x):
    return _cumsum_impl(x, False)


_cumsum.defvjp(lambda x: (_cumsum_impl(x, False), None), lambda _, g: (_cumsum_impl(g, True),))


def _wkv_prep(r, lw, k, v, a, b, inv=None):
    lane = lax.broadcasted_iota(jnp.int32, (r.shape[0], 128), 1)
    low = lane < RWKV_HD

    def heads(t):
        out = []
        for p in range(RWKV_HEADS // 2):
            pair = t[:, 128 * p:128 * (p + 1)]
            out += [jnp.where(low, pair, 0.0), jnp.where(low, 0.0, pair)]
        return jnp.concatenate([t[None] for t in out], axis=0)

    r, lw, k, v, a, b = [heads(t) for t in (r, lw, k, v, a, b)]
    H, C, D = r.shape
    row, col = _masks(H, C)
    incl, strict = row >= col, row > col
    cw = _cumsum(lw)
    cwp = cw - lw
    cwl = jnp.sum(lw, axis=1, keepdims=True)
    en = jnp.exp(-cw)
    at, rt, bt, kt = a * jnp.exp(cwp), r * jnp.exp(cw), b * en, k * en
    Lab = -jnp.where(strict, _bmm_nt(at, bt), 0.0)
    Tm = _tri_inv(Lab) if inv is None else _tri_inv_known(Lab, inv)
    ar = jnp.concatenate([at, rt], axis=1)
    gram = _bmm_nt(ar, jnp.concatenate([bt, kt], axis=1))
    row2 = lax.broadcasted_iota(jnp.int32, (H, 2 * C, 2 * C), 1)
    col2 = lax.broadcasted_iota(jnp.int32, (H, 2 * C, 2 * C), 2) % C
    gram = jnp.where(((row2 < C) & (row2 > col2)) | ((row2 >= C) & (row2 - C >= col2)), gram, 0.0)
    a_bk, r_bk = gram[:, :C], gram[:, C:]
    lak_v = _bmm(a_bk, jnp.concatenate([jnp.zeros_like(v), v], axis=1))
    ed = jnp.exp(cwl - cw)
    zdec = jnp.swapaxes(jnp.broadcast_to(jnp.exp(cwl), (H, D, D)), 1, 2)
    return (ar, Tm, lak_v, r_bk, jnp.concatenate([b * ed, k * ed], axis=1), zdec, v), Tm


def _wkv_step(Z, ar, Tm, lak_v, r_bk, bk_d, zdec, v):
    C = Tm.shape[1]
    ar_z = _bmm(ar, Z)
    uv = jnp.concatenate([_bmm(Tm, ar_z[:, :C] + lak_v), v], axis=1)
    y = ar_z[:, C:] + _bmm(r_bk, uv)
    Z1 = Z * zdec + _bmm_tn(bk_d, uv)
    return jnp.concatenate([y[2 * p] + y[2 * p + 1] for p in range(RWKV_HEADS // 2)], axis=1), Z1


def _split3(x):
    hi = x.astype(bf16)
    mid = (x - hi.astype(f32)).astype(bf16)
    lo = (x - hi.astype(f32) - mid.astype(f32)).astype(bf16)
    return hi, mid, lo


@jax.custom_vjp
def _spread(x, sel):
    return sum(jnp.dot(t, sel, preferred_element_type=f32) for t in _split3(x))


def _spread_bwd(sel, g):
    dn = (((1,), (1,)), ((), ()))
    return sum(lax.dot_general(t, sel, dn, preferred_element_type=f32) for t in _split3(g)), None


_spread.defvjp(lambda x, sel: (_spread(x, sel), sel), _spread_bwd)


def _gdn_prep(q, k, v, gbeta, inv=None):
    heads = lambda t: jnp.concatenate([t[None, :, GDN_HD * h:GDN_HD * (h + 1)] for h in range(GDN_HEADS)], axis=0)
    src = lax.broadcasted_iota(jnp.int32, (W_AB, 2 * GDN_W), 0)
    dst = lax.broadcasted_iota(jnp.int32, (W_AB, 2 * GDN_W), 1) // GDN_HD
    spread = _spread(gbeta, (src == dst).astype(bf16))
    q, k, v, g, beta = heads(q), heads(k), heads(v), heads(spread[:, :GDN_W]), heads(spread[:, GDN_W:])
    H, C, D = q.shape
    row, col = _masks(H, C)
    incl, strict = row >= col, row > col
    gc = _cumsum(g)
    diff = gc - jnp.swapaxes(gc, 1, 2)
    decay = jnp.where(incl, jnp.exp(jnp.where(incl, diff, 0.0)), 0.0)
    gl = jnp.sum(g, axis=1, keepdims=True)
    kb, vb = k * beta, v * beta
    gram = _bmm_nt(jnp.concatenate([kb, q], axis=1), k)
    L = jnp.where(strict, gram[:, :C] * decay, 0.0)
    attn = jnp.where(incl, gram[:, C:] * decay, 0.0)
    egc = jnp.exp(gc)
    Tm = _tri_inv(L) if inv is None else _tri_inv_known(L, inv)
    t_vk = _bmm(Tm, jnp.concatenate([vb, kb * egc], axis=2))
    return (t_vk[:, :, :D], jnp.concatenate([t_vk[:, :, D:], q * egc], axis=1), attn, k * jnp.exp(gl - gc), jnp.exp(gl)), Tm


def _gdn_step(S, u, wq, attn, ke, sdec):
    C = u.shape[1]
    wq_s = _bmm(wq, S)
    v_new = u - wq_s[:, :C]
    o = wq_s[:, C:] + _bmm(attn, v_new)
    S1 = S * sdec + _bmm_tn(ke, v_new)
    return jnp.concatenate([o[h] for h in range(GDN_HEADS)], axis=1), S1


def _scan_fwd(name, fns, ins, C, H, dh, w_out, per_step, side=None):
    prep, step = fns
    T = ins[0].shape[0]
    n_in = len(ins)
    blk = C * per_step
    nblk = T // blk
    n_side = 0 if side is None else len(side[0])

    def body(*refs):
        in_refs, refs = refs[:n_in], refs[n_in:]
        side_in, refs = refs[:n_side], refs[n_side:]
        y_ref, zs_ref, inv_ref, refs = refs[0], refs[1], refs[2], refs[3:]
        side_out, refs = refs[:n_side], refs[n_side:]
        z_scr = refs[0]
        if side is not None:
            start, finish = _xy_copies(side_in, side_out, refs[1:], side[1])
            pl.when(pl.program_id(0) == 0)(start)

        @pl.when(pl.program_id(0) == 0)
        def _():
            z_scr[...] = jnp.zeros_like(z_scr)

        rows = [slice(C * j, C * (j + 1)) for j in range(per_step)]
        prepped = [prep(*[r[rw, :] for r in in_refs]) for rw in rows]
        Z = z_scr[...]
        for j, rw in enumerate(rows):
            zs_ref[j] = Z
            inv_ref[j] = prepped[j][1]
            y, Z = step(Z, *prepped[j][0])
            y_ref[rw, :] = y
        z_scr[...] = Z
        if side is not None:
            pl.when(pl.program_id(0) == nblk - 1)(finish)

    side_bufs = [] if side is None else list(side[0])
    any_spec = pl.BlockSpec(memory_space=pl.ANY)
    return pl.pallas_call(
        body, grid=(nblk,),
        in_specs=[pl.BlockSpec((blk, a.shape[1]), lambda i: (i, 0)) for a in ins] + [any_spec] * n_side,
        out_specs=[pl.BlockSpec((blk, w_out), lambda i: (i, 0)), pl.BlockSpec((per_step, H, dh, dh), lambda i: (i, 0, 0, 0)),
                   pl.BlockSpec((per_step, H, C, C), lambda i: (i, 0, 0, 0))] + [any_spec] * n_side,
        out_shape=[jax.ShapeDtypeStruct((T, w_out), f32), jax.ShapeDtypeStruct((T // C, H, dh, dh), f32),
                   jax.ShapeDtypeStruct((T // C, H, C, C), f32)]
        + (_xy_out_shapes(side_bufs, side[1]) if side is not None else []),
        scratch_shapes=[pltpu.VMEM((H, dh, dh), f32)] + (_xy_sems(n_side, side[1]) if side is not None else []), name=name,
        compiler_params=_params("arbitrary"))(*ins, *side_bufs)


def _scan_bwd(name, fns, ins, dy, zs, invs, C, per_step, side=None):
    prep, step = fns
    T = ins[0].shape[0]
    _, H, dh, _ = zs.shape
    n_in = len(ins)
    blk = C * per_step
    nblk = T // blk
    n_side = 0 if side is None else len(side[0])

    def body(*refs):
        in_refs, dy_ref, zs_ref, inv_ref, refs = refs[:n_in], refs[n_in], refs[n_in + 1], refs[n_in + 2], refs[n_in + 3:]
        side_in, refs = refs[:n_side], refs[n_side:]
        out_refs, refs = refs[:n_in], refs[n_in:]
        side_out, refs = refs[:n_side], refs[n_side:]
        dz_scr = refs[0]
        if side is not None:
            start, finish = _xy_copies(side_in, side_out, refs[1:], side[1])
            pl.when(pl.program_id(0) == 0)(start)

        @pl.when(pl.program_id(0) == 0)
        def _():
            dz_scr[...] = jnp.zeros_like(dz_scr)

        rows = [slice(C * j, C * (j + 1)) for j in range(per_step)]
        prepped = [jax.vjp(lambda *a, j=j: prep(*a, inv=inv_ref[j])[0], *[r[rw, :] for r in in_refs])
                   for j, rw in enumerate(rows)]
        d_prepped = [None] * per_step
        dZ = dz_scr[...]
        for j in reversed(range(per_step)):
            _, pull = jax.vjp(step, zs_ref[j], *prepped[j][0])
            dZ, *d_prepped[j] = pull((dy_ref[rows[j], :], dZ))
        dz_scr[...] = dZ
        for j, rw in enumerate(rows):
            for o_ref, gval in zip(out_refs, prepped[j][1](tuple(d_prepped[j]))):
                o_ref[rw, :] = gval
        if side is not None:
            pl.when(pl.program_id(0) == nblk - 1)(finish)

    side_bufs = [] if side is None else list(side[0])
    any_spec = pl.BlockSpec(memory_space=pl.ANY)
    rev = lambda i: (nblk - 1 - i, 0)
    return pl.pallas_call(
        body, grid=(nblk,),
        in_specs=[pl.BlockSpec((blk, a.shape[1]), rev) for a in ins]
        + [pl.BlockSpec((blk, dy.shape[1]), rev), pl.BlockSpec((per_step, H, dh, dh), lambda i: (nblk - 1 - i, 0, 0, 0)),
           pl.BlockSpec((per_step, H, C, C), lambda i: (nblk - 1 - i, 0, 0, 0))] + [any_spec] * n_side,
        out_specs=[pl.BlockSpec((blk, a.shape[1]), rev) for a in ins] + [any_spec] * n_side,
        out_shape=[jax.ShapeDtypeStruct(a.shape, f32) for a in ins]
        + (_xy_out_shapes(side_bufs, side[1]) if side is not None else []),
        scratch_shapes=[pltpu.VMEM((H, dh, dh), f32)] + (_xy_sems(n_side, side[1]) if side is not None else []), name=name,
        compiler_params=_params("arbitrary"))(*ins, dy, zs, invs, *side_bufs)


def _residual_mm(name, a, b, res, tail, row_extras, consts, row_out, acc_out, tm, head=None):
    K, N = b.shape
    h_rows, h_consts = ([], []) if head is None else (list(head[1]), list(head[2]))
    lhs = h_rows + h_consts if head is not None else [a]
    M = lhs[0].shape[0]
    row_extras = [_row_of(e) for e in row_extras]
    n_lhs, n_res = len(lhs), 0 if res is None else 1
    ne, nc, nr = len(row_extras), len(consts), len(row_out)

    def body(*refs):
        lhs_refs, refs = refs[:n_lhs], refs[n_lhs:]
        b_ref, refs = refs[0], refs[1:]
        res_ref, refs = (refs[0], refs[1:]) if res is not None else (None, refs)
        extra_refs, const_refs, out_refs = refs[:ne], refs[ne:ne + nc], refs[ne + nc:]
        if head is not None:
            left = head[0](*[r[...] for r in lhs_refs])[0].astype(bf16)
            out_refs[0][...] = left
            out_refs = out_refs[1:]
        else:
            left = lhs_refs[0][...].astype(bf16)
        tile = jnp.dot(left, b_ref[...].astype(bf16), preferred_element_type=f32)
        if res is not None:
            tile = res_ref[...] + tile
        outs = tail(tile, *[r[...] for r in extra_refs], *[c[...] for c in const_refs])
        for o_ref, o in zip(out_refs[:nr], outs[:nr]):
            o_ref[...] = o.astype(o_ref.dtype)

        @pl.when(pl.program_id(0) == 0)
        def _():
            for o_ref in out_refs[nr:]:
                o_ref[...] = jnp.zeros_like(o_ref)

        for o_ref, o in zip(out_refs[nr:], outs[nr:]):
            o_ref[...] += o

    lhs_specs = ([_row_spec(tm, r.shape[1]) for r in h_rows] + [_full_spec(c.shape) for c in h_consts]
                 if head is not None else [_row_spec(tm, K)])
    head_out = [(K, bf16)] if head is not None else []
    outs = pl.pallas_call(
        body, grid=(M // tm,),
        in_specs=lhs_specs + [_full_spec(b.shape)] + ([_row_spec(tm, N)] if res is not None else [])
        + [_row_spec(tm, e[1], e[2]) for e in row_extras] + [_full_spec(c.shape) for c in consts],
        out_specs=[_row_spec(tm, w) for w, _ in head_out + list(row_out)] + [_full_spec(sh) for sh in acc_out],
        out_shape=[jax.ShapeDtypeStruct((M, w), d) for w, d in head_out + list(row_out)]
        + [jax.ShapeDtypeStruct(sh, f32) for sh in acc_out],
        name=name, compiler_params=_params("arbitrary"))(
            *lhs, b, *([res] if res is not None else []), *[e[0] for e in row_extras], *consts)
    return outs


def _pull_tail(fn):
    def tail(cot, *args):
        _, vjp = jax.vjp(fn, *args)
        return vjp((cot,))
    return tail


def _norm_tail(x1, g):
    return x1, _rms(x1, g)


def _loss_tail(x2, tgt, g):
    l, vjp = jax.vjp(lambda xv, gv: _loss_rows(xv, tgt, gv), x2, g)
    dx, dg = vjp(jnp.ones_like(l))
    return dx, dg, jnp.zeros((1, 128), f32) + jnp.sum(l)


def _local_step(x, tgt, W, late=None):
    row = lambda a: a.reshape(1, -1)
    wp = W['w_in_pad']
    w_rwkv, w_qkv, w_z = wp[:, :OFF_QKV], wp[:, OFF_QKV:OFF_Z], wp[:, OFF_Z:OFF_GATES]
    w_gates, w_ab = wp[:, OFF_GATES:OFF_AB], wp[:, OFF_AB:]
    mu = row(W['rwkv_mu'])
    mixw = jnp.concatenate([mu, 1.0 - mu], axis=0)
    zpad = jnp.zeros((64, RWKV_W), f32)
    w2p = jnp.concatenate([W['rwkv_w2'], zpad], axis=0)
    a2p = jnp.concatenate([zpad, W['rwkv_a2']], axis=0)
    rw_consts = [row(W['rwkv_w0']), w2p, row(W['rwkv_a0']), a2p, W['rwkv_g2'], row(W['rwkv_k_k']), row(W['rwkv_k_a'])]
    post_consts = [row(W['rwkv_ln_w']), row(W['rwkv_ln_b']), row(W['rwkv_r_k'])]
    pad4 = lambda a: jnp.pad(row(a), ((0, 0), (0, W_AB - GDN_HEADS)))
    gd_consts = [pad4(W['gdn_a_log']), pad4(W['gdn_dt_bias'])]
    nw_t = jnp.tile(row(W['gdn_norm_w']), (1, GDN_HEADS))
    g1, g2n, gf = row(W['norm1_g']), row(W['norm2_g']), row(W['final_g'])

    (u,) = _pw_fwd("norm1", _rms_fn, [x], [g1], [D_MODEL], 512, out_dtype=bf16)
    p_rwkv = _mm(u, w_rwkv, 'nn', "in_rwkv")
    qkv_raw = _mm(u, w_qkv, 'nn', "in_qkv")
    z = _mm(u, w_z, 'nn', "in_z")
    gates = _mm(u, w_gates, 'nn', "in_gates")
    ab = _mm(u, w_ab, 'nn', "in_ab")

    r, lw, k2, v, a_, b_, g = _pw_fwd("rwkv_prep", _rwkv_prep_fn, [p_rwkv], rw_consts, [RWKV_W] * 7, 256, conv_w=mixw)
    wkv_in = [r, lw, k2, v, a_, b_]
    y, zs_wkv, inv_wkv, *gathered = _scan_fwd("wkv_fwd", (_wkv_prep, _wkv_step), wkv_in, WKV_CHUNK, RWKV_HEADS, 2 * RWKV_HD, RWKV_W, WKV_PER_STEP,
                                     side=None if late is None else (late['shards'][0], False))
    if late is not None:
        W = dict(W, **late['assemble'](0, gathered))
    ya_in, ya = _residual_mm("rwkv_proj", None, W['rwkv_proj'], None, lambda t: (t,), [], [], [(D_MODEL, f32)], [], 512,
                             head=(_rwkv_post_fn, [y, r, k2, v, g], post_consts))

    lanes = lambda off: slice(off, off + STRIP)
    gd_groups = [[lanes(GDN_HD * h), lanes(GDN_W + GDN_HD * h), lanes(2 * GDN_W + GDN_HD * h)] for h in range(GDN_HEADS)]
    gq, gk, gv = _group_fwd("gdn_prep", _gdn_prep_fn, qkv_raw, W['gdn_conv_w'], [], gd_groups, [], 3, 256)
    (gbeta,) = _pw_fwd("gdn_gate", _gdn_gate_fn, [ab], gd_consts, [W_AB], 512)
    gdn_in = [gq, gk, gv, gbeta]
    o, zs_gdn, inv_gdn, *gathered = _scan_fwd("gdn_fwd", (_gdn_prep, _gdn_step), gdn_in, GDN_CHUNK, GDN_HEADS, GDN_HD, GDN_W, GDN_PER_STEP,
                                     side=None if late is None else (late['shards'][1], False))
    if late is not None:
        W = dict(W, **late['assemble'](1, gathered))
    ga, gb = _cols(gates, D_MODEL, 0), _cols(gates, D_MODEL, 1)
    yb_in, yb, mixed = _residual_mm("gdn_proj", None, W['gdn_proj'], None, lambda t, a_, b_, c_: (t,) + _mix_fn(a_, b_, c_, t),
                                    [ga, gb, ya], [], [(D_MODEL, f32), (D_MODEL, bf16)], [], 512,
                                    head=(_gdn_post_fn, [o, z], [nw_t]))

    x1, u2 = _residual_mm("w_out", mixed, W['w_out'], x, _norm_tail, [], [g2n], [(D_MODEL, f32), (D_MODEL, bf16)], [], 512)
    h = _mm(u2, W['ffn_up'], 'nn', "ffn_up")
    act = _ffn_act_fwd(h, W['ffn_conv_w'], 256)

    G = {}
    slab_out = None if late is None else N_POS
    dx2, dgf, loss = _residual_mm("ffn_down", act, W['ffn_down'], x1, _loss_tail, [tgt], [gf], [(D_MODEL, f32)],
                                  [gf.shape, (1, 128)], 512)
    G['final_g'] = dgf
    dact = _mm(dx2, W['ffn_down'], 'nt', "d_act")
    G['ffn_down'] = _mm(act, dx2, 'tn', "g_ffn_down", out_dtype=bf16)
    dh, G['ffn_conv_w'] = _ffn_act_bwd(h, dact, W['ffn_conv_w'], 128)
    du2 = _mm(dh, W['ffn_up'], 'nt', "d_u2")
    G['ffn_up'] = _mm(u2, dh, 'tn', "g_ffn_up", out_dtype=bf16, col_slabs=slab_out)
    (dx1,), (G['norm2_g'],) = _pw_bwd("norm2_bwd", _rms_fn, [x1], [g2n], [(du2,)], 512, add_to_first=dx2)
    G['w_out'] = _mm(mixed, dx1, 'tn', "g_w_out", out_dtype=bf16)
    dga, dgb, dya, dyb = _residual_mm("d_mixed", dx1, W['w_out'].T, None, _pull_tail(_mix_fn), [ga, gb, ya, yb], [],
                                      [(D_MODEL, bf16)] * 4, [], 512)
    G['rwkv_proj'] = _mm(ya_in, dya, 'tn', "g_rwkv_proj", out_dtype=bf16, col_slabs=slab_out)
    G['gdn_proj'] = _mm(yb_in, dyb, 'tn', "g_gdn_proj", out_dtype=bf16, col_slabs=slab_out)

    do, dz, dnw_t = _residual_mm("d_yb_in", dyb, W['gdn_proj'].T, None, _pull_tail(_gdn_post_fn), [o, z], [nw_t],
                                 [(GDN_W, f32), (GDN_W, bf16)], [nw_t.shape], 512)
    G['gdn_norm_w'] = dnw_t.reshape(GDN_HEADS, GDN_HD).sum(axis=0)
    dgq, dgk, dgv, dgbeta, *arrived_b = _scan_bwd("gdn_bwd", (_gdn_prep, _gdn_step), gdn_in, do, zs_gdn, inv_gdn, GDN_CHUNK,
                                                  GDN_PER_STEP, side=None if late is None else (late['slabs'](G, 1), True))
    dqkv_raw, G['gdn_conv_w'], _ = _group_bwd("gdn_prep_bwd", _gdn_prep_fn, qkv_raw, W['gdn_conv_w'], [], gd_groups, [],
                                              [(dgq,), (dgk,), (dgv,)], 128)
    (dab,), (dal_p, ddt_p) = _pw_bwd("gdn_gate_bwd", _gdn_gate_fn, [ab], gd_consts, [(dgbeta,)], 512, row_dtypes=[bf16])
    G['gdn_a_log'], G['gdn_dt_bias'] = dal_p[0, :GDN_HEADS], ddt_p[0, :GDN_HEADS]

    dy, dr1, dk21, dv1, dg_, G['rwkv_ln_w'], G['rwkv_ln_b'], G['rwkv_r_k'] = _residual_mm(
        "d_ya_in", dya, W['rwkv_proj'].T, None, _pull_tail(_rwkv_post_fn), [y, r, k2, v, g], post_consts,
        [(RWKV_W, f32)] * 5, [c.shape for c in post_consts], 512)
    dr2, dlw, dk22, dv2, da_, db_, *arrived_a = _scan_bwd(
        "wkv_bwd", (_wkv_prep, _wkv_step), wkv_in, dy, zs_wkv, inv_wkv, WKV_CHUNK, WKV_PER_STEP,
        side=None if late is None else (late['slabs'](G, 0), True))
    G['_arrived'] = (arrived_a, arrived_b)
    (dp_rwkv,), dmixw, rw_grads = _pw_conv_bwd(
        "rwkv_prep_bwd", _rwkv_prep_fn, [p_rwkv], rw_consts,
        [(dr1, dr2), (dlw,), (dk21, dk22), (dv1, dv2), (da_,), (db_,), (dg_,)], mixw, 256, row_dtypes=[bf16])
    G['rwkv_w0'], dw2p, G['rwkv_a0'], da2p, G['rwkv_g2'], G['rwkv_k_k'], G['rwkv_k_a'] = rw_grads
    G['rwkv_w2'], G['rwkv_a2'] = dw2p[:64], da2p[64:]
    G['rwkv_mu'] = dmixw[0] - dmixw[1]

    dps = [dp_rwkv, dqkv_raw, dz, dga, dgb, dab]
    offs = [0, OFF_QKV, OFF_Z, OFF_GATES, OFF_GATES + D_MODEL, OFF_AB]
    G['w_in_pad'] = list(zip(offs, _mm_tn_parts(u, dps, "g_w_in")))
    pairs = [(dp_rwkv, w_rwkv, 0), (dqkv_raw, w_qkv, 0), (dz, w_z, 0), (dga, w_gates, 0), (dgb, w_gates, 1), (dab, w_ab, 0)]
    if late is None:
        du = _mm_nt_parts(pairs, "d_u")
    else:
        du, *G['_arrived_w_in'] = _mm_nt_parts(pairs, "d_u", side=(late['w_in_slabs'](G), True))
    (dx,), (G['norm1_g'],) = _pw_bwd("norm1_bwd", _rms_fn, [x], [g1], [(du,)], 512, add_to_first=dx1)
    return loss, dx, G


IN_WIDTH = OFF_AB + 8
PAD_ORDER = ((0, OFF_GATES), (OFF_GATES + 8, IN_WIDTH), (OFF_GATES, OFF_GATES + 8))


def _pad_w_in_shards(shards):
    width = shards[0].shape[1]
    parts = []
    for a, b in PAD_ORDER:
        for j, sh in enumerate(shards):
            lo, hi = max(a, j * width), min(b, (j + 1) * width)
            if lo < hi:
                parts.append(sh[:, lo - j * width:hi - j * width])
    return jnp.concatenate(parts + [jnp.zeros((shards[0].shape[0], W_AB - 8), shards[0].dtype)], axis=1)


def _padded_cols(sections, s, e):
    pieces = [arr[:, max(s, o) - o:min(e, o + arr.shape[1]) - o] for o, arr in sections if max(s, o) < min(e, o + arr.shape[1])]
    return pieces[0] if len(pieces) == 1 else jnp.concatenate(pieces, axis=1)


def _unpad_cols(sections, lo, hi):
    parts, off = [], 0
    for a, b in PAD_ORDER:
        l, h = max(a, lo), min(b, hi)
        if l < h:
            parts.append((l, _padded_cols(sections, off + l - a, off + h - a)))
        off += b - a
    parts.sort(key=lambda t: t[0])
    return parts[0][1] if len(parts) == 1 else jnp.concatenate([p for _, p in parts], axis=1)


BIG = ('w_in', 'rwkv_proj', 'gdn_proj', 'w_out', 'ffn_up', 'ffn_down')
SMALL_SHARDED = ('rwkv_w2', 'rwkv_a2', 'rwkv_g2', 'gdn_conv_w', 'ffn_conv_w')


def _rows128(shape):
    n = 1
    for d in shape:
        n *= d
    return -(-n // LANES)


def _pack128(arrays):
    parts = []
    for a in arrays:
        flat = a.reshape(-1)
        rows = _rows128(a.shape)
        parts.append(jnp.pad(flat, (0, rows * LANES - flat.shape[0])).reshape(rows, LANES))
    buf = jnp.concatenate(parts, axis=0)
    return jnp.pad(buf, ((0, -buf.shape[0] % HALO), (0, 0)))


def _unpack128(buf, shapes):
    out, off = [], 0
    for s in shapes:
        rows, n = _rows128(s), 1
        for d in s:
            n *= d
        out.append(buf[off:off + rows].reshape(-1)[:n].reshape(s))
        off += rows
    return out


def _param_tile(r, c):
    best = None
    for d in range(2 * HALO, r + 1, 2 * HALO):
        if r % d == 0 and d * c * 4 <= TILE_BYTES:
            best = d
    if best is not None or r * c * 4 <= TILE_BYTES:
        return (best if best is not None else r), c
    return r, 128


def _xy_exchange(name, bufs, scatter):
    n = len(bufs)

    def body(*refs):
        start, finish = _xy_copies(refs[:n], refs[n:2 * n], refs[2 * n:], scatter)
        start()
        finish()

    return pl.pallas_call(
        body, in_specs=[pl.BlockSpec(memory_space=pl.ANY)] * n, out_specs=[pl.BlockSpec(memory_space=pl.ANY)] * n,
        out_shape=_xy_out_shapes(bufs, scatter), scratch_shapes=_xy_sems(n, scatter), name=name)(*bufs)


def _sibling_exchange(name, bufs):
    n = len(bufs)

    def body(*refs):
        in_refs, out_refs, send_sems, recv_sems = refs[:n], refs[n:2 * n], refs[2 * n], refs[2 * n + 1]
        x, y, c = lax.axis_index("x"), lax.axis_index("y"), lax.axis_index("c")
        copies = [pltpu.make_async_remote_copy(
            src_ref=in_refs[a], dst_ref=out_refs[a], send_sem=send_sems.at[a], recv_sem=recv_sems.at[a],
            device_id=(x, y, 1 - c), device_id_type=pl.DeviceIdType.MESH) for a in range(n)]
        for cp in copies:
            cp.start()
        for cp in copies:
            cp.wait()

    return pl.pallas_call(
        body, in_specs=[pl.BlockSpec(memory_space=pl.ANY)] * n, out_specs=[pl.BlockSpec(memory_space=pl.ANY)] * n,
        out_shape=[jax.ShapeDtypeStruct(b.shape, b.dtype) for b in bufs],
        scratch_shapes=[pltpu.SemaphoreType.DMA((n,)), pltpu.SemaphoreType.DMA((n,))], name=name)(*bufs)


def _sum_slots(name, buf):
    _, R, L = buf.shape
    tr, tc = _param_tile(R, L)

    def body(b_ref, o_ref):
        part = lambda s: b_ref[s].astype(f32)
        o_ref[...] = ((part(0) + part(1)) + part(2)) + part(3)

    return pl.pallas_call(
        body, grid=(R // tr, L // tc),
        in_specs=[pl.BlockSpec((N_POS, tr, tc), lambda i, j: (0, i, j))],
        out_specs=pl.BlockSpec((tr, tc), lambda i, j: (i, j)),
        out_shape=jax.ShapeDtypeStruct((R, L), f32), name=name,
        compiler_params=_params("parallel", "parallel"))(buf)


def _adamw(name, w, ga, gb, m, v):
    R, L = w.shape
    tr, tc = _param_tile(R, L)
    c1 = 1.0 / (1.0 - ADAM_B1 ** ADAM_STEP)
    c2 = 1.0 / (1.0 - ADAM_B2 ** ADAM_STEP)

    def body(w_ref, ga_ref, gb_ref, m_ref, v_ref, g_out, d_out, m_out, v_out):
        g = ga_ref[...] + gb_ref[...]
        m_new = ADAM_B1 * m_ref[...] + (1.0 - ADAM_B1) * g
        v_new = ADAM_B2 * v_ref[...] + (1.0 - ADAM_B2) * (g * g)
        g_out[...] = g
        m_out[...] = m_new
        v_out[...] = v_new
        d_out[...] = -ADAM_LR * ((m_new * c1) / (jnp.sqrt(v_new * c2) + ADAM_EPS) + ADAM_WD * w_ref[...])

    spec = pl.BlockSpec((tr, tc), lambda i, j: (i, j))
    return pl.pallas_call(
        body, grid=(R // tr, L // tc), in_specs=[spec] * 5, out_specs=[spec] * 4,
        out_shape=[jax.ShapeDtypeStruct((R, L), f32)] * 4, name=name,
        compiler_params=_params("parallel", "parallel"))(w, ga, gb, m, v)


def _step(x, loss_target, P, M, V):
    shapes = {n: tuple(P[n].shape) for n in WEIGHTS}
    sh_shapes = [shapes[n] for n in SMALL_SHARDED]
    packed = SMALL_SHARDED + SMALL

    def whole(n, g):
        return g.reshape(-1, g.shape[2]) if n in ROW_SHARDED else jnp.concatenate([g[j] for j in range(N_POS)], axis=1)

    def slabs(G, n, dtype=f32):
        r, c = shapes[n]
        full = G[n].astype(dtype)
        if full.ndim == 3:
            return full
        return full.reshape(N_POS, r, c) if n in ROW_SHARDED else full.reshape(r, N_POS, c).transpose(1, 0, 2)

    g_w_in, g_small = _xy_exchange("gather_w_in", [P['w_in'].astype(bf16), _pack128([P[n] for n in SMALL_SHARDED])],
                                   scatter=False)
    W = {n: P[n] for n in SMALL}
    W['w_in_pad'] = _pad_w_in_shards([g_w_in[j] for j in range(N_POS)])
    per_pos = [_unpack128(g_small[j], sh_shapes) for j in range(N_POS)]
    for q, n in enumerate(SMALL_SHARDED):
        W[n] = jnp.concatenate([per_pos[j][q] for j in range(N_POS)], axis=1)
    groups = (('rwkv_proj', 'gdn_proj', 'ffn_up'), ('w_out', 'ffn_down'))
    late = dict(shards=[[P[n].astype(bf16) for n in grp] for grp in groups],
                assemble=lambda q, gathered: {n: whole(n, g) for n, g in zip(groups[q], gathered)},
                slabs=lambda G, q: [slabs(G, n, bf16) for n in groups[q]],
                w_in_slabs=lambda G: [jnp.stack([_unpad_cols(G['w_in_pad'], j * shapes['w_in'][1], (j + 1) * shapes['w_in'][1])
                                                 for j in range(N_POS)])])

    loss_rows, dx, G = _local_step(x, loss_target, W, late)
    arrived = {n: a for grp, got in zip(groups, G.pop('_arrived')) for n, a in zip(grp, got)}
    (arrived_w_in,) = G.pop('_arrived_w_in')
    G.pop('w_in_pad')

    small_slabs = jnp.stack([_pack128([slabs(G, n)[j] for n in SMALL_SHARDED] + [G[n] for n in SMALL]) for j in range(N_POS)])
    (arrived_small,) = _xy_exchange("scatter_small", [small_slabs], scatter=True)
    contributions = [arrived_w_in] + [arrived[n] for n in BIG[1:]] + [arrived_small]
    tags = list(BIG) + ['small']
    plane = [_sum_slots("sum_" + t, cbuf) for t, cbuf in zip(tags, contributions)]
    sibling = _sibling_exchange("sibling_grads", plane)

    out = {}
    names4 = ('grad', 'delta', 'new_m', 'new_v')
    for q, n in enumerate(BIG):
        tr = (lambda t: t.T) if n == 'w_in' else (lambda t: t)
        for tag, t in zip(names4, _adamw("adamw_" + n, tr(P[n]), tr(plane[q]), tr(sibling[q]), tr(M[n]), tr(V[n]))):
            out[tag + '_' + n] = tr(t)
    small_out = _adamw("adamw_small", _pack128([P[n] for n in packed]), plane[-1], sibling[-1],
                       _pack128([M[n] for n in packed]), _pack128([V[n] for n in packed]))
    for tag, buf in zip(names4, small_out):
        for n, t in zip(packed, _unpack128(buf, [shapes[n] for n in packed])):
            out[tag + '_' + n] = t
    loss = lax.psum(loss_rows[0, 0], ("x", "y", "c"))
    return loss, dx, out


def kernel(x, norm1_g, w_in, rwkv_mu, rwkv_w0, rwkv_w2, rwkv_a0, rwkv_a2, rwkv_g2, rwkv_k_k, rwkv_k_a, rwkv_r_k, rwkv_ln_w, rwkv_ln_b, rwkv_proj, gdn_conv_w, gdn_a_log, gdn_dt_bias, gdn_norm_w, gdn_proj, w_out, norm2_g, ffn_up, ffn_conv_w, ffn_down, final_g, loss_target, m_norm1_g, m_w_in, m_rwkv_mu, m_rwkv_w0, m_rwkv_w2, m_rwkv_a0, m_rwkv_a2, m_rwkv_g2, m_rwkv_k_k, m_rwkv_k_a, m_rwkv_r_k, m_rwkv_ln_w, m_rwkv_ln_b, m_rwkv_proj, m_gdn_conv_w, m_gdn_a_log, m_gdn_dt_bias, m_gdn_norm_w, m_gdn_proj, m_w_out, m_norm2_g, m_ffn_up, m_ffn_conv_w, m_ffn_down, m_final_g, v_norm1_g, v_w_in, v_rwkv_mu, v_rwkv_w0, v_rwkv_w2, v_rwkv_a0, v_rwkv_a2, v_rwkv_g2, v_rwkv_k_k, v_rwkv_k_a, v_rwkv_r_k, v_rwkv_ln_w, v_rwkv_ln_b, v_rwkv_proj, v_gdn_conv_w, v_gdn_a_log, v_gdn_dt_bias, v_gdn_norm_w, v_gdn_proj, v_w_out, v_norm2_g, v_ffn_up, v_ffn_conv_w, v_ffn_down, v_final_g):
    weights = (norm1_g, w_in, rwkv_mu, rwkv_w0, rwkv_w2, rwkv_a0, rwkv_a2, rwkv_g2, rwkv_k_k, rwkv_k_a, rwkv_r_k, rwkv_ln_w,
               rwkv_ln_b, rwkv_proj, gdn_conv_w, gdn_a_log, gdn_dt_bias, gdn_norm_w, gdn_proj, w_out, norm2_g, ffn_up,
               ffn_conv_w, ffn_down, final_g)
    m_in = (m_norm1_g, m_w_in, m_rwkv_mu, m_rwkv_w0, m_rwkv_w2, m_rwkv_a0, m_rwkv_a2, m_rwkv_g2, m_rwkv_k_k, m_rwkv_k_a,
            m_rwkv_r_k, m_rwkv_ln_w, m_rwkv_ln_b, m_rwkv_proj, m_gdn_conv_w, m_gdn_a_log, m_gdn_dt_bias, m_gdn_norm_w,
            m_gdn_proj, m_w_out, m_norm2_g, m_ffn_up, m_ffn_conv_w, m_ffn_down, m_final_g)
    v_in = (v_norm1_g, v_w_in, v_rwkv_mu, v_rwkv_w0, v_rwkv_w2, v_rwkv_a0, v_rwkv_a2, v_rwkv_g2, v_rwkv_k_k, v_rwkv_k_a,
            v_rwkv_r_k, v_rwkv_ln_w, v_rwkv_ln_b, v_rwkv_proj, v_gdn_conv_w, v_gdn_a_log, v_gdn_dt_bias, v_gdn_norm_w,
            v_gdn_proj, v_w_out, v_norm2_g, v_ffn_up, v_ffn_conv_w, v_ffn_down, v_final_g)
    drop = lambda n, a: a if n == 'final_g' else a[0]
    P = {n: drop(n, a) for n, a in zip(WEIGHTS, weights)}
    M = {n: drop(n, a) for n, a in zip(WEIGHTS, m_in)}
    V = {n: drop(n, a) for n, a in zip(WEIGHTS, v_in)}
    loss, dx, out = _step(x[0], loss_target[0], P, M, V)
    lift = lambda n, a: a if n == 'final_g' else a[None]
    res = [loss, dx[None]]
    for tag in ('grad', 'delta', 'new_m', 'new_v'):
        res += [lift(n, out[tag + '_' + n]) for n in WEIGHTS]
    return tuple(res)
```

```python
import functools

import jax
import jax.numpy as jnp
from jax import lax
from jax.experimental import pallas as pl
from jax.experimental.pallas import tpu as pltpu

f32 = jnp.float32
bf16 = jnp.bfloat16

D_MODEL = 1024
RWKV_HEADS, RWKV_HD, RWKV_W = 8, 64, 512
GDN_HEADS, GDN_HD, GDN_W = 4, 128, 512
NORM_EPS, L2_EPS, GN_EPS = 1e-6, 1e-6, 64e-5
W_AB = 256
OFF_QKV, OFF_Z, OFF_GATES, OFF_AB = 1792, 3328, 3840, 5888
W_IN_PAD = OFF_AB + W_AB
WKV_CHUNK, WKV_PER_STEP = 64, 4
GDN_CHUNK, GDN_PER_STEP = 128, 4
HALO = 8
LANES = 128
TILE_BYTES = 1 << 20
VMEM_LIMIT = 56 * 1024 * 1024

ADAM_LR, ADAM_B1, ADAM_B2, ADAM_EPS, ADAM_WD, ADAM_STEP = 0.001, 0.9, 0.999, 1e-08, 0.01, 10

ROW_SHARDED = ('w_out', 'ffn_down')
SMALL = ('norm1_g', 'rwkv_mu', 'rwkv_w0', 'rwkv_a0', 'rwkv_k_k', 'rwkv_k_a', 'rwkv_r_k', 'rwkv_ln_w', 'rwkv_ln_b',
         'gdn_a_log', 'gdn_dt_bias', 'gdn_norm_w', 'norm2_g', 'final_g')
WEIGHTS = ('norm1_g', 'w_in', 'rwkv_mu', 'rwkv_w0', 'rwkv_w2', 'rwkv_a0', 'rwkv_a2', 'rwkv_g2', 'rwkv_k_k', 'rwkv_k_a',
           'rwkv_r_k', 'rwkv_ln_w', 'rwkv_ln_b', 'rwkv_proj', 'gdn_conv_w', 'gdn_a_log', 'gdn_dt_bias', 'gdn_norm_w',
           'gdn_proj', 'w_out', 'norm2_g', 'ffn_up', 'ffn_conv_w', 'ffn_down', 'final_g')


def _params(*sem):
    return pltpu.CompilerParams(dimension_semantics=sem, vmem_limit_bytes=VMEM_LIMIT)


def _tile(n, limit):
    if n <= limit:
        return n
    best = None
    for d in range(128, limit + 1, 128):
        if n % d == 0:
            best = d
    if best is None:
        raise ValueError(f"no tile for {n} under {limit}")
    return best


MM_BLOCK_BYTES = 6 << 20
MM_MAX_COLS = 1536


def _mm(a, b, mode, name, add=None, out_dtype=f32, side=None, col_slabs=None):
    if mode == 'nn':
        (M, K), N = a.shape, b.shape[1]
    elif mode == 'nt':
        (M, K), N = a.shape, b.shape[0]
    else:
        (K, M), N = a.shape, b.shape[1]
    tm = _tile(M, 1408)
    tk = _tile(K, min(2816, MM_BLOCK_BYTES // (tm * a.dtype.itemsize)))
    tn = _tile(N, max(128, min(MM_BLOCK_BYTES // (tk * b.dtype.itemsize), MM_BLOCK_BYTES // (tm * 4), MM_MAX_COLS) // 128 * 128))
    if col_slabs is not None:
        tn = N // col_slabs
    nk = K // tk
    grid = (M // tm, N // tn, nk)
    dn = {'nn': (((1,), (0,)), ((), ())), 'nt': (((1,), (1,)), ((), ())), 'tn': (((0,), (0,)), ((), ()))}[mode]
    n_add = 0 if add is None else 1
    n_side = 0 if side is None else len(side[0])

    def body(a_ref, b_ref, *rest):
        add_ref = rest[0] if add is not None else None
        side_in, rest = rest[n_add:n_add + n_side], rest[n_add + n_side:]
        o_ref, side_out, rest = rest[0], rest[1:1 + n_side], rest[1 + n_side:]
        acc_ref, rest = (rest[0], rest[1:]) if nk > 1 else (None, rest)
        ids = [pl.program_id(d) for d in range(3)]
        if side is not None:
            start, finish = _xy_copies(side_in, side_out, rest, side[1])
            pl.when((ids[0] == 0) & (ids[1] == 0) & (ids[2] == 0))(start)
        acc = lax.dot_general(a_ref[...].astype(bf16), b_ref[...].astype(bf16), dn, preferred_element_type=f32)
        if nk == 1:
            o_ref[...] = (acc + add_ref[...] if add is not None else acc).astype(out_dtype)
        else:
            k = ids[2]

            @pl.when(k == 0)
            def _():
                acc_ref[...] = acc + add_ref[...] if add is not None else acc

            @pl.when(k > 0)
            def _():
                acc_ref[...] += acc

            @pl.when(k == nk - 1)
            def _():
                o_ref[...] = acc_ref[...].astype(out_dtype)
        if side is not None:
            pl.when((ids[0] == grid[0] - 1) & (ids[1] == grid[1] - 1) & (ids[2] == nk - 1))(finish)

    a_spec = (pl.BlockSpec((tk, tm), lambda i, j, k: (k, i)) if mode == 'tn'
              else pl.BlockSpec((tm, tk), lambda i, j, k: (i, k)))
    b_spec = (pl.BlockSpec((tn, tk), lambda i, j, k: (j, k)) if mode == 'nt'
              else pl.BlockSpec((tk, tn), lambda i, j, k: (k, j)))
    o_spec = pl.BlockSpec((tm, tn), lambda i, j, k: (i, j))
    o_shape = jax.ShapeDtypeStruct((M, N), out_dtype)
    if col_slabs is not None:
        o_spec = pl.BlockSpec((None, tm, tn), lambda i, j, k: (j, i, 0))
        o_shape = jax.ShapeDtypeStruct((col_slabs, M, tn), out_dtype)
    any_spec = pl.BlockSpec(memory_space=pl.ANY)
    side_bufs = [] if side is None else list(side[0])
    ins, specs = [a, b], [a_spec, b_spec]
    if add is not None:
        ins.append(add)
        specs.append(o_spec)
    outs = pl.pallas_call(
        body, grid=grid, in_specs=specs + [any_spec] * n_side, out_specs=[o_spec] + [any_spec] * n_side,
        out_shape=[o_shape] + (_xy_out_shapes(side_bufs, side[1]) if side is not None else []),
        scratch_shapes=([pltpu.VMEM((tm, tn), f32)] if nk > 1 else []) + (_xy_sems(n_side, side[1]) if side is not None else []),
        name=name,
        compiler_params=_params(*(("arbitrary",) * 3 if side is not None else ("parallel", "parallel", "arbitrary"))))(
            *ins, *side_bufs)
    return list(outs) if side is not None else outs[0]


PARTS_TILE = 512
PARTS_DEPTH = 1024


def _mm_nt_parts(pairs, name, side=None):
    n, M, N = len(pairs), pairs[0][0].shape[0], pairs[0][1].shape[0]
    tm = min(M, PARTS_TILE)
    steps = M // tm
    n_side = 0 if side is None else len(side[0])

    def body(*refs):
        a_refs, b_refs, side_in = refs[:n], refs[n:2 * n], refs[2 * n:2 * n + n_side]
        o_ref, side_out, sems = refs[2 * n + n_side], refs[2 * n + n_side + 1:2 * n + 2 * n_side + 1], refs[2 * n + 2 * n_side + 1:]
        i = pl.program_id(0)
        if side is not None:
            start, finish = _xy_copies(side_in, side_out, sems, side[1])
            pl.when(i == 0)(start)
        acc = None
        for a_ref, b_ref in zip(a_refs, b_refs):
            part = lax.dot_general(a_ref[...].astype(bf16), b_ref[...].astype(bf16), (((1,), (1,)), ((), ())),
                                   preferred_element_type=f32)
            acc = part if acc is None else acc + part
        o_ref[...] = acc
        if side is not None:
            pl.when(i == steps - 1)(finish)

    any_spec = pl.BlockSpec(memory_space=pl.ANY)
    side_bufs = [] if side is None else list(side[0])
    o_spec = pl.BlockSpec((tm, N), lambda i: (i, 0))
    outs = pl.pallas_call(
        body, grid=(steps,),
        in_specs=[pl.BlockSpec((tm, a.shape[1]), lambda i: (i, 0)) for a, _, _ in pairs]
        + [pl.BlockSpec((N, a.shape[1]), lambda i, col=col: (0, col)) for a, _, col in pairs] + [any_spec] * n_side,
        out_specs=[o_spec] + [any_spec] * n_side,
        out_shape=[jax.ShapeDtypeStruct((M, N), f32)] + (_xy_out_shapes(side_bufs, side[1]) if side is not None else []),
        scratch_shapes=_xy_sems(n_side, side[1]) if side is not None else [],
        name=name, compiler_params=_params("arbitrary" if side is not None else "parallel"))(
            *[a for a, _, _ in pairs], *[b for _, b, _ in pairs], *side_bufs)
    return list(outs) if side is not None else outs[0]


def _mm_tn_parts(a, parts, name):
    n, (K, M) = len(parts), a.shape
    tm, tk = min(M, PARTS_TILE), min(K, PARTS_DEPTH)
    nk = K // tk
    widths = [p.shape[1] for p in parts]
    offs = [sum(widths[:q]) for q in range(n)]

    def body(a_ref, *refs):
        p_refs, o_refs, acc_ref = refs[:n], refs[n:2 * n], refs[2 * n]
        k = pl.program_id(1)
        lhs = a_ref[...].astype(bf16)
        for p_ref, o_ref, off, w in zip(p_refs, o_refs, offs, widths):
            part = lax.dot_general(lhs, p_ref[...].astype(bf16), (((0,), (0,)), ((), ())), preferred_element_type=f32)
            if nk == 1:
                o_ref[...] = part.astype(bf16)
                continue
            cols = (slice(None), slice(off, off + w))

            @pl.when(k == 0)
            def _():
                acc_ref[cols] = part

            @pl.when(k > 0)
            def _():
                acc_ref[cols] += part

            @pl.when(k == nk - 1)
            def _():
                o_ref[...] = acc_ref[cols].astype(bf16)

    return pl.pallas_call(
        body, grid=(M // tm, nk),
        in_specs=[pl.BlockSpec((tk, tm), lambda i, k: (k, i))] + [pl.BlockSpec((tk, w), lambda i, k: (k, 0)) for w in widths],
        out_specs=[pl.BlockSpec((tm, w), lambda i, k: (i, 0)) for w in widths],
        out_shape=[jax.ShapeDtypeStruct((M, w), bf16) for w in widths],
        scratch_shapes=[pltpu.VMEM((tm, sum(widths)), f32)],
        name=name, compiler_params=_params("parallel", "arbitrary"))(a, *parts)


def _shift_down(cur, prev, s):
    if s == 0:
        return cur
    ext = jnp.concatenate([prev, cur], axis=0)
    return pltpu.roll(ext, s, 0)[HALO:]


def _shift_up(cur, nxt, s):
    if s == 0:
        return cur
    ext = jnp.concatenate([cur, nxt], axis=0)
    return pltpu.roll(ext, ext.shape[0] - s, 0)[:cur.shape[0]]


def _conv_apply(cur, prev, w_ref, shifted=None):
    taps = w_ref.shape[0]
    out = None
    for i in range(taps):
        s = taps - 1 - i
        term = (shifted[s] if shifted is not None else _shift_down(cur, prev, s)) * w_ref[pl.ds(i, 1), :]
        out = term if out is None else out + term
    return out


def _row_spec(tm, w, col=0):
    return pl.BlockSpec((tm, w), lambda i: (i, col))


def _cols(a, width, col):
    return (a, width, col)


def _row_of(r):
    return r if isinstance(r, tuple) else (r, r.shape[1], 0)


def _prev_spec(tm, w):
    return pl.BlockSpec((HALO, w), lambda i: (jnp.maximum(i * (tm // HALO) - 1, 0), 0))


def _next_spec(tm, w, T):
    return pl.BlockSpec((HALO, w), lambda i: (jnp.minimum((i + 1) * (tm // HALO), T // HALO - 1), 0))


def _full_spec(shape):
    return pl.BlockSpec(shape, lambda i: (0,) * len(shape))


def _pw_fwd(name, fn, rows, consts, out_widths, tm, conv_w=None, out_dtype=f32):
    T = _row_of(rows[0])[0].shape[0]
    nr, nc = len(rows), len(consts)

    def body(*refs):
        i = pl.program_id(0)
        vals = [r[...] for r in refs[:nr]]
        p = nr
        if conv_w is not None:
            prev = jnp.where(i > 0, refs[p][...], 0.0)
            vals[0] = _conv_apply(vals[0], prev, refs[p + 1])
            p += 2
        cvals = [r[...] for r in refs[p:p + nc]]
        outs = fn(*vals, *cvals)
        for o_ref, o in zip(refs[p + nc:], outs):
            o_ref[...] = o.astype(out_dtype)

    ins = [_row_of(r)[0] for r in rows]
    specs = [_row_spec(tm, *_row_of(r)[1:]) for r in rows]
    if conv_w is not None:
        ins += [rows[0], conv_w]
        specs += [_prev_spec(tm, rows[0].shape[1]), _full_spec(conv_w.shape)]
    ins += list(consts)
    specs += [_full_spec(c.shape) for c in consts]
    outs = pl.pallas_call(
        body, grid=(T // tm,), in_specs=specs,
        out_specs=[_row_spec(tm, w) for w in out_widths],
        out_shape=[jax.ShapeDtypeStruct((T, w), out_dtype) for w in out_widths], name=name,
        compiler_params=_params("parallel"))(*ins)
    return outs


def _pw_bwd(name, fn, rows, consts, cots, tm, add_to_first=None, row_dtypes=None):
    rows = [_row_of(r) for r in rows]
    T = rows[0][0].shape[0]
    nr, nc = len(rows), len(consts)
    flat_cots = [c for grp in cots for c in grp]
    row_dtypes = row_dtypes or [f32] * nr
    n_extra = 0 if add_to_first is None else 1

    def body(*refs):
        i = pl.program_id(0)
        in_refs, cot_refs = refs[:nr + nc], refs[nr + nc:nr + nc + len(flat_cots)]
        extra_ref = refs[nr + nc + len(flat_cots)] if add_to_first is not None else None
        row_out = refs[nr + nc + len(flat_cots) + n_extra:][:nr]
        const_out = refs[nr + nc + len(flat_cots) + n_extra + nr:]

        @pl.when(i == 0)
        def _():
            for q in range(nc):
                const_out[q][...] = jnp.zeros_like(const_out[q])

        def part(sl):
            cot_vals, p = [], 0
            for grp in cots:
                acc = cot_refs[p][:, sl]
                for q in range(1, len(grp)):
                    acc = acc + cot_refs[p + q][:, sl]
                p += len(grp)
                cot_vals.append(acc)
            _, vjp = jax.vjp(fn, *[r[:, sl] for r in in_refs])
            grads = vjp(tuple(cot_vals))
            for q in range(nr):
                g = grads[q]
                if q == 0 and extra_ref is not None:
                    g = g + extra_ref[:, sl]
                row_out[q][:, sl] = g.astype(row_dtypes[q])
            for q in range(nc):
                const_out[q][:, sl] += grads[nr + q]

        part(slice(None))

    ins = [r[0] for r in rows] + list(consts) + flat_cots
    specs = ([_row_spec(tm, r[1], r[2]) for r in rows] + [_full_spec(c.shape) for c in consts]
             + [_row_spec(tm, c.shape[1]) for c in flat_cots])
    if add_to_first is not None:
        ins.append(add_to_first)
        specs.append(_row_spec(tm, add_to_first.shape[1]))
    out_shapes = ([jax.ShapeDtypeStruct((T, r[1]), d) for r, d in zip(rows, row_dtypes)]
                  + [jax.ShapeDtypeStruct(c.shape, f32) for c in consts])
    out_specs = [_row_spec(tm, r[1]) for r in rows] + [_full_spec(c.shape) for c in consts]
    outs = pl.pallas_call(
        body, grid=(T // tm,), in_specs=specs, out_specs=out_specs, out_shape=out_shapes, name=name,
        compiler_params=_params("arbitrary"))(*ins)
    return list(outs[:nr]), list(outs[nr:])


def _pw_conv_bwd(name, fn, rows, consts, cots, conv_w, tm, row_dtypes=None):
    T, W0 = rows[0].shape
    nr, nc = len(rows), len(consts)
    taps = conv_w.shape[0]
    nblk = T // tm
    flat_cots = [c for grp in cots for c in grp]
    row_dtypes = row_dtypes or [f32] * nr

    def body(*refs):
        i = pl.program_id(0)
        p = 0
        cur = [r[...] for r in refs[p:p + nr]]; p += nr
        nxt = [r[...] for r in refs[p:p + nr]]; p += nr
        prev = jnp.where(i > 0, refs[p][...], 0.0); p += 1
        w_ref = refs[p]; p += 1
        cvals = [r[...] for r in refs[p:p + nc]]; p += nc

        def summed(p0):
            out, q = [], p0
            for grp in cots:
                acc = refs[q][...]
                for t in range(1, len(grp)):
                    acc = acc + refs[q + t][...]
                q += len(grp)
                out.append(acc)
            return out, q

        cot_cur, p = summed(p)
        cot_nxt, p = summed(p)
        row_out, dw_ref, const_out = refs[p:p + nr], refs[p + nr], refs[p + nr + 1:]

        x_cur = cur[0]
        x_down = [_shift_down(x_cur, prev, s_) for s_ in range(taps)]
        _, vjp = jax.vjp(fn, _conv_apply(x_cur, prev, w_ref, x_down), *cur[1:], *cvals)
        grads = vjp(tuple(cot_cur))
        _, vjp_n = jax.vjp(fn, _conv_apply(nxt[0], x_cur[tm - HALO:], w_ref), *nxt[1:], *cvals)
        dc_n = jnp.where(i < nblk - 1, vjp_n(tuple(cot_nxt))[0], 0.0)
        dc = grads[0]

        @pl.when(i == 0)
        def _():
            dw_ref[...] = jnp.zeros_like(dw_ref)
            for q in range(nc):
                const_out[q][...] = jnp.zeros_like(const_out[q])

        dx = None
        for k in range(taps):
            s_ = taps - 1 - k
            term = _shift_up(dc, dc_n, s_) * w_ref[pl.ds(k, 1), :]
            dx = term if dx is None else dx + term
            dw_ref[pl.ds(k, 1), :] += jnp.sum(dc * x_down[s_], axis=0, keepdims=True)
        row_out[0][...] = dx.astype(row_dtypes[0])
        for q in range(1, nr):
            row_out[q][...] = grads[q].astype(row_dtypes[q])
        for q in range(nc):
            const_out[q][...] += grads[nr + q]

    ins = list(rows) + list(rows) + [rows[0], conv_w] + list(consts) + flat_cots + flat_cots
    specs = ([_row_spec(tm, r.shape[1]) for r in rows] + [_next_spec(tm, r.shape[1], T) for r in rows]
             + [_prev_spec(tm, W0), _full_spec(conv_w.shape)] + [_full_spec(c.shape) for c in consts]
             + [_row_spec(tm, c.shape[1]) for c in flat_cots] + [_next_spec(tm, c.shape[1], T) for c in flat_cots])
    out_shapes = ([jax.ShapeDtypeStruct(r.shape, d) for r, d in zip(rows, row_dtypes)]
                  + [jax.ShapeDtypeStruct(conv_w.shape, f32)] + [jax.ShapeDtypeStruct(c.shape, f32) for c in consts])
    out_specs = ([_row_spec(tm, r.shape[1]) for r in rows] + [_full_spec(conv_w.shape)]
                 + [_full_spec(c.shape) for c in consts])
    outs = pl.pallas_call(
        body, grid=(nblk,), in_specs=specs, out_specs=out_specs, out_shape=out_shapes, name=name,
        compiler_params=_params("arbitrary"))(*ins)
    return list(outs[:nr]), outs[nr], list(outs[nr + 1:])


def _sigmoid(x):
    return 0.5 * jnp.tanh(0.5 * x) + 0.5


def _softplus(x):
    return jnp.maximum(x, 0.0) + jnp.log(1.0 + jnp.exp(jnp.minimum(x, -x)))


def _seg_sum_impl(x, seg):
    w = x.shape[-1]
    r = lax.broadcasted_iota(jnp.int32, (w, w), 0) // seg
    c = lax.broadcasted_iota(jnp.int32, (w, w), 1) // seg
    ones = (r == c).astype(bf16)
    hi = x.astype(bf16)
    lo = (x - hi.astype(f32)).astype(bf16)
    return (jnp.dot(hi, ones, preferred_element_type=f32) + jnp.dot(lo, ones, preferred_element_type=f32))


@functools.partial(jax.custom_vjp, nondiff_argnums=(1,))
def _seg_sum(x, seg):
    return _seg_sum_impl(x, seg)


_seg_sum.defvjp(lambda x, seg: (_seg_sum_impl(x, seg), None), lambda seg, _, g: (_seg_sum_impl(g, seg),))


def _rms(x, g):
    return x * lax.rsqrt(jnp.mean(x * x, axis=-1, keepdims=True) + NORM_EPS) * g


def _rms_fn(x, g):
    return (_rms(x, g),)


def _loss_rows(x2, tgt, g):
    e = _rms(x2, g) - tgt
    return 0.5 * jnp.sum(e * e, axis=-1, keepdims=True) * (1.0 / D_MODEL)


@jax.custom_vjp
def _dot_lo(a, b):
    return jnp.dot(a.astype(bf16), b.astype(bf16), preferred_element_type=f32)


def _dot_lo_bwd(ab, g):
    a, b = ab
    gl = g.astype(bf16)
    return (lax.dot_general(gl, b.astype(bf16), (((1,), (1,)), ((), ())), preferred_element_type=f32),
            lax.dot_general(a.astype(bf16), gl, (((0,), (0,)), ((), ())), preferred_element_type=f32))


_dot_lo.defvjp(lambda a, b: (_dot_lo(a, b), (a, b)), _dot_lo_bwd)


def _rwkv_prep_fn(ps, w0, w2p, a0, a2p, g2, k_k, k_a):
    r, k, v = ps[:, 0:512], ps[:, 512:1024], ps[:, 1024:1536]
    wa, gl = ps[:, 1536:1664], ps[:, 1664:1792]
    z = w0 + _dot_lo(jnp.tanh(wa), w2p)
    w_log = -_softplus(-z) - 0.5
    lw = -jnp.exp(w_log)
    a = _sigmoid(a0 + _dot_lo(wa, a2p))
    g = _dot_lo(_sigmoid(gl), g2)
    kx = k * k_k
    kk = kx * lax.rsqrt(_seg_sum(kx * kx, RWKV_HD) + L2_EPS)
    k2 = k * (1.0 + (a - 1.0) * k_a)
    return r, lw, k2, v, -kk, kk * a, g


def _rwkv_post_fn(y, r, k2, v, g, ln_w, ln_b, rk):
    mean = _seg_sum(y, RWKV_HD) * (1.0 / RWKV_HD)
    yc = y - mean
    var = _seg_sum(yc * yc, RWKV_HD) * (1.0 / RWKV_HD)
    yn = yc * lax.rsqrt(var + GN_EPS) * ln_w + ln_b
    bonus = _seg_sum(r * k2 * rk, RWKV_HD) * v
    return ((yn + bonus) * g,)


def _gdn_prep_fn(cq, ck, cv):
    silu = lambda c: c * _sigmoid(c)
    q, k = silu(cq), silu(ck)
    q = q * lax.rsqrt(jnp.sum(q * q, axis=-1, keepdims=True) + L2_EPS) * (GDN_HD ** -0.5)
    k = k * lax.rsqrt(jnp.sum(k * k, axis=-1, keepdims=True) + L2_EPS)
    return q, k, silu(cv)


def _gdn_gate_fn(ab, al_p, dt_p):
    lane = lax.broadcasted_iota(jnp.int32, ab.shape, 1)
    gpart = -jnp.exp(al_p) * _softplus(ab + dt_p)
    return (jnp.where(lane < GDN_HEADS, gpart, jnp.where(lane < 2 * GDN_HEADS, _sigmoid(ab), 0.0)),)


def _gdn_post_fn(o, z, nw):
    ms = _seg_sum(o * o, GDN_HD) * (1.0 / GDN_HD)
    return (o * lax.rsqrt(ms + NORM_EPS) * nw * (z * _sigmoid(z)),)


def _mix_fn(ga, gb, ya, yb):
    return (_sigmoid(ga) * ya + _sigmoid(gb) * yb,)


STRIP = 128


def _strip_conv(ref, prev_ref, w_ref, sl, first, taps):
    cur = ref[:, sl]
    prev = jnp.where(first, 0.0, prev_ref[:, sl])
    down = [_shift_down(cur, prev, s) for s in range(taps)]
    conv = None
    for k in range(taps):
        term = down[taps - 1 - k] * w_ref[pl.ds(k, 1), sl]
        conv = term if conv is None else conv + term
    return cur, down, conv


def _group_fwd(name, fn, x, w, shared_cols, group_cols, consts, n_out, tm):
    T, W = x.shape
    taps = w.shape[0]
    n_groups = len(group_cols)
    nc = len(consts)

    def body(x_ref, xp_ref, w_ref, *refs):
        const_refs, out_refs = refs[:nc], refs[nc:]
        first = pl.program_id(0) == 0
        shared = [_strip_conv(x_ref, xp_ref, w_ref, sl, first, taps)[2] for sl in shared_cols]
        for j, cols in enumerate(group_cols):
            sl = slice(STRIP * j, STRIP * (j + 1))
            convs = [_strip_conv(x_ref, xp_ref, w_ref, c, first, taps)[2] for c in cols]
            outs = fn(*convs, *shared, *[c[:, sl] for c in const_refs])
            for o_ref, o in zip(out_refs, outs):
                o_ref[:, sl] = o

    return pl.pallas_call(
        body, grid=(T // tm,),
        in_specs=[_row_spec(tm, W), _prev_spec(tm, W), _full_spec(w.shape)] + [_full_spec(c.shape) for c in consts],
        out_specs=[_row_spec(tm, STRIP * n_groups)] * n_out,
        out_shape=[jax.ShapeDtypeStruct((T, STRIP * n_groups), f32)] * n_out, name=name,
        compiler_params=_params("parallel"))(x, x, w, *consts)


def _group_bwd(name, fn, x, w, shared_cols, group_cols, consts, cots, tm):
    T, W = x.shape
    taps = w.shape[0]
    nblk = T // tm
    nc, ns = len(consts), len(shared_cols)
    flat_cots = [c for grp in cots for c in grp]
    n_cot = len(flat_cots)

    def body(x_ref, xp_ref, xn_ref, w_ref, *refs):
        const_refs, refs = refs[:nc], refs[nc:]
        cot_refs, cotn_refs, refs = refs[:n_cot], refs[n_cot:2 * n_cot], refs[2 * n_cot:]
        dx_ref, dw_ref, const_out = refs[0], refs[1], refs[2:]
        i = pl.program_id(0)
        first, last = i == 0, i == nblk - 1

        @pl.when(first)
        def _():
            dw_ref[...] = jnp.zeros_like(dw_ref)
            for q in range(nc):
                const_out[q][...] = jnp.zeros_like(const_out[q])

        def convs_of(sl):
            cur, down, conv = _strip_conv(x_ref, xp_ref, w_ref, sl, first, taps)
            nxt, conv_n = xn_ref[:, sl], None
            for k in range(taps):
                term = _shift_down(nxt, cur[tm - HALO:], taps - 1 - k) * w_ref[pl.ds(k, 1), sl]
                conv_n = term if conv_n is None else conv_n + term
            return down, conv, conv_n

        def conv_back(sl, down, dc, dc_n):
            dx = None
            for k in range(taps):
                s_ = taps - 1 - k
                term = _shift_up(dc, dc_n, s_) * w_ref[pl.ds(k, 1), sl]
                dx = term if dx is None else dx + term
                dw_ref[pl.ds(k, 1), sl] += jnp.sum(dc * down[s_], axis=0, keepdims=True)
            dx_ref[:, sl] = dx.astype(dx_ref.dtype)

        def summed(refs_, sl, mask):
            out, p = [], 0
            for grp in cots:
                acc = refs_[p][:, sl]
                for t in range(1, len(grp)):
                    acc = acc + refs_[p + t][:, sl]
                p += len(grp)
                out.append(jnp.where(last, 0.0, acc) if mask else acc)
            return tuple(out)

        shared = [convs_of(sl) for sl in shared_cols]
        d_shared, d_shared_n = [None] * ns, [None] * ns
        for j, cols in enumerate(group_cols):
            sl = slice(STRIP * j, STRIP * (j + 1))
            mine = [convs_of(c) for c in cols]
            cj = [c[:, sl] for c in const_refs]
            _, vjp = jax.vjp(fn, *[m[1] for m in mine], *[m[1] for m in shared], *cj)
            grads = vjp(summed(cot_refs, sl, False))
            _, vjp_n = jax.vjp(fn, *[m[2] for m in mine], *[m[2] for m in shared], *cj)
            grads_n = vjp_n(summed(cotn_refs, sl, True))
            for q, c in enumerate(cols):
                conv_back(c, mine[q][0], grads[q], grads_n[q])
            for q in range(ns):
                g, gn = grads[len(cols) + q], grads_n[len(cols) + q]
                d_shared[q] = g if d_shared[q] is None else d_shared[q] + g
                d_shared_n[q] = gn if d_shared_n[q] is None else d_shared_n[q] + gn
            for q in range(nc):
                const_out[q][:, sl] += grads[len(cols) + ns + q]
        for q, c in enumerate(shared_cols):
            conv_back(c, shared[q][0], d_shared[q], d_shared_n[q])

    outs = pl.pallas_call(
        body, grid=(nblk,),
        in_specs=[_row_spec(tm, W), _prev_spec(tm, W), _next_spec(tm, W, T), _full_spec(w.shape)]
        + [_full_spec(c.shape) for c in consts] + [_row_spec(tm, c.shape[1]) for c in flat_cots]
        + [_next_spec(tm, c.shape[1], T) for c in flat_cots],
        out_specs=[_row_spec(tm, W), _full_spec(w.shape)] + [_full_spec(c.shape) for c in consts],
        out_shape=[jax.ShapeDtypeStruct((T, W), bf16), jax.ShapeDtypeStruct(w.shape, f32)]
        + [jax.ShapeDtypeStruct(c.shape, f32) for c in consts], name=name,
        compiler_params=_params("arbitrary"))(x, x, x, w, *consts, *flat_cots, *flat_cots)
    return outs[0], outs[1], list(outs[2:])


def _ffn_strip_fn(cg, cu):
    return cg * _sigmoid(cg) * cu


def _ffn_act_fwd(h, w, tm):
    T, W2 = h.shape
    H = W2 // 2
    taps = w.shape[0]

    def body(h_ref, hp_ref, w_ref, o_ref):
        first = pl.program_id(0) == 0
        for j in range(H // STRIP):
            gs, us = slice(STRIP * j, STRIP * (j + 1)), slice(H + STRIP * j, H + STRIP * (j + 1))
            cg = _strip_conv(h_ref, hp_ref, w_ref, gs, first, taps)[2]
            cu = _strip_conv(h_ref, hp_ref, w_ref, us, first, taps)[2]
            o_ref[:, gs] = _ffn_strip_fn(cg, cu).astype(o_ref.dtype)

    return pl.pallas_call(
        body, grid=(T // tm,), in_specs=[_row_spec(tm, W2), _prev_spec(tm, W2), _full_spec(w.shape)],
        out_specs=_row_spec(tm, H), out_shape=jax.ShapeDtypeStruct((T, H), bf16), name="ffn_act",
        compiler_params=_params("parallel"))(h, h, w)


def _ffn_act_bwd(h, dact, w, tm):
    T, W2 = h.shape
    H = W2 // 2
    taps = w.shape[0]
    nblk = T // tm

    def body(h_ref, hp_ref, hn_ref, d_ref, dn_ref, w_ref, dh_ref, dw_ref):
        i = pl.program_id(0)
        first, last = i == 0, i == nblk - 1

        @pl.when(first)
        def _():
            dw_ref[...] = jnp.zeros_like(dw_ref)

        for j in range(H // STRIP):
            gs, us = slice(STRIP * j, STRIP * (j + 1)), slice(H + STRIP * j, H + STRIP * (j + 1))
            parts = {}
            for name, sl in (('g', gs), ('u', us)):
                cur, down, conv = _strip_conv(h_ref, hp_ref, w_ref, sl, first, taps)
                nxt = hn_ref[:, sl]
                conv_n = None
                for k in range(taps):
                    term = _shift_down(nxt, cur[tm - HALO:], taps - 1 - k) * w_ref[pl.ds(k, 1), sl]
                    conv_n = term if conv_n is None else conv_n + term
                parts[name] = (down, conv, conv_n)
            _, vjp = jax.vjp(_ffn_strip_fn, parts['g'][1], parts['u'][1])
            dcs = vjp(d_ref[:, gs])
            _, vjp_n = jax.vjp(_ffn_strip_fn, parts['g'][2], parts['u'][2])
            dcs_n = vjp_n(jnp.where(last, 0.0, dn_ref[:, gs]))
            for (name, sl), dc, dc_n in zip((('g', gs), ('u', us)), dcs, dcs_n):
                down = parts[name][0]
                dx = None
                for k in range(taps):
                    s_ = taps - 1 - k
                    term = _shift_up(dc, dc_n, s_) * w_ref[pl.ds(k, 1), sl]
                    dx = term if dx is None else dx + term
                    dw_ref[pl.ds(k, 1), sl] += jnp.sum(dc * down[s_], axis=0, keepdims=True)
                dh_ref[:, sl] = dx.astype(dh_ref.dtype)

    return pl.pallas_call(
        body, grid=(nblk,),
        in_specs=[_row_spec(tm, W2), _prev_spec(tm, W2), _next_spec(tm, W2, T), _row_spec(tm, H), _next_spec(tm, H, T),
                  _full_spec(w.shape)],
        out_specs=[_row_spec(tm, W2), _full_spec(w.shape)],
        out_shape=[jax.ShapeDtypeStruct((T, W2), bf16), jax.ShapeDtypeStruct(w.shape, f32)], name="ffn_act_bwd",
        compiler_params=_params("arbitrary"))(h, h, h, dact, dact, w)


N_POS = 4


def _xy_out_shapes(bufs, scatter):
    return [jax.ShapeDtypeStruct((N_POS,) + tuple(b.shape[1:] if scatter else b.shape), b.dtype) for b in bufs]


def _xy_sems(n, scatter):
    sems = [pltpu.SemaphoreType.DMA((3 * n,)), pltpu.SemaphoreType.DMA((3 * n,)), pltpu.SemaphoreType.DMA((n,))]
    return sems if scatter else sems + [pltpu.SemaphoreType.DMA((3 * n,)), pltpu.SemaphoreType.DMA((3 * n,))]


def _xy_copies(in_refs, out_refs, sems, scatter):
    n = len(in_refs)
    send_sems, recv_sems, local_sems = sems[:3]

    def place():
        x, y, c = lax.axis_index("x"), lax.axis_index("y"), lax.axis_index("c")
        return x, y, c, 2 * x + y, [(1 - x, y), (x, 1 - y), (1 - x, 1 - y)]

    def half(ref, a, which):
        rows = in_refs[a].shape[0] // 2
        return ref.at[pl.ds(pl.multiple_of(which * rows, HALO), rows)]

    def ici(a, k, src, dst, peer, c):
        return pltpu.make_async_remote_copy(
            src_ref=src, dst_ref=dst, send_sem=send_sems.at[3 * a + k], recv_sem=recv_sems.at[3 * a + k],
            device_id=(peer[0], peer[1], c), device_id_type=pl.DeviceIdType.MESH)

    def outgoing():
        x, y, c, me, peers = place()
        own = [pltpu.make_async_copy(in_refs[a].at[me] if scatter else in_refs[a], out_refs[a].at[me], local_sems.at[a])
               for a in range(n)]
        if scatter:
            sends = [ici(a, k, in_refs[a].at[2 * p[0] + p[1]], out_refs[a].at[me], p, c)
                     for a in range(n) for k, p in enumerate(peers)]
        else:
            sends = [ici(a, k, half(in_refs[a], a, c), half(out_refs[a].at[me], a, c), p, c)
                     for a in range(n) for k, p in enumerate(peers)]
        return own, sends

    def arrivals():
        x, y, c, me, peers = place()
        if scatter:
            return [ici(a, k, in_refs[a].at[me], out_refs[a].at[2 * p[0] + p[1]], p, c)
                    for a in range(n) for k, p in enumerate(peers)]
        return [ici(a, k, half(in_refs[a], a, c), half(out_refs[a].at[2 * p[0] + p[1]], a, c), p, c)
                for a in range(n) for k, p in enumerate(peers)]

    def to_sibling(mine):
        x, y, c, me, peers = place()
        which = c if mine else 1 - c
        return [pltpu.make_async_remote_copy(
            src_ref=half(out_refs[a].at[2 * p[0] + p[1]], a, which), dst_ref=half(out_refs[a].at[2 * p[0] + p[1]], a, which),
            send_sem=sems[3].at[3 * a + k], recv_sem=sems[4].at[3 * a + k],
            device_id=(x, y, 1 - c), device_id_type=pl.DeviceIdType.MESH) for a in range(n) for k, p in enumerate(peers)]

    def start():
        own, sends = outgoing()
        for cp in own + sends:
            cp.start()

    def finish():
        if scatter:
            for cp in arrivals():
                cp.wait_recv()
        else:
            passed = to_sibling(True)
            for cp, fwd in zip(arrivals(), passed):
                cp.wait_recv()
                fwd.start()
            for cp in to_sibling(False):
                cp.wait_recv()
            for fwd in passed:
                fwd.wait_send()
        own, sends = outgoing()
        for cp in sends:
            cp.wait_send()
        for cp in own:
            cp.wait()

    return start, finish


_NN, _NT, _TN = 'hcs,hsd->hcd', 'hcd,hsd->hcs', 'hcd,hce->hde'


def _lo(spec, a, b):
    return jnp.einsum(spec, a.astype(bf16), b.astype(bf16), preferred_element_type=f32)


@jax.custom_vjp
def _bmm(a, b):
    return _lo(_NN, a, b)


_bmm.defvjp(lambda a, b: (_lo(_NN, a, b), (a, b)), lambda ab, g: (_lo(_NT, g, ab[1]), _lo(_TN, ab[0], g)))


@jax.custom_vjp
def _bmm_nt(a, b):
    return _lo(_NT, a, b)


_bmm_nt.defvjp(lambda a, b: (_lo(_NT, a, b), (a, b)), lambda ab, g: (_lo(_NN, g, ab[1]), _lo(_TN, g, ab[0])))


@jax.custom_vjp
def _bmm_tn(a, b):
    return _lo(_TN, a, b)


_bmm_tn.defvjp(lambda a, b: (_lo(_TN, a, b), (a, b)), lambda ab, g: (_lo(_NT, ab[1], g), _lo(_NN, ab[0], g)))


def _masks(H, C):
    row = lax.broadcasted_iota(jnp.int32, (H, C, C), 1)
    col = lax.broadcasted_iota(jnp.int32, (H, C, C), 2)
    return row, col


def _tri_inv_impl(L):
    H, C, _ = L.shape
    row, col = _masks(H, C)
    eye = (row == col).astype(f32)
    base = 16
    same = (row // base) == (col // base)
    Ld = jnp.where(same, L, 0.0)
    X = -Ld
    inv = eye + X
    for _ in range(3):
        X = _bmm(X, X)
        inv = _bmm(inv, eye + X)
    if C == base:
        return inv
    N = _bmm(inv, L - Ld)
    out = eye - N
    levels = C // base
    P = N
    span = 2
    while span < levels:
        P = _bmm(P, P)
        out = _bmm(out, eye + P)
        span *= 2
    return _bmm(out, inv)


@jax.custom_vjp
def _tri_inv(L):
    return _tri_inv_impl(L)


def _tri_inv_fwd(L):
    T = _tri_inv_impl(L)
    return T, T


def _tri_inv_bwd(T, dT):
    return (-_bmm_nt(_bmm_tn(T, dT), T),)


_tri_inv.defvjp(_tri_inv_fwd, _tri_inv_bwd)


@jax.custom_vjp
def _tri_inv_known(L, T):
    return T


_tri_inv_known.defvjp(lambda L, T: (T, T), lambda T, dT: (_tri_inv_bwd(T, dT)[0], jnp.zeros_like(T)))


def _cumsum_impl(x, reverse):
    C = x.shape[1]
    row = lax.broadcasted_iota(jnp.int32, x.shape, 1)
    s = 1
    while s < C:
        if reverse:
            x = x + jnp.where(row < C - s, pltpu.roll(x, C - s, 1), 0.0)
        else:
            x = x + jnp.where(row >= s, pltpu.roll(x, s, 1), 0.0)
        s *= 2
    return x


@jax.custom_vjp
def _cumsum(x):
    return _cumsum_impl(x, False)


_cumsum.defvjp(lambda x: (_cumsum_impl(x, False), None), lambda _, g: (_cumsum_impl(g, True),))


def _wkv_prep(r, lw, k, v, a, b, inv=None):
    lane = lax.broadcasted_iota(jnp.int32, (r.shape[0], 128), 1)
    low = lane < RWKV_HD

    def heads(t):
        out = []
        for p in range(RWKV_HEADS // 2):
            pair = t[:, 128 * p:128 * (p + 1)]
            out += [jnp.where(low, pair, 0.0), jnp.where(low, 0.0, pair)]
        return jnp.concatenate([t[None] for t in out], axis=0)

    r, lw, k, v, a, b = [heads(t) for t in (r, lw, k, v, a, b)]
    H, C, D = r.shape
    row, col = _masks(H, C)
    incl, strict = row >= col, row > col
    cw = _cumsum(lw)
    cwp = cw - lw
    cwl = jnp.sum(lw, axis=1, keepdims=True)
    en = jnp.exp(-cw)
    at, rt, bt, kt = a * jnp.exp(cwp), r * jnp.exp(cw), b * en, k * en
    Lab = -jnp.where(strict, _bmm_nt(at, bt), 0.0)
    Tm = _tri_inv(Lab) if inv is None else _tri_inv_known(Lab, inv)
    ar = jnp.concatenate([at, rt], axis=1)
    gram = _bmm_nt(ar, jnp.concatenate([bt, kt], axis=1))
    row2 = lax.broadcasted_iota(jnp.int32, (H, 2 * C, 2 * C), 1)
    col2 = lax.broadcasted_iota(jnp.int32, (H, 2 * C, 2 * C), 2) % C
    gram = jnp.where(((row2 < C) & (row2 > col2)) | ((row2 >= C) & (row2 - C >= col2)), gram, 0.0)
    a_bk, r_bk = gram[:, :C], gram[:, C:]
    lak_v = _bmm(a_bk, jnp.concatenate([jnp.zeros_like(v), v], axis=1))
    ed = jnp.exp(cwl - cw)
    zdec = jnp.swapaxes(jnp.broadcast_to(jnp.exp(cwl), (H, D, D)), 1, 2)
    return (ar, Tm, lak_v, r_bk, jnp.concatenate([b * ed, k * ed], axis=1), zdec, v), Tm


def _wkv_step(Z, ar, Tm, lak_v, r_bk, bk_d, zdec, v):
    C = Tm.shape[1]
    ar_z = _bmm(ar, Z)
    uv = jnp.concatenate([_bmm(Tm, ar_z[:, :C] + lak_v), v], axis=1)
    y = ar_z[:, C:] + _bmm(r_bk, uv)
    Z1 = Z * zdec + _bmm_tn(bk_d, uv)
    return jnp.concatenate([y[2 * p] + y[2 * p + 1] for p in range(RWKV_HEADS // 2)], axis=1), Z1


def _split3(x):
    hi = x.astype(bf16)
    mid = (x - hi.astype(f32)).astype(bf16)
    lo = (x - hi.astype(f32) - mid.astype(f32)).astype(bf16)
    return hi, mid, lo


@jax.custom_vjp
def _spread(x, sel):
    return sum(jnp.dot(t, sel, preferred_element_type=f32) for t in _split3(x))


def _spread_bwd(sel, g):
    dn = (((1,), (1,)), ((), ()))
    return sum(lax.dot_general(t, sel, dn, preferred_element_type=f32) for t in _split3(g)), None


_spread.defvjp(lambda x, sel: (_spread(x, sel), sel), _spread_bwd)


def _gdn_prep(q, k, v, gbeta, inv=None):
    heads = lambda t: jnp.concatenate([t[None, :, GDN_HD * h:GDN_HD * (h + 1)] for h in range(GDN_HEADS)], axis=0)
    src = lax.broadcasted_iota(jnp.int32, (W_AB, 2 * GDN_W), 0)
    dst = lax.broadcasted_iota(jnp.int32, (W_AB, 2 * GDN_W), 1) // GDN_HD
    spread = _spread(gbeta, (src == dst).astype(bf16))
    q, k, v, g, beta = heads(q), heads(k), heads(v), heads(spread[:, :GDN_W]), heads(spread[:, GDN_W:])
    H, C, D = q.shape
    row, col = _masks(H, C)
    incl, strict = row >= col, row > col
    gc = _cumsum(g)
    diff = gc - jnp.swapaxes(gc, 1, 2)
    decay = jnp.where(incl, jnp.exp(jnp.where(incl, diff, 0.0)), 0.0)
    gl = jnp.sum(g, axis=1, keepdims=True)
    kb, vb = k * beta, v * beta
    gram = _bmm_nt(jnp.concatenate([kb, q], axis=1), k)
    L = jnp.where(strict, gram[:, :C] * decay, 0.0)
    attn = jnp.where(incl, gram[:, C:] * decay, 0.0)
    egc = jnp.exp(gc)
    Tm = _tri_inv(L) if inv is None else _tri_inv_known(L, inv)
    t_vk = _bmm(Tm, jnp.concatenate([vb, kb * egc], axis=2))
    return (t_vk[:, :, :D], jnp.concatenate([t_vk[:, :, D:], q * egc], axis=1), attn, k * jnp.exp(gl - gc), jnp.exp(gl)), Tm


def _gdn_step(S, u, wq, attn, ke, sdec):
    C = u.shape[1]
    wq_s = _bmm(wq, S)
    v_new = u - wq_s[:, :C]
    o = wq_s[:, C:] + _bmm(attn, v_new)
    S1 = S * sdec + _bmm_tn(ke, v_new)
    return jnp.concatenate([o[h] for h in range(GDN_HEADS)], axis=1), S1


def _scan_fwd(name, fns, ins, C, H, dh, w_out, per_step, side=None):
    prep, step = fns
    T = ins[0].shape[0]
    n_in = len(ins)
    blk = C * per_step
    nblk = T // blk
    n_side = 0 if side is None else len(side[0])

    def body(*refs):
        in_refs, refs = refs[:n_in], refs[n_in:]
        side_in, refs = refs[:n_side], refs[n_side:]
        y_ref, zs_ref, inv_ref, refs = refs[0], refs[1], refs[2], refs[3:]
        side_out, refs = refs[:n_side], refs[n_side:]
        z_scr = refs[0]
        if side is not None:
            start, finish = _xy_copies(side_in, side_out, refs[1:], side[1])
            pl.when(pl.program_id(0) == 0)(start)

        @pl.when(pl.program_id(0) == 0)
        def _():
            z_scr[...] = jnp.zeros_like(z_scr)

        rows = [slice(C * j, C * (j + 1)) for j in range(per_step)]
        prepped = [prep(*[r[rw, :] for r in in_refs]) for rw in rows]
        Z = z_scr[...]
        for j, rw in enumerate(rows):
            zs_ref[j] = Z
            inv_ref[j] = prepped[j][1]
            y, Z = step(Z, *prepped[j][0])
            y_ref[rw, :] = y
        z_scr[...] = Z
        if side is not None:
            pl.when(pl.program_id(0) == nblk - 1)(finish)

    side_bufs = [] if side is None else list(side[0])
    any_spec = pl.BlockSpec(memory_space=pl.ANY)
    return pl.pallas_call(
        body, grid=(nblk,),
        in_specs=[pl.BlockSpec((blk, a.shape[1]), lambda i: (i, 0)) for a in ins] + [any_spec] * n_side,
        out_specs=[pl.BlockSpec((blk, w_out), lambda i: (i, 0)), pl.BlockSpec((per_step, H, dh, dh), lambda i: (i, 0, 0, 0)),
                   pl.BlockSpec((per_step, H, C, C), lambda i: (i, 0, 0, 0))] + [any_spec] * n_side,
        out_shape=[jax.ShapeDtypeStruct((T, w_out), f32), jax.ShapeDtypeStruct((T // C, H, dh, dh), f32),
                   jax.ShapeDtypeStruct((T // C, H, C, C), f32)]
        + (_xy_out_shapes(side_bufs, side[1]) if side is not None else []),
        scratch_shapes=[pltpu.VMEM((H, dh, dh), f32)] + (_xy_sems(n_side, side[1]) if side is not None else []), name=name,
        compiler_params=_params("arbitrary"))(*ins, *side_bufs)


def _scan_bwd(name, fns, ins, dy, zs, invs, C, per_step, side=None):
    prep, step = fns
    T = ins[0].shape[0]
    _, H, dh, _ = zs.shape
    n_in = len(ins)
    blk = C * per_step
    nblk = T // blk
    n_side = 0 if side is None else len(side[0])

    def body(*refs):
        in_refs, dy_ref, zs_ref, inv_ref, refs = refs[:n_in], refs[n_in], refs[n_in + 1], refs[n_in + 2], refs[n_in + 3:]
        side_in, refs = refs[:n_side], refs[n_side:]
        out_refs, refs = refs[:n_in], refs[n_in:]
        side_out, refs = refs[:n_side], refs[n_side:]
        dz_scr = refs[0]
        if side is not None:
            start, finish = _xy_copies(side_in, side_out, refs[1:], side[1])
            pl.when(pl.program_id(0) == 0)(start)

        @pl.when(pl.program_id(0) == 0)
        def _():
            dz_scr[...] = jnp.zeros_like(dz_scr)

        rows = [slice(C * j, C * (j + 1)) for j in range(per_step)]
        prepped = [jax.vjp(lambda *a, j=j: prep(*a, inv=inv_ref[j])[0], *[r[rw, :] for r in in_refs])
                   for j, rw in enumerate(rows)]
        d_prepped = [None] * per_step
        dZ = dz_scr[...]
        for j in reversed(range(per_step)):
            _, pull = jax.vjp(step, zs_ref[j], *prepped[j][0])
            dZ, *d_prepped[j] = pull((dy_ref[rows[j], :], dZ))
        dz_scr[...] = dZ
        for j, rw in enumerate(rows):
            for o_ref, gval in zip(out_refs, prepped[j][1](tuple(d_prepped[j]))):
                o_ref[rw, :] = gval
        if side is not None:
            pl.when(pl.program_id(0) == nblk - 1)(finish)

    side_bufs = [] if side is None else list(side[0])
    any_spec = pl.BlockSpec(memory_space=pl.ANY)
    rev = lambda i: (nblk - 1 - i, 0)
    return pl.pallas_call(
        body, grid=(nblk,),
        in_specs=[pl.BlockSpec((blk, a.shape[1]), rev) for a in ins]
        + [pl.BlockSpec((blk, dy.shape[1]), rev), pl.BlockSpec((per_step, H, dh, dh), lambda i: (nblk - 1 - i, 0, 0, 0)),
           pl.BlockSpec((per_step, H, C, C), lambda i: (nblk - 1 - i, 0, 0, 0))] + [any_spec] * n_side,
        out_specs=[pl.BlockSpec((blk, a.shape[1]), rev) for a in ins] + [any_spec] * n_side,
        out_shape=[jax.ShapeDtypeStruct(a.shape, f32) for a in ins]
        + (_xy_out_shapes(side_bufs, side[1]) if side is not None else []),
        scratch_shapes=[pltpu.VMEM((H, dh, dh), f32)] + (_xy_sems(n_side, side[1]) if side is not None else []), name=name,
        compiler_params=_params("arbitrary"))(*ins, dy, zs, invs, *side_bufs)


def _residual_mm(name, a, b, res, tail, row_extras, consts, row_out, acc_out, tm, head=None):
    K, N = b.shape
    h_rows, h_consts = ([], []) if head is None else (list(head[1]), list(head[2]))
    lhs = h_rows + h_consts if head is not None else [a]
    M = lhs[0].shape[0]
    row_extras = [_row_of(e) for e in row_extras]
    n_lhs, n_res = len(lhs), 0 if res is None else 1
    ne, nc, nr = len(row_extras), len(consts), len(row_out)

    def body(*refs):
        lhs_refs, refs = refs[:n_lhs], refs[n_lhs:]
        b_ref, refs = refs[0], refs[1:]
        res_ref, refs = (refs[0], refs[1:]) if res is not None else (None, refs)
        extra_refs, const_refs, out_refs = refs[:ne], refs[ne:ne + nc], refs[ne + nc:]
        if head is not None:
            left = head[0](*[r[...] for r in lhs_refs])[0].astype(bf16)
            out_refs[0][...] = left
            out_refs = out_refs[1:]
        else:
            left = lhs_refs[0][...].astype(bf16)
        tile = jnp.dot(left, b_ref[...].astype(bf16), preferred_element_type=f32)
        if res is not None:
            tile = res_ref[...] + tile
        outs = tail(tile, *[r[...] for r in extra_refs], *[c[...] for c in const_refs])
        for o_ref, o in zip(out_refs[:nr], outs[:nr]):
            o_ref[...] = o.astype(o_ref.dtype)

        @pl.when(pl.program_id(0) == 0)
        def _():
            for o_ref in out_refs[nr:]:
                o_ref[...] = jnp.zeros_like(o_ref)

        for o_ref, o in zip(out_refs[nr:], outs[nr:]):
            o_ref[...] += o

    lhs_specs = ([_row_spec(tm, r.shape[1]) for r in h_rows] + [_full_spec(c.shape) for c in h_consts]
                 if head is not None else [_row_spec(tm, K)])
    head_out = [(K, bf16)] if head is not None else []
    outs = pl.pallas_call(
        body, grid=(M // tm,),
        in_specs=lhs_specs + [_full_spec(b.shape)] + ([_row_spec(tm, N)] if res is not None else [])
        + [_row_spec(tm, e[1], e[2]) for e in row_extras] + [_full_spec(c.shape) for c in consts],
        out_specs=[_row_spec(tm, w) for w, _ in head_out + list(row_out)] + [_full_spec(sh) for sh in acc_out],
        out_shape=[jax.ShapeDtypeStruct((M, w), d) for w, d in head_out + list(row_out)]
        + [jax.ShapeDtypeStruct(sh, f32) for sh in acc_out],
        name=name, compiler_params=_params("arbitrary"))(
            *lhs, b, *([res] if res is not None else []), *[e[0] for e in row_extras], *consts)
    return outs


def _pull_tail(fn):
    def tail(cot, *args):
        _, vjp = jax.vjp(fn, *args)
        return vjp((cot,))
    return tail


def _norm_tail(x1, g):
    return x1, _rms(x1, g)


def _loss_tail(x2, tgt, g):
    l, vjp = jax.vjp(lambda xv, gv: _loss_rows(xv, tgt, gv), x2, g)
    dx, dg = vjp(jnp.ones_like(l))
    return dx, dg, jnp.zeros((1, 128), f32) + jnp.sum(l)


def _local_step(x, tgt, W, late=None):
    row = lambda a: a.reshape(1, -1)
    wp = W['w_in_pad']
    w_rwkv, w_qkv, w_z = wp[:, :OFF_QKV], wp[:, OFF_QKV:OFF_Z], wp[:, OFF_Z:OFF_GATES]
    w_gates, w_ab = wp[:, OFF_GATES:OFF_AB], wp[:, OFF_AB:]
    mu = row(W['rwkv_mu'])
    mixw = jnp.concatenate([mu, 1.0 - mu], axis=0)
    zpad = jnp.zeros((64, RWKV_W), f32)
    w2p = jnp.concatenate([W['rwkv_w2'], zpad], axis=0)
    a2p = jnp.concatenate([zpad, W['rwkv_a2']], axis=0)
    rw_consts = [row(W['rwkv_w0']), w2p, row(W['rwkv_a0']), a2p, W['rwkv_g2'], row(W['rwkv_k_k']), row(W['rwkv_k_a'])]
    post_consts = [row(W['rwkv_ln_w']), row(W['rwkv_ln_b']), row(W['rwkv_r_k'])]
    pad4 = lambda a: jnp.pad(row(a), ((0, 0), (0, W_AB - GDN_HEADS)))
    gd_consts = [pad4(W['gdn_a_log']), pad4(W['gdn_dt_bias'])]
    nw_t = jnp.tile(row(W['gdn_norm_w']), (1, GDN_HEADS))
    g1, g2n, gf = row(W['norm1_g']), row(W['norm2_g']), row(W['final_g'])

    (u,) = _pw_fwd("norm1", _rms_fn, [x], [g1], [D_MODEL], 512, out_dtype=bf16)
    p_rwkv = _mm(u, w_rwkv, 'nn', "in_rwkv")
    qkv_raw = _mm(u, w_qkv, 'nn', "in_qkv")
    z = _mm(u, w_z, 'nn', "in_z")
    gates = _mm(u, w_gates, 'nn', "in_gates")
    ab = _mm(u, w_ab, 'nn', "in_ab")

    r, lw, k2, v, a_, b_, g = _pw_fwd("rwkv_prep", _rwkv_prep_fn, [p_rwkv], rw_consts, [RWKV_W] * 7, 256, conv_w=mixw)
    wkv_in = [r, lw, k2, v, a_, b_]
    y, zs_wkv, inv_wkv, *gathered = _scan_fwd("wkv_fwd", (_wkv_prep, _wkv_step), wkv_in, WKV_CHUNK, RWKV_HEADS, 2 * RWKV_HD, RWKV_W, WKV_PER_STEP,
                                     side=None if late is None else (late['shards'][0], False))
    if late is not None:
        W = dict(W, **late['assemble'](0, gathered))
    ya_in, ya = _residual_mm("rwkv_proj", None, W['rwkv_proj'], None, lambda t: (t,), [], [], [(D_MODEL, f32)], [], 512,
                             head=(_rwkv_post_fn, [y, r, k2, v, g], post_consts))

    lanes = lambda off: slice(off, off + STRIP)
    gd_groups = [[lanes(GDN_HD * h), lanes(GDN_W + GDN_HD * h), lanes(2 * GDN_W + GDN_HD * h)] for h in range(GDN_HEADS)]
    gq, gk, gv = _group_fwd("gdn_prep", _gdn_prep_fn, qkv_raw, W['gdn_conv_w'], [], gd_groups, [], 3, 256)
    (gbeta,) = _pw_fwd("gdn_gate", _gdn_gate_fn, [ab], gd_consts, [W_AB], 512)
    gdn_in = [gq, gk, gv, gbeta]
    o, zs_gdn, inv_gdn, *gathered = _scan_fwd("gdn_fwd", (_gdn_prep, _gdn_step), gdn_in, GDN_CHUNK, GDN_HEADS, GDN_HD, GDN_W, GDN_PER_STEP,
                                     side=None if late is None else (late['shards'][1], False))
    if late is not None:
        W = dict(W, **late['assemble'](1, gathered))
    ga, gb = _cols(gates, D_MODEL, 0), _cols(gates, D_MODEL, 1)
    yb_in, yb, mixed = _residual_mm("gdn_proj", None, W['gdn_proj'], None, lambda t, a_, b_, c_: (t,) + _mix_fn(a_, b_, c_, t),
                                    [ga, gb, ya], [], [(D_MODEL, f32), (D_MODEL, bf16)], [], 512,
                                    head=(_gdn_post_fn, [o, z], [nw_t]))

    x1, u2 = _residual_mm("w_out", mixed, W['w_out'], x, _norm_tail, [], [g2n], [(D_MODEL, f32), (D_MODEL, bf16)], [], 512)
    h = _mm(u2, W['ffn_up'], 'nn', "ffn_up")
    act = _ffn_act_fwd(h, W['ffn_conv_w'], 256)

    G = {}
    slab_out = None if late is None else N_POS
    dx2, dgf, loss = _residual_mm("ffn_down", act, W['ffn_down'], x1, _loss_tail, [tgt], [gf], [(D_MODEL, f32)],
                                  [gf.shape, (1, 128)], 512)
    G['final_g'] = dgf
    dact = _mm(dx2, W['ffn_down'], 'nt', "d_act")
    G['ffn_down'] = _mm(act, dx2, 'tn', "g_ffn_down", out_dtype=bf16)
    dh, G['ffn_conv_w'] = _ffn_act_bwd(h, dact, W['ffn_conv_w'], 128)
    du2 = _mm(dh, W['ffn_up'], 'nt', "d_u2")
    G['ffn_up'] = _mm(u2, dh, 'tn', "g_ffn_up", out_dtype=bf16, col_slabs=slab_out)
    (dx1,), (G['norm2_g'],) = _pw_bwd("norm2_bwd", _rms_fn, [x1], [g2n], [(du2,)], 512, add_to_first=dx2)
    G['w_out'] = _mm(mixed, dx1, 'tn', "g_w_out", out_dtype=bf16)
    dga, dgb, dya, dyb = _residual_mm("d_mixed", dx1, W['w_out'].T, None, _pull_tail(_mix_fn), [ga, gb, ya, yb], [],
                                      [(D_MODEL, bf16)] * 4, [], 512)
    G['rwkv_proj'] = _mm(ya_in, dya, 'tn', "g_rwkv_proj", out_dtype=bf16, col_slabs=slab_out)
    G['gdn_proj'] = _mm(yb_in, dyb, 'tn', "g_gdn_proj", out_dtype=bf16, col_slabs=slab_out)

    do, dz, dnw_t = _residual_mm("d_yb_in", dyb, W['gdn_proj'].T, None, _pull_tail(_gdn_post_fn), [o, z], [nw_t],
                                 [(GDN_W, f32), (GDN_W, bf16)], [nw_t.shape], 512)
    G['gdn_norm_w'] = dnw_t.reshape(GDN_HEADS, GDN_HD).sum(axis=0)
    dgq, dgk, dgv, dgbeta, *arrived_b = _scan_bwd("gdn_bwd", (_gdn_prep, _gdn_step), gdn_in, do, zs_gdn, inv_gdn, GDN_CHUNK,
                                                  GDN_PER_STEP, side=None if late is None else (late['slabs'](G, 1), True))
    dqkv_raw, G['gdn_conv_w'], _ = _group_bwd("gdn_prep_bwd", _gdn_prep_fn, qkv_raw, W['gdn_conv_w'], [], gd_groups, [],
                                              [(dgq,), (dgk,), (dgv,)], 128)
    (dab,), (dal_p, ddt_p) = _pw_bwd("gdn_gate_bwd", _gdn_gate_fn, [ab], gd_consts, [(dgbeta,)], 512, row_dtypes=[bf16])
    G['gdn_a_log'], G['gdn_dt_bias'] = dal_p[0, :GDN_HEADS], ddt_p[0, :GDN_HEADS]

    dy, dr1, dk21, dv1, dg_, G['rwkv_ln_w'], G['rwkv_ln_b'], G['rwkv_r_k'] = _residual_mm(
        "d_ya_in", dya, W['rwkv_proj'].T, None, _pull_tail(_rwkv_post_fn), [y, r, k2, v, g], post_consts,
        [(RWKV_W, f32)] * 5, [c.shape for c in post_consts], 512)
    dr2, dlw, dk22, dv2, da_, db_, *arrived_a = _scan_bwd(
        "wkv_bwd", (_wkv_prep, _wkv_step), wkv_in, dy, zs_wkv, inv_wkv, WKV_CHUNK, WKV_PER_STEP,
        side=None if late is None else (late['slabs'](G, 0), True))
    G['_arrived'] = (arrived_a, arrived_b)
    (dp_rwkv,), dmixw, rw_grads = _pw_conv_bwd(
        "rwkv_prep_bwd", _rwkv_prep_fn, [p_rwkv], rw_consts,
        [(dr1, dr2), (dlw,), (dk21, dk22), (dv1, dv2), (da_,), (db_,), (dg_,)], mixw, 256, row_dtypes=[bf16])
    G['rwkv_w0'], dw2p, G['rwkv_a0'], da2p, G['rwkv_g2'], G['rwkv_k_k'], G['rwkv_k_a'] = rw_grads
    G['rwkv_w2'], G['rwkv_a2'] = dw2p[:64], da2p[64:]
    G['rwkv_mu'] = dmixw[0] - dmixw[1]

    dps = [dp_rwkv, dqkv_raw, dz, dga, dgb, dab]
    offs = [0, OFF_QKV, OFF_Z, OFF_GATES, OFF_GATES + D_MODEL, OFF_AB]
    G['w_in_pad'] = list(zip(offs, _mm_tn_parts(u, dps[:2], "g_w_in_rwkv_qkv") + _mm_tn_parts(u, dps[2:], "g_w_in_rest")))
    pairs = [(dp_rwkv, w_rwkv, 0), (dqkv_raw, w_qkv, 0), (dz, w_z, 0), (dga, w_gates, 0), (dgb, w_gates, 1), (dab, w_ab, 0)]
    if late is None:
        du = _mm_nt_parts(pairs, "d_u")
    else:
        du, *G['_arrived_w_in'] = _mm_nt_parts(pairs, "d_u", side=(late['w_in_slabs'](G), True))
    (dx,), (G['norm1_g'],) = _pw_bwd("norm1_bwd", _rms_fn, [x], [g1], [(du,)], 512, add_to_first=dx1)
    return loss, dx, G


IN_WIDTH = OFF_AB + 8
PAD_ORDER = ((0, OFF_GATES), (OFF_GATES + 8, IN_WIDTH), (OFF_GATES, OFF_GATES + 8))


def _pad_w_in_shards(shards):
    width = shards[0].shape[1]
    parts = []
    for a, b in PAD_ORDER:
        for j, sh in enumerate(shards):
            lo, hi = max(a, j * width), min(b, (j + 1) * width)
            if lo < hi:
                parts.append(sh[:, lo - j * width:hi - j * width])
    return jnp.concatenate(parts + [jnp.zeros((shards[0].shape[0], W_AB - 8), shards[0].dtype)], axis=1)


def _padded_cols(sections, s, e):
    pieces = [arr[:, max(s, o) - o:min(e, o + arr.shape[1]) - o] for o, arr in sections if max(s, o) < min(e, o + arr.shape[1])]
    return pieces[0] if len(pieces) == 1 else jnp.concatenate(pieces, axis=1)


def _unpad_cols(sections, lo, hi):
    parts, off = [], 0
    for a, b in PAD_ORDER:
        l, h = max(a, lo), min(b, hi)
        if l < h:
            parts.append((l, _padded_cols(sections, off + l - a, off + h - a)))
        off += b - a
    parts.sort(key=lambda t: t[0])
    return parts[0][1] if len(parts) == 1 else jnp.concatenate([p for _, p in parts], axis=1)


BIG = ('w_in', 'rwkv_proj', 'gdn_proj', 'w_out', 'ffn_up', 'ffn_down')
SMALL_SHARDED = ('rwkv_w2', 'rwkv_a2', 'rwkv_g2', 'gdn_conv_w', 'ffn_conv_w')


def _rows128(shape):
    n = 1
    for d in shape:
        n *= d
    return -(-n // LANES)


def _pack128(arrays):
    parts = []
    for a in arrays:
        flat = a.reshape(-1)
        rows = _rows128(a.shape)
        parts.append(jnp.pad(flat, (0, rows * LANES - flat.shape[0])).reshape(rows, LANES))
    buf = jnp.concatenate(parts, axis=0)
    return jnp.pad(buf, ((0, -buf.shape[0] % HALO), (0, 0)))


def _unpack128(buf, shapes):
    out, off = [], 0
    for s in shapes:
        rows, n = _rows128(s), 1
        for d in s:
            n *= d
        out.append(buf[off:off + rows].reshape(-1)[:n].reshape(s))
        off += rows
    return out


def _param_tile(r, c):
    best = None
    for d in range(2 * HALO, r + 1, 2 * HALO):
        if r % d == 0 and d * c * 4 <= TILE_BYTES:
            best = d
    if best is not None or r * c * 4 <= TILE_BYTES:
        return (best if best is not None else r), c
    return r, 128


def _xy_exchange(name, bufs, scatter):
    n = len(bufs)

    def body(*refs):
        start, finish = _xy_copies(refs[:n], refs[n:2 * n], refs[2 * n:], scatter)
        start()
        finish()

    return pl.pallas_call(
        body, in_specs=[pl.BlockSpec(memory_space=pl.ANY)] * n, out_specs=[pl.BlockSpec(memory_space=pl.ANY)] * n,
        out_shape=_xy_out_shapes(bufs, scatter), scratch_shapes=_xy_sems(n, scatter), name=name)(*bufs)


def _sibling_exchange(name, bufs):
    n = len(bufs)

    def body(*refs):
        in_refs, out_refs, send_sems, recv_sems = refs[:n], refs[n:2 * n], refs[2 * n], refs[2 * n + 1]
        x, y, c = lax.axis_index("x"), lax.axis_index("y"), lax.axis_index("c")
        copies = [pltpu.make_async_remote_copy(
            src_ref=in_refs[a], dst_ref=out_refs[a], send_sem=send_sems.at[a], recv_sem=recv_sems.at[a],
            device_id=(x, y, 1 - c), device_id_type=pl.DeviceIdType.MESH) for a in range(n)]
        for cp in copies:
            cp.start()
        for cp in copies:
            cp.wait()

    return pl.pallas_call(
        body, in_specs=[pl.BlockSpec(memory_space=pl.ANY)] * n, out_specs=[pl.BlockSpec(memory_space=pl.ANY)] * n,
        out_shape=[jax.ShapeDtypeStruct(b.shape, b.dtype) for b in bufs],
        scratch_shapes=[pltpu.SemaphoreType.DMA((n,)), pltpu.SemaphoreType.DMA((n,))], name=name)(*bufs)


def _sum_slots(name, buf):
    _, R, L = buf.shape
    tr, tc = _param_tile(R, L)

    def body(b_ref, o_ref):
        part = lambda s: b_ref[s].astype(f32)
        o_ref[...] = ((part(0) + part(1)) + part(2)) + part(3)

    return pl.pallas_call(
        body, grid=(R // tr, L // tc),
        in_specs=[pl.BlockSpec((N_POS, tr, tc), lambda i, j: (0, i, j))],
        out_specs=pl.BlockSpec((tr, tc), lambda i, j: (i, j)),
        out_shape=jax.ShapeDtypeStruct((R, L), f32), name=name,
        compiler_params=_params("parallel", "parallel"))(buf)


def _adamw(name, w, ga, gb, m, v):
    R, L = w.shape
    tr, tc = _param_tile(R, L)
    c1 = 1.0 / (1.0 - ADAM_B1 ** ADAM_STEP)
    c2 = 1.0 / (1.0 - ADAM_B2 ** ADAM_STEP)

    def body(w_ref, ga_ref, gb_ref, m_ref, v_ref, g_out, d_out, m_out, v_out):
        g = ga_ref[...] + gb_ref[...]
        m_new = ADAM_B1 * m_ref[...] + (1.0 - ADAM_B1) * g
        v_new = ADAM_B2 * v_ref[...] + (1.0 - ADAM_B2) * (g * g)
        g_out[...] = g
        m_out[...] = m_new
        v_out[...] = v_new
        d_out[...] = -ADAM_LR * ((m_new * c1) / (jnp.sqrt(v_new * c2) + ADAM_EPS) + ADAM_WD * w_ref[...])

    spec = pl.BlockSpec((tr, tc), lambda i, j: (i, j))
    return pl.pallas_call(
        body, grid=(R // tr, L // tc), in_specs=[spec] * 5, out_specs=[spec] * 4,
        out_shape=[jax.ShapeDtypeStruct((R, L), f32)] * 4, name=name,
        compiler_params=_params("parallel", "parallel"))(w, ga, gb, m, v)


def _step(x, loss_target, P, M, V):
    shapes = {n: tuple(P[n].shape) for n in WEIGHTS}
    sh_shapes = [shapes[n] for n in SMALL_SHARDED]
    packed = SMALL_SHARDED + SMALL

    def whole(n, g):
        return g.reshape(-1, g.shape[2]) if n in ROW_SHARDED else jnp.concatenate([g[j] for j in range(N_POS)], axis=1)

    def slabs(G, n, dtype=f32):
        r, c = shapes[n]
        full = G[n].astype(dtype)
        if full.ndim == 3:
            return full
        return full.reshape(N_POS, r, c) if n in ROW_SHARDED else full.reshape(r, N_POS, c).transpose(1, 0, 2)

    g_w_in, g_small = _xy_exchange("gather_w_in", [P['w_in'].astype(bf16), _pack128([P[n] for n in SMALL_SHARDED])],
                                   scatter=False)
    W = {n: P[n] for n in SMALL}
    W['w_in_pad'] = _pad_w_in_shards([g_w_in[j] for j in range(N_POS)])
    per_pos = [_unpack128(g_small[j], sh_shapes) for j in range(N_POS)]
    for q, n in enumerate(SMALL_SHARDED):
        W[n] = jnp.concatenate([per_pos[j][q] for j in range(N_POS)], axis=1)
    groups = (('rwkv_proj', 'gdn_proj', 'ffn_up'), ('w_out', 'ffn_down'))
    late = dict(shards=[[P[n].astype(bf16) for n in grp] for grp in groups],
                assemble=lambda q, gathered: {n: whole(n, g) for n, g in zip(groups[q], gathered)},
                slabs=lambda G, q: [slabs(G, n, bf16) for n in groups[q]],
                w_in_slabs=lambda G: [jnp.stack([_unpad_cols(G['w_in_pad'], j * shapes['w_in'][1], (j + 1) * shapes['w_in'][1])
                                                 for j in range(N_POS)])])

    loss_rows, dx, G = _local_step(x, loss_target, W, late)
    arrived = {n: a for grp, got in zip(groups, G.pop('_arrived')) for n, a in zip(grp, got)}
    (arrived_w_in,) = G.pop('_arrived_w_in')
    G.pop('w_in_pad')

    small_slabs = jnp.stack([_pack128([slabs(G, n)[j] for n in SMALL_SHARDED] + [G[n] for n in SMALL]) for j in range(N_POS)])
    (arrived_small,) = _xy_exchange("scatter_small", [small_slabs], scatter=True)
    contributions = [arrived_w_in] + [arrived[n] for n in BIG[1:]] + [arrived_small]
    tags = list(BIG) + ['small']
    plane = [_sum_slots("sum_" + t, cbuf) for t, cbuf in zip(tags, contributions)]
    sibling = _sibling_exchange("sibling_grads", plane)

    out = {}
    names4 = ('grad', 'delta', 'new_m', 'new_v')
    for q, n in enumerate(BIG):
        tr = (lambda t: t.T) if n == 'w_in' else (lambda t: t)
        for tag, t in zip(names4, _adamw("adamw_" + n, tr(P[n]), tr(plane[q]), tr(sibling[q]), tr(M[n]), tr(V[n]))):
            out[tag + '_' + n] = tr(t)
    small_out = _adamw("adamw_small", _pack128([P[n] for n in packed]), plane[-1], sibling[-1],
                       _pack128([M[n] for n in packed]), _pack128([V[n] for n in packed]))
    for tag, buf in zip(names4, small_out):
        for n, t in zip(packed, _unpack128(buf, [shapes[n] for n in packed])):
            out[tag + '_' + n] = t
    loss = lax.psum(loss_rows[0, 0], ("x", "y", "c"))
    return loss, dx, out


def kernel(x, norm1_g, w_in, rwkv_mu, rwkv_w0, rwkv_w2, rwkv_a0, rwkv_a2, rwkv_g2, rwkv_k_k, rwkv_k_a, rwkv_r_k, rwkv_ln_w, rwkv_ln_b, rwkv_proj, gdn_conv_w, gdn_a_log, gdn_dt_bias, gdn_norm_w, gdn_proj, w_out, norm2_g, ffn_up, ffn_conv_w, ffn_down, final_g, loss_target, m_norm1_g, m_w_in, m_rwkv_mu, m_rwkv_w0, m_rwkv_w2, m_rwkv_a0, m_rwkv_a2, m_rwkv_g2, m_rwkv_k_k, m_rwkv_k_a, m_rwkv_r_k, m_rwkv_ln_w, m_rwkv_ln_b, m_rwkv_proj, m_gdn_conv_w, m_gdn_a_log, m_gdn_dt_bias, m_gdn_norm_w, m_gdn_proj, m_w_out, m_norm2_g, m_ffn_up, m_ffn_conv_w, m_ffn_down, m_final_g, v_norm1_g, v_w_in, v_rwkv_mu, v_rwkv_w0, v_rwkv_w2, v_rwkv_a0, v_rwkv_a2, v_rwkv_g2, v_rwkv_k_k, v_rwkv_k_a, v_rwkv_r_k, v_rwkv_ln_w, v_rwkv_ln_b, v_rwkv_proj, v_gdn_conv_w, v_gdn_a_log, v_gdn_dt_bias, v_gdn_norm_w, v_gdn_proj, v_w_out, v_norm2_g, v_ffn_up, v_ffn_conv_w, v_ffn_down, v_final_g):
    weights = (norm1_g, w_in, rwkv_mu, rwkv_w0, rwkv_w2, rwkv_a0, rwkv_a2, rwkv_g2, rwkv_k_k, rwkv_k_a, rwkv_r_k, rwkv_ln_w,
               rwkv_ln_b, rwkv_proj, gdn_conv_w, gdn_a_log, gdn_dt_bias, gdn_norm_w, gdn_proj, w_out, norm2_g, ffn_up,
               ffn_conv_w, ffn_down, final_g)
    m_in = (m_norm1_g, m_w_in, m_rwkv_mu, m_rwkv_w0, m_rwkv_w2, m_rwkv_a0, m_rwkv_a2, m_rwkv_g2, m_rwkv_k_k, m_rwkv_k_a,
            m_rwkv_r_k, m_rwkv_ln_w, m_rwkv_ln_b, m_rwkv_proj, m_gdn_conv_w, m_gdn_a_log, m_gdn_dt_bias, m_gdn_norm_w,
            m_gdn_proj, m_w_out, m_norm2_g, m_ffn_up, m_ffn_conv_w, m_ffn_down, m_final_g)
    v_in = (v_norm1_g, v_w_in, v_rwkv_mu, v_rwkv_w0, v_rwkv_w2, v_rwkv_a0, v_rwkv_a2, v_rwkv_g2, v_rwkv_k_k, v_rwkv_k_a,
            v_rwkv_r_k, v_rwkv_ln_w, v_rwkv_ln_b, v_rwkv_proj, v_gdn_conv_w, v_gdn_a_log, v_gdn_dt_bias, v_gdn_norm_w,
            v_gdn_proj, v_w_out, v_norm2_g, v_ffn_up, v_ffn_conv_w, v_ffn_down, v_final_g)
    drop = lambda n, a: a if n == 'final_g' else a[0]
    P = {n: drop(n, a) for n, a in zip(WEIGHTS, weights)}
    M = {n: drop(n, a) for n, a in zip(WEIGHTS, m_in)}
    V = {n: drop(n, a) for n, a in zip(WEIGHTS, v_in)}
    loss, dx, out = _step(x[0], loss_target[0], P, M, V)
    lift = lambda n, a: a if n == 'final_g' else a[None]
    res = [loss, dx[None]]
    for tag in ('grad', 'delta', 'new_m', 'new_v'):
        res += [lift(n, out[tag + '_' + n]) for n in WEIGHTS]
    return tuple(res)
```

```python
import functools

import jax
import jax.numpy as jnp
from jax import lax
from jax.experimental import pallas as pl
from jax.experimental.pallas import tpu as pltpu

f32 = jnp.float32
bf16 = jnp.bfloat16

D_MODEL = 1024
RWKV_HEADS, RWKV_HD, RWKV_W = 8, 64, 512
GDN_HEADS, GDN_HD, GDN_W = 4, 128, 512
NORM_EPS, L2_EPS, GN_EPS = 1e-6, 1e-6, 64e-5
W_AB = 256
OFF_QKV, OFF_Z, OFF_GATES, OFF_AB = 1792, 3328, 3840, 5888
W_IN_PAD = OFF_AB + W_AB
WKV_CHUNK, WKV_PER_STEP = 64, 4
GDN_CHUNK, GDN_PER_STEP = 128, 4
HALO = 8
LANES = 128
TILE_BYTES = 1 << 20
VMEM_LIMIT = 56 * 1024 * 1024

ADAM_LR, ADAM_B1, ADAM_B2, ADAM_EPS, ADAM_WD, ADAM_STEP = 0.001, 0.9, 0.999, 1e-08, 0.01, 10

ROW_SHARDED = ('w_out', 'ffn_down')
SMALL = ('norm1_g', 'rwkv_mu', 'rwkv_w0', 'rwkv_a0', 'rwkv_k_k', 'rwkv_k_a', 'rwkv_r_k', 'rwkv_ln_w', 'rwkv_ln_b',
         'gdn_a_log', 'gdn_dt_bias', 'gdn_norm_w', 'norm2_g', 'final_g')
WEIGHTS = ('norm1_g', 'w_in', 'rwkv_mu', 'rwkv_w0', 'rwkv_w2', 'rwkv_a0', 'rwkv_a2', 'rwkv_g2', 'rwkv_k_k', 'rwkv_k_a',
           'rwkv_r_k', 'rwkv_ln_w', 'rwkv_ln_b', 'rwkv_proj', 'gdn_conv_w', 'gdn_a_log', 'gdn_dt_bias', 'gdn_norm_w',
           'gdn_proj', 'w_out', 'norm2_g', 'ffn_up', 'ffn_conv_w', 'ffn_down', 'final_g')


def _params(*sem):
    return pltpu.CompilerParams(dimension_semantics=sem, vmem_limit_bytes=VMEM_LIMIT)


def _tile(n, limit):
    if n <= limit:
        return n
    best = None
    for d in range(128, limit + 1, 128):
        if n % d == 0:
            best = d
    if best is None:
        raise ValueError(f"no tile for {n} under {limit}")
    return best


MM_BLOCK_BYTES = 6 << 20
MM_MAX_COLS = 1536


def _mm(a, b, mode, name, add=None, out_dtype=f32, side=None, col_slabs=None):
    if mode == 'nn':
        (M, K), N = a.shape, b.shape[1]
    elif mode == 'nt':
        (M, K), N = a.shape, b.shape[0]
    else:
        (K, M), N = a.shape, b.shape[1]
    tm = _tile(M, 1408)
    tk = _tile(K, min(2816, MM_BLOCK_BYTES // (tm * a.dtype.itemsize)))
    tn = _tile(N, max(128, min(MM_BLOCK_BYTES // (tk * b.dtype.itemsize), MM_BLOCK_BYTES // (tm * 4), MM_MAX_COLS) // 128 * 128))
    if col_slabs is not None:
        tn = N // col_slabs
    nk = K // tk
    grid = (M // tm, N // tn, nk)
    dn = {'nn': (((1,), (0,)), ((), ())), 'nt': (((1,), (1,)), ((), ())), 'tn': (((0,), (0,)), ((), ()))}[mode]
    n_add = 0 if add is None else 1
    n_side = 0 if side is None else len(side[0])

    def body(a_ref, b_ref, *rest):
        add_ref = rest[0] if add is not None else None
        side_in, rest = rest[n_add:n_add + n_side], rest[n_add + n_side:]
        o_ref, side_out, rest = rest[0], rest[1:1 + n_side], rest[1 + n_side:]
        acc_ref, rest = (rest[0], rest[1:]) if nk > 1 else (None, rest)
        ids = [pl.program_id(d) for d in range(3)]
        if side is not None:
            start, finish = _xy_copies(side_in, side_out, rest, side[1])
            pl.when((ids[0] == 0) & (ids[1] == 0) & (ids[2] == 0))(start)
        acc = lax.dot_general(a_ref[...].astype(bf16), b_ref[...].astype(bf16), dn, preferred_element_type=f32)
        if nk == 1:
            o_ref[...] = (acc + add_ref[...] if add is not None else acc).astype(out_dtype)
        else:
            k = ids[2]

            @pl.when(k == 0)
            def _():
                acc_ref[...] = acc + add_ref[...] if add is not None else acc

            @pl.when(k > 0)
            def _():
                acc_ref[...] += acc

            @pl.when(k == nk - 1)
            def _():
                o_ref[...] = acc_ref[...].astype(out_dtype)
        if side is not None:
            pl.when((ids[0] == grid[0] - 1) & (ids[1] == grid[1] - 1) & (ids[2] == nk - 1))(finish)

    a_spec = (pl.BlockSpec((tk, tm), lambda i, j, k: (k, i)) if mode == 'tn'
              else pl.BlockSpec((tm, tk), lambda i, j, k: (i, k)))
    b_spec = (pl.BlockSpec((tn, tk), lambda i, j, k: (j, k)) if mode == 'nt'
              else pl.BlockSpec((tk, tn), lambda i, j, k: (k, j)))
    o_spec = pl.BlockSpec((tm, tn), lambda i, j, k: (i, j))
    o_shape = jax.ShapeDtypeStruct((M, N), out_dtype)
    if col_slabs is not None:
        o_spec = pl.BlockSpec((None, tm, tn), lambda i, j, k: (j, i, 0))
        o_shape = jax.ShapeDtypeStruct((col_slabs, M, tn), out_dtype)
    any_spec = pl.BlockSpec(memory_space=pl.ANY)
    side_bufs = [] if side is None else list(side[0])
    ins, specs = [a, b], [a_spec, b_spec]
    if add is not None:
        ins.append(add)
        specs.append(o_spec)
    outs = pl.pallas_call(
        body, grid=grid, in_specs=specs + [any_spec] * n_side, out_specs=[o_spec] + [any_spec] * n_side,
        out_shape=[o_shape] + (_xy_out_shapes(side_bufs, side[1]) if side is not None else []),
        scratch_shapes=([pltpu.VMEM((tm, tn), f32)] if nk > 1 else []) + (_xy_sems(n_side, side[1]) if side is not None else []),
        name=name,
        compiler_params=_params(*(("arbitrary",) * 3 if side is not None else ("parallel", "parallel", "arbitrary"))))(
            *ins, *side_bufs)
    return list(outs) if side is not None else outs[0]


PARTS_TILE = 512
PARTS_DEPTH = 1024


def _mm_nt_parts(pairs, name, side=None):
    n, M, N = len(pairs), pairs[0][0].shape[0], pairs[0][1].shape[0]
    tm = min(M, PARTS_TILE)
    steps = M // tm
    n_side = 0 if side is None else len(side[0])

    def body(*refs):
        a_refs, b_refs, side_in = refs[:n], refs[n:2 * n], refs[2 * n:2 * n + n_side]
        o_ref, side_out, sems = refs[2 * n + n_side], refs[2 * n + n_side + 1:2 * n + 2 * n_side + 1], refs[2 * n + 2 * n_side + 1:]
        i = pl.program_id(0)
        if side is not None:
            start, finish = _xy_copies(side_in, side_out, sems, side[1])
            pl.when(i == 0)(start)
        acc = None
        for a_ref, b_ref in zip(a_refs, b_refs):
            part = lax.dot_general(a_ref[...].astype(bf16), b_ref[...].astype(bf16), (((1,), (1,)), ((), ())),
                                   preferred_element_type=f32)
            acc = part if acc is None else acc + part
        o_ref[...] = acc
        if side is not None:
            pl.when(i == steps - 1)(finish)

    any_spec = pl.BlockSpec(memory_space=pl.ANY)
    side_bufs = [] if side is None else list(side[0])
    o_spec = pl.BlockSpec((tm, N), lambda i: (i, 0))
    outs = pl.pallas_call(
        body, grid=(steps,),
        in_specs=[pl.BlockSpec((tm, a.shape[1]), lambda i: (i, 0)) for a, _, _ in pairs]
        + [pl.BlockSpec((N, a.shape[1]), lambda i, col=col: (0, col)) for a, _, col in pairs] + [any_spec] * n_side,
        out_specs=[o_spec] + [any_spec] * n_side,
        out_shape=[jax.ShapeDtypeStruct((M, N), f32)] + (_xy_out_shapes(side_bufs, side[1]) if side is not None else []),
        scratch_shapes=_xy_sems(n_side, side[1]) if side is not None else [],
        name=name, compiler_params=_params("arbitrary" if side is not None else "parallel"))(
            *[a for a, _, _ in pairs], *[b for _, b, _ in pairs], *side_bufs)
    return list(outs) if side is not None else outs[0]


def _mm_tn_parts(a, parts, name):
    n, (K, M) = len(parts), a.shape
    tm, tk = min(M, PARTS_TILE), min(K, PARTS_DEPTH)
    nk = K // tk
    widths = [p.shape[1] for p in parts]
    offs = [sum(widths[:q]) for q in range(n)]

    def body(a_ref, *refs):
        p_refs, o_refs, acc_ref = refs[:n], refs[n:2 * n], refs[2 * n]
        k = pl.program_id(1)
        lhs = a_ref[...].astype(bf16)
        for p_ref, o_ref, off, w in zip(p_refs, o_refs, offs, widths):
            part = lax.dot_general(lhs, p_ref[...].astype(bf16), (((0,), (0,)), ((), ())), preferred_element_type=f32)
            if nk == 1:
                o_ref[...] = part.astype(bf16)
                continue
            cols = (slice(None), slice(off, off + w))

            @pl.when(k == 0)
            def _():
                acc_ref[cols] = part

            @pl.when(k > 0)
            def _():
                acc_ref[cols] += part

            @pl.when(k == nk - 1)
            def _():
                o_ref[...] = acc_ref[cols].astype(bf16)

    return pl.pallas_call(
        body, grid=(M // tm, nk),
        in_specs=[pl.BlockSpec((tk, tm), lambda i, k: (k, i))] + [pl.BlockSpec((tk, w), lambda i, k: (k, 0)) for w in widths],
        out_specs=[pl.BlockSpec((tm, w), lambda i, k: (i, 0)) for w in widths],
        out_shape=[jax.ShapeDtypeStruct((M, w), bf16) for w in widths],
        scratch_shapes=[pltpu.VMEM((tm, sum(widths)), f32)],
        name=name, compiler_params=_params("parallel", "arbitrary"))(a, *parts)


def _shift_down(cur, prev, s):
    if s == 0:
        return cur
    ext = jnp.concatenate([prev, cur], axis=0)
    return pltpu.roll(ext, s, 0)[HALO:]


def _shift_up(cur, nxt, s):
    if s == 0:
        return cur
    ext = jnp.concatenate([cur, nxt], axis=0)
    return pltpu.roll(ext, ext.shape[0] - s, 0)[:cur.shape[0]]


def _conv_apply(cur, prev, w_ref, shifted=None):
    taps = w_ref.shape[0]
    out = None
    for i in range(taps):
        s = taps - 1 - i
        term = (shifted[s] if shifted is not None else _shift_down(cur, prev, s)) * w_ref[pl.ds(i, 1), :]
        out = term if out is None else out + term
    return out


def _row_spec(tm, w, col=0):
    return pl.BlockSpec((tm, w), lambda i: (i, col))


def _cols(a, width, col):
    return (a, width, col)


def _row_of(r):
    return r if isinstance(r, tuple) else (r, r.shape[1], 0)


def _prev_spec(tm, w):
    return pl.BlockSpec((HALO, w), lambda i: (jnp.maximum(i * (tm // HALO) - 1, 0), 0))


def _next_spec(tm, w, T):
    return pl.BlockSpec((HALO, w), lambda i: (jnp.minimum((i + 1) * (tm // HALO), T // HALO - 1), 0))


def _full_spec(shape):
    return pl.BlockSpec(shape, lambda i: (0,) * len(shape))


def _pw_fwd(name, fn, rows, consts, out_widths, tm, conv_w=None, out_dtype=f32):
    T = _row_of(rows[0])[0].shape[0]
    nr, nc = len(rows), len(consts)

    def body(*refs):
        i = pl.program_id(0)
        vals = [r[...] for r in refs[:nr]]
        p = nr
        if conv_w is not None:
            prev = jnp.where(i > 0, refs[p][...], 0.0)
            vals[0] = _conv_apply(vals[0], prev, refs[p + 1])
            p += 2
        cvals = [r[...] for r in refs[p:p + nc]]
        outs = fn(*vals, *cvals)
        for o_ref, o in zip(refs[p + nc:], outs):
            o_ref[...] = o.astype(out_dtype)

    ins = [_row_of(r)[0] for r in rows]
    specs = [_row_spec(tm, *_row_of(r)[1:]) for r in rows]
    if conv_w is not None:
        ins += [rows[0], conv_w]
        specs += [_prev_spec(tm, rows[0].shape[1]), _full_spec(conv_w.shape)]
    ins += list(consts)
    specs += [_full_spec(c.shape) for c in consts]
    outs = pl.pallas_call(
        body, grid=(T // tm,), in_specs=specs,
        out_specs=[_row_spec(tm, w) for w in out_widths],
        out_shape=[jax.ShapeDtypeStruct((T, w), out_dtype) for w in out_widths], name=name,
        compiler_params=_params("parallel"))(*ins)
    return outs


def _pw_bwd(name, fn, rows, consts, cots, tm, add_to_first=None, row_dtypes=None):
    rows = [_row_of(r) for r in rows]
    T = rows[0][0].shape[0]
    nr, nc = len(rows), len(consts)
    flat_cots = [c for grp in cots for c in grp]
    row_dtypes = row_dtypes or [f32] * nr
    n_extra = 0 if add_to_first is None else 1

    def body(*refs):
        i = pl.program_id(0)
        in_refs, cot_refs = refs[:nr + nc], refs[nr + nc:nr + nc + len(flat_cots)]
        extra_ref = refs[nr + nc + len(flat_cots)] if add_to_first is not None else None
        row_out = refs[nr + nc + len(flat_cots) + n_extra:][:nr]
        const_out = refs[nr + nc + len(flat_cots) + n_extra + nr:]

        @pl.when(i == 0)
        def _():
            for q in range(nc):
                const_out[q][...] = jnp.zeros_like(const_out[q])

        def part(sl):
            cot_vals, p = [], 0
            for grp in cots:
                acc = cot_refs[p][:, sl]
                for q in range(1, len(grp)):
                    acc = acc + cot_refs[p + q][:, sl]
                p += len(grp)
                cot_vals.append(acc)
            _, vjp = jax.vjp(fn, *[r[:, sl] for r in in_refs])
            grads = vjp(tuple(cot_vals))
            for q in range(nr):
                g = grads[q]
                if q == 0 and extra_ref is not None:
                    g = g + extra_ref[:, sl]
                row_out[q][:, sl] = g.astype(row_dtypes[q])
            for q in range(nc):
                const_out[q][:, sl] += grads[nr + q]

        part(slice(None))

    ins = [r[0] for r in rows] + list(consts) + flat_cots
    specs = ([_row_spec(tm, r[1], r[2]) for r in rows] + [_full_spec(c.shape) for c in consts]
             + [_row_spec(tm, c.shape[1]) for c in flat_cots])
    if add_to_first is not None:
        ins.append(add_to_first)
        specs.append(_row_spec(tm, add_to_first.shape[1]))
    out_shapes = ([jax.ShapeDtypeStruct((T, r[1]), d) for r, d in zip(rows, row_dtypes)]
                  + [jax.ShapeDtypeStruct(c.shape, f32) for c in consts])
    out_specs = [_row_spec(tm, r[1]) for r in rows] + [_full_spec(c.shape) for c in consts]
    outs = pl.pallas_call(
        body, grid=(T // tm,), in_specs=specs, out_specs=out_specs, out_shape=out_shapes, name=name,
        compiler_params=_params("arbitrary"))(*ins)
    return list(outs[:nr]), list(outs[nr:])


def _pw_conv_bwd(name, fn, rows, consts, cots, conv_w, tm, row_dtypes=None):
    T, W0 = rows[0].shape
    nr, nc = len(rows), len(consts)
    taps = conv_w.shape[0]
    nblk = T // tm
    flat_cots = [c for grp in cots for c in grp]
    row_dtypes = row_dtypes or [f32] * nr

    def body(*refs):
        i = pl.program_id(0)
        p = 0
        cur = [r[...] for r in refs[p:p + nr]]; p += nr
        nxt = [r[...] for r in refs[p:p + nr]]; p += nr
        prev = jnp.where(i > 0, refs[p][...], 0.0); p += 1
        w_ref = refs[p]; p += 1
        cvals = [r[...] for r in refs[p:p + nc]]; p += nc

        def summed(p0):
            out, q = [], p0
            for grp in cots:
                acc = refs[q][...]
                for t in range(1, len(grp)):
                    acc = acc + refs[q + t][...]
                q += len(grp)
                out.append(acc)
            return out, q

        cot_cur, p = summed(p)
        cot_nxt, p = summed(p)
        row_out, dw_ref, const_out = refs[p:p + nr], refs[p + nr], refs[p + nr + 1:]

        x_cur = cur[0]
        x_down = [_shift_down(x_cur, prev, s_) for s_ in range(taps)]
        _, vjp = jax.vjp(fn, _conv_apply(x_cur, prev, w_ref, x_down), *cur[1:], *cvals)
        grads = vjp(tuple(cot_cur))
        _, vjp_n = jax.vjp(fn, _conv_apply(nxt[0], x_cur[tm - HALO:], w_ref), *nxt[1:], *cvals)
        dc_n = jnp.where(i < nblk - 1, vjp_n(tuple(cot_nxt))[0], 0.0)
        dc = grads[0]

        @pl.when(i == 0)
        def _():
            dw_ref[...] = jnp.zeros_like(dw_ref)
            for q in range(nc):
                const_out[q][...] = jnp.zeros_like(const_out[q])

        dx = None
        for k in range(taps):
            s_ = taps - 1 - k
            term = _shift_up(dc, dc_n, s_) * w_ref[pl.ds(k, 1), :]
            dx = term if dx is None else dx + term
            dw_ref[pl.ds(k, 1), :] += jnp.sum(dc * x_down[s_], axis=0, keepdims=True)
        row_out[0][...] = dx.astype(row_dtypes[0])
        for q in range(1, nr):
            row_out[q][...] = grads[q].astype(row_dtypes[q])
        for q in range(nc):
            const_out[q][...] += grads[nr + q]

    ins = list(rows) + list(rows) + [rows[0], conv_w] + list(consts) + flat_cots + flat_cots
    specs = ([_row_spec(tm, r.shape[1]) for r in rows] + [_next_spec(tm, r.shape[1], T) for r in rows]
             + [_prev_spec(tm, W0), _full_spec(conv_w.shape)] + [_full_spec(c.shape) for c in consts]
             + [_row_spec(tm, c.shape[1]) for c in flat_cots] + [_next_spec(tm, c.shape[1], T) for c in flat_cots])
    out_shapes = ([jax.ShapeDtypeStruct(r.shape, d) for r, d in zip(rows, row_dtypes)]
                  + [jax.ShapeDtypeStruct(conv_w.shape, f32)] + [jax.ShapeDtypeStruct(c.shape, f32) for c in consts])
    out_specs = ([_row_spec(tm, r.shape[1]) for r in rows] + [_full_spec(conv_w.shape)]
                 + [_full_spec(c.shape) for c in consts])
    outs = pl.pallas_call(
        body, grid=(nblk,), in_specs=specs, out_specs=out_specs, out_shape=out_shapes, name=name,
        compiler_params=_params("arbitrary"))(*ins)
    return list(outs[:nr]), outs[nr], list(outs[nr + 1:])


def _sigmoid(x):
    return 0.5 * jnp.tanh(0.5 * x) + 0.5


def _softplus(x):
    return jnp.maximum(x, 0.0) + jnp.log(1.0 + jnp.exp(jnp.minimum(x, -x)))


def _seg_sum_impl(x, seg):
    w = x.shape[-1]
    r = lax.broadcasted_iota(jnp.int32, (w, w), 0) // seg
    c = lax.broadcasted_iota(jnp.int32, (w, w), 1) // seg
    ones = (r == c).astype(bf16)
    hi = x.astype(bf16)
    lo = (x - hi.astype(f32)).astype(bf16)
    return (jnp.dot(hi, ones, preferred_element_type=f32) + jnp.dot(lo, ones, preferred_element_type=f32))


@functools.partial(jax.custom_vjp, nondiff_argnums=(1,))
def _seg_sum(x, seg):
    return _seg_sum_impl(x, seg)


_seg_sum.defvjp(lambda x, seg: (_seg_sum_impl(x, seg), None), lambda seg, _, g: (_seg_sum_impl(g, seg),))


def _rms(x, g):
    return x * lax.rsqrt(jnp.mean(x * x, axis=-1, keepdims=True) + NORM_EPS) * g


def _rms_fn(x, g):
    return (_rms(x, g),)


def _loss_rows(x2, tgt, g):
    e = _rms(x2, g) - tgt
    return 0.5 * jnp.sum(e * e, axis=-1, keepdims=True) * (1.0 / D_MODEL)


@jax.custom_vjp
def _dot_lo(a, b):
    return jnp.dot(a.astype(bf16), b.astype(bf16), preferred_element_type=f32)


def _dot_lo_bwd(ab, g):
    a, b = ab
    gl = g.astype(bf16)
    return (lax.dot_general(gl, b.astype(bf16), (((1,), (1,)), ((), ())), preferred_element_type=f32),
            lax.dot_general(a.astype(bf16), gl, (((0,), (0,)), ((), ())), preferred_element_type=f32))


_dot_lo.defvjp(lambda a, b: (_dot_lo(a, b), (a, b)), _dot_lo_bwd)


def _rwkv_prep_fn(ps, w0, w2p, a0, a2p, g2, k_k, k_a):
    r, k, v = ps[:, 0:512], ps[:, 512:1024], ps[:, 1024:1536]
    wa, gl = ps[:, 1536:1664], ps[:, 1664:1792]
    z = w0 + _dot_lo(jnp.tanh(wa), w2p)
    w_log = -_softplus(-z) - 0.5
    lw = -jnp.exp(w_log)
    a = _sigmoid(a0 + _dot_lo(wa, a2p))
    g = _dot_lo(_sigmoid(gl), g2)
    kx = k * k_k
    kk = kx * lax.rsqrt(_seg_sum(kx * kx, RWKV_HD) + L2_EPS)
    k2 = k * (1.0 + (a - 1.0) * k_a)
    return r, lw, k2, v, -kk, kk * a, g


def _rwkv_post_fn(y, r, k2, v, g, ln_w, ln_b, rk):
    mean = _seg_sum(y, RWKV_HD) * (1.0 / RWKV_HD)
    yc = y - mean
    var = _seg_sum(yc * yc, RWKV_HD) * (1.0 / RWKV_HD)
    yn = yc * lax.rsqrt(var + GN_EPS) * ln_w + ln_b
    bonus = _seg_sum(r * k2 * rk, RWKV_HD) * v
    return ((yn + bonus) * g,)


def _gdn_prep_fn(cq, ck, cv):
    silu = lambda c: c * _sigmoid(c)
    q, k = silu(cq), silu(ck)
    q = q * lax.rsqrt(jnp.sum(q * q, axis=-1, keepdims=True) + L2_EPS) * (GDN_HD ** -0.5)
    k = k * lax.rsqrt(jnp.sum(k * k, axis=-1, keepdims=True) + L2_EPS)
    return q, k, silu(cv)


def _gdn_gate_fn(ab, al_p, dt_p):
    lane = lax.broadcasted_iota(jnp.int32, ab.shape, 1)
    gpart = -jnp.exp(al_p) * _softplus(ab + dt_p)
    return (jnp.where(lane < GDN_HEADS, gpart, jnp.where(lane < 2 * GDN_HEADS, _sigmoid(ab), 0.0)),)


def _gdn_post_fn(o, z, nw):
    ms = _seg_sum(o * o, GDN_HD) * (1.0 / GDN_HD)
    return (o * lax.rsqrt(ms + NORM_EPS) * nw * (z * _sigmoid(z)),)


def _mix_fn(ga, gb, ya, yb):
    return (_sigmoid(ga) * ya + _sigmoid(gb) * yb,)


STRIP = 128


def _strip_conv(ref, prev_ref, w_ref, sl, first, taps):
    cur = ref[:, sl]
    prev = jnp.where(first, 0.0, prev_ref[:, sl])
    down = [_shift_down(cur, prev, s) for s in range(taps)]
    conv = None
    for k in range(taps):
        term = down[taps - 1 - k] * w_ref[pl.ds(k, 1), sl]
        conv = term if conv is None else conv + term
    return cur, down, conv


def _group_fwd(name, fn, x, w, shared_cols, group_cols, consts, n_out, tm):
    T, W = x.shape
    taps = w.shape[0]
    n_groups = len(group_cols)
    nc = len(consts)

    def body(x_ref, xp_ref, w_ref, *refs):
        const_refs, out_refs = refs[:nc], refs[nc:]
        first = pl.program_id(0) == 0
        shared = [_strip_conv(x_ref, xp_ref, w_ref, sl, first, taps)[2] for sl in shared_cols]
        for j, cols in enumerate(group_cols):
            sl = slice(STRIP * j, STRIP * (j + 1))
            convs = [_strip_conv(x_ref, xp_ref, w_ref, c, first, taps)[2] for c in cols]
            outs = fn(*convs, *shared, *[c[:, sl] for c in const_refs])
            for o_ref, o in zip(out_refs, outs):
                o_ref[:, sl] = o

    return pl.pallas_call(
        body, grid=(T // tm,),
        in_specs=[_row_spec(tm, W), _prev_spec(tm, W), _full_spec(w.shape)] + [_full_spec(c.shape) for c in consts],
        out_specs=[_row_spec(tm, STRIP * n_groups)] * n_out,
        out_shape=[jax.ShapeDtypeStruct((T, STRIP * n_groups), f32)] * n_out, name=name,
        compiler_params=_params("parallel"))(x, x, w, *consts)


def _group_bwd(name, fn, x, w, shared_cols, group_cols, consts, cots, tm):
    T, W = x.shape
    taps = w.shape[0]
    nblk = T // tm
    nc, ns = len(consts), len(shared_cols)
    flat_cots = [c for grp in cots for c in grp]
    n_cot = len(flat_cots)

    def body(x_ref, xp_ref, xn_ref, w_ref, *refs):
        const_refs, refs = refs[:nc], refs[nc:]
        cot_refs, cotn_refs, refs = refs[:n_cot], refs[n_cot:2 * n_cot], refs[2 * n_cot:]
        dx_ref, dw_ref, const_out = refs[0], refs[1], refs[2:]
        i = pl.program_id(0)
        first, last = i == 0, i == nblk - 1

        @pl.when(first)
        def _():
            dw_ref[...] = jnp.zeros_like(dw_ref)
            for q in range(nc):
                const_out[q][...] = jnp.zeros_like(const_out[q])

        def convs_of(sl):
            cur, down, conv = _strip_conv(x_ref, xp_ref, w_ref, sl, first, taps)
            nxt, conv_n = xn_ref[:, sl], None
            for k in range(taps):
                term = _shift_down(nxt, cur[tm - HALO:], taps - 1 - k) * w_ref[pl.ds(k, 1), sl]
                conv_n = term if conv_n is None else conv_n + term
            return down, conv, conv_n

        def conv_back(sl, down, dc, dc_n):
            dx = None
            for k in range(taps):
                s_ = taps - 1 - k
                term = _shift_up(dc, dc_n, s_) * w_ref[pl.ds(k, 1), sl]
                dx = term if dx is None else dx + term
                dw_ref[pl.ds(k, 1), sl] += jnp.sum(dc * down[s_], axis=0, keepdims=True)
            dx_ref[:, sl] = dx.astype(dx_ref.dtype)

        def summed(refs_, sl, mask):
            out, p = [], 0
            for grp in cots:
                acc = refs_[p][:, sl]
                for t in range(1, len(grp)):
                    acc = acc + refs_[p + t][:, sl]
                p += len(grp)
                out.append(jnp.where(last, 0.0, acc) if mask else acc)
            return tuple(out)

        shared = [convs_of(sl) for sl in shared_cols]
        d_shared, d_shared_n = [None] * ns, [None] * ns
        for j, cols in enumerate(group_cols):
            sl = slice(STRIP * j, STRIP * (j + 1))
            mine = [convs_of(c) for c in cols]
            cj = [c[:, sl] for c in const_refs]
            _, vjp = jax.vjp(fn, *[m[1] for m in mine], *[m[1] for m in shared], *cj)
            grads = vjp(summed(cot_refs, sl, False))
            _, vjp_n = jax.vjp(fn, *[m[2] for m in mine], *[m[2] for m in shared], *cj)
            grads_n = vjp_n(summed(cotn_refs, sl, True))
            for q, c in enumerate(cols):
                conv_back(c, mine[q][0], grads[q], grads_n[q])
            for q in range(ns):
                g, gn = grads[len(cols) + q], grads_n[len(cols) + q]
                d_shared[q] = g if d_shared[q] is None else d_shared[q] + g
                d_shared_n[q] = gn if d_shared_n[q] is None else d_shared_n[q] + gn
            for q in range(nc):
                const_out[q][:, sl] += grads[len(cols) + ns + q]
        for q, c in enumerate(shared_cols):
            conv_back(c, shared[q][0], d_shared[q], d_shared_n[q])

    outs = pl.pallas_call(
        body, grid=(nblk,),
        in_specs=[_row_spec(tm, W), _prev_spec(tm, W), _next_spec(tm, W, T), _full_spec(w.shape)]
        + [_full_spec(c.shape) for c in consts] + [_row_spec(tm, c.shape[1]) for c in flat_cots]
        + [_next_spec(tm, c.shape[1], T) for c in flat_cots],
        out_specs=[_row_spec(tm, W), _full_spec(w.shape)] + [_full_spec(c.shape) for c in consts],
        out_shape=[jax.ShapeDtypeStruct((T, W), bf16), jax.ShapeDtypeStruct(w.shape, f32)]
        + [jax.ShapeDtypeStruct(c.shape, f32) for c in consts], name=name,
        compiler_params=_params("arbitrary"))(x, x, x, w, *consts, *flat_cots, *flat_cots)
    return outs[0], outs[1], list(outs[2:])


def _ffn_strip_fn(cg, cu):
    return cg * _sigmoid(cg) * cu


def _ffn_act_fwd(h, w, tm):
    T, W2 = h.shape
    H = W2 // 2
    taps = w.shape[0]

    def body(h_ref, hp_ref, w_ref, o_ref):
        first = pl.program_id(0) == 0
        for j in range(H // STRIP):
            gs, us = slice(STRIP * j, STRIP * (j + 1)), slice(H + STRIP * j, H + STRIP * (j + 1))
            cg = _strip_conv(h_ref, hp_ref, w_ref, gs, first, taps)[2]
            cu = _strip_conv(h_ref, hp_ref, w_ref, us, first, taps)[2]
            o_ref[:, gs] = _ffn_strip_fn(cg, cu).astype(o_ref.dtype)

    return pl.pallas_call(
        body, grid=(T // tm,), in_specs=[_row_spec(tm, W2), _prev_spec(tm, W2), _full_spec(w.shape)],
        out_specs=_row_spec(tm, H), out_shape=jax.ShapeDtypeStruct((T, H), bf16), name="ffn_act",
        compiler_params=_params("parallel"))(h, h, w)


def _ffn_act_bwd(h, dact, w, tm):
    T, W2 = h.shape
    H = W2 // 2
    taps = w.shape[0]
    nblk = T // tm

    def body(h_ref, hp_ref, hn_ref, d_ref, dn_ref, w_ref, dh_ref, dw_ref):
        i = pl.program_id(0)
        first, last = i == 0, i == nblk - 1

        @pl.when(first)
        def _():
            dw_ref[...] = jnp.zeros_like(dw_ref)

        for j in range(H // STRIP):
            gs, us = slice(STRIP * j, STRIP * (j + 1)), slice(H + STRIP * j, H + STRIP * (j + 1))
            parts = {}
            for name, sl in (('g', gs), ('u', us)):
                cur, down, conv = _strip_conv(h_ref, hp_ref, w_ref, sl, first, taps)
                nxt = hn_ref[:, sl]
                conv_n = None
                for k in range(taps):
                    term = _shift_down(nxt, cur[tm - HALO:], taps - 1 - k) * w_ref[pl.ds(k, 1), sl]
                    conv_n = term if conv_n is None else conv_n + term
                parts[name] = (down, conv, conv_n)
            _, vjp = jax.vjp(_ffn_strip_fn, parts['g'][1], parts['u'][1])
            dcs = vjp(d_ref[:, gs])
            _, vjp_n = jax.vjp(_ffn_strip_fn, parts['g'][2], parts['u'][2])
            dcs_n = vjp_n(jnp.where(last, 0.0, dn_ref[:, gs]))
            for (name, sl), dc, dc_n in zip((('g', gs), ('u', us)), dcs, dcs_n):
                down = parts[name][0]
                dx = None
                for k in range(taps):
                    s_ = taps - 1 - k
                    term = _shift_up(dc, dc_n, s_) * w_ref[pl.ds(k, 1), sl]
                    dx = term if dx is None else dx + term
                    dw_ref[pl.ds(k, 1), sl] += jnp.sum(dc * down[s_], axis=0, keepdims=True)
                dh_ref[:, sl] = dx.astype(dh_ref.dtype)

    return pl.pallas_call(
        body, grid=(nblk,),
        in_specs=[_row_spec(tm, W2), _prev_spec(tm, W2), _next_spec(tm, W2, T), _row_spec(tm, H), _next_spec(tm, H, T),
                  _full_spec(w.shape)],
        out_specs=[_row_spec(tm, W2), _full_spec(w.shape)],
        out_shape=[jax.ShapeDtypeStruct((T, W2), bf16), jax.ShapeDtypeStruct(w.shape, f32)], name="ffn_act_bwd",
        compiler_params=_params("arbitrary"))(h, h, h, dact, dact, w)


N_POS = 4


def _xy_out_shapes(bufs, scatter):
    return [jax.ShapeDtypeStruct((N_POS,) + tuple(b.shape[1:] if scatter else b.shape), b.dtype) for b in bufs]


def _xy_sems(n, scatter):
    sems = [pltpu.SemaphoreType.DMA((3 * n,)), pltpu.SemaphoreType.DMA((3 * n,)), pltpu.SemaphoreType.DMA((n,))]
    return sems if scatter else sems + [pltpu.SemaphoreType.DMA((3 * n,)), pltpu.SemaphoreType.DMA((3 * n,))]


def _xy_copies(in_refs, out_refs, sems, scatter):
    n = len(in_refs)
    send_sems, recv_sems, local_sems = sems[:3]

    def place():
        x, y, c = lax.axis_index("x"), lax.axis_index("y"), lax.axis_index("c")
        return x, y, c, 2 * x + y, [(1 - x, y), (x, 1 - y), (1 - x, 1 - y)]

    def half(ref, a, which):
        rows = in_refs[a].shape[0] // 2
        return ref.at[pl.ds(pl.multiple_of(which * rows, HALO), rows)]

    def ici(a, k, src, dst, peer, c):
        return pltpu.make_async_remote_copy(
            src_ref=src, dst_ref=dst, send_sem=send_sems.at[3 * a + k], recv_sem=recv_sems.at[3 * a + k],
            device_id=(peer[0], peer[1], c), device_id_type=pl.DeviceIdType.MESH)

    def outgoing():
        x, y, c, me, peers = place()
        own = [pltpu.make_async_copy(in_refs[a].at[me] if scatter else in_refs[a], out_refs[a].at[me], local_sems.at[a])
               for a in range(n)]
        if scatter:
            sends = [ici(a, k, in_refs[a].at[2 * p[0] + p[1]], out_refs[a].at[me], p, c)
                     for a in range(n) for k, p in enumerate(peers)]
        else:
            sends = [ici(a, k, half(in_refs[a], a, c), half(out_refs[a].at[me], a, c), p, c)
                     for a in range(n) for k, p in enumerate(peers)]
        return own, sends

    def arrivals():
        x, y, c, me, peers = place()
        if scatter:
            return [ici(a, k, in_refs[a].at[me], out_refs[a].at[2 * p[0] + p[1]], p, c)
                    for a in range(n) for k, p in enumerate(peers)]
        return [ici(a, k, half(in_refs[a], a, c), half(out_refs[a].at[2 * p[0] + p[1]], a, c), p, c)
                for a in range(n) for k, p in enumerate(peers)]

    def to_sibling(mine):
        x, y, c, me, peers = place()
        which = c if mine else 1 - c
        return [pltpu.make_async_remote_copy(
            src_ref=half(out_refs[a].at[2 * p[0] + p[1]], a, which), dst_ref=half(out_refs[a].at[2 * p[0] + p[1]], a, which),
            send_sem=sems[3].at[3 * a + k], recv_sem=sems[4].at[3 * a + k],
            device_id=(x, y, 1 - c), device_id_type=pl.DeviceIdType.MESH) for a in range(n) for k, p in enumerate(peers)]

    def start():
        own, sends = outgoing()
        for cp in own + sends:
            cp.start()

    def finish():
        if scatter:
            for cp in arrivals():
                cp.wait_recv()
        else:
            passed = to_sibling(True)
            for cp, fwd in zip(arrivals(), passed):
                cp.wait_recv()
                fwd.start()
            for cp in to_sibling(False):
                cp.wait_recv()
            for fwd in passed:
                fwd.wait_send()
        own, sends = outgoing()
        for cp in sends:
            cp.wait_send()
        for cp in own:
            cp.wait()

    return start, finish


_NN, _NT, _TN = 'hcs,hsd->hcd', 'hcd,hsd->hcs', 'hcd,hce->hde'


def _lo(spec, a, b):
    return jnp.einsum(spec, a.astype(bf16), b.astype(bf16), preferred_element_type=f32)


@jax.custom_vjp
def _bmm(a, b):
    return _lo(_NN, a, b)


_bmm.defvjp(lambda a, b: (_lo(_NN, a, b), (a, b)), lambda ab, g: (_lo(_NT, g, ab[1]), _lo(_TN, ab[0], g)))


@jax.custom_vjp
def _bmm_nt(a, b):
    return _lo(_NT, a, b)


_bmm_nt.defvjp(lambda a, b: (_lo(_NT, a, b), (a, b)), lambda ab, g: (_lo(_NN, g, ab[1]), _lo(_TN, g, ab[0])))


@jax.custom_vjp
def _bmm_tn(a, b):
    return _lo(_TN, a, b)


_bmm_tn.defvjp(lambda a, b: (_lo(_TN, a, b), (a, b)), lambda ab, g: (_lo(_NT, ab[1], g), _lo(_NN, ab[0], g)))


def _masks(H, C):
    row = lax.broadcasted_iota(jnp.int32, (H, C, C), 1)
    col = lax.broadcasted_iota(jnp.int32, (H, C, C), 2)
    return row, col


def _tri_inv_impl(L):
    H, C, _ = L.shape
    row, col = _masks(H, C)
    eye = (row == col).astype(f32)
    base = 16
    same = (row // base) == (col // base)
    Ld = jnp.where(same, L, 0.0)
    X = -Ld
    inv = eye + X
    for _ in range(3):
        X = _bmm(X, X)
        inv = _bmm(inv, eye + X)
    if C == base:
        return inv
    N = _bmm(inv, L - Ld)
    out = eye - N
    levels = C // base
    P = N
    span = 2
    while span < levels:
        P = _bmm(P, P)
        out = _bmm(out, eye + P)
        span *= 2
    return _bmm(out, inv)


@jax.custom_vjp
def _tri_inv(L):
    return _tri_inv_impl(L)


def _tri_inv_fwd(L):
    T = _tri_inv_impl(L)
    return T, T


def _tri_inv_bwd(T, dT):
    return (-_bmm_nt(_bmm_tn(T, dT), T),)


_tri_inv.defvjp(_tri_inv_fwd, _tri_inv_bwd)


@jax.custom_vjp
def _tri_inv_known(L, T):
    return T


_tri_inv_known.defvjp(lambda L, T: (T, T), lambda T, dT: (_tri_inv_bwd(T, dT)[0], jnp.zeros_like(T)))


def _cumsum_impl(x, reverse):
    C = x.shape[1]
    row = lax.broadcasted_iota(jnp.int32, x.shape, 1)
    s = 1
    while s < C:
        if reverse:
            x = x + jnp.where(row < C - s, pltpu.roll(x, C - s, 1), 0.0)
        else:
            x = x + jnp.where(row >= s, pltpu.roll(x, s, 1), 0.0)
        s *= 2
    return x


@jax.custom_vjp
def _cumsum(x):
    return _cumsum_impl(x, False)


_cumsum.defvjp(lambda x: (_cumsum_impl(x, False), None), lambda _, g: (_cumsum_impl(g, True),))


def _wkv_prep(r, lw, k, v, a, b, inv=None):
    lane = lax.broadcasted_iota(jnp.int32, (r.shape[0], 128), 1)
    low = lane < RWKV_HD

    def heads(t):
        out = []
        for p in range(RWKV_HEADS // 2):
            pair = t[:, 128 * p:128 * (p + 1)]
            out += [jnp.where(low, pair, 0.0), jnp.where(low, 0.0, pair)]
        return jnp.concatenate([t[None] for t in out], axis=0)

    r, lw, k, v, a, b = [heads(t) for t in (r, lw, k, v, a, b)]
    H, C, D = r.shape
    row, col = _masks(H, C)
    incl, strict = row >= col, row > col
    cw = _cumsum(lw)
    cwp = cw - lw
    cwl = jnp.sum(lw, axis=1, keepdims=True)
    en = jnp.exp(-cw)
    at, rt, bt, kt = a * jnp.exp(cwp), r * jnp.exp(cw), b * en, k * en
    Lab = -jnp.where(strict, _bmm_nt(at, bt), 0.0)
    Tm = _tri_inv(Lab) if inv is None else _tri_inv_known(Lab, inv)
    ar = jnp.concatenate([at, rt], axis=1)
    gram = _bmm_nt(ar, jnp.concatenate([bt, kt], axis=1))
    row2 = lax.broadcasted_iota(jnp.int32, (H, 2 * C, 2 * C), 1)
    col2 = lax.broadcasted_iota(jnp.int32, (H, 2 * C, 2 * C), 2) % C
    gram = jnp.where(((row2 < C) & (row2 > col2)) | ((row2 >= C) & (row2 - C >= col2)), gram, 0.0)
    a_bk, r_bk = gram[:, :C], gram[:, C:]
    lak_v = _bmm(a_bk, jnp.concatenate([jnp.zeros_like(v), v], axis=1))
    ed = jnp.exp(cwl - cw)
    zdec = jnp.swapaxes(jnp.broadcast_to(jnp.exp(cwl), (H, D, D)), 1, 2)
    return (ar, Tm, lak_v, r_bk, jnp.concatenate([b * ed, k * ed], axis=1), zdec, v), Tm


def _wkv_step(Z, ar, Tm, lak_v, r_bk, bk_d, zdec, v):
    C = Tm.shape[1]
    ar_z = _bmm(ar, Z)
    uv = jnp.concatenate([_bmm(Tm, ar_z[:, :C] + lak_v), v], axis=1)
    y = ar_z[:, C:] + _bmm(r_bk, uv)
    Z1 = Z * zdec + _bmm_tn(bk_d, uv)
    return jnp.concatenate([y[2 * p] + y[2 * p + 1] for p in range(RWKV_HEADS // 2)], axis=1), Z1


def _split3(x):
    hi = x.astype(bf16)
    mid = (x - hi.astype(f32)).astype(bf16)
    lo = (x - hi.astype(f32) - mid.astype(f32)).astype(bf16)
    return hi, mid, lo


@jax.custom_vjp
def _spread(x, sel):
    return sum(jnp.dot(t, sel, preferred_element_type=f32) for t in _split3(x))


def _spread_bwd(sel, g):
    dn = (((1,), (1,)), ((), ()))
    return sum(lax.dot_general(t, sel, dn, preferred_element_type=f32) for t in _split3(g)), None


_spread.defvjp(lambda x, sel: (_spread(x, sel), sel), _spread_bwd)


def _gdn_prep(q, k, v, gbeta, inv=None):
    heads = lambda t: jnp.concatenate([t[None, :, GDN_HD * h:GDN_HD * (h + 1)] for h in range(GDN_HEADS)], axis=0)
    src = lax.broadcasted_iota(jnp.int32, (W_AB, 2 * GDN_W), 0)
    dst = lax.broadcasted_iota(jnp.int32, (W_AB, 2 * GDN_W), 1) // GDN_HD
    spread = _spread(gbeta, (src == dst).astype(bf16))
    q, k, v, g, beta = heads(q), heads(k), heads(v), heads(spread[:, :GDN_W]), heads(spread[:, GDN_W:])
    H, C, D = q.shape
    row, col = _masks(H, C)
    incl, strict = row >= col, row > col
    gc = _cumsum(g)
    diff = gc - jnp.swapaxes(gc, 1, 2)
    decay = jnp.where(incl, jnp.exp(jnp.where(incl, diff, 0.0)), 0.0)
    gl = jnp.sum(g, axis=1, keepdims=True)
    kb, vb = k * beta, v * beta
    gram = _bmm_nt(jnp.concatenate([kb, q], axis=1), k)
    L = jnp.where(strict, gram[:, :C] * decay, 0.0)
    attn = jnp.where(incl, gram[:, C:] * decay, 0.0)
    egc = jnp.exp(gc)
    Tm = _tri_inv(L) if inv is None else _tri_inv_known(L, inv)
    t_vk = _bmm(Tm, jnp.concatenate([vb, kb * egc], axis=2))
    return (t_vk[:, :, :D], jnp.concatenate([t_vk[:, :, D:], q * egc], axis=1), attn, k * jnp.exp(gl - gc), jnp.exp(gl)), Tm


def _gdn_step(S, u, wq, attn, ke, sdec):
    C = u.shape[1]
    wq_s = _bmm(wq, S)
    v_new = u - wq_s[:, :C]
    o = wq_s[:, C:] + _bmm(attn, v_new)
    S1 = S * sdec + _bmm_tn(ke, v_new)
    return jnp.concatenate([o[h] for h in range(GDN_HEADS)], axis=1), S1


def _scan_fwd(name, fns, ins, C, H, dh, w_out, per_step, side=None):
    prep, step = fns
    T = ins[0].shape[0]
    n_in = len(ins)
    blk = C * per_step
    nblk = T // blk
    n_side = 0 if side is None else len(side[0])

    def body(*refs):
        in_refs, refs = refs[:n_in], refs[n_in:]
        side_in, refs = refs[:n_side], refs[n_side:]
        y_ref, zs_ref, inv_ref, refs = refs[0], refs[1], refs[2], refs[3:]
        side_out, refs = refs[:n_side], refs[n_side:]
        z_scr = refs[0]
        if side is not None:
            start, finish = _xy_copies(side_in, side_out, refs[1:], side[1])
            pl.when(pl.program_id(0) == 0)(start)

        @pl.when(pl.program_id(0) == 0)
        def _():
            z_scr[...] = jnp.zeros_like(z_scr)

        rows = [slice(C * j, C * (j + 1)) for j in range(per_step)]
        prepped = [prep(*[r[rw, :] for r in in_refs]) for rw in rows]
        Z = z_scr[...]
        for j, rw in enumerate(rows):
            zs_ref[j] = Z
            inv_ref[j] = prepped[j][1]
            y, Z = step(Z, *prepped[j][0])
            y_ref[rw, :] = y
        z_scr[...] = Z
        if side is not None:
            pl.when(pl.program_id(0) == nblk - 1)(finish)

    side_bufs = [] if side is None else list(side[0])
    any_spec = pl.BlockSpec(memory_space=pl.ANY)
    return pl.pallas_call(
        body, grid=(nblk,),
        in_specs=[pl.BlockSpec((blk, a.shape[1]), lambda i: (i, 0)) for a in ins] + [any_spec] * n_side,
        out_specs=[pl.BlockSpec((blk, w_out), lambda i: (i, 0)), pl.BlockSpec((per_step, H, dh, dh), lambda i: (i, 0, 0, 0)),
                   pl.BlockSpec((per_step, H, C, C), lambda i: (i, 0, 0, 0))] + [any_spec] * n_side,
        out_shape=[jax.ShapeDtypeStruct((T, w_out), f32), jax.ShapeDtypeStruct((T // C, H, dh, dh), f32),
                   jax.ShapeDtypeStruct((T // C, H, C, C), f32)]
        + (_xy_out_shapes(side_bufs, side[1]) if side is not None else []),
        scratch_shapes=[pltpu.VMEM((H, dh, dh), f32)] + (_xy_sems(n_side, side[1]) if side is not None else []), name=name,
        compiler_params=_params("arbitrary"))(*ins, *side_bufs)


def _scan_bwd(name, fns, ins, dy, zs, invs, C, per_step, side=None):
    prep, step = fns
    T = ins[0].shape[0]
    _, H, dh, _ = zs.shape
    n_in = len(ins)
    blk = C * per_step
    nblk = T // blk
    n_side = 0 if side is None else len(side[0])

    def body(*refs):
        in_refs, dy_ref, zs_ref, inv_ref, refs = refs[:n_in], refs[n_in], refs[n_in + 1], refs[n_in + 2], refs[n_in + 3:]
        side_in, refs = refs[:n_side], refs[n_side:]
        out_refs, refs = refs[:n_in], refs[n_in:]
        side_out, refs = refs[:n_side], refs[n_side:]
        dz_scr = refs[0]
        if side is not None:
            start, finish = _xy_copies(side_in, side_out, refs[1:], side[1])
            pl.when(pl.program_id(0) == 0)(start)

        @pl.when(pl.program_id(0) == 0)
        def _():
            dz_scr[...] = jnp.zeros_like(dz_scr)

        rows = [slice(C * j, C * (j + 1)) for j in range(per_step)]
        prepped = [jax.vjp(lambda *a, j=j: prep(*a, inv=inv_ref[j])[0], *[r[rw, :] for r in in_refs])
                   for j, rw in enumerate(rows)]
        d_prepped = [None] * per_step
        dZ = dz_scr[...]
        for j in reversed(range(per_step)):
            _, pull = jax.vjp(step, zs_ref[j], *prepped[j][0])
            dZ, *d_prepped[j] = pull((dy_ref[rows[j], :], dZ))
        dz_scr[...] = dZ
        for j, rw in enumerate(rows):
            for o_ref, gval in zip(out_refs, prepped[j][1](tuple(d_prepped[j]))):
                o_ref[rw, :] = gval
        if side is not None:
            pl.when(pl.program_id(0) == nblk - 1)(finish)

    side_bufs = [] if side is None else list(side[0])
    any_spec = pl.BlockSpec(memory_space=pl.ANY)
    rev = lambda i: (nblk - 1 - i, 0)
    return pl.pallas_call(
        body, grid=(nblk,),
        in_specs=[pl.BlockSpec((blk, a.shape[1]), rev) for a in ins]
        + [pl.BlockSpec((blk, dy.shape[1]), rev), pl.BlockSpec((per_step, H, dh, dh), lambda i: (nblk - 1 - i, 0, 0, 0)),
           pl.BlockSpec((per_step, H, C, C), lambda i: (nblk - 1 - i, 0, 0, 0))] + [any_spec] * n_side,
        out_specs=[pl.BlockSpec((blk, a.shape[1]), rev) for a in ins] + [any_spec] * n_side,
        out_shape=[jax.ShapeDtypeStruct(a.shape, f32) for a in ins]
        + (_xy_out_shapes(side_bufs, side[1]) if side is not None else []),
        scratch_shapes=[pltpu.VMEM((H, dh, dh), f32)] + (_xy_sems(n_side, side[1]) if side is not None else []), name=name,
        compiler_params=_params("arbitrary"))(*ins, dy, zs, invs, *side_bufs)


def _residual_mm(name, a, b, res, tail, row_extras, consts, row_out, acc_out, tm, head=None):
    K, N = b.shape
    h_rows, h_consts = ([], []) if head is None else (list(head[1]), list(head[2]))
    lhs = h_rows + h_consts if head is not None else [a]
    M = lhs[0].shape[0]
    row_extras = [_row_of(e) for e in row_extras]
    n_lhs, n_res = len(lhs), 0 if res is None else 1
    ne, nc, nr = len(row_extras), len(consts), len(row_out)

    def body(*refs):
        lhs_refs, refs = refs[:n_lhs], refs[n_lhs:]
        b_ref, refs = refs[0], refs[1:]
        res_ref, refs = (refs[0], refs[1:]) if res is not None else (None, refs)
        extra_refs, const_refs, out_refs = refs[:ne], refs[ne:ne + nc], refs[ne + nc:]
        if head is not None:
            left = head[0](*[r[...] for r in lhs_refs])[0].astype(bf16)
            out_refs[0][...] = left
            out_refs = out_refs[1:]
        else:
            left = lhs_refs[0][...].astype(bf16)
        tile = jnp.dot(left, b_ref[...].astype(bf16), preferred_element_type=f32)
        if res is not None:
            tile = res_ref[...] + tile
        outs = tail(tile, *[r[...] for r in extra_refs], *[c[...] for c in const_refs])
        for o_ref, o in zip(out_refs[:nr], outs[:nr]):
            o_ref[...] = o.astype(o_ref.dtype)

        @pl.when(pl.program_id(0) == 0)
        def _():
            for o_ref in out_refs[nr:]:
                o_ref[...] = jnp.zeros_like(o_ref)

        for o_ref, o in zip(out_refs[nr:], outs[nr:]):
            o_ref[...] += o

    lhs_specs = ([_row_spec(tm, r.shape[1]) for r in h_rows] + [_full_spec(c.shape) for c in h_consts]
                 if head is not None else [_row_spec(tm, K)])
    head_out = [(K, bf16)] if head is not None else []
    outs = pl.pallas_call(
        body, grid=(M // tm,),
        in_specs=lhs_specs + [_full_spec(b.shape)] + ([_row_spec(tm, N)] if res is not None else [])
        + [_row_spec(tm, e[1], e[2]) for e in row_extras] + [_full_spec(c.shape) for c in consts],
        out_specs=[_row_spec(tm, w) for w, _ in head_out + list(row_out)] + [_full_spec(sh) for sh in acc_out],
        out_shape=[jax.ShapeDtypeStruct((M, w), d) for w, d in head_out + list(row_out)]
        + [jax.ShapeDtypeStruct(sh, f32) for sh in acc_out],
        name=name, compiler_params=_params("arbitrary"))(
            *lhs, b, *([res] if res is not None else []), *[e[0] for e in row_extras], *consts)
    return outs


def _pull_tail(fn):
    def tail(cot, *args):
        _, vjp = jax.vjp(fn, *args)
        return vjp((cot,))
    return tail


def _norm_tail(x1, g):
    return x1, _rms(x1, g)


def _loss_tail(x2, tgt, g):
    l, vjp = jax.vjp(lambda xv, gv: _loss_rows(xv, tgt, gv), x2, g)
    dx, dg = vjp(jnp.ones_like(l))
    return dx, dg, jnp.zeros((1, 128), f32) + jnp.sum(l)


def _local_step(x, tgt, W, late=None):
    row = lambda a: a.reshape(1, -1)
    wp = W['w_in_pad']
    w_rwkv, w_qkv, w_z = wp[:, :OFF_QKV], wp[:, OFF_QKV:OFF_Z], wp[:, OFF_Z:OFF_GATES]
    w_gates, w_ab = wp[:, OFF_GATES:OFF_AB], wp[:, OFF_AB:]
    mu = row(W['rwkv_mu'])
    mixw = jnp.concatenate([mu, 1.0 - mu], axis=0)
    zpad = jnp.zeros((64, RWKV_W), f32)
    w2p = jnp.concatenate([W['rwkv_w2'], zpad], axis=0)
    a2p = jnp.concatenate([zpad, W['rwkv_a2']], axis=0)
    rw_consts = [row(W['rwkv_w0']), w2p, row(W['rwkv_a0']), a2p, W['rwkv_g2'], row(W['rwkv_k_k']), row(W['rwkv_k_a'])]
    post_consts = [row(W['rwkv_ln_w']), row(W['rwkv_ln_b']), row(W['rwkv_r_k'])]
    pad4 = lambda a: jnp.pad(row(a), ((0, 0), (0, W_AB - GDN_HEADS)))
    gd_consts = [pad4(W['gdn_a_log']), pad4(W['gdn_dt_bias'])]
    nw_t = jnp.tile(row(W['gdn_norm_w']), (1, GDN_HEADS))
    g1, g2n, gf = row(W['norm1_g']), row(W['norm2_g']), row(W['final_g'])

    u = late['u'] if late is not None else _pw_fwd("norm1", _rms_fn, [x], [g1], [D_MODEL], 512, out_dtype=bf16)[0]
    p_rwkv = _mm(u, w_rwkv, 'nn', "in_rwkv")
    qkv_raw = _mm(u, w_qkv, 'nn', "in_qkv")
    z = _mm(u, w_z, 'nn', "in_z")
    gates = _mm(u, w_gates, 'nn', "in_gates")
    ab = _mm(u, w_ab, 'nn', "in_ab")

    r, lw, k2, v, a_, b_, g = _pw_fwd("rwkv_prep", _rwkv_prep_fn, [p_rwkv], rw_consts, [RWKV_W] * 7, 256, conv_w=mixw)
    wkv_in = [r, lw, k2, v, a_, b_]
    y, zs_wkv, inv_wkv, *gathered = _scan_fwd("wkv_fwd", (_wkv_prep, _wkv_step), wkv_in, WKV_CHUNK, RWKV_HEADS, 2 * RWKV_HD, RWKV_W, WKV_PER_STEP,
                                     side=None if late is None else (late['shards'][0], False))
    if late is not None:
        W = dict(W, **late['assemble'](0, gathered))
    ya_in, ya = _residual_mm("rwkv_proj", None, W['rwkv_proj'], None, lambda t: (t,), [], [], [(D_MODEL, f32)], [], 512,
                             head=(_rwkv_post_fn, [y, r, k2, v, g], post_consts))

    lanes = lambda off: slice(off, off + STRIP)
    gd_groups = [[lanes(GDN_HD * h), lanes(GDN_W + GDN_HD * h), lanes(2 * GDN_W + GDN_HD * h)] for h in range(GDN_HEADS)]
    gq, gk, gv = _group_fwd("gdn_prep", _gdn_prep_fn, qkv_raw, W['gdn_conv_w'], [], gd_groups, [], 3, 256)
    (gbeta,) = _pw_fwd("gdn_gate", _gdn_gate_fn, [ab], gd_consts, [W_AB], 512)
    gdn_in = [gq, gk, gv, gbeta]
    o, zs_gdn, inv_gdn, *gathered = _scan_fwd("gdn_fwd", (_gdn_prep, _gdn_step), gdn_in, GDN_CHUNK, GDN_HEADS, GDN_HD, GDN_W, GDN_PER_STEP,
                                     side=None if late is None else (late['shards'][1], False))
    if late is not None:
        W = dict(W, **late['assemble'](1, gathered))
    ga, gb = _cols(gates, D_MODEL, 0), _cols(gates, D_MODEL, 1)
    yb_in, yb, mixed = _residual_mm("gdn_proj", None, W['gdn_proj'], None, lambda t, a_, b_, c_: (t,) + _mix_fn(a_, b_, c_, t),
                                    [ga, gb, ya], [], [(D_MODEL, f32), (D_MODEL, bf16)], [], 512,
                                    head=(_gdn_post_fn, [o, z], [nw_t]))

    x1, u2 = _residual_mm("w_out", mixed, W['w_out'], x, _norm_tail, [], [g2n], [(D_MODEL, f32), (D_MODEL, bf16)], [], 512)
    h = _mm(u2, W['ffn_up'], 'nn', "ffn_up")
    act = _ffn_act_fwd(h, W['ffn_conv_w'], 256)

    G = {}
    slab_out = None if late is None else N_POS
    dx2, dgf, loss = _residual_mm("ffn_down", act, W['ffn_down'], x1, _loss_tail, [tgt], [gf], [(D_MODEL, f32)],
                                  [gf.shape, (1, 128)], 512)
    G['final_g'] = dgf
    dact = _mm(dx2, W['ffn_down'], 'nt', "d_act")
    G['ffn_down'] = _mm(act, dx2, 'tn', "g_ffn_down", out_dtype=bf16)
    dh, G['ffn_conv_w'] = _ffn_act_bwd(h, dact, W['ffn_conv_w'], 128)
    du2 = _mm(dh, W['ffn_up'], 'nt', "d_u2")
    G['ffn_up'] = _mm(u2, dh, 'tn', "g_ffn_up", out_dtype=bf16, col_slabs=slab_out)
    (dx1,), (G['norm2_g'],) = _pw_bwd("norm2_bwd", _rms_fn, [x1], [g2n], [(du2,)], 512, add_to_first=dx2)
    G['w_out'] = _mm(mixed, dx1, 'tn', "g_w_out", out_dtype=bf16)
    dga, dgb, dya, dyb = _residual_mm("d_mixed", dx1, W['w_out'].T, None, _pull_tail(_mix_fn), [ga, gb, ya, yb], [],
                                      [(D_MODEL, bf16)] * 4, [], 512)
    G['rwkv_proj'] = _mm(ya_in, dya, 'tn', "g_rwkv_proj", out_dtype=bf16, col_slabs=slab_out)
    G['gdn_proj'] = _mm(yb_in, dyb, 'tn', "g_gdn_proj", out_dtype=bf16, col_slabs=slab_out)

    do, dz, dnw_t = _residual_mm("d_yb_in", dyb, W['gdn_proj'].T, None, _pull_tail(_gdn_post_fn), [o, z], [nw_t],
                                 [(GDN_W, f32), (GDN_W, bf16)], [nw_t.shape], 512)
    G['gdn_norm_w'] = dnw_t.reshape(GDN_HEADS, GDN_HD).sum(axis=0)
    dgq, dgk, dgv, dgbeta, *arrived_b = _scan_bwd("gdn_bwd", (_gdn_prep, _gdn_step), gdn_in, do, zs_gdn, inv_gdn, GDN_CHUNK,
                                                  GDN_PER_STEP, side=None if late is None else (late['slabs'](G, 1), True))
    dqkv_raw, G['gdn_conv_w'], _ = _group_bwd("gdn_prep_bwd", _gdn_prep_fn, qkv_raw, W['gdn_conv_w'], [], gd_groups, [],
                                              [(dgq,), (dgk,), (dgv,)], 128)
    (dab,), (dal_p, ddt_p) = _pw_bwd("gdn_gate_bwd", _gdn_gate_fn, [ab], gd_consts, [(dgbeta,)], 512, row_dtypes=[bf16])
    G['gdn_a_log'], G['gdn_dt_bias'] = dal_p[0, :GDN_HEADS], ddt_p[0, :GDN_HEADS]

    dy, dr1, dk21, dv1, dg_, G['rwkv_ln_w'], G['rwkv_ln_b'], G['rwkv_r_k'] = _residual_mm(
        "d_ya_in", dya, W['rwkv_proj'].T, None, _pull_tail(_rwkv_post_fn), [y, r, k2, v, g], post_consts,
        [(RWKV_W, f32)] * 5, [c.shape for c in post_consts], 512)
    dr2, dlw, dk22, dv2, da_, db_, *arrived_a = _scan_bwd(
        "wkv_bwd", (_wkv_prep, _wkv_step), wkv_in, dy, zs_wkv, inv_wkv, WKV_CHUNK, WKV_PER_STEP,
        side=None if late is None else (late['slabs'](G, 0), True))
    G['_arrived'] = (arrived_a, arrived_b)
    (dp_rwkv,), dmixw, rw_grads = _pw_conv_bwd(
        "rwkv_prep_bwd", _rwkv_prep_fn, [p_rwkv], rw_consts,
        [(dr1, dr2), (dlw,), (dk21, dk22), (dv1, dv2), (da_,), (db_,), (dg_,)], mixw, 256, row_dtypes=[bf16])
    G['rwkv_w0'], dw2p, G['rwkv_a0'], da2p, G['rwkv_g2'], G['rwkv_k_k'], G['rwkv_k_a'] = rw_grads
    G['rwkv_w2'], G['rwkv_a2'] = dw2p[:64], da2p[64:]
    G['rwkv_mu'] = dmixw[0] - dmixw[1]

    dps = [dp_rwkv, dqkv_raw, dz, dga, dgb, dab]
    offs = [0, OFF_QKV, OFF_Z, OFF_GATES, OFF_GATES + D_MODEL, OFF_AB]
    G['w_in_pad'] = list(zip(offs, _mm_tn_parts(u, dps[:2], "g_w_in_rwkv_qkv") + _mm_tn_parts(u, dps[2:], "g_w_in_rest")))
    pairs = [(dp_rwkv, w_rwkv, 0), (dqkv_raw, w_qkv, 0), (dz, w_z, 0), (dga, w_gates, 0), (dgb, w_gates, 1), (dab, w_ab, 0)]
    if late is None:
        du = _mm_nt_parts(pairs, "d_u")
    else:
        du, *G['_arrived_w_in'] = _mm_nt_parts(pairs, "d_u", side=(late['w_in_slabs'](G), True))
    (dx,), (G['norm1_g'],) = _pw_bwd("norm1_bwd", _rms_fn, [x], [g1], [(du,)], 512, add_to_first=dx1)
    return loss, dx, G


IN_WIDTH = OFF_AB + 8
PAD_ORDER = ((0, OFF_GATES), (OFF_GATES + 8, IN_WIDTH), (OFF_GATES, OFF_GATES + 8))


def _pad_w_in_shards(shards):
    width = shards[0].shape[1]
    parts = []
    for a, b in PAD_ORDER:
        for j, sh in enumerate(shards):
            lo, hi = max(a, j * width), min(b, (j + 1) * width)
            if lo < hi:
                parts.append(sh[:, lo - j * width:hi - j * width])
    return jnp.concatenate(parts + [jnp.zeros((shards[0].shape[0], W_AB - 8), shards[0].dtype)], axis=1)


def _padded_cols(sections, s, e):
    pieces = [arr[:, max(s, o) - o:min(e, o + arr.shape[1]) - o] for o, arr in sections if max(s, o) < min(e, o + arr.shape[1])]
    return pieces[0] if len(pieces) == 1 else jnp.concatenate(pieces, axis=1)


def _unpad_cols(sections, lo, hi):
    parts, off = [], 0
    for a, b in PAD_ORDER:
        l, h = max(a, lo), min(b, hi)
        if l < h:
            parts.append((l, _padded_cols(sections, off + l - a, off + h - a)))
        off += b - a
    parts.sort(key=lambda t: t[0])
    return parts[0][1] if len(parts) == 1 else jnp.concatenate([p for _, p in parts], axis=1)


BIG = ('w_in', 'rwkv_proj', 'gdn_proj', 'w_out', 'ffn_up', 'ffn_down')
SMALL_SHARDED = ('rwkv_w2', 'rwkv_a2', 'rwkv_g2', 'gdn_conv_w', 'ffn_conv_w')


def _rows128(shape):
    n = 1
    for d in shape:
        n *= d
    return -(-n // LANES)


def _pack128(arrays):
    parts = []
    for a in arrays:
        flat = a.reshape(-1)
        rows = _rows128(a.shape)
        parts.append(jnp.pad(flat, (0, rows * LANES - flat.shape[0])).reshape(rows, LANES))
    buf = jnp.concatenate(parts, axis=0)
    return jnp.pad(buf, ((0, -buf.shape[0] % HALO), (0, 0)))


def _unpack128(buf, shapes):
    out, off = [], 0
    for s in shapes:
        rows, n = _rows128(s), 1
        for d in s:
            n *= d
        out.append(buf[off:off + rows].reshape(-1)[:n].reshape(s))
        off += rows
    return out


def _param_tile(r, c):
    best = None
    for d in range(2 * HALO, r + 1, 2 * HALO):
        if r % d == 0 and d * c * 4 <= TILE_BYTES:
            best = d
    if best is not None or r * c * 4 <= TILE_BYTES:
        return (best if best is not None else r), c
    return r, 128


def _xy_exchange(name, bufs, scatter):
    n = len(bufs)

    def body(*refs):
        start, finish = _xy_copies(refs[:n], refs[n:2 * n], refs[2 * n:], scatter)
        start()
        finish()

    return pl.pallas_call(
        body, in_specs=[pl.BlockSpec(memory_space=pl.ANY)] * n, out_specs=[pl.BlockSpec(memory_space=pl.ANY)] * n,
        out_shape=_xy_out_shapes(bufs, scatter), scratch_shapes=_xy_sems(n, scatter), name=name)(*bufs)


def _norm_gather(name, x, g, bufs, tm):
    n, (T, D) = len(bufs), x.shape
    steps = T // tm

    def body(x_ref, g_ref, *refs):
        side_in, u_ref, side_out, sems = refs[:n], refs[n], refs[n + 1:2 * n + 1], refs[2 * n + 1:]
        i = pl.program_id(0)
        start, finish = _xy_copies(side_in, side_out, sems, False)
        pl.when(i == 0)(start)
        u_ref[...] = _rms(x_ref[...], g_ref[...]).astype(bf16)
        pl.when(i == steps - 1)(finish)

    any_spec = pl.BlockSpec(memory_space=pl.ANY)
    return pl.pallas_call(
        body, grid=(steps,), in_specs=[_row_spec(tm, D), _full_spec(g.shape)] + [any_spec] * n,
        out_specs=[_row_spec(tm, D)] + [any_spec] * n,
        out_shape=[jax.ShapeDtypeStruct((T, D), bf16)] + _xy_out_shapes(bufs, False),
        scratch_shapes=_xy_sems(n, False), name=name, compiler_params=_params("arbitrary"))(x, g, *bufs)


def _sibling_exchange(name, bufs):
    n = len(bufs)

    def body(*refs):
        in_refs, out_refs, send_sems, recv_sems = refs[:n], refs[n:2 * n], refs[2 * n], refs[2 * n + 1]
        x, y, c = lax.axis_index("x"), lax.axis_index("y"), lax.axis_index("c")
        copies = [pltpu.make_async_remote_copy(
            src_ref=in_refs[a], dst_ref=out_refs[a], send_sem=send_sems.at[a], recv_sem=recv_sems.at[a],
            device_id=(x, y, 1 - c), device_id_type=pl.DeviceIdType.MESH) for a in range(n)]
        for cp in copies:
            cp.start()
        for cp in copies:
            cp.wait()

    return pl.pallas_call(
        body, in_specs=[pl.BlockSpec(memory_space=pl.ANY)] * n, out_specs=[pl.BlockSpec(memory_space=pl.ANY)] * n,
        out_shape=[jax.ShapeDtypeStruct(b.shape, b.dtype) for b in bufs],
        scratch_shapes=[pltpu.SemaphoreType.DMA((n,)), pltpu.SemaphoreType.DMA((n,))], name=name)(*bufs)


def _sum_slots(name, buf):
    _, R, L = buf.shape
    tr, tc = _param_tile(R, L)

    def body(b_ref, o_ref):
        part = lambda s: b_ref[s].astype(f32)
        o_ref[...] = ((part(0) + part(1)) + part(2)) + part(3)

    return pl.pallas_call(
        body, grid=(R // tr, L // tc),
        in_specs=[pl.BlockSpec((N_POS, tr, tc), lambda i, j: (0, i, j))],
        out_specs=pl.BlockSpec((tr, tc), lambda i, j: (i, j)),
        out_shape=jax.ShapeDtypeStruct((R, L), f32), name=name,
        compiler_params=_params("parallel", "parallel"))(buf)


def _adamw(name, w, ga, gb, m, v):
    R, L = w.shape
    tr, tc = _param_tile(R, L)
    c1 = 1.0 / (1.0 - ADAM_B1 ** ADAM_STEP)
    c2 = 1.0 / (1.0 - ADAM_B2 ** ADAM_STEP)

    def body(w_ref, ga_ref, gb_ref, m_ref, v_ref, g_out, d_out, m_out, v_out):
        g = ga_ref[...] + gb_ref[...]
        m_new = ADAM_B1 * m_ref[...] + (1.0 - ADAM_B1) * g
        v_new = ADAM_B2 * v_ref[...] + (1.0 - ADAM_B2) * (g * g)
        g_out[...] = g
        m_out[...] = m_new
        v_out[...] = v_new
        d_out[...] = -ADAM_LR * ((m_new * c1) / (jnp.sqrt(v_new * c2) + ADAM_EPS) + ADAM_WD * w_ref[...])

    spec = pl.BlockSpec((tr, tc), lambda i, j: (i, j))
    return pl.pallas_call(
        body, grid=(R // tr, L // tc), in_specs=[spec] * 5, out_specs=[spec] * 4,
        out_shape=[jax.ShapeDtypeStruct((R, L), f32)] * 4, name=name,
        compiler_params=_params("parallel", "parallel"))(w, ga, gb, m, v)


def _step(x, loss_target, P, M, V):
    shapes = {n: tuple(P[n].shape) for n in WEIGHTS}
    sh_shapes = [shapes[n] for n in SMALL_SHARDED]
    packed = SMALL_SHARDED + SMALL

    def whole(n, g):
        return g.reshape(-1, g.shape[2]) if n in ROW_SHARDED else jnp.concatenate([g[j] for j in range(N_POS)], axis=1)

    def slabs(G, n, dtype=f32):
        r, c = shapes[n]
        full = G[n].astype(dtype)
        if full.ndim == 3:
            return full
        return full.reshape(N_POS, r, c) if n in ROW_SHARDED else full.reshape(r, N_POS, c).transpose(1, 0, 2)

    u, g_w_in, g_small = _norm_gather("norm1_gather_w_in", x, P['norm1_g'].reshape(1, -1),
                                      [P['w_in'].astype(bf16), _pack128([P[n] for n in SMALL_SHARDED])], 512)
    W = {n: P[n] for n in SMALL}
    W['w_in_pad'] = _pad_w_in_shards([g_w_in[j] for j in range(N_POS)])
    per_pos = [_unpack128(g_small[j], sh_shapes) for j in range(N_POS)]
    for q, n in enumerate(SMALL_SHARDED):
        W[n] = jnp.concatenate([per_pos[j][q] for j in range(N_POS)], axis=1)
    groups = (('rwkv_proj', 'gdn_proj', 'ffn_up'), ('w_out', 'ffn_down'))
    late = dict(u=u, shards=[[P[n].astype(bf16) for n in grp] for grp in groups],
                assemble=lambda q, gathered: {n: whole(n, g) for n, g in zip(groups[q], gathered)},
                slabs=lambda G, q: [slabs(G, n, bf16) for n in groups[q]],
                w_in_slabs=lambda G: [jnp.stack([_unpad_cols(G['w_in_pad'], j * shapes['w_in'][1], (j + 1) * shapes['w_in'][1])
                                                 for j in range(N_POS)])])

    loss_rows, dx, G = _local_step(x, loss_target, W, late)
    arrived = {n: a for grp, got in zip(groups, G.pop('_arrived')) for n, a in zip(grp, got)}
    (arrived_w_in,) = G.pop('_arrived_w_in')
    G.pop('w_in_pad')

    small_slabs = jnp.stack([_pack128([slabs(G, n)[j] for n in SMALL_SHARDED] + [G[n] for n in SMALL]) for j in range(N_POS)])
    (arrived_small,) = _xy_exchange("scatter_small", [small_slabs], scatter=True)
    contributions = [arrived_w_in] + [arrived[n] for n in BIG[1:]] + [arrived_small]
    tags = list(BIG) + ['small']
    plane = [_sum_slots("sum_" + t, cbuf) for t, cbuf in zip(tags, contributions)]
    sibling = _sibling_exchange("sibling_grads", plane)

    out = {}
    names4 = ('grad', 'delta', 'new_m', 'new_v')
    for q, n in enumerate(BIG):
        tr = (lambda t: t.T) if n == 'w_in' else (lambda t: t)
        for tag, t in zip(names4, _adamw("adamw_" + n, tr(P[n]), tr(plane[q]), tr(sibling[q]), tr(M[n]), tr(V[n]))):
            out[tag + '_' + n] = tr(t)
    small_out = _adamw("adamw_small", _pack128([P[n] for n in packed]), plane[-1], sibling[-1],
                       _pack128([M[n] for n in packed]), _pack128([V[n] for n in packed]))
    for tag, buf in zip(names4, small_out):
        for n, t in zip(packed, _unpack128(buf, [shapes[n] for n in packed])):
            out[tag + '_' + n] = t
    loss = lax.psum(loss_rows[0, 0], ("x", "y", "c"))
    return loss, dx, out


def kernel(x, norm1_g, w_in, rwkv_mu, rwkv_w0, rwkv_w2, rwkv_a0, rwkv_a2, rwkv_g2, rwkv_k_k, rwkv_k_a, rwkv_r_k, rwkv_ln_w, rwkv_ln_b, rwkv_proj, gdn_conv_w, gdn_a_log, gdn_dt_bias, gdn_norm_w, gdn_proj, w_out, norm2_g, ffn_up, ffn_conv_w, ffn_down, final_g, loss_target, m_norm1_g, m_w_in, m_rwkv_mu, m_rwkv_w0, m_rwkv_w2, m_rwkv_a0, m_rwkv_a2, m_rwkv_g2, m_rwkv_k_k, m_rwkv_k_a, m_rwkv_r_k, m_rwkv_ln_w, m_rwkv_ln_b, m_rwkv_proj, m_gdn_conv_w, m_gdn_a_log, m_gdn_dt_bias, m_gdn_norm_w, m_gdn_proj, m_w_out, m_norm2_g, m_ffn_up, m_ffn_conv_w, m_ffn_down, m_final_g, v_norm1_g, v_w_in, v_rwkv_mu, v_rwkv_w0, v_rwkv_w2, v_rwkv_a0, v_rwkv_a2, v_rwkv_g2, v_rwkv_k_k, v_rwkv_k_a, v_rwkv_r_k, v_rwkv_ln_w, v_rwkv_ln_b, v_rwkv_proj, v_gdn_conv_w, v_gdn_a_log, v_gdn_dt_bias, v_gdn_norm_w, v_gdn_proj, v_w_out, v_norm2_g, v_ffn_up, v_ffn_conv_w, v_ffn_down, v_final_g):
    weights = (norm1_g, w_in, rwkv_mu, rwkv_w0, rwkv_w2, rwkv_a0, rwkv_a2, rwkv_g2, rwkv_k_k, rwkv_k_a, rwkv_r_k, rwkv_ln_w,
               rwkv_ln_b, rwkv_proj, gdn_conv_w, gdn_a_log, gdn_dt_bias, gdn_norm_w, gdn_proj, w_out, norm2_g, ffn_up,
               ffn_conv_w, ffn_down, final_g)
    m_in = (m_norm1_g, m_w_in, m_rwkv_mu, m_rwkv_w0, m_rwkv_w2, m_rwkv_a0, m_rwkv_a2, m_rwkv_g2, m_rwkv_k_k, m_rwkv_k_a,
            m_rwkv_r_k, m_rwkv_ln_w, m_rwkv_ln_b, m_rwkv_proj, m_gdn_conv_w, m_gdn_a_log, m_gdn_dt_bias, m_gdn_norm_w,
            m_gdn_proj, m_w_out, m_norm2_g, m_ffn_up, m_ffn_conv_w, m_ffn_down, m_final_g)
    v_in = (v_norm1_g, v_w_in, v_rwkv_mu, v_rwkv_w0, v_rwkv_w2, v_rwkv_a0, v_rwkv_a2, v_rwkv_g2, v_rwkv_k_k, v_rwkv_k_a,
            v_rwkv_r_k, v_rwkv_ln_w, v_rwkv_ln_b, v_rwkv_proj, v_gdn_conv_w, v_gdn_a_log, v_gdn_dt_bias, v_gdn_norm_w,
            v_gdn_proj, v_w_out, v_norm2_g, v_ffn_up, v_ffn_conv_w, v_ffn_down, v_final_g)
    drop = lambda n, a: a if n == 'final_g' else a[0]
    P = {n: drop(n, a) for n, a in zip(WEIGHTS, weights)}
    M = {n: drop(n, a) for n, a in zip(WEIGHTS, m_in)}
    V = {n: drop(n, a) for n, a in zip(WEIGHTS, v_in)}
    loss, dx, out = _step(x[0], loss_target[0], P, M, V)
    lift = lambda n, a: a if n == 'final_g' else a[None]
    res = [loss, dx[None]]
    for tag in ('grad', 'delta', 'new_m', 'new_v'):
        res += [lift(n, out[tag + '_' + n]) for n in WEIGHTS]
    return tuple(res)
```

```python
import functools

import jax
import jax.numpy as jnp
from jax import lax
from jax.experimental import pallas as pl
from jax.experimental.pallas import tpu as pltpu

f32 = jnp.float32
bf16 = jnp.bfloat16

D_MODEL = 1024
RWKV_HEADS, RWKV_HD, RWKV_W = 8, 64, 512
GDN_HEADS, GDN_HD, GDN_W = 4, 128, 512
NORM_EPS, L2_EPS, GN_EPS = 1e-6, 1e-6, 64e-5
W_AB = 256
OFF_QKV, OFF_Z, OFF_GATES, OFF_AB = 1792, 3328, 3840, 5888
W_IN_PAD = OFF_AB + W_AB
WKV_CHUNK, WKV_PER_STEP = 64, 4
GDN_CHUNK, GDN_PER_STEP = 128, 4
HALO = 8
LANES = 128
TILE_BYTES = 1 << 20
VMEM_LIMIT = 56 * 1024 * 1024

ADAM_LR, ADAM_B1, ADAM_B2, ADAM_EPS, ADAM_WD, ADAM_STEP = 0.001, 0.9, 0.999, 1e-08, 0.01, 10

ROW_SHARDED = ('w_out', 'ffn_down')
SMALL = ('norm1_g', 'rwkv_mu', 'rwkv_w0', 'rwkv_a0', 'rwkv_k_k', 'rwkv_k_a', 'rwkv_r_k', 'rwkv_ln_w', 'rwkv_ln_b',
         'gdn_a_log', 'gdn_dt_bias', 'gdn_norm_w', 'norm2_g', 'final_g')
WEIGHTS = ('norm1_g', 'w_in', 'rwkv_mu', 'rwkv_w0', 'rwkv_w2', 'rwkv_a0', 'rwkv_a2', 'rwkv_g2', 'rwkv_k_k', 'rwkv_k_a',
           'rwkv_r_k', 'rwkv_ln_w', 'rwkv_ln_b', 'rwkv_proj', 'gdn_conv_w', 'gdn_a_log', 'gdn_dt_bias', 'gdn_norm_w',
           'gdn_proj', 'w_out', 'norm2_g', 'ffn_up', 'ffn_conv_w', 'ffn_down', 'final_g')


def _params(*sem):
    return pltpu.CompilerParams(dimension_semantics=sem, vmem_limit_bytes=VMEM_LIMIT)


def _tile(n, limit):
    if n <= limit:
        return n
    best = None
    for d in range(128, limit + 1, 128):
        if n % d == 0:
            best = d
    if best is None:
        raise ValueError(f"no tile for {n} under {limit}")
    return best


MM_BLOCK_BYTES = 6 << 20
MM_MAX_COLS = 1536


def _mm(a, b, mode, name, add=None, out_dtype=f32, side=None, col_slabs=None):
    if mode == 'nn':
        (M, K), N = a.shape, b.shape[1]
    elif mode == 'nt':
        (M, K), N = a.shape, b.shape[0]
    else:
        (K, M), N = a.shape, b.shape[1]
    tm = _tile(M, 1408)
    tk = _tile(K, min(2816, MM_BLOCK_BYTES // (tm * a.dtype.itemsize)))
    tn = _tile(N, max(128, min(MM_BLOCK_BYTES // (tk * b.dtype.itemsize), MM_BLOCK_BYTES // (tm * 4), MM_MAX_COLS) // 128 * 128))
    if col_slabs is not None:
        tn = N // col_slabs
    nk = K // tk
    grid = (M // tm, N // tn, nk)
    dn = {'nn': (((1,), (0,)), ((), ())), 'nt': (((1,), (1,)), ((), ())), 'tn': (((0,), (0,)), ((), ()))}[mode]
    n_add = 0 if add is None else 1
    n_side = 0 if side is None else len(side[0])

    def body(a_ref, b_ref, *rest):
        add_ref = rest[0] if add is not None else None
        side_in, rest = rest[n_add:n_add + n_side], rest[n_add + n_side:]
        o_ref, side_out, rest = rest[0], rest[1:1 + n_side], rest[1 + n_side:]
        acc_ref, rest = (rest[0], rest[1:]) if nk > 1 else (None, rest)
        ids = [pl.program_id(d) for d in range(3)]
        if side is not None:
            start, finish = _xy_copies(side_in, side_out, rest, side[1])
            pl.when((ids[0] == 0) & (ids[1] == 0) & (ids[2] == 0))(start)
        acc = lax.dot_general(a_ref[...].astype(bf16), b_ref[...].astype(bf16), dn, preferred_element_type=f32)
        if nk == 1:
            o_ref[...] = (acc + add_ref[...] if add is not None else acc).astype(out_dtype)
        else:
            k = ids[2]

            @pl.when(k == 0)
            def _():
                acc_ref[...] = acc + add_ref[...] if add is not None else acc

            @pl.when(k > 0)
            def _():
                acc_ref[...] += acc

            @pl.when(k == nk - 1)
            def _():
                o_ref[...] = acc_ref[...].astype(out_dtype)
        if side is not None:
            pl.when((ids[0] == grid[0] - 1) & (ids[1] == grid[1] - 1) & (ids[2] == nk - 1))(finish)

    a_spec = (pl.BlockSpec((tk, tm), lambda i, j, k: (k, i)) if mode == 'tn'
              else pl.BlockSpec((tm, tk), lambda i, j, k: (i, k)))
    b_spec = (pl.BlockSpec((tn, tk), lambda i, j, k: (j, k)) if mode == 'nt'
              else pl.BlockSpec((tk, tn), lambda i, j, k: (k, j)))
    o_spec = pl.BlockSpec((tm, tn), lambda i, j, k: (i, j))
    o_shape = jax.ShapeDtypeStruct((M, N), out_dtype)
    if col_slabs is not None:
        o_spec = pl.BlockSpec((None, tm, tn), lambda i, j, k: (j, i, 0))
        o_shape = jax.ShapeDtypeStruct((col_slabs, M, tn), out_dtype)
    any_spec = pl.BlockSpec(memory_space=pl.ANY)
    side_bufs = [] if side is None else list(side[0])
    ins, specs = [a, b], [a_spec, b_spec]
    if add is not None:
        ins.append(add)
        specs.append(o_spec)
    outs = pl.pallas_call(
        body, grid=grid, in_specs=specs + [any_spec] * n_side, out_specs=[o_spec] + [any_spec] * n_side,
        out_shape=[o_shape] + (_xy_out_shapes(side_bufs, side[1]) if side is not None else []),
        scratch_shapes=([pltpu.VMEM((tm, tn), f32)] if nk > 1 else []) + (_xy_sems(n_side, side[1]) if side is not None else []),
        name=name,
        compiler_params=_params(*(("arbitrary",) * 3 if side is not None else ("parallel", "parallel", "arbitrary"))))(
            *ins, *side_bufs)
    return list(outs) if side is not None else outs[0]


PARTS_TILE = 512
PARTS_DEPTH = 1024


def _mm_nt_parts(pairs, name, side=None):
    n, M, N = len(pairs), pairs[0][0].shape[0], pairs[0][1].shape[0]
    tm = min(M, PARTS_TILE)
    steps = M // tm
    n_side = 0 if side is None else len(side[0])

    def body(*refs):
        a_refs, b_refs, side_in = refs[:n], refs[n:2 * n], refs[2 * n:2 * n + n_side]
        o_ref, side_out, sems = refs[2 * n + n_side], refs[2 * n + n_side + 1:2 * n + 2 * n_side + 1], refs[2 * n + 2 * n_side + 1:]
        i = pl.program_id(0)
        if side is not None:
            start, finish = _xy_copies(side_in, side_out, sems, side[1])
            pl.when(i == 0)(start)
        acc = None
        for a_ref, b_ref in zip(a_refs, b_refs):
            part = lax.dot_general(a_ref[...].astype(bf16), b_ref[...].astype(bf16), (((1,), (1,)), ((), ())),
                                   preferred_element_type=f32)
            acc = part if acc is None else acc + part
        o_ref[...] = acc
        if side is not None:
            pl.when(i == steps - 1)(finish)

    any_spec = pl.BlockSpec(memory_space=pl.ANY)
    side_bufs = [] if side is None else list(side[0])
    o_spec = pl.BlockSpec((tm, N), lambda i: (i, 0))
    outs = pl.pallas_call(
        body, grid=(steps,),
        in_specs=[pl.BlockSpec((tm, a.shape[1]), lambda i: (i, 0)) for a, _, _ in pairs]
        + [pl.BlockSpec((N, a.shape[1]), lambda i, col=col: (0, col)) for a, _, col in pairs] + [any_spec] * n_side,
        out_specs=[o_spec] + [any_spec] * n_side,
        out_shape=[jax.ShapeDtypeStruct((M, N), f32)] + (_xy_out_shapes(side_bufs, side[1]) if side is not None else []),
        scratch_shapes=_xy_sems(n_side, side[1]) if side is not None else [],
        name=name, compiler_params=_params("arbitrary" if side is not None else "parallel"))(
            *[a for a, _, _ in pairs], *[b for _, b, _ in pairs], *side_bufs)
    return list(outs) if side is not None else outs[0]


def _mm_tn_parts(a, parts, name):
    n, (K, M) = len(parts), a.shape
    tm, tk = min(M, PARTS_TILE), min(K, PARTS_DEPTH)
    nk = K // tk
    widths = [p.shape[1] for p in parts]
    offs = [sum(widths[:q]) for q in range(n)]

    def body(a_ref, *refs):
        p_refs, o_refs, acc_ref = refs[:n], refs[n:2 * n], refs[2 * n]
        k = pl.program_id(1)
        lhs = a_ref[...].astype(bf16)
        for p_ref, o_ref, off, w in zip(p_refs, o_refs, offs, widths):
            part = lax.dot_general(lhs, p_ref[...].astype(bf16), (((0,), (0,)), ((), ())), preferred_element_type=f32)
            if nk == 1:
                o_ref[...] = part.astype(bf16)
                continue
            cols = (slice(None), slice(off, off + w))

            @pl.when(k == 0)
            def _():
                acc_ref[cols] = part

            @pl.when(k > 0)
            def _():
                acc_ref[cols] += part

            @pl.when(k == nk - 1)
            def _():
                o_ref[...] = acc_ref[cols].astype(bf16)

    return pl.pallas_call(
        body, grid=(M // tm, nk),
        in_specs=[pl.BlockSpec((tk, tm), lambda i, k: (k, i))] + [pl.BlockSpec((tk, w), lambda i, k: (k, 0)) for w in widths],
        out_specs=[pl.BlockSpec((tm, w), lambda i, k: (i, 0)) for w in widths],
        out_shape=[jax.ShapeDtypeStruct((M, w), bf16) for w in widths],
        scratch_shapes=[pltpu.VMEM((tm, sum(widths)), f32)],
        name=name, compiler_params=_params("parallel", "arbitrary"))(a, *parts)


def _shift_down(cur, prev, s):
    if s == 0:
        return cur
    ext = jnp.concatenate([prev, cur], axis=0)
    return pltpu.roll(ext, s, 0)[HALO:]


def _shift_up(cur, nxt, s):
    if s == 0:
        return cur
    ext = jnp.concatenate([cur, nxt], axis=0)
    return pltpu.roll(ext, ext.shape[0] - s, 0)[:cur.shape[0]]


def _conv_apply(cur, prev, w_ref, shifted=None):
    taps = w_ref.shape[0]
    out = None
    for i in range(taps):
        s = taps - 1 - i
        term = (shifted[s] if shifted is not None else _shift_down(cur, prev, s)) * w_ref[pl.ds(i, 1), :]
        out = term if out is None else out + term
    return out


def _row_spec(tm, w, col=0):
    return pl.BlockSpec((tm, w), lambda i: (i, col))


def _cols(a, width, col):
    return (a, width, col)


def _row_of(r):
    return r if isinstance(r, tuple) else (r, r.shape[1], 0)


def _prev_spec(tm, w):
    return pl.BlockSpec((HALO, w), lambda i: (jnp.maximum(i * (tm // HALO) - 1, 0), 0))


def _next_spec(tm, w, T):
    return pl.BlockSpec((HALO, w), lambda i: (jnp.minimum((i + 1) * (tm // HALO), T // HALO - 1), 0))


def _full_spec(shape):
    return pl.BlockSpec(shape, lambda i: (0,) * len(shape))


def _pw_fwd(name, fn, rows, consts, out_widths, tm, conv_w=None, out_dtype=f32):
    T = _row_of(rows[0])[0].shape[0]
    nr, nc = len(rows), len(consts)

    def body(*refs):
        i = pl.program_id(0)
        vals = [r[...] for r in refs[:nr]]
        p = nr
        if conv_w is not None:
            prev = jnp.where(i > 0, refs[p][...], 0.0)
            vals[0] = _conv_apply(vals[0], prev, refs[p + 1])
            p += 2
        cvals = [r[...] for r in refs[p:p + nc]]
        outs = fn(*vals, *cvals)
        for o_ref, o in zip(refs[p + nc:], outs):
            o_ref[...] = o.astype(out_dtype)

    ins = [_row_of(r)[0] for r in rows]
    specs = [_row_spec(tm, *_row_of(r)[1:]) for r in rows]
    if conv_w is not None:
        ins += [rows[0], conv_w]
        specs += [_prev_spec(tm, rows[0].shape[1]), _full_spec(conv_w.shape)]
    ins += list(consts)
    specs += [_full_spec(c.shape) for c in consts]
    outs = pl.pallas_call(
        body, grid=(T // tm,), in_specs=specs,
        out_specs=[_row_spec(tm, w) for w in out_widths],
        out_shape=[jax.ShapeDtypeStruct((T, w), out_dtype) for w in out_widths], name=name,
        compiler_params=_params("parallel"))(*ins)
    return outs


def _pw_bwd(name, fn, rows, consts, cots, tm, add_to_first=None, row_dtypes=None):
    rows = [_row_of(r) for r in rows]
    T = rows[0][0].shape[0]
    nr, nc = len(rows), len(consts)
    flat_cots = [c for grp in cots for c in grp]
    row_dtypes = row_dtypes or [f32] * nr
    n_extra = 0 if add_to_first is None else 1

    def body(*refs):
        i = pl.program_id(0)
        in_refs, cot_refs = refs[:nr + nc], refs[nr + nc:nr + nc + len(flat_cots)]
        extra_ref = refs[nr + nc + len(flat_cots)] if add_to_first is not None else None
        row_out = refs[nr + nc + len(flat_cots) + n_extra:][:nr]
        const_out = refs[nr + nc + len(flat_cots) + n_extra + nr:]

        @pl.when(i == 0)
        def _():
            for q in range(nc):
                const_out[q][...] = jnp.zeros_like(const_out[q])

        def part(sl):
            cot_vals, p = [], 0
            for grp in cots:
                acc = cot_refs[p][:, sl]
                for q in range(1, len(grp)):
                    acc = acc + cot_refs[p + q][:, sl]
                p += len(grp)
                cot_vals.append(acc)
            _, vjp = jax.vjp(fn, *[r[:, sl] for r in in_refs])
            grads = vjp(tuple(cot_vals))
            for q in range(nr):
                g = grads[q]
                if q == 0 and extra_ref is not None:
                    g = g + extra_ref[:, sl]
                row_out[q][:, sl] = g.astype(row_dtypes[q])
            for q in range(nc):
                const_out[q][:, sl] += grads[nr + q]

        part(slice(None))

    ins = [r[0] for r in rows] + list(consts) + flat_cots
    specs = ([_row_spec(tm, r[1], r[2]) for r in rows] + [_full_spec(c.shape) for c in consts]
             + [_row_spec(tm, c.shape[1]) for c in flat_cots])
    if add_to_first is not None:
        ins.append(add_to_first)
        specs.append(_row_spec(tm, add_to_first.shape[1]))
    out_shapes = ([jax.ShapeDtypeStruct((T, r[1]), d) for r, d in zip(rows, row_dtypes)]
                  + [jax.ShapeDtypeStruct(c.shape, f32) for c in consts])
    out_specs = [_row_spec(tm, r[1]) for r in rows] + [_full_spec(c.shape) for c in consts]
    outs = pl.pallas_call(
        body, grid=(T // tm,), in_specs=specs, out_specs=out_specs, out_shape=out_shapes, name=name,
        compiler_params=_params("arbitrary"))(*ins)
    return list(outs[:nr]), list(outs[nr:])


def _pw_conv_bwd(name, fn, rows, consts, cots, conv_w, tm, row_dtypes=None):
    T, W0 = rows[0].shape
    nr, nc = len(rows), len(consts)
    taps = conv_w.shape[0]
    nblk = T // tm
    flat_cots = [c for grp in cots for c in grp]
    row_dtypes = row_dtypes or [f32] * nr

    def body(*refs):
        i = pl.program_id(0)
        p = 0
        cur = [r[...] for r in refs[p:p + nr]]; p += nr
        nxt = [r[...] for r in refs[p:p + nr]]; p += nr
        prev = jnp.where(i > 0, refs[p][...], 0.0); p += 1
        w_ref = refs[p]; p += 1
        cvals = [r[...] for r in refs[p:p + nc]]; p += nc

        def summed(p0):
            out, q = [], p0
            for grp in cots:
                acc = refs[q][...]
                for t in range(1, len(grp)):
                    acc = acc + refs[q + t][...]
                q += len(grp)
                out.append(acc)
            return out, q

        cot_cur, p = summed(p)
        cot_nxt, p = summed(p)
        row_out, dw_ref, const_out = refs[p:p + nr], refs[p + nr], refs[p + nr + 1:]

        x_cur = cur[0]
        x_down = [_shift_down(x_cur, prev, s_) for s_ in range(taps)]
        _, vjp = jax.vjp(fn, _conv_apply(x_cur, prev, w_ref, x_down), *cur[1:], *cvals)
        grads = vjp(tuple(cot_cur))
        _, vjp_n = jax.vjp(fn, _conv_apply(nxt[0], x_cur[tm - HALO:], w_ref), *nxt[1:], *cvals)
        dc_n = jnp.where(i < nblk - 1, vjp_n(tuple(cot_nxt))[0], 0.0)
        dc = grads[0]

        @pl.when(i == 0)
        def _():
            dw_ref[...] = jnp.zeros_like(dw_ref)
            for q in range(nc):
                const_out[q][...] = jnp.zeros_like(const_out[q])

        dx = None
        for k in range(taps):
            s_ = taps - 1 - k
            term = _shift_up(dc, dc_n, s_) * w_ref[pl.ds(k, 1), :]
            dx = term if dx is None else dx + term
            dw_ref[pl.ds(k, 1), :] += jnp.sum(dc * x_down[s_], axis=0, keepdims=True)
        row_out[0][...] = dx.astype(row_dtypes[0])
        for q in range(1, nr):
            row_out[q][...] = grads[q].astype(row_dtypes[q])
        for q in range(nc):
            const_out[q][...] += grads[nr + q]

    ins = list(rows) + list(rows) + [rows[0], conv_w] + list(consts) + flat_cots + flat_cots
    specs = ([_row_spec(tm, r.shape[1]) for r in rows] + [_next_spec(tm, r.shape[1], T) for r in rows]
             + [_prev_spec(tm, W0), _full_spec(conv_w.shape)] + [_full_spec(c.shape) for c in consts]
             + [_row_spec(tm, c.shape[1]) for c in flat_cots] + [_next_spec(tm, c.shape[1], T) for c in flat_cots])
    out_shapes = ([jax.ShapeDtypeStruct(r.shape, d) for r, d in zip(rows, row_dtypes)]
                  + [jax.ShapeDtypeStruct(conv_w.shape, f32)] + [jax.ShapeDtypeStruct(c.shape, f32) for c in consts])
    out_specs = ([_row_spec(tm, r.shape[1]) for r in rows] + [_full_spec(conv_w.shape)]
                 + [_full_spec(c.shape) for c in consts])
    outs = pl.pallas_call(
        body, grid=(nblk,), in_specs=specs, out_specs=out_specs, out_shape=out_shapes, name=name,
        compiler_params=_params("arbitrary"))(*ins)
    return list(outs[:nr]), outs[nr], list(outs[nr + 1:])


def _sigmoid(x):
    return 0.5 * jnp.tanh(0.5 * x) + 0.5


def _softplus(x):
    return jnp.maximum(x, 0.0) + jnp.log(1.0 + jnp.exp(jnp.minimum(x, -x)))


def _seg_sum_impl(x, seg):
    w = x.shape[-1]
    r = lax.broadcasted_iota(jnp.int32, (w, w), 0) // seg
    c = lax.broadcasted_iota(jnp.int32, (w, w), 1) // seg
    ones = (r == c).astype(bf16)
    hi = x.astype(bf16)
    lo = (x - hi.astype(f32)).astype(bf16)
    return (jnp.dot(hi, ones, preferred_element_type=f32) + jnp.dot(lo, ones, preferred_element_type=f32))


@functools.partial(jax.custom_vjp, nondiff_argnums=(1,))
def _seg_sum(x, seg):
    return _seg_sum_impl(x, seg)


_seg_sum.defvjp(lambda x, seg: (_seg_sum_impl(x, seg), None), lambda seg, _, g: (_seg_sum_impl(g, seg),))


def _rms(x, g):
    return x * lax.rsqrt(jnp.mean(x * x, axis=-1, keepdims=True) + NORM_EPS) * g


def _rms_fn(x, g):
    return (_rms(x, g),)


def _loss_rows(x2, tgt, g):
    e = _rms(x2, g) - tgt
    return 0.5 * jnp.sum(e * e, axis=-1, keepdims=True) * (1.0 / D_MODEL)


@jax.custom_vjp
def _dot_lo(a, b):
    return jnp.dot(a.astype(bf16), b.astype(bf16), preferred_element_type=f32)


def _dot_lo_bwd(ab, g):
    a, b = ab
    gl = g.astype(bf16)
    return (lax.dot_general(gl, b.astype(bf16), (((1,), (1,)), ((), ())), preferred_element_type=f32),
            lax.dot_general(a.astype(bf16), gl, (((0,), (0,)), ((), ())), preferred_element_type=f32))


_dot_lo.defvjp(lambda a, b: (_dot_lo(a, b), (a, b)), _dot_lo_bwd)


def _rwkv_prep_fn(ps, w0, w2p, a0, a2p, g2, k_k, k_a):
    r, k, v = ps[:, 0:512], ps[:, 512:1024], ps[:, 1024:1536]
    wa, gl = ps[:, 1536:1664], ps[:, 1664:1792]
    z = w0 + _dot_lo(jnp.tanh(wa), w2p)
    w_log = -_softplus(-z) - 0.5
    lw = -jnp.exp(w_log)
    a = _sigmoid(a0 + _dot_lo(wa, a2p))
    g = _dot_lo(_sigmoid(gl), g2)
    kx = k * k_k
    kk = kx * lax.rsqrt(_seg_sum(kx * kx, RWKV_HD) + L2_EPS)
    k2 = k * (1.0 + (a - 1.0) * k_a)
    return r, lw, k2, v, -kk, kk * a, g


def _rwkv_post_fn(y, r, k2, v, g, ln_w, ln_b, rk):
    mean = _seg_sum(y, RWKV_HD) * (1.0 / RWKV_HD)
    yc = y - mean
    var = _seg_sum(yc * yc, RWKV_HD) * (1.0 / RWKV_HD)
    yn = yc * lax.rsqrt(var + GN_EPS) * ln_w + ln_b
    bonus = _seg_sum(r * k2 * rk, RWKV_HD) * v
    return ((yn + bonus) * g,)


def _gdn_prep_fn(cq, ck, cv):
    silu = lambda c: c * _sigmoid(c)
    q, k = silu(cq), silu(ck)
    q = q * lax.rsqrt(jnp.sum(q * q, axis=-1, keepdims=True) + L2_EPS) * (GDN_HD ** -0.5)
    k = k * lax.rsqrt(jnp.sum(k * k, axis=-1, keepdims=True) + L2_EPS)
    return q, k, silu(cv)


def _gdn_gate_fn(ab, al_p, dt_p):
    lane = lax.broadcasted_iota(jnp.int32, ab.shape, 1)
    gpart = -jnp.exp(al_p) * _softplus(ab + dt_p)
    return (jnp.where(lane < GDN_HEADS, gpart, jnp.where(lane < 2 * GDN_HEADS, _sigmoid(ab), 0.0)),)


def _gdn_post_fn(o, z, nw):
    ms = _seg_sum(o * o, GDN_HD) * (1.0 / GDN_HD)
    return (o * lax.rsqrt(ms + NORM_EPS) * nw * (z * _sigmoid(z)),)


def _mix_fn(ga, gb, ya, yb):
    return (_sigmoid(ga) * ya + _sigmoid(gb) * yb,)


STRIP = 128


def _strip_conv(ref, prev_ref, w_ref, sl, first, taps):
    cur = ref[:, sl]
    prev = jnp.where(first, 0.0, prev_ref[:, sl])
    down = [_shift_down(cur, prev, s) for s in range(taps)]
    conv = None
    for k in range(taps):
        term = down[taps - 1 - k] * w_ref[pl.ds(k, 1), sl]
        conv = term if conv is None else conv + term
    return cur, down, conv


def _group_fwd(name, fn, x, w, shared_cols, group_cols, consts, n_out, tm):
    T, W = x.shape
    taps = w.shape[0]
    n_groups = len(group_cols)
    nc = len(consts)

    def body(x_ref, xp_ref, w_ref, *refs):
        const_refs, out_refs = refs[:nc], refs[nc:]
        first = pl.program_id(0) == 0
        shared = [_strip_conv(x_ref, xp_ref, w_ref, sl, first, taps)[2] for sl in shared_cols]
        for j, cols in enumerate(group_cols):
            sl = slice(STRIP * j, STRIP * (j + 1))
            convs = [_strip_conv(x_ref, xp_ref, w_ref, c, first, taps)[2] for c in cols]
            outs = fn(*convs, *shared, *[c[:, sl] for c in const_refs])
            for o_ref, o in zip(out_refs, outs):
                o_ref[:, sl] = o

    return pl.pallas_call(
        body, grid=(T // tm,),
        in_specs=[_row_spec(tm, W), _prev_spec(tm, W), _full_spec(w.shape)] + [_full_spec(c.shape) for c in consts],
        out_specs=[_row_spec(tm, STRIP * n_groups)] * n_out,
        out_shape=[jax.ShapeDtypeStruct((T, STRIP * n_groups), f32)] * n_out, name=name,
        compiler_params=_params("parallel"))(x, x, w, *consts)


def _group_bwd(name, fn, x, w, shared_cols, group_cols, consts, cots, tm):
    T, W = x.shape
    taps = w.shape[0]
    nblk = T // tm
    nc, ns = len(consts), len(shared_cols)
    flat_cots = [c for grp in cots for c in grp]
    n_cot = len(flat_cots)

    def body(x_ref, xp_ref, xn_ref, w_ref, *refs):
        const_refs, refs = refs[:nc], refs[nc:]
        cot_refs, cotn_refs, refs = refs[:n_cot], refs[n_cot:2 * n_cot], refs[2 * n_cot:]
        dx_ref, dw_ref, const_out = refs[0], refs[1], refs[2:]
        i = pl.program_id(0)
        first, last = i == 0, i == nblk - 1

        @pl.when(first)
        def _():
            dw_ref[...] = jnp.zeros_like(dw_ref)
            for q in range(nc):
                const_out[q][...] = jnp.zeros_like(const_out[q])

        def convs_of(sl):
            cur, down, conv = _strip_conv(x_ref, xp_ref, w_ref, sl, first, taps)
            nxt, conv_n = xn_ref[:, sl], None
            for k in range(taps):
                term = _shift_down(nxt, cur[tm - HALO:], taps - 1 - k) * w_ref[pl.ds(k, 1), sl]
                conv_n = term if conv_n is None else conv_n + term
            return down, conv, conv_n

        def conv_back(sl, down, dc, dc_n):
            dx = None
            for k in range(taps):
                s_ = taps - 1 - k
                term = _shift_up(dc, dc_n, s_) * w_ref[pl.ds(k, 1), sl]
                dx = term if dx is None else dx + term
                dw_ref[pl.ds(k, 1), sl] += jnp.sum(dc * down[s_], axis=0, keepdims=True)
            dx_ref[:, sl] = dx.astype(dx_ref.dtype)

        def summed(refs_, sl, mask):
            out, p = [], 0
            for grp in cots:
                acc = refs_[p][:, sl]
                for t in range(1, len(grp)):
                    acc = acc + refs_[p + t][:, sl]
                p += len(grp)
                out.append(jnp.where(last, 0.0, acc) if mask else acc)
            return tuple(out)

        shared = [convs_of(sl) for sl in shared_cols]
        d_shared, d_shared_n = [None] * ns, [None] * ns
        for j, cols in enumerate(group_cols):
            sl = slice(STRIP * j, STRIP * (j + 1))
            mine = [convs_of(c) for c in cols]
            cj = [c[:, sl] for c in const_refs]
            _, vjp = jax.vjp(fn, *[m[1] for m in mine], *[m[1] for m in shared], *cj)
            grads = vjp(summed(cot_refs, sl, False))
            _, vjp_n = jax.vjp(fn, *[m[2] for m in mine], *[m[2] for m in shared], *cj)
            grads_n = vjp_n(summed(cotn_refs, sl, True))
            for q, c in enumerate(cols):
                conv_back(c, mine[q][0], grads[q], grads_n[q])
            for q in range(ns):
                g, gn = grads[len(cols) + q], grads_n[len(cols) + q]
                d_shared[q] = g if d_shared[q] is None else d_shared[q] + g
                d_shared_n[q] = gn if d_shared_n[q] is None else d_shared_n[q] + gn
            for q in range(nc):
                const_out[q][:, sl] += grads[len(cols) + ns + q]
        for q, c in enumerate(shared_cols):
            conv_back(c, shared[q][0], d_shared[q], d_shared_n[q])

    outs = pl.pallas_call(
        body, grid=(nblk,),
        in_specs=[_row_spec(tm, W), _prev_spec(tm, W), _next_spec(tm, W, T), _full_spec(w.shape)]
        + [_full_spec(c.shape) for c in consts] + [_row_spec(tm, c.shape[1]) for c in flat_cots]
        + [_next_spec(tm, c.shape[1], T) for c in flat_cots],
        out_specs=[_row_spec(tm, W), _full_spec(w.shape)] + [_full_spec(c.shape) for c in consts],
        out_shape=[jax.ShapeDtypeStruct((T, W), bf16), jax.ShapeDtypeStruct(w.shape, f32)]
        + [jax.ShapeDtypeStruct(c.shape, f32) for c in consts], name=name,
        compiler_params=_params("arbitrary"))(x, x, x, w, *consts, *flat_cots, *flat_cots)
    return outs[0], outs[1], list(outs[2:])


def _ffn_strip_fn(cg, cu):
    return cg * _sigmoid(cg) * cu


def _ffn_act_fwd(h, w, tm):
    T, W2 = h.shape
    H = W2 // 2
    taps = w.shape[0]

    def body(h_ref, hp_ref, w_ref, o_ref):
        first = pl.program_id(0) == 0
        for j in range(H // STRIP):
            gs, us = slice(STRIP * j, STRIP * (j + 1)), slice(H + STRIP * j, H + STRIP * (j + 1))
            cg = _strip_conv(h_ref, hp_ref, w_ref, gs, first, taps)[2]
            cu = _strip_conv(h_ref, hp_ref, w_ref, us, first, taps)[2]
            o_ref[:, gs] = _ffn_strip_fn(cg, cu).astype(o_ref.dtype)

    return pl.pallas_call(
        body, grid=(T // tm,), in_specs=[_row_spec(tm, W2), _prev_spec(tm, W2), _full_spec(w.shape)],
        out_specs=_row_spec(tm, H), out_shape=jax.ShapeDtypeStruct((T, H), bf16), name="ffn_act",
        compiler_params=_params("parallel"))(h, h, w)


def _ffn_act_bwd(h, dact, w, tm):
    T, W2 = h.shape
    H = W2 // 2
    taps = w.shape[0]
    nblk = T // tm

    def body(h_ref, hp_ref, hn_ref, d_ref, dn_ref, w_ref, dh_ref, dw_ref):
        i = pl.program_id(0)
        first, last = i == 0, i == nblk - 1

        @pl.when(first)
        def _():
            dw_ref[...] = jnp.zeros_like(dw_ref)

        for j in range(H // STRIP):
            gs, us = slice(STRIP * j, STRIP * (j + 1)), slice(H + STRIP * j, H + STRIP * (j + 1))
            parts = {}
            for name, sl in (('g', gs), ('u', us)):
                cur, down, conv = _strip_conv(h_ref, hp_ref, w_ref, sl, first, taps)
                nxt = hn_ref[:, sl]
                conv_n = None
                for k in range(taps):
                    term = _shift_down(nxt, cur[tm - HALO:], taps - 1 - k) * w_ref[pl.ds(k, 1), sl]
                    conv_n = term if conv_n is None else conv_n + term
                parts[name] = (down, conv, conv_n)
            _, vjp = jax.vjp(_ffn_strip_fn, parts['g'][1], parts['u'][1])
            dcs = vjp(d_ref[:, gs])
            _, vjp_n = jax.vjp(_ffn_strip_fn, parts['g'][2], parts['u'][2])
            dcs_n = vjp_n(jnp.where(last, 0.0, dn_ref[:, gs]))
            for (name, sl), dc, dc_n in zip((('g', gs), ('u', us)), dcs, dcs_n):
                down = parts[name][0]
                dx = None
                for k in range(taps):
                    s_ = taps - 1 - k
                    term = _shift_up(dc, dc_n, s_) * w_ref[pl.ds(k, 1), sl]
                    dx = term if dx is None else dx + term
                    dw_ref[pl.ds(k, 1), sl] += jnp.sum(dc * down[s_], axis=0, keepdims=True)
                dh_ref[:, sl] = dx.astype(dh_ref.dtype)

    return pl.pallas_call(
        body, grid=(nblk,),
        in_specs=[_row_spec(tm, W2), _prev_spec(tm, W2), _next_spec(tm, W2, T), _row_spec(tm, H), _next_spec(tm, H, T),
                  _full_spec(w.shape)],
        out_specs=[_row_spec(tm, W2), _full_spec(w.shape)],
        out_shape=[jax.ShapeDtypeStruct((T, W2), bf16), jax.ShapeDtypeStruct(w.shape, f32)], name="ffn_act_bwd",
        compiler_params=_params("arbitrary"))(h, h, h, dact, dact, w)


N_POS = 4


def _xy_out_shapes(bufs, scatter):
    return [jax.ShapeDtypeStruct((N_POS,) + tuple(b.shape[1:] if scatter else b.shape), b.dtype) for b in bufs]


def _xy_sems(n, scatter):
    sems = [pltpu.SemaphoreType.DMA((3 * n,)), pltpu.SemaphoreType.DMA((3 * n,)), pltpu.SemaphoreType.DMA((n,))]
    return sems if scatter else sems + [pltpu.SemaphoreType.DMA((3 * n,)), pltpu.SemaphoreType.DMA((3 * n,))]


def _xy_copies(in_refs, out_refs, sems, scatter):
    n = len(in_refs)
    send_sems, recv_sems, local_sems = sems[:3]

    def place():
        x, y, c = lax.axis_index("x"), lax.axis_index("y"), lax.axis_index("c")
        return x, y, c, 2 * x + y, [(1 - x, y), (x, 1 - y), (1 - x, 1 - y)]

    def half(ref, a, which):
        rows = in_refs[a].shape[0] // 2
        return ref.at[pl.ds(pl.multiple_of(which * rows, HALO), rows)]

    def ici(a, k, src, dst, peer, c):
        return pltpu.make_async_remote_copy(
            src_ref=src, dst_ref=dst, send_sem=send_sems.at[3 * a + k], recv_sem=recv_sems.at[3 * a + k],
            device_id=(peer[0], peer[1], c), device_id_type=pl.DeviceIdType.MESH)

    def outgoing():
        x, y, c, me, peers = place()
        own = [pltpu.make_async_copy(in_refs[a].at[me] if scatter else in_refs[a], out_refs[a].at[me], local_sems.at[a])
               for a in range(n)]
        if scatter:
            sends = [ici(a, k, in_refs[a].at[2 * p[0] + p[1]], out_refs[a].at[me], p, c)
                     for a in range(n) for k, p in enumerate(peers)]
        else:
            sends = [ici(a, k, half(in_refs[a], a, c), half(out_refs[a].at[me], a, c), p, c)
                     for a in range(n) for k, p in enumerate(peers)]
        return own, sends

    def arrivals():
        x, y, c, me, peers = place()
        if scatter:
            return [ici(a, k, in_refs[a].at[me], out_refs[a].at[2 * p[0] + p[1]], p, c)
                    for a in range(n) for k, p in enumerate(peers)]
        return [ici(a, k, half(in_refs[a], a, c), half(out_refs[a].at[2 * p[0] + p[1]], a, c), p, c)
                for a in range(n) for k, p in enumerate(peers)]

    def to_sibling(mine):
        x, y, c, me, peers = place()
        which = c if mine else 1 - c
        return [pltpu.make_async_remote_copy(
            src_ref=half(out_refs[a].at[2 * p[0] + p[1]], a, which), dst_ref=half(out_refs[a].at[2 * p[0] + p[1]], a, which),
            send_sem=sems[3].at[3 * a + k], recv_sem=sems[4].at[3 * a + k],
            device_id=(x, y, 1 - c), device_id_type=pl.DeviceIdType.MESH) for a in range(n) for k, p in enumerate(peers)]

    def start():
        own, sends = outgoing()
        for cp in own + sends:
            cp.start()

    def finish():
        if scatter:
            for cp in arrivals():
                cp.wait_recv()
        else:
            passed = to_sibling(True)
            for cp, fwd in zip(arrivals(), passed):
                cp.wait_recv()
                fwd.start()
            for cp in to_sibling(False):
                cp.wait_recv()
            for fwd in passed:
                fwd.wait_send()
        own, sends = outgoing()
        for cp in sends:
            cp.wait_send()
        for cp in own:
            cp.wait()

    return start, finish


_NN, _NT, _TN = 'hcs,hsd->hcd', 'hcd,hsd->hcs', 'hcd,hce->hde'


def _lo(spec, a, b):
    return jnp.einsum(spec, a.astype(bf16), b.astype(bf16), preferred_element_type=f32)


@jax.custom_vjp
def _bmm(a, b):
    return _lo(_NN, a, b)


_bmm.defvjp(lambda a, b: (_lo(_NN, a, b), (a, b)), lambda ab, g: (_lo(_NT, g, ab[1]), _lo(_TN, ab[0], g)))


@jax.custom_vjp
def _bmm_nt(a, b):
    return _lo(_NT, a, b)


_bmm_nt.defvjp(lambda a, b: (_lo(_NT, a, b), (a, b)), lambda ab, g: (_lo(_NN, g, ab[1]), _lo(_TN, g, ab[0])))


@jax.custom_vjp
def _bmm_tn(a, b):
    return _lo(_TN, a, b)


_bmm_tn.defvjp(lambda a, b: (_lo(_TN, a, b), (a, b)), lambda ab, g: (_lo(_NT, ab[1], g), _lo(_NN, ab[0], g)))


def _masks(H, C):
    row = lax.broadcasted_iota(jnp.int32, (H, C, C), 1)
    col = lax.broadcasted_iota(jnp.int32, (H, C, C), 2)
    return row, col


def _tri_inv_impl(L):
    H, C, _ = L.shape
    row, col = _masks(H, C)
    eye = (row == col).astype(f32)
    base = 16
    same = (row // base) == (col // base)
    Ld = jnp.where(same, L, 0.0)
    X = -Ld
    inv = eye + X
    for _ in range(3):
        X = _bmm(X, X)
        inv = _bmm(inv, eye + X)
    if C == base:
        return inv
    N = _bmm(inv, L - Ld)
    out = eye - N
    levels = C // base
    P = N
    span = 2
    while span < levels:
        P = _bmm(P, P)
        out = _bmm(out, eye + P)
        span *= 2
    return _bmm(out, inv)


@jax.custom_vjp
def _tri_inv(L):
    return _tri_inv_impl(L)


def _tri_inv_fwd(L):
    T = _tri_inv_impl(L)
    return T, T


def _tri_inv_bwd(T, dT):
    return (-_bmm_nt(_bmm_tn(T, dT), T),)


_tri_inv.defvjp(_tri_inv_fwd, _tri_inv_bwd)


@jax.custom_vjp
def _tri_inv_known(L, T):
    return T


_tri_inv_known.defvjp(lambda L, T: (T, T), lambda T, dT: (_tri_inv_bwd(T, dT)[0], jnp.zeros_like(T)))


def _cumsum_impl(x, reverse):
    C = x.shape[1]
    row = lax.broadcasted_iota(jnp.int32, x.shape, 1)
    s = 1
    while s < C:
        if reverse:
            x = x + jnp.where(row < C - s, pltpu.roll(x, C - s, 1), 0.0)
        else:
            x = x + jnp.where(row >= s, pltpu.roll(x, s, 1), 0.0)
        s *= 2
    return x


@jax.custom_vjp
def _cumsum(x):
    return _cumsum_impl(x, False)


_cumsum.defvjp(lambda x: (_cumsum_impl(x, False), None), lambda _, g: (_cumsum_impl(g, True),))


def _wkv_prep(r, lw, k, v, a, b, inv=None):
    lane = lax.broadcasted_iota(jnp.int32, (r.shape[0], 128), 1)
    low = lane < RWKV_HD

    def heads(t):
        out = []
        for p in range(RWKV_HEADS // 2):
            pair = t[:, 128 * p:128 * (p + 1)]
            out += [jnp.where(low, pair, 0.0), jnp.where(low, 0.0, pair)]
        return jnp.concatenate([t[None] for t in out], axis=0)

    r, lw, k, v, a, b = [heads(t) for t in (r, lw, k, v, a, b)]
    H, C, D = r.shape
    row, col = _masks(H, C)
    incl, strict = row >= col, row > col
    cw = _cumsum(lw)
    cwp = cw - lw
    cwl = jnp.sum(lw, axis=1, keepdims=True)
    en = jnp.exp(-cw)
    at, rt, bt, kt = a * jnp.exp(cwp), r * jnp.exp(cw), b * en, k * en
    Lab = -jnp.where(strict, _bmm_nt(at, bt), 0.0)
    Tm = _tri_inv(Lab) if inv is None else _tri_inv_known(Lab, inv)
    ar = jnp.concatenate([at, rt], axis=1)
    gram = _bmm_nt(ar, jnp.concatenate([bt, kt], axis=1))
    row2 = lax.broadcasted_iota(jnp.int32, (H, 2 * C, 2 * C), 1)
    col2 = lax.broadcasted_iota(jnp.int32, (H, 2 * C, 2 * C), 2) % C
    gram = jnp.where(((row2 < C) & (row2 > col2)) | ((row2 >= C) & (row2 - C >= col2)), gram, 0.0)
    a_bk, r_bk = gram[:, :C], gram[:, C:]
    lak_v = _bmm(a_bk, jnp.concatenate([jnp.zeros_like(v), v], axis=1))
    ed = jnp.exp(cwl - cw)
    zdec = jnp.swapaxes(jnp.broadcast_to(jnp.exp(cwl), (H, D, D)), 1, 2)
    return (ar, Tm, lak_v, r_bk, jnp.concatenate([b * ed, k * ed], axis=1), zdec, v), Tm


def _wkv_step(Z, ar, Tm, lak_v, r_bk, bk_d, zdec, v):
    C = Tm.shape[1]
    ar_z = _bmm(ar, Z)
    uv = jnp.concatenate([_bmm(Tm, ar_z[:, :C] + lak_v), v], axis=1)
    y = ar_z[:, C:] + _bmm(r_bk, uv)
    Z1 = Z * zdec + _bmm_tn(bk_d, uv)
    return jnp.concatenate([y[2 * p] + y[2 * p + 1] for p in range(RWKV_HEADS // 2)], axis=1), Z1


def _split3(x):
    hi = x.astype(bf16)
    mid = (x - hi.astype(f32)).astype(bf16)
    lo = (x - hi.astype(f32) - mid.astype(f32)).astype(bf16)
    return hi, mid, lo


@jax.custom_vjp
def _spread(x, sel):
    return sum(jnp.dot(t, sel, preferred_element_type=f32) for t in _split3(x))


def _spread_bwd(sel, g):
    dn = (((1,), (1,)), ((), ()))
    return sum(lax.dot_general(t, sel, dn, preferred_element_type=f32) for t in _split3(g)), None


_spread.defvjp(lambda x, sel: (_spread(x, sel), sel), _spread_bwd)


def _gdn_prep(q, k, v, gbeta, inv=None):
    heads = lambda t: jnp.concatenate([t[None, :, GDN_HD * h:GDN_HD * (h + 1)] for h in range(GDN_HEADS)], axis=0)
    src = lax.broadcasted_iota(jnp.int32, (W_AB, 2 * GDN_W), 0)
    dst = lax.broadcasted_iota(jnp.int32, (W_AB, 2 * GDN_W), 1) // GDN_HD
    spread = _spread(gbeta, (src == dst).astype(bf16))
    q, k, v, g, beta = heads(q), heads(k), heads(v), heads(spread[:, :GDN_W]), heads(spread[:, GDN_W:])
    H, C, D = q.shape
    row, col = _masks(H, C)
    incl, strict = row >= col, row > col
    gc = _cumsum(g)
    diff = gc - jnp.swapaxes(gc, 1, 2)
    decay = jnp.where(incl, jnp.exp(jnp.where(incl, diff, 0.0)), 0.0)
    gl = jnp.sum(g, axis=1, keepdims=True)
    kb, vb = k * beta, v * beta
    gram = _bmm_nt(jnp.concatenate([kb, q], axis=1), k)
    L = jnp.where(strict, gram[:, :C] * decay, 0.0)
    attn = jnp.where(incl, gram[:, C:] * decay, 0.0)
    egc = jnp.exp(gc)
    Tm = _tri_inv(L) if inv is None else _tri_inv_known(L, inv)
    t_vk = _bmm(Tm, jnp.concatenate([vb, kb * egc], axis=2))
    return (t_vk[:, :, :D], jnp.concatenate([t_vk[:, :, D:], q * egc], axis=1), attn, k * jnp.exp(gl - gc), jnp.exp(gl)), Tm


def _gdn_step(S, u, wq, attn, ke, sdec):
    C = u.shape[1]
    wq_s = _bmm(wq, S)
    v_new = u - wq_s[:, :C]
    o = wq_s[:, C:] + _bmm(attn, v_new)
    S1 = S * sdec + _bmm_tn(ke, v_new)
    return jnp.concatenate([o[h] for h in range(GDN_HEADS)], axis=1), S1


def _scan_fwd(name, fns, ins, C, H, dh, w_out, per_step, side=None):
    prep, step = fns
    T = ins[0].shape[0]
    n_in = len(ins)
    blk = C * per_step
    nblk = T // blk
    n_side = 0 if side is None else len(side[0])

    def body(*refs):
        in_refs, refs = refs[:n_in], refs[n_in:]
        side_in, refs = refs[:n_side], refs[n_side:]
        y_ref, zs_ref, inv_ref, refs = refs[0], refs[1], refs[2], refs[3:]
        side_out, refs = refs[:n_side], refs[n_side:]
        z_scr = refs[0]
        if side is not None:
            start, finish = _xy_copies(side_in, side_out, refs[1:], side[1])
            pl.when(pl.program_id(0) == 0)(start)

        @pl.when(pl.program_id(0) == 0)
        def _():
            z_scr[...] = jnp.zeros_like(z_scr)

        rows = [slice(C * j, C * (j + 1)) for j in range(per_step)]
        prepped = [prep(*[r[rw, :] for r in in_refs]) for rw in rows]
        Z = z_scr[...]
        for j, rw in enumerate(rows):
            zs_ref[j] = Z
            inv_ref[j] = prepped[j][1]
            y, Z = step(Z, *prepped[j][0])
            y_ref[rw, :] = y
        z_scr[...] = Z
        if side is not None:
            pl.when(pl.program_id(0) == nblk - 1)(finish)

    side_bufs = [] if side is None else list(side[0])
    any_spec = pl.BlockSpec(memory_space=pl.ANY)
    return pl.pallas_call(
        body, grid=(nblk,),
        in_specs=[pl.BlockSpec((blk, a.shape[1]), lambda i: (i, 0)) for a in ins] + [any_spec] * n_side,
        out_specs=[pl.BlockSpec((blk, w_out), lambda i: (i, 0)), pl.BlockSpec((per_step, H, dh, dh), lambda i: (i, 0, 0, 0)),
                   pl.BlockSpec((per_step, H, C, C), lambda i: (i, 0, 0, 0))] + [any_spec] * n_side,
        out_shape=[jax.ShapeDtypeStruct((T, w_out), f32), jax.ShapeDtypeStruct((T // C, H, dh, dh), f32),
                   jax.ShapeDtypeStruct((T // C, H, C, C), f32)]
        + (_xy_out_shapes(side_bufs, side[1]) if side is not None else []),
        scratch_shapes=[pltpu.VMEM((H, dh, dh), f32)] + (_xy_sems(n_side, side[1]) if side is not None else []), name=name,
        compiler_params=_params("arbitrary"))(*ins, *side_bufs)


def _scan_bwd(name, fns, ins, dy, zs, invs, C, per_step, side=None):
    prep, step = fns
    T = ins[0].shape[0]
    _, H, dh, _ = zs.shape
    n_in = len(ins)
    blk = C * per_step
    nblk = T // blk
    n_side = 0 if side is None else len(side[0])

    def body(*refs):
        in_refs, dy_ref, zs_ref, inv_ref, refs = refs[:n_in], refs[n_in], refs[n_in + 1], refs[n_in + 2], refs[n_in + 3:]
        side_in, refs = refs[:n_side], refs[n_side:]
        out_refs, refs = refs[:n_in], refs[n_in:]
        side_out, refs = refs[:n_side], refs[n_side:]
        dz_scr = refs[0]
        if side is not None:
            start, finish = _xy_copies(side_in, side_out, refs[1:], side[1])
            pl.when(pl.program_id(0) == 0)(start)

        @pl.when(pl.program_id(0) == 0)
        def _():
            dz_scr[...] = jnp.zeros_like(dz_scr)

        rows = [slice(C * j, C * (j + 1)) for j in range(per_step)]
        prepped = [jax.vjp(lambda *a, j=j: prep(*a, inv=inv_ref[j])[0], *[r[rw, :] for r in in_refs])
                   for j, rw in enumerate(rows)]
        d_prepped = [None] * per_step
        dZ = dz_scr[...]
        for j in reversed(range(per_step)):
            _, pull = jax.vjp(step, zs_ref[j], *prepped[j][0])
            dZ, *d_prepped[j] = pull((dy_ref[rows[j], :], dZ))
        dz_scr[...] = dZ
        for j, rw in enumerate(rows):
            for o_ref, gval in zip(out_refs, prepped[j][1](tuple(d_prepped[j]))):
                o_ref[rw, :] = gval
        if side is not None:
            pl.when(pl.program_id(0) == nblk - 1)(finish)

    side_bufs = [] if side is None else list(side[0])
    any_spec = pl.BlockSpec(memory_space=pl.ANY)
    rev = lambda i: (nblk - 1 - i, 0)
    return pl.pallas_call(
        body, grid=(nblk,),
        in_specs=[pl.BlockSpec((blk, a.shape[1]), rev) for a in ins]
        + [pl.BlockSpec((blk, dy.shape[1]), rev), pl.BlockSpec((per_step, H, dh, dh), lambda i: (nblk - 1 - i, 0, 0, 0)),
           pl.BlockSpec((per_step, H, C, C), lambda i: (nblk - 1 - i, 0, 0, 0))] + [any_spec] * n_side,
        out_specs=[pl.BlockSpec((blk, a.shape[1]), rev) for a in ins] + [any_spec] * n_side,
        out_shape=[jax.ShapeDtypeStruct(a.shape, f32) for a in ins]
        + (_xy_out_shapes(side_bufs, side[1]) if side is not None else []),
        scratch_shapes=[pltpu.VMEM((H, dh, dh), f32)] + (_xy_sems(n_side, side[1]) if side is not None else []), name=name,
        compiler_params=_params("arbitrary"))(*ins, dy, zs, invs, *side_bufs)


def _residual_mm(name, a, b, res, tail, row_extras, consts, row_out, acc_out, tm, head=None):
    K, N = b.shape
    h_rows, h_consts = ([], []) if head is None else (list(head[1]), list(head[2]))
    lhs = h_rows + h_consts if head is not None else [a]
    M = lhs[0].shape[0]
    row_extras = [_row_of(e) for e in row_extras]
    n_lhs, n_res = len(lhs), 0 if res is None else 1
    ne, nc, nr = len(row_extras), len(consts), len(row_out)

    def body(*refs):
        lhs_refs, refs = refs[:n_lhs], refs[n_lhs:]
        b_ref, refs = refs[0], refs[1:]
        res_ref, refs = (refs[0], refs[1:]) if res is not None else (None, refs)
        extra_refs, const_refs, out_refs = refs[:ne], refs[ne:ne + nc], refs[ne + nc:]
        if head is not None:
            left = head[0](*[r[...] for r in lhs_refs])[0].astype(bf16)
            out_refs[0][...] = left
            out_refs = out_refs[1:]
        else:
            left = lhs_refs[0][...].astype(bf16)
        tile = jnp.dot(left, b_ref[...].astype(bf16), preferred_element_type=f32)
        if res is not None:
            tile = res_ref[...] + tile
        outs = tail(tile, *[r[...] for r in extra_refs], *[c[...] for c in const_refs])
        for o_ref, o in zip(out_refs[:nr], outs[:nr]):
            o_ref[...] = o.astype(o_ref.dtype)

        @pl.when(pl.program_id(0) == 0)
        def _():
            for o_ref in out_refs[nr:]:
                o_ref[...] = jnp.zeros_like(o_ref)

        for o_ref, o in zip(out_refs[nr:], outs[nr:]):
            o_ref[...] += o

    lhs_specs = ([_row_spec(tm, r.shape[1]) for r in h_rows] + [_full_spec(c.shape) for c in h_consts]
                 if head is not None else [_row_spec(tm, K)])
    head_out = [(K, bf16)] if head is not None else []
    outs = pl.pallas_call(
        body, grid=(M // tm,),
        in_specs=lhs_specs + [_full_spec(b.shape)] + ([_row_spec(tm, N)] if res is not None else [])
        + [_row_spec(tm, e[1], e[2]) for e in row_extras] + [_full_spec(c.shape) for c in consts],
        out_specs=[_row_spec(tm, w) for w, _ in head_out + list(row_out)] + [_full_spec(sh) for sh in acc_out],
        out_shape=[jax.ShapeDtypeStruct((M, w), d) for w, d in head_out + list(row_out)]
        + [jax.ShapeDtypeStruct(sh, f32) for sh in acc_out],
        name=name, compiler_params=_params("arbitrary"))(
            *lhs, b, *([res] if res is not None else []), *[e[0] for e in row_extras], *consts)
    return outs


def _pull_tail(fn):
    def tail(cot, *args):
        _, vjp = jax.vjp(fn, *args)
        return vjp((cot,))
    return tail


def _norm_tail(x1, g):
    return x1, _rms(x1, g)


def _loss_tail(x2, tgt, g):
    l, vjp = jax.vjp(lambda xv, gv: _loss_rows(xv, tgt, gv), x2, g)
    dx, dg = vjp(jnp.ones_like(l))
    return dx, dg, jnp.zeros((1, 128), f32) + jnp.sum(l)


def _local_step(x, tgt, W, late=None):
    row = lambda a: a.reshape(1, -1)
    wp = W['w_in_pad']
    w_rwkv, w_qkv, w_z = wp[:, :OFF_QKV], wp[:, OFF_QKV:OFF_Z], wp[:, OFF_Z:OFF_GATES]
    w_gates, w_ab = wp[:, OFF_GATES:OFF_AB], wp[:, OFF_AB:]
    mu = row(W['rwkv_mu'])
    mixw = jnp.concatenate([mu, 1.0 - mu], axis=0)
    zpad = jnp.zeros((64, RWKV_W), f32)
    w2p = jnp.concatenate([W['rwkv_w2'], zpad], axis=0)
    a2p = jnp.concatenate([zpad, W['rwkv_a2']], axis=0)
    rw_consts = [row(W['rwkv_w0']), w2p, row(W['rwkv_a0']), a2p, W['rwkv_g2'], row(W['rwkv_k_k']), row(W['rwkv_k_a'])]
    post_consts = [row(W['rwkv_ln_w']), row(W['rwkv_ln_b']), row(W['rwkv_r_k'])]
    pad4 = lambda a: jnp.pad(row(a), ((0, 0), (0, W_AB - GDN_HEADS)))
    gd_consts = [pad4(W['gdn_a_log']), pad4(W['gdn_dt_bias'])]
    nw_t = jnp.tile(row(W['gdn_norm_w']), (1, GDN_HEADS))
    g1, g2n, gf = row(W['norm1_g']), row(W['norm2_g']), row(W['final_g'])

    u = late['u'] if late is not None else _pw_fwd("norm1", _rms_fn, [x], [g1], [D_MODEL], 512, out_dtype=bf16)[0]
    p_rwkv = _mm(u, w_rwkv, 'nn', "in_rwkv")
    qkv_raw = _mm(u, w_qkv, 'nn', "in_qkv")
    z = _mm(u, w_z, 'nn', "in_z")
    gates = _mm(u, w_gates, 'nn', "in_gates")
    ab = _mm(u, w_ab, 'nn', "in_ab")

    r, lw, k2, v, a_, b_, g = _pw_fwd("rwkv_prep", _rwkv_prep_fn, [p_rwkv], rw_consts, [RWKV_W] * 7, 256, conv_w=mixw)
    wkv_in = [r, lw, k2, v, a_, b_]
    y, zs_wkv, inv_wkv, *gathered = _scan_fwd("wkv_fwd", (_wkv_prep, _wkv_step), wkv_in, WKV_CHUNK, RWKV_HEADS, 2 * RWKV_HD, RWKV_W, WKV_PER_STEP,
                                     side=None if late is None else (late['shards'][0], False))
    if late is not None:
        W = dict(W, **late['assemble'](0, gathered))
    ya_in, ya = _residual_mm("rwkv_proj", None, W['rwkv_proj'], None, lambda t: (t,), [], [], [(D_MODEL, f32)], [], 512,
                             head=(_rwkv_post_fn, [y, r, k2, v, g], post_consts))

    lanes = lambda off: slice(off, off + STRIP)
    gd_groups = [[lanes(GDN_HD * h), lanes(GDN_W + GDN_HD * h), lanes(2 * GDN_W + GDN_HD * h)] for h in range(GDN_HEADS)]
    gq, gk, gv = _group_fwd("gdn_prep", _gdn_prep_fn, qkv_raw, W['gdn_conv_w'], [], gd_groups, [], 3, 256)
    (gbeta,) = _pw_fwd("gdn_gate", _gdn_gate_fn, [ab], gd_consts, [W_AB], 512)
    gdn_in = [gq, gk, gv, gbeta]
    o, zs_gdn, inv_gdn, *gathered = _scan_fwd("gdn_fwd", (_gdn_prep, _gdn_step), gdn_in, GDN_CHUNK, GDN_HEADS, GDN_HD, GDN_W, GDN_PER_STEP,
                                     side=None if late is None else (late['shards'][1], False))
    if late is not None:
        W = dict(W, **late['assemble'](1, gathered))
    ga, gb = _cols(gates, D_MODEL, 0), _cols(gates, D_MODEL, 1)
    yb_in, yb, mixed = _residual_mm("gdn_proj", None, W['gdn_proj'], None, lambda t, a_, b_, c_: (t,) + _mix_fn(a_, b_, c_, t),
                                    [ga, gb, ya], [], [(D_MODEL, f32), (D_MODEL, bf16)], [], 512,
                                    head=(_gdn_post_fn, [o, z], [nw_t]))

    x1, u2 = _residual_mm("w_out", mixed, W['w_out'], x, _norm_tail, [], [g2n], [(D_MODEL, f32), (D_MODEL, bf16)], [], 512)
    h = _mm(u2, W['ffn_up'], 'nn', "ffn_up")
    act = _ffn_act_fwd(h, W['ffn_conv_w'], 256)

    G = {}
    slab_out = None if late is None else N_POS
    dx2, dgf, loss = _residual_mm("ffn_down", act, W['ffn_down'], x1, _loss_tail, [tgt], [gf], [(D_MODEL, f32)],
                                  [gf.shape, (1, 128)], 512)
    G['final_g'] = dgf
    dact = _mm(dx2, W['ffn_down'], 'nt', "d_act")
    G['ffn_down'] = _mm(act, dx2, 'tn', "g_ffn_down", out_dtype=bf16)
    dh, G['ffn_conv_w'] = _ffn_act_bwd(h, dact, W['ffn_conv_w'], 128)
    du2 = _mm(dh, W['ffn_up'], 'nt', "d_u2")
    G['ffn_up'] = _mm(u2, dh, 'tn', "g_ffn_up", out_dtype=bf16, col_slabs=slab_out)
    (dx1,), (G['norm2_g'],) = _pw_bwd("norm2_bwd", _rms_fn, [x1], [g2n], [(du2,)], 512, add_to_first=dx2)
    G['w_out'] = _mm(mixed, dx1, 'tn', "g_w_out", out_dtype=bf16)
    dga, dgb, dya, dyb = _residual_mm("d_mixed", dx1, W['w_out'].T, None, _pull_tail(_mix_fn), [ga, gb, ya, yb], [],
                                      [(D_MODEL, bf16)] * 4, [], 512)
    G['rwkv_proj'] = _mm(ya_in, dya, 'tn', "g_rwkv_proj", out_dtype=bf16, col_slabs=slab_out)
    G['gdn_proj'] = _mm(yb_in, dyb, 'tn', "g_gdn_proj", out_dtype=bf16, col_slabs=slab_out)

    do, dz, dnw_t = _residual_mm("d_yb_in", dyb, W['gdn_proj'].T, None, _pull_tail(_gdn_post_fn), [o, z], [nw_t],
                                 [(GDN_W, f32), (GDN_W, bf16)], [nw_t.shape], 512)
    G['gdn_norm_w'] = dnw_t.reshape(GDN_HEADS, GDN_HD).sum(axis=0)
    dgq, dgk, dgv, dgbeta, *arrived_b = _scan_bwd("gdn_bwd", (_gdn_prep, _gdn_step), gdn_in, do, zs_gdn, inv_gdn, GDN_CHUNK,
                                                  GDN_PER_STEP, side=None if late is None else (late['slabs'](G, 1), True))
    dqkv_raw, G['gdn_conv_w'], _ = _group_bwd("gdn_prep_bwd", _gdn_prep_fn, qkv_raw, W['gdn_conv_w'], [], gd_groups, [],
                                              [(dgq,), (dgk,), (dgv,)], 128)
    (dab,), (dal_p, ddt_p) = _pw_bwd("gdn_gate_bwd", _gdn_gate_fn, [ab], gd_consts, [(dgbeta,)], 512, row_dtypes=[bf16])
    G['gdn_a_log'], G['gdn_dt_bias'] = dal_p[0, :GDN_HEADS], ddt_p[0, :GDN_HEADS]

    dy, dr1, dk21, dv1, dg_, G['rwkv_ln_w'], G['rwkv_ln_b'], G['rwkv_r_k'] = _residual_mm(
        "d_ya_in", dya, W['rwkv_proj'].T, None, _pull_tail(_rwkv_post_fn), [y, r, k2, v, g], post_consts,
        [(RWKV_W, f32)] * 5, [c.shape for c in post_consts], 512)
    dr2, dlw, dk22, dv2, da_, db_, *arrived_a = _scan_bwd(
        "wkv_bwd", (_wkv_prep, _wkv_step), wkv_in, dy, zs_wkv, inv_wkv, WKV_CHUNK, WKV_PER_STEP,
        side=None if late is None else (late['slabs'](G, 0), True))
    G['_arrived'] = (arrived_a, arrived_b)
    (dp_rwkv,), dmixw, rw_grads = _pw_conv_bwd(
        "rwkv_prep_bwd", _rwkv_prep_fn, [p_rwkv], rw_consts,
        [(dr1, dr2), (dlw,), (dk21, dk22), (dv1, dv2), (da_,), (db_,), (dg_,)], mixw, 256, row_dtypes=[bf16])
    G['rwkv_w0'], dw2p, G['rwkv_a0'], da2p, G['rwkv_g2'], G['rwkv_k_k'], G['rwkv_k_a'] = rw_grads
    G['rwkv_w2'], G['rwkv_a2'] = dw2p[:64], da2p[64:]
    G['rwkv_mu'] = dmixw[0] - dmixw[1]

    dps = [dp_rwkv, dqkv_raw, dz, dga, dgb, dab]
    offs = [0, OFF_QKV, OFF_Z, OFF_GATES, OFF_GATES + D_MODEL, OFF_AB]
    G['w_in_pad'] = list(zip(offs, _mm_tn_parts(u, dps[:2], "g_w_in_rwkv_qkv") + _mm_tn_parts(u, dps[2:], "g_w_in_rest")))
    pairs = [(dp_rwkv, w_rwkv, 0), (dqkv_raw, w_qkv, 0), (dz, w_z, 0), (dga, w_gates, 0), (dgb, w_gates, 1), (dab, w_ab, 0)]
    if late is None:
        du = _mm_nt_parts(pairs, "d_u")
    else:
        du, *G['_arrived_w_in'] = _mm_nt_parts(pairs, "d_u", side=(late['w_in_slabs'](G), True))
    (dx,), (G['norm1_g'],) = _pw_bwd("norm1_bwd", _rms_fn, [x], [g1], [(du,)], 512, add_to_first=dx1)
    return loss, dx, G


IN_WIDTH = OFF_AB + 8
PAD_ORDER = ((0, OFF_GATES), (OFF_GATES + 8, IN_WIDTH), (OFF_GATES, OFF_GATES + 8))


def _pad_w_in_shards(shards):
    width = shards[0].shape[1]
    parts = []
    for a, b in PAD_ORDER:
        for j, sh in enumerate(shards):
            lo, hi = max(a, j * width), min(b, (j + 1) * width)
            if lo < hi:
                parts.append(sh[:, lo - j * width:hi - j * width])
    return jnp.concatenate(parts + [jnp.zeros((shards[0].shape[0], W_AB - 8), shards[0].dtype)], axis=1)


def _padded_cols(sections, s, e):
    pieces = [arr[:, max(s, o) - o:min(e, o + arr.shape[1]) - o] for o, arr in sections if max(s, o) < min(e, o + arr.shape[1])]
    return pieces[0] if len(pieces) == 1 else jnp.concatenate(pieces, axis=1)


def _unpad_cols(sections, lo, hi):
    parts, off = [], 0
    for a, b in PAD_ORDER:
        l, h = max(a, lo), min(b, hi)
        if l < h:
            parts.append((l, _padded_cols(sections, off + l - a, off + h - a)))
        off += b - a
    parts.sort(key=lambda t: t[0])
    return parts[0][1] if len(parts) == 1 else jnp.concatenate([p for _, p in parts], axis=1)


BIG = ('w_in', 'rwkv_proj', 'gdn_proj', 'w_out', 'ffn_up', 'ffn_down')
SMALL_SHARDED = ('rwkv_w2', 'rwkv_a2', 'rwkv_g2', 'gdn_conv_w', 'ffn_conv_w')


def _rows128(shape):
    n = 1
    for d in shape:
        n *= d
    return -(-n // LANES)


def _pack128(arrays):
    parts = []
    for a in arrays:
        flat = a.reshape(-1)
        rows = _rows128(a.shape)
        parts.append(jnp.pad(flat, (0, rows * LANES - flat.shape[0])).reshape(rows, LANES))
    buf = jnp.concatenate(parts, axis=0)
    return jnp.pad(buf, ((0, -buf.shape[0] % HALO), (0, 0)))


def _unpack128(buf, shapes):
    out, off = [], 0
    for s in shapes:
        rows, n = _rows128(s), 1
        for d in s:
            n *= d
        out.append(buf[off:off + rows].reshape(-1)[:n].reshape(s))
        off += rows
    return out


def _param_tile(r, c):
    best = None
    for d in range(2 * HALO, r + 1, 2 * HALO):
        if r % d == 0 and d * c * 4 <= TILE_BYTES:
            best = d
    if best is not None or r * c * 4 <= TILE_BYTES:
        return (best if best is not None else r), c
    return r, 128


def _norm_gather(name, x, g, bufs, tm):
    n, (T, D) = len(bufs), x.shape
    steps = T // tm

    def body(x_ref, g_ref, *refs):
        side_in, u_ref, side_out, sems = refs[:n], refs[n], refs[n + 1:2 * n + 1], refs[2 * n + 1:]
        i = pl.program_id(0)
        start, finish = _xy_copies(side_in, side_out, sems, False)
        pl.when(i == 0)(start)
        u_ref[...] = _rms(x_ref[...], g_ref[...]).astype(bf16)
        pl.when(i == steps - 1)(finish)

    any_spec = pl.BlockSpec(memory_space=pl.ANY)
    return pl.pallas_call(
        body, grid=(steps,), in_specs=[_row_spec(tm, D), _full_spec(g.shape)] + [any_spec] * n,
        out_specs=[_row_spec(tm, D)] + [any_spec] * n,
        out_shape=[jax.ShapeDtypeStruct((T, D), bf16)] + _xy_out_shapes(bufs, False),
        scratch_shapes=_xy_sems(n, False), name=name, compiler_params=_params("arbitrary"))(x, g, *bufs)


def _sibling_exchange(name, bufs):
    n = len(bufs)

    def body(*refs):
        in_refs, out_refs, send_sems, recv_sems = refs[:n], refs[n:2 * n], refs[2 * n], refs[2 * n + 1]
        x, y, c = lax.axis_index("x"), lax.axis_index("y"), lax.axis_index("c")
        copies = [pltpu.make_async_remote_copy(
            src_ref=in_refs[a], dst_ref=out_refs[a], send_sem=send_sems.at[a], recv_sem=recv_sems.at[a],
            device_id=(x, y, 1 - c), device_id_type=pl.DeviceIdType.MESH) for a in range(n)]
        for cp in copies:
            cp.start()
        for cp in copies:
            cp.wait()

    return pl.pallas_call(
        body, in_specs=[pl.BlockSpec(memory_space=pl.ANY)] * n, out_specs=[pl.BlockSpec(memory_space=pl.ANY)] * n,
        out_shape=[jax.ShapeDtypeStruct(b.shape, b.dtype) for b in bufs],
        scratch_shapes=[pltpu.SemaphoreType.DMA((n,)), pltpu.SemaphoreType.DMA((n,))], name=name)(*bufs)


def _sum_slots(name, buf, side=None):
    _, R, L = buf.shape
    tr, tc = _param_tile(R, L)
    grid = (R // tr, L // tc)
    n_side = 0 if side is None else len(side)

    def body(b_ref, *refs):
        side_in, o_ref, side_out, sems = refs[:n_side], refs[n_side], refs[n_side + 1:2 * n_side + 1], refs[2 * n_side + 1:]
        i, j = pl.program_id(0), pl.program_id(1)
        if side is not None:
            start, finish = _xy_copies(side_in, side_out, sems, True)
            pl.when((i == 0) & (j == 0))(start)
        part = lambda s: b_ref[s].astype(f32)
        o_ref[...] = ((part(0) + part(1)) + part(2)) + part(3)
        if side is not None:
            pl.when((i == grid[0] - 1) & (j == grid[1] - 1))(finish)

    any_spec = pl.BlockSpec(memory_space=pl.ANY)
    side_bufs = [] if side is None else list(side)
    outs = pl.pallas_call(
        body, grid=grid,
        in_specs=[pl.BlockSpec((N_POS, tr, tc), lambda i, j: (0, i, j))] + [any_spec] * n_side,
        out_specs=[pl.BlockSpec((tr, tc), lambda i, j: (i, j))] + [any_spec] * n_side,
        out_shape=[jax.ShapeDtypeStruct((R, L), f32)] + (_xy_out_shapes(side_bufs, True) if side is not None else []),
        scratch_shapes=_xy_sems(n_side, True) if side is not None else [], name=name,
        compiler_params=_params(*(("arbitrary",) * 2 if side is not None else ("parallel",) * 2)))(buf, *side_bufs)
    return list(outs) if side is not None else outs[0]


def _adamw(name, w, ga, gb, m, v):
    R, L = w.shape
    tr, tc = _param_tile(R, L)
    c1 = 1.0 / (1.0 - ADAM_B1 ** ADAM_STEP)
    c2 = 1.0 / (1.0 - ADAM_B2 ** ADAM_STEP)

    def body(w_ref, ga_ref, gb_ref, m_ref, v_ref, g_out, d_out, m_out, v_out):
        g = ga_ref[...] + gb_ref[...]
        m_new = ADAM_B1 * m_ref[...] + (1.0 - ADAM_B1) * g
        v_new = ADAM_B2 * v_ref[...] + (1.0 - ADAM_B2) * (g * g)
        g_out[...] = g
        m_out[...] = m_new
        v_out[...] = v_new
        d_out[...] = -ADAM_LR * ((m_new * c1) / (jnp.sqrt(v_new * c2) + ADAM_EPS) + ADAM_WD * w_ref[...])

    spec = pl.BlockSpec((tr, tc), lambda i, j: (i, j))
    return pl.pallas_call(
        body, grid=(R // tr, L // tc), in_specs=[spec] * 5, out_specs=[spec] * 4,
        out_shape=[jax.ShapeDtypeStruct((R, L), f32)] * 4, name=name,
        compiler_params=_params("parallel", "parallel"))(w, ga, gb, m, v)


def _step(x, loss_target, P, M, V):
    shapes = {n: tuple(P[n].shape) for n in WEIGHTS}
    sh_shapes = [shapes[n] for n in SMALL_SHARDED]
    packed = SMALL_SHARDED + SMALL

    def whole(n, g):
        return g.reshape(-1, g.shape[2]) if n in ROW_SHARDED else jnp.concatenate([g[j] for j in range(N_POS)], axis=1)

    def slabs(G, n, dtype=f32):
        r, c = shapes[n]
        full = G[n].astype(dtype)
        if full.ndim == 3:
            return full
        return full.reshape(N_POS, r, c) if n in ROW_SHARDED else full.reshape(r, N_POS, c).transpose(1, 0, 2)

    u, g_w_in, g_small = _norm_gather("norm1_gather_w_in", x, P['norm1_g'].reshape(1, -1),
                                      [P['w_in'].astype(bf16), _pack128([P[n] for n in SMALL_SHARDED])], 512)
    W = {n: P[n] for n in SMALL}
    W['w_in_pad'] = _pad_w_in_shards([g_w_in[j] for j in range(N_POS)])
    per_pos = [_unpack128(g_small[j], sh_shapes) for j in range(N_POS)]
    for q, n in enumerate(SMALL_SHARDED):
        W[n] = jnp.concatenate([per_pos[j][q] for j in range(N_POS)], axis=1)
    groups = (('rwkv_proj', 'gdn_proj', 'ffn_up'), ('w_out', 'ffn_down'))
    late = dict(u=u, shards=[[P[n].astype(bf16) for n in grp] for grp in groups],
                assemble=lambda q, gathered: {n: whole(n, g) for n, g in zip(groups[q], gathered)},
                slabs=lambda G, q: [slabs(G, n, bf16) for n in groups[q]],
                w_in_slabs=lambda G: [jnp.stack([_unpad_cols(G['w_in_pad'], j * shapes['w_in'][1], (j + 1) * shapes['w_in'][1])
                                                 for j in range(N_POS)])])

    loss_rows, dx, G = _local_step(x, loss_target, W, late)
    arrived = {n: a for grp, got in zip(groups, G.pop('_arrived')) for n, a in zip(grp, got)}
    (arrived_w_in,) = G.pop('_arrived_w_in')
    G.pop('w_in_pad')

    small_slabs = jnp.stack([_pack128([slabs(G, n)[j] for n in SMALL_SHARDED] + [G[n] for n in SMALL]) for j in range(N_POS)])
    plane_w_in, arrived_small = _sum_slots("sum_w_in", arrived_w_in, side=[small_slabs])
    contributions = [arrived[n] for n in BIG[1:]] + [arrived_small]
    tags = list(BIG[1:]) + ['small']
    plane = [plane_w_in] + [_sum_slots("sum_" + t, cbuf) for t, cbuf in zip(tags, contributions)]
    sibling = _sibling_exchange("sibling_grads", plane)

    out = {}
    names4 = ('grad', 'delta', 'new_m', 'new_v')
    for q, n in enumerate(BIG):
        tr = (lambda t: t.T) if n == 'w_in' else (lambda t: t)
        for tag, t in zip(names4, _adamw("adamw_" + n, tr(P[n]), tr(plane[q]), tr(sibling[q]), tr(M[n]), tr(V[n]))):
            out[tag + '_' + n] = tr(t)
    small_out = _adamw("adamw_small", _pack128([P[n] for n in packed]), plane[-1], sibling[-1],
                       _pack128([M[n] for n in packed]), _pack128([V[n] for n in packed]))
    for tag, buf in zip(names4, small_out):
        for n, t in zip(packed, _unpack128(buf, [shapes[n] for n in packed])):
            out[tag + '_' + n] = t
    loss = lax.psum(loss_rows[0, 0], ("x", "y", "c"))
    return loss, dx, out


def kernel(x, norm1_g, w_in, rwkv_mu, rwkv_w0, rwkv_w2, rwkv_a0, rwkv_a2, rwkv_g2, rwkv_k_k, rwkv_k_a, rwkv_r_k, rwkv_ln_w, rwkv_ln_b, rwkv_proj, gdn_conv_w, gdn_a_log, gdn_dt_bias, gdn_norm_w, gdn_proj, w_out, norm2_g, ffn_up, ffn_conv_w, ffn_down, final_g, loss_target, m_norm1_g, m_w_in, m_rwkv_mu, m_rwkv_w0, m_rwkv_w2, m_rwkv_a0, m_rwkv_a2, m_rwkv_g2, m_rwkv_k_k, m_rwkv_k_a, m_rwkv_r_k, m_rwkv_ln_w, m_rwkv_ln_b, m_rwkv_proj, m_gdn_conv_w, m_gdn_a_log, m_gdn_dt_bias, m_gdn_norm_w, m_gdn_proj, m_w_out, m_norm2_g, m_ffn_up, m_ffn_conv_w, m_ffn_down, m_final_g, v_norm1_g, v_w_in, v_rwkv_mu, v_rwkv_w0, v_rwkv_w2, v_rwkv_a0, v_rwkv_a2, v_rwkv_g2, v_rwkv_k_k, v_rwkv_k_a, v_rwkv_r_k, v_rwkv_ln_w, v_rwkv_ln_b, v_rwkv_proj, v_gdn_conv_w, v_gdn_a_log, v_gdn_dt_bias, v_gdn_norm_w, v_gdn_proj, v_w_out, v_norm2_g, v_ffn_up, v_ffn_conv_w, v_ffn_down, v_final_g):
    weights = (norm1_g, w_in, rwkv_mu, rwkv_w0, rwkv_w2, rwkv_a0, rwkv_a2, rwkv_g2, rwkv_k_k, rwkv_k_a, rwkv_r_k, rwkv_ln_w,
               rwkv_ln_b, rwkv_proj, gdn_conv_w, gdn_a_log, gdn_dt_bias, gdn_norm_w, gdn_proj, w_out, norm2_g, ffn_up,
               ffn_conv_w, ffn_down, final_g)
    m_in = (m_norm1_g, m_w_in, m_rwkv_mu, m_rwkv_w0, m_rwkv_w2, m_rwkv_a0, m_rwkv_a2, m_rwkv_g2, m_rwkv_k_k, m_rwkv_k_a,
            m_rwkv_r_k, m_rwkv_ln_w, m_rwkv_ln_b, m_rwkv_proj, m_gdn_conv_w, m_gdn_a_log, m_gdn_dt_bias, m_gdn_norm_w,
            m_gdn_proj, m_w_out, m_norm2_g, m_ffn_up, m_ffn_conv_w, m_ffn_down, m_final_g)
    v_in = (v_norm1_g, v_w_in, v_rwkv_mu, v_rwkv_w0, v_rwkv_w2, v_rwkv_a0, v_rwkv_a2, v_rwkv_g2, v_rwkv_k_k, v_rwkv_k_a,
            v_rwkv_r_k, v_rwkv_ln_w, v_rwkv_ln_b, v_rwkv_proj, v_gdn_conv_w, v_gdn_a_log, v_gdn_dt_bias, v_gdn_norm_w,
            v_gdn_proj, v_w_out, v_norm2_g, v_ffn_up, v_ffn_conv_w, v_ffn_down, v_final_g)
    drop = lambda n, a: a if n == 'final_g' else a[0]
    P = {n: drop(n, a) for n, a in zip(WEIGHTS, weights)}
    M = {n: drop(n, a) for n, a in zip(WEIGHTS, m_in)}
    V = {n: drop(n, a) for n, a in zip(WEIGHTS, v_in)}
    loss, dx, out = _step(x[0], loss_target[0], P, M, V)
    lift = lambda n, a: a if n == 'final_g' else a[None]
    res = [loss, dx[None]]
    for tag in ('grad', 'delta', 'new_m', 'new_v'):
        res += [lift(n, out[tag + '_' + n]) for n in WEIGHTS]
    return tuple(res)
```

```python
import functools

import jax
import jax.numpy as jnp
from jax import lax
from jax.experimental import pallas as pl
from jax.experimental.pallas import tpu as pltpu

f32 = jnp.float32
bf16 = jnp.bfloat16

D_MODEL = 1024
RWKV_HEADS, RWKV_HD, RWKV_W = 8, 64, 512
GDN_HEADS, GDN_HD, GDN_W = 4, 128, 512
NORM_EPS, L2_EPS, GN_EPS = 1e-6, 1e-6, 64e-5
W_AB = 256
OFF_QKV, OFF_Z, OFF_GATES, OFF_AB = 1792, 3328, 3840, 5888
W_IN_PAD = OFF_AB + W_AB
WKV_CHUNK, WKV_PER_STEP = 64, 4
GDN_CHUNK, GDN_PER_STEP = 128, 4
HALO = 8
LANES = 128
TILE_BYTES = 1 << 20
VMEM_LIMIT = 56 * 1024 * 1024

ADAM_LR, ADAM_B1, ADAM_B2, ADAM_EPS, ADAM_WD, ADAM_STEP = 0.001, 0.9, 0.999, 1e-08, 0.01, 10

ROW_SHARDED = ('w_out', 'ffn_down')
SMALL = ('norm1_g', 'rwkv_mu', 'rwkv_w0', 'rwkv_a0', 'rwkv_k_k', 'rwkv_k_a', 'rwkv_r_k', 'rwkv_ln_w', 'rwkv_ln_b',
         'gdn_a_log', 'gdn_dt_bias', 'gdn_norm_w', 'norm2_g', 'final_g')
WEIGHTS = ('norm1_g', 'w_in', 'rwkv_mu', 'rwkv_w0', 'rwkv_w2', 'rwkv_a0', 'rwkv_a2', 'rwkv_g2', 'rwkv_k_k', 'rwkv_k_a',
           'rwkv_r_k', 'rwkv_ln_w', 'rwkv_ln_b', 'rwkv_proj', 'gdn_conv_w', 'gdn_a_log', 'gdn_dt_bias', 'gdn_norm_w',
           'gdn_proj', 'w_out', 'norm2_g', 'ffn_up', 'ffn_conv_w', 'ffn_down', 'final_g')


def _params(*sem):
    return pltpu.CompilerParams(dimension_semantics=sem, vmem_limit_bytes=VMEM_LIMIT)


def _tile(n, limit):
    if n <= limit:
        return n
    best = None
    for d in range(128, limit + 1, 128):
        if n % d == 0:
            best = d
    if best is None:
        raise ValueError(f"no tile for {n} under {limit}")
    return best


MM_BLOCK_BYTES = 6 << 20
MM_MAX_COLS = 1536


def _mm(a, b, mode, name, add=None, out_dtype=f32, side=None, col_slabs=None):
    if mode == 'nn':
        (M, K), N = a.shape, b.shape[1]
    elif mode == 'nt':
        (M, K), N = a.shape, b.shape[0]
    else:
        (K, M), N = a.shape, b.shape[1]
    tm = _tile(M, 1408)
    tk = _tile(K, min(2816, MM_BLOCK_BYTES // (tm * a.dtype.itemsize)))
    tn = _tile(N, max(128, min(MM_BLOCK_BYTES // (tk * b.dtype.itemsize), MM_BLOCK_BYTES // (tm * 4), MM_MAX_COLS) // 128 * 128))
    if col_slabs is not None:
        tn = N // col_slabs
    nk = K // tk
    grid = (M // tm, N // tn, nk)
    dn = {'nn': (((1,), (0,)), ((), ())), 'nt': (((1,), (1,)), ((), ())), 'tn': (((0,), (0,)), ((), ()))}[mode]
    n_add = 0 if add is None else 1
    n_side = 0 if side is None else len(side[0])

    def body(a_ref, b_ref, *rest):
        add_ref = rest[0] if add is not None else None
        side_in, rest = rest[n_add:n_add + n_side], rest[n_add + n_side:]
        o_ref, side_out, rest = rest[0], rest[1:1 + n_side], rest[1 + n_side:]
        acc_ref, rest = (rest[0], rest[1:]) if nk > 1 else (None, rest)
        ids = [pl.program_id(d) for d in range(3)]
        if side is not None:
            start, finish = _xy_copies(side_in, side_out, rest, side[1])
            pl.when((ids[0] == 0) & (ids[1] == 0) & (ids[2] == 0))(start)
        acc = lax.dot_general(a_ref[...].astype(bf16), b_ref[...].astype(bf16), dn, preferred_element_type=f32)
        if nk == 1:
            o_ref[...] = (acc + add_ref[...] if add is not None else acc).astype(out_dtype)
        else:
            k = ids[2]

            @pl.when(k == 0)
            def _():
                acc_ref[...] = acc + add_ref[...] if add is not None else acc

            @pl.when(k > 0)
            def _():
                acc_ref[...] += acc

            @pl.when(k == nk - 1)
            def _():
                o_ref[...] = acc_ref[...].astype(out_dtype)
        if side is not None:
            pl.when((ids[0] == grid[0] - 1) & (ids[1] == grid[1] - 1) & (ids[2] == nk - 1))(finish)

    a_spec = (pl.BlockSpec((tk, tm), lambda i, j, k: (k, i)) if mode == 'tn'
              else pl.BlockSpec((tm, tk), lambda i, j, k: (i, k)))
    b_spec = (pl.BlockSpec((tn, tk), lambda i, j, k: (j, k)) if mode == 'nt'
              else pl.BlockSpec((tk, tn), lambda i, j, k: (k, j)))
    o_spec = pl.BlockSpec((tm, tn), lambda i, j, k: (i, j))
    o_shape = jax.ShapeDtypeStruct((M, N), out_dtype)
    if col_slabs is not None:
        o_spec = pl.BlockSpec((None, tm, tn), lambda i, j, k: (j, i, 0))
        o_shape = jax.ShapeDtypeStruct((col_slabs, M, tn), out_dtype)
    any_spec = pl.BlockSpec(memory_space=pl.ANY)
    side_bufs = [] if side is None else list(side[0])
    ins, specs = [a, b], [a_spec, b_spec]
    if add is not None:
        ins.append(add)
        specs.append(o_spec)
    outs = pl.pallas_call(
        body, grid=grid, in_specs=specs + [any_spec] * n_side, out_specs=[o_spec] + [any_spec] * n_side,
        out_shape=[o_shape] + (_xy_out_shapes(side_bufs, side[1]) if side is not None else []),
        scratch_shapes=([pltpu.VMEM((tm, tn), f32)] if nk > 1 else []) + (_xy_sems(n_side, side[1]) if side is not None else []),
        name=name,
        compiler_params=_params(*(("arbitrary",) * 3 if side is not None else ("parallel", "parallel", "arbitrary"))))(
            *ins, *side_bufs)
    return list(outs) if side is not None else outs[0]


PARTS_TILE = 512
PARTS_DEPTH = 2048


def _mm_nt_parts(pairs, name, side=None):
    n, M, N = len(pairs), pairs[0][0].shape[0], pairs[0][1].shape[0]
    tm = min(M, PARTS_TILE)
    steps = M // tm
    n_side = 0 if side is None else len(side[0])

    def body(*refs):
        a_refs, b_refs, side_in = refs[:n], refs[n:2 * n], refs[2 * n:2 * n + n_side]
        o_ref, side_out, sems = refs[2 * n + n_side], refs[2 * n + n_side + 1:2 * n + 2 * n_side + 1], refs[2 * n + 2 * n_side + 1:]
        i = pl.program_id(0)
        if side is not None:
            start, finish = _xy_copies(side_in, side_out, sems, side[1])
            pl.when(i == 0)(start)
        acc = None
        for a_ref, b_ref in zip(a_refs, b_refs):
            part = lax.dot_general(a_ref[...].astype(bf16), b_ref[...].astype(bf16), (((1,), (1,)), ((), ())),
                                   preferred_element_type=f32)
            acc = part if acc is None else acc + part
        o_ref[...] = acc
        if side is not None:
            pl.when(i == steps - 1)(finish)

    any_spec = pl.BlockSpec(memory_space=pl.ANY)
    side_bufs = [] if side is None else list(side[0])
    o_spec = pl.BlockSpec((tm, N), lambda i: (i, 0))
    outs = pl.pallas_call(
        body, grid=(steps,),
        in_specs=[pl.BlockSpec((tm, a.shape[1]), lambda i: (i, 0)) for a, _, _ in pairs]
        + [pl.BlockSpec((N, a.shape[1]), lambda i, col=col: (0, col)) for a, _, col in pairs] + [any_spec] * n_side,
        out_specs=[o_spec] + [any_spec] * n_side,
        out_shape=[jax.ShapeDtypeStruct((M, N), f32)] + (_xy_out_shapes(side_bufs, side[1]) if side is not None else []),
        scratch_shapes=_xy_sems(n_side, side[1]) if side is not None else [],
        name=name, compiler_params=_params("arbitrary" if side is not None else "parallel"))(
            *[a for a, _, _ in pairs], *[b for _, b, _ in pairs], *side_bufs)
    return list(outs) if side is not None else outs[0]


def _mm_tn_parts(a, parts, name):
    n, (K, M) = len(parts), a.shape
    tm, tk = min(M, PARTS_TILE), min(K, PARTS_DEPTH)
    nk = K // tk
    widths = [p.shape[1] for p in parts]
    offs = [sum(widths[:q]) for q in range(n)]

    def body(a_ref, *refs):
        p_refs, o_refs, acc_ref = refs[:n], refs[n:2 * n], refs[2 * n]
        k = pl.program_id(1)
        lhs = a_ref[...].astype(bf16)
        for p_ref, o_ref, off, w in zip(p_refs, o_refs, offs, widths):
            part = lax.dot_general(lhs, p_ref[...].astype(bf16), (((0,), (0,)), ((), ())), preferred_element_type=f32)
            if nk == 1:
                o_ref[...] = part.astype(bf16)
                continue
            cols = (slice(None), slice(off, off + w))

            @pl.when(k == 0)
            def _():
                acc_ref[cols] = part

            @pl.when(k > 0)
            def _():
                acc_ref[cols] += part

            @pl.when(k == nk - 1)
            def _():
                o_ref[...] = acc_ref[cols].astype(bf16)

    return pl.pallas_call(
        body, grid=(M // tm, nk),
        in_specs=[pl.BlockSpec((tk, tm), lambda i, k: (k, i))] + [pl.BlockSpec((tk, w), lambda i, k: (k, 0)) for w in widths],
        out_specs=[pl.BlockSpec((tm, w), lambda i, k: (i, 0)) for w in widths],
        out_shape=[jax.ShapeDtypeStruct((M, w), bf16) for w in widths],
        scratch_shapes=[pltpu.VMEM((tm, sum(widths)), f32)],
        name=name, compiler_params=_params("parallel", "arbitrary"))(a, *parts)


def _shift_down(cur, prev, s):
    if s == 0:
        return cur
    ext = jnp.concatenate([prev, cur], axis=0)
    return pltpu.roll(ext, s, 0)[HALO:]


def _shift_up(cur, nxt, s):
    if s == 0:
        return cur
    ext = jnp.concatenate([cur, nxt], axis=0)
    return pltpu.roll(ext, ext.shape[0] - s, 0)[:cur.shape[0]]


def _conv_apply(cur, prev, w_ref, shifted=None):
    taps = w_ref.shape[0]
    out = None
    for i in range(taps):
        s = taps - 1 - i
        term = (shifted[s] if shifted is not None else _shift_down(cur, prev, s)) * w_ref[pl.ds(i, 1), :]
        out = term if out is None else out + term
    return out


def _row_spec(tm, w, col=0):
    return pl.BlockSpec((tm, w), lambda i: (i, col))


def _cols(a, width, col):
    return (a, width, col)


def _row_of(r):
    return r if isinstance(r, tuple) else (r, r.shape[1], 0)


def _prev_spec(tm, w):
    return pl.BlockSpec((HALO, w), lambda i: (jnp.maximum(i * (tm // HALO) - 1, 0), 0))


def _next_spec(tm, w, T):
    return pl.BlockSpec((HALO, w), lambda i: (jnp.minimum((i + 1) * (tm // HALO), T // HALO - 1), 0))


def _full_spec(shape):
    return pl.BlockSpec(shape, lambda i: (0,) * len(shape))


def _pw_fwd(name, fn, rows, consts, out_widths, tm, conv_w=None, out_dtype=f32):
    T = _row_of(rows[0])[0].shape[0]
    nr, nc = len(rows), len(consts)

    def body(*refs):
        i = pl.program_id(0)
        vals = [r[...] for r in refs[:nr]]
        p = nr
        if conv_w is not None:
            prev = jnp.where(i > 0, refs[p][...], 0.0)
            vals[0] = _conv_apply(vals[0], prev, refs[p + 1])
            p += 2
        cvals = [r[...] for r in refs[p:p + nc]]
        outs = fn(*vals, *cvals)
        for o_ref, o in zip(refs[p + nc:], outs):
            o_ref[...] = o.astype(out_dtype)

    ins = [_row_of(r)[0] for r in rows]
    specs = [_row_spec(tm, *_row_of(r)[1:]) for r in rows]
    if conv_w is not None:
        ins += [rows[0], conv_w]
        specs += [_prev_spec(tm, rows[0].shape[1]), _full_spec(conv_w.shape)]
    ins += list(consts)
    specs += [_full_spec(c.shape) for c in consts]
    outs = pl.pallas_call(
        body, grid=(T // tm,), in_specs=specs,
        out_specs=[_row_spec(tm, w) for w in out_widths],
        out_shape=[jax.ShapeDtypeStruct((T, w), out_dtype) for w in out_widths], name=name,
        compiler_params=_params("parallel"))(*ins)
    return outs


def _pw_bwd(name, fn, rows, consts, cots, tm, add_to_first=None, row_dtypes=None):
    rows = [_row_of(r) for r in rows]
    T = rows[0][0].shape[0]
    nr, nc = len(rows), len(consts)
    flat_cots = [c for grp in cots for c in grp]
    row_dtypes = row_dtypes or [f32] * nr
    n_extra = 0 if add_to_first is None else 1

    def body(*refs):
        i = pl.program_id(0)
        in_refs, cot_refs = refs[:nr + nc], refs[nr + nc:nr + nc + len(flat_cots)]
        extra_ref = refs[nr + nc + len(flat_cots)] if add_to_first is not None else None
        row_out = refs[nr + nc + len(flat_cots) + n_extra:][:nr]
        const_out = refs[nr + nc + len(flat_cots) + n_extra + nr:]

        @pl.when(i == 0)
        def _():
            for q in range(nc):
                const_out[q][...] = jnp.zeros_like(const_out[q])

        def part(sl):
            cot_vals, p = [], 0
            for grp in cots:
                acc = cot_refs[p][:, sl]
                for q in range(1, len(grp)):
                    acc = acc + cot_refs[p + q][:, sl]
                p += len(grp)
                cot_vals.append(acc)
            _, vjp = jax.vjp(fn, *[r[:, sl] for r in in_refs])
            grads = vjp(tuple(cot_vals))
            for q in range(nr):
                g = grads[q]
                if q == 0 and extra_ref is not None:
                    g = g + extra_ref[:, sl]
                row_out[q][:, sl] = g.astype(row_dtypes[q])
            for q in range(nc):
                const_out[q][:, sl] += grads[nr + q]

        part(slice(None))

    ins = [r[0] for r in rows] + list(consts) + flat_cots
    specs = ([_row_spec(tm, r[1], r[2]) for r in rows] + [_full_spec(c.shape) for c in consts]
             + [_row_spec(tm, c.shape[1]) for c in flat_cots])
    if add_to_first is not None:
        ins.append(add_to_first)
        specs.append(_row_spec(tm, add_to_first.shape[1]))
    out_shapes = ([jax.ShapeDtypeStruct((T, r[1]), d) for r, d in zip(rows, row_dtypes)]
                  + [jax.ShapeDtypeStruct(c.shape, f32) for c in consts])
    out_specs = [_row_spec(tm, r[1]) for r in rows] + [_full_spec(c.shape) for c in consts]
    outs = pl.pallas_call(
        body, grid=(T // tm,), in_specs=specs, out_specs=out_specs, out_shape=out_shapes, name=name,
        compiler_params=_params("arbitrary"))(*ins)
    return list(outs[:nr]), list(outs[nr:])


def _pw_conv_bwd(name, fn, rows, consts, cots, conv_w, tm, row_dtypes=None):
    T, W0 = rows[0].shape
    nr, nc = len(rows), len(consts)
    taps = conv_w.shape[0]
    nblk = T // tm
    flat_cots = [c for grp in cots for c in grp]
    row_dtypes = row_dtypes or [f32] * nr

    def body(*refs):
        i = pl.program_id(0)
        p = 0
        cur = [r[...] for r in refs[p:p + nr]]; p += nr
        nxt = [r[...] for r in refs[p:p + nr]]; p += nr
        prev = jnp.where(i > 0, refs[p][...], 0.0); p += 1
        w_ref = refs[p]; p += 1
        cvals = [r[...] for r in refs[p:p + nc]]; p += nc

        def summed(p0):
            out, q = [], p0
            for grp in cots:
                acc = refs[q][...]
                for t in range(1, len(grp)):
                    acc = acc + refs[q + t][...]
                q += len(grp)
                out.append(acc)
            return out, q

        cot_cur, p = summed(p)
        cot_nxt, p = summed(p)
        row_out, dw_ref, const_out = refs[p:p + nr], refs[p + nr], refs[p + nr + 1:]

        x_cur = cur[0]
        x_down = [_shift_down(x_cur, prev, s_) for s_ in range(taps)]
        _, vjp = jax.vjp(fn, _conv_apply(x_cur, prev, w_ref, x_down), *cur[1:], *cvals)
        grads = vjp(tuple(cot_cur))
        _, vjp_n = jax.vjp(fn, _conv_apply(nxt[0], x_cur[tm - HALO:], w_ref), *nxt[1:], *cvals)
        dc_n = jnp.where(i < nblk - 1, vjp_n(tuple(cot_nxt))[0], 0.0)
        dc = grads[0]

        @pl.when(i == 0)
        def _():
            dw_ref[...] = jnp.zeros_like(dw_ref)
            for q in range(nc):
                const_out[q][...] = jnp.zeros_like(const_out[q])

        dx = None
        for k in range(taps):
            s_ = taps - 1 - k
            term = _shift_up(dc, dc_n, s_) * w_ref[pl.ds(k, 1), :]
            dx = term if dx is None else dx + term
            dw_ref[pl.ds(k, 1), :] += jnp.sum(dc * x_down[s_], axis=0, keepdims=True)
        row_out[0][...] = dx.astype(row_dtypes[0])
        for q in range(1, nr):
            row_out[q][...] = grads[q].astype(row_dtypes[q])
        for q in range(nc):
            const_out[q][...] += grads[nr + q]

    ins = list(rows) + list(rows) + [rows[0], conv_w] + list(consts) + flat_cots + flat_cots
    specs = ([_row_spec(tm, r.shape[1]) for r in rows] + [_next_spec(tm, r.shape[1], T) for r in rows]
             + [_prev_spec(tm, W0), _full_spec(conv_w.shape)] + [_full_spec(c.shape) for c in consts]
             + [_row_spec(tm, c.shape[1]) for c in flat_cots] + [_next_spec(tm, c.shape[1], T) for c in flat_cots])
    out_shapes = ([jax.ShapeDtypeStruct(r.shape, d) for r, d in zip(rows, row_dtypes)]
                  + [jax.ShapeDtypeStruct(conv_w.shape, f32)] + [jax.ShapeDtypeStruct(c.shape, f32) for c in consts])
    out_specs = ([_row_spec(tm, r.shape[1]) for r in rows] + [_full_spec(conv_w.shape)]
                 + [_full_spec(c.shape) for c in consts])
    outs = pl.pallas_call(
        body, grid=(nblk,), in_specs=specs, out_specs=out_specs, out_shape=out_shapes, name=name,
        compiler_params=_params("arbitrary"))(*ins)
    return list(outs[:nr]), outs[nr], list(outs[nr + 1:])


def _sigmoid(x):
    return 0.5 * jnp.tanh(0.5 * x) + 0.5


def _softplus(x):
    return jnp.maximum(x, 0.0) + jnp.log(1.0 + jnp.exp(jnp.minimum(x, -x)))


def _seg_sum_impl(x, seg):
    w = x.shape[-1]
    r = lax.broadcasted_iota(jnp.int32, (w, w), 0) // seg
    c = lax.broadcasted_iota(jnp.int32, (w, w), 1) // seg
    ones = (r == c).astype(bf16)
    hi = x.astype(bf16)
    lo = (x - hi.astype(f32)).astype(bf16)
    return (jnp.dot(hi, ones, preferred_element_type=f32) + jnp.dot(lo, ones, preferred_element_type=f32))


@functools.partial(jax.custom_vjp, nondiff_argnums=(1,))
def _seg_sum(x, seg):
    return _seg_sum_impl(x, seg)


_seg_sum.defvjp(lambda x, seg: (_seg_sum_impl(x, seg), None), lambda seg, _, g: (_seg_sum_impl(g, seg),))


def _rms(x, g):
    return x * lax.rsqrt(jnp.mean(x * x, axis=-1, keepdims=True) + NORM_EPS) * g


def _rms_fn(x, g):
    return (_rms(x, g),)


def _loss_rows(x2, tgt, g):
    e = _rms(x2, g) - tgt
    return 0.5 * jnp.sum(e * e, axis=-1, keepdims=True) * (1.0 / D_MODEL)


@jax.custom_vjp
def _dot_lo(a, b):
    return jnp.dot(a.astype(bf16), b.astype(bf16), preferred_element_type=f32)


def _dot_lo_bwd(ab, g):
    a, b = ab
    gl = g.astype(bf16)
    return (lax.dot_general(gl, b.astype(bf16), (((1,), (1,)), ((), ())), preferred_element_type=f32),
            lax.dot_general(a.astype(bf16), gl, (((0,), (0,)), ((), ())), preferred_element_type=f32))


_dot_lo.defvjp(lambda a, b: (_dot_lo(a, b), (a, b)), _dot_lo_bwd)


def _rwkv_prep_fn(ps, w0, w2p, a0, a2p, g2, k_k, k_a):
    r, k, v = ps[:, 0:512], ps[:, 512:1024], ps[:, 1024:1536]
    wa, gl = ps[:, 1536:1664], ps[:, 1664:1792]
    z = w0 + _dot_lo(jnp.tanh(wa), w2p)
    w_log = -_softplus(-z) - 0.5
    lw = -jnp.exp(w_log)
    a = _sigmoid(a0 + _dot_lo(wa, a2p))
    g = _dot_lo(_sigmoid(gl), g2)
    kx = k * k_k
    kk = kx * lax.rsqrt(_seg_sum(kx * kx, RWKV_HD) + L2_EPS)
    k2 = k * (1.0 + (a - 1.0) * k_a)
    return r, lw, k2, v, -kk, kk * a, g


def _rwkv_post_fn(y, r, k2, v, g, ln_w, ln_b, rk):
    mean = _seg_sum(y, RWKV_HD) * (1.0 / RWKV_HD)
    yc = y - mean
    var = _seg_sum(yc * yc, RWKV_HD) * (1.0 / RWKV_HD)
    yn = yc * lax.rsqrt(var + GN_EPS) * ln_w + ln_b
    bonus = _seg_sum(r * k2 * rk, RWKV_HD) * v
    return ((yn + bonus) * g,)


def _gdn_prep_fn(cq, ck, cv):
    silu = lambda c: c * _sigmoid(c)
    q, k = silu(cq), silu(ck)
    q = q * lax.rsqrt(jnp.sum(q * q, axis=-1, keepdims=True) + L2_EPS) * (GDN_HD ** -0.5)
    k = k * lax.rsqrt(jnp.sum(k * k, axis=-1, keepdims=True) + L2_EPS)
    return q, k, silu(cv)


def _gdn_gate_fn(ab, al_p, dt_p):
    lane = lax.broadcasted_iota(jnp.int32, ab.shape, 1)
    gpart = -jnp.exp(al_p) * _softplus(ab + dt_p)
    return (jnp.where(lane < GDN_HEADS, gpart, jnp.where(lane < 2 * GDN_HEADS, _sigmoid(ab), 0.0)),)


def _gdn_post_fn(o, z, nw):
    ms = _seg_sum(o * o, GDN_HD) * (1.0 / GDN_HD)
    return (o * lax.rsqrt(ms + NORM_EPS) * nw * (z * _sigmoid(z)),)


def _mix_fn(ga, gb, ya, yb):
    return (_sigmoid(ga) * ya + _sigmoid(gb) * yb,)


STRIP = 128


def _strip_conv(ref, prev_ref, w_ref, sl, first, taps):
    cur = ref[:, sl]
    prev = jnp.where(first, 0.0, prev_ref[:, sl])
    down = [_shift_down(cur, prev, s) for s in range(taps)]
    conv = None
    for k in range(taps):
        term = down[taps - 1 - k] * w_ref[pl.ds(k, 1), sl]
        conv = term if conv is None else conv + term
    return cur, down, conv


def _group_fwd(name, fn, x, w, shared_cols, group_cols, consts, n_out, tm):
    T, W = x.shape
    taps = w.shape[0]
    n_groups = len(group_cols)
    nc = len(consts)

    def body(x_ref, xp_ref, w_ref, *refs):
        const_refs, out_refs = refs[:nc], refs[nc:]
        first = pl.program_id(0) == 0
        shared = [_strip_conv(x_ref, xp_ref, w_ref, sl, first, taps)[2] for sl in shared_cols]
        for j, cols in enumerate(group_cols):
            sl = slice(STRIP * j, STRIP * (j + 1))
            convs = [_strip_conv(x_ref, xp_ref, w_ref, c, first, taps)[2] for c in cols]
            outs = fn(*convs, *shared, *[c[:, sl] for c in const_refs])
            for o_ref, o in zip(out_refs, outs):
                o_ref[:, sl] = o

    return pl.pallas_call(
        body, grid=(T // tm,),
        in_specs=[_row_spec(tm, W), _prev_spec(tm, W), _full_spec(w.shape)] + [_full_spec(c.shape) for c in consts],
        out_specs=[_row_spec(tm, STRIP * n_groups)] * n_out,
        out_shape=[jax.ShapeDtypeStruct((T, STRIP * n_groups), f32)] * n_out, name=name,
        compiler_params=_params("parallel"))(x, x, w, *consts)


def _group_bwd(name, fn, x, w, shared_cols, group_cols, consts, cots, tm):
    T, W = x.shape
    taps = w.shape[0]
    nblk = T // tm
    nc, ns = len(consts), len(shared_cols)
    flat_cots = [c for grp in cots for c in grp]
    n_cot = len(flat_cots)

    def body(x_ref, xp_ref, xn_ref, w_ref, *refs):
        const_refs, refs = refs[:nc], refs[nc:]
        cot_refs, cotn_refs, refs = refs[:n_cot], refs[n_cot:2 * n_cot], refs[2 * n_cot:]
        dx_ref, dw_ref, const_out = refs[0], refs[1], refs[2:]
        i = pl.program_id(0)
        first, last = i == 0, i == nblk - 1

        @pl.when(first)
        def _():
            dw_ref[...] = jnp.zeros_like(dw_ref)
            for q in range(nc):
                const_out[q][...] = jnp.zeros_like(const_out[q])

        def convs_of(sl):
            cur, down, conv = _strip_conv(x_ref, xp_ref, w_ref, sl, first, taps)
            nxt, conv_n = xn_ref[:, sl], None
            for k in range(taps):
                term = _shift_down(nxt, cur[tm - HALO:], taps - 1 - k) * w_ref[pl.ds(k, 1), sl]
                conv_n = term if conv_n is None else conv_n + term
            return down, conv, conv_n

        def conv_back(sl, down, dc, dc_n):
            dx = None
            for k in range(taps):
                s_ = taps - 1 - k
                term = _shift_up(dc, dc_n, s_) * w_ref[pl.ds(k, 1), sl]
                dx = term if dx is None else dx + term
                dw_ref[pl.ds(k, 1), sl] += jnp.sum(dc * down[s_], axis=0, keepdims=True)
            dx_ref[:, sl] = dx.astype(dx_ref.dtype)

        def summed(refs_, sl, mask):
            out, p = [], 0
            for grp in cots:
                acc = refs_[p][:, sl]
                for t in range(1, len(grp)):
                    acc = acc + refs_[p + t][:, sl]
                p += len(grp)
                out.append(jnp.where(last, 0.0, acc) if mask else acc)
            return tuple(out)

        shared = [convs_of(sl) for sl in shared_cols]
        d_shared, d_shared_n = [None] * ns, [None] * ns
        for j, cols in enumerate(group_cols):
            sl = slice(STRIP * j, STRIP * (j + 1))
            mine = [convs_of(c) for c in cols]
            cj = [c[:, sl] for c in const_refs]
            _, vjp = jax.vjp(fn, *[m[1] for m in mine], *[m[1] for m in shared], *cj)
            grads = vjp(summed(cot_refs, sl, False))
            _, vjp_n = jax.vjp(fn, *[m[2] for m in mine], *[m[2] for m in shared], *cj)
            grads_n = vjp_n(summed(cotn_refs, sl, True))
            for q, c in enumerate(cols):
                conv_back(c, mine[q][0], grads[q], grads_n[q])
            for q in range(ns):
                g, gn = grads[len(cols) + q], grads_n[len(cols) + q]
                d_shared[q] = g if d_shared[q] is None else d_shared[q] + g
                d_shared_n[q] = gn if d_shared_n[q] is None else d_shared_n[q] + gn
            for q in range(nc):
                const_out[q][:, sl] += grads[len(cols) + ns + q]
        for q, c in enumerate(shared_cols):
            conv_back(c, shared[q][0], d_shared[q], d_shared_n[q])

    outs = pl.pallas_call(
        body, grid=(nblk,),
        in_specs=[_row_spec(tm, W), _prev_spec(tm, W), _next_spec(tm, W, T), _full_spec(w.shape)]
        + [_full_spec(c.shape) for c in consts] + [_row_spec(tm, c.shape[1]) for c in flat_cots]
        + [_next_spec(tm, c.shape[1], T) for c in flat_cots],
        out_specs=[_row_spec(tm, W), _full_spec(w.shape)] + [_full_spec(c.shape) for c in consts],
        out_shape=[jax.ShapeDtypeStruct((T, W), bf16), jax.ShapeDtypeStruct(w.shape, f32)]
        + [jax.ShapeDtypeStruct(c.shape, f32) for c in consts], name=name,
        compiler_params=_params("arbitrary"))(x, x, x, w, *consts, *flat_cots, *flat_cots)
    return outs[0], outs[1], list(outs[2:])


def _ffn_strip_fn(cg, cu):
    return cg * _sigmoid(cg) * cu


def _ffn_act_fwd(h, w, tm):
    T, W2 = h.shape
    H = W2 // 2
    taps = w.shape[0]

    def body(h_ref, hp_ref, w_ref, o_ref):
        first = pl.program_id(0) == 0
        for j in range(H // STRIP):
            gs, us = slice(STRIP * j, STRIP * (j + 1)), slice(H + STRIP * j, H + STRIP * (j + 1))
            cg = _strip_conv(h_ref, hp_ref, w_ref, gs, first, taps)[2]
            cu = _strip_conv(h_ref, hp_ref, w_ref, us, first, taps)[2]
            o_ref[:, gs] = _ffn_strip_fn(cg, cu).astype(o_ref.dtype)

    return pl.pallas_call(
        body, grid=(T // tm,), in_specs=[_row_spec(tm, W2), _prev_spec(tm, W2), _full_spec(w.shape)],
        out_specs=_row_spec(tm, H), out_shape=jax.ShapeDtypeStruct((T, H), bf16), name="ffn_act",
        compiler_params=_params("parallel"))(h, h, w)


def _ffn_act_bwd(h, dact, w, tm):
    T, W2 = h.shape
    H = W2 // 2
    taps = w.shape[0]
    nblk = T // tm

    def body(h_ref, hp_ref, hn_ref, d_ref, dn_ref, w_ref, dh_ref, dw_ref):
        i = pl.program_id(0)
        first, last = i == 0, i == nblk - 1

        @pl.when(first)
        def _():
            dw_ref[...] = jnp.zeros_like(dw_ref)

        for j in range(H // STRIP):
            gs, us = slice(STRIP * j, STRIP * (j + 1)), slice(H + STRIP * j, H + STRIP * (j + 1))
            parts = {}
            for name, sl in (('g', gs), ('u', us)):
                cur, down, conv = _strip_conv(h_ref, hp_ref, w_ref, sl, first, taps)
                nxt = hn_ref[:, sl]
                conv_n = None
                for k in range(taps):
                    term = _shift_down(nxt, cur[tm - HALO:], taps - 1 - k) * w_ref[pl.ds(k, 1), sl]
                    conv_n = term if conv_n is None else conv_n + term
                parts[name] = (down, conv, conv_n)
            _, vjp = jax.vjp(_ffn_strip_fn, parts['g'][1], parts['u'][1])
            dcs = vjp(d_ref[:, gs])
            _, vjp_n = jax.vjp(_ffn_strip_fn, parts['g'][2], parts['u'][2])
            dcs_n = vjp_n(jnp.where(last, 0.0, dn_ref[:, gs]))
            for (name, sl), dc, dc_n in zip((('g', gs), ('u', us)), dcs, dcs_n):
                down = parts[name][0]
                dx = None
                for k in range(taps):
                    s_ = taps - 1 - k
                    term = _shift_up(dc, dc_n, s_) * w_ref[pl.ds(k, 1), sl]
                    dx = term if dx is None else dx + term
                    dw_ref[pl.ds(k, 1), sl] += jnp.sum(dc * down[s_], axis=0, keepdims=True)
                dh_ref[:, sl] = dx.astype(dh_ref.dtype)

    return pl.pallas_call(
        body, grid=(nblk,),
        in_specs=[_row_spec(tm, W2), _prev_spec(tm, W2), _next_spec(tm, W2, T), _row_spec(tm, H), _next_spec(tm, H, T),
                  _full_spec(w.shape)],
        out_specs=[_row_spec(tm, W2), _full_spec(w.shape)],
        out_shape=[jax.ShapeDtypeStruct((T, W2), bf16), jax.ShapeDtypeStruct(w.shape, f32)], name="ffn_act_bwd",
        compiler_params=_params("arbitrary"))(h, h, h, dact, dact, w)


N_POS = 4


def _xy_out_shapes(bufs, scatter):
    return [jax.ShapeDtypeStruct((N_POS,) + tuple(b.shape[1:] if scatter else b.shape), b.dtype) for b in bufs]


def _xy_sems(n, scatter):
    sems = [pltpu.SemaphoreType.DMA((3 * n,)), pltpu.SemaphoreType.DMA((3 * n,)), pltpu.SemaphoreType.DMA((n,))]
    return sems if scatter else sems + [pltpu.SemaphoreType.DMA((3 * n,)), pltpu.SemaphoreType.DMA((3 * n,))]


def _xy_copies(in_refs, out_refs, sems, scatter):
    n = len(in_refs)
    send_sems, recv_sems, local_sems = sems[:3]

    def place():
        x, y, c = lax.axis_index("x"), lax.axis_index("y"), lax.axis_index("c")
        return x, y, c, 2 * x + y, [(1 - x, y), (x, 1 - y), (1 - x, 1 - y)]

    def half(ref, a, which):
        rows = in_refs[a].shape[0] // 2
        return ref.at[pl.ds(pl.multiple_of(which * rows, HALO), rows)]

    def ici(a, k, src, dst, peer, c):
        return pltpu.make_async_remote_copy(
            src_ref=src, dst_ref=dst, send_sem=send_sems.at[3 * a + k], recv_sem=recv_sems.at[3 * a + k],
            device_id=(peer[0], peer[1], c), device_id_type=pl.DeviceIdType.MESH)

    def outgoing():
        x, y, c, me, peers = place()
        own = [pltpu.make_async_copy(in_refs[a].at[me] if scatter else in_refs[a], out_refs[a].at[me], local_sems.at[a])
               for a in range(n)]
        if scatter:
            sends = [ici(a, k, in_refs[a].at[2 * p[0] + p[1]], out_refs[a].at[me], p, c)
                     for a in range(n) for k, p in enumerate(peers)]
        else:
            sends = [ici(a, k, half(in_refs[a], a, c), half(out_refs[a].at[me], a, c), p, c)
                     for a in range(n) for k, p in enumerate(peers)]
        return own, sends

    def arrivals():
        x, y, c, me, peers = place()
        if scatter:
            return [ici(a, k, in_refs[a].at[me], out_refs[a].at[2 * p[0] + p[1]], p, c)
                    for a in range(n) for k, p in enumerate(peers)]
        return [ici(a, k, half(in_refs[a], a, c), half(out_refs[a].at[2 * p[0] + p[1]], a, c), p, c)
                for a in range(n) for k, p in enumerate(peers)]

    def to_sibling(mine):
        x, y, c, me, peers = place()
        which = c if mine else 1 - c
        return [pltpu.make_async_remote_copy(
            src_ref=half(out_refs[a].at[2 * p[0] + p[1]], a, which), dst_ref=half(out_refs[a].at[2 * p[0] + p[1]], a, which),
            send_sem=sems[3].at[3 * a + k], recv_sem=sems[4].at[3 * a + k],
            device_id=(x, y, 1 - c), device_id_type=pl.DeviceIdType.MESH) for a in range(n) for k, p in enumerate(peers)]

    def start():
        own, sends = outgoing()
        for cp in own + sends:
            cp.start()

    def finish():
        if scatter:
            for cp in arrivals():
                cp.wait_recv()
        else:
            passed = to_sibling(True)
            for cp, fwd in zip(arrivals(), passed):
                cp.wait_recv()
                fwd.start()
            for cp in to_sibling(False):
                cp.wait_recv()
            for fwd in passed:
                fwd.wait_send()
        own, sends = outgoing()
        for cp in sends:
            cp.wait_send()
        for cp in own:
            cp.wait()

    return start, finish


_NN, _NT, _TN = 'hcs,hsd->hcd', 'hcd,hsd->hcs', 'hcd,hce->hde'


def _lo(spec, a, b):
    return jnp.einsum(spec, a.astype(bf16), b.astype(bf16), preferred_element_type=f32)


@jax.custom_vjp
def _bmm(a, b):
    return _lo(_NN, a, b)


_bmm.defvjp(lambda a, b: (_lo(_NN, a, b), (a, b)), lambda ab, g: (_lo(_NT, g, ab[1]), _lo(_TN, ab[0], g)))


@jax.custom_vjp
def _bmm_nt(a, b):
    return _lo(_NT, a, b)


_bmm_nt.defvjp(lambda a, b: (_lo(_NT, a, b), (a, b)), lambda ab, g: (_lo(_NN, g, ab[1]), _lo(_TN, g, ab[0])))


@jax.custom_vjp
def _bmm_tn(a, b):
    return _lo(_TN, a, b)


_bmm_tn.defvjp(lambda a, b: (_lo(_TN, a, b), (a, b)), lambda ab, g: (_lo(_NT, ab[1], g), _lo(_NN, ab[0], g)))


def _masks(H, C):
    row = lax.broadcasted_iota(jnp.int32, (H, C, C), 1)
    col = lax.broadcasted_iota(jnp.int32, (H, C, C), 2)
    return row, col


def _tri_inv_impl(L):
    H, C, _ = L.shape
    row, col = _masks(H, C)
    eye = (row == col).astype(f32)
    base = 16
    same = (row // base) == (col // base)
    Ld = jnp.where(same, L, 0.0)
    X = -Ld
    inv = eye + X
    for _ in range(3):
        X = _bmm(X, X)
        inv = _bmm(inv, eye + X)
    if C == base:
        return inv
    N = _bmm(inv, L - Ld)
    out = eye - N
    levels = C // base
    P = N
    span = 2
    while span < levels:
        P = _bmm(P, P)
        out = _bmm(out, eye + P)
        span *= 2
    return _bmm(out, inv)


@jax.custom_vjp
def _tri_inv(L):
    return _tri_inv_impl(L)


def _tri_inv_fwd(L):
    T = _tri_inv_impl(L)
    return T, T


def _tri_inv_bwd(T, dT):
    return (-_bmm_nt(_bmm_tn(T, dT), T),)


_tri_inv.defvjp(_tri_inv_fwd, _tri_inv_bwd)


@jax.custom_vjp
def _tri_inv_known(L, T):
    return T


_tri_inv_known.defvjp(lambda L, T: (T, T), lambda T, dT: (_tri_inv_bwd(T, dT)[0], jnp.zeros_like(T)))


def _cumsum_impl(x, reverse):
    C = x.shape[1]
    row = lax.broadcasted_iota(jnp.int32, x.shape, 1)
    s = 1
    while s < C:
        if reverse:
            x = x + jnp.where(row < C - s, pltpu.roll(x, C - s, 1), 0.0)
        else:
            x = x + jnp.where(row >= s, pltpu.roll(x, s, 1), 0.0)
        s *= 2
    return x


@jax.custom_vjp
def _cumsum(x):
    return _cumsum_impl(x, False)


_cumsum.defvjp(lambda x: (_cumsum_impl(x, False), None), lambda _, g: (_cumsum_impl(g, True),))


def _wkv_prep(r, lw, k, v, a, b, inv=None):
    lane = lax.broadcasted_iota(jnp.int32, (r.shape[0], 128), 1)
    low = lane < RWKV_HD

    def heads(t):
        out = []
        for p in range(RWKV_HEADS // 2):
            pair = t[:, 128 * p:128 * (p + 1)]
            out += [jnp.where(low, pair, 0.0), jnp.where(low, 0.0, pair)]
        return jnp.concatenate([t[None] for t in out], axis=0)

    r, lw, k, v, a, b = [heads(t) for t in (r, lw, k, v, a, b)]
    H, C, D = r.shape
    row, col = _masks(H, C)
    incl, strict = row >= col, row > col
    cw = _cumsum(lw)
    cwp = cw - lw
    cwl = jnp.sum(lw, axis=1, keepdims=True)
    en = jnp.exp(-cw)
    at, rt, bt, kt = a * jnp.exp(cwp), r * jnp.exp(cw), b * en, k * en
    Lab = -jnp.where(strict, _bmm_nt(at, bt), 0.0)
    Tm = _tri_inv(Lab) if inv is None else _tri_inv_known(Lab, inv)
    ar = jnp.concatenate([at, rt], axis=1)
    gram = _bmm_nt(ar, jnp.concatenate([bt, kt], axis=1))
    row2 = lax.broadcasted_iota(jnp.int32, (H, 2 * C, 2 * C), 1)
    col2 = lax.broadcasted_iota(jnp.int32, (H, 2 * C, 2 * C), 2) % C
    gram = jnp.where(((row2 < C) & (row2 > col2)) | ((row2 >= C) & (row2 - C >= col2)), gram, 0.0)
    a_bk, r_bk = gram[:, :C], gram[:, C:]
    lak_v = _bmm(a_bk, jnp.concatenate([jnp.zeros_like(v), v], axis=1))
    ed = jnp.exp(cwl - cw)
    zdec = jnp.swapaxes(jnp.broadcast_to(jnp.exp(cwl), (H, D, D)), 1, 2)
    return (ar, Tm, lak_v, r_bk, jnp.concatenate([b * ed, k * ed], axis=1), zdec, v), Tm


def _wkv_step(Z, ar, Tm, lak_v, r_bk, bk_d, zdec, v):
    C = Tm.shape[1]
    ar_z = _bmm(ar, Z)
    uv = jnp.concatenate([_bmm(Tm, ar_z[:, :C] + lak_v), v], axis=1)
    y = ar_z[:, C:] + _bmm(r_bk, uv)
    Z1 = Z * zdec + _bmm_tn(bk_d, uv)
    return jnp.concatenate([y[2 * p] + y[2 * p + 1] for p in range(RWKV_HEADS // 2)], axis=1), Z1


def _split3(x):
    hi = x.astype(bf16)
    mid = (x - hi.astype(f32)).astype(bf16)
    lo = (x - hi.astype(f32) - mid.astype(f32)).astype(bf16)
    return hi, mid, lo


@jax.custom_vjp
def _spread(x, sel):
    return sum(jnp.dot(t, sel, preferred_element_type=f32) for t in _split3(x))


def _spread_bwd(sel, g):
    dn = (((1,), (1,)), ((), ()))
    return sum(lax.dot_general(t, sel, dn, preferred_element_type=f32) for t in _split3(g)), None


_spread.defvjp(lambda x, sel: (_spread(x, sel), sel), _spread_bwd)


def _gdn_prep(q, k, v, gbeta, inv=None):
    heads = lambda t: jnp.concatenate([t[None, :, GDN_HD * h:GDN_HD * (h + 1)] for h in range(GDN_HEADS)], axis=0)
    src = lax.broadcasted_iota(jnp.int32, (W_AB, 2 * GDN_W), 0)
    dst = lax.broadcasted_iota(jnp.int32, (W_AB, 2 * GDN_W), 1) // GDN_HD
    spread = _spread(gbeta, (src == dst).astype(bf16))
    q, k, v, g, beta = heads(q), heads(k), heads(v), heads(spread[:, :GDN_W]), heads(spread[:, GDN_W:])
    H, C, D = q.shape
    row, col = _masks(H, C)
    incl, strict = row >= col, row > col
    gc = _cumsum(g)
    diff = gc - jnp.swapaxes(gc, 1, 2)
    decay = jnp.where(incl, jnp.exp(jnp.where(incl, diff, 0.0)), 0.0)
    gl = jnp.sum(g, axis=1, keepdims=True)
    kb, vb = k * beta, v * beta
    gram = _bmm_nt(jnp.concatenate([kb, q], axis=1), k)
    L = jnp.where(strict, gram[:, :C] * decay, 0.0)
    attn = jnp.where(incl, gram[:, C:] * decay, 0.0)
    egc = jnp.exp(gc)
    Tm = _tri_inv(L) if inv is None else _tri_inv_known(L, inv)
    t_vk = _bmm(Tm, jnp.concatenate([vb, kb * egc], axis=2))
    return (t_vk[:, :, :D], jnp.concatenate([t_vk[:, :, D:], q * egc], axis=1), attn, k * jnp.exp(gl - gc), jnp.exp(gl)), Tm


def _gdn_step(S, u, wq, attn, ke, sdec):
    C = u.shape[1]
    wq_s = _bmm(wq, S)
    v_new = u - wq_s[:, :C]
    o = wq_s[:, C:] + _bmm(attn, v_new)
    S1 = S * sdec + _bmm_tn(ke, v_new)
    return jnp.concatenate([o[h] for h in range(GDN_HEADS)], axis=1), S1


def _scan_fwd(name, fns, ins, C, H, dh, w_out, per_step, side=None):
    prep, step = fns
    T = ins[0].shape[0]
    n_in = len(ins)
    blk = C * per_step
    nblk = T // blk
    n_side = 0 if side is None else len(side[0])

    def body(*refs):
        in_refs, refs = refs[:n_in], refs[n_in:]
        side_in, refs = refs[:n_side], refs[n_side:]
        y_ref, zs_ref, inv_ref, refs = refs[0], refs[1], refs[2], refs[3:]
        side_out, refs = refs[:n_side], refs[n_side:]
        z_scr = refs[0]
        if side is not None:
            start, finish = _xy_copies(side_in, side_out, refs[1:], side[1])
            pl.when(pl.program_id(0) == 0)(start)

        @pl.when(pl.program_id(0) == 0)
        def _():
            z_scr[...] = jnp.zeros_like(z_scr)

        rows = [slice(C * j, C * (j + 1)) for j in range(per_step)]
        prepped = [prep(*[r[rw, :] for r in in_refs]) for rw in rows]
        Z = z_scr[...]
        for j, rw in enumerate(rows):
            zs_ref[j] = Z
            inv_ref[j] = prepped[j][1]
            y, Z = step(Z, *prepped[j][0])
            y_ref[rw, :] = y
        z_scr[...] = Z
        if side is not None:
            pl.when(pl.program_id(0) == nblk - 1)(finish)

    side_bufs = [] if side is None else list(side[0])
    any_spec = pl.BlockSpec(memory_space=pl.ANY)
    return pl.pallas_call(
        body, grid=(nblk,),
        in_specs=[pl.BlockSpec((blk, a.shape[1]), lambda i: (i, 0)) for a in ins] + [any_spec] * n_side,
        out_specs=[pl.BlockSpec((blk, w_out), lambda i: (i, 0)), pl.BlockSpec((per_step, H, dh, dh), lambda i: (i, 0, 0, 0)),
                   pl.BlockSpec((per_step, H, C, C), lambda i: (i, 0, 0, 0))] + [any_spec] * n_side,
        out_shape=[jax.ShapeDtypeStruct((T, w_out), f32), jax.ShapeDtypeStruct((T // C, H, dh, dh), f32),
                   jax.ShapeDtypeStruct((T // C, H, C, C), f32)]
        + (_xy_out_shapes(side_bufs, side[1]) if side is not None else []),
        scratch_shapes=[pltpu.VMEM((H, dh, dh), f32)] + (_xy_sems(n_side, side[1]) if side is not None else []), name=name,
        compiler_params=_params("arbitrary"))(*ins, *side_bufs)


def _scan_bwd(name, fns, ins, dy, zs, invs, C, per_step, side=None):
    prep, step = fns
    T = ins[0].shape[0]
    _, H, dh, _ = zs.shape
    n_in = len(ins)
    blk = C * per_step
    nblk = T // blk
    n_side = 0 if side is None else len(side[0])

    def body(*refs):
        in_refs, dy_ref, zs_ref, inv_ref, refs = refs[:n_in], refs[n_in], refs[n_in + 1], refs[n_in + 2], refs[n_in + 3:]
        side_in, refs = refs[:n_side], refs[n_side:]
        out_refs, refs = refs[:n_in], refs[n_in:]
        side_out, refs = refs[:n_side], refs[n_side:]
        dz_scr = refs[0]
        if side is not None:
            start, finish = _xy_copies(side_in, side_out, refs[1:], side[1])
            pl.when(pl.program_id(0) == 0)(start)

        @pl.when(pl.program_id(0) == 0)
        def _():
            dz_scr[...] = jnp.zeros_like(dz_scr)

        rows = [slice(C * j, C * (j + 1)) for j in range(per_step)]
        prepped = [jax.vjp(lambda *a, j=j: prep(*a, inv=inv_ref[j])[0], *[r[rw, :] for r in in_refs])
                   for j, rw in enumerate(rows)]
        d_prepped = [None] * per_step
        dZ = dz_scr[...]
        for j in reversed(range(per_step)):
            _, pull = jax.vjp(step, zs_ref[j], *prepped[j][0])
            dZ, *d_prepped[j] = pull((dy_ref[rows[j], :], dZ))
        dz_scr[...] = dZ
        for j, rw in enumerate(rows):
            for o_ref, gval in zip(out_refs, prepped[j][1](tuple(d_prepped[j]))):
                o_ref[rw, :] = gval
        if side is not None:
            pl.when(pl.program_id(0) == nblk - 1)(finish)

    side_bufs = [] if side is None else list(side[0])
    any_spec = pl.BlockSpec(memory_space=pl.ANY)
    rev = lambda i: (nblk - 1 - i, 0)
    return pl.pallas_call(
        body, grid=(nblk,),
        in_specs=[pl.BlockSpec((blk, a.shape[1]), rev) for a in ins]
        + [pl.BlockSpec((blk, dy.shape[1]), rev), pl.BlockSpec((per_step, H, dh, dh), lambda i: (nblk - 1 - i, 0, 0, 0)),
           pl.BlockSpec((per_step, H, C, C), lambda i: (nblk - 1 - i, 0, 0, 0))] + [any_spec] * n_side,
        out_specs=[pl.BlockSpec((blk, a.shape[1]), rev) for a in ins] + [any_spec] * n_side,
        out_shape=[jax.ShapeDtypeStruct(a.shape, f32) for a in ins]
        + (_xy_out_shapes(side_bufs, side[1]) if side is not None else []),
        scratch_shapes=[pltpu.VMEM((H, dh, dh), f32)] + (_xy_sems(n_side, side[1]) if side is not None else []), name=name,
        compiler_params=_params("arbitrary"))(*ins, dy, zs, invs, *side_bufs)


def _residual_mm(name, a, b, res, tail, row_extras, consts, row_out, acc_out, tm, head=None):
    K, N = b.shape
    h_rows, h_consts = ([], []) if head is None else (list(head[1]), list(head[2]))
    lhs = h_rows + h_consts if head is not None else [a]
    M = lhs[0].shape[0]
    row_extras = [_row_of(e) for e in row_extras]
    n_lhs, n_res = len(lhs), 0 if res is None else 1
    ne, nc, nr = len(row_extras), len(consts), len(row_out)

    def body(*refs):
        lhs_refs, refs = refs[:n_lhs], refs[n_lhs:]
        b_ref, refs = refs[0], refs[1:]
        res_ref, refs = (refs[0], refs[1:]) if res is not None else (None, refs)
        extra_refs, const_refs, out_refs = refs[:ne], refs[ne:ne + nc], refs[ne + nc:]
        if head is not None:
            left = head[0](*[r[...] for r in lhs_refs])[0].astype(bf16)
            out_refs[0][...] = left
            out_refs = out_refs[1:]
        else:
            left = lhs_refs[0][...].astype(bf16)
        tile = jnp.dot(left, b_ref[...].astype(bf16), preferred_element_type=f32)
        if res is not None:
            tile = res_ref[...] + tile
        outs = tail(tile, *[r[...] for r in extra_refs], *[c[...] for c in const_refs])
        for o_ref, o in zip(out_refs[:nr], outs[:nr]):
            o_ref[...] = o.astype(o_ref.dtype)

        @pl.when(pl.program_id(0) == 0)
        def _():
            for o_ref in out_refs[nr:]:
                o_ref[...] = jnp.zeros_like(o_ref)

        for o_ref, o in zip(out_refs[nr:], outs[nr:]):
            o_ref[...] += o

    lhs_specs = ([_row_spec(tm, r.shape[1]) for r in h_rows] + [_full_spec(c.shape) for c in h_consts]
                 if head is not None else [_row_spec(tm, K)])
    head_out = [(K, bf16)] if head is not None else []
    outs = pl.pallas_call(
        body, grid=(M // tm,),
        in_specs=lhs_specs + [_full_spec(b.shape)] + ([_row_spec(tm, N)] if res is not None else [])
        + [_row_spec(tm, e[1], e[2]) for e in row_extras] + [_full_spec(c.shape) for c in consts],
        out_specs=[_row_spec(tm, w) for w, _ in head_out + list(row_out)] + [_full_spec(sh) for sh in acc_out],
        out_shape=[jax.ShapeDtypeStruct((M, w), d) for w, d in head_out + list(row_out)]
        + [jax.ShapeDtypeStruct(sh, f32) for sh in acc_out],
        name=name, compiler_params=_params("arbitrary"))(
            *lhs, b, *([res] if res is not None else []), *[e[0] for e in row_extras], *consts)
    return outs


def _pull_tail(fn):
    def tail(cot, *args):
        _, vjp = jax.vjp(fn, *args)
        return vjp((cot,))
    return tail


def _norm_tail(x1, g):
    return x1, _rms(x1, g)


def _loss_tail(x2, tgt, g):
    l, vjp = jax.vjp(lambda xv, gv: _loss_rows(xv, tgt, gv), x2, g)
    dx, dg = vjp(jnp.ones_like(l))
    return dx, dg, jnp.zeros((1, 128), f32) + jnp.sum(l)


def _local_step(x, tgt, W, late=None):
    row = lambda a: a.reshape(1, -1)
    wp = W['w_in_pad']
    w_rwkv, w_qkv, w_z = wp[:, :OFF_QKV], wp[:, OFF_QKV:OFF_Z], wp[:, OFF_Z:OFF_GATES]
    w_gates, w_ab = wp[:, OFF_GATES:OFF_AB], wp[:, OFF_AB:]
    mu = row(W['rwkv_mu'])
    mixw = jnp.concatenate([mu, 1.0 - mu], axis=0)
    zpad = jnp.zeros((64, RWKV_W), f32)
    w2p = jnp.concatenate([W['rwkv_w2'], zpad], axis=0)
    a2p = jnp.concatenate([zpad, W['rwkv_a2']], axis=0)
    rw_consts = [row(W['rwkv_w0']), w2p, row(W['rwkv_a0']), a2p, W['rwkv_g2'], row(W['rwkv_k_k']), row(W['rwkv_k_a'])]
    post_consts = [row(W['rwkv_ln_w']), row(W['rwkv_ln_b']), row(W['rwkv_r_k'])]
    pad4 = lambda a: jnp.pad(row(a), ((0, 0), (0, W_AB - GDN_HEADS)))
    gd_consts = [pad4(W['gdn_a_log']), pad4(W['gdn_dt_bias'])]
    nw_t = jnp.tile(row(W['gdn_norm_w']), (1, GDN_HEADS))
    g1, g2n, gf = row(W['norm1_g']), row(W['norm2_g']), row(W['final_g'])

    u = late['u'] if late is not None else _pw_fwd("norm1", _rms_fn, [x], [g1], [D_MODEL], 512, out_dtype=bf16)[0]
    p_rwkv = _mm(u, w_rwkv, 'nn', "in_rwkv")
    qkv_raw = _mm(u, w_qkv, 'nn', "in_qkv")
    z = _mm(u, w_z, 'nn', "in_z")
    gates = _mm(u, w_gates, 'nn', "in_gates")
    ab = _mm(u, w_ab, 'nn', "in_ab")

    r, lw, k2, v, a_, b_, g = _pw_fwd("rwkv_prep", _rwkv_prep_fn, [p_rwkv], rw_consts, [RWKV_W] * 7, 256, conv_w=mixw)
    wkv_in = [r, lw, k2, v, a_, b_]
    y, zs_wkv, inv_wkv, *gathered = _scan_fwd("wkv_fwd", (_wkv_prep, _wkv_step), wkv_in, WKV_CHUNK, RWKV_HEADS, 2 * RWKV_HD, RWKV_W, WKV_PER_STEP,
                                     side=None if late is None else (late['shards'][0], False))
    if late is not None:
        W = dict(W, **late['assemble'](0, gathered))
    ya_in, ya = _residual_mm("rwkv_proj", None, W['rwkv_proj'], None, lambda t: (t,), [], [], [(D_MODEL, f32)], [], 512,
                             head=(_rwkv_post_fn, [y, r, k2, v, g], post_consts))

    lanes = lambda off: slice(off, off + STRIP)
    gd_groups = [[lanes(GDN_HD * h), lanes(GDN_W + GDN_HD * h), lanes(2 * GDN_W + GDN_HD * h)] for h in range(GDN_HEADS)]
    gq, gk, gv = _group_fwd("gdn_prep", _gdn_prep_fn, qkv_raw, W['gdn_conv_w'], [], gd_groups, [], 3, 256)
    (gbeta,) = _pw_fwd("gdn_gate", _gdn_gate_fn, [ab], gd_consts, [W_AB], 512)
    gdn_in = [gq, gk, gv, gbeta]
    o, zs_gdn, inv_gdn, *gathered = _scan_fwd("gdn_fwd", (_gdn_prep, _gdn_step), gdn_in, GDN_CHUNK, GDN_HEADS, GDN_HD, GDN_W, GDN_PER_STEP,
                                     side=None if late is None else (late['shards'][1], False))
    if late is not None:
        W = dict(W, **late['assemble'](1, gathered))
    ga, gb = _cols(gates, D_MODEL, 0), _cols(gates, D_MODEL, 1)
    yb_in, yb, mixed = _residual_mm("gdn_proj", None, W['gdn_proj'], None, lambda t, a_, b_, c_: (t,) + _mix_fn(a_, b_, c_, t),
                                    [ga, gb, ya], [], [(D_MODEL, f32), (D_MODEL, bf16)], [], 512,
                                    head=(_gdn_post_fn, [o, z], [nw_t]))

    x1, u2 = _residual_mm("w_out", mixed, W['w_out'], x, _norm_tail, [], [g2n], [(D_MODEL, f32), (D_MODEL, bf16)], [], 512)
    h = _mm(u2, W['ffn_up'], 'nn', "ffn_up")
    act = _ffn_act_fwd(h, W['ffn_conv_w'], 256)

    G = {}
    slab_out = None if late is None else N_POS
    dx2, dgf, loss = _residual_mm("ffn_down", act, W['ffn_down'], x1, _loss_tail, [tgt], [gf], [(D_MODEL, f32)],
                                  [gf.shape, (1, 128)], 512)
    G['final_g'] = dgf
    dact = _mm(dx2, W['ffn_down'], 'nt', "d_act")
    G['ffn_down'] = _mm(act, dx2, 'tn', "g_ffn_down", out_dtype=bf16)
    dh, G['ffn_conv_w'] = _ffn_act_bwd(h, dact, W['ffn_conv_w'], 128)
    du2 = _mm(dh, W['ffn_up'], 'nt', "d_u2")
    G['ffn_up'] = _mm(u2, dh, 'tn', "g_ffn_up", out_dtype=bf16, col_slabs=slab_out)
    (dx1,), (G['norm2_g'],) = _pw_bwd("norm2_bwd", _rms_fn, [x1], [g2n], [(du2,)], 512, add_to_first=dx2)
    G['w_out'] = _mm(mixed, dx1, 'tn', "g_w_out", out_dtype=bf16)
    dga, dgb, dya, dyb = _residual_mm("d_mixed", dx1, W['w_out'].T, None, _pull_tail(_mix_fn), [ga, gb, ya, yb], [],
                                      [(D_MODEL, bf16)] * 4, [], 512)
    G['rwkv_proj'] = _mm(ya_in, dya, 'tn', "g_rwkv_proj", out_dtype=bf16, col_slabs=slab_out)
    G['gdn_proj'] = _mm(yb_in, dyb, 'tn', "g_gdn_proj", out_dtype=bf16, col_slabs=slab_out)

    do, dz, dnw_t = _residual_mm("d_yb_in", dyb, W['gdn_proj'].T, None, _pull_tail(_gdn_post_fn), [o, z], [nw_t],
                                 [(GDN_W, f32), (GDN_W, bf16)], [nw_t.shape], 512)
    G['gdn_norm_w'] = dnw_t.reshape(GDN_HEADS, GDN_HD).sum(axis=0)
    dgq, dgk, dgv, dgbeta, *arrived_b = _scan_bwd("gdn_bwd", (_gdn_prep, _gdn_step), gdn_in, do, zs_gdn, inv_gdn, GDN_CHUNK,
                                                  GDN_PER_STEP, side=None if late is None else (late['slabs'](G, 1), True))
    dqkv_raw, G['gdn_conv_w'], _ = _group_bwd("gdn_prep_bwd", _gdn_prep_fn, qkv_raw, W['gdn_conv_w'], [], gd_groups, [],
                                              [(dgq,), (dgk,), (dgv,)], 128)
    (dab,), (dal_p, ddt_p) = _pw_bwd("gdn_gate_bwd", _gdn_gate_fn, [ab], gd_consts, [(dgbeta,)], 512, row_dtypes=[bf16])
    G['gdn_a_log'], G['gdn_dt_bias'] = dal_p[0, :GDN_HEADS], ddt_p[0, :GDN_HEADS]

    dy, dr1, dk21, dv1, dg_, G['rwkv_ln_w'], G['rwkv_ln_b'], G['rwkv_r_k'] = _residual_mm(
        "d_ya_in", dya, W['rwkv_proj'].T, None, _pull_tail(_rwkv_post_fn), [y, r, k2, v, g], post_consts,
        [(RWKV_W, f32)] * 5, [c.shape for c in post_consts], 512)
    dr2, dlw, dk22, dv2, da_, db_, *arrived_a = _scan_bwd(
        "wkv_bwd", (_wkv_prep, _wkv_step), wkv_in, dy, zs_wkv, inv_wkv, WKV_CHUNK, WKV_PER_STEP,
        side=None if late is None else (late['slabs'](G, 0), True))
    G['_arrived'] = (arrived_a, arrived_b)
    (dp_rwkv,), dmixw, rw_grads = _pw_conv_bwd(
        "rwkv_prep_bwd", _rwkv_prep_fn, [p_rwkv], rw_consts,
        [(dr1, dr2), (dlw,), (dk21, dk22), (dv1, dv2), (da_,), (db_,), (dg_,)], mixw, 256, row_dtypes=[bf16])
    G['rwkv_w0'], dw2p, G['rwkv_a0'], da2p, G['rwkv_g2'], G['rwkv_k_k'], G['rwkv_k_a'] = rw_grads
    G['rwkv_w2'], G['rwkv_a2'] = dw2p[:64], da2p[64:]
    G['rwkv_mu'] = dmixw[0] - dmixw[1]

    dps = [dp_rwkv, dqkv_raw, dz, dga, dgb, dab]
    offs = [0, OFF_QKV, OFF_Z, OFF_GATES, OFF_GATES + D_MODEL, OFF_AB]
    G['w_in_pad'] = list(zip(offs, _mm_tn_parts(u, dps[:2], "g_w_in_rwkv_qkv") + _mm_tn_parts(u, dps[2:], "g_w_in_rest")))
    pairs = [(dp_rwkv, w_rwkv, 0), (dqkv_raw, w_qkv, 0), (dz, w_z, 0), (dga, w_gates, 0), (dgb, w_gates, 1), (dab, w_ab, 0)]
    if late is None:
        du = _mm_nt_parts(pairs, "d_u")
    else:
        du, *G['_arrived_w_in'] = _mm_nt_parts(pairs, "d_u", side=(late['w_in_slabs'](G), True))
    (dx,), (G['norm1_g'],) = _pw_bwd("norm1_bwd", _rms_fn, [x], [g1], [(du,)], 512, add_to_first=dx1)
    return loss, dx, G


IN_WIDTH = OFF_AB + 8
PAD_ORDER = ((0, OFF_GATES), (OFF_GATES + 8, IN_WIDTH), (OFF_GATES, OFF_GATES + 8))


def _pad_w_in_shards(shards):
    width = shards[0].shape[1]
    parts = []
    for a, b in PAD_ORDER:
        for j, sh in enumerate(shards):
            lo, hi = max(a, j * width), min(b, (j + 1) * width)
            if lo < hi:
                parts.append(sh[:, lo - j * width:hi - j * width])
    return jnp.concatenate(parts + [jnp.zeros((shards[0].shape[0], W_AB - 8), shards[0].dtype)], axis=1)


def _padded_cols(sections, s, e):
    pieces = [arr[:, max(s, o) - o:min(e, o + arr.shape[1]) - o] for o, arr in sections if max(s, o) < min(e, o + arr.shape[1])]
    return pieces[0] if len(pieces) == 1 else jnp.concatenate(pieces, axis=1)


def _unpad_cols(sections, lo, hi):
    parts, off = [], 0
    for a, b in PAD_ORDER:
        l, h = max(a, lo), min(b, hi)
        if l < h:
            parts.append((l, _padded_cols(sections, off + l - a, off + h - a)))
        off += b - a
    parts.sort(key=lambda t: t[0])
    return parts[0][1] if len(parts) == 1 else jnp.concatenate([p for _, p in parts], axis=1)


BIG = ('w_in', 'rwkv_proj', 'gdn_proj', 'w_out', 'ffn_up', 'ffn_down')
SMALL_SHARDED = ('rwkv_w2', 'rwkv_a2', 'rwkv_g2', 'gdn_conv_w', 'ffn_conv_w')


def _rows128(shape):
    n = 1
    for d in shape:
        n *= d
    return -(-n // LANES)


def _pack128(arrays):
    parts = []
    for a in arrays:
        flat = a.reshape(-1)
        rows = _rows128(a.shape)
        parts.append(jnp.pad(flat, (0, rows * LANES - flat.shape[0])).reshape(rows, LANES))
    buf = jnp.concatenate(parts, axis=0)
    return jnp.pad(buf, ((0, -buf.shape[0] % HALO), (0, 0)))


def _unpack128(buf, shapes):
    out, off = [], 0
    for s in shapes:
        rows, n = _rows128(s), 1
        for d in s:
            n *= d
        out.append(buf[off:off + rows].reshape(-1)[:n].reshape(s))
        off += rows
    return out


def _param_tile(r, c):
    best = None
    for d in range(2 * HALO, r + 1, 2 * HALO):
        if r % d == 0 and d * c * 4 <= TILE_BYTES:
            best = d
    if best is not None or r * c * 4 <= TILE_BYTES:
        return (best if best is not None else r), c
    return r, 128


def _norm_gather(name, x, g, bufs, tm):
    n, (T, D) = len(bufs), x.shape
    steps = T // tm

    def body(x_ref, g_ref, *refs):
        side_in, u_ref, side_out, sems = refs[:n], refs[n], refs[n + 1:2 * n + 1], refs[2 * n + 1:]
        i = pl.program_id(0)
        start, finish = _xy_copies(side_in, side_out, sems, False)
        pl.when(i == 0)(start)
        u_ref[...] = _rms(x_ref[...], g_ref[...]).astype(bf16)
        pl.when(i == steps - 1)(finish)

    any_spec = pl.BlockSpec(memory_space=pl.ANY)
    return pl.pallas_call(
        body, grid=(steps,), in_specs=[_row_spec(tm, D), _full_spec(g.shape)] + [any_spec] * n,
        out_specs=[_row_spec(tm, D)] + [any_spec] * n,
        out_shape=[jax.ShapeDtypeStruct((T, D), bf16)] + _xy_out_shapes(bufs, False),
        scratch_shapes=_xy_sems(n, False), name=name, compiler_params=_params("arbitrary"))(x, g, *bufs)


def _sibling_exchange(name, bufs):
    n = len(bufs)

    def body(*refs):
        in_refs, out_refs, send_sems, recv_sems = refs[:n], refs[n:2 * n], refs[2 * n], refs[2 * n + 1]
        x, y, c = lax.axis_index("x"), lax.axis_index("y"), lax.axis_index("c")
        copies = [pltpu.make_async_remote_copy(
            src_ref=in_refs[a], dst_ref=out_refs[a], send_sem=send_sems.at[a], recv_sem=recv_sems.at[a],
            device_id=(x, y, 1 - c), device_id_type=pl.DeviceIdType.MESH) for a in range(n)]
        for cp in copies:
            cp.start()
        for cp in copies:
            cp.wait()

    return pl.pallas_call(
        body, in_specs=[pl.BlockSpec(memory_space=pl.ANY)] * n, out_specs=[pl.BlockSpec(memory_space=pl.ANY)] * n,
        out_shape=[jax.ShapeDtypeStruct(b.shape, b.dtype) for b in bufs],
        scratch_shapes=[pltpu.SemaphoreType.DMA((n,)), pltpu.SemaphoreType.DMA((n,))], name=name)(*bufs)


def _sum_slots(name, buf, side=None):
    _, R, L = buf.shape
    tr, tc = _param_tile(R, L)
    grid = (R // tr, L // tc)
    n_side = 0 if side is None else len(side)

    def body(b_ref, *refs):
        side_in, o_ref, side_out, sems = refs[:n_side], refs[n_side], refs[n_side + 1:2 * n_side + 1], refs[2 * n_side + 1:]
        i, j = pl.program_id(0), pl.program_id(1)
        if side is not None:
            start, finish = _xy_copies(side_in, side_out, sems, True)
            pl.when((i == 0) & (j == 0))(start)
        part = lambda s: b_ref[s].astype(f32)
        o_ref[...] = ((part(0) + part(1)) + part(2)) + part(3)
        if side is not None:
            pl.when((i == grid[0] - 1) & (j == grid[1] - 1))(finish)

    any_spec = pl.BlockSpec(memory_space=pl.ANY)
    side_bufs = [] if side is None else list(side)
    outs = pl.pallas_call(
        body, grid=grid,
        in_specs=[pl.BlockSpec((N_POS, tr, tc), lambda i, j: (0, i, j))] + [any_spec] * n_side,
        out_specs=[pl.BlockSpec((tr, tc), lambda i, j: (i, j))] + [any_spec] * n_side,
        out_shape=[jax.ShapeDtypeStruct((R, L), f32)] + (_xy_out_shapes(side_bufs, True) if side is not None else []),
        scratch_shapes=_xy_sems(n_side, True) if side is not None else [], name=name,
        compiler_params=_params(*(("arbitrary",) * 2 if side is not None else ("parallel",) * 2)))(buf, *side_bufs)
    return list(outs) if side is not None else outs[0]


def _adamw(name, w, ga, gb, m, v):
    R, L = w.shape
    tr, tc = _param_tile(R, L)
    c1 = 1.0 / (1.0 - ADAM_B1 ** ADAM_STEP)
    c2 = 1.0 / (1.0 - ADAM_B2 ** ADAM_STEP)

    def body(w_ref, ga_ref, gb_ref, m_ref, v_ref, g_out, d_out, m_out, v_out):
        g = ga_ref[...] + gb_ref[...]
        m_new = ADAM_B1 * m_ref[...] + (1.0 - ADAM_B1) * g
        v_new = ADAM_B2 * v_ref[...] + (1.0 - ADAM_B2) * (g * g)
        g_out[...] = g
        m_out[...] = m_new
        v_out[...] = v_new
        d_out[...] = -ADAM_LR * ((m_new * c1) / (jnp.sqrt(v_new * c2) + ADAM_EPS) + ADAM_WD * w_ref[...])

    spec = pl.BlockSpec((tr, tc), lambda i, j: (i, j))
    return pl.pallas_call(
        body, grid=(R // tr, L // tc), in_specs=[spec] * 5, out_specs=[spec] * 4,
        out_shape=[jax.ShapeDtypeStruct((R, L), f32)] * 4, name=name,
        compiler_params=_params("parallel", "parallel"))(w, ga, gb, m, v)


def _step(x, loss_target, P, M, V):
    shapes = {n: tuple(P[n].shape) for n in WEIGHTS}
    sh_shapes = [shapes[n] for n in SMALL_SHARDED]
    packed = SMALL_SHARDED + SMALL

    def whole(n, g):
        return g.reshape(-1, g.shape[2]) if n in ROW_SHARDED else jnp.concatenate([g[j] for j in range(N_POS)], axis=1)

    def slabs(G, n, dtype=f32):
        r, c = shapes[n]
        full = G[n].astype(dtype)
        if full.ndim == 3:
            return full
        return full.reshape(N_POS, r, c) if n in ROW_SHARDED else full.reshape(r, N_POS, c).transpose(1, 0, 2)

    u, g_w_in, g_small = _norm_gather("norm1_gather_w_in", x, P['norm1_g'].reshape(1, -1),
                                      [P['w_in'].astype(bf16), _pack128([P[n] for n in SMALL_SHARDED])], 512)
    W = {n: P[n] for n in SMALL}
    W['w_in_pad'] = _pad_w_in_shards([g_w_in[j] for j in range(N_POS)])
    per_pos = [_unpack128(g_small[j], sh_shapes) for j in range(N_POS)]
    for q, n in enumerate(SMALL_SHARDED):
        W[n] = jnp.concatenate([per_pos[j][q] for j in range(N_POS)], axis=1)
    groups = (('rwkv_proj', 'gdn_proj', 'ffn_up'), ('w_out', 'ffn_down'))
    late = dict(u=u, shards=[[P[n].astype(bf16) for n in grp] for grp in groups],
                assemble=lambda q, gathered: {n: whole(n, g) for n, g in zip(groups[q], gathered)},
                slabs=lambda G, q: [slabs(G, n, bf16) for n in groups[q]],
                w_in_slabs=lambda G: [jnp.stack([_unpad_cols(G['w_in_pad'], j * shapes['w_in'][1], (j + 1) * shapes['w_in'][1])
                                                 for j in range(N_POS)])])

    loss_rows, dx, G = _local_step(x, loss_target, W, late)
    arrived = {n: a for grp, got in zip(groups, G.pop('_arrived')) for n, a in zip(grp, got)}
    (arrived_w_in,) = G.pop('_arrived_w_in')
    G.pop('w_in_pad')

    small_slabs = jnp.stack([_pack128([slabs(G, n)[j] for n in SMALL_SHARDED] + [G[n] for n in SMALL]) for j in range(N_POS)])
    plane_w_in, arrived_small = _sum_slots("sum_w_in", arrived_w_in, side=[small_slabs])
    contributions = [arrived[n] for n in BIG[1:]] + [arrived_small]
    tags = list(BIG[1:]) + ['small']
    plane = [plane_w_in] + [_sum_slots("sum_" + t, cbuf) for t, cbuf in zip(tags, contributions)]
    sibling = _sibling_exchange("sibling_grads", plane)

    out = {}
    names4 = ('grad', 'delta', 'new_m', 'new_v')
    for q, n in enumerate(BIG):
        tr = (lambda t: t.T) if n == 'w_in' else (lambda t: t)
        for tag, t in zip(names4, _adamw("adamw_" + n, tr(P[n]), tr(plane[q]), tr(sibling[q]), tr(M[n]), tr(V[n]))):
            out[tag + '_' + n] = tr(t)
    small_out = _adamw("adamw_small", _pack128([P[n] for n in packed]), plane[-1], sibling[-1],
                       _pack128([M[n] for n in packed]), _pack128([V[n] for n in packed]))
    for tag, buf in zip(names4, small_out):
        for n, t in zip(packed, _unpack128(buf, [shapes[n] for n in packed])):
            out[tag + '_' + n] = t
    loss = lax.psum(loss_rows[0, 0], ("x", "y", "c"))
    return loss, dx, out


def kernel(x, norm1_g, w_in, rwkv_mu, rwkv_w0, rwkv_w2, rwkv_a0, rwkv_a2, rwkv_g2, rwkv_k_k, rwkv_k_a, rwkv_r_k, rwkv_ln_w, rwkv_ln_b, rwkv_proj, gdn_conv_w, gdn_a_log, gdn_dt_bias, gdn_norm_w, gdn_proj, w_out, norm2_g, ffn_up, ffn_conv_w, ffn_down, final_g, loss_target, m_norm1_g, m_w_in, m_rwkv_mu, m_rwkv_w0, m_rwkv_w2, m_rwkv_a0, m_rwkv_a2, m_rwkv_g2, m_rwkv_k_k, m_rwkv_k_a, m_rwkv_r_k, m_rwkv_ln_w, m_rwkv_ln_b, m_rwkv_proj, m_gdn_conv_w, m_gdn_a_log, m_gdn_dt_bias, m_gdn_norm_w, m_gdn_proj, m_w_out, m_norm2_g, m_ffn_up, m_ffn_conv_w, m_ffn_down, m_final_g, v_norm1_g, v_w_in, v_rwkv_mu, v_rwkv_w0, v_rwkv_w2, v_rwkv_a0, v_rwkv_a2, v_rwkv_g2, v_rwkv_k_k, v_rwkv_k_a, v_rwkv_r_k, v_rwkv_ln_w, v_rwkv_ln_b, v_rwkv_proj, v_gdn_conv_w, v_gdn_a_log, v_gdn_dt_bias, v_gdn_norm_w, v_gdn_proj, v_w_out, v_norm2_g, v_ffn_up, v_ffn_conv_w, v_ffn_down, v_final_g):
    weights = (norm1_g, w_in, rwkv_mu, rwkv_w0, rwkv_w2, rwkv_a0, rwkv_a2, rwkv_g2, rwkv_k_k, rwkv_k_a, rwkv_r_k, rwkv_ln_w,
               rwkv_ln_b, rwkv_proj, gdn_conv_w, gdn_a_log, gdn_dt_bias, gdn_norm_w, gdn_proj, w_out, norm2_g, ffn_up,
               ffn_conv_w, ffn_down, final_g)
    m_in = (m_norm1_g, m_w_in, m_rwkv_mu, m_rwkv_w0, m_rwkv_w2, m_rwkv_a0, m_rwkv_a2, m_rwkv_g2, m_rwkv_k_k, m_rwkv_k_a,
            m_rwkv_r_k, m_rwkv_ln_w, m_rwkv_ln_b, m_rwkv_proj, m_gdn_conv_w, m_gdn_a_log, m_gdn_dt_bias, m_gdn_norm_w,
            m_gdn_proj, m_w_out, m_norm2_g, m_ffn_up, m_ffn_conv_w, m_ffn_down, m_final_g)
    v_in = (v_norm1_g, v_w_in, v_rwkv_mu, v_rwkv_w0, v_rwkv_w2, v_rwkv_a0, v_rwkv_a2, v_rwkv_g2, v_rwkv_k_k, v_rwkv_k_a,
            v_rwkv_r_k, v_rwkv_ln_w, v_rwkv_ln_b, v_rwkv_proj, v_gdn_conv_w, v_gdn_a_log, v_gdn_dt_bias, v_gdn_norm_w,
            v_gdn_proj, v_w_out, v_norm2_g, v_ffn_up, v_ffn_conv_w, v_ffn_down, v_final_g)
    drop = lambda n, a: a if n == 'final_g' else a[0]
    P = {n: drop(n, a) for n, a in zip(WEIGHTS, weights)}
    M = {n: drop(n, a) for n, a in zip(WEIGHTS, m_in)}
    V = {n: drop(n, a) for n, a in zip(WEIGHTS, v_in)}
    loss, dx, out = _step(x[0], loss_target[0], P, M, V)
    lift = lambda n, a: a if n == 'final_g' else a[None]
    res = [loss, dx[None]]
    for tag in ('grad', 'delta', 'new_m', 'new_v'):
        res += [lift(n, out[tag + '_' + n]) for n in WEIGHTS]
    return tuple(res)
```

```python
import functools

import jax
import jax.numpy as jnp
from jax import lax
from jax.experimental import pallas as pl
from jax.experimental.pallas import tpu as pltpu

f32 = jnp.float32
bf16 = jnp.bfloat16

D_MODEL = 1024
RWKV_HEADS, RWKV_HD, RWKV_W = 8, 64, 512
GDN_HEADS, GDN_HD, GDN_W = 4, 128, 512
NORM_EPS, L2_EPS, GN_EPS = 1e-6, 1e-6, 64e-5
W_AB = 256
OFF_QKV, OFF_Z, OFF_GATES, OFF_AB = 1792, 3328, 3840, 5888
W_IN_PAD = OFF_AB + W_AB
WKV_CHUNK, WKV_PER_STEP = 64, 4
GDN_CHUNK, GDN_PER_STEP = 128, 4
HALO = 8
LANES = 128
TILE_BYTES = 1 << 20
VMEM_LIMIT = 56 * 1024 * 1024

ADAM_LR, ADAM_B1, ADAM_B2, ADAM_EPS, ADAM_WD, ADAM_STEP = 0.001, 0.9, 0.999, 1e-08, 0.01, 10

ROW_SHARDED = ('w_out', 'ffn_down')
SMALL = ('norm1_g', 'rwkv_mu', 'rwkv_w0', 'rwkv_a0', 'rwkv_k_k', 'rwkv_k_a', 'rwkv_r_k', 'rwkv_ln_w', 'rwkv_ln_b',
         'gdn_a_log', 'gdn_dt_bias', 'gdn_norm_w', 'norm2_g', 'final_g')
WEIGHTS = ('norm1_g', 'w_in', 'rwkv_mu', 'rwkv_w0', 'rwkv_w2', 'rwkv_a0', 'rwkv_a2', 'rwkv_g2', 'rwkv_k_k', 'rwkv_k_a',
           'rwkv_r_k', 'rwkv_ln_w', 'rwkv_ln_b', 'rwkv_proj', 'gdn_conv_w', 'gdn_a_log', 'gdn_dt_bias', 'gdn_norm_w',
           'gdn_proj', 'w_out', 'norm2_g', 'ffn_up', 'ffn_conv_w', 'ffn_down', 'final_g')


def _params(*sem):
    return pltpu.CompilerParams(dimension_semantics=sem, vmem_limit_bytes=VMEM_LIMIT)


def _tile(n, limit):
    if n <= limit:
        return n
    best = None
    for d in range(128, limit + 1, 128):
        if n % d == 0:
            best = d
    if best is None:
        raise ValueError(f"no tile for {n} under {limit}")
    return best


MM_BLOCK_BYTES = 6 << 20
MM_MAX_COLS = 1536


def _mm(a, b, mode, name, add=None, out_dtype=f32, side=None, col_slabs=None):
    if mode == 'nn':
        (M, K), N = a.shape, b.shape[1]
    elif mode == 'nt':
        (M, K), N = a.shape, b.shape[0]
    else:
        (K, M), N = a.shape, b.shape[1]
    tm = _tile(M, 1408)
    tk = _tile(K, min(2816, MM_BLOCK_BYTES // (tm * a.dtype.itemsize)))
    tn = _tile(N, max(128, min(MM_BLOCK_BYTES // (tk * b.dtype.itemsize), MM_BLOCK_BYTES // (tm * 4), MM_MAX_COLS) // 128 * 128))
    if col_slabs is not None:
        tn = N // col_slabs
    nk = K // tk
    grid = (M // tm, N // tn, nk)
    dn = {'nn': (((1,), (0,)), ((), ())), 'nt': (((1,), (1,)), ((), ())), 'tn': (((0,), (0,)), ((), ()))}[mode]
    n_add = 0 if add is None else 1
    n_side = 0 if side is None else len(side[0])

    def body(a_ref, b_ref, *rest):
        add_ref = rest[0] if add is not None else None
        side_in, rest = rest[n_add:n_add + n_side], rest[n_add + n_side:]
        o_ref, side_out, rest = rest[0], rest[1:1 + n_side], rest[1 + n_side:]
        acc_ref, rest = (rest[0], rest[1:]) if nk > 1 else (None, rest)
        ids = [pl.program_id(d) for d in range(3)]
        if side is not None:
            start, finish = _xy_copies(side_in, side_out, rest, side[1])
            pl.when((ids[0] == 0) & (ids[1] == 0) & (ids[2] == 0))(start)
        acc = lax.dot_general(a_ref[...].astype(bf16), b_ref[...].astype(bf16), dn, preferred_element_type=f32)
        if nk == 1:
            o_ref[...] = (acc + add_ref[...] if add is not None else acc).astype(out_dtype)
        else:
            k = ids[2]

            @pl.when(k == 0)
            def _():
                acc_ref[...] = acc + add_ref[...] if add is not None else acc

            @pl.when(k > 0)
            def _():
                acc_ref[...] += acc

            @pl.when(k == nk - 1)
            def _():
                o_ref[...] = acc_ref[...].astype(out_dtype)
        if side is not None:
            pl.when((ids[0] == grid[0] - 1) & (ids[1] == grid[1] - 1) & (ids[2] == nk - 1))(finish)

    a_spec = (pl.BlockSpec((tk, tm), lambda i, j, k: (k, i)) if mode == 'tn'
              else pl.BlockSpec((tm, tk), lambda i, j, k: (i, k)))
    b_spec = (pl.BlockSpec((tn, tk), lambda i, j, k: (j, k)) if mode == 'nt'
              else pl.BlockSpec((tk, tn), lambda i, j, k: (k, j)))
    o_spec = pl.BlockSpec((tm, tn), lambda i, j, k: (i, j))
    o_shape = jax.ShapeDtypeStruct((M, N), out_dtype)
    if col_slabs is not None:
        o_spec = pl.BlockSpec((None, tm, tn), lambda i, j, k: (j, i, 0))
        o_shape = jax.ShapeDtypeStruct((col_slabs, M, tn), out_dtype)
    any_spec = pl.BlockSpec(memory_space=pl.ANY)
    side_bufs = [] if side is None else list(side[0])
    ins, specs = [a, b], [a_spec, b_spec]
    if add is not None:
        ins.append(add)
        specs.append(o_spec)
    outs = pl.pallas_call(
        body, grid=grid, in_specs=specs + [any_spec] * n_side, out_specs=[o_spec] + [any_spec] * n_side,
        out_shape=[o_shape] + (_xy_out_shapes(side_bufs, side[1]) if side is not None else []),
        scratch_shapes=([pltpu.VMEM((tm, tn), f32)] if nk > 1 else []) + (_xy_sems(n_side, side[1]) if side is not None else []),
        name=name,
        compiler_params=_params(*(("arbitrary",) * 3 if side is not None else ("parallel", "parallel", "arbitrary"))))(
            *ins, *side_bufs)
    return list(outs) if side is not None else outs[0]


PARTS_TILE = 512
PARTS_DEPTH = 1024


def _mm_nt_parts(pairs, name, side=None):
    n, M, N = len(pairs), pairs[0][0].shape[0], pairs[0][1].shape[0]
    tm = min(M, PARTS_TILE)
    steps = M // tm
    n_side = 0 if side is None else len(side[0])

    def body(*refs):
        a_refs, b_refs, side_in = refs[:n], refs[n:2 * n], refs[2 * n:2 * n + n_side]
        o_ref, side_out, sems = refs[2 * n + n_side], refs[2 * n + n_side + 1:2 * n + 2 * n_side + 1], refs[2 * n + 2 * n_side + 1:]
        i = pl.program_id(0)
        if side is not None:
            start, finish = _xy_copies(side_in, side_out, sems, side[1])
            pl.when(i == 0)(start)
        acc = None
        for a_ref, b_ref in zip(a_refs, b_refs):
            part = lax.dot_general(a_ref[...].astype(bf16), b_ref[...].astype(bf16), (((1,), (1,)), ((), ())),
                                   preferred_element_type=f32)
            acc = part if acc is None else acc + part
        o_ref[...] = acc
        if side is not None:
            pl.when(i == steps - 1)(finish)

    any_spec = pl.BlockSpec(memory_space=pl.ANY)
    side_bufs = [] if side is None else list(side[0])
    o_spec = pl.BlockSpec((tm, N), lambda i: (i, 0))
    outs = pl.pallas_call(
        body, grid=(steps,),
        in_specs=[pl.BlockSpec((tm, a.shape[1]), lambda i: (i, 0)) for a, _, _ in pairs]
        + [pl.BlockSpec((N, a.shape[1]), lambda i, col=col: (0, col)) for a, _, col in pairs] + [any_spec] * n_side,
        out_specs=[o_spec] + [any_spec] * n_side,
        out_shape=[jax.ShapeDtypeStruct((M, N), f32)] + (_xy_out_shapes(side_bufs, side[1]) if side is not None else []),
        scratch_shapes=_xy_sems(n_side, side[1]) if side is not None else [],
        name=name, compiler_params=_params("arbitrary" if side is not None else "parallel"))(
            *[a for a, _, _ in pairs], *[b for _, b, _ in pairs], *side_bufs)
    return list(outs) if side is not None else outs[0]


def _mm_tn_parts(a, parts, name):
    n, (K, M) = len(parts), a.shape
    tm, tk = min(M, PARTS_TILE), min(K, PARTS_DEPTH)
    nk = K // tk
    widths = [p.shape[1] for p in parts]
    offs = [sum(widths[:q]) for q in range(n)]

    def body(a_ref, *refs):
        p_refs, o_refs, acc_ref = refs[:n], refs[n:2 * n], refs[2 * n]
        k = pl.program_id(1)
        lhs = a_ref[...].astype(bf16)
        for p_ref, o_ref, off, w in zip(p_refs, o_refs, offs, widths):
            part = lax.dot_general(lhs, p_ref[...].astype(bf16), (((0,), (0,)), ((), ())), preferred_element_type=f32)
            if nk == 1:
                o_ref[...] = part.astype(bf16)
                continue
            cols = (slice(None), slice(off, off + w))

            @pl.when(k == 0)
            def _():
                acc_ref[cols] = part

            @pl.when(k > 0)
            def _():
                acc_ref[cols] += part

            @pl.when(k == nk - 1)
            def _():
                o_ref[...] = acc_ref[cols].astype(bf16)

    return pl.pallas_call(
        body, grid=(M // tm, nk),
        in_specs=[pl.BlockSpec((tk, tm), lambda i, k: (k, i))] + [pl.BlockSpec((tk, w), lambda i, k: (k, 0)) for w in widths],
        out_specs=[pl.BlockSpec((tm, w), lambda i, k: (i, 0)) for w in widths],
        out_shape=[jax.ShapeDtypeStruct((M, w), bf16) for w in widths],
        scratch_shapes=[pltpu.VMEM((tm, sum(widths)), f32)],
        name=name, compiler_params=_params("parallel", "arbitrary"))(a, *parts)


def _shift_down(cur, prev, s):
    if s == 0:
        return cur
    ext = jnp.concatenate([prev, cur], axis=0)
    return pltpu.roll(ext, s, 0)[HALO:]


def _shift_up(cur, nxt, s):
    if s == 0:
        return cur
    ext = jnp.concatenate([cur, nxt], axis=0)
    return pltpu.roll(ext, ext.shape[0] - s, 0)[:cur.shape[0]]


def _conv_apply(cur, prev, w_ref, shifted=None):
    taps = w_ref.shape[0]
    out = None
    for i in range(taps):
        s = taps - 1 - i
        term = (shifted[s] if shifted is not None else _shift_down(cur, prev, s)) * w_ref[pl.ds(i, 1), :]
        out = term if out is None else out + term
    return out


def _row_spec(tm, w, col=0):
    return pl.BlockSpec((tm, w), lambda i: (i, col))


def _cols(a, width, col):
    return (a, width, col)


def _row_of(r):
    return r if isinstance(r, tuple) else (r, r.shape[1], 0)


def _prev_spec(tm, w):
    return pl.BlockSpec((HALO, w), lambda i: (jnp.maximum(i * (tm // HALO) - 1, 0), 0))


def _next_spec(tm, w, T):
    return pl.BlockSpec((HALO, w), lambda i: (jnp.minimum((i + 1) * (tm // HALO), T // HALO - 1), 0))


def _full_spec(shape):
    return pl.BlockSpec(shape, lambda i: (0,) * len(shape))


def _pw_fwd(name, fn, rows, consts, out_widths, tm, conv_w=None, out_dtype=f32):
    T = _row_of(rows[0])[0].shape[0]
    nr, nc = len(rows), len(consts)

    def body(*refs):
        i = pl.program_id(0)
        vals = [r[...] for r in refs[:nr]]
        p = nr
        if conv_w is not None:
            prev = jnp.where(i > 0, refs[p][...], 0.0)
            vals[0] = _conv_apply(vals[0], prev, refs[p + 1])
            p += 2
        cvals = [r[...] for r in refs[p:p + nc]]
        outs = fn(*vals, *cvals)
        for o_ref, o in zip(refs[p + nc:], outs):
            o_ref[...] = o.astype(out_dtype)

    ins = [_row_of(r)[0] for r in rows]
    specs = [_row_spec(tm, *_row_of(r)[1:]) for r in rows]
    if conv_w is not None:
        ins += [rows[0], conv_w]
        specs += [_prev_spec(tm, rows[0].shape[1]), _full_spec(conv_w.shape)]
    ins += list(consts)
    specs += [_full_spec(c.shape) for c in consts]
    outs = pl.pallas_call(
        body, grid=(T // tm,), in_specs=specs,
        out_specs=[_row_spec(tm, w) for w in out_widths],
        out_shape=[jax.ShapeDtypeStruct((T, w), out_dtype) for w in out_widths], name=name,
        compiler_params=_params("parallel"))(*ins)
    return outs


def _pw_bwd(name, fn, rows, consts, cots, tm, add_to_first=None, row_dtypes=None):
    rows = [_row_of(r) for r in rows]
    T = rows[0][0].shape[0]
    nr, nc = len(rows), len(consts)
    flat_cots = [c for grp in cots for c in grp]
    row_dtypes = row_dtypes or [f32] * nr
    n_extra = 0 if add_to_first is None else 1

    def body(*refs):
        i = pl.program_id(0)
        in_refs, cot_refs = refs[:nr + nc], refs[nr + nc:nr + nc + len(flat_cots)]
        extra_ref = refs[nr + nc + len(flat_cots)] if add_to_first is not None else None
        row_out = refs[nr + nc + len(flat_cots) + n_extra:][:nr]
        const_out = refs[nr + nc + len(flat_cots) + n_extra + nr:]

        @pl.when(i == 0)
        def _():
            for q in range(nc):
                const_out[q][...] = jnp.zeros_like(const_out[q])

        def part(sl):
            cot_vals, p = [], 0
            for grp in cots:
                acc = cot_refs[p][:, sl]
                for q in range(1, len(grp)):
                    acc = acc + cot_refs[p + q][:, sl]
                p += len(grp)
                cot_vals.append(acc)
            _, vjp = jax.vjp(fn, *[r[:, sl] for r in in_refs])
            grads = vjp(tuple(cot_vals))
            for q in range(nr):
                g = grads[q]
                if q == 0 and extra_ref is not None:
                    g = g + extra_ref[:, sl]
                row_out[q][:, sl] = g.astype(row_dtypes[q])
            for q in range(nc):
                const_out[q][:, sl] += grads[nr + q]

        part(slice(None))

    ins = [r[0] for r in rows] + list(consts) + flat_cots
    specs = ([_row_spec(tm, r[1], r[2]) for r in rows] + [_full_spec(c.shape) for c in consts]
             + [_row_spec(tm, c.shape[1]) for c in flat_cots])
    if add_to_first is not None:
        ins.append(add_to_first)
        specs.append(_row_spec(tm, add_to_first.shape[1]))
    out_shapes = ([jax.ShapeDtypeStruct((T, r[1]), d) for r, d in zip(rows, row_dtypes)]
                  + [jax.ShapeDtypeStruct(c.shape, f32) for c in consts])
    out_specs = [_row_spec(tm, r[1]) for r in rows] + [_full_spec(c.shape) for c in consts]
    outs = pl.pallas_call(
        body, grid=(T // tm,), in_specs=specs, out_specs=out_specs, out_shape=out_shapes, name=name,
        compiler_params=_params("arbitrary"))(*ins)
    return list(outs[:nr]), list(outs[nr:])


def _pw_conv_bwd(name, fn, rows, consts, cots, conv_w, tm, row_dtypes=None):
    T, W0 = rows[0].shape
    nr, nc = len(rows), len(consts)
    taps = conv_w.shape[0]
    nblk = T // tm
    flat_cots = [c for grp in cots for c in grp]
    row_dtypes = row_dtypes or [f32] * nr

    def body(*refs):
        i = pl.program_id(0)
        p = 0
        cur = [r[...] for r in refs[p:p + nr]]; p += nr
        nxt = [r[...] for r in refs[p:p + nr]]; p += nr
        prev = jnp.where(i > 0, refs[p][...], 0.0); p += 1
        w_ref = refs[p]; p += 1
        cvals = [r[...] for r in refs[p:p + nc]]; p += nc

        def summed(p0):
            out, q = [], p0
            for grp in cots:
                acc = refs[q][...]
                for t in range(1, len(grp)):
                    acc = acc + refs[q + t][...]
                q += len(grp)
                out.append(acc)
            return out, q

        cot_cur, p = summed(p)
        cot_nxt, p = summed(p)
        row_out, dw_ref, const_out = refs[p:p + nr], refs[p + nr], refs[p + nr + 1:]

        x_cur = cur[0]
        x_down = [_shift_down(x_cur, prev, s_) for s_ in range(taps)]
        _, vjp = jax.vjp(fn, _conv_apply(x_cur, prev, w_ref, x_down), *cur[1:], *cvals)
        grads = vjp(tuple(cot_cur))
        _, vjp_n = jax.vjp(fn, _conv_apply(nxt[0], x_cur[tm - HALO:], w_ref), *nxt[1:], *cvals)
        dc_n = jnp.where(i < nblk - 1, vjp_n(tuple(cot_nxt))[0], 0.0)
        dc = grads[0]

        @pl.when(i == 0)
        def _():
            dw_ref[...] = jnp.zeros_like(dw_ref)
            for q in range(nc):
                const_out[q][...] = jnp.zeros_like(const_out[q])

        dx = None
        for k in range(taps):
            s_ = taps - 1 - k
            term = _shift_up(dc, dc_n, s_) * w_ref[pl.ds(k, 1), :]
            dx = term if dx is None else dx + term
            dw_ref[pl.ds(k, 1), :] += jnp.sum(dc * x_down[s_], axis=0, keepdims=True)
        row_out[0][...] = dx.astype(row_dtypes[0])
        for q in range(1, nr):
            row_out[q][...] = grads[q].astype(row_dtypes[q])
        for q in range(nc):
            const_out[q][...] += grads[nr + q]

    ins = list(rows) + list(rows) + [rows[0], conv_w] + list(consts) + flat_cots + flat_cots
    specs = ([_row_spec(tm, r.shape[1]) for r in rows] + [_next_spec(tm, r.shape[1], T) for r in rows]
             + [_prev_spec(tm, W0), _full_spec(conv_w.shape)] + [_full_spec(c.shape) for c in consts]
             + [_row_spec(tm, c.shape[1]) for c in flat_cots] + [_next_spec(tm, c.shape[1], T) for c in flat_cots])
    out_shapes = ([jax.ShapeDtypeStruct(r.shape, d) for r, d in zip(rows, row_dtypes)]
                  + [jax.ShapeDtypeStruct(conv_w.shape, f32)] + [jax.ShapeDtypeStruct(c.shape, f32) for c in consts])
    out_specs = ([_row_spec(tm, r.shape[1]) for r in rows] + [_full_spec(conv_w.shape)]
                 + [_full_spec(c.shape) for c in consts])
    outs = pl.pallas_call(
        body, grid=(nblk,), in_specs=specs, out_specs=out_specs, out_shape=out_shapes, name=name,
        compiler_params=_params("arbitrary"))(*ins)
    return list(outs[:nr]), outs[nr], list(outs[nr + 1:])


def _sigmoid(x):
    return 0.5 * jnp.tanh(0.5 * x) + 0.5


def _softplus(x):
    return jnp.maximum(x, 0.0) + jnp.log(1.0 + jnp.exp(jnp.minimum(x, -x)))


def _seg_sum_impl(x, seg):
    w = x.shape[-1]
    r = lax.broadcasted_iota(jnp.int32, (w, w), 0) // seg
    c = lax.broadcasted_iota(jnp.int32, (w, w), 1) // seg
    ones = (r == c).astype(bf16)
    hi = x.astype(bf16)
    lo = (x - hi.astype(f32)).astype(bf16)
    return (jnp.dot(hi, ones, preferred_element_type=f32) + jnp.dot(lo, ones, preferred_element_type=f32))


@functools.partial(jax.custom_vjp, nondiff_argnums=(1,))
def _seg_sum(x, seg):
    return _seg_sum_impl(x, seg)


_seg_sum.defvjp(lambda x, seg: (_seg_sum_impl(x, seg), None), lambda seg, _, g: (_seg_sum_impl(g, seg),))


def _rms(x, g):
    return x * lax.rsqrt(jnp.mean(x * x, axis=-1, keepdims=True) + NORM_EPS) * g


def _rms_fn(x, g):
    return (_rms(x, g),)


def _loss_rows(x2, tgt, g):
    e = _rms(x2, g) - tgt
    return 0.5 * jnp.sum(e * e, axis=-1, keepdims=True) * (1.0 / D_MODEL)


@jax.custom_vjp
def _dot_lo(a, b):
    return jnp.dot(a.astype(bf16), b.astype(bf16), preferred_element_type=f32)


def _dot_lo_bwd(ab, g):
    a, b = ab
    gl = g.astype(bf16)
    return (lax.dot_general(gl, b.astype(bf16), (((1,), (1,)), ((), ())), preferred_element_type=f32),
            lax.dot_general(a.astype(bf16), gl, (((0,), (0,)), ((), ())), preferred_element_type=f32))


_dot_lo.defvjp(lambda a, b: (_dot_lo(a, b), (a, b)), _dot_lo_bwd)


def _rwkv_prep_fn(ps, w0, w2p, a0, a2p, g2, k_k, k_a):
    r, k, v = ps[:, 0:512], ps[:, 512:1024], ps[:, 1024:1536]
    wa, gl = ps[:, 1536:1664], ps[:, 1664:1792]
    z = w0 + _dot_lo(jnp.tanh(wa), w2p)
    w_log = -_softplus(-z) - 0.5
    lw = -jnp.exp(w_log)
    a = _sigmoid(a0 + _dot_lo(wa, a2p))
    g = _dot_lo(_sigmoid(gl), g2)
    kx = k * k_k
    kk = kx * lax.rsqrt(_seg_sum(kx * kx, RWKV_HD) + L2_EPS)
    k2 = k * (1.0 + (a - 1.0) * k_a)
    return r, lw, k2, v, -kk, kk * a, g


def _rwkv_post_fn(y, r, k2, v, g, ln_w, ln_b, rk):
    mean = _seg_sum(y, RWKV_HD) * (1.0 / RWKV_HD)
    yc = y - mean
    var = _seg_sum(yc * yc, RWKV_HD) * (1.0 / RWKV_HD)
    yn = yc * lax.rsqrt(var + GN_EPS) * ln_w + ln_b
    bonus = _seg_sum(r * k2 * rk, RWKV_HD) * v
    return ((yn + bonus) * g,)


def _gdn_prep_fn(cq, ck, cv):
    silu = lambda c: c * _sigmoid(c)
    q, k = silu(cq), silu(ck)
    q = q * lax.rsqrt(jnp.sum(q * q, axis=-1, keepdims=True) + L2_EPS) * (GDN_HD ** -0.5)
    k = k * lax.rsqrt(jnp.sum(k * k, axis=-1, keepdims=True) + L2_EPS)
    return q, k, silu(cv)


def _gdn_gate_fn(ab, al_p, dt_p):
    lane = lax.broadcasted_iota(jnp.int32, ab.shape, 1)
    gpart = -jnp.exp(al_p) * _softplus(ab + dt_p)
    return (jnp.where(lane < GDN_HEADS, gpart, jnp.where(lane < 2 * GDN_HEADS, _sigmoid(ab), 0.0)),)


def _gdn_post_fn(o, z, nw):
    ms = _seg_sum(o * o, GDN_HD) * (1.0 / GDN_HD)
    return (o * lax.rsqrt(ms + NORM_EPS) * nw * (z * _sigmoid(z)),)


def _mix_fn(ga, gb, ya, yb):
    return (_sigmoid(ga) * ya + _sigmoid(gb) * yb,)


STRIP = 128


def _strip_conv(ref, prev_ref, w_ref, sl, first, taps):
    cur = ref[:, sl]
    prev = jnp.where(first, 0.0, prev_ref[:, sl])
    down = [_shift_down(cur, prev, s) for s in range(taps)]
    conv = None
    for k in range(taps):
        term = down[taps - 1 - k] * w_ref[pl.ds(k, 1), sl]
        conv = term if conv is None else conv + term
    return cur, down, conv


def _group_fwd(name, fn, x, w, shared_cols, group_cols, consts, n_out, tm):
    T, W = x.shape
    taps = w.shape[0]
    n_groups = len(group_cols)
    nc = len(consts)

    def body(x_ref, xp_ref, w_ref, *refs):
        const_refs, out_refs = refs[:nc], refs[nc:]
        first = pl.program_id(0) == 0
        shared = [_strip_conv(x_ref, xp_ref, w_ref, sl, first, taps)[2] for sl in shared_cols]
        for j, cols in enumerate(group_cols):
            sl = slice(STRIP * j, STRIP * (j + 1))
            convs = [_strip_conv(x_ref, xp_ref, w_ref, c, first, taps)[2] for c in cols]
            outs = fn(*convs, *shared, *[c[:, sl] for c in const_refs])
            for o_ref, o in zip(out_refs, outs):
                o_ref[:, sl] = o

    return pl.pallas_call(
        body, grid=(T // tm,),
        in_specs=[_row_spec(tm, W), _prev_spec(tm, W), _full_spec(w.shape)] + [_full_spec(c.shape) for c in consts],
        out_specs=[_row_spec(tm, STRIP * n_groups)] * n_out,
        out_shape=[jax.ShapeDtypeStruct((T, STRIP * n_groups), f32)] * n_out, name=name,
        compiler_params=_params("parallel"))(x, x, w, *consts)


def _group_bwd(name, fn, x, w, shared_cols, group_cols, consts, cots, tm):
    T, W = x.shape
    taps = w.shape[0]
    nblk = T // tm
    nc, ns = len(consts), len(shared_cols)
    flat_cots = [c for grp in cots for c in grp]
    n_cot = len(flat_cots)

    def body(x_ref, xp_ref, xn_ref, w_ref, *refs):
        const_refs, refs = refs[:nc], refs[nc:]
        cot_refs, cotn_refs, refs = refs[:n_cot], refs[n_cot:2 * n_cot], refs[2 * n_cot:]
        dx_ref, dw_ref, const_out = refs[0], refs[1], refs[2:]
        i = pl.program_id(0)
        first, last = i == 0, i == nblk - 1

        @pl.when(first)
        def _():
            dw_ref[...] = jnp.zeros_like(dw_ref)
            for q in range(nc):
                const_out[q][...] = jnp.zeros_like(const_out[q])

        def convs_of(sl):
            cur, down, conv = _strip_conv(x_ref, xp_ref, w_ref, sl, first, taps)
            nxt, conv_n = xn_ref[:, sl], None
            for k in range(taps):
                term = _shift_down(nxt, cur[tm - HALO:], taps - 1 - k) * w_ref[pl.ds(k, 1), sl]
                conv_n = term if conv_n is None else conv_n + term
            return down, conv, conv_n

        def conv_back(sl, down, dc, dc_n):
            dx = None
            for k in range(taps):
                s_ = taps - 1 - k
                term = _shift_up(dc, dc_n, s_) * w_ref[pl.ds(k, 1), sl]
                dx = term if dx is None else dx + term
                dw_ref[pl.ds(k, 1), sl] += jnp.sum(dc * down[s_], axis=0, keepdims=True)
            dx_ref[:, sl] = dx.astype(dx_ref.dtype)

        def summed(refs_, sl, mask):
            out, p = [], 0
            for grp in cots:
                acc = refs_[p][:, sl]
                for t in range(1, len(grp)):
                    acc = acc + refs_[p + t][:, sl]
                p += len(grp)
                out.append(jnp.where(last, 0.0, acc) if mask else acc)
            return tuple(out)

        shared = [convs_of(sl) for sl in shared_cols]
        d_shared, d_shared_n = [None] * ns, [None] * ns
        for j, cols in enumerate(group_cols):
            sl = slice(STRIP * j, STRIP * (j + 1))
            mine = [convs_of(c) for c in cols]
            cj = [c[:, sl] for c in const_refs]
            _, vjp = jax.vjp(fn, *[m[1] for m in mine], *[m[1] for m in shared], *cj)
            grads = vjp(summed(cot_refs, sl, False))
            _, vjp_n = jax.vjp(fn, *[m[2] for m in mine], *[m[2] for m in shared], *cj)
            grads_n = vjp_n(summed(cotn_refs, sl, True))
            for q, c in enumerate(cols):
                conv_back(c, mine[q][0], grads[q], grads_n[q])
            for q in range(ns):
                g, gn = grads[len(cols) + q], grads_n[len(cols) + q]
                d_shared[q] = g if d_shared[q] is None else d_shared[q] + g
                d_shared_n[q] = gn if d_shared_n[q] is None else d_shared_n[q] + gn
            for q in range(nc):
                const_out[q][:, sl] += grads[len(cols) + ns + q]
        for q, c in enumerate(shared_cols):
            conv_back(c, shared[q][0], d_shared[q], d_shared_n[q])

    outs = pl.pallas_call(
        body, grid=(nblk,),
        in_specs=[_row_spec(tm, W), _prev_spec(tm, W), _next_spec(tm, W, T), _full_spec(w.shape)]
        + [_full_spec(c.shape) for c in consts] + [_row_spec(tm, c.shape[1]) for c in flat_cots]
        + [_next_spec(tm, c.shape[1], T) for c in flat_cots],
        out_specs=[_row_spec(tm, W), _full_spec(w.shape)] + [_full_spec(c.shape) for c in consts],
        out_shape=[jax.ShapeDtypeStruct((T, W), bf16), jax.ShapeDtypeStruct(w.shape, f32)]
        + [jax.ShapeDtypeStruct(c.shape, f32) for c in consts], name=name,
        compiler_params=_params("arbitrary"))(x, x, x, w, *consts, *flat_cots, *flat_cots)
    return outs[0], outs[1], list(outs[2:])


def _ffn_strip_fn(cg, cu):
    return cg * _sigmoid(cg) * cu


def _ffn_act_fwd(h, w, tm):
    T, W2 = h.shape
    H = W2 // 2
    taps = w.shape[0]

    def body(h_ref, hp_ref, w_ref, o_ref):
        first = pl.program_id(0) == 0
        for j in range(H // STRIP):
            gs, us = slice(STRIP * j, STRIP * (j + 1)), slice(H + STRIP * j, H + STRIP * (j + 1))
            cg = _strip_conv(h_ref, hp_ref, w_ref, gs, first, taps)[2]
            cu = _strip_conv(h_ref, hp_ref, w_ref, us, first, taps)[2]
            o_ref[:, gs] = _ffn_strip_fn(cg, cu).astype(o_ref.dtype)

    return pl.pallas_call(
        body, grid=(T // tm,), in_specs=[_row_spec(tm, W2), _prev_spec(tm, W2), _full_spec(w.shape)],
        out_specs=_row_spec(tm, H), out_shape=jax.ShapeDtypeStruct((T, H), bf16), name="ffn_act",
        compiler_params=_params("parallel"))(h, h, w)


def _ffn_act_bwd(h, dact, w, tm):
    T, W2 = h.shape
    H = W2 // 2
    taps = w.shape[0]
    nblk = T // tm

    def body(h_ref, hp_ref, hn_ref, d_ref, dn_ref, w_ref, dh_ref, dw_ref):
        i = pl.program_id(0)
        first, last = i == 0, i == nblk - 1

        @pl.when(first)
        def _():
            dw_ref[...] = jnp.zeros_like(dw_ref)

        for j in range(H // STRIP):
            gs, us = slice(STRIP * j, STRIP * (j + 1)), slice(H + STRIP * j, H + STRIP * (j + 1))
            parts = {}
            for name, sl in (('g', gs), ('u', us)):
                cur, down, conv = _strip_conv(h_ref, hp_ref, w_ref, sl, first, taps)
                nxt = hn_ref[:, sl]
                conv_n = None
                for k in range(taps):
                    term = _shift_down(nxt, cur[tm - HALO:], taps - 1 - k) * w_ref[pl.ds(k, 1), sl]
                    conv_n = term if conv_n is None else conv_n + term
                parts[name] = (down, conv, conv_n)
            _, vjp = jax.vjp(_ffn_strip_fn, parts['g'][1], parts['u'][1])
            dcs = vjp(d_ref[:, gs])
            _, vjp_n = jax.vjp(_ffn_strip_fn, parts['g'][2], parts['u'][2])
            dcs_n = vjp_n(jnp.where(last, 0.0, dn_ref[:, gs]))
            for (name, sl), dc, dc_n in zip((('g', gs), ('u', us)), dcs, dcs_n):
                down = parts[name][0]
                dx = None
                for k in range(taps):
                    s_ = taps - 1 - k
                    term = _shift_up(dc, dc_n, s_) * w_ref[pl.ds(k, 1), sl]
                    dx = term if dx is None else dx + term
                    dw_ref[pl.ds(k, 1), sl] += jnp.sum(dc * down[s_], axis=0, keepdims=True)
                dh_ref[:, sl] = dx.astype(dh_ref.dtype)

    return pl.pallas_call(
        body, grid=(nblk,),
        in_specs=[_row_spec(tm, W2), _prev_spec(tm, W2), _next_spec(tm, W2, T), _row_spec(tm, H), _next_spec(tm, H, T),
                  _full_spec(w.shape)],
        out_specs=[_row_spec(tm, W2), _full_spec(w.shape)],
        out_shape=[jax.ShapeDtypeStruct((T, W2), bf16), jax.ShapeDtypeStruct(w.shape, f32)], name="ffn_act_bwd",
        compiler_params=_params("arbitrary"))(h, h, h, dact, dact, w)


N_POS = 4


def _xy_out_shapes(bufs, scatter):
    return [jax.ShapeDtypeStruct((N_POS,) + tuple(b.shape[1:] if scatter else b.shape), b.dtype) for b in bufs]


def _xy_sems(n, scatter):
    sems = [pltpu.SemaphoreType.DMA((3 * n,)), pltpu.SemaphoreType.DMA((3 * n,)), pltpu.SemaphoreType.DMA((n,))]
    return sems if scatter else sems + [pltpu.SemaphoreType.DMA((3 * n,)), pltpu.SemaphoreType.DMA((3 * n,))]


def _xy_copies(in_refs, out_refs, sems, scatter):
    n = len(in_refs)
    send_sems, recv_sems, local_sems = sems[:3]

    def place():
        x, y, c = lax.axis_index("x"), lax.axis_index("y"), lax.axis_index("c")
        return x, y, c, 2 * x + y, [(1 - x, y), (x, 1 - y), (1 - x, 1 - y)]

    def half(ref, a, which):
        rows = in_refs[a].shape[0] // 2
        return ref.at[pl.ds(pl.multiple_of(which * rows, HALO), rows)]

    def ici(a, k, src, dst, peer, c):
        return pltpu.make_async_remote_copy(
            src_ref=src, dst_ref=dst, send_sem=send_sems.at[3 * a + k], recv_sem=recv_sems.at[3 * a + k],
            device_id=(peer[0], peer[1], c), device_id_type=pl.DeviceIdType.MESH)

    def outgoing():
        x, y, c, me, peers = place()
        own = [pltpu.make_async_copy(in_refs[a].at[me] if scatter else in_refs[a], out_refs[a].at[me], local_sems.at[a])
               for a in range(n)]
        if scatter:
            sends = [ici(a, k, in_refs[a].at[2 * p[0] + p[1]], out_refs[a].at[me], p, c)
                     for a in range(n) for k, p in enumerate(peers)]
        else:
            sends = [ici(a, k, half(in_refs[a], a, c), half(out_refs[a].at[me], a, c), p, c)
                     for a in range(n) for k, p in enumerate(peers)]
        return own, sends

    def arrivals():
        x, y, c, me, peers = place()
        if scatter:
            return [ici(a, k, in_refs[a].at[me], out_refs[a].at[2 * p[0] + p[1]], p, c)
                    for a in range(n) for k, p in enumerate(peers)]
        return [ici(a, k, half(in_refs[a], a, c), half(out_refs[a].at[2 * p[0] + p[1]], a, c), p, c)
                for a in range(n) for k, p in enumerate(peers)]

    def to_sibling(mine):
        x, y, c, me, peers = place()
        which = c if mine else 1 - c
        return [pltpu.make_async_remote_copy(
            src_ref=half(out_refs[a].at[2 * p[0] + p[1]], a, which), dst_ref=half(out_refs[a].at[2 * p[0] + p[1]], a, which),
            send_sem=sems[3].at[3 * a + k], recv_sem=sems[4].at[3 * a + k],
            device_id=(x, y, 1 - c), device_id_type=pl.DeviceIdType.MESH) for a in range(n) for k, p in enumerate(peers)]

    def start():
        own, sends = outgoing()
        for cp in own + sends:
            cp.start()

    def finish():
        if scatter:
            for cp in arrivals():
                cp.wait_recv()
        else:
            passed = to_sibling(True)
            for cp, fwd in zip(arrivals(), passed):
                cp.wait_recv()
                fwd.start()
            for cp in to_sibling(False):
                cp.wait_recv()
            for fwd in passed:
                fwd.wait_send()
        own, sends = outgoing()
        for cp in sends:
            cp.wait_send()
        for cp in own:
            cp.wait()

    return start, finish


_NN, _NT, _TN = 'hcs,hsd->hcd', 'hcd,hsd->hcs', 'hcd,hce->hde'


def _lo(spec, a, b):
    return jnp.einsum(spec, a.astype(bf16), b.astype(bf16), preferred_element_type=f32)


@jax.custom_vjp
def _bmm(a, b):
    return _lo(_NN, a, b)


_bmm.defvjp(lambda a, b: (_lo(_NN, a, b), (a, b)), lambda ab, g: (_lo(_NT, g, ab[1]), _lo(_TN, ab[0], g)))


@jax.custom_vjp
def _bmm_nt(a, b):
    return _lo(_NT, a, b)


_bmm_nt.defvjp(lambda a, b: (_lo(_NT, a, b), (a, b)), lambda ab, g: (_lo(_NN, g, ab[1]), _lo(_TN, g, ab[0])))


@jax.custom_vjp
def _bmm_tn(a, b):
    return _lo(_TN, a, b)


_bmm_tn.defvjp(lambda a, b: (_lo(_TN, a, b), (a, b)), lambda ab, g: (_lo(_NT, ab[1], g), _lo(_NN, ab[0], g)))


def _masks(H, C):
    row = lax.broadcasted_iota(jnp.int32, (H, C, C), 1)
    col = lax.broadcasted_iota(jnp.int32, (H, C, C), 2)
    return row, col


def _tri_inv_impl(L):
    H, C, _ = L.shape
    row, col = _masks(H, C)
    eye = (row == col).astype(f32)
    base = 16
    same = (row // base) == (col // base)
    Ld = jnp.where(same, L, 0.0)
    X = -Ld
    inv = eye + X
    for _ in range(3):
        X = _bmm(X, X)
        inv = _bmm(inv, eye + X)
    if C == base:
        return inv
    N = _bmm(inv, L - Ld)
    out = eye - N
    levels = C // base
    P = N
    span = 2
    while span < levels:
        P = _bmm(P, P)
        out = _bmm(out, eye + P)
        span *= 2
    return _bmm(out, inv)


@jax.custom_vjp
def _tri_inv(L):
    return _tri_inv_impl(L)


def _tri_inv_fwd(L):
    T = _tri_inv_impl(L)
    return T, T


def _tri_inv_bwd(T, dT):
    return (-_bmm_nt(_bmm_tn(T, dT), T),)


_tri_inv.defvjp(_tri_inv_fwd, _tri_inv_bwd)


@jax.custom_vjp
def _tri_inv_known(L, T):
    return T


_tri_inv_known.defvjp(lambda L, T: (T, T), lambda T, dT: (_tri_inv_bwd(T, dT)[0], jnp.zeros_like(T)))


def _cumsum_impl(x, reverse):
    C = x.shape[1]
    row = lax.broadcasted_iota(jnp.int32, x.shape, 1)
    s = 1
    while s < C:
        if reverse:
            x = x + jnp.where(row < C - s, pltpu.roll(x, C - s, 1), 0.0)
        else:
            x = x + jnp.where(row >= s, pltpu.roll(x, s, 1), 0.0)
        s *= 2
    return x


@jax.custom_vjp
def _cumsum(x):
    return _cumsum_impl(x, False)


_cumsum.defvjp(lambda x: (_cumsum_impl(x, False), None), lambda _, g: (_cumsum_impl(g, True),))


def _wkv_prep(r, lw, k, v, a, b, inv=None):
    lane = lax.broadcasted_iota(jnp.int32, (r.shape[0], 128), 1)
    low = lane < RWKV_HD

    def heads(t):
        out = []
        for p in range(RWKV_HEADS // 2):
            pair = t[:, 128 * p:128 * (p + 1)]
            out += [jnp.where(low, pair, 0.0), jnp.where(low, 0.0, pair)]
        return jnp.concatenate([t[None] for t in out], axis=0)

    r, lw, k, v, a, b = [heads(t) for t in (r, lw, k, v, a, b)]
    H, C, D = r.shape
    row, col = _masks(H, C)
    incl, strict = row >= col, row > col
    cw = _cumsum(lw)
    cwp = cw - lw
    cwl = jnp.sum(lw, axis=1, keepdims=True)
    en = jnp.exp(-cw)
    at, rt, bt, kt = a * jnp.exp(cwp), r * jnp.exp(cw), b * en, k * en
    Lab = -jnp.where(strict, _bmm_nt(at, bt), 0.0)
    Tm = _tri_inv(Lab) if inv is None else _tri_inv_known(Lab, inv)
    ar = jnp.concatenate([at, rt], axis=1)
    gram = _bmm_nt(ar, jnp.concatenate([bt, kt], axis=1))
    row2 = lax.broadcasted_iota(jnp.int32, (H, 2 * C, 2 * C), 1)
    col2 = lax.broadcasted_iota(jnp.int32, (H, 2 * C, 2 * C), 2) % C
    gram = jnp.where(((row2 < C) & (row2 > col2)) | ((row2 >= C) & (row2 - C >= col2)), gram, 0.0)
    a_bk, r_bk = gram[:, :C], gram[:, C:]
    lak_v = _bmm(a_bk, jnp.concatenate([jnp.zeros_like(v), v], axis=1))
    ed = jnp.exp(cwl - cw)
    zdec = jnp.swapaxes(jnp.broadcast_to(jnp.exp(cwl), (H, D, D)), 1, 2)
    return (ar, Tm, lak_v, r_bk, jnp.concatenate([b * ed, k * ed], axis=1), zdec, v), Tm


def _wkv_step(Z, ar, Tm, lak_v, r_bk, bk_d, zdec, v):
    C = Tm.shape[1]
    ar_z = _bmm(ar, Z)
    uv = jnp.concatenate([_bmm(Tm, ar_z[:, :C] + lak_v), v], axis=1)
    y = ar_z[:, C:] + _bmm(r_bk, uv)
    Z1 = Z * zdec + _bmm_tn(bk_d, uv)
    return jnp.concatenate([y[2 * p] + y[2 * p + 1] for p in range(RWKV_HEADS // 2)], axis=1), Z1


def _split3(x):
    hi = x.astype(bf16)
    mid = (x - hi.astype(f32)).astype(bf16)
    lo = (x - hi.astype(f32) - mid.astype(f32)).astype(bf16)
    return hi, mid, lo


@jax.custom_vjp
def _spread(x, sel):
    return sum(jnp.dot(t, sel, preferred_element_type=f32) for t in _split3(x))


def _spread_bwd(sel, g):
    dn = (((1,), (1,)), ((), ()))
    return sum(lax.dot_general(t, sel, dn, preferred_element_type=f32) for t in _split3(g)), None


_spread.defvjp(lambda x, sel: (_spread(x, sel), sel), _spread_bwd)


def _gdn_prep(q, k, v, gbeta, inv=None):
    heads = lambda t: jnp.concatenate([t[None, :, GDN_HD * h:GDN_HD * (h + 1)] for h in range(GDN_HEADS)], axis=0)
    src = lax.broadcasted_iota(jnp.int32, (W_AB, 2 * GDN_W), 0)
    dst = lax.broadcasted_iota(jnp.int32, (W_AB, 2 * GDN_W), 1) // GDN_HD
    spread = _spread(gbeta, (src == dst).astype(bf16))
    q, k, v, g, beta = heads(q), heads(k), heads(v), heads(spread[:, :GDN_W]), heads(spread[:, GDN_W:])
    H, C, D = q.shape
    row, col = _masks(H, C)
    incl, strict = row >= col, row > col
    gc = _cumsum(g)
    diff = gc - jnp.swapaxes(gc, 1, 2)
    decay = jnp.where(incl, jnp.exp(jnp.where(incl, diff, 0.0)), 0.0)
    gl = jnp.sum(g, axis=1, keepdims=True)
    kb, vb = k * beta, v * beta
    gram = _bmm_nt(jnp.concatenate([kb, q], axis=1), k)
    L = jnp.where(strict, gram[:, :C] * decay, 0.0)
    attn = jnp.where(incl, gram[:, C:] * decay, 0.0)
    egc = jnp.exp(gc)
    Tm = _tri_inv(L) if inv is None else _tri_inv_known(L, inv)
    t_vk = _bmm(Tm, jnp.concatenate([vb, kb * egc], axis=2))
    return (t_vk[:, :, :D], jnp.concatenate([t_vk[:, :, D:], q * egc], axis=1), attn, k * jnp.exp(gl - gc), jnp.exp(gl)), Tm


def _gdn_step(S, u, wq, attn, ke, sdec):
    C = u.shape[1]
    wq_s = _bmm(wq, S)
    v_new = u - wq_s[:, :C]
    o = wq_s[:, C:] + _bmm(attn, v_new)
    S1 = S * sdec + _bmm_tn(ke, v_new)
    return jnp.concatenate([o[h] for h in range(GDN_HEADS)], axis=1), S1


def _scan_fwd(name, fns, ins, C, H, dh, w_out, per_step, side=None):
    prep, step = fns
    T = ins[0].shape[0]
    n_in = len(ins)
    blk = C * per_step
    nblk = T // blk
    n_side = 0 if side is None else len(side[0])

    def body(*refs):
        in_refs, refs = refs[:n_in], refs[n_in:]
        side_in, refs = refs[:n_side], refs[n_side:]
        y_ref, zs_ref, inv_ref, refs = refs[0], refs[1], refs[2], refs[3:]
        side_out, refs = refs[:n_side], refs[n_side:]
        z_scr = refs[0]
        if side is not None:
            start, finish = _xy_copies(side_in, side_out, refs[1:], side[1])
            pl.when(pl.program_id(0) == 0)(start)

        @pl.when(pl.program_id(0) == 0)
        def _():
            z_scr[...] = jnp.zeros_like(z_scr)

        rows = [slice(C * j, C * (j + 1)) for j in range(per_step)]
        prepped = [prep(*[r[rw, :] for r in in_refs]) for rw in rows]
        Z = z_scr[...]
        for j, rw in enumerate(rows):
            zs_ref[j] = Z
            inv_ref[j] = prepped[j][1]
            y, Z = step(Z, *prepped[j][0])
            y_ref[rw, :] = y
        z_scr[...] = Z
        if side is not None:
            pl.when(pl.program_id(0) == nblk - 1)(finish)

    side_bufs = [] if side is None else list(side[0])
    any_spec = pl.BlockSpec(memory_space=pl.ANY)
    return pl.pallas_call(
        body, grid=(nblk,),
        in_specs=[pl.BlockSpec((blk, a.shape[1]), lambda i: (i, 0)) for a in ins] + [any_spec] * n_side,
        out_specs=[pl.BlockSpec((blk, w_out), lambda i: (i, 0)), pl.BlockSpec((per_step, H, dh, dh), lambda i: (i, 0, 0, 0)),
                   pl.BlockSpec((per_step, H, C, C), lambda i: (i, 0, 0, 0))] + [any_spec] * n_side,
        out_shape=[jax.ShapeDtypeStruct((T, w_out), f32), jax.ShapeDtypeStruct((T // C, H, dh, dh), f32),
                   jax.ShapeDtypeStruct((T // C, H, C, C), f32)]
        + (_xy_out_shapes(side_bufs, side[1]) if side is not None else []),
        scratch_shapes=[pltpu.VMEM((H, dh, dh), f32)] + (_xy_sems(n_side, side[1]) if side is not None else []), name=name,
        compiler_params=_params("arbitrary"))(*ins, *side_bufs)


def _scan_bwd(name, fns, ins, dy, zs, invs, C, per_step, side=None):
    prep, step = fns
    T = ins[0].shape[0]
    _, H, dh, _ = zs.shape
    n_in = len(ins)
    blk = C * per_step
    nblk = T // blk
    n_side = 0 if side is None else len(side[0])

    def body(*refs):
        in_refs, dy_ref, zs_ref, inv_ref, refs = refs[:n_in], refs[n_in], refs[n_in + 1], refs[n_in + 2], refs[n_in + 3:]
        side_in, refs = refs[:n_side], refs[n_side:]
        out_refs, refs = refs[:n_in], refs[n_in:]
        side_out, refs = refs[:n_side], refs[n_side:]
        dz_scr = refs[0]
        if side is not None:
            start, finish = _xy_copies(side_in, side_out, refs[1:], side[1])
            pl.when(pl.program_id(0) == 0)(start)

        @pl.when(pl.program_id(0) == 0)
        def _():
            dz_scr[...] = jnp.zeros_like(dz_scr)

        rows = [slice(C * j, C * (j + 1)) for j in range(per_step)]
        prepped = [jax.vjp(lambda *a, j=j: prep(*a, inv=inv_ref[j])[0], *[r[rw, :] for r in in_refs])
                   for j, rw in enumerate(rows)]
        d_prepped = [None] * per_step
        dZ = dz_scr[...]
        for j in reversed(range(per_step)):
            _, pull = jax.vjp(step, zs_ref[j], *prepped[j][0])
            dZ, *d_prepped[j] = pull((dy_ref[rows[j], :], dZ))
        dz_scr[...] = dZ
        for j, rw in enumerate(rows):
            for o_ref, gval in zip(out_refs, prepped[j][1](tuple(d_prepped[j]))):
                o_ref[rw, :] = gval
        if side is not None:
            pl.when(pl.program_id(0) == nblk - 1)(finish)

    side_bufs = [] if side is None else list(side[0])
    any_spec = pl.BlockSpec(memory_space=pl.ANY)
    rev = lambda i: (nblk - 1 - i, 0)
    return pl.pallas_call(
        body, grid=(nblk,),
        in_specs=[pl.BlockSpec((blk, a.shape[1]), rev) for a in ins]
        + [pl.BlockSpec((blk, dy.shape[1]), rev), pl.BlockSpec((per_step, H, dh, dh), lambda i: (nblk - 1 - i, 0, 0, 0)),
           pl.BlockSpec((per_step, H, C, C), lambda i: (nblk - 1 - i, 0, 0, 0))] + [any_spec] * n_side,
        out_specs=[pl.BlockSpec((blk, a.shape[1]), rev) for a in ins] + [any_spec] * n_side,
        out_shape=[jax.ShapeDtypeStruct(a.shape, f32) for a in ins]
        + (_xy_out_shapes(side_bufs, side[1]) if side is not None else []),
        scratch_shapes=[pltpu.VMEM((H, dh, dh), f32)] + (_xy_sems(n_side, side[1]) if side is not None else []), name=name,
        compiler_params=_params("arbitrary"))(*ins, dy, zs, invs, *side_bufs)


def _residual_mm(name, a, b, res, tail, row_extras, consts, row_out, acc_out, tm, head=None):
    K, N = b.shape
    h_rows, h_consts = ([], []) if head is None else (list(head[1]), list(head[2]))
    lhs = h_rows + h_consts if head is not None else [a]
    M = lhs[0].shape[0]
    row_extras = [_row_of(e) for e in row_extras]
    n_lhs, n_res = len(lhs), 0 if res is None else 1
    ne, nc, nr = len(row_extras), len(consts), len(row_out)

    def body(*refs):
        lhs_refs, refs = refs[:n_lhs], refs[n_lhs:]
        b_ref, refs = refs[0], refs[1:]
        res_ref, refs = (refs[0], refs[1:]) if res is not None else (None, refs)
        extra_refs, const_refs, out_refs = refs[:ne], refs[ne:ne + nc], refs[ne + nc:]
        if head is not None:
            left = head[0](*[r[...] for r in lhs_refs])[0].astype(bf16)
            out_refs[0][...] = left
            out_refs = out_refs[1:]
        else:
            left = lhs_refs[0][...].astype(bf16)
        tile = jnp.dot(left, b_ref[...].astype(bf16), preferred_element_type=f32)
        if res is not None:
            tile = res_ref[...] + tile
        outs = tail(tile, *[r[...] for r in extra_refs], *[c[...] for c in const_refs])
        for o_ref, o in zip(out_refs[:nr], outs[:nr]):
            o_ref[...] = o.astype(o_ref.dtype)

        @pl.when(pl.program_id(0) == 0)
        def _():
            for o_ref in out_refs[nr:]:
                o_ref[...] = jnp.zeros_like(o_ref)

        for o_ref, o in zip(out_refs[nr:], outs[nr:]):
            o_ref[...] += o

    lhs_specs = ([_row_spec(tm, r.shape[1]) for r in h_rows] + [_full_spec(c.shape) for c in h_consts]
                 if head is not None else [_row_spec(tm, K)])
    head_out = [(K, bf16)] if head is not None else []
    outs = pl.pallas_call(
        body, grid=(M // tm,),
        in_specs=lhs_specs + [_full_spec(b.shape)] + ([_row_spec(tm, N)] if res is not None else [])
        + [_row_spec(tm, e[1], e[2]) for e in row_extras] + [_full_spec(c.shape) for c in consts],
        out_specs=[_row_spec(tm, w) for w, _ in head_out + list(row_out)] + [_full_spec(sh) for sh in acc_out],
        out_shape=[jax.ShapeDtypeStruct((M, w), d) for w, d in head_out + list(row_out)]
        + [jax.ShapeDtypeStruct(sh, f32) for sh in acc_out],
        name=name, compiler_params=_params("arbitrary"))(
            *lhs, b, *([res] if res is not None else []), *[e[0] for e in row_extras], *consts)
    return outs


def _pull_tail(fn):
    def tail(cot, *args):
        _, vjp = jax.vjp(fn, *args)
        return vjp((cot,))
    return tail


def _norm_tail(x1, g):
    return x1, _rms(x1, g)


def _loss_tail(x2, tgt, g):
    l, vjp = jax.vjp(lambda xv, gv: _loss_rows(xv, tgt, gv), x2, g)
    dx, dg = vjp(jnp.ones_like(l))
    return dx, dg, jnp.zeros((1, 128), f32) + jnp.sum(l)


def _local_step(x, tgt, W, late=None):
    row = lambda a: a.reshape(1, -1)
    wp = W['w_in_pad']
    w_rwkv, w_qkv, w_z = wp[:, :OFF_QKV], wp[:, OFF_QKV:OFF_Z], wp[:, OFF_Z:OFF_GATES]
    w_gates, w_ab = wp[:, OFF_GATES:OFF_AB], wp[:, OFF_AB:]
    mu = row(W['rwkv_mu'])
    mixw = jnp.concatenate([mu, 1.0 - mu], axis=0)
    zpad = jnp.zeros((64, RWKV_W), f32)
    w2p = jnp.concatenate([W['rwkv_w2'], zpad], axis=0)
    a2p = jnp.concatenate([zpad, W['rwkv_a2']], axis=0)
    rw_consts = [row(W['rwkv_w0']), w2p, row(W['rwkv_a0']), a2p, W['rwkv_g2'], row(W['rwkv_k_k']), row(W['rwkv_k_a'])]
    post_consts = [row(W['rwkv_ln_w']), row(W['rwkv_ln_b']), row(W['rwkv_r_k'])]
    pad4 = lambda a: jnp.pad(row(a), ((0, 0), (0, W_AB - GDN_HEADS)))
    gd_consts = [pad4(W['gdn_a_log']), pad4(W['gdn_dt_bias'])]
    nw_t = jnp.tile(row(W['gdn_norm_w']), (1, GDN_HEADS))
    g1, g2n, gf = row(W['norm1_g']), row(W['norm2_g']), row(W['final_g'])

    u = late['u'] if late is not None else _pw_fwd("norm1", _rms_fn, [x], [g1], [D_MODEL], 512, out_dtype=bf16)[0]
    p_rwkv = _mm(u, w_rwkv, 'nn', "in_rwkv")
    qkv_raw = _mm(u, w_qkv, 'nn', "in_qkv")
    z = _mm(u, w_z, 'nn', "in_z")
    gates = _mm(u, w_gates, 'nn', "in_gates")
    ab = _mm(u, w_ab, 'nn', "in_ab")

    r, lw, k2, v, a_, b_, g = _pw_fwd("rwkv_prep", _rwkv_prep_fn, [p_rwkv], rw_consts, [RWKV_W] * 7, 256, conv_w=mixw)
    wkv_in = [r, lw, k2, v, a_, b_]
    y, zs_wkv, inv_wkv, *gathered = _scan_fwd("wkv_fwd", (_wkv_prep, _wkv_step), wkv_in, WKV_CHUNK, RWKV_HEADS, 2 * RWKV_HD, RWKV_W, WKV_PER_STEP,
                                     side=None if late is None else (late['shards'][0], False))
    if late is not None:
        W = dict(W, **late['assemble'](0, gathered))
    ya_in, ya = _residual_mm("rwkv_proj", None, W['rwkv_proj'], None, lambda t: (t,), [], [], [(D_MODEL, f32)], [], 512,
                             head=(_rwkv_post_fn, [y, r, k2, v, g], post_consts))

    lanes = lambda off: slice(off, off + STRIP)
    gd_groups = [[lanes(GDN_HD * h), lanes(GDN_W + GDN_HD * h), lanes(2 * GDN_W + GDN_HD * h)] for h in range(GDN_HEADS)]
    gq, gk, gv = _group_fwd("gdn_prep", _gdn_prep_fn, qkv_raw, W['gdn_conv_w'], [], gd_groups, [], 3, 256)
    (gbeta,) = _pw_fwd("gdn_gate", _gdn_gate_fn, [ab], gd_consts, [W_AB], 512)
    gdn_in = [gq, gk, gv, gbeta]
    o, zs_gdn, inv_gdn, *gathered = _scan_fwd("gdn_fwd", (_gdn_prep, _gdn_step), gdn_in, GDN_CHUNK, GDN_HEADS, GDN_HD, GDN_W, GDN_PER_STEP,
                                     side=None if late is None else (late['shards'][1], False))
    if late is not None:
        W = dict(W, **late['assemble'](1, gathered))
    ga, gb = _cols(gates, D_MODEL, 0), _cols(gates, D_MODEL, 1)
    yb_in, yb, mixed = _residual_mm("gdn_proj", None, W['gdn_proj'], None, lambda t, a_, b_, c_: (t,) + _mix_fn(a_, b_, c_, t),
                                    [ga, gb, ya], [], [(D_MODEL, f32), (D_MODEL, bf16)], [], 512,
                                    head=(_gdn_post_fn, [o, z], [nw_t]))

    x1, u2 = _residual_mm("w_out", mixed, W['w_out'], x, _norm_tail, [], [g2n], [(D_MODEL, f32), (D_MODEL, bf16)], [], 512)
    h = _mm(u2, W['ffn_up'], 'nn', "ffn_up")
    act = _ffn_act_fwd(h, W['ffn_conv_w'], 256)

    G = {}
    slab_out = None if late is None else N_POS
    dx2, dgf, loss = _residual_mm("ffn_down", act, W['ffn_down'], x1, _loss_tail, [tgt], [gf], [(D_MODEL, f32)],
                                  [gf.shape, (1, 128)], 512)
    G['final_g'] = dgf
    dact = _mm(dx2, W['ffn_down'], 'nt', "d_act")
    G['ffn_down'] = _mm(act, dx2, 'tn', "g_ffn_down", out_dtype=bf16)
    dh, G['ffn_conv_w'] = _ffn_act_bwd(h, dact, W['ffn_conv_w'], 128)
    du2 = _mm(dh, W['ffn_up'], 'nt', "d_u2")
    G['ffn_up'] = _mm(u2, dh, 'tn', "g_ffn_up", out_dtype=bf16, col_slabs=slab_out)
    (dx1,), (G['norm2_g'],) = _pw_bwd("norm2_bwd", _rms_fn, [x1], [g2n], [(du2,)], 512, add_to_first=dx2)
    G['w_out'] = _mm(mixed, dx1, 'tn', "g_w_out", out_dtype=bf16)
    dga, dgb, dya, dyb = _residual_mm("d_mixed", dx1, W['w_out'].T, None, _pull_tail(_mix_fn), [ga, gb, ya, yb], [],
                                      [(D_MODEL, bf16)] * 4, [], 512)
    G['rwkv_proj'] = _mm(ya_in, dya, 'tn', "g_rwkv_proj", out_dtype=bf16, col_slabs=slab_out)
    G['gdn_proj'] = _mm(yb_in, dyb, 'tn', "g_gdn_proj", out_dtype=bf16, col_slabs=slab_out)

    do, dz, dnw_t = _residual_mm("d_yb_in", dyb, W['gdn_proj'].T, None, _pull_tail(_gdn_post_fn), [o, z], [nw_t],
                                 [(GDN_W, f32), (GDN_W, bf16)], [nw_t.shape], 512)
    G['gdn_norm_w'] = dnw_t.reshape(GDN_HEADS, GDN_HD).sum(axis=0)
    dgq, dgk, dgv, dgbeta, *arrived_b = _scan_bwd("gdn_bwd", (_gdn_prep, _gdn_step), gdn_in, do, zs_gdn, inv_gdn, GDN_CHUNK,
                                                  GDN_PER_STEP, side=None if late is None else (late['slabs'](G, 1), True))
    dqkv_raw, G['gdn_conv_w'], _ = _group_bwd("gdn_prep_bwd", _gdn_prep_fn, qkv_raw, W['gdn_conv_w'], [], gd_groups, [],
                                              [(dgq,), (dgk,), (dgv,)], 128)
    (dab,), (dal_p, ddt_p) = _pw_bwd("gdn_gate_bwd", _gdn_gate_fn, [ab], gd_consts, [(dgbeta,)], 512, row_dtypes=[bf16])
    G['gdn_a_log'], G['gdn_dt_bias'] = dal_p[0, :GDN_HEADS], ddt_p[0, :GDN_HEADS]

    dy, dr1, dk21, dv1, dg_, G['rwkv_ln_w'], G['rwkv_ln_b'], G['rwkv_r_k'] = _residual_mm(
        "d_ya_in", dya, W['rwkv_proj'].T, None, _pull_tail(_rwkv_post_fn), [y, r, k2, v, g], post_consts,
        [(RWKV_W, f32)] * 5, [c.shape for c in post_consts], 512)
    dr2, dlw, dk22, dv2, da_, db_, *arrived_a = _scan_bwd(
        "wkv_bwd", (_wkv_prep, _wkv_step), wkv_in, dy, zs_wkv, inv_wkv, WKV_CHUNK, WKV_PER_STEP,
        side=None if late is None else (late['slabs'](G, 0), True))
    G['_arrived'] = (arrived_a, arrived_b)
    (dp_rwkv,), dmixw, rw_grads = _pw_conv_bwd(
        "rwkv_prep_bwd", _rwkv_prep_fn, [p_rwkv], rw_consts,
        [(dr1, dr2), (dlw,), (dk21, dk22), (dv1, dv2), (da_,), (db_,), (dg_,)], mixw, 256, row_dtypes=[bf16])
    G['rwkv_w0'], dw2p, G['rwkv_a0'], da2p, G['rwkv_g2'], G['rwkv_k_k'], G['rwkv_k_a'] = rw_grads
    G['rwkv_w2'], G['rwkv_a2'] = dw2p[:64], da2p[64:]
    G['rwkv_mu'] = dmixw[0] - dmixw[1]

    dps = [dp_rwkv, dqkv_raw, dz, dga, dgb, dab]
    offs = [0, OFF_QKV, OFF_Z, OFF_GATES, OFF_GATES + D_MODEL, OFF_AB]
    G['w_in_pad'] = list(zip(offs, _mm_tn_parts(u, dps[:2], "g_w_in_rwkv_qkv") + _mm_tn_parts(u, dps[2:], "g_w_in_rest")))
    pairs = [(dp_rwkv, w_rwkv, 0), (dqkv_raw, w_qkv, 0), (dz, w_z, 0), (dga, w_gates, 0), (dgb, w_gates, 1), (dab, w_ab, 0)]
    if late is None:
        du = _mm_nt_parts(pairs, "d_u")
    else:
        du, *G['_arrived_w_in'] = _mm_nt_parts(pairs, "d_u", side=(late['w_in_slabs'](G), True))
    (dx,), (G['norm1_g'],) = _pw_bwd("norm1_bwd", _rms_fn, [x], [g1], [(du,)], 512, add_to_first=dx1)
    return loss, dx, G


IN_WIDTH = OFF_AB + 8
PAD_ORDER = ((0, OFF_GATES), (OFF_GATES + 8, IN_WIDTH), (OFF_GATES, OFF_GATES + 8))


def _pad_w_in_shards(shards):
    width = shards[0].shape[1]
    parts = []
    for a, b in PAD_ORDER:
        for j, sh in enumerate(shards):
            lo, hi = max(a, j * width), min(b, (j + 1) * width)
            if lo < hi:
                parts.append(sh[:, lo - j * width:hi - j * width])
    return jnp.concatenate(parts + [jnp.zeros((shards[0].shape[0], W_AB - 8), shards[0].dtype)], axis=1)


def _padded_cols(sections, s, e):
    pieces = [arr[:, max(s, o) - o:min(e, o + arr.shape[1]) - o] for o, arr in sections if max(s, o) < min(e, o + arr.shape[1])]
    return pieces[0] if len(pieces) == 1 else jnp.concatenate(pieces, axis=1)


def _unpad_cols(sections, lo, hi):
    parts, off = [], 0
    for a, b in PAD_ORDER:
        l, h = max(a, lo), min(b, hi)
        if l < h:
            parts.append((l, _padded_cols(sections, off + l - a, off + h - a)))
        off += b - a
    parts.sort(key=lambda t: t[0])
    return parts[0][1] if len(parts) == 1 else jnp.concatenate([p for _, p in parts], axis=1)


BIG = ('w_in', 'rwkv_proj', 'gdn_proj', 'w_out', 'ffn_up', 'ffn_down')
SMALL_SHARDED = ('rwkv_w2', 'rwkv_a2', 'rwkv_g2', 'gdn_conv_w', 'ffn_conv_w')


def _rows128(shape):
    n = 1
    for d in shape:
        n *= d
    return -(-n // LANES)


def _pack128(arrays):
    parts = []
    for a in arrays:
        flat = a.reshape(-1)
        rows = _rows128(a.shape)
        parts.append(jnp.pad(flat, (0, rows * LANES - flat.shape[0])).reshape(rows, LANES))
    buf = jnp.concatenate(parts, axis=0)
    return jnp.pad(buf, ((0, -buf.shape[0] % HALO), (0, 0)))


def _unpack128(buf, shapes):
    out, off = [], 0
    for s in shapes:
        rows, n = _rows128(s), 1
        for d in s:
            n *= d
        out.append(buf[off:off + rows].reshape(-1)[:n].reshape(s))
        off += rows
    return out


def _param_tile(r, c):
    best = None
    for d in range(2 * HALO, r + 1, 2 * HALO):
        if r % d == 0 and d * c * 4 <= TILE_BYTES:
            best = d
    if best is not None or r * c * 4 <= TILE_BYTES:
        return (best if best is not None else r), c
    return r, 128


def _norm_gather(name, x, g, bufs, tm):
    n, (T, D) = len(bufs), x.shape
    steps = T // tm

    def body(x_ref, g_ref, *refs):
        side_in, u_ref, side_out, sems = refs[:n], refs[n], refs[n + 1:2 * n + 1], refs[2 * n + 1:]
        i = pl.program_id(0)
        start, finish = _xy_copies(side_in, side_out, sems, False)
        pl.when(i == 0)(start)
        u_ref[...] = _rms(x_ref[...], g_ref[...]).astype(bf16)
        pl.when(i == steps - 1)(finish)

    any_spec = pl.BlockSpec(memory_space=pl.ANY)
    return pl.pallas_call(
        body, grid=(steps,), in_specs=[_row_spec(tm, D), _full_spec(g.shape)] + [any_spec] * n,
        out_specs=[_row_spec(tm, D)] + [any_spec] * n,
        out_shape=[jax.ShapeDtypeStruct((T, D), bf16)] + _xy_out_shapes(bufs, False),
        scratch_shapes=_xy_sems(n, False), name=name, compiler_params=_params("arbitrary"))(x, g, *bufs)


def _sibling_exchange(name, bufs):
    n = len(bufs)

    def body(*refs):
        start, finish = _sibling_copies(refs[:n], refs[n:2 * n], refs[2 * n:])
        start()
        finish()

    return pl.pallas_call(
        body, in_specs=[pl.BlockSpec(memory_space=pl.ANY)] * n, out_specs=[pl.BlockSpec(memory_space=pl.ANY)] * n,
        out_shape=[jax.ShapeDtypeStruct(b.shape, b.dtype) for b in bufs], scratch_shapes=_sibling_sems(n), name=name)(*bufs)


def _sibling_sems(n):
    return [pltpu.SemaphoreType.DMA((n,)), pltpu.SemaphoreType.DMA((n,))]


def _sibling_copies(in_refs, out_refs, sems):
    send_sems, recv_sems = sems

    def copies():
        x, y, c = lax.axis_index("x"), lax.axis_index("y"), lax.axis_index("c")
        return [pltpu.make_async_remote_copy(
            src_ref=in_refs[a], dst_ref=out_refs[a], send_sem=send_sems.at[a], recv_sem=recv_sems.at[a],
            device_id=(x, y, 1 - c), device_id_type=pl.DeviceIdType.MESH) for a in range(len(in_refs))]

    def start():
        for cp in copies():
            cp.start()

    def finish():
        for cp in copies():
            cp.wait()

    return start, finish


def _sum_slots(name, buf, side=None, to_sibling=None):
    _, R, L = buf.shape
    tr, tc = _param_tile(R, L)
    grid = (R // tr, L // tc)
    assert side is None or to_sibling is None
    carried = list(side or to_sibling or [])
    n_c = len(carried)

    def body(b_ref, *refs):
        c_in, o_ref, c_out, sems = refs[:n_c], refs[n_c], refs[n_c + 1:2 * n_c + 1], refs[2 * n_c + 1:]
        i, j = pl.program_id(0), pl.program_id(1)
        if carried:
            start, finish = _xy_copies(c_in, c_out, sems, True) if side is not None else _sibling_copies(c_in, c_out, sems)
            pl.when((i == 0) & (j == 0))(start)
        part = lambda s: b_ref[s].astype(f32)
        o_ref[...] = ((part(0) + part(1)) + part(2)) + part(3)
        if carried:
            pl.when((i == grid[0] - 1) & (j == grid[1] - 1))(finish)

    any_spec = pl.BlockSpec(memory_space=pl.ANY)
    c_shapes = _xy_out_shapes(carried, True) if side is not None else [jax.ShapeDtypeStruct(b.shape, b.dtype) for b in carried]
    c_sems = [] if not carried else _xy_sems(n_c, True) if side is not None else _sibling_sems(n_c)
    outs = pl.pallas_call(
        body, grid=grid,
        in_specs=[pl.BlockSpec((N_POS, tr, tc), lambda i, j: (0, i, j))] + [any_spec] * n_c,
        out_specs=[pl.BlockSpec((tr, tc), lambda i, j: (i, j))] + [any_spec] * n_c,
        out_shape=[jax.ShapeDtypeStruct((R, L), f32)] + c_shapes, scratch_shapes=c_sems, name=name,
        compiler_params=_params(*(("arbitrary",) * 2 if carried else ("parallel",) * 2)))(buf, *carried)
    return list(outs) if carried else outs[0]


def _adamw(name, w, ga, gb, m, v):
    R, L = w.shape
    tr, tc = _param_tile(R, L)
    c1 = 1.0 / (1.0 - ADAM_B1 ** ADAM_STEP)
    c2 = 1.0 / (1.0 - ADAM_B2 ** ADAM_STEP)

    def body(w_ref, ga_ref, gb_ref, m_ref, v_ref, g_out, d_out, m_out, v_out):
        g = ga_ref[...] + gb_ref[...]
        m_new = ADAM_B1 * m_ref[...] + (1.0 - ADAM_B1) * g
        v_new = ADAM_B2 * v_ref[...] + (1.0 - ADAM_B2) * (g * g)
        g_out[...] = g
        m_out[...] = m_new
        v_out[...] = v_new
        d_out[...] = -ADAM_LR * ((m_new * c1) / (jnp.sqrt(v_new * c2) + ADAM_EPS) + ADAM_WD * w_ref[...])

    spec = pl.BlockSpec((tr, tc), lambda i, j: (i, j))
    return pl.pallas_call(
        body, grid=(R // tr, L // tc), in_specs=[spec] * 5, out_specs=[spec] * 4,
        out_shape=[jax.ShapeDtypeStruct((R, L), f32)] * 4, name=name,
        compiler_params=_params("parallel", "parallel"))(w, ga, gb, m, v)


def _step(x, loss_target, P, M, V):
    shapes = {n: tuple(P[n].shape) for n in WEIGHTS}
    sh_shapes = [shapes[n] for n in SMALL_SHARDED]
    packed = SMALL_SHARDED + SMALL

    def whole(n, g):
        return g.reshape(-1, g.shape[2]) if n in ROW_SHARDED else jnp.concatenate([g[j] for j in range(N_POS)], axis=1)

    def slabs(G, n, dtype=f32):
        r, c = shapes[n]
        full = G[n].astype(dtype)
        if full.ndim == 3:
            return full
        return full.reshape(N_POS, r, c) if n in ROW_SHARDED else full.reshape(r, N_POS, c).transpose(1, 0, 2)

    u, g_w_in, g_small = _norm_gather("norm1_gather_w_in", x, P['norm1_g'].reshape(1, -1),
                                      [P['w_in'].astype(bf16), _pack128([P[n] for n in SMALL_SHARDED])], 512)
    W = {n: P[n] for n in SMALL}
    W['w_in_pad'] = _pad_w_in_shards([g_w_in[j] for j in range(N_POS)])
    per_pos = [_unpack128(g_small[j], sh_shapes) for j in range(N_POS)]
    for q, n in enumerate(SMALL_SHARDED):
        W[n] = jnp.concatenate([per_pos[j][q] for j in range(N_POS)], axis=1)
    groups = (('rwkv_proj', 'gdn_proj', 'ffn_up'), ('w_out', 'ffn_down'))
    late = dict(u=u, shards=[[P[n].astype(bf16) for n in grp] for grp in groups],
                assemble=lambda q, gathered: {n: whole(n, g) for n, g in zip(groups[q], gathered)},
                slabs=lambda G, q: [slabs(G, n, bf16) for n in groups[q]],
                w_in_slabs=lambda G: [jnp.stack([_unpad_cols(G['w_in_pad'], j * shapes['w_in'][1], (j + 1) * shapes['w_in'][1])
                                                 for j in range(N_POS)])])

    loss_rows, dx, G = _local_step(x, loss_target, W, late)
    arrived = {n: a for grp, got in zip(groups, G.pop('_arrived')) for n, a in zip(grp, got)}
    (arrived_w_in,) = G.pop('_arrived_w_in')
    G.pop('w_in_pad')

    small_slabs = jnp.stack([_pack128([slabs(G, n)[j] for n in SMALL_SHARDED] + [G[n] for n in SMALL]) for j in range(N_POS)])
    plane_w_in, arrived_small = _sum_slots("sum_w_in", arrived_w_in, side=[small_slabs])
    plane_ffn_up, sib_w_in = _sum_slots("sum_ffn_up", arrived['ffn_up'], to_sibling=[plane_w_in])
    plane_ffn_down, sib_ffn_up = _sum_slots("sum_ffn_down", arrived['ffn_down'], to_sibling=[plane_ffn_up])
    done = {'w_in': (plane_w_in, sib_w_in), 'ffn_up': (plane_ffn_up, sib_ffn_up)}
    tags = [n for n in BIG if n not in done] + ['small']
    rest = [plane_ffn_down if t == 'ffn_down' else _sum_slots("sum_" + t, arrived_small if t == 'small' else arrived[t])
            for t in tags]
    for t, p, s in zip(tags, rest, _sibling_exchange("sibling_grads", rest)):
        done[t] = (p, s)
    plane, sibling = [[done[t][q] for t in list(BIG) + ['small']] for q in range(2)]

    out = {}
    names4 = ('grad', 'delta', 'new_m', 'new_v')
    for q, n in enumerate(BIG):
        tr = (lambda t: t.T) if n == 'w_in' else (lambda t: t)
        for tag, t in zip(names4, _adamw("adamw_" + n, tr(P[n]), tr(plane[q]), tr(sibling[q]), tr(M[n]), tr(V[n]))):
            out[tag + '_' + n] = tr(t)
    small_out = _adamw("adamw_small", _pack128([P[n] for n in packed]), plane[-1], sibling[-1],
                       _pack128([M[n] for n in packed]), _pack128([V[n] for n in packed]))
    for tag, buf in zip(names4, small_out):
        for n, t in zip(packed, _unpack128(buf, [shapes[n] for n in packed])):
            out[tag + '_' + n] = t
    loss = lax.psum(loss_rows[0, 0], ("x", "y", "c"))
    return loss, dx, out


def kernel(x, norm1_g, w_in, rwkv_mu, rwkv_w0, rwkv_w2, rwkv_a0, rwkv_a2, rwkv_g2, rwkv_k_k, rwkv_k_a, rwkv_r_k, rwkv_ln_w, rwkv_ln_b, rwkv_proj, gdn_conv_w, gdn_a_log, gdn_dt_bias, gdn_norm_w, gdn_proj, w_out, norm2_g, ffn_up, ffn_conv_w, ffn_down, final_g, loss_target, m_norm1_g, m_w_in, m_rwkv_mu, m_rwkv_w0, m_rwkv_w2, m_rwkv_a0, m_rwkv_a2, m_rwkv_g2, m_rwkv_k_k, m_rwkv_k_a, m_rwkv_r_k, m_rwkv_ln_w, m_rwkv_ln_b, m_rwkv_proj, m_gdn_conv_w, m_gdn_a_log, m_gdn_dt_bias, m_gdn_norm_w, m_gdn_proj, m_w_out, m_norm2_g, m_ffn_up, m_ffn_conv_w, m_ffn_down, m_final_g, v_norm1_g, v_w_in, v_rwkv_mu, v_rwkv_w0, v_rwkv_w2, v_rwkv_a0, v_rwkv_a2, v_rwkv_g2, v_rwkv_k_k, v_rwkv_k_a, v_rwkv_r_k, v_rwkv_ln_w, v_rwkv_ln_b, v_rwkv_proj, v_gdn_conv_w, v_gdn_a_log, v_gdn_dt_bias, v_gdn_norm_w, v_gdn_proj, v_w_out, v_norm2_g, v_ffn_up, v_ffn_conv_w, v_ffn_down, v_final_g):
    weights = (norm1_g, w_in, rwkv_mu, rwkv_w0, rwkv_w2, rwkv_a0, rwkv_a2, rwkv_g2, rwkv_k_k, rwkv_k_a, rwkv_r_k, rwkv_ln_w,
               rwkv_ln_b, rwkv_proj, gdn_conv_w, gdn_a_log, gdn_dt_bias, gdn_norm_w, gdn_proj, w_out, norm2_g, ffn_up,
               ffn_conv_w, ffn_down, final_g)
    m_in = (m_norm1_g, m_w_in, m_rwkv_mu, m_rwkv_w0, m_rwkv_w2, m_rwkv_a0, m_rwkv_a2, m_rwkv_g2, m_rwkv_k_k, m_rwkv_k_a,
            m_rwkv_r_k, m_rwkv_ln_w, m_rwkv_ln_b, m_rwkv_proj, m_gdn_conv_w, m_gdn_a_log, m_gdn_dt_bias, m_gdn_norm_w,
            m_gdn_proj, m_w_out, m_norm2_g, m_ffn_up, m_ffn_conv_w, m_ffn_down, m_final_g)
    v_in = (v_norm1_g, v_w_in, v_rwkv_mu, v_rwkv_w0, v_rwkv_w2, v_rwkv_a0, v_rwkv_a2, v_rwkv_g2, v_rwkv_k_k, v_rwkv_k_a,
            v_rwkv_r_k, v_rwkv_ln_w, v_rwkv_ln_b, v_rwkv_proj, v_gdn_conv_w, v_gdn_a_log, v_gdn_dt_bias, v_gdn_norm_w,
            v_gdn_proj, v_w_out, v_norm2_g, v_ffn_up, v_ffn_conv_w, v_ffn_down, v_final_g)
    drop = lambda n, a: a if n == 'final_g' else a[0]
    P = {n: drop(n, a) for n, a in zip(WEIGHTS, weights)}
    M = {n: drop(n, a) for n, a in zip(WEIGHTS, m_in)}
    V = {n: drop(n, a) for n, a in zip(WEIGHTS, v_in)}
    loss, dx, out = _step(x[0], loss_target[0], P, M, V)
    lift = lambda n, a: a if n == 'final_g' else a[None]
    res = [loss, dx[None]]
    for tag in ('grad', 'delta', 'new_m', 'new_v'):
        res += [lift(n, out[tag + '_' + n]) for n in WEIGHTS]
    return tuple(res)
```
